```python
import math
import jax
import jax.numpy as jnp
from jax import lax
import numpy as np

D_MODEL = 1024
BATCH = 1
SEQ = 16384
DEPTH = 2

CHUNK = 64
QBLOCK = 128
GROUP_WIDTH = D_MODEL // 2
DK_A = 64
DV_A = 2 * DK_A
N_HEADS_A = GROUP_WIDTH // DV_A
DH_B = 64
N_HEADS_B = GROUP_WIDTH // DH_B
LEFT_CHUNKS = 8
BAND = (LEFT_CHUNKS + 1) * CHUNK
REL_CLIP = 2 * CHUNK
NUM_BUCKETS = 32
MAX_DISTANCE = 128
DV_C = 128
DQK_C = DV_C // 2
N_HEADS_C = GROUP_WIDTH // DV_C
ROPE_BASE = 10000.0
S5_CH = GROUP_WIDTH
S5_GROUP = 16
S5_GROUPS = S5_CH // S5_GROUP
S5_STATE = 64
D_FF = ((8 * D_MODEL // 3 + 255) // 256) * 256
CONV_W = 3
N_EVEN = (DEPTH + 1) // 2
N_ODD = DEPTH // 2
EVEN_IN = 3 * N_HEADS_A * DV_A + 3 * N_HEADS_B * DH_B
ODD_IN = 2 * N_HEADS_C * DQK_C + 2 * N_HEADS_C * DV_C + S5_CH
EPS = 1e-6
NEG_INF = -1e30

kernel_name = 'hybrid_diffattn_band_retention_s5_trunk'


def rmsnorm(x, g=None):
    xf = x.astype(jnp.float32)
    y = xf * lax.rsqrt(jnp.mean(xf * xf, axis=-1, keepdims=True) + EPS)
    if g is not None:
        y = y * g.astype(jnp.float32)
    return y


def modulated_rmsnorm(x, g, shift, scale):
    y = rmsnorm(x, g) * (1.0 + scale.astype(jnp.float32)[:, None, :]) + shift.astype(jnp.float32)[:, None, :]
    return y.astype(x.dtype)


def t5_bucket(rel):
    nb = NUM_BUCKETS // 2
    max_exact = nb // 2
    bucket = jnp.where(rel > 0, nb, 0)
    n = jnp.abs(rel)
    nf = jnp.maximum(n, 1).astype(jnp.float32)
    large = max_exact + (jnp.log(nf / max_exact) / math.log(MAX_DISTANCE / max_exact) * (nb - max_exact)).astype(jnp.int32)
    large = jnp.minimum(large, nb - 1)
    return bucket + jnp.where(n < max_exact, n, large)


def diff_attention(q, k, v, t5_table, lam, lam_init, subln_g):
    B_, L = q.shape[0], q.shape[1]
    nqb = L // QBLOCK
    q = q.astype(jnp.float32) * (DK_A ** -0.5)
    k = k.astype(jnp.float32)
    v = v.astype(jnp.float32)
    table = t5_table.astype(jnp.float32)
    key_pos = jnp.arange(L, dtype=jnp.int32)
    key_chunk = key_pos // CHUNK
    q_blocks = jnp.moveaxis(q.reshape(B_, nqb, QBLOCK, N_HEADS_A, 2, DK_A), 1, 0)

    def one_block(args):
        q_blk, blk = args
        q_pos = blk * QBLOCK + jnp.arange(QBLOCK, dtype=jnp.int32)
        bias = jnp.moveaxis(table[t5_bucket(key_pos[None, :] - q_pos[:, None])], -1, 0)
        visible = key_chunk[None, :] <= (q_pos // CHUNK)[:, None]
        s = jnp.einsum('bqhmd,bkhmd->bhmqk', q_blk, k) + bias[None, :, None]
        p = jax.nn.softmax(jnp.where(visible, s, NEG_INF), axis=-1)
        w = p[:, :, 0] - lam * p[:, :, 1]
        return jnp.einsum('bhqk,bkhd->bqhd', w, v)

    out = lax.map(one_block, (q_blocks, jnp.arange(nqb, dtype=jnp.int32)))
    out = jnp.moveaxis(out, 0, 1).reshape(B_, L, N_HEADS_A, DV_A)
    out = rmsnorm(out, subln_g) * (1.0 - lam_init)
    return out.reshape(B_, L, N_HEADS_A * DV_A)


def band_attention(q, k, v, rel_bias):
    B_, L = q.shape[0], q.shape[1]
    nc = L // CHUNK
    shp = (B_, nc, CHUNK, N_HEADS_B, DH_B)
    qc = q.astype(jnp.float32).reshape(shp) * (DH_B ** -0.5)

    def gather_band(t):
        tp = jnp.pad(t.astype(jnp.float32).reshape(shp), ((0, 0), (LEFT_CHUNKS, 0), (0, 0), (0, 0), (0, 0)))
        return jnp.concatenate([tp[:, j:j + nc] for j in range(LEFT_CHUNKS + 1)], axis=2)

    kb = gather_band(k)
    vb = gather_band(v)
    m = jnp.arange(BAND, dtype=jnp.int32)
    i = jnp.arange(CHUNK, dtype=jnp.int32)
    rel = jnp.clip(m[None, :] - LEFT_CHUNKS * CHUNK - i[:, None], -REL_CLIP, REL_CLIP) + REL_CLIP
    bias = rel_bias.astype(jnp.float32)[:, rel]
    valid = (jnp.arange(nc, dtype=jnp.int32)[:, None] - LEFT_CHUNKS + m[None, :] // CHUNK) >= 0
    s = jnp.einsum('bnqhd,bnkhd->bhnqk', qc, kb) + bias[None, :, None]
    p = jax.nn.softmax(jnp.where(valid[None, None, :, None, :], s, NEG_INF), axis=-1)
    out = jnp.einsum('bhnqk,bnkhd->bnqhd', p, vb)
    return out.reshape(B_, L, N_HEADS_B * DH_B)


def rotary(t):
    L, d = t.shape[1], t.shape[-1]
    inv_freq = 1.0 / (ROPE_BASE ** (jnp.arange(0, d, 2, dtype=jnp.float32) / d))
    ang = jnp.arange(L, dtype=jnp.float32)[:, None] * inv_freq[None, :]
    cos = jnp.cos(ang)[None, :, None, :]
    sin = jnp.sin(ang)[None, :, None, :]
    t1, t2 = jnp.split(t, 2, axis=-1)
    return jnp.concatenate([t1 * cos - t2 * sin, t1 * sin + t2 * cos], axis=-1)


def retention(q, k, v):
    B_, L = q.shape[0], q.shape[1]
    nc = L // CHUNK
    log_g = jnp.log(1.0 - jnp.power(2.0, -5.0 - jnp.arange(N_HEADS_C, dtype=jnp.float32)))
    pos = jnp.arange(CHUNK, dtype=jnp.float32)
    intra_decay = jnp.exp(log_g[:, None, None] * jnp.abs(pos[:, None] - pos[None, :]))
    q_decay = jnp.exp(log_g[:, None] * (pos[None, :] + 1.0))
    k_decay = jnp.exp(log_g[:, None] * (CHUNK - 1.0 - pos[None, :]))
    chunk_decay = jnp.exp(log_g * CHUNK)
    qc = q.reshape(B_, nc, CHUNK, N_HEADS_C, DQK_C)
    kc = k.reshape(B_, nc, CHUNK, N_HEADS_C, DQK_C) * (DQK_C ** -0.5)
    vc = v.reshape(B_, nc, CHUNK, N_HEADS_C, DV_C)
    scores = jnp.einsum('bnihd,bnjhd->bnhij', qc, kc) * intra_decay
    intra = jnp.einsum('bnhij,bnjhe->bnihe', scores, vc)
    kv = jnp.einsum('bnjhd,hj,bnjhe->bnhde', kc, k_decay, vc)

    def step(state, kv_n):
        return state * chunk_decay[None, :, None, None] + kv_n, state

    init = jnp.zeros((B_, N_HEADS_C, DQK_C, DV_C), jnp.float32)
    _, s_prev = lax.scan(step, init, jnp.moveaxis(kv, 1, 0))
    s_prev = jnp.moveaxis(s_prev, 0, 1)
    cross = jnp.einsum('bnihd,hi,bnhde->bnihe', qc, q_decay, s_prev)
    return (intra + cross).reshape(B_, L, N_HEADS_C, DV_C)


def _ssm_combine(e1, e2):
    a1, b1 = e1
    a2, b2 = e2
    return a1 * a2, a2 * b1 + b2


def s5_ssm(u, lam_re, lam_im, log_step, b_re, b_im, c_re, c_im, d_skip):
    B_, L = u.shape[0], u.shape[1]
    uf = u.astype(jnp.float32).reshape(B_, L, S5_GROUPS, S5_GROUP)
    lam = lax.complex(lam_re.astype(jnp.float32), lam_im.astype(jnp.float32))
    step = jnp.exp(log_step.astype(jnp.float32))[:, None]
    a_bar = jnp.exp(lam * step)
    b = lax.complex(b_re.astype(jnp.float32), b_im.astype(jnp.float32))
    b_bar = ((a_bar - 1.0) / lam)[..., None] * b
    bu = jnp.einsum('gnp,blgp->blgn', b_bar, uf.astype(jnp.complex64))
    a_seq = jnp.broadcast_to(a_bar, bu.shape)
    _, states = lax.associative_scan(_ssm_combine, (a_seq, bu), axis=1)
    cm = lax.complex(c_re.astype(jnp.float32), c_im.astype(jnp.float32))
    y = jnp.einsum('gpn,blgn->blgp', cm, states).real + d_skip.astype(jnp.float32).reshape(S5_GROUPS, S5_GROUP) * uf
    return y.reshape(B_, L, S5_CH)


def even_mixer(h, w_in, w_out, t5_table, lam_params, subln_g, rel_bias, lam_init):
    B_, L = h.shape[0], h.shape[1]
    wa = N_HEADS_A * DV_A
    wb = N_HEADS_B * DH_B
    proj = h @ w_in
    qa, ka, va, qb, kb, vb = jnp.split(proj, [wa, 2 * wa, 3 * wa, 3 * wa + wb, 3 * wa + 2 * wb], axis=-1)
    lp = lam_params.astype(jnp.float32)
    lam = jnp.exp(jnp.sum(lp[0] * lp[1])) - jnp.exp(jnp.sum(lp[2] * lp[3])) + lam_init
    out_a = diff_attention(qa.reshape(B_, L, N_HEADS_A, 2, DK_A), ka.reshape(B_, L, N_HEADS_A, 2, DK_A),
                           va.reshape(B_, L, N_HEADS_A, DV_A), t5_table, lam, lam_init, subln_g)
    out_b = band_attention(qb.reshape(B_, L, N_HEADS_B, DH_B), kb.reshape(B_, L, N_HEADS_B, DH_B),
                           vb.reshape(B_, L, N_HEADS_B, DH_B), rel_bias)
    return jnp.concatenate([out_a, out_b], axis=-1).astype(h.dtype) @ w_out


def odd_mixer(h, w_in, w_out, lam_re, lam_im, log_step, b_re, b_im, c_re, c_im, d_skip, glu_w):
    B_, L = h.shape[0], h.shape[1]
    wqk = N_HEADS_C * DQK_C
    wv = N_HEADS_C * DV_C
    proj = h @ w_in
    q, k, v, gate, u = jnp.split(proj, [wqk, 2 * wqk, 2 * wqk + wv, 2 * wqk + 2 * wv], axis=-1)
    q = rotary(q.astype(jnp.float32).reshape(B_, L, N_HEADS_C, DQK_C))
    k = rotary(k.astype(jnp.float32).reshape(B_, L, N_HEADS_C, DQK_C))
    r = retention(q, k, v.astype(jnp.float32).reshape(B_, L, N_HEADS_C, DV_C))
    r = rmsnorm(r).reshape(B_, L, wv) * jax.nn.silu(gate.astype(jnp.float32))
    y = jax.nn.gelu(s5_ssm(u, lam_re, lam_im, log_step, b_re, b_im, c_re, c_im, d_skip))
    ga, gb = jnp.split(y.astype(h.dtype) @ glu_w, 2, axis=-1)
    y = ga.astype(jnp.float32) * jax.nn.sigmoid(gb.astype(jnp.float32))
    return jnp.concatenate([r, y], axis=-1).astype(h.dtype) @ w_out


def conv_ffn(h, w_in, conv_w, conv_b, w_out):
    up = h @ w_in
    val, gate = jnp.split(up, 2, axis=-1)
    gate = lax.conv_general_dilated(gate, conv_w[:, None, :], (1,), [(CONV_W - 1, 0)],
                                    dimension_numbers=('NWC', 'WIO', 'NWC'),
                                    feature_group_count=D_FF) + conv_b
    return (jax.nn.gelu(gate) * val) @ w_out


def setup_inputs(seed: int = 0) -> dict:
    key = jax.random.key(seed)
    ks = jax.random.split(key, 32)
    f32 = jnp.float32

    def nrm(i, shape, scale):
        return scale * jax.random.normal(ks[i], shape, f32)

    n_idx = jnp.arange(S5_STATE, dtype=f32)
    return {
        'x': nrm(0, (BATCH, SEQ, D_MODEL), 1.0),
        'c': nrm(1, (BATCH, D_MODEL), 1.0),
        't5_table': nrm(2, (NUM_BUCKETS, N_HEADS_A), 0.5),
        'mod_w': nrm(3, (DEPTH, D_MODEL, 6 * D_MODEL), 0.5 * D_MODEL ** -0.5),
        'mod_b': nrm(4, (DEPTH, 6 * D_MODEL), 0.02),
        'norm1_g': 1.0 + nrm(5, (DEPTH, D_MODEL), 0.02),
        'norm2_g': 1.0 + nrm(6, (DEPTH, D_MODEL), 0.02),
        'ffn_w_in': nrm(7, (DEPTH, D_MODEL, 2 * D_FF), D_MODEL ** -0.5),
        'ffn_conv_w': nrm(8, (DEPTH, CONV_W, D_FF), CONV_W ** -0.5),
        'ffn_conv_b': nrm(9, (DEPTH, D_FF), 0.02),
        'ffn_w_out': nrm(10, (DEPTH, D_FF, D_MODEL), D_FF ** -0.5),
        'ev_w_in': nrm(11, (N_EVEN, D_MODEL, EVEN_IN), D_MODEL ** -0.5),
        'ev_w_out': nrm(12, (N_EVEN, 2 * GROUP_WIDTH, D_MODEL), (2 * GROUP_WIDTH) ** -0.5),
        'diff_lambda': nrm(13, (N_EVEN, 4, DK_A), 0.1),
        'diff_subln_g': 1.0 + nrm(14, (N_EVEN, DV_A), 0.02),
        'band_rel_bias': nrm(15, (N_EVEN, N_HEADS_B, 2 * REL_CLIP + 1), 0.5),
        'od_w_in': nrm(16, (N_ODD, D_MODEL, ODD_IN), D_MODEL ** -0.5),
        'od_w_out': nrm(17, (N_ODD, 2 * GROUP_WIDTH, D_MODEL), (2 * GROUP_WIDTH) ** -0.5),
        's5_lam_re': -0.5 + nrm(18, (N_ODD, S5_GROUPS, S5_STATE), 0.01),
        's5_lam_im': math.pi * n_idx + nrm(19, (N_ODD, S5_GROUPS, S5_STATE), 0.01),
        's5_log_step': jax.random.uniform(ks[20], (N_ODD, S5_GROUPS), f32, math.log(1e-3), math.log(1e-1)),
        's5_b_re': nrm(21, (N_ODD, S5_GROUPS, S5_STATE, S5_GROUP), (2 * S5_GROUP) ** -0.5),
        's5_b_im': nrm(22, (N_ODD, S5_GROUPS, S5_STATE, S5_GROUP), (2 * S5_GROUP) ** -0.5),
        's5_c_re': nrm(23, (N_ODD, S5_GROUPS, S5_GROUP, S5_STATE), (2 * S5_STATE) ** -0.5),
        's5_c_im': nrm(24, (N_ODD, S5_GROUPS, S5_GROUP, S5_STATE), (2 * S5_STATE) ** -0.5),
        's5_d': nrm(25, (N_ODD, S5_CH), 1.0),
        's5_glu_w': nrm(26, (N_ODD, S5_CH, 2 * S5_CH), S5_CH ** -0.5),
        'final_g': 1.0 + nrm(27, (D_MODEL,), 0.02),
    }


def reference(x, c, t5_table, mod_w, mod_b, norm1_g, norm2_g, ffn_w_in, ffn_conv_w, ffn_conv_b, ffn_w_out,
              ev_w_in, ev_w_out, diff_lambda, diff_subln_g, band_rel_bias,
              od_w_in, od_w_out, s5_lam_re, s5_lam_im, s5_log_step, s5_b_re, s5_b_im, s5_c_re, s5_c_im,
              s5_d, s5_glu_w, final_g):
    cond = jax.nn.silu(c)
    for i in range(DEPTH):
        mod = cond @ mod_w[i] + mod_b[i]
        sh1, sc1, g1, sh2, sc2, g2 = jnp.split(mod, 6, axis=-1)
        h = modulated_rmsnorm(x, norm1_g[i], sh1, sc1)
        if i % 2 == 0:
            e = i // 2
            lam_init = 0.8 - 0.6 * math.exp(-0.3 * i)
            mixed = even_mixer(h, ev_w_in[e], ev_w_out[e], t5_table, diff_lambda[e], diff_subln_g[e],
                               band_rel_bias[e], lam_init)
        else:
            o = i // 2
            mixed = odd_mixer(h, od_w_in[o], od_w_out[o], s5_lam_re[o], s5_lam_im[o], s5_log_step[o],
                              s5_b_re[o], s5_b_im[o], s5_c_re[o], s5_c_im[o], s5_d[o], s5_glu_w[o])
        x = x + g1[:, None, :] * mixed
        h = modulated_rmsnorm(x, norm2_g[i], sh2, sc2)
        x = x + g2[:, None, :] * conv_ffn(h, ffn_w_in[i], ffn_conv_w[i], ffn_conv_b[i], ffn_w_out[i])
    return rmsnorm(x, final_g).astype(x.dtype)
```

```python
import functools
import math

import jax
import jax.numpy as jnp
from jax import lax
from jax.experimental import pallas as pl
from jax.experimental.pallas import tpu as pltpu

F32 = jnp.float32
BF16 = jnp.bfloat16

D_MODEL = 1024
DEPTH = 2
CHUNK = 64
GROUP_WIDTH = D_MODEL // 2
DK_A = 64
DV_A = 2 * DK_A
N_HEADS_A = GROUP_WIDTH // DV_A
DH_B = 64
N_HEADS_B = GROUP_WIDTH // DH_B
LEFT_CHUNKS = 8
REL_CLIP = 2 * CHUNK
NUM_BUCKETS = 32
MAX_DISTANCE = 128
DV_C = 128
DQK_C = DV_C // 2
N_HEADS_C = GROUP_WIDTH // DV_C
ROPE_BASE = 10000.0
S5_CH = GROUP_WIDTH
S5_GROUP = 16
S5_GROUPS = S5_CH // S5_GROUP
S5_STATE = 64
D_FF = ((8 * D_MODEL // 3 + 255) // 256) * 256
CONV_W = 3
EVEN_IN = 3 * N_HEADS_A * DV_A + 3 * N_HEADS_B * DH_B
ODD_IN = 2 * N_HEADS_C * DQK_C + 2 * N_HEADS_C * DV_C + S5_CH
EPS = 1e-6
NEG_INF = -1e30

LANES = 128
MXU_DIM = 256

TM_PROJ = 1024
TM_FFN = 512
TF_FFN = MXU_DIM
BLK_A = 512
BLK_B = 512
BLK_C = 256
S5_T = 16
S5_TC = LANES

assert BLK_B == LEFT_CHUNKS * CHUNK, "band window must be exactly one previous block"
assert BLK_A >= MAX_DISTANCE, "far key blocks must sit in the saturated T5 bucket"


def _dot(a, b):
    return jnp.dot(a, b, preferred_element_type=F32)


def _dot_nt(a, b):
    return lax.dot_general(a, b, (((1,), (1,)), ((), ())), preferred_element_type=F32)


def _dot_tn(a, b):
    return lax.dot_general(a, b, (((0,), (0,)), ((), ())), preferred_element_type=F32)


def _const_spec(shape):
    zeros = (0,) * len(shape)
    return pl.BlockSpec(shape, lambda *_: zeros, pipeline_mode=pl.Buffered(1))


def _mod_rmsnorm(x, g, scale, shift):
    y = x * lax.rsqrt(jnp.mean(x * x, axis=-1, keepdims=True) + EPS)
    y = y * g
    return y * (1.0 + scale) + shift


def _mod_kernel(c_ref, w_ref, b_ref, o_ref):
    c = c_ref[...]
    cond = c * jax.nn.sigmoid(c)
    o_ref[0] = jnp.sum(cond * w_ref[0], axis=0, keepdims=True) + b_ref[0]


def _modulation(c, mod_w, mod_b):
    depth, d, n = mod_w.shape
    tn = 1536
    return pl.pallas_call(
        _mod_kernel,
        grid=(depth, n // tn),
        in_specs=[
            pl.BlockSpec((d, 1), lambda i, j: (0, 0)),
            pl.BlockSpec((1, d, tn), lambda i, j: (i, 0, j)),
            pl.BlockSpec((1, 1, tn), lambda i, j: (i, 0, j)),
        ],
        out_specs=pl.BlockSpec((1, 1, tn), lambda i, j: (i, 0, j)),
        out_shape=jax.ShapeDtypeStruct((depth, 1, n), F32),
        name="modulation",
    )(c.reshape(d, 1), mod_w, mod_b.reshape(depth, 1, n))


def _normproj_kernel(x_ref, g_ref, sc_ref, sh_ref, w_ref, o_ref, h_ref):
    @pl.when(pl.program_id(1) == 0)
    def _():
        h_ref[...] = _mod_rmsnorm(x_ref[...], g_ref[...], sc_ref[...], sh_ref[...]).astype(BF16)

    o_ref[...] = _dot(h_ref[...], w_ref[...]).astype(o_ref.dtype)


def _normproj(x, g, scale, shift, w, out_dtype):
    seq, d = x.shape
    n = w.shape[1]
    tm, tn = TM_PROJ, 1024
    row = pl.BlockSpec((1, d), lambda i, j: (0, 0))
    return pl.pallas_call(
        _normproj_kernel,
        grid=(seq // tm, n // tn),
        in_specs=[
            pl.BlockSpec((tm, d), lambda i, j: (i, 0)),
            row, row, row,
            pl.BlockSpec((d, tn), lambda i, j: (0, j)),
        ],
        out_specs=pl.BlockSpec((tm, tn), lambda i, j: (i, j)),
        out_shape=jax.ShapeDtypeStruct((seq, n), out_dtype),
        scratch_shapes=[pltpu.VMEM((tm, d), BF16)],
        compiler_params=pltpu.CompilerParams(dimension_semantics=("parallel", "arbitrary")),
        name="normproj",
    )(x, g.reshape(1, d), scale, shift, w)


def _diffattn_kernel(q_ref, k_ref, v_ref, bprev_ref, bdiag_ref, lam_ref, g_ref, o_ref,
                     qs_ref, m_ref, l_ref, acc_ref, *, out_scale):
    blk = BLK_A
    i = pl.program_id(1)
    q = q_ref[...] * (DK_A ** -0.5)
    lane = lax.broadcasted_iota(jnp.int32, q.shape, 1)
    qs_ref[0:blk, :] = jnp.where(lane < DK_A, q, 0).astype(BF16)
    qs_ref[blk:2 * blk, :] = jnp.where(lane >= DK_A, q, 0).astype(BF16)
    m_ref[...] = jnp.full(m_ref.shape, NEG_INF, F32)
    l_ref[...] = jnp.zeros(l_ref.shape, F32)
    acc_ref[...] = jnp.zeros(acc_ref.shape, F32)

    def update(kstart, bias):
        k = k_ref[pl.ds(kstart, blk), :]
        v = v_ref[pl.ds(kstart, blk), :]
        s = _dot_nt(qs_ref[...], k)
        if bias is not None:
            s = s + jnp.concatenate([bias, bias], axis=0)
        m_prev = m_ref[...]
        m_new = jnp.maximum(m_prev, jnp.max(s, axis=-1, keepdims=True))
        alpha = jnp.exp(m_prev - m_new)
        p = jnp.exp(s - m_new)
        l_ref[...] = alpha * l_ref[...] + jnp.sum(p, axis=-1, keepdims=True)
        acc_ref[...] = alpha * acc_ref[...] + _dot(p.astype(BF16), v)
        m_ref[...] = m_new

    def far_body(j, carry):
        update(pl.multiple_of(j * blk, blk), None)
        return carry

    lax.fori_loop(0, jnp.maximum(i - 1, 0), far_body, 0)

    @pl.when(i > 0)
    def _():
        update(pl.multiple_of((i - 1) * blk, blk), bprev_ref[0])

    update(pl.multiple_of(i * blk, blk), bdiag_ref[0])

    o = acc_ref[...] / l_ref[...]
    o = o[0:blk] - lam_ref[...] * o[blk:2 * blk]
    o = o * lax.rsqrt(jnp.mean(o * o, axis=-1, keepdims=True) + EPS) * g_ref[...]
    o_ref[...] = (o * out_scale).astype(o_ref.dtype)


def _t5_bucket(rel):
    nb = NUM_BUCKETS // 2
    max_exact = nb // 2
    bucket = jnp.where(rel > 0, nb, 0)
    n = jnp.abs(rel)
    nf = jnp.maximum(n, 1).astype(F32)
    large = max_exact + (jnp.log(nf / max_exact) / math.log(MAX_DISTANCE / max_exact)
                         * (nb - max_exact)).astype(jnp.int32)
    large = jnp.minimum(large, nb - 1)
    return bucket + jnp.where(n < max_exact, n, large)


def _diff_bias_tiles(t5_table):
    blk = BLK_A
    table = t5_table.astype(F32)
    qpos = jnp.arange(blk, dtype=jnp.int32)[:, None]
    kpos = jnp.arange(blk, dtype=jnp.int32)[None, :]
    far = table[_t5_bucket(jnp.full((), -(blk + 1), jnp.int32))]
    prev = jnp.moveaxis(table[_t5_bucket(kpos - blk - qpos)] - far, -1, 0)
    diag = jnp.moveaxis(table[_t5_bucket(kpos - qpos)] - far, -1, 0)
    visible = (kpos // CHUNK) <= (qpos // CHUNK)
    diag = jnp.where(visible[None], diag, NEG_INF)
    return prev, diag


def _diff_attention(proj, t5_table, lam, subln_g, lam_init):
    seq = proj.shape[0]
    blk = BLK_A
    prev, diag = _diff_bias_tiles(t5_table)
    ha = N_HEADS_A
    kern = functools.partial(_diffattn_kernel, out_scale=1.0 - lam_init)
    return pl.pallas_call(
        kern,
        grid=(ha, seq // blk),
        in_specs=[
            pl.BlockSpec((blk, DV_A), lambda h, i: (i, h)),
            pl.BlockSpec((seq, DV_A), lambda h, i: (0, ha + h)),
            pl.BlockSpec((seq, DV_A), lambda h, i: (0, 2 * ha + h)),
            pl.BlockSpec((1, blk, blk), lambda h, i: (h, 0, 0)),
            pl.BlockSpec((1, blk, blk), lambda h, i: (h, 0, 0)),
            pl.BlockSpec((1, DV_A), lambda h, i: (0, 0)),
            pl.BlockSpec((1, DV_A), lambda h, i: (0, 0)),
        ],
        out_specs=pl.BlockSpec((blk, DV_A), lambda h, i: (i, h)),
        out_shape=jax.ShapeDtypeStruct((seq, ha * DV_A), BF16),
        scratch_shapes=[
            pltpu.VMEM((2 * blk, DV_A), BF16),
            pltpu.VMEM((2 * blk, 1), F32),
            pltpu.VMEM((2 * blk, 1), F32),
            pltpu.VMEM((2 * blk, DV_A), F32),
        ],
        compiler_params=pltpu.CompilerParams(dimension_semantics=("parallel", "arbitrary")),
        name="diff_attention",
    )(proj, proj, proj, prev, diag, jnp.full((1, DV_A), lam, F32), subln_g.reshape(1, DV_A).astype(F32))


def _band_kernel(q_ref, kp_ref, kc_ref, vp_ref, vc_ref, bias_ref, o_ref):
    blk = BLK_B
    i = pl.program_id(1)
    q = q_ref[...] * (DH_B ** -0.5)
    lane = lax.broadcasted_iota(jnp.int32, q.shape, 1)
    kp, kc, vp, vc = kp_ref[...], kc_ref[...], vp_ref[...], vc_ref[...]
    no_prev = jnp.where(i == 0, NEG_INF, 0.0).astype(F32)
    outs = []
    for hh in range(2):
        sel = (lane < DH_B) if hh == 0 else (lane >= DH_B)
        qm = jnp.where(sel, q, 0).astype(BF16)
        sp = _dot_nt(qm, kp) + bias_ref[hh, :, 0:blk] + no_prev
        sc = _dot_nt(qm, kc) + bias_ref[hh, :, blk:2 * blk]
        m = jnp.maximum(jnp.max(sp, axis=-1, keepdims=True), jnp.max(sc, axis=-1, keepdims=True))
        pp = jnp.exp(sp - m)
        pc = jnp.exp(sc - m)
        l = jnp.sum(pp, axis=-1, keepdims=True) + jnp.sum(pc, axis=-1, keepdims=True)
        outs.append((_dot(pp.astype(BF16), vp) + _dot(pc.astype(BF16), vc)) / l)
    o_ref[...] = jnp.where(lane < DH_B, outs[0], outs[1]).astype(o_ref.dtype)


def _band_bias_tiles(rel_bias):
    blk = BLK_B
    iq = jnp.arange(blk, dtype=jnp.int32)[:, None]
    jk = jnp.arange(2 * blk, dtype=jnp.int32)[None, :] - blk
    d = jk - iq
    qchunk = iq // CHUNK
    kchunk = jnp.floor_divide(jk, CHUNK)
    valid = (kchunk <= qchunk) & (kchunk >= qchunk - LEFT_CHUNKS)
    bias = rel_bias.astype(F32)[:, jnp.clip(d, -REL_CLIP, REL_CLIP) + REL_CLIP]
    return jnp.where(valid[None], bias, NEG_INF)


def _band_attention(proj, rel_bias):
    seq = proj.shape[0]
    blk = BLK_B
    bias = _band_bias_tiles(rel_bias)
    npair = N_HEADS_B // 2
    qc0 = 3 * N_HEADS_A
    prev = lambda c0: (lambda hp, i: (jnp.maximum(i - 1, 0), c0 + hp))
    cur = lambda c0: (lambda hp, i: (i, c0 + hp))
    return pl.pallas_call(
        _band_kernel,
        grid=(npair, seq // blk),
        in_specs=[
            pl.BlockSpec((blk, LANES), cur(qc0)),
            pl.BlockSpec((blk, LANES), prev(qc0 + npair)),
            pl.BlockSpec((blk, LANES), cur(qc0 + npair)),
            pl.BlockSpec((blk, LANES), prev(qc0 + 2 * npair)),
            pl.BlockSpec((blk, LANES), cur(qc0 + 2 * npair)),
            pl.BlockSpec((2, blk, 2 * blk), lambda hp, i: (hp, 0, 0)),
        ],
        out_specs=pl.BlockSpec((blk, LANES), lambda hp, i: (i, hp)),
        out_shape=jax.ShapeDtypeStruct((seq, N_HEADS_B * DH_B), BF16),
        compiler_params=pltpu.CompilerParams(dimension_semantics=("parallel", "arbitrary")),
        name="band_attention",
    )(proj, proj, proj, proj, proj, bias)


def _retention_kernel(qk_ref, v_ref, gate_ref, cos_ref, sin_ref, qdec_ref, kdec_ref, dmat_ref,
                      sdec_ref, o_ref, state_ref):
    @pl.when(pl.program_id(0) == 0)
    def _():
        state_ref[...] = jnp.zeros(state_ref.shape, F32)

    cos = cos_ref[...]
    sin = sin_ref[...]
    lane = lax.broadcasted_iota(jnp.int32, cos.shape, 1)
    first_half = (lane % DQK_C) < (DQK_C // 2)
    qk = qk_ref[...]
    parts = []
    for j in range(qk.shape[1] // LANES):
        t = qk[:, j * LANES:(j + 1) * LANES]
        partner = jnp.where(first_half, pltpu.roll(t, LANES - DQK_C // 2, 1), pltpu.roll(t, DQK_C // 2, 1))
        parts.append(t * cos + partner * sin)
    wq = N_HEADS_C * DQK_C
    q = jnp.concatenate(parts[:wq // LANES], axis=1)
    k = jnp.concatenate(parts[wq // LANES:], axis=1) * (DQK_C ** -0.5)
    qd = (q * qdec_ref[...]).astype(BF16)
    kd = (k * kdec_ref[...]).astype(BF16)
    qb = q.astype(BF16)
    kb = k.astype(BF16)
    vb = v_ref[...].astype(BF16)
    gate = gate_ref[...]
    outs = []
    for h in range(N_HEADS_C):
        qs = slice(h * DQK_C, (h + 1) * DQK_C)
        vs = slice(h * DV_C, (h + 1) * DV_C)
        scores = _dot_nt(qb[:, qs], kb[:, qs]) * dmat_ref[h]
        state = state_ref[h]
        r = _dot(scores.astype(BF16), vb[:, vs]) + _dot(qd[:, qs], state.astype(BF16))
        state_ref[h] = state * sdec_ref[h] + _dot_tn(kd[:, qs], vb[:, vs])
        r = r * lax.rsqrt(jnp.mean(r * r, axis=-1, keepdims=True) + EPS)
        g = gate[:, vs]
        outs.append(r * (g * jax.nn.sigmoid(g)))
    o_ref[...] = jnp.concatenate(outs, axis=1).astype(o_ref.dtype)


def _retention_tables(seq):
    t = BLK_C
    half = DQK_C // 2
    inv_freq = 1.0 / (ROPE_BASE ** (jnp.arange(0, DQK_C, 2, dtype=F32) / DQK_C))
    ang = jnp.arange(seq, dtype=F32)[:, None] * inv_freq[None, :]
    reps = LANES // half
    cos = jnp.tile(jnp.cos(ang), (1, reps))
    sign = jnp.where((jnp.arange(LANES) % DQK_C) < half, -1.0, 1.0).astype(F32)
    sin = jnp.tile(jnp.sin(ang), (1, reps)) * sign[None, :]
    log_g = jnp.log(1.0 - jnp.power(2.0, -5.0 - jnp.arange(N_HEADS_C, dtype=F32)))
    pos = jnp.arange(t, dtype=F32)
    diff = pos[:, None] - pos[None, :]
    same_or_past = (jnp.arange(t)[None, :] // CHUNK) <= (jnp.arange(t)[:, None] // CHUNK)
    dmat = jnp.where(same_or_past[None], jnp.exp(log_g[:, None, None] * jnp.abs(diff)[None]), 0.0)
    qdec = jnp.repeat(jnp.exp(log_g[None, :] * (pos[:, None] + 1.0)), DQK_C, axis=1)
    kdec = jnp.repeat(jnp.exp(log_g[None, :] * (t - 1.0 - pos[:, None])), DQK_C, axis=1)
    sdec = jnp.broadcast_to(jnp.exp(log_g * t)[:, None, None], (N_HEADS_C, 1, DV_C))
    return cos, sin, qdec, kdec, dmat, sdec


def _retention(proj):
    seq = proj.shape[0]
    t = BLK_C
    cos, sin, qdec, kdec, dmat, sdec = _retention_tables(seq)
    wv = N_HEADS_C * DV_C
    return pl.pallas_call(
        _retention_kernel,
        grid=(seq // t,),
        in_specs=[
            pl.BlockSpec((t, wv), lambda i: (i, 0)),
            pl.BlockSpec((t, wv), lambda i: (i, 1)),
            pl.BlockSpec((t, wv), lambda i: (i, 2)),
            pl.BlockSpec((t, LANES), lambda i: (i, 0)),
            pl.BlockSpec((t, LANES), lambda i: (i, 0)),
            pl.BlockSpec((t, N_HEADS_C * DQK_C), lambda i: (0, 0)),
            pl.BlockSpec((t, N_HEADS_C * DQK_C), lambda i: (0, 0)),
            pl.BlockSpec((N_HEADS_C, t, t), lambda i: (0, 0, 0)),
            pl.BlockSpec((N_HEADS_C, 1, DV_C), lambda i: (0, 0, 0)),
        ],
        out_specs=pl.BlockSpec((t, wv), lambda i: (i, 0)),
        out_shape=jax.ShapeDtypeStruct((seq, wv), BF16),
        scratch_shapes=[pltpu.VMEM((N_HEADS_C, DQK_C, DV_C), F32)],
        compiler_params=pltpu.CompilerParams(dimension_semantics=("arbitrary",)),
        name="retention",
    )(proj, proj, proj, cos, sin, qdec, kdec, dmat, sdec)


def _s5_kernel(*refs):
    u_refs = refs[:S5_T]
    mt_ref, bt_ref, ct_ref, are_ref, aim_ref, y2_ref, ut_ref, yt_ref, vr_ref, vi_ref, spr_ref, spi_ref, carry_ref = refs[S5_T:]
    tc = S5_TC
    gp = S5_GROUP
    n = S5_STATE
    half = S5_CH // 2

    @pl.when(pl.program_id(0) == 0)
    def _():
        carry_ref[...] = jnp.zeros(carry_ref.shape, F32)

    for s in range(S5_T):
        for hh in range(2):
            ut_ref[s, hh * half:(hh + 1) * half, :] = u_refs[s][:, hh * half:(hh + 1) * half].T

    def intra(g, carry):
        r0 = pl.multiple_of(g * gp, gp)
        ug = ut_ref[:, pl.ds(r0, gp), :].reshape(S5_T * gp, tc).astype(BF16)
        yt_ref[:, pl.ds(r0, gp), :] = _dot(mt_ref[g], ug).reshape(S5_T, gp, tc)
        vt = _dot(bt_ref[g], ug)
        n0 = pl.multiple_of(g * n, n)
        vr_ref[pl.ds(n0, n), :] = vt[0:n]
        vi_ref[pl.ds(n0, n), :] = vt[n:2 * n]
        return carry

    lax.fori_loop(0, S5_GROUPS, intra, 0)

    ar, ai = are_ref[...], aim_ref[...]
    cr, ci = carry_ref[0], carry_ref[1]
    xr, xi = vr_ref[...], vi_ref[...]
    lane = lax.broadcasted_iota(jnp.int32, xr.shape, 1)
    first = lane == 0
    xr = xr + jnp.where(first, ar * cr - ai * ci, 0.0)
    xi = xi + jnp.where(first, ar * ci + ai * cr, 0.0)
    pr, pi = ar, ai
    d = 1
    while d < tc:
        sr = jnp.where(lane >= d, pltpu.roll(xr, d, 1), 0.0)
        si = jnp.where(lane >= d, pltpu.roll(xi, d, 1), 0.0)
        xr, xi = xr + (pr * sr - pi * si), xi + (pr * si + pi * sr)
        pr, pi = pr * pr - pi * pi, 2.0 * pr * pi
        d *= 2
    spr_ref[...] = jnp.where(first, cr, pltpu.roll(xr, 1, 1))
    spi_ref[...] = jnp.where(first, ci, pltpu.roll(xi, 1, 1))
    carry_ref[0] = jnp.broadcast_to(xr[:, tc - 1:tc], xr.shape)
    carry_ref[1] = jnp.broadcast_to(xi[:, tc - 1:tc], xi.shape)

    def cross(g, carry):
        r0 = pl.multiple_of(g * gp, gp)
        n0 = pl.multiple_of(g * n, n)
        sp = jnp.concatenate([spr_ref[pl.ds(n0, n), :], spi_ref[pl.ds(n0, n), :]], axis=0).astype(BF16)
        yt_ref[:, pl.ds(r0, gp), :] += _dot(ct_ref[g], sp).reshape(S5_T, gp, tc)
        return carry

    lax.fori_loop(0, S5_GROUPS, cross, 0)

    for s in range(S5_T):
        for hh in range(2):
            c0 = s * S5_CH + hh * half
            y2_ref[:, c0:c0 + half] = yt_ref[s, hh * half:(hh + 1) * half, :].T


def _s5_matrices(lam_re, lam_im, log_step, b_re, b_im, c_re, c_im, d_skip):
    hi = lax.Precision.HIGHEST
    t, gp, n, ng = S5_T, S5_GROUP, S5_STATE, S5_GROUPS
    lam = lax.complex(lam_re.astype(F32), lam_im.astype(F32))
    step = jnp.exp(log_step.astype(F32))[:, None]
    ls = lam * step
    a_bar = jnp.exp(ls)
    b_bar = ((a_bar - 1.0) / lam)[..., None] * lax.complex(b_re.astype(F32), b_im.astype(F32))
    cm = lax.complex(c_re.astype(F32), c_im.astype(F32))
    k = jnp.arange(t + 1, dtype=F32)
    apow = jnp.exp(ls[:, None, :] * k[None, :, None].astype(jnp.complex64))
    kmat = jnp.einsum('gpn,gln,gnq->glpq', cm, apow[:, :t], b_bar, precision=hi).real
    lag = jnp.arange(t)[:, None] - jnp.arange(t)[None, :]
    toep = jnp.where((lag >= 0)[None, :, :, None, None], kmat[:, jnp.clip(lag, 0, t - 1)], 0.0)
    mt = jnp.transpose(toep, (0, 1, 3, 2, 4)).reshape(ng, t * gp, t * gp)
    dvec = jnp.tile(d_skip.astype(F32).reshape(ng, 1, gp), (1, t, 1)).reshape(ng, t * gp)
    mt = mt + jnp.eye(t * gp, dtype=F32)[None] * dvec[:, :, None]
    z = apow[:, t - 1 - jnp.arange(t)][:, :, :, None] * b_bar[:, None, :, :]
    z = jnp.transpose(z, (0, 2, 1, 3)).reshape(ng, n, t * gp)
    bt = jnp.concatenate([z.real, z.imag], axis=1)
    w = cm[:, None, :, :] * apow[:, 1:t + 1][:, :, None, :]
    w = w.reshape(ng, t * gp, n)
    ct = jnp.concatenate([w.real, -w.imag], axis=2)
    a_chunk = apow[:, t].reshape(ng * n, 1)
    are = jnp.broadcast_to(a_chunk.real, (ng * n, LANES))
    aim = jnp.broadcast_to(a_chunk.imag, (ng * n, LANES))
    return mt.astype(BF16), bt.astype(BF16), ct.astype(BF16), are, aim


def _s5(proj, mats):
    seq, width = proj.shape
    t, tc, gp, n, ng = S5_T, S5_TC, S5_GROUP, S5_STATE, S5_GROUPS
    nchunks = seq // t
    mt, bt, ct, are, aim = mats
    pview = proj.reshape(nchunks, t * width)
    cpb = width // S5_CH
    u_specs = [pl.BlockSpec((tc, S5_CH), (lambda i, s=s: (i, s * cpb + cpb - 1))) for s in range(t)]
    y2 = pl.pallas_call(
        _s5_kernel,
        grid=(nchunks // tc,),
        in_specs=u_specs + [
            _const_spec(mt.shape), _const_spec(bt.shape), _const_spec(ct.shape),
            _const_spec(are.shape), _const_spec(aim.shape),
        ],
        out_specs=pl.BlockSpec((tc, t * S5_CH), lambda i: (i, 0)),
        out_shape=jax.ShapeDtypeStruct((nchunks, t * S5_CH), F32),
        scratch_shapes=[
            pltpu.VMEM((t, S5_CH, tc), F32),
            pltpu.VMEM((t, S5_CH, tc), F32),
            pltpu.VMEM((ng * n, tc), F32),
            pltpu.VMEM((ng * n, tc), F32),
            pltpu.VMEM((ng * n, tc), F32),
            pltpu.VMEM((ng * n, tc), F32),
            pltpu.VMEM((2, ng * n, tc), F32),
        ],
        compiler_params=pltpu.CompilerParams(dimension_semantics=("arbitrary",)),
        name="s5_scan",
    )(*([pview] * t), mt, bt, ct, are, aim)
    return y2.reshape(seq, S5_CH)


def _outproj_kernel(*refs, glu):
    if glu:
        x_ref, a_ref, b_ref, wo_ref, g_ref, gw_ref, o_ref = refs
        y = jax.nn.gelu(b_ref[...]).astype(BF16)
        gg = _dot(y, gw_ref[...])
        half = gg.shape[1] // 2
        b = (gg[:, :half] * jax.nn.sigmoid(gg[:, half:])).astype(BF16)
    else:
        x_ref, a_ref, b_ref, wo_ref, g_ref, o_ref = refs
        b = b_ref[...]
    cat = jnp.concatenate([a_ref[...], b], axis=1)
    o_ref[...] = x_ref[...] + g_ref[...] * _dot(cat, wo_ref[...])


def _outproj(x, a, b, wo, gate, glu_w=None):
    seq, d = x.shape
    tm = TM_PROJ
    wa, wb = a.shape[1], b.shape[1]
    in_specs = [
        pl.BlockSpec((tm, d), lambda i: (i, 0)),
        pl.BlockSpec((tm, wa), lambda i: (i, 0)),
        pl.BlockSpec((tm, wb), lambda i: (i, 0)),
        _const_spec(wo.shape),
        pl.BlockSpec((1, d), lambda i: (0, 0)),
    ]
    args = [x, a, b, wo, gate]
    if glu_w is not None:
        in_specs.append(_const_spec(glu_w.shape))
        args.append(glu_w)
    return pl.pallas_call(
        functools.partial(_outproj_kernel, glu=glu_w is not None),
        grid=(seq // tm,),
        in_specs=in_specs,
        out_specs=pl.BlockSpec((tm, d), lambda i: (i, 0)),
        out_shape=jax.ShapeDtypeStruct((seq, d), F32),
        compiler_params=pltpu.CompilerParams(dimension_semantics=("parallel",)),
        name="outproj",
    )(*args)


def _ffn_kernel(x_ref, g_ref, sc_ref, sh_ref, gate_ref, win_ref, cw_ref, cb_ref, wout_ref, fg_ref,
                o_ref, h_ref, act_ref, gbuf_ref, carry_ref, *, final):
    tm = x_ref.shape[0]
    halo = gbuf_ref.shape[0] - tm

    @pl.when(pl.program_id(0) == 0)
    def _():
        carry_ref[...] = jnp.zeros(carry_ref.shape, F32)

    x = x_ref[...]
    h_ref[...] = _mod_rmsnorm(x, g_ref[...], sc_ref[...], sh_ref[...]).astype(BF16)
    for f in range(D_FF // TF_FFN):
        cs = slice(f * TF_FFN, (f + 1) * TF_FFN)
        gs = slice(D_FF + f * TF_FFN, D_FF + (f + 1) * TF_FFN)
        h = h_ref[...]
        val = _dot(h, win_ref[:, cs])
        gate = _dot(h, win_ref[:, gs])
        gbuf_ref[0:halo, :] = carry_ref[:, cs]
        gbuf_ref[halo:halo + tm, :] = gate
        carry_ref[:, cs] = gate[tm - halo:tm, :]
        conv = (gate * cw_ref[2:3, cs] + gbuf_ref[halo - 1:halo - 1 + tm, :] * cw_ref[1:2, cs]
                + gbuf_ref[halo - 2:halo - 2 + tm, :] * cw_ref[0:1, cs] + cb_ref[:, cs])
        act_ref[:, cs] = (jax.nn.gelu(conv) * val).astype(BF16)
    xn = x + gate_ref[...] * _dot(act_ref[...], wout_ref[...])
    if final:
        xn = xn * lax.rsqrt(jnp.mean(xn * xn, axis=-1, keepdims=True) + EPS) * fg_ref[...]
    o_ref[...] = xn


def _ffn(x, g, scale, shift, gate, w_in, conv_w, conv_b, w_out, final_g, final):
    seq, d = x.shape
    tm = TM_FFN
    halo = 8
    row = pl.BlockSpec((1, d), lambda i: (0, 0))
    return pl.pallas_call(
        functools.partial(_ffn_kernel, final=final),
        grid=(seq // tm,),
        in_specs=[
            pl.BlockSpec((tm, d), lambda i: (i, 0)),
            row, row, row, row,
            _const_spec(w_in.shape),
            _const_spec(conv_w.shape),
            _const_spec((1, D_FF)),
            _const_spec(w_out.shape),
            row,
        ],
        out_specs=pl.BlockSpec((tm, d), lambda i: (i, 0)),
        out_shape=jax.ShapeDtypeStruct((seq, d), F32),
        scratch_shapes=[
            pltpu.VMEM((tm, d), BF16),
            pltpu.VMEM((tm, D_FF), BF16),
            pltpu.VMEM((tm + halo, TF_FFN), F32),
            pltpu.VMEM((halo, D_FF), F32),
        ],
        compiler_params=pltpu.CompilerParams(dimension_semantics=("arbitrary",)),
        name="conv_ffn",
    )(x, g.reshape(1, d), scale, shift, gate, w_in, conv_w, conv_b.reshape(1, D_FF), w_out,
      final_g.reshape(1, d))


def kernel(x, c, t5_table, mod_w, mod_b, norm1_g, norm2_g, ffn_w_in, ffn_conv_w, ffn_conv_b, ffn_w_out,
           ev_w_in, ev_w_out, diff_lambda, diff_subln_g, band_rel_bias,
           od_w_in, od_w_out, s5_lam_re, s5_lam_im, s5_log_step, s5_b_re, s5_b_im, s5_c_re, s5_c_im,
           s5_d, s5_glu_w, final_g):
    assert x.shape[0] == 1 and x.shape[2] == D_MODEL
    seq = x.shape[1]
    assert seq % TM_PROJ == 0 and seq % (S5_T * S5_TC) == 0
    d = D_MODEL
    xs = x[0]
    mod = _modulation(c, mod_w, mod_b)
    for i in range(DEPTH):
        sh1, sc1, g1, sh2, sc2, g2 = [mod[i, :, k * d:(k + 1) * d] for k in range(6)]
        if i % 2 == 0:
            e = i // 2
            lam_init = 0.8 - 0.6 * math.exp(-0.3 * i)
            lp = diff_lambda[e].astype(F32)
            lam = jnp.exp(jnp.sum(lp[0] * lp[1])) - jnp.exp(jnp.sum(lp[2] * lp[3])) + lam_init
            proj = _normproj(xs, norm1_g[i], sc1, sh1, ev_w_in[e].astype(BF16), BF16)
            out_a = _diff_attention(proj, t5_table, lam, diff_subln_g[e], lam_init)
            out_b = _band_attention(proj, band_rel_bias[e])
            xs = _outproj(xs, out_a, out_b, ev_w_out[e].astype(BF16), g1)
        else:
            o = i // 2
            proj = _normproj(xs, norm1_g[i], sc1, sh1, od_w_in[o].astype(BF16), F32)
            r = _retention(proj)
            mats = _s5_matrices(s5_lam_re[o], s5_lam_im[o], s5_log_step[o], s5_b_re[o], s5_b_im[o],
                                s5_c_re[o], s5_c_im[o], s5_d[o])
            y = _s5(proj, mats)
            xs = _outproj(xs, r, y, od_w_out[o].astype(BF16), g1, glu_w=s5_glu_w[o].astype(BF16))
        xs = _ffn(xs, norm2_g[i], sc2, sh2, g2, ffn_w_in[i].astype(BF16), ffn_conv_w[i], ffn_conv_b[i],
                  ffn_w_out[i].astype(BF16), final_g, final=(i == DEPTH - 1))
    return xs[None]
```

```python
import functools
import math

import jax
import jax.numpy as jnp
from jax import lax
from jax.experimental import pallas as pl
from jax.experimental.pallas import tpu as pltpu

F32 = jnp.float32
BF16 = jnp.bfloat16

D_MODEL = 1024
DEPTH = 2
CHUNK = 64
GROUP_WIDTH = D_MODEL // 2
DK_A = 64
DV_A = 2 * DK_A
N_HEADS_A = GROUP_WIDTH // DV_A
DH_B = 64
N_HEADS_B = GROUP_WIDTH // DH_B
LEFT_CHUNKS = 8
REL_CLIP = 2 * CHUNK
NUM_BUCKETS = 32
MAX_DISTANCE = 128
DV_C = 128
DQK_C = DV_C // 2
N_HEADS_C = GROUP_WIDTH // DV_C
ROPE_BASE = 10000.0
S5_CH = GROUP_WIDTH
S5_GROUP = 16
S5_GROUPS = S5_CH // S5_GROUP
S5_STATE = 64
D_FF = ((8 * D_MODEL // 3 + 255) // 256) * 256
CONV_W = 3
EVEN_IN = 3 * N_HEADS_A * DV_A + 3 * N_HEADS_B * DH_B
ODD_IN = 2 * N_HEADS_C * DQK_C + 2 * N_HEADS_C * DV_C + S5_CH
EPS = 1e-6
NEG_INF = -1e30
LOG2E = math.log2(math.e)

LANES = 128
MXU_DIM = 256

TM_PROJ = 1024
TM_FFN = 512
TF_FFN = MXU_DIM
BLK_A = 512
BLK_B = 512
BLK_C = 256
S5_T = 16
S5_TC = LANES

assert BLK_B == LEFT_CHUNKS * CHUNK, "band window must be exactly one previous block"
assert BLK_A >= MAX_DISTANCE, "far key blocks must sit in the saturated T5 bucket"
assert DV_A == LANES, "diff-attention statistics are kept lane-replicated beside the accumulator"


def _dot(a, b):
    return jnp.dot(a, b, preferred_element_type=F32)


def _dot_nt(a, b):
    return lax.dot_general(a, b, (((1,), (1,)), ((), ())), preferred_element_type=F32)


def _dot_tn(a, b):
    return lax.dot_general(a, b, (((0,), (0,)), ((), ())), preferred_element_type=F32)


def _const_spec(shape):
    zeros = (0,) * len(shape)
    return pl.BlockSpec(shape, lambda *_: zeros, pipeline_mode=pl.Buffered(1))


def _mod_rmsnorm(x, g, scale, shift):
    y = x * lax.rsqrt(jnp.mean(x * x, axis=-1, keepdims=True) + EPS)
    y = y * g
    return y * (1.0 + scale) + shift


def _mod_kernel(c_ref, w_ref, b_ref, o_ref):
    c = c_ref[...]
    cond = c * jax.nn.sigmoid(c)
    o_ref[0] = jnp.sum(cond * w_ref[0], axis=0, keepdims=True) + b_ref[0]


def _modulation(c, mod_w, mod_b):
    depth, d, n = mod_w.shape
    tn = 1536
    return pl.pallas_call(
        _mod_kernel,
        grid=(depth, n // tn),
        in_specs=[
            pl.BlockSpec((d, 1), lambda i, j: (0, 0)),
            pl.BlockSpec((1, d, tn), lambda i, j: (i, 0, j)),
            pl.BlockSpec((1, 1, tn), lambda i, j: (i, 0, j)),
        ],
        out_specs=pl.BlockSpec((1, 1, tn), lambda i, j: (i, 0, j)),
        out_shape=jax.ShapeDtypeStruct((depth, 1, n), F32),
        name="modulation",
    )(c.reshape(d, 1), mod_w, mod_b.reshape(depth, 1, n))


def _normproj_kernel(x_ref, g_ref, sc_ref, sh_ref, w_ref, o_ref, h_ref):
    @pl.when(pl.program_id(1) == 0)
    def _():
        h_ref[...] = _mod_rmsnorm(x_ref[...], g_ref[...], sc_ref[...], sh_ref[...]).astype(BF16)

    o_ref[...] = _dot(h_ref[...], w_ref[...]).astype(o_ref.dtype)


def _normproj(x, g, scale, shift, w, out_dtype):
    seq, d = x.shape
    n = w.shape[1]
    tm, tn = TM_PROJ, 1024
    row = pl.BlockSpec((1, d), lambda i, j: (0, 0))
    return pl.pallas_call(
        _normproj_kernel,
        grid=(seq // tm, n // tn),
        in_specs=[
            pl.BlockSpec((tm, d), lambda i, j: (i, 0)),
            row, row, row,
            pl.BlockSpec((d, tn), lambda i, j: (0, j)),
        ],
        out_specs=pl.BlockSpec((tm, tn), lambda i, j: (i, j)),
        out_shape=jax.ShapeDtypeStruct((seq, n), out_dtype),
        scratch_shapes=[pltpu.VMEM((tm, d), BF16)],
        compiler_params=pltpu.CompilerParams(dimension_semantics=("parallel", "arbitrary")),
        name="normproj",
    )(x, g.reshape(1, d), scale, shift, w)


def _diffattn_kernel(q_ref, k_ref, v_ref, bias_ref, lam_ref, g_ref, o_ref,
                     qs_ref, sa_ref, sb_ref, m_ref, l_ref, acc_ref, *, out_scale):
    blk = BLK_A
    i = pl.program_id(1)
    q = q_ref[...].astype(F32) * (DK_A ** -0.5 * LOG2E)
    lane = lax.broadcasted_iota(jnp.int32, q.shape, 1)
    qs_ref[0:blk, :] = jnp.where(lane < DK_A, q, 0.0).astype(BF16)
    qs_ref[blk:2 * blk, :] = jnp.where(lane >= DK_A, q, 0.0).astype(BF16)
    m_ref[...] = jnp.full(m_ref.shape, NEG_INF, F32)
    l_ref[...] = jnp.zeros(l_ref.shape, F32)
    acc_ref[...] = jnp.zeros(acc_ref.shape, F32)

    def scores(b, s_ref):
        k = k_ref[pl.ds(pl.multiple_of(b * blk, blk), blk), :]
        s_ref[...] = _dot_nt(qs_ref[...], k)

    def lanes_of(x):
        return jnp.concatenate([x] * (blk // LANES), axis=1)

    def softmax_pv(b, s_ref, bias):
        v = v_ref[pl.ds(pl.multiple_of(b * blk, blk), blk), :]
        s = s_ref[...]
        if bias is not None:
            s = s + jnp.concatenate([bias, bias], axis=0)
        m_prev = m_ref[...]
        m_new = jnp.maximum(m_prev, jnp.max(s, axis=-1, keepdims=True))
        alpha = jnp.exp2(m_prev - m_new)
        p = jnp.exp2(s - lanes_of(m_new))
        psum = p[:, 0:LANES]
        for c in range(1, blk // LANES):
            psum = psum + p[:, c * LANES:(c + 1) * LANES]
        l_ref[...] = alpha * l_ref[...] + psum
        acc_ref[...] = alpha * acc_ref[...] + _dot(p.astype(BF16), v)
        m_ref[...] = m_new

    nfar = jnp.maximum(i - 1, 0)
    odd = lax.rem(nfar, 2)

    @pl.when(i == 0)
    def _():
        scores(0, sb_ref)

    @pl.when(i > 0)
    def _():
        @pl.when(odd == 1)
        def _():
            scores(0, sb_ref)
            scores(1, sa_ref)
            softmax_pv(0, sb_ref, None)

        @pl.when(odd == 0)
        def _():
            scores(0, sa_ref)

        def pair_body(t, carry):
            b = odd + 2 * t
            scores(b + 1, sb_ref)
            softmax_pv(b, sa_ref, None)
            scores(b + 2, sa_ref)
            softmax_pv(b + 1, sb_ref, None)
            return carry

        lax.fori_loop(0, nfar // 2, pair_body, 0)
        scores(i, sb_ref)
        softmax_pv(i - 1, sa_ref, bias_ref[0, :, 0:blk])

    softmax_pv(i, sb_ref, bias_ref[0, :, blk:2 * blk])

    o = acc_ref[...] / jnp.sum(l_ref[...], axis=-1, keepdims=True)
    o = o[0:blk] - lam_ref[...] * o[blk:2 * blk]
    o = o * lax.rsqrt(jnp.mean(o * o, axis=-1, keepdims=True) + EPS) * g_ref[...]
    o_ref[...] = (o * out_scale).astype(o_ref.dtype)


def _toeplitz_kernel(v_ref, o_ref):
    rows, cols = o_ref.shape[1:]
    x = jnp.broadcast_to(v_ref[0], (rows, v_ref.shape[-1]))
    o_ref[0] = pltpu.roll(x, 0, 1, stride=1, stride_axis=0)[:, :cols]


def _toeplitz_tiles(fn, heads, rows, cols):
    n = 2 * cols
    assert n >= rows + cols
    r = jnp.arange(n, dtype=jnp.int32)
    rel = jnp.where(r < cols, r, r - n) - rows
    vec = fn(rel).astype(F32).reshape(heads, 1, n)
    return pl.pallas_call(
        _toeplitz_kernel,
        grid=(heads,),
        in_specs=[pl.BlockSpec((1, 1, n), lambda h: (h, 0, 0))],
        out_specs=pl.BlockSpec((1, rows, cols), lambda h: (h, 0, 0)),
        out_shape=jax.ShapeDtypeStruct((heads, rows, cols), F32),
        name="toeplitz_tiles",
    )(vec)


def _t5_bucket(rel):
    nb = NUM_BUCKETS // 2
    max_exact = nb // 2
    bucket = jnp.where(rel > 0, nb, 0)
    n = jnp.abs(rel)
    nf = jnp.maximum(n, 1).astype(F32)
    large = max_exact + (jnp.log(nf / max_exact) / math.log(MAX_DISTANCE / max_exact)
                         * (nb - max_exact)).astype(jnp.int32)
    large = jnp.minimum(large, nb - 1)
    return bucket + jnp.where(n < max_exact, n, large)


def _diff_bias_tiles(t5_table):
    blk = BLK_A
    table = t5_table.astype(F32)
    far = table[_t5_bucket(jnp.full((), -(blk + 1), jnp.int32))]
    tiles = _toeplitz_tiles(lambda rel: ((table[_t5_bucket(rel)] - far) * LOG2E).T,
                            N_HEADS_A, blk, 2 * blk)
    qpos = jnp.arange(blk, dtype=jnp.int32)[:, None]
    kpos = jnp.arange(2 * blk, dtype=jnp.int32)[None, :] - blk
    visible = jnp.floor_divide(kpos, CHUNK) <= (qpos // CHUNK)
    return jnp.where(visible[None], tiles, NEG_INF)


def _diff_attention(proj, t5_table, lam, subln_g, lam_init):
    seq = proj.shape[0]
    blk = BLK_A
    bias = _diff_bias_tiles(t5_table)
    ha = N_HEADS_A
    kern = functools.partial(_diffattn_kernel, out_scale=1.0 - lam_init)
    return pl.pallas_call(
        kern,
        grid=(ha, seq // blk),
        in_specs=[
            pl.BlockSpec((blk, DV_A), lambda h, i: (i, h)),
            pl.BlockSpec((seq, DV_A), lambda h, i: (0, ha + h)),
            pl.BlockSpec((seq, DV_A), lambda h, i: (0, 2 * ha + h)),
            pl.BlockSpec((1, blk, 2 * blk), lambda h, i: (h, 0, 0)),
            pl.BlockSpec((1, DV_A), lambda h, i: (0, 0)),
            pl.BlockSpec((1, DV_A), lambda h, i: (0, 0)),
        ],
        out_specs=pl.BlockSpec((blk, DV_A), lambda h, i: (i, h)),
        out_shape=jax.ShapeDtypeStruct((seq, ha * DV_A), BF16),
        scratch_shapes=[
            pltpu.VMEM((2 * blk, DV_A), BF16),
            pltpu.VMEM((2 * blk, blk), F32),
            pltpu.VMEM((2 * blk, blk), F32),
            pltpu.VMEM((2 * blk, LANES), F32),
            pltpu.VMEM((2 * blk, LANES), F32),
            pltpu.VMEM((2 * blk, DV_A), F32),
        ],
        compiler_params=pltpu.CompilerParams(dimension_semantics=("parallel", "arbitrary")),
        name="diff_attention",
    )(proj, proj, proj, bias, jnp.full((1, DV_A), lam, F32), subln_g.reshape(1, DV_A).astype(F32))


def _band_kernel(q_ref, kp_ref, kc_ref, vp_ref, vc_ref, bias_ref, o_ref):
    blk = BLK_B
    i = pl.program_id(1)
    q = q_ref[...] * (DH_B ** -0.5)
    lane = lax.broadcasted_iota(jnp.int32, q.shape, 1)
    kp, kc, vp, vc = kp_ref[...], kc_ref[...], vp_ref[...], vc_ref[...]
    no_prev = jnp.where(i == 0, NEG_INF, 0.0).astype(F32)
    outs = []
    for hh in range(2):
        sel = (lane < DH_B) if hh == 0 else (lane >= DH_B)
        qm = jnp.where(sel, q, 0).astype(BF16)
        sp = _dot_nt(qm, kp) + bias_ref[hh, :, 0:blk] + no_prev
        sc = _dot_nt(qm, kc) + bias_ref[hh, :, blk:2 * blk]
        m = jnp.maximum(jnp.max(sp, axis=-1, keepdims=True), jnp.max(sc, axis=-1, keepdims=True))
        pp = jnp.exp(sp - m)
        pc = jnp.exp(sc - m)
        l = jnp.sum(pp, axis=-1, keepdims=True) + jnp.sum(pc, axis=-1, keepdims=True)
        outs.append((_dot(pp.astype(BF16), vp) + _dot(pc.astype(BF16), vc)) / l)
    o_ref[...] = jnp.where(lane < DH_B, outs[0], outs[1]).astype(o_ref.dtype)


def _band_bias_tiles(rel_bias):
    blk = BLK_B
    iq = jnp.arange(blk, dtype=jnp.int32)[:, None]
    jk = jnp.arange(2 * blk, dtype=jnp.int32)[None, :] - blk
    qchunk = iq // CHUNK
    kchunk = jnp.floor_divide(jk, CHUNK)
    valid = (kchunk <= qchunk) & (kchunk >= qchunk - LEFT_CHUNKS)
    bias = _toeplitz_tiles(lambda rel: rel_bias.astype(F32)[:, jnp.clip(rel, -REL_CLIP, REL_CLIP) + REL_CLIP],
                           N_HEADS_B, blk, 2 * blk)
    return jnp.where(valid[None], bias, NEG_INF)


def _band_attention(proj, rel_bias):
    seq = proj.shape[0]
    blk = BLK_B
    bias = _band_bias_tiles(rel_bias)
    npair = N_HEADS_B // 2
    qc0 = 3 * N_HEADS_A
    prev = lambda c0: (lambda hp, i: (jnp.maximum(i - 1, 0), c0 + hp))
    cur = lambda c0: (lambda hp, i: (i, c0 + hp))
    return pl.pallas_call(
        _band_kernel,
        grid=(npair, seq // blk),
        in_specs=[
            pl.BlockSpec((blk, LANES), cur(qc0)),
            pl.BlockSpec((blk, LANES), prev(qc0 + npair)),
            pl.BlockSpec((blk, LANES), cur(qc0 + npair)),
            pl.BlockSpec((blk, LANES), prev(qc0 + 2 * npair)),
            pl.BlockSpec((blk, LANES), cur(qc0 + 2 * npair)),
            pl.BlockSpec((2, blk, 2 * blk), lambda hp, i: (hp, 0, 0)),
        ],
        out_specs=pl.BlockSpec((blk, LANES), lambda hp, i: (i, hp)),
        out_shape=jax.ShapeDtypeStruct((seq, N_HEADS_B * DH_B), BF16),
        compiler_params=pltpu.CompilerParams(dimension_semantics=("parallel", "arbitrary")),
        name="band_attention",
    )(proj, proj, proj, proj, proj, bias)


def _retention_kernel(qk_ref, v_ref, gate_ref, cos_ref, sin_ref, qdec_ref, kdec_ref, dmat_ref,
                      sdec_ref, o_ref, state_ref):
    @pl.when(pl.program_id(0) == 0)
    def _():
        state_ref[...] = jnp.zeros(state_ref.shape, F32)

    cos = cos_ref[...]
    sin = sin_ref[...]
    lane = lax.broadcasted_iota(jnp.int32, cos.shape, 1)
    first_half = (lane % DQK_C) < (DQK_C // 2)
    qk = qk_ref[...]
    parts = []
    for j in range(qk.shape[1] // LANES):
        t = qk[:, j * LANES:(j + 1) * LANES]
        partner = jnp.where(first_half, pltpu.roll(t, LANES - DQK_C // 2, 1), pltpu.roll(t, DQK_C // 2, 1))
        parts.append(t * cos + partner * sin)
    wq = N_HEADS_C * DQK_C
    q = jnp.concatenate(parts[:wq // LANES], axis=1)
    k = jnp.concatenate(parts[wq // LANES:], axis=1) * (DQK_C ** -0.5)
    qd = (q * qdec_ref[...]).astype(BF16)
    kd = (k * kdec_ref[...]).astype(BF16)
    qb = q.astype(BF16)
    kb = k.astype(BF16)
    vb = v_ref[...].astype(BF16)
    gate = gate_ref[...]
    outs = []
    for h in range(N_HEADS_C):
        qs = slice(h * DQK_C, (h + 1) * DQK_C)
        vs = slice(h * DV_C, (h + 1) * DV_C)
        scores = _dot_nt(qb[:, qs], kb[:, qs]) * dmat_ref[h]
        state = state_ref[h]
        r = _dot(scores.astype(BF16), vb[:, vs]) + _dot(qd[:, qs], state.astype(BF16))
        state_ref[h] = state * sdec_ref[h] + _dot_tn(kd[:, qs], vb[:, vs])
        r = r * lax.rsqrt(jnp.mean(r * r, axis=-1, keepdims=True) + EPS)
        g = gate[:, vs]
        outs.append(r * (g * jax.nn.sigmoid(g)))
    o_ref[...] = jnp.concatenate(outs, axis=1).astype(o_ref.dtype)


def _retention_tables(seq):
    t = BLK_C
    half = DQK_C // 2
    inv_freq = 1.0 / (ROPE_BASE ** (jnp.arange(0, DQK_C, 2, dtype=F32) / DQK_C))
    ang = jnp.arange(seq, dtype=F32)[:, None] * inv_freq[None, :]
    reps = LANES // half
    cos = jnp.tile(jnp.cos(ang), (1, reps))
    sign = jnp.where((jnp.arange(LANES) % DQK_C) < half, -1.0, 1.0).astype(F32)
    sin = jnp.tile(jnp.sin(ang), (1, reps)) * sign[None, :]
    log_g = jnp.log(1.0 - jnp.power(2.0, -5.0 - jnp.arange(N_HEADS_C, dtype=F32)))
    pos = jnp.arange(t, dtype=F32)
    diff = pos[:, None] - pos[None, :]
    same_or_past = (jnp.arange(t)[None, :] // CHUNK) <= (jnp.arange(t)[:, None] // CHUNK)
    dmat = jnp.where(same_or_past[None], jnp.exp(log_g[:, None, None] * jnp.abs(diff)[None]), 0.0)
    qdec = jnp.repeat(jnp.exp(log_g[None, :] * (pos[:, None] + 1.0)), DQK_C, axis=1)
    kdec = jnp.repeat(jnp.exp(log_g[None, :] * (t - 1.0 - pos[:, None])), DQK_C, axis=1)
    sdec = jnp.broadcast_to(jnp.exp(log_g * t)[:, None, None], (N_HEADS_C, 1, DV_C))
    return cos, sin, qdec, kdec, dmat, sdec


def _retention(proj):
    seq = proj.shape[0]
    t = BLK_C
    cos, sin, qdec, kdec, dmat, sdec = _retention_tables(seq)
    wv = N_HEADS_C * DV_C
    return pl.pallas_call(
        _retention_kernel,
        grid=(seq // t,),
        in_specs=[
            pl.BlockSpec((t, wv), lambda i: (i, 0)),
            pl.BlockSpec((t, wv), lambda i: (i, 1)),
            pl.BlockSpec((t, wv), lambda i: (i, 2)),
            pl.BlockSpec((t, LANES), lambda i: (i, 0)),
            pl.BlockSpec((t, LANES), lambda i: (i, 0)),
            pl.BlockSpec((t, N_HEADS_C * DQK_C), lambda i: (0, 0)),
            pl.BlockSpec((t, N_HEADS_C * DQK_C), lambda i: (0, 0)),
            pl.BlockSpec((N_HEADS_C, t, t), lambda i: (0, 0, 0)),
            pl.BlockSpec((N_HEADS_C, 1, DV_C), lambda i: (0, 0, 0)),
        ],
        out_specs=pl.BlockSpec((t, wv), lambda i: (i, 0)),
        out_shape=jax.ShapeDtypeStruct((seq, wv), BF16),
        scratch_shapes=[pltpu.VMEM((N_HEADS_C, DQK_C, DV_C), F32)],
        compiler_params=pltpu.CompilerParams(dimension_semantics=("arbitrary",)),
        name="retention",
    )(proj, proj, proj, cos, sin, qdec, kdec, dmat, sdec)


def _s5_kernel(*refs):
    u_refs = refs[:S5_T]
    mt_ref, bt_ref, ct_ref, are_ref, aim_ref, y2_ref, ut_ref, yt_ref, vr_ref, vi_ref, spr_ref, spi_ref, carry_ref = refs[S5_T:]
    tc = S5_TC
    gp = S5_GROUP
    n = S5_STATE
    half = S5_CH // 2

    @pl.when(pl.program_id(0) == 0)
    def _():
        carry_ref[...] = jnp.zeros(carry_ref.shape, F32)

    for s in range(S5_T):
        for hh in range(2):
            ut_ref[s, hh * half:(hh + 1) * half, :] = u_refs[s][:, hh * half:(hh + 1) * half].T

    def intra(g, carry):
        r0 = pl.multiple_of(g * gp, gp)
        ug = ut_ref[:, pl.ds(r0, gp), :].reshape(S5_T * gp, tc).astype(BF16)
        yt_ref[:, pl.ds(r0, gp), :] = _dot(mt_ref[g], ug).reshape(S5_T, gp, tc)
        vt = _dot(bt_ref[g], ug)
        n0 = pl.multiple_of(g * n, n)
        vr_ref[pl.ds(n0, n), :] = vt[0:n]
        vi_ref[pl.ds(n0, n), :] = vt[n:2 * n]
        return carry

    lax.fori_loop(0, S5_GROUPS, intra, 0)

    ar, ai = are_ref[...], aim_ref[...]
    cr, ci = carry_ref[0], carry_ref[1]
    xr, xi = vr_ref[...], vi_ref[...]
    lane = lax.broadcasted_iota(jnp.int32, xr.shape, 1)
    first = lane == 0
    xr = xr + jnp.where(first, ar * cr - ai * ci, 0.0)
    xi = xi + jnp.where(first, ar * ci + ai * cr, 0.0)
    pr, pi = ar, ai
    d = 1
    while d < tc:
        sr = jnp.where(lane >= d, pltpu.roll(xr, d, 1), 0.0)
        si = jnp.where(lane >= d, pltpu.roll(xi, d, 1), 0.0)
        xr, xi = xr + (pr * sr - pi * si), xi + (pr * si + pi * sr)
        pr, pi = pr * pr - pi * pi, 2.0 * pr * pi
        d *= 2
    spr_ref[...] = jnp.where(first, cr, pltpu.roll(xr, 1, 1))
    spi_ref[...] = jnp.where(first, ci, pltpu.roll(xi, 1, 1))
    carry_ref[0] = jnp.broadcast_to(xr[:, tc - 1:tc], xr.shape)
    carry_ref[1] = jnp.broadcast_to(xi[:, tc - 1:tc], xi.shape)

    def cross(g, carry):
        r0 = pl.multiple_of(g * gp, gp)
        n0 = pl.multiple_of(g * n, n)
        sp = jnp.concatenate([spr_ref[pl.ds(n0, n), :], spi_ref[pl.ds(n0, n), :]], axis=0).astype(BF16)
        yt_ref[:, pl.ds(r0, gp), :] += _dot(ct_ref[g], sp).reshape(S5_T, gp, tc)
        return carry

    lax.fori_loop(0, S5_GROUPS, cross, 0)

    for s in range(S5_T):
        for hh in range(2):
            c0 = s * S5_CH + hh * half
            y2_ref[:, c0:c0 + half] = yt_ref[s, hh * half:(hh + 1) * half, :].T


def _s5_matrices(lam_re, lam_im, log_step, b_re, b_im, c_re, c_im, d_skip):
    hi = lax.Precision.HIGHEST
    t, gp, n, ng = S5_T, S5_GROUP, S5_STATE, S5_GROUPS
    lam = lax.complex(lam_re.astype(F32), lam_im.astype(F32))
    step = jnp.exp(log_step.astype(F32))[:, None]
    ls = lam * step
    a_bar = jnp.exp(ls)
    b_bar = ((a_bar - 1.0) / lam)[..., None] * lax.complex(b_re.astype(F32), b_im.astype(F32))
    cm = lax.complex(c_re.astype(F32), c_im.astype(F32))
    k = jnp.arange(t + 1, dtype=F32)
    apow = jnp.exp(ls[:, None, :] * k[None, :, None].astype(jnp.complex64))
    kmat = jnp.einsum('gpn,gln,gnq->glpq', cm, apow[:, :t], b_bar, precision=hi).real
    lag = jnp.arange(t)[:, None] - jnp.arange(t)[None, :]
    toep = jnp.where((lag >= 0)[None, :, :, None, None], kmat[:, jnp.clip(lag, 0, t - 1)], 0.0)
    mt = jnp.transpose(toep, (0, 1, 3, 2, 4)).reshape(ng, t * gp, t * gp)
    dvec = jnp.tile(d_skip.astype(F32).reshape(ng, 1, gp), (1, t, 1)).reshape(ng, t * gp)
    mt = mt + jnp.eye(t * gp, dtype=F32)[None] * dvec[:, :, None]
    z = apow[:, t - 1 - jnp.arange(t)][:, :, :, None] * b_bar[:, None, :, :]
    z = jnp.transpose(z, (0, 2, 1, 3)).reshape(ng, n, t * gp)
    bt = jnp.concatenate([z.real, z.imag], axis=1)
    w = cm[:, None, :, :] * apow[:, 1:t + 1][:, :, None, :]
    w = w.reshape(ng, t * gp, n)
    ct = jnp.concatenate([w.real, -w.imag], axis=2)
    a_chunk = apow[:, t].reshape(ng * n, 1)
    are = jnp.broadcast_to(a_chunk.real, (ng * n, LANES))
    aim = jnp.broadcast_to(a_chunk.imag, (ng * n, LANES))
    return mt.astype(BF16), bt.astype(BF16), ct.astype(BF16), are, aim


def _s5(proj, mats):
    seq, width = proj.shape
    t, tc, gp, n, ng = S5_T, S5_TC, S5_GROUP, S5_STATE, S5_GROUPS
    nchunks = seq // t
    mt, bt, ct, are, aim = mats
    pview = proj.reshape(nchunks, t * width)
    cpb = width // S5_CH
    u_specs = [pl.BlockSpec((tc, S5_CH), (lambda i, s=s: (i, s * cpb + cpb - 1))) for s in range(t)]
    y2 = pl.pallas_call(
        _s5_kernel,
        grid=(nchunks // tc,),
        in_specs=u_specs + [
            _const_spec(mt.shape), _const_spec(bt.shape), _const_spec(ct.shape),
            _const_spec(are.shape), _const_spec(aim.shape),
        ],
        out_specs=pl.BlockSpec((tc, t * S5_CH), lambda i: (i, 0)),
        out_shape=jax.ShapeDtypeStruct((nchunks, t * S5_CH), F32),
        scratch_shapes=[
            pltpu.VMEM((t, S5_CH, tc), F32),
            pltpu.VMEM((t, S5_CH, tc), F32),
            pltpu.VMEM((ng * n, tc), F32),
            pltpu.VMEM((ng * n, tc), F32),
            pltpu.VMEM((ng * n, tc), F32),
            pltpu.VMEM((ng * n, tc), F32),
            pltpu.VMEM((2, ng * n, tc), F32),
        ],
        compiler_params=pltpu.CompilerParams(dimension_semantics=("arbitrary",)),
        name="s5_scan",
    )(*([pview] * t), mt, bt, ct, are, aim)
    return y2.reshape(seq, S5_CH)


def _outproj_kernel(*refs, glu):
    if glu:
        x_ref, a_ref, b_ref, wo_ref, g_ref, gw_ref, o_ref = refs
        y = jax.nn.gelu(b_ref[...]).astype(BF16)
        gg = _dot(y, gw_ref[...])
        half = gg.shape[1] // 2
        b = (gg[:, :half] * jax.nn.sigmoid(gg[:, half:])).astype(BF16)
    else:
        x_ref, a_ref, b_ref, wo_ref, g_ref, o_ref = refs
        b = b_ref[...]
    cat = jnp.concatenate([a_ref[...], b], axis=1)
    o_ref[...] = x_ref[...] + g_ref[...] * _dot(cat, wo_ref[...])


def _outproj(x, a, b, wo, gate, glu_w=None):
    seq, d = x.shape
    tm = TM_PROJ
    wa, wb = a.shape[1], b.shape[1]
    in_specs = [
        pl.BlockSpec((tm, d), lambda i: (i, 0)),
        pl.BlockSpec((tm, wa), lambda i: (i, 0)),
        pl.BlockSpec((tm, wb), lambda i: (i, 0)),
        _const_spec(wo.shape),
        pl.BlockSpec((1, d), lambda i: (0, 0)),
    ]
    args = [x, a, b, wo, gate]
    if glu_w is not None:
        in_specs.append(_const_spec(glu_w.shape))
        args.append(glu_w)
    return pl.pallas_call(
        functools.partial(_outproj_kernel, glu=glu_w is not None),
        grid=(seq // tm,),
        in_specs=in_specs,
        out_specs=pl.BlockSpec((tm, d), lambda i: (i, 0)),
        out_shape=jax.ShapeDtypeStruct((seq, d), F32),
        compiler_params=pltpu.CompilerParams(dimension_semantics=("parallel",)),
        name="outproj",
    )(*args)


def _ffn_kernel(x_ref, g_ref, sc_ref, sh_ref, gate_ref, win_ref, cw_ref, cb_ref, wout_ref, fg_ref,
                o_ref, h_ref, act_ref, gbuf_ref, carry_ref, *, final):
    tm = x_ref.shape[0]
    halo = gbuf_ref.shape[0] - tm

    @pl.when(pl.program_id(0) == 0)
    def _():
        carry_ref[...] = jnp.zeros(carry_ref.shape, F32)

    x = x_ref[...]
    h_ref[...] = _mod_rmsnorm(x, g_ref[...], sc_ref[...], sh_ref[...]).astype(BF16)
    for f in range(D_FF // TF_FFN):
        cs = slice(f * TF_FFN, (f + 1) * TF_FFN)
        gs = slice(D_FF + f * TF_FFN, D_FF + (f + 1) * TF_FFN)
        h = h_ref[...]
        val = _dot(h, win_ref[:, cs])
        gate = _dot(h, win_ref[:, gs])
        gbuf_ref[0:halo, :] = carry_ref[:, cs]
        gbuf_ref[halo:halo + tm, :] = gate
        carry_ref[:, cs] = gate[tm - halo:tm, :]
        conv = (gate * cw_ref[2:3, cs] + gbuf_ref[halo - 1:halo - 1 + tm, :] * cw_ref[1:2, cs]
                + gbuf_ref[halo - 2:halo - 2 + tm, :] * cw_ref[0:1, cs] + cb_ref[:, cs])
        act_ref[:, cs] = (jax.nn.gelu(conv) * val).astype(BF16)
    xn = x + gate_ref[...] * _dot(act_ref[...], wout_ref[...])
    if final:
        xn = xn * lax.rsqrt(jnp.mean(xn * xn, axis=-1, keepdims=True) + EPS) * fg_ref[...]
    o_ref[...] = xn


def _ffn(x, g, scale, shift, gate, w_in, conv_w, conv_b, w_out, final_g, final):
    seq, d = x.shape
    tm = TM_FFN
    halo = 8
    row = pl.BlockSpec((1, d), lambda i: (0, 0))
    return pl.pallas_call(
        functools.partial(_ffn_kernel, final=final),
        grid=(seq // tm,),
        in_specs=[
            pl.BlockSpec((tm, d), lambda i: (i, 0)),
            row, row, row, row,
            _const_spec(w_in.shape),
            _const_spec(conv_w.shape),
            _const_spec((1, D_FF)),
            _const_spec(w_out.shape),
            row,
        ],
        out_specs=pl.BlockSpec((tm, d), lambda i: (i, 0)),
        out_shape=jax.ShapeDtypeStruct((seq, d), F32),
        scratch_shapes=[
            pltpu.VMEM((tm, d), BF16),
            pltpu.VMEM((tm, D_FF), BF16),
            pltpu.VMEM((tm + halo, TF_FFN), F32),
            pltpu.VMEM((halo, D_FF), F32),
        ],
        compiler_params=pltpu.CompilerParams(dimension_semantics=("arbitrary",)),
        name="conv_ffn",
    )(x, g.reshape(1, d), scale, shift, gate, w_in, conv_w, conv_b.reshape(1, D_FF), w_out,
      final_g.reshape(1, d))


def kernel(x, c, t5_table, mod_w, mod_b, norm1_g, norm2_g, ffn_w_in, ffn_conv_w, ffn_conv_b, ffn_w_out,
           ev_w_in, ev_w_out, diff_lambda, diff_subln_g, band_rel_bias,
           od_w_in, od_w_out, s5_lam_re, s5_lam_im, s5_log_step, s5_b_re, s5_b_im, s5_c_re, s5_c_im,
           s5_d, s5_glu_w, final_g):
    assert x.shape[0] == 1 and x.shape[2] == D_MODEL
    seq = x.shape[1]
    assert seq % TM_PROJ == 0 and seq % (S5_T * S5_TC) == 0
    d = D_MODEL
    xs = x[0]
    mod = _modulation(c, mod_w, mod_b)
    for i in range(DEPTH):
        sh1, sc1, g1, sh2, sc2, g2 = [mod[i, :, k * d:(k + 1) * d] for k in range(6)]
        if i % 2 == 0:
            e = i // 2
            lam_init = 0.8 - 0.6 * math.exp(-0.3 * i)
            lp = diff_lambda[e].astype(F32)
            lam = jnp.exp(jnp.sum(lp[0] * lp[1])) - jnp.exp(jnp.sum(lp[2] * lp[3])) + lam_init
            proj = _normproj(xs, norm1_g[i], sc1, sh1, ev_w_in[e].astype(BF16), BF16)
            out_a = _diff_attention(proj, t5_table, lam, diff_subln_g[e], lam_init)
            out_b = _band_attention(proj, band_rel_bias[e])
            xs = _outproj(xs, out_a, out_b, ev_w_out[e].astype(BF16), g1)
        else:
            o = i // 2
            proj = _normproj(xs, norm1_g[i], sc1, sh1, od_w_in[o].astype(BF16), F32)
            r = _retention(proj)
            mats = _s5_matrices(s5_lam_re[o], s5_lam_im[o], s5_log_step[o], s5_b_re[o], s5_b_im[o],
                                s5_c_re[o], s5_c_im[o], s5_d[o])
            y = _s5(proj, mats)
            xs = _outproj(xs, r, y, od_w_out[o].astype(BF16), g1, glu_w=s5_glu_w[o].astype(BF16))
        xs = _ffn(xs, norm2_g[i], sc2, sh2, g2, ffn_w_in[i].astype(BF16), ffn_conv_w[i], ffn_conv_b[i],
                  ffn_w_out[i].astype(BF16), final_g, final=(i == DEPTH - 1))
    return xs[None]
```

```python
import functools
import math

import jax
import jax.numpy as jnp
from jax import lax
from jax.experimental import pallas as pl
from jax.experimental.pallas import tpu as pltpu

F32 = jnp.float32
BF16 = jnp.bfloat16

D_MODEL = 1024
DEPTH = 2
CHUNK = 64
GROUP_WIDTH = D_MODEL // 2
DK_A = 64
DV_A = 2 * DK_A
N_HEADS_A = GROUP_WIDTH // DV_A
DH_B = 64
N_HEADS_B = GROUP_WIDTH // DH_B
LEFT_CHUNKS = 8
REL_CLIP = 2 * CHUNK
NUM_BUCKETS = 32
MAX_DISTANCE = 128
DV_C = 128
DQK_C = DV_C // 2
N_HEADS_C = GROUP_WIDTH // DV_C
ROPE_BASE = 10000.0
S5_CH = GROUP_WIDTH
S5_GROUP = 16
S5_GROUPS = S5_CH // S5_GROUP
S5_STATE = 64
D_FF = ((8 * D_MODEL // 3 + 255) // 256) * 256
CONV_W = 3
EVEN_IN = 3 * N_HEADS_A * DV_A + 3 * N_HEADS_B * DH_B
ODD_IN = 2 * N_HEADS_C * DQK_C + 2 * N_HEADS_C * DV_C + S5_CH
EPS = 1e-6
NEG_INF = -1e30
LOG2E = math.log2(math.e)

LANES = 128
MXU_DIM = 256

TM_PROJ = 1024
TM_FFN = 512
TF_FFN = MXU_DIM
BLK_A = 512
BLK_B = 512
BLK_C = 256
S5_T = 16
S5_TC = LANES

assert BLK_B == LEFT_CHUNKS * CHUNK, "band window must be exactly one previous block"
assert BLK_A >= MAX_DISTANCE, "far key blocks must sit in the saturated T5 bucket"
assert DV_A == LANES, "diff-attention statistics are kept lane-replicated beside the accumulator"


def _dot(a, b):
    return jnp.dot(a, b, preferred_element_type=F32)


def _dot_nt(a, b):
    return lax.dot_general(a, b, (((1,), (1,)), ((), ())), preferred_element_type=F32)


def _dot_tn(a, b):
    return lax.dot_general(a, b, (((0,), (0,)), ((), ())), preferred_element_type=F32)


def _const_spec(shape):
    zeros = (0,) * len(shape)
    return pl.BlockSpec(shape, lambda *_: zeros, pipeline_mode=pl.Buffered(1))


def _mod_rmsnorm(x, g, scale, shift):
    y = x * lax.rsqrt(jnp.mean(x * x, axis=-1, keepdims=True) + EPS)
    y = y * g
    return y * (1.0 + scale) + shift


def _mod_kernel(c_ref, w_ref, b_ref, o_ref):
    c = c_ref[...]
    cond = c * jax.nn.sigmoid(c)
    o_ref[0] = jnp.sum(cond * w_ref[0], axis=0, keepdims=True) + b_ref[0]


def _modulation(c, mod_w, mod_b):
    depth, d, n = mod_w.shape
    tn = 1536
    return pl.pallas_call(
        _mod_kernel,
        grid=(depth, n // tn),
        in_specs=[
            pl.BlockSpec((d, 1), lambda i, j: (0, 0)),
            pl.BlockSpec((1, d, tn), lambda i, j: (i, 0, j)),
            pl.BlockSpec((1, 1, tn), lambda i, j: (i, 0, j)),
        ],
        out_specs=pl.BlockSpec((1, 1, tn), lambda i, j: (i, 0, j)),
        out_shape=jax.ShapeDtypeStruct((depth, 1, n), F32),
        name="modulation",
    )(c.reshape(d, 1), mod_w, mod_b.reshape(depth, 1, n))


def _normproj_kernel(x_ref, g_ref, sc_ref, sh_ref, w_ref, o_ref, h_ref):
    @pl.when(pl.program_id(1) == 0)
    def _():
        h_ref[...] = _mod_rmsnorm(x_ref[...], g_ref[...], sc_ref[...], sh_ref[...]).astype(BF16)

    o_ref[...] = _dot(h_ref[...], w_ref[...]).astype(o_ref.dtype)


def _normproj(x, g, scale, shift, w, out_dtype):
    seq, d = x.shape
    n = w.shape[1]
    tm, tn = TM_PROJ, 1024
    row = pl.BlockSpec((1, d), lambda i, j: (0, 0))
    return pl.pallas_call(
        _normproj_kernel,
        grid=(seq // tm, n // tn),
        in_specs=[
            pl.BlockSpec((tm, d), lambda i, j: (i, 0)),
            row, row, row,
            pl.BlockSpec((d, tn), lambda i, j: (0, j)),
        ],
        out_specs=pl.BlockSpec((tm, tn), lambda i, j: (i, j)),
        out_shape=jax.ShapeDtypeStruct((seq, n), out_dtype),
        scratch_shapes=[pltpu.VMEM((tm, d), BF16)],
        compiler_params=pltpu.CompilerParams(dimension_semantics=("parallel", "arbitrary")),
        name="normproj",
    )(x, g.reshape(1, d), scale, shift, w)


def _diffattn_kernel(q_ref, k_ref, v_ref, bias_ref, lam_ref, g_ref, o_ref,
                     qs_ref, vt_ref, sa_ref, sb_ref, m_ref, l_ref, acc_ref, *, out_scale):
    blk = BLK_A
    nq = 2 * blk
    sub = 8
    i = pl.program_id(1)

    @pl.when(i == 0)
    def _():
        def tr(b, carry):
            r0 = pl.multiple_of(b * blk, blk)
            vt_ref[:, pl.ds(r0, blk)] = v_ref[pl.ds(r0, blk), :].astype(F32).T.astype(BF16)
            return carry
        lax.fori_loop(0, v_ref.shape[0] // blk, tr, 0)

    q = q_ref[...].astype(F32) * (DK_A ** -0.5 * LOG2E)
    lane = lax.broadcasted_iota(jnp.int32, q.shape, 1)
    qs_ref[0:blk, :] = jnp.where(lane < DK_A, q, 0.0).astype(BF16)
    qs_ref[blk:nq, :] = jnp.where(lane >= DK_A, q, 0.0).astype(BF16)
    m_ref[...] = jnp.full(m_ref.shape, NEG_INF, F32)
    l_ref[...] = jnp.zeros(l_ref.shape, F32)
    acc_ref[...] = jnp.zeros(acc_ref.shape, F32)

    def scores(b, s_ref):
        k = k_ref[pl.ds(pl.multiple_of(b * blk, blk), blk), :]
        s_ref[...] = _dot_nt(k, qs_ref[...])

    def softmax_pv(b, s_ref, bias):
        vt = vt_ref[:, pl.ds(pl.multiple_of(b * blk, blk), blk)]
        s = s_ref[...]
        if bias is not None:
            s = s + jnp.concatenate([bias, bias], axis=1)
        s = s.reshape(blk // sub, sub, nq)
        m_prev = m_ref[...]
        m_cur = jnp.max(jnp.max(s, axis=0), axis=0, keepdims=True)
        m_new = jnp.maximum(m_prev, m_cur)
        alpha = jnp.exp2(m_prev - m_new)
        p = jnp.exp2(s - m_new[None])
        l_ref[...] = alpha * l_ref[...] + jnp.sum(p, axis=0)
        pv = _dot(vt, p.reshape(blk, nq).astype(BF16))
        acc_ref[...] = acc_ref[...] * alpha[0:1] + pv
        m_ref[...] = m_new

    nfar = jnp.maximum(i - 1, 0)
    odd = lax.rem(nfar, 2)

    @pl.when(i == 0)
    def _():
        scores(0, sb_ref)

    @pl.when(i > 0)
    def _():
        @pl.when(odd == 1)
        def _():
            scores(0, sb_ref)
            scores(1, sa_ref)
            softmax_pv(0, sb_ref, None)

        @pl.when(odd == 0)
        def _():
            scores(0, sa_ref)

        def pair_body(t, carry):
            b = odd + 2 * t
            scores(b + 1, sb_ref)
            softmax_pv(b, sa_ref, None)
            scores(b + 2, sa_ref)
            softmax_pv(b + 1, sb_ref, None)
            return carry

        lax.fori_loop(0, nfar // 2, pair_body, 0)
        scores(i, sb_ref)
        softmax_pv(i - 1, sa_ref, bias_ref[0, 0])

    softmax_pv(i, sb_ref, bias_ref[0, 1])

    ot = acc_ref[...] / jnp.sum(l_ref[...], axis=0, keepdims=True)
    o = ot[:, 0:blk].T - lam_ref[...] * ot[:, blk:nq].T
    o = o * lax.rsqrt(jnp.mean(o * o, axis=-1, keepdims=True) + EPS) * g_ref[...]
    o_ref[...] = (o * out_scale).astype(o_ref.dtype)


_TOEPLITZ_ROWS = 512
_TOEPLITZ_N = 2048


def _toeplitz_kernel(v_ref, o_ref):
    rows, cols = o_ref.shape[1:]
    x = jnp.broadcast_to(v_ref[0, 0], (rows, v_ref.shape[-1]))
    o_ref[0] = pltpu.roll(x, 0, 1, stride=1, stride_axis=0)[:, :cols]


def _toeplitz_tiles(fn, heads, rows, cols):
    n, rb = _TOEPLITZ_N, _TOEPLITZ_ROWS
    assert rows % rb == 0 and rows <= n // 2 and cols <= n // 2
    idx = jnp.arange(n, dtype=jnp.int32)
    vec = fn(jnp.where(idx < n // 2, idx, idx - n)).astype(F32)
    vecs = jnp.stack([jnp.roll(vec, k * rb, axis=1) for k in range(rows // rb)], axis=1)
    return pl.pallas_call(
        _toeplitz_kernel,
        grid=(heads, rows // rb),
        in_specs=[pl.BlockSpec((1, 1, 1, n), lambda h, k: (h, k, 0, 0))],
        out_specs=pl.BlockSpec((1, rb, cols), lambda h, k: (h, k, 0)),
        out_shape=jax.ShapeDtypeStruct((heads, rows, cols), F32),
        name="toeplitz_tiles",
    )(vecs.reshape(heads, rows // rb, 1, n))


def _t5_bucket(rel):
    nb = NUM_BUCKETS // 2
    max_exact = nb // 2
    bucket = jnp.where(rel > 0, nb, 0)
    n = jnp.abs(rel)
    nf = jnp.maximum(n, 1).astype(F32)
    large = max_exact + (jnp.log(nf / max_exact) / math.log(MAX_DISTANCE / max_exact)
                         * (nb - max_exact)).astype(jnp.int32)
    large = jnp.minimum(large, nb - 1)
    return bucket + jnp.where(n < max_exact, n, large)


def _diff_bias_tiles(t5_table):
    blk = BLK_A
    table = t5_table.astype(F32)
    far = table[_t5_bucket(jnp.full((), -(blk + 1), jnp.int32))]
    tiles = _toeplitz_tiles(lambda x: ((table[_t5_bucket(-x - blk)] - far) * LOG2E).T,
                            N_HEADS_A, 2 * blk, blk)
    kpos = jnp.arange(2 * blk, dtype=jnp.int32)[:, None] - blk
    qpos = jnp.arange(blk, dtype=jnp.int32)[None, :]
    visible = jnp.floor_divide(kpos, CHUNK) <= (qpos // CHUNK)
    return jnp.where(visible[None], tiles, NEG_INF).reshape(N_HEADS_A, 2, blk, blk)


def _diff_attention(proj, t5_table, lam, subln_g, lam_init):
    seq = proj.shape[0]
    blk = BLK_A
    bias = _diff_bias_tiles(t5_table)
    ha = N_HEADS_A
    kern = functools.partial(_diffattn_kernel, out_scale=1.0 - lam_init)
    return pl.pallas_call(
        kern,
        grid=(ha, seq // blk),
        in_specs=[
            pl.BlockSpec((blk, DV_A), lambda h, i: (i, h)),
            pl.BlockSpec((seq, DV_A), lambda h, i: (0, ha + h)),
            pl.BlockSpec((seq, DV_A), lambda h, i: (0, 2 * ha + h)),
            pl.BlockSpec((1, 2, blk, blk), lambda h, i: (h, 0, 0, 0)),
            pl.BlockSpec((1, DV_A), lambda h, i: (0, 0)),
            pl.BlockSpec((1, DV_A), lambda h, i: (0, 0)),
        ],
        out_specs=pl.BlockSpec((blk, DV_A), lambda h, i: (i, h)),
        out_shape=jax.ShapeDtypeStruct((seq, ha * DV_A), BF16),
        scratch_shapes=[
            pltpu.VMEM((2 * blk, DV_A), BF16),
            pltpu.VMEM((DV_A, seq), BF16),
            pltpu.VMEM((blk, 2 * blk), F32),
            pltpu.VMEM((blk, 2 * blk), F32),
            pltpu.VMEM((8, 2 * blk), F32),
            pltpu.VMEM((8, 2 * blk), F32),
            pltpu.VMEM((DV_A, 2 * blk), F32),
        ],
        compiler_params=pltpu.CompilerParams(dimension_semantics=("parallel", "arbitrary")),
        name="diff_attention",
    )(proj, proj, proj, bias, jnp.full((1, DV_A), lam, F32), subln_g.reshape(1, DV_A).astype(F32))


def _band_kernel(q_ref, kp_ref, kc_ref, vp_ref, vc_ref, bias_ref, o_ref):
    blk = BLK_B
    i = pl.program_id(1)
    q = q_ref[...] * (DH_B ** -0.5)
    lane = lax.broadcasted_iota(jnp.int32, q.shape, 1)
    kp, kc, vp, vc = kp_ref[...], kc_ref[...], vp_ref[...], vc_ref[...]
    no_prev = jnp.where(i == 0, NEG_INF, 0.0).astype(F32)
    outs = []
    for hh in range(2):
        sel = (lane < DH_B) if hh == 0 else (lane >= DH_B)
        qm = jnp.where(sel, q, 0).astype(BF16)
        sp = _dot_nt(qm, kp) + bias_ref[hh, :, 0:blk] + no_prev
        sc = _dot_nt(qm, kc) + bias_ref[hh, :, blk:2 * blk]
        m = jnp.maximum(jnp.max(sp, axis=-1, keepdims=True), jnp.max(sc, axis=-1, keepdims=True))
        pp = jnp.exp(sp - m)
        pc = jnp.exp(sc - m)
        l = jnp.sum(pp, axis=-1, keepdims=True) + jnp.sum(pc, axis=-1, keepdims=True)
        outs.append((_dot(pp.astype(BF16), vp) + _dot(pc.astype(BF16), vc)) / l)
    o_ref[...] = jnp.where(lane < DH_B, outs[0], outs[1]).astype(o_ref.dtype)


def _band_bias_tiles(rel_bias):
    blk = BLK_B
    iq = jnp.arange(blk, dtype=jnp.int32)[:, None]
    jk = jnp.arange(2 * blk, dtype=jnp.int32)[None, :] - blk
    qchunk = iq // CHUNK
    kchunk = jnp.floor_divide(jk, CHUNK)
    valid = (kchunk <= qchunk) & (kchunk >= qchunk - LEFT_CHUNKS)
    bias = _toeplitz_tiles(
        lambda x: rel_bias.astype(F32)[:, jnp.clip(x - blk, -REL_CLIP, REL_CLIP) + REL_CLIP],
        N_HEADS_B, blk, 2 * blk)
    return jnp.where(valid[None], bias, NEG_INF)


def _band_attention(proj, rel_bias):
    seq = proj.shape[0]
    blk = BLK_B
    bias = _band_bias_tiles(rel_bias)
    npair = N_HEADS_B // 2
    qc0 = 3 * N_HEADS_A
    prev = lambda c0: (lambda hp, i: (jnp.maximum(i - 1, 0), c0 + hp))
    cur = lambda c0: (lambda hp, i: (i, c0 + hp))
    return pl.pallas_call(
        _band_kernel,
        grid=(npair, seq // blk),
        in_specs=[
            pl.BlockSpec((blk, LANES), cur(qc0)),
            pl.BlockSpec((blk, LANES), prev(qc0 + npair)),
            pl.BlockSpec((blk, LANES), cur(qc0 + npair)),
            pl.BlockSpec((blk, LANES), prev(qc0 + 2 * npair)),
            pl.BlockSpec((blk, LANES), cur(qc0 + 2 * npair)),
            pl.BlockSpec((2, blk, 2 * blk), lambda hp, i: (hp, 0, 0)),
        ],
        out_specs=pl.BlockSpec((blk, LANES), lambda hp, i: (i, hp)),
        out_shape=jax.ShapeDtypeStruct((seq, N_HEADS_B * DH_B), BF16),
        compiler_params=pltpu.CompilerParams(dimension_semantics=("parallel", "arbitrary")),
        name="band_attention",
    )(proj, proj, proj, proj, proj, bias)


def _retention_kernel(qk_ref, v_ref, gate_ref, cos_ref, sin_ref, qdec_ref, kdec_ref, dmat_ref,
                      sdec_ref, o_ref, state_ref):
    @pl.when(pl.program_id(0) == 0)
    def _():
        state_ref[...] = jnp.zeros(state_ref.shape, F32)

    cos = cos_ref[...]
    sin = sin_ref[...]
    lane = lax.broadcasted_iota(jnp.int32, cos.shape, 1)
    first_half = (lane % DQK_C) < (DQK_C // 2)
    qk = qk_ref[...]
    parts = []
    for j in range(qk.shape[1] // LANES):
        t = qk[:, j * LANES:(j + 1) * LANES]
        partner = jnp.where(first_half, pltpu.roll(t, LANES - DQK_C // 2, 1), pltpu.roll(t, DQK_C // 2, 1))
        parts.append(t * cos + partner * sin)
    wq = N_HEADS_C * DQK_C
    q = jnp.concatenate(parts[:wq // LANES], axis=1)
    k = jnp.concatenate(parts[wq // LANES:], axis=1) * (DQK_C ** -0.5)
    qd = (q * qdec_ref[...]).astype(BF16)
    kd = (k * kdec_ref[...]).astype(BF16)
    qb = q.astype(BF16)
    kb = k.astype(BF16)
    vb = v_ref[...].astype(BF16)
    gate = gate_ref[...]
    outs = []
    for h in range(N_HEADS_C):
        qs = slice(h * DQK_C, (h + 1) * DQK_C)
        vs = slice(h * DV_C, (h + 1) * DV_C)
        scores = _dot_nt(qb[:, qs], kb[:, qs]) * dmat_ref[h]
        state = state_ref[h]
        r = _dot(scores.astype(BF16), vb[:, vs]) + _dot(qd[:, qs], state.astype(BF16))
        state_ref[h] = state * sdec_ref[h] + _dot_tn(kd[:, qs], vb[:, vs])
        r = r * lax.rsqrt(jnp.mean(r * r, axis=-1, keepdims=True) + EPS)
        g = gate[:, vs]
        outs.append(r * (g * jax.nn.sigmoid(g)))
    o_ref[...] = jnp.concatenate(outs, axis=1).astype(o_ref.dtype)


def _retention_tables(seq):
    t = BLK_C
    half = DQK_C // 2
    inv_freq = 1.0 / (ROPE_BASE ** (jnp.arange(0, DQK_C, 2, dtype=F32) / DQK_C))
    ang = jnp.arange(seq, dtype=F32)[:, None] * inv_freq[None, :]
    reps = LANES // half
    cos = jnp.tile(jnp.cos(ang), (1, reps))
    sign = jnp.where((jnp.arange(LANES) % DQK_C) < half, -1.0, 1.0).astype(F32)
    sin = jnp.tile(jnp.sin(ang), (1, reps)) * sign[None, :]
    log_g = jnp.log(1.0 - jnp.power(2.0, -5.0 - jnp.arange(N_HEADS_C, dtype=F32)))
    pos = jnp.arange(t, dtype=F32)
    diff = pos[:, None] - pos[None, :]
    same_or_past = (jnp.arange(t)[None, :] // CHUNK) <= (jnp.arange(t)[:, None] // CHUNK)
    dmat = jnp.where(same_or_past[None], jnp.exp(log_g[:, None, None] * jnp.abs(diff)[None]), 0.0)
    qdec = jnp.repeat(jnp.exp(log_g[None, :] * (pos[:, None] + 1.0)), DQK_C, axis=1)
    kdec = jnp.repeat(jnp.exp(log_g[None, :] * (t - 1.0 - pos[:, None])), DQK_C, axis=1)
    sdec = jnp.broadcast_to(jnp.exp(log_g * t)[:, None, None], (N_HEADS_C, 1, DV_C))
    return cos, sin, qdec, kdec, dmat, sdec


def _retention(proj):
    seq = proj.shape[0]
    t = BLK_C
    cos, sin, qdec, kdec, dmat, sdec = _retention_tables(seq)
    wv = N_HEADS_C * DV_C
    return pl.pallas_call(
        _retention_kernel,
        grid=(seq // t,),
        in_specs=[
            pl.BlockSpec((t, wv), lambda i: (i, 0)),
            pl.BlockSpec((t, wv), lambda i: (i, 1)),
            pl.BlockSpec((t, wv), lambda i: (i, 2)),
            pl.BlockSpec((t, LANES), lambda i: (i, 0)),
            pl.BlockSpec((t, LANES), lambda i: (i, 0)),
            pl.BlockSpec((t, N_HEADS_C * DQK_C), lambda i: (0, 0)),
            pl.BlockSpec((t, N_HEADS_C * DQK_C), lambda i: (0, 0)),
            pl.BlockSpec((N_HEADS_C, t, t), lambda i: (0, 0, 0)),
            pl.BlockSpec((N_HEADS_C, 1, DV_C), lambda i: (0, 0, 0)),
        ],
        out_specs=pl.BlockSpec((t, wv), lambda i: (i, 0)),
        out_shape=jax.ShapeDtypeStruct((seq, wv), BF16),
        scratch_shapes=[pltpu.VMEM((N_HEADS_C, DQK_C, DV_C), F32)],
        compiler_params=pltpu.CompilerParams(dimension_semantics=("arbitrary",)),
        name="retention",
    )(proj, proj, proj, cos, sin, qdec, kdec, dmat, sdec)


def _s5_kernel(*refs):
    u_refs = refs[:S5_T]
    mt_ref, bt_ref, ct_ref, are_ref, aim_ref, y2_ref, ut_ref, yt_ref, vr_ref, vi_ref, spr_ref, spi_ref, carry_ref = refs[S5_T:]
    tc = S5_TC
    gp = S5_GROUP
    n = S5_STATE
    half = S5_CH // 2

    @pl.when(pl.program_id(0) == 0)
    def _():
        carry_ref[...] = jnp.zeros(carry_ref.shape, F32)

    for s in range(S5_T):
        for hh in range(2):
            ut_ref[s, hh * half:(hh + 1) * half, :] = u_refs[s][:, hh * half:(hh + 1) * half].T

    def intra(g, carry):
        r0 = pl.multiple_of(g * gp, gp)
        ug = ut_ref[:, pl.ds(r0, gp), :].reshape(S5_T * gp, tc).astype(BF16)
        yt_ref[:, pl.ds(r0, gp), :] = _dot(mt_ref[g], ug).reshape(S5_T, gp, tc)
        vt = _dot(bt_ref[g], ug)
        n0 = pl.multiple_of(g * n, n)
        vr_ref[pl.ds(n0, n), :] = vt[0:n]
        vi_ref[pl.ds(n0, n), :] = vt[n:2 * n]
        return carry

    lax.fori_loop(0, S5_GROUPS, intra, 0)

    ar, ai = are_ref[...], aim_ref[...]
    cr, ci = carry_ref[0], carry_ref[1]
    xr, xi = vr_ref[...], vi_ref[...]
    lane = lax.broadcasted_iota(jnp.int32, xr.shape, 1)
    first = lane == 0
    xr = xr + jnp.where(first, ar * cr - ai * ci, 0.0)
    xi = xi + jnp.where(first, ar * ci + ai * cr, 0.0)
    pr, pi = ar, ai
    d = 1
    while d < tc:
        sr = jnp.where(lane >= d, pltpu.roll(xr, d, 1), 0.0)
        si = jnp.where(lane >= d, pltpu.roll(xi, d, 1), 0.0)
        xr, xi = xr + (pr * sr - pi * si), xi + (pr * si + pi * sr)
        pr, pi = pr * pr - pi * pi, 2.0 * pr * pi
        d *= 2
    spr_ref[...] = jnp.where(first, cr, pltpu.roll(xr, 1, 1))
    spi_ref[...] = jnp.where(first, ci, pltpu.roll(xi, 1, 1))
    carry_ref[0] = jnp.broadcast_to(xr[:, tc - 1:tc], xr.shape)
    carry_ref[1] = jnp.broadcast_to(xi[:, tc - 1:tc], xi.shape)

    def cross(g, carry):
        r0 = pl.multiple_of(g * gp, gp)
        n0 = pl.multiple_of(g * n, n)
        sp = jnp.concatenate([spr_ref[pl.ds(n0, n), :], spi_ref[pl.ds(n0, n), :]], axis=0).astype(BF16)
        yt_ref[:, pl.ds(r0, gp), :] += _dot(ct_ref[g], sp).reshape(S5_T, gp, tc)
        return carry

    lax.fori_loop(0, S5_GROUPS, cross, 0)

    for s in range(S5_T):
        for hh in range(2):
            c0 = s * S5_CH + hh * half
            y2_ref[:, c0:c0 + half] = yt_ref[s, hh * half:(hh + 1) * half, :].T


def _s5_matrices(lam_re, lam_im, log_step, b_re, b_im, c_re, c_im, d_skip):
    hi = lax.Precision.HIGHEST
    t, gp, n, ng = S5_T, S5_GROUP, S5_STATE, S5_GROUPS
    lam = lax.complex(lam_re.astype(F32), lam_im.astype(F32))
    step = jnp.exp(log_step.astype(F32))[:, None]
    ls = lam * step
    a_bar = jnp.exp(ls)
    b_bar = ((a_bar - 1.0) / lam)[..., None] * lax.complex(b_re.astype(F32), b_im.astype(F32))
    cm = lax.complex(c_re.astype(F32), c_im.astype(F32))
    k = jnp.arange(t + 1, dtype=F32)
    apow = jnp.exp(ls[:, None, :] * k[None, :, None].astype(jnp.complex64))
    kmat = jnp.einsum('gpn,gln,gnq->glpq', cm, apow[:, :t], b_bar, precision=hi).real
    lag = jnp.arange(t)[:, None] - jnp.arange(t)[None, :]
    toep = jnp.where((lag >= 0)[None, :, :, None, None], kmat[:, jnp.clip(lag, 0, t - 1)], 0.0)
    mt = jnp.transpose(toep, (0, 1, 3, 2, 4)).reshape(ng, t * gp, t * gp)
    dvec = jnp.tile(d_skip.astype(F32).reshape(ng, 1, gp), (1, t, 1)).reshape(ng, t * gp)
    mt = mt + jnp.eye(t * gp, dtype=F32)[None] * dvec[:, :, None]
    z = apow[:, t - 1 - jnp.arange(t)][:, :, :, None] * b_bar[:, None, :, :]
    z = jnp.transpose(z, (0, 2, 1, 3)).reshape(ng, n, t * gp)
    bt = jnp.concatenate([z.real, z.imag], axis=1)
    w = cm[:, None, :, :] * apow[:, 1:t + 1][:, :, None, :]
    w = w.reshape(ng, t * gp, n)
    ct = jnp.concatenate([w.real, -w.imag], axis=2)
    a_chunk = apow[:, t].reshape(ng * n, 1)
    are = jnp.broadcast_to(a_chunk.real, (ng * n, LANES))
    aim = jnp.broadcast_to(a_chunk.imag, (ng * n, LANES))
    return mt.astype(BF16), bt.astype(BF16), ct.astype(BF16), are, aim


def _s5(proj, mats):
    seq, width = proj.shape
    t, tc, gp, n, ng = S5_T, S5_TC, S5_GROUP, S5_STATE, S5_GROUPS
    nchunks = seq // t
    mt, bt, ct, are, aim = mats
    pview = proj.reshape(nchunks, t * width)
    cpb = width // S5_CH
    u_specs = [pl.BlockSpec((tc, S5_CH), (lambda i, s=s: (i, s * cpb + cpb - 1))) for s in range(t)]
    y2 = pl.pallas_call(
        _s5_kernel,
        grid=(nchunks // tc,),
        in_specs=u_specs + [
            _const_spec(mt.shape), _const_spec(bt.shape), _const_spec(ct.shape),
            _const_spec(are.shape), _const_spec(aim.shape),
        ],
        out_specs=pl.BlockSpec((tc, t * S5_CH), lambda i: (i, 0)),
        out_shape=jax.ShapeDtypeStruct((nchunks, t * S5_CH), F32),
        scratch_shapes=[
            pltpu.VMEM((t, S5_CH, tc), F32),
            pltpu.VMEM((t, S5_CH, tc), F32),
            pltpu.VMEM((ng * n, tc), F32),
            pltpu.VMEM((ng * n, tc), F32),
            pltpu.VMEM((ng * n, tc), F32),
            pltpu.VMEM((ng * n, tc), F32),
            pltpu.VMEM((2, ng * n, tc), F32),
        ],
        compiler_params=pltpu.CompilerParams(dimension_semantics=("arbitrary",)),
        name="s5_scan",
    )(*([pview] * t), mt, bt, ct, are, aim)
    return y2.reshape(seq, S5_CH)


def _outproj_kernel(*refs, glu):
    if glu:
        x_ref, a_ref, b_ref, wo_ref, g_ref, gw_ref, o_ref = refs
        y = jax.nn.gelu(b_ref[...]).astype(BF16)
        gg = _dot(y, gw_ref[...])
        half = gg.shape[1] // 2
        b = (gg[:, :half] * jax.nn.sigmoid(gg[:, half:])).astype(BF16)
    else:
        x_ref, a_ref, b_ref, wo_ref, g_ref, o_ref = refs
        b = b_ref[...]
    cat = jnp.concatenate([a_ref[...], b], axis=1)
    o_ref[...] = x_ref[...] + g_ref[...] * _dot(cat, wo_ref[...])


def _outproj(x, a, b, wo, gate, glu_w=None):
    seq, d = x.shape
    tm = TM_PROJ
    wa, wb = a.shape[1], b.shape[1]
    in_specs = [
        pl.BlockSpec((tm, d), lambda i: (i, 0)),
        pl.BlockSpec((tm, wa), lambda i: (i, 0)),
        pl.BlockSpec((tm, wb), lambda i: (i, 0)),
        _const_spec(wo.shape),
        pl.BlockSpec((1, d), lambda i: (0, 0)),
    ]
    args = [x, a, b, wo, gate]
    if glu_w is not None:
        in_specs.append(_const_spec(glu_w.shape))
        args.append(glu_w)
    return pl.pallas_call(
        functools.partial(_outproj_kernel, glu=glu_w is not None),
        grid=(seq // tm,),
        in_specs=in_specs,
        out_specs=pl.BlockSpec((tm, d), lambda i: (i, 0)),
        out_shape=jax.ShapeDtypeStruct((seq, d), F32),
        compiler_params=pltpu.CompilerParams(dimension_semantics=("parallel",)),
        name="outproj",
    )(*args)


def _ffn_kernel(x_ref, g_ref, sc_ref, sh_ref, gate_ref, win_ref, cw_ref, cb_ref, wout_ref, fg_ref,
                o_ref, h_ref, act_ref, gbuf_ref, carry_ref, *, final):
    tm = x_ref.shape[0]
    halo = gbuf_ref.shape[0] - tm

    @pl.when(pl.program_id(0) == 0)
    def _():
        carry_ref[...] = jnp.zeros(carry_ref.shape, F32)

    x = x_ref[...]
    h_ref[...] = _mod_rmsnorm(x, g_ref[...], sc_ref[...], sh_ref[...]).astype(BF16)
    for f in range(D_FF // TF_FFN):
        cs = slice(f * TF_FFN, (f + 1) * TF_FFN)
        gs = slice(D_FF + f * TF_FFN, D_FF + (f + 1) * TF_FFN)
        h = h_ref[...]
        val = _dot(h, win_ref[:, cs])
        gate = _dot(h, win_ref[:, gs])
        gbuf_ref[0:halo, :] = carry_ref[:, cs]
        gbuf_ref[halo:halo + tm, :] = gate
        carry_ref[:, cs] = gate[tm - halo:tm, :]
        conv = (gate * cw_ref[2:3, cs] + gbuf_ref[halo - 1:halo - 1 + tm, :] * cw_ref[1:2, cs]
                + gbuf_ref[halo - 2:halo - 2 + tm, :] * cw_ref[0:1, cs] + cb_ref[:, cs])
        act_ref[:, cs] = (jax.nn.gelu(conv) * val).astype(BF16)
    xn = x + gate_ref[...] * _dot(act_ref[...], wout_ref[...])
    if final:
        xn = xn * lax.rsqrt(jnp.mean(xn * xn, axis=-1, keepdims=True) + EPS) * fg_ref[...]
    o_ref[...] = xn


def _ffn(x, g, scale, shift, gate, w_in, conv_w, conv_b, w_out, final_g, final):
    seq, d = x.shape
    tm = TM_FFN
    halo = 8
    row = pl.BlockSpec((1, d), lambda i: (0, 0))
    return pl.pallas_call(
        functools.partial(_ffn_kernel, final=final),
        grid=(seq // tm,),
        in_specs=[
            pl.BlockSpec((tm, d), lambda i: (i, 0)),
            row, row, row, row,
            _const_spec(w_in.shape),
            _const_spec(conv_w.shape),
            _const_spec((1, D_FF)),
            _const_spec(w_out.shape),
            row,
        ],
        out_specs=pl.BlockSpec((tm, d), lambda i: (i, 0)),
        out_shape=jax.ShapeDtypeStruct((seq, d), F32),
        scratch_shapes=[
            pltpu.VMEM((tm, d), BF16),
            pltpu.VMEM((tm, D_FF), BF16),
            pltpu.VMEM((tm + halo, TF_FFN), F32),
            pltpu.VMEM((halo, D_FF), F32),
        ],
        compiler_params=pltpu.CompilerParams(dimension_semantics=("arbitrary",)),
        name="conv_ffn",
    )(x, g.reshape(1, d), scale, shift, gate, w_in, conv_w, conv_b.reshape(1, D_FF), w_out,
      final_g.reshape(1, d))


def kernel(x, c, t5_table, mod_w, mod_b, norm1_g, norm2_g, ffn_w_in, ffn_conv_w, ffn_conv_b, ffn_w_out,
           ev_w_in, ev_w_out, diff_lambda, diff_subln_g, band_rel_bias,
           od_w_in, od_w_out, s5_lam_re, s5_lam_im, s5_log_step, s5_b_re, s5_b_im, s5_c_re, s5_c_im,
           s5_d, s5_glu_w, final_g):
    assert x.shape[0] == 1 and x.shape[2] == D_MODEL
    seq = x.shape[1]
    assert seq % TM_PROJ == 0 and seq % (S5_T * S5_TC) == 0
    d = D_MODEL
    xs = x[0]
    mod = _modulation(c, mod_w, mod_b)
    for i in range(DEPTH):
        sh1, sc1, g1, sh2, sc2, g2 = [mod[i, :, k * d:(k + 1) * d] for k in range(6)]
        if i % 2 == 0:
            e = i // 2
            lam_init = 0.8 - 0.6 * math.exp(-0.3 * i)
            lp = diff_lambda[e].astype(F32)
            lam = jnp.exp(jnp.sum(lp[0] * lp[1])) - jnp.exp(jnp.sum(lp[2] * lp[3])) + lam_init
            proj = _normproj(xs, norm1_g[i], sc1, sh1, ev_w_in[e].astype(BF16), BF16)
            out_a = _diff_attention(proj, t5_table, lam, diff_subln_g[e], lam_init)
            out_b = _band_attention(proj, band_rel_bias[e])
            xs = _outproj(xs, out_a, out_b, ev_w_out[e].astype(BF16), g1)
        else:
            o = i // 2
            proj = _normproj(xs, norm1_g[i], sc1, sh1, od_w_in[o].astype(BF16), F32)
            r = _retention(proj)
            mats = _s5_matrices(s5_lam_re[o], s5_lam_im[o], s5_log_step[o], s5_b_re[o], s5_b_im[o],
                                s5_c_re[o], s5_c_im[o], s5_d[o])
            y = _s5(proj, mats)
            xs = _outproj(xs, r, y, od_w_out[o].astype(BF16), g1, glu_w=s5_glu_w[o].astype(BF16))
        xs = _ffn(xs, norm2_g[i], sc2, sh2, g2, ffn_w_in[i].astype(BF16), ffn_conv_w[i], ffn_conv_b[i],
                  ffn_w_out[i].astype(BF16), final_g, final=(i == DEPTH - 1))
    return xs[None]
```

```python
import functools
import math

import jax
import jax.numpy as jnp
from jax import lax
from jax.experimental import pallas as pl
from jax.experimental.pallas import tpu as pltpu

F32 = jnp.float32
BF16 = jnp.bfloat16

D_MODEL = 1024
DEPTH = 2
CHUNK = 64
GROUP_WIDTH = D_MODEL // 2
DK_A = 64
DV_A = 2 * DK_A
N_HEADS_A = GROUP_WIDTH // DV_A
DH_B = 64
N_HEADS_B = GROUP_WIDTH // DH_B
LEFT_CHUNKS = 8
REL_CLIP = 2 * CHUNK
NUM_BUCKETS = 32
MAX_DISTANCE = 128
DV_C = 128
DQK_C = DV_C // 2
N_HEADS_C = GROUP_WIDTH // DV_C
ROPE_BASE = 10000.0
S5_CH = GROUP_WIDTH
S5_GROUP = 16
S5_GROUPS = S5_CH // S5_GROUP
S5_STATE = 64
D_FF = ((8 * D_MODEL // 3 + 255) // 256) * 256
CONV_W = 3
EVEN_IN = 3 * N_HEADS_A * DV_A + 3 * N_HEADS_B * DH_B
ODD_IN = 2 * N_HEADS_C * DQK_C + 2 * N_HEADS_C * DV_C + S5_CH
EPS = 1e-6
NEG_INF = -1e30
LOG2E = math.log2(math.e)

LANES = 128
MXU_DIM = 256

TM_PROJ = 1024
TM_FFN = 512
TF_FFN = MXU_DIM
BLK_A = 512
BLK_B = 512
BLK_C = 256
S5_T = 16
S5_TC = LANES

assert BLK_B == LEFT_CHUNKS * CHUNK, "band window must be exactly one previous block"
assert BLK_A >= MAX_DISTANCE, "far key blocks must sit in the saturated T5 bucket"
assert DV_A == LANES, "diff-attention statistics are kept lane-replicated beside the accumulator"


def _dot(a, b):
    return jnp.dot(a, b, preferred_element_type=F32)


def _dot_nt(a, b):
    return lax.dot_general(a, b, (((1,), (1,)), ((), ())), preferred_element_type=F32)


def _dot_tn(a, b):
    return lax.dot_general(a, b, (((0,), (0,)), ((), ())), preferred_element_type=F32)


def _const_spec(shape):
    zeros = (0,) * len(shape)
    return pl.BlockSpec(shape, lambda *_: zeros, pipeline_mode=pl.Buffered(1))


def _mod_rmsnorm(x, g, scale, shift):
    y = x * lax.rsqrt(jnp.mean(x * x, axis=-1, keepdims=True) + EPS)
    y = y * g
    return y * (1.0 + scale) + shift


def _mod_kernel(c_ref, w_ref, b_ref, o_ref):
    c = c_ref[...]
    cond = c * jax.nn.sigmoid(c)
    o_ref[0] = jnp.sum(cond * w_ref[0], axis=0, keepdims=True) + b_ref[0]


def _modulation(c, mod_w, mod_b):
    depth, d, n = mod_w.shape
    tn = 1536
    return pl.pallas_call(
        _mod_kernel,
        grid=(depth, n // tn),
        in_specs=[
            pl.BlockSpec((d, 1), lambda i, j: (0, 0)),
            pl.BlockSpec((1, d, tn), lambda i, j: (i, 0, j)),
            pl.BlockSpec((1, 1, tn), lambda i, j: (i, 0, j)),
        ],
        out_specs=pl.BlockSpec((1, 1, tn), lambda i, j: (i, 0, j)),
        out_shape=jax.ShapeDtypeStruct((depth, 1, n), F32),
        name="modulation",
    )(c.reshape(d, 1), mod_w, mod_b.reshape(depth, 1, n))


def _normproj_kernel(x_ref, g_ref, sc_ref, sh_ref, w_ref, o_ref, h_ref):
    @pl.when(pl.program_id(1) == 0)
    def _():
        h_ref[...] = _mod_rmsnorm(x_ref[...], g_ref[...], sc_ref[...], sh_ref[...]).astype(BF16)

    o_ref[...] = _dot(h_ref[...], w_ref[...]).astype(o_ref.dtype)


def _normproj(x, g, scale, shift, w, out_dtype):
    seq, d = x.shape
    n = w.shape[1]
    tm, tn = TM_PROJ, 1024
    row = pl.BlockSpec((1, d), lambda i, j: (0, 0))
    return pl.pallas_call(
        _normproj_kernel,
        grid=(seq // tm, n // tn),
        in_specs=[
            pl.BlockSpec((tm, d), lambda i, j: (i, 0)),
            row, row, row,
            pl.BlockSpec((d, tn), lambda i, j: (0, j)),
        ],
        out_specs=pl.BlockSpec((tm, tn), lambda i, j: (i, j)),
        out_shape=jax.ShapeDtypeStruct((seq, n), out_dtype),
        scratch_shapes=[pltpu.VMEM((tm, d), BF16)],
        compiler_params=pltpu.CompilerParams(dimension_semantics=("parallel", "arbitrary")),
        name="normproj",
    )(x, g.reshape(1, d), scale, shift, w)


def _diffattn_kernel(q_ref, k_ref, v_ref, bias_ref, lam_ref, g_ref, o_ref,
                     qs_ref, vt_ref, sa_ref, sb_ref, m_ref, l_ref, acc_ref, *, out_scale):
    blk = BLK_A
    nq = 2 * blk
    sub = 8
    i = pl.program_id(1)

    @pl.when(i == 0)
    def _():
        def tr(b, carry):
            r0 = pl.multiple_of(b * blk, blk)
            vt_ref[:, pl.ds(r0, blk)] = v_ref[pl.ds(r0, blk), :].astype(F32).T.astype(BF16)
            return carry
        lax.fori_loop(0, v_ref.shape[0] // blk, tr, 0)

    q = q_ref[...].astype(F32) * (DK_A ** -0.5 * LOG2E)
    lane = lax.broadcasted_iota(jnp.int32, q.shape, 1)
    qs_ref[0:blk, :] = jnp.where(lane < DK_A, q, 0.0).astype(BF16)
    qs_ref[blk:nq, :] = jnp.where(lane >= DK_A, q, 0.0).astype(BF16)
    m_ref[...] = jnp.full(m_ref.shape, NEG_INF, F32)
    l_ref[...] = jnp.zeros(l_ref.shape, F32)
    acc_ref[...] = jnp.zeros(acc_ref.shape, F32)

    def scores(b, s_ref):
        k = k_ref[pl.ds(pl.multiple_of(b * blk, blk), blk), :]
        s_ref[...] = _dot_nt(k, qs_ref[...])

    def softmax_pv(b, s_ref, bias):
        vt = vt_ref[:, pl.ds(pl.multiple_of(b * blk, blk), blk)]
        s = s_ref[...]
        if bias is not None:
            s = s + jnp.concatenate([bias, bias], axis=1)
        s = s.reshape(blk // sub, sub, nq)
        m_prev = m_ref[...]
        m_cur = jnp.max(jnp.max(s, axis=0), axis=0, keepdims=True)
        m_new = jnp.maximum(m_prev, m_cur)
        alpha = jnp.exp2(m_prev - m_new)
        p = jnp.exp2(s - m_new[None])
        l_ref[...] = alpha * l_ref[...] + jnp.sum(p, axis=0)
        pv = _dot(vt, p.reshape(blk, nq).astype(BF16))
        acc_ref[...] = acc_ref[...] * alpha[0:1] + pv
        m_ref[...] = m_new

    nfar = jnp.maximum(i - 1, 0)
    odd = lax.rem(nfar, 2)

    @pl.when(i == 0)
    def _():
        scores(0, sb_ref)

    @pl.when(i > 0)
    def _():
        @pl.when(odd == 1)
        def _():
            scores(0, sb_ref)
            scores(1, sa_ref)
            softmax_pv(0, sb_ref, None)

        @pl.when(odd == 0)
        def _():
            scores(0, sa_ref)

        def pair_body(t, carry):
            b = odd + 2 * t
            scores(b + 1, sb_ref)
            softmax_pv(b, sa_ref, None)
            scores(b + 2, sa_ref)
            softmax_pv(b + 1, sb_ref, None)
            return carry

        lax.fori_loop(0, nfar // 2, pair_body, 0)
        scores(i, sb_ref)
        softmax_pv(i - 1, sa_ref, bias_ref[0, 0])

    softmax_pv(i, sb_ref, bias_ref[0, 1])

    ot = acc_ref[...] / jnp.sum(l_ref[...], axis=0, keepdims=True)
    o = ot[:, 0:blk].T - lam_ref[...] * ot[:, blk:nq].T
    o = o * lax.rsqrt(jnp.mean(o * o, axis=-1, keepdims=True) + EPS) * g_ref[...]
    o_ref[...] = (o * out_scale).astype(o_ref.dtype)


_TOEPLITZ_ROWS = 512
_TOEPLITZ_N = 2048


def _toeplitz_kernel(v_ref, o_ref, *, keep):
    rows, cols = o_ref.shape[1:]
    x = jnp.broadcast_to(v_ref[0, 0], (rows, v_ref.shape[-1]))
    tile = pltpu.roll(x, 0, 1, stride=1, stride_axis=0)[:, :cols]
    r = lax.broadcasted_iota(jnp.int32, (rows, cols), 0) + pl.program_id(1) * rows
    c = lax.broadcasted_iota(jnp.int32, (rows, cols), 1)
    o_ref[0] = jnp.where(keep(r, c), tile, NEG_INF)


def _toeplitz_tiles(fn, keep, heads, rows, cols):
    n, rb = _TOEPLITZ_N, _TOEPLITZ_ROWS
    assert rows % rb == 0 and rows <= n // 2 and cols <= n // 2
    idx = jnp.arange(n, dtype=jnp.int32)
    vec = fn(jnp.where(idx < n // 2, idx, idx - n)).astype(F32)
    vecs = jnp.stack([jnp.roll(vec, k * rb, axis=1) for k in range(rows // rb)], axis=1)
    return pl.pallas_call(
        functools.partial(_toeplitz_kernel, keep=keep),
        grid=(heads, rows // rb),
        in_specs=[pl.BlockSpec((1, 1, 1, n), lambda h, k: (h, k, 0, 0))],
        out_specs=pl.BlockSpec((1, rb, cols), lambda h, k: (h, k, 0)),
        out_shape=jax.ShapeDtypeStruct((heads, rows, cols), F32),
        name="toeplitz_tiles",
    )(vecs.reshape(heads, rows // rb, 1, n))


def _t5_bucket(rel):
    nb = NUM_BUCKETS // 2
    max_exact = nb // 2
    bucket = jnp.where(rel > 0, nb, 0)
    n = jnp.abs(rel)
    nf = jnp.maximum(n, 1).astype(F32)
    large = max_exact + (jnp.log(nf / max_exact) / math.log(MAX_DISTANCE / max_exact)
                         * (nb - max_exact)).astype(jnp.int32)
    large = jnp.minimum(large, nb - 1)
    return bucket + jnp.where(n < max_exact, n, large)


def _diff_bias_tiles(t5_table):
    blk = BLK_A
    table = t5_table.astype(F32)
    far = table[_t5_bucket(jnp.full((), -(blk + 1), jnp.int32))]
    def visible(r, c):
        return jnp.floor_divide(r - blk, CHUNK) <= jnp.floor_divide(c, CHUNK)

    tiles = _toeplitz_tiles(lambda x: ((table[_t5_bucket(-x - blk)] - far) * LOG2E).T, visible,
                            N_HEADS_A, 2 * blk, blk)
    return tiles.reshape(N_HEADS_A, 2, blk, blk)


def _diff_attention(proj, t5_table, lam, subln_g, lam_init):
    seq = proj.shape[0]
    blk = BLK_A
    bias = _diff_bias_tiles(t5_table)
    ha = N_HEADS_A
    kern = functools.partial(_diffattn_kernel, out_scale=1.0 - lam_init)
    return pl.pallas_call(
        kern,
        grid=(ha, seq // blk),
        in_specs=[
            pl.BlockSpec((blk, DV_A), lambda h, i: (i, h)),
            pl.BlockSpec((seq, DV_A), lambda h, i: (0, ha + h)),
            pl.BlockSpec((seq, DV_A), lambda h, i: (0, 2 * ha + h)),
            pl.BlockSpec((1, 2, blk, blk), lambda h, i: (h, 0, 0, 0)),
            pl.BlockSpec((1, DV_A), lambda h, i: (0, 0)),
            pl.BlockSpec((1, DV_A), lambda h, i: (0, 0)),
        ],
        out_specs=pl.BlockSpec((blk, DV_A), lambda h, i: (i, h)),
        out_shape=jax.ShapeDtypeStruct((seq, ha * DV_A), BF16),
        scratch_shapes=[
            pltpu.VMEM((2 * blk, DV_A), BF16),
            pltpu.VMEM((DV_A, seq), BF16),
            pltpu.VMEM((blk, 2 * blk), F32),
            pltpu.VMEM((blk, 2 * blk), F32),
            pltpu.VMEM((8, 2 * blk), F32),
            pltpu.VMEM((8, 2 * blk), F32),
            pltpu.VMEM((DV_A, 2 * blk), F32),
        ],
        compiler_params=pltpu.CompilerParams(dimension_semantics=("parallel", "arbitrary")),
        name="diff_attention",
    )(proj, proj, proj, bias, jnp.full((1, DV_A), lam, F32), subln_g.reshape(1, DV_A).astype(F32))


def _band_kernel(q_ref, kp_ref, kc_ref, vp_ref, vc_ref, bias_ref, o_ref):
    blk = BLK_B
    i = pl.program_id(1)
    q = q_ref[...] * (DH_B ** -0.5)
    lane = lax.broadcasted_iota(jnp.int32, q.shape, 1)
    kp, kc, vp, vc = kp_ref[...], kc_ref[...], vp_ref[...], vc_ref[...]
    no_prev = jnp.where(i == 0, NEG_INF, 0.0).astype(F32)
    outs = []
    for hh in range(2):
        sel = (lane < DH_B) if hh == 0 else (lane >= DH_B)
        qm = jnp.where(sel, q, 0).astype(BF16)
        sp = _dot_nt(qm, kp) + bias_ref[hh, :, 0:blk] + no_prev
        sc = _dot_nt(qm, kc) + bias_ref[hh, :, blk:2 * blk]
        m = jnp.maximum(jnp.max(sp, axis=-1, keepdims=True), jnp.max(sc, axis=-1, keepdims=True))
        pp = jnp.exp(sp - m)
        pc = jnp.exp(sc - m)
        l = jnp.sum(pp, axis=-1, keepdims=True) + jnp.sum(pc, axis=-1, keepdims=True)
        outs.append((_dot(pp.astype(BF16), vp) + _dot(pc.astype(BF16), vc)) / l)
    o_ref[...] = jnp.where(lane < DH_B, outs[0], outs[1]).astype(o_ref.dtype)


def _band_bias_tiles(rel_bias):
    blk = BLK_B

    def valid(r, c):
        qchunk = jnp.floor_divide(r, CHUNK)
        kchunk = jnp.floor_divide(c - blk, CHUNK)
        return (kchunk <= qchunk) & (kchunk >= qchunk - LEFT_CHUNKS)

    return _toeplitz_tiles(
        lambda x: rel_bias.astype(F32)[:, jnp.clip(x - blk, -REL_CLIP, REL_CLIP) + REL_CLIP], valid,
        N_HEADS_B, blk, 2 * blk)


def _band_attention(proj, rel_bias):
    seq = proj.shape[0]
    blk = BLK_B
    bias = _band_bias_tiles(rel_bias)
    npair = N_HEADS_B // 2
    qc0 = 3 * N_HEADS_A
    prev = lambda c0: (lambda hp, i: (jnp.maximum(i - 1, 0), c0 + hp))
    cur = lambda c0: (lambda hp, i: (i, c0 + hp))
    return pl.pallas_call(
        _band_kernel,
        grid=(npair, seq // blk),
        in_specs=[
            pl.BlockSpec((blk, LANES), cur(qc0)),
            pl.BlockSpec((blk, LANES), prev(qc0 + npair)),
            pl.BlockSpec((blk, LANES), cur(qc0 + npair)),
            pl.BlockSpec((blk, LANES), prev(qc0 + 2 * npair)),
            pl.BlockSpec((blk, LANES), cur(qc0 + 2 * npair)),
            pl.BlockSpec((2, blk, 2 * blk), lambda hp, i: (hp, 0, 0)),
        ],
        out_specs=pl.BlockSpec((blk, LANES), lambda hp, i: (i, hp)),
        out_shape=jax.ShapeDtypeStruct((seq, N_HEADS_B * DH_B), BF16),
        compiler_params=pltpu.CompilerParams(dimension_semantics=("parallel", "arbitrary")),
        name="band_attention",
    )(proj, proj, proj, proj, proj, bias)


def _retention_kernel(qk_ref, v_ref, gate_ref, cos_ref, sin_ref, qdec_ref, kdec_ref, dmat_ref,
                      sdec_ref, o_ref, state_ref):
    @pl.when(pl.program_id(0) == 0)
    def _():
        state_ref[...] = jnp.zeros(state_ref.shape, F32)

    cos = cos_ref[...]
    sin = sin_ref[...]
    lane = lax.broadcasted_iota(jnp.int32, cos.shape, 1)
    first_half = (lane % DQK_C) < (DQK_C // 2)
    qk = qk_ref[...]
    parts = []
    for j in range(qk.shape[1] // LANES):
        t = qk[:, j * LANES:(j + 1) * LANES]
        partner = jnp.where(first_half, pltpu.roll(t, LANES - DQK_C // 2, 1), pltpu.roll(t, DQK_C // 2, 1))
        parts.append(t * cos + partner * sin)
    wq = N_HEADS_C * DQK_C
    q = jnp.concatenate(parts[:wq // LANES], axis=1)
    k = jnp.concatenate(parts[wq // LANES:], axis=1) * (DQK_C ** -0.5)
    qd = (q * qdec_ref[...]).astype(BF16)
    kd = (k * kdec_ref[...]).astype(BF16)
    qb = q.astype(BF16)
    kb = k.astype(BF16)
    vb = v_ref[...].astype(BF16)
    gate = gate_ref[...]
    outs = []
    for h in range(N_HEADS_C):
        qs = slice(h * DQK_C, (h + 1) * DQK_C)
        vs = slice(h * DV_C, (h + 1) * DV_C)
        scores = _dot_nt(qb[:, qs], kb[:, qs]) * dmat_ref[h]
        state = state_ref[h]
        r = _dot(scores.astype(BF16), vb[:, vs]) + _dot(qd[:, qs], state.astype(BF16))
        state_ref[h] = state * sdec_ref[h] + _dot_tn(kd[:, qs], vb[:, vs])
        r = r * lax.rsqrt(jnp.mean(r * r, axis=-1, keepdims=True) + EPS)
        g = gate[:, vs]
        outs.append(r * (g * jax.nn.sigmoid(g)))
    o_ref[...] = jnp.concatenate(outs, axis=1).astype(o_ref.dtype)


def _retention_tables(seq):
    t = BLK_C
    half = DQK_C // 2
    inv_freq = 1.0 / (ROPE_BASE ** (jnp.arange(0, DQK_C, 2, dtype=F32) / DQK_C))
    ang = jnp.arange(seq, dtype=F32)[:, None] * inv_freq[None, :]
    reps = LANES // half
    cos = jnp.tile(jnp.cos(ang), (1, reps))
    sign = jnp.where((jnp.arange(LANES) % DQK_C) < half, -1.0, 1.0).astype(F32)
    sin = jnp.tile(jnp.sin(ang), (1, reps)) * sign[None, :]
    log_g = jnp.log(1.0 - jnp.power(2.0, -5.0 - jnp.arange(N_HEADS_C, dtype=F32)))
    pos = jnp.arange(t, dtype=F32)
    diff = pos[:, None] - pos[None, :]
    same_or_past = (jnp.arange(t)[None, :] // CHUNK) <= (jnp.arange(t)[:, None] // CHUNK)
    dmat = jnp.where(same_or_past[None], jnp.exp(log_g[:, None, None] * jnp.abs(diff)[None]), 0.0)
    qdec = jnp.repeat(jnp.exp(log_g[None, :] * (pos[:, None] + 1.0)), DQK_C, axis=1)
    kdec = jnp.repeat(jnp.exp(log_g[None, :] * (t - 1.0 - pos[:, None])), DQK_C, axis=1)
    sdec = jnp.broadcast_to(jnp.exp(log_g * t)[:, None, None], (N_HEADS_C, 1, DV_C))
    return cos, sin, qdec, kdec, dmat, sdec


def _retention(proj):
    seq = proj.shape[0]
    t = BLK_C
    cos, sin, qdec, kdec, dmat, sdec = _retention_tables(seq)
    wv = N_HEADS_C * DV_C
    return pl.pallas_call(
        _retention_kernel,
        grid=(seq // t,),
        in_specs=[
            pl.BlockSpec((t, wv), lambda i: (i, 0)),
            pl.BlockSpec((t, wv), lambda i: (i, 1)),
            pl.BlockSpec((t, wv), lambda i: (i, 2)),
            pl.BlockSpec((t, LANES), lambda i: (i, 0)),
            pl.BlockSpec((t, LANES), lambda i: (i, 0)),
            pl.BlockSpec((t, N_HEADS_C * DQK_C), lambda i: (0, 0)),
            pl.BlockSpec((t, N_HEADS_C * DQK_C), lambda i: (0, 0)),
            pl.BlockSpec((N_HEADS_C, t, t), lambda i: (0, 0, 0)),
            pl.BlockSpec((N_HEADS_C, 1, DV_C), lambda i: (0, 0, 0)),
        ],
        out_specs=pl.BlockSpec((t, wv), lambda i: (i, 0)),
        out_shape=jax.ShapeDtypeStruct((seq, wv), BF16),
        scratch_shapes=[pltpu.VMEM((N_HEADS_C, DQK_C, DV_C), F32)],
        compiler_params=pltpu.CompilerParams(dimension_semantics=("arbitrary",)),
        name="retention",
    )(proj, proj, proj, cos, sin, qdec, kdec, dmat, sdec)


def _s5_kernel(*refs):
    ncb = S5_CH // LANES
    u_refs = refs[:ncb]
    (mt_ref, bt_ref, ct_ref, are_ref, aim_ref, y_ref,
     ut_ref, yt_ref, ys_ref, vr_ref, vi_ref, spr_ref, spi_ref, carry_ref) = refs[ncb:]
    tc = S5_TC
    gp = S5_GROUP
    n = S5_STATE

    @pl.when(pl.program_id(0) == 0)
    def _():
        carry_ref[...] = jnp.zeros(carry_ref.shape, F32)

    for s in range(S5_T):
        for k in range(ncb):
            ut_ref[s, k * LANES:(k + 1) * LANES, :] = u_refs[k][pl.ds(s, tc, stride=S5_T), :].T

    def intra(g, carry):
        r0 = pl.multiple_of(g * gp, gp)
        ug = ut_ref[:, pl.ds(r0, gp), :].reshape(S5_T * gp, tc).astype(BF16)
        yt_ref[:, pl.ds(r0, gp), :] = _dot(mt_ref[g], ug).reshape(S5_T, gp, tc)
        vt = _dot(bt_ref[g], ug)
        n0 = pl.multiple_of(g * n, n)
        vr_ref[pl.ds(n0, n), :] = vt[0:n]
        vi_ref[pl.ds(n0, n), :] = vt[n:2 * n]
        return carry

    lax.fori_loop(0, S5_GROUPS, intra, 0)

    ar, ai = are_ref[...], aim_ref[...]
    cr, ci = carry_ref[0], carry_ref[1]
    xr, xi = vr_ref[...], vi_ref[...]
    lane = lax.broadcasted_iota(jnp.int32, xr.shape, 1)
    first = lane == 0
    xr = xr + jnp.where(first, ar * cr - ai * ci, 0.0)
    xi = xi + jnp.where(first, ar * ci + ai * cr, 0.0)
    pr, pi = ar, ai
    d = 1
    while d < tc:
        sr = jnp.where(lane >= d, pltpu.roll(xr, d, 1), 0.0)
        si = jnp.where(lane >= d, pltpu.roll(xi, d, 1), 0.0)
        xr, xi = xr + (pr * sr - pi * si), xi + (pr * si + pi * sr)
        pr, pi = pr * pr - pi * pi, 2.0 * pr * pi
        d *= 2
    spr_ref[...] = jnp.where(first, cr, pltpu.roll(xr, 1, 1))
    spi_ref[...] = jnp.where(first, ci, pltpu.roll(xi, 1, 1))
    carry_ref[0] = jnp.broadcast_to(xr[:, tc - 1:tc], xr.shape)
    carry_ref[1] = jnp.broadcast_to(xi[:, tc - 1:tc], xi.shape)

    def cross(g, carry):
        r0 = pl.multiple_of(g * gp, gp)
        n0 = pl.multiple_of(g * n, n)
        sp = jnp.concatenate([spr_ref[pl.ds(n0, n), :], spi_ref[pl.ds(n0, n), :]], axis=0).astype(BF16)
        yt_ref[:, pl.ds(r0, gp), :] += _dot(ct_ref[g], sp).reshape(S5_T, gp, tc)
        return carry

    lax.fori_loop(0, S5_GROUPS, cross, 0)

    for s in range(S5_T):
        for k in range(ncb):
            ys_ref[k, pl.ds(s, tc, stride=S5_T), :] = yt_ref[s, k * LANES:(k + 1) * LANES, :].T
    for k in range(ncb):
        y_ref[:, k * LANES:(k + 1) * LANES] = ys_ref[k]


def _s5_matrices(lam_re, lam_im, log_step, b_re, b_im, c_re, c_im, d_skip):
    hi = lax.Precision.HIGHEST
    t, gp, n, ng = S5_T, S5_GROUP, S5_STATE, S5_GROUPS
    lam = lax.complex(lam_re.astype(F32), lam_im.astype(F32))
    step = jnp.exp(log_step.astype(F32))[:, None]
    ls = lam * step
    a_bar = jnp.exp(ls)
    b_bar = ((a_bar - 1.0) / lam)[..., None] * lax.complex(b_re.astype(F32), b_im.astype(F32))
    cm = lax.complex(c_re.astype(F32), c_im.astype(F32))

    def apow(k):
        kk = k.astype(F32).astype(jnp.complex64)
        return jnp.exp(ls.reshape((ng,) + (1,) * k.ndim + (n,)) * kk[None, ..., None])

    tt = jnp.arange(t)
    lag = tt[:, None] - tt[None, :]
    alag = jnp.where((lag >= 0)[None, :, :, None], apow(jnp.maximum(lag, 0)), 0.0)
    mt = jnp.einsum('gpn,gtun,gnq->gtpuq', cm, alag, b_bar, precision=hi).real
    mt = mt.reshape(ng, t * gp, t * gp)
    dvec = jnp.tile(d_skip.astype(F32).reshape(ng, 1, gp), (1, t, 1)).reshape(ng, t * gp)
    mt = mt + jnp.eye(t * gp, dtype=F32)[None] * dvec[:, :, None]
    z = jnp.swapaxes(apow(t - 1 - tt), 1, 2)[:, :, :, None] * b_bar[:, :, None, :]
    z = z.reshape(ng, n, t * gp)
    bt = jnp.concatenate([z.real, z.imag], axis=1)
    w = cm[:, None, :, :] * apow(tt + 1)[:, :, None, :]
    w = w.reshape(ng, t * gp, n)
    ct = jnp.concatenate([w.real, -w.imag], axis=2)
    a_chunk = apow(jnp.full((), t)).reshape(ng * n, 1)
    are = jnp.broadcast_to(a_chunk.real, (ng * n, LANES))
    aim = jnp.broadcast_to(a_chunk.imag, (ng * n, LANES))
    return mt.astype(BF16), bt.astype(BF16), ct.astype(BF16), are, aim


def _s5(proj, mats):
    seq, width = proj.shape
    t, tc, gp, n, ng = S5_T, S5_TC, S5_GROUP, S5_STATE, S5_GROUPS
    rows = t * tc
    mt, bt, ct, are, aim = mats
    ncb = S5_CH // LANES
    cb0 = (width - S5_CH) // LANES
    u_specs = [pl.BlockSpec((rows, LANES), (lambda i, k=k: (i, cb0 + k))) for k in range(ncb)]
    return pl.pallas_call(
        _s5_kernel,
        grid=(seq // rows,),
        in_specs=u_specs + [
            _const_spec(mt.shape), _const_spec(bt.shape), _const_spec(ct.shape),
            _const_spec(are.shape), _const_spec(aim.shape),
        ],
        out_specs=pl.BlockSpec((rows, S5_CH), lambda i: (i, 0)),
        out_shape=jax.ShapeDtypeStruct((seq, S5_CH), F32),
        scratch_shapes=[
            pltpu.VMEM((t, S5_CH, tc), F32),
            pltpu.VMEM((t, S5_CH, tc), F32),
            pltpu.VMEM((ncb, rows, LANES), F32),
            pltpu.VMEM((ng * n, tc), F32),
            pltpu.VMEM((ng * n, tc), F32),
            pltpu.VMEM((ng * n, tc), F32),
            pltpu.VMEM((ng * n, tc), F32),
            pltpu.VMEM((2, ng * n, tc), F32),
        ],
        compiler_params=pltpu.CompilerParams(dimension_semantics=("arbitrary",)),
        name="s5_scan",
    )(*([proj] * ncb), mt, bt, ct, are, aim)


def _outproj_kernel(*refs, glu):
    if glu:
        x_ref, a_ref, b_ref, wo_ref, g_ref, gw_ref, o_ref = refs
        y = jax.nn.gelu(b_ref[...]).astype(BF16)
        gg = _dot(y, gw_ref[...])
        half = gg.shape[1] // 2
        b = (gg[:, :half] * jax.nn.sigmoid(gg[:, half:])).astype(BF16)
    else:
        x_ref, a_ref, b_ref, wo_ref, g_ref, o_ref = refs
        b = b_ref[...]
    cat = jnp.concatenate([a_ref[...], b], axis=1)
    o_ref[...] = x_ref[...] + g_ref[...] * _dot(cat, wo_ref[...])


def _outproj(x, a, b, wo, gate, glu_w=None):
    seq, d = x.shape
    tm = TM_PROJ
    wa, wb = a.shape[1], b.shape[1]
    in_specs = [
        pl.BlockSpec((tm, d), lambda i: (i, 0)),
        pl.BlockSpec((tm, wa), lambda i: (i, 0)),
        pl.BlockSpec((tm, wb), lambda i: (i, 0)),
        _const_spec(wo.shape),
        pl.BlockSpec((1, d), lambda i: (0, 0)),
    ]
    args = [x, a, b, wo, gate]
    if glu_w is not None:
        in_specs.append(_const_spec(glu_w.shape))
        args.append(glu_w)
    return pl.pallas_call(
        functools.partial(_outproj_kernel, glu=glu_w is not None),
        grid=(seq // tm,),
        in_specs=in_specs,
        out_specs=pl.BlockSpec((tm, d), lambda i: (i, 0)),
        out_shape=jax.ShapeDtypeStruct((seq, d), F32),
        compiler_params=pltpu.CompilerParams(dimension_semantics=("parallel",)),
        name="outproj",
    )(*args)


def _ffn_kernel(x_ref, g_ref, sc_ref, sh_ref, gate_ref, win_ref, cw_ref, cb_ref, wout_ref, fg_ref,
                o_ref, h_ref, act_ref, gbuf_ref, carry_ref, *, final):
    tm = x_ref.shape[0]
    halo = gbuf_ref.shape[0] - tm

    @pl.when(pl.program_id(0) == 0)
    def _():
        carry_ref[...] = jnp.zeros(carry_ref.shape, F32)

    x = x_ref[...]
    h_ref[...] = _mod_rmsnorm(x, g_ref[...], sc_ref[...], sh_ref[...]).astype(BF16)
    for f in range(D_FF // TF_FFN):
        cs = slice(f * TF_FFN, (f + 1) * TF_FFN)
        gs = slice(D_FF + f * TF_FFN, D_FF + (f + 1) * TF_FFN)
        h = h_ref[...]
        val = _dot(h, win_ref[:, cs])
        gate = _dot(h, win_ref[:, gs])
        gbuf_ref[0:halo, :] = carry_ref[:, cs]
        gbuf_ref[halo:halo + tm, :] = gate
        carry_ref[:, cs] = gate[tm - halo:tm, :]
        conv = (gate * cw_ref[2:3, cs] + gbuf_ref[halo - 1:halo - 1 + tm, :] * cw_ref[1:2, cs]
                + gbuf_ref[halo - 2:halo - 2 + tm, :] * cw_ref[0:1, cs] + cb_ref[:, cs])
        act_ref[:, cs] = (jax.nn.gelu(conv) * val).astype(BF16)
    xn = x + gate_ref[...] * _dot(act_ref[...], wout_ref[...])
    if final:
        xn = xn * lax.rsqrt(jnp.mean(xn * xn, axis=-1, keepdims=True) + EPS) * fg_ref[...]
    o_ref[...] = xn


def _layer_spec(shape, layer):
    idx = (layer,) + (0,) * (len(shape) - 1)
    return pl.BlockSpec((None,) + tuple(shape[1:]), lambda *_: idx, pipeline_mode=pl.Buffered(1))


def _ffn(x, g, scale, shift, gate, w_in, conv_w, conv_b, w_out, final_g, layer, final):
    seq, d = x.shape
    tm = TM_FFN
    halo = 8
    row = pl.BlockSpec((1, d), lambda i: (0, 0))
    conv_b = conv_b.reshape(conv_b.shape[0], 1, D_FF)
    return pl.pallas_call(
        functools.partial(_ffn_kernel, final=final),
        grid=(seq // tm,),
        in_specs=[
            pl.BlockSpec((tm, d), lambda i: (i, 0)),
            row, row, row, row,
            _layer_spec(w_in.shape, layer),
            _layer_spec(conv_w.shape, layer),
            _layer_spec(conv_b.shape, layer),
            _layer_spec(w_out.shape, layer),
            row,
        ],
        out_specs=pl.BlockSpec((tm, d), lambda i: (i, 0)),
        out_shape=jax.ShapeDtypeStruct((seq, d), F32),
        scratch_shapes=[
            pltpu.VMEM((tm, d), BF16),
            pltpu.VMEM((tm, D_FF), BF16),
            pltpu.VMEM((tm + halo, TF_FFN), F32),
            pltpu.VMEM((halo, D_FF), F32),
        ],
        compiler_params=pltpu.CompilerParams(dimension_semantics=("arbitrary",)),
        name="conv_ffn",
    )(x, g.reshape(1, d), scale, shift, gate, w_in, conv_w, conv_b, w_out, final_g.reshape(1, d))


def kernel(x, c, t5_table, mod_w, mod_b, norm1_g, norm2_g, ffn_w_in, ffn_conv_w, ffn_conv_b, ffn_w_out,
           ev_w_in, ev_w_out, diff_lambda, diff_subln_g, band_rel_bias,
           od_w_in, od_w_out, s5_lam_re, s5_lam_im, s5_log_step, s5_b_re, s5_b_im, s5_c_re, s5_c_im,
           s5_d, s5_glu_w, final_g):
    assert x.shape[0] == 1 and x.shape[2] == D_MODEL
    seq = x.shape[1]
    assert seq % TM_PROJ == 0 and seq % (S5_T * S5_TC) == 0
    d = D_MODEL
    xs = x[0]
    mod = _modulation(c, mod_w, mod_b)
    ffn_w_in_b = ffn_w_in.astype(BF16)
    ffn_w_out_b = ffn_w_out.astype(BF16)
    for i in range(DEPTH):
        sh1, sc1, g1, sh2, sc2, g2 = [mod[i, :, k * d:(k + 1) * d] for k in range(6)]
        if i % 2 == 0:
            e = i // 2
            lam_init = 0.8 - 0.6 * math.exp(-0.3 * i)
            lp = diff_lambda[e].astype(F32)
            lam = jnp.exp(jnp.sum(lp[0] * lp[1])) - jnp.exp(jnp.sum(lp[2] * lp[3])) + lam_init
            proj = _normproj(xs, norm1_g[i], sc1, sh1, ev_w_in[e].astype(BF16), BF16)
            out_a = _diff_attention(proj, t5_table, lam, diff_subln_g[e], lam_init)
            out_b = _band_attention(proj, band_rel_bias[e])
            xs = _outproj(xs, out_a, out_b, ev_w_out[e].astype(BF16), g1)
        else:
            o = i // 2
            proj = _normproj(xs, norm1_g[i], sc1, sh1, od_w_in[o].astype(BF16), F32)
            r = _retention(proj)
            mats = _s5_matrices(s5_lam_re[o], s5_lam_im[o], s5_log_step[o], s5_b_re[o], s5_b_im[o],
                                s5_c_re[o], s5_c_im[o], s5_d[o])
            y = _s5(proj, mats)
            xs = _outproj(xs, r, y, od_w_out[o].astype(BF16), g1, glu_w=s5_glu_w[o].astype(BF16))
        xs = _ffn(xs, norm2_g[i], sc2, sh2, g2, ffn_w_in_b, ffn_conv_w, ffn_conv_b, ffn_w_out_b, final_g,
                  layer=i, final=(i == DEPTH - 1))
    return xs[None]
```

```python
import functools
import math

import jax
import jax.numpy as jnp
from jax import lax
from jax.experimental import pallas as pl
from jax.experimental.pallas import tpu as pltpu

F32 = jnp.float32
BF16 = jnp.bfloat16

D_MODEL = 1024
DEPTH = 2
CHUNK = 64
GROUP_WIDTH = D_MODEL // 2
DK_A = 64
DV_A = 2 * DK_A
N_HEADS_A = GROUP_WIDTH // DV_A
DH_B = 64
N_HEADS_B = GROUP_WIDTH // DH_B
LEFT_CHUNKS = 8
REL_CLIP = 2 * CHUNK
NUM_BUCKETS = 32
MAX_DISTANCE = 128
DV_C = 128
DQK_C = DV_C // 2
N_HEADS_C = GROUP_WIDTH // DV_C
ROPE_BASE = 10000.0
S5_CH = GROUP_WIDTH
S5_GROUP = 16
S5_GROUPS = S5_CH // S5_GROUP
S5_STATE = 64
D_FF = ((8 * D_MODEL // 3 + 255) // 256) * 256
CONV_W = 3
EVEN_IN = 3 * N_HEADS_A * DV_A + 3 * N_HEADS_B * DH_B
ODD_IN = 2 * N_HEADS_C * DQK_C + 2 * N_HEADS_C * DV_C + S5_CH
EPS = 1e-6
NEG_INF = -1e30
LOG2E = math.log2(math.e)

LANES = 128
MXU_DIM = 256

TM_PROJ = 1024
TM_FFN = 512
TF_FFN = MXU_DIM
BLK_A = 512
BLK_B = 512
BLK_C = 256
S5_T = 16
S5_TC = LANES

assert BLK_B == LEFT_CHUNKS * CHUNK, "band window must be exactly one previous block"
assert BLK_A >= MAX_DISTANCE, "far key blocks must sit in the saturated T5 bucket"
assert DV_A == LANES, "diff-attention statistics are kept lane-replicated beside the accumulator"


def _dot(a, b):
    return jnp.dot(a, b, preferred_element_type=F32)


def _dot_nt(a, b):
    return lax.dot_general(a, b, (((1,), (1,)), ((), ())), preferred_element_type=F32)


def _dot_tn(a, b):
    return lax.dot_general(a, b, (((0,), (0,)), ((), ())), preferred_element_type=F32)


def _const_spec(shape):
    zeros = (0,) * len(shape)
    return pl.BlockSpec(shape, lambda *_: zeros, pipeline_mode=pl.Buffered(1))


def _mod_rmsnorm(x, g, scale, shift):
    y = x * lax.rsqrt(jnp.mean(x * x, axis=-1, keepdims=True) + EPS)
    y = y * g
    return y * (1.0 + scale) + shift


def _mod_kernel(c_ref, w_ref, b_ref, o_ref):
    c = c_ref[...]
    cond = c * jax.nn.sigmoid(c)
    o_ref[0] = jnp.sum(cond * w_ref[0], axis=0, keepdims=True) + b_ref[0]


def _modulation(c, mod_w, mod_b):
    depth, d, n = mod_w.shape
    tn = 1536
    return pl.pallas_call(
        _mod_kernel,
        grid=(depth, n // tn),
        in_specs=[
            pl.BlockSpec((d, 1), lambda i, j: (0, 0)),
            pl.BlockSpec((1, d, tn), lambda i, j: (i, 0, j)),
            pl.BlockSpec((1, 1, tn), lambda i, j: (i, 0, j)),
        ],
        out_specs=pl.BlockSpec((1, 1, tn), lambda i, j: (i, 0, j)),
        out_shape=jax.ShapeDtypeStruct((depth, 1, n), F32),
        name="modulation",
    )(c.reshape(d, 1), mod_w, mod_b.reshape(depth, 1, n))


def _normproj_kernel(x_ref, g_ref, sc_ref, sh_ref, w_ref, o_ref, h_ref):
    @pl.when(pl.program_id(1) == 0)
    def _():
        h_ref[...] = _mod_rmsnorm(x_ref[...], g_ref[...], sc_ref[...], sh_ref[...]).astype(BF16)

    o_ref[...] = _dot(h_ref[...], w_ref[...]).astype(o_ref.dtype)


def _normproj(x, g, scale, shift, w, out_dtype):
    seq, d = x.shape
    n = w.shape[1]
    tm, tn = TM_PROJ, 1024
    row = pl.BlockSpec((1, d), lambda i, j: (0, 0))
    return pl.pallas_call(
        _normproj_kernel,
        grid=(seq // tm, n // tn),
        in_specs=[
            pl.BlockSpec((tm, d), lambda i, j: (i, 0)),
            row, row, row,
            pl.BlockSpec((d, tn), lambda i, j: (0, j)),
        ],
        out_specs=pl.BlockSpec((tm, tn), lambda i, j: (i, j)),
        out_shape=jax.ShapeDtypeStruct((seq, n), out_dtype),
        scratch_shapes=[pltpu.VMEM((tm, d), BF16)],
        compiler_params=pltpu.CompilerParams(dimension_semantics=("parallel", "arbitrary")),
        name="normproj",
    )(x, g.reshape(1, d), scale, shift, w)


def _diffattn_kernel(q_ref, k_ref, v_ref, bias_ref, lam_ref, g_ref, o_ref,
                     qs_ref, vt_ref, sa_ref, sb_ref, m_ref, l_ref, acc_ref, *, out_scale):
    blk = BLK_A
    nq = 2 * blk
    sub = 8
    i = pl.program_id(1)

    @pl.when(i == 0)
    def _():
        def tr(b, carry):
            r0 = pl.multiple_of(b * blk, blk)
            vt_ref[:, pl.ds(r0, blk)] = v_ref[pl.ds(r0, blk), :].astype(F32).T.astype(BF16)
            return carry
        lax.fori_loop(0, v_ref.shape[0] // blk, tr, 0)

    q = q_ref[...].astype(F32) * (DK_A ** -0.5 * LOG2E)
    lane = lax.broadcasted_iota(jnp.int32, q.shape, 1)
    qs_ref[0:blk, :] = jnp.where(lane < DK_A, q, 0.0).astype(BF16)
    qs_ref[blk:nq, :] = jnp.where(lane >= DK_A, q, 0.0).astype(BF16)
    m_ref[...] = jnp.full(m_ref.shape, NEG_INF, F32)
    l_ref[...] = jnp.zeros(l_ref.shape, F32)
    acc_ref[...] = jnp.zeros(acc_ref.shape, F32)

    def scores(b, s_ref):
        k = k_ref[pl.ds(pl.multiple_of(b * blk, blk), blk), :]
        s_ref[...] = _dot_nt(k, qs_ref[...])

    def softmax_pv(b, s_ref, bias):
        vt = vt_ref[:, pl.ds(pl.multiple_of(b * blk, blk), blk)]
        s = s_ref[...]
        if bias is not None:
            s = s + jnp.concatenate([bias, bias], axis=1)
        s = s.reshape(blk // sub, sub, nq)
        m_prev = m_ref[...]
        m_cur = jnp.max(jnp.max(s, axis=0), axis=0, keepdims=True)
        m_new = jnp.maximum(m_prev, m_cur)
        alpha = jnp.exp2(m_prev - m_new)
        p = jnp.exp2(s - m_new[None])
        l_ref[...] = alpha * l_ref[...] + jnp.sum(p, axis=0)
        pv = _dot(vt, p.reshape(blk, nq).astype(BF16))
        acc_ref[...] = acc_ref[...] * alpha[0:1] + pv
        m_ref[...] = m_new

    nfar = jnp.maximum(i - 1, 0)
    odd = lax.rem(nfar, 2)

    @pl.when(i == 0)
    def _():
        scores(0, sb_ref)

    @pl.when(i > 0)
    def _():
        @pl.when(odd == 1)
        def _():
            scores(0, sb_ref)
            scores(1, sa_ref)
            softmax_pv(0, sb_ref, None)

        @pl.when(odd == 0)
        def _():
            scores(0, sa_ref)

        def pair_body(t, carry):
            b = odd + 2 * t
            scores(b + 1, sb_ref)
            softmax_pv(b, sa_ref, None)
            scores(b + 2, sa_ref)
            softmax_pv(b + 1, sb_ref, None)
            return carry

        lax.fori_loop(0, nfar // 2, pair_body, 0)
        scores(i, sb_ref)
        softmax_pv(i - 1, sa_ref, bias_ref[0, 0])

    softmax_pv(i, sb_ref, bias_ref[0, 1])

    ot = acc_ref[...] / jnp.sum(l_ref[...], axis=0, keepdims=True)
    o = ot[:, 0:blk].T - lam_ref[...] * ot[:, blk:nq].T
    o = o * lax.rsqrt(jnp.mean(o * o, axis=-1, keepdims=True) + EPS) * g_ref[...]
    o_ref[...] = (o * out_scale).astype(o_ref.dtype)


_TOEPLITZ_ROWS = 512
_TOEPLITZ_N = 2048


def _toeplitz_kernel(v_ref, o_ref, *, keep):
    rows, cols = o_ref.shape[1:]
    x = jnp.broadcast_to(v_ref[0, 0], (rows, v_ref.shape[-1]))
    tile = pltpu.roll(x, 0, 1, stride=1, stride_axis=0)[:, :cols]
    r = lax.broadcasted_iota(jnp.int32, (rows, cols), 0) + pl.program_id(1) * rows
    c = lax.broadcasted_iota(jnp.int32, (rows, cols), 1)
    o_ref[0] = jnp.where(keep(r, c), tile, NEG_INF)


def _toeplitz_tiles(fn, keep, heads, rows, cols):
    n, rb = _TOEPLITZ_N, _TOEPLITZ_ROWS
    assert rows % rb == 0 and rows <= n // 2 and cols <= n // 2
    idx = jnp.arange(n, dtype=jnp.int32)
    vec = fn(jnp.where(idx < n // 2, idx, idx - n)).astype(F32)
    vecs = jnp.stack([jnp.roll(vec, k * rb, axis=1) for k in range(rows // rb)], axis=1)
    return pl.pallas_call(
        functools.partial(_toeplitz_kernel, keep=keep),
        grid=(heads, rows // rb),
        in_specs=[pl.BlockSpec((1, 1, 1, n), lambda h, k: (h, k, 0, 0))],
        out_specs=pl.BlockSpec((1, rb, cols), lambda h, k: (h, k, 0)),
        out_shape=jax.ShapeDtypeStruct((heads, rows, cols), F32),
        name="toeplitz_tiles",
    )(vecs.reshape(heads, rows // rb, 1, n))


def _t5_bucket(rel):
    nb = NUM_BUCKETS // 2
    max_exact = nb // 2
    bucket = jnp.where(rel > 0, nb, 0)
    n = jnp.abs(rel)
    nf = jnp.maximum(n, 1).astype(F32)
    large = max_exact + (jnp.log(nf / max_exact) / math.log(MAX_DISTANCE / max_exact)
                         * (nb - max_exact)).astype(jnp.int32)
    large = jnp.minimum(large, nb - 1)
    return bucket + jnp.where(n < max_exact, n, large)


def _diff_bias_tiles(t5_table):
    blk = BLK_A
    table = t5_table.astype(F32)
    far = table[_t5_bucket(jnp.full((), -(blk + 1), jnp.int32))]
    def visible(r, c):
        return jnp.floor_divide(r - blk, CHUNK) <= jnp.floor_divide(c, CHUNK)

    tiles = _toeplitz_tiles(lambda x: ((table[_t5_bucket(-x - blk)] - far) * LOG2E).T, visible,
                            N_HEADS_A, 2 * blk, blk)
    return tiles.reshape(N_HEADS_A, 2, blk, blk)


def _diff_attention(proj, t5_table, lam, subln_g, lam_init):
    seq = proj.shape[0]
    blk = BLK_A
    bias = _diff_bias_tiles(t5_table)
    ha = N_HEADS_A
    kern = functools.partial(_diffattn_kernel, out_scale=1.0 - lam_init)
    return pl.pallas_call(
        kern,
        grid=(ha, seq // blk),
        in_specs=[
            pl.BlockSpec((blk, DV_A), lambda h, i: (i, h)),
            pl.BlockSpec((seq, DV_A), lambda h, i: (0, ha + h)),
            pl.BlockSpec((seq, DV_A), lambda h, i: (0, 2 * ha + h)),
            pl.BlockSpec((1, 2, blk, blk), lambda h, i: (h, 0, 0, 0)),
            pl.BlockSpec((1, DV_A), lambda h, i: (0, 0)),
            pl.BlockSpec((1, DV_A), lambda h, i: (0, 0)),
        ],
        out_specs=pl.BlockSpec((blk, DV_A), lambda h, i: (i, h)),
        out_shape=jax.ShapeDtypeStruct((seq, ha * DV_A), BF16),
        scratch_shapes=[
            pltpu.VMEM((2 * blk, DV_A), BF16),
            pltpu.VMEM((DV_A, seq), BF16),
            pltpu.VMEM((blk, 2 * blk), F32),
            pltpu.VMEM((blk, 2 * blk), F32),
            pltpu.VMEM((8, 2 * blk), F32),
            pltpu.VMEM((8, 2 * blk), F32),
            pltpu.VMEM((DV_A, 2 * blk), F32),
        ],
        compiler_params=pltpu.CompilerParams(dimension_semantics=("parallel", "arbitrary")),
        name="diff_attention",
    )(proj, proj, proj, bias, jnp.full((1, DV_A), lam, F32), subln_g.reshape(1, DV_A).astype(F32))


def _band_kernel(q_ref, kp_ref, kc_ref, vp_ref, vc_ref, bias_ref, o_ref):
    blk = BLK_B
    i = pl.program_id(1)
    q = q_ref[...] * (DH_B ** -0.5)
    lane = lax.broadcasted_iota(jnp.int32, q.shape, 1)
    kp, kc, vp, vc = kp_ref[...], kc_ref[...], vp_ref[...], vc_ref[...]
    no_prev = jnp.where(i == 0, NEG_INF, 0.0).astype(F32)
    outs = []
    for hh in range(2):
        sel = (lane < DH_B) if hh == 0 else (lane >= DH_B)
        qm = jnp.where(sel, q, 0).astype(BF16)
        sp = _dot_nt(qm, kp) + bias_ref[hh, :, 0:blk] + no_prev
        sc = _dot_nt(qm, kc) + bias_ref[hh, :, blk:2 * blk]
        m = jnp.maximum(jnp.max(sp, axis=-1, keepdims=True), jnp.max(sc, axis=-1, keepdims=True))
        pp = jnp.exp(sp - m)
        pc = jnp.exp(sc - m)
        l = jnp.sum(pp, axis=-1, keepdims=True) + jnp.sum(pc, axis=-1, keepdims=True)
        outs.append((_dot(pp.astype(BF16), vp) + _dot(pc.astype(BF16), vc)) / l)
    o_ref[...] = jnp.where(lane < DH_B, outs[0], outs[1]).astype(o_ref.dtype)


def _band_bias_tiles(rel_bias):
    blk = BLK_B

    def valid(r, c):
        qchunk = jnp.floor_divide(r, CHUNK)
        kchunk = jnp.floor_divide(c - blk, CHUNK)
        return (kchunk <= qchunk) & (kchunk >= qchunk - LEFT_CHUNKS)

    return _toeplitz_tiles(
        lambda x: rel_bias.astype(F32)[:, jnp.clip(x - blk, -REL_CLIP, REL_CLIP) + REL_CLIP], valid,
        N_HEADS_B, blk, 2 * blk)


def _band_attention(proj, rel_bias):
    seq = proj.shape[0]
    blk = BLK_B
    bias = _band_bias_tiles(rel_bias)
    npair = N_HEADS_B // 2
    qc0 = 3 * N_HEADS_A
    prev = lambda c0: (lambda hp, i: (jnp.maximum(i - 1, 0), c0 + hp))
    cur = lambda c0: (lambda hp, i: (i, c0 + hp))
    return pl.pallas_call(
        _band_kernel,
        grid=(npair, seq // blk),
        in_specs=[
            pl.BlockSpec((blk, LANES), cur(qc0)),
            pl.BlockSpec((blk, LANES), prev(qc0 + npair)),
            pl.BlockSpec((blk, LANES), cur(qc0 + npair)),
            pl.BlockSpec((blk, LANES), prev(qc0 + 2 * npair)),
            pl.BlockSpec((blk, LANES), cur(qc0 + 2 * npair)),
            pl.BlockSpec((2, blk, 2 * blk), lambda hp, i: (hp, 0, 0)),
        ],
        out_specs=pl.BlockSpec((blk, LANES), lambda hp, i: (i, hp)),
        out_shape=jax.ShapeDtypeStruct((seq, N_HEADS_B * DH_B), BF16),
        compiler_params=pltpu.CompilerParams(dimension_semantics=("parallel", "arbitrary")),
        name="band_attention",
    )(proj, proj, proj, proj, proj, bias)


def _retention_kernel(qk_ref, v_ref, gate_ref, cos_ref, sin_ref, qdec_ref, kdec_ref, dmat_ref,
                      sdec_ref, o_ref, state_ref):
    @pl.when(pl.program_id(0) == 0)
    def _():
        state_ref[...] = jnp.zeros(state_ref.shape, F32)

    cos = cos_ref[...]
    sin = sin_ref[...]
    lane = lax.broadcasted_iota(jnp.int32, cos.shape, 1)
    first_half = (lane % DQK_C) < (DQK_C // 2)
    qk = qk_ref[...]
    parts = []
    for j in range(qk.shape[1] // LANES):
        t = qk[:, j * LANES:(j + 1) * LANES]
        partner = jnp.where(first_half, pltpu.roll(t, LANES - DQK_C // 2, 1), pltpu.roll(t, DQK_C // 2, 1))
        parts.append(t * cos + partner * sin)
    wq = N_HEADS_C * DQK_C
    q = jnp.concatenate(parts[:wq // LANES], axis=1)
    k = jnp.concatenate(parts[wq // LANES:], axis=1) * (DQK_C ** -0.5)
    qd = (q * qdec_ref[...]).astype(BF16)
    kd = (k * kdec_ref[...]).astype(BF16)
    qb = q.astype(BF16)
    kb = k.astype(BF16)
    vb = v_ref[...].astype(BF16)
    gate = gate_ref[...]
    outs = []
    for h in range(N_HEADS_C):
        qs = slice(h * DQK_C, (h + 1) * DQK_C)
        vs = slice(h * DV_C, (h + 1) * DV_C)
        scores = _dot_nt(qb[:, qs], kb[:, qs]) * dmat_ref[h]
        state = state_ref[h]
        r = _dot(scores.astype(BF16), vb[:, vs]) + _dot(qd[:, qs], state.astype(BF16))
        state_ref[h] = state * sdec_ref[h] + _dot_tn(kd[:, qs], vb[:, vs])
        r = r * lax.rsqrt(jnp.mean(r * r, axis=-1, keepdims=True) + EPS)
        g = gate[:, vs]
        outs.append(r * (g * jax.nn.sigmoid(g)))
    o_ref[...] = jnp.concatenate(outs, axis=1).astype(o_ref.dtype)


def _retention_tables(seq):
    t = BLK_C
    half = DQK_C // 2
    inv_freq = 1.0 / (ROPE_BASE ** (jnp.arange(0, DQK_C, 2, dtype=F32) / DQK_C))
    ang = jnp.arange(seq, dtype=F32)[:, None] * inv_freq[None, :]
    reps = LANES // half
    cos = jnp.tile(jnp.cos(ang), (1, reps))
    sign = jnp.where((jnp.arange(LANES) % DQK_C) < half, -1.0, 1.0).astype(F32)
    sin = jnp.tile(jnp.sin(ang), (1, reps)) * sign[None, :]
    log_g = jnp.log(1.0 - jnp.power(2.0, -5.0 - jnp.arange(N_HEADS_C, dtype=F32)))
    pos = jnp.arange(t, dtype=F32)
    diff = pos[:, None] - pos[None, :]
    same_or_past = (jnp.arange(t)[None, :] // CHUNK) <= (jnp.arange(t)[:, None] // CHUNK)
    dmat = jnp.where(same_or_past[None], jnp.exp(log_g[:, None, None] * jnp.abs(diff)[None]), 0.0)
    qdec = jnp.repeat(jnp.exp(log_g[None, :] * (pos[:, None] + 1.0)), DQK_C, axis=1)
    kdec = jnp.repeat(jnp.exp(log_g[None, :] * (t - 1.0 - pos[:, None])), DQK_C, axis=1)
    sdec = jnp.broadcast_to(jnp.exp(log_g * t)[:, None, None], (N_HEADS_C, 1, DV_C))
    return cos, sin, qdec, kdec, dmat, sdec


def _retention(proj):
    seq = proj.shape[0]
    t = BLK_C
    cos, sin, qdec, kdec, dmat, sdec = _retention_tables(seq)
    wv = N_HEADS_C * DV_C
    return pl.pallas_call(
        _retention_kernel,
        grid=(seq // t,),
        in_specs=[
            pl.BlockSpec((t, wv), lambda i: (i, 0)),
            pl.BlockSpec((t, wv), lambda i: (i, 1)),
            pl.BlockSpec((t, wv), lambda i: (i, 2)),
            pl.BlockSpec((t, LANES), lambda i: (i, 0)),
            pl.BlockSpec((t, LANES), lambda i: (i, 0)),
            pl.BlockSpec((t, N_HEADS_C * DQK_C), lambda i: (0, 0)),
            pl.BlockSpec((t, N_HEADS_C * DQK_C), lambda i: (0, 0)),
            pl.BlockSpec((N_HEADS_C, t, t), lambda i: (0, 0, 0)),
            pl.BlockSpec((N_HEADS_C, 1, DV_C), lambda i: (0, 0, 0)),
        ],
        out_specs=pl.BlockSpec((t, wv), lambda i: (i, 0)),
        out_shape=jax.ShapeDtypeStruct((seq, wv), BF16),
        scratch_shapes=[pltpu.VMEM((N_HEADS_C, DQK_C, DV_C), F32)],
        compiler_params=pltpu.CompilerParams(dimension_semantics=("arbitrary",)),
        name="retention",
    )(proj, proj, proj, cos, sin, qdec, kdec, dmat, sdec)


def _s5_kernel(*refs):
    ncb = S5_CH // LANES
    u_refs = refs[:ncb]
    (mt_ref, bt_ref, ct_ref, are_ref, aim_ref, y_ref,
     ut_ref, yt_ref, ys_ref, vr_ref, vi_ref, spr_ref, spi_ref, carry_ref) = refs[ncb:]
    tc = S5_TC
    gp = S5_GROUP
    n = S5_STATE

    @pl.when(pl.program_id(0) == 0)
    def _():
        carry_ref[...] = jnp.zeros(carry_ref.shape, F32)

    for s in range(S5_T):
        for k in range(ncb):
            ut_ref[s, k * LANES:(k + 1) * LANES, :] = u_refs[k][pl.ds(s, tc, stride=S5_T), :].T

    def intra(g, carry):
        r0 = pl.multiple_of(g * gp, gp)
        ug = ut_ref[:, pl.ds(r0, gp), :].reshape(S5_T * gp, tc).astype(BF16)
        yt_ref[:, pl.ds(r0, gp), :] = _dot(mt_ref[g], ug).reshape(S5_T, gp, tc)
        vt = _dot(bt_ref[g], ug)
        n0 = pl.multiple_of(g * n, n)
        vr_ref[pl.ds(n0, n), :] = vt[0:n]
        vi_ref[pl.ds(n0, n), :] = vt[n:2 * n]
        return carry

    lax.fori_loop(0, S5_GROUPS, intra, 0)

    ar, ai = are_ref[...], aim_ref[...]
    cr, ci = carry_ref[0], carry_ref[1]
    xr, xi = vr_ref[...], vi_ref[...]
    lane = lax.broadcasted_iota(jnp.int32, xr.shape, 1)
    first = lane == 0
    xr = xr + jnp.where(first, ar * cr - ai * ci, 0.0)
    xi = xi + jnp.where(first, ar * ci + ai * cr, 0.0)
    pr, pi = ar, ai
    d = 1
    while d < tc:
        sr = jnp.where(lane >= d, pltpu.roll(xr, d, 1), 0.0)
        si = jnp.where(lane >= d, pltpu.roll(xi, d, 1), 0.0)
        xr, xi = xr + (pr * sr - pi * si), xi + (pr * si + pi * sr)
        pr, pi = pr * pr - pi * pi, 2.0 * pr * pi
        d *= 2
    spr_ref[...] = jnp.where(first, cr, pltpu.roll(xr, 1, 1))
    spi_ref[...] = jnp.where(first, ci, pltpu.roll(xi, 1, 1))
    carry_ref[0] = jnp.broadcast_to(xr[:, tc - 1:tc], xr.shape)
    carry_ref[1] = jnp.broadcast_to(xi[:, tc - 1:tc], xi.shape)

    def cross(g, carry):
        r0 = pl.multiple_of(g * gp, gp)
        n0 = pl.multiple_of(g * n, n)
        sp = jnp.concatenate([spr_ref[pl.ds(n0, n), :], spi_ref[pl.ds(n0, n), :]], axis=0).astype(BF16)
        yt_ref[:, pl.ds(r0, gp), :] += _dot(ct_ref[g], sp).reshape(S5_T, gp, tc)
        return carry

    lax.fori_loop(0, S5_GROUPS, cross, 0)

    for s in range(S5_T):
        for k in range(ncb):
            ys_ref[k, pl.ds(s, tc, stride=S5_T), :] = yt_ref[s, k * LANES:(k + 1) * LANES, :].T
    for k in range(ncb):
        y_ref[:, k * LANES:(k + 1) * LANES] = ys_ref[k]


def _s5_matrices(lam_re, lam_im, log_step, b_re, b_im, c_re, c_im, d_skip):
    hi = lax.Precision.HIGHEST
    t, gp, n, ng = S5_T, S5_GROUP, S5_STATE, S5_GROUPS
    lam = lax.complex(lam_re.astype(F32), lam_im.astype(F32))
    step = jnp.exp(log_step.astype(F32))[:, None]
    ls = lam * step
    a_bar = jnp.exp(ls)
    b_bar = ((a_bar - 1.0) / lam)[..., None] * lax.complex(b_re.astype(F32), b_im.astype(F32))
    cm = lax.complex(c_re.astype(F32), c_im.astype(F32))

    def apow(k):
        kk = k.astype(F32).astype(jnp.complex64)
        return jnp.exp(ls.reshape((ng,) + (1,) * k.ndim + (n,)) * kk[None, ..., None])

    tt = jnp.arange(t)
    kmat = jnp.einsum('gpn,gln,gnq->glpq', cm, apow(tt), b_bar, precision=hi).real
    krev = jnp.transpose(kmat[:, ::-1], (0, 2, 1, 3)).reshape(ng, gp, t * gp)
    kpad = jnp.pad(krev, ((0, 0), (0, 0), (0, t * gp)))
    mt = jnp.concatenate([kpad[:, :, (t - 1 - to) * gp:(2 * t - 1 - to) * gp] for to in range(t)], axis=1)
    dvec = jnp.tile(d_skip.astype(F32).reshape(ng, 1, gp), (1, t, 1)).reshape(ng, t * gp)
    mt = mt + jnp.eye(t * gp, dtype=F32)[None] * dvec[:, :, None]
    z = jnp.swapaxes(apow(t - 1 - tt), 1, 2)[:, :, :, None] * b_bar[:, :, None, :]
    z = z.reshape(ng, n, t * gp)
    bt = jnp.concatenate([z.real, z.imag], axis=1)
    w = cm[:, None, :, :] * apow(tt + 1)[:, :, None, :]
    w = w.reshape(ng, t * gp, n)
    ct = jnp.concatenate([w.real, -w.imag], axis=2)
    a_chunk = apow(jnp.full((), t)).reshape(ng * n, 1)
    are = jnp.broadcast_to(a_chunk.real, (ng * n, LANES))
    aim = jnp.broadcast_to(a_chunk.imag, (ng * n, LANES))
    return mt.astype(BF16), bt.astype(BF16), ct.astype(BF16), are, aim


def _s5(proj, mats):
    seq, width = proj.shape
    t, tc, gp, n, ng = S5_T, S5_TC, S5_GROUP, S5_STATE, S5_GROUPS
    rows = t * tc
    mt, bt, ct, are, aim = mats
    ncb = S5_CH // LANES
    cb0 = (width - S5_CH) // LANES
    u_specs = [pl.BlockSpec((rows, LANES), (lambda i, k=k: (i, cb0 + k))) for k in range(ncb)]
    return pl.pallas_call(
        _s5_kernel,
        grid=(seq // rows,),
        in_specs=u_specs + [
            _const_spec(mt.shape), _const_spec(bt.shape), _const_spec(ct.shape),
            _const_spec(are.shape), _const_spec(aim.shape),
        ],
        out_specs=pl.BlockSpec((rows, S5_CH), lambda i: (i, 0)),
        out_shape=jax.ShapeDtypeStruct((seq, S5_CH), F32),
        scratch_shapes=[
            pltpu.VMEM((t, S5_CH, tc), F32),
            pltpu.VMEM((t, S5_CH, tc), F32),
            pltpu.VMEM((ncb, rows, LANES), F32),
            pltpu.VMEM((ng * n, tc), F32),
            pltpu.VMEM((ng * n, tc), F32),
            pltpu.VMEM((ng * n, tc), F32),
            pltpu.VMEM((ng * n, tc), F32),
            pltpu.VMEM((2, ng * n, tc), F32),
        ],
        compiler_params=pltpu.CompilerParams(dimension_semantics=("arbitrary",)),
        name="s5_scan",
    )(*([proj] * ncb), mt, bt, ct, are, aim)


def _mix_ffn_kernel(*refs, glu, final):
    (x_ref, a_ref, b_ref, wo_ref, g1_ref), refs = refs[:5], refs[5:]
    if glu:
        gw_ref, refs = refs[0], refs[1:]
    (g_ref, sc_ref, sh_ref, gate_ref, win_ref, cw_ref, cb_ref, wout_ref, fg_ref,
     o_ref, h_ref, act_ref, gbuf_ref, carry_ref) = refs
    tm = x_ref.shape[0]
    halo = gbuf_ref.shape[0] - tm

    @pl.when(pl.program_id(0) == 0)
    def _():
        carry_ref[...] = jnp.zeros(carry_ref.shape, F32)

    if glu:
        y = jax.nn.gelu(b_ref[...]).astype(BF16)
        gg = _dot(y, gw_ref[...])
        half = gg.shape[1] // 2
        b = (gg[:, :half] * jax.nn.sigmoid(gg[:, half:])).astype(BF16)
    else:
        b = b_ref[...]
    cat = jnp.concatenate([a_ref[...], b], axis=1)
    x = x_ref[...] + g1_ref[...] * _dot(cat, wo_ref[...])
    h_ref[...] = _mod_rmsnorm(x, g_ref[...], sc_ref[...], sh_ref[...]).astype(BF16)
    for f in range(D_FF // TF_FFN):
        cs = slice(f * TF_FFN, (f + 1) * TF_FFN)
        gs = slice(D_FF + f * TF_FFN, D_FF + (f + 1) * TF_FFN)
        h = h_ref[...]
        val = _dot(h, win_ref[:, cs])
        gate = _dot(h, win_ref[:, gs])
        gbuf_ref[0:halo, :] = carry_ref[:, cs]
        gbuf_ref[halo:halo + tm, :] = gate
        carry_ref[:, cs] = gate[tm - halo:tm, :]
        conv = (gate * cw_ref[2:3, cs] + gbuf_ref[halo - 1:halo - 1 + tm, :] * cw_ref[1:2, cs]
                + gbuf_ref[halo - 2:halo - 2 + tm, :] * cw_ref[0:1, cs] + cb_ref[:, cs])
        act_ref[:, cs] = (jax.nn.gelu(conv) * val).astype(BF16)
    xn = x + gate_ref[...] * _dot(act_ref[...], wout_ref[...])
    if final:
        xn = xn * lax.rsqrt(jnp.mean(xn * xn, axis=-1, keepdims=True) + EPS) * fg_ref[...]
    o_ref[...] = xn


def _layer_spec(shape, layer):
    idx = (layer,) + (0,) * (len(shape) - 1)
    return pl.BlockSpec((None,) + tuple(shape[1:]), lambda *_: idx, pipeline_mode=pl.Buffered(1))


def _mix_ffn(x, a, b, wo, gate1, glu_w, g, scale, shift, gate2, w_in, conv_w, conv_b, w_out, final_g,
             layer, final):
    seq, d = x.shape
    tm = TM_FFN
    halo = 8
    row = pl.BlockSpec((1, d), lambda i: (0, 0))
    rows = lambda w: pl.BlockSpec((tm, w), lambda i: (i, 0))
    conv_b = conv_b.reshape(conv_b.shape[0], 1, D_FF)
    in_specs = [rows(d), rows(a.shape[1]), rows(b.shape[1]), _const_spec(wo.shape), row]
    args = [x, a, b, wo, gate1]
    if glu_w is not None:
        in_specs.append(_const_spec(glu_w.shape))
        args.append(glu_w)
    in_specs += [
        row, row, row, row,
        _layer_spec(w_in.shape, layer),
        _layer_spec(conv_w.shape, layer),
        _layer_spec(conv_b.shape, layer),
        _layer_spec(w_out.shape, layer),
        row,
    ]
    args += [g.reshape(1, d), scale, shift, gate2, w_in, conv_w, conv_b, w_out, final_g.reshape(1, d)]
    return pl.pallas_call(
        functools.partial(_mix_ffn_kernel, glu=glu_w is not None, final=final),
        grid=(seq // tm,),
        in_specs=in_specs,
        out_specs=pl.BlockSpec((tm, d), lambda i: (i, 0)),
        out_shape=jax.ShapeDtypeStruct((seq, d), F32),
        scratch_shapes=[
            pltpu.VMEM((tm, d), BF16),
            pltpu.VMEM((tm, D_FF), BF16),
            pltpu.VMEM((tm + halo, TF_FFN), F32),
            pltpu.VMEM((halo, D_FF), F32),
        ],
        compiler_params=pltpu.CompilerParams(dimension_semantics=("arbitrary",)),
        name="mix_ffn",
    )(*args)


def kernel(x, c, t5_table, mod_w, mod_b, norm1_g, norm2_g, ffn_w_in, ffn_conv_w, ffn_conv_b, ffn_w_out,
           ev_w_in, ev_w_out, diff_lambda, diff_subln_g, band_rel_bias,
           od_w_in, od_w_out, s5_lam_re, s5_lam_im, s5_log_step, s5_b_re, s5_b_im, s5_c_re, s5_c_im,
           s5_d, s5_glu_w, final_g):
    assert x.shape[0] == 1 and x.shape[2] == D_MODEL
    seq = x.shape[1]
    assert seq % TM_PROJ == 0 and seq % (S5_T * S5_TC) == 0
    d = D_MODEL
    xs = x[0]
    mod = _modulation(c, mod_w, mod_b)
    ffn_w_in_b = ffn_w_in.astype(BF16)
    ffn_w_out_b = ffn_w_out.astype(BF16)
    for i in range(DEPTH):
        sh1, sc1, g1, sh2, sc2, g2 = [mod[i, :, k * d:(k + 1) * d] for k in range(6)]
        if i % 2 == 0:
            e = i // 2
            lam_init = 0.8 - 0.6 * math.exp(-0.3 * i)
            lp = diff_lambda[e].astype(F32)
            lam = jnp.exp(jnp.sum(lp[0] * lp[1])) - jnp.exp(jnp.sum(lp[2] * lp[3])) + lam_init
            proj = _normproj(xs, norm1_g[i], sc1, sh1, ev_w_in[e].astype(BF16), BF16)
            mix_a = _diff_attention(proj, t5_table, lam, diff_subln_g[e], lam_init)
            mix_b = _band_attention(proj, band_rel_bias[e])
            wo, glu_w = ev_w_out[e].astype(BF16), None
        else:
            o = i // 2
            proj = _normproj(xs, norm1_g[i], sc1, sh1, od_w_in[o].astype(BF16), F32)
            mix_a = _retention(proj)
            mats = _s5_matrices(s5_lam_re[o], s5_lam_im[o], s5_log_step[o], s5_b_re[o], s5_b_im[o],
                                s5_c_re[o], s5_c_im[o], s5_d[o])
            mix_b = _s5(proj, mats)
            wo, glu_w = od_w_out[o].astype(BF16), s5_glu_w[o].astype(BF16)
        xs = _mix_ffn(xs, mix_a, mix_b, wo, g1, glu_w, norm2_g[i], sc2, sh2, g2,
                      ffn_w_in_b, ffn_conv_w, ffn_conv_b, ffn_w_out_b, final_g,
                      layer=i, final=(i == DEPTH - 1))
    return xs[None]
```

```python
import functools
import math

import jax
import jax.numpy as jnp
from jax import lax
from jax.experimental import pallas as pl
from jax.experimental.pallas import tpu as pltpu

F32 = jnp.float32
BF16 = jnp.bfloat16

D_MODEL = 1024
DEPTH = 2
CHUNK = 64
GROUP_WIDTH = D_MODEL // 2
DK_A = 64
DV_A = 2 * DK_A
N_HEADS_A = GROUP_WIDTH // DV_A
DH_B = 64
N_HEADS_B = GROUP_WIDTH // DH_B
LEFT_CHUNKS = 8
REL_CLIP = 2 * CHUNK
NUM_BUCKETS = 32
MAX_DISTANCE = 128
DV_C = 128
DQK_C = DV_C // 2
N_HEADS_C = GROUP_WIDTH // DV_C
ROPE_BASE = 10000.0
S5_CH = GROUP_WIDTH
S5_GROUP = 16
S5_GROUPS = S5_CH // S5_GROUP
S5_STATE = 64
D_FF = ((8 * D_MODEL // 3 + 255) // 256) * 256
CONV_W = 3
EVEN_IN = 3 * N_HEADS_A * DV_A + 3 * N_HEADS_B * DH_B
ODD_IN = 2 * N_HEADS_C * DQK_C + 2 * N_HEADS_C * DV_C + S5_CH
EPS = 1e-6
NEG_INF = -1e30
LOG2E = math.log2(math.e)

LANES = 128
MXU_DIM = 256

TM_PROJ = 1024
TM_FFN = 512
TF_FFN = MXU_DIM
BLK_A = 512
NPART_A = 2
BLK_B = 512
BLK_C = 256
S5_T = 16
S5_TC = LANES

assert BLK_B == LEFT_CHUNKS * CHUNK, "band window must be exactly one previous block"
assert BLK_A >= MAX_DISTANCE, "far key blocks must sit in the saturated T5 bucket"
assert DV_A == LANES, "diff-attention statistics are kept lane-replicated beside the accumulator"


def _dot(a, b):
    return jnp.dot(a, b, preferred_element_type=F32)


def _dot_nt(a, b):
    return lax.dot_general(a, b, (((1,), (1,)), ((), ())), preferred_element_type=F32)


def _dot_tn(a, b):
    return lax.dot_general(a, b, (((0,), (0,)), ((), ())), preferred_element_type=F32)


def _const_spec(shape):
    zeros = (0,) * len(shape)
    return pl.BlockSpec(shape, lambda *_: zeros, pipeline_mode=pl.Buffered(1))


def _mod_rmsnorm(x, g, scale, shift):
    y = x * lax.rsqrt(jnp.mean(x * x, axis=-1, keepdims=True) + EPS)
    y = y * g
    return y * (1.0 + scale) + shift


def _mod_kernel(c_ref, w_ref, b_ref, o_ref):
    c = c_ref[...]
    cond = c * jax.nn.sigmoid(c)
    o_ref[0] = jnp.sum(cond * w_ref[0], axis=0, keepdims=True) + b_ref[0]


def _modulation(c, mod_w, mod_b):
    depth, d, n = mod_w.shape
    tn = 1536
    return pl.pallas_call(
        _mod_kernel,
        grid=(depth, n // tn),
        in_specs=[
            pl.BlockSpec((d, 1), lambda i, j: (0, 0)),
            pl.BlockSpec((1, d, tn), lambda i, j: (i, 0, j)),
            pl.BlockSpec((1, 1, tn), lambda i, j: (i, 0, j)),
        ],
        out_specs=pl.BlockSpec((1, 1, tn), lambda i, j: (i, 0, j)),
        out_shape=jax.ShapeDtypeStruct((depth, 1, n), F32),
        name="modulation",
    )(c.reshape(d, 1), mod_w, mod_b.reshape(depth, 1, n))


def _normproj_kernel(x_ref, g_ref, sc_ref, sh_ref, w_ref, o_ref, h_ref):
    @pl.when(pl.program_id(1) == 0)
    def _():
        h_ref[...] = _mod_rmsnorm(x_ref[...], g_ref[...], sc_ref[...], sh_ref[...]).astype(BF16)

    o_ref[...] = _dot(h_ref[...], w_ref[...]).astype(o_ref.dtype)


def _normproj(x, g, scale, shift, w, out_dtype):
    seq, d = x.shape
    n = w.shape[1]
    tm, tn = TM_PROJ, 1024
    row = pl.BlockSpec((1, d), lambda i, j: (0, 0))
    return pl.pallas_call(
        _normproj_kernel,
        grid=(seq // tm, n // tn),
        in_specs=[
            pl.BlockSpec((tm, d), lambda i, j: (i, 0)),
            row, row, row,
            pl.BlockSpec((d, tn), lambda i, j: (0, j)),
        ],
        out_specs=pl.BlockSpec((tm, tn), lambda i, j: (i, j)),
        out_shape=jax.ShapeDtypeStruct((seq, n), out_dtype),
        scratch_shapes=[pltpu.VMEM((tm, d), BF16)],
        compiler_params=pltpu.CompilerParams(dimension_semantics=("parallel", "arbitrary")),
        name="normproj",
    )(x, g.reshape(1, d), scale, shift, w)


def _diffattn_kernel(q_ref, k_ref, v_ref, bias_ref, lam_ref, g_ref, o_ref,
                     qs_ref, vt_ref, m_ref, l_ref, acc_ref, *s_refs, out_scale):
    blk = BLK_A
    nq = 2 * blk
    sub = 8
    npart = len(s_refs) // 2
    wq = nq // npart
    sa_ref, sb_ref = s_refs[:npart], s_refs[npart:]
    i = pl.program_id(1)

    @pl.when(i == 0)
    def _():
        def tr(b, carry):
            r0 = pl.multiple_of(b * blk, blk)
            vt_ref[:, pl.ds(r0, blk)] = v_ref[pl.ds(r0, blk), :].astype(F32).T.astype(BF16)
            return carry
        lax.fori_loop(0, v_ref.shape[0] // blk, tr, 0)

    q = q_ref[...].astype(F32) * (DK_A ** -0.5 * LOG2E)
    lane = lax.broadcasted_iota(jnp.int32, q.shape, 1)
    qs_ref[0:blk, :] = jnp.where(lane < DK_A, q, 0.0).astype(BF16)
    qs_ref[blk:nq, :] = jnp.where(lane >= DK_A, q, 0.0).astype(BF16)
    m_ref[...] = jnp.full(m_ref.shape, NEG_INF, F32)
    l_ref[...] = jnp.zeros(l_ref.shape, F32)
    acc_ref[...] = jnp.zeros(acc_ref.shape, F32)

    def scores(b, s_ref):
        k = k_ref[pl.ds(pl.multiple_of(b * blk, blk), blk), :]
        for part in range(npart):
            s_ref[part][...] = _dot_nt(k, qs_ref[part * wq:(part + 1) * wq, :])

    def softmax_pv(b, s_ref, bias):
        vt = vt_ref[:, pl.ds(pl.multiple_of(b * blk, blk), blk)]
        for part in range(npart):
            cols = slice(part * wq, (part + 1) * wq)
            s = s_ref[part][...]
            if bias is not None:
                q0 = (part * wq) % blk
                s = s + bias[:, q0:q0 + wq]
            s = s.reshape(blk // sub, sub, wq)
            m_prev = m_ref[:, cols]
            m_cur = jnp.max(jnp.max(s, axis=0), axis=0, keepdims=True)
            m_new = jnp.maximum(m_prev, m_cur)
            alpha = jnp.exp2(m_prev - m_new)
            p = jnp.exp2(s - m_new[None])
            l_ref[:, cols] = alpha * l_ref[:, cols] + jnp.sum(p, axis=0)
            pv = _dot(vt, p.reshape(blk, wq).astype(BF16))
            acc_ref[:, cols] = acc_ref[:, cols] * alpha[0:1] + pv
            m_ref[:, cols] = m_new

    nfar = jnp.maximum(i - 1, 0)
    odd = lax.rem(nfar, 2)

    @pl.when(i == 0)
    def _():
        scores(0, sb_ref)

    @pl.when(i > 0)
    def _():
        @pl.when(odd == 1)
        def _():
            scores(0, sb_ref)
            scores(1, sa_ref)
            softmax_pv(0, sb_ref, None)

        @pl.when(odd == 0)
        def _():
            scores(0, sa_ref)

        def pair(b):
            scores(b + 1, sb_ref)
            softmax_pv(b, sa_ref, None)
            scores(b + 2, sa_ref)
            softmax_pv(b + 1, sb_ref, None)

        def quad_body(t, carry):
            pair(odd + 4 * t)
            pair(odd + 4 * t + 2)
            return carry

        npairs = nfar // 2
        lax.fori_loop(0, npairs // 2, quad_body, 0)

        @pl.when(lax.rem(npairs, 2) == 1)
        def _():
            pair(odd + 2 * (npairs - 1))
        scores(i, sb_ref)
        softmax_pv(i - 1, sa_ref, bias_ref[0, 0])

    softmax_pv(i, sb_ref, bias_ref[0, 1])

    ot = acc_ref[...] / jnp.sum(l_ref[...], axis=0, keepdims=True)
    o = ot[:, 0:blk].T - lam_ref[...] * ot[:, blk:nq].T
    o = o * lax.rsqrt(jnp.mean(o * o, axis=-1, keepdims=True) + EPS) * g_ref[...]
    o_ref[...] = (o * out_scale).astype(o_ref.dtype)


_TOEPLITZ_ROWS = 512
_TOEPLITZ_N = 2048


def _toeplitz_kernel(v_ref, o_ref, *, keep):
    rows, cols = o_ref.shape[1:]
    x = jnp.broadcast_to(v_ref[0, 0], (rows, v_ref.shape[-1]))
    tile = pltpu.roll(x, 0, 1, stride=1, stride_axis=0)[:, :cols]
    r = lax.broadcasted_iota(jnp.int32, (rows, cols), 0) + pl.program_id(1) * rows
    c = lax.broadcasted_iota(jnp.int32, (rows, cols), 1)
    o_ref[0] = jnp.where(keep(r, c), tile, NEG_INF)


def _toeplitz_tiles(fn, keep, heads, rows, cols):
    n, rb = _TOEPLITZ_N, _TOEPLITZ_ROWS
    assert rows % rb == 0 and rows <= n // 2 and cols <= n // 2
    idx = jnp.arange(n, dtype=jnp.int32)
    vec = fn(jnp.where(idx < n // 2, idx, idx - n)).astype(F32)
    vecs = jnp.stack([jnp.roll(vec, k * rb, axis=1) for k in range(rows // rb)], axis=1)
    return pl.pallas_call(
        functools.partial(_toeplitz_kernel, keep=keep),
        grid=(heads, rows // rb),
        in_specs=[pl.BlockSpec((1, 1, 1, n), lambda h, k: (h, k, 0, 0))],
        out_specs=pl.BlockSpec((1, rb, cols), lambda h, k: (h, k, 0)),
        out_shape=jax.ShapeDtypeStruct((heads, rows, cols), F32),
        name="toeplitz_tiles",
    )(vecs.reshape(heads, rows // rb, 1, n))


def _t5_bucket(rel):
    nb = NUM_BUCKETS // 2
    max_exact = nb // 2
    bucket = jnp.where(rel > 0, nb, 0)
    n = jnp.abs(rel)
    nf = jnp.maximum(n, 1).astype(F32)
    large = max_exact + (jnp.log(nf / max_exact) / math.log(MAX_DISTANCE / max_exact)
                         * (nb - max_exact)).astype(jnp.int32)
    large = jnp.minimum(large, nb - 1)
    return bucket + jnp.where(n < max_exact, n, large)


def _diff_bias_tiles(t5_table):
    blk = BLK_A
    table = t5_table.astype(F32)
    far = table[_t5_bucket(jnp.full((), -(blk + 1), jnp.int32))]
    def visible(r, c):
        return jnp.floor_divide(r - blk, CHUNK) <= jnp.floor_divide(c, CHUNK)

    tiles = _toeplitz_tiles(lambda x: ((table[_t5_bucket(-x - blk)] - far) * LOG2E).T, visible,
                            N_HEADS_A, 2 * blk, blk)
    return tiles.reshape(N_HEADS_A, 2, blk, blk)


def _diff_attention(proj, t5_table, lam, subln_g, lam_init):
    seq = proj.shape[0]
    blk = BLK_A
    bias = _diff_bias_tiles(t5_table)
    ha = N_HEADS_A
    kern = functools.partial(_diffattn_kernel, out_scale=1.0 - lam_init)
    return pl.pallas_call(
        kern,
        grid=(ha, seq // blk),
        in_specs=[
            pl.BlockSpec((blk, DV_A), lambda h, i: (i, h)),
            pl.BlockSpec((seq, DV_A), lambda h, i: (0, ha + h)),
            pl.BlockSpec((seq, DV_A), lambda h, i: (0, 2 * ha + h)),
            pl.BlockSpec((1, 2, blk, blk), lambda h, i: (h, 0, 0, 0)),
            pl.BlockSpec((1, DV_A), lambda h, i: (0, 0)),
            pl.BlockSpec((1, DV_A), lambda h, i: (0, 0)),
        ],
        out_specs=pl.BlockSpec((blk, DV_A), lambda h, i: (i, h)),
        out_shape=jax.ShapeDtypeStruct((seq, ha * DV_A), BF16),
        scratch_shapes=[
            pltpu.VMEM((2 * blk, DV_A), BF16),
            pltpu.VMEM((DV_A, seq), BF16),
            pltpu.VMEM((8, 2 * blk), F32),
            pltpu.VMEM((8, 2 * blk), F32),
            pltpu.VMEM((DV_A, 2 * blk), F32),
        ] + [pltpu.VMEM((blk, 2 * blk // NPART_A), F32)] * (2 * NPART_A),
        compiler_params=pltpu.CompilerParams(dimension_semantics=("parallel", "arbitrary")),
        name="diff_attention",
    )(proj, proj, proj, bias, jnp.full((1, DV_A), lam, F32), subln_g.reshape(1, DV_A).astype(F32))


def _band_kernel(q_ref, kp_ref, kc_ref, vp_ref, vc_ref, bias_ref, o_ref):
    blk = BLK_B
    i = pl.program_id(1)
    q = q_ref[...] * (DH_B ** -0.5)
    lane = lax.broadcasted_iota(jnp.int32, q.shape, 1)
    kp, kc, vp, vc = kp_ref[...], kc_ref[...], vp_ref[...], vc_ref[...]
    no_prev = jnp.where(i == 0, NEG_INF, 0.0).astype(F32)
    outs = []
    for hh in range(2):
        sel = (lane < DH_B) if hh == 0 else (lane >= DH_B)
        qm = jnp.where(sel, q, 0).astype(BF16)
        sp = _dot_nt(qm, kp) + bias_ref[hh, :, 0:blk] + no_prev
        sc = _dot_nt(qm, kc) + bias_ref[hh, :, blk:2 * blk]
        m = jnp.maximum(jnp.max(sp, axis=-1, keepdims=True), jnp.max(sc, axis=-1, keepdims=True))
        pp = jnp.exp(sp - m)
        pc = jnp.exp(sc - m)
        l = jnp.sum(pp, axis=-1, keepdims=True) + jnp.sum(pc, axis=-1, keepdims=True)
        outs.append((_dot(pp.astype(BF16), vp) + _dot(pc.astype(BF16), vc)) / l)
    o_ref[...] = jnp.where(lane < DH_B, outs[0], outs[1]).astype(o_ref.dtype)


def _band_bias_tiles(rel_bias):
    blk = BLK_B

    def valid(r, c):
        qchunk = jnp.floor_divide(r, CHUNK)
        kchunk = jnp.floor_divide(c - blk, CHUNK)
        return (kchunk <= qchunk) & (kchunk >= qchunk - LEFT_CHUNKS)

    return _toeplitz_tiles(
        lambda x: rel_bias.astype(F32)[:, jnp.clip(x - blk, -REL_CLIP, REL_CLIP) + REL_CLIP], valid,
        N_HEADS_B, blk, 2 * blk)


def _band_attention(proj, rel_bias):
    seq = proj.shape[0]
    blk = BLK_B
    bias = _band_bias_tiles(rel_bias)
    npair = N_HEADS_B // 2
    qc0 = 3 * N_HEADS_A
    prev = lambda c0: (lambda hp, i: (jnp.maximum(i - 1, 0), c0 + hp))
    cur = lambda c0: (lambda hp, i: (i, c0 + hp))
    return pl.pallas_call(
        _band_kernel,
        grid=(npair, seq // blk),
        in_specs=[
            pl.BlockSpec((blk, LANES), cur(qc0)),
            pl.BlockSpec((blk, LANES), prev(qc0 + npair)),
            pl.BlockSpec((blk, LANES), cur(qc0 + npair)),
            pl.BlockSpec((blk, LANES), prev(qc0 + 2 * npair)),
            pl.BlockSpec((blk, LANES), cur(qc0 + 2 * npair)),
            pl.BlockSpec((2, blk, 2 * blk), lambda hp, i: (hp, 0, 0)),
        ],
        out_specs=pl.BlockSpec((blk, LANES), lambda hp, i: (i, hp)),
        out_shape=jax.ShapeDtypeStruct((seq, N_HEADS_B * DH_B), BF16),
        compiler_params=pltpu.CompilerParams(dimension_semantics=("parallel", "arbitrary")),
        name="band_attention",
    )(proj, proj, proj, proj, proj, bias)


def _retention_kernel(qk_ref, v_ref, gate_ref, cos_ref, sin_ref, qdec_ref, kdec_ref, dmat_ref,
                      sdec_ref, o_ref, state_ref):
    @pl.when(pl.program_id(0) == 0)
    def _():
        state_ref[...] = jnp.zeros(state_ref.shape, F32)

    cos = cos_ref[...]
    sin = sin_ref[...]
    lane = lax.broadcasted_iota(jnp.int32, cos.shape, 1)
    first_half = (lane % DQK_C) < (DQK_C // 2)
    qk = qk_ref[...]
    parts = []
    for j in range(qk.shape[1] // LANES):
        t = qk[:, j * LANES:(j + 1) * LANES]
        partner = jnp.where(first_half, pltpu.roll(t, LANES - DQK_C // 2, 1), pltpu.roll(t, DQK_C // 2, 1))
        parts.append(t * cos + partner * sin)
    wq = N_HEADS_C * DQK_C
    q = jnp.concatenate(parts[:wq // LANES], axis=1)
    k = jnp.concatenate(parts[wq // LANES:], axis=1) * (DQK_C ** -0.5)
    qd = (q * qdec_ref[...]).astype(BF16)
    kd = (k * kdec_ref[...]).astype(BF16)
    qb = q.astype(BF16)
    kb = k.astype(BF16)
    vb = v_ref[...].astype(BF16)
    gate = gate_ref[...]
    outs = []
    for h in range(N_HEADS_C):
        qs = slice(h * DQK_C, (h + 1) * DQK_C)
        vs = slice(h * DV_C, (h + 1) * DV_C)
        scores = _dot_nt(qb[:, qs], kb[:, qs]) * dmat_ref[h]
        state = state_ref[h]
        r = _dot(scores.astype(BF16), vb[:, vs]) + _dot(qd[:, qs], state.astype(BF16))
        state_ref[h] = state * sdec_ref[h] + _dot_tn(kd[:, qs], vb[:, vs])
        r = r * lax.rsqrt(jnp.mean(r * r, axis=-1, keepdims=True) + EPS)
        g = gate[:, vs]
        outs.append(r * (g * jax.nn.sigmoid(g)))
    o_ref[...] = jnp.concatenate(outs, axis=1).astype(o_ref.dtype)


def _retention_tables(seq):
    t = BLK_C
    half = DQK_C // 2
    inv_freq = 1.0 / (ROPE_BASE ** (jnp.arange(0, DQK_C, 2, dtype=F32) / DQK_C))
    ang = jnp.arange(seq, dtype=F32)[:, None] * inv_freq[None, :]
    reps = LANES // half
    cos = jnp.tile(jnp.cos(ang), (1, reps))
    sign = jnp.where((jnp.arange(LANES) % DQK_C) < half, -1.0, 1.0).astype(F32)
    sin = jnp.tile(jnp.sin(ang), (1, reps)) * sign[None, :]
    log_g = jnp.log(1.0 - jnp.power(2.0, -5.0 - jnp.arange(N_HEADS_C, dtype=F32)))
    pos = jnp.arange(t, dtype=F32)
    diff = pos[:, None] - pos[None, :]
    same_or_past = (jnp.arange(t)[None, :] // CHUNK) <= (jnp.arange(t)[:, None] // CHUNK)
    dmat = jnp.where(same_or_past[None], jnp.exp(log_g[:, None, None] * jnp.abs(diff)[None]), 0.0)
    qdec = jnp.repeat(jnp.exp(log_g[None, :] * (pos[:, None] + 1.0)), DQK_C, axis=1)
    kdec = jnp.repeat(jnp.exp(log_g[None, :] * (t - 1.0 - pos[:, None])), DQK_C, axis=1)
    sdec = jnp.broadcast_to(jnp.exp(log_g * t)[:, None, None], (N_HEADS_C, 1, DV_C))
    return cos, sin, qdec, kdec, dmat, sdec


def _retention(proj):
    seq = proj.shape[0]
    t = BLK_C
    cos, sin, qdec, kdec, dmat, sdec = _retention_tables(seq)
    wv = N_HEADS_C * DV_C
    return pl.pallas_call(
        _retention_kernel,
        grid=(seq // t,),
        in_specs=[
            pl.BlockSpec((t, wv), lambda i: (i, 0)),
            pl.BlockSpec((t, wv), lambda i: (i, 1)),
            pl.BlockSpec((t, wv), lambda i: (i, 2)),
            pl.BlockSpec((t, LANES), lambda i: (i, 0)),
            pl.BlockSpec((t, LANES), lambda i: (i, 0)),
            pl.BlockSpec((t, N_HEADS_C * DQK_C), lambda i: (0, 0)),
            pl.BlockSpec((t, N_HEADS_C * DQK_C), lambda i: (0, 0)),
            pl.BlockSpec((N_HEADS_C, t, t), lambda i: (0, 0, 0)),
            pl.BlockSpec((N_HEADS_C, 1, DV_C), lambda i: (0, 0, 0)),
        ],
        out_specs=pl.BlockSpec((t, wv), lambda i: (i, 0)),
        out_shape=jax.ShapeDtypeStruct((seq, wv), BF16),
        scratch_shapes=[pltpu.VMEM((N_HEADS_C, DQK_C, DV_C), F32)],
        compiler_params=pltpu.CompilerParams(dimension_semantics=("arbitrary",)),
        name="retention",
    )(proj, proj, proj, cos, sin, qdec, kdec, dmat, sdec)


def _s5_kernel(*refs):
    ncb = S5_CH // LANES
    u_refs = refs[:ncb]
    (mt_ref, bt_ref, ct_ref, are_ref, aim_ref, y_ref,
     ut_ref, yt_ref, ys_ref, vr_ref, vi_ref, spr_ref, spi_ref, carry_ref) = refs[ncb:]
    tc = S5_TC
    gp = S5_GROUP
    n = S5_STATE

    @pl.when(pl.program_id(0) == 0)
    def _():
        carry_ref[...] = jnp.zeros(carry_ref.shape, F32)

    for s in range(S5_T):
        for k in range(ncb):
            ut_ref[s, k * LANES:(k + 1) * LANES, :] = u_refs[k][pl.ds(s, tc, stride=S5_T), :].T

    def intra(g, carry):
        r0 = pl.multiple_of(g * gp, gp)
        ug = ut_ref[:, pl.ds(r0, gp), :].reshape(S5_T * gp, tc).astype(BF16)
        yt_ref[:, pl.ds(r0, gp), :] = _dot(mt_ref[g], ug).reshape(S5_T, gp, tc)
        vt = _dot(bt_ref[g], ug)
        n0 = pl.multiple_of(g * n, n)
        vr_ref[pl.ds(n0, n), :] = vt[0:n]
        vi_ref[pl.ds(n0, n), :] = vt[n:2 * n]
        return carry

    lax.fori_loop(0, S5_GROUPS, intra, 0)

    ar, ai = are_ref[...], aim_ref[...]
    cr, ci = carry_ref[0], carry_ref[1]
    xr, xi = vr_ref[...], vi_ref[...]
    lane = lax.broadcasted_iota(jnp.int32, xr.shape, 1)
    first = lane == 0
    xr = xr + jnp.where(first, ar * cr - ai * ci, 0.0)
    xi = xi + jnp.where(first, ar * ci + ai * cr, 0.0)
    pr, pi = ar, ai
    d = 1
    while d < tc:
        sr = jnp.where(lane >= d, pltpu.roll(xr, d, 1), 0.0)
        si = jnp.where(lane >= d, pltpu.roll(xi, d, 1), 0.0)
        xr, xi = xr + (pr * sr - pi * si), xi + (pr * si + pi * sr)
        pr, pi = pr * pr - pi * pi, 2.0 * pr * pi
        d *= 2
    spr_ref[...] = jnp.where(first, cr, pltpu.roll(xr, 1, 1))
    spi_ref[...] = jnp.where(first, ci, pltpu.roll(xi, 1, 1))
    carry_ref[0] = jnp.broadcast_to(xr[:, tc - 1:tc], xr.shape)
    carry_ref[1] = jnp.broadcast_to(xi[:, tc - 1:tc], xi.shape)

    def cross(g, carry):
        r0 = pl.multiple_of(g * gp, gp)
        n0 = pl.multiple_of(g * n, n)
        sp = jnp.concatenate([spr_ref[pl.ds(n0, n), :], spi_ref[pl.ds(n0, n), :]], axis=0).astype(BF16)
        yt_ref[:, pl.ds(r0, gp), :] += _dot(ct_ref[g], sp).reshape(S5_T, gp, tc)
        return carry

    lax.fori_loop(0, S5_GROUPS, cross, 0)

    for s in range(S5_T):
        for k in range(ncb):
            ys_ref[k, pl.ds(s, tc, stride=S5_T), :] = yt_ref[s, k * LANES:(k + 1) * LANES, :].T
    for k in range(ncb):
        y_ref[:, k * LANES:(k + 1) * LANES] = ys_ref[k]


def _s5_matrices(lam_re, lam_im, log_step, b_re, b_im, c_re, c_im, d_skip):
    hi = lax.Precision.HIGHEST
    t, gp, n, ng = S5_T, S5_GROUP, S5_STATE, S5_GROUPS
    lam = lax.complex(lam_re.astype(F32), lam_im.astype(F32))
    step = jnp.exp(log_step.astype(F32))[:, None]
    ls = lam * step
    a_bar = jnp.exp(ls)
    b_bar = ((a_bar - 1.0) / lam)[..., None] * lax.complex(b_re.astype(F32), b_im.astype(F32))
    cm = lax.complex(c_re.astype(F32), c_im.astype(F32))

    def apow(k):
        kk = k.astype(F32).astype(jnp.complex64)
        return jnp.exp(ls.reshape((ng,) + (1,) * k.ndim + (n,)) * kk[None, ..., None])

    tt = jnp.arange(t)
    kmat = jnp.einsum('gpn,gln,gnq->glpq', cm, apow(tt), b_bar, precision=hi).real
    krev = jnp.transpose(kmat[:, ::-1], (0, 2, 1, 3)).reshape(ng, gp, t * gp)
    kpad = jnp.pad(krev, ((0, 0), (0, 0), (0, t * gp)))
    mt = jnp.concatenate([kpad[:, :, (t - 1 - to) * gp:(2 * t - 1 - to) * gp] for to in range(t)], axis=1)
    dvec = jnp.tile(d_skip.astype(F32).reshape(ng, 1, gp), (1, t, 1)).reshape(ng, t * gp)
    mt = mt + jnp.eye(t * gp, dtype=F32)[None] * dvec[:, :, None]
    z = jnp.swapaxes(apow(t - 1 - tt), 1, 2)[:, :, :, None] * b_bar[:, :, None, :]
    z = z.reshape(ng, n, t * gp)
    bt = jnp.concatenate([z.real, z.imag], axis=1)
    w = cm[:, None, :, :] * apow(tt + 1)[:, :, None, :]
    w = w.reshape(ng, t * gp, n)
    ct = jnp.concatenate([w.real, -w.imag], axis=2)
    a_chunk = apow(jnp.full((), t)).reshape(ng * n, 1)
    are = jnp.broadcast_to(a_chunk.real, (ng * n, LANES))
    aim = jnp.broadcast_to(a_chunk.imag, (ng * n, LANES))
    return mt.astype(BF16), bt.astype(BF16), ct.astype(BF16), are, aim


def _s5(proj, mats):
    seq, width = proj.shape
    t, tc, gp, n, ng = S5_T, S5_TC, S5_GROUP, S5_STATE, S5_GROUPS
    rows = t * tc
    mt, bt, ct, are, aim = mats
    ncb = S5_CH // LANES
    cb0 = (width - S5_CH) // LANES
    u_specs = [pl.BlockSpec((rows, LANES), (lambda i, k=k: (i, cb0 + k))) for k in range(ncb)]
    return pl.pallas_call(
        _s5_kernel,
        grid=(seq // rows,),
        in_specs=u_specs + [
            _const_spec(mt.shape), _const_spec(bt.shape), _const_spec(ct.shape),
            _const_spec(are.shape), _const_spec(aim.shape),
        ],
        out_specs=pl.BlockSpec((rows, S5_CH), lambda i: (i, 0)),
        out_shape=jax.ShapeDtypeStruct((seq, S5_CH), F32),
        scratch_shapes=[
            pltpu.VMEM((t, S5_CH, tc), F32),
            pltpu.VMEM((t, S5_CH, tc), F32),
            pltpu.VMEM((ncb, rows, LANES), F32),
            pltpu.VMEM((ng * n, tc), F32),
            pltpu.VMEM((ng * n, tc), F32),
            pltpu.VMEM((ng * n, tc), F32),
            pltpu.VMEM((ng * n, tc), F32),
            pltpu.VMEM((2, ng * n, tc), F32),
        ],
        compiler_params=pltpu.CompilerParams(dimension_semantics=("arbitrary",)),
        name="s5_scan",
    )(*([proj] * ncb), mt, bt, ct, are, aim)


def _mix_ffn_kernel(*refs, glu, final):
    (x_ref, a_ref, b_ref, wo_ref, g1_ref), refs = refs[:5], refs[5:]
    if glu:
        gw_ref, refs = refs[0], refs[1:]
    (g_ref, sc_ref, sh_ref, gate_ref, win_ref, cw_ref, cb_ref, wout_ref, fg_ref,
     o_ref, h_ref, act_ref, gbuf_ref, carry_ref) = refs
    tm = x_ref.shape[0]
    halo = gbuf_ref.shape[0] - tm

    @pl.when(pl.program_id(0) == 0)
    def _():
        carry_ref[...] = jnp.zeros(carry_ref.shape, F32)

    if glu:
        y = jax.nn.gelu(b_ref[...]).astype(BF16)
        gg = _dot(y, gw_ref[...])
        half = gg.shape[1] // 2
        b = (gg[:, :half] * jax.nn.sigmoid(gg[:, half:])).astype(BF16)
    else:
        b = b_ref[...]
    cat = jnp.concatenate([a_ref[...], b], axis=1)
    x = x_ref[...] + g1_ref[...] * _dot(cat, wo_ref[...])
    h_ref[...] = _mod_rmsnorm(x, g_ref[...], sc_ref[...], sh_ref[...]).astype(BF16)
    for f in range(D_FF // TF_FFN):
        cs = slice(f * TF_FFN, (f + 1) * TF_FFN)
        gs = slice(D_FF + f * TF_FFN, D_FF + (f + 1) * TF_FFN)
        h = h_ref[...]
        val = _dot(h, win_ref[:, cs])
        gate = _dot(h, win_ref[:, gs])
        gbuf_ref[0:halo, :] = carry_ref[:, cs]
        gbuf_ref[halo:halo + tm, :] = gate
        carry_ref[:, cs] = gate[tm - halo:tm, :]
        conv = (gate * cw_ref[2:3, cs] + gbuf_ref[halo - 1:halo - 1 + tm, :] * cw_ref[1:2, cs]
                + gbuf_ref[halo - 2:halo - 2 + tm, :] * cw_ref[0:1, cs] + cb_ref[:, cs])
        act_ref[:, cs] = (jax.nn.gelu(conv) * val).astype(BF16)
    xn = x + gate_ref[...] * _dot(act_ref[...], wout_ref[...])
    if final:
        xn = xn * lax.rsqrt(jnp.mean(xn * xn, axis=-1, keepdims=True) + EPS) * fg_ref[...]
    o_ref[...] = xn


def _layer_spec(shape, layer):
    idx = (layer,) + (0,) * (len(shape) - 1)
    return pl.BlockSpec((None,) + tuple(shape[1:]), lambda *_: idx, pipeline_mode=pl.Buffered(1))


def _mix_ffn(x, a, b, wo, gate1, glu_w, g, scale, shift, gate2, w_in, conv_w, conv_b, w_out, final_g,
             layer, final):
    seq, d = x.shape
    tm = TM_FFN
    halo = 8
    row = pl.BlockSpec((1, d), lambda i: (0, 0))
    rows = lambda w: pl.BlockSpec((tm, w), lambda i: (i, 0))
    conv_b = conv_b.reshape(conv_b.shape[0], 1, D_FF)
    in_specs = [rows(d), rows(a.shape[1]), rows(b.shape[1]), _const_spec(wo.shape), row]
    args = [x, a, b, wo, gate1]
    if glu_w is not None:
        in_specs.append(_const_spec(glu_w.shape))
        args.append(glu_w)
    in_specs += [
        row, row, row, row,
        _layer_spec(w_in.shape, layer),
        _layer_spec(conv_w.shape, layer),
        _layer_spec(conv_b.shape, layer),
        _layer_spec(w_out.shape, layer),
        row,
    ]
    args += [g.reshape(1, d), scale, shift, gate2, w_in, conv_w, conv_b, w_out, final_g.reshape(1, d)]
    return pl.pallas_call(
        functools.partial(_mix_ffn_kernel, glu=glu_w is not None, final=final),
        grid=(seq // tm,),
        in_specs=in_specs,
        out_specs=pl.BlockSpec((tm, d), lambda i: (i, 0)),
        out_shape=jax.ShapeDtypeStruct((seq, d), F32),
        scratch_shapes=[
            pltpu.VMEM((tm, d), BF16),
            pltpu.VMEM((tm, D_FF), BF16),
            pltpu.VMEM((tm + halo, TF_FFN), F32),
            pltpu.VMEM((halo, D_FF), F32),
        ],
        compiler_params=pltpu.CompilerParams(dimension_semantics=("arbitrary",)),
        name="mix_ffn",
    )(*args)


def kernel(x, c, t5_table, mod_w, mod_b, norm1_g, norm2_g, ffn_w_in, ffn_conv_w, ffn_conv_b, ffn_w_out,
           ev_w_in, ev_w_out, diff_lambda, diff_subln_g, band_rel_bias,
           od_w_in, od_w_out, s5_lam_re, s5_lam_im, s5_log_step, s5_b_re, s5_b_im, s5_c_re, s5_c_im,
           s5_d, s5_glu_w, final_g):
    assert x.shape[0] == 1 and x.shape[2] == D_MODEL
    seq = x.shape[1]
    assert seq % TM_PROJ == 0 and seq % (S5_T * S5_TC) == 0
    d = D_MODEL
    xs = x[0]
    mod = _modulation(c, mod_w, mod_b)
    ffn_w_in_b = ffn_w_in.astype(BF16)
    ffn_w_out_b = ffn_w_out.astype(BF16)
    for i in range(DEPTH):
        sh1, sc1, g1, sh2, sc2, g2 = [mod[i, :, k * d:(k + 1) * d] for k in range(6)]
        if i % 2 == 0:
            e = i // 2
            lam_init = 0.8 - 0.6 * math.exp(-0.3 * i)
            lp = diff_lambda[e].astype(F32)
            lam = jnp.exp(jnp.sum(lp[0] * lp[1])) - jnp.exp(jnp.sum(lp[2] * lp[3])) + lam_init
            proj = _normproj(xs, norm1_g[i], sc1, sh1, ev_w_in[e].astype(BF16), BF16)
            mix_a = _diff_attention(proj, t5_table, lam, diff_subln_g[e], lam_init)
            mix_b = _band_attention(proj, band_rel_bias[e])
            wo, glu_w = ev_w_out[e].astype(BF16), None
        else:
            o = i // 2
            proj = _normproj(xs, norm1_g[i], sc1, sh1, od_w_in[o].astype(BF16), F32)
            mix_a = _retention(proj)
            mats = _s5_matrices(s5_lam_re[o], s5_lam_im[o], s5_log_step[o], s5_b_re[o], s5_b_im[o],
                                s5_c_re[o], s5_c_im[o], s5_d[o])
            mix_b = _s5(proj, mats)
            wo, glu_w = od_w_out[o].astype(BF16), s5_glu_w[o].astype(BF16)
        xs = _mix_ffn(xs, mix_a, mix_b, wo, g1, glu_w, norm2_g[i], sc2, sh2, g2,
                      ffn_w_in_b, ffn_conv_w, ffn_conv_b, ffn_w_out_b, final_g,
                      layer=i, final=(i == DEPTH - 1))
    return xs[None]
```

```python
import functools
import math

import jax
import jax.numpy as jnp
from jax import lax
from jax.experimental import pallas as pl
from jax.experimental.pallas import tpu as pltpu

F32 = jnp.float32
BF16 = jnp.bfloat16

D_MODEL = 1024
DEPTH = 2
CHUNK = 64
GROUP_WIDTH = D_MODEL // 2
DK_A = 64
DV_A = 2 * DK_A
N_HEADS_A = GROUP_WIDTH // DV_A
DH_B = 64
N_HEADS_B = GROUP_WIDTH // DH_B
LEFT_CHUNKS = 8
REL_CLIP = 2 * CHUNK
NUM_BUCKETS = 32
MAX_DISTANCE = 128
DV_C = 128
DQK_C = DV_C // 2
N_HEADS_C = GROUP_WIDTH // DV_C
ROPE_BASE = 10000.0
S5_CH = GROUP_WIDTH
S5_GROUP = 16
S5_GROUPS = S5_CH // S5_GROUP
S5_STATE = 64
D_FF = ((8 * D_MODEL // 3 + 255) // 256) * 256
CONV_W = 3
EVEN_IN = 3 * N_HEADS_A * DV_A + 3 * N_HEADS_B * DH_B
ODD_IN = 2 * N_HEADS_C * DQK_C + 2 * N_HEADS_C * DV_C + S5_CH
EPS = 1e-6
NEG_INF = -1e30
LOG2E = math.log2(math.e)

LANES = 128
MXU_DIM = 256

TM_PROJ = 1024
TM_FFN = 512
TF_FFN = MXU_DIM
BLK_A = 512
NPART_A = 2
BLK_B = 512
BLK_C = 256
S5_T = 16
S5_TC = LANES

assert BLK_B == LEFT_CHUNKS * CHUNK, "band window must be exactly one previous block"
assert BLK_A >= MAX_DISTANCE, "far key blocks must sit in the saturated T5 bucket"
assert DV_A == LANES, "diff-attention statistics are kept lane-replicated beside the accumulator"


def _dot(a, b):
    return jnp.dot(a, b, preferred_element_type=F32)


def _dot_nt(a, b):
    return lax.dot_general(a, b, (((1,), (1,)), ((), ())), preferred_element_type=F32)


def _dot_tn(a, b):
    return lax.dot_general(a, b, (((0,), (0,)), ((), ())), preferred_element_type=F32)


def _const_spec(shape):
    zeros = (0,) * len(shape)
    return pl.BlockSpec(shape, lambda *_: zeros, pipeline_mode=pl.Buffered(1))


def _mod_rmsnorm(x, g, scale, shift):
    y = x * lax.rsqrt(jnp.mean(x * x, axis=-1, keepdims=True) + EPS)
    y = y * g
    return y * (1.0 + scale) + shift


def _mod_kernel(c_ref, w_ref, b_ref, o_ref):
    c = c_ref[...]
    cond = c * jax.nn.sigmoid(c)
    o_ref[0] = jnp.sum(cond * w_ref[0], axis=0, keepdims=True) + b_ref[0]


def _modulation(c, mod_w, mod_b):
    depth, d, n = mod_w.shape
    tn = 1536
    return pl.pallas_call(
        _mod_kernel,
        grid=(depth, n // tn),
        in_specs=[
            pl.BlockSpec((d, 1), lambda i, j: (0, 0)),
            pl.BlockSpec((1, d, tn), lambda i, j: (i, 0, j)),
            pl.BlockSpec((1, 1, tn), lambda i, j: (i, 0, j)),
        ],
        out_specs=pl.BlockSpec((1, 1, tn), lambda i, j: (i, 0, j)),
        out_shape=jax.ShapeDtypeStruct((depth, 1, n), F32),
        name="modulation",
    )(c.reshape(d, 1), mod_w, mod_b.reshape(depth, 1, n))


def _normproj_kernel(x_ref, g_ref, sc_ref, sh_ref, w_ref, o_ref, h_ref):
    @pl.when(pl.program_id(1) == 0)
    def _():
        h_ref[...] = _mod_rmsnorm(x_ref[...], g_ref[...], sc_ref[...], sh_ref[...]).astype(BF16)

    o_ref[...] = _dot(h_ref[...], w_ref[...]).astype(o_ref.dtype)


def _normproj(x, g, scale, shift, w, out_dtype):
    seq, d = x.shape
    n = w.shape[1]
    tm, tn = TM_PROJ, 1024
    row = pl.BlockSpec((1, d), lambda i, j: (0, 0))
    return pl.pallas_call(
        _normproj_kernel,
        grid=(seq // tm, n // tn),
        in_specs=[
            pl.BlockSpec((tm, d), lambda i, j: (i, 0)),
            row, row, row,
            pl.BlockSpec((d, tn), lambda i, j: (0, j)),
        ],
        out_specs=pl.BlockSpec((tm, tn), lambda i, j: (i, j)),
        out_shape=jax.ShapeDtypeStruct((seq, n), out_dtype),
        scratch_shapes=[pltpu.VMEM((tm, d), BF16)],
        compiler_params=pltpu.CompilerParams(dimension_semantics=("parallel", "arbitrary")),
        name="normproj",
    )(x, g.reshape(1, d), scale, shift, w)


def _diffattn_kernel(q_ref, k_ref, v_ref, bias_ref, lam_ref, g_ref, o_ref,
                     qs_ref, vt_ref, m_ref, l_ref, acc_ref, *s_refs, out_scale):
    blk = BLK_A
    nq = 2 * blk
    sub = 8
    npart = len(s_refs) // 4
    wq = nq // npart
    sa_ref, sb_ref = s_refs[:2 * npart], s_refs[2 * npart:]
    i = pl.program_id(1)

    @pl.when(i == 0)
    def _():
        def tr(b, carry):
            r0 = pl.multiple_of(b * blk, blk)
            vt_ref[:, pl.ds(r0, blk)] = v_ref[pl.ds(r0, blk), :].astype(F32).T.astype(BF16)
            return carry
        lax.fori_loop(0, v_ref.shape[0] // blk, tr, 0)

    q = q_ref[...].astype(F32) * (DK_A ** -0.5 * LOG2E)
    lane = lax.broadcasted_iota(jnp.int32, q.shape, 1)
    qs_ref[0:blk, :] = jnp.where(lane < DK_A, q, 0.0).astype(BF16)
    qs_ref[blk:nq, :] = jnp.where(lane >= DK_A, q, 0.0).astype(BF16)
    m_ref[...] = jnp.full(m_ref.shape, NEG_INF, F32)
    l_ref[...] = jnp.zeros(l_ref.shape, F32)
    acc_ref[...] = jnp.zeros(acc_ref.shape, F32)

    def scores(b, s_ref):
        k = k_ref[pl.ds(pl.multiple_of(b * blk, blk), blk), :]
        for part in range(npart):
            s = _dot_nt(k, qs_ref[part * wq:(part + 1) * wq, :])
            s_ref[part][...] = s
            s_ref[npart + part][...] = jnp.max(s.reshape(blk // sub, sub, wq), axis=0)

    def softmax_pv(b, s_ref, bias):
        vt = vt_ref[:, pl.ds(pl.multiple_of(b * blk, blk), blk)]
        for part in range(npart):
            cols = slice(part * wq, (part + 1) * wq)
            s = s_ref[part][...]
            if bias is not None:
                q0 = (part * wq) % blk
                s = s + bias[:, q0:q0 + wq]
            s = s.reshape(blk // sub, sub, wq)
            m_prev = m_ref[:, cols]
            smax = jnp.max(s, axis=0) if bias is not None else s_ref[npart + part][...]
            m_cur = jnp.max(smax, axis=0, keepdims=True)
            m_new = jnp.maximum(m_prev, m_cur)
            alpha = jnp.exp2(m_prev - m_new)
            p = jnp.exp2(s - m_new[None])
            l_ref[:, cols] = alpha * l_ref[:, cols] + jnp.sum(p, axis=0)
            pv = _dot(vt, p.reshape(blk, wq).astype(BF16))
            acc_ref[:, cols] = acc_ref[:, cols] * alpha[0:1] + pv
            m_ref[:, cols] = m_new

    nfar = jnp.maximum(i - 1, 0)
    odd = lax.rem(nfar, 2)

    @pl.when(i == 0)
    def _():
        scores(0, sb_ref)

    @pl.when(i > 0)
    def _():
        @pl.when(odd == 1)
        def _():
            scores(0, sb_ref)
            scores(1, sa_ref)
            softmax_pv(0, sb_ref, None)

        @pl.when(odd == 0)
        def _():
            scores(0, sa_ref)

        def pair(b):
            scores(b + 1, sb_ref)
            softmax_pv(b, sa_ref, None)
            scores(b + 2, sa_ref)
            softmax_pv(b + 1, sb_ref, None)

        def quad_body(t, carry):
            pair(odd + 4 * t)
            pair(odd + 4 * t + 2)
            return carry

        npairs = nfar // 2
        lax.fori_loop(0, npairs // 2, quad_body, 0)

        @pl.when(lax.rem(npairs, 2) == 1)
        def _():
            pair(odd + 2 * (npairs - 1))
        scores(i, sb_ref)
        softmax_pv(i - 1, sa_ref, bias_ref[0, 0])

    softmax_pv(i, sb_ref, bias_ref[0, 1])

    ot = acc_ref[...] / jnp.sum(l_ref[...], axis=0, keepdims=True)
    o = ot[:, 0:blk].T - lam_ref[...] * ot[:, blk:nq].T
    o = o * lax.rsqrt(jnp.mean(o * o, axis=-1, keepdims=True) + EPS) * g_ref[...]
    o_ref[...] = (o * out_scale).astype(o_ref.dtype)


_TOEPLITZ_ROWS = 512
_TOEPLITZ_N = 2048


def _toeplitz_kernel(v_ref, o_ref, *, keep):
    rows, cols = o_ref.shape[1:]
    x = jnp.broadcast_to(v_ref[0, 0], (rows, v_ref.shape[-1]))
    tile = pltpu.roll(x, 0, 1, stride=1, stride_axis=0)[:, :cols]
    r = lax.broadcasted_iota(jnp.int32, (rows, cols), 0) + pl.program_id(1) * rows
    c = lax.broadcasted_iota(jnp.int32, (rows, cols), 1)
    o_ref[0] = jnp.where(keep(r, c), tile, NEG_INF)


def _toeplitz_tiles(fn, keep, heads, rows, cols):
    n, rb = _TOEPLITZ_N, _TOEPLITZ_ROWS
    assert rows % rb == 0 and rows <= n // 2 and cols <= n // 2
    idx = jnp.arange(n, dtype=jnp.int32)
    vec = fn(jnp.where(idx < n // 2, idx, idx - n)).astype(F32)
    vecs = jnp.stack([jnp.roll(vec, k * rb, axis=1) for k in range(rows // rb)], axis=1)
    return pl.pallas_call(
        functools.partial(_toeplitz_kernel, keep=keep),
        grid=(heads, rows // rb),
        in_specs=[pl.BlockSpec((1, 1, 1, n), lambda h, k: (h, k, 0, 0))],
        out_specs=pl.BlockSpec((1, rb, cols), lambda h, k: (h, k, 0)),
        out_shape=jax.ShapeDtypeStruct((heads, rows, cols), F32),
        name="toeplitz_tiles",
    )(vecs.reshape(heads, rows // rb, 1, n))


def _t5_bucket(rel):
    nb = NUM_BUCKETS // 2
    max_exact = nb // 2
    bucket = jnp.where(rel > 0, nb, 0)
    n = jnp.abs(rel)
    nf = jnp.maximum(n, 1).astype(F32)
    large = max_exact + (jnp.log(nf / max_exact) / math.log(MAX_DISTANCE / max_exact)
                         * (nb - max_exact)).astype(jnp.int32)
    large = jnp.minimum(large, nb - 1)
    return bucket + jnp.where(n < max_exact, n, large)


def _diff_bias_tiles(t5_table):
    blk = BLK_A
    table = t5_table.astype(F32)
    far = table[_t5_bucket(jnp.full((), -(blk + 1), jnp.int32))]
    def visible(r, c):
        return jnp.floor_divide(r - blk, CHUNK) <= jnp.floor_divide(c, CHUNK)

    tiles = _toeplitz_tiles(lambda x: ((table[_t5_bucket(-x - blk)] - far) * LOG2E).T, visible,
                            N_HEADS_A, 2 * blk, blk)
    return tiles.reshape(N_HEADS_A, 2, blk, blk)


def _diff_attention(proj, t5_table, lam, subln_g, lam_init):
    seq = proj.shape[0]
    blk = BLK_A
    bias = _diff_bias_tiles(t5_table)
    ha = N_HEADS_A
    kern = functools.partial(_diffattn_kernel, out_scale=1.0 - lam_init)
    return pl.pallas_call(
        kern,
        grid=(ha, seq // blk),
        in_specs=[
            pl.BlockSpec((blk, DV_A), lambda h, i: (i, h)),
            pl.BlockSpec((seq, DV_A), lambda h, i: (0, ha + h)),
            pl.BlockSpec((seq, DV_A), lambda h, i: (0, 2 * ha + h)),
            pl.BlockSpec((1, 2, blk, blk), lambda h, i: (h, 0, 0, 0)),
            pl.BlockSpec((1, DV_A), lambda h, i: (0, 0)),
            pl.BlockSpec((1, DV_A), lambda h, i: (0, 0)),
        ],
        out_specs=pl.BlockSpec((blk, DV_A), lambda h, i: (i, h)),
        out_shape=jax.ShapeDtypeStruct((seq, ha * DV_A), BF16),
        scratch_shapes=[
            pltpu.VMEM((2 * blk, DV_A), BF16),
            pltpu.VMEM((DV_A, seq), BF16),
            pltpu.VMEM((8, 2 * blk), F32),
            pltpu.VMEM((8, 2 * blk), F32),
            pltpu.VMEM((DV_A, 2 * blk), F32),
        ] + 2 * ([pltpu.VMEM((blk, 2 * blk // NPART_A), F32)] * NPART_A
                 + [pltpu.VMEM((8, 2 * blk // NPART_A), F32)] * NPART_A),
        compiler_params=pltpu.CompilerParams(dimension_semantics=("parallel", "arbitrary")),
        name="diff_attention",
    )(proj, proj, proj, bias, jnp.full((1, DV_A), lam, F32), subln_g.reshape(1, DV_A).astype(F32))


def _band_kernel(q_ref, kp_ref, kc_ref, vp_ref, vc_ref, bias_ref, o_ref):
    blk = BLK_B
    i = pl.program_id(1)
    q = q_ref[...] * (DH_B ** -0.5)
    lane = lax.broadcasted_iota(jnp.int32, q.shape, 1)
    kp, kc, vp, vc = kp_ref[...], kc_ref[...], vp_ref[...], vc_ref[...]
    no_prev = jnp.where(i == 0, NEG_INF, 0.0).astype(F32)
    outs = []
    for hh in range(2):
        sel = (lane < DH_B) if hh == 0 else (lane >= DH_B)
        qm = jnp.where(sel, q, 0).astype(BF16)
        sp = _dot_nt(qm, kp) + bias_ref[hh, :, 0:blk] + no_prev
        sc = _dot_nt(qm, kc) + bias_ref[hh, :, blk:2 * blk]
        m = jnp.maximum(jnp.max(sp, axis=-1, keepdims=True), jnp.max(sc, axis=-1, keepdims=True))
        pp = jnp.exp(sp - m)
        pc = jnp.exp(sc - m)
        l = jnp.sum(pp, axis=-1, keepdims=True) + jnp.sum(pc, axis=-1, keepdims=True)
        outs.append((_dot(pp.astype(BF16), vp) + _dot(pc.astype(BF16), vc)) / l)
    o_ref[...] = jnp.where(lane < DH_B, outs[0], outs[1]).astype(o_ref.dtype)


def _band_bias_tiles(rel_bias):
    blk = BLK_B

    def valid(r, c):
        qchunk = jnp.floor_divide(r, CHUNK)
        kchunk = jnp.floor_divide(c - blk, CHUNK)
        return (kchunk <= qchunk) & (kchunk >= qchunk - LEFT_CHUNKS)

    return _toeplitz_tiles(
        lambda x: rel_bias.astype(F32)[:, jnp.clip(x - blk, -REL_CLIP, REL_CLIP) + REL_CLIP], valid,
        N_HEADS_B, blk, 2 * blk)


def _band_attention(proj, rel_bias):
    seq = proj.shape[0]
    blk = BLK_B
    bias = _band_bias_tiles(rel_bias)
    npair = N_HEADS_B // 2
    qc0 = 3 * N_HEADS_A
    prev = lambda c0: (lambda hp, i: (jnp.maximum(i - 1, 0), c0 + hp))
    cur = lambda c0: (lambda hp, i: (i, c0 + hp))
    return pl.pallas_call(
        _band_kernel,
        grid=(npair, seq // blk),
        in_specs=[
            pl.BlockSpec((blk, LANES), cur(qc0)),
            pl.BlockSpec((blk, LANES), prev(qc0 + npair)),
            pl.BlockSpec((blk, LANES), cur(qc0 + npair)),
            pl.BlockSpec((blk, LANES), prev(qc0 + 2 * npair)),
            pl.BlockSpec((blk, LANES), cur(qc0 + 2 * npair)),
            pl.BlockSpec((2, blk, 2 * blk), lambda hp, i: (hp, 0, 0)),
        ],
        out_specs=pl.BlockSpec((blk, LANES), lambda hp, i: (i, hp)),
        out_shape=jax.ShapeDtypeStruct((seq, N_HEADS_B * DH_B), BF16),
        compiler_params=pltpu.CompilerParams(dimension_semantics=("parallel", "arbitrary")),
        name="band_attention",
    )(proj, proj, proj, proj, proj, bias)


def _retention_kernel(qk_ref, v_ref, gate_ref, cos_ref, sin_ref, qdec_ref, kdec_ref, dmat_ref,
                      sdec_ref, o_ref, state_ref):
    @pl.when(pl.program_id(0) == 0)
    def _():
        state_ref[...] = jnp.zeros(state_ref.shape, F32)

    cos = cos_ref[...]
    sin = sin_ref[...]
    lane = lax.broadcasted_iota(jnp.int32, cos.shape, 1)
    first_half = (lane % DQK_C) < (DQK_C // 2)
    qk = qk_ref[...]
    parts = []
    for j in range(qk.shape[1] // LANES):
        t = qk[:, j * LANES:(j + 1) * LANES]
        partner = jnp.where(first_half, pltpu.roll(t, LANES - DQK_C // 2, 1), pltpu.roll(t, DQK_C // 2, 1))
        parts.append(t * cos + partner * sin)
    wq = N_HEADS_C * DQK_C
    q = jnp.concatenate(parts[:wq // LANES], axis=1)
    k = jnp.concatenate(parts[wq // LANES:], axis=1) * (DQK_C ** -0.5)
    qd = (q * qdec_ref[...]).astype(BF16)
    kd = (k * kdec_ref[...]).astype(BF16)
    qb = q.astype(BF16)
    kb = k.astype(BF16)
    vb = v_ref[...].astype(BF16)
    gate = gate_ref[...]
    outs = []
    for h in range(N_HEADS_C):
        qs = slice(h * DQK_C, (h + 1) * DQK_C)
        vs = slice(h * DV_C, (h + 1) * DV_C)
        scores = _dot_nt(qb[:, qs], kb[:, qs]) * dmat_ref[h]
        state = state_ref[h]
        r = _dot(scores.astype(BF16), vb[:, vs]) + _dot(qd[:, qs], state.astype(BF16))
        state_ref[h] = state * sdec_ref[h] + _dot_tn(kd[:, qs], vb[:, vs])
        r = r * lax.rsqrt(jnp.mean(r * r, axis=-1, keepdims=True) + EPS)
        g = gate[:, vs]
        outs.append(r * (g * jax.nn.sigmoid(g)))
    o_ref[...] = jnp.concatenate(outs, axis=1).astype(o_ref.dtype)


def _retention_tables(seq):
    t = BLK_C
    half = DQK_C // 2
    inv_freq = 1.0 / (ROPE_BASE ** (jnp.arange(0, DQK_C, 2, dtype=F32) / DQK_C))
    ang = jnp.arange(seq, dtype=F32)[:, None] * inv_freq[None, :]
    reps = LANES // half
    cos = jnp.tile(jnp.cos(ang), (1, reps))
    sign = jnp.where((jnp.arange(LANES) % DQK_C) < half, -1.0, 1.0).astype(F32)
    sin = jnp.tile(jnp.sin(ang), (1, reps)) * sign[None, :]
    log_g = jnp.log(1.0 - jnp.power(2.0, -5.0 - jnp.arange(N_HEADS_C, dtype=F32)))
    pos = jnp.arange(t, dtype=F32)
    diff = pos[:, None] - pos[None, :]
    same_or_past = (jnp.arange(t)[None, :] // CHUNK) <= (jnp.arange(t)[:, None] // CHUNK)
    dmat = jnp.where(same_or_past[None], jnp.exp(log_g[:, None, None] * jnp.abs(diff)[None]), 0.0)
    qdec = jnp.repeat(jnp.exp(log_g[None, :] * (pos[:, None] + 1.0)), DQK_C, axis=1)
    kdec = jnp.repeat(jnp.exp(log_g[None, :] * (t - 1.0 - pos[:, None])), DQK_C, axis=1)
    sdec = jnp.broadcast_to(jnp.exp(log_g * t)[:, None, None], (N_HEADS_C, 1, DV_C))
    return cos, sin, qdec, kdec, dmat, sdec


def _retention(proj):
    seq = proj.shape[0]
    t = BLK_C
    cos, sin, qdec, kdec, dmat, sdec = _retention_tables(seq)
    wv = N_HEADS_C * DV_C
    return pl.pallas_call(
        _retention_kernel,
        grid=(seq // t,),
        in_specs=[
            pl.BlockSpec((t, wv), lambda i: (i, 0)),
            pl.BlockSpec((t, wv), lambda i: (i, 1)),
            pl.BlockSpec((t, wv), lambda i: (i, 2)),
            pl.BlockSpec((t, LANES), lambda i: (i, 0)),
            pl.BlockSpec((t, LANES), lambda i: (i, 0)),
            pl.BlockSpec((t, N_HEADS_C * DQK_C), lambda i: (0, 0)),
            pl.BlockSpec((t, N_HEADS_C * DQK_C), lambda i: (0, 0)),
            pl.BlockSpec((N_HEADS_C, t, t), lambda i: (0, 0, 0)),
            pl.BlockSpec((N_HEADS_C, 1, DV_C), lambda i: (0, 0, 0)),
        ],
        out_specs=pl.BlockSpec((t, wv), lambda i: (i, 0)),
        out_shape=jax.ShapeDtypeStruct((seq, wv), BF16),
        scratch_shapes=[pltpu.VMEM((N_HEADS_C, DQK_C, DV_C), F32)],
        compiler_params=pltpu.CompilerParams(dimension_semantics=("arbitrary",)),
        name="retention",
    )(proj, proj, proj, cos, sin, qdec, kdec, dmat, sdec)


def _s5_kernel(*refs):
    ncb = S5_CH // LANES
    u_refs = refs[:ncb]
    (mt_ref, bt_ref, ctr_ref, cti_ref, are_ref, aim_ref, y_ref,
     ut_ref, yt_ref, ys_ref, vr_ref, vi_ref, spr_ref, spi_ref, carry_ref) = refs[ncb:]
    tc = S5_TC
    gp = S5_GROUP
    n = S5_STATE
    ng = S5_GROUPS

    @pl.when(pl.program_id(0) == 0)
    def _():
        carry_ref[...] = jnp.zeros(carry_ref.shape, F32)

    for s in range(S5_T):
        for k in range(ncb):
            ut_ref[s, k * LANES:(k + 1) * LANES, :] = u_refs[k][pl.ds(s, tc, stride=S5_T), :].T

    unroll = 4

    def intra(it, carry):
        for k in range(unroll):
            g = it * unroll + k
            r0 = pl.multiple_of(g * gp, gp)
            ug = ut_ref[:, pl.ds(r0, gp), :].reshape(S5_T * gp, tc).astype(BF16)
            yt_ref[:, pl.ds(r0, gp), :] = _dot(mt_ref[g], ug).reshape(S5_T, gp, tc)
            vt = _dot(bt_ref[g], ug)
            n0 = pl.multiple_of(g * n, n)
            vr_ref[pl.ds(n0, n), :] = vt[0:n]
            vi_ref[pl.ds(n0, n), :] = vt[n:2 * n]
        return carry

    lax.fori_loop(0, ng // unroll, intra, 0)

    sub = 8
    nv = tc // sub
    row = lax.broadcasted_iota(jnp.int32, (tc, LANES), 0)
    in_vreg = lax.rem(row, sub)

    def rows_of(v, r):
        return jnp.broadcast_to(v[r:r + 1], (tc, LANES))

    for j in range(ng * n // LANES):
        cols = slice(j * LANES, (j + 1) * LANES)
        pwr, pwi = are_ref[:, cols], aim_ref[:, cols]
        xr = vr_ref[cols, :].T
        xi = vi_ref[cols, :].T
        for d in (1, 2, 4):
            keep = in_vreg >= d
            sr = jnp.where(keep, pltpu.roll(xr, d, 0), 0.0)
            si = jnp.where(keep, pltpu.roll(xi, d, 0), 0.0)
            fr, fi = rows_of(pwr, d - 1), rows_of(pwi, d - 1)
            xr, xi = xr + (fr * sr - fi * si), xi + (fr * si + fi * sr)
        cr, ci = carry_ref[0, :, cols], carry_ref[1, :, cols]
        cr0, ci0 = cr, ci
        outr, outi = [], []
        for v in range(nv):
            yr = xr[v * sub:(v + 1) * sub] + (pwr * cr - pwi * ci)
            yi = xi[v * sub:(v + 1) * sub] + (pwr * ci + pwi * cr)
            outr.append(yr)
            outi.append(yi)
            cr = jnp.broadcast_to(yr[sub - 1:sub], (sub, LANES))
            ci = jnp.broadcast_to(yi[sub - 1:sub], (sub, LANES))
        carry_ref[0, :, cols] = cr
        carry_ref[1, :, cols] = ci
        sr = jnp.concatenate(outr, axis=0)
        si = jnp.concatenate(outi, axis=0)
        first = row == 0
        spr_ref[j] = jnp.where(first, rows_of(cr0, 0), pltpu.roll(sr, 1, 0))
        spi_ref[j] = jnp.where(first, rows_of(ci0, 0), pltpu.roll(si, 1, 0))

    def cross(it, carry):
        for k in range(unroll):
            jp = it * unroll + k
            r0 = pl.multiple_of(jp * 2 * gp, 2 * gp)
            yc = (_dot_nt(ctr_ref[jp], spr_ref[jp].astype(BF16))
                  + _dot_nt(cti_ref[jp], spi_ref[jp].astype(BF16)))
            yt_ref[:, pl.ds(r0, 2 * gp), :] += yc.reshape(S5_T, 2 * gp, tc)
        return carry

    lax.fori_loop(0, ng // 2 // unroll, cross, 0)

    for s in range(S5_T):
        for k in range(ncb):
            ys_ref[k, pl.ds(s, tc, stride=S5_T), :] = yt_ref[s, k * LANES:(k + 1) * LANES, :].T
    for k in range(ncb):
        y_ref[:, k * LANES:(k + 1) * LANES] = ys_ref[k]


def _s5_matrices(lam_re, lam_im, log_step, b_re, b_im, c_re, c_im, d_skip):
    hi = lax.Precision.HIGHEST
    t, gp, n, ng = S5_T, S5_GROUP, S5_STATE, S5_GROUPS
    lam = lax.complex(lam_re.astype(F32), lam_im.astype(F32))
    step = jnp.exp(log_step.astype(F32))[:, None]
    ls = lam * step
    a_bar = jnp.exp(ls)
    b_bar = ((a_bar - 1.0) / lam)[..., None] * lax.complex(b_re.astype(F32), b_im.astype(F32))
    cm = lax.complex(c_re.astype(F32), c_im.astype(F32))

    def apow(k):
        kk = k.astype(F32).astype(jnp.complex64)
        return jnp.exp(ls.reshape((ng,) + (1,) * k.ndim + (n,)) * kk[None, ..., None])

    tt = jnp.arange(t)
    kmat = jnp.einsum('gpn,gln,gnq->glpq', cm, apow(tt), b_bar, precision=hi).real
    krev = jnp.transpose(kmat[:, ::-1], (0, 2, 1, 3)).reshape(ng, gp, t * gp)
    kpad = jnp.pad(krev, ((0, 0), (0, 0), (0, t * gp)))
    mt = jnp.concatenate([kpad[:, :, (t - 1 - to) * gp:(2 * t - 1 - to) * gp] for to in range(t)], axis=1)
    dvec = jnp.tile(d_skip.astype(F32).reshape(ng, 1, gp), (1, t, 1)).reshape(ng, t * gp)
    mt = mt + jnp.eye(t * gp, dtype=F32)[None] * dvec[:, :, None]
    z = jnp.swapaxes(apow(t - 1 - tt), 1, 2)[:, :, :, None] * b_bar[:, :, None, :]
    z = z.reshape(ng, n, t * gp)
    bt = jnp.concatenate([z.real, z.imag], axis=1)
    w = cm[:, None, :, :] * apow(tt + 1)[:, :, None, :]

    def pair_readout(x):
        x = jnp.transpose(x.reshape(ng // 2, 2, t, gp, n), (0, 2, 1, 3, 4))
        x = x[:, :, :, :, None, :] * jnp.eye(2, dtype=F32)[None, None, :, None, :, None]
        return x.reshape(ng // 2, t * 2 * gp, 2 * n).astype(BF16)

    ctr, cti = pair_readout(w.real), pair_readout(-w.imag)
    a_chunk = jnp.transpose(apow(t * (jnp.arange(8) + 1)), (1, 0, 2)).reshape(8, ng * n)
    return mt.astype(BF16), bt.astype(BF16), ctr, cti, a_chunk.real, a_chunk.imag


def _s5(proj, mats):
    seq, width = proj.shape
    t, tc, gp, n, ng = S5_T, S5_TC, S5_GROUP, S5_STATE, S5_GROUPS
    rows = t * tc
    ncb = S5_CH // LANES
    cb0 = (width - S5_CH) // LANES
    u_specs = [pl.BlockSpec((rows, LANES), (lambda i, k=k: (i, cb0 + k))) for k in range(ncb)]
    nsb = ng * n // LANES
    return pl.pallas_call(
        _s5_kernel,
        grid=(seq // rows,),
        in_specs=u_specs + [_const_spec(m.shape) for m in mats],
        out_specs=pl.BlockSpec((rows, S5_CH), lambda i: (i, 0)),
        out_shape=jax.ShapeDtypeStruct((seq, S5_CH), F32),
        scratch_shapes=[
            pltpu.VMEM((t, S5_CH, tc), F32),
            pltpu.VMEM((t, S5_CH, tc), F32),
            pltpu.VMEM((ncb, rows, LANES), F32),
            pltpu.VMEM((ng * n, tc), F32),
            pltpu.VMEM((ng * n, tc), F32),
            pltpu.VMEM((nsb, tc, LANES), F32),
            pltpu.VMEM((nsb, tc, LANES), F32),
            pltpu.VMEM((2, 8, ng * n), F32),
        ],
        compiler_params=pltpu.CompilerParams(dimension_semantics=("arbitrary",)),
        name="s5_scan",
    )(*([proj] * ncb), *mats)


def _mix_ffn_kernel(*refs, glu, final):
    (x_ref, a_ref, b_ref, wo_ref, g1_ref), refs = refs[:5], refs[5:]
    if glu:
        gw_ref, refs = refs[0], refs[1:]
    (g_ref, sc_ref, sh_ref, gate_ref, win_ref, cw_ref, cb_ref, wout_ref, fg_ref,
     o_ref, h_ref, act_ref, gbuf_ref, carry_ref) = refs
    tm = x_ref.shape[0]
    halo = gbuf_ref.shape[0] - tm

    @pl.when(pl.program_id(0) == 0)
    def _():
        carry_ref[...] = jnp.zeros(carry_ref.shape, F32)

    if glu:
        y = jax.nn.gelu(b_ref[...]).astype(BF16)
        gg = _dot(y, gw_ref[...])
        half = gg.shape[1] // 2
        b = (gg[:, :half] * jax.nn.sigmoid(gg[:, half:])).astype(BF16)
    else:
        b = b_ref[...]
    cat = jnp.concatenate([a_ref[...], b], axis=1)
    x = x_ref[...] + g1_ref[...] * _dot(cat, wo_ref[...])
    h_ref[...] = _mod_rmsnorm(x, g_ref[...], sc_ref[...], sh_ref[...]).astype(BF16)
    for f in range(D_FF // TF_FFN):
        cs = slice(f * TF_FFN, (f + 1) * TF_FFN)
        gs = slice(D_FF + f * TF_FFN, D_FF + (f + 1) * TF_FFN)
        h = h_ref[...]
        val = _dot(h, win_ref[:, cs])
        gate = _dot(h, win_ref[:, gs])
        gbuf_ref[0:halo, :] = carry_ref[:, cs]
        gbuf_ref[halo:halo + tm, :] = gate
        carry_ref[:, cs] = gate[tm - halo:tm, :]
        conv = (gate * cw_ref[2:3, cs] + gbuf_ref[halo - 1:halo - 1 + tm, :] * cw_ref[1:2, cs]
                + gbuf_ref[halo - 2:halo - 2 + tm, :] * cw_ref[0:1, cs] + cb_ref[:, cs])
        act_ref[:, cs] = (jax.nn.gelu(conv) * val).astype(BF16)
    xn = x + gate_ref[...] * _dot(act_ref[...], wout_ref[...])
    if final:
        xn = xn * lax.rsqrt(jnp.mean(xn * xn, axis=-1, keepdims=True) + EPS) * fg_ref[...]
    o_ref[...] = xn


def _layer_spec(shape, layer):
    idx = (layer,) + (0,) * (len(shape) - 1)
    return pl.BlockSpec((None,) + tuple(shape[1:]), lambda *_: idx, pipeline_mode=pl.Buffered(1))


def _mix_ffn(x, a, b, wo, gate1, glu_w, g, scale, shift, gate2, w_in, conv_w, conv_b, w_out, final_g,
             layer, final):
    seq, d = x.shape
    tm = TM_FFN
    halo = 8
    row = pl.BlockSpec((1, d), lambda i: (0, 0))
    rows = lambda w: pl.BlockSpec((tm, w), lambda i: (i, 0))
    conv_b = conv_b.reshape(conv_b.shape[0], 1, D_FF)
    in_specs = [rows(d), rows(a.shape[1]), rows(b.shape[1]), _const_spec(wo.shape), row]
    args = [x, a, b, wo, gate1]
    if glu_w is not None:
        in_specs.append(_const_spec(glu_w.shape))
        args.append(glu_w)
    in_specs += [
        row, row, row, row,
        _layer_spec(w_in.shape, layer),
        _layer_spec(conv_w.shape, layer),
        _layer_spec(conv_b.shape, layer),
        _layer_spec(w_out.shape, layer),
        row,
    ]
    args += [g.reshape(1, d), scale, shift, gate2, w_in, conv_w, conv_b, w_out, final_g.reshape(1, d)]
    return pl.pallas_call(
        functools.partial(_mix_ffn_kernel, glu=glu_w is not None, final=final),
        grid=(seq // tm,),
        in_specs=in_specs,
        out_specs=pl.BlockSpec((tm, d), lambda i: (i, 0)),
        out_shape=jax.ShapeDtypeStruct((seq, d), F32),
        scratch_shapes=[
            pltpu.VMEM((tm, d), BF16),
            pltpu.VMEM((tm, D_FF), BF16),
            pltpu.VMEM((tm + halo, TF_FFN), F32),
            pltpu.VMEM((halo, D_FF), F32),
        ],
        compiler_params=pltpu.CompilerParams(dimension_semantics=("arbitrary",)),
        name="mix_ffn",
    )(*args)


def kernel(x, c, t5_table, mod_w, mod_b, norm1_g, norm2_g, ffn_w_in, ffn_conv_w, ffn_conv_b, ffn_w_out,
           ev_w_in, ev_w_out, diff_lambda, diff_subln_g, band_rel_bias,
           od_w_in, od_w_out, s5_lam_re, s5_lam_im, s5_log_step, s5_b_re, s5_b_im, s5_c_re, s5_c_im,
           s5_d, s5_glu_w, final_g):
    assert x.shape[0] == 1 and x.shape[2] == D_MODEL
    seq = x.shape[1]
    assert seq % TM_PROJ == 0 and seq % (S5_T * S5_TC) == 0
    d = D_MODEL
    xs = x[0]
    mod = _modulation(c, mod_w, mod_b)
    ffn_w_in_b = ffn_w_in.astype(BF16)
    ffn_w_out_b = ffn_w_out.astype(BF16)
    for i in range(DEPTH):
        sh1, sc1, g1, sh2, sc2, g2 = [mod[i, :, k * d:(k + 1) * d] for k in range(6)]
        if i % 2 == 0:
            e = i // 2
            lam_init = 0.8 - 0.6 * math.exp(-0.3 * i)
            lp = diff_lambda[e].astype(F32)
            lam = jnp.exp(jnp.sum(lp[0] * lp[1])) - jnp.exp(jnp.sum(lp[2] * lp[3])) + lam_init
            proj = _normproj(xs, norm1_g[i], sc1, sh1, ev_w_in[e].astype(BF16), BF16)
            mix_a = _diff_attention(proj, t5_table, lam, diff_subln_g[e], lam_init)
            mix_b = _band_attention(proj, band_rel_bias[e])
            wo, glu_w = ev_w_out[e].astype(BF16), None
        else:
            o = i // 2
            proj = _normproj(xs, norm1_g[i], sc1, sh1, od_w_in[o].astype(BF16), F32)
            mix_a = _retention(proj)
            mats = _s5_matrices(s5_lam_re[o], s5_lam_im[o], s5_log_step[o], s5_b_re[o], s5_b_im[o],
                                s5_c_re[o], s5_c_im[o], s5_d[o])
            mix_b = _s5(proj, mats)
            wo, glu_w = od_w_out[o].astype(BF16), s5_glu_w[o].astype(BF16)
        xs = _mix_ffn(xs, mix_a, mix_b, wo, g1, glu_w, norm2_g[i], sc2, sh2, g2,
                      ffn_w_in_b, ffn_conv_w, ffn_conv_b, ffn_w_out_b, final_g,
                      layer=i, final=(i == DEPTH - 1))
    return xs[None]
```

```python
import functools
import math

import jax
import jax.numpy as jnp
from jax import lax
from jax.experimental import pallas as pl
from jax.experimental.pallas import tpu as pltpu

F32 = jnp.float32
BF16 = jnp.bfloat16

D_MODEL = 1024
DEPTH = 2
CHUNK = 64
GROUP_WIDTH = D_MODEL // 2
DK_A = 64
DV_A = 2 * DK_A
N_HEADS_A = GROUP_WIDTH // DV_A
DH_B = 64
N_HEADS_B = GROUP_WIDTH // DH_B
LEFT_CHUNKS = 8
REL_CLIP = 2 * CHUNK
NUM_BUCKETS = 32
MAX_DISTANCE = 128
DV_C = 128
DQK_C = DV_C // 2
N_HEADS_C = GROUP_WIDTH // DV_C
ROPE_BASE = 10000.0
S5_CH = GROUP_WIDTH
S5_GROUP = 16
S5_GROUPS = S5_CH // S5_GROUP
S5_STATE = 64
D_FF = ((8 * D_MODEL // 3 + 255) // 256) * 256
CONV_W = 3
EVEN_IN = 3 * N_HEADS_A * DV_A + 3 * N_HEADS_B * DH_B
ODD_IN = 2 * N_HEADS_C * DQK_C + 2 * N_HEADS_C * DV_C + S5_CH
EPS = 1e-6
NEG_INF = -1e30
LOG2E = math.log2(math.e)

LANES = 128
MXU_DIM = 256

TM_PROJ = 1024
TM_FFN = 512
TF_FFN = MXU_DIM
BLK_A = 512
NPART_A = 2
BLK_B = 512
BLK_C = 256
S5_T = 16
S5_TC = LANES

assert BLK_B == LEFT_CHUNKS * CHUNK, "band window must be exactly one previous block"
assert BLK_A >= MAX_DISTANCE, "far key blocks must sit in the saturated T5 bucket"
assert DV_A == LANES, "diff-attention statistics are kept lane-replicated beside the accumulator"


def _dot(a, b):
    return jnp.dot(a, b, preferred_element_type=F32)


def _dot_nt(a, b):
    return lax.dot_general(a, b, (((1,), (1,)), ((), ())), preferred_element_type=F32)


def _dot_tn(a, b):
    return lax.dot_general(a, b, (((0,), (0,)), ((), ())), preferred_element_type=F32)


def _const_spec(shape):
    zeros = (0,) * len(shape)
    return pl.BlockSpec(shape, lambda *_: zeros, pipeline_mode=pl.Buffered(1))


def _mod_rmsnorm(x, g, scale, shift):
    y = x * lax.rsqrt(jnp.mean(x * x, axis=-1, keepdims=True) + EPS)
    y = y * g
    return y * (1.0 + scale) + shift


def _mod_kernel(c_ref, w_ref, b_ref, o_ref):
    c = c_ref[...]
    cond = c * jax.nn.sigmoid(c)
    o_ref[0] = jnp.sum(cond * w_ref[0], axis=0, keepdims=True) + b_ref[0]


def _modulation(c, mod_w, mod_b):
    depth, d, n = mod_w.shape
    tn = 1536
    return pl.pallas_call(
        _mod_kernel,
        grid=(depth, n // tn),
        in_specs=[
            pl.BlockSpec((d, 1), lambda i, j: (0, 0)),
            pl.BlockSpec((1, d, tn), lambda i, j: (i, 0, j)),
            pl.BlockSpec((1, 1, tn), lambda i, j: (i, 0, j)),
        ],
        out_specs=pl.BlockSpec((1, 1, tn), lambda i, j: (i, 0, j)),
        out_shape=jax.ShapeDtypeStruct((depth, 1, n), F32),
        name="modulation",
    )(c.reshape(d, 1), mod_w, mod_b.reshape(depth, 1, n))


def _normproj_kernel(x_ref, g_ref, sc_ref, sh_ref, w_ref, o_ref, h_ref):
    @pl.when(pl.program_id(1) == 0)
    def _():
        h_ref[...] = _mod_rmsnorm(x_ref[...], g_ref[...], sc_ref[...], sh_ref[...]).astype(BF16)

    o_ref[...] = _dot(h_ref[...], w_ref[...]).astype(o_ref.dtype)


def _normproj(x, g, scale, shift, w, out_dtype):
    seq, d = x.shape
    n = w.shape[1]
    tm, tn = TM_PROJ, 1024
    row = pl.BlockSpec((1, d), lambda i, j: (0, 0))
    return pl.pallas_call(
        _normproj_kernel,
        grid=(seq // tm, n // tn),
        in_specs=[
            pl.BlockSpec((tm, d), lambda i, j: (i, 0)),
            row, row, row,
            pl.BlockSpec((d, tn), lambda i, j: (0, j)),
        ],
        out_specs=pl.BlockSpec((tm, tn), lambda i, j: (i, j)),
        out_shape=jax.ShapeDtypeStruct((seq, n), out_dtype),
        scratch_shapes=[pltpu.VMEM((tm, d), BF16)],
        compiler_params=pltpu.CompilerParams(dimension_semantics=("parallel", "arbitrary")),
        name="normproj",
    )(x, g.reshape(1, d), scale, shift, w)


def _diffattn_kernel(q_ref, k_ref, v_ref, bias_ref, lam_ref, g_ref, o_ref,
                     qs_ref, vt_ref, m_ref, l_ref, acc_ref, *s_refs, out_scale):
    blk = BLK_A
    nq = 2 * blk
    sub = 8
    npart = len(s_refs) // 4
    wq = nq // npart
    sa_ref, sb_ref = s_refs[:2 * npart], s_refs[2 * npart:]
    i = pl.program_id(1)

    @pl.when(i == 0)
    def _():
        def tr(b, carry):
            r0 = pl.multiple_of(b * blk, blk)
            vt_ref[:, pl.ds(r0, blk)] = v_ref[pl.ds(r0, blk), :].astype(F32).T.astype(BF16)
            return carry
        lax.fori_loop(0, v_ref.shape[0] // blk, tr, 0)

    q = q_ref[...].astype(F32) * (DK_A ** -0.5 * LOG2E)
    lane = lax.broadcasted_iota(jnp.int32, q.shape, 1)
    qs_ref[0:blk, :] = jnp.where(lane < DK_A, q, 0.0).astype(BF16)
    qs_ref[blk:nq, :] = jnp.where(lane >= DK_A, q, 0.0).astype(BF16)
    m_ref[...] = jnp.full(m_ref.shape, NEG_INF, F32)
    l_ref[...] = jnp.zeros(l_ref.shape, F32)
    acc_ref[...] = jnp.zeros(acc_ref.shape, F32)

    def scores(b, s_ref):
        k = k_ref[pl.ds(pl.multiple_of(b * blk, blk), blk), :]
        for part in range(npart):
            s = _dot_nt(k, qs_ref[part * wq:(part + 1) * wq, :])
            s_ref[part][...] = s
            s_ref[npart + part][...] = jnp.max(s.reshape(blk // sub, sub, wq), axis=0)

    def softmax_pv(b, s_ref, bias):
        vt = vt_ref[:, pl.ds(pl.multiple_of(b * blk, blk), blk)]
        for part in range(npart):
            cols = slice(part * wq, (part + 1) * wq)
            s = s_ref[part][...]
            if bias is not None:
                q0 = (part * wq) % blk
                s = s + bias[:, q0:q0 + wq]
            s = s.reshape(blk // sub, sub, wq)
            m_prev = m_ref[:, cols]
            smax = jnp.max(s, axis=0) if bias is not None else s_ref[npart + part][...]
            m_cur = jnp.max(smax, axis=0, keepdims=True)
            m_new = jnp.maximum(m_prev, m_cur)
            alpha = jnp.exp2(m_prev - m_new)
            p = jnp.exp2(s - m_new[None])
            l_ref[:, cols] = alpha * l_ref[:, cols] + jnp.sum(p, axis=0)
            pv = _dot(vt, p.reshape(blk, wq).astype(BF16))
            acc_ref[:, cols] = acc_ref[:, cols] * alpha[0:1] + pv
            m_ref[:, cols] = m_new

    nfar = jnp.maximum(i - 1, 0)
    odd = lax.rem(nfar, 2)

    @pl.when(i == 0)
    def _():
        scores(0, sb_ref)

    @pl.when(i > 0)
    def _():
        @pl.when(odd == 1)
        def _():
            scores(0, sb_ref)
            scores(1, sa_ref)
            softmax_pv(0, sb_ref, None)

        @pl.when(odd == 0)
        def _():
            scores(0, sa_ref)

        def pair(b):
            scores(b + 1, sb_ref)
            softmax_pv(b, sa_ref, None)
            scores(b + 2, sa_ref)
            softmax_pv(b + 1, sb_ref, None)

        def quad_body(t, carry):
            pair(odd + 4 * t)
            pair(odd + 4 * t + 2)
            return carry

        npairs = nfar // 2
        lax.fori_loop(0, npairs // 2, quad_body, 0)

        @pl.when(lax.rem(npairs, 2) == 1)
        def _():
            pair(odd + 2 * (npairs - 1))
        scores(i, sb_ref)
        softmax_pv(i - 1, sa_ref, bias_ref[0, 0])

    softmax_pv(i, sb_ref, bias_ref[0, 1])

    ot = acc_ref[...] / jnp.sum(l_ref[...], axis=0, keepdims=True)
    o = ot[:, 0:blk].T - lam_ref[...] * ot[:, blk:nq].T
    o = o * lax.rsqrt(jnp.mean(o * o, axis=-1, keepdims=True) + EPS) * g_ref[...]
    o_ref[...] = (o * out_scale).astype(o_ref.dtype)


_TOEPLITZ_ROWS = 512
_TOEPLITZ_N = 2048


def _toeplitz_kernel(v_ref, o_ref, *, keep):
    rows, cols = o_ref.shape[1:]
    x = jnp.broadcast_to(v_ref[0, 0], (rows, v_ref.shape[-1]))
    tile = pltpu.roll(x, 0, 1, stride=1, stride_axis=0)[:, :cols]
    r = lax.broadcasted_iota(jnp.int32, (rows, cols), 0) + pl.program_id(1) * rows
    c = lax.broadcasted_iota(jnp.int32, (rows, cols), 1)
    o_ref[0] = jnp.where(keep(r, c), tile, NEG_INF)


def _toeplitz_tiles(fn, keep, heads, rows, cols):
    n, rb = _TOEPLITZ_N, _TOEPLITZ_ROWS
    assert rows % rb == 0 and rows <= n // 2 and cols <= n // 2
    idx = jnp.arange(n, dtype=jnp.int32)
    vec = fn(jnp.where(idx < n // 2, idx, idx - n)).astype(F32)
    vecs = jnp.stack([jnp.roll(vec, k * rb, axis=1) for k in range(rows // rb)], axis=1)
    return pl.pallas_call(
        functools.partial(_toeplitz_kernel, keep=keep),
        grid=(heads, rows // rb),
        in_specs=[pl.BlockSpec((1, 1, 1, n), lambda h, k: (h, k, 0, 0))],
        out_specs=pl.BlockSpec((1, rb, cols), lambda h, k: (h, k, 0)),
        out_shape=jax.ShapeDtypeStruct((heads, rows, cols), F32),
        name="toeplitz_tiles",
    )(vecs.reshape(heads, rows // rb, 1, n))


def _t5_bucket(rel):
    nb = NUM_BUCKETS // 2
    max_exact = nb // 2
    bucket = jnp.where(rel > 0, nb, 0)
    n = jnp.abs(rel)
    nf = jnp.maximum(n, 1).astype(F32)
    large = max_exact + (jnp.log(nf / max_exact) / math.log(MAX_DISTANCE / max_exact)
                         * (nb - max_exact)).astype(jnp.int32)
    large = jnp.minimum(large, nb - 1)
    return bucket + jnp.where(n < max_exact, n, large)


def _diff_bias_tiles(t5_table):
    blk = BLK_A
    table = t5_table.astype(F32)
    far = table[_t5_bucket(jnp.full((), -(blk + 1), jnp.int32))]
    def visible(r, c):
        return jnp.floor_divide(r - blk, CHUNK) <= jnp.floor_divide(c, CHUNK)

    tiles = _toeplitz_tiles(lambda x: ((table[_t5_bucket(-x - blk)] - far) * LOG2E).T, visible,
                            N_HEADS_A, 2 * blk, blk)
    return tiles.reshape(N_HEADS_A, 2, blk, blk)


def _diff_attention(proj, t5_table, lam, subln_g, lam_init):
    seq = proj.shape[0]
    blk = BLK_A
    bias = _diff_bias_tiles(t5_table)
    ha = N_HEADS_A
    kern = functools.partial(_diffattn_kernel, out_scale=1.0 - lam_init)
    return pl.pallas_call(
        kern,
        grid=(ha, seq // blk),
        in_specs=[
            pl.BlockSpec((blk, DV_A), lambda h, i: (i, h)),
            pl.BlockSpec((seq, DV_A), lambda h, i: (0, ha + h)),
            pl.BlockSpec((seq, DV_A), lambda h, i: (0, 2 * ha + h)),
            pl.BlockSpec((1, 2, blk, blk), lambda h, i: (h, 0, 0, 0)),
            pl.BlockSpec((1, DV_A), lambda h, i: (0, 0)),
            pl.BlockSpec((1, DV_A), lambda h, i: (0, 0)),
        ],
        out_specs=pl.BlockSpec((blk, DV_A), lambda h, i: (i, h)),
        out_shape=jax.ShapeDtypeStruct((seq, ha * DV_A), BF16),
        scratch_shapes=[
            pltpu.VMEM((2 * blk, DV_A), BF16),
            pltpu.VMEM((DV_A, seq), BF16),
            pltpu.VMEM((8, 2 * blk), F32),
            pltpu.VMEM((8, 2 * blk), F32),
            pltpu.VMEM((DV_A, 2 * blk), F32),
        ] + 2 * ([pltpu.VMEM((blk, 2 * blk // NPART_A), F32)] * NPART_A
                 + [pltpu.VMEM((8, 2 * blk // NPART_A), F32)] * NPART_A),
        compiler_params=pltpu.CompilerParams(dimension_semantics=("parallel", "arbitrary")),
        name="diff_attention",
    )(proj, proj, proj, bias, jnp.full((1, DV_A), lam, F32), subln_g.reshape(1, DV_A).astype(F32))


def _band_kernel(q_ref, kp_ref, kc_ref, vp_ref, vc_ref, bias_ref, o_ref, *s_refs):
    blk = BLK_B
    half = blk // 2
    nk = 3 * half
    sub = 8
    i = pl.program_id(1)
    q = q_ref[...].astype(F32) * (DH_B ** -0.5 * LOG2E)
    lane = lax.broadcasted_iota(jnp.int32, q.shape, 1)
    qh = (jnp.where(lane < DH_B, q, 0.0).astype(BF16), jnp.where(lane >= DH_B, q, 0.0).astype(BF16))
    k_all = jnp.concatenate([kp_ref[...], kc_ref[...]], axis=0)
    vt_all = jnp.concatenate([vp_ref[...], vc_ref[...]], axis=0).astype(F32).T.astype(BF16)
    no_prev = jnp.where(i == 0, NEG_INF, 0.0).astype(F32)
    krow = lax.broadcasted_iota(jnp.int32, (nk, 2 * half), 0)
    for hf in range(2):
        k0 = hf * half
        qs = jnp.concatenate([qh[0][k0:k0 + half], qh[1][k0:k0 + half]], axis=0)
        s_refs[hf][...] = _dot_nt(k_all[k0:k0 + nk], qs)
    for hf in range(2):
        k0 = hf * half
        bias = jnp.concatenate([bias_ref[0, k0:k0 + nk, k0:k0 + half],
                                bias_ref[1, k0:k0 + nk, k0:k0 + half]], axis=1)
        s = s_refs[hf][...] + bias + jnp.where(krow < blk - k0, no_prev, 0.0)
        s = s.reshape(nk // sub, sub, 2 * half)
        m = jnp.max(jnp.max(s, axis=0), axis=0, keepdims=True)
        p = jnp.exp2(s - m[None])
        l = jnp.sum(jnp.sum(p, axis=0), axis=0, keepdims=True)
        ot = _dot(vt_all[:, k0:k0 + nk], p.reshape(nk, 2 * half).astype(BF16)) / l
        o = jnp.concatenate([ot[0:DH_B, 0:half], ot[DH_B:2 * DH_B, half:2 * half]], axis=0)
        o_ref[k0:k0 + half, :] = o.T.astype(o_ref.dtype)


def _band_bias_tiles(rel_bias):
    blk = BLK_B

    def valid(r, c):
        qchunk = jnp.floor_divide(c, CHUNK)
        kchunk = jnp.floor_divide(r - blk, CHUNK)
        return (kchunk <= qchunk) & (kchunk >= qchunk - LEFT_CHUNKS)

    return _toeplitz_tiles(
        lambda x: rel_bias.astype(F32)[:, jnp.clip(-x - blk, -REL_CLIP, REL_CLIP) + REL_CLIP] * LOG2E, valid,
        N_HEADS_B, 2 * blk, blk)


def _band_attention(proj, rel_bias):
    seq = proj.shape[0]
    blk = BLK_B
    bias = _band_bias_tiles(rel_bias)
    npair = N_HEADS_B // 2
    qc0 = 3 * N_HEADS_A
    prev = lambda c0: (lambda hp, i: (jnp.maximum(i - 1, 0), c0 + hp))
    cur = lambda c0: (lambda hp, i: (i, c0 + hp))
    return pl.pallas_call(
        _band_kernel,
        grid=(npair, seq // blk),
        in_specs=[
            pl.BlockSpec((blk, LANES), cur(qc0)),
            pl.BlockSpec((blk, LANES), prev(qc0 + npair)),
            pl.BlockSpec((blk, LANES), cur(qc0 + npair)),
            pl.BlockSpec((blk, LANES), prev(qc0 + 2 * npair)),
            pl.BlockSpec((blk, LANES), cur(qc0 + 2 * npair)),
            pl.BlockSpec((2, 2 * blk, blk), lambda hp, i: (hp, 0, 0)),
        ],
        out_specs=pl.BlockSpec((blk, LANES), lambda hp, i: (i, hp)),
        out_shape=jax.ShapeDtypeStruct((seq, N_HEADS_B * DH_B), BF16),
        scratch_shapes=[pltpu.VMEM((3 * blk // 2, blk), F32)] * 2,
        compiler_params=pltpu.CompilerParams(dimension_semantics=("parallel", "arbitrary")),
        name="band_attention",
    )(proj, proj, proj, proj, proj, bias)


def _retention_kernel(qk_ref, v_ref, gate_ref, cos_ref, sin_ref, qdec_ref, kdec_ref, dmat_ref,
                      sdec_ref, o_ref, state_ref):
    @pl.when(pl.program_id(0) == 0)
    def _():
        state_ref[...] = jnp.zeros(state_ref.shape, F32)

    cos = cos_ref[...]
    sin = sin_ref[...]
    lane = lax.broadcasted_iota(jnp.int32, cos.shape, 1)
    first_half = (lane % DQK_C) < (DQK_C // 2)
    qk = qk_ref[...]
    parts = []
    for j in range(qk.shape[1] // LANES):
        t = qk[:, j * LANES:(j + 1) * LANES]
        partner = jnp.where(first_half, pltpu.roll(t, LANES - DQK_C // 2, 1), pltpu.roll(t, DQK_C // 2, 1))
        parts.append(t * cos + partner * sin)
    wq = N_HEADS_C * DQK_C
    q = jnp.concatenate(parts[:wq // LANES], axis=1)
    k = jnp.concatenate(parts[wq // LANES:], axis=1) * (DQK_C ** -0.5)
    qd = (q * qdec_ref[...]).astype(BF16)
    kd = (k * kdec_ref[...]).astype(BF16)
    qb = q.astype(BF16)
    kb = k.astype(BF16)
    vb = v_ref[...].astype(BF16)
    gate = gate_ref[...]
    outs = []
    for h in range(N_HEADS_C):
        qs = slice(h * DQK_C, (h + 1) * DQK_C)
        vs = slice(h * DV_C, (h + 1) * DV_C)
        scores = _dot_nt(qb[:, qs], kb[:, qs]) * dmat_ref[h]
        state = state_ref[h]
        r = _dot(scores.astype(BF16), vb[:, vs]) + _dot(qd[:, qs], state.astype(BF16))
        state_ref[h] = state * sdec_ref[h] + _dot_tn(kd[:, qs], vb[:, vs])
        r = r * lax.rsqrt(jnp.mean(r * r, axis=-1, keepdims=True) + EPS)
        g = gate[:, vs]
        outs.append(r * (g * jax.nn.sigmoid(g)))
    o_ref[...] = jnp.concatenate(outs, axis=1).astype(o_ref.dtype)


def _retention_tables(seq):
    t = BLK_C
    half = DQK_C // 2
    inv_freq = 1.0 / (ROPE_BASE ** (jnp.arange(0, DQK_C, 2, dtype=F32) / DQK_C))
    ang = jnp.arange(seq, dtype=F32)[:, None] * inv_freq[None, :]
    reps = LANES // half
    cos = jnp.tile(jnp.cos(ang), (1, reps))
    sign = jnp.where((jnp.arange(LANES) % DQK_C) < half, -1.0, 1.0).astype(F32)
    sin = jnp.tile(jnp.sin(ang), (1, reps)) * sign[None, :]
    log_g = jnp.log(1.0 - jnp.power(2.0, -5.0 - jnp.arange(N_HEADS_C, dtype=F32)))
    pos = jnp.arange(t, dtype=F32)
    diff = pos[:, None] - pos[None, :]
    same_or_past = (jnp.arange(t)[None, :] // CHUNK) <= (jnp.arange(t)[:, None] // CHUNK)
    dmat = jnp.where(same_or_past[None], jnp.exp(log_g[:, None, None] * jnp.abs(diff)[None]), 0.0)
    qdec = jnp.repeat(jnp.exp(log_g[None, :] * (pos[:, None] + 1.0)), DQK_C, axis=1)
    kdec = jnp.repeat(jnp.exp(log_g[None, :] * (t - 1.0 - pos[:, None])), DQK_C, axis=1)
    sdec = jnp.broadcast_to(jnp.exp(log_g * t)[:, None, None], (N_HEADS_C, 1, DV_C))
    return cos, sin, qdec, kdec, dmat, sdec


def _retention(proj):
    seq = proj.shape[0]
    t = BLK_C
    cos, sin, qdec, kdec, dmat, sdec = _retention_tables(seq)
    wv = N_HEADS_C * DV_C
    return pl.pallas_call(
        _retention_kernel,
        grid=(seq // t,),
        in_specs=[
            pl.BlockSpec((t, wv), lambda i: (i, 0)),
            pl.BlockSpec((t, wv), lambda i: (i, 1)),
            pl.BlockSpec((t, wv), lambda i: (i, 2)),
            pl.BlockSpec((t, LANES), lambda i: (i, 0)),
            pl.BlockSpec((t, LANES), lambda i: (i, 0)),
            pl.BlockSpec((t, N_HEADS_C * DQK_C), lambda i: (0, 0)),
            pl.BlockSpec((t, N_HEADS_C * DQK_C), lambda i: (0, 0)),
            pl.BlockSpec((N_HEADS_C, t, t), lambda i: (0, 0, 0)),
            pl.BlockSpec((N_HEADS_C, 1, DV_C), lambda i: (0, 0, 0)),
        ],
        out_specs=pl.BlockSpec((t, wv), lambda i: (i, 0)),
        out_shape=jax.ShapeDtypeStruct((seq, wv), BF16),
        scratch_shapes=[pltpu.VMEM((N_HEADS_C, DQK_C, DV_C), F32)],
        compiler_params=pltpu.CompilerParams(dimension_semantics=("arbitrary",)),
        name="retention",
    )(proj, proj, proj, cos, sin, qdec, kdec, dmat, sdec)


def _s5_kernel(*refs):
    ncb = S5_CH // LANES
    u_refs = refs[:ncb]
    (mt_ref, bt_ref, ctr_ref, cti_ref, are_ref, aim_ref, y_ref,
     ut_ref, yt_ref, ys_ref, vr_ref, vi_ref, spr_ref, spi_ref, carry_ref) = refs[ncb:]
    tc = S5_TC
    gp = S5_GROUP
    n = S5_STATE
    ng = S5_GROUPS

    @pl.when(pl.program_id(0) == 0)
    def _():
        carry_ref[...] = jnp.zeros(carry_ref.shape, F32)

    for s in range(S5_T):
        for k in range(ncb):
            ut_ref[s, k * LANES:(k + 1) * LANES, :] = u_refs[k][pl.ds(s, tc, stride=S5_T), :].T

    unroll = 4

    def intra(it, carry):
        for k in range(unroll):
            g = it * unroll + k
            r0 = pl.multiple_of(g * gp, gp)
            ug = ut_ref[:, pl.ds(r0, gp), :].reshape(S5_T * gp, tc).astype(BF16)
            yt_ref[:, pl.ds(r0, gp), :] = _dot(mt_ref[g], ug).reshape(S5_T, gp, tc)
            vt = _dot(bt_ref[g], ug)
            n0 = pl.multiple_of(g * n, n)
            vr_ref[pl.ds(n0, n), :] = vt[0:n]
            vi_ref[pl.ds(n0, n), :] = vt[n:2 * n]
        return carry

    lax.fori_loop(0, ng // unroll, intra, 0)

    sub = 8
    nv = tc // sub
    row = lax.broadcasted_iota(jnp.int32, (tc, LANES), 0)
    in_vreg = lax.rem(row, sub)

    def rows_of(v, r):
        return jnp.broadcast_to(v[r:r + 1], (tc, LANES))

    for j in range(ng * n // LANES):
        cols = slice(j * LANES, (j + 1) * LANES)
        pwr, pwi = are_ref[:, cols], aim_ref[:, cols]
        xr = vr_ref[cols, :].T
        xi = vi_ref[cols, :].T
        for d in (1, 2, 4):
            keep = in_vreg >= d
            sr = jnp.where(keep, pltpu.roll(xr, d, 0), 0.0)
            si = jnp.where(keep, pltpu.roll(xi, d, 0), 0.0)
            fr, fi = rows_of(pwr, d - 1), rows_of(pwi, d - 1)
            xr, xi = xr + (fr * sr - fi * si), xi + (fr * si + fi * sr)
        cr, ci = carry_ref[0, :, cols], carry_ref[1, :, cols]
        cr0, ci0 = cr, ci
        outr, outi = [], []
        for v in range(nv):
            yr = xr[v * sub:(v + 1) * sub] + (pwr * cr - pwi * ci)
            yi = xi[v * sub:(v + 1) * sub] + (pwr * ci + pwi * cr)
            outr.append(yr)
            outi.append(yi)
            cr = jnp.broadcast_to(yr[sub - 1:sub], (sub, LANES))
            ci = jnp.broadcast_to(yi[sub - 1:sub], (sub, LANES))
        carry_ref[0, :, cols] = cr
        carry_ref[1, :, cols] = ci
        sr = jnp.concatenate(outr, axis=0)
        si = jnp.concatenate(outi, axis=0)
        first = row == 0
        spr_ref[j] = jnp.where(first, rows_of(cr0, 0), pltpu.roll(sr, 1, 0))
        spi_ref[j] = jnp.where(first, rows_of(ci0, 0), pltpu.roll(si, 1, 0))

    def cross(it, carry):
        for k in range(unroll):
            jp = it * unroll + k
            r0 = pl.multiple_of(jp * 2 * gp, 2 * gp)
            yc = (_dot_nt(ctr_ref[jp], spr_ref[jp].astype(BF16))
                  + _dot_nt(cti_ref[jp], spi_ref[jp].astype(BF16)))
            yt_ref[:, pl.ds(r0, 2 * gp), :] += yc.reshape(S5_T, 2 * gp, tc)
        return carry

    lax.fori_loop(0, ng // 2 // unroll, cross, 0)

    for s in range(S5_T):
        for k in range(ncb):
            ys_ref[k, pl.ds(s, tc, stride=S5_T), :] = yt_ref[s, k * LANES:(k + 1) * LANES, :].T
    for k in range(ncb):
        y_ref[:, k * LANES:(k + 1) * LANES] = ys_ref[k]


def _s5_matrices(lam_re, lam_im, log_step, b_re, b_im, c_re, c_im, d_skip):
    hi = lax.Precision.HIGHEST
    t, gp, n, ng = S5_T, S5_GROUP, S5_STATE, S5_GROUPS
    lam = lax.complex(lam_re.astype(F32), lam_im.astype(F32))
    step = jnp.exp(log_step.astype(F32))[:, None]
    ls = lam * step
    a_bar = jnp.exp(ls)
    b_bar = ((a_bar - 1.0) / lam)[..., None] * lax.complex(b_re.astype(F32), b_im.astype(F32))
    cm = lax.complex(c_re.astype(F32), c_im.astype(F32))

    def apow(k):
        kk = k.astype(F32).astype(jnp.complex64)
        return jnp.exp(ls.reshape((ng,) + (1,) * k.ndim + (n,)) * kk[None, ..., None])

    tt = jnp.arange(t)
    kmat = jnp.einsum('gpn,gln,gnq->glpq', cm, apow(tt), b_bar, precision=hi).real
    krev = jnp.transpose(kmat[:, ::-1], (0, 2, 1, 3)).reshape(ng, gp, t * gp)
    kpad = jnp.pad(krev, ((0, 0), (0, 0), (0, t * gp)))
    mt = jnp.concatenate([kpad[:, :, (t - 1 - to) * gp:(2 * t - 1 - to) * gp] for to in range(t)], axis=1)
    dvec = jnp.tile(d_skip.astype(F32).reshape(ng, 1, gp), (1, t, 1)).reshape(ng, t * gp)
    mt = mt + jnp.eye(t * gp, dtype=F32)[None] * dvec[:, :, None]
    z = jnp.swapaxes(apow(t - 1 - tt), 1, 2)[:, :, :, None] * b_bar[:, :, None, :]
    z = z.reshape(ng, n, t * gp)
    bt = jnp.concatenate([z.real, z.imag], axis=1)
    w = cm[:, None, :, :] * apow(tt + 1)[:, :, None, :]

    def pair_readout(x):
        x = jnp.transpose(x.reshape(ng // 2, 2, t, gp, n), (0, 2, 1, 3, 4))
        x = x[:, :, :, :, None, :] * jnp.eye(2, dtype=F32)[None, None, :, None, :, None]
        return x.reshape(ng // 2, t * 2 * gp, 2 * n).astype(BF16)

    ctr, cti = pair_readout(w.real), pair_readout(-w.imag)
    a_chunk = jnp.transpose(apow(t * (jnp.arange(8) + 1)), (1, 0, 2)).reshape(8, ng * n)
    return mt.astype(BF16), bt.astype(BF16), ctr, cti, a_chunk.real, a_chunk.imag


def _s5(proj, mats):
    seq, width = proj.shape
    t, tc, gp, n, ng = S5_T, S5_TC, S5_GROUP, S5_STATE, S5_GROUPS
    rows = t * tc
    ncb = S5_CH // LANES
    cb0 = (width - S5_CH) // LANES
    u_specs = [pl.BlockSpec((rows, LANES), (lambda i, k=k: (i, cb0 + k))) for k in range(ncb)]
    nsb = ng * n // LANES
    return pl.pallas_call(
        _s5_kernel,
        grid=(seq // rows,),
        in_specs=u_specs + [_const_spec(m.shape) for m in mats],
        out_specs=pl.BlockSpec((rows, S5_CH), lambda i: (i, 0)),
        out_shape=jax.ShapeDtypeStruct((seq, S5_CH), F32),
        scratch_shapes=[
            pltpu.VMEM((t, S5_CH, tc), F32),
            pltpu.VMEM((t, S5_CH, tc), F32),
            pltpu.VMEM((ncb, rows, LANES), F32),
            pltpu.VMEM((ng * n, tc), F32),
            pltpu.VMEM((ng * n, tc), F32),
            pltpu.VMEM((nsb, tc, LANES), F32),
            pltpu.VMEM((nsb, tc, LANES), F32),
            pltpu.VMEM((2, 8, ng * n), F32),
        ],
        compiler_params=pltpu.CompilerParams(dimension_semantics=("arbitrary",)),
        name="s5_scan",
    )(*([proj] * ncb), *mats)


def _mix_ffn_kernel(*refs, glu, final):
    (x_ref, a_ref, b_ref, wo_ref, g1_ref), refs = refs[:5], refs[5:]
    if glu:
        gw_ref, refs = refs[0], refs[1:]
    (g_ref, sc_ref, sh_ref, gate_ref, win_ref, cw_ref, cb_ref, wout_ref, fg_ref,
     o_ref, h_ref, act_ref, gbuf_ref, carry_ref) = refs
    tm = x_ref.shape[0]
    halo = gbuf_ref.shape[0] - tm

    @pl.when(pl.program_id(0) == 0)
    def _():
        carry_ref[...] = jnp.zeros(carry_ref.shape, F32)

    if glu:
        y = jax.nn.gelu(b_ref[...]).astype(BF16)
        gg = _dot(y, gw_ref[...])
        half = gg.shape[1] // 2
        b = (gg[:, :half] * jax.nn.sigmoid(gg[:, half:])).astype(BF16)
    else:
        b = b_ref[...]
    cat = jnp.concatenate([a_ref[...], b], axis=1)
    x = x_ref[...] + g1_ref[...] * _dot(cat, wo_ref[...])
    h_ref[...] = _mod_rmsnorm(x, g_ref[...], sc_ref[...], sh_ref[...]).astype(BF16)
    for f in range(D_FF // TF_FFN):
        cs = slice(f * TF_FFN, (f + 1) * TF_FFN)
        gs = slice(D_FF + f * TF_FFN, D_FF + (f + 1) * TF_FFN)
        h = h_ref[...]
        val = _dot(h, win_ref[:, cs])
        gate = _dot(h, win_ref[:, gs])
        gbuf_ref[0:halo, :] = carry_ref[:, cs]
        gbuf_ref[halo:halo + tm, :] = gate
        carry_ref[:, cs] = gate[tm - halo:tm, :]
        conv = (gate * cw_ref[2:3, cs] + gbuf_ref[halo - 1:halo - 1 + tm, :] * cw_ref[1:2, cs]
                + gbuf_ref[halo - 2:halo - 2 + tm, :] * cw_ref[0:1, cs] + cb_ref[:, cs])
        act_ref[:, cs] = (jax.nn.gelu(conv) * val).astype(BF16)
    xn = x + gate_ref[...] * _dot(act_ref[...], wout_ref[...])
    if final:
        xn = xn * lax.rsqrt(jnp.mean(xn * xn, axis=-1, keepdims=True) + EPS) * fg_ref[...]
    o_ref[...] = xn


def _layer_spec(shape, layer):
    idx = (layer,) + (0,) * (len(shape) - 1)
    return pl.BlockSpec((None,) + tuple(shape[1:]), lambda *_: idx, pipeline_mode=pl.Buffered(1))


def _mix_ffn(x, a, b, wo, gate1, glu_w, g, scale, shift, gate2, w_in, conv_w, conv_b, w_out, final_g,
             layer, final):
    seq, d = x.shape
    tm = TM_FFN
    halo = 8
    row = pl.BlockSpec((1, d), lambda i: (0, 0))
    rows = lambda w: pl.BlockSpec((tm, w), lambda i: (i, 0))
    conv_b = conv_b.reshape(conv_b.shape[0], 1, D_FF)
    in_specs = [rows(d), rows(a.shape[1]), rows(b.shape[1]), _const_spec(wo.shape), row]
    args = [x, a, b, wo, gate1]
    if glu_w is not None:
        in_specs.append(_const_spec(glu_w.shape))
        args.append(glu_w)
    in_specs += [
        row, row, row, row,
        _layer_spec(w_in.shape, layer),
        _layer_spec(conv_w.shape, layer),
        _layer_spec(conv_b.shape, layer),
        _layer_spec(w_out.shape, layer),
        row,
    ]
    args += [g.reshape(1, d), scale, shift, gate2, w_in, conv_w, conv_b, w_out, final_g.reshape(1, d)]
    return pl.pallas_call(
        functools.partial(_mix_ffn_kernel, glu=glu_w is not None, final=final),
        grid=(seq // tm,),
        in_specs=in_specs,
        out_specs=pl.BlockSpec((tm, d), lambda i: (i, 0)),
        out_shape=jax.ShapeDtypeStruct((seq, d), F32),
        scratch_shapes=[
            pltpu.VMEM((tm, d), BF16),
            pltpu.VMEM((tm, D_FF), BF16),
            pltpu.VMEM((tm + halo, TF_FFN), F32),
            pltpu.VMEM((halo, D_FF), F32),
        ],
        compiler_params=pltpu.CompilerParams(dimension_semantics=("arbitrary",)),
        name="mix_ffn",
    )(*args)


def kernel(x, c, t5_table, mod_w, mod_b, norm1_g, norm2_g, ffn_w_in, ffn_conv_w, ffn_conv_b, ffn_w_out,
           ev_w_in, ev_w_out, diff_lambda, diff_subln_g, band_rel_bias,
           od_w_in, od_w_out, s5_lam_re, s5_lam_im, s5_log_step, s5_b_re, s5_b_im, s5_c_re, s5_c_im,
           s5_d, s5_glu_w, final_g):
    assert x.shape[0] == 1 and x.shape[2] == D_MODEL
    seq = x.shape[1]
    assert seq % TM_PROJ == 0 and seq % (S5_T * S5_TC) == 0
    d = D_MODEL
    xs = x[0]
    mod = _modulation(c, mod_w, mod_b)
    ffn_w_in_b = ffn_w_in.astype(BF16)
    ffn_w_out_b = ffn_w_out.astype(BF16)
    for i in range(DEPTH):
        sh1, sc1, g1, sh2, sc2, g2 = [mod[i, :, k * d:(k + 1) * d] for k in range(6)]
        if i % 2 == 0:
            e = i // 2
            lam_init = 0.8 - 0.6 * math.exp(-0.3 * i)
            lp = diff_lambda[e].astype(F32)
            lam = jnp.exp(jnp.sum(lp[0] * lp[1])) - jnp.exp(jnp.sum(lp[2] * lp[3])) + lam_init
            proj = _normproj(xs, norm1_g[i], sc1, sh1, ev_w_in[e].astype(BF16), BF16)
            mix_a = _diff_attention(proj, t5_table, lam, diff_subln_g[e], lam_init)
            mix_b = _band_attention(proj, band_rel_bias[e])
            wo, glu_w = ev_w_out[e].astype(BF16), None
        else:
            o = i // 2
            proj = _normproj(xs, norm1_g[i], sc1, sh1, od_w_in[o].astype(BF16), F32)
            mix_a = _retention(proj)
            mats = _s5_matrices(s5_lam_re[o], s5_lam_im[o], s5_log_step[o], s5_b_re[o], s5_b_im[o],
                                s5_c_re[o], s5_c_im[o], s5_d[o])
            mix_b = _s5(proj, mats)
            wo, glu_w = od_w_out[o].astype(BF16), s5_glu_w[o].astype(BF16)
        xs = _mix_ffn(xs, mix_a, mix_b, wo, g1, glu_w, norm2_g[i], sc2, sh2, g2,
                      ffn_w_in_b, ffn_conv_w, ffn_conv_b, ffn_w_out_b, final_g,
                      layer=i, final=(i == DEPTH - 1))
    return xs[None]
```

```python
import functools
import math

import jax
import jax.numpy as jnp
from jax import lax
from jax.experimental import pallas as pl
from jax.experimental.pallas import tpu as pltpu

F32 = jnp.float32
BF16 = jnp.bfloat16

D_MODEL = 1024
DEPTH = 2
CHUNK = 64
GROUP_WIDTH = D_MODEL // 2
DK_A = 64
DV_A = 2 * DK_A
N_HEADS_A = GROUP_WIDTH // DV_A
DH_B = 64
N_HEADS_B = GROUP_WIDTH // DH_B
LEFT_CHUNKS = 8
REL_CLIP = 2 * CHUNK
NUM_BUCKETS = 32
MAX_DISTANCE = 128
DV_C = 128
DQK_C = DV_C // 2
N_HEADS_C = GROUP_WIDTH // DV_C
ROPE_BASE = 10000.0
S5_CH = GROUP_WIDTH
S5_GROUP = 16
S5_GROUPS = S5_CH // S5_GROUP
S5_STATE = 64
D_FF = ((8 * D_MODEL // 3 + 255) // 256) * 256
CONV_W = 3
EVEN_IN = 3 * N_HEADS_A * DV_A + 3 * N_HEADS_B * DH_B
ODD_IN = 2 * N_HEADS_C * DQK_C + 2 * N_HEADS_C * DV_C + S5_CH
EPS = 1e-6
NEG_INF = -1e30
LOG2E = math.log2(math.e)

LANES = 128
MXU_DIM = 256

TM_PROJ = 1024
TM_FFN = 512
TF_FFN = MXU_DIM
BLK_A = 512
NPART_A = 2
BLK_B = 512
BLK_C = 256
S5_T = 16
S5_TC = LANES

assert BLK_B == LEFT_CHUNKS * CHUNK, "band window must be exactly one previous block"
assert BLK_A >= MAX_DISTANCE, "far key blocks must sit in the saturated T5 bucket"
assert DV_A == LANES, "diff-attention statistics are kept lane-replicated beside the accumulator"


def _dot(a, b):
    return jnp.dot(a, b, preferred_element_type=F32)


def _dot_nt(a, b):
    return lax.dot_general(a, b, (((1,), (1,)), ((), ())), preferred_element_type=F32)


def _dot_tn(a, b):
    return lax.dot_general(a, b, (((0,), (0,)), ((), ())), preferred_element_type=F32)


def _const_spec(shape):
    zeros = (0,) * len(shape)
    return pl.BlockSpec(shape, lambda *_: zeros, pipeline_mode=pl.Buffered(1))


def _mod_rmsnorm(x, g, scale, shift):
    y = x * lax.rsqrt(jnp.mean(x * x, axis=-1, keepdims=True) + EPS)
    y = y * g
    return y * (1.0 + scale) + shift


def _mod_kernel(c_ref, w_ref, b_ref, o_ref):
    c = c_ref[...]
    cond = c * jax.nn.sigmoid(c)
    o_ref[0] = jnp.sum(cond * w_ref[0], axis=0, keepdims=True) + b_ref[0]


def _modulation(c, mod_w, mod_b):
    depth, d, n = mod_w.shape
    tn = 1536
    return pl.pallas_call(
        _mod_kernel,
        grid=(depth, n // tn),
        in_specs=[
            pl.BlockSpec((d, 1), lambda i, j: (0, 0)),
            pl.BlockSpec((1, d, tn), lambda i, j: (i, 0, j)),
            pl.BlockSpec((1, 1, tn), lambda i, j: (i, 0, j)),
        ],
        out_specs=pl.BlockSpec((1, 1, tn), lambda i, j: (i, 0, j)),
        out_shape=jax.ShapeDtypeStruct((depth, 1, n), F32),
        name="modulation",
    )(c.reshape(d, 1), mod_w, mod_b.reshape(depth, 1, n))


def _normproj_kernel(x_ref, g_ref, sc_ref, sh_ref, w_ref, o_ref, h_ref):
    @pl.when(pl.program_id(1) == 0)
    def _():
        h_ref[...] = _mod_rmsnorm(x_ref[...], g_ref[...], sc_ref[...], sh_ref[...]).astype(BF16)

    o_ref[...] = _dot(h_ref[...], w_ref[...]).astype(o_ref.dtype)


def _normproj(x, g, scale, shift, w, out_dtype):
    seq, d = x.shape
    n = w.shape[1]
    tm, tn = TM_PROJ, 1024
    row = pl.BlockSpec((1, d), lambda i, j: (0, 0))
    return pl.pallas_call(
        _normproj_kernel,
        grid=(seq // tm, n // tn),
        in_specs=[
            pl.BlockSpec((tm, d), lambda i, j: (i, 0)),
            row, row, row,
            pl.BlockSpec((d, tn), lambda i, j: (0, j)),
        ],
        out_specs=pl.BlockSpec((tm, tn), lambda i, j: (i, j)),
        out_shape=jax.ShapeDtypeStruct((seq, n), out_dtype),
        scratch_shapes=[pltpu.VMEM((tm, d), BF16)],
        compiler_params=pltpu.CompilerParams(dimension_semantics=("parallel", "arbitrary")),
        name="normproj",
    )(x, g.reshape(1, d), scale, shift, w)


def _proj_kernel(h_ref, w_ref, o_ref):
    o_ref[...] = _dot(h_ref[...], w_ref[...]).astype(o_ref.dtype)


def _proj(h, w, out_dtype):
    seq, d = h.shape
    n = w.shape[1]
    tm, tn = TM_PROJ, 1024
    return pl.pallas_call(
        _proj_kernel,
        grid=(seq // tm, n // tn),
        in_specs=[
            pl.BlockSpec((tm, d), lambda i, j: (i, 0)),
            pl.BlockSpec((d, tn), lambda i, j: (0, j)),
        ],
        out_specs=pl.BlockSpec((tm, tn), lambda i, j: (i, j)),
        out_shape=jax.ShapeDtypeStruct((seq, n), out_dtype),
        compiler_params=pltpu.CompilerParams(dimension_semantics=("parallel", "arbitrary")),
        name="proj",
    )(h, w)


def _diffattn_kernel(q_ref, k_ref, v_ref, bias_ref, lam_ref, g_ref, o_ref,
                     qs_ref, vt_ref, m_ref, l_ref, acc_ref, *s_refs, out_scale):
    blk = BLK_A
    nq = 2 * blk
    sub = 8
    npart = len(s_refs) // 4
    wq = nq // npart
    sa_ref, sb_ref = s_refs[:2 * npart], s_refs[2 * npart:]
    i = pl.program_id(1)

    @pl.when(i == 0)
    def _():
        def tr(b, carry):
            r0 = pl.multiple_of(b * blk, blk)
            vt_ref[:, pl.ds(r0, blk)] = v_ref[pl.ds(r0, blk), :].astype(F32).T.astype(BF16)
            return carry
        lax.fori_loop(0, v_ref.shape[0] // blk, tr, 0)

    q = q_ref[...].astype(F32) * (DK_A ** -0.5 * LOG2E)
    lane = lax.broadcasted_iota(jnp.int32, q.shape, 1)
    qs_ref[0:blk, :] = jnp.where(lane < DK_A, q, 0.0).astype(BF16)
    qs_ref[blk:nq, :] = jnp.where(lane >= DK_A, q, 0.0).astype(BF16)
    m_ref[...] = jnp.full(m_ref.shape, NEG_INF, F32)
    l_ref[...] = jnp.zeros(l_ref.shape, F32)
    acc_ref[...] = jnp.zeros(acc_ref.shape, F32)

    def scores(b, s_ref):
        k = k_ref[pl.ds(pl.multiple_of(b * blk, blk), blk), :]
        for part in range(npart):
            s = _dot_nt(k, qs_ref[part * wq:(part + 1) * wq, :])
            s_ref[part][...] = s
            s_ref[npart + part][...] = jnp.max(s.reshape(blk // sub, sub, wq), axis=0)

    def softmax_pv(b, s_ref, bias):
        vt = vt_ref[:, pl.ds(pl.multiple_of(b * blk, blk), blk)]
        for part in range(npart):
            cols = slice(part * wq, (part + 1) * wq)
            s = s_ref[part][...]
            if bias is not None:
                q0 = (part * wq) % blk
                s = s + bias[:, q0:q0 + wq]
            s = s.reshape(blk // sub, sub, wq)
            m_prev = m_ref[:, cols]
            smax = jnp.max(s, axis=0) if bias is not None else s_ref[npart + part][...]
            m_cur = jnp.max(smax, axis=0, keepdims=True)
            m_new = jnp.maximum(m_prev, m_cur)
            alpha = jnp.exp2(m_prev - m_new)
            p = jnp.exp2(s - m_new[None])
            l_ref[:, cols] = alpha * l_ref[:, cols] + jnp.sum(p, axis=0)
            pv = _dot(vt, p.reshape(blk, wq).astype(BF16))
            acc_ref[:, cols] = acc_ref[:, cols] * alpha[0:1] + pv
            m_ref[:, cols] = m_new

    nfar = jnp.maximum(i - 1, 0)
    odd = lax.rem(nfar, 2)

    @pl.when(i == 0)
    def _():
        scores(0, sb_ref)

    @pl.when(i > 0)
    def _():
        @pl.when(odd == 1)
        def _():
            scores(0, sb_ref)
            scores(1, sa_ref)
            softmax_pv(0, sb_ref, None)

        @pl.when(odd == 0)
        def _():
            scores(0, sa_ref)

        def pair(b):
            scores(b + 1, sb_ref)
            softmax_pv(b, sa_ref, None)
            scores(b + 2, sa_ref)
            softmax_pv(b + 1, sb_ref, None)

        def quad_body(t, carry):
            pair(odd + 4 * t)
            pair(odd + 4 * t + 2)
            return carry

        npairs = nfar // 2
        lax.fori_loop(0, npairs // 2, quad_body, 0)

        @pl.when(lax.rem(npairs, 2) == 1)
        def _():
            pair(odd + 2 * (npairs - 1))
        scores(i, sb_ref)
        softmax_pv(i - 1, sa_ref, bias_ref[0, 0])

    softmax_pv(i, sb_ref, bias_ref[0, 1])

    ot = acc_ref[...] / jnp.sum(l_ref[...], axis=0, keepdims=True)
    o = ot[:, 0:blk].T - lam_ref[...] * ot[:, blk:nq].T
    o = o * lax.rsqrt(jnp.mean(o * o, axis=-1, keepdims=True) + EPS) * g_ref[...]
    o_ref[...] = (o * out_scale).astype(o_ref.dtype)


_TOEPLITZ_ROWS = 512
_TOEPLITZ_N = 2048


def _toeplitz_kernel(v_ref, o_ref, *, keep):
    rows, cols = o_ref.shape[1:]
    x = jnp.broadcast_to(v_ref[0, 0], (rows, v_ref.shape[-1]))
    tile = pltpu.roll(x, 0, 1, stride=1, stride_axis=0)[:, :cols]
    r = lax.broadcasted_iota(jnp.int32, (rows, cols), 0) + pl.program_id(1) * rows
    c = lax.broadcasted_iota(jnp.int32, (rows, cols), 1)
    o_ref[0] = jnp.where(keep(r, c), tile, NEG_INF)


def _toeplitz_tiles(fn, keep, heads, rows, cols):
    n, rb = _TOEPLITZ_N, _TOEPLITZ_ROWS
    assert rows % rb == 0 and rows <= n // 2 and cols <= n // 2
    idx = jnp.arange(n, dtype=jnp.int32)
    vec = fn(jnp.where(idx < n // 2, idx, idx - n)).astype(F32)
    vecs = jnp.stack([jnp.roll(vec, k * rb, axis=1) for k in range(rows // rb)], axis=1)
    return pl.pallas_call(
        functools.partial(_toeplitz_kernel, keep=keep),
        grid=(heads, rows // rb),
        in_specs=[pl.BlockSpec((1, 1, 1, n), lambda h, k: (h, k, 0, 0))],
        out_specs=pl.BlockSpec((1, rb, cols), lambda h, k: (h, k, 0)),
        out_shape=jax.ShapeDtypeStruct((heads, rows, cols), F32),
        name="toeplitz_tiles",
    )(vecs.reshape(heads, rows // rb, 1, n))


def _t5_bucket(rel):
    nb = NUM_BUCKETS // 2
    max_exact = nb // 2
    bucket = jnp.where(rel > 0, nb, 0)
    n = jnp.abs(rel)
    nf = jnp.maximum(n, 1).astype(F32)
    large = max_exact + (jnp.log(nf / max_exact) / math.log(MAX_DISTANCE / max_exact)
                         * (nb - max_exact)).astype(jnp.int32)
    large = jnp.minimum(large, nb - 1)
    return bucket + jnp.where(n < max_exact, n, large)


def _diff_bias_tiles(t5_table):
    blk = BLK_A
    table = t5_table.astype(F32)
    far = table[_t5_bucket(jnp.full((), -(blk + 1), jnp.int32))]
    def visible(r, c):
        return jnp.floor_divide(r - blk, CHUNK) <= jnp.floor_divide(c, CHUNK)

    tiles = _toeplitz_tiles(lambda x: ((table[_t5_bucket(-x - blk)] - far) * LOG2E).T, visible,
                            N_HEADS_A, 2 * blk, blk)
    return tiles.reshape(N_HEADS_A, 2, blk, blk)


def _diff_attention(proj, t5_table, lam, subln_g, lam_init):
    seq = proj.shape[0]
    blk = BLK_A
    bias = _diff_bias_tiles(t5_table)
    ha = N_HEADS_A
    kern = functools.partial(_diffattn_kernel, out_scale=1.0 - lam_init)
    return pl.pallas_call(
        kern,
        grid=(ha, seq // blk),
        in_specs=[
            pl.BlockSpec((blk, DV_A), lambda h, i: (i, h)),
            pl.BlockSpec((seq, DV_A), lambda h, i: (0, ha + h)),
            pl.BlockSpec((seq, DV_A), lambda h, i: (0, 2 * ha + h)),
            pl.BlockSpec((1, 2, blk, blk), lambda h, i: (h, 0, 0, 0)),
            pl.BlockSpec((1, DV_A), lambda h, i: (0, 0)),
            pl.BlockSpec((1, DV_A), lambda h, i: (0, 0)),
        ],
        out_specs=pl.BlockSpec((blk, DV_A), lambda h, i: (i, h)),
        out_shape=jax.ShapeDtypeStruct((seq, ha * DV_A), BF16),
        scratch_shapes=[
            pltpu.VMEM((2 * blk, DV_A), BF16),
            pltpu.VMEM((DV_A, seq), BF16),
            pltpu.VMEM((8, 2 * blk), F32),
            pltpu.VMEM((8, 2 * blk), F32),
            pltpu.VMEM((DV_A, 2 * blk), F32),
        ] + 2 * ([pltpu.VMEM((blk, 2 * blk // NPART_A), F32)] * NPART_A
                 + [pltpu.VMEM((8, 2 * blk // NPART_A), F32)] * NPART_A),
        compiler_params=pltpu.CompilerParams(dimension_semantics=("parallel", "arbitrary")),
        name="diff_attention",
    )(proj, proj, proj, bias, jnp.full((1, DV_A), lam, F32), subln_g.reshape(1, DV_A).astype(F32))


def _band_kernel(q_ref, kp_ref, kc_ref, vp_ref, vc_ref, bias_ref, o_ref, *s_refs):
    blk = BLK_B
    half = blk // 2
    nk = 3 * half
    sub = 8
    i = pl.program_id(1)
    q = q_ref[...].astype(F32) * (DH_B ** -0.5 * LOG2E)
    lane = lax.broadcasted_iota(jnp.int32, q.shape, 1)
    qh = (jnp.where(lane < DH_B, q, 0.0).astype(BF16), jnp.where(lane >= DH_B, q, 0.0).astype(BF16))
    k_all = jnp.concatenate([kp_ref[...], kc_ref[...]], axis=0)
    vt_all = jnp.concatenate([vp_ref[...], vc_ref[...]], axis=0).astype(F32).T.astype(BF16)
    no_prev = jnp.where(i == 0, NEG_INF, 0.0).astype(F32)
    krow = lax.broadcasted_iota(jnp.int32, (nk, 2 * half), 0)
    for hf in range(2):
        k0 = hf * half
        qs = jnp.concatenate([qh[0][k0:k0 + half], qh[1][k0:k0 + half]], axis=0)
        s_refs[hf][...] = _dot_nt(k_all[k0:k0 + nk], qs)
    for hf in range(2):
        k0 = hf * half
        bias = jnp.concatenate([bias_ref[0, k0:k0 + nk, k0:k0 + half],
                                bias_ref[1, k0:k0 + nk, k0:k0 + half]], axis=1)
        s = s_refs[hf][...] + bias + jnp.where(krow < blk - k0, no_prev, 0.0)
        s = s.reshape(nk // sub, sub, 2 * half)
        m = jnp.max(jnp.max(s, axis=0), axis=0, keepdims=True)
        p = jnp.exp2(s - m[None])
        l = jnp.sum(jnp.sum(p, axis=0), axis=0, keepdims=True)
        ot = _dot(vt_all[:, k0:k0 + nk], p.reshape(nk, 2 * half).astype(BF16)) / l
        o = jnp.concatenate([ot[0:DH_B, 0:half], ot[DH_B:2 * DH_B, half:2 * half]], axis=0)
        o_ref[k0:k0 + half, :] = o.T.astype(o_ref.dtype)


def _band_bias_tiles(rel_bias):
    blk = BLK_B

    def valid(r, c):
        qchunk = jnp.floor_divide(c, CHUNK)
        kchunk = jnp.floor_divide(r - blk, CHUNK)
        return (kchunk <= qchunk) & (kchunk >= qchunk - LEFT_CHUNKS)

    return _toeplitz_tiles(
        lambda x: rel_bias.astype(F32)[:, jnp.clip(-x - blk, -REL_CLIP, REL_CLIP) + REL_CLIP] * LOG2E, valid,
        N_HEADS_B, 2 * blk, blk)


def _band_attention(proj, rel_bias):
    seq = proj.shape[0]
    blk = BLK_B
    bias = _band_bias_tiles(rel_bias)
    npair = N_HEADS_B // 2
    qc0 = 3 * N_HEADS_A
    prev = lambda c0: (lambda hp, i: (jnp.maximum(i - 1, 0), c0 + hp))
    cur = lambda c0: (lambda hp, i: (i, c0 + hp))
    return pl.pallas_call(
        _band_kernel,
        grid=(npair, seq // blk),
        in_specs=[
            pl.BlockSpec((blk, LANES), cur(qc0)),
            pl.BlockSpec((blk, LANES), prev(qc0 + npair)),
            pl.BlockSpec((blk, LANES), cur(qc0 + npair)),
            pl.BlockSpec((blk, LANES), prev(qc0 + 2 * npair)),
            pl.BlockSpec((blk, LANES), cur(qc0 + 2 * npair)),
            pl.BlockSpec((2, 2 * blk, blk), lambda hp, i: (hp, 0, 0)),
        ],
        out_specs=pl.BlockSpec((blk, LANES), lambda hp, i: (i, hp)),
        out_shape=jax.ShapeDtypeStruct((seq, N_HEADS_B * DH_B), BF16),
        scratch_shapes=[pltpu.VMEM((3 * blk // 2, blk), F32)] * 2,
        compiler_params=pltpu.CompilerParams(dimension_semantics=("parallel", "arbitrary")),
        name="band_attention",
    )(proj, proj, proj, proj, proj, bias)


def _retention_kernel(qk_ref, v_ref, gate_ref, cos_ref, sin_ref, qdec_ref, kdec_ref, dmat_ref,
                      sdec_ref, o_ref, state_ref):
    @pl.when(pl.program_id(0) == 0)
    def _():
        state_ref[...] = jnp.zeros(state_ref.shape, F32)

    cos = cos_ref[...]
    sin = sin_ref[...]
    lane = lax.broadcasted_iota(jnp.int32, cos.shape, 1)
    first_half = (lane % DQK_C) < (DQK_C // 2)
    qk = qk_ref[...]
    parts = []
    for j in range(qk.shape[1] // LANES):
        t = qk[:, j * LANES:(j + 1) * LANES]
        partner = jnp.where(first_half, pltpu.roll(t, LANES - DQK_C // 2, 1), pltpu.roll(t, DQK_C // 2, 1))
        parts.append(t * cos + partner * sin)
    wq = N_HEADS_C * DQK_C
    q = jnp.concatenate(parts[:wq // LANES], axis=1)
    k = jnp.concatenate(parts[wq // LANES:], axis=1) * (DQK_C ** -0.5)
    qd = (q * qdec_ref[...]).astype(BF16)
    kd = (k * kdec_ref[...]).astype(BF16)
    qb = q.astype(BF16)
    kb = k.astype(BF16)
    vb = v_ref[...].astype(BF16)
    gate = gate_ref[...]
    outs = []
    for h in range(N_HEADS_C):
        qs = slice(h * DQK_C, (h + 1) * DQK_C)
        vs = slice(h * DV_C, (h + 1) * DV_C)
        scores = _dot_nt(qb[:, qs], kb[:, qs]) * dmat_ref[h]
        state = state_ref[h]
        r = _dot(scores.astype(BF16), vb[:, vs]) + _dot(qd[:, qs], state.astype(BF16))
        state_ref[h] = state * sdec_ref[h] + _dot_tn(kd[:, qs], vb[:, vs])
        r = r * lax.rsqrt(jnp.mean(r * r, axis=-1, keepdims=True) + EPS)
        g = gate[:, vs]
        outs.append(r * (g * jax.nn.sigmoid(g)))
    o_ref[...] = jnp.concatenate(outs, axis=1).astype(o_ref.dtype)


def _retention_tables(seq):
    t = BLK_C
    half = DQK_C // 2
    inv_freq = 1.0 / (ROPE_BASE ** (jnp.arange(0, DQK_C, 2, dtype=F32) / DQK_C))
    ang = jnp.arange(seq, dtype=F32)[:, None] * inv_freq[None, :]
    reps = LANES // half
    cos = jnp.tile(jnp.cos(ang), (1, reps))
    sign = jnp.where((jnp.arange(LANES) % DQK_C) < half, -1.0, 1.0).astype(F32)
    sin = jnp.tile(jnp.sin(ang), (1, reps)) * sign[None, :]
    log_g = jnp.log(1.0 - jnp.power(2.0, -5.0 - jnp.arange(N_HEADS_C, dtype=F32)))
    pos = jnp.arange(t, dtype=F32)
    diff = pos[:, None] - pos[None, :]
    same_or_past = (jnp.arange(t)[None, :] // CHUNK) <= (jnp.arange(t)[:, None] // CHUNK)
    dmat = jnp.where(same_or_past[None], jnp.exp(log_g[:, None, None] * jnp.abs(diff)[None]), 0.0)
    qdec = jnp.repeat(jnp.exp(log_g[None, :] * (pos[:, None] + 1.0)), DQK_C, axis=1)
    kdec = jnp.repeat(jnp.exp(log_g[None, :] * (t - 1.0 - pos[:, None])), DQK_C, axis=1)
    sdec = jnp.broadcast_to(jnp.exp(log_g * t)[:, None, None], (N_HEADS_C, 1, DV_C))
    return cos, sin, qdec, kdec, dmat, sdec


def _retention(proj):
    seq = proj.shape[0]
    t = BLK_C
    cos, sin, qdec, kdec, dmat, sdec = _retention_tables(seq)
    wv = N_HEADS_C * DV_C
    return pl.pallas_call(
        _retention_kernel,
        grid=(seq // t,),
        in_specs=[
            pl.BlockSpec((t, wv), lambda i: (i, 0)),
            pl.BlockSpec((t, wv), lambda i: (i, 1)),
            pl.BlockSpec((t, wv), lambda i: (i, 2)),
            pl.BlockSpec((t, LANES), lambda i: (i, 0)),
            pl.BlockSpec((t, LANES), lambda i: (i, 0)),
            pl.BlockSpec((t, N_HEADS_C * DQK_C), lambda i: (0, 0)),
            pl.BlockSpec((t, N_HEADS_C * DQK_C), lambda i: (0, 0)),
            pl.BlockSpec((N_HEADS_C, t, t), lambda i: (0, 0, 0)),
            pl.BlockSpec((N_HEADS_C, 1, DV_C), lambda i: (0, 0, 0)),
        ],
        out_specs=pl.BlockSpec((t, wv), lambda i: (i, 0)),
        out_shape=jax.ShapeDtypeStruct((seq, wv), BF16),
        scratch_shapes=[pltpu.VMEM((N_HEADS_C, DQK_C, DV_C), F32)],
        compiler_params=pltpu.CompilerParams(dimension_semantics=("arbitrary",)),
        name="retention",
    )(proj, proj, proj, cos, sin, qdec, kdec, dmat, sdec)


def _s5_kernel(*refs):
    ncb = S5_CH // LANES
    u_refs = refs[:ncb]
    (mt_ref, bt_ref, ctr_ref, cti_ref, are_ref, aim_ref, y_ref,
     ut_ref, yt_ref, ys_ref, vr_ref, vi_ref, spr_ref, spi_ref, carry_ref) = refs[ncb:]
    tc = S5_TC
    gp = S5_GROUP
    n = S5_STATE
    ng = S5_GROUPS

    @pl.when(pl.program_id(0) == 0)
    def _():
        carry_ref[...] = jnp.zeros(carry_ref.shape, F32)

    for s in range(S5_T):
        for k in range(ncb):
            ut_ref[s, k * LANES:(k + 1) * LANES, :] = u_refs[k][pl.ds(s, tc, stride=S5_T), :].T

    unroll = 4

    def intra(it, carry):
        for k in range(unroll):
            g = it * unroll + k
            r0 = pl.multiple_of(g * gp, gp)
            ug = ut_ref[:, pl.ds(r0, gp), :].reshape(S5_T * gp, tc).astype(BF16)
            yt_ref[:, pl.ds(r0, gp), :] = _dot(mt_ref[g], ug).reshape(S5_T, gp, tc)
            vt = _dot(bt_ref[g], ug)
            n0 = pl.multiple_of(g * n, n)
            vr_ref[pl.ds(n0, n), :] = vt[0:n]
            vi_ref[pl.ds(n0, n), :] = vt[n:2 * n]
        return carry

    lax.fori_loop(0, ng // unroll, intra, 0)

    sub = 8
    nv = tc // sub
    row = lax.broadcasted_iota(jnp.int32, (tc, LANES), 0)
    in_vreg = lax.rem(row, sub)

    def rows_of(v, r):
        return jnp.broadcast_to(v[r:r + 1], (tc, LANES))

    for j in range(ng * n // LANES):
        cols = slice(j * LANES, (j + 1) * LANES)
        pwr, pwi = are_ref[:, cols], aim_ref[:, cols]
        xr = vr_ref[cols, :].T
        xi = vi_ref[cols, :].T
        for d in (1, 2, 4):
            keep = in_vreg >= d
            sr = jnp.where(keep, pltpu.roll(xr, d, 0), 0.0)
            si = jnp.where(keep, pltpu.roll(xi, d, 0), 0.0)
            fr, fi = rows_of(pwr, d - 1), rows_of(pwi, d - 1)
            xr, xi = xr + (fr * sr - fi * si), xi + (fr * si + fi * sr)
        cr, ci = carry_ref[0, :, cols], carry_ref[1, :, cols]
        cr0, ci0 = cr, ci
        outr, outi = [], []
        for v in range(nv):
            yr = xr[v * sub:(v + 1) * sub] + (pwr * cr - pwi * ci)
            yi = xi[v * sub:(v + 1) * sub] + (pwr * ci + pwi * cr)
            outr.append(yr)
            outi.append(yi)
            cr = jnp.broadcast_to(yr[sub - 1:sub], (sub, LANES))
            ci = jnp.broadcast_to(yi[sub - 1:sub], (sub, LANES))
        carry_ref[0, :, cols] = cr
        carry_ref[1, :, cols] = ci
        sr = jnp.concatenate(outr, axis=0)
        si = jnp.concatenate(outi, axis=0)
        first = row == 0
        spr_ref[j] = jnp.where(first, rows_of(cr0, 0), pltpu.roll(sr, 1, 0))
        spi_ref[j] = jnp.where(first, rows_of(ci0, 0), pltpu.roll(si, 1, 0))

    def cross(it, carry):
        for k in range(unroll):
            jp = it * unroll + k
            r0 = pl.multiple_of(jp * 2 * gp, 2 * gp)
            yc = (_dot_nt(ctr_ref[jp], spr_ref[jp].astype(BF16))
                  + _dot_nt(cti_ref[jp], spi_ref[jp].astype(BF16)))
            yt_ref[:, pl.ds(r0, 2 * gp), :] += yc.reshape(S5_T, 2 * gp, tc)
        return carry

    lax.fori_loop(0, ng // 2 // unroll, cross, 0)

    for s in range(S5_T):
        for k in range(ncb):
            ys_ref[k, pl.ds(s, tc, stride=S5_T), :] = yt_ref[s, k * LANES:(k + 1) * LANES, :].T
    for k in range(ncb):
        y_ref[:, k * LANES:(k + 1) * LANES] = ys_ref[k]


def _s5_matrices(lam_re, lam_im, log_step, b_re, b_im, c_re, c_im, d_skip):
    hi = lax.Precision.HIGHEST
    t, gp, n, ng = S5_T, S5_GROUP, S5_STATE, S5_GROUPS
    lam = lax.complex(lam_re.astype(F32), lam_im.astype(F32))
    step = jnp.exp(log_step.astype(F32))[:, None]
    ls = lam * step
    a_bar = jnp.exp(ls)
    b_bar = ((a_bar - 1.0) / lam)[..., None] * lax.complex(b_re.astype(F32), b_im.astype(F32))
    cm = lax.complex(c_re.astype(F32), c_im.astype(F32))

    def apow(k):
        kk = k.astype(F32).astype(jnp.complex64)
        return jnp.exp(ls.reshape((ng,) + (1,) * k.ndim + (n,)) * kk[None, ..., None])

    tt = jnp.arange(t)
    kmat = jnp.einsum('gpn,gln,gnq->glpq', cm, apow(tt), b_bar, precision=hi).real
    krev = jnp.transpose(kmat[:, ::-1], (0, 2, 1, 3)).reshape(ng, gp, t * gp)
    kpad = jnp.pad(krev, ((0, 0), (0, 0), (0, t * gp)))
    mt = jnp.concatenate([kpad[:, :, (t - 1 - to) * gp:(2 * t - 1 - to) * gp] for to in range(t)], axis=1)
    dvec = jnp.tile(d_skip.astype(F32).reshape(ng, 1, gp), (1, t, 1)).reshape(ng, t * gp)
    mt = mt + jnp.eye(t * gp, dtype=F32)[None] * dvec[:, :, None]
    z = jnp.swapaxes(apow(t - 1 - tt), 1, 2)[:, :, :, None] * b_bar[:, :, None, :]
    z = z.reshape(ng, n, t * gp)
    bt = jnp.concatenate([z.real, z.imag], axis=1)
    w = cm[:, None, :, :] * apow(tt + 1)[:, :, None, :]

    def pair_readout(x):
        x = x.reshape(ng // 2, 2, t, gp, n)
        first = jnp.pad(x[:, 0], ((0, 0), (0, 0), (0, 0), (0, n)))
        second = jnp.pad(x[:, 1], ((0, 0), (0, 0), (0, 0), (n, 0)))
        return jnp.stack([first, second], axis=2).reshape(ng // 2, t * 2 * gp, 2 * n).astype(BF16)

    ctr, cti = pair_readout(w.real), pair_readout(-w.imag)
    a_chunk = jnp.transpose(apow(t * (jnp.arange(8) + 1)), (1, 0, 2)).reshape(8, ng * n)
    return mt.astype(BF16), bt.astype(BF16), ctr, cti, a_chunk.real, a_chunk.imag


def _s5(proj, mats):
    seq, width = proj.shape
    t, tc, gp, n, ng = S5_T, S5_TC, S5_GROUP, S5_STATE, S5_GROUPS
    rows = t * tc
    ncb = S5_CH // LANES
    cb0 = (width - S5_CH) // LANES
    u_specs = [pl.BlockSpec((rows, LANES), (lambda i, k=k: (i, cb0 + k))) for k in range(ncb)]
    nsb = ng * n // LANES
    return pl.pallas_call(
        _s5_kernel,
        grid=(seq // rows,),
        in_specs=u_specs + [_const_spec(m.shape) for m in mats],
        out_specs=pl.BlockSpec((rows, S5_CH), lambda i: (i, 0)),
        out_shape=jax.ShapeDtypeStruct((seq, S5_CH), F32),
        scratch_shapes=[
            pltpu.VMEM((t, S5_CH, tc), F32),
            pltpu.VMEM((t, S5_CH, tc), F32),
            pltpu.VMEM((ncb, rows, LANES), F32),
            pltpu.VMEM((ng * n, tc), F32),
            pltpu.VMEM((ng * n, tc), F32),
            pltpu.VMEM((nsb, tc, LANES), F32),
            pltpu.VMEM((nsb, tc, LANES), F32),
            pltpu.VMEM((2, 8, ng * n), F32),
        ],
        compiler_params=pltpu.CompilerParams(dimension_semantics=("arbitrary",)),
        name="s5_scan",
    )(*([proj] * ncb), *mats)


def _mix_ffn_kernel(*refs, glu, final):
    (x_ref, a_ref, b_ref, wo_ref, g1_ref), refs = refs[:5], refs[5:]
    if glu:
        gw_ref, refs = refs[0], refs[1:]
    (g_ref, sc_ref, sh_ref, gate_ref, win_ref, cw_ref, cb_ref, wout_ref), refs = refs[:8], refs[8:]
    if final:
        fg_ref, o_ref, h_ref, act_ref, gbuf_ref, carry_ref = refs
    else:
        ng_ref, nsc_ref, nsh_ref, o_ref, hn_ref, h_ref, act_ref, gbuf_ref, carry_ref = refs
    tm = x_ref.shape[0]
    halo = gbuf_ref.shape[0] - tm

    @pl.when(pl.program_id(0) == 0)
    def _():
        carry_ref[...] = jnp.zeros(carry_ref.shape, F32)

    if glu:
        y = jax.nn.gelu(b_ref[...]).astype(BF16)
        gg = _dot(y, gw_ref[...])
        half = gg.shape[1] // 2
        b = (gg[:, :half] * jax.nn.sigmoid(gg[:, half:])).astype(BF16)
    else:
        b = b_ref[...]
    cat = jnp.concatenate([a_ref[...], b], axis=1)
    x = x_ref[...] + g1_ref[...] * _dot(cat, wo_ref[...])
    h_ref[...] = _mod_rmsnorm(x, g_ref[...], sc_ref[...], sh_ref[...]).astype(BF16)
    for f in range(D_FF // TF_FFN):
        cs = slice(f * TF_FFN, (f + 1) * TF_FFN)
        gs = slice(D_FF + f * TF_FFN, D_FF + (f + 1) * TF_FFN)
        h = h_ref[...]
        val = _dot(h, win_ref[:, cs])
        gate = _dot(h, win_ref[:, gs])
        gbuf_ref[0:halo, :] = carry_ref[:, cs]
        gbuf_ref[halo:halo + tm, :] = gate
        carry_ref[:, cs] = gate[tm - halo:tm, :]
        conv = (gate * cw_ref[2:3, cs] + gbuf_ref[halo - 1:halo - 1 + tm, :] * cw_ref[1:2, cs]
                + gbuf_ref[halo - 2:halo - 2 + tm, :] * cw_ref[0:1, cs] + cb_ref[:, cs])
        act_ref[:, cs] = (jax.nn.gelu(conv) * val).astype(BF16)
    xn = x + gate_ref[...] * _dot(act_ref[...], wout_ref[...])
    if final:
        xn = xn * lax.rsqrt(jnp.mean(xn * xn, axis=-1, keepdims=True) + EPS) * fg_ref[...]
    else:
        hn_ref[...] = _mod_rmsnorm(xn, ng_ref[...], nsc_ref[...], nsh_ref[...]).astype(BF16)
    o_ref[...] = xn


def _layer_spec(shape, layer):
    idx = (layer,) + (0,) * (len(shape) - 1)
    return pl.BlockSpec((None,) + tuple(shape[1:]), lambda *_: idx, pipeline_mode=pl.Buffered(1))


def _mix_ffn(x, a, b, wo, gate1, glu_w, g, scale, shift, gate2, w_in, conv_w, conv_b, w_out, tail, layer):
    seq, d = x.shape
    final = len(tail) == 1
    tm = TM_FFN
    halo = 8
    row = pl.BlockSpec((1, d), lambda i: (0, 0))
    rows = lambda w: pl.BlockSpec((tm, w), lambda i: (i, 0))
    conv_b = conv_b.reshape(conv_b.shape[0], 1, D_FF)
    in_specs = [rows(d), rows(a.shape[1]), rows(b.shape[1]), _const_spec(wo.shape), row]
    args = [x, a, b, wo, gate1]
    if glu_w is not None:
        in_specs.append(_const_spec(glu_w.shape))
        args.append(glu_w)
    in_specs += [
        row, row, row, row,
        _layer_spec(w_in.shape, layer),
        _layer_spec(conv_w.shape, layer),
        _layer_spec(conv_b.shape, layer),
        _layer_spec(w_out.shape, layer),
    ] + [row] * len(tail)
    args += [g.reshape(1, d), scale, shift, gate2, w_in, conv_w, conv_b, w_out]
    args += [t.reshape(1, d) for t in tail]
    out_specs = [rows(d)] if final else [rows(d), rows(d)]
    out_shape = [jax.ShapeDtypeStruct((seq, d), F32)] + ([] if final else [jax.ShapeDtypeStruct((seq, d), BF16)])
    return pl.pallas_call(
        functools.partial(_mix_ffn_kernel, glu=glu_w is not None, final=final),
        grid=(seq // tm,),
        in_specs=in_specs,
        out_specs=out_specs,
        out_shape=out_shape,
        scratch_shapes=[
            pltpu.VMEM((tm, d), BF16),
            pltpu.VMEM((tm, D_FF), BF16),
            pltpu.VMEM((tm + halo, TF_FFN), F32),
            pltpu.VMEM((halo, D_FF), F32),
        ],
        compiler_params=pltpu.CompilerParams(dimension_semantics=("arbitrary",)),
        name="mix_ffn",
    )(*args)


def kernel(x, c, t5_table, mod_w, mod_b, norm1_g, norm2_g, ffn_w_in, ffn_conv_w, ffn_conv_b, ffn_w_out,
           ev_w_in, ev_w_out, diff_lambda, diff_subln_g, band_rel_bias,
           od_w_in, od_w_out, s5_lam_re, s5_lam_im, s5_log_step, s5_b_re, s5_b_im, s5_c_re, s5_c_im,
           s5_d, s5_glu_w, final_g):
    assert x.shape[0] == 1 and x.shape[2] == D_MODEL
    seq = x.shape[1]
    assert seq % TM_PROJ == 0 and seq % (S5_T * S5_TC) == 0
    d = D_MODEL
    xs = x[0]
    mod = _modulation(c, mod_w, mod_b)
    ffn_w_in_b = ffn_w_in.astype(BF16)
    ffn_w_out_b = ffn_w_out.astype(BF16)
    mods = [[mod[i, :, k * d:(k + 1) * d] for k in range(6)] for i in range(DEPTH)]
    h = None
    for i in range(DEPTH):
        sh1, sc1, g1, sh2, sc2, g2 = mods[i]
        w_in = (ev_w_in if i % 2 == 0 else od_w_in)[i // 2].astype(BF16)
        proj_dtype = BF16 if i % 2 == 0 else F32
        if h is None:
            proj = _normproj(xs, norm1_g[i], sc1, sh1, w_in, proj_dtype)
        else:
            proj = _proj(h, w_in, proj_dtype)
        if i % 2 == 0:
            e = i // 2
            lam_init = 0.8 - 0.6 * math.exp(-0.3 * i)
            lp = diff_lambda[e].astype(F32)
            lam = jnp.exp(jnp.sum(lp[0] * lp[1])) - jnp.exp(jnp.sum(lp[2] * lp[3])) + lam_init
            mix_a = _diff_attention(proj, t5_table, lam, diff_subln_g[e], lam_init)
            mix_b = _band_attention(proj, band_rel_bias[e])
            wo, glu_w = ev_w_out[e].astype(BF16), None
        else:
            o = i // 2
            mix_a = _retention(proj)
            mats = _s5_matrices(s5_lam_re[o], s5_lam_im[o], s5_log_step[o], s5_b_re[o], s5_b_im[o],
                                s5_c_re[o], s5_c_im[o], s5_d[o])
            mix_b = _s5(proj, mats)
            wo, glu_w = od_w_out[o].astype(BF16), s5_glu_w[o].astype(BF16)
        if i == DEPTH - 1:
            tail = (final_g,)
        else:
            nsh1, nsc1 = mods[i + 1][0], mods[i + 1][1]
            tail = (norm1_g[i + 1], nsc1, nsh1)
        out = _mix_ffn(xs, mix_a, mix_b, wo, g1, glu_w, norm2_g[i], sc2, sh2, g2,
                       ffn_w_in_b, ffn_conv_w, ffn_conv_b, ffn_w_out_b, tail, layer=i)
        if i == DEPTH - 1:
            xs = out[0]
        else:
            xs, h = out
    return xs[None]
```

```python
import functools
import math

import jax
import jax.numpy as jnp
from jax import lax
from jax.experimental import pallas as pl
from jax.experimental.pallas import tpu as pltpu

F32 = jnp.float32
BF16 = jnp.bfloat16

D_MODEL = 1024
DEPTH = 2
CHUNK = 64
GROUP_WIDTH = D_MODEL // 2
DK_A = 64
DV_A = 2 * DK_A
N_HEADS_A = GROUP_WIDTH // DV_A
DH_B = 64
N_HEADS_B = GROUP_WIDTH // DH_B
LEFT_CHUNKS = 8
REL_CLIP = 2 * CHUNK
NUM_BUCKETS = 32
MAX_DISTANCE = 128
DV_C = 128
DQK_C = DV_C // 2
N_HEADS_C = GROUP_WIDTH // DV_C
ROPE_BASE = 10000.0
S5_CH = GROUP_WIDTH
S5_GROUP = 16
S5_GROUPS = S5_CH // S5_GROUP
S5_STATE = 64
D_FF = ((8 * D_MODEL // 3 + 255) // 256) * 256
CONV_W = 3
EVEN_IN = 3 * N_HEADS_A * DV_A + 3 * N_HEADS_B * DH_B
ODD_IN = 2 * N_HEADS_C * DQK_C + 2 * N_HEADS_C * DV_C + S5_CH
EPS = 1e-6
NEG_INF = -1e30
LOG2E = math.log2(math.e)

LANES = 128
MXU_DIM = 256

TM_PROJ = 1024
TM_FFN = 512
TF_FFN = MXU_DIM
BLK_A = 512
NPART_A = 2
BLK_B = 512
BLK_C = 256
S5_T = 16
S5_TC = LANES

assert BLK_B == LEFT_CHUNKS * CHUNK, "band window must be exactly one previous block"
assert BLK_A >= MAX_DISTANCE, "far key blocks must sit in the saturated T5 bucket"
assert DV_A == LANES, "diff-attention statistics are kept lane-replicated beside the accumulator"


def _dot(a, b):
    return jnp.dot(a, b, preferred_element_type=F32)


def _dot_nt(a, b):
    return lax.dot_general(a, b, (((1,), (1,)), ((), ())), preferred_element_type=F32)


def _dot_tn(a, b):
    return lax.dot_general(a, b, (((0,), (0,)), ((), ())), preferred_element_type=F32)


def _const_spec(shape):
    zeros = (0,) * len(shape)
    return pl.BlockSpec(shape, lambda *_: zeros, pipeline_mode=pl.Buffered(1))


def _mod_rmsnorm(x, g, scale, shift):
    y = x * lax.rsqrt(jnp.mean(x * x, axis=-1, keepdims=True) + EPS)
    y = y * g
    return y * (1.0 + scale) + shift


def _mod_kernel(c_ref, w_ref, b_ref, o_ref):
    c = c_ref[...]
    cond = c * jax.nn.sigmoid(c)
    o_ref[0] = jnp.sum(cond * w_ref[0], axis=0, keepdims=True) + b_ref[0]


def _modulation(c, mod_w, mod_b):
    depth, d, n = mod_w.shape
    tn = 1536
    return pl.pallas_call(
        _mod_kernel,
        grid=(depth, n // tn),
        in_specs=[
            pl.BlockSpec((d, 1), lambda i, j: (0, 0)),
            pl.BlockSpec((1, d, tn), lambda i, j: (i, 0, j)),
            pl.BlockSpec((1, 1, tn), lambda i, j: (i, 0, j)),
        ],
        out_specs=pl.BlockSpec((1, 1, tn), lambda i, j: (i, 0, j)),
        out_shape=jax.ShapeDtypeStruct((depth, 1, n), F32),
        name="modulation",
    )(c.reshape(d, 1), mod_w, mod_b.reshape(depth, 1, n))


TN_PROJ = 1024


def _normproj_kernel(x_ref, g_ref, sc_ref, sh_ref, w_ref, o_ref):
    tm, n = o_ref.shape
    half = tm // 2
    for r in range(2):
        rows = slice(r * half, (r + 1) * half)
        h = _mod_rmsnorm(x_ref[rows, :], g_ref[...], sc_ref[...], sh_ref[...]).astype(BF16)
        for j in range(n // TN_PROJ):
            cols = slice(j * TN_PROJ, (j + 1) * TN_PROJ)
            o_ref[rows, cols] = _dot(h, w_ref[:, cols]).astype(o_ref.dtype)


def _normproj(x, g, scale, shift, w, out_dtype):
    seq, d = x.shape
    n = w.shape[1]
    tm = TM_PROJ
    row = pl.BlockSpec((1, d), lambda i: (0, 0))
    return pl.pallas_call(
        _normproj_kernel,
        grid=(seq // tm,),
        in_specs=[pl.BlockSpec((tm, d), lambda i: (i, 0)), row, row, row, _const_spec(w.shape)],
        out_specs=pl.BlockSpec((tm, n), lambda i: (i, 0)),
        out_shape=jax.ShapeDtypeStruct((seq, n), out_dtype),
        compiler_params=pltpu.CompilerParams(dimension_semantics=("parallel",)),
        name="normproj",
    )(x, g.reshape(1, d), scale, shift, w)


def _proj_kernel(h_ref, w_ref, o_ref):
    for j in range(o_ref.shape[1] // TN_PROJ):
        cols = slice(j * TN_PROJ, (j + 1) * TN_PROJ)
        o_ref[:, cols] = _dot(h_ref[...], w_ref[:, cols]).astype(o_ref.dtype)


def _proj(h, w, out_dtype):
    seq, d = h.shape
    n = w.shape[1]
    tm = TM_PROJ
    return pl.pallas_call(
        _proj_kernel,
        grid=(seq // tm,),
        in_specs=[pl.BlockSpec((tm, d), lambda i: (i, 0)), _const_spec(w.shape)],
        out_specs=pl.BlockSpec((tm, n), lambda i: (i, 0)),
        out_shape=jax.ShapeDtypeStruct((seq, n), out_dtype),
        compiler_params=pltpu.CompilerParams(dimension_semantics=("parallel",)),
        name="proj",
    )(h, w)


def _diffattn_kernel(q_ref, k_ref, v_ref, bias_ref, lam_ref, g_ref, o_ref,
                     qs_ref, vt_ref, m_ref, l_ref, acc_ref, *s_refs, out_scale):
    blk = BLK_A
    nq = 2 * blk
    sub = 8
    npart = len(s_refs) // 4
    wq = nq // npart
    sa_ref, sb_ref = s_refs[:2 * npart], s_refs[2 * npart:]
    i = pl.program_id(1)

    @pl.when(i == 0)
    def _():
        def tr(b, carry):
            r0 = pl.multiple_of(b * blk, blk)
            vt_ref[:, pl.ds(r0, blk)] = v_ref[pl.ds(r0, blk), :].astype(F32).T.astype(BF16)
            return carry
        lax.fori_loop(0, v_ref.shape[0] // blk, tr, 0)

    q = q_ref[...].astype(F32) * (DK_A ** -0.5 * LOG2E)
    lane = lax.broadcasted_iota(jnp.int32, q.shape, 1)
    qs_ref[0:blk, :] = jnp.where(lane < DK_A, q, 0.0).astype(BF16)
    qs_ref[blk:nq, :] = jnp.where(lane >= DK_A, q, 0.0).astype(BF16)
    m_ref[...] = jnp.full(m_ref.shape, NEG_INF, F32)
    l_ref[...] = jnp.zeros(l_ref.shape, F32)
    acc_ref[...] = jnp.zeros(acc_ref.shape, F32)

    def scores(b, s_ref):
        k = k_ref[pl.ds(pl.multiple_of(b * blk, blk), blk), :]
        for part in range(npart):
            s = _dot_nt(k, qs_ref[part * wq:(part + 1) * wq, :])
            s_ref[part][...] = s
            s_ref[npart + part][...] = jnp.max(s.reshape(blk // sub, sub, wq), axis=0)

    def softmax_pv(b, s_ref, bias):
        vt = vt_ref[:, pl.ds(pl.multiple_of(b * blk, blk), blk)]
        for part in range(npart):
            cols = slice(part * wq, (part + 1) * wq)
            s = s_ref[part][...]
            if bias is not None:
                q0 = (part * wq) % blk
                s = s + bias[:, q0:q0 + wq]
            s = s.reshape(blk // sub, sub, wq)
            m_prev = m_ref[:, cols]
            smax = jnp.max(s, axis=0) if bias is not None else s_ref[npart + part][...]
            m_cur = jnp.max(smax, axis=0, keepdims=True)
            m_new = jnp.maximum(m_prev, m_cur)
            alpha = jnp.exp2(m_prev - m_new)
            p = jnp.exp2(s - m_new[None])
            l_ref[:, cols] = alpha * l_ref[:, cols] + jnp.sum(p, axis=0)
            pv = _dot(vt, p.reshape(blk, wq).astype(BF16))
            acc_ref[:, cols] = acc_ref[:, cols] * alpha[0:1] + pv
            m_ref[:, cols] = m_new

    nfar = jnp.maximum(i - 1, 0)
    odd = lax.rem(nfar, 2)

    @pl.when(i == 0)
    def _():
        scores(0, sb_ref)

    @pl.when(i > 0)
    def _():
        @pl.when(odd == 1)
        def _():
            scores(0, sb_ref)
            scores(1, sa_ref)
            softmax_pv(0, sb_ref, None)

        @pl.when(odd == 0)
        def _():
            scores(0, sa_ref)

        def pair(b):
            scores(b + 1, sb_ref)
            softmax_pv(b, sa_ref, None)
            scores(b + 2, sa_ref)
            softmax_pv(b + 1, sb_ref, None)

        def quad_body(t, carry):
            pair(odd + 4 * t)
            pair(odd + 4 * t + 2)
            return carry

        npairs = nfar // 2
        lax.fori_loop(0, npairs // 2, quad_body, 0)

        @pl.when(lax.rem(npairs, 2) == 1)
        def _():
            pair(odd + 2 * (npairs - 1))
        scores(i, sb_ref)
        softmax_pv(i - 1, sa_ref, bias_ref[0, 0])

    softmax_pv(i, sb_ref, bias_ref[0, 1])

    ot = acc_ref[...] / jnp.sum(l_ref[...], axis=0, keepdims=True)
    o = ot[:, 0:blk].T - lam_ref[...] * ot[:, blk:nq].T
    o = o * lax.rsqrt(jnp.mean(o * o, axis=-1, keepdims=True) + EPS) * g_ref[...]
    o_ref[...] = (o * out_scale).astype(o_ref.dtype)


_TOEPLITZ_ROWS = 512
_TOEPLITZ_N = 2048


def _toeplitz_kernel(v_ref, o_ref, *, keep):
    rows, cols = o_ref.shape[1:]
    x = jnp.broadcast_to(v_ref[0, 0], (rows, v_ref.shape[-1]))
    tile = pltpu.roll(x, 0, 1, stride=1, stride_axis=0)[:, :cols]
    r = lax.broadcasted_iota(jnp.int32, (rows, cols), 0) + pl.program_id(1) * rows
    c = lax.broadcasted_iota(jnp.int32, (rows, cols), 1)
    o_ref[0] = jnp.where(keep(r, c), tile, NEG_INF)


def _toeplitz_tiles(fn, keep, heads, rows, cols):
    n, rb = _TOEPLITZ_N, _TOEPLITZ_ROWS
    assert rows % rb == 0 and rows <= n // 2 and cols <= n // 2
    idx = jnp.arange(n, dtype=jnp.int32)
    vec = fn(jnp.where(idx < n // 2, idx, idx - n)).astype(F32)
    vecs = jnp.stack([jnp.roll(vec, k * rb, axis=1) for k in range(rows // rb)], axis=1)
    return pl.pallas_call(
        functools.partial(_toeplitz_kernel, keep=keep),
        grid=(heads, rows // rb),
        in_specs=[pl.BlockSpec((1, 1, 1, n), lambda h, k: (h, k, 0, 0))],
        out_specs=pl.BlockSpec((1, rb, cols), lambda h, k: (h, k, 0)),
        out_shape=jax.ShapeDtypeStruct((heads, rows, cols), F32),
        name="toeplitz_tiles",
    )(vecs.reshape(heads, rows // rb, 1, n))


def _t5_bucket(rel):
    nb = NUM_BUCKETS // 2
    max_exact = nb // 2
    bucket = jnp.where(rel > 0, nb, 0)
    n = jnp.abs(rel)
    nf = jnp.maximum(n, 1).astype(F32)
    large = max_exact + (jnp.log(nf / max_exact) / math.log(MAX_DISTANCE / max_exact)
                         * (nb - max_exact)).astype(jnp.int32)
    large = jnp.minimum(large, nb - 1)
    return bucket + jnp.where(n < max_exact, n, large)


def _diff_bias_tiles(t5_table):
    blk = BLK_A
    table = t5_table.astype(F32)
    far = table[_t5_bucket(jnp.full((), -(blk + 1), jnp.int32))]
    def visible(r, c):
        return jnp.floor_divide(r - blk, CHUNK) <= jnp.floor_divide(c, CHUNK)

    tiles = _toeplitz_tiles(lambda x: ((table[_t5_bucket(-x - blk)] - far) * LOG2E).T, visible,
                            N_HEADS_A, 2 * blk, blk)
    return tiles.reshape(N_HEADS_A, 2, blk, blk)


def _diff_attention(proj, t5_table, lam, subln_g, lam_init):
    seq = proj.shape[0]
    blk = BLK_A
    bias = _diff_bias_tiles(t5_table)
    ha = N_HEADS_A
    kern = functools.partial(_diffattn_kernel, out_scale=1.0 - lam_init)
    return pl.pallas_call(
        kern,
        grid=(ha, seq // blk),
        in_specs=[
            pl.BlockSpec((blk, DV_A), lambda h, i: (i, h)),
            pl.BlockSpec((seq, DV_A), lambda h, i: (0, ha + h)),
            pl.BlockSpec((seq, DV_A), lambda h, i: (0, 2 * ha + h)),
            pl.BlockSpec((1, 2, blk, blk), lambda h, i: (h, 0, 0, 0)),
            pl.BlockSpec((1, DV_A), lambda h, i: (0, 0)),
            pl.BlockSpec((1, DV_A), lambda h, i: (0, 0)),
        ],
        out_specs=pl.BlockSpec((blk, DV_A), lambda h, i: (i, h)),
        out_shape=jax.ShapeDtypeStruct((seq, ha * DV_A), BF16),
        scratch_shapes=[
            pltpu.VMEM((2 * blk, DV_A), BF16),
            pltpu.VMEM((DV_A, seq), BF16),
            pltpu.VMEM((8, 2 * blk), F32),
            pltpu.VMEM((8, 2 * blk), F32),
            pltpu.VMEM((DV_A, 2 * blk), F32),
        ] + 2 * ([pltpu.VMEM((blk, 2 * blk // NPART_A), F32)] * NPART_A
                 + [pltpu.VMEM((8, 2 * blk // NPART_A), F32)] * NPART_A),
        compiler_params=pltpu.CompilerParams(dimension_semantics=("parallel", "arbitrary")),
        name="diff_attention",
    )(proj, proj, proj, bias, jnp.full((1, DV_A), lam, F32), subln_g.reshape(1, DV_A).astype(F32))


def _band_kernel(q_ref, kp_ref, kc_ref, vp_ref, vc_ref, bias_ref, o_ref, *s_refs):
    blk = BLK_B
    half = blk // 2
    nk = 3 * half
    sub = 8
    i = pl.program_id(1)
    q = q_ref[...].astype(F32) * (DH_B ** -0.5 * LOG2E)
    lane = lax.broadcasted_iota(jnp.int32, q.shape, 1)
    qh = (jnp.where(lane < DH_B, q, 0.0).astype(BF16), jnp.where(lane >= DH_B, q, 0.0).astype(BF16))
    k_all = jnp.concatenate([kp_ref[...], kc_ref[...]], axis=0)
    vt_all = jnp.concatenate([vp_ref[...], vc_ref[...]], axis=0).astype(F32).T.astype(BF16)
    no_prev = jnp.where(i == 0, NEG_INF, 0.0).astype(F32)
    krow = lax.broadcasted_iota(jnp.int32, (nk, 2 * half), 0)
    for hf in range(2):
        k0 = hf * half
        qs = jnp.concatenate([qh[0][k0:k0 + half], qh[1][k0:k0 + half]], axis=0)
        s_refs[hf][...] = _dot_nt(k_all[k0:k0 + nk], qs)
    for hf in range(2):
        k0 = hf * half
        bias = jnp.concatenate([bias_ref[0, k0:k0 + nk, k0:k0 + half],
                                bias_ref[1, k0:k0 + nk, k0:k0 + half]], axis=1)
        s = s_refs[hf][...] + bias + jnp.where(krow < blk - k0, no_prev, 0.0)
        s = s.reshape(nk // sub, sub, 2 * half)
        m = jnp.max(jnp.max(s, axis=0), axis=0, keepdims=True)
        p = jnp.exp2(s - m[None])
        l = jnp.sum(jnp.sum(p, axis=0), axis=0, keepdims=True)
        ot = _dot(vt_all[:, k0:k0 + nk], p.reshape(nk, 2 * half).astype(BF16)) / l
        o = jnp.concatenate([ot[0:DH_B, 0:half], ot[DH_B:2 * DH_B, half:2 * half]], axis=0)
        o_ref[k0:k0 + half, :] = o.T.astype(o_ref.dtype)


def _band_bias_tiles(rel_bias):
    blk = BLK_B

    def valid(r, c):
        qchunk = jnp.floor_divide(c, CHUNK)
        kchunk = jnp.floor_divide(r - blk, CHUNK)
        return (kchunk <= qchunk) & (kchunk >= qchunk - LEFT_CHUNKS)

    return _toeplitz_tiles(
        lambda x: rel_bias.astype(F32)[:, jnp.clip(-x - blk, -REL_CLIP, REL_CLIP) + REL_CLIP] * LOG2E, valid,
        N_HEADS_B, 2 * blk, blk)


def _band_attention(proj, rel_bias):
    seq = proj.shape[0]
    blk = BLK_B
    bias = _band_bias_tiles(rel_bias)
    npair = N_HEADS_B // 2
    qc0 = 3 * N_HEADS_A
    prev = lambda c0: (lambda hp, i: (jnp.maximum(i - 1, 0), c0 + hp))
    cur = lambda c0: (lambda hp, i: (i, c0 + hp))
    return pl.pallas_call(
        _band_kernel,
        grid=(npair, seq // blk),
        in_specs=[
            pl.BlockSpec((blk, LANES), cur(qc0)),
            pl.BlockSpec((blk, LANES), prev(qc0 + npair)),
            pl.BlockSpec((blk, LANES), cur(qc0 + npair)),
            pl.BlockSpec((blk, LANES), prev(qc0 + 2 * npair)),
            pl.BlockSpec((blk, LANES), cur(qc0 + 2 * npair)),
            pl.BlockSpec((2, 2 * blk, blk), lambda hp, i: (hp, 0, 0)),
        ],
        out_specs=pl.BlockSpec((blk, LANES), lambda hp, i: (i, hp)),
        out_shape=jax.ShapeDtypeStruct((seq, N_HEADS_B * DH_B), BF16),
        scratch_shapes=[pltpu.VMEM((3 * blk // 2, blk), F32)] * 2,
        compiler_params=pltpu.CompilerParams(dimension_semantics=("parallel", "arbitrary")),
        name="band_attention",
    )(proj, proj, proj, proj, proj, bias)


def _retention_kernel(qk_ref, v_ref, gate_ref, cos_ref, sin_ref, qdec_ref, kdec_ref, dmat_ref,
                      sdec_ref, o_ref, state_ref):
    @pl.when(pl.program_id(0) == 0)
    def _():
        state_ref[...] = jnp.zeros(state_ref.shape, F32)

    cos = cos_ref[...]
    sin = sin_ref[...]
    lane = lax.broadcasted_iota(jnp.int32, cos.shape, 1)
    first_half = (lane % DQK_C) < (DQK_C // 2)
    qk = qk_ref[...]
    parts = []
    for j in range(qk.shape[1] // LANES):
        t = qk[:, j * LANES:(j + 1) * LANES]
        partner = jnp.where(first_half, pltpu.roll(t, LANES - DQK_C // 2, 1), pltpu.roll(t, DQK_C // 2, 1))
        parts.append(t * cos + partner * sin)
    wq = N_HEADS_C * DQK_C
    q = jnp.concatenate(parts[:wq // LANES], axis=1)
    k = jnp.concatenate(parts[wq // LANES:], axis=1) * (DQK_C ** -0.5)
    qd = (q * qdec_ref[...]).astype(BF16)
    kd = (k * kdec_ref[...]).astype(BF16)
    qb = q.astype(BF16)
    kb = k.astype(BF16)
    vb = v_ref[...].astype(BF16)
    gate = gate_ref[...]
    outs = []
    for h in range(N_HEADS_C):
        qs = slice(h * DQK_C, (h + 1) * DQK_C)
        vs = slice(h * DV_C, (h + 1) * DV_C)
        scores = _dot_nt(qb[:, qs], kb[:, qs]) * dmat_ref[h]
        state = state_ref[h]
        r = _dot(scores.astype(BF16), vb[:, vs]) + _dot(qd[:, qs], state.astype(BF16))
        state_ref[h] = state * sdec_ref[h] + _dot_tn(kd[:, qs], vb[:, vs])
        r = r * lax.rsqrt(jnp.mean(r * r, axis=-1, keepdims=True) + EPS)
        g = gate[:, vs]
        outs.append(r * (g * jax.nn.sigmoid(g)))
    o_ref[...] = jnp.concatenate(outs, axis=1).astype(o_ref.dtype)


def _retention_tables(seq):
    t = BLK_C
    half = DQK_C // 2
    inv_freq = 1.0 / (ROPE_BASE ** (jnp.arange(0, DQK_C, 2, dtype=F32) / DQK_C))
    ang = jnp.arange(seq, dtype=F32)[:, None] * inv_freq[None, :]
    reps = LANES // half
    cos = jnp.tile(jnp.cos(ang), (1, reps))
    sign = jnp.where((jnp.arange(LANES) % DQK_C) < half, -1.0, 1.0).astype(F32)
    sin = jnp.tile(jnp.sin(ang), (1, reps)) * sign[None, :]
    log_g = jnp.log(1.0 - jnp.power(2.0, -5.0 - jnp.arange(N_HEADS_C, dtype=F32)))
    pos = jnp.arange(t, dtype=F32)
    diff = pos[:, None] - pos[None, :]
    same_or_past = (jnp.arange(t)[None, :] // CHUNK) <= (jnp.arange(t)[:, None] // CHUNK)
    dmat = jnp.where(same_or_past[None], jnp.exp(log_g[:, None, None] * jnp.abs(diff)[None]), 0.0)
    qdec = jnp.repeat(jnp.exp(log_g[None, :] * (pos[:, None] + 1.0)), DQK_C, axis=1)
    kdec = jnp.repeat(jnp.exp(log_g[None, :] * (t - 1.0 - pos[:, None])), DQK_C, axis=1)
    sdec = jnp.broadcast_to(jnp.exp(log_g * t)[:, None, None], (N_HEADS_C, 1, DV_C))
    return cos, sin, qdec, kdec, dmat, sdec


def _retention(proj):
    seq = proj.shape[0]
    t = BLK_C
    cos, sin, qdec, kdec, dmat, sdec = _retention_tables(seq)
    wv = N_HEADS_C * DV_C
    return pl.pallas_call(
        _retention_kernel,
        grid=(seq // t,),
        in_specs=[
            pl.BlockSpec((t, wv), lambda i: (i, 0)),
            pl.BlockSpec((t, wv), lambda i: (i, 1)),
            pl.BlockSpec((t, wv), lambda i: (i, 2)),
            pl.BlockSpec((t, LANES), lambda i: (i, 0)),
            pl.BlockSpec((t, LANES), lambda i: (i, 0)),
            pl.BlockSpec((t, N_HEADS_C * DQK_C), lambda i: (0, 0)),
            pl.BlockSpec((t, N_HEADS_C * DQK_C), lambda i: (0, 0)),
            pl.BlockSpec((N_HEADS_C, t, t), lambda i: (0, 0, 0)),
            pl.BlockSpec((N_HEADS_C, 1, DV_C), lambda i: (0, 0, 0)),
        ],
        out_specs=pl.BlockSpec((t, wv), lambda i: (i, 0)),
        out_shape=jax.ShapeDtypeStruct((seq, wv), BF16),
        scratch_shapes=[pltpu.VMEM((N_HEADS_C, DQK_C, DV_C), F32)],
        compiler_params=pltpu.CompilerParams(dimension_semantics=("arbitrary",)),
        name="retention",
    )(proj, proj, proj, cos, sin, qdec, kdec, dmat, sdec)


def _s5_kernel(*refs):
    ncb = S5_CH // LANES
    u_refs = refs[:ncb]
    (mt_ref, bt_ref, ctr_ref, cti_ref, are_ref, aim_ref, y_ref,
     ut_ref, yt_ref, ys_ref, vr_ref, vi_ref, spr_ref, spi_ref, carry_ref) = refs[ncb:]
    tc = S5_TC
    gp = S5_GROUP
    n = S5_STATE
    ng = S5_GROUPS

    @pl.when(pl.program_id(0) == 0)
    def _():
        carry_ref[...] = jnp.zeros(carry_ref.shape, F32)

    for s in range(S5_T):
        for k in range(ncb):
            ut_ref[s, k * LANES:(k + 1) * LANES, :] = u_refs[k][pl.ds(s, tc, stride=S5_T), :].T

    unroll = 4

    def intra(it, carry):
        for k in range(unroll):
            g = it * unroll + k
            r0 = pl.multiple_of(g * gp, gp)
            ug = ut_ref[:, pl.ds(r0, gp), :].reshape(S5_T * gp, tc).astype(BF16)
            yt_ref[:, pl.ds(r0, gp), :] = _dot(mt_ref[g], ug).reshape(S5_T, gp, tc)
            vt = _dot(bt_ref[g], ug)
            n0 = pl.multiple_of(g * n, n)
            vr_ref[pl.ds(n0, n), :] = vt[0:n]
            vi_ref[pl.ds(n0, n), :] = vt[n:2 * n]
        return carry

    lax.fori_loop(0, ng // unroll, intra, 0)

    sub = 8
    nv = tc // sub
    row = lax.broadcasted_iota(jnp.int32, (tc, LANES), 0)
    in_vreg = lax.rem(row, sub)

    def rows_of(v, r):
        return jnp.broadcast_to(v[r:r + 1], (tc, LANES))

    for j in range(ng * n // LANES):
        cols = slice(j * LANES, (j + 1) * LANES)
        pwr, pwi = are_ref[:, cols], aim_ref[:, cols]
        xr = vr_ref[cols, :].T
        xi = vi_ref[cols, :].T
        for d in (1, 2, 4):
            keep = in_vreg >= d
            sr = jnp.where(keep, pltpu.roll(xr, d, 0), 0.0)
            si = jnp.where(keep, pltpu.roll(xi, d, 0), 0.0)
            fr, fi = rows_of(pwr, d - 1), rows_of(pwi, d - 1)
            xr, xi = xr + (fr * sr - fi * si), xi + (fr * si + fi * sr)
        cr, ci = carry_ref[0, :, cols], carry_ref[1, :, cols]
        cr0, ci0 = cr, ci
        outr, outi = [], []
        for v in range(nv):
            yr = xr[v * sub:(v + 1) * sub] + (pwr * cr - pwi * ci)
            yi = xi[v * sub:(v + 1) * sub] + (pwr * ci + pwi * cr)
            outr.append(yr)
            outi.append(yi)
            cr = jnp.broadcast_to(yr[sub - 1:sub], (sub, LANES))
            ci = jnp.broadcast_to(yi[sub - 1:sub], (sub, LANES))
        carry_ref[0, :, cols] = cr
        carry_ref[1, :, cols] = ci
        sr = jnp.concatenate(outr, axis=0)
        si = jnp.concatenate(outi, axis=0)
        first = row == 0
        spr_ref[j] = jnp.where(first, rows_of(cr0, 0), pltpu.roll(sr, 1, 0))
        spi_ref[j] = jnp.where(first, rows_of(ci0, 0), pltpu.roll(si, 1, 0))

    def cross(it, carry):
        for k in range(unroll):
            jp = it * unroll + k
            r0 = pl.multiple_of(jp * 2 * gp, 2 * gp)
            yc = (_dot_nt(ctr_ref[jp], spr_ref[jp].astype(BF16))
                  + _dot_nt(cti_ref[jp], spi_ref[jp].astype(BF16)))
            yt_ref[:, pl.ds(r0, 2 * gp), :] += yc.reshape(S5_T, 2 * gp, tc)
        return carry

    lax.fori_loop(0, ng // 2 // unroll, cross, 0)

    for s in range(S5_T):
        for k in range(ncb):
            ys_ref[k, pl.ds(s, tc, stride=S5_T), :] = yt_ref[s, k * LANES:(k + 1) * LANES, :].T
    for k in range(ncb):
        y_ref[:, k * LANES:(k + 1) * LANES] = ys_ref[k]


def _s5_matrices(lam_re, lam_im, log_step, b_re, b_im, c_re, c_im, d_skip):
    hi = lax.Precision.HIGHEST
    t, gp, n, ng = S5_T, S5_GROUP, S5_STATE, S5_GROUPS
    lam = lax.complex(lam_re.astype(F32), lam_im.astype(F32))
    step = jnp.exp(log_step.astype(F32))[:, None]
    ls = lam * step
    a_bar = jnp.exp(ls)
    b_bar = ((a_bar - 1.0) / lam)[..., None] * lax.complex(b_re.astype(F32), b_im.astype(F32))
    cm = lax.complex(c_re.astype(F32), c_im.astype(F32))

    def apow(k):
        kk = k.astype(F32).astype(jnp.complex64)
        return jnp.exp(ls.reshape((ng,) + (1,) * k.ndim + (n,)) * kk[None, ..., None])

    tt = jnp.arange(t)
    kmat = jnp.einsum('gpn,gln,gnq->glpq', cm, apow(tt), b_bar, precision=hi).real
    krev = jnp.transpose(kmat[:, ::-1], (0, 2, 1, 3)).reshape(ng, gp, t * gp)
    kpad = jnp.pad(krev, ((0, 0), (0, 0), (0, t * gp)))
    mt = jnp.concatenate([kpad[:, :, (t - 1 - to) * gp:(2 * t - 1 - to) * gp] for to in range(t)], axis=1)
    dvec = jnp.tile(d_skip.astype(F32).reshape(ng, 1, gp), (1, t, 1)).reshape(ng, t * gp)
    mt = mt + jnp.eye(t * gp, dtype=F32)[None] * dvec[:, :, None]
    z = jnp.swapaxes(apow(t - 1 - tt), 1, 2)[:, :, :, None] * b_bar[:, :, None, :]
    z = z.reshape(ng, n, t * gp)
    bt = jnp.concatenate([z.real, z.imag], axis=1)
    w = cm[:, None, :, :] * apow(tt + 1)[:, :, None, :]

    def pair_readout(x):
        x = x.reshape(ng // 2, 2, t, gp, n)
        first = jnp.pad(x[:, 0], ((0, 0), (0, 0), (0, 0), (0, n)))
        second = jnp.pad(x[:, 1], ((0, 0), (0, 0), (0, 0), (n, 0)))
        return jnp.stack([first, second], axis=2).reshape(ng // 2, t * 2 * gp, 2 * n).astype(BF16)

    ctr, cti = pair_readout(w.real), pair_readout(-w.imag)
    a_chunk = jnp.transpose(apow(t * (jnp.arange(8) + 1)), (1, 0, 2)).reshape(8, ng * n)
    return mt.astype(BF16), bt.astype(BF16), ctr, cti, a_chunk.real, a_chunk.imag


def _s5(proj, mats):
    seq, width = proj.shape
    t, tc, gp, n, ng = S5_T, S5_TC, S5_GROUP, S5_STATE, S5_GROUPS
    rows = t * tc
    ncb = S5_CH // LANES
    cb0 = (width - S5_CH) // LANES
    u_specs = [pl.BlockSpec((rows, LANES), (lambda i, k=k: (i, cb0 + k))) for k in range(ncb)]
    nsb = ng * n // LANES
    return pl.pallas_call(
        _s5_kernel,
        grid=(seq // rows,),
        in_specs=u_specs + [_const_spec(m.shape) for m in mats],
        out_specs=pl.BlockSpec((rows, S5_CH), lambda i: (i, 0)),
        out_shape=jax.ShapeDtypeStruct((seq, S5_CH), F32),
        scratch_shapes=[
            pltpu.VMEM((t, S5_CH, tc), F32),
            pltpu.VMEM((t, S5_CH, tc), F32),
            pltpu.VMEM((ncb, rows, LANES), F32),
            pltpu.VMEM((ng * n, tc), F32),
            pltpu.VMEM((ng * n, tc), F32),
            pltpu.VMEM((nsb, tc, LANES), F32),
            pltpu.VMEM((nsb, tc, LANES), F32),
            pltpu.VMEM((2, 8, ng * n), F32),
        ],
        compiler_params=pltpu.CompilerParams(dimension_semantics=("arbitrary",)),
        name="s5_scan",
    )(*([proj] * ncb), *mats)


def _mix_ffn_kernel(*refs, glu, final):
    (x_ref, a_ref, b_ref, wo_ref, g1_ref), refs = refs[:5], refs[5:]
    if glu:
        gw_ref, refs = refs[0], refs[1:]
    (g_ref, sc_ref, sh_ref, gate_ref, win_ref, cw_ref, cb_ref, wout_ref), refs = refs[:8], refs[8:]
    if final:
        fg_ref, o_ref, h_ref, act_ref, gbuf_ref, carry_ref = refs
    else:
        ng_ref, nsc_ref, nsh_ref, o_ref, hn_ref, h_ref, act_ref, gbuf_ref, carry_ref = refs
    tm = x_ref.shape[0]
    halo = gbuf_ref.shape[0] - tm

    @pl.when(pl.program_id(0) == 0)
    def _():
        carry_ref[...] = jnp.zeros(carry_ref.shape, F32)

    if glu:
        y = jax.nn.gelu(b_ref[...]).astype(BF16)
        gg = _dot(y, gw_ref[...])
        half = gg.shape[1] // 2
        b = (gg[:, :half] * jax.nn.sigmoid(gg[:, half:])).astype(BF16)
    else:
        b = b_ref[...]
    cat = jnp.concatenate([a_ref[...], b], axis=1)
    x = x_ref[...] + g1_ref[...] * _dot(cat, wo_ref[...])
    h_ref[...] = _mod_rmsnorm(x, g_ref[...], sc_ref[...], sh_ref[...]).astype(BF16)
    for f in range(D_FF // TF_FFN):
        cs = slice(f * TF_FFN, (f + 1) * TF_FFN)
        gs = slice(D_FF + f * TF_FFN, D_FF + (f + 1) * TF_FFN)
        h = h_ref[...]
        val = _dot(h, win_ref[:, cs])
        gate = _dot(h, win_ref[:, gs])
        gbuf_ref[0:halo, :] = carry_ref[:, cs]
        gbuf_ref[halo:halo + tm, :] = gate
        carry_ref[:, cs] = gate[tm - halo:tm, :]
        conv = (gate * cw_ref[2:3, cs] + gbuf_ref[halo - 1:halo - 1 + tm, :] * cw_ref[1:2, cs]
                + gbuf_ref[halo - 2:halo - 2 + tm, :] * cw_ref[0:1, cs] + cb_ref[:, cs])
        act_ref[:, cs] = (jax.nn.gelu(conv) * val).astype(BF16)
    xn = x + gate_ref[...] * _dot(act_ref[...], wout_ref[...])
    if final:
        xn = xn * lax.rsqrt(jnp.mean(xn * xn, axis=-1, keepdims=True) + EPS) * fg_ref[...]
    else:
        hn_ref[...] = _mod_rmsnorm(xn, ng_ref[...], nsc_ref[...], nsh_ref[...]).astype(BF16)
    o_ref[...] = xn


def _layer_spec(shape, layer):
    idx = (layer,) + (0,) * (len(shape) - 1)
    return pl.BlockSpec((None,) + tuple(shape[1:]), lambda *_: idx, pipeline_mode=pl.Buffered(1))


def _mix_ffn(x, a, b, wo, gate1, glu_w, g, scale, shift, gate2, w_in, conv_w, conv_b, w_out, tail, layer):
    seq, d = x.shape
    final = len(tail) == 1
    tm = TM_FFN
    halo = 8
    row = pl.BlockSpec((1, d), lambda i: (0, 0))
    rows = lambda w: pl.BlockSpec((tm, w), lambda i: (i, 0))
    conv_b = conv_b.reshape(conv_b.shape[0], 1, D_FF)
    in_specs = [rows(d), rows(a.shape[1]), rows(b.shape[1]), _const_spec(wo.shape), row]
    args = [x, a, b, wo, gate1]
    if glu_w is not None:
        in_specs.append(_const_spec(glu_w.shape))
        args.append(glu_w)
    in_specs += [
        row, row, row, row,
        _layer_spec(w_in.shape, layer),
        _layer_spec(conv_w.shape, layer),
        _layer_spec(conv_b.shape, layer),
        _layer_spec(w_out.shape, layer),
    ] + [row] * len(tail)
    args += [g.reshape(1, d), scale, shift, gate2, w_in, conv_w, conv_b, w_out]
    args += [t.reshape(1, d) for t in tail]
    out_specs = [rows(d)] if final else [rows(d), rows(d)]
    out_shape = [jax.ShapeDtypeStruct((seq, d), F32)] + ([] if final else [jax.ShapeDtypeStruct((seq, d), BF16)])
    return pl.pallas_call(
        functools.partial(_mix_ffn_kernel, glu=glu_w is not None, final=final),
        grid=(seq // tm,),
        in_specs=in_specs,
        out_specs=out_specs,
        out_shape=out_shape,
        scratch_shapes=[
            pltpu.VMEM((tm, d), BF16),
            pltpu.VMEM((tm, D_FF), BF16),
            pltpu.VMEM((tm + halo, TF_FFN), F32),
            pltpu.VMEM((halo, D_FF), F32),
        ],
        compiler_params=pltpu.CompilerParams(dimension_semantics=("arbitrary",)),
        name="mix_ffn",
    )(*args)


def kernel(x, c, t5_table, mod_w, mod_b, norm1_g, norm2_g, ffn_w_in, ffn_conv_w, ffn_conv_b, ffn_w_out,
           ev_w_in, ev_w_out, diff_lambda, diff_subln_g, band_rel_bias,
           od_w_in, od_w_out, s5_lam_re, s5_lam_im, s5_log_step, s5_b_re, s5_b_im, s5_c_re, s5_c_im,
           s5_d, s5_glu_w, final_g):
    assert x.shape[0] == 1 and x.shape[2] == D_MODEL
    seq = x.shape[1]
    assert seq % TM_PROJ == 0 and seq % (S5_T * S5_TC) == 0
    d = D_MODEL
    xs = x[0]
    mod = _modulation(c, mod_w, mod_b)
    ffn_w_in_b = ffn_w_in.astype(BF16)
    ffn_w_out_b = ffn_w_out.astype(BF16)
    mods = [[mod[i, :, k * d:(k + 1) * d] for k in range(6)] for i in range(DEPTH)]
    h = None
    for i in range(DEPTH):
        sh1, sc1, g1, sh2, sc2, g2 = mods[i]
        w_in = (ev_w_in if i % 2 == 0 else od_w_in)[i // 2].astype(BF16)
        proj_dtype = BF16 if i % 2 == 0 else F32
        if h is None:
            proj = _normproj(xs, norm1_g[i], sc1, sh1, w_in, proj_dtype)
        else:
            proj = _proj(h, w_in, proj_dtype)
        if i % 2 == 0:
            e = i // 2
            lam_init = 0.8 - 0.6 * math.exp(-0.3 * i)
            lp = diff_lambda[e].astype(F32)
            lam = jnp.exp(jnp.sum(lp[0] * lp[1])) - jnp.exp(jnp.sum(lp[2] * lp[3])) + lam_init
            mix_a = _diff_attention(proj, t5_table, lam, diff_subln_g[e], lam_init)
            mix_b = _band_attention(proj, band_rel_bias[e])
            wo, glu_w = ev_w_out[e].astype(BF16), None
        else:
            o = i // 2
            mix_a = _retention(proj)
            mats = _s5_matrices(s5_lam_re[o], s5_lam_im[o], s5_log_step[o], s5_b_re[o], s5_b_im[o],
                                s5_c_re[o], s5_c_im[o], s5_d[o])
            mix_b = _s5(proj, mats)
            wo, glu_w = od_w_out[o].astype(BF16), s5_glu_w[o].astype(BF16)
        if i == DEPTH - 1:
            tail = (final_g,)
        else:
            nsh1, nsc1 = mods[i + 1][0], mods[i + 1][1]
            tail = (norm1_g[i + 1], nsc1, nsh1)
        out = _mix_ffn(xs, mix_a, mix_b, wo, g1, glu_w, norm2_g[i], sc2, sh2, g2,
                       ffn_w_in_b, ffn_conv_w, ffn_conv_b, ffn_w_out_b, tail, layer=i)
        if i == DEPTH - 1:
            xs = out[0]
        else:
            xs, h = out
    return xs[None]
```

```python
import functools
import math

import jax
import jax.numpy as jnp
from jax import lax
from jax.experimental import pallas as pl
from jax.experimental.pallas import tpu as pltpu

F32 = jnp.float32
BF16 = jnp.bfloat16

D_MODEL = 1024
DEPTH = 2
CHUNK = 64
GROUP_WIDTH = D_MODEL // 2
DK_A = 64
DV_A = 2 * DK_A
N_HEADS_A = GROUP_WIDTH // DV_A
DH_B = 64
N_HEADS_B = GROUP_WIDTH // DH_B
LEFT_CHUNKS = 8
REL_CLIP = 2 * CHUNK
NUM_BUCKETS = 32
MAX_DISTANCE = 128
DV_C = 128
DQK_C = DV_C // 2
N_HEADS_C = GROUP_WIDTH // DV_C
ROPE_BASE = 10000.0
S5_CH = GROUP_WIDTH
S5_GROUP = 16
S5_GROUPS = S5_CH // S5_GROUP
S5_STATE = 64
D_FF = ((8 * D_MODEL // 3 + 255) // 256) * 256
CONV_W = 3
EVEN_IN = 3 * N_HEADS_A * DV_A + 3 * N_HEADS_B * DH_B
ODD_IN = 2 * N_HEADS_C * DQK_C + 2 * N_HEADS_C * DV_C + S5_CH
EPS = 1e-6
NEG_INF = -1e30
LOG2E = math.log2(math.e)

LANES = 128
MXU_DIM = 256

TM_PROJ = 1024
TM_FFN = 512
TF_FFN = MXU_DIM
BLK_A = 512
NPART_A = 2
ONES_A = 16
BLK_B = 512
BLK_C = 256
S5_T = 16
S5_TC = LANES

assert BLK_B == LEFT_CHUNKS * CHUNK, "band window must be exactly one previous block"
assert BLK_A >= MAX_DISTANCE, "far key blocks must sit in the saturated T5 bucket"
assert DV_A == LANES, "diff-attention statistics are kept lane-replicated beside the accumulator"


def _dot(a, b):
    return jnp.dot(a, b, preferred_element_type=F32)


def _dot_nt(a, b):
    return lax.dot_general(a, b, (((1,), (1,)), ((), ())), preferred_element_type=F32)


def _dot_tn(a, b):
    return lax.dot_general(a, b, (((0,), (0,)), ((), ())), preferred_element_type=F32)


def _const_spec(shape):
    zeros = (0,) * len(shape)
    return pl.BlockSpec(shape, lambda *_: zeros, pipeline_mode=pl.Buffered(1))


def _mod_rmsnorm(x, g, scale, shift):
    y = x * lax.rsqrt(jnp.mean(x * x, axis=-1, keepdims=True) + EPS)
    y = y * g
    return y * (1.0 + scale) + shift


def _mod_kernel(c_ref, w_ref, b_ref, o_ref):
    c = c_ref[...]
    cond = c * jax.nn.sigmoid(c)
    o_ref[0] = jnp.sum(cond * w_ref[0], axis=0, keepdims=True) + b_ref[0]


def _modulation(c, mod_w, mod_b):
    depth, d, n = mod_w.shape
    tn = 1536
    return pl.pallas_call(
        _mod_kernel,
        grid=(depth, n // tn),
        in_specs=[
            pl.BlockSpec((d, 1), lambda i, j: (0, 0)),
            pl.BlockSpec((1, d, tn), lambda i, j: (i, 0, j)),
            pl.BlockSpec((1, 1, tn), lambda i, j: (i, 0, j)),
        ],
        out_specs=pl.BlockSpec((1, 1, tn), lambda i, j: (i, 0, j)),
        out_shape=jax.ShapeDtypeStruct((depth, 1, n), F32),
        name="modulation",
    )(c.reshape(d, 1), mod_w, mod_b.reshape(depth, 1, n))


TN_PROJ = 1024


def _normproj_kernel(x_ref, g_ref, sc_ref, sh_ref, w_ref, o_ref):
    tm, n = o_ref.shape
    half = tm // 2
    for r in range(2):
        rows = slice(r * half, (r + 1) * half)
        h = _mod_rmsnorm(x_ref[rows, :], g_ref[...], sc_ref[...], sh_ref[...]).astype(BF16)
        for j in range(n // TN_PROJ):
            cols = slice(j * TN_PROJ, (j + 1) * TN_PROJ)
            o_ref[rows, cols] = _dot(h, w_ref[:, cols]).astype(o_ref.dtype)


def _normproj(x, g, scale, shift, w, out_dtype):
    seq, d = x.shape
    n = w.shape[1]
    tm = TM_PROJ
    row = pl.BlockSpec((1, d), lambda i: (0, 0))
    return pl.pallas_call(
        _normproj_kernel,
        grid=(seq // tm,),
        in_specs=[pl.BlockSpec((tm, d), lambda i: (i, 0)), row, row, row, _const_spec(w.shape)],
        out_specs=pl.BlockSpec((tm, n), lambda i: (i, 0)),
        out_shape=jax.ShapeDtypeStruct((seq, n), out_dtype),
        compiler_params=pltpu.CompilerParams(dimension_semantics=("parallel",)),
        name="normproj",
    )(x, g.reshape(1, d), scale, shift, w)


def _proj_kernel(h_ref, w_ref, o_ref):
    for j in range(o_ref.shape[1] // TN_PROJ):
        cols = slice(j * TN_PROJ, (j + 1) * TN_PROJ)
        o_ref[:, cols] = _dot(h_ref[...], w_ref[:, cols]).astype(o_ref.dtype)


def _proj(h, w, out_dtype):
    seq, d = h.shape
    n = w.shape[1]
    tm = TM_PROJ
    return pl.pallas_call(
        _proj_kernel,
        grid=(seq // tm,),
        in_specs=[pl.BlockSpec((tm, d), lambda i: (i, 0)), _const_spec(w.shape)],
        out_specs=pl.BlockSpec((tm, n), lambda i: (i, 0)),
        out_shape=jax.ShapeDtypeStruct((seq, n), out_dtype),
        compiler_params=pltpu.CompilerParams(dimension_semantics=("parallel",)),
        name="proj",
    )(h, w)


def _diffattn_kernel(q_ref, k_ref, v_ref, bias_ref, lam_ref, g_ref, o_ref,
                     qs_ref, vt_ref, m_ref, acc_ref, *s_refs, out_scale):
    blk = BLK_A
    nq = 2 * blk
    sub = 8
    dv = DV_A
    npart = len(s_refs) // 4
    wq = nq // npart
    sa_ref, sb_ref = s_refs[:2 * npart], s_refs[2 * npart:]
    i = pl.program_id(1)

    @pl.when(i == 0)
    def _():
        def tr(b, carry):
            r0 = pl.multiple_of(b * blk, blk)
            vt_ref[0:dv, pl.ds(r0, blk)] = v_ref[pl.ds(r0, blk), :].astype(F32).T.astype(BF16)
            vt_ref[dv:dv + ONES_A, pl.ds(r0, blk)] = jnp.ones((ONES_A, blk), BF16)
            return carry
        lax.fori_loop(0, v_ref.shape[0] // blk, tr, 0)

    q = q_ref[...].astype(F32) * (DK_A ** -0.5 * LOG2E)
    lane = lax.broadcasted_iota(jnp.int32, q.shape, 1)
    qs_ref[0:blk, :] = jnp.where(lane < DK_A, q, 0.0).astype(BF16)
    qs_ref[blk:nq, :] = jnp.where(lane >= DK_A, q, 0.0).astype(BF16)
    m_ref[...] = jnp.full(m_ref.shape, NEG_INF, F32)
    acc_ref[...] = jnp.zeros(acc_ref.shape, F32)

    def scores(b, s_ref):
        k = k_ref[pl.ds(pl.multiple_of(b * blk, blk), blk), :]
        for part in range(npart):
            s = _dot_nt(k, qs_ref[part * wq:(part + 1) * wq, :])
            s_ref[part][...] = s
            s_ref[npart + part][...] = jnp.max(s.reshape(blk // sub, sub, wq), axis=0)

    def softmax_pv(b, s_ref, bias):
        vt = vt_ref[:, pl.ds(pl.multiple_of(b * blk, blk), blk)]
        for part in range(npart):
            cols = slice(part * wq, (part + 1) * wq)
            s = s_ref[part][...]
            if bias is not None:
                q0 = (part * wq) % blk
                s = s + bias[:, q0:q0 + wq]
            s = s.reshape(blk // sub, sub, wq)
            m_prev = m_ref[:, cols]
            smax = jnp.max(s, axis=0) if bias is not None else s_ref[npart + part][...]
            m_cur = jnp.max(smax, axis=0, keepdims=True)
            m_new = jnp.maximum(m_prev, m_cur)
            alpha = jnp.exp2(m_prev - m_new)
            p = jnp.exp2(s - m_new[None])
            pv = _dot(vt, p.reshape(blk, wq).astype(BF16))
            acc_ref[:, cols] = acc_ref[:, cols] * alpha[0:1] + pv
            m_ref[:, cols] = m_new

    nfar = jnp.maximum(i - 1, 0)
    odd = lax.rem(nfar, 2)

    @pl.when(i == 0)
    def _():
        scores(0, sb_ref)

    @pl.when(i > 0)
    def _():
        @pl.when(odd == 1)
        def _():
            scores(0, sb_ref)
            scores(1, sa_ref)
            softmax_pv(0, sb_ref, None)

        @pl.when(odd == 0)
        def _():
            scores(0, sa_ref)

        def pair(b):
            scores(b + 1, sb_ref)
            softmax_pv(b, sa_ref, None)
            scores(b + 2, sa_ref)
            softmax_pv(b + 1, sb_ref, None)

        def quad_body(t, carry):
            pair(odd + 4 * t)
            pair(odd + 4 * t + 2)
            return carry

        npairs = nfar // 2
        lax.fori_loop(0, npairs // 2, quad_body, 0)

        @pl.when(lax.rem(npairs, 2) == 1)
        def _():
            pair(odd + 2 * (npairs - 1))
        scores(i, sb_ref)
        softmax_pv(i - 1, sa_ref, bias_ref[0, 0])

    softmax_pv(i, sb_ref, bias_ref[0, 1])

    ot = acc_ref[0:dv, :] / acc_ref[dv:dv + 1, :]
    o = ot[:, 0:blk].T - lam_ref[...] * ot[:, blk:nq].T
    o = o * lax.rsqrt(jnp.mean(o * o, axis=-1, keepdims=True) + EPS) * g_ref[...]
    o_ref[...] = (o * out_scale).astype(o_ref.dtype)


_TOEPLITZ_ROWS = 512
_TOEPLITZ_N = 2048


def _toeplitz_kernel(v_ref, o_ref, *, keep):
    rows, cols = o_ref.shape[1:]
    x = jnp.broadcast_to(v_ref[0, 0], (rows, v_ref.shape[-1]))
    tile = pltpu.roll(x, 0, 1, stride=1, stride_axis=0)[:, :cols]
    r = lax.broadcasted_iota(jnp.int32, (rows, cols), 0) + pl.program_id(1) * rows
    c = lax.broadcasted_iota(jnp.int32, (rows, cols), 1)
    o_ref[0] = jnp.where(keep(r, c), tile, NEG_INF)


def _toeplitz_tiles(fn, keep, heads, rows, cols):
    n, rb = _TOEPLITZ_N, _TOEPLITZ_ROWS
    assert rows % rb == 0 and rows <= n // 2 and cols <= n // 2
    idx = jnp.arange(n, dtype=jnp.int32)
    vec = fn(jnp.where(idx < n // 2, idx, idx - n)).astype(F32)
    vecs = jnp.stack([jnp.roll(vec, k * rb, axis=1) for k in range(rows // rb)], axis=1)
    return pl.pallas_call(
        functools.partial(_toeplitz_kernel, keep=keep),
        grid=(heads, rows // rb),
        in_specs=[pl.BlockSpec((1, 1, 1, n), lambda h, k: (h, k, 0, 0))],
        out_specs=pl.BlockSpec((1, rb, cols), lambda h, k: (h, k, 0)),
        out_shape=jax.ShapeDtypeStruct((heads, rows, cols), F32),
        name="toeplitz_tiles",
    )(vecs.reshape(heads, rows // rb, 1, n))


def _t5_bucket(rel):
    nb = NUM_BUCKETS // 2
    max_exact = nb // 2
    bucket = jnp.where(rel > 0, nb, 0)
    n = jnp.abs(rel)
    nf = jnp.maximum(n, 1).astype(F32)
    large = max_exact + (jnp.log(nf / max_exact) / math.log(MAX_DISTANCE / max_exact)
                         * (nb - max_exact)).astype(jnp.int32)
    large = jnp.minimum(large, nb - 1)
    return bucket + jnp.where(n < max_exact, n, large)


def _diff_bias_tiles(t5_table):
    blk = BLK_A
    table = t5_table.astype(F32)
    far = table[_t5_bucket(jnp.full((), -(blk + 1), jnp.int32))]
    def visible(r, c):
        return jnp.floor_divide(r - blk, CHUNK) <= jnp.floor_divide(c, CHUNK)

    tiles = _toeplitz_tiles(lambda x: ((table[_t5_bucket(-x - blk)] - far) * LOG2E).T, visible,
                            N_HEADS_A, 2 * blk, blk)
    return tiles.reshape(N_HEADS_A, 2, blk, blk)


def _diff_attention(proj, t5_table, lam, subln_g, lam_init):
    seq = proj.shape[0]
    blk = BLK_A
    bias = _diff_bias_tiles(t5_table)
    ha = N_HEADS_A
    kern = functools.partial(_diffattn_kernel, out_scale=1.0 - lam_init)
    return pl.pallas_call(
        kern,
        grid=(ha, seq // blk),
        in_specs=[
            pl.BlockSpec((blk, DV_A), lambda h, i: (i, h)),
            pl.BlockSpec((seq, DV_A), lambda h, i: (0, ha + h)),
            pl.BlockSpec((seq, DV_A), lambda h, i: (0, 2 * ha + h)),
            pl.BlockSpec((1, 2, blk, blk), lambda h, i: (h, 0, 0, 0)),
            pl.BlockSpec((1, DV_A), lambda h, i: (0, 0)),
            pl.BlockSpec((1, DV_A), lambda h, i: (0, 0)),
        ],
        out_specs=pl.BlockSpec((blk, DV_A), lambda h, i: (i, h)),
        out_shape=jax.ShapeDtypeStruct((seq, ha * DV_A), BF16),
        scratch_shapes=[
            pltpu.VMEM((2 * blk, DV_A), BF16),
            pltpu.VMEM((DV_A + ONES_A, seq), BF16),
            pltpu.VMEM((8, 2 * blk), F32),
            pltpu.VMEM((DV_A + ONES_A, 2 * blk), F32),
        ] + 2 * ([pltpu.VMEM((blk, 2 * blk // NPART_A), F32)] * NPART_A
                 + [pltpu.VMEM((8, 2 * blk // NPART_A), F32)] * NPART_A),
        compiler_params=pltpu.CompilerParams(dimension_semantics=("parallel", "arbitrary")),
        name="diff_attention",
    )(proj, proj, proj, bias, jnp.full((1, DV_A), lam, F32), subln_g.reshape(1, DV_A).astype(F32))


def _band_kernel(q_ref, kp_ref, kc_ref, vp_ref, vc_ref, bias_ref, o_ref, *s_refs):
    blk = BLK_B
    half = blk // 2
    nk = 3 * half
    sub = 8
    i = pl.program_id(1)
    q = q_ref[...].astype(F32) * (DH_B ** -0.5 * LOG2E)
    lane = lax.broadcasted_iota(jnp.int32, q.shape, 1)
    qh = (jnp.where(lane < DH_B, q, 0.0).astype(BF16), jnp.where(lane >= DH_B, q, 0.0).astype(BF16))
    k_all = jnp.concatenate([kp_ref[...], kc_ref[...]], axis=0)
    vt_all = jnp.concatenate([vp_ref[...], vc_ref[...]], axis=0).astype(F32).T.astype(BF16)
    no_prev = jnp.where(i == 0, NEG_INF, 0.0).astype(F32)
    krow = lax.broadcasted_iota(jnp.int32, (nk, 2 * half), 0)
    for hf in range(2):
        k0 = hf * half
        qs = jnp.concatenate([qh[0][k0:k0 + half], qh[1][k0:k0 + half]], axis=0)
        s_refs[hf][...] = _dot_nt(k_all[k0:k0 + nk], qs)
    for hf in range(2):
        k0 = hf * half
        bias = jnp.concatenate([bias_ref[0, k0:k0 + nk, k0:k0 + half],
                                bias_ref[1, k0:k0 + nk, k0:k0 + half]], axis=1)
        s = s_refs[hf][...] + bias + jnp.where(krow < blk - k0, no_prev, 0.0)
        s = s.reshape(nk // sub, sub, 2 * half)
        m = jnp.max(jnp.max(s, axis=0), axis=0, keepdims=True)
        p = jnp.exp2(s - m[None])
        l = jnp.sum(jnp.sum(p, axis=0), axis=0, keepdims=True)
        ot = _dot(vt_all[:, k0:k0 + nk], p.reshape(nk, 2 * half).astype(BF16)) / l
        o = jnp.concatenate([ot[0:DH_B, 0:half], ot[DH_B:2 * DH_B, half:2 * half]], axis=0)
        o_ref[k0:k0 + half, :] = o.T.astype(o_ref.dtype)


def _band_bias_tiles(rel_bias):
    blk = BLK_B

    def valid(r, c):
        qchunk = jnp.floor_divide(c, CHUNK)
        kchunk = jnp.floor_divide(r - blk, CHUNK)
        return (kchunk <= qchunk) & (kchunk >= qchunk - LEFT_CHUNKS)

    return _toeplitz_tiles(
        lambda x: rel_bias.astype(F32)[:, jnp.clip(-x - blk, -REL_CLIP, REL_CLIP) + REL_CLIP] * LOG2E, valid,
        N_HEADS_B, 2 * blk, blk)


def _band_attention(proj, rel_bias):
    seq = proj.shape[0]
    blk = BLK_B
    bias = _band_bias_tiles(rel_bias)
    npair = N_HEADS_B // 2
    qc0 = 3 * N_HEADS_A
    prev = lambda c0: (lambda hp, i: (jnp.maximum(i - 1, 0), c0 + hp))
    cur = lambda c0: (lambda hp, i: (i, c0 + hp))
    return pl.pallas_call(
        _band_kernel,
        grid=(npair, seq // blk),
        in_specs=[
            pl.BlockSpec((blk, LANES), cur(qc0)),
            pl.BlockSpec((blk, LANES), prev(qc0 + npair)),
            pl.BlockSpec((blk, LANES), cur(qc0 + npair)),
            pl.BlockSpec((blk, LANES), prev(qc0 + 2 * npair)),
            pl.BlockSpec((blk, LANES), cur(qc0 + 2 * npair)),
            pl.BlockSpec((2, 2 * blk, blk), lambda hp, i: (hp, 0, 0)),
        ],
        out_specs=pl.BlockSpec((blk, LANES), lambda hp, i: (i, hp)),
        out_shape=jax.ShapeDtypeStruct((seq, N_HEADS_B * DH_B), BF16),
        scratch_shapes=[pltpu.VMEM((3 * blk // 2, blk), F32)] * 2,
        compiler_params=pltpu.CompilerParams(dimension_semantics=("parallel", "arbitrary")),
        name="band_attention",
    )(proj, proj, proj, proj, proj, bias)


def _retention_kernel(qk_ref, v_ref, gate_ref, cos_ref, sin_ref, qdec_ref, kdec_ref, dmat_ref,
                      sdec_ref, o_ref, state_ref):
    @pl.when(pl.program_id(0) == 0)
    def _():
        state_ref[...] = jnp.zeros(state_ref.shape, F32)

    cos = cos_ref[...]
    sin = sin_ref[...]
    lane = lax.broadcasted_iota(jnp.int32, cos.shape, 1)
    first_half = (lane % DQK_C) < (DQK_C // 2)
    qk = qk_ref[...]
    parts = []
    for j in range(qk.shape[1] // LANES):
        t = qk[:, j * LANES:(j + 1) * LANES]
        partner = jnp.where(first_half, pltpu.roll(t, LANES - DQK_C // 2, 1), pltpu.roll(t, DQK_C // 2, 1))
        parts.append(t * cos + partner * sin)
    wq = N_HEADS_C * DQK_C
    q = jnp.concatenate(parts[:wq // LANES], axis=1)
    k = jnp.concatenate(parts[wq // LANES:], axis=1) * (DQK_C ** -0.5)
    qd = (q * qdec_ref[...]).astype(BF16)
    kd = (k * kdec_ref[...]).astype(BF16)
    qb = q.astype(BF16)
    kb = k.astype(BF16)
    vb = v_ref[...].astype(BF16)
    gate = gate_ref[...]
    outs = []
    for h in range(N_HEADS_C):
        qs = slice(h * DQK_C, (h + 1) * DQK_C)
        vs = slice(h * DV_C, (h + 1) * DV_C)
        scores = _dot_nt(qb[:, qs], kb[:, qs]) * dmat_ref[h]
        state = state_ref[h]
        r = _dot(scores.astype(BF16), vb[:, vs]) + _dot(qd[:, qs], state.astype(BF16))
        state_ref[h] = state * sdec_ref[h] + _dot_tn(kd[:, qs], vb[:, vs])
        r = r * lax.rsqrt(jnp.mean(r * r, axis=-1, keepdims=True) + EPS)
        g = gate[:, vs]
        outs.append(r * (g * jax.nn.sigmoid(g)))
    o_ref[...] = jnp.concatenate(outs, axis=1).astype(o_ref.dtype)


def _retention_tables(seq):
    t = BLK_C
    half = DQK_C // 2
    inv_freq = 1.0 / (ROPE_BASE ** (jnp.arange(0, DQK_C, 2, dtype=F32) / DQK_C))
    ang = jnp.arange(seq, dtype=F32)[:, None] * inv_freq[None, :]
    reps = LANES // half
    cos = jnp.tile(jnp.cos(ang), (1, reps))
    sign = jnp.where((jnp.arange(LANES) % DQK_C) < half, -1.0, 1.0).astype(F32)
    sin = jnp.tile(jnp.sin(ang), (1, reps)) * sign[None, :]
    log_g = jnp.log(1.0 - jnp.power(2.0, -5.0 - jnp.arange(N_HEADS_C, dtype=F32)))
    pos = jnp.arange(t, dtype=F32)
    diff = pos[:, None] - pos[None, :]
    same_or_past = (jnp.arange(t)[None, :] // CHUNK) <= (jnp.arange(t)[:, None] // CHUNK)
    dmat = jnp.where(same_or_past[None], jnp.exp(log_g[:, None, None] * jnp.abs(diff)[None]), 0.0)
    qdec = jnp.repeat(jnp.exp(log_g[None, :] * (pos[:, None] + 1.0)), DQK_C, axis=1)
    kdec = jnp.repeat(jnp.exp(log_g[None, :] * (t - 1.0 - pos[:, None])), DQK_C, axis=1)
    sdec = jnp.broadcast_to(jnp.exp(log_g * t)[:, None, None], (N_HEADS_C, 1, DV_C))
    return cos, sin, qdec, kdec, dmat, sdec


def _retention(proj):
    seq = proj.shape[0]
    t = BLK_C
    cos, sin, qdec, kdec, dmat, sdec = _retention_tables(seq)
    wv = N_HEADS_C * DV_C
    return pl.pallas_call(
        _retention_kernel,
        grid=(seq // t,),
        in_specs=[
            pl.BlockSpec((t, wv), lambda i: (i, 0)),
            pl.BlockSpec((t, wv), lambda i: (i, 1)),
            pl.BlockSpec((t, wv), lambda i: (i, 2)),
            pl.BlockSpec((t, LANES), lambda i: (i, 0)),
            pl.BlockSpec((t, LANES), lambda i: (i, 0)),
            pl.BlockSpec((t, N_HEADS_C * DQK_C), lambda i: (0, 0)),
            pl.BlockSpec((t, N_HEADS_C * DQK_C), lambda i: (0, 0)),
            pl.BlockSpec((N_HEADS_C, t, t), lambda i: (0, 0, 0)),
            pl.BlockSpec((N_HEADS_C, 1, DV_C), lambda i: (0, 0, 0)),
        ],
        out_specs=pl.BlockSpec((t, wv), lambda i: (i, 0)),
        out_shape=jax.ShapeDtypeStruct((seq, wv), BF16),
        scratch_shapes=[pltpu.VMEM((N_HEADS_C, DQK_C, DV_C), F32)],
        compiler_params=pltpu.CompilerParams(dimension_semantics=("arbitrary",)),
        name="retention",
    )(proj, proj, proj, cos, sin, qdec, kdec, dmat, sdec)


def _s5_kernel(*refs):
    ncb = S5_CH // LANES
    u_refs = refs[:ncb]
    (mt_ref, bt_ref, ctr_ref, cti_ref, are_ref, aim_ref, y_ref,
     ut_ref, yt_ref, ys_ref, vr_ref, vi_ref, spr_ref, spi_ref, carry_ref) = refs[ncb:]
    tc = S5_TC
    gp = S5_GROUP
    n = S5_STATE
    ng = S5_GROUPS

    @pl.when(pl.program_id(0) == 0)
    def _():
        carry_ref[...] = jnp.zeros(carry_ref.shape, F32)

    for s in range(S5_T):
        for k in range(ncb):
            ut_ref[s, k * LANES:(k + 1) * LANES, :] = u_refs[k][pl.ds(s, tc, stride=S5_T), :].T

    unroll = 4

    def intra(it, carry):
        for k in range(unroll):
            g = it * unroll + k
            r0 = pl.multiple_of(g * gp, gp)
            ug = ut_ref[:, pl.ds(r0, gp), :].reshape(S5_T * gp, tc).astype(BF16)
            yt_ref[:, pl.ds(r0, gp), :] = _dot(mt_ref[g], ug).reshape(S5_T, gp, tc)
            vt = _dot(bt_ref[g], ug)
            n0 = pl.multiple_of(g * n, n)
            vr_ref[pl.ds(n0, n), :] = vt[0:n]
            vi_ref[pl.ds(n0, n), :] = vt[n:2 * n]
        return carry

    lax.fori_loop(0, ng // unroll, intra, 0)

    sub = 8
    nv = tc // sub
    row = lax.broadcasted_iota(jnp.int32, (tc, LANES), 0)
    in_vreg = lax.rem(row, sub)

    def rows_of(v, r):
        return jnp.broadcast_to(v[r:r + 1], (tc, LANES))

    for j in range(ng * n // LANES):
        cols = slice(j * LANES, (j + 1) * LANES)
        pwr, pwi = are_ref[:, cols], aim_ref[:, cols]
        xr = vr_ref[cols, :].T
        xi = vi_ref[cols, :].T
        for d in (1, 2, 4):
            keep = in_vreg >= d
            sr = jnp.where(keep, pltpu.roll(xr, d, 0), 0.0)
            si = jnp.where(keep, pltpu.roll(xi, d, 0), 0.0)
            fr, fi = rows_of(pwr, d - 1), rows_of(pwi, d - 1)
            xr, xi = xr + (fr * sr - fi * si), xi + (fr * si + fi * sr)
        cr, ci = carry_ref[0, :, cols], carry_ref[1, :, cols]
        cr0, ci0 = cr, ci
        outr, outi = [], []
        for v in range(nv):
            yr = xr[v * sub:(v + 1) * sub] + (pwr * cr - pwi * ci)
            yi = xi[v * sub:(v + 1) * sub] + (pwr * ci + pwi * cr)
            outr.append(yr)
            outi.append(yi)
            cr = jnp.broadcast_to(yr[sub - 1:sub], (sub, LANES))
            ci = jnp.broadcast_to(yi[sub - 1:sub], (sub, LANES))
        carry_ref[0, :, cols] = cr
        carry_ref[1, :, cols] = ci
        sr = jnp.concatenate(outr, axis=0)
        si = jnp.concatenate(outi, axis=0)
        first = row == 0
        spr_ref[j] = jnp.where(first, rows_of(cr0, 0), pltpu.roll(sr, 1, 0))
        spi_ref[j] = jnp.where(first, rows_of(ci0, 0), pltpu.roll(si, 1, 0))

    def cross(it, carry):
        for k in range(unroll):
            jp = it * unroll + k
            r0 = pl.multiple_of(jp * 2 * gp, 2 * gp)
            yc = (_dot_nt(ctr_ref[jp], spr_ref[jp].astype(BF16))
                  + _dot_nt(cti_ref[jp], spi_ref[jp].astype(BF16)))
            yt_ref[:, pl.ds(r0, 2 * gp), :] += yc.reshape(S5_T, 2 * gp, tc)
        return carry

    lax.fori_loop(0, ng // 2 // unroll, cross, 0)

    for s in range(S5_T):
        for k in range(ncb):
            ys_ref[k, pl.ds(s, tc, stride=S5_T), :] = yt_ref[s, k * LANES:(k + 1) * LANES, :].T
    for k in range(ncb):
        y_ref[:, k * LANES:(k + 1) * LANES] = ys_ref[k]


def _s5_matrices(lam_re, lam_im, log_step, b_re, b_im, c_re, c_im, d_skip):
    hi = lax.Precision.HIGHEST
    t, gp, n, ng = S5_T, S5_GROUP, S5_STATE, S5_GROUPS
    lam = lax.complex(lam_re.astype(F32), lam_im.astype(F32))
    step = jnp.exp(log_step.astype(F32))[:, None]
    ls = lam * step
    a_bar = jnp.exp(ls)
    b_bar = ((a_bar - 1.0) / lam)[..., None] * lax.complex(b_re.astype(F32), b_im.astype(F32))
    cm = lax.complex(c_re.astype(F32), c_im.astype(F32))

    def apow(k):
        kk = k.astype(F32).astype(jnp.complex64)
        return jnp.exp(ls.reshape((ng,) + (1,) * k.ndim + (n,)) * kk[None, ..., None])

    tt = jnp.arange(t)
    kmat = jnp.einsum('gpn,gln,gnq->glpq', cm, apow(tt), b_bar, precision=hi).real
    krev = jnp.transpose(kmat[:, ::-1], (0, 2, 1, 3)).reshape(ng, gp, t * gp)
    kpad = jnp.pad(krev, ((0, 0), (0, 0), (0, t * gp)))
    mt = jnp.concatenate([kpad[:, :, (t - 1 - to) * gp:(2 * t - 1 - to) * gp] for to in range(t)], axis=1)
    dvec = jnp.tile(d_skip.astype(F32).reshape(ng, 1, gp), (1, t, 1)).reshape(ng, t * gp)
    mt = mt + jnp.eye(t * gp, dtype=F32)[None] * dvec[:, :, None]
    z = jnp.swapaxes(apow(t - 1 - tt), 1, 2)[:, :, :, None] * b_bar[:, :, None, :]
    z = z.reshape(ng, n, t * gp)
    bt = jnp.concatenate([z.real, z.imag], axis=1)
    w = cm[:, None, :, :] * apow(tt + 1)[:, :, None, :]

    def pair_readout(x):
        x = x.reshape(ng // 2, 2, t, gp, n)
        first = jnp.pad(x[:, 0], ((0, 0), (0, 0), (0, 0), (0, n)))
        second = jnp.pad(x[:, 1], ((0, 0), (0, 0), (0, 0), (n, 0)))
        return jnp.stack([first, second], axis=2).reshape(ng // 2, t * 2 * gp, 2 * n).astype(BF16)

    ctr, cti = pair_readout(w.real), pair_readout(-w.imag)
    a_chunk = jnp.transpose(apow(t * (jnp.arange(8) + 1)), (1, 0, 2)).reshape(8, ng * n)
    return mt.astype(BF16), bt.astype(BF16), ctr, cti, a_chunk.real, a_chunk.imag


def _s5(proj, mats):
    seq, width = proj.shape
    t, tc, gp, n, ng = S5_T, S5_TC, S5_GROUP, S5_STATE, S5_GROUPS
    rows = t * tc
    ncb = S5_CH // LANES
    cb0 = (width - S5_CH) // LANES
    u_specs = [pl.BlockSpec((rows, LANES), (lambda i, k=k: (i, cb0 + k))) for k in range(ncb)]
    nsb = ng * n // LANES
    return pl.pallas_call(
        _s5_kernel,
        grid=(seq // rows,),
        in_specs=u_specs + [_const_spec(m.shape) for m in mats],
        out_specs=pl.BlockSpec((rows, S5_CH), lambda i: (i, 0)),
        out_shape=jax.ShapeDtypeStruct((seq, S5_CH), F32),
        scratch_shapes=[
            pltpu.VMEM((t, S5_CH, tc), F32),
            pltpu.VMEM((t, S5_CH, tc), F32),
            pltpu.VMEM((ncb, rows, LANES), F32),
            pltpu.VMEM((ng * n, tc), F32),
            pltpu.VMEM((ng * n, tc), F32),
            pltpu.VMEM((nsb, tc, LANES), F32),
            pltpu.VMEM((nsb, tc, LANES), F32),
            pltpu.VMEM((2, 8, ng * n), F32),
        ],
        compiler_params=pltpu.CompilerParams(dimension_semantics=("arbitrary",)),
        name="s5_scan",
    )(*([proj] * ncb), *mats)


def _mix_ffn_kernel(*refs, glu, final):
    (x_ref, a_ref, b_ref, wo_ref, g1_ref), refs = refs[:5], refs[5:]
    if glu:
        gw_ref, refs = refs[0], refs[1:]
    (g_ref, sc_ref, sh_ref, gate_ref, win_ref, cw_ref, cb_ref, wout_ref), refs = refs[:8], refs[8:]
    if final:
        fg_ref, o_ref, h_ref, act_ref, gbuf_ref, carry_ref = refs
    else:
        ng_ref, nsc_ref, nsh_ref, o_ref, hn_ref, h_ref, act_ref, gbuf_ref, carry_ref = refs
    tm = x_ref.shape[0]
    halo = gbuf_ref.shape[0] - tm

    @pl.when(pl.program_id(0) == 0)
    def _():
        carry_ref[...] = jnp.zeros(carry_ref.shape, F32)

    if glu:
        y = jax.nn.gelu(b_ref[...]).astype(BF16)
        gg = _dot(y, gw_ref[...])
        half = gg.shape[1] // 2
        b = (gg[:, :half] * jax.nn.sigmoid(gg[:, half:])).astype(BF16)
    else:
        b = b_ref[...]
    cat = jnp.concatenate([a_ref[...], b], axis=1)
    x = x_ref[...] + g1_ref[...] * _dot(cat, wo_ref[...])
    h_ref[...] = _mod_rmsnorm(x, g_ref[...], sc_ref[...], sh_ref[...]).astype(BF16)
    for f in range(D_FF // TF_FFN):
        cs = slice(f * TF_FFN, (f + 1) * TF_FFN)
        gs = slice(D_FF + f * TF_FFN, D_FF + (f + 1) * TF_FFN)
        h = h_ref[...]
        val = _dot(h, win_ref[:, cs])
        gate = _dot(h, win_ref[:, gs])
        gbuf_ref[0:halo, :] = carry_ref[:, cs]
        gbuf_ref[halo:halo + tm, :] = gate
        carry_ref[:, cs] = gate[tm - halo:tm, :]
        conv = (gate * cw_ref[2:3, cs] + gbuf_ref[halo - 1:halo - 1 + tm, :] * cw_ref[1:2, cs]
                + gbuf_ref[halo - 2:halo - 2 + tm, :] * cw_ref[0:1, cs] + cb_ref[:, cs])
        act_ref[:, cs] = (jax.nn.gelu(conv) * val).astype(BF16)
    xn = x + gate_ref[...] * _dot(act_ref[...], wout_ref[...])
    if final:
        xn = xn * lax.rsqrt(jnp.mean(xn * xn, axis=-1, keepdims=True) + EPS) * fg_ref[...]
    else:
        hn_ref[...] = _mod_rmsnorm(xn, ng_ref[...], nsc_ref[...], nsh_ref[...]).astype(BF16)
    o_ref[...] = xn


def _layer_spec(shape, layer):
    idx = (layer,) + (0,) * (len(shape) - 1)
    return pl.BlockSpec((None,) + tuple(shape[1:]), lambda *_: idx, pipeline_mode=pl.Buffered(1))


def _mix_ffn(x, a, b, wo, gate1, glu_w, g, scale, shift, gate2, w_in, conv_w, conv_b, w_out, tail, layer):
    seq, d = x.shape
    final = len(tail) == 1
    tm = TM_FFN
    halo = 8
    row = pl.BlockSpec((1, d), lambda i: (0, 0))
    rows = lambda w: pl.BlockSpec((tm, w), lambda i: (i, 0))
    conv_b = conv_b.reshape(conv_b.shape[0], 1, D_FF)
    in_specs = [rows(d), rows(a.shape[1]), rows(b.shape[1]), _const_spec(wo.shape), row]
    args = [x, a, b, wo, gate1]
    if glu_w is not None:
        in_specs.append(_const_spec(glu_w.shape))
        args.append(glu_w)
    in_specs += [
        row, row, row, row,
        _layer_spec(w_in.shape, layer),
        _layer_spec(conv_w.shape, layer),
        _layer_spec(conv_b.shape, layer),
        _layer_spec(w_out.shape, layer),
    ] + [row] * len(tail)
    args += [g.reshape(1, d), scale, shift, gate2, w_in, conv_w, conv_b, w_out]
    args += [t.reshape(1, d) for t in tail]
    out_specs = [rows(d)] if final else [rows(d), rows(d)]
    out_shape = [jax.ShapeDtypeStruct((seq, d), F32)] + ([] if final else [jax.ShapeDtypeStruct((seq, d), BF16)])
    return pl.pallas_call(
        functools.partial(_mix_ffn_kernel, glu=glu_w is not None, final=final),
        grid=(seq // tm,),
        in_specs=in_specs,
        out_specs=out_specs,
        out_shape=out_shape,
        scratch_shapes=[
            pltpu.VMEM((tm, d), BF16),
            pltpu.VMEM((tm, D_FF), BF16),
            pltpu.VMEM((tm + halo, TF_FFN), F32),
            pltpu.VMEM((halo, D_FF), F32),
        ],
        compiler_params=pltpu.CompilerParams(dimension_semantics=("arbitrary",)),
        name="mix_ffn",
    )(*args)


def kernel(x, c, t5_table, mod_w, mod_b, norm1_g, norm2_g, ffn_w_in, ffn_conv_w, ffn_conv_b, ffn_w_out,
           ev_w_in, ev_w_out, diff_lambda, diff_subln_g, band_rel_bias,
           od_w_in, od_w_out, s5_lam_re, s5_lam_im, s5_log_step, s5_b_re, s5_b_im, s5_c_re, s5_c_im,
           s5_d, s5_glu_w, final_g):
    assert x.shape[0] == 1 and x.shape[2] == D_MODEL
    seq = x.shape[1]
    assert seq % TM_PROJ == 0 and seq % (S5_T * S5_TC) == 0
    d = D_MODEL
    xs = x[0]
    mod = _modulation(c, mod_w, mod_b)
    ffn_w_in_b = ffn_w_in.astype(BF16)
    ffn_w_out_b = ffn_w_out.astype(BF16)
    mods = [[mod[i, :, k * d:(k + 1) * d] for k in range(6)] for i in range(DEPTH)]
    h = None
    for i in range(DEPTH):
        sh1, sc1, g1, sh2, sc2, g2 = mods[i]
        w_in = (ev_w_in if i % 2 == 0 else od_w_in)[i // 2].astype(BF16)
        proj_dtype = BF16 if i % 2 == 0 else F32
        if h is None:
            proj = _normproj(xs, norm1_g[i], sc1, sh1, w_in, proj_dtype)
        else:
            proj = _proj(h, w_in, proj_dtype)
        if i % 2 == 0:
            e = i // 2
            lam_init = 0.8 - 0.6 * math.exp(-0.3 * i)
            lp = diff_lambda[e].astype(F32)
            lam = jnp.exp(jnp.sum(lp[0] * lp[1])) - jnp.exp(jnp.sum(lp[2] * lp[3])) + lam_init
            mix_a = _diff_attention(proj, t5_table, lam, diff_subln_g[e], lam_init)
            mix_b = _band_attention(proj, band_rel_bias[e])
            wo, glu_w = ev_w_out[e].astype(BF16), None
        else:
            o = i // 2
            mix_a = _retention(proj)
            mats = _s5_matrices(s5_lam_re[o], s5_lam_im[o], s5_log_step[o], s5_b_re[o], s5_b_im[o],
                                s5_c_re[o], s5_c_im[o], s5_d[o])
            mix_b = _s5(proj, mats)
            wo, glu_w = od_w_out[o].astype(BF16), s5_glu_w[o].astype(BF16)
        if i == DEPTH - 1:
            tail = (final_g,)
        else:
            nsh1, nsc1 = mods[i + 1][0], mods[i + 1][1]
            tail = (norm1_g[i + 1], nsc1, nsh1)
        out = _mix_ffn(xs, mix_a, mix_b, wo, g1, glu_w, norm2_g[i], sc2, sh2, g2,
                       ffn_w_in_b, ffn_conv_w, ffn_conv_b, ffn_w_out_b, tail, layer=i)
        if i == DEPTH - 1:
            xs = out[0]
        else:
            xs, h = out
    return xs[None]
```

```python
import functools
import math

import jax
import jax.numpy as jnp
from jax import lax
from jax.experimental import pallas as pl
from jax.experimental.pallas import tpu as pltpu

F32 = jnp.float32
BF16 = jnp.bfloat16

D_MODEL = 1024
DEPTH = 2
CHUNK = 64
GROUP_WIDTH = D_MODEL // 2
DK_A = 64
DV_A = 2 * DK_A
N_HEADS_A = GROUP_WIDTH // DV_A
DH_B = 64
N_HEADS_B = GROUP_WIDTH // DH_B
LEFT_CHUNKS = 8
REL_CLIP = 2 * CHUNK
NUM_BUCKETS = 32
MAX_DISTANCE = 128
DV_C = 128
DQK_C = DV_C // 2
N_HEADS_C = GROUP_WIDTH // DV_C
ROPE_BASE = 10000.0
S5_CH = GROUP_WIDTH
S5_GROUP = 16
S5_GROUPS = S5_CH // S5_GROUP
S5_STATE = 64
D_FF = ((8 * D_MODEL // 3 + 255) // 256) * 256
CONV_W = 3
EVEN_IN = 3 * N_HEADS_A * DV_A + 3 * N_HEADS_B * DH_B
ODD_IN = 2 * N_HEADS_C * DQK_C + 2 * N_HEADS_C * DV_C + S5_CH
EPS = 1e-6
NEG_INF = -1e30
LOG2E = math.log2(math.e)

LANES = 128
MXU_DIM = 256

TM_PROJ = 1024
TM_FFN = 512
TF_FFN = MXU_DIM
BLK_A = 512
NPART_A = 2
ONES_A = 16
BLK_B = 1024
BAND_B = LEFT_CHUNKS * CHUNK
QW_B = 4 * CHUNK
BLK_C = 512
S5_T = 16
S5_TC = LANES

assert BLK_B % BAND_B == 0 and BLK_B % QW_B == 0 and BAND_B % QW_B == 0
assert BLK_A >= MAX_DISTANCE, "far key blocks must sit in the saturated T5 bucket"
assert DV_A == LANES, "diff-attention statistics are kept lane-replicated beside the accumulator"


def _dot(a, b):
    return jnp.dot(a, b, preferred_element_type=F32)


def _dot_nt(a, b):
    return lax.dot_general(a, b, (((1,), (1,)), ((), ())), preferred_element_type=F32)


def _dot_tn(a, b):
    return lax.dot_general(a, b, (((0,), (0,)), ((), ())), preferred_element_type=F32)


def _const_spec(shape):
    zeros = (0,) * len(shape)
    return pl.BlockSpec(shape, lambda *_: zeros, pipeline_mode=pl.Buffered(1))


def _mod_rmsnorm(x, g, scale, shift):
    y = x * lax.rsqrt(jnp.mean(x * x, axis=-1, keepdims=True) + EPS)
    y = y * g
    return y * (1.0 + scale) + shift


def _mod_kernel(c_ref, w_ref, b_ref, o_ref):
    c = c_ref[...]
    cond = c * jax.nn.sigmoid(c)
    o_ref[0] = jnp.sum(cond * w_ref[0], axis=0, keepdims=True) + b_ref[0]


def _modulation(c, mod_w, mod_b):
    depth, d, n = mod_w.shape
    tn = 1536
    return pl.pallas_call(
        _mod_kernel,
        grid=(depth, n // tn),
        in_specs=[
            pl.BlockSpec((d, 1), lambda i, j: (0, 0)),
            pl.BlockSpec((1, d, tn), lambda i, j: (i, 0, j)),
            pl.BlockSpec((1, 1, tn), lambda i, j: (i, 0, j)),
        ],
        out_specs=pl.BlockSpec((1, 1, tn), lambda i, j: (i, 0, j)),
        out_shape=jax.ShapeDtypeStruct((depth, 1, n), F32),
        name="modulation",
    )(c.reshape(d, 1), mod_w, mod_b.reshape(depth, 1, n))


TN_PROJ = 1024


def _normproj_kernel(x_ref, g_ref, sc_ref, sh_ref, w_ref, o_ref):
    tm, n = o_ref.shape
    half = tm // 2
    for r in range(2):
        rows = slice(r * half, (r + 1) * half)
        h = _mod_rmsnorm(x_ref[rows, :], g_ref[...], sc_ref[...], sh_ref[...]).astype(BF16)
        for j in range(n // TN_PROJ):
            cols = slice(j * TN_PROJ, (j + 1) * TN_PROJ)
            o_ref[rows, cols] = _dot(h, w_ref[:, cols]).astype(o_ref.dtype)


def _normproj(x, g, scale, shift, w, out_dtype):
    seq, d = x.shape
    n = w.shape[1]
    tm = TM_PROJ
    row = pl.BlockSpec((1, d), lambda i: (0, 0))
    return pl.pallas_call(
        _normproj_kernel,
        grid=(seq // tm,),
        in_specs=[pl.BlockSpec((tm, d), lambda i: (i, 0)), row, row, row, _const_spec(w.shape)],
        out_specs=pl.BlockSpec((tm, n), lambda i: (i, 0)),
        out_shape=jax.ShapeDtypeStruct((seq, n), out_dtype),
        compiler_params=pltpu.CompilerParams(dimension_semantics=("parallel",)),
        name="normproj",
    )(x, g.reshape(1, d), scale, shift, w)


def _proj_kernel(h_ref, w_ref, o_ref):
    for j in range(o_ref.shape[1] // TN_PROJ):
        cols = slice(j * TN_PROJ, (j + 1) * TN_PROJ)
        o_ref[:, cols] = _dot(h_ref[...], w_ref[:, cols]).astype(o_ref.dtype)


def _proj(h, w, out_dtype):
    seq, d = h.shape
    n = w.shape[1]
    tm = TM_PROJ
    return pl.pallas_call(
        _proj_kernel,
        grid=(seq // tm,),
        in_specs=[pl.BlockSpec((tm, d), lambda i: (i, 0)), _const_spec(w.shape)],
        out_specs=pl.BlockSpec((tm, n), lambda i: (i, 0)),
        out_shape=jax.ShapeDtypeStruct((seq, n), out_dtype),
        compiler_params=pltpu.CompilerParams(dimension_semantics=("parallel",)),
        name="proj",
    )(h, w)


def _diffattn_kernel(q_ref, k_ref, v_ref, bias_ref, lam_ref, g_ref, o_ref,
                     qs_ref, vt_ref, m_ref, acc_ref, *s_refs, out_scale):
    blk = BLK_A
    nq = 2 * blk
    sub = 8
    dv = DV_A
    npart = len(s_refs) // 4
    wq = nq // npart
    sa_ref, sb_ref = s_refs[:2 * npart], s_refs[2 * npart:]
    i = pl.program_id(1)

    @pl.when(i == 0)
    def _():
        def tr(b, carry):
            r0 = pl.multiple_of(b * blk, blk)
            vt_ref[0:dv, pl.ds(r0, blk)] = v_ref[pl.ds(r0, blk), :].astype(F32).T.astype(BF16)
            vt_ref[dv:dv + ONES_A, pl.ds(r0, blk)] = jnp.ones((ONES_A, blk), BF16)
            return carry
        lax.fori_loop(0, v_ref.shape[0] // blk, tr, 0)

    q = q_ref[...].astype(F32) * (DK_A ** -0.5 * LOG2E)
    lane = lax.broadcasted_iota(jnp.int32, q.shape, 1)
    qs_ref[0:blk, :] = jnp.where(lane < DK_A, q, 0.0).astype(BF16)
    qs_ref[blk:nq, :] = jnp.where(lane >= DK_A, q, 0.0).astype(BF16)
    m_ref[...] = jnp.full(m_ref.shape, NEG_INF, F32)
    acc_ref[...] = jnp.zeros(acc_ref.shape, F32)

    def scores(b, s_ref):
        k = k_ref[pl.ds(pl.multiple_of(b * blk, blk), blk), :]
        for part in range(npart):
            s = _dot_nt(k, qs_ref[part * wq:(part + 1) * wq, :])
            s_ref[part][...] = s
            s_ref[npart + part][...] = jnp.max(s.reshape(blk // sub, sub, wq), axis=0)

    def softmax_pv(b, s_ref, bias):
        vt = vt_ref[:, pl.ds(pl.multiple_of(b * blk, blk), blk)]
        for part in range(npart):
            cols = slice(part * wq, (part + 1) * wq)
            s = s_ref[part][...]
            if bias is not None:
                q0 = (part * wq) % blk
                s = s + bias[:, q0:q0 + wq]
            s = s.reshape(blk // sub, sub, wq)
            m_prev = m_ref[:, cols]
            smax = jnp.max(s, axis=0) if bias is not None else s_ref[npart + part][...]
            m_cur = jnp.max(smax, axis=0, keepdims=True)
            m_new = jnp.maximum(m_prev, m_cur)
            alpha = jnp.exp2(m_prev - m_new)
            p = jnp.exp2(s - m_new[None])
            pv = _dot(vt, p.reshape(blk, wq).astype(BF16))
            acc_ref[:, cols] = acc_ref[:, cols] * alpha[0:1] + pv
            m_ref[:, cols] = m_new

    nfar = jnp.maximum(i - 1, 0)
    odd = lax.rem(nfar, 2)

    @pl.when(i == 0)
    def _():
        scores(0, sb_ref)

    @pl.when(i > 0)
    def _():
        @pl.when(odd == 1)
        def _():
            scores(0, sb_ref)
            scores(1, sa_ref)
            softmax_pv(0, sb_ref, None)

        @pl.when(odd == 0)
        def _():
            scores(0, sa_ref)

        def pair(b):
            scores(b + 1, sb_ref)
            softmax_pv(b, sa_ref, None)
            scores(b + 2, sa_ref)
            softmax_pv(b + 1, sb_ref, None)

        def quad_body(t, carry):
            pair(odd + 4 * t)
            pair(odd + 4 * t + 2)
            return carry

        npairs = nfar // 2
        lax.fori_loop(0, npairs // 2, quad_body, 0)

        @pl.when(lax.rem(npairs, 2) == 1)
        def _():
            pair(odd + 2 * (npairs - 1))
        scores(i, sb_ref)
        softmax_pv(i - 1, sa_ref, bias_ref[0, 0])

    softmax_pv(i, sb_ref, bias_ref[0, 1])

    ot = acc_ref[0:dv, :] / acc_ref[dv:dv + 1, :]
    o = ot[:, 0:blk].T - lam_ref[...] * ot[:, blk:nq].T
    o = o * lax.rsqrt(jnp.mean(o * o, axis=-1, keepdims=True) + EPS) * g_ref[...]
    o_ref[...] = (o * out_scale).astype(o_ref.dtype)


_TOEPLITZ_ROWS = 256
_TOEPLITZ_N = 2048


def _toeplitz_kernel(v_ref, o_ref, *, keep):
    rows, cols = o_ref.shape[1:]
    x = jnp.broadcast_to(v_ref[0, 0], (rows, v_ref.shape[-1]))
    tile = pltpu.roll(x, 0, 1, stride=1, stride_axis=0)[:, :cols]
    r = lax.broadcasted_iota(jnp.int32, (rows, cols), 0) + pl.program_id(1) * rows
    c = lax.broadcasted_iota(jnp.int32, (rows, cols), 1)
    o_ref[0] = jnp.where(keep(r, c), tile, NEG_INF)


def _toeplitz_tiles(fn, keep, heads, rows, cols):
    n, rb = _TOEPLITZ_N, _TOEPLITZ_ROWS
    assert rows % rb == 0 and rows <= n // 2 and cols <= n // 2
    idx = jnp.arange(n, dtype=jnp.int32)
    vec = fn(jnp.where(idx < n // 2, idx, idx - n)).astype(F32)
    vecs = jnp.stack([jnp.roll(vec, k * rb, axis=1) for k in range(rows // rb)], axis=1)
    return pl.pallas_call(
        functools.partial(_toeplitz_kernel, keep=keep),
        grid=(heads, rows // rb),
        in_specs=[pl.BlockSpec((1, 1, 1, n), lambda h, k: (h, k, 0, 0))],
        out_specs=pl.BlockSpec((1, rb, cols), lambda h, k: (h, k, 0)),
        out_shape=jax.ShapeDtypeStruct((heads, rows, cols), F32),
        name="toeplitz_tiles",
    )(vecs.reshape(heads, rows // rb, 1, n))


def _t5_bucket(rel):
    nb = NUM_BUCKETS // 2
    max_exact = nb // 2
    bucket = jnp.where(rel > 0, nb, 0)
    n = jnp.abs(rel)
    nf = jnp.maximum(n, 1).astype(F32)
    large = max_exact + (jnp.log(nf / max_exact) / math.log(MAX_DISTANCE / max_exact)
                         * (nb - max_exact)).astype(jnp.int32)
    large = jnp.minimum(large, nb - 1)
    return bucket + jnp.where(n < max_exact, n, large)


def _diff_bias_tiles(t5_table):
    blk = BLK_A
    table = t5_table.astype(F32)
    far = table[_t5_bucket(jnp.full((), -(blk + 1), jnp.int32))]
    def visible(r, c):
        return jnp.floor_divide(r - blk, CHUNK) <= jnp.floor_divide(c, CHUNK)

    tiles = _toeplitz_tiles(lambda x: ((table[_t5_bucket(-x - blk)] - far) * LOG2E).T, visible,
                            N_HEADS_A, 2 * blk, blk)
    return tiles.reshape(N_HEADS_A, 2, blk, blk)


def _diff_attention(proj, t5_table, lam, subln_g, lam_init):
    seq = proj.shape[0]
    blk = BLK_A
    bias = _diff_bias_tiles(t5_table)
    ha = N_HEADS_A
    kern = functools.partial(_diffattn_kernel, out_scale=1.0 - lam_init)
    return pl.pallas_call(
        kern,
        grid=(ha, seq // blk),
        in_specs=[
            pl.BlockSpec((blk, DV_A), lambda h, i: (i, h)),
            pl.BlockSpec((seq, DV_A), lambda h, i: (0, ha + h)),
            pl.BlockSpec((seq, DV_A), lambda h, i: (0, 2 * ha + h)),
            pl.BlockSpec((1, 2, blk, blk), lambda h, i: (h, 0, 0, 0)),
            pl.BlockSpec((1, DV_A), lambda h, i: (0, 0)),
            pl.BlockSpec((1, DV_A), lambda h, i: (0, 0)),
        ],
        out_specs=pl.BlockSpec((blk, DV_A), lambda h, i: (i, h)),
        out_shape=jax.ShapeDtypeStruct((seq, ha * DV_A), BF16),
        scratch_shapes=[
            pltpu.VMEM((2 * blk, DV_A), BF16),
            pltpu.VMEM((DV_A + ONES_A, seq), BF16),
            pltpu.VMEM((8, 2 * blk), F32),
            pltpu.VMEM((DV_A + ONES_A, 2 * blk), F32),
        ] + 2 * ([pltpu.VMEM((blk, 2 * blk // NPART_A), F32)] * NPART_A
                 + [pltpu.VMEM((8, 2 * blk // NPART_A), F32)] * NPART_A),
        compiler_params=pltpu.CompilerParams(dimension_semantics=("parallel", "arbitrary")),
        name="diff_attention",
    )(proj, proj, proj, bias, jnp.full((1, DV_A), lam, F32), subln_g.reshape(1, DV_A).astype(F32))


def _band_kernel(q_ref, kp_ref, kc_ref, vp_ref, vc_ref, bias_ref, o_ref, *s_refs):
    qw, band = QW_B, BAND_B
    nk = band + qw
    sub = 8
    i = pl.program_id(1)
    q = q_ref[...].astype(F32) * (DH_B ** -0.5 * LOG2E)
    lane = lax.broadcasted_iota(jnp.int32, q.shape, 1)
    qh = (jnp.where(lane < DH_B, q, 0.0).astype(BF16), jnp.where(lane >= DH_B, q, 0.0).astype(BF16))
    k_all = jnp.concatenate([kp_ref[...], kc_ref[...]], axis=0)
    vt_all = jnp.concatenate([vp_ref[...], vc_ref[...]], axis=0).astype(F32).T.astype(BF16)
    bias = jnp.concatenate([bias_ref[0], bias_ref[1]], axis=1)
    no_prev = jnp.where(i == 0, NEG_INF, 0.0).astype(F32)
    krow = lax.broadcasted_iota(jnp.int32, (nk, 2 * qw), 0)
    ngroups = len(s_refs)
    for g in range(ngroups):
        k0 = g * qw
        qs = jnp.concatenate([qh[0][k0:k0 + qw], qh[1][k0:k0 + qw]], axis=0)
        s_refs[g][...] = _dot_nt(k_all[k0:k0 + nk], qs)
    for g in range(ngroups):
        k0 = g * qw
        s = s_refs[g][...] + bias
        if k0 < band:
            s = s + jnp.where(krow < band - k0, no_prev, 0.0)
        s = s.reshape(nk // sub, sub, 2 * qw)
        m = jnp.max(jnp.max(s, axis=0), axis=0, keepdims=True)
        p = jnp.exp2(s - m[None])
        l = jnp.sum(jnp.sum(p, axis=0), axis=0, keepdims=True)
        ot = _dot(vt_all[:, k0:k0 + nk], p.reshape(nk, 2 * qw).astype(BF16)) / l
        o = jnp.concatenate([ot[0:DH_B, 0:qw], ot[DH_B:2 * DH_B, qw:2 * qw]], axis=0)
        o_ref[k0:k0 + qw, :] = o.T.astype(o_ref.dtype)


def _band_bias_tiles(rel_bias):
    band = BAND_B

    def valid(r, c):
        qchunk = jnp.floor_divide(c, CHUNK)
        kchunk = jnp.floor_divide(r - band, CHUNK)
        return (kchunk <= qchunk) & (kchunk >= qchunk - LEFT_CHUNKS)

    return _toeplitz_tiles(
        lambda x: rel_bias.astype(F32)[:, jnp.clip(-x - band, -REL_CLIP, REL_CLIP) + REL_CLIP] * LOG2E, valid,
        N_HEADS_B, band + QW_B, QW_B)


def _band_attention(proj, rel_bias):
    seq = proj.shape[0]
    blk, band, qw = BLK_B, BAND_B, QW_B
    bias = _band_bias_tiles(rel_bias)
    npair = N_HEADS_B // 2
    qc0 = 3 * N_HEADS_A
    per = blk // band
    prev = lambda c0: (lambda hp, i: (jnp.maximum(i * per - 1, 0), c0 + hp))
    cur = lambda c0: (lambda hp, i: (i, c0 + hp))
    return pl.pallas_call(
        _band_kernel,
        grid=(npair, seq // blk),
        in_specs=[
            pl.BlockSpec((blk, LANES), cur(qc0)),
            pl.BlockSpec((band, LANES), prev(qc0 + npair)),
            pl.BlockSpec((blk, LANES), cur(qc0 + npair)),
            pl.BlockSpec((band, LANES), prev(qc0 + 2 * npair)),
            pl.BlockSpec((blk, LANES), cur(qc0 + 2 * npair)),
            pl.BlockSpec((2, band + qw, qw), lambda hp, i: (hp, 0, 0)),
        ],
        out_specs=pl.BlockSpec((blk, LANES), lambda hp, i: (i, hp)),
        out_shape=jax.ShapeDtypeStruct((seq, N_HEADS_B * DH_B), BF16),
        scratch_shapes=[pltpu.VMEM((band + qw, 2 * qw), F32)] * (blk // qw),
        compiler_params=pltpu.CompilerParams(dimension_semantics=("parallel", "arbitrary")),
        name="band_attention",
    )(proj, proj, proj, proj, proj, bias)


def _retention_kernel(qk_ref, v_ref, gate_ref, cos_ref, sin_ref, qdec_ref, kdec_ref, dmat_ref,
                      sdec_ref, o_ref, state_ref):
    @pl.when(pl.program_id(0) == 0)
    def _():
        state_ref[...] = jnp.zeros(state_ref.shape, F32)

    cos = cos_ref[...]
    sin = sin_ref[...]
    lane = lax.broadcasted_iota(jnp.int32, cos.shape, 1)
    first_half = (lane % DQK_C) < (DQK_C // 2)
    qk = qk_ref[...]
    parts = []
    for j in range(qk.shape[1] // LANES):
        t = qk[:, j * LANES:(j + 1) * LANES]
        partner = jnp.where(first_half, pltpu.roll(t, LANES - DQK_C // 2, 1), pltpu.roll(t, DQK_C // 2, 1))
        parts.append(t * cos + partner * sin)
    wq = N_HEADS_C * DQK_C
    q = jnp.concatenate(parts[:wq // LANES], axis=1)
    k = jnp.concatenate(parts[wq // LANES:], axis=1) * (DQK_C ** -0.5)
    qd = (q * qdec_ref[...]).astype(BF16)
    kd = (k * kdec_ref[...]).astype(BF16)
    qb = q.astype(BF16)
    kb = k.astype(BF16)
    vb = v_ref[...].astype(BF16)
    gate = gate_ref[...]
    outs = []
    for h in range(N_HEADS_C):
        qs = slice(h * DQK_C, (h + 1) * DQK_C)
        vs = slice(h * DV_C, (h + 1) * DV_C)
        scores = _dot_nt(qb[:, qs], kb[:, qs]) * dmat_ref[h]
        state = state_ref[h]
        r = _dot(scores.astype(BF16), vb[:, vs]) + _dot(qd[:, qs], state.astype(BF16))
        state_ref[h] = state * sdec_ref[h] + _dot_tn(kd[:, qs], vb[:, vs])
        r = r * lax.rsqrt(jnp.mean(r * r, axis=-1, keepdims=True) + EPS)
        g = gate[:, vs]
        outs.append(r * (g * jax.nn.sigmoid(g)))
    o_ref[...] = jnp.concatenate(outs, axis=1).astype(o_ref.dtype)


def _retention_tables(seq):
    t = BLK_C
    half = DQK_C // 2
    inv_freq = 1.0 / (ROPE_BASE ** (jnp.arange(0, DQK_C, 2, dtype=F32) / DQK_C))
    ang = jnp.arange(seq, dtype=F32)[:, None] * inv_freq[None, :]
    reps = LANES // half
    cos = jnp.tile(jnp.cos(ang), (1, reps))
    sign = jnp.where((jnp.arange(LANES) % DQK_C) < half, -1.0, 1.0).astype(F32)
    sin = jnp.tile(jnp.sin(ang), (1, reps)) * sign[None, :]
    log_g = jnp.log(1.0 - jnp.power(2.0, -5.0 - jnp.arange(N_HEADS_C, dtype=F32)))
    pos = jnp.arange(t, dtype=F32)
    diff = pos[:, None] - pos[None, :]
    same_or_past = (jnp.arange(t)[None, :] // CHUNK) <= (jnp.arange(t)[:, None] // CHUNK)
    dmat = jnp.where(same_or_past[None], jnp.exp(log_g[:, None, None] * jnp.abs(diff)[None]), 0.0)
    qdec = jnp.repeat(jnp.exp(log_g[None, :] * (pos[:, None] + 1.0)), DQK_C, axis=1)
    kdec = jnp.repeat(jnp.exp(log_g[None, :] * (t - 1.0 - pos[:, None])), DQK_C, axis=1)
    sdec = jnp.broadcast_to(jnp.exp(log_g * t)[:, None, None], (N_HEADS_C, 1, DV_C))
    return cos, sin, qdec, kdec, dmat, sdec


def _retention(proj):
    seq = proj.shape[0]
    t = BLK_C
    cos, sin, qdec, kdec, dmat, sdec = _retention_tables(seq)
    wv = N_HEADS_C * DV_C
    return pl.pallas_call(
        _retention_kernel,
        grid=(seq // t,),
        in_specs=[
            pl.BlockSpec((t, wv), lambda i: (i, 0)),
            pl.BlockSpec((t, wv), lambda i: (i, 1)),
            pl.BlockSpec((t, wv), lambda i: (i, 2)),
            pl.BlockSpec((t, LANES), lambda i: (i, 0)),
            pl.BlockSpec((t, LANES), lambda i: (i, 0)),
            pl.BlockSpec((t, N_HEADS_C * DQK_C), lambda i: (0, 0)),
            pl.BlockSpec((t, N_HEADS_C * DQK_C), lambda i: (0, 0)),
            pl.BlockSpec((N_HEADS_C, t, t), lambda i: (0, 0, 0)),
            pl.BlockSpec((N_HEADS_C, 1, DV_C), lambda i: (0, 0, 0)),
        ],
        out_specs=pl.BlockSpec((t, wv), lambda i: (i, 0)),
        out_shape=jax.ShapeDtypeStruct((seq, wv), BF16),
        scratch_shapes=[pltpu.VMEM((N_HEADS_C, DQK_C, DV_C), F32)],
        compiler_params=pltpu.CompilerParams(dimension_semantics=("arbitrary",)),
        name="retention",
    )(proj, proj, proj, cos, sin, qdec, kdec, dmat, sdec)


def _s5_kernel(*refs):
    ncb = S5_CH // LANES
    u_refs = refs[:ncb]
    (mt_ref, bt_ref, ctr_ref, cti_ref, are_ref, aim_ref, y_ref,
     ut_ref, yt_ref, ys_ref, vr_ref, vi_ref, spr_ref, spi_ref, carry_ref) = refs[ncb:]
    tc = S5_TC
    gp = S5_GROUP
    n = S5_STATE
    ng = S5_GROUPS

    @pl.when(pl.program_id(0) == 0)
    def _():
        carry_ref[...] = jnp.zeros(carry_ref.shape, F32)

    for s in range(S5_T):
        for k in range(ncb):
            ut_ref[s, k * LANES:(k + 1) * LANES, :] = u_refs[k][pl.ds(s, tc, stride=S5_T), :].T

    unroll = 4

    def intra(it, carry):
        for k in range(unroll):
            g = it * unroll + k
            r0 = pl.multiple_of(g * gp, gp)
            ug = ut_ref[:, pl.ds(r0, gp), :].reshape(S5_T * gp, tc).astype(BF16)
            yt_ref[:, pl.ds(r0, gp), :] = _dot(mt_ref[g], ug).reshape(S5_T, gp, tc)
            vt = _dot(bt_ref[g], ug)
            n0 = pl.multiple_of(g * n, n)
            vr_ref[pl.ds(n0, n), :] = vt[0:n]
            vi_ref[pl.ds(n0, n), :] = vt[n:2 * n]
        return carry

    lax.fori_loop(0, ng // unroll, intra, 0)

    sub = 8
    nv = tc // sub
    row = lax.broadcasted_iota(jnp.int32, (tc, LANES), 0)
    in_vreg = lax.rem(row, sub)

    def rows_of(v, r):
        return jnp.broadcast_to(v[r:r + 1], (tc, LANES))

    for j in range(ng * n // LANES):
        cols = slice(j * LANES, (j + 1) * LANES)
        pwr, pwi = are_ref[:, cols], aim_ref[:, cols]
        xr = vr_ref[cols, :].T
        xi = vi_ref[cols, :].T
        for d in (1, 2, 4):
            keep = in_vreg >= d
            sr = jnp.where(keep, pltpu.roll(xr, d, 0), 0.0)
            si = jnp.where(keep, pltpu.roll(xi, d, 0), 0.0)
            fr, fi = rows_of(pwr, d - 1), rows_of(pwi, d - 1)
            xr, xi = xr + (fr * sr - fi * si), xi + (fr * si + fi * sr)
        cr, ci = carry_ref[0, :, cols], carry_ref[1, :, cols]
        cr0, ci0 = cr, ci
        outr, outi = [], []
        for v in range(nv):
            yr = xr[v * sub:(v + 1) * sub] + (pwr * cr - pwi * ci)
            yi = xi[v * sub:(v + 1) * sub] + (pwr * ci + pwi * cr)
            outr.append(yr)
            outi.append(yi)
            cr = jnp.broadcast_to(yr[sub - 1:sub], (sub, LANES))
            ci = jnp.broadcast_to(yi[sub - 1:sub], (sub, LANES))
        carry_ref[0, :, cols] = cr
        carry_ref[1, :, cols] = ci
        sr = jnp.concatenate(outr, axis=0)
        si = jnp.concatenate(outi, axis=0)
        first = row == 0
        spr_ref[j] = jnp.where(first, rows_of(cr0, 0), pltpu.roll(sr, 1, 0))
        spi_ref[j] = jnp.where(first, rows_of(ci0, 0), pltpu.roll(si, 1, 0))

    def cross(it, carry):
        for k in range(unroll):
            jp = it * unroll + k
            r0 = pl.multiple_of(jp * 2 * gp, 2 * gp)
            yc = (_dot_nt(ctr_ref[jp], spr_ref[jp].astype(BF16))
                  + _dot_nt(cti_ref[jp], spi_ref[jp].astype(BF16)))
            yt_ref[:, pl.ds(r0, 2 * gp), :] += yc.reshape(S5_T, 2 * gp, tc)
        return carry

    lax.fori_loop(0, ng // 2 // unroll, cross, 0)

    for s in range(S5_T):
        for k in range(ncb):
            ys_ref[k, pl.ds(s, tc, stride=S5_T), :] = yt_ref[s, k * LANES:(k + 1) * LANES, :].T
    for k in range(ncb):
        y_ref[:, k * LANES:(k + 1) * LANES] = ys_ref[k]


def _s5_matrices(lam_re, lam_im, log_step, b_re, b_im, c_re, c_im, d_skip):
    hi = lax.Precision.HIGHEST
    t, gp, n, ng = S5_T, S5_GROUP, S5_STATE, S5_GROUPS
    lam = lax.complex(lam_re.astype(F32), lam_im.astype(F32))
    step = jnp.exp(log_step.astype(F32))[:, None]
    ls = lam * step
    a_bar = jnp.exp(ls)
    b_bar = ((a_bar - 1.0) / lam)[..., None] * lax.complex(b_re.astype(F32), b_im.astype(F32))
    cm = lax.complex(c_re.astype(F32), c_im.astype(F32))

    def apow(k):
        kk = k.astype(F32).astype(jnp.complex64)
        return jnp.exp(ls.reshape((ng,) + (1,) * k.ndim + (n,)) * kk[None, ..., None])

    tt = jnp.arange(t)
    kmat = jnp.einsum('gpn,gln,gnq->glpq', cm, apow(tt), b_bar, precision=hi).real
    krev = jnp.transpose(kmat[:, ::-1], (0, 2, 1, 3)).reshape(ng, gp, t * gp)
    kpad = jnp.pad(krev, ((0, 0), (0, 0), (0, t * gp)))
    mt = jnp.concatenate([kpad[:, :, (t - 1 - to) * gp:(2 * t - 1 - to) * gp] for to in range(t)], axis=1)
    dvec = jnp.tile(d_skip.astype(F32).reshape(ng, 1, gp), (1, t, 1)).reshape(ng, t * gp)
    mt = mt + jnp.eye(t * gp, dtype=F32)[None] * dvec[:, :, None]
    z = jnp.swapaxes(apow(t - 1 - tt), 1, 2)[:, :, :, None] * b_bar[:, :, None, :]
    z = z.reshape(ng, n, t * gp)
    bt = jnp.concatenate([z.real, z.imag], axis=1)
    w = cm[:, None, :, :] * apow(tt + 1)[:, :, None, :]

    def pair_readout(x):
        x = x.reshape(ng // 2, 2, t, gp, n)
        first = jnp.pad(x[:, 0], ((0, 0), (0, 0), (0, 0), (0, n)))
        second = jnp.pad(x[:, 1], ((0, 0), (0, 0), (0, 0), (n, 0)))
        return jnp.stack([first, second], axis=2).reshape(ng // 2, t * 2 * gp, 2 * n).astype(BF16)

    ctr, cti = pair_readout(w.real), pair_readout(-w.imag)
    a_chunk = jnp.transpose(apow(t * (jnp.arange(8) + 1)), (1, 0, 2)).reshape(8, ng * n)
    return mt.astype(BF16), bt.astype(BF16), ctr, cti, a_chunk.real, a_chunk.imag


def _s5(proj, mats):
    seq, width = proj.shape
    t, tc, gp, n, ng = S5_T, S5_TC, S5_GROUP, S5_STATE, S5_GROUPS
    rows = t * tc
    ncb = S5_CH // LANES
    cb0 = (width - S5_CH) // LANES
    u_specs = [pl.BlockSpec((rows, LANES), (lambda i, k=k: (i, cb0 + k))) for k in range(ncb)]
    nsb = ng * n // LANES
    return pl.pallas_call(
        _s5_kernel,
        grid=(seq // rows,),
        in_specs=u_specs + [_const_spec(m.shape) for m in mats],
        out_specs=pl.BlockSpec((rows, S5_CH), lambda i: (i, 0)),
        out_shape=jax.ShapeDtypeStruct((seq, S5_CH), F32),
        scratch_shapes=[
            pltpu.VMEM((t, S5_CH, tc), F32),
            pltpu.VMEM((t, S5_CH, tc), F32),
            pltpu.VMEM((ncb, rows, LANES), F32),
            pltpu.VMEM((ng * n, tc), F32),
            pltpu.VMEM((ng * n, tc), F32),
            pltpu.VMEM((nsb, tc, LANES), F32),
            pltpu.VMEM((nsb, tc, LANES), F32),
            pltpu.VMEM((2, 8, ng * n), F32),
        ],
        compiler_params=pltpu.CompilerParams(dimension_semantics=("arbitrary",)),
        name="s5_scan",
    )(*([proj] * ncb), *mats)


def _mix_ffn_kernel(*refs, glu, final):
    (x_ref, a_ref, b_ref, wo_ref, g1_ref), refs = refs[:5], refs[5:]
    if glu:
        gw_ref, refs = refs[0], refs[1:]
    (g_ref, sc_ref, sh_ref, gate_ref, win_ref, cw_ref, cb_ref, wout_ref), refs = refs[:8], refs[8:]
    if final:
        fg_ref, o_ref, h_ref, act_ref, gbuf_ref, carry_ref = refs
    else:
        ng_ref, nsc_ref, nsh_ref, o_ref, hn_ref, h_ref, act_ref, gbuf_ref, carry_ref = refs
    tm = x_ref.shape[0]
    halo = gbuf_ref.shape[0] - tm

    @pl.when(pl.program_id(0) == 0)
    def _():
        carry_ref[...] = jnp.zeros(carry_ref.shape, F32)

    if glu:
        y = jax.nn.gelu(b_ref[...]).astype(BF16)
        gg = _dot(y, gw_ref[...])
        half = gg.shape[1] // 2
        b = (gg[:, :half] * jax.nn.sigmoid(gg[:, half:])).astype(BF16)
    else:
        b = b_ref[...]
    cat = jnp.concatenate([a_ref[...], b], axis=1)
    x = x_ref[...] + g1_ref[...] * _dot(cat, wo_ref[...])
    h_ref[...] = _mod_rmsnorm(x, g_ref[...], sc_ref[...], sh_ref[...]).astype(BF16)
    for f in range(D_FF // TF_FFN):
        cs = slice(f * TF_FFN, (f + 1) * TF_FFN)
        gs = slice(D_FF + f * TF_FFN, D_FF + (f + 1) * TF_FFN)
        h = h_ref[...]
        val = _dot(h, win_ref[:, cs])
        gate = _dot(h, win_ref[:, gs])
        gbuf_ref[0:halo, :] = carry_ref[:, cs]
        gbuf_ref[halo:halo + tm, :] = gate
        carry_ref[:, cs] = gate[tm - halo:tm, :]
        conv = (gate * cw_ref[2:3, cs] + gbuf_ref[halo - 1:halo - 1 + tm, :] * cw_ref[1:2, cs]
                + gbuf_ref[halo - 2:halo - 2 + tm, :] * cw_ref[0:1, cs] + cb_ref[:, cs])
        act_ref[:, cs] = (jax.nn.gelu(conv) * val).astype(BF16)
    xn = x + gate_ref[...] * _dot(act_ref[...], wout_ref[...])
    if final:
        xn = xn * lax.rsqrt(jnp.mean(xn * xn, axis=-1, keepdims=True) + EPS) * fg_ref[...]
    else:
        hn_ref[...] = _mod_rmsnorm(xn, ng_ref[...], nsc_ref[...], nsh_ref[...]).astype(BF16)
    o_ref[...] = xn


def _layer_spec(shape, layer):
    idx = (layer,) + (0,) * (len(shape) - 1)
    return pl.BlockSpec((None,) + tuple(shape[1:]), lambda *_: idx, pipeline_mode=pl.Buffered(1))


def _mix_ffn(x, a, b, wo, gate1, glu_w, g, scale, shift, gate2, w_in, conv_w, conv_b, w_out, tail, layer):
    seq, d = x.shape
    final = len(tail) == 1
    tm = TM_FFN
    halo = 8
    row = pl.BlockSpec((1, d), lambda i: (0, 0))
    rows = lambda w: pl.BlockSpec((tm, w), lambda i: (i, 0))
    conv_b = conv_b.reshape(conv_b.shape[0], 1, D_FF)
    in_specs = [rows(d), rows(a.shape[1]), rows(b.shape[1]), _const_spec(wo.shape), row]
    args = [x, a, b, wo, gate1]
    if glu_w is not None:
        in_specs.append(_const_spec(glu_w.shape))
        args.append(glu_w)
    in_specs += [
        row, row, row, row,
        _layer_spec(w_in.shape, layer),
        _layer_spec(conv_w.shape, layer),
        _layer_spec(conv_b.shape, layer),
        _layer_spec(w_out.shape, layer),
    ] + [row] * len(tail)
    args += [g.reshape(1, d), scale, shift, gate2, w_in, conv_w, conv_b, w_out]
    args += [t.reshape(1, d) for t in tail]
    out_specs = [rows(d)] if final else [rows(d), rows(d)]
    out_shape = [jax.ShapeDtypeStruct((seq, d), F32)] + ([] if final else [jax.ShapeDtypeStruct((seq, d), BF16)])
    return pl.pallas_call(
        functools.partial(_mix_ffn_kernel, glu=glu_w is not None, final=final),
        grid=(seq // tm,),
        in_specs=in_specs,
        out_specs=out_specs,
        out_shape=out_shape,
        scratch_shapes=[
            pltpu.VMEM((tm, d), BF16),
            pltpu.VMEM((tm, D_FF), BF16),
            pltpu.VMEM((tm + halo, TF_FFN), F32),
            pltpu.VMEM((halo, D_FF), F32),
        ],
        compiler_params=pltpu.CompilerParams(dimension_semantics=("arbitrary",)),
        name="mix_ffn",
    )(*args)


def kernel(x, c, t5_table, mod_w, mod_b, norm1_g, norm2_g, ffn_w_in, ffn_conv_w, ffn_conv_b, ffn_w_out,
           ev_w_in, ev_w_out, diff_lambda, diff_subln_g, band_rel_bias,
           od_w_in, od_w_out, s5_lam_re, s5_lam_im, s5_log_step, s5_b_re, s5_b_im, s5_c_re, s5_c_im,
           s5_d, s5_glu_w, final_g):
    assert x.shape[0] == 1 and x.shape[2] == D_MODEL
    seq = x.shape[1]
    assert seq % TM_PROJ == 0 and seq % (S5_T * S5_TC) == 0
    d = D_MODEL
    xs = x[0]
    mod = _modulation(c, mod_w, mod_b)
    ffn_w_in_b = ffn_w_in.astype(BF16)
    ffn_w_out_b = ffn_w_out.astype(BF16)
    mods = [[mod[i, :, k * d:(k + 1) * d] for k in range(6)] for i in range(DEPTH)]
    h = None
    for i in range(DEPTH):
        sh1, sc1, g1, sh2, sc2, g2 = mods[i]
        w_in = (ev_w_in if i % 2 == 0 else od_w_in)[i // 2].astype(BF16)
        proj_dtype = BF16 if i % 2 == 0 else F32
        if h is None:
            proj = _normproj(xs, norm1_g[i], sc1, sh1, w_in, proj_dtype)
        else:
            proj = _proj(h, w_in, proj_dtype)
        if i % 2 == 0:
            e = i // 2
            lam_init = 0.8 - 0.6 * math.exp(-0.3 * i)
            lp = diff_lambda[e].astype(F32)
            lam = jnp.exp(jnp.sum(lp[0] * lp[1])) - jnp.exp(jnp.sum(lp[2] * lp[3])) + lam_init
            mix_a = _diff_attention(proj, t5_table, lam, diff_subln_g[e], lam_init)
            mix_b = _band_attention(proj, band_rel_bias[e])
            wo, glu_w = ev_w_out[e].astype(BF16), None
        else:
            o = i // 2
            mix_a = _retention(proj)
            mats = _s5_matrices(s5_lam_re[o], s5_lam_im[o], s5_log_step[o], s5_b_re[o], s5_b_im[o],
                                s5_c_re[o], s5_c_im[o], s5_d[o])
            mix_b = _s5(proj, mats)
            wo, glu_w = od_w_out[o].astype(BF16), s5_glu_w[o].astype(BF16)
        if i == DEPTH - 1:
            tail = (final_g,)
        else:
            nsh1, nsc1 = mods[i + 1][0], mods[i + 1][1]
            tail = (norm1_g[i + 1], nsc1, nsh1)
        out = _mix_ffn(xs, mix_a, mix_b, wo, g1, glu_w, norm2_g[i], sc2, sh2, g2,
                       ffn_w_in_b, ffn_conv_w, ffn_conv_b, ffn_w_out_b, tail, layer=i)
        if i == DEPTH - 1:
            xs = out[0]
        else:
            xs, h = out
    return xs[None]
```

```python
import functools
import math

import jax
import jax.numpy as jnp
from jax import lax
from jax.experimental import pallas as pl
from jax.experimental.pallas import tpu as pltpu

F32 = jnp.float32
BF16 = jnp.bfloat16

D_MODEL = 1024
DEPTH = 2
CHUNK = 64
GROUP_WIDTH = D_MODEL // 2
DK_A = 64
DV_A = 2 * DK_A
N_HEADS_A = GROUP_WIDTH // DV_A
DH_B = 64
N_HEADS_B = GROUP_WIDTH // DH_B
LEFT_CHUNKS = 8
REL_CLIP = 2 * CHUNK
NUM_BUCKETS = 32
MAX_DISTANCE = 128
DV_C = 128
DQK_C = DV_C // 2
N_HEADS_C = GROUP_WIDTH // DV_C
ROPE_BASE = 10000.0
S5_CH = GROUP_WIDTH
S5_GROUP = 16
S5_GROUPS = S5_CH // S5_GROUP
S5_STATE = 64
D_FF = ((8 * D_MODEL // 3 + 255) // 256) * 256
CONV_W = 3
EVEN_IN = 3 * N_HEADS_A * DV_A + 3 * N_HEADS_B * DH_B
ODD_IN = 2 * N_HEADS_C * DQK_C + 2 * N_HEADS_C * DV_C + S5_CH
EPS = 1e-6
NEG_INF = -1e30
LOG2E = math.log2(math.e)

LANES = 128
MXU_DIM = 256

TM_PROJ = 1024
TM_FFN = 512
TF_FFN = MXU_DIM
BLK_A = 512
NPART_A = 2
ONES_A = 16
BLK_B = 1024
BAND_B = LEFT_CHUNKS * CHUNK
QW_B = 4 * CHUNK
BLK_C = 512
S5_T = 16
S5_TC = LANES

assert BLK_B % BAND_B == 0 and BLK_B % QW_B == 0 and BAND_B % QW_B == 0
assert BLK_A >= MAX_DISTANCE, "far key blocks must sit in the saturated T5 bucket"
assert DV_A == LANES, "diff-attention statistics are kept lane-replicated beside the accumulator"


def _dot(a, b):
    return jnp.dot(a, b, preferred_element_type=F32)


def _dot_nt(a, b):
    return lax.dot_general(a, b, (((1,), (1,)), ((), ())), preferred_element_type=F32)


def _dot_tn(a, b):
    return lax.dot_general(a, b, (((0,), (0,)), ((), ())), preferred_element_type=F32)


def _const_spec(shape):
    zeros = (0,) * len(shape)
    return pl.BlockSpec(shape, lambda *_: zeros, pipeline_mode=pl.Buffered(1))


def _mod_rmsnorm(x, g, scale, shift):
    y = x * lax.rsqrt(jnp.mean(x * x, axis=-1, keepdims=True) + EPS)
    y = y * g
    return y * (1.0 + scale) + shift


def _mod_kernel(c_ref, w_ref, b_ref, o_ref):
    c = c_ref[...]
    cond = c * jax.nn.sigmoid(c)
    o_ref[0] = jnp.sum(cond * w_ref[0], axis=0, keepdims=True) + b_ref[0]


def _modulation(c, mod_w, mod_b):
    depth, d, n = mod_w.shape
    tn = 1536
    return pl.pallas_call(
        _mod_kernel,
        grid=(depth, n // tn),
        in_specs=[
            pl.BlockSpec((d, 1), lambda i, j: (0, 0)),
            pl.BlockSpec((1, d, tn), lambda i, j: (i, 0, j)),
            pl.BlockSpec((1, 1, tn), lambda i, j: (i, 0, j)),
        ],
        out_specs=pl.BlockSpec((1, 1, tn), lambda i, j: (i, 0, j)),
        out_shape=jax.ShapeDtypeStruct((depth, 1, n), F32),
        name="modulation",
    )(c.reshape(d, 1), mod_w, mod_b.reshape(depth, 1, n))


TN_PROJ = 1024


def _normproj_kernel(x_ref, g_ref, sc_ref, sh_ref, w_ref, o_ref):
    tm, n = o_ref.shape
    half = tm // 2
    for r in range(2):
        rows = slice(r * half, (r + 1) * half)
        h = _mod_rmsnorm(x_ref[rows, :], g_ref[...], sc_ref[...], sh_ref[...]).astype(BF16)
        for j in range(n // TN_PROJ):
            cols = slice(j * TN_PROJ, (j + 1) * TN_PROJ)
            o_ref[rows, cols] = _dot(h, w_ref[:, cols]).astype(o_ref.dtype)


def _normproj(x, g, scale, shift, w, out_dtype):
    seq, d = x.shape
    n = w.shape[1]
    tm = TM_PROJ
    row = pl.BlockSpec((1, d), lambda i: (0, 0))
    return pl.pallas_call(
        _normproj_kernel,
        grid=(seq // tm,),
        in_specs=[pl.BlockSpec((tm, d), lambda i: (i, 0)), row, row, row, _const_spec(w.shape)],
        out_specs=pl.BlockSpec((tm, n), lambda i: (i, 0)),
        out_shape=jax.ShapeDtypeStruct((seq, n), out_dtype),
        compiler_params=pltpu.CompilerParams(dimension_semantics=("parallel",)),
        name="normproj",
    )(x, g.reshape(1, d), scale, shift, w)


def _proj_kernel(h_ref, w_ref, o_ref):
    for j in range(o_ref.shape[1] // TN_PROJ):
        cols = slice(j * TN_PROJ, (j + 1) * TN_PROJ)
        o_ref[:, cols] = _dot(h_ref[...], w_ref[:, cols]).astype(o_ref.dtype)


def _proj(h, w, out_dtype):
    seq, d = h.shape
    n = w.shape[1]
    tm = TM_PROJ
    return pl.pallas_call(
        _proj_kernel,
        grid=(seq // tm,),
        in_specs=[pl.BlockSpec((tm, d), lambda i: (i, 0)), _const_spec(w.shape)],
        out_specs=pl.BlockSpec((tm, n), lambda i: (i, 0)),
        out_shape=jax.ShapeDtypeStruct((seq, n), out_dtype),
        compiler_params=pltpu.CompilerParams(dimension_semantics=("parallel",)),
        name="proj",
    )(h, w)


def _diffattn_kernel(q_ref, k_ref, v_ref, bias_ref, lam_ref, g_ref, o_ref,
                     qs_ref, vt_ref, m_ref, acc_ref, *s_refs, out_scale):
    blk = BLK_A
    nq = 2 * blk
    sub = 8
    dv = DV_A
    npart = len(s_refs) // 4
    wq = nq // npart
    sa_ref, sb_ref = s_refs[:2 * npart], s_refs[2 * npart:]
    i = pl.program_id(1)

    @pl.when(i == 0)
    def _():
        def tr(b, carry):
            r0 = pl.multiple_of(b * blk, blk)
            vt_ref[0:dv, pl.ds(r0, blk)] = v_ref[pl.ds(r0, blk), :].astype(F32).T.astype(BF16)
            vt_ref[dv:dv + ONES_A, pl.ds(r0, blk)] = jnp.ones((ONES_A, blk), BF16)
            return carry
        lax.fori_loop(0, v_ref.shape[0] // blk, tr, 0)

    q = q_ref[...].astype(F32) * (DK_A ** -0.5 * LOG2E)
    lane = lax.broadcasted_iota(jnp.int32, q.shape, 1)
    qs_ref[:, 0:blk] = jnp.where(lane < DK_A, q, 0.0).T.astype(BF16)
    qs_ref[:, blk:nq] = jnp.where(lane >= DK_A, q, 0.0).T.astype(BF16)
    m_ref[...] = jnp.full(m_ref.shape, NEG_INF, F32)
    acc_ref[...] = jnp.zeros(acc_ref.shape, F32)

    def scores(b, s_ref):
        k = k_ref[pl.ds(pl.multiple_of(b * blk, blk), blk), :]
        for part in range(npart):
            s = _dot(k, qs_ref[:, part * wq:(part + 1) * wq])
            s_ref[part][...] = s
            s_ref[npart + part][...] = jnp.max(s.reshape(blk // sub, sub, wq), axis=0)

    def softmax_pv(b, s_ref, bias):
        vt = vt_ref[:, pl.ds(pl.multiple_of(b * blk, blk), blk)]
        for part in range(npart):
            cols = slice(part * wq, (part + 1) * wq)
            s = s_ref[part][...]
            if bias is not None:
                q0 = (part * wq) % blk
                s = s + bias[:, q0:q0 + wq]
            s = s.reshape(blk // sub, sub, wq)
            m_prev = m_ref[:, cols]
            smax = jnp.max(s, axis=0) if bias is not None else s_ref[npart + part][...]
            m_cur = jnp.max(smax, axis=0, keepdims=True)
            m_new = jnp.maximum(m_prev, m_cur)
            alpha = jnp.exp2(m_prev - m_new)
            p = jnp.exp2(s - m_new[None])
            pv = _dot(vt, p.reshape(blk, wq).astype(BF16))
            acc_ref[:, cols] = acc_ref[:, cols] * alpha[0:1] + pv
            m_ref[:, cols] = m_new

    nfar = jnp.maximum(i - 1, 0)
    odd = lax.rem(nfar, 2)

    @pl.when(i == 0)
    def _():
        scores(0, sb_ref)

    @pl.when(i > 0)
    def _():
        @pl.when(odd == 1)
        def _():
            scores(0, sb_ref)
            scores(1, sa_ref)
            softmax_pv(0, sb_ref, None)

        @pl.when(odd == 0)
        def _():
            scores(0, sa_ref)

        def pair(b):
            scores(b + 1, sb_ref)
            softmax_pv(b, sa_ref, None)
            scores(b + 2, sa_ref)
            softmax_pv(b + 1, sb_ref, None)

        def quad_body(t, carry):
            pair(odd + 4 * t)
            pair(odd + 4 * t + 2)
            return carry

        npairs = nfar // 2
        lax.fori_loop(0, npairs // 2, quad_body, 0)

        @pl.when(lax.rem(npairs, 2) == 1)
        def _():
            pair(odd + 2 * (npairs - 1))
        scores(i, sb_ref)
        softmax_pv(i - 1, sa_ref, bias_ref[0, 0])

    softmax_pv(i, sb_ref, bias_ref[0, 1])

    ot = acc_ref[0:dv, :] / acc_ref[dv:dv + 1, :]
    o = ot[:, 0:blk].T - lam_ref[...] * ot[:, blk:nq].T
    o = o * lax.rsqrt(jnp.mean(o * o, axis=-1, keepdims=True) + EPS) * g_ref[...]
    o_ref[...] = (o * out_scale).astype(o_ref.dtype)


_TOEPLITZ_ROWS = 256
_TOEPLITZ_N = 2048


def _toeplitz_kernel(v_ref, o_ref, *, keep):
    rows, cols = o_ref.shape[1:]
    x = jnp.broadcast_to(v_ref[0, 0], (rows, v_ref.shape[-1]))
    tile = pltpu.roll(x, 0, 1, stride=1, stride_axis=0)[:, :cols]
    r = lax.broadcasted_iota(jnp.int32, (rows, cols), 0) + pl.program_id(1) * rows
    c = lax.broadcasted_iota(jnp.int32, (rows, cols), 1)
    o_ref[0] = jnp.where(keep(r, c), tile, NEG_INF)


def _toeplitz_tiles(fn, keep, heads, rows, cols):
    n, rb = _TOEPLITZ_N, _TOEPLITZ_ROWS
    assert rows % rb == 0 and rows <= n // 2 and cols <= n // 2
    idx = jnp.arange(n, dtype=jnp.int32)
    vec = fn(jnp.where(idx < n // 2, idx, idx - n)).astype(F32)
    vecs = jnp.stack([jnp.roll(vec, k * rb, axis=1) for k in range(rows // rb)], axis=1)
    return pl.pallas_call(
        functools.partial(_toeplitz_kernel, keep=keep),
        grid=(heads, rows // rb),
        in_specs=[pl.BlockSpec((1, 1, 1, n), lambda h, k: (h, k, 0, 0))],
        out_specs=pl.BlockSpec((1, rb, cols), lambda h, k: (h, k, 0)),
        out_shape=jax.ShapeDtypeStruct((heads, rows, cols), F32),
        name="toeplitz_tiles",
    )(vecs.reshape(heads, rows // rb, 1, n))


def _t5_bucket(rel):
    nb = NUM_BUCKETS // 2
    max_exact = nb // 2
    bucket = jnp.where(rel > 0, nb, 0)
    n = jnp.abs(rel)
    nf = jnp.maximum(n, 1).astype(F32)
    large = max_exact + (jnp.log(nf / max_exact) / math.log(MAX_DISTANCE / max_exact)
                         * (nb - max_exact)).astype(jnp.int32)
    large = jnp.minimum(large, nb - 1)
    return bucket + jnp.where(n < max_exact, n, large)


def _diff_bias_tiles(t5_table):
    blk = BLK_A
    table = t5_table.astype(F32)
    far = table[_t5_bucket(jnp.full((), -(blk + 1), jnp.int32))]
    def visible(r, c):
        return jnp.floor_divide(r - blk, CHUNK) <= jnp.floor_divide(c, CHUNK)

    tiles = _toeplitz_tiles(lambda x: ((table[_t5_bucket(-x - blk)] - far) * LOG2E).T, visible,
                            N_HEADS_A, 2 * blk, blk)
    return tiles.reshape(N_HEADS_A, 2, blk, blk)


def _diff_attention(proj, t5_table, lam, subln_g, lam_init):
    seq = proj.shape[0]
    blk = BLK_A
    bias = _diff_bias_tiles(t5_table)
    ha = N_HEADS_A
    kern = functools.partial(_diffattn_kernel, out_scale=1.0 - lam_init)
    return pl.pallas_call(
        kern,
        grid=(ha, seq // blk),
        in_specs=[
            pl.BlockSpec((blk, DV_A), lambda h, i: (i, h)),
            pl.BlockSpec((seq, DV_A), lambda h, i: (0, ha + h)),
            pl.BlockSpec((seq, DV_A), lambda h, i: (0, 2 * ha + h)),
            pl.BlockSpec((1, 2, blk, blk), lambda h, i: (h, 0, 0, 0)),
            pl.BlockSpec((1, DV_A), lambda h, i: (0, 0)),
            pl.BlockSpec((1, DV_A), lambda h, i: (0, 0)),
        ],
        out_specs=pl.BlockSpec((blk, DV_A), lambda h, i: (i, h)),
        out_shape=jax.ShapeDtypeStruct((seq, ha * DV_A), BF16),
        scratch_shapes=[
            pltpu.VMEM((DV_A, 2 * blk), BF16),
            pltpu.VMEM((DV_A + ONES_A, seq), BF16),
            pltpu.VMEM((8, 2 * blk), F32),
            pltpu.VMEM((DV_A + ONES_A, 2 * blk), F32),
        ] + 2 * ([pltpu.VMEM((blk, 2 * blk // NPART_A), F32)] * NPART_A
                 + [pltpu.VMEM((8, 2 * blk // NPART_A), F32)] * NPART_A),
        compiler_params=pltpu.CompilerParams(dimension_semantics=("parallel", "arbitrary")),
        name="diff_attention",
    )(proj, proj, proj, bias, jnp.full((1, DV_A), lam, F32), subln_g.reshape(1, DV_A).astype(F32))


def _band_kernel(q_ref, kp_ref, kc_ref, vp_ref, vc_ref, bias_ref, o_ref, *s_refs):
    qw, band = QW_B, BAND_B
    nk = band + qw
    sub = 8
    i = pl.program_id(1)
    q = q_ref[...].astype(F32) * (DH_B ** -0.5 * LOG2E)
    lane = lax.broadcasted_iota(jnp.int32, q.shape, 1)
    qh = (jnp.where(lane < DH_B, q, 0.0).T.astype(BF16), jnp.where(lane >= DH_B, q, 0.0).T.astype(BF16))
    k_all = jnp.concatenate([kp_ref[...], kc_ref[...]], axis=0)
    vt_all = jnp.concatenate([vp_ref[...], vc_ref[...]], axis=0).astype(F32).T.astype(BF16)
    bias = jnp.concatenate([bias_ref[0], bias_ref[1]], axis=1)
    no_prev = jnp.where(i == 0, NEG_INF, 0.0).astype(F32)
    krow = lax.broadcasted_iota(jnp.int32, (nk, 2 * qw), 0)
    ngroups = len(s_refs)
    for g in range(ngroups):
        k0 = g * qw
        qs = jnp.concatenate([qh[0][:, k0:k0 + qw], qh[1][:, k0:k0 + qw]], axis=1)
        s_refs[g][...] = _dot(k_all[k0:k0 + nk], qs)
    for g in range(ngroups):
        k0 = g * qw
        s = s_refs[g][...] + bias
        if k0 < band:
            s = s + jnp.where(krow < band - k0, no_prev, 0.0)
        s = s.reshape(nk // sub, sub, 2 * qw)
        m = jnp.max(jnp.max(s, axis=0), axis=0, keepdims=True)
        p = jnp.exp2(s - m[None])
        l = jnp.sum(jnp.sum(p, axis=0), axis=0, keepdims=True)
        ot = _dot(vt_all[:, k0:k0 + nk], p.reshape(nk, 2 * qw).astype(BF16)) / l
        o = jnp.concatenate([ot[0:DH_B, 0:qw], ot[DH_B:2 * DH_B, qw:2 * qw]], axis=0)
        o_ref[k0:k0 + qw, :] = o.T.astype(o_ref.dtype)


def _band_bias_tiles(rel_bias):
    band = BAND_B

    def valid(r, c):
        qchunk = jnp.floor_divide(c, CHUNK)
        kchunk = jnp.floor_divide(r - band, CHUNK)
        return (kchunk <= qchunk) & (kchunk >= qchunk - LEFT_CHUNKS)

    return _toeplitz_tiles(
        lambda x: rel_bias.astype(F32)[:, jnp.clip(-x - band, -REL_CLIP, REL_CLIP) + REL_CLIP] * LOG2E, valid,
        N_HEADS_B, band + QW_B, QW_B)


def _band_attention(proj, rel_bias):
    seq = proj.shape[0]
    blk, band, qw = BLK_B, BAND_B, QW_B
    bias = _band_bias_tiles(rel_bias)
    npair = N_HEADS_B // 2
    qc0 = 3 * N_HEADS_A
    per = blk // band
    prev = lambda c0: (lambda hp, i: (jnp.maximum(i * per - 1, 0), c0 + hp))
    cur = lambda c0: (lambda hp, i: (i, c0 + hp))
    return pl.pallas_call(
        _band_kernel,
        grid=(npair, seq // blk),
        in_specs=[
            pl.BlockSpec((blk, LANES), cur(qc0)),
            pl.BlockSpec((band, LANES), prev(qc0 + npair)),
            pl.BlockSpec((blk, LANES), cur(qc0 + npair)),
            pl.BlockSpec((band, LANES), prev(qc0 + 2 * npair)),
            pl.BlockSpec((blk, LANES), cur(qc0 + 2 * npair)),
            pl.BlockSpec((2, band + qw, qw), lambda hp, i: (hp, 0, 0)),
        ],
        out_specs=pl.BlockSpec((blk, LANES), lambda hp, i: (i, hp)),
        out_shape=jax.ShapeDtypeStruct((seq, N_HEADS_B * DH_B), BF16),
        scratch_shapes=[pltpu.VMEM((band + qw, 2 * qw), F32)] * (blk // qw),
        compiler_params=pltpu.CompilerParams(dimension_semantics=("parallel", "arbitrary")),
        name="band_attention",
    )(proj, proj, proj, proj, proj, bias)


def _retention_kernel(qk_ref, v_ref, gate_ref, cos_ref, sin_ref, qdec_ref, kdec_ref, dmat_ref,
                      sdec_ref, o_ref, state_ref):
    @pl.when(pl.program_id(0) == 0)
    def _():
        state_ref[...] = jnp.zeros(state_ref.shape, F32)

    cos = cos_ref[...]
    sin = sin_ref[...]
    lane = lax.broadcasted_iota(jnp.int32, cos.shape, 1)
    first_half = (lane % DQK_C) < (DQK_C // 2)
    qk = qk_ref[...]
    parts = []
    for j in range(qk.shape[1] // LANES):
        t = qk[:, j * LANES:(j + 1) * LANES]
        partner = jnp.where(first_half, pltpu.roll(t, LANES - DQK_C // 2, 1), pltpu.roll(t, DQK_C // 2, 1))
        parts.append(t * cos + partner * sin)
    wq = N_HEADS_C * DQK_C
    q = jnp.concatenate(parts[:wq // LANES], axis=1)
    k = jnp.concatenate(parts[wq // LANES:], axis=1) * (DQK_C ** -0.5)
    qd = (q * qdec_ref[...]).astype(BF16)
    kd = (k * kdec_ref[...]).astype(BF16)
    qb = q.astype(BF16)
    kb = k.astype(BF16)
    vb = v_ref[...].astype(BF16)
    gate = gate_ref[...]
    outs = []
    for h in range(N_HEADS_C):
        qs = slice(h * DQK_C, (h + 1) * DQK_C)
        vs = slice(h * DV_C, (h + 1) * DV_C)
        scores = _dot_nt(qb[:, qs], kb[:, qs]) * dmat_ref[h]
        state = state_ref[h]
        r = _dot(scores.astype(BF16), vb[:, vs]) + _dot(qd[:, qs], state.astype(BF16))
        state_ref[h] = state * sdec_ref[h] + _dot_tn(kd[:, qs], vb[:, vs])
        r = r * lax.rsqrt(jnp.mean(r * r, axis=-1, keepdims=True) + EPS)
        g = gate[:, vs]
        outs.append(r * (g * jax.nn.sigmoid(g)))
    o_ref[...] = jnp.concatenate(outs, axis=1).astype(o_ref.dtype)


def _retention_tables(seq):
    t = BLK_C
    half = DQK_C // 2
    inv_freq = 1.0 / (ROPE_BASE ** (jnp.arange(0, DQK_C, 2, dtype=F32) / DQK_C))
    ang = jnp.arange(seq, dtype=F32)[:, None] * inv_freq[None, :]
    reps = LANES // half
    cos = jnp.tile(jnp.cos(ang), (1, reps))
    sign = jnp.where((jnp.arange(LANES) % DQK_C) < half, -1.0, 1.0).astype(F32)
    sin = jnp.tile(jnp.sin(ang), (1, reps)) * sign[None, :]
    log_g = jnp.log(1.0 - jnp.power(2.0, -5.0 - jnp.arange(N_HEADS_C, dtype=F32)))
    pos = jnp.arange(t, dtype=F32)
    diff = pos[:, None] - pos[None, :]
    same_or_past = (jnp.arange(t)[None, :] // CHUNK) <= (jnp.arange(t)[:, None] // CHUNK)
    dmat = jnp.where(same_or_past[None], jnp.exp(log_g[:, None, None] * jnp.abs(diff)[None]), 0.0)
    qdec = jnp.repeat(jnp.exp(log_g[None, :] * (pos[:, None] + 1.0)), DQK_C, axis=1)
    kdec = jnp.repeat(jnp.exp(log_g[None, :] * (t - 1.0 - pos[:, None])), DQK_C, axis=1)
    sdec = jnp.broadcast_to(jnp.exp(log_g * t)[:, None, None], (N_HEADS_C, 1, DV_C))
    return cos, sin, qdec, kdec, dmat, sdec


def _retention(proj):
    seq = proj.shape[0]
    t = BLK_C
    cos, sin, qdec, kdec, dmat, sdec = _retention_tables(seq)
    wv = N_HEADS_C * DV_C
    return pl.pallas_call(
        _retention_kernel,
        grid=(seq // t,),
        in_specs=[
            pl.BlockSpec((t, wv), lambda i: (i, 0)),
            pl.BlockSpec((t, wv), lambda i: (i, 1)),
            pl.BlockSpec((t, wv), lambda i: (i, 2)),
            pl.BlockSpec((t, LANES), lambda i: (i, 0)),
            pl.BlockSpec((t, LANES), lambda i: (i, 0)),
            pl.BlockSpec((t, N_HEADS_C * DQK_C), lambda i: (0, 0)),
            pl.BlockSpec((t, N_HEADS_C * DQK_C), lambda i: (0, 0)),
            pl.BlockSpec((N_HEADS_C, t, t), lambda i: (0, 0, 0)),
            pl.BlockSpec((N_HEADS_C, 1, DV_C), lambda i: (0, 0, 0)),
        ],
        out_specs=pl.BlockSpec((t, wv), lambda i: (i, 0)),
        out_shape=jax.ShapeDtypeStruct((seq, wv), BF16),
        scratch_shapes=[pltpu.VMEM((N_HEADS_C, DQK_C, DV_C), F32)],
        compiler_params=pltpu.CompilerParams(dimension_semantics=("arbitrary",)),
        name="retention",
    )(proj, proj, proj, cos, sin, qdec, kdec, dmat, sdec)


def _s5_kernel(*refs):
    ncb = S5_CH // LANES
    u_refs = refs[:ncb]
    (mt_ref, bt_ref, ctr_ref, cti_ref, are_ref, aim_ref, y_ref,
     ut_ref, yt_ref, ys_ref, vr_ref, vi_ref, spr_ref, spi_ref, carry_ref) = refs[ncb:]
    tc = S5_TC
    gp = S5_GROUP
    n = S5_STATE
    ng = S5_GROUPS

    @pl.when(pl.program_id(0) == 0)
    def _():
        carry_ref[...] = jnp.zeros(carry_ref.shape, F32)

    for s in range(S5_T):
        for k in range(ncb):
            ut_ref[s, k * LANES:(k + 1) * LANES, :] = u_refs[k][pl.ds(s, tc, stride=S5_T), :].T

    unroll = 4

    def intra(it, carry):
        for k in range(unroll):
            g = it * unroll + k
            r0 = pl.multiple_of(g * gp, gp)
            ug = ut_ref[:, pl.ds(r0, gp), :].reshape(S5_T * gp, tc).astype(BF16)
            yt_ref[:, pl.ds(r0, gp), :] = _dot(mt_ref[g], ug).reshape(S5_T, gp, tc)
            vt = _dot(bt_ref[g], ug)
            n0 = pl.multiple_of(g * n, n)
            vr_ref[pl.ds(n0, n), :] = vt[0:n]
            vi_ref[pl.ds(n0, n), :] = vt[n:2 * n]
        return carry

    lax.fori_loop(0, ng // unroll, intra, 0)

    sub = 8
    nv = tc // sub
    row = lax.broadcasted_iota(jnp.int32, (tc, LANES), 0)
    in_vreg = lax.rem(row, sub)

    def rows_of(v, r):
        return jnp.broadcast_to(v[r:r + 1], (tc, LANES))

    for j in range(ng * n // LANES):
        cols = slice(j * LANES, (j + 1) * LANES)
        pwr, pwi = are_ref[:, cols], aim_ref[:, cols]
        xr = vr_ref[cols, :].T
        xi = vi_ref[cols, :].T
        for d in (1, 2, 4):
            keep = in_vreg >= d
            sr = jnp.where(keep, pltpu.roll(xr, d, 0), 0.0)
            si = jnp.where(keep, pltpu.roll(xi, d, 0), 0.0)
            fr, fi = rows_of(pwr, d - 1), rows_of(pwi, d - 1)
            xr, xi = xr + (fr * sr - fi * si), xi + (fr * si + fi * sr)
        cr, ci = carry_ref[0, :, cols], carry_ref[1, :, cols]
        cr0, ci0 = cr, ci
        outr, outi = [], []
        for v in range(nv):
            yr = xr[v * sub:(v + 1) * sub] + (pwr * cr - pwi * ci)
            yi = xi[v * sub:(v + 1) * sub] + (pwr * ci + pwi * cr)
            outr.append(yr)
            outi.append(yi)
            cr = jnp.broadcast_to(yr[sub - 1:sub], (sub, LANES))
            ci = jnp.broadcast_to(yi[sub - 1:sub], (sub, LANES))
        carry_ref[0, :, cols] = cr
        carry_ref[1, :, cols] = ci
        sr = jnp.concatenate(outr, axis=0)
        si = jnp.concatenate(outi, axis=0)
        first = row == 0
        spr_ref[j] = jnp.where(first, rows_of(cr0, 0), pltpu.roll(sr, 1, 0))
        spi_ref[j] = jnp.where(first, rows_of(ci0, 0), pltpu.roll(si, 1, 0))

    def cross(it, carry):
        for k in range(unroll):
            jp = it * unroll + k
            r0 = pl.multiple_of(jp * 2 * gp, 2 * gp)
            yc = (_dot_nt(ctr_ref[jp], spr_ref[jp].astype(BF16))
                  + _dot_nt(cti_ref[jp], spi_ref[jp].astype(BF16)))
            yt_ref[:, pl.ds(r0, 2 * gp), :] += yc.reshape(S5_T, 2 * gp, tc)
        return carry

    lax.fori_loop(0, ng // 2 // unroll, cross, 0)

    for s in range(S5_T):
        for k in range(ncb):
            ys_ref[k, pl.ds(s, tc, stride=S5_T), :] = yt_ref[s, k * LANES:(k + 1) * LANES, :].T
    for k in range(ncb):
        y_ref[:, k * LANES:(k + 1) * LANES] = ys_ref[k]


def _s5_matrices(lam_re, lam_im, log_step, b_re, b_im, c_re, c_im, d_skip):
    hi = lax.Precision.HIGHEST
    t, gp, n, ng = S5_T, S5_GROUP, S5_STATE, S5_GROUPS
    lam = lax.complex(lam_re.astype(F32), lam_im.astype(F32))
    step = jnp.exp(log_step.astype(F32))[:, None]
    ls = lam * step
    a_bar = jnp.exp(ls)
    b_bar = ((a_bar - 1.0) / lam)[..., None] * lax.complex(b_re.astype(F32), b_im.astype(F32))
    cm = lax.complex(c_re.astype(F32), c_im.astype(F32))

    def apow(k):
        kk = k.astype(F32).astype(jnp.complex64)
        return jnp.exp(ls.reshape((ng,) + (1,) * k.ndim + (n,)) * kk[None, ..., None])

    tt = jnp.arange(t)
    kmat = jnp.einsum('gpn,gln,gnq->glpq', cm, apow(tt), b_bar, precision=hi).real
    krev = jnp.transpose(kmat[:, ::-1], (0, 2, 1, 3)).reshape(ng, gp, t * gp)
    kpad = jnp.pad(krev, ((0, 0), (0, 0), (0, t * gp)))
    mt = jnp.concatenate([kpad[:, :, (t - 1 - to) * gp:(2 * t - 1 - to) * gp] for to in range(t)], axis=1)
    dvec = jnp.tile(d_skip.astype(F32).reshape(ng, 1, gp), (1, t, 1)).reshape(ng, t * gp)
    mt = mt + jnp.eye(t * gp, dtype=F32)[None] * dvec[:, :, None]
    z = jnp.swapaxes(apow(t - 1 - tt), 1, 2)[:, :, :, None] * b_bar[:, :, None, :]
    z = z.reshape(ng, n, t * gp)
    bt = jnp.concatenate([z.real, z.imag], axis=1)
    w = cm[:, None, :, :] * apow(tt + 1)[:, :, None, :]

    def pair_readout(x):
        x = x.reshape(ng // 2, 2, t, gp, n)
        first = jnp.pad(x[:, 0], ((0, 0), (0, 0), (0, 0), (0, n)))
        second = jnp.pad(x[:, 1], ((0, 0), (0, 0), (0, 0), (n, 0)))
        return jnp.stack([first, second], axis=2).reshape(ng // 2, t * 2 * gp, 2 * n).astype(BF16)

    ctr, cti = pair_readout(w.real), pair_readout(-w.imag)
    a_chunk = jnp.transpose(apow(t * (jnp.arange(8) + 1)), (1, 0, 2)).reshape(8, ng * n)
    return mt.astype(BF16), bt.astype(BF16), ctr, cti, a_chunk.real, a_chunk.imag


def _s5(proj, mats):
    seq, width = proj.shape
    t, tc, gp, n, ng = S5_T, S5_TC, S5_GROUP, S5_STATE, S5_GROUPS
    rows = t * tc
    ncb = S5_CH // LANES
    cb0 = (width - S5_CH) // LANES
    u_specs = [pl.BlockSpec((rows, LANES), (lambda i, k=k: (i, cb0 + k))) for k in range(ncb)]
    nsb = ng * n // LANES
    return pl.pallas_call(
        _s5_kernel,
        grid=(seq // rows,),
        in_specs=u_specs + [_const_spec(m.shape) for m in mats],
        out_specs=pl.BlockSpec((rows, S5_CH), lambda i: (i, 0)),
        out_shape=jax.ShapeDtypeStruct((seq, S5_CH), F32),
        scratch_shapes=[
            pltpu.VMEM((t, S5_CH, tc), F32),
            pltpu.VMEM((t, S5_CH, tc), F32),
            pltpu.VMEM((ncb, rows, LANES), F32),
            pltpu.VMEM((ng * n, tc), F32),
            pltpu.VMEM((ng * n, tc), F32),
            pltpu.VMEM((nsb, tc, LANES), F32),
            pltpu.VMEM((nsb, tc, LANES), F32),
            pltpu.VMEM((2, 8, ng * n), F32),
        ],
        compiler_params=pltpu.CompilerParams(dimension_semantics=("arbitrary",)),
        name="s5_scan",
    )(*([proj] * ncb), *mats)


def _mix_ffn_kernel(*refs, glu, final):
    (x_ref, a_ref, b_ref, wo_ref, g1_ref), refs = refs[:5], refs[5:]
    if glu:
        gw_ref, refs = refs[0], refs[1:]
    (g_ref, sc_ref, sh_ref, gate_ref, win_ref, cw_ref, cb_ref, wout_ref), refs = refs[:8], refs[8:]
    if final:
        fg_ref, o_ref, h_ref, act_ref, gbuf_ref, carry_ref = refs
    else:
        ng_ref, nsc_ref, nsh_ref, o_ref, hn_ref, h_ref, act_ref, gbuf_ref, carry_ref = refs
    tm = x_ref.shape[0]
    halo = gbuf_ref.shape[0] - tm

    @pl.when(pl.program_id(0) == 0)
    def _():
        carry_ref[...] = jnp.zeros(carry_ref.shape, F32)

    if glu:
        y = jax.nn.gelu(b_ref[...]).astype(BF16)
        gg = _dot(y, gw_ref[...])
        half = gg.shape[1] // 2
        b = (gg[:, :half] * jax.nn.sigmoid(gg[:, half:])).astype(BF16)
    else:
        b = b_ref[...]
    cat = jnp.concatenate([a_ref[...], b], axis=1)
    x = x_ref[...] + g1_ref[...] * _dot(cat, wo_ref[...])
    h_ref[...] = _mod_rmsnorm(x, g_ref[...], sc_ref[...], sh_ref[...]).astype(BF16)
    for f in range(D_FF // TF_FFN):
        cs = slice(f * TF_FFN, (f + 1) * TF_FFN)
        gs = slice(D_FF + f * TF_FFN, D_FF + (f + 1) * TF_FFN)
        h = h_ref[...]
        val = _dot(h, win_ref[:, cs])
        gate = _dot(h, win_ref[:, gs])
        gbuf_ref[0:halo, :] = carry_ref[:, cs]
        gbuf_ref[halo:halo + tm, :] = gate
        carry_ref[:, cs] = gate[tm - halo:tm, :]
        conv = (gate * cw_ref[2:3, cs] + gbuf_ref[halo - 1:halo - 1 + tm, :] * cw_ref[1:2, cs]
                + gbuf_ref[halo - 2:halo - 2 + tm, :] * cw_ref[0:1, cs] + cb_ref[:, cs])
        act_ref[:, cs] = (jax.nn.gelu(conv) * val).astype(BF16)
    xn = x + gate_ref[...] * _dot(act_ref[...], wout_ref[...])
    if final:
        xn = xn * lax.rsqrt(jnp.mean(xn * xn, axis=-1, keepdims=True) + EPS) * fg_ref[...]
    else:
        hn_ref[...] = _mod_rmsnorm(xn, ng_ref[...], nsc_ref[...], nsh_ref[...]).astype(BF16)
    o_ref[...] = xn


def _layer_spec(shape, layer):
    idx = (layer,) + (0,) * (len(shape) - 1)
    return pl.BlockSpec((None,) + tuple(shape[1:]), lambda *_: idx, pipeline_mode=pl.Buffered(1))


def _mix_ffn(x, a, b, wo, gate1, glu_w, g, scale, shift, gate2, w_in, conv_w, conv_b, w_out, tail, layer):
    seq, d = x.shape
    final = len(tail) == 1
    tm = TM_FFN
    halo = 8
    row = pl.BlockSpec((1, d), lambda i: (0, 0))
    rows = lambda w: pl.BlockSpec((tm, w), lambda i: (i, 0))
    conv_b = conv_b.reshape(conv_b.shape[0], 1, D_FF)
    in_specs = [rows(d), rows(a.shape[1]), rows(b.shape[1]), _const_spec(wo.shape), row]
    args = [x, a, b, wo, gate1]
    if glu_w is not None:
        in_specs.append(_const_spec(glu_w.shape))
        args.append(glu_w)
    in_specs += [
        row, row, row, row,
        _layer_spec(w_in.shape, layer),
        _layer_spec(conv_w.shape, layer),
        _layer_spec(conv_b.shape, layer),
        _layer_spec(w_out.shape, layer),
    ] + [row] * len(tail)
    args += [g.reshape(1, d), scale, shift, gate2, w_in, conv_w, conv_b, w_out]
    args += [t.reshape(1, d) for t in tail]
    out_specs = [rows(d)] if final else [rows(d), rows(d)]
    out_shape = [jax.ShapeDtypeStruct((seq, d), F32)] + ([] if final else [jax.ShapeDtypeStruct((seq, d), BF16)])
    return pl.pallas_call(
        functools.partial(_mix_ffn_kernel, glu=glu_w is not None, final=final),
        grid=(seq // tm,),
        in_specs=in_specs,
        out_specs=out_specs,
        out_shape=out_shape,
        scratch_shapes=[
            pltpu.VMEM((tm, d), BF16),
            pltpu.VMEM((tm, D_FF), BF16),
            pltpu.VMEM((tm + halo, TF_FFN), F32),
            pltpu.VMEM((halo, D_FF), F32),
        ],
        compiler_params=pltpu.CompilerParams(dimension_semantics=("arbitrary",)),
        name="mix_ffn",
    )(*args)


def kernel(x, c, t5_table, mod_w, mod_b, norm1_g, norm2_g, ffn_w_in, ffn_conv_w, ffn_conv_b, ffn_w_out,
           ev_w_in, ev_w_out, diff_lambda, diff_subln_g, band_rel_bias,
           od_w_in, od_w_out, s5_lam_re, s5_lam_im, s5_log_step, s5_b_re, s5_b_im, s5_c_re, s5_c_im,
           s5_d, s5_glu_w, final_g):
    assert x.shape[0] == 1 and x.shape[2] == D_MODEL
    seq = x.shape[1]
    assert seq % TM_PROJ == 0 and seq % (S5_T * S5_TC) == 0
    d = D_MODEL
    xs = x[0]
    mod = _modulation(c, mod_w, mod_b)
    ffn_w_in_b = ffn_w_in.astype(BF16)
    ffn_w_out_b = ffn_w_out.astype(BF16)
    mods = [[mod[i, :, k * d:(k + 1) * d] for k in range(6)] for i in range(DEPTH)]
    h = None
    for i in range(DEPTH):
        sh1, sc1, g1, sh2, sc2, g2 = mods[i]
        w_in = (ev_w_in if i % 2 == 0 else od_w_in)[i // 2].astype(BF16)
        proj_dtype = BF16 if i % 2 == 0 else F32
        if h is None:
            proj = _normproj(xs, norm1_g[i], sc1, sh1, w_in, proj_dtype)
        else:
            proj = _proj(h, w_in, proj_dtype)
        if i % 2 == 0:
            e = i // 2
            lam_init = 0.8 - 0.6 * math.exp(-0.3 * i)
            lp = diff_lambda[e].astype(F32)
            lam = jnp.exp(jnp.sum(lp[0] * lp[1])) - jnp.exp(jnp.sum(lp[2] * lp[3])) + lam_init
            mix_a = _diff_attention(proj, t5_table, lam, diff_subln_g[e], lam_init)
            mix_b = _band_attention(proj, band_rel_bias[e])
            wo, glu_w = ev_w_out[e].astype(BF16), None
        else:
            o = i // 2
            mix_a = _retention(proj)
            mats = _s5_matrices(s5_lam_re[o], s5_lam_im[o], s5_log_step[o], s5_b_re[o], s5_b_im[o],
                                s5_c_re[o], s5_c_im[o], s5_d[o])
            mix_b = _s5(proj, mats)
            wo, glu_w = od_w_out[o].astype(BF16), s5_glu_w[o].astype(BF16)
        if i == DEPTH - 1:
            tail = (final_g,)
        else:
            nsh1, nsc1 = mods[i + 1][0], mods[i + 1][1]
            tail = (norm1_g[i + 1], nsc1, nsh1)
        out = _mix_ffn(xs, mix_a, mix_b, wo, g1, glu_w, norm2_g[i], sc2, sh2, g2,
                       ffn_w_in_b, ffn_conv_w, ffn_conv_b, ffn_w_out_b, tail, layer=i)
        if i == DEPTH - 1:
            xs = out[0]
        else:
            xs, h = out
    return xs[None]
```

```python
import functools
import math

import jax
import jax.numpy as jnp
from jax import lax
from jax.experimental import pallas as pl
from jax.experimental.pallas import tpu as pltpu

F32 = jnp.float32
BF16 = jnp.bfloat16

D_MODEL = 1024
DEPTH = 2
CHUNK = 64
GROUP_WIDTH = D_MODEL // 2
DK_A = 64
DV_A = 2 * DK_A
N_HEADS_A = GROUP_WIDTH // DV_A
DH_B = 64
N_HEADS_B = GROUP_WIDTH // DH_B
LEFT_CHUNKS = 8
REL_CLIP = 2 * CHUNK
NUM_BUCKETS = 32
MAX_DISTANCE = 128
DV_C = 128
DQK_C = DV_C // 2
N_HEADS_C = GROUP_WIDTH // DV_C
ROPE_BASE = 10000.0
S5_CH = GROUP_WIDTH
S5_GROUP = 16
S5_GROUPS = S5_CH // S5_GROUP
S5_STATE = 64
D_FF = ((8 * D_MODEL // 3 + 255) // 256) * 256
CONV_W = 3
EVEN_IN = 3 * N_HEADS_A * DV_A + 3 * N_HEADS_B * DH_B
ODD_IN = 2 * N_HEADS_C * DQK_C + 2 * N_HEADS_C * DV_C + S5_CH
EPS = 1e-6
NEG_INF = -1e30
LOG2E = math.log2(math.e)

LANES = 128
MXU_DIM = 256

TM_PROJ = 1024
TM_FFN = 512
TF_FFN = MXU_DIM
BLK_A = 512
NPART_A = 2
ONES_A = 16
BLK_B = 1024
BAND_B = LEFT_CHUNKS * CHUNK
QW_B = 4 * CHUNK
BLK_C = 512
S5_T = 16
S5_TC = LANES

assert BLK_B % BAND_B == 0 and BLK_B % QW_B == 0 and BAND_B % QW_B == 0
assert BLK_A >= MAX_DISTANCE, "far key blocks must sit in the saturated T5 bucket"
assert DV_A == LANES, "diff-attention statistics are kept lane-replicated beside the accumulator"


def _dot(a, b):
    return jnp.dot(a, b, preferred_element_type=F32)


def _dot_nt(a, b):
    return lax.dot_general(a, b, (((1,), (1,)), ((), ())), preferred_element_type=F32)


def _dot_tn(a, b):
    return lax.dot_general(a, b, (((0,), (0,)), ((), ())), preferred_element_type=F32)


def _const_spec(shape):
    zeros = (0,) * len(shape)
    return pl.BlockSpec(shape, lambda *_: zeros, pipeline_mode=pl.Buffered(1))


def _mod_rmsnorm(x, g, scale, shift):
    y = x * lax.rsqrt(jnp.mean(x * x, axis=-1, keepdims=True) + EPS)
    y = y * g
    return y * (1.0 + scale) + shift


def _mod_kernel(c_ref, w_ref, b_ref, o_ref):
    c = c_ref[...]
    cond = c * jax.nn.sigmoid(c)
    o_ref[0] = jnp.sum(cond * w_ref[0], axis=0, keepdims=True) + b_ref[0]


def _modulation(c, mod_w, mod_b):
    depth, d, n = mod_w.shape
    tn = 1536
    return pl.pallas_call(
        _mod_kernel,
        grid=(depth, n // tn),
        in_specs=[
            pl.BlockSpec((d, 1), lambda i, j: (0, 0)),
            pl.BlockSpec((1, d, tn), lambda i, j: (i, 0, j)),
            pl.BlockSpec((1, 1, tn), lambda i, j: (i, 0, j)),
        ],
        out_specs=pl.BlockSpec((1, 1, tn), lambda i, j: (i, 0, j)),
        out_shape=jax.ShapeDtypeStruct((depth, 1, n), F32),
        name="modulation",
    )(c.reshape(d, 1), mod_w, mod_b.reshape(depth, 1, n))


TN_PROJ = 1024


def _normproj_kernel(x_ref, g_ref, sc_ref, sh_ref, w_ref, o_ref):
    tm, n = o_ref.shape
    half = tm // 2
    for r in range(2):
        rows = slice(r * half, (r + 1) * half)
        h = _mod_rmsnorm(x_ref[rows, :], g_ref[...], sc_ref[...], sh_ref[...]).astype(BF16)
        for j in range(n // TN_PROJ):
            cols = slice(j * TN_PROJ, (j + 1) * TN_PROJ)
            o_ref[rows, cols] = _dot(h, w_ref[:, cols]).astype(o_ref.dtype)


def _normproj(x, g, scale, shift, w, out_dtype):
    seq, d = x.shape
    n = w.shape[1]
    tm = TM_PROJ
    row = pl.BlockSpec((1, d), lambda i: (0, 0))
    return pl.pallas_call(
        _normproj_kernel,
        grid=(seq // tm,),
        in_specs=[pl.BlockSpec((tm, d), lambda i: (i, 0)), row, row, row, _const_spec(w.shape)],
        out_specs=pl.BlockSpec((tm, n), lambda i: (i, 0)),
        out_shape=jax.ShapeDtypeStruct((seq, n), out_dtype),
        compiler_params=pltpu.CompilerParams(dimension_semantics=("parallel",)),
        name="normproj",
    )(x, g.reshape(1, d), scale, shift, w)


def _proj_kernel(h_ref, w_ref, o_ref):
    for j in range(o_ref.shape[1] // TN_PROJ):
        cols = slice(j * TN_PROJ, (j + 1) * TN_PROJ)
        o_ref[:, cols] = _dot(h_ref[...], w_ref[:, cols]).astype(o_ref.dtype)


def _proj(h, w, out_dtype):
    seq, d = h.shape
    n = w.shape[1]
    tm = TM_PROJ
    return pl.pallas_call(
        _proj_kernel,
        grid=(seq // tm,),
        in_specs=[pl.BlockSpec((tm, d), lambda i: (i, 0)), _const_spec(w.shape)],
        out_specs=pl.BlockSpec((tm, n), lambda i: (i, 0)),
        out_shape=jax.ShapeDtypeStruct((seq, n), out_dtype),
        compiler_params=pltpu.CompilerParams(dimension_semantics=("parallel",)),
        name="proj",
    )(h, w)


def _diffattn_kernel(q_ref, k_ref, v_ref, bias_ref, lam_ref, g_ref, o_ref,
                     qs_ref, vt_ref, m_ref, acc_ref, *s_refs, out_scale):
    blk = BLK_A
    nq = 2 * blk
    sub = 8
    dv = DV_A
    npart = len(s_refs) // 4
    wq = nq // npart
    sa_ref, sb_ref = s_refs[:2 * npart], s_refs[2 * npart:]
    i = pl.program_id(1)

    @pl.when(i == 0)
    def _():
        def tr(b, carry):
            r0 = pl.multiple_of(b * blk, blk)
            vt_ref[0:dv, pl.ds(r0, blk)] = v_ref[pl.ds(r0, blk), :].astype(F32).T.astype(BF16)
            vt_ref[dv:dv + ONES_A, pl.ds(r0, blk)] = jnp.ones((ONES_A, blk), BF16)
            return carry
        lax.fori_loop(0, v_ref.shape[0] // blk, tr, 0)

    q = q_ref[...].astype(F32) * (DK_A ** -0.5 * LOG2E)
    lane = lax.broadcasted_iota(jnp.int32, q.shape, 1)
    qs_ref[:, 0:blk] = jnp.where(lane < DK_A, q, 0.0).T.astype(BF16)
    qs_ref[:, blk:nq] = jnp.where(lane >= DK_A, q, 0.0).T.astype(BF16)
    m_ref[...] = jnp.full(m_ref.shape, NEG_INF, F32)
    acc_ref[...] = jnp.zeros(acc_ref.shape, F32)

    def scores(b, s_ref):
        k = k_ref[pl.ds(pl.multiple_of(b * blk, blk), blk), :]
        for part in range(npart):
            s = _dot(k, qs_ref[:, part * wq:(part + 1) * wq])
            s_ref[part][:, 0:wq] = s
            s_ref[npart + part][...] = jnp.max(s.reshape(blk // sub, sub, wq), axis=0)

    def softmax_pv(b, s_ref, bias):
        vt = vt_ref[:, pl.ds(pl.multiple_of(b * blk, blk), blk)]
        for part in range(npart):
            cols = slice(part * wq, (part + 1) * wq)
            s = s_ref[part][:, 0:wq]
            if bias is not None:
                q0 = (part * wq) % blk
                s = s + bias[:, q0:q0 + wq]
            s = s.reshape(blk // sub, sub, wq)
            m_prev = m_ref[:, cols]
            smax = jnp.max(s, axis=0) if bias is not None else s_ref[npart + part][...]
            m_cur = jnp.max(smax, axis=0, keepdims=True)
            m_new = jnp.maximum(m_prev, m_cur)
            alpha = jnp.exp2(m_prev - m_new)
            p = jnp.exp2(s - m_new[None])
            pv = _dot(vt, p.reshape(blk, wq).astype(BF16))
            acc_ref[:, cols] = acc_ref[:, cols] * alpha[0:1] + pv
            m_ref[:, cols] = m_new

    nfar = jnp.maximum(i - 1, 0)
    odd = lax.rem(nfar, 2)

    @pl.when(i == 0)
    def _():
        scores(0, sb_ref)

    @pl.when(i > 0)
    def _():
        @pl.when(odd == 1)
        def _():
            scores(0, sb_ref)
            scores(1, sa_ref)
            softmax_pv(0, sb_ref, None)

        @pl.when(odd == 0)
        def _():
            scores(0, sa_ref)

        def pair(b):
            scores(b + 1, sb_ref)
            softmax_pv(b, sa_ref, None)
            scores(b + 2, sa_ref)
            softmax_pv(b + 1, sb_ref, None)

        def quad_body(t, carry):
            pair(odd + 4 * t)
            pair(odd + 4 * t + 2)
            return carry

        npairs = nfar // 2
        lax.fori_loop(0, npairs // 2, quad_body, 0)

        @pl.when(lax.rem(npairs, 2) == 1)
        def _():
            pair(odd + 2 * (npairs - 1))
        scores(i, sb_ref)
        softmax_pv(i - 1, sa_ref, bias_ref[0, 0])

    softmax_pv(i, sb_ref, bias_ref[0, 1])

    ot = acc_ref[0:dv, :] / acc_ref[dv:dv + 1, :]
    o = ot[:, 0:blk].T - lam_ref[...] * ot[:, blk:nq].T
    o = o * lax.rsqrt(jnp.mean(o * o, axis=-1, keepdims=True) + EPS) * g_ref[...]
    o_ref[...] = (o * out_scale).astype(o_ref.dtype)


_TOEPLITZ_ROWS = 256
_TOEPLITZ_N = 2048


def _toeplitz_kernel(v_ref, o_ref, *, keep):
    rows, cols = o_ref.shape[1:]
    x = jnp.broadcast_to(v_ref[0, 0], (rows, v_ref.shape[-1]))
    tile = pltpu.roll(x, 0, 1, stride=1, stride_axis=0)[:, :cols]
    r = lax.broadcasted_iota(jnp.int32, (rows, cols), 0) + pl.program_id(1) * rows
    c = lax.broadcasted_iota(jnp.int32, (rows, cols), 1)
    o_ref[0] = jnp.where(keep(r, c), tile, NEG_INF)


def _toeplitz_tiles(fn, keep, heads, rows, cols):
    n, rb = _TOEPLITZ_N, _TOEPLITZ_ROWS
    assert rows % rb == 0 and rows <= n // 2 and cols <= n // 2
    idx = jnp.arange(n, dtype=jnp.int32)
    vec = fn(jnp.where(idx < n // 2, idx, idx - n)).astype(F32)
    vecs = jnp.stack([jnp.roll(vec, k * rb, axis=1) for k in range(rows // rb)], axis=1)
    return pl.pallas_call(
        functools.partial(_toeplitz_kernel, keep=keep),
        grid=(heads, rows // rb),
        in_specs=[pl.BlockSpec((1, 1, 1, n), lambda h, k: (h, k, 0, 0))],
        out_specs=pl.BlockSpec((1, rb, cols), lambda h, k: (h, k, 0)),
        out_shape=jax.ShapeDtypeStruct((heads, rows, cols), F32),
        name="toeplitz_tiles",
    )(vecs.reshape(heads, rows // rb, 1, n))


def _t5_bucket(rel):
    nb = NUM_BUCKETS // 2
    max_exact = nb // 2
    bucket = jnp.where(rel > 0, nb, 0)
    n = jnp.abs(rel)
    nf = jnp.maximum(n, 1).astype(F32)
    large = max_exact + (jnp.log(nf / max_exact) / math.log(MAX_DISTANCE / max_exact)
                         * (nb - max_exact)).astype(jnp.int32)
    large = jnp.minimum(large, nb - 1)
    return bucket + jnp.where(n < max_exact, n, large)


def _diff_bias_tiles(t5_table):
    blk = BLK_A
    table = t5_table.astype(F32)
    far = table[_t5_bucket(jnp.full((), -(blk + 1), jnp.int32))]
    def visible(r, c):
        return jnp.floor_divide(r - blk, CHUNK) <= jnp.floor_divide(c, CHUNK)

    tiles = _toeplitz_tiles(lambda x: ((table[_t5_bucket(-x - blk)] - far) * LOG2E).T, visible,
                            N_HEADS_A, 2 * blk, blk)
    return tiles.reshape(N_HEADS_A, 2, blk, blk)


def _diff_attention(proj, t5_table, lam, subln_g, lam_init):
    seq = proj.shape[0]
    blk = BLK_A
    bias = _diff_bias_tiles(t5_table)
    ha = N_HEADS_A
    kern = functools.partial(_diffattn_kernel, out_scale=1.0 - lam_init)
    return pl.pallas_call(
        kern,
        grid=(ha, seq // blk),
        in_specs=[
            pl.BlockSpec((blk, DV_A), lambda h, i: (i, h)),
            pl.BlockSpec((seq, DV_A), lambda h, i: (0, ha + h)),
            pl.BlockSpec((seq, DV_A), lambda h, i: (0, 2 * ha + h)),
            pl.BlockSpec((1, 2, blk, blk), lambda h, i: (h, 0, 0, 0)),
            pl.BlockSpec((1, DV_A), lambda h, i: (0, 0)),
            pl.BlockSpec((1, DV_A), lambda h, i: (0, 0)),
        ],
        out_specs=pl.BlockSpec((blk, DV_A), lambda h, i: (i, h)),
        out_shape=jax.ShapeDtypeStruct((seq, ha * DV_A), BF16),
        scratch_shapes=[
            pltpu.VMEM((DV_A, 2 * blk), BF16),
            pltpu.VMEM((DV_A + ONES_A, seq), BF16),
            pltpu.VMEM((8, 2 * blk), F32),
            pltpu.VMEM((DV_A + ONES_A, 2 * blk), F32),
        ] + 2 * ([pltpu.VMEM((blk, 2 * blk // NPART_A + LANES), F32)] * NPART_A
                 + [pltpu.VMEM((8, 2 * blk // NPART_A), F32)] * NPART_A),
        compiler_params=pltpu.CompilerParams(dimension_semantics=("parallel", "arbitrary")),
        name="diff_attention",
    )(proj, proj, proj, bias, jnp.full((1, DV_A), lam, F32), subln_g.reshape(1, DV_A).astype(F32))


def _band_kernel(q_ref, kp_ref, kc_ref, vp_ref, vc_ref, bias_ref, o_ref, *s_refs):
    qw, band = QW_B, BAND_B
    nk = band + qw
    sub = 8
    i = pl.program_id(1)
    q = q_ref[...].astype(F32) * (DH_B ** -0.5 * LOG2E)
    lane = lax.broadcasted_iota(jnp.int32, q.shape, 1)
    qh = (jnp.where(lane < DH_B, q, 0.0).T.astype(BF16), jnp.where(lane >= DH_B, q, 0.0).T.astype(BF16))
    k_all = jnp.concatenate([kp_ref[...], kc_ref[...]], axis=0)
    vt_all = jnp.concatenate([vp_ref[...], vc_ref[...]], axis=0).astype(F32).T.astype(BF16)
    bias = jnp.concatenate([bias_ref[0], bias_ref[1]], axis=1)
    no_prev = jnp.where(i == 0, NEG_INF, 0.0).astype(F32)
    krow = lax.broadcasted_iota(jnp.int32, (nk, 2 * qw), 0)
    ngroups = len(s_refs)
    for g in range(ngroups):
        k0 = g * qw
        qs = jnp.concatenate([qh[0][:, k0:k0 + qw], qh[1][:, k0:k0 + qw]], axis=1)
        s_refs[g][:, 0:2 * qw] = _dot(k_all[k0:k0 + nk], qs)
    for g in range(ngroups):
        k0 = g * qw
        s = s_refs[g][:, 0:2 * qw] + bias
        if k0 < band:
            s = s + jnp.where(krow < band - k0, no_prev, 0.0)
        s = s.reshape(nk // sub, sub, 2 * qw)
        m = jnp.max(jnp.max(s, axis=0), axis=0, keepdims=True)
        p = jnp.exp2(s - m[None])
        l = jnp.sum(jnp.sum(p, axis=0), axis=0, keepdims=True)
        ot = _dot(vt_all[:, k0:k0 + nk], p.reshape(nk, 2 * qw).astype(BF16)) / l
        o = jnp.concatenate([ot[0:DH_B, 0:qw], ot[DH_B:2 * DH_B, qw:2 * qw]], axis=0)
        o_ref[k0:k0 + qw, :] = o.T.astype(o_ref.dtype)


def _band_bias_tiles(rel_bias):
    band = BAND_B

    def valid(r, c):
        qchunk = jnp.floor_divide(c, CHUNK)
        kchunk = jnp.floor_divide(r - band, CHUNK)
        return (kchunk <= qchunk) & (kchunk >= qchunk - LEFT_CHUNKS)

    return _toeplitz_tiles(
        lambda x: rel_bias.astype(F32)[:, jnp.clip(-x - band, -REL_CLIP, REL_CLIP) + REL_CLIP] * LOG2E, valid,
        N_HEADS_B, band + QW_B, QW_B)


def _band_attention(proj, rel_bias):
    seq = proj.shape[0]
    blk, band, qw = BLK_B, BAND_B, QW_B
    bias = _band_bias_tiles(rel_bias)
    npair = N_HEADS_B // 2
    qc0 = 3 * N_HEADS_A
    per = blk // band
    prev = lambda c0: (lambda hp, i: (jnp.maximum(i * per - 1, 0), c0 + hp))
    cur = lambda c0: (lambda hp, i: (i, c0 + hp))
    return pl.pallas_call(
        _band_kernel,
        grid=(npair, seq // blk),
        in_specs=[
            pl.BlockSpec((blk, LANES), cur(qc0)),
            pl.BlockSpec((band, LANES), prev(qc0 + npair)),
            pl.BlockSpec((blk, LANES), cur(qc0 + npair)),
            pl.BlockSpec((band, LANES), prev(qc0 + 2 * npair)),
            pl.BlockSpec((blk, LANES), cur(qc0 + 2 * npair)),
            pl.BlockSpec((2, band + qw, qw), lambda hp, i: (hp, 0, 0)),
        ],
        out_specs=pl.BlockSpec((blk, LANES), lambda hp, i: (i, hp)),
        out_shape=jax.ShapeDtypeStruct((seq, N_HEADS_B * DH_B), BF16),
        scratch_shapes=[pltpu.VMEM((band + qw, 2 * qw + LANES), F32)] * (blk // qw),
        compiler_params=pltpu.CompilerParams(dimension_semantics=("parallel", "arbitrary")),
        name="band_attention",
    )(proj, proj, proj, proj, proj, bias)


def _retention_kernel(qk_ref, v_ref, gate_ref, cos_ref, sin_ref, qdec_ref, kdec_ref, dmat_ref,
                      sdec_ref, o_ref, state_ref):
    @pl.when(pl.program_id(0) == 0)
    def _():
        state_ref[...] = jnp.zeros(state_ref.shape, F32)

    cos = cos_ref[...]
    sin = sin_ref[...]
    lane = lax.broadcasted_iota(jnp.int32, cos.shape, 1)
    first_half = (lane % DQK_C) < (DQK_C // 2)
    qk = qk_ref[...]
    parts = []
    for j in range(qk.shape[1] // LANES):
        t = qk[:, j * LANES:(j + 1) * LANES]
        partner = jnp.where(first_half, pltpu.roll(t, LANES - DQK_C // 2, 1), pltpu.roll(t, DQK_C // 2, 1))
        parts.append(t * cos + partner * sin)
    wq = N_HEADS_C * DQK_C
    q = jnp.concatenate(parts[:wq // LANES], axis=1)
    k = jnp.concatenate(parts[wq // LANES:], axis=1) * (DQK_C ** -0.5)
    qd = (q * qdec_ref[...]).astype(BF16)
    kd = (k * kdec_ref[...]).astype(BF16)
    qb = q.astype(BF16)
    kb = k.astype(BF16)
    vb = v_ref[...].astype(BF16)
    gate = gate_ref[...]
    outs = []
    for h in range(N_HEADS_C):
        qs = slice(h * DQK_C, (h + 1) * DQK_C)
        vs = slice(h * DV_C, (h + 1) * DV_C)
        scores = _dot_nt(qb[:, qs], kb[:, qs]) * dmat_ref[h]
        state = state_ref[h]
        r = _dot(scores.astype(BF16), vb[:, vs]) + _dot(qd[:, qs], state.astype(BF16))
        state_ref[h] = state * sdec_ref[h] + _dot_tn(kd[:, qs], vb[:, vs])
        r = r * lax.rsqrt(jnp.mean(r * r, axis=-1, keepdims=True) + EPS)
        g = gate[:, vs]
        outs.append(r * (g * jax.nn.sigmoid(g)))
    o_ref[...] = jnp.concatenate(outs, axis=1).astype(o_ref.dtype)


def _retention_tables(seq):
    t = BLK_C
    half = DQK_C // 2
    inv_freq = 1.0 / (ROPE_BASE ** (jnp.arange(0, DQK_C, 2, dtype=F32) / DQK_C))
    ang = jnp.arange(seq, dtype=F32)[:, None] * inv_freq[None, :]
    reps = LANES // half
    cos = jnp.tile(jnp.cos(ang), (1, reps))
    sign = jnp.where((jnp.arange(LANES) % DQK_C) < half, -1.0, 1.0).astype(F32)
    sin = jnp.tile(jnp.sin(ang), (1, reps)) * sign[None, :]
    log_g = jnp.log(1.0 - jnp.power(2.0, -5.0 - jnp.arange(N_HEADS_C, dtype=F32)))
    pos = jnp.arange(t, dtype=F32)
    diff = pos[:, None] - pos[None, :]
    same_or_past = (jnp.arange(t)[None, :] // CHUNK) <= (jnp.arange(t)[:, None] // CHUNK)
    dmat = jnp.where(same_or_past[None], jnp.exp(log_g[:, None, None] * jnp.abs(diff)[None]), 0.0)
    qdec = jnp.repeat(jnp.exp(log_g[None, :] * (pos[:, None] + 1.0)), DQK_C, axis=1)
    kdec = jnp.repeat(jnp.exp(log_g[None, :] * (t - 1.0 - pos[:, None])), DQK_C, axis=1)
    sdec = jnp.broadcast_to(jnp.exp(log_g * t)[:, None, None], (N_HEADS_C, 1, DV_C))
    return cos, sin, qdec, kdec, dmat, sdec


def _retention(proj):
    seq = proj.shape[0]
    t = BLK_C
    cos, sin, qdec, kdec, dmat, sdec = _retention_tables(seq)
    wv = N_HEADS_C * DV_C
    return pl.pallas_call(
        _retention_kernel,
        grid=(seq // t,),
        in_specs=[
            pl.BlockSpec((t, wv), lambda i: (i, 0)),
            pl.BlockSpec((t, wv), lambda i: (i, 1)),
            pl.BlockSpec((t, wv), lambda i: (i, 2)),
            pl.BlockSpec((t, LANES), lambda i: (i, 0)),
            pl.BlockSpec((t, LANES), lambda i: (i, 0)),
            pl.BlockSpec((t, N_HEADS_C * DQK_C), lambda i: (0, 0)),
            pl.BlockSpec((t, N_HEADS_C * DQK_C), lambda i: (0, 0)),
            pl.BlockSpec((N_HEADS_C, t, t), lambda i: (0, 0, 0)),
            pl.BlockSpec((N_HEADS_C, 1, DV_C), lambda i: (0, 0, 0)),
        ],
        out_specs=pl.BlockSpec((t, wv), lambda i: (i, 0)),
        out_shape=jax.ShapeDtypeStruct((seq, wv), BF16),
        scratch_shapes=[pltpu.VMEM((N_HEADS_C, DQK_C, DV_C), F32)],
        compiler_params=pltpu.CompilerParams(dimension_semantics=("arbitrary",)),
        name="retention",
    )(proj, proj, proj, cos, sin, qdec, kdec, dmat, sdec)


def _s5_kernel(*refs):
    ncb = S5_CH // LANES
    u_refs = refs[:ncb]
    (mt_ref, bt_ref, ctr_ref, cti_ref, are_ref, aim_ref, y_ref,
     ut_ref, yt_ref, ys_ref, vr_ref, vi_ref, spr_ref, spi_ref, carry_ref) = refs[ncb:]
    tc = S5_TC
    gp = S5_GROUP
    n = S5_STATE
    ng = S5_GROUPS

    @pl.when(pl.program_id(0) == 0)
    def _():
        carry_ref[...] = jnp.zeros(carry_ref.shape, F32)

    for s in range(S5_T):
        for k in range(ncb):
            ut_ref[s, k * LANES:(k + 1) * LANES, :] = u_refs[k][pl.ds(s, tc, stride=S5_T), :].T

    unroll = 4

    def intra(it, carry):
        for k in range(unroll):
            g = it * unroll + k
            r0 = pl.multiple_of(g * gp, gp)
            ug = ut_ref[:, pl.ds(r0, gp), :].reshape(S5_T * gp, tc).astype(BF16)
            yt_ref[:, pl.ds(r0, gp), :] = _dot(mt_ref[g], ug).reshape(S5_T, gp, tc)
            vt = _dot(bt_ref[g], ug)
            n0 = pl.multiple_of(g * n, n)
            vr_ref[pl.ds(n0, n), :] = vt[0:n]
            vi_ref[pl.ds(n0, n), :] = vt[n:2 * n]
        return carry

    lax.fori_loop(0, ng // unroll, intra, 0)

    sub = 8
    nv = tc // sub
    row = lax.broadcasted_iota(jnp.int32, (tc, LANES), 0)
    in_vreg = lax.rem(row, sub)

    def rows_of(v, r):
        return jnp.broadcast_to(v[r:r + 1], (tc, LANES))

    for j in range(ng * n // LANES):
        cols = slice(j * LANES, (j + 1) * LANES)
        pwr, pwi = are_ref[:, cols], aim_ref[:, cols]
        xr = vr_ref[cols, :].T
        xi = vi_ref[cols, :].T
        for d in (1, 2, 4):
            keep = in_vreg >= d
            sr = jnp.where(keep, pltpu.roll(xr, d, 0), 0.0)
            si = jnp.where(keep, pltpu.roll(xi, d, 0), 0.0)
            fr, fi = rows_of(pwr, d - 1), rows_of(pwi, d - 1)
            xr, xi = xr + (fr * sr - fi * si), xi + (fr * si + fi * sr)
        cr, ci = carry_ref[0, :, cols], carry_ref[1, :, cols]
        cr0, ci0 = cr, ci
        outr, outi = [], []
        for v in range(nv):
            yr = xr[v * sub:(v + 1) * sub] + (pwr * cr - pwi * ci)
            yi = xi[v * sub:(v + 1) * sub] + (pwr * ci + pwi * cr)
            outr.append(yr)
            outi.append(yi)
            cr = jnp.broadcast_to(yr[sub - 1:sub], (sub, LANES))
            ci = jnp.broadcast_to(yi[sub - 1:sub], (sub, LANES))
        carry_ref[0, :, cols] = cr
        carry_ref[1, :, cols] = ci
        sr = jnp.concatenate(outr, axis=0)
        si = jnp.concatenate(outi, axis=0)
        first = row == 0
        spr_ref[j] = jnp.where(first, rows_of(cr0, 0), pltpu.roll(sr, 1, 0))
        spi_ref[j] = jnp.where(first, rows_of(ci0, 0), pltpu.roll(si, 1, 0))

    def cross(it, carry):
        for k in range(unroll):
            jp = it * unroll + k
            r0 = pl.multiple_of(jp * 2 * gp, 2 * gp)
            yc = (_dot_nt(ctr_ref[jp], spr_ref[jp].astype(BF16))
                  + _dot_nt(cti_ref[jp], spi_ref[jp].astype(BF16)))
            yt_ref[:, pl.ds(r0, 2 * gp), :] += yc.reshape(S5_T, 2 * gp, tc)
        return carry

    lax.fori_loop(0, ng // 2 // unroll, cross, 0)

    for s in range(S5_T):
        for k in range(ncb):
            ys_ref[k, pl.ds(s, tc, stride=S5_T), :] = yt_ref[s, k * LANES:(k + 1) * LANES, :].T
    for k in range(ncb):
        y_ref[:, k * LANES:(k + 1) * LANES] = ys_ref[k]


def _s5_matrices(lam_re, lam_im, log_step, b_re, b_im, c_re, c_im, d_skip):
    hi = lax.Precision.HIGHEST
    t, gp, n, ng = S5_T, S5_GROUP, S5_STATE, S5_GROUPS
    lam = lax.complex(lam_re.astype(F32), lam_im.astype(F32))
    step = jnp.exp(log_step.astype(F32))[:, None]
    ls = lam * step
    a_bar = jnp.exp(ls)
    b_bar = ((a_bar - 1.0) / lam)[..., None] * lax.complex(b_re.astype(F32), b_im.astype(F32))
    cm = lax.complex(c_re.astype(F32), c_im.astype(F32))

    def apow(k):
        kk = k.astype(F32).astype(jnp.complex64)
        return jnp.exp(ls.reshape((ng,) + (1,) * k.ndim + (n,)) * kk[None, ..., None])

    tt = jnp.arange(t)
    kmat = jnp.einsum('gpn,gln,gnq->glpq', cm, apow(tt), b_bar, precision=hi).real
    krev = jnp.transpose(kmat[:, ::-1], (0, 2, 1, 3)).reshape(ng, gp, t * gp)
    kpad = jnp.pad(krev, ((0, 0), (0, 0), (0, t * gp)))
    mt = jnp.concatenate([kpad[:, :, (t - 1 - to) * gp:(2 * t - 1 - to) * gp] for to in range(t)], axis=1)
    dvec = jnp.tile(d_skip.astype(F32).reshape(ng, 1, gp), (1, t, 1)).reshape(ng, t * gp)
    mt = mt + jnp.eye(t * gp, dtype=F32)[None] * dvec[:, :, None]
    z = jnp.swapaxes(apow(t - 1 - tt), 1, 2)[:, :, :, None] * b_bar[:, :, None, :]
    z = z.reshape(ng, n, t * gp)
    bt = jnp.concatenate([z.real, z.imag], axis=1)
    w = cm[:, None, :, :] * apow(tt + 1)[:, :, None, :]

    def pair_readout(x):
        x = x.reshape(ng // 2, 2, t, gp, n)
        first = jnp.pad(x[:, 0], ((0, 0), (0, 0), (0, 0), (0, n)))
        second = jnp.pad(x[:, 1], ((0, 0), (0, 0), (0, 0), (n, 0)))
        return jnp.stack([first, second], axis=2).reshape(ng // 2, t * 2 * gp, 2 * n).astype(BF16)

    ctr, cti = pair_readout(w.real), pair_readout(-w.imag)
    a_chunk = jnp.transpose(apow(t * (jnp.arange(8) + 1)), (1, 0, 2)).reshape(8, ng * n)
    return mt.astype(BF16), bt.astype(BF16), ctr, cti, a_chunk.real, a_chunk.imag


def _s5(proj, mats):
    seq, width = proj.shape
    t, tc, gp, n, ng = S5_T, S5_TC, S5_GROUP, S5_STATE, S5_GROUPS
    rows = t * tc
    ncb = S5_CH // LANES
    cb0 = (width - S5_CH) // LANES
    u_specs = [pl.BlockSpec((rows, LANES), (lambda i, k=k: (i, cb0 + k))) for k in range(ncb)]
    nsb = ng * n // LANES
    return pl.pallas_call(
        _s5_kernel,
        grid=(seq // rows,),
        in_specs=u_specs + [_const_spec(m.shape) for m in mats],
        out_specs=pl.BlockSpec((rows, S5_CH), lambda i: (i, 0)),
        out_shape=jax.ShapeDtypeStruct((seq, S5_CH), F32),
        scratch_shapes=[
            pltpu.VMEM((t, S5_CH, tc), F32),
            pltpu.VMEM((t, S5_CH, tc), F32),
            pltpu.VMEM((ncb, rows, LANES), F32),
            pltpu.VMEM((ng * n, tc), F32),
            pltpu.VMEM((ng * n, tc), F32),
            pltpu.VMEM((nsb, tc, LANES), F32),
            pltpu.VMEM((nsb, tc, LANES), F32),
            pltpu.VMEM((2, 8, ng * n), F32),
        ],
        compiler_params=pltpu.CompilerParams(dimension_semantics=("arbitrary",)),
        name="s5_scan",
    )(*([proj] * ncb), *mats)


def _mix_ffn_kernel(*refs, glu, final):
    (x_ref, a_ref, b_ref, wo_ref, g1_ref), refs = refs[:5], refs[5:]
    if glu:
        gw_ref, refs = refs[0], refs[1:]
    (g_ref, sc_ref, sh_ref, gate_ref, win_ref, cw_ref, cb_ref, wout_ref), refs = refs[:8], refs[8:]
    if final:
        fg_ref, o_ref, h_ref, act_ref, gbuf_ref, carry_ref = refs
    else:
        ng_ref, nsc_ref, nsh_ref, o_ref, hn_ref, h_ref, act_ref, gbuf_ref, carry_ref = refs
    tm = x_ref.shape[0]
    halo = gbuf_ref.shape[0] - tm

    @pl.when(pl.program_id(0) == 0)
    def _():
        carry_ref[...] = jnp.zeros(carry_ref.shape, F32)

    if glu:
        y = jax.nn.gelu(b_ref[...]).astype(BF16)
        gg = _dot(y, gw_ref[...])
        half = gg.shape[1] // 2
        b = (gg[:, :half] * jax.nn.sigmoid(gg[:, half:])).astype(BF16)
    else:
        b = b_ref[...]
    cat = jnp.concatenate([a_ref[...], b], axis=1)
    x = x_ref[...] + g1_ref[...] * _dot(cat, wo_ref[...])
    h_ref[...] = _mod_rmsnorm(x, g_ref[...], sc_ref[...], sh_ref[...]).astype(BF16)
    for f in range(D_FF // TF_FFN):
        cs = slice(f * TF_FFN, (f + 1) * TF_FFN)
        gs = slice(D_FF + f * TF_FFN, D_FF + (f + 1) * TF_FFN)
        h = h_ref[...]
        val = _dot(h, win_ref[:, cs])
        gate = _dot(h, win_ref[:, gs])
        gbuf_ref[0:halo, :] = carry_ref[:, cs]
        gbuf_ref[halo:halo + tm, :] = gate
        carry_ref[:, cs] = gate[tm - halo:tm, :]
        conv = (gate * cw_ref[2:3, cs] + gbuf_ref[halo - 1:halo - 1 + tm, :] * cw_ref[1:2, cs]
                + gbuf_ref[halo - 2:halo - 2 + tm, :] * cw_ref[0:1, cs] + cb_ref[:, cs])
        act_ref[:, cs] = (jax.nn.gelu(conv) * val).astype(BF16)
    xn = x + gate_ref[...] * _dot(act_ref[...], wout_ref[...])
    if final:
        xn = xn * lax.rsqrt(jnp.mean(xn * xn, axis=-1, keepdims=True) + EPS) * fg_ref[...]
    else:
        hn_ref[...] = _mod_rmsnorm(xn, ng_ref[...], nsc_ref[...], nsh_ref[...]).astype(BF16)
    o_ref[...] = xn


def _layer_spec(shape, layer):
    idx = (layer,) + (0,) * (len(shape) - 1)
    return pl.BlockSpec((None,) + tuple(shape[1:]), lambda *_: idx, pipeline_mode=pl.Buffered(1))


def _mix_ffn(x, a, b, wo, gate1, glu_w, g, scale, shift, gate2, w_in, conv_w, conv_b, w_out, tail, layer):
    seq, d = x.shape
    final = len(tail) == 1
    tm = TM_FFN
    halo = 8
    row = pl.BlockSpec((1, d), lambda i: (0, 0))
    rows = lambda w: pl.BlockSpec((tm, w), lambda i: (i, 0))
    conv_b = conv_b.reshape(conv_b.shape[0], 1, D_FF)
    in_specs = [rows(d), rows(a.shape[1]), rows(b.shape[1]), _const_spec(wo.shape), row]
    args = [x, a, b, wo, gate1]
    if glu_w is not None:
        in_specs.append(_const_spec(glu_w.shape))
        args.append(glu_w)
    in_specs += [
        row, row, row, row,
        _layer_spec(w_in.shape, layer),
        _layer_spec(conv_w.shape, layer),
        _layer_spec(conv_b.shape, layer),
        _layer_spec(w_out.shape, layer),
    ] + [row] * len(tail)
    args += [g.reshape(1, d), scale, shift, gate2, w_in, conv_w, conv_b, w_out]
    args += [t.reshape(1, d) for t in tail]
    out_specs = [rows(d)] if final else [rows(d), rows(d)]
    out_shape = [jax.ShapeDtypeStruct((seq, d), F32)] + ([] if final else [jax.ShapeDtypeStruct((seq, d), BF16)])
    return pl.pallas_call(
        functools.partial(_mix_ffn_kernel, glu=glu_w is not None, final=final),
        grid=(seq // tm,),
        in_specs=in_specs,
        out_specs=out_specs,
        out_shape=out_shape,
        scratch_shapes=[
            pltpu.VMEM((tm, d), BF16),
            pltpu.VMEM((tm, D_FF), BF16),
            pltpu.VMEM((tm + halo, TF_FFN), F32),
            pltpu.VMEM((halo, D_FF), F32),
        ],
        compiler_params=pltpu.CompilerParams(dimension_semantics=("arbitrary",)),
        name="mix_ffn",
    )(*args)


def kernel(x, c, t5_table, mod_w, mod_b, norm1_g, norm2_g, ffn_w_in, ffn_conv_w, ffn_conv_b, ffn_w_out,
           ev_w_in, ev_w_out, diff_lambda, diff_subln_g, band_rel_bias,
           od_w_in, od_w_out, s5_lam_re, s5_lam_im, s5_log_step, s5_b_re, s5_b_im, s5_c_re, s5_c_im,
           s5_d, s5_glu_w, final_g):
    assert x.shape[0] == 1 and x.shape[2] == D_MODEL
    seq = x.shape[1]
    assert seq % TM_PROJ == 0 and seq % (S5_T * S5_TC) == 0
    d = D_MODEL
    xs = x[0]
    mod = _modulation(c, mod_w, mod_b)
    ffn_w_in_b = ffn_w_in.astype(BF16)
    ffn_w_out_b = ffn_w_out.astype(BF16)
    mods = [[mod[i, :, k * d:(k + 1) * d] for k in range(6)] for i in range(DEPTH)]
    h = None
    for i in range(DEPTH):
        sh1, sc1, g1, sh2, sc2, g2 = mods[i]
        w_in = (ev_w_in if i % 2 == 0 else od_w_in)[i // 2].astype(BF16)
        proj_dtype = BF16 if i % 2 == 0 else F32
        if h is None:
            proj = _normproj(xs, norm1_g[i], sc1, sh1, w_in, proj_dtype)
        else:
            proj = _proj(h, w_in, proj_dtype)
        if i % 2 == 0:
            e = i // 2
            lam_init = 0.8 - 0.6 * math.exp(-0.3 * i)
            lp = diff_lambda[e].astype(F32)
            lam = jnp.exp(jnp.sum(lp[0] * lp[1])) - jnp.exp(jnp.sum(lp[2] * lp[3])) + lam_init
            mix_a = _diff_attention(proj, t5_table, lam, diff_subln_g[e], lam_init)
            mix_b = _band_attention(proj, band_rel_bias[e])
            wo, glu_w = ev_w_out[e].astype(BF16), None
        else:
            o = i // 2
            mix_a = _retention(proj)
            mats = _s5_matrices(s5_lam_re[o], s5_lam_im[o], s5_log_step[o], s5_b_re[o], s5_b_im[o],
                                s5_c_re[o], s5_c_im[o], s5_d[o])
            mix_b = _s5(proj, mats)
            wo, glu_w = od_w_out[o].astype(BF16), s5_glu_w[o].astype(BF16)
        if i == DEPTH - 1:
            tail = (final_g,)
        else:
            nsh1, nsc1 = mods[i + 1][0], mods[i + 1][1]
            tail = (norm1_g[i + 1], nsc1, nsh1)
        out = _mix_ffn(xs, mix_a, mix_b, wo, g1, glu_w, norm2_g[i], sc2, sh2, g2,
                       ffn_w_in_b, ffn_conv_w, ffn_conv_b, ffn_w_out_b, tail, layer=i)
        if i == DEPTH - 1:
            xs = out[0]
        else:
            xs, h = out
    return xs[None]
```

```python
import functools
import math

import jax
import jax.numpy as jnp
from jax import lax
from jax.experimental import pallas as pl
from jax.experimental.pallas import tpu as pltpu

F32 = jnp.float32
BF16 = jnp.bfloat16

D_MODEL = 1024
DEPTH = 2
CHUNK = 64
GROUP_WIDTH = D_MODEL // 2
DK_A = 64
DV_A = 2 * DK_A
N_HEADS_A = GROUP_WIDTH // DV_A
DH_B = 64
N_HEADS_B = GROUP_WIDTH // DH_B
LEFT_CHUNKS = 8
REL_CLIP = 2 * CHUNK
NUM_BUCKETS = 32
MAX_DISTANCE = 128
DV_C = 128
DQK_C = DV_C // 2
N_HEADS_C = GROUP_WIDTH // DV_C
ROPE_BASE = 10000.0
S5_CH = GROUP_WIDTH
S5_GROUP = 16
S5_GROUPS = S5_CH // S5_GROUP
S5_STATE = 64
D_FF = ((8 * D_MODEL // 3 + 255) // 256) * 256
CONV_W = 3
EVEN_IN = 3 * N_HEADS_A * DV_A + 3 * N_HEADS_B * DH_B
ODD_IN = 2 * N_HEADS_C * DQK_C + 2 * N_HEADS_C * DV_C + S5_CH
EPS = 1e-6
NEG_INF = -1e30
LOG2E = math.log2(math.e)

LANES = 128
SUBLANES = 8
MXU_DIM = 256

TM_PROJ = 1024
TN_PROJ = 1024
TN_MOD = 1536
TM_FFN = 512
TF_FFN = MXU_DIM
BLK_A = 512
NPART_A = 2
ONES_A = 16
BLK_B = 1024
BAND_B = LEFT_CHUNKS * CHUNK
QW_B = 4 * CHUNK
BLK_C = 512
S5_T = 16
S5_TC = LANES

assert BLK_B % BAND_B == 0 and BLK_B % QW_B == 0 and BAND_B % QW_B == 0
assert BLK_A >= MAX_DISTANCE, "far key blocks must sit in the saturated T5 bucket"
assert DV_A == LANES and 2 * DK_A == LANES and 2 * DH_B == LANES, "attention heads are read as 128-lane column blocks"


def _dot(a, b):
    return jnp.dot(a, b, preferred_element_type=F32)


def _dot_nt(a, b):
    return lax.dot_general(a, b, (((1,), (1,)), ((), ())), preferred_element_type=F32)


def _dot_tn(a, b):
    return lax.dot_general(a, b, (((0,), (0,)), ((), ())), preferred_element_type=F32)


def _const_spec(shape):
    zeros = (0,) * len(shape)
    return pl.BlockSpec(shape, lambda *_: zeros, pipeline_mode=pl.Buffered(1))


def _mod_rmsnorm(x, g, scale, shift):
    y = x * lax.rsqrt(jnp.mean(x * x, axis=-1, keepdims=True) + EPS)
    y = y * g
    return y * (1.0 + scale) + shift


def _mod_kernel(c_ref, w_ref, b_ref, o_ref):
    c = c_ref[...]
    cond = c * jax.nn.sigmoid(c)
    o_ref[0] = jnp.sum(cond * w_ref[0], axis=0, keepdims=True) + b_ref[0]


def _modulation(c, mod_w, mod_b):
    depth, d, n = mod_w.shape
    tn = TN_MOD
    return pl.pallas_call(
        _mod_kernel,
        grid=(depth, n // tn),
        in_specs=[
            pl.BlockSpec((d, 1), lambda i, j: (0, 0)),
            pl.BlockSpec((1, d, tn), lambda i, j: (i, 0, j)),
            pl.BlockSpec((1, 1, tn), lambda i, j: (i, 0, j)),
        ],
        out_specs=pl.BlockSpec((1, 1, tn), lambda i, j: (i, 0, j)),
        out_shape=jax.ShapeDtypeStruct((depth, 1, n), F32),
        name="modulation",
    )(c.reshape(d, 1), mod_w, mod_b.reshape(depth, 1, n))


def _normproj_kernel(x_ref, g_ref, sc_ref, sh_ref, w_ref, o_ref):
    tm, n = o_ref.shape
    half = tm // 2
    for r in range(2):
        rows = slice(r * half, (r + 1) * half)
        h = _mod_rmsnorm(x_ref[rows, :], g_ref[...], sc_ref[...], sh_ref[...]).astype(BF16)
        for j in range(n // TN_PROJ):
            cols = slice(j * TN_PROJ, (j + 1) * TN_PROJ)
            o_ref[rows, cols] = _dot(h, w_ref[:, cols]).astype(o_ref.dtype)


def _normproj(x, g, scale, shift, w, out_dtype):
    seq, d = x.shape
    n = w.shape[1]
    tm = TM_PROJ
    row = pl.BlockSpec((1, d), lambda i: (0, 0))
    return pl.pallas_call(
        _normproj_kernel,
        grid=(seq // tm,),
        in_specs=[pl.BlockSpec((tm, d), lambda i: (i, 0)), row, row, row, _const_spec(w.shape)],
        out_specs=pl.BlockSpec((tm, n), lambda i: (i, 0)),
        out_shape=jax.ShapeDtypeStruct((seq, n), out_dtype),
        compiler_params=pltpu.CompilerParams(dimension_semantics=("parallel",)),
        name="normproj",
    )(x, g.reshape(1, d), scale, shift, w)


def _proj_kernel(h_ref, w_ref, o_ref):
    for j in range(o_ref.shape[1] // TN_PROJ):
        cols = slice(j * TN_PROJ, (j + 1) * TN_PROJ)
        o_ref[:, cols] = _dot(h_ref[...], w_ref[:, cols]).astype(o_ref.dtype)


def _proj(h, w, out_dtype):
    seq, d = h.shape
    n = w.shape[1]
    tm = TM_PROJ
    return pl.pallas_call(
        _proj_kernel,
        grid=(seq // tm,),
        in_specs=[pl.BlockSpec((tm, d), lambda i: (i, 0)), _const_spec(w.shape)],
        out_specs=pl.BlockSpec((tm, n), lambda i: (i, 0)),
        out_shape=jax.ShapeDtypeStruct((seq, n), out_dtype),
        compiler_params=pltpu.CompilerParams(dimension_semantics=("parallel",)),
        name="proj",
    )(h, w)


def _diffattn_kernel(q_ref, k_ref, v_ref, bias_ref, lam_ref, g_ref, o_ref,
                     qs_ref, vt_ref, m_ref, acc_ref, *s_refs, out_scale):
    blk = BLK_A
    nq = 2 * blk
    sub = SUBLANES
    dv = DV_A
    npart = len(s_refs) // 4
    wq = nq // npart
    sa_ref, sb_ref = s_refs[:2 * npart], s_refs[2 * npart:]
    i = pl.program_id(1)

    @pl.when(i == 0)
    def _():
        def tr(b, carry):
            r0 = pl.multiple_of(b * blk, blk)
            vt_ref[0:dv, pl.ds(r0, blk)] = v_ref[pl.ds(r0, blk), :].astype(F32).T.astype(BF16)
            vt_ref[dv:dv + ONES_A, pl.ds(r0, blk)] = jnp.ones((ONES_A, blk), BF16)
            return carry
        lax.fori_loop(0, v_ref.shape[0] // blk, tr, 0)

    q = q_ref[...].astype(F32) * (DK_A ** -0.5 * LOG2E)
    lane = lax.broadcasted_iota(jnp.int32, q.shape, 1)
    qs_ref[:, 0:blk] = jnp.where(lane < DK_A, q, 0.0).T.astype(BF16)
    qs_ref[:, blk:nq] = jnp.where(lane >= DK_A, q, 0.0).T.astype(BF16)
    m_ref[...] = jnp.full(m_ref.shape, NEG_INF, F32)
    acc_ref[...] = jnp.zeros(acc_ref.shape, F32)

    def scores(b, s_ref):
        k = k_ref[pl.ds(pl.multiple_of(b * blk, blk), blk), :]
        for part in range(npart):
            s = _dot(k, qs_ref[:, part * wq:(part + 1) * wq])
            s_ref[part][:, 0:wq] = s
            s_ref[npart + part][...] = jnp.max(s.reshape(blk // sub, sub, wq), axis=0)

    def softmax_pv(b, s_ref, bias):
        vt = vt_ref[:, pl.ds(pl.multiple_of(b * blk, blk), blk)]
        for part in range(npart):
            cols = slice(part * wq, (part + 1) * wq)
            s = s_ref[part][:, 0:wq]
            if bias is not None:
                q0 = (part * wq) % blk
                s = s + bias[:, q0:q0 + wq]
            s = s.reshape(blk // sub, sub, wq)
            m_prev = m_ref[:, cols]
            smax = jnp.max(s, axis=0) if bias is not None else s_ref[npart + part][...]
            m_cur = jnp.max(smax, axis=0, keepdims=True)
            m_new = jnp.maximum(m_prev, m_cur)
            alpha = jnp.exp2(m_prev - m_new)
            p = jnp.exp2(s - m_new[None])
            pv = _dot(vt, p.reshape(blk, wq).astype(BF16))
            acc_ref[:, cols] = acc_ref[:, cols] * alpha[0:1] + pv
            m_ref[:, cols] = m_new

    nfar = jnp.maximum(i - 1, 0)
    odd = lax.rem(nfar, 2)

    @pl.when(i == 0)
    def _():
        scores(0, sb_ref)

    @pl.when(i > 0)
    def _():
        @pl.when(odd == 1)
        def _():
            scores(0, sb_ref)
            scores(1, sa_ref)
            softmax_pv(0, sb_ref, None)

        @pl.when(odd == 0)
        def _():
            scores(0, sa_ref)

        def pair(b):
            scores(b + 1, sb_ref)
            softmax_pv(b, sa_ref, None)
            scores(b + 2, sa_ref)
            softmax_pv(b + 1, sb_ref, None)

        def quad_body(t, carry):
            pair(odd + 4 * t)
            pair(odd + 4 * t + 2)
            return carry

        npairs = nfar // 2
        lax.fori_loop(0, npairs // 2, quad_body, 0)

        @pl.when(lax.rem(npairs, 2) == 1)
        def _():
            pair(odd + 2 * (npairs - 1))
        scores(i, sb_ref)
        softmax_pv(i - 1, sa_ref, bias_ref[0, 0])

    softmax_pv(i, sb_ref, bias_ref[0, 1])

    ot = acc_ref[0:dv, 0:nq] / acc_ref[dv:dv + 1, 0:nq]
    o = ot[:, 0:blk].T - lam_ref[...] * ot[:, blk:nq].T
    o = o * lax.rsqrt(jnp.mean(o * o, axis=-1, keepdims=True) + EPS) * g_ref[...]
    o_ref[...] = (o * out_scale).astype(o_ref.dtype)


_TOEPLITZ_ROWS = 256
_TOEPLITZ_N = 2048


def _toeplitz_kernel(v_ref, o_ref, *, keep):
    rows, cols = o_ref.shape[1:]
    x = jnp.broadcast_to(v_ref[0, 0], (rows, v_ref.shape[-1]))
    tile = pltpu.roll(x, 0, 1, stride=1, stride_axis=0)[:, :cols]
    r = lax.broadcasted_iota(jnp.int32, (rows, cols), 0) + pl.program_id(1) * rows
    c = lax.broadcasted_iota(jnp.int32, (rows, cols), 1)
    o_ref[0] = jnp.where(keep(r, c), tile, NEG_INF)


def _toeplitz_tiles(fn, keep, heads, rows, cols):
    n, rb = _TOEPLITZ_N, _TOEPLITZ_ROWS
    assert rows % rb == 0 and rows <= n // 2 and cols <= n // 2
    idx = jnp.arange(n, dtype=jnp.int32)
    vec = fn(jnp.where(idx < n // 2, idx, idx - n)).astype(F32)
    vecs = jnp.stack([jnp.roll(vec, k * rb, axis=1) for k in range(rows // rb)], axis=1)
    return pl.pallas_call(
        functools.partial(_toeplitz_kernel, keep=keep),
        grid=(heads, rows // rb),
        in_specs=[pl.BlockSpec((1, 1, 1, n), lambda h, k: (h, k, 0, 0))],
        out_specs=pl.BlockSpec((1, rb, cols), lambda h, k: (h, k, 0)),
        out_shape=jax.ShapeDtypeStruct((heads, rows, cols), F32),
        name="toeplitz_tiles",
    )(vecs.reshape(heads, rows // rb, 1, n))


def _t5_bucket(rel):
    nb = NUM_BUCKETS // 2
    max_exact = nb // 2
    bucket = jnp.where(rel > 0, nb, 0)
    n = jnp.abs(rel)
    nf = jnp.maximum(n, 1).astype(F32)
    large = max_exact + (jnp.log(nf / max_exact) / math.log(MAX_DISTANCE / max_exact)
                         * (nb - max_exact)).astype(jnp.int32)
    large = jnp.minimum(large, nb - 1)
    return bucket + jnp.where(n < max_exact, n, large)


def _diff_bias_tiles(t5_table):
    blk = BLK_A
    table = t5_table.astype(F32)
    far = table[_t5_bucket(jnp.full((), -(blk + 1), jnp.int32))]
    def visible(r, c):
        return jnp.floor_divide(r - blk, CHUNK) <= jnp.floor_divide(c, CHUNK)

    tiles = _toeplitz_tiles(lambda x: ((table[_t5_bucket(-x - blk)] - far) * LOG2E).T, visible,
                            N_HEADS_A, 2 * blk, blk)
    return tiles.reshape(N_HEADS_A, 2, blk, blk)


def _diff_attention(proj, t5_table, lam, subln_g, lam_init):
    seq = proj.shape[0]
    blk = BLK_A
    bias = _diff_bias_tiles(t5_table)
    ha = N_HEADS_A
    kern = functools.partial(_diffattn_kernel, out_scale=1.0 - lam_init)
    return pl.pallas_call(
        kern,
        grid=(ha, seq // blk),
        in_specs=[
            pl.BlockSpec((blk, DV_A), lambda h, i: (i, h)),
            pl.BlockSpec((seq, DV_A), lambda h, i: (0, ha + h)),
            pl.BlockSpec((seq, DV_A), lambda h, i: (0, 2 * ha + h)),
            pl.BlockSpec((1, 2, blk, blk), lambda h, i: (h, 0, 0, 0)),
            pl.BlockSpec((1, DV_A), lambda h, i: (0, 0)),
            pl.BlockSpec((1, DV_A), lambda h, i: (0, 0)),
        ],
        out_specs=pl.BlockSpec((blk, DV_A), lambda h, i: (i, h)),
        out_shape=jax.ShapeDtypeStruct((seq, ha * DV_A), BF16),
        scratch_shapes=[
            pltpu.VMEM((DV_A, 2 * blk), BF16),
            pltpu.VMEM((DV_A + ONES_A, seq), BF16),
            pltpu.VMEM((SUBLANES, 2 * blk), F32),
            pltpu.VMEM((DV_A + ONES_A, 2 * blk + LANES), F32),
        ] + 2 * ([pltpu.VMEM((blk, 2 * blk // NPART_A + LANES), F32)] * NPART_A
                 + [pltpu.VMEM((SUBLANES, 2 * blk // NPART_A), F32)] * NPART_A),
        compiler_params=pltpu.CompilerParams(dimension_semantics=("parallel", "arbitrary")),
        name="diff_attention",
    )(proj, proj, proj, bias, jnp.full((1, DV_A), lam, F32), subln_g.reshape(1, DV_A).astype(F32))


def _band_kernel(q_ref, kp_ref, kc_ref, vp_ref, vc_ref, bias_ref, o_ref, *s_refs):
    qw, band = QW_B, BAND_B
    nk = band + qw
    sub = SUBLANES
    i = pl.program_id(1)
    q = q_ref[...].astype(F32) * (DH_B ** -0.5 * LOG2E)
    lane = lax.broadcasted_iota(jnp.int32, q.shape, 1)
    qh = (jnp.where(lane < DH_B, q, 0.0).T.astype(BF16), jnp.where(lane >= DH_B, q, 0.0).T.astype(BF16))
    k_all = jnp.concatenate([kp_ref[...], kc_ref[...]], axis=0)
    vt_all = jnp.concatenate([vp_ref[...], vc_ref[...]], axis=0).astype(F32).T.astype(BF16)
    bias = jnp.concatenate([bias_ref[0], bias_ref[1]], axis=1)
    no_prev = jnp.where(i == 0, NEG_INF, 0.0).astype(F32)
    krow = lax.broadcasted_iota(jnp.int32, (nk, 2 * qw), 0)
    ngroups = len(s_refs)
    for g in range(ngroups):
        k0 = g * qw
        qs = jnp.concatenate([qh[0][:, k0:k0 + qw], qh[1][:, k0:k0 + qw]], axis=1)
        s_refs[g][:, 0:2 * qw] = _dot(k_all[k0:k0 + nk], qs)
    for g in range(ngroups):
        k0 = g * qw
        s = s_refs[g][:, 0:2 * qw] + bias
        if k0 < band:
            s = s + jnp.where(krow < band - k0, no_prev, 0.0)
        s = s.reshape(nk // sub, sub, 2 * qw)
        m = jnp.max(jnp.max(s, axis=0), axis=0, keepdims=True)
        p = jnp.exp2(s - m[None])
        l = jnp.sum(jnp.sum(p, axis=0), axis=0, keepdims=True)
        ot = _dot(vt_all[:, k0:k0 + nk], p.reshape(nk, 2 * qw).astype(BF16)) / l
        o = jnp.concatenate([ot[0:DH_B, 0:qw], ot[DH_B:2 * DH_B, qw:2 * qw]], axis=0)
        o_ref[k0:k0 + qw, :] = o.T.astype(o_ref.dtype)


def _band_bias_tiles(rel_bias):
    band = BAND_B

    def valid(r, c):
        qchunk = jnp.floor_divide(c, CHUNK)
        kchunk = jnp.floor_divide(r - band, CHUNK)
        return (kchunk <= qchunk) & (kchunk >= qchunk - LEFT_CHUNKS)

    return _toeplitz_tiles(
        lambda x: rel_bias.astype(F32)[:, jnp.clip(-x - band, -REL_CLIP, REL_CLIP) + REL_CLIP] * LOG2E, valid,
        N_HEADS_B, band + QW_B, QW_B)


def _band_attention(proj, rel_bias):
    seq = proj.shape[0]
    blk, band, qw = BLK_B, BAND_B, QW_B
    bias = _band_bias_tiles(rel_bias)
    npair = N_HEADS_B // 2
    qc0 = 3 * N_HEADS_A
    per = blk // band
    prev = lambda c0: (lambda hp, i: (jnp.maximum(i * per - 1, 0), c0 + hp))
    cur = lambda c0: (lambda hp, i: (i, c0 + hp))
    return pl.pallas_call(
        _band_kernel,
        grid=(npair, seq // blk),
        in_specs=[
            pl.BlockSpec((blk, LANES), cur(qc0)),
            pl.BlockSpec((band, LANES), prev(qc0 + npair)),
            pl.BlockSpec((blk, LANES), cur(qc0 + npair)),
            pl.BlockSpec((band, LANES), prev(qc0 + 2 * npair)),
            pl.BlockSpec((blk, LANES), cur(qc0 + 2 * npair)),
            pl.BlockSpec((2, band + qw, qw), lambda hp, i: (hp, 0, 0)),
        ],
        out_specs=pl.BlockSpec((blk, LANES), lambda hp, i: (i, hp)),
        out_shape=jax.ShapeDtypeStruct((seq, N_HEADS_B * DH_B), BF16),
        scratch_shapes=[pltpu.VMEM((band + qw, 2 * qw + LANES), F32)] * (blk // qw),
        compiler_params=pltpu.CompilerParams(dimension_semantics=("parallel", "arbitrary")),
        name="band_attention",
    )(proj, proj, proj, proj, proj, bias)


def _retention_kernel(qk_ref, v_ref, gate_ref, cos_ref, sin_ref, qdec_ref, kdec_ref, dmat_ref,
                      sdec_ref, o_ref, state_ref):
    @pl.when(pl.program_id(0) == 0)
    def _():
        state_ref[...] = jnp.zeros(state_ref.shape, F32)

    cos = cos_ref[...]
    sin = sin_ref[...]
    lane = lax.broadcasted_iota(jnp.int32, cos.shape, 1)
    first_half = (lane % DQK_C) < (DQK_C // 2)
    qk = qk_ref[...]
    parts = []
    for j in range(qk.shape[1] // LANES):
        t = qk[:, j * LANES:(j + 1) * LANES]
        partner = jnp.where(first_half, pltpu.roll(t, LANES - DQK_C // 2, 1), pltpu.roll(t, DQK_C // 2, 1))
        parts.append(t * cos + partner * sin)
    wq = N_HEADS_C * DQK_C
    q = jnp.concatenate(parts[:wq // LANES], axis=1)
    k = jnp.concatenate(parts[wq // LANES:], axis=1) * (DQK_C ** -0.5)
    qd = (q * qdec_ref[...]).astype(BF16)
    kd = (k * kdec_ref[...]).astype(BF16)
    qb = q.astype(BF16)
    kb = k.astype(BF16)
    vb = v_ref[...].astype(BF16)
    gate = gate_ref[...]
    outs = []
    for h in range(N_HEADS_C):
        qs = slice(h * DQK_C, (h + 1) * DQK_C)
        vs = slice(h * DV_C, (h + 1) * DV_C)
        scores = _dot_nt(qb[:, qs], kb[:, qs]) * dmat_ref[h]
        state = state_ref[h]
        r = _dot(scores.astype(BF16), vb[:, vs]) + _dot(qd[:, qs], state.astype(BF16))
        state_ref[h] = state * sdec_ref[h] + _dot_tn(kd[:, qs], vb[:, vs])
        r = r * lax.rsqrt(jnp.mean(r * r, axis=-1, keepdims=True) + EPS)
        g = gate[:, vs]
        outs.append(r * (g * jax.nn.sigmoid(g)))
    o_ref[...] = jnp.concatenate(outs, axis=1).astype(o_ref.dtype)


def _retention_tables(seq):
    t = BLK_C
    half = DQK_C // 2
    inv_freq = 1.0 / (ROPE_BASE ** (jnp.arange(0, DQK_C, 2, dtype=F32) / DQK_C))
    ang = jnp.arange(seq, dtype=F32)[:, None] * inv_freq[None, :]
    reps = LANES // half
    cos = jnp.tile(jnp.cos(ang), (1, reps))
    sign = jnp.where((jnp.arange(LANES) % DQK_C) < half, -1.0, 1.0).astype(F32)
    sin = jnp.tile(jnp.sin(ang), (1, reps)) * sign[None, :]
    log_g = jnp.log(1.0 - jnp.power(2.0, -5.0 - jnp.arange(N_HEADS_C, dtype=F32)))
    pos = jnp.arange(t, dtype=F32)
    diff = pos[:, None] - pos[None, :]
    same_or_past = (jnp.arange(t)[None, :] // CHUNK) <= (jnp.arange(t)[:, None] // CHUNK)
    dmat = jnp.where(same_or_past[None], jnp.exp(log_g[:, None, None] * jnp.abs(diff)[None]), 0.0)
    qdec = jnp.repeat(jnp.exp(log_g[None, :] * (pos[:, None] + 1.0)), DQK_C, axis=1)
    kdec = jnp.repeat(jnp.exp(log_g[None, :] * (t - 1.0 - pos[:, None])), DQK_C, axis=1)
    sdec = jnp.broadcast_to(jnp.exp(log_g * t)[:, None, None], (N_HEADS_C, 1, DV_C))
    return cos, sin, qdec, kdec, dmat, sdec


def _retention(proj):
    seq = proj.shape[0]
    t = BLK_C
    cos, sin, qdec, kdec, dmat, sdec = _retention_tables(seq)
    wv = N_HEADS_C * DV_C
    return pl.pallas_call(
        _retention_kernel,
        grid=(seq // t,),
        in_specs=[
            pl.BlockSpec((t, wv), lambda i: (i, 0)),
            pl.BlockSpec((t, wv), lambda i: (i, 1)),
            pl.BlockSpec((t, wv), lambda i: (i, 2)),
            pl.BlockSpec((t, LANES), lambda i: (i, 0)),
            pl.BlockSpec((t, LANES), lambda i: (i, 0)),
            pl.BlockSpec((t, N_HEADS_C * DQK_C), lambda i: (0, 0)),
            pl.BlockSpec((t, N_HEADS_C * DQK_C), lambda i: (0, 0)),
            pl.BlockSpec((N_HEADS_C, t, t), lambda i: (0, 0, 0)),
            pl.BlockSpec((N_HEADS_C, 1, DV_C), lambda i: (0, 0, 0)),
        ],
        out_specs=pl.BlockSpec((t, wv), lambda i: (i, 0)),
        out_shape=jax.ShapeDtypeStruct((seq, wv), BF16),
        scratch_shapes=[pltpu.VMEM((N_HEADS_C, DQK_C, DV_C), F32)],
        compiler_params=pltpu.CompilerParams(dimension_semantics=("arbitrary",)),
        name="retention",
    )(proj, proj, proj, cos, sin, qdec, kdec, dmat, sdec)


def _s5_kernel(*refs):
    ncb = S5_CH // LANES
    u_refs = refs[:ncb]
    (mt_ref, bt_ref, ctr_ref, cti_ref, are_ref, aim_ref, y_ref,
     ut_ref, yt_ref, ys_ref, vr_ref, vi_ref, spr_ref, spi_ref, carry_ref) = refs[ncb:]
    tc = S5_TC
    gp = S5_GROUP
    n = S5_STATE
    ng = S5_GROUPS

    @pl.when(pl.program_id(0) == 0)
    def _():
        carry_ref[...] = jnp.zeros(carry_ref.shape, F32)

    for s in range(S5_T):
        for k in range(ncb):
            ut_ref[s, k * LANES:(k + 1) * LANES, :] = u_refs[k][pl.ds(s, tc, stride=S5_T), :].T

    unroll = 4

    def intra(it, carry):
        for k in range(unroll):
            g = it * unroll + k
            r0 = pl.multiple_of(g * gp, gp)
            ug = ut_ref[:, pl.ds(r0, gp), :].reshape(S5_T * gp, tc).astype(BF16)
            yt_ref[:, pl.ds(r0, gp), :] = _dot(mt_ref[g], ug).reshape(S5_T, gp, tc)
            vt = _dot(bt_ref[g], ug)
            n0 = pl.multiple_of(g * n, n)
            vr_ref[pl.ds(n0, n), :] = vt[0:n]
            vi_ref[pl.ds(n0, n), :] = vt[n:2 * n]
        return carry

    lax.fori_loop(0, ng // unroll, intra, 0)

    sub = SUBLANES
    nv = tc // sub
    row = lax.broadcasted_iota(jnp.int32, (tc, LANES), 0)
    in_vreg = lax.rem(row, sub)

    def rows_of(v, r):
        return jnp.broadcast_to(v[r:r + 1], (tc, LANES))

    for j in range(ng * n // LANES):
        cols = slice(j * LANES, (j + 1) * LANES)
        pwr, pwi = are_ref[:, cols], aim_ref[:, cols]
        xr = vr_ref[cols, :].T
        xi = vi_ref[cols, :].T
        for d in (1, 2, 4):
            keep = in_vreg >= d
            sr = jnp.where(keep, pltpu.roll(xr, d, 0), 0.0)
            si = jnp.where(keep, pltpu.roll(xi, d, 0), 0.0)
            fr, fi = rows_of(pwr, d - 1), rows_of(pwi, d - 1)
            xr, xi = xr + (fr * sr - fi * si), xi + (fr * si + fi * sr)
        cr, ci = carry_ref[0, :, cols], carry_ref[1, :, cols]
        cr0, ci0 = cr, ci
        outr, outi = [], []
        for v in range(nv):
            yr = xr[v * sub:(v + 1) * sub] + (pwr * cr - pwi * ci)
            yi = xi[v * sub:(v + 1) * sub] + (pwr * ci + pwi * cr)
            outr.append(yr)
            outi.append(yi)
            cr = jnp.broadcast_to(yr[sub - 1:sub], (sub, LANES))
            ci = jnp.broadcast_to(yi[sub - 1:sub], (sub, LANES))
        carry_ref[0, :, cols] = cr
        carry_ref[1, :, cols] = ci
        sr = jnp.concatenate(outr, axis=0)
        si = jnp.concatenate(outi, axis=0)
        first = row == 0
        spr_ref[j] = jnp.where(first, rows_of(cr0, 0), pltpu.roll(sr, 1, 0))
        spi_ref[j] = jnp.where(first, rows_of(ci0, 0), pltpu.roll(si, 1, 0))

    def cross(it, carry):
        for k in range(unroll):
            jp = it * unroll + k
            r0 = pl.multiple_of(jp * 2 * gp, 2 * gp)
            yc = (_dot_nt(ctr_ref[jp], spr_ref[jp].astype(BF16))
                  + _dot_nt(cti_ref[jp], spi_ref[jp].astype(BF16)))
            yt_ref[:, pl.ds(r0, 2 * gp), :] += yc.reshape(S5_T, 2 * gp, tc)
        return carry

    lax.fori_loop(0, ng // 2 // unroll, cross, 0)

    for s in range(S5_T):
        for k in range(ncb):
            ys_ref[k, pl.ds(s, tc, stride=S5_T), :] = yt_ref[s, k * LANES:(k + 1) * LANES, :].T
    for k in range(ncb):
        y_ref[:, k * LANES:(k + 1) * LANES] = ys_ref[k]


def _s5_matrices(lam_re, lam_im, log_step, b_re, b_im, c_re, c_im, d_skip):
    hi = lax.Precision.HIGHEST
    t, gp, n, ng = S5_T, S5_GROUP, S5_STATE, S5_GROUPS
    lam = lax.complex(lam_re.astype(F32), lam_im.astype(F32))
    step = jnp.exp(log_step.astype(F32))[:, None]
    ls = lam * step
    a_bar = jnp.exp(ls)
    b_bar = ((a_bar - 1.0) / lam)[..., None] * lax.complex(b_re.astype(F32), b_im.astype(F32))
    cm = lax.complex(c_re.astype(F32), c_im.astype(F32))

    def apow(k):
        kk = k.astype(F32).astype(jnp.complex64)
        return jnp.exp(ls.reshape((ng,) + (1,) * k.ndim + (n,)) * kk[None, ..., None])

    tt = jnp.arange(t)
    kmat = jnp.einsum('gpn,gln,gnq->glpq', cm, apow(tt), b_bar, precision=hi).real
    krev = jnp.transpose(kmat[:, ::-1], (0, 2, 1, 3)).reshape(ng, gp, t * gp)
    kpad = jnp.pad(krev, ((0, 0), (0, 0), (0, t * gp)))
    mt = jnp.concatenate([kpad[:, :, (t - 1 - to) * gp:(2 * t - 1 - to) * gp] for to in range(t)], axis=1)
    dvec = jnp.tile(d_skip.astype(F32).reshape(ng, 1, gp), (1, t, 1)).reshape(ng, t * gp)
    mt = mt + jnp.eye(t * gp, dtype=F32)[None] * dvec[:, :, None]
    z = jnp.swapaxes(apow(t - 1 - tt), 1, 2)[:, :, :, None] * b_bar[:, :, None, :]
    z = z.reshape(ng, n, t * gp)
    bt = jnp.concatenate([z.real, z.imag], axis=1)
    w = cm[:, None, :, :] * apow(tt + 1)[:, :, None, :]

    def pair_readout(x):
        x = x.reshape(ng // 2, 2, t, gp, n)
        first = jnp.pad(x[:, 0], ((0, 0), (0, 0), (0, 0), (0, n)))
        second = jnp.pad(x[:, 1], ((0, 0), (0, 0), (0, 0), (n, 0)))
        return jnp.stack([first, second], axis=2).reshape(ng // 2, t * 2 * gp, 2 * n).astype(BF16)

    ctr, cti = pair_readout(w.real), pair_readout(-w.imag)
    a_chunk = jnp.transpose(apow(t * (jnp.arange(SUBLANES) + 1)), (1, 0, 2)).reshape(SUBLANES, ng * n)
    return mt.astype(BF16), bt.astype(BF16), ctr, cti, a_chunk.real, a_chunk.imag


def _s5(proj, mats):
    seq, width = proj.shape
    t, tc, gp, n, ng = S5_T, S5_TC, S5_GROUP, S5_STATE, S5_GROUPS
    rows = t * tc
    ncb = S5_CH // LANES
    cb0 = (width - S5_CH) // LANES
    u_specs = [pl.BlockSpec((rows, LANES), (lambda i, k=k: (i, cb0 + k))) for k in range(ncb)]
    nsb = ng * n // LANES
    return pl.pallas_call(
        _s5_kernel,
        grid=(seq // rows,),
        in_specs=u_specs + [_const_spec(m.shape) for m in mats],
        out_specs=pl.BlockSpec((rows, S5_CH), lambda i: (i, 0)),
        out_shape=jax.ShapeDtypeStruct((seq, S5_CH), F32),
        scratch_shapes=[
            pltpu.VMEM((t, S5_CH, tc), F32),
            pltpu.VMEM((t, S5_CH, tc), F32),
            pltpu.VMEM((ncb, rows, LANES), F32),
            pltpu.VMEM((ng * n, tc), F32),
            pltpu.VMEM((ng * n, tc), F32),
            pltpu.VMEM((nsb, tc, LANES), F32),
            pltpu.VMEM((nsb, tc, LANES), F32),
            pltpu.VMEM((2, SUBLANES, ng * n), F32),
        ],
        compiler_params=pltpu.CompilerParams(dimension_semantics=("arbitrary",)),
        name="s5_scan",
    )(*([proj] * ncb), *mats)


def _mix_ffn_kernel(*refs, glu, final):
    (x_ref, a_ref, b_ref, wo_ref, g1_ref), refs = refs[:5], refs[5:]
    if glu:
        gw_ref, refs = refs[0], refs[1:]
    (g_ref, sc_ref, sh_ref, gate_ref, win_ref, cw_ref, cb_ref, wout_ref), refs = refs[:8], refs[8:]
    if final:
        fg_ref, o_ref, h_ref, act_ref, gbuf_ref, carry_ref = refs
    else:
        ng_ref, nsc_ref, nsh_ref, o_ref, hn_ref, h_ref, act_ref, gbuf_ref, carry_ref = refs
    tm = x_ref.shape[0]
    halo = gbuf_ref.shape[0] - tm

    @pl.when(pl.program_id(0) == 0)
    def _():
        carry_ref[...] = jnp.zeros(carry_ref.shape, F32)

    if glu:
        y = jax.nn.gelu(b_ref[...]).astype(BF16)
        gg = _dot(y, gw_ref[...])
        half = gg.shape[1] // 2
        b = (gg[:, :half] * jax.nn.sigmoid(gg[:, half:])).astype(BF16)
    else:
        b = b_ref[...]
    cat = jnp.concatenate([a_ref[...], b], axis=1)
    x = x_ref[...] + g1_ref[...] * _dot(cat, wo_ref[...])
    d = x.shape[1]
    h_ref[:, 0:d] = _mod_rmsnorm(x, g_ref[...], sc_ref[...], sh_ref[...]).astype(BF16)
    for f in range(D_FF // TF_FFN):
        cs = slice(f * TF_FFN, (f + 1) * TF_FFN)
        gs = slice(D_FF + f * TF_FFN, D_FF + (f + 1) * TF_FFN)
        h = h_ref[:, 0:d]
        val = _dot(h, win_ref[:, cs])
        gate = _dot(h, win_ref[:, gs])
        gbuf_ref[0:halo, :] = carry_ref[:, cs]
        gbuf_ref[halo:halo + tm, :] = gate
        carry_ref[:, cs] = gate[tm - halo:tm, :]
        conv = (gate * cw_ref[2:3, cs] + gbuf_ref[halo - 1:halo - 1 + tm, :] * cw_ref[1:2, cs]
                + gbuf_ref[halo - 2:halo - 2 + tm, :] * cw_ref[0:1, cs] + cb_ref[:, cs])
        act_ref[:, cs] = (jax.nn.gelu(conv) * val).astype(BF16)
    xn = x + gate_ref[...] * _dot(act_ref[:, 0:D_FF], wout_ref[...])
    if final:
        xn = xn * lax.rsqrt(jnp.mean(xn * xn, axis=-1, keepdims=True) + EPS) * fg_ref[...]
    else:
        hn_ref[...] = _mod_rmsnorm(xn, ng_ref[...], nsc_ref[...], nsh_ref[...]).astype(BF16)
    o_ref[...] = xn


def _layer_spec(shape, layer):
    idx = (layer,) + (0,) * (len(shape) - 1)
    return pl.BlockSpec((None,) + tuple(shape[1:]), lambda *_: idx, pipeline_mode=pl.Buffered(1))


def _mix_ffn(x, a, b, wo, gate1, glu_w, g, scale, shift, gate2, w_in, conv_w, conv_b, w_out, tail, layer):
    seq, d = x.shape
    final = len(tail) == 1
    tm = TM_FFN
    halo = SUBLANES
    row = pl.BlockSpec((1, d), lambda i: (0, 0))
    rows = lambda w: pl.BlockSpec((tm, w), lambda i: (i, 0))
    conv_b = conv_b.reshape(conv_b.shape[0], 1, D_FF)
    in_specs = [rows(d), rows(a.shape[1]), rows(b.shape[1]), _const_spec(wo.shape), row]
    args = [x, a, b, wo, gate1]
    if glu_w is not None:
        in_specs.append(_const_spec(glu_w.shape))
        args.append(glu_w)
    in_specs += [
        row, row, row, row,
        _layer_spec(w_in.shape, layer),
        _layer_spec(conv_w.shape, layer),
        _layer_spec(conv_b.shape, layer),
        _layer_spec(w_out.shape, layer),
    ] + [row] * len(tail)
    args += [g.reshape(1, d), scale, shift, gate2, w_in, conv_w, conv_b, w_out]
    args += [t.reshape(1, d) for t in tail]
    out_specs = [rows(d)] if final else [rows(d), rows(d)]
    out_shape = [jax.ShapeDtypeStruct((seq, d), F32)] + ([] if final else [jax.ShapeDtypeStruct((seq, d), BF16)])
    return pl.pallas_call(
        functools.partial(_mix_ffn_kernel, glu=glu_w is not None, final=final),
        grid=(seq // tm,),
        in_specs=in_specs,
        out_specs=out_specs,
        out_shape=out_shape,
        scratch_shapes=[
            pltpu.VMEM((tm, d + LANES), BF16),
            pltpu.VMEM((tm, D_FF + LANES), BF16),
            pltpu.VMEM((tm + halo, TF_FFN), F32),
            pltpu.VMEM((halo, D_FF), F32),
        ],
        compiler_params=pltpu.CompilerParams(dimension_semantics=("arbitrary",)),
        name="mix_ffn",
    )(*args)


def kernel(x, c, t5_table, mod_w, mod_b, norm1_g, norm2_g, ffn_w_in, ffn_conv_w, ffn_conv_b, ffn_w_out,
           ev_w_in, ev_w_out, diff_lambda, diff_subln_g, band_rel_bias,
           od_w_in, od_w_out, s5_lam_re, s5_lam_im, s5_log_step, s5_b_re, s5_b_im, s5_c_re, s5_c_im,
           s5_d, s5_glu_w, final_g):
    assert x.shape[0] == 1 and x.shape[2] == D_MODEL
    seq = x.shape[1]
    assert seq % TM_PROJ == 0 and seq % (S5_T * S5_TC) == 0
    d = D_MODEL
    xs = x[0]
    mod = _modulation(c, mod_w, mod_b)
    ffn_w_in_b = ffn_w_in.astype(BF16)
    ffn_w_out_b = ffn_w_out.astype(BF16)
    mods = [[mod[i, :, k * d:(k + 1) * d] for k in range(6)] for i in range(DEPTH)]
    h = None
    for i in range(DEPTH):
        sh1, sc1, g1, sh2, sc2, g2 = mods[i]
        w_in = (ev_w_in if i % 2 == 0 else od_w_in)[i // 2].astype(BF16)
        proj_dtype = BF16 if i % 2 == 0 else F32
        if h is None:
            proj = _normproj(xs, norm1_g[i], sc1, sh1, w_in, proj_dtype)
        else:
            proj = _proj(h, w_in, proj_dtype)
        if i % 2 == 0:
            e = i // 2
            lam_init = 0.8 - 0.6 * math.exp(-0.3 * i)
            lp = diff_lambda[e].astype(F32)
            lam = jnp.exp(jnp.sum(lp[0] * lp[1])) - jnp.exp(jnp.sum(lp[2] * lp[3])) + lam_init
            mix_a = _diff_attention(proj, t5_table, lam, diff_subln_g[e], lam_init)
            mix_b = _band_attention(proj, band_rel_bias[e])
            wo, glu_w = ev_w_out[e].astype(BF16), None
        else:
            o = i // 2
            mix_a = _retention(proj)
            mats = _s5_matrices(s5_lam_re[o], s5_lam_im[o], s5_log_step[o], s5_b_re[o], s5_b_im[o],
                                s5_c_re[o], s5_c_im[o], s5_d[o])
            mix_b = _s5(proj, mats)
            wo, glu_w = od_w_out[o].astype(BF16), s5_glu_w[o].astype(BF16)
        if i == DEPTH - 1:
            tail = (final_g,)
        else:
            nsh1, nsc1 = mods[i + 1][0], mods[i + 1][1]
            tail = (norm1_g[i + 1], nsc1, nsh1)
        out = _mix_ffn(xs, mix_a, mix_b, wo, g1, glu_w, norm2_g[i], sc2, sh2, g2,
                       ffn_w_in_b, ffn_conv_w, ffn_conv_b, ffn_w_out_b, tail, layer=i)
        if i == DEPTH - 1:
            xs = out[0]
        else:
            xs, h = out
    return xs[None]
```

```python
import functools
import math

import jax
import jax.numpy as jnp
from jax import lax
from jax.experimental import pallas as pl
from jax.experimental.pallas import tpu as pltpu

F32 = jnp.float32
BF16 = jnp.bfloat16

D_MODEL = 1024
DEPTH = 2
CHUNK = 64
GROUP_WIDTH = D_MODEL // 2
DK_A = 64
DV_A = 2 * DK_A
N_HEADS_A = GROUP_WIDTH // DV_A
DH_B = 64
N_HEADS_B = GROUP_WIDTH // DH_B
LEFT_CHUNKS = 8
REL_CLIP = 2 * CHUNK
NUM_BUCKETS = 32
MAX_DISTANCE = 128
DV_C = 128
DQK_C = DV_C // 2
N_HEADS_C = GROUP_WIDTH // DV_C
ROPE_BASE = 10000.0
S5_CH = GROUP_WIDTH
S5_GROUP = 16
S5_GROUPS = S5_CH // S5_GROUP
S5_STATE = 64
D_FF = ((8 * D_MODEL // 3 + 255) // 256) * 256
CONV_W = 3
EVEN_IN = 3 * N_HEADS_A * DV_A + 3 * N_HEADS_B * DH_B
ODD_IN = 2 * N_HEADS_C * DQK_C + 2 * N_HEADS_C * DV_C + S5_CH
EPS = 1e-6
NEG_INF = -1e30
LOG2E = math.log2(math.e)

LANES = 128
SUBLANES = 8
MXU_DIM = 256

TM_PROJ = 1024
TN_PROJ = 1024
TN_MOD = 1536
TM_FFN = 512
TF_FFN = MXU_DIM
BLK_A = 512
NPART_A = 2
ONES_A = 16
SCORE_PAD = 2 * LANES
BLK_B = 1024
BAND_B = LEFT_CHUNKS * CHUNK
QW_B = 4 * CHUNK
BLK_C = 512
S5_T = 16
S5_TC = LANES

assert BLK_B % BAND_B == 0 and BLK_B % QW_B == 0 and BAND_B % QW_B == 0
assert BLK_A >= MAX_DISTANCE, "far key blocks must sit in the saturated T5 bucket"
assert DV_A == LANES and 2 * DK_A == LANES and 2 * DH_B == LANES, "attention heads are read as 128-lane column blocks"


def _dot(a, b):
    return jnp.dot(a, b, preferred_element_type=F32)


def _dot_nt(a, b):
    return lax.dot_general(a, b, (((1,), (1,)), ((), ())), preferred_element_type=F32)


def _dot_tn(a, b):
    return lax.dot_general(a, b, (((0,), (0,)), ((), ())), preferred_element_type=F32)


def _const_spec(shape):
    zeros = (0,) * len(shape)
    return pl.BlockSpec(shape, lambda *_: zeros, pipeline_mode=pl.Buffered(1))


def _mod_rmsnorm(x, g, scale, shift):
    y = x * lax.rsqrt(jnp.mean(x * x, axis=-1, keepdims=True) + EPS)
    y = y * g
    return y * (1.0 + scale) + shift


def _mod_kernel(c_ref, w_ref, b_ref, o_ref):
    c = c_ref[...]
    cond = c * jax.nn.sigmoid(c)
    o_ref[0] = jnp.sum(cond * w_ref[0], axis=0, keepdims=True) + b_ref[0]


def _modulation(c, mod_w, mod_b):
    depth, d, n = mod_w.shape
    tn = TN_MOD
    return pl.pallas_call(
        _mod_kernel,
        grid=(depth, n // tn),
        in_specs=[
            pl.BlockSpec((d, 1), lambda i, j: (0, 0)),
            pl.BlockSpec((1, d, tn), lambda i, j: (i, 0, j)),
            pl.BlockSpec((1, 1, tn), lambda i, j: (i, 0, j)),
        ],
        out_specs=pl.BlockSpec((1, 1, tn), lambda i, j: (i, 0, j)),
        out_shape=jax.ShapeDtypeStruct((depth, 1, n), F32),
        name="modulation",
    )(c.reshape(d, 1), mod_w, mod_b.reshape(depth, 1, n))


def _normproj_kernel(x_ref, g_ref, sc_ref, sh_ref, w_ref, o_ref):
    tm, n = o_ref.shape
    half = tm // 2
    for r in range(2):
        rows = slice(r * half, (r + 1) * half)
        h = _mod_rmsnorm(x_ref[rows, :], g_ref[...], sc_ref[...], sh_ref[...]).astype(BF16)
        for j in range(n // TN_PROJ):
            cols = slice(j * TN_PROJ, (j + 1) * TN_PROJ)
            o_ref[rows, cols] = _dot(h, w_ref[:, cols]).astype(o_ref.dtype)


def _normproj(x, g, scale, shift, w, out_dtype):
    seq, d = x.shape
    n = w.shape[1]
    tm = TM_PROJ
    row = pl.BlockSpec((1, d), lambda i: (0, 0))
    return pl.pallas_call(
        _normproj_kernel,
        grid=(seq // tm,),
        in_specs=[pl.BlockSpec((tm, d), lambda i: (i, 0)), row, row, row, _const_spec(w.shape)],
        out_specs=pl.BlockSpec((tm, n), lambda i: (i, 0)),
        out_shape=jax.ShapeDtypeStruct((seq, n), out_dtype),
        compiler_params=pltpu.CompilerParams(dimension_semantics=("parallel",)),
        name="normproj",
    )(x, g.reshape(1, d), scale, shift, w)


def _proj_kernel(h_ref, w_ref, o_ref):
    for j in range(o_ref.shape[1] // TN_PROJ):
        cols = slice(j * TN_PROJ, (j + 1) * TN_PROJ)
        o_ref[:, cols] = _dot(h_ref[...], w_ref[:, cols]).astype(o_ref.dtype)


def _proj(h, w, out_dtype):
    seq, d = h.shape
    n = w.shape[1]
    tm = TM_PROJ
    return pl.pallas_call(
        _proj_kernel,
        grid=(seq // tm,),
        in_specs=[pl.BlockSpec((tm, d), lambda i: (i, 0)), _const_spec(w.shape)],
        out_specs=pl.BlockSpec((tm, n), lambda i: (i, 0)),
        out_shape=jax.ShapeDtypeStruct((seq, n), out_dtype),
        compiler_params=pltpu.CompilerParams(dimension_semantics=("parallel",)),
        name="proj",
    )(h, w)


def _diffattn_kernel(q_ref, k_ref, v_ref, bias_ref, lam_ref, g_ref, o_ref,
                     qs_ref, vt_ref, m_ref, acc_ref, *s_refs, out_scale):
    blk = BLK_A
    nq = 2 * blk
    sub = SUBLANES
    dv = DV_A
    npart = len(s_refs) // 4
    wq = nq // npart
    sa_ref, sb_ref = s_refs[:2 * npart], s_refs[2 * npart:]
    i = pl.program_id(1)

    @pl.when(i == 0)
    def _():
        def tr(b, carry):
            r0 = pl.multiple_of(b * blk, blk)
            vt_ref[0:dv, pl.ds(r0, blk)] = v_ref[pl.ds(r0, blk), :].astype(F32).T.astype(BF16)
            vt_ref[dv:dv + ONES_A, pl.ds(r0, blk)] = jnp.ones((ONES_A, blk), BF16)
            return carry
        lax.fori_loop(0, v_ref.shape[0] // blk, tr, 0)

    q = q_ref[...].astype(F32) * (DK_A ** -0.5 * LOG2E)
    lane = lax.broadcasted_iota(jnp.int32, q.shape, 1)
    qs_ref[:, 0:blk] = jnp.where(lane < DK_A, q, 0.0).T.astype(BF16)
    qs_ref[:, blk:nq] = jnp.where(lane >= DK_A, q, 0.0).T.astype(BF16)
    m_ref[...] = jnp.full(m_ref.shape, NEG_INF, F32)
    acc_ref[...] = jnp.zeros(acc_ref.shape, F32)

    def scores(b, s_ref):
        k = k_ref[pl.ds(pl.multiple_of(b * blk, blk), blk), :]
        for part in range(npart):
            s = _dot(k, qs_ref[:, part * wq:(part + 1) * wq])
            s_ref[part][:, 0:wq] = s
            s_ref[npart + part][...] = jnp.max(s.reshape(blk // sub, sub, wq), axis=0)

    def softmax_pv(b, s_ref, bias):
        vt = vt_ref[:, pl.ds(pl.multiple_of(b * blk, blk), blk)]
        for part in range(npart):
            cols = slice(part * wq, (part + 1) * wq)
            s = s_ref[part][:, 0:wq]
            if bias is not None:
                q0 = (part * wq) % blk
                s = s + bias[:, q0:q0 + wq]
            s = s.reshape(blk // sub, sub, wq)
            m_prev = m_ref[:, cols]
            smax = jnp.max(s, axis=0) if bias is not None else s_ref[npart + part][...]
            m_cur = jnp.max(smax, axis=0, keepdims=True)
            m_new = jnp.maximum(m_prev, m_cur)
            alpha = jnp.exp2(m_prev - m_new)
            p = jnp.exp2(s - m_new[None])
            pv = _dot(vt, p.reshape(blk, wq).astype(BF16))
            acc_ref[:, cols] = acc_ref[:, cols] * alpha[0:1] + pv
            m_ref[:, cols] = m_new

    nfar = jnp.maximum(i - 1, 0)
    odd = lax.rem(nfar, 2)

    @pl.when(i == 0)
    def _():
        scores(0, sb_ref)

    @pl.when(i > 0)
    def _():
        @pl.when(odd == 1)
        def _():
            scores(0, sb_ref)
            scores(1, sa_ref)
            softmax_pv(0, sb_ref, None)

        @pl.when(odd == 0)
        def _():
            scores(0, sa_ref)

        def pair(b):
            scores(b + 1, sb_ref)
            softmax_pv(b, sa_ref, None)
            scores(b + 2, sa_ref)
            softmax_pv(b + 1, sb_ref, None)

        def quad_body(t, carry):
            pair(odd + 4 * t)
            pair(odd + 4 * t + 2)
            return carry

        npairs = nfar // 2
        lax.fori_loop(0, npairs // 2, quad_body, 0)

        @pl.when(lax.rem(npairs, 2) == 1)
        def _():
            pair(odd + 2 * (npairs - 1))
        scores(i, sb_ref)
        softmax_pv(i - 1, sa_ref, bias_ref[0, 0])

    softmax_pv(i, sb_ref, bias_ref[0, 1])

    ot = acc_ref[0:dv, 0:nq] / acc_ref[dv:dv + 1, 0:nq]
    o = ot[:, 0:blk].T - lam_ref[...] * ot[:, blk:nq].T
    o = o * lax.rsqrt(jnp.mean(o * o, axis=-1, keepdims=True) + EPS) * g_ref[...]
    o_ref[...] = (o * out_scale).astype(o_ref.dtype)


_TOEPLITZ_ROWS = 256
_TOEPLITZ_N = 2048


def _toeplitz_kernel(v_ref, o_ref, *, keep):
    rows, cols = o_ref.shape[1:]
    x = jnp.broadcast_to(v_ref[0, 0], (rows, v_ref.shape[-1]))
    tile = pltpu.roll(x, 0, 1, stride=1, stride_axis=0)[:, :cols]
    r = lax.broadcasted_iota(jnp.int32, (rows, cols), 0) + pl.program_id(1) * rows
    c = lax.broadcasted_iota(jnp.int32, (rows, cols), 1)
    o_ref[0] = jnp.where(keep(r, c), tile, NEG_INF)


def _toeplitz_tiles(fn, keep, heads, rows, cols):
    n, rb = _TOEPLITZ_N, _TOEPLITZ_ROWS
    assert rows % rb == 0 and rows <= n // 2 and cols <= n // 2
    idx = jnp.arange(n, dtype=jnp.int32)
    vec = fn(jnp.where(idx < n // 2, idx, idx - n)).astype(F32)
    vecs = jnp.stack([jnp.roll(vec, k * rb, axis=1) for k in range(rows // rb)], axis=1)
    return pl.pallas_call(
        functools.partial(_toeplitz_kernel, keep=keep),
        grid=(heads, rows // rb),
        in_specs=[pl.BlockSpec((1, 1, 1, n), lambda h, k: (h, k, 0, 0))],
        out_specs=pl.BlockSpec((1, rb, cols), lambda h, k: (h, k, 0)),
        out_shape=jax.ShapeDtypeStruct((heads, rows, cols), F32),
        name="toeplitz_tiles",
    )(vecs.reshape(heads, rows // rb, 1, n))


def _t5_bucket(rel):
    nb = NUM_BUCKETS // 2
    max_exact = nb // 2
    bucket = jnp.where(rel > 0, nb, 0)
    n = jnp.abs(rel)
    nf = jnp.maximum(n, 1).astype(F32)
    large = max_exact + (jnp.log(nf / max_exact) / math.log(MAX_DISTANCE / max_exact)
                         * (nb - max_exact)).astype(jnp.int32)
    large = jnp.minimum(large, nb - 1)
    return bucket + jnp.where(n < max_exact, n, large)


def _diff_bias_tiles(t5_table):
    blk = BLK_A
    table = t5_table.astype(F32)
    far = table[_t5_bucket(jnp.full((), -(blk + 1), jnp.int32))]
    def visible(r, c):
        return jnp.floor_divide(r - blk, CHUNK) <= jnp.floor_divide(c, CHUNK)

    tiles = _toeplitz_tiles(lambda x: ((table[_t5_bucket(-x - blk)] - far) * LOG2E).T, visible,
                            N_HEADS_A, 2 * blk, blk)
    return tiles.reshape(N_HEADS_A, 2, blk, blk)


def _diff_attention(proj, t5_table, lam, subln_g, lam_init):
    seq = proj.shape[0]
    blk = BLK_A
    bias = _diff_bias_tiles(t5_table)
    ha = N_HEADS_A
    kern = functools.partial(_diffattn_kernel, out_scale=1.0 - lam_init)
    return pl.pallas_call(
        kern,
        grid=(ha, seq // blk),
        in_specs=[
            pl.BlockSpec((blk, DV_A), lambda h, i: (i, h)),
            pl.BlockSpec((seq, DV_A), lambda h, i: (0, ha + h)),
            pl.BlockSpec((seq, DV_A), lambda h, i: (0, 2 * ha + h)),
            pl.BlockSpec((1, 2, blk, blk), lambda h, i: (h, 0, 0, 0)),
            pl.BlockSpec((1, DV_A), lambda h, i: (0, 0)),
            pl.BlockSpec((1, DV_A), lambda h, i: (0, 0)),
        ],
        out_specs=pl.BlockSpec((blk, DV_A), lambda h, i: (i, h)),
        out_shape=jax.ShapeDtypeStruct((seq, ha * DV_A), BF16),
        scratch_shapes=[
            pltpu.VMEM((DV_A, 2 * blk), BF16),
            pltpu.VMEM((DV_A + ONES_A, seq), BF16),
            pltpu.VMEM((SUBLANES, 2 * blk), F32),
            pltpu.VMEM((DV_A + ONES_A, 2 * blk), F32),
        ] + 2 * ([pltpu.VMEM((blk, 2 * blk // NPART_A + SCORE_PAD), F32)] * NPART_A
                 + [pltpu.VMEM((SUBLANES, 2 * blk // NPART_A), F32)] * NPART_A),
        compiler_params=pltpu.CompilerParams(dimension_semantics=("parallel", "arbitrary")),
        name="diff_attention",
    )(proj, proj, proj, bias, jnp.full((1, DV_A), lam, F32), subln_g.reshape(1, DV_A).astype(F32))


def _band_kernel(q_ref, kp_ref, kc_ref, vp_ref, vc_ref, bias_ref, o_ref, *s_refs):
    qw, band = QW_B, BAND_B
    nk = band + qw
    sub = SUBLANES
    i = pl.program_id(1)
    q = q_ref[...].astype(F32) * (DH_B ** -0.5 * LOG2E)
    lane = lax.broadcasted_iota(jnp.int32, q.shape, 1)
    qh = (jnp.where(lane < DH_B, q, 0.0).T.astype(BF16), jnp.where(lane >= DH_B, q, 0.0).T.astype(BF16))
    k_all = jnp.concatenate([kp_ref[...], kc_ref[...]], axis=0)
    vt_all = jnp.concatenate([vp_ref[...], vc_ref[...]], axis=0).astype(F32).T.astype(BF16)
    bias = jnp.concatenate([bias_ref[0], bias_ref[1]], axis=1)
    no_prev = jnp.where(i == 0, NEG_INF, 0.0).astype(F32)
    krow = lax.broadcasted_iota(jnp.int32, (nk, 2 * qw), 0)
    ngroups = len(s_refs)
    for g in range(ngroups):
        k0 = g * qw
        qs = jnp.concatenate([qh[0][:, k0:k0 + qw], qh[1][:, k0:k0 + qw]], axis=1)
        s_refs[g][:, 0:2 * qw] = _dot(k_all[k0:k0 + nk], qs)
    for g in range(ngroups):
        k0 = g * qw
        s = s_refs[g][:, 0:2 * qw] + bias
        if k0 < band:
            s = s + jnp.where(krow < band - k0, no_prev, 0.0)
        s = s.reshape(nk // sub, sub, 2 * qw)
        m = jnp.max(jnp.max(s, axis=0), axis=0, keepdims=True)
        p = jnp.exp2(s - m[None])
        l = jnp.sum(jnp.sum(p, axis=0), axis=0, keepdims=True)
        ot = _dot(vt_all[:, k0:k0 + nk], p.reshape(nk, 2 * qw).astype(BF16)) / l
        o = jnp.concatenate([ot[0:DH_B, 0:qw], ot[DH_B:2 * DH_B, qw:2 * qw]], axis=0)
        o_ref[k0:k0 + qw, :] = o.T.astype(o_ref.dtype)


def _band_bias_tiles(rel_bias):
    band = BAND_B

    def valid(r, c):
        qchunk = jnp.floor_divide(c, CHUNK)
        kchunk = jnp.floor_divide(r - band, CHUNK)
        return (kchunk <= qchunk) & (kchunk >= qchunk - LEFT_CHUNKS)

    return _toeplitz_tiles(
        lambda x: rel_bias.astype(F32)[:, jnp.clip(-x - band, -REL_CLIP, REL_CLIP) + REL_CLIP] * LOG2E, valid,
        N_HEADS_B, band + QW_B, QW_B)


def _band_attention(proj, rel_bias):
    seq = proj.shape[0]
    blk, band, qw = BLK_B, BAND_B, QW_B
    bias = _band_bias_tiles(rel_bias)
    npair = N_HEADS_B // 2
    qc0 = 3 * N_HEADS_A
    per = blk // band
    prev = lambda c0: (lambda hp, i: (jnp.maximum(i * per - 1, 0), c0 + hp))
    cur = lambda c0: (lambda hp, i: (i, c0 + hp))
    return pl.pallas_call(
        _band_kernel,
        grid=(npair, seq // blk),
        in_specs=[
            pl.BlockSpec((blk, LANES), cur(qc0)),
            pl.BlockSpec((band, LANES), prev(qc0 + npair)),
            pl.BlockSpec((blk, LANES), cur(qc0 + npair)),
            pl.BlockSpec((band, LANES), prev(qc0 + 2 * npair)),
            pl.BlockSpec((blk, LANES), cur(qc0 + 2 * npair)),
            pl.BlockSpec((2, band + qw, qw), lambda hp, i: (hp, 0, 0)),
        ],
        out_specs=pl.BlockSpec((blk, LANES), lambda hp, i: (i, hp)),
        out_shape=jax.ShapeDtypeStruct((seq, N_HEADS_B * DH_B), BF16),
        scratch_shapes=[pltpu.VMEM((band + qw, 2 * qw + SCORE_PAD), F32)] * (blk // qw),
        compiler_params=pltpu.CompilerParams(dimension_semantics=("parallel", "arbitrary")),
        name="band_attention",
    )(proj, proj, proj, proj, proj, bias)


def _retention_kernel(qk_ref, v_ref, gate_ref, cos_ref, sin_ref, qdec_ref, kdec_ref, dmat_ref,
                      sdec_ref, o_ref, state_ref):
    @pl.when(pl.program_id(0) == 0)
    def _():
        state_ref[...] = jnp.zeros(state_ref.shape, F32)

    cos = cos_ref[...]
    sin = sin_ref[...]
    lane = lax.broadcasted_iota(jnp.int32, cos.shape, 1)
    first_half = (lane % DQK_C) < (DQK_C // 2)
    qk = qk_ref[...]
    parts = []
    for j in range(qk.shape[1] // LANES):
        t = qk[:, j * LANES:(j + 1) * LANES]
        partner = jnp.where(first_half, pltpu.roll(t, LANES - DQK_C // 2, 1), pltpu.roll(t, DQK_C // 2, 1))
        parts.append(t * cos + partner * sin)
    wq = N_HEADS_C * DQK_C
    q = jnp.concatenate(parts[:wq // LANES], axis=1)
    k = jnp.concatenate(parts[wq // LANES:], axis=1) * (DQK_C ** -0.5)
    qd = (q * qdec_ref[...]).astype(BF16)
    kd = (k * kdec_ref[...]).astype(BF16)
    qb = q.astype(BF16)
    kb = k.astype(BF16)
    vb = v_ref[...].astype(BF16)
    gate = gate_ref[...]
    outs = []
    for h in range(N_HEADS_C):
        qs = slice(h * DQK_C, (h + 1) * DQK_C)
        vs = slice(h * DV_C, (h + 1) * DV_C)
        scores = _dot_nt(qb[:, qs], kb[:, qs]) * dmat_ref[h]
        state = state_ref[h]
        r = _dot(scores.astype(BF16), vb[:, vs]) + _dot(qd[:, qs], state.astype(BF16))
        state_ref[h] = state * sdec_ref[h] + _dot_tn(kd[:, qs], vb[:, vs])
        r = r * lax.rsqrt(jnp.mean(r * r, axis=-1, keepdims=True) + EPS)
        g = gate[:, vs]
        outs.append(r * (g * jax.nn.sigmoid(g)))
    o_ref[...] = jnp.concatenate(outs, axis=1).astype(o_ref.dtype)


def _retention_tables(seq):
    t = BLK_C
    half = DQK_C // 2
    inv_freq = 1.0 / (ROPE_BASE ** (jnp.arange(0, DQK_C, 2, dtype=F32) / DQK_C))
    ang = jnp.arange(seq, dtype=F32)[:, None] * inv_freq[None, :]
    reps = LANES // half
    cos = jnp.tile(jnp.cos(ang), (1, reps))
    sign = jnp.where((jnp.arange(LANES) % DQK_C) < half, -1.0, 1.0).astype(F32)
    sin = jnp.tile(jnp.sin(ang), (1, reps)) * sign[None, :]
    log_g = jnp.log(1.0 - jnp.power(2.0, -5.0 - jnp.arange(N_HEADS_C, dtype=F32)))
    pos = jnp.arange(t, dtype=F32)
    diff = pos[:, None] - pos[None, :]
    same_or_past = (jnp.arange(t)[None, :] // CHUNK) <= (jnp.arange(t)[:, None] // CHUNK)
    dmat = jnp.where(same_or_past[None], jnp.exp(log_g[:, None, None] * jnp.abs(diff)[None]), 0.0)
    qdec = jnp.repeat(jnp.exp(log_g[None, :] * (pos[:, None] + 1.0)), DQK_C, axis=1)
    kdec = jnp.repeat(jnp.exp(log_g[None, :] * (t - 1.0 - pos[:, None])), DQK_C, axis=1)
    sdec = jnp.broadcast_to(jnp.exp(log_g * t)[:, None, None], (N_HEADS_C, 1, DV_C))
    return cos, sin, qdec, kdec, dmat, sdec


def _retention(proj):
    seq = proj.shape[0]
    t = BLK_C
    cos, sin, qdec, kdec, dmat, sdec = _retention_tables(seq)
    wv = N_HEADS_C * DV_C
    return pl.pallas_call(
        _retention_kernel,
        grid=(seq // t,),
        in_specs=[
            pl.BlockSpec((t, wv), lambda i: (i, 0)),
            pl.BlockSpec((t, wv), lambda i: (i, 1)),
            pl.BlockSpec((t, wv), lambda i: (i, 2)),
            pl.BlockSpec((t, LANES), lambda i: (i, 0)),
            pl.BlockSpec((t, LANES), lambda i: (i, 0)),
            pl.BlockSpec((t, N_HEADS_C * DQK_C), lambda i: (0, 0)),
            pl.BlockSpec((t, N_HEADS_C * DQK_C), lambda i: (0, 0)),
            pl.BlockSpec((N_HEADS_C, t, t), lambda i: (0, 0, 0)),
            pl.BlockSpec((N_HEADS_C, 1, DV_C), lambda i: (0, 0, 0)),
        ],
        out_specs=pl.BlockSpec((t, wv), lambda i: (i, 0)),
        out_shape=jax.ShapeDtypeStruct((seq, wv), BF16),
        scratch_shapes=[pltpu.VMEM((N_HEADS_C, DQK_C, DV_C), F32)],
        compiler_params=pltpu.CompilerParams(dimension_semantics=("arbitrary",)),
        name="retention",
    )(proj, proj, proj, cos, sin, qdec, kdec, dmat, sdec)


def _s5_kernel(*refs):
    ncb = S5_CH // LANES
    u_refs = refs[:ncb]
    (mt_ref, bt_ref, ctr_ref, cti_ref, are_ref, aim_ref, y_ref,
     ut_ref, yt_ref, ys_ref, vr_ref, vi_ref, spr_ref, spi_ref, carry_ref) = refs[ncb:]
    tc = S5_TC
    gp = S5_GROUP
    n = S5_STATE
    ng = S5_GROUPS

    @pl.when(pl.program_id(0) == 0)
    def _():
        carry_ref[...] = jnp.zeros(carry_ref.shape, F32)

    for s in range(S5_T):
        for k in range(ncb):
            ut_ref[s, k * LANES:(k + 1) * LANES, :] = u_refs[k][pl.ds(s, tc, stride=S5_T), :].T

    unroll = 4

    def intra(it, carry):
        for k in range(unroll):
            g = it * unroll + k
            r0 = pl.multiple_of(g * gp, gp)
            ug = ut_ref[:, pl.ds(r0, gp), :].reshape(S5_T * gp, tc).astype(BF16)
            yt_ref[:, pl.ds(r0, gp), :] = _dot(mt_ref[g], ug).reshape(S5_T, gp, tc)
            vt = _dot(bt_ref[g], ug)
            n0 = pl.multiple_of(g * n, n)
            vr_ref[pl.ds(n0, n), :] = vt[0:n]
            vi_ref[pl.ds(n0, n), :] = vt[n:2 * n]
        return carry

    lax.fori_loop(0, ng // unroll, intra, 0)

    sub = SUBLANES
    nv = tc // sub
    row = lax.broadcasted_iota(jnp.int32, (tc, LANES), 0)
    in_vreg = lax.rem(row, sub)

    def rows_of(v, r):
        return jnp.broadcast_to(v[r:r + 1], (tc, LANES))

    for j in range(ng * n // LANES):
        cols = slice(j * LANES, (j + 1) * LANES)
        pwr, pwi = are_ref[:, cols], aim_ref[:, cols]
        xr = vr_ref[cols, :].T
        xi = vi_ref[cols, :].T
        for d in (1, 2, 4):
            keep = in_vreg >= d
            sr = jnp.where(keep, pltpu.roll(xr, d, 0), 0.0)
            si = jnp.where(keep, pltpu.roll(xi, d, 0), 0.0)
            fr, fi = rows_of(pwr, d - 1), rows_of(pwi, d - 1)
            xr, xi = xr + (fr * sr - fi * si), xi + (fr * si + fi * sr)
        cr, ci = carry_ref[0, :, cols], carry_ref[1, :, cols]
        cr0, ci0 = cr, ci
        outr, outi = [], []
        for v in range(nv):
            yr = xr[v * sub:(v + 1) * sub] + (pwr * cr - pwi * ci)
            yi = xi[v * sub:(v + 1) * sub] + (pwr * ci + pwi * cr)
            outr.append(yr)
            outi.append(yi)
            cr = jnp.broadcast_to(yr[sub - 1:sub], (sub, LANES))
            ci = jnp.broadcast_to(yi[sub - 1:sub], (sub, LANES))
        carry_ref[0, :, cols] = cr
        carry_ref[1, :, cols] = ci
        sr = jnp.concatenate(outr, axis=0)
        si = jnp.concatenate(outi, axis=0)
        first = row == 0
        spr_ref[j] = jnp.where(first, rows_of(cr0, 0), pltpu.roll(sr, 1, 0))
        spi_ref[j] = jnp.where(first, rows_of(ci0, 0), pltpu.roll(si, 1, 0))

    def cross(it, carry):
        for k in range(unroll):
            jp = it * unroll + k
            r0 = pl.multiple_of(jp * 2 * gp, 2 * gp)
            yc = (_dot_nt(ctr_ref[jp], spr_ref[jp].astype(BF16))
                  + _dot_nt(cti_ref[jp], spi_ref[jp].astype(BF16)))
            yt_ref[:, pl.ds(r0, 2 * gp), :] += yc.reshape(S5_T, 2 * gp, tc)
        return carry

    lax.fori_loop(0, ng // 2 // unroll, cross, 0)

    for s in range(S5_T):
        for k in range(ncb):
            ys_ref[k, pl.ds(s, tc, stride=S5_T), :] = yt_ref[s, k * LANES:(k + 1) * LANES, :].T
    for k in range(ncb):
        y_ref[:, k * LANES:(k + 1) * LANES] = ys_ref[k]


def _s5_matrices(lam_re, lam_im, log_step, b_re, b_im, c_re, c_im, d_skip):
    hi = lax.Precision.HIGHEST
    t, gp, n, ng = S5_T, S5_GROUP, S5_STATE, S5_GROUPS
    lam = lax.complex(lam_re.astype(F32), lam_im.astype(F32))
    step = jnp.exp(log_step.astype(F32))[:, None]
    ls = lam * step
    a_bar = jnp.exp(ls)
    b_bar = ((a_bar - 1.0) / lam)[..., None] * lax.complex(b_re.astype(F32), b_im.astype(F32))
    cm = lax.complex(c_re.astype(F32), c_im.astype(F32))

    def apow(k):
        kk = k.astype(F32).astype(jnp.complex64)
        return jnp.exp(ls.reshape((ng,) + (1,) * k.ndim + (n,)) * kk[None, ..., None])

    tt = jnp.arange(t)
    kmat = jnp.einsum('gpn,gln,gnq->glpq', cm, apow(tt), b_bar, precision=hi).real
    krev = jnp.transpose(kmat[:, ::-1], (0, 2, 1, 3)).reshape(ng, gp, t * gp)
    kpad = jnp.pad(krev, ((0, 0), (0, 0), (0, t * gp)))
    mt = jnp.concatenate([kpad[:, :, (t - 1 - to) * gp:(2 * t - 1 - to) * gp] for to in range(t)], axis=1)
    dvec = jnp.tile(d_skip.astype(F32).reshape(ng, 1, gp), (1, t, 1)).reshape(ng, t * gp)
    mt = mt + jnp.eye(t * gp, dtype=F32)[None] * dvec[:, :, None]
    z = jnp.swapaxes(apow(t - 1 - tt), 1, 2)[:, :, :, None] * b_bar[:, :, None, :]
    z = z.reshape(ng, n, t * gp)
    bt = jnp.concatenate([z.real, z.imag], axis=1)
    w = cm[:, None, :, :] * apow(tt + 1)[:, :, None, :]

    def pair_readout(x):
        x = x.reshape(ng // 2, 2, t, gp, n)
        first = jnp.pad(x[:, 0], ((0, 0), (0, 0), (0, 0), (0, n)))
        second = jnp.pad(x[:, 1], ((0, 0), (0, 0), (0, 0), (n, 0)))
        return jnp.stack([first, second], axis=2).reshape(ng // 2, t * 2 * gp, 2 * n).astype(BF16)

    ctr, cti = pair_readout(w.real), pair_readout(-w.imag)
    a_chunk = jnp.transpose(apow(t * (jnp.arange(SUBLANES) + 1)), (1, 0, 2)).reshape(SUBLANES, ng * n)
    return mt.astype(BF16), bt.astype(BF16), ctr, cti, a_chunk.real, a_chunk.imag


def _s5(proj, mats):
    seq, width = proj.shape
    t, tc, gp, n, ng = S5_T, S5_TC, S5_GROUP, S5_STATE, S5_GROUPS
    rows = t * tc
    ncb = S5_CH // LANES
    cb0 = (width - S5_CH) // LANES
    u_specs = [pl.BlockSpec((rows, LANES), (lambda i, k=k: (i, cb0 + k))) for k in range(ncb)]
    nsb = ng * n // LANES
    return pl.pallas_call(
        _s5_kernel,
        grid=(seq // rows,),
        in_specs=u_specs + [_const_spec(m.shape) for m in mats],
        out_specs=pl.BlockSpec((rows, S5_CH), lambda i: (i, 0)),
        out_shape=jax.ShapeDtypeStruct((seq, S5_CH), F32),
        scratch_shapes=[
            pltpu.VMEM((t, S5_CH, tc), F32),
            pltpu.VMEM((t, S5_CH, tc), F32),
            pltpu.VMEM((ncb, rows, LANES), F32),
            pltpu.VMEM((ng * n, tc), F32),
            pltpu.VMEM((ng * n, tc), F32),
            pltpu.VMEM((nsb, tc, LANES), F32),
            pltpu.VMEM((nsb, tc, LANES), F32),
            pltpu.VMEM((2, SUBLANES, ng * n), F32),
        ],
        compiler_params=pltpu.CompilerParams(dimension_semantics=("arbitrary",)),
        name="s5_scan",
    )(*([proj] * ncb), *mats)


def _mix_ffn_kernel(*refs, glu, final):
    (x_ref, a_ref, b_ref, wo_ref, g1_ref), refs = refs[:5], refs[5:]
    if glu:
        gw_ref, refs = refs[0], refs[1:]
    (g_ref, sc_ref, sh_ref, gate_ref, win_ref, cw_ref, cb_ref, wout_ref), refs = refs[:8], refs[8:]
    if final:
        fg_ref, o_ref, h_ref, act_ref, gbuf_ref, carry_ref = refs
    else:
        ng_ref, nsc_ref, nsh_ref, o_ref, hn_ref, h_ref, act_ref, gbuf_ref, carry_ref = refs
    tm = x_ref.shape[0]
    halo = gbuf_ref.shape[0] - tm

    @pl.when(pl.program_id(0) == 0)
    def _():
        carry_ref[...] = jnp.zeros(carry_ref.shape, F32)

    if glu:
        y = jax.nn.gelu(b_ref[...]).astype(BF16)
        gg = _dot(y, gw_ref[...])
        half = gg.shape[1] // 2
        b = (gg[:, :half] * jax.nn.sigmoid(gg[:, half:])).astype(BF16)
    else:
        b = b_ref[...]
    cat = jnp.concatenate([a_ref[...], b], axis=1)
    x = x_ref[...] + g1_ref[...] * _dot(cat, wo_ref[...])
    h_ref[...] = _mod_rmsnorm(x, g_ref[...], sc_ref[...], sh_ref[...]).astype(BF16)
    for f in range(D_FF // TF_FFN):
        cs = slice(f * TF_FFN, (f + 1) * TF_FFN)
        gs = slice(D_FF + f * TF_FFN, D_FF + (f + 1) * TF_FFN)
        h = h_ref[...]
        val = _dot(h, win_ref[:, cs])
        gate = _dot(h, win_ref[:, gs])
        gbuf_ref[0:halo, :] = carry_ref[:, cs]
        gbuf_ref[halo:halo + tm, :] = gate
        carry_ref[:, cs] = gate[tm - halo:tm, :]
        conv = (gate * cw_ref[2:3, cs] + gbuf_ref[halo - 1:halo - 1 + tm, :] * cw_ref[1:2, cs]
                + gbuf_ref[halo - 2:halo - 2 + tm, :] * cw_ref[0:1, cs] + cb_ref[:, cs])
        act_ref[:, cs] = (jax.nn.gelu(conv) * val).astype(BF16)
    xn = x + gate_ref[...] * _dot(act_ref[...], wout_ref[...])
    if final:
        xn = xn * lax.rsqrt(jnp.mean(xn * xn, axis=-1, keepdims=True) + EPS) * fg_ref[...]
    else:
        hn_ref[...] = _mod_rmsnorm(xn, ng_ref[...], nsc_ref[...], nsh_ref[...]).astype(BF16)
    o_ref[...] = xn


def _layer_spec(shape, layer):
    idx = (layer,) + (0,) * (len(shape) - 1)
    return pl.BlockSpec((None,) + tuple(shape[1:]), lambda *_: idx, pipeline_mode=pl.Buffered(1))


def _mix_ffn(x, a, b, wo, gate1, glu_w, g, scale, shift, gate2, w_in, conv_w, conv_b, w_out, tail, layer):
    seq, d = x.shape
    final = len(tail) == 1
    tm = TM_FFN
    halo = SUBLANES
    row = pl.BlockSpec((1, d), lambda i: (0, 0))
    rows = lambda w: pl.BlockSpec((tm, w), lambda i: (i, 0))
    conv_b = conv_b.reshape(conv_b.shape[0], 1, D_FF)
    in_specs = [rows(d), rows(a.shape[1]), rows(b.shape[1]), _const_spec(wo.shape), row]
    args = [x, a, b, wo, gate1]
    if glu_w is not None:
        in_specs.append(_const_spec(glu_w.shape))
        args.append(glu_w)
    in_specs += [
        row, row, row, row,
        _layer_spec(w_in.shape, layer),
        _layer_spec(conv_w.shape, layer),
        _layer_spec(conv_b.shape, layer),
        _layer_spec(w_out.shape, layer),
    ] + [row] * len(tail)
    args += [g.reshape(1, d), scale, shift, gate2, w_in, conv_w, conv_b, w_out]
    args += [t.reshape(1, d) for t in tail]
    out_specs = [rows(d)] if final else [rows(d), rows(d)]
    out_shape = [jax.ShapeDtypeStruct((seq, d), F32)] + ([] if final else [jax.ShapeDtypeStruct((seq, d), BF16)])
    return pl.pallas_call(
        functools.partial(_mix_ffn_kernel, glu=glu_w is not None, final=final),
        grid=(seq // tm,),
        in_specs=in_specs,
        out_specs=out_specs,
        out_shape=out_shape,
        scratch_shapes=[
            pltpu.VMEM((tm, d), BF16),
            pltpu.VMEM((tm, D_FF), BF16),
            pltpu.VMEM((tm + halo, TF_FFN), F32),
            pltpu.VMEM((halo, D_FF), F32),
        ],
        compiler_params=pltpu.CompilerParams(dimension_semantics=("arbitrary",)),
        name="mix_ffn",
    )(*args)


def kernel(x, c, t5_table, mod_w, mod_b, norm1_g, norm2_g, ffn_w_in, ffn_conv_w, ffn_conv_b, ffn_w_out,
           ev_w_in, ev_w_out, diff_lambda, diff_subln_g, band_rel_bias,
           od_w_in, od_w_out, s5_lam_re, s5_lam_im, s5_log_step, s5_b_re, s5_b_im, s5_c_re, s5_c_im,
           s5_d, s5_glu_w, final_g):
    assert x.shape[0] == 1 and x.shape[2] == D_MODEL
    seq = x.shape[1]
    assert seq % TM_PROJ == 0 and seq % (S5_T * S5_TC) == 0
    d = D_MODEL
    xs = x[0]
    mod = _modulation(c, mod_w, mod_b)
    ffn_w_in_b = ffn_w_in.astype(BF16)
    ffn_w_out_b = ffn_w_out.astype(BF16)
    mods = [[mod[i, :, k * d:(k + 1) * d] for k in range(6)] for i in range(DEPTH)]
    h = None
    for i in range(DEPTH):
        sh1, sc1, g1, sh2, sc2, g2 = mods[i]
        w_in = (ev_w_in if i % 2 == 0 else od_w_in)[i // 2].astype(BF16)
        proj_dtype = BF16 if i % 2 == 0 else F32
        if h is None:
            proj = _normproj(xs, norm1_g[i], sc1, sh1, w_in, proj_dtype)
        else:
            proj = _proj(h, w_in, proj_dtype)
        if i % 2 == 0:
            e = i // 2
            lam_init = 0.8 - 0.6 * math.exp(-0.3 * i)
            lp = diff_lambda[e].astype(F32)
            lam = jnp.exp(jnp.sum(lp[0] * lp[1])) - jnp.exp(jnp.sum(lp[2] * lp[3])) + lam_init
            mix_a = _diff_attention(proj, t5_table, lam, diff_subln_g[e], lam_init)
            mix_b = _band_attention(proj, band_rel_bias[e])
            wo, glu_w = ev_w_out[e].astype(BF16), None
        else:
            o = i // 2
            mix_a = _retention(proj)
            mats = _s5_matrices(s5_lam_re[o], s5_lam_im[o], s5_log_step[o], s5_b_re[o], s5_b_im[o],
                                s5_c_re[o], s5_c_im[o], s5_d[o])
            mix_b = _s5(proj, mats)
            wo, glu_w = od_w_out[o].astype(BF16), s5_glu_w[o].astype(BF16)
        if i == DEPTH - 1:
            tail = (final_g,)
        else:
            nsh1, nsc1 = mods[i + 1][0], mods[i + 1][1]
            tail = (norm1_g[i + 1], nsc1, nsh1)
        out = _mix_ffn(xs, mix_a, mix_b, wo, g1, glu_w, norm2_g[i], sc2, sh2, g2,
                       ffn_w_in_b, ffn_conv_w, ffn_conv_b, ffn_w_out_b, tail, layer=i)
        if i == DEPTH - 1:
            xs = out[0]
        else:
            xs, h = out
    return xs[None]
```

```python
import functools
import math

import jax
import jax.numpy as jnp
from jax import lax
from jax.experimental import pallas as pl
from jax.experimental.pallas import tpu as pltpu

F32 = jnp.float32
BF16 = jnp.bfloat16

D_MODEL = 1024
DEPTH = 2
CHUNK = 64
GROUP_WIDTH = D_MODEL // 2
DK_A = 64
DV_A = 2 * DK_A
N_HEADS_A = GROUP_WIDTH // DV_A
DH_B = 64
N_HEADS_B = GROUP_WIDTH // DH_B
LEFT_CHUNKS = 8
REL_CLIP = 2 * CHUNK
NUM_BUCKETS = 32
MAX_DISTANCE = 128
DV_C = 128
DQK_C = DV_C // 2
N_HEADS_C = GROUP_WIDTH // DV_C
ROPE_BASE = 10000.0
S5_CH = GROUP_WIDTH
S5_GROUP = 16
S5_GROUPS = S5_CH // S5_GROUP
S5_STATE = 64
D_FF = ((8 * D_MODEL // 3 + 255) // 256) * 256
CONV_W = 3
EVEN_IN = 3 * N_HEADS_A * DV_A + 3 * N_HEADS_B * DH_B
ODD_IN = 2 * N_HEADS_C * DQK_C + 2 * N_HEADS_C * DV_C + S5_CH
EPS = 1e-6
NEG_INF = -1e30
LOG2E = math.log2(math.e)

LANES = 128
SUBLANES = 8
MXU_DIM = 256

TM_PROJ = 1024
TN_PROJ = 1024
TN_MOD = 1536
TM_FFN = 512
TF_FFN = MXU_DIM
BLK_A = 512
NPART_A = 2
ONES_A = 16
SCORE_PAD = LANES
BLK_B = 1024
BAND_B = LEFT_CHUNKS * CHUNK
QW_B = 4 * CHUNK
BLK_C = 512
S5_T = 16
S5_TC = LANES

assert BLK_B % BAND_B == 0 and BLK_B % QW_B == 0 and BAND_B % QW_B == 0
assert BLK_A >= MAX_DISTANCE, "far key blocks must sit in the saturated T5 bucket"
assert DV_A == LANES and 2 * DK_A == LANES and 2 * DH_B == LANES, "attention heads are read as 128-lane column blocks"


def _dot(a, b):
    return jnp.dot(a, b, preferred_element_type=F32)


def _dot_nt(a, b):
    return lax.dot_general(a, b, (((1,), (1,)), ((), ())), preferred_element_type=F32)


def _dot_tn(a, b):
    return lax.dot_general(a, b, (((0,), (0,)), ((), ())), preferred_element_type=F32)


def _const_spec(shape):
    zeros = (0,) * len(shape)
    return pl.BlockSpec(shape, lambda *_: zeros, pipeline_mode=pl.Buffered(1))


def _mod_rmsnorm(x, g, scale, shift):
    y = x * lax.rsqrt(jnp.mean(x * x, axis=-1, keepdims=True) + EPS)
    y = y * g
    return y * (1.0 + scale) + shift


def _mod_kernel(c_ref, w_ref, b_ref, o_ref):
    c = c_ref[...]
    cond = c * jax.nn.sigmoid(c)
    o_ref[0] = jnp.sum(cond * w_ref[0], axis=0, keepdims=True) + b_ref[0]


def _modulation(c, mod_w, mod_b):
    depth, d, n = mod_w.shape
    tn = TN_MOD
    return pl.pallas_call(
        _mod_kernel,
        grid=(depth, n // tn),
        in_specs=[
            pl.BlockSpec((d, 1), lambda i, j: (0, 0)),
            pl.BlockSpec((1, d, tn), lambda i, j: (i, 0, j)),
            pl.BlockSpec((1, 1, tn), lambda i, j: (i, 0, j)),
        ],
        out_specs=pl.BlockSpec((1, 1, tn), lambda i, j: (i, 0, j)),
        out_shape=jax.ShapeDtypeStruct((depth, 1, n), F32),
        name="modulation",
    )(c.reshape(d, 1), mod_w, mod_b.reshape(depth, 1, n))


def _normproj_kernel(x_ref, g_ref, sc_ref, sh_ref, w_ref, o_ref):
    tm, n = o_ref.shape
    half = tm // 2
    for r in range(2):
        rows = slice(r * half, (r + 1) * half)
        h = _mod_rmsnorm(x_ref[rows, :], g_ref[...], sc_ref[...], sh_ref[...]).astype(BF16)
        for j in range(n // TN_PROJ):
            cols = slice(j * TN_PROJ, (j + 1) * TN_PROJ)
            o_ref[rows, cols] = _dot(h, w_ref[:, cols]).astype(o_ref.dtype)


def _normproj(x, g, scale, shift, w, out_dtype):
    seq, d = x.shape
    n = w.shape[1]
    tm = TM_PROJ
    row = pl.BlockSpec((1, d), lambda i: (0, 0))
    return pl.pallas_call(
        _normproj_kernel,
        grid=(seq // tm,),
        in_specs=[pl.BlockSpec((tm, d), lambda i: (i, 0)), row, row, row, _const_spec(w.shape)],
        out_specs=pl.BlockSpec((tm, n), lambda i: (i, 0)),
        out_shape=jax.ShapeDtypeStruct((seq, n), out_dtype),
        compiler_params=pltpu.CompilerParams(dimension_semantics=("parallel",)),
        name="normproj",
    )(x, g.reshape(1, d), scale, shift, w)


def _proj_kernel(h_ref, w_ref, o_ref):
    for j in range(o_ref.shape[1] // TN_PROJ):
        cols = slice(j * TN_PROJ, (j + 1) * TN_PROJ)
        o_ref[:, cols] = _dot(h_ref[...], w_ref[:, cols]).astype(o_ref.dtype)


def _proj(h, w, out_dtype):
    seq, d = h.shape
    n = w.shape[1]
    tm = TM_PROJ
    return pl.pallas_call(
        _proj_kernel,
        grid=(seq // tm,),
        in_specs=[pl.BlockSpec((tm, d), lambda i: (i, 0)), _const_spec(w.shape)],
        out_specs=pl.BlockSpec((tm, n), lambda i: (i, 0)),
        out_shape=jax.ShapeDtypeStruct((seq, n), out_dtype),
        compiler_params=pltpu.CompilerParams(dimension_semantics=("parallel",)),
        name="proj",
    )(h, w)


def _diffattn_kernel(q_ref, k_ref, v_ref, bias_ref, lam_ref, g_ref, o_ref,
                     qs_ref, vt_ref, m_ref, acc_ref, *s_refs, out_scale):
    blk = BLK_A
    nq = 2 * blk
    sub = SUBLANES
    dv = DV_A
    npart = len(s_refs) // 4
    wq = nq // npart
    sa_ref, sb_ref = s_refs[:2 * npart], s_refs[2 * npart:]
    i = pl.program_id(1)

    @pl.when(i == 0)
    def _():
        def tr(b, carry):
            r0 = pl.multiple_of(b * blk, blk)
            vt_ref[0:dv, pl.ds(r0, blk)] = v_ref[pl.ds(r0, blk), :].astype(F32).T.astype(BF16)
            vt_ref[dv:dv + ONES_A, pl.ds(r0, blk)] = jnp.ones((ONES_A, blk), BF16)
            return carry
        lax.fori_loop(0, v_ref.shape[0] // blk, tr, 0)

    q = q_ref[...].astype(F32) * (DK_A ** -0.5 * LOG2E)
    lane = lax.broadcasted_iota(jnp.int32, q.shape, 1)
    qs_ref[:, 0:blk] = jnp.where(lane < DK_A, q, 0.0).T.astype(BF16)
    qs_ref[:, blk:nq] = jnp.where(lane >= DK_A, q, 0.0).T.astype(BF16)
    m_ref[...] = jnp.full(m_ref.shape, NEG_INF, F32)
    acc_ref[...] = jnp.zeros(acc_ref.shape, F32)

    def scores(b, s_ref):
        k = k_ref[pl.ds(pl.multiple_of(b * blk, blk), blk), :]
        for part in range(npart):
            s = _dot(k, qs_ref[:, part * wq:(part + 1) * wq])
            s_ref[part][:, 0:wq] = s
            s_ref[npart + part][...] = jnp.max(s.reshape(blk // sub, sub, wq), axis=0)

    def softmax_pv(b, s_ref, bias):
        vt = vt_ref[:, pl.ds(pl.multiple_of(b * blk, blk), blk)]
        for part in range(npart):
            cols = slice(part * wq, (part + 1) * wq)
            s = s_ref[part][:, 0:wq]
            if bias is not None:
                q0 = (part * wq) % blk
                s = s + bias[:, q0:q0 + wq]
            s = s.reshape(blk // sub, sub, wq)
            m_prev = m_ref[:, cols]
            smax = jnp.max(s, axis=0) if bias is not None else s_ref[npart + part][...]
            m_cur = jnp.max(smax, axis=0, keepdims=True)
            m_new = jnp.maximum(m_prev, m_cur)
            alpha = jnp.exp2(m_prev - m_new)
            p = jnp.exp2(s - m_new[None])
            pv = _dot(vt, p.reshape(blk, wq).astype(BF16))
            acc_ref[:, cols] = acc_ref[:, cols] * alpha[0:1] + pv
            m_ref[:, cols] = m_new

    nfar = jnp.maximum(i - 1, 0)
    odd = lax.rem(nfar, 2)

    @pl.when(i == 0)
    def _():
        scores(0, sb_ref)

    @pl.when(i > 0)
    def _():
        @pl.when(odd == 1)
        def _():
            scores(0, sb_ref)
            scores(1, sa_ref)
            softmax_pv(0, sb_ref, None)

        @pl.when(odd == 0)
        def _():
            scores(0, sa_ref)

        def pair(b):
            scores(b + 1, sb_ref)
            softmax_pv(b, sa_ref, None)
            scores(b + 2, sa_ref)
            softmax_pv(b + 1, sb_ref, None)

        def quad_body(t, carry):
            pair(odd + 4 * t)
            pair(odd + 4 * t + 2)
            return carry

        npairs = nfar // 2
        lax.fori_loop(0, npairs // 2, quad_body, 0)

        @pl.when(lax.rem(npairs, 2) == 1)
        def _():
            pair(odd + 2 * (npairs - 1))
        scores(i, sb_ref)
        softmax_pv(i - 1, sa_ref, bias_ref[0, 0])

    softmax_pv(i, sb_ref, bias_ref[0, 1])

    ot = acc_ref[0:dv, 0:nq] / acc_ref[dv:dv + 1, 0:nq]
    o = ot[:, 0:blk].T - lam_ref[...] * ot[:, blk:nq].T
    o = o * lax.rsqrt(jnp.mean(o * o, axis=-1, keepdims=True) + EPS) * g_ref[...]
    o_ref[...] = (o * out_scale).astype(o_ref.dtype)


_TOEPLITZ_ROWS = 256
_TOEPLITZ_N = 2048


def _toeplitz_kernel(v_ref, o_ref, *, keep):
    rows, cols = o_ref.shape[1:]
    x = jnp.broadcast_to(v_ref[0, 0], (rows, v_ref.shape[-1]))
    tile = pltpu.roll(x, 0, 1, stride=1, stride_axis=0)[:, :cols]
    r = lax.broadcasted_iota(jnp.int32, (rows, cols), 0) + pl.program_id(1) * rows
    c = lax.broadcasted_iota(jnp.int32, (rows, cols), 1)
    o_ref[0] = jnp.where(keep(r, c), tile, NEG_INF)


def _toeplitz_tiles(fn, keep, heads, rows, cols):
    n, rb = _TOEPLITZ_N, _TOEPLITZ_ROWS
    assert rows % rb == 0 and rows <= n // 2 and cols <= n // 2
    idx = jnp.arange(n, dtype=jnp.int32)
    vec = fn(jnp.where(idx < n // 2, idx, idx - n)).astype(F32)
    vecs = jnp.stack([jnp.roll(vec, k * rb, axis=1) for k in range(rows // rb)], axis=1)
    return pl.pallas_call(
        functools.partial(_toeplitz_kernel, keep=keep),
        grid=(heads, rows // rb),
        in_specs=[pl.BlockSpec((1, 1, 1, n), lambda h, k: (h, k, 0, 0))],
        out_specs=pl.BlockSpec((1, rb, cols), lambda h, k: (h, k, 0)),
        out_shape=jax.ShapeDtypeStruct((heads, rows, cols), F32),
        name="toeplitz_tiles",
    )(vecs.reshape(heads, rows // rb, 1, n))


def _t5_bucket(rel):
    nb = NUM_BUCKETS // 2
    max_exact = nb // 2
    bucket = jnp.where(rel > 0, nb, 0)
    n = jnp.abs(rel)
    nf = jnp.maximum(n, 1).astype(F32)
    large = max_exact + (jnp.log(nf / max_exact) / math.log(MAX_DISTANCE / max_exact)
                         * (nb - max_exact)).astype(jnp.int32)
    large = jnp.minimum(large, nb - 1)
    return bucket + jnp.where(n < max_exact, n, large)


def _diff_bias_tiles(t5_table):
    blk = BLK_A
    table = t5_table.astype(F32)
    far = table[_t5_bucket(jnp.full((), -(blk + 1), jnp.int32))]
    def visible(r, c):
        return jnp.floor_divide(r - blk, CHUNK) <= jnp.floor_divide(c, CHUNK)

    tiles = _toeplitz_tiles(lambda x: ((table[_t5_bucket(-x - blk)] - far) * LOG2E).T, visible,
                            N_HEADS_A, 2 * blk, blk)
    return tiles.reshape(N_HEADS_A, 2, blk, blk)


def _diff_attention(proj, t5_table, lam, subln_g, lam_init):
    seq = proj.shape[0]
    blk = BLK_A
    bias = _diff_bias_tiles(t5_table)
    ha = N_HEADS_A
    kern = functools.partial(_diffattn_kernel, out_scale=1.0 - lam_init)
    return pl.pallas_call(
        kern,
        grid=(ha, seq // blk),
        in_specs=[
            pl.BlockSpec((blk, DV_A), lambda h, i: (i, h)),
            pl.BlockSpec((seq, DV_A), lambda h, i: (0, ha + h)),
            pl.BlockSpec((seq, DV_A), lambda h, i: (0, 2 * ha + h)),
            pl.BlockSpec((1, 2, blk, blk), lambda h, i: (h, 0, 0, 0)),
            pl.BlockSpec((1, DV_A), lambda h, i: (0, 0)),
            pl.BlockSpec((1, DV_A), lambda h, i: (0, 0)),
        ],
        out_specs=pl.BlockSpec((blk, DV_A), lambda h, i: (i, h)),
        out_shape=jax.ShapeDtypeStruct((seq, ha * DV_A), BF16),
        scratch_shapes=[
            pltpu.VMEM((DV_A, 2 * blk), BF16),
            pltpu.VMEM((DV_A + ONES_A, seq), BF16),
            pltpu.VMEM((SUBLANES, 2 * blk), F32),
            pltpu.VMEM((DV_A + ONES_A, 2 * blk), F32),
        ] + 2 * ([pltpu.VMEM((blk, 2 * blk // NPART_A + SCORE_PAD), F32)] * NPART_A
                 + [pltpu.VMEM((SUBLANES, 2 * blk // NPART_A), F32)] * NPART_A),
        compiler_params=pltpu.CompilerParams(dimension_semantics=("parallel", "arbitrary")),
        name="diff_attention",
    )(proj, proj, proj, bias, jnp.full((1, DV_A), lam, F32), subln_g.reshape(1, DV_A).astype(F32))


def _band_kernel(q_ref, kp_ref, kc_ref, vp_ref, vc_ref, bias_ref, o_ref, *s_refs):
    qw, band = QW_B, BAND_B
    nk = band + qw
    sub = SUBLANES
    i = pl.program_id(1)
    q = q_ref[...].astype(F32) * (DH_B ** -0.5 * LOG2E)
    lane = lax.broadcasted_iota(jnp.int32, q.shape, 1)
    qh = (jnp.where(lane < DH_B, q, 0.0).T.astype(BF16), jnp.where(lane >= DH_B, q, 0.0).T.astype(BF16))
    k_all = jnp.concatenate([kp_ref[...], kc_ref[...]], axis=0)
    vt_all = jnp.concatenate([vp_ref[...], vc_ref[...]], axis=0).astype(F32).T.astype(BF16)
    bias = jnp.concatenate([bias_ref[0], bias_ref[1]], axis=1)
    no_prev = jnp.where(i == 0, NEG_INF, 0.0).astype(F32)
    krow = lax.broadcasted_iota(jnp.int32, (nk, 2 * qw), 0)
    ngroups = len(s_refs)
    for g in range(ngroups):
        k0 = g * qw
        qs = jnp.concatenate([qh[0][:, k0:k0 + qw], qh[1][:, k0:k0 + qw]], axis=1)
        s_refs[g][:, 0:2 * qw] = _dot(k_all[k0:k0 + nk], qs)
    for g in range(ngroups):
        k0 = g * qw
        s = s_refs[g][:, 0:2 * qw] + bias
        if k0 < band:
            s = s + jnp.where(krow < band - k0, no_prev, 0.0)
        s = s.reshape(nk // sub, sub, 2 * qw)
        m = jnp.max(jnp.max(s, axis=0), axis=0, keepdims=True)
        p = jnp.exp2(s - m[None])
        l = jnp.sum(jnp.sum(p, axis=0), axis=0, keepdims=True)
        ot = _dot(vt_all[:, k0:k0 + nk], p.reshape(nk, 2 * qw).astype(BF16)) / l
        o = jnp.concatenate([ot[0:DH_B, 0:qw], ot[DH_B:2 * DH_B, qw:2 * qw]], axis=0)
        o_ref[k0:k0 + qw, :] = o.T.astype(o_ref.dtype)


def _band_bias_tiles(rel_bias):
    band = BAND_B

    def valid(r, c):
        qchunk = jnp.floor_divide(c, CHUNK)
        kchunk = jnp.floor_divide(r - band, CHUNK)
        return (kchunk <= qchunk) & (kchunk >= qchunk - LEFT_CHUNKS)

    return _toeplitz_tiles(
        lambda x: rel_bias.astype(F32)[:, jnp.clip(-x - band, -REL_CLIP, REL_CLIP) + REL_CLIP] * LOG2E, valid,
        N_HEADS_B, band + QW_B, QW_B)


def _band_attention(proj, rel_bias):
    seq = proj.shape[0]
    blk, band, qw = BLK_B, BAND_B, QW_B
    bias = _band_bias_tiles(rel_bias)
    npair = N_HEADS_B // 2
    qc0 = 3 * N_HEADS_A
    per = blk // band
    prev = lambda c0: (lambda hp, i: (jnp.maximum(i * per - 1, 0), c0 + hp))
    cur = lambda c0: (lambda hp, i: (i, c0 + hp))
    return pl.pallas_call(
        _band_kernel,
        grid=(npair, seq // blk),
        in_specs=[
            pl.BlockSpec((blk, LANES), cur(qc0)),
            pl.BlockSpec((band, LANES), prev(qc0 + npair)),
            pl.BlockSpec((blk, LANES), cur(qc0 + npair)),
            pl.BlockSpec((band, LANES), prev(qc0 + 2 * npair)),
            pl.BlockSpec((blk, LANES), cur(qc0 + 2 * npair)),
            pl.BlockSpec((2, band + qw, qw), lambda hp, i: (hp, 0, 0)),
        ],
        out_specs=pl.BlockSpec((blk, LANES), lambda hp, i: (i, hp)),
        out_shape=jax.ShapeDtypeStruct((seq, N_HEADS_B * DH_B), BF16),
        scratch_shapes=[pltpu.VMEM((band + qw, 2 * qw + SCORE_PAD), F32)] * (blk // qw),
        compiler_params=pltpu.CompilerParams(dimension_semantics=("parallel", "arbitrary")),
        name="band_attention",
    )(proj, proj, proj, proj, proj, bias)


def _retention_kernel(qk_ref, v_ref, gate_ref, cos_ref, sin_ref, qdec_ref, kdec_ref, dmat_ref,
                      sdec_ref, o_ref, state_ref):
    @pl.when(pl.program_id(0) == 0)
    def _():
        state_ref[...] = jnp.zeros(state_ref.shape, F32)

    cos = cos_ref[...]
    sin = sin_ref[...]
    lane = lax.broadcasted_iota(jnp.int32, cos.shape, 1)
    first_half = (lane % DQK_C) < (DQK_C // 2)
    qk = qk_ref[...]
    parts = []
    for j in range(qk.shape[1] // LANES):
        t = qk[:, j * LANES:(j + 1) * LANES]
        partner = jnp.where(first_half, pltpu.roll(t, LANES - DQK_C // 2, 1), pltpu.roll(t, DQK_C // 2, 1))
        parts.append(t * cos + partner * sin)
    wq = N_HEADS_C * DQK_C
    q = jnp.concatenate(parts[:wq // LANES], axis=1)
    k = jnp.concatenate(parts[wq // LANES:], axis=1) * (DQK_C ** -0.5)
    qd = (q * qdec_ref[...]).astype(BF16)
    kd = (k * kdec_ref[...]).astype(BF16)
    qb = q.astype(BF16)
    kb = k.astype(BF16)
    vb = v_ref[...].astype(BF16)
    gate = gate_ref[...]
    outs = []
    for h in range(N_HEADS_C):
        qs = slice(h * DQK_C, (h + 1) * DQK_C)
        vs = slice(h * DV_C, (h + 1) * DV_C)
        scores = _dot_nt(qb[:, qs], kb[:, qs]) * dmat_ref[h]
        state = state_ref[h]
        r = _dot(scores.astype(BF16), vb[:, vs]) + _dot(qd[:, qs], state.astype(BF16))
        state_ref[h] = state * sdec_ref[h] + _dot_tn(kd[:, qs], vb[:, vs])
        r = r * lax.rsqrt(jnp.mean(r * r, axis=-1, keepdims=True) + EPS)
        g = gate[:, vs]
        outs.append(r * (g * jax.nn.sigmoid(g)))
    o_ref[...] = jnp.concatenate(outs, axis=1).astype(o_ref.dtype)


def _retention_tables(seq):
    t = BLK_C
    half = DQK_C // 2
    inv_freq = 1.0 / (ROPE_BASE ** (jnp.arange(0, DQK_C, 2, dtype=F32) / DQK_C))
    ang = jnp.arange(seq, dtype=F32)[:, None] * inv_freq[None, :]
    reps = LANES // half
    cos = jnp.tile(jnp.cos(ang), (1, reps))
    sign = jnp.where((jnp.arange(LANES) % DQK_C) < half, -1.0, 1.0).astype(F32)
    sin = jnp.tile(jnp.sin(ang), (1, reps)) * sign[None, :]
    log_g = jnp.log(1.0 - jnp.power(2.0, -5.0 - jnp.arange(N_HEADS_C, dtype=F32)))
    pos = jnp.arange(t, dtype=F32)
    diff = pos[:, None] - pos[None, :]
    same_or_past = (jnp.arange(t)[None, :] // CHUNK) <= (jnp.arange(t)[:, None] // CHUNK)
    dmat = jnp.where(same_or_past[None], jnp.exp(log_g[:, None, None] * jnp.abs(diff)[None]), 0.0)
    qdec = jnp.repeat(jnp.exp(log_g[None, :] * (pos[:, None] + 1.0)), DQK_C, axis=1)
    kdec = jnp.repeat(jnp.exp(log_g[None, :] * (t - 1.0 - pos[:, None])), DQK_C, axis=1)
    sdec = jnp.broadcast_to(jnp.exp(log_g * t)[:, None, None], (N_HEADS_C, 1, DV_C))
    return cos, sin, qdec, kdec, dmat, sdec


def _retention(proj):
    seq = proj.shape[0]
    t = BLK_C
    cos, sin, qdec, kdec, dmat, sdec = _retention_tables(seq)
    wv = N_HEADS_C * DV_C
    return pl.pallas_call(
        _retention_kernel,
        grid=(seq // t,),
        in_specs=[
            pl.BlockSpec((t, wv), lambda i: (i, 0)),
            pl.BlockSpec((t, wv), lambda i: (i, 1)),
            pl.BlockSpec((t, wv), lambda i: (i, 2)),
            pl.BlockSpec((t, LANES), lambda i: (i, 0)),
            pl.BlockSpec((t, LANES), lambda i: (i, 0)),
            pl.BlockSpec((t, N_HEADS_C * DQK_C), lambda i: (0, 0)),
            pl.BlockSpec((t, N_HEADS_C * DQK_C), lambda i: (0, 0)),
            pl.BlockSpec((N_HEADS_C, t, t), lambda i: (0, 0, 0)),
            pl.BlockSpec((N_HEADS_C, 1, DV_C), lambda i: (0, 0, 0)),
        ],
        out_specs=pl.BlockSpec((t, wv), lambda i: (i, 0)),
        out_shape=jax.ShapeDtypeStruct((seq, wv), BF16),
        scratch_shapes=[pltpu.VMEM((N_HEADS_C, DQK_C, DV_C), F32)],
        compiler_params=pltpu.CompilerParams(dimension_semantics=("arbitrary",)),
        name="retention",
    )(proj, proj, proj, cos, sin, qdec, kdec, dmat, sdec)


def _s5_kernel(*refs):
    ncb = S5_CH // LANES
    u_refs = refs[:ncb]
    (mt_ref, bt_ref, ctr_ref, cti_ref, are_ref, aim_ref, y_ref,
     ut_ref, yt_ref, ys_ref, vr_ref, vi_ref, spr_ref, spi_ref, carry_ref) = refs[ncb:]
    tc = S5_TC
    gp = S5_GROUP
    n = S5_STATE
    ng = S5_GROUPS

    @pl.when(pl.program_id(0) == 0)
    def _():
        carry_ref[...] = jnp.zeros(carry_ref.shape, F32)

    for s in range(S5_T):
        for k in range(ncb):
            ut_ref[s, k * LANES:(k + 1) * LANES, :] = u_refs[k][pl.ds(s, tc, stride=S5_T), :].T

    unroll = 4

    def intra(it, carry):
        for k in range(unroll):
            g = it * unroll + k
            r0 = pl.multiple_of(g * gp, gp)
            ug = ut_ref[:, pl.ds(r0, gp), :].reshape(S5_T * gp, tc).astype(BF16)
            yt_ref[:, pl.ds(r0, gp), :] = _dot(mt_ref[g], ug).reshape(S5_T, gp, tc)
            vt = _dot(bt_ref[g], ug)
            n0 = pl.multiple_of(g * n, n)
            vr_ref[pl.ds(n0, n), :] = vt[0:n]
            vi_ref[pl.ds(n0, n), :] = vt[n:2 * n]
        return carry

    lax.fori_loop(0, ng // unroll, intra, 0)

    sub = SUBLANES
    nv = tc // sub
    row = lax.broadcasted_iota(jnp.int32, (tc, LANES), 0)
    in_vreg = lax.rem(row, sub)

    def rows_of(v, r):
        return jnp.broadcast_to(v[r:r + 1], (tc, LANES))

    for j in range(ng * n // LANES):
        cols = slice(j * LANES, (j + 1) * LANES)
        pwr, pwi = are_ref[:, cols], aim_ref[:, cols]
        xr = vr_ref[cols, :].T
        xi = vi_ref[cols, :].T
        for d in (1, 2, 4):
            keep = in_vreg >= d
            sr = jnp.where(keep, pltpu.roll(xr, d, 0), 0.0)
            si = jnp.where(keep, pltpu.roll(xi, d, 0), 0.0)
            fr, fi = rows_of(pwr, d - 1), rows_of(pwi, d - 1)
            xr, xi = xr + (fr * sr - fi * si), xi + (fr * si + fi * sr)
        cr, ci = carry_ref[0, :, cols], carry_ref[1, :, cols]
        cr0, ci0 = cr, ci
        outr, outi = [], []
        for v in range(nv):
            yr = xr[v * sub:(v + 1) * sub] + (pwr * cr - pwi * ci)
            yi = xi[v * sub:(v + 1) * sub] + (pwr * ci + pwi * cr)
            outr.append(yr)
            outi.append(yi)
            cr = jnp.broadcast_to(yr[sub - 1:sub], (sub, LANES))
            ci = jnp.broadcast_to(yi[sub - 1:sub], (sub, LANES))
        carry_ref[0, :, cols] = cr
        carry_ref[1, :, cols] = ci
        sr = jnp.concatenate(outr, axis=0)
        si = jnp.concatenate(outi, axis=0)
        first = row == 0
        spr_ref[j] = jnp.where(first, rows_of(cr0, 0), pltpu.roll(sr, 1, 0))
        spi_ref[j] = jnp.where(first, rows_of(ci0, 0), pltpu.roll(si, 1, 0))

    def cross(it, carry):
        for k in range(unroll):
            jp = it * unroll + k
            r0 = pl.multiple_of(jp * 2 * gp, 2 * gp)
            yc = (_dot_nt(ctr_ref[jp], spr_ref[jp].astype(BF16))
                  + _dot_nt(cti_ref[jp], spi_ref[jp].astype(BF16)))
            yt_ref[:, pl.ds(r0, 2 * gp), :] += yc.reshape(S5_T, 2 * gp, tc)
        return carry

    lax.fori_loop(0, ng // 2 // unroll, cross, 0)

    for s in range(S5_T):
        for k in range(ncb):
            ys_ref[k, pl.ds(s, tc, stride=S5_T), :] = yt_ref[s, k * LANES:(k + 1) * LANES, :].T
    for k in range(ncb):
        y_ref[:, k * LANES:(k + 1) * LANES] = ys_ref[k]


def _s5_matrices(lam_re, lam_im, log_step, b_re, b_im, c_re, c_im, d_skip):
    hi = lax.Precision.HIGHEST
    t, gp, n, ng = S5_T, S5_GROUP, S5_STATE, S5_GROUPS
    lam = lax.complex(lam_re.astype(F32), lam_im.astype(F32))
    step = jnp.exp(log_step.astype(F32))[:, None]
    ls = lam * step
    a_bar = jnp.exp(ls)
    b_bar = ((a_bar - 1.0) / lam)[..., None] * lax.complex(b_re.astype(F32), b_im.astype(F32))
    cm = lax.complex(c_re.astype(F32), c_im.astype(F32))

    def apow(k):
        kk = k.astype(F32).astype(jnp.complex64)
        return jnp.exp(ls.reshape((ng,) + (1,) * k.ndim + (n,)) * kk[None, ..., None])

    tt = jnp.arange(t)
    kmat = jnp.einsum('gpn,gln,gnq->glpq', cm, apow(tt), b_bar, precision=hi).real
    krev = jnp.transpose(kmat[:, ::-1], (0, 2, 1, 3)).reshape(ng, gp, t * gp)
    kpad = jnp.pad(krev, ((0, 0), (0, 0), (0, t * gp)))
    mt = jnp.concatenate([kpad[:, :, (t - 1 - to) * gp:(2 * t - 1 - to) * gp] for to in range(t)], axis=1)
    dvec = jnp.tile(d_skip.astype(F32).reshape(ng, 1, gp), (1, t, 1)).reshape(ng, t * gp)
    mt = mt + jnp.eye(t * gp, dtype=F32)[None] * dvec[:, :, None]
    z = jnp.swapaxes(apow(t - 1 - tt), 1, 2)[:, :, :, None] * b_bar[:, :, None, :]
    z = z.reshape(ng, n, t * gp)
    bt = jnp.concatenate([z.real, z.imag], axis=1)
    w = cm[:, None, :, :] * apow(tt + 1)[:, :, None, :]

    def pair_readout(x):
        x = x.reshape(ng // 2, 2, t, gp, n)
        first = jnp.pad(x[:, 0], ((0, 0), (0, 0), (0, 0), (0, n)))
        second = jnp.pad(x[:, 1], ((0, 0), (0, 0), (0, 0), (n, 0)))
        return jnp.stack([first, second], axis=2).reshape(ng // 2, t * 2 * gp, 2 * n).astype(BF16)

    ctr, cti = pair_readout(w.real), pair_readout(-w.imag)
    a_chunk = jnp.transpose(apow(t * (jnp.arange(SUBLANES) + 1)), (1, 0, 2)).reshape(SUBLANES, ng * n)
    return mt.astype(BF16), bt.astype(BF16), ctr, cti, a_chunk.real, a_chunk.imag


def _s5(proj, mats):
    seq, width = proj.shape
    t, tc, gp, n, ng = S5_T, S5_TC, S5_GROUP, S5_STATE, S5_GROUPS
    rows = t * tc
    ncb = S5_CH // LANES
    cb0 = (width - S5_CH) // LANES
    u_specs = [pl.BlockSpec((rows, LANES), (lambda i, k=k: (i, cb0 + k))) for k in range(ncb)]
    nsb = ng * n // LANES
    return pl.pallas_call(
        _s5_kernel,
        grid=(seq // rows,),
        in_specs=u_specs + [_const_spec(m.shape) for m in mats],
        out_specs=pl.BlockSpec((rows, S5_CH), lambda i: (i, 0)),
        out_shape=jax.ShapeDtypeStruct((seq, S5_CH), F32),
        scratch_shapes=[
            pltpu.VMEM((t, S5_CH, tc), F32),
            pltpu.VMEM((t, S5_CH, tc), F32),
            pltpu.VMEM((ncb, rows, LANES), F32),
            pltpu.VMEM((ng * n, tc), F32),
            pltpu.VMEM((ng * n, tc), F32),
            pltpu.VMEM((nsb, tc, LANES), F32),
            pltpu.VMEM((nsb, tc, LANES), F32),
            pltpu.VMEM((2, SUBLANES, ng * n), F32),
        ],
        compiler_params=pltpu.CompilerParams(dimension_semantics=("arbitrary",)),
        name="s5_scan",
    )(*([proj] * ncb), *mats)


def _mix_ffn_kernel(*refs, glu, final):
    (x_ref, a_ref, b_ref, wo_ref, g1_ref), refs = refs[:5], refs[5:]
    if glu:
        gw_ref, refs = refs[0], refs[1:]
    (g_ref, sc_ref, sh_ref, gate_ref, win_ref, cw_ref, cb_ref, wout_ref), refs = refs[:8], refs[8:]
    if final:
        fg_ref, o_ref, h_ref, act_ref, gbuf_ref, carry_ref = refs
    else:
        ng_ref, nsc_ref, nsh_ref, o_ref, hn_ref, h_ref, act_ref, gbuf_ref, carry_ref = refs
    tm = x_ref.shape[0]
    halo = gbuf_ref.shape[0] - tm

    @pl.when(pl.program_id(0) == 0)
    def _():
        carry_ref[...] = jnp.zeros(carry_ref.shape, F32)

    if glu:
        y = jax.nn.gelu(b_ref[...]).astype(BF16)
        gg = _dot(y, gw_ref[...])
        half = gg.shape[1] // 2
        b = (gg[:, :half] * jax.nn.sigmoid(gg[:, half:])).astype(BF16)
    else:
        b = b_ref[...]
    cat = jnp.concatenate([a_ref[...], b], axis=1)
    x = x_ref[...] + g1_ref[...] * _dot(cat, wo_ref[...])
    h_ref[...] = _mod_rmsnorm(x, g_ref[...], sc_ref[...], sh_ref[...]).astype(BF16)
    for f in range(D_FF // TF_FFN):
        cs = slice(f * TF_FFN, (f + 1) * TF_FFN)
        gs = slice(D_FF + f * TF_FFN, D_FF + (f + 1) * TF_FFN)
        h = h_ref[...]
        val = _dot(h, win_ref[:, cs])
        gate = _dot(h, win_ref[:, gs])
        gbuf_ref[0:halo, :] = carry_ref[:, cs]
        gbuf_ref[halo:halo + tm, :] = gate
        carry_ref[:, cs] = gate[tm - halo:tm, :]
        conv = (gate * cw_ref[2:3, cs] + gbuf_ref[halo - 1:halo - 1 + tm, :] * cw_ref[1:2, cs]
                + gbuf_ref[halo - 2:halo - 2 + tm, :] * cw_ref[0:1, cs] + cb_ref[:, cs])
        act_ref[:, cs] = (jax.nn.gelu(conv) * val).astype(BF16)
    xn = x + gate_ref[...] * _dot(act_ref[...], wout_ref[:, 0:x.shape[1]])
    if final:
        xn = xn * lax.rsqrt(jnp.mean(xn * xn, axis=-1, keepdims=True) + EPS) * fg_ref[...]
    else:
        hn_ref[...] = _mod_rmsnorm(xn, ng_ref[...], nsc_ref[...], nsh_ref[...]).astype(BF16)
    o_ref[...] = xn


def _layer_spec(shape, layer):
    idx = (layer,) + (0,) * (len(shape) - 1)
    return pl.BlockSpec((None,) + tuple(shape[1:]), lambda *_: idx, pipeline_mode=pl.Buffered(1))


def _mix_ffn(x, a, b, wo, gate1, glu_w, g, scale, shift, gate2, w_in, conv_w, conv_b, w_out, tail, layer):
    seq, d = x.shape
    final = len(tail) == 1
    tm = TM_FFN
    halo = SUBLANES
    row = pl.BlockSpec((1, d), lambda i: (0, 0))
    rows = lambda w: pl.BlockSpec((tm, w), lambda i: (i, 0))
    conv_b = conv_b.reshape(conv_b.shape[0], 1, D_FF)
    in_specs = [rows(d), rows(a.shape[1]), rows(b.shape[1]), _const_spec(wo.shape), row]
    args = [x, a, b, wo, gate1]
    if glu_w is not None:
        in_specs.append(_const_spec(glu_w.shape))
        args.append(glu_w)
    in_specs += [
        row, row, row, row,
        _layer_spec(w_in.shape, layer),
        _layer_spec(conv_w.shape, layer),
        _layer_spec(conv_b.shape, layer),
        _layer_spec(w_out.shape, layer),
    ] + [row] * len(tail)
    args += [g.reshape(1, d), scale, shift, gate2, w_in, conv_w, conv_b, w_out]
    args += [t.reshape(1, d) for t in tail]
    out_specs = [rows(d)] if final else [rows(d), rows(d)]
    out_shape = [jax.ShapeDtypeStruct((seq, d), F32)] + ([] if final else [jax.ShapeDtypeStruct((seq, d), BF16)])
    return pl.pallas_call(
        functools.partial(_mix_ffn_kernel, glu=glu_w is not None, final=final),
        grid=(seq // tm,),
        in_specs=in_specs,
        out_specs=out_specs,
        out_shape=out_shape,
        scratch_shapes=[
            pltpu.VMEM((tm, d), BF16),
            pltpu.VMEM((tm, D_FF), BF16),
            pltpu.VMEM((tm + halo, TF_FFN), F32),
            pltpu.VMEM((halo, D_FF), F32),
        ],
        compiler_params=pltpu.CompilerParams(dimension_semantics=("arbitrary",)),
        name="mix_ffn",
    )(*args)


def kernel(x, c, t5_table, mod_w, mod_b, norm1_g, norm2_g, ffn_w_in, ffn_conv_w, ffn_conv_b, ffn_w_out,
           ev_w_in, ev_w_out, diff_lambda, diff_subln_g, band_rel_bias,
           od_w_in, od_w_out, s5_lam_re, s5_lam_im, s5_log_step, s5_b_re, s5_b_im, s5_c_re, s5_c_im,
           s5_d, s5_glu_w, final_g):
    assert x.shape[0] == 1 and x.shape[2] == D_MODEL
    seq = x.shape[1]
    assert seq % TM_PROJ == 0 and seq % (S5_T * S5_TC) == 0
    d = D_MODEL
    xs = x[0]
    mod = _modulation(c, mod_w, mod_b)
    ffn_w_in_b = jnp.pad(ffn_w_in.astype(BF16), ((0, 0), (0, 0), (0, LANES)))
    ffn_w_out_b = jnp.pad(ffn_w_out.astype(BF16), ((0, 0), (0, 0), (0, LANES)))
    mods = [[mod[i, :, k * d:(k + 1) * d] for k in range(6)] for i in range(DEPTH)]
    h = None
    for i in range(DEPTH):
        sh1, sc1, g1, sh2, sc2, g2 = mods[i]
        w_in = (ev_w_in if i % 2 == 0 else od_w_in)[i // 2].astype(BF16)
        proj_dtype = BF16 if i % 2 == 0 else F32
        if h is None:
            proj = _normproj(xs, norm1_g[i], sc1, sh1, w_in, proj_dtype)
        else:
            proj = _proj(h, w_in, proj_dtype)
        if i % 2 == 0:
            e = i // 2
            lam_init = 0.8 - 0.6 * math.exp(-0.3 * i)
            lp = diff_lambda[e].astype(F32)
            lam = jnp.exp(jnp.sum(lp[0] * lp[1])) - jnp.exp(jnp.sum(lp[2] * lp[3])) + lam_init
            mix_a = _diff_attention(proj, t5_table, lam, diff_subln_g[e], lam_init)
            mix_b = _band_attention(proj, band_rel_bias[e])
            wo, glu_w = ev_w_out[e].astype(BF16), None
        else:
            o = i // 2
            mix_a = _retention(proj)
            mats = _s5_matrices(s5_lam_re[o], s5_lam_im[o], s5_log_step[o], s5_b_re[o], s5_b_im[o],
                                s5_c_re[o], s5_c_im[o], s5_d[o])
            mix_b = _s5(proj, mats)
            wo, glu_w = od_w_out[o].astype(BF16), s5_glu_w[o].astype(BF16)
        if i == DEPTH - 1:
            tail = (final_g,)
        else:
            nsh1, nsc1 = mods[i + 1][0], mods[i + 1][1]
            tail = (norm1_g[i + 1], nsc1, nsh1)
        out = _mix_ffn(xs, mix_a, mix_b, wo, g1, glu_w, norm2_g[i], sc2, sh2, g2,
                       ffn_w_in_b, ffn_conv_w, ffn_conv_b, ffn_w_out_b, tail, layer=i)
        if i == DEPTH - 1:
            xs = out[0]
        else:
            xs, h = out
    return xs[None]
```

```python
import functools
import math

import jax
import jax.numpy as jnp
from jax import lax
from jax.experimental import pallas as pl
from jax.experimental.pallas import tpu as pltpu

F32 = jnp.float32
BF16 = jnp.bfloat16

D_MODEL = 1024
DEPTH = 2
CHUNK = 64
GROUP_WIDTH = D_MODEL // 2
DK_A = 64
DV_A = 2 * DK_A
N_HEADS_A = GROUP_WIDTH // DV_A
DH_B = 64
N_HEADS_B = GROUP_WIDTH // DH_B
LEFT_CHUNKS = 8
REL_CLIP = 2 * CHUNK
NUM_BUCKETS = 32
MAX_DISTANCE = 128
DV_C = 128
DQK_C = DV_C // 2
N_HEADS_C = GROUP_WIDTH // DV_C
ROPE_BASE = 10000.0
S5_CH = GROUP_WIDTH
S5_GROUP = 16
S5_GROUPS = S5_CH // S5_GROUP
S5_STATE = 64
D_FF = ((8 * D_MODEL // 3 + 255) // 256) * 256
CONV_W = 3
EVEN_IN = 3 * N_HEADS_A * DV_A + 3 * N_HEADS_B * DH_B
ODD_IN = 2 * N_HEADS_C * DQK_C + 2 * N_HEADS_C * DV_C + S5_CH
EPS = 1e-6
NEG_INF = -1e30
LOG2E = math.log2(math.e)

LANES = 128
SUBLANES = 8
MXU_DIM = 256

TM_PROJ = 1024
TN_PROJ = 1024
TN_MOD = 1536
TM_FFN = 512
TF_FFN = MXU_DIM
BLK_A = 512
NPART_A = 4
ONES_A = 16
SCORE_PAD = LANES
BLK_B = 1024
BAND_B = LEFT_CHUNKS * CHUNK
QW_B = 4 * CHUNK
BLK_C = 512
S5_T = 16
S5_TC = LANES

assert BLK_B % BAND_B == 0 and BLK_B % QW_B == 0 and BAND_B % QW_B == 0
assert BLK_A >= MAX_DISTANCE, "far key blocks must sit in the saturated T5 bucket"
assert DV_A == LANES and 2 * DK_A == LANES and 2 * DH_B == LANES, "attention heads are read as 128-lane column blocks"


def _dot(a, b):
    return jnp.dot(a, b, preferred_element_type=F32)


def _dot_nt(a, b):
    return lax.dot_general(a, b, (((1,), (1,)), ((), ())), preferred_element_type=F32)


def _dot_tn(a, b):
    return lax.dot_general(a, b, (((0,), (0,)), ((), ())), preferred_element_type=F32)


def _const_spec(shape):
    zeros = (0,) * len(shape)
    return pl.BlockSpec(shape, lambda *_: zeros, pipeline_mode=pl.Buffered(1))


def _mod_rmsnorm(x, g, scale, shift):
    y = x * lax.rsqrt(jnp.mean(x * x, axis=-1, keepdims=True) + EPS)
    y = y * g
    return y * (1.0 + scale) + shift


def _mod_kernel(c_ref, w_ref, b_ref, o_ref):
    c = c_ref[...]
    cond = c * jax.nn.sigmoid(c)
    o_ref[0] = jnp.sum(cond * w_ref[0], axis=0, keepdims=True) + b_ref[0]


def _modulation(c, mod_w, mod_b):
    depth, d, n = mod_w.shape
    tn = TN_MOD
    return pl.pallas_call(
        _mod_kernel,
        grid=(depth, n // tn),
        in_specs=[
            pl.BlockSpec((d, 1), lambda i, j: (0, 0)),
            pl.BlockSpec((1, d, tn), lambda i, j: (i, 0, j)),
            pl.BlockSpec((1, 1, tn), lambda i, j: (i, 0, j)),
        ],
        out_specs=pl.BlockSpec((1, 1, tn), lambda i, j: (i, 0, j)),
        out_shape=jax.ShapeDtypeStruct((depth, 1, n), F32),
        name="modulation",
    )(c.reshape(d, 1), mod_w, mod_b.reshape(depth, 1, n))


def _normproj_kernel(x_ref, g_ref, sc_ref, sh_ref, w_ref, o_ref):
    tm, n = o_ref.shape
    half = tm // 2
    for r in range(2):
        rows = slice(r * half, (r + 1) * half)
        h = _mod_rmsnorm(x_ref[rows, :], g_ref[...], sc_ref[...], sh_ref[...]).astype(BF16)
        for j in range(n // TN_PROJ):
            cols = slice(j * TN_PROJ, (j + 1) * TN_PROJ)
            o_ref[rows, cols] = _dot(h, w_ref[:, cols]).astype(o_ref.dtype)


def _normproj(x, g, scale, shift, w, out_dtype):
    seq, d = x.shape
    n = w.shape[1]
    tm = TM_PROJ
    row = pl.BlockSpec((1, d), lambda i: (0, 0))
    return pl.pallas_call(
        _normproj_kernel,
        grid=(seq // tm,),
        in_specs=[pl.BlockSpec((tm, d), lambda i: (i, 0)), row, row, row, _const_spec(w.shape)],
        out_specs=pl.BlockSpec((tm, n), lambda i: (i, 0)),
        out_shape=jax.ShapeDtypeStruct((seq, n), out_dtype),
        compiler_params=pltpu.CompilerParams(dimension_semantics=("parallel",)),
        name="normproj",
    )(x, g.reshape(1, d), scale, shift, w)


def _proj_kernel(h_ref, w_ref, o_ref):
    for j in range(o_ref.shape[1] // TN_PROJ):
        cols = slice(j * TN_PROJ, (j + 1) * TN_PROJ)
        o_ref[:, cols] = _dot(h_ref[...], w_ref[:, cols]).astype(o_ref.dtype)


def _proj(h, w, out_dtype):
    seq, d = h.shape
    n = w.shape[1]
    tm = TM_PROJ
    return pl.pallas_call(
        _proj_kernel,
        grid=(seq // tm,),
        in_specs=[pl.BlockSpec((tm, d), lambda i: (i, 0)), _const_spec(w.shape)],
        out_specs=pl.BlockSpec((tm, n), lambda i: (i, 0)),
        out_shape=jax.ShapeDtypeStruct((seq, n), out_dtype),
        compiler_params=pltpu.CompilerParams(dimension_semantics=("parallel",)),
        name="proj",
    )(h, w)


def _diffattn_kernel(q_ref, k_ref, v_ref, bias_ref, lam_ref, g_ref, o_ref,
                     qs_ref, vt_ref, m_ref, acc_ref, *s_refs, out_scale):
    blk = BLK_A
    nq = 2 * blk
    sub = SUBLANES
    dv = DV_A
    npart = len(s_refs) // 4
    wq = nq // npart
    sa_ref, sb_ref = s_refs[:2 * npart], s_refs[2 * npart:]
    i = pl.program_id(1)

    @pl.when(i == 0)
    def _():
        def tr(b, carry):
            r0 = pl.multiple_of(b * blk, blk)
            vt_ref[0:dv, pl.ds(r0, blk)] = v_ref[pl.ds(r0, blk), :].astype(F32).T.astype(BF16)
            vt_ref[dv:dv + ONES_A, pl.ds(r0, blk)] = jnp.ones((ONES_A, blk), BF16)
            return carry
        lax.fori_loop(0, v_ref.shape[0] // blk, tr, 0)

    q = q_ref[...].astype(F32) * (DK_A ** -0.5 * LOG2E)
    lane = lax.broadcasted_iota(jnp.int32, q.shape, 1)
    qs_ref[:, 0:blk] = jnp.where(lane < DK_A, q, 0.0).T.astype(BF16)
    qs_ref[:, blk:nq] = jnp.where(lane >= DK_A, q, 0.0).T.astype(BF16)
    m_ref[...] = jnp.full(m_ref.shape, NEG_INF, F32)
    acc_ref[...] = jnp.zeros(acc_ref.shape, F32)

    def scores(b, s_ref):
        k = k_ref[pl.ds(pl.multiple_of(b * blk, blk), blk), :]
        for part in range(npart):
            s = _dot(k, qs_ref[:, part * wq:(part + 1) * wq])
            s_ref[part][:, 0:wq] = s
            s_ref[npart + part][...] = jnp.max(s.reshape(blk // sub, sub, wq), axis=0)

    def softmax_pv(b, s_ref, bias):
        vt = vt_ref[:, pl.ds(pl.multiple_of(b * blk, blk), blk)]
        for part in range(npart):
            cols = slice(part * wq, (part + 1) * wq)
            s = s_ref[part][:, 0:wq]
            if bias is not None:
                q0 = (part * wq) % blk
                s = s + bias[:, q0:q0 + wq]
            s = s.reshape(blk // sub, sub, wq)
            m_prev = m_ref[:, cols]
            smax = jnp.max(s, axis=0) if bias is not None else s_ref[npart + part][...]
            m_cur = jnp.max(smax, axis=0, keepdims=True)
            m_new = jnp.maximum(m_prev, m_cur)
            alpha = jnp.exp2(m_prev - m_new)
            p = jnp.exp2(s - m_new[None])
            pv = _dot(vt, p.reshape(blk, wq).astype(BF16))
            acc_ref[:, cols] = acc_ref[:, cols] * alpha[0:1] + pv
            m_ref[:, cols] = m_new

    nfar = jnp.maximum(i - 1, 0)
    odd = lax.rem(nfar, 2)

    @pl.when(i == 0)
    def _():
        scores(0, sb_ref)

    @pl.when(i > 0)
    def _():
        @pl.when(odd == 1)
        def _():
            scores(0, sb_ref)
            scores(1, sa_ref)
            softmax_pv(0, sb_ref, None)

        @pl.when(odd == 0)
        def _():
            scores(0, sa_ref)

        def pair(b):
            scores(b + 1, sb_ref)
            softmax_pv(b, sa_ref, None)
            scores(b + 2, sa_ref)
            softmax_pv(b + 1, sb_ref, None)

        def quad_body(t, carry):
            pair(odd + 4 * t)
            pair(odd + 4 * t + 2)
            return carry

        npairs = nfar // 2
        lax.fori_loop(0, npairs // 2, quad_body, 0)

        @pl.when(lax.rem(npairs, 2) == 1)
        def _():
            pair(odd + 2 * (npairs - 1))
        scores(i, sb_ref)
        softmax_pv(i - 1, sa_ref, bias_ref[0, 0])

    softmax_pv(i, sb_ref, bias_ref[0, 1])

    ot = acc_ref[0:dv, 0:nq] / acc_ref[dv:dv + 1, 0:nq]
    o = ot[:, 0:blk].T - lam_ref[...] * ot[:, blk:nq].T
    o = o * lax.rsqrt(jnp.mean(o * o, axis=-1, keepdims=True) + EPS) * g_ref[...]
    o_ref[...] = (o * out_scale).astype(o_ref.dtype)


_TOEPLITZ_ROWS = 256
_TOEPLITZ_N = 2048


def _toeplitz_kernel(v_ref, o_ref, *, keep):
    rows, cols = o_ref.shape[1:]
    x = jnp.broadcast_to(v_ref[0, 0], (rows, v_ref.shape[-1]))
    tile = pltpu.roll(x, 0, 1, stride=1, stride_axis=0)[:, :cols]
    r = lax.broadcasted_iota(jnp.int32, (rows, cols), 0) + pl.program_id(1) * rows
    c = lax.broadcasted_iota(jnp.int32, (rows, cols), 1)
    o_ref[0] = jnp.where(keep(r, c), tile, NEG_INF)


def _toeplitz_tiles(fn, keep, heads, rows, cols):
    n, rb = _TOEPLITZ_N, _TOEPLITZ_ROWS
    assert rows % rb == 0 and rows <= n // 2 and cols <= n // 2
    idx = jnp.arange(n, dtype=jnp.int32)
    vec = fn(jnp.where(idx < n // 2, idx, idx - n)).astype(F32)
    vecs = jnp.stack([jnp.roll(vec, k * rb, axis=1) for k in range(rows // rb)], axis=1)
    return pl.pallas_call(
        functools.partial(_toeplitz_kernel, keep=keep),
        grid=(heads, rows // rb),
        in_specs=[pl.BlockSpec((1, 1, 1, n), lambda h, k: (h, k, 0, 0))],
        out_specs=pl.BlockSpec((1, rb, cols), lambda h, k: (h, k, 0)),
        out_shape=jax.ShapeDtypeStruct((heads, rows, cols), F32),
        name="toeplitz_tiles",
    )(vecs.reshape(heads, rows // rb, 1, n))


def _t5_bucket(rel):
    nb = NUM_BUCKETS // 2
    max_exact = nb // 2
    bucket = jnp.where(rel > 0, nb, 0)
    n = jnp.abs(rel)
    nf = jnp.maximum(n, 1).astype(F32)
    large = max_exact + (jnp.log(nf / max_exact) / math.log(MAX_DISTANCE / max_exact)
                         * (nb - max_exact)).astype(jnp.int32)
    large = jnp.minimum(large, nb - 1)
    return bucket + jnp.where(n < max_exact, n, large)


def _diff_bias_tiles(t5_table):
    blk = BLK_A
    table = t5_table.astype(F32)
    far = table[_t5_bucket(jnp.full((), -(blk + 1), jnp.int32))]
    def visible(r, c):
        return jnp.floor_divide(r - blk, CHUNK) <= jnp.floor_divide(c, CHUNK)

    tiles = _toeplitz_tiles(lambda x: ((table[_t5_bucket(-x - blk)] - far) * LOG2E).T, visible,
                            N_HEADS_A, 2 * blk, blk)
    return tiles.reshape(N_HEADS_A, 2, blk, blk)


def _diff_attention(proj, t5_table, lam, subln_g, lam_init):
    seq = proj.shape[0]
    blk = BLK_A
    bias = _diff_bias_tiles(t5_table)
    ha = N_HEADS_A
    kern = functools.partial(_diffattn_kernel, out_scale=1.0 - lam_init)
    return pl.pallas_call(
        kern,
        grid=(ha, seq // blk),
        in_specs=[
            pl.BlockSpec((blk, DV_A), lambda h, i: (i, h)),
            pl.BlockSpec((seq, DV_A), lambda h, i: (0, ha + h)),
            pl.BlockSpec((seq, DV_A), lambda h, i: (0, 2 * ha + h)),
            pl.BlockSpec((1, 2, blk, blk), lambda h, i: (h, 0, 0, 0)),
            pl.BlockSpec((1, DV_A), lambda h, i: (0, 0)),
            pl.BlockSpec((1, DV_A), lambda h, i: (0, 0)),
        ],
        out_specs=pl.BlockSpec((blk, DV_A), lambda h, i: (i, h)),
        out_shape=jax.ShapeDtypeStruct((seq, ha * DV_A), BF16),
        scratch_shapes=[
            pltpu.VMEM((DV_A, 2 * blk), BF16),
            pltpu.VMEM((DV_A + ONES_A, seq), BF16),
            pltpu.VMEM((SUBLANES, 2 * blk), F32),
            pltpu.VMEM((DV_A + ONES_A, 2 * blk), F32),
        ] + 2 * ([pltpu.VMEM((blk, 2 * blk // NPART_A + SCORE_PAD), F32)] * NPART_A
                 + [pltpu.VMEM((SUBLANES, 2 * blk // NPART_A), F32)] * NPART_A),
        compiler_params=pltpu.CompilerParams(dimension_semantics=("parallel", "arbitrary")),
        name="diff_attention",
    )(proj, proj, proj, bias, jnp.full((1, DV_A), lam, F32), subln_g.reshape(1, DV_A).astype(F32))


def _band_kernel(q_ref, kp_ref, kc_ref, vp_ref, vc_ref, bias_ref, o_ref, *s_refs):
    qw, band = QW_B, BAND_B
    nk = band + qw
    sub = SUBLANES
    i = pl.program_id(1)
    q = q_ref[...].astype(F32) * (DH_B ** -0.5 * LOG2E)
    lane = lax.broadcasted_iota(jnp.int32, q.shape, 1)
    qh = (jnp.where(lane < DH_B, q, 0.0).T.astype(BF16), jnp.where(lane >= DH_B, q, 0.0).T.astype(BF16))
    k_all = jnp.concatenate([kp_ref[...], kc_ref[...]], axis=0)
    vt_all = jnp.concatenate([vp_ref[...], vc_ref[...]], axis=0).astype(F32).T.astype(BF16)
    bias = jnp.concatenate([bias_ref[0], bias_ref[1]], axis=1)
    no_prev = jnp.where(i == 0, NEG_INF, 0.0).astype(F32)
    krow = lax.broadcasted_iota(jnp.int32, (nk, 2 * qw), 0)
    ngroups = len(s_refs)
    for g in range(ngroups):
        k0 = g * qw
        qs = jnp.concatenate([qh[0][:, k0:k0 + qw], qh[1][:, k0:k0 + qw]], axis=1)
        s_refs[g][:, 0:2 * qw] = _dot(k_all[k0:k0 + nk], qs)
    for g in range(ngroups):
        k0 = g * qw
        s = s_refs[g][:, 0:2 * qw] + bias
        if k0 < band:
            s = s + jnp.where(krow < band - k0, no_prev, 0.0)
        s = s.reshape(nk // sub, sub, 2 * qw)
        m = jnp.max(jnp.max(s, axis=0), axis=0, keepdims=True)
        p = jnp.exp2(s - m[None])
        l = jnp.sum(jnp.sum(p, axis=0), axis=0, keepdims=True)
        ot = _dot(vt_all[:, k0:k0 + nk], p.reshape(nk, 2 * qw).astype(BF16)) / l
        o = jnp.concatenate([ot[0:DH_B, 0:qw], ot[DH_B:2 * DH_B, qw:2 * qw]], axis=0)
        o_ref[k0:k0 + qw, :] = o.T.astype(o_ref.dtype)


def _band_bias_tiles(rel_bias):
    band = BAND_B

    def valid(r, c):
        qchunk = jnp.floor_divide(c, CHUNK)
        kchunk = jnp.floor_divide(r - band, CHUNK)
        return (kchunk <= qchunk) & (kchunk >= qchunk - LEFT_CHUNKS)

    return _toeplitz_tiles(
        lambda x: rel_bias.astype(F32)[:, jnp.clip(-x - band, -REL_CLIP, REL_CLIP) + REL_CLIP] * LOG2E, valid,
        N_HEADS_B, band + QW_B, QW_B)


def _band_attention(proj, rel_bias):
    seq = proj.shape[0]
    blk, band, qw = BLK_B, BAND_B, QW_B
    bias = _band_bias_tiles(rel_bias)
    npair = N_HEADS_B // 2
    qc0 = 3 * N_HEADS_A
    per = blk // band
    prev = lambda c0: (lambda hp, i: (jnp.maximum(i * per - 1, 0), c0 + hp))
    cur = lambda c0: (lambda hp, i: (i, c0 + hp))
    return pl.pallas_call(
        _band_kernel,
        grid=(npair, seq // blk),
        in_specs=[
            pl.BlockSpec((blk, LANES), cur(qc0)),
            pl.BlockSpec((band, LANES), prev(qc0 + npair)),
            pl.BlockSpec((blk, LANES), cur(qc0 + npair)),
            pl.BlockSpec((band, LANES), prev(qc0 + 2 * npair)),
            pl.BlockSpec((blk, LANES), cur(qc0 + 2 * npair)),
            pl.BlockSpec((2, band + qw, qw), lambda hp, i: (hp, 0, 0)),
        ],
        out_specs=pl.BlockSpec((blk, LANES), lambda hp, i: (i, hp)),
        out_shape=jax.ShapeDtypeStruct((seq, N_HEADS_B * DH_B), BF16),
        scratch_shapes=[pltpu.VMEM((band + qw, 2 * qw + SCORE_PAD), F32)] * (blk // qw),
        compiler_params=pltpu.CompilerParams(dimension_semantics=("parallel", "arbitrary")),
        name="band_attention",
    )(proj, proj, proj, proj, proj, bias)


def _retention_kernel(qk_ref, v_ref, gate_ref, cos_ref, sin_ref, qdec_ref, kdec_ref, dmat_ref,
                      sdec_ref, o_ref, state_ref):
    @pl.when(pl.program_id(0) == 0)
    def _():
        state_ref[...] = jnp.zeros(state_ref.shape, F32)

    cos = cos_ref[...]
    sin = sin_ref[...]
    lane = lax.broadcasted_iota(jnp.int32, cos.shape, 1)
    first_half = (lane % DQK_C) < (DQK_C // 2)
    qk = qk_ref[...]
    parts = []
    for j in range(qk.shape[1] // LANES):
        t = qk[:, j * LANES:(j + 1) * LANES]
        partner = jnp.where(first_half, pltpu.roll(t, LANES - DQK_C // 2, 1), pltpu.roll(t, DQK_C // 2, 1))
        parts.append(t * cos + partner * sin)
    wq = N_HEADS_C * DQK_C
    q = jnp.concatenate(parts[:wq // LANES], axis=1)
    k = jnp.concatenate(parts[wq // LANES:], axis=1) * (DQK_C ** -0.5)
    qd = (q * qdec_ref[...]).astype(BF16)
    kd = (k * kdec_ref[...]).astype(BF16)
    qb = q.astype(BF16)
    kb = k.astype(BF16)
    vb = v_ref[...].astype(BF16)
    gate = gate_ref[...]
    outs = []
    for h in range(N_HEADS_C):
        qs = slice(h * DQK_C, (h + 1) * DQK_C)
        vs = slice(h * DV_C, (h + 1) * DV_C)
        scores = _dot_nt(qb[:, qs], kb[:, qs]) * dmat_ref[h]
        state = state_ref[h]
        r = _dot(scores.astype(BF16), vb[:, vs]) + _dot(qd[:, qs], state.astype(BF16))
        state_ref[h] = state * sdec_ref[h] + _dot_tn(kd[:, qs], vb[:, vs])
        r = r * lax.rsqrt(jnp.mean(r * r, axis=-1, keepdims=True) + EPS)
        g = gate[:, vs]
        outs.append(r * (g * jax.nn.sigmoid(g)))
    o_ref[...] = jnp.concatenate(outs, axis=1).astype(o_ref.dtype)


def _retention_tables(seq):
    t = BLK_C
    half = DQK_C // 2
    inv_freq = 1.0 / (ROPE_BASE ** (jnp.arange(0, DQK_C, 2, dtype=F32) / DQK_C))
    ang = jnp.arange(seq, dtype=F32)[:, None] * inv_freq[None, :]
    reps = LANES // half
    cos = jnp.tile(jnp.cos(ang), (1, reps))
    sign = jnp.where((jnp.arange(LANES) % DQK_C) < half, -1.0, 1.0).astype(F32)
    sin = jnp.tile(jnp.sin(ang), (1, reps)) * sign[None, :]
    log_g = jnp.log(1.0 - jnp.power(2.0, -5.0 - jnp.arange(N_HEADS_C, dtype=F32)))
    pos = jnp.arange(t, dtype=F32)
    diff = pos[:, None] - pos[None, :]
    same_or_past = (jnp.arange(t)[None, :] // CHUNK) <= (jnp.arange(t)[:, None] // CHUNK)
    dmat = jnp.where(same_or_past[None], jnp.exp(log_g[:, None, None] * jnp.abs(diff)[None]), 0.0)
    qdec = jnp.repeat(jnp.exp(log_g[None, :] * (pos[:, None] + 1.0)), DQK_C, axis=1)
    kdec = jnp.repeat(jnp.exp(log_g[None, :] * (t - 1.0 - pos[:, None])), DQK_C, axis=1)
    sdec = jnp.broadcast_to(jnp.exp(log_g * t)[:, None, None], (N_HEADS_C, 1, DV_C))
    return cos, sin, qdec, kdec, dmat, sdec


def _retention(proj):
    seq = proj.shape[0]
    t = BLK_C
    cos, sin, qdec, kdec, dmat, sdec = _retention_tables(seq)
    wv = N_HEADS_C * DV_C
    return pl.pallas_call(
        _retention_kernel,
        grid=(seq // t,),
        in_specs=[
            pl.BlockSpec((t, wv), lambda i: (i, 0)),
            pl.BlockSpec((t, wv), lambda i: (i, 1)),
            pl.BlockSpec((t, wv), lambda i: (i, 2)),
            pl.BlockSpec((t, LANES), lambda i: (i, 0)),
            pl.BlockSpec((t, LANES), lambda i: (i, 0)),
            pl.BlockSpec((t, N_HEADS_C * DQK_C), lambda i: (0, 0)),
            pl.BlockSpec((t, N_HEADS_C * DQK_C), lambda i: (0, 0)),
            pl.BlockSpec((N_HEADS_C, t, t), lambda i: (0, 0, 0)),
            pl.BlockSpec((N_HEADS_C, 1, DV_C), lambda i: (0, 0, 0)),
        ],
        out_specs=pl.BlockSpec((t, wv), lambda i: (i, 0)),
        out_shape=jax.ShapeDtypeStruct((seq, wv), BF16),
        scratch_shapes=[pltpu.VMEM((N_HEADS_C, DQK_C, DV_C), F32)],
        compiler_params=pltpu.CompilerParams(dimension_semantics=("arbitrary",)),
        name="retention",
    )(proj, proj, proj, cos, sin, qdec, kdec, dmat, sdec)


def _s5_kernel(*refs):
    ncb = S5_CH // LANES
    u_refs = refs[:ncb]
    (mt_ref, bt_ref, ctr_ref, cti_ref, are_ref, aim_ref, y_ref,
     ut_ref, yt_ref, ys_ref, vr_ref, vi_ref, spr_ref, spi_ref, carry_ref) = refs[ncb:]
    tc = S5_TC
    gp = S5_GROUP
    n = S5_STATE
    ng = S5_GROUPS

    @pl.when(pl.program_id(0) == 0)
    def _():
        carry_ref[...] = jnp.zeros(carry_ref.shape, F32)

    for s in range(S5_T):
        for k in range(ncb):
            ut_ref[s, k * LANES:(k + 1) * LANES, :] = u_refs[k][pl.ds(s, tc, stride=S5_T), :].T

    unroll = 4

    def intra(it, carry):
        for k in range(unroll):
            g = it * unroll + k
            r0 = pl.multiple_of(g * gp, gp)
            ug = ut_ref[:, pl.ds(r0, gp), :].reshape(S5_T * gp, tc).astype(BF16)
            yt_ref[:, pl.ds(r0, gp), :] = _dot(mt_ref[g], ug).reshape(S5_T, gp, tc)
            vt = _dot(bt_ref[g], ug)
            n0 = pl.multiple_of(g * n, n)
            vr_ref[pl.ds(n0, n), :] = vt[0:n]
            vi_ref[pl.ds(n0, n), :] = vt[n:2 * n]
        return carry

    lax.fori_loop(0, ng // unroll, intra, 0)

    sub = SUBLANES
    nv = tc // sub
    row = lax.broadcasted_iota(jnp.int32, (tc, LANES), 0)
    in_vreg = lax.rem(row, sub)

    def rows_of(v, r):
        return jnp.broadcast_to(v[r:r + 1], (tc, LANES))

    for j in range(ng * n // LANES):
        cols = slice(j * LANES, (j + 1) * LANES)
        pwr, pwi = are_ref[:, cols], aim_ref[:, cols]
        xr = vr_ref[cols, :].T
        xi = vi_ref[cols, :].T
        for d in (1, 2, 4):
            keep = in_vreg >= d
            sr = jnp.where(keep, pltpu.roll(xr, d, 0), 0.0)
            si = jnp.where(keep, pltpu.roll(xi, d, 0), 0.0)
            fr, fi = rows_of(pwr, d - 1), rows_of(pwi, d - 1)
            xr, xi = xr + (fr * sr - fi * si), xi + (fr * si + fi * sr)
        cr, ci = carry_ref[0, :, cols], carry_ref[1, :, cols]
        cr0, ci0 = cr, ci
        outr, outi = [], []
        for v in range(nv):
            yr = xr[v * sub:(v + 1) * sub] + (pwr * cr - pwi * ci)
            yi = xi[v * sub:(v + 1) * sub] + (pwr * ci + pwi * cr)
            outr.append(yr)
            outi.append(yi)
            cr = jnp.broadcast_to(yr[sub - 1:sub], (sub, LANES))
            ci = jnp.broadcast_to(yi[sub - 1:sub], (sub, LANES))
        carry_ref[0, :, cols] = cr
        carry_ref[1, :, cols] = ci
        sr = jnp.concatenate(outr, axis=0)
        si = jnp.concatenate(outi, axis=0)
        first = row == 0
        spr_ref[j] = jnp.where(first, rows_of(cr0, 0), pltpu.roll(sr, 1, 0))
        spi_ref[j] = jnp.where(first, rows_of(ci0, 0), pltpu.roll(si, 1, 0))

    def cross(it, carry):
        for k in range(unroll):
            jp = it * unroll + k
            r0 = pl.multiple_of(jp * 2 * gp, 2 * gp)
            yc = (_dot_nt(ctr_ref[jp], spr_ref[jp].astype(BF16))
                  + _dot_nt(cti_ref[jp], spi_ref[jp].astype(BF16)))
            yt_ref[:, pl.ds(r0, 2 * gp), :] += yc.reshape(S5_T, 2 * gp, tc)
        return carry

    lax.fori_loop(0, ng // 2 // unroll, cross, 0)

    for s in range(S5_T):
        for k in range(ncb):
            ys_ref[k, pl.ds(s, tc, stride=S5_T), :] = yt_ref[s, k * LANES:(k + 1) * LANES, :].T
    for k in range(ncb):
        y_ref[:, k * LANES:(k + 1) * LANES] = ys_ref[k]


def _s5_matrices(lam_re, lam_im, log_step, b_re, b_im, c_re, c_im, d_skip):
    hi = lax.Precision.HIGHEST
    t, gp, n, ng = S5_T, S5_GROUP, S5_STATE, S5_GROUPS
    lam = lax.complex(lam_re.astype(F32), lam_im.astype(F32))
    step = jnp.exp(log_step.astype(F32))[:, None]
    ls = lam * step
    a_bar = jnp.exp(ls)
    b_bar = ((a_bar - 1.0) / lam)[..., None] * lax.complex(b_re.astype(F32), b_im.astype(F32))
    cm = lax.complex(c_re.astype(F32), c_im.astype(F32))

    def apow(k):
        kk = k.astype(F32).astype(jnp.complex64)
        return jnp.exp(ls.reshape((ng,) + (1,) * k.ndim + (n,)) * kk[None, ..., None])

    tt = jnp.arange(t)
    kmat = jnp.einsum('gpn,gln,gnq->glpq', cm, apow(tt), b_bar, precision=hi).real
    krev = jnp.transpose(kmat[:, ::-1], (0, 2, 1, 3)).reshape(ng, gp, t * gp)
    kpad = jnp.pad(krev, ((0, 0), (0, 0), (0, t * gp)))
    mt = jnp.concatenate([kpad[:, :, (t - 1 - to) * gp:(2 * t - 1 - to) * gp] for to in range(t)], axis=1)
    dvec = jnp.tile(d_skip.astype(F32).reshape(ng, 1, gp), (1, t, 1)).reshape(ng, t * gp)
    mt = mt + jnp.eye(t * gp, dtype=F32)[None] * dvec[:, :, None]
    z = jnp.swapaxes(apow(t - 1 - tt), 1, 2)[:, :, :, None] * b_bar[:, :, None, :]
    z = z.reshape(ng, n, t * gp)
    bt = jnp.concatenate([z.real, z.imag], axis=1)
    w = cm[:, None, :, :] * apow(tt + 1)[:, :, None, :]

    def pair_readout(x):
        x = x.reshape(ng // 2, 2, t, gp, n)
        first = jnp.pad(x[:, 0], ((0, 0), (0, 0), (0, 0), (0, n)))
        second = jnp.pad(x[:, 1], ((0, 0), (0, 0), (0, 0), (n, 0)))
        return jnp.stack([first, second], axis=2).reshape(ng // 2, t * 2 * gp, 2 * n).astype(BF16)

    ctr, cti = pair_readout(w.real), pair_readout(-w.imag)
    a_chunk = jnp.transpose(apow(t * (jnp.arange(SUBLANES) + 1)), (1, 0, 2)).reshape(SUBLANES, ng * n)
    return mt.astype(BF16), bt.astype(BF16), ctr, cti, a_chunk.real, a_chunk.imag


def _s5(proj, mats):
    seq, width = proj.shape
    t, tc, gp, n, ng = S5_T, S5_TC, S5_GROUP, S5_STATE, S5_GROUPS
    rows = t * tc
    ncb = S5_CH // LANES
    cb0 = (width - S5_CH) // LANES
    u_specs = [pl.BlockSpec((rows, LANES), (lambda i, k=k: (i, cb0 + k))) for k in range(ncb)]
    nsb = ng * n // LANES
    return pl.pallas_call(
        _s5_kernel,
        grid=(seq // rows,),
        in_specs=u_specs + [_const_spec(m.shape) for m in mats],
        out_specs=pl.BlockSpec((rows, S5_CH), lambda i: (i, 0)),
        out_shape=jax.ShapeDtypeStruct((seq, S5_CH), F32),
        scratch_shapes=[
            pltpu.VMEM((t, S5_CH, tc), F32),
            pltpu.VMEM((t, S5_CH, tc), F32),
            pltpu.VMEM((ncb, rows, LANES), F32),
            pltpu.VMEM((ng * n, tc), F32),
            pltpu.VMEM((ng * n, tc), F32),
            pltpu.VMEM((nsb, tc, LANES), F32),
            pltpu.VMEM((nsb, tc, LANES), F32),
            pltpu.VMEM((2, SUBLANES, ng * n), F32),
        ],
        compiler_params=pltpu.CompilerParams(dimension_semantics=("arbitrary",)),
        name="s5_scan",
    )(*([proj] * ncb), *mats)


def _mix_ffn_kernel(*refs, glu, final):
    (x_ref, a_ref, b_ref, wo_ref, g1_ref), refs = refs[:5], refs[5:]
    if glu:
        gw_ref, refs = refs[0], refs[1:]
    (g_ref, sc_ref, sh_ref, gate_ref, win_ref, cw_ref, cb_ref, wout_ref), refs = refs[:8], refs[8:]
    if final:
        fg_ref, o_ref, h_ref, act_ref, gbuf_ref, carry_ref = refs
    else:
        ng_ref, nsc_ref, nsh_ref, o_ref, hn_ref, h_ref, act_ref, gbuf_ref, carry_ref = refs
    tm = x_ref.shape[0]
    halo = gbuf_ref.shape[0] - tm

    @pl.when(pl.program_id(0) == 0)
    def _():
        carry_ref[...] = jnp.zeros(carry_ref.shape, F32)

    if glu:
        y = jax.nn.gelu(b_ref[...]).astype(BF16)
        gg = _dot(y, gw_ref[...])
        half = gg.shape[1] // 2
        b = (gg[:, :half] * jax.nn.sigmoid(gg[:, half:])).astype(BF16)
    else:
        b = b_ref[...]
    cat = jnp.concatenate([a_ref[...], b], axis=1)
    x = x_ref[...] + g1_ref[...] * _dot(cat, wo_ref[...])
    h_ref[...] = _mod_rmsnorm(x, g_ref[...], sc_ref[...], sh_ref[...]).astype(BF16)
    for f in range(D_FF // TF_FFN):
        cs = slice(f * TF_FFN, (f + 1) * TF_FFN)
        gs = slice(D_FF + f * TF_FFN, D_FF + (f + 1) * TF_FFN)
        h = h_ref[...]
        val = _dot(h, win_ref[:, cs])
        gate = _dot(h, win_ref[:, gs])
        gbuf_ref[0:halo, :] = carry_ref[:, cs]
        gbuf_ref[halo:halo + tm, :] = gate
        carry_ref[:, cs] = gate[tm - halo:tm, :]
        conv = (gate * cw_ref[2:3, cs] + gbuf_ref[halo - 1:halo - 1 + tm, :] * cw_ref[1:2, cs]
                + gbuf_ref[halo - 2:halo - 2 + tm, :] * cw_ref[0:1, cs] + cb_ref[:, cs])
        act_ref[:, cs] = (jax.nn.gelu(conv) * val).astype(BF16)
    xn = x + gate_ref[...] * _dot(act_ref[...], wout_ref[...])
    if final:
        xn = xn * lax.rsqrt(jnp.mean(xn * xn, axis=-1, keepdims=True) + EPS) * fg_ref[...]
    else:
        hn_ref[...] = _mod_rmsnorm(xn, ng_ref[...], nsc_ref[...], nsh_ref[...]).astype(BF16)
    o_ref[...] = xn


def _layer_spec(shape, layer):
    idx = (layer,) + (0,) * (len(shape) - 1)
    return pl.BlockSpec((None,) + tuple(shape[1:]), lambda *_: idx, pipeline_mode=pl.Buffered(1))


def _mix_ffn(x, a, b, wo, gate1, glu_w, g, scale, shift, gate2, w_in, conv_w, conv_b, w_out, tail, layer):
    seq, d = x.shape
    final = len(tail) == 1
    tm = TM_FFN
    halo = SUBLANES
    row = pl.BlockSpec((1, d), lambda i: (0, 0))
    rows = lambda w: pl.BlockSpec((tm, w), lambda i: (i, 0))
    conv_b = conv_b.reshape(conv_b.shape[0], 1, D_FF)
    in_specs = [rows(d), rows(a.shape[1]), rows(b.shape[1]), _const_spec(wo.shape), row]
    args = [x, a, b, wo, gate1]
    if glu_w is not None:
        in_specs.append(_const_spec(glu_w.shape))
        args.append(glu_w)
    in_specs += [
        row, row, row, row,
        _layer_spec(w_in.shape, layer),
        _layer_spec(conv_w.shape, layer),
        _layer_spec(conv_b.shape, layer),
        _layer_spec(w_out.shape, layer),
    ] + [row] * len(tail)
    args += [g.reshape(1, d), scale, shift, gate2, w_in, conv_w, conv_b, w_out]
    args += [t.reshape(1, d) for t in tail]
    out_specs = [rows(d)] if final else [rows(d), rows(d)]
    out_shape = [jax.ShapeDtypeStruct((seq, d), F32)] + ([] if final else [jax.ShapeDtypeStruct((seq, d), BF16)])
    return pl.pallas_call(
        functools.partial(_mix_ffn_kernel, glu=glu_w is not None, final=final),
        grid=(seq // tm,),
        in_specs=in_specs,
        out_specs=out_specs,
        out_shape=out_shape,
        scratch_shapes=[
            pltpu.VMEM((tm, d), BF16),
            pltpu.VMEM((tm, D_FF), BF16),
            pltpu.VMEM((tm + halo, TF_FFN), F32),
            pltpu.VMEM((halo, D_FF), F32),
        ],
        compiler_params=pltpu.CompilerParams(dimension_semantics=("arbitrary",)),
        name="mix_ffn",
    )(*args)


def kernel(x, c, t5_table, mod_w, mod_b, norm1_g, norm2_g, ffn_w_in, ffn_conv_w, ffn_conv_b, ffn_w_out,
           ev_w_in, ev_w_out, diff_lambda, diff_subln_g, band_rel_bias,
           od_w_in, od_w_out, s5_lam_re, s5_lam_im, s5_log_step, s5_b_re, s5_b_im, s5_c_re, s5_c_im,
           s5_d, s5_glu_w, final_g):
    assert x.shape[0] == 1 and x.shape[2] == D_MODEL
    seq = x.shape[1]
    assert seq % TM_PROJ == 0 and seq % (S5_T * S5_TC) == 0
    d = D_MODEL
    xs = x[0]
    mod = _modulation(c, mod_w, mod_b)
    ffn_w_in_b = ffn_w_in.astype(BF16)
    ffn_w_out_b = ffn_w_out.astype(BF16)
    mods = [[mod[i, :, k * d:(k + 1) * d] for k in range(6)] for i in range(DEPTH)]
    h = None
    for i in range(DEPTH):
        sh1, sc1, g1, sh2, sc2, g2 = mods[i]
        w_in = (ev_w_in if i % 2 == 0 else od_w_in)[i // 2].astype(BF16)
        proj_dtype = BF16 if i % 2 == 0 else F32
        if h is None:
            proj = _normproj(xs, norm1_g[i], sc1, sh1, w_in, proj_dtype)
        else:
            proj = _proj(h, w_in, proj_dtype)
        if i % 2 == 0:
            e = i // 2
            lam_init = 0.8 - 0.6 * math.exp(-0.3 * i)
            lp = diff_lambda[e].astype(F32)
            lam = jnp.exp(jnp.sum(lp[0] * lp[1])) - jnp.exp(jnp.sum(lp[2] * lp[3])) + lam_init
            mix_a = _diff_attention(proj, t5_table, lam, diff_subln_g[e], lam_init)
            mix_b = _band_attention(proj, band_rel_bias[e])
            wo, glu_w = ev_w_out[e].astype(BF16), None
        else:
            o = i // 2
            mix_a = _retention(proj)
            mats = _s5_matrices(s5_lam_re[o], s5_lam_im[o], s5_log_step[o], s5_b_re[o], s5_b_im[o],
                                s5_c_re[o], s5_c_im[o], s5_d[o])
            mix_b = _s5(proj, mats)
            wo, glu_w = od_w_out[o].astype(BF16), s5_glu_w[o].astype(BF16)
        if i == DEPTH - 1:
            tail = (final_g,)
        else:
            nsh1, nsc1 = mods[i + 1][0], mods[i + 1][1]
            tail = (norm1_g[i + 1], nsc1, nsh1)
        out = _mix_ffn(xs, mix_a, mix_b, wo, g1, glu_w, norm2_g[i], sc2, sh2, g2,
                       ffn_w_in_b, ffn_conv_w, ffn_conv_b, ffn_w_out_b, tail, layer=i)
        if i == DEPTH - 1:
            xs = out[0]
        else:
            xs, h = out
    return xs[None]
```

```python
import functools
import math

import jax
import jax.numpy as jnp
from jax import lax
from jax.experimental import pallas as pl
from jax.experimental.pallas import tpu as pltpu

F32 = jnp.float32
BF16 = jnp.bfloat16

D_MODEL = 1024
DEPTH = 2
CHUNK = 64
GROUP_WIDTH = D_MODEL // 2
DK_A = 64
DV_A = 2 * DK_A
N_HEADS_A = GROUP_WIDTH // DV_A
DH_B = 64
N_HEADS_B = GROUP_WIDTH // DH_B
LEFT_CHUNKS = 8
REL_CLIP = 2 * CHUNK
NUM_BUCKETS = 32
MAX_DISTANCE = 128
DV_C = 128
DQK_C = DV_C // 2
N_HEADS_C = GROUP_WIDTH // DV_C
ROPE_BASE = 10000.0
S5_CH = GROUP_WIDTH
S5_GROUP = 16
S5_GROUPS = S5_CH // S5_GROUP
S5_STATE = 64
D_FF = ((8 * D_MODEL // 3 + 255) // 256) * 256
CONV_W = 3
EVEN_IN = 3 * N_HEADS_A * DV_A + 3 * N_HEADS_B * DH_B
ODD_IN = 2 * N_HEADS_C * DQK_C + 2 * N_HEADS_C * DV_C + S5_CH
EPS = 1e-6
NEG_INF = -1e30
LOG2E = math.log2(math.e)

LANES = 128
SUBLANES = 8
MXU_DIM = 256

TM_PROJ = 1024
TN_PROJ = 1024
TN_MOD = 1536
TM_FFN = 512
TF_FFN = MXU_DIM
BLK_A = 512
NPART_A = 2
ONES_A = 16
SCORE_PAD = LANES
BLK_B = 1024
BAND_B = LEFT_CHUNKS * CHUNK
QW_B = 4 * CHUNK
BLK_C = 512
S5_T = 16
S5_TC = LANES

assert BLK_B % BAND_B == 0 and BLK_B % QW_B == 0 and BAND_B % QW_B == 0
assert BLK_A >= MAX_DISTANCE, "far key blocks must sit in the saturated T5 bucket"
assert DV_A == LANES and 2 * DK_A == LANES and 2 * DH_B == LANES, "attention heads are read as 128-lane column blocks"


def _dot(a, b):
    return jnp.dot(a, b, preferred_element_type=F32)


def _dot_nt(a, b):
    return lax.dot_general(a, b, (((1,), (1,)), ((), ())), preferred_element_type=F32)


def _dot_tn(a, b):
    return lax.dot_general(a, b, (((0,), (0,)), ((), ())), preferred_element_type=F32)


def _const_spec(shape):
    zeros = (0,) * len(shape)
    return pl.BlockSpec(shape, lambda *_: zeros, pipeline_mode=pl.Buffered(1))


def _mod_rmsnorm(x, g, scale, shift):
    y = x * lax.rsqrt(jnp.mean(x * x, axis=-1, keepdims=True) + EPS)
    y = y * g
    return y * (1.0 + scale) + shift


def _mod_kernel(c_ref, w_ref, b_ref, o_ref):
    c = c_ref[...]
    cond = c * jax.nn.sigmoid(c)
    o_ref[0] = jnp.sum(cond * w_ref[0], axis=0, keepdims=True) + b_ref[0]


def _modulation(c, mod_w, mod_b):
    depth, d, n = mod_w.shape
    tn = TN_MOD
    return pl.pallas_call(
        _mod_kernel,
        grid=(depth, n // tn),
        in_specs=[
            pl.BlockSpec((d, 1), lambda i, j: (0, 0)),
            pl.BlockSpec((1, d, tn), lambda i, j: (i, 0, j)),
            pl.BlockSpec((1, 1, tn), lambda i, j: (i, 0, j)),
        ],
        out_specs=pl.BlockSpec((1, 1, tn), lambda i, j: (i, 0, j)),
        out_shape=jax.ShapeDtypeStruct((depth, 1, n), F32),
        name="modulation",
    )(c.reshape(d, 1), mod_w, mod_b.reshape(depth, 1, n))


def _normproj_kernel(x_ref, g_ref, sc_ref, sh_ref, w_ref, o_ref):
    tm, n = o_ref.shape
    half = tm // 2
    for r in range(2):
        rows = slice(r * half, (r + 1) * half)
        h = _mod_rmsnorm(x_ref[rows, :], g_ref[...], sc_ref[...], sh_ref[...]).astype(BF16)
        for j in range(n // TN_PROJ):
            cols = slice(j * TN_PROJ, (j + 1) * TN_PROJ)
            o_ref[rows, cols] = _dot(h, w_ref[:, cols]).astype(o_ref.dtype)


def _normproj(x, g, scale, shift, w, out_dtype):
    seq, d = x.shape
    n = w.shape[1]
    tm = TM_PROJ
    row = pl.BlockSpec((1, d), lambda i: (0, 0))
    return pl.pallas_call(
        _normproj_kernel,
        grid=(seq // tm,),
        in_specs=[pl.BlockSpec((tm, d), lambda i: (i, 0)), row, row, row, _const_spec(w.shape)],
        out_specs=pl.BlockSpec((tm, n), lambda i: (i, 0)),
        out_shape=jax.ShapeDtypeStruct((seq, n), out_dtype),
        compiler_params=pltpu.CompilerParams(dimension_semantics=("parallel",)),
        name="normproj",
    )(x, g.reshape(1, d), scale, shift, w)


def _proj_kernel(h_ref, w_ref, o_ref):
    for j in range(o_ref.shape[1] // TN_PROJ):
        cols = slice(j * TN_PROJ, (j + 1) * TN_PROJ)
        o_ref[:, cols] = _dot(h_ref[...], w_ref[:, cols]).astype(o_ref.dtype)


def _proj(h, w, out_dtype):
    seq, d = h.shape
    n = w.shape[1]
    tm = TM_PROJ
    return pl.pallas_call(
        _proj_kernel,
        grid=(seq // tm,),
        in_specs=[pl.BlockSpec((tm, d), lambda i: (i, 0)), _const_spec(w.shape)],
        out_specs=pl.BlockSpec((tm, n), lambda i: (i, 0)),
        out_shape=jax.ShapeDtypeStruct((seq, n), out_dtype),
        compiler_params=pltpu.CompilerParams(dimension_semantics=("parallel",)),
        name="proj",
    )(h, w)


def _diffattn_kernel(q_ref, k_ref, v_ref, bias_ref, lam_ref, g_ref, o_ref,
                     qs_ref, vt_ref, m_ref, acc_ref, *s_refs, out_scale):
    blk = BLK_A
    nq = 2 * blk
    sub = SUBLANES
    dv = DV_A
    npart = len(s_refs) // 4
    wq = nq // npart
    sa_ref, sb_ref = s_refs[:2 * npart], s_refs[2 * npart:]
    i = pl.program_id(1)

    @pl.when(i == 0)
    def _():
        def tr(b, carry):
            r0 = pl.multiple_of(b * blk, blk)
            vt_ref[0:dv, pl.ds(r0, blk)] = v_ref[pl.ds(r0, blk), :].astype(F32).T.astype(BF16)
            vt_ref[dv:dv + ONES_A, pl.ds(r0, blk)] = jnp.ones((ONES_A, blk), BF16)
            return carry
        lax.fori_loop(0, v_ref.shape[0] // blk, tr, 0)

    q = q_ref[...].astype(F32) * (DK_A ** -0.5 * LOG2E)
    lane = lax.broadcasted_iota(jnp.int32, q.shape, 1)
    qs_ref[:, 0:blk] = jnp.where(lane < DK_A, q, 0.0).T.astype(BF16)
    qs_ref[:, blk:nq] = jnp.where(lane >= DK_A, q, 0.0).T.astype(BF16)
    m_ref[...] = jnp.full(m_ref.shape, NEG_INF, F32)
    acc_ref[...] = jnp.zeros(acc_ref.shape, F32)

    def scores(b, s_ref):
        k = k_ref[pl.ds(pl.multiple_of(b * blk, blk), blk), :]
        for part in range(npart):
            s = _dot(k, qs_ref[:, part * wq:(part + 1) * wq])
            s_ref[part][:, 0:wq] = s
            s_ref[npart + part][...] = jnp.max(s.reshape(blk // sub, sub, wq), axis=0)

    def softmax_pv(b, s_ref, bias):
        vt = vt_ref[:, pl.ds(pl.multiple_of(b * blk, blk), blk)]
        for part in range(npart):
            cols = slice(part * wq, (part + 1) * wq)
            s = s_ref[part][:, 0:wq]
            if bias is not None:
                q0 = (part * wq) % blk
                s = s + bias[:, q0:q0 + wq]
            s = s.reshape(blk // sub, sub, wq)
            m_prev = m_ref[:, cols]
            smax = jnp.max(s, axis=0) if bias is not None else s_ref[npart + part][...]
            m_cur = jnp.max(smax, axis=0, keepdims=True)
            m_new = jnp.maximum(m_prev, m_cur)
            alpha = jnp.exp2(m_prev - m_new)
            p = jnp.exp2(s - m_new[None])
            pv = _dot(vt, p.reshape(blk, wq).astype(BF16))
            acc_ref[:, cols] = acc_ref[:, cols] * alpha[0:1] + pv
            m_ref[:, cols] = m_new

    nfar = jnp.maximum(i - 1, 0)
    odd = lax.rem(nfar, 2)

    @pl.when(i == 0)
    def _():
        scores(0, sb_ref)

    @pl.when(i > 0)
    def _():
        @pl.when(odd == 1)
        def _():
            scores(0, sb_ref)
            scores(1, sa_ref)
            softmax_pv(0, sb_ref, None)

        @pl.when(odd == 0)
        def _():
            scores(0, sa_ref)

        def pair(b):
            scores(b + 1, sb_ref)
            softmax_pv(b, sa_ref, None)
            scores(b + 2, sa_ref)
            softmax_pv(b + 1, sb_ref, None)

        def quad_body(t, carry):
            pair(odd + 4 * t)
            pair(odd + 4 * t + 2)
            return carry

        npairs = nfar // 2
        lax.fori_loop(0, npairs // 2, quad_body, 0)

        @pl.when(lax.rem(npairs, 2) == 1)
        def _():
            pair(odd + 2 * (npairs - 1))
        scores(i, sb_ref)
        softmax_pv(i - 1, sa_ref, bias_ref[0, 0])

    softmax_pv(i, sb_ref, bias_ref[0, 1])

    ot = acc_ref[0:dv, 0:nq] / acc_ref[dv:dv + 1, 0:nq]
    o = ot[:, 0:blk].T - lam_ref[...] * ot[:, blk:nq].T
    o = o * lax.rsqrt(jnp.mean(o * o, axis=-1, keepdims=True) + EPS) * g_ref[...]
    o_ref[...] = (o * out_scale).astype(o_ref.dtype)


_TOEPLITZ_ROWS = 256
_TOEPLITZ_N = 2048


def _toeplitz_kernel(v_ref, o_ref, *, keep):
    rows, cols = o_ref.shape[2:]
    x = jnp.broadcast_to(v_ref[0, 0], (rows, v_ref.shape[-1]))
    tile = pltpu.roll(x, 0, 1, stride=1, stride_axis=0)[:, :cols]
    r = lax.broadcasted_iota(jnp.int32, (rows, cols), 0) + pl.program_id(1) * rows
    c = lax.broadcasted_iota(jnp.int32, (rows, cols), 1)
    for variant in range(o_ref.shape[0]):
        o_ref[variant, 0] = jnp.where(keep(r, c, variant), tile, NEG_INF)


def _toeplitz_tiles(fn, keep, heads, rows, cols, variants=1):
    n, rb = _TOEPLITZ_N, _TOEPLITZ_ROWS
    assert rows % rb == 0 and rows <= n // 2 and cols <= n // 2
    idx = jnp.arange(n, dtype=jnp.int32)
    vec = fn(jnp.where(idx < n // 2, idx, idx - n)).astype(F32)
    vecs = jnp.stack([jnp.roll(vec, k * rb, axis=1) for k in range(rows // rb)], axis=1)
    return pl.pallas_call(
        functools.partial(_toeplitz_kernel, keep=keep),
        grid=(heads, rows // rb),
        in_specs=[pl.BlockSpec((1, 1, 1, n), lambda h, k: (h, k, 0, 0))],
        out_specs=pl.BlockSpec((variants, 1, rb, cols), lambda h, k: (0, h, k, 0)),
        out_shape=jax.ShapeDtypeStruct((variants, heads, rows, cols), F32),
        name="toeplitz_tiles",
    )(vecs.reshape(heads, rows // rb, 1, n))


def _t5_bucket(rel):
    nb = NUM_BUCKETS // 2
    max_exact = nb // 2
    bucket = jnp.where(rel > 0, nb, 0)
    n = jnp.abs(rel)
    nf = jnp.maximum(n, 1).astype(F32)
    large = max_exact + (jnp.log(nf / max_exact) / math.log(MAX_DISTANCE / max_exact)
                         * (nb - max_exact)).astype(jnp.int32)
    large = jnp.minimum(large, nb - 1)
    return bucket + jnp.where(n < max_exact, n, large)


def _diff_bias_tiles(t5_table):
    blk = BLK_A
    table = t5_table.astype(F32)
    far = table[_t5_bucket(jnp.full((), -(blk + 1), jnp.int32))]
    def visible(r, c, variant):
        return jnp.floor_divide(r - blk, CHUNK) <= jnp.floor_divide(c, CHUNK)

    tiles = _toeplitz_tiles(lambda x: ((table[_t5_bucket(-x - blk)] - far) * LOG2E).T, visible,
                            N_HEADS_A, 2 * blk, blk)
    return tiles.reshape(N_HEADS_A, 2, blk, blk)


def _diff_attention(proj, t5_table, lam, subln_g, lam_init):
    seq = proj.shape[0]
    blk = BLK_A
    bias = _diff_bias_tiles(t5_table)
    ha = N_HEADS_A
    kern = functools.partial(_diffattn_kernel, out_scale=1.0 - lam_init)
    return pl.pallas_call(
        kern,
        grid=(ha, seq // blk),
        in_specs=[
            pl.BlockSpec((blk, DV_A), lambda h, i: (i, h)),
            pl.BlockSpec((seq, DV_A), lambda h, i: (0, ha + h)),
            pl.BlockSpec((seq, DV_A), lambda h, i: (0, 2 * ha + h)),
            pl.BlockSpec((1, 2, blk, blk), lambda h, i: (h, 0, 0, 0)),
            pl.BlockSpec((1, DV_A), lambda h, i: (0, 0)),
            pl.BlockSpec((1, DV_A), lambda h, i: (0, 0)),
        ],
        out_specs=pl.BlockSpec((blk, DV_A), lambda h, i: (i, h)),
        out_shape=jax.ShapeDtypeStruct((seq, ha * DV_A), BF16),
        scratch_shapes=[
            pltpu.VMEM((DV_A, 2 * blk), BF16),
            pltpu.VMEM((DV_A + ONES_A, seq), BF16),
            pltpu.VMEM((SUBLANES, 2 * blk), F32),
            pltpu.VMEM((DV_A + ONES_A, 2 * blk), F32),
        ] + 2 * ([pltpu.VMEM((blk, 2 * blk // NPART_A + SCORE_PAD), F32)] * NPART_A
                 + [pltpu.VMEM((SUBLANES, 2 * blk // NPART_A), F32)] * NPART_A),
        compiler_params=pltpu.CompilerParams(dimension_semantics=("parallel", "arbitrary")),
        name="diff_attention",
    )(proj, proj, proj, bias, jnp.full((1, DV_A), lam, F32), subln_g.reshape(1, DV_A).astype(F32))


def _band_kernel(q_ref, kp_ref, kc_ref, vp_ref, vc_ref, *refs):
    qw, band = QW_B, BAND_B
    nbias = band // qw + 1
    bias_refs, o_ref, s_refs = refs[:nbias], refs[nbias], refs[nbias + 1:]
    nk = band + qw
    sub = SUBLANES
    q = q_ref[...].astype(F32) * (DH_B ** -0.5 * LOG2E)
    lane = lax.broadcasted_iota(jnp.int32, q.shape, 1)
    qh = (jnp.where(lane < DH_B, q, 0.0).T.astype(BF16), jnp.where(lane >= DH_B, q, 0.0).T.astype(BF16))
    k_all = jnp.concatenate([kp_ref[...], kc_ref[...]], axis=0)
    vt_all = jnp.concatenate([vp_ref[...], vc_ref[...]], axis=0).astype(F32).T.astype(BF16)
    vt_all = jnp.concatenate([vt_all, jnp.ones((ONES_A, vt_all.shape[1]), BF16)], axis=0)
    ngroups = len(s_refs)
    for g in range(ngroups):
        k0 = g * qw
        qs = jnp.concatenate([qh[0][:, k0:k0 + qw], qh[1][:, k0:k0 + qw]], axis=1)
        s_refs[g][:, 0:2 * qw] = _dot(k_all[k0:k0 + nk], qs)
    for g in range(ngroups):
        k0 = g * qw
        bias_ref = bias_refs[min(g, nbias - 1)]
        bias = jnp.concatenate([bias_ref[0, 0], bias_ref[0, 1]], axis=1)
        s = (s_refs[g][:, 0:2 * qw] + bias).reshape(nk // sub, sub, 2 * qw)
        m = jnp.max(jnp.max(s, axis=0), axis=0, keepdims=True)
        p = jnp.exp2(s - m[None])
        pv = _dot(vt_all[:, k0:k0 + nk], p.reshape(nk, 2 * qw).astype(BF16))
        ot = pv[0:2 * DH_B] / pv[2 * DH_B:2 * DH_B + 1]
        o = jnp.concatenate([ot[0:DH_B, 0:qw], ot[DH_B:2 * DH_B, qw:2 * qw]], axis=0)
        o_ref[k0:k0 + qw, :] = o.T.astype(o_ref.dtype)


def _band_bias_tiles(rel_bias):
    band = BAND_B

    def valid(r, c, variant):
        qchunk = jnp.floor_divide(c, CHUNK)
        kchunk = jnp.floor_divide(r - band, CHUNK)
        missing = jnp.where(variant == 0, 0, band - (variant - 1) * QW_B)
        return (kchunk <= qchunk) & (kchunk >= qchunk - LEFT_CHUNKS) & (r >= missing)

    return _toeplitz_tiles(
        lambda x: rel_bias.astype(F32)[:, jnp.clip(-x - band, -REL_CLIP, REL_CLIP) + REL_CLIP] * LOG2E, valid,
        N_HEADS_B, band + QW_B, QW_B, variants=1 + band // QW_B)


def _band_attention(proj, rel_bias):
    seq = proj.shape[0]
    blk, band, qw = BLK_B, BAND_B, QW_B
    bias = _band_bias_tiles(rel_bias)
    npair = N_HEADS_B // 2
    qc0 = 3 * N_HEADS_A
    per = blk // band
    prev = lambda c0: (lambda hp, i: (jnp.maximum(i * per - 1, 0), c0 + hp))
    cur = lambda c0: (lambda hp, i: (i, c0 + hp))
    return pl.pallas_call(
        _band_kernel,
        grid=(npair, seq // blk),
        in_specs=[
            pl.BlockSpec((blk, LANES), cur(qc0)),
            pl.BlockSpec((band, LANES), prev(qc0 + npair)),
            pl.BlockSpec((blk, LANES), cur(qc0 + npair)),
            pl.BlockSpec((band, LANES), prev(qc0 + 2 * npair)),
            pl.BlockSpec((blk, LANES), cur(qc0 + 2 * npair)),
        ] + [
            pl.BlockSpec((1, 2, band + qw, qw), (lambda hp, i, t=t: (jnp.where(i == 0, 1 + t, 0), hp, 0, 0)))
            for t in range(band // qw)
        ] + [
            pl.BlockSpec((1, 2, band + qw, qw), lambda hp, i: (0, hp, 0, 0)),
        ],
        out_specs=pl.BlockSpec((blk, LANES), lambda hp, i: (i, hp)),
        out_shape=jax.ShapeDtypeStruct((seq, N_HEADS_B * DH_B), BF16),
        scratch_shapes=[pltpu.VMEM((band + qw, 2 * qw + SCORE_PAD), F32)] * (blk // qw),
        compiler_params=pltpu.CompilerParams(dimension_semantics=("parallel", "arbitrary")),
        name="band_attention",
    )(proj, proj, proj, proj, proj, *([bias] * bias.shape[0]))


def _retention_kernel(qk_ref, v_ref, gate_ref, cos_ref, sin_ref, qdec_ref, kdec_ref, dmat_ref,
                      sdec_ref, o_ref, state_ref):
    @pl.when(pl.program_id(0) == 0)
    def _():
        state_ref[...] = jnp.zeros(state_ref.shape, F32)

    cos = cos_ref[...]
    sin = sin_ref[...]
    lane = lax.broadcasted_iota(jnp.int32, cos.shape, 1)
    first_half = (lane % DQK_C) < (DQK_C // 2)
    qk = qk_ref[...]
    parts = []
    for j in range(qk.shape[1] // LANES):
        t = qk[:, j * LANES:(j + 1) * LANES]
        partner = jnp.where(first_half, pltpu.roll(t, LANES - DQK_C // 2, 1), pltpu.roll(t, DQK_C // 2, 1))
        parts.append(t * cos + partner * sin)
    wq = N_HEADS_C * DQK_C
    q = jnp.concatenate(parts[:wq // LANES], axis=1)
    k = jnp.concatenate(parts[wq // LANES:], axis=1) * (DQK_C ** -0.5)
    qd = (q * qdec_ref[...]).astype(BF16)
    kd = (k * kdec_ref[...]).astype(BF16)
    qb = q.astype(BF16)
    kb = k.astype(BF16)
    vb = v_ref[...].astype(BF16)
    gate = gate_ref[...]
    outs = []
    for h in range(N_HEADS_C):
        qs = slice(h * DQK_C, (h + 1) * DQK_C)
        vs = slice(h * DV_C, (h + 1) * DV_C)
        scores = _dot_nt(qb[:, qs], kb[:, qs]) * dmat_ref[h]
        state = state_ref[h]
        r = _dot(scores.astype(BF16), vb[:, vs]) + _dot(qd[:, qs], state.astype(BF16))
        state_ref[h] = state * sdec_ref[h] + _dot_tn(kd[:, qs], vb[:, vs])
        r = r * lax.rsqrt(jnp.mean(r * r, axis=-1, keepdims=True) + EPS)
        g = gate[:, vs]
        outs.append(r * (g * jax.nn.sigmoid(g)))
    o_ref[...] = jnp.concatenate(outs, axis=1).astype(o_ref.dtype)


def _retention_tables(seq):
    t = BLK_C
    half = DQK_C // 2
    inv_freq = 1.0 / (ROPE_BASE ** (jnp.arange(0, DQK_C, 2, dtype=F32) / DQK_C))
    ang = jnp.arange(seq, dtype=F32)[:, None] * inv_freq[None, :]
    reps = LANES // half
    cos = jnp.tile(jnp.cos(ang), (1, reps))
    sign = jnp.where((jnp.arange(LANES) % DQK_C) < half, -1.0, 1.0).astype(F32)
    sin = jnp.tile(jnp.sin(ang), (1, reps)) * sign[None, :]
    log_g = jnp.log(1.0 - jnp.power(2.0, -5.0 - jnp.arange(N_HEADS_C, dtype=F32)))
    pos = jnp.arange(t, dtype=F32)
    diff = pos[:, None] - pos[None, :]
    same_or_past = (jnp.arange(t)[None, :] // CHUNK) <= (jnp.arange(t)[:, None] // CHUNK)
    dmat = jnp.where(same_or_past[None], jnp.exp(log_g[:, None, None] * jnp.abs(diff)[None]), 0.0)
    qdec = jnp.repeat(jnp.exp(log_g[None, :] * (pos[:, None] + 1.0)), DQK_C, axis=1)
    kdec = jnp.repeat(jnp.exp(log_g[None, :] * (t - 1.0 - pos[:, None])), DQK_C, axis=1)
    sdec = jnp.broadcast_to(jnp.exp(log_g * t)[:, None, None], (N_HEADS_C, 1, DV_C))
    return cos, sin, qdec, kdec, dmat, sdec


def _retention(proj):
    seq = proj.shape[0]
    t = BLK_C
    cos, sin, qdec, kdec, dmat, sdec = _retention_tables(seq)
    wv = N_HEADS_C * DV_C
    return pl.pallas_call(
        _retention_kernel,
        grid=(seq // t,),
        in_specs=[
            pl.BlockSpec((t, wv), lambda i: (i, 0)),
            pl.BlockSpec((t, wv), lambda i: (i, 1)),
            pl.BlockSpec((t, wv), lambda i: (i, 2)),
            pl.BlockSpec((t, LANES), lambda i: (i, 0)),
            pl.BlockSpec((t, LANES), lambda i: (i, 0)),
            pl.BlockSpec((t, N_HEADS_C * DQK_C), lambda i: (0, 0)),
            pl.BlockSpec((t, N_HEADS_C * DQK_C), lambda i: (0, 0)),
            pl.BlockSpec((N_HEADS_C, t, t), lambda i: (0, 0, 0)),
            pl.BlockSpec((N_HEADS_C, 1, DV_C), lambda i: (0, 0, 0)),
        ],
        out_specs=pl.BlockSpec((t, wv), lambda i: (i, 0)),
        out_shape=jax.ShapeDtypeStruct((seq, wv), BF16),
        scratch_shapes=[pltpu.VMEM((N_HEADS_C, DQK_C, DV_C), F32)],
        compiler_params=pltpu.CompilerParams(dimension_semantics=("arbitrary",)),
        name="retention",
    )(proj, proj, proj, cos, sin, qdec, kdec, dmat, sdec)


def _s5_kernel(*refs):
    ncb = S5_CH // LANES
    u_refs = refs[:ncb]
    (mt_ref, bt_ref, ctr_ref, cti_ref, are_ref, aim_ref, y_ref,
     ut_ref, yt_ref, ys_ref, vr_ref, vi_ref, spr_ref, spi_ref, carry_ref) = refs[ncb:]
    tc = S5_TC
    gp = S5_GROUP
    n = S5_STATE
    ng = S5_GROUPS

    @pl.when(pl.program_id(0) == 0)
    def _():
        carry_ref[...] = jnp.zeros(carry_ref.shape, F32)

    for s in range(S5_T):
        for k in range(ncb):
            ut_ref[s, k * LANES:(k + 1) * LANES, :] = u_refs[k][pl.ds(s, tc, stride=S5_T), :].T

    unroll = 4

    def intra(it, carry):
        for k in range(unroll):
            g = it * unroll + k
            r0 = pl.multiple_of(g * gp, gp)
            ug = ut_ref[:, pl.ds(r0, gp), :].reshape(S5_T * gp, tc).astype(BF16)
            yt_ref[:, pl.ds(r0, gp), :] = _dot(mt_ref[g], ug).reshape(S5_T, gp, tc)
            vt = _dot(bt_ref[g], ug)
            n0 = pl.multiple_of(g * n, n)
            vr_ref[pl.ds(n0, n), :] = vt[0:n]
            vi_ref[pl.ds(n0, n), :] = vt[n:2 * n]
        return carry

    lax.fori_loop(0, ng // unroll, intra, 0)

    sub = SUBLANES
    nv = tc // sub
    row = lax.broadcasted_iota(jnp.int32, (tc, LANES), 0)
    in_vreg = lax.rem(row, sub)

    def rows_of(v, r):
        return jnp.broadcast_to(v[r:r + 1], (tc, LANES))

    for j in range(ng * n // LANES):
        cols = slice(j * LANES, (j + 1) * LANES)
        pwr, pwi = are_ref[:, cols], aim_ref[:, cols]
        xr = vr_ref[cols, :].T
        xi = vi_ref[cols, :].T
        for d in (1, 2, 4):
            keep = in_vreg >= d
            sr = jnp.where(keep, pltpu.roll(xr, d, 0), 0.0)
            si = jnp.where(keep, pltpu.roll(xi, d, 0), 0.0)
            fr, fi = rows_of(pwr, d - 1), rows_of(pwi, d - 1)
            xr, xi = xr + (fr * sr - fi * si), xi + (fr * si + fi * sr)
        cr, ci = carry_ref[0, :, cols], carry_ref[1, :, cols]
        cr0, ci0 = cr, ci
        outr, outi = [], []
        for v in range(nv):
            yr = xr[v * sub:(v + 1) * sub] + (pwr * cr - pwi * ci)
            yi = xi[v * sub:(v + 1) * sub] + (pwr * ci + pwi * cr)
            outr.append(yr)
            outi.append(yi)
            cr = jnp.broadcast_to(yr[sub - 1:sub], (sub, LANES))
            ci = jnp.broadcast_to(yi[sub - 1:sub], (sub, LANES))
        carry_ref[0, :, cols] = cr
        carry_ref[1, :, cols] = ci
        sr = jnp.concatenate(outr, axis=0)
        si = jnp.concatenate(outi, axis=0)
        first = row == 0
        spr_ref[j] = jnp.where(first, rows_of(cr0, 0), pltpu.roll(sr, 1, 0))
        spi_ref[j] = jnp.where(first, rows_of(ci0, 0), pltpu.roll(si, 1, 0))

    def cross(it, carry):
        for k in range(unroll):
            jp = it * unroll + k
            r0 = pl.multiple_of(jp * 2 * gp, 2 * gp)
            yc = (_dot_nt(ctr_ref[jp], spr_ref[jp].astype(BF16))
                  + _dot_nt(cti_ref[jp], spi_ref[jp].astype(BF16)))
            yt_ref[:, pl.ds(r0, 2 * gp), :] += yc.reshape(S5_T, 2 * gp, tc)
        return carry

    lax.fori_loop(0, ng // 2 // unroll, cross, 0)

    for s in range(S5_T):
        for k in range(ncb):
            ys_ref[k, pl.ds(s, tc, stride=S5_T), :] = yt_ref[s, k * LANES:(k + 1) * LANES, :].T
    for k in range(ncb):
        y_ref[:, k * LANES:(k + 1) * LANES] = ys_ref[k]


def _s5_matrices(lam_re, lam_im, log_step, b_re, b_im, c_re, c_im, d_skip):
    hi = lax.Precision.HIGHEST
    t, gp, n, ng = S5_T, S5_GROUP, S5_STATE, S5_GROUPS
    lam = lax.complex(lam_re.astype(F32), lam_im.astype(F32))
    step = jnp.exp(log_step.astype(F32))[:, None]
    ls = lam * step
    a_bar = jnp.exp(ls)
    b_bar = ((a_bar - 1.0) / lam)[..., None] * lax.complex(b_re.astype(F32), b_im.astype(F32))
    cm = lax.complex(c_re.astype(F32), c_im.astype(F32))

    def apow(k):
        kk = k.astype(F32).astype(jnp.complex64)
        return jnp.exp(ls.reshape((ng,) + (1,) * k.ndim + (n,)) * kk[None, ..., None])

    tt = jnp.arange(t)
    kmat = jnp.einsum('gpn,gln,gnq->glpq', cm, apow(tt), b_bar, precision=hi).real
    krev = jnp.transpose(kmat[:, ::-1], (0, 2, 1, 3)).reshape(ng, gp, t * gp)
    kpad = jnp.pad(krev, ((0, 0), (0, 0), (0, t * gp)))
    mt = jnp.concatenate([kpad[:, :, (t - 1 - to) * gp:(2 * t - 1 - to) * gp] for to in range(t)], axis=1)
    dvec = jnp.tile(d_skip.astype(F32).reshape(ng, 1, gp), (1, t, 1)).reshape(ng, t * gp)
    mt = mt + jnp.eye(t * gp, dtype=F32)[None] * dvec[:, :, None]
    z = jnp.swapaxes(apow(t - 1 - tt), 1, 2)[:, :, :, None] * b_bar[:, :, None, :]
    z = z.reshape(ng, n, t * gp)
    bt = jnp.concatenate([z.real, z.imag], axis=1)
    w = cm[:, None, :, :] * apow(tt + 1)[:, :, None, :]

    def pair_readout(x):
        x = x.reshape(ng // 2, 2, t, gp, n)
        first = jnp.pad(x[:, 0], ((0, 0), (0, 0), (0, 0), (0, n)))
        second = jnp.pad(x[:, 1], ((0, 0), (0, 0), (0, 0), (n, 0)))
        return jnp.stack([first, second], axis=2).reshape(ng // 2, t * 2 * gp, 2 * n).astype(BF16)

    ctr, cti = pair_readout(w.real), pair_readout(-w.imag)
    a_chunk = jnp.transpose(apow(t * (jnp.arange(SUBLANES) + 1)), (1, 0, 2)).reshape(SUBLANES, ng * n)
    return mt.astype(BF16), bt.astype(BF16), ctr, cti, a_chunk.real, a_chunk.imag


def _s5(proj, mats):
    seq, width = proj.shape
    t, tc, gp, n, ng = S5_T, S5_TC, S5_GROUP, S5_STATE, S5_GROUPS
    rows = t * tc
    ncb = S5_CH // LANES
    cb0 = (width - S5_CH) // LANES
    u_specs = [pl.BlockSpec((rows, LANES), (lambda i, k=k: (i, cb0 + k))) for k in range(ncb)]
    nsb = ng * n // LANES
    return pl.pallas_call(
        _s5_kernel,
        grid=(seq // rows,),
        in_specs=u_specs + [_const_spec(m.shape) for m in mats],
        out_specs=pl.BlockSpec((rows, S5_CH), lambda i: (i, 0)),
        out_shape=jax.ShapeDtypeStruct((seq, S5_CH), F32),
        scratch_shapes=[
            pltpu.VMEM((t, S5_CH, tc), F32),
            pltpu.VMEM((t, S5_CH, tc), F32),
            pltpu.VMEM((ncb, rows, LANES), F32),
            pltpu.VMEM((ng * n, tc), F32),
            pltpu.VMEM((ng * n, tc), F32),
            pltpu.VMEM((nsb, tc, LANES), F32),
            pltpu.VMEM((nsb, tc, LANES), F32),
            pltpu.VMEM((2, SUBLANES, ng * n), F32),
        ],
        compiler_params=pltpu.CompilerParams(dimension_semantics=("arbitrary",)),
        name="s5_scan",
    )(*([proj] * ncb), *mats)


def _mix_ffn_kernel(*refs, glu, final):
    (x_ref, a_ref, b_ref, wo_ref, g1_ref), refs = refs[:5], refs[5:]
    if glu:
        gw_ref, refs = refs[0], refs[1:]
    (g_ref, sc_ref, sh_ref, gate_ref, win_ref, cw_ref, cb_ref, wout_ref), refs = refs[:8], refs[8:]
    if final:
        fg_ref, o_ref, h_ref, act_ref, gbuf_ref, carry_ref = refs
    else:
        ng_ref, nsc_ref, nsh_ref, o_ref, hn_ref, h_ref, act_ref, gbuf_ref, carry_ref = refs
    tm = x_ref.shape[0]
    halo = gbuf_ref.shape[0] - tm

    @pl.when(pl.program_id(0) == 0)
    def _():
        carry_ref[...] = jnp.zeros(carry_ref.shape, F32)

    if glu:
        y = jax.nn.gelu(b_ref[...]).astype(BF16)
        gg = _dot(y, gw_ref[...])
        half = gg.shape[1] // 2
        b = (gg[:, :half] * jax.nn.sigmoid(gg[:, half:])).astype(BF16)
    else:
        b = b_ref[...]
    cat = jnp.concatenate([a_ref[...], b], axis=1)
    x = x_ref[...] + g1_ref[...] * _dot(cat, wo_ref[...])
    h_ref[...] = _mod_rmsnorm(x, g_ref[...], sc_ref[...], sh_ref[...]).astype(BF16)
    for f in range(D_FF // TF_FFN):
        cs = slice(f * TF_FFN, (f + 1) * TF_FFN)
        gs = slice(D_FF + f * TF_FFN, D_FF + (f + 1) * TF_FFN)
        h = h_ref[...]
        val = _dot(h, win_ref[:, cs])
        gate = _dot(h, win_ref[:, gs])
        gbuf_ref[0:halo, :] = carry_ref[:, cs]
        gbuf_ref[halo:halo + tm, :] = gate
        carry_ref[:, cs] = gate[tm - halo:tm, :]
        conv = (gate * cw_ref[2:3, cs] + gbuf_ref[halo - 1:halo - 1 + tm, :] * cw_ref[1:2, cs]
                + gbuf_ref[halo - 2:halo - 2 + tm, :] * cw_ref[0:1, cs] + cb_ref[:, cs])
        act_ref[:, cs] = (jax.nn.gelu(conv) * val).astype(BF16)
    xn = x + gate_ref[...] * _dot(act_ref[...], wout_ref[...])
    if final:
        xn = xn * lax.rsqrt(jnp.mean(xn * xn, axis=-1, keepdims=True) + EPS) * fg_ref[...]
    else:
        hn_ref[...] = _mod_rmsnorm(xn, ng_ref[...], nsc_ref[...], nsh_ref[...]).astype(BF16)
    o_ref[...] = xn


def _layer_spec(shape, layer):
    idx = (layer,) + (0,) * (len(shape) - 1)
    return pl.BlockSpec((None,) + tuple(shape[1:]), lambda *_: idx, pipeline_mode=pl.Buffered(1))


def _mix_ffn(x, a, b, wo, gate1, glu_w, g, scale, shift, gate2, w_in, conv_w, conv_b, w_out, tail, layer):
    seq, d = x.shape
    final = len(tail) == 1
    tm = TM_FFN
    halo = SUBLANES
    row = pl.BlockSpec((1, d), lambda i: (0, 0))
    rows = lambda w: pl.BlockSpec((tm, w), lambda i: (i, 0))
    conv_b = conv_b.reshape(conv_b.shape[0], 1, D_FF)
    in_specs = [rows(d), rows(a.shape[1]), rows(b.shape[1]), _const_spec(wo.shape), row]
    args = [x, a, b, wo, gate1]
    if glu_w is not None:
        in_specs.append(_const_spec(glu_w.shape))
        args.append(glu_w)
    in_specs += [
        row, row, row, row,
        _layer_spec(w_in.shape, layer),
        _layer_spec(conv_w.shape, layer),
        _layer_spec(conv_b.shape, layer),
        _layer_spec(w_out.shape, layer),
    ] + [row] * len(tail)
    args += [g.reshape(1, d), scale, shift, gate2, w_in, conv_w, conv_b, w_out]
    args += [t.reshape(1, d) for t in tail]
    out_specs = [rows(d)] if final else [rows(d), rows(d)]
    out_shape = [jax.ShapeDtypeStruct((seq, d), F32)] + ([] if final else [jax.ShapeDtypeStruct((seq, d), BF16)])
    return pl.pallas_call(
        functools.partial(_mix_ffn_kernel, glu=glu_w is not None, final=final),
        grid=(seq // tm,),
        in_specs=in_specs,
        out_specs=out_specs,
        out_shape=out_shape,
        scratch_shapes=[
            pltpu.VMEM((tm, d), BF16),
            pltpu.VMEM((tm, D_FF), BF16),
            pltpu.VMEM((tm + halo, TF_FFN), F32),
            pltpu.VMEM((halo, D_FF), F32),
        ],
        compiler_params=pltpu.CompilerParams(dimension_semantics=("arbitrary",)),
        name="mix_ffn",
    )(*args)


def kernel(x, c, t5_table, mod_w, mod_b, norm1_g, norm2_g, ffn_w_in, ffn_conv_w, ffn_conv_b, ffn_w_out,
           ev_w_in, ev_w_out, diff_lambda, diff_subln_g, band_rel_bias,
           od_w_in, od_w_out, s5_lam_re, s5_lam_im, s5_log_step, s5_b_re, s5_b_im, s5_c_re, s5_c_im,
           s5_d, s5_glu_w, final_g):
    assert x.shape[0] == 1 and x.shape[2] == D_MODEL
    seq = x.shape[1]
    assert seq % TM_PROJ == 0 and seq % (S5_T * S5_TC) == 0
    d = D_MODEL
    xs = x[0]
    mod = _modulation(c, mod_w, mod_b)
    ffn_w_in_b = ffn_w_in.astype(BF16)
    ffn_w_out_b = ffn_w_out.astype(BF16)
    mods = [[mod[i, :, k * d:(k + 1) * d] for k in range(6)] for i in range(DEPTH)]
    h = None
    for i in range(DEPTH):
        sh1, sc1, g1, sh2, sc2, g2 = mods[i]
        w_in = (ev_w_in if i % 2 == 0 else od_w_in)[i // 2].astype(BF16)
        proj_dtype = BF16 if i % 2 == 0 else F32
        if h is None:
            proj = _normproj(xs, norm1_g[i], sc1, sh1, w_in, proj_dtype)
        else:
            proj = _proj(h, w_in, proj_dtype)
        if i % 2 == 0:
            e = i // 2
            lam_init = 0.8 - 0.6 * math.exp(-0.3 * i)
            lp = diff_lambda[e].astype(F32)
            lam = jnp.exp(jnp.sum(lp[0] * lp[1])) - jnp.exp(jnp.sum(lp[2] * lp[3])) + lam_init
            mix_a = _diff_attention(proj, t5_table, lam, diff_subln_g[e], lam_init)
            mix_b = _band_attention(proj, band_rel_bias[e])
            wo, glu_w = ev_w_out[e].astype(BF16), None
        else:
            o = i // 2
            mix_a = _retention(proj)
            mats = _s5_matrices(s5_lam_re[o], s5_lam_im[o], s5_log_step[o], s5_b_re[o], s5_b_im[o],
                                s5_c_re[o], s5_c_im[o], s5_d[o])
            mix_b = _s5(proj, mats)
            wo, glu_w = od_w_out[o].astype(BF16), s5_glu_w[o].astype(BF16)
        if i == DEPTH - 1:
            tail = (final_g,)
        else:
            nsh1, nsc1 = mods[i + 1][0], mods[i + 1][1]
            tail = (norm1_g[i + 1], nsc1, nsh1)
        out = _mix_ffn(xs, mix_a, mix_b, wo, g1, glu_w, norm2_g[i], sc2, sh2, g2,
                       ffn_w_in_b, ffn_conv_w, ffn_conv_b, ffn_w_out_b, tail, layer=i)
        if i == DEPTH - 1:
            xs = out[0]
        else:
            xs, h = out
    return xs[None]
```

```python
import functools
import math

import jax
import jax.numpy as jnp
from jax import lax
from jax.experimental import pallas as pl
from jax.experimental.pallas import tpu as pltpu

F32 = jnp.float32
BF16 = jnp.bfloat16

D_MODEL = 1024
DEPTH = 2
CHUNK = 64
GROUP_WIDTH = D_MODEL // 2
DK_A = 64
DV_A = 2 * DK_A
N_HEADS_A = GROUP_WIDTH // DV_A
DH_B = 64
N_HEADS_B = GROUP_WIDTH // DH_B
LEFT_CHUNKS = 8
REL_CLIP = 2 * CHUNK
NUM_BUCKETS = 32
MAX_DISTANCE = 128
DV_C = 128
DQK_C = DV_C // 2
N_HEADS_C = GROUP_WIDTH // DV_C
ROPE_BASE = 10000.0
S5_CH = GROUP_WIDTH
S5_GROUP = 16
S5_GROUPS = S5_CH // S5_GROUP
S5_STATE = 64
D_FF = ((8 * D_MODEL // 3 + 255) // 256) * 256
CONV_W = 3
EVEN_IN = 3 * N_HEADS_A * DV_A + 3 * N_HEADS_B * DH_B
ODD_IN = 2 * N_HEADS_C * DQK_C + 2 * N_HEADS_C * DV_C + S5_CH
EPS = 1e-6
NEG_INF = -1e30
LOG2E = math.log2(math.e)

LANES = 128
SUBLANES = 8
MXU_DIM = 256

TM_PROJ = 1024
TN_PROJ = 1024
TN_MOD = 1536
TM_FFN = 512
TF_FFN = MXU_DIM
BLK_A = 512
NPART_A = 2
ONES_A = 16
SINGLE_PASS_LOG2_RANGE = 96.0
SCORE_PAD = LANES
BLK_B = 1024
BAND_B = LEFT_CHUNKS * CHUNK
QW_B = 4 * CHUNK
BLK_C = 512
S5_T = 16
S5_TC = LANES

assert BLK_B % BAND_B == 0 and BLK_B % QW_B == 0 and BAND_B % QW_B == 0
assert BLK_A >= MAX_DISTANCE, "far key blocks must sit in the saturated T5 bucket"
assert DV_A == LANES and 2 * DK_A == LANES and 2 * DH_B == LANES, "attention heads are read as 128-lane column blocks"


def _dot(a, b):
    return jnp.dot(a, b, preferred_element_type=F32)


def _dot_nt(a, b):
    return lax.dot_general(a, b, (((1,), (1,)), ((), ())), preferred_element_type=F32)


def _dot_tn(a, b):
    return lax.dot_general(a, b, (((0,), (0,)), ((), ())), preferred_element_type=F32)


def _const_spec(shape):
    zeros = (0,) * len(shape)
    return pl.BlockSpec(shape, lambda *_: zeros, pipeline_mode=pl.Buffered(1))


def _mod_rmsnorm(x, g, scale, shift):
    y = x * lax.rsqrt(jnp.mean(x * x, axis=-1, keepdims=True) + EPS)
    y = y * g
    return y * (1.0 + scale) + shift


def _mod_kernel(c_ref, w_ref, b_ref, o_ref):
    c = c_ref[...]
    cond = c * jax.nn.sigmoid(c)
    o_ref[0] = jnp.sum(cond * w_ref[0], axis=0, keepdims=True) + b_ref[0]


def _modulation(c, mod_w, mod_b):
    depth, d, n = mod_w.shape
    tn = TN_MOD
    return pl.pallas_call(
        _mod_kernel,
        grid=(depth, n // tn),
        in_specs=[
            pl.BlockSpec((d, 1), lambda i, j: (0, 0)),
            pl.BlockSpec((1, d, tn), lambda i, j: (i, 0, j)),
            pl.BlockSpec((1, 1, tn), lambda i, j: (i, 0, j)),
        ],
        out_specs=pl.BlockSpec((1, 1, tn), lambda i, j: (i, 0, j)),
        out_shape=jax.ShapeDtypeStruct((depth, 1, n), F32),
        name="modulation",
    )(c.reshape(d, 1), mod_w, mod_b.reshape(depth, 1, n))


def _normproj_kernel(x_ref, g_ref, sc_ref, sh_ref, w_ref, o_ref):
    tm, n = o_ref.shape
    half = tm // 2
    for r in range(2):
        rows = slice(r * half, (r + 1) * half)
        h = _mod_rmsnorm(x_ref[rows, :], g_ref[...], sc_ref[...], sh_ref[...]).astype(BF16)
        for j in range(n // TN_PROJ):
            cols = slice(j * TN_PROJ, (j + 1) * TN_PROJ)
            o_ref[rows, cols] = _dot(h, w_ref[:, cols]).astype(o_ref.dtype)


def _normproj(x, g, scale, shift, w, out_dtype):
    seq, d = x.shape
    n = w.shape[1]
    tm = TM_PROJ
    row = pl.BlockSpec((1, d), lambda i: (0, 0))
    return pl.pallas_call(
        _normproj_kernel,
        grid=(seq // tm,),
        in_specs=[pl.BlockSpec((tm, d), lambda i: (i, 0)), row, row, row, _const_spec(w.shape)],
        out_specs=pl.BlockSpec((tm, n), lambda i: (i, 0)),
        out_shape=jax.ShapeDtypeStruct((seq, n), out_dtype),
        compiler_params=pltpu.CompilerParams(dimension_semantics=("parallel",)),
        name="normproj",
    )(x, g.reshape(1, d), scale, shift, w)


def _proj_kernel(h_ref, w_ref, o_ref):
    for j in range(o_ref.shape[1] // TN_PROJ):
        cols = slice(j * TN_PROJ, (j + 1) * TN_PROJ)
        o_ref[:, cols] = _dot(h_ref[...], w_ref[:, cols]).astype(o_ref.dtype)


def _proj(h, w, out_dtype):
    seq, d = h.shape
    n = w.shape[1]
    tm = TM_PROJ
    return pl.pallas_call(
        _proj_kernel,
        grid=(seq // tm,),
        in_specs=[pl.BlockSpec((tm, d), lambda i: (i, 0)), _const_spec(w.shape)],
        out_specs=pl.BlockSpec((tm, n), lambda i: (i, 0)),
        out_shape=jax.ShapeDtypeStruct((seq, n), out_dtype),
        compiler_params=pltpu.CompilerParams(dimension_semantics=("parallel",)),
        name="proj",
    )(h, w)


def _diffattn_kernel(q_ref, k_ref, v_ref, bias_ref, bstat_ref, lam_ref, g_ref, o_ref,
                     qs_ref, vt_ref, kmax_ref, r_ref, m_ref, acc_ref, *s_refs, out_scale):
    blk = BLK_A
    nq = 2 * blk
    sub = SUBLANES
    dv = DV_A
    npart = len(s_refs) // 4
    wq = nq // npart
    sa_ref, sb_ref = s_refs[:2 * npart], s_refs[2 * npart:]
    i = pl.program_id(1)
    lane = lax.broadcasted_iota(jnp.int32, (blk, LANES), 1)
    same_subhead = (lax.broadcasted_iota(jnp.int32, (LANES, LANES), 0) // DK_A
                    == lax.broadcasted_iota(jnp.int32, (LANES, LANES), 1) // DK_A).astype(BF16)

    @pl.when(i == 0)
    def _():
        kmax_ref[...] = jnp.zeros(kmax_ref.shape, F32)

        def tr(b, carry):
            r0 = pl.multiple_of(b * blk, blk)
            vt_ref[0:dv, pl.ds(r0, blk)] = v_ref[pl.ds(r0, blk), :].astype(F32).T.astype(BF16)
            vt_ref[dv:dv + ONES_A, pl.ds(r0, blk)] = jnp.ones((ONES_A, blk), BF16)
            kf = k_ref[pl.ds(r0, blk), :].astype(F32)
            kn2 = _dot((kf * kf).astype(BF16), same_subhead)
            kmax_ref[...] = jnp.maximum(kmax_ref[...], jnp.max(kn2.reshape(blk // sub, sub, LANES), axis=0))
            return carry
        lax.fori_loop(0, v_ref.shape[0] // blk, tr, 0)
        kmax_ref[...] = jnp.broadcast_to(jnp.max(kmax_ref[...], axis=0, keepdims=True), kmax_ref.shape)

    q = (q_ref[...].astype(F32) * (DK_A ** -0.5 * LOG2E)).astype(BF16)
    qf = q.astype(F32)
    qs_ref[:, 0:blk] = jnp.where(lane < DK_A, qf, 0.0).T.astype(BF16)
    qs_ref[:, blk:nq] = jnp.where(lane >= DK_A, qf, 0.0).T.astype(BF16)
    acc_ref[...] = jnp.zeros(acc_ref.shape, F32)

    qsq = (qf * qf).astype(BF16)
    bound = []
    for m in range(2):
        lanes_m = (lax.broadcasted_iota(jnp.int32, (sub, LANES), 1) // DK_A == m).astype(BF16)
        qn2 = _dot_nt(lanes_m, qsq)
        bound.append(jnp.sqrt(qn2 * kmax_ref[:, m * DK_A:m * DK_A + 1]) * 1.03)
    bound = jnp.concatenate(bound, axis=1)
    bias_max, bias_span = bstat_ref[0, 0:1, 0:1], bstat_ref[0, 1:2, 0:1]
    r_ref[...] = bound + bias_max
    single_pass = jnp.max(2.0 * bound + bias_span) < SINGLE_PASS_LOG2_RANGE

    def single_pass_block(b, bias):
        r0 = pl.multiple_of(b * blk, blk)
        k = k_ref[pl.ds(r0, blk), :]
        vt = vt_ref[:, pl.ds(r0, blk)]
        for part in range(npart):
            cols = slice(part * wq, (part + 1) * wq)
            s = _dot(k, qs_ref[:, cols])
            if bias is not None:
                q0 = (part * wq) % blk
                s = s + bias[:, q0:q0 + wq]
            p = jnp.exp2(s.reshape(blk // sub, sub, wq) - r_ref[:, cols][None])
            acc_ref[:, cols] += _dot(vt, p.reshape(blk, wq).astype(BF16))

    @pl.when(single_pass)
    def _():
        nfar = jnp.maximum(i - 1, 0)
        unroll = 4

        def far_body(t, carry):
            for u in range(unroll):
                single_pass_block(unroll * t + u, None)
            return carry
        lax.fori_loop(0, nfar // unroll, far_body, 0)

        def far_rest(b, carry):
            single_pass_block(b, None)
            return carry
        lax.fori_loop(unroll * (nfar // unroll), nfar, far_rest, 0)

        @pl.when(i > 0)
        def _():
            single_pass_block(i - 1, bias_ref[0, 0])
        single_pass_block(i, bias_ref[0, 1])

    @pl.when(jnp.logical_not(single_pass))
    def _():
        _diffattn_two_pass(i, k_ref, bias_ref, qs_ref, vt_ref, m_ref, acc_ref, sa_ref, sb_ref)

    ot = acc_ref[0:dv, 0:nq] / acc_ref[dv:dv + 1, 0:nq]
    o = ot[:, 0:blk].T - lam_ref[...] * ot[:, blk:nq].T
    o = o * lax.rsqrt(jnp.mean(o * o, axis=-1, keepdims=True) + EPS) * g_ref[...]
    o_ref[...] = (o * out_scale).astype(o_ref.dtype)


def _diffattn_two_pass(i, k_ref, bias_ref, qs_ref, vt_ref, m_ref, acc_ref, sa_ref, sb_ref):
    blk = BLK_A
    nq = 2 * blk
    sub = SUBLANES
    npart = len(sa_ref) // 2
    wq = nq // npart
    m_ref[...] = jnp.full(m_ref.shape, NEG_INF, F32)

    def scores(b, s_ref):
        k = k_ref[pl.ds(pl.multiple_of(b * blk, blk), blk), :]
        for part in range(npart):
            s = _dot(k, qs_ref[:, part * wq:(part + 1) * wq])
            s_ref[part][:, 0:wq] = s
            s_ref[npart + part][...] = jnp.max(s.reshape(blk // sub, sub, wq), axis=0)

    def softmax_pv(b, s_ref, bias):
        vt = vt_ref[:, pl.ds(pl.multiple_of(b * blk, blk), blk)]
        for part in range(npart):
            cols = slice(part * wq, (part + 1) * wq)
            s = s_ref[part][:, 0:wq]
            if bias is not None:
                q0 = (part * wq) % blk
                s = s + bias[:, q0:q0 + wq]
            s = s.reshape(blk // sub, sub, wq)
            m_prev = m_ref[:, cols]
            smax = jnp.max(s, axis=0) if bias is not None else s_ref[npart + part][...]
            m_cur = jnp.max(smax, axis=0, keepdims=True)
            m_new = jnp.maximum(m_prev, m_cur)
            alpha = jnp.exp2(m_prev - m_new)
            p = jnp.exp2(s - m_new[None])
            pv = _dot(vt, p.reshape(blk, wq).astype(BF16))
            acc_ref[:, cols] = acc_ref[:, cols] * alpha[0:1] + pv
            m_ref[:, cols] = m_new

    nfar = jnp.maximum(i - 1, 0)
    odd = lax.rem(nfar, 2)

    @pl.when(i == 0)
    def _():
        scores(0, sb_ref)

    @pl.when(i > 0)
    def _():
        @pl.when(odd == 1)
        def _():
            scores(0, sb_ref)
            scores(1, sa_ref)
            softmax_pv(0, sb_ref, None)

        @pl.when(odd == 0)
        def _():
            scores(0, sa_ref)

        def pair(b):
            scores(b + 1, sb_ref)
            softmax_pv(b, sa_ref, None)
            scores(b + 2, sa_ref)
            softmax_pv(b + 1, sb_ref, None)

        def quad_body(t, carry):
            pair(odd + 4 * t)
            pair(odd + 4 * t + 2)
            return carry

        npairs = nfar // 2
        lax.fori_loop(0, npairs // 2, quad_body, 0)

        @pl.when(lax.rem(npairs, 2) == 1)
        def _():
            pair(odd + 2 * (npairs - 1))
        scores(i, sb_ref)
        softmax_pv(i - 1, sa_ref, bias_ref[0, 0])

    softmax_pv(i, sb_ref, bias_ref[0, 1])


_TOEPLITZ_ROWS = 256
_TOEPLITZ_N = 2048


def _toeplitz_kernel(v_ref, o_ref, *, keep):
    rows, cols = o_ref.shape[2:]
    x = jnp.broadcast_to(v_ref[0, 0], (rows, v_ref.shape[-1]))
    tile = pltpu.roll(x, 0, 1, stride=1, stride_axis=0)[:, :cols]
    r = lax.broadcasted_iota(jnp.int32, (rows, cols), 0) + pl.program_id(1) * rows
    c = lax.broadcasted_iota(jnp.int32, (rows, cols), 1)
    for variant in range(o_ref.shape[0]):
        o_ref[variant, 0] = jnp.where(keep(r, c, variant), tile, NEG_INF)


def _toeplitz_tiles(fn, keep, heads, rows, cols, variants=1):
    n, rb = _TOEPLITZ_N, _TOEPLITZ_ROWS
    assert rows % rb == 0 and rows <= n // 2 and cols <= n // 2
    idx = jnp.arange(n, dtype=jnp.int32)
    vec = fn(jnp.where(idx < n // 2, idx, idx - n)).astype(F32)
    vecs = jnp.stack([jnp.roll(vec, k * rb, axis=1) for k in range(rows // rb)], axis=1)
    return pl.pallas_call(
        functools.partial(_toeplitz_kernel, keep=keep),
        grid=(heads, rows // rb),
        in_specs=[pl.BlockSpec((1, 1, 1, n), lambda h, k: (h, k, 0, 0))],
        out_specs=pl.BlockSpec((variants, 1, rb, cols), lambda h, k: (0, h, k, 0)),
        out_shape=jax.ShapeDtypeStruct((variants, heads, rows, cols), F32),
        name="toeplitz_tiles",
    )(vecs.reshape(heads, rows // rb, 1, n))


def _t5_bucket(rel):
    nb = NUM_BUCKETS // 2
    max_exact = nb // 2
    bucket = jnp.where(rel > 0, nb, 0)
    n = jnp.abs(rel)
    nf = jnp.maximum(n, 1).astype(F32)
    large = max_exact + (jnp.log(nf / max_exact) / math.log(MAX_DISTANCE / max_exact)
                         * (nb - max_exact)).astype(jnp.int32)
    large = jnp.minimum(large, nb - 1)
    return bucket + jnp.where(n < max_exact, n, large)


def _diff_bias_tiles(t5_table):
    blk = BLK_A
    table = t5_table.astype(F32)
    far = table[_t5_bucket(jnp.full((), -(blk + 1), jnp.int32))]
    def visible(r, c, variant):
        return jnp.floor_divide(r - blk, CHUNK) <= jnp.floor_divide(c, CHUNK)

    tiles = _toeplitz_tiles(lambda x: ((table[_t5_bucket(-x - blk)] - far) * LOG2E).T, visible,
                            N_HEADS_A, 2 * blk, blk)
    return tiles.reshape(N_HEADS_A, 2, blk, blk)


def _diff_attention(proj, t5_table, lam, subln_g, lam_init):
    seq = proj.shape[0]
    blk = BLK_A
    bias = _diff_bias_tiles(t5_table)
    ha = N_HEADS_A
    finite = bias > 0.5 * NEG_INF
    bias_max = jnp.maximum(jnp.max(jnp.where(finite, bias, NEG_INF), axis=(1, 2, 3)), 0.0)
    bias_min = jnp.minimum(jnp.min(jnp.where(finite, bias, -NEG_INF), axis=(1, 2, 3)), 0.0)
    bstat = jnp.broadcast_to(jnp.stack([bias_max, bias_max - bias_min], axis=1)[:, :, None], (ha, 2, LANES))
    kern = functools.partial(_diffattn_kernel, out_scale=1.0 - lam_init)
    return pl.pallas_call(
        kern,
        grid=(ha, seq // blk),
        in_specs=[
            pl.BlockSpec((blk, DV_A), lambda h, i: (i, h)),
            pl.BlockSpec((seq, DV_A), lambda h, i: (0, ha + h)),
            pl.BlockSpec((seq, DV_A), lambda h, i: (0, 2 * ha + h)),
            pl.BlockSpec((1, 2, blk, blk), lambda h, i: (h, 0, 0, 0)),
            pl.BlockSpec((1, 2, LANES), lambda h, i: (h, 0, 0)),
            pl.BlockSpec((1, DV_A), lambda h, i: (0, 0)),
            pl.BlockSpec((1, DV_A), lambda h, i: (0, 0)),
        ],
        out_specs=pl.BlockSpec((blk, DV_A), lambda h, i: (i, h)),
        out_shape=jax.ShapeDtypeStruct((seq, ha * DV_A), BF16),
        scratch_shapes=[
            pltpu.VMEM((DV_A, 2 * blk), BF16),
            pltpu.VMEM((DV_A + ONES_A, seq), BF16),
            pltpu.VMEM((SUBLANES, LANES), F32),
            pltpu.VMEM((SUBLANES, 2 * blk), F32),
            pltpu.VMEM((SUBLANES, 2 * blk), F32),
            pltpu.VMEM((DV_A + ONES_A, 2 * blk), F32),
        ] + 2 * ([pltpu.VMEM((blk, 2 * blk // NPART_A + SCORE_PAD), F32)] * NPART_A
                 + [pltpu.VMEM((SUBLANES, 2 * blk // NPART_A), F32)] * NPART_A),
        compiler_params=pltpu.CompilerParams(dimension_semantics=("parallel", "arbitrary")),
        name="diff_attention",
    )(proj, proj, proj, bias, bstat, jnp.full((1, DV_A), lam, F32), subln_g.reshape(1, DV_A).astype(F32))


def _band_kernel(q_ref, kp_ref, kc_ref, vp_ref, vc_ref, *refs):
    qw, band = QW_B, BAND_B
    nbias = band // qw + 1
    bias_refs, o_ref, s_refs = refs[:nbias], refs[nbias], refs[nbias + 1:]
    nk = band + qw
    sub = SUBLANES
    q = q_ref[...].astype(F32) * (DH_B ** -0.5 * LOG2E)
    lane = lax.broadcasted_iota(jnp.int32, q.shape, 1)
    qh = (jnp.where(lane < DH_B, q, 0.0).T.astype(BF16), jnp.where(lane >= DH_B, q, 0.0).T.astype(BF16))
    k_all = jnp.concatenate([kp_ref[...], kc_ref[...]], axis=0)
    vt_all = jnp.concatenate([vp_ref[...], vc_ref[...]], axis=0).astype(F32).T.astype(BF16)
    vt_all = jnp.concatenate([vt_all, jnp.ones((ONES_A, vt_all.shape[1]), BF16)], axis=0)
    ngroups = len(s_refs)
    for g in range(ngroups):
        k0 = g * qw
        qs = jnp.concatenate([qh[0][:, k0:k0 + qw], qh[1][:, k0:k0 + qw]], axis=1)
        s_refs[g][:, 0:2 * qw] = _dot(k_all[k0:k0 + nk], qs)
    for g in range(ngroups):
        k0 = g * qw
        bias_ref = bias_refs[min(g, nbias - 1)]
        bias = jnp.concatenate([bias_ref[0, 0], bias_ref[0, 1]], axis=1)
        s = (s_refs[g][:, 0:2 * qw] + bias).reshape(nk // sub, sub, 2 * qw)
        m = jnp.max(jnp.max(s, axis=0), axis=0, keepdims=True)
        p = jnp.exp2(s - m[None])
        pv = _dot(vt_all[:, k0:k0 + nk], p.reshape(nk, 2 * qw).astype(BF16))
        ot = pv[0:2 * DH_B] / pv[2 * DH_B:2 * DH_B + 1]
        o = jnp.concatenate([ot[0:DH_B, 0:qw], ot[DH_B:2 * DH_B, qw:2 * qw]], axis=0)
        o_ref[k0:k0 + qw, :] = o.T.astype(o_ref.dtype)


def _band_bias_tiles(rel_bias):
    band = BAND_B

    def valid(r, c, variant):
        qchunk = jnp.floor_divide(c, CHUNK)
        kchunk = jnp.floor_divide(r - band, CHUNK)
        missing = jnp.where(variant == 0, 0, band - (variant - 1) * QW_B)
        return (kchunk <= qchunk) & (kchunk >= qchunk - LEFT_CHUNKS) & (r >= missing)

    return _toeplitz_tiles(
        lambda x: rel_bias.astype(F32)[:, jnp.clip(-x - band, -REL_CLIP, REL_CLIP) + REL_CLIP] * LOG2E, valid,
        N_HEADS_B, band + QW_B, QW_B, variants=1 + band // QW_B)


def _band_attention(proj, rel_bias):
    seq = proj.shape[0]
    blk, band, qw = BLK_B, BAND_B, QW_B
    bias = _band_bias_tiles(rel_bias)
    npair = N_HEADS_B // 2
    qc0 = 3 * N_HEADS_A
    per = blk // band
    prev = lambda c0: (lambda hp, i: (jnp.maximum(i * per - 1, 0), c0 + hp))
    cur = lambda c0: (lambda hp, i: (i, c0 + hp))
    return pl.pallas_call(
        _band_kernel,
        grid=(npair, seq // blk),
        in_specs=[
            pl.BlockSpec((blk, LANES), cur(qc0)),
            pl.BlockSpec((band, LANES), prev(qc0 + npair)),
            pl.BlockSpec((blk, LANES), cur(qc0 + npair)),
            pl.BlockSpec((band, LANES), prev(qc0 + 2 * npair)),
            pl.BlockSpec((blk, LANES), cur(qc0 + 2 * npair)),
        ] + [
            pl.BlockSpec((1, 2, band + qw, qw), (lambda hp, i, t=t: (jnp.where(i == 0, 1 + t, 0), hp, 0, 0)))
            for t in range(band // qw)
        ] + [
            pl.BlockSpec((1, 2, band + qw, qw), lambda hp, i: (0, hp, 0, 0)),
        ],
        out_specs=pl.BlockSpec((blk, LANES), lambda hp, i: (i, hp)),
        out_shape=jax.ShapeDtypeStruct((seq, N_HEADS_B * DH_B), BF16),
        scratch_shapes=[pltpu.VMEM((band + qw, 2 * qw + SCORE_PAD), F32)] * (blk // qw),
        compiler_params=pltpu.CompilerParams(dimension_semantics=("parallel", "arbitrary")),
        name="band_attention",
    )(proj, proj, proj, proj, proj, *([bias] * bias.shape[0]))


def _retention_kernel(qk_ref, v_ref, gate_ref, cos_ref, sin_ref, qdec_ref, kdec_ref, dmat_ref,
                      sdec_ref, o_ref, state_ref):
    @pl.when(pl.program_id(0) == 0)
    def _():
        state_ref[...] = jnp.zeros(state_ref.shape, F32)

    cos = cos_ref[...]
    sin = sin_ref[...]
    lane = lax.broadcasted_iota(jnp.int32, cos.shape, 1)
    first_half = (lane % DQK_C) < (DQK_C // 2)
    qk = qk_ref[...]
    parts = []
    for j in range(qk.shape[1] // LANES):
        t = qk[:, j * LANES:(j + 1) * LANES]
        partner = jnp.where(first_half, pltpu.roll(t, LANES - DQK_C // 2, 1), pltpu.roll(t, DQK_C // 2, 1))
        parts.append(t * cos + partner * sin)
    wq = N_HEADS_C * DQK_C
    q = jnp.concatenate(parts[:wq // LANES], axis=1)
    k = jnp.concatenate(parts[wq // LANES:], axis=1) * (DQK_C ** -0.5)
    qd = (q * qdec_ref[...]).astype(BF16)
    kd = (k * kdec_ref[...]).astype(BF16)
    qb = q.astype(BF16)
    kb = k.astype(BF16)
    vb = v_ref[...].astype(BF16)
    gate = gate_ref[...]
    outs = []
    for h in range(N_HEADS_C):
        qs = slice(h * DQK_C, (h + 1) * DQK_C)
        vs = slice(h * DV_C, (h + 1) * DV_C)
        scores = _dot_nt(qb[:, qs], kb[:, qs]) * dmat_ref[h]
        state = state_ref[h]
        r = _dot(scores.astype(BF16), vb[:, vs]) + _dot(qd[:, qs], state.astype(BF16))
        state_ref[h] = state * sdec_ref[h] + _dot_tn(kd[:, qs], vb[:, vs])
        r = r * lax.rsqrt(jnp.mean(r * r, axis=-1, keepdims=True) + EPS)
        g = gate[:, vs]
        outs.append(r * (g * jax.nn.sigmoid(g)))
    o_ref[...] = jnp.concatenate(outs, axis=1).astype(o_ref.dtype)


def _retention_tables(seq):
    t = BLK_C
    half = DQK_C // 2
    inv_freq = 1.0 / (ROPE_BASE ** (jnp.arange(0, DQK_C, 2, dtype=F32) / DQK_C))
    ang = jnp.arange(seq, dtype=F32)[:, None] * inv_freq[None, :]
    reps = LANES // half
    cos = jnp.tile(jnp.cos(ang), (1, reps))
    sign = jnp.where((jnp.arange(LANES) % DQK_C) < half, -1.0, 1.0).astype(F32)
    sin = jnp.tile(jnp.sin(ang), (1, reps)) * sign[None, :]
    log_g = jnp.log(1.0 - jnp.power(2.0, -5.0 - jnp.arange(N_HEADS_C, dtype=F32)))
    pos = jnp.arange(t, dtype=F32)
    diff = pos[:, None] - pos[None, :]
    same_or_past = (jnp.arange(t)[None, :] // CHUNK) <= (jnp.arange(t)[:, None] // CHUNK)
    dmat = jnp.where(same_or_past[None], jnp.exp(log_g[:, None, None] * jnp.abs(diff)[None]), 0.0)
    qdec = jnp.repeat(jnp.exp(log_g[None, :] * (pos[:, None] + 1.0)), DQK_C, axis=1)
    kdec = jnp.repeat(jnp.exp(log_g[None, :] * (t - 1.0 - pos[:, None])), DQK_C, axis=1)
    sdec = jnp.broadcast_to(jnp.exp(log_g * t)[:, None, None], (N_HEADS_C, 1, DV_C))
    return cos, sin, qdec, kdec, dmat, sdec


def _retention(proj):
    seq = proj.shape[0]
    t = BLK_C
    cos, sin, qdec, kdec, dmat, sdec = _retention_tables(seq)
    wv = N_HEADS_C * DV_C
    return pl.pallas_call(
        _retention_kernel,
        grid=(seq // t,),
        in_specs=[
            pl.BlockSpec((t, wv), lambda i: (i, 0)),
            pl.BlockSpec((t, wv), lambda i: (i, 1)),
            pl.BlockSpec((t, wv), lambda i: (i, 2)),
            pl.BlockSpec((t, LANES), lambda i: (i, 0)),
            pl.BlockSpec((t, LANES), lambda i: (i, 0)),
            pl.BlockSpec((t, N_HEADS_C * DQK_C), lambda i: (0, 0)),
            pl.BlockSpec((t, N_HEADS_C * DQK_C), lambda i: (0, 0)),
            pl.BlockSpec((N_HEADS_C, t, t), lambda i: (0, 0, 0)),
            pl.BlockSpec((N_HEADS_C, 1, DV_C), lambda i: (0, 0, 0)),
        ],
        out_specs=pl.BlockSpec((t, wv), lambda i: (i, 0)),
        out_shape=jax.ShapeDtypeStruct((seq, wv), BF16),
        scratch_shapes=[pltpu.VMEM((N_HEADS_C, DQK_C, DV_C), F32)],
        compiler_params=pltpu.CompilerParams(dimension_semantics=("arbitrary",)),
        name="retention",
    )(proj, proj, proj, cos, sin, qdec, kdec, dmat, sdec)


def _s5_kernel(*refs):
    ncb = S5_CH // LANES
    u_refs = refs[:ncb]
    (mt_ref, bt_ref, ctr_ref, cti_ref, are_ref, aim_ref, y_ref,
     ut_ref, yt_ref, ys_ref, vr_ref, vi_ref, spr_ref, spi_ref, carry_ref) = refs[ncb:]
    tc = S5_TC
    gp = S5_GROUP
    n = S5_STATE
    ng = S5_GROUPS

    @pl.when(pl.program_id(0) == 0)
    def _():
        carry_ref[...] = jnp.zeros(carry_ref.shape, F32)

    for s in range(S5_T):
        for k in range(ncb):
            ut_ref[s, k * LANES:(k + 1) * LANES, :] = u_refs[k][pl.ds(s, tc, stride=S5_T), :].T

    unroll = 4

    def intra(it, carry):
        for k in range(unroll):
            g = it * unroll + k
            r0 = pl.multiple_of(g * gp, gp)
            ug = ut_ref[:, pl.ds(r0, gp), :].reshape(S5_T * gp, tc).astype(BF16)
            yt_ref[:, pl.ds(r0, gp), :] = _dot(mt_ref[g], ug).reshape(S5_T, gp, tc)
            vt = _dot(bt_ref[g], ug)
            n0 = pl.multiple_of(g * n, n)
            vr_ref[pl.ds(n0, n), :] = vt[0:n]
            vi_ref[pl.ds(n0, n), :] = vt[n:2 * n]
        return carry

    lax.fori_loop(0, ng // unroll, intra, 0)

    sub = SUBLANES
    nv = tc // sub
    row = lax.broadcasted_iota(jnp.int32, (tc, LANES), 0)
    in_vreg = lax.rem(row, sub)

    def rows_of(v, r):
        return jnp.broadcast_to(v[r:r + 1], (tc, LANES))

    for j in range(ng * n // LANES):
        cols = slice(j * LANES, (j + 1) * LANES)
        pwr, pwi = are_ref[:, cols], aim_ref[:, cols]
        xr = vr_ref[cols, :].T
        xi = vi_ref[cols, :].T
        for d in (1, 2, 4):
            keep = in_vreg >= d
            sr = jnp.where(keep, pltpu.roll(xr, d, 0), 0.0)
            si = jnp.where(keep, pltpu.roll(xi, d, 0), 0.0)
            fr, fi = rows_of(pwr, d - 1), rows_of(pwi, d - 1)
            xr, xi = xr + (fr * sr - fi * si), xi + (fr * si + fi * sr)
        cr, ci = carry_ref[0, :, cols], carry_ref[1, :, cols]
        cr0, ci0 = cr, ci
        outr, outi = [], []
        for v in range(nv):
            yr = xr[v * sub:(v + 1) * sub] + (pwr * cr - pwi * ci)
            yi = xi[v * sub:(v + 1) * sub] + (pwr * ci + pwi * cr)
            outr.append(yr)
            outi.append(yi)
            cr = jnp.broadcast_to(yr[sub - 1:sub], (sub, LANES))
            ci = jnp.broadcast_to(yi[sub - 1:sub], (sub, LANES))
        carry_ref[0, :, cols] = cr
        carry_ref[1, :, cols] = ci
        sr = jnp.concatenate(outr, axis=0)
        si = jnp.concatenate(outi, axis=0)
        first = row == 0
        spr_ref[j] = jnp.where(first, rows_of(cr0, 0), pltpu.roll(sr, 1, 0))
        spi_ref[j] = jnp.where(first, rows_of(ci0, 0), pltpu.roll(si, 1, 0))

    def cross(it, carry):
        for k in range(unroll):
            jp = it * unroll + k
            r0 = pl.multiple_of(jp * 2 * gp, 2 * gp)
            yc = (_dot_nt(ctr_ref[jp], spr_ref[jp].astype(BF16))
                  + _dot_nt(cti_ref[jp], spi_ref[jp].astype(BF16)))
            yt_ref[:, pl.ds(r0, 2 * gp), :] += yc.reshape(S5_T, 2 * gp, tc)
        return carry

    lax.fori_loop(0, ng // 2 // unroll, cross, 0)

    for s in range(S5_T):
        for k in range(ncb):
            ys_ref[k, pl.ds(s, tc, stride=S5_T), :] = yt_ref[s, k * LANES:(k + 1) * LANES, :].T
    for k in range(ncb):
        y_ref[:, k * LANES:(k + 1) * LANES] = ys_ref[k]


def _s5_matrices(lam_re, lam_im, log_step, b_re, b_im, c_re, c_im, d_skip):
    hi = lax.Precision.HIGHEST
    t, gp, n, ng = S5_T, S5_GROUP, S5_STATE, S5_GROUPS
    lam = lax.complex(lam_re.astype(F32), lam_im.astype(F32))
    step = jnp.exp(log_step.astype(F32))[:, None]
    ls = lam * step
    a_bar = jnp.exp(ls)
    b_bar = ((a_bar - 1.0) / lam)[..., None] * lax.complex(b_re.astype(F32), b_im.astype(F32))
    cm = lax.complex(c_re.astype(F32), c_im.astype(F32))

    def apow(k):
        kk = k.astype(F32).astype(jnp.complex64)
        return jnp.exp(ls.reshape((ng,) + (1,) * k.ndim + (n,)) * kk[None, ..., None])

    tt = jnp.arange(t)
    kmat = jnp.einsum('gpn,gln,gnq->glpq', cm, apow(tt), b_bar, precision=hi).real
    krev = jnp.transpose(kmat[:, ::-1], (0, 2, 1, 3)).reshape(ng, gp, t * gp)
    kpad = jnp.pad(krev, ((0, 0), (0, 0), (0, t * gp)))
    mt = jnp.concatenate([kpad[:, :, (t - 1 - to) * gp:(2 * t - 1 - to) * gp] for to in range(t)], axis=1)
    dvec = jnp.tile(d_skip.astype(F32).reshape(ng, 1, gp), (1, t, 1)).reshape(ng, t * gp)
    mt = mt + jnp.eye(t * gp, dtype=F32)[None] * dvec[:, :, None]
    z = jnp.swapaxes(apow(t - 1 - tt), 1, 2)[:, :, :, None] * b_bar[:, :, None, :]
    z = z.reshape(ng, n, t * gp)
    bt = jnp.concatenate([z.real, z.imag], axis=1)
    w = cm[:, None, :, :] * apow(tt + 1)[:, :, None, :]

    def pair_readout(x):
        x = x.reshape(ng // 2, 2, t, gp, n)
        first = jnp.pad(x[:, 0], ((0, 0), (0, 0), (0, 0), (0, n)))
        second = jnp.pad(x[:, 1], ((0, 0), (0, 0), (0, 0), (n, 0)))
        return jnp.stack([first, second], axis=2).reshape(ng // 2, t * 2 * gp, 2 * n).astype(BF16)

    ctr, cti = pair_readout(w.real), pair_readout(-w.imag)
    a_chunk = jnp.transpose(apow(t * (jnp.arange(SUBLANES) + 1)), (1, 0, 2)).reshape(SUBLANES, ng * n)
    return mt.astype(BF16), bt.astype(BF16), ctr, cti, a_chunk.real, a_chunk.imag


def _s5(proj, mats):
    seq, width = proj.shape
    t, tc, gp, n, ng = S5_T, S5_TC, S5_GROUP, S5_STATE, S5_GROUPS
    rows = t * tc
    ncb = S5_CH // LANES
    cb0 = (width - S5_CH) // LANES
    u_specs = [pl.BlockSpec((rows, LANES), (lambda i, k=k: (i, cb0 + k))) for k in range(ncb)]
    nsb = ng * n // LANES
    return pl.pallas_call(
        _s5_kernel,
        grid=(seq // rows,),
        in_specs=u_specs + [_const_spec(m.shape) for m in mats],
        out_specs=pl.BlockSpec((rows, S5_CH), lambda i: (i, 0)),
        out_shape=jax.ShapeDtypeStruct((seq, S5_CH), F32),
        scratch_shapes=[
            pltpu.VMEM((t, S5_CH, tc), F32),
            pltpu.VMEM((t, S5_CH, tc), F32),
            pltpu.VMEM((ncb, rows, LANES), F32),
            pltpu.VMEM((ng * n, tc), F32),
            pltpu.VMEM((ng * n, tc), F32),
            pltpu.VMEM((nsb, tc, LANES), F32),
            pltpu.VMEM((nsb, tc, LANES), F32),
            pltpu.VMEM((2, SUBLANES, ng * n), F32),
        ],
        compiler_params=pltpu.CompilerParams(dimension_semantics=("arbitrary",)),
        name="s5_scan",
    )(*([proj] * ncb), *mats)


def _mix_ffn_kernel(*refs, glu, final):
    (x_ref, a_ref, b_ref, wo_ref, g1_ref), refs = refs[:5], refs[5:]
    if glu:
        gw_ref, refs = refs[0], refs[1:]
    (g_ref, sc_ref, sh_ref, gate_ref, win_ref, cw_ref, cb_ref, wout_ref), refs = refs[:8], refs[8:]
    if final:
        fg_ref, o_ref, h_ref, act_ref, gbuf_ref, carry_ref = refs
    else:
        ng_ref, nsc_ref, nsh_ref, o_ref, hn_ref, h_ref, act_ref, gbuf_ref, carry_ref = refs
    tm = x_ref.shape[0]
    halo = gbuf_ref.shape[0] - tm

    @pl.when(pl.program_id(0) == 0)
    def _():
        carry_ref[...] = jnp.zeros(carry_ref.shape, F32)

    if glu:
        y = jax.nn.gelu(b_ref[...]).astype(BF16)
        gg = _dot(y, gw_ref[...])
        half = gg.shape[1] // 2
        b = (gg[:, :half] * jax.nn.sigmoid(gg[:, half:])).astype(BF16)
    else:
        b = b_ref[...]
    cat = jnp.concatenate([a_ref[...], b], axis=1)
    x = x_ref[...] + g1_ref[...] * _dot(cat, wo_ref[...])
    h_ref[...] = _mod_rmsnorm(x, g_ref[...], sc_ref[...], sh_ref[...]).astype(BF16)
    for f in range(D_FF // TF_FFN):
        cs = slice(f * TF_FFN, (f + 1) * TF_FFN)
        gs = slice(D_FF + f * TF_FFN, D_FF + (f + 1) * TF_FFN)
        h = h_ref[...]
        val = _dot(h, win_ref[:, cs])
        gate = _dot(h, win_ref[:, gs])
        gbuf_ref[0:halo, :] = carry_ref[:, cs]
        gbuf_ref[halo:halo + tm, :] = gate
        carry_ref[:, cs] = gate[tm - halo:tm, :]
        conv = (gate * cw_ref[2:3, cs] + gbuf_ref[halo - 1:halo - 1 + tm, :] * cw_ref[1:2, cs]
                + gbuf_ref[halo - 2:halo - 2 + tm, :] * cw_ref[0:1, cs] + cb_ref[:, cs])
        act_ref[:, cs] = (jax.nn.gelu(conv) * val).astype(BF16)
    xn = x + gate_ref[...] * _dot(act_ref[...], wout_ref[...])
    if final:
        xn = xn * lax.rsqrt(jnp.mean(xn * xn, axis=-1, keepdims=True) + EPS) * fg_ref[...]
    else:
        hn_ref[...] = _mod_rmsnorm(xn, ng_ref[...], nsc_ref[...], nsh_ref[...]).astype(BF16)
    o_ref[...] = xn


def _layer_spec(shape, layer):
    idx = (layer,) + (0,) * (len(shape) - 1)
    return pl.BlockSpec((None,) + tuple(shape[1:]), lambda *_: idx, pipeline_mode=pl.Buffered(1))


def _mix_ffn(x, a, b, wo, gate1, glu_w, g, scale, shift, gate2, w_in, conv_w, conv_b, w_out, tail, layer):
    seq, d = x.shape
    final = len(tail) == 1
    tm = TM_FFN
    halo = SUBLANES
    row = pl.BlockSpec((1, d), lambda i: (0, 0))
    rows = lambda w: pl.BlockSpec((tm, w), lambda i: (i, 0))
    conv_b = conv_b.reshape(conv_b.shape[0], 1, D_FF)
    in_specs = [rows(d), rows(a.shape[1]), rows(b.shape[1]), _const_spec(wo.shape), row]
    args = [x, a, b, wo, gate1]
    if glu_w is not None:
        in_specs.append(_const_spec(glu_w.shape))
        args.append(glu_w)
    in_specs += [
        row, row, row, row,
        _layer_spec(w_in.shape, layer),
        _layer_spec(conv_w.shape, layer),
        _layer_spec(conv_b.shape, layer),
        _layer_spec(w_out.shape, layer),
    ] + [row] * len(tail)
    args += [g.reshape(1, d), scale, shift, gate2, w_in, conv_w, conv_b, w_out]
    args += [t.reshape(1, d) for t in tail]
    out_specs = [rows(d)] if final else [rows(d), rows(d)]
    out_shape = [jax.ShapeDtypeStruct((seq, d), F32)] + ([] if final else [jax.ShapeDtypeStruct((seq, d), BF16)])
    return pl.pallas_call(
        functools.partial(_mix_ffn_kernel, glu=glu_w is not None, final=final),
        grid=(seq // tm,),
        in_specs=in_specs,
        out_specs=out_specs,
        out_shape=out_shape,
        scratch_shapes=[
            pltpu.VMEM((tm, d), BF16),
            pltpu.VMEM((tm, D_FF), BF16),
            pltpu.VMEM((tm + halo, TF_FFN), F32),
            pltpu.VMEM((halo, D_FF), F32),
        ],
        compiler_params=pltpu.CompilerParams(dimension_semantics=("arbitrary",)),
        name="mix_ffn",
    )(*args)


def kernel(x, c, t5_table, mod_w, mod_b, norm1_g, norm2_g, ffn_w_in, ffn_conv_w, ffn_conv_b, ffn_w_out,
           ev_w_in, ev_w_out, diff_lambda, diff_subln_g, band_rel_bias,
           od_w_in, od_w_out, s5_lam_re, s5_lam_im, s5_log_step, s5_b_re, s5_b_im, s5_c_re, s5_c_im,
           s5_d, s5_glu_w, final_g):
    assert x.shape[0] == 1 and x.shape[2] == D_MODEL
    seq = x.shape[1]
    assert seq % TM_PROJ == 0 and seq % (S5_T * S5_TC) == 0
    d = D_MODEL
    xs = x[0]
    mod = _modulation(c, mod_w, mod_b)
    ffn_w_in_b = ffn_w_in.astype(BF16)
    ffn_w_out_b = ffn_w_out.astype(BF16)
    mods = [[mod[i, :, k * d:(k + 1) * d] for k in range(6)] for i in range(DEPTH)]
    h = None
    for i in range(DEPTH):
        sh1, sc1, g1, sh2, sc2, g2 = mods[i]
        w_in = (ev_w_in if i % 2 == 0 else od_w_in)[i // 2].astype(BF16)
        proj_dtype = BF16 if i % 2 == 0 else F32
        if h is None:
            proj = _normproj(xs, norm1_g[i], sc1, sh1, w_in, proj_dtype)
        else:
            proj = _proj(h, w_in, proj_dtype)
        if i % 2 == 0:
            e = i // 2
            lam_init = 0.8 - 0.6 * math.exp(-0.3 * i)
            lp = diff_lambda[e].astype(F32)
            lam = jnp.exp(jnp.sum(lp[0] * lp[1])) - jnp.exp(jnp.sum(lp[2] * lp[3])) + lam_init
            mix_a = _diff_attention(proj, t5_table, lam, diff_subln_g[e], lam_init)
            mix_b = _band_attention(proj, band_rel_bias[e])
            wo, glu_w = ev_w_out[e].astype(BF16), None
        else:
            o = i // 2
            mix_a = _retention(proj)
            mats = _s5_matrices(s5_lam_re[o], s5_lam_im[o], s5_log_step[o], s5_b_re[o], s5_b_im[o],
                                s5_c_re[o], s5_c_im[o], s5_d[o])
            mix_b = _s5(proj, mats)
            wo, glu_w = od_w_out[o].astype(BF16), s5_glu_w[o].astype(BF16)
        if i == DEPTH - 1:
            tail = (final_g,)
        else:
            nsh1, nsc1 = mods[i + 1][0], mods[i + 1][1]
            tail = (norm1_g[i + 1], nsc1, nsh1)
        out = _mix_ffn(xs, mix_a, mix_b, wo, g1, glu_w, norm2_g[i], sc2, sh2, g2,
                       ffn_w_in_b, ffn_conv_w, ffn_conv_b, ffn_w_out_b, tail, layer=i)
        if i == DEPTH - 1:
            xs = out[0]
        else:
            xs, h = out
    return xs[None]
```

```python
import functools
import math

import jax
import jax.numpy as jnp
from jax import lax
from jax.experimental import pallas as pl
from jax.experimental.pallas import tpu as pltpu

F32 = jnp.float32
BF16 = jnp.bfloat16

D_MODEL = 1024
DEPTH = 2
CHUNK = 64
GROUP_WIDTH = D_MODEL // 2
DK_A = 64
DV_A = 2 * DK_A
N_HEADS_A = GROUP_WIDTH // DV_A
DH_B = 64
N_HEADS_B = GROUP_WIDTH // DH_B
LEFT_CHUNKS = 8
REL_CLIP = 2 * CHUNK
NUM_BUCKETS = 32
MAX_DISTANCE = 128
DV_C = 128
DQK_C = DV_C // 2
N_HEADS_C = GROUP_WIDTH // DV_C
ROPE_BASE = 10000.0
S5_CH = GROUP_WIDTH
S5_GROUP = 16
S5_GROUPS = S5_CH // S5_GROUP
S5_STATE = 64
D_FF = ((8 * D_MODEL // 3 + 255) // 256) * 256
CONV_W = 3
EVEN_IN = 3 * N_HEADS_A * DV_A + 3 * N_HEADS_B * DH_B
ODD_IN = 2 * N_HEADS_C * DQK_C + 2 * N_HEADS_C * DV_C + S5_CH
EPS = 1e-6
NEG_INF = -1e30
LOG2E = math.log2(math.e)

LANES = 128
SUBLANES = 8
MXU_DIM = 256

TM_PROJ = 1024
TN_PROJ = 1024
TN_MOD = 1536
TM_FFN = 512
TF_FFN = MXU_DIM
BLK_A = 512
NPART_A = 2
ONES_A = 16
SINGLE_PASS_LOG2_RANGE = 96.0
SCORE_PAD = LANES
BLK_B = 1024
BAND_B = LEFT_CHUNKS * CHUNK
QW_B = 4 * CHUNK
BLK_C = 512
S5_T = 16
S5_TC = LANES

assert BLK_B % BAND_B == 0 and BLK_B % QW_B == 0 and BAND_B % QW_B == 0
assert BLK_A >= MAX_DISTANCE, "far key blocks must sit in the saturated T5 bucket"
assert DV_A == LANES and 2 * DK_A == LANES and 2 * DH_B == LANES, "attention heads are read as 128-lane column blocks"


def _dot(a, b):
    return jnp.dot(a, b, preferred_element_type=F32)


def _dot_nt(a, b):
    return lax.dot_general(a, b, (((1,), (1,)), ((), ())), preferred_element_type=F32)


def _dot_tn(a, b):
    return lax.dot_general(a, b, (((0,), (0,)), ((), ())), preferred_element_type=F32)


def _const_spec(shape):
    zeros = (0,) * len(shape)
    return pl.BlockSpec(shape, lambda *_: zeros, pipeline_mode=pl.Buffered(1))


def _mod_rmsnorm(x, g, scale, shift):
    y = x * lax.rsqrt(jnp.mean(x * x, axis=-1, keepdims=True) + EPS)
    y = y * g
    return y * (1.0 + scale) + shift


def _mod_kernel(c_ref, w_ref, b_ref, o_ref):
    c = c_ref[...]
    cond = c * jax.nn.sigmoid(c)
    o_ref[0] = jnp.sum(cond * w_ref[0], axis=0, keepdims=True) + b_ref[0]


def _modulation(c, mod_w, mod_b):
    depth, d, n = mod_w.shape
    tn = TN_MOD
    return pl.pallas_call(
        _mod_kernel,
        grid=(depth, n // tn),
        in_specs=[
            pl.BlockSpec((d, 1), lambda i, j: (0, 0)),
            pl.BlockSpec((1, d, tn), lambda i, j: (i, 0, j)),
            pl.BlockSpec((1, 1, tn), lambda i, j: (i, 0, j)),
        ],
        out_specs=pl.BlockSpec((1, 1, tn), lambda i, j: (i, 0, j)),
        out_shape=jax.ShapeDtypeStruct((depth, 1, n), F32),
        name="modulation",
    )(c.reshape(d, 1), mod_w, mod_b.reshape(depth, 1, n))


def _normproj_kernel(x_ref, g_ref, sc_ref, sh_ref, w_ref, o_ref):
    tm, n = o_ref.shape
    half = tm // 2
    for r in range(2):
        rows = slice(r * half, (r + 1) * half)
        h = _mod_rmsnorm(x_ref[rows, :], g_ref[...], sc_ref[...], sh_ref[...]).astype(BF16)
        for j in range(n // TN_PROJ):
            cols = slice(j * TN_PROJ, (j + 1) * TN_PROJ)
            o_ref[rows, cols] = _dot(h, w_ref[:, cols]).astype(o_ref.dtype)


def _normproj(x, g, scale, shift, w, out_dtype):
    seq, d = x.shape
    n = w.shape[1]
    tm = TM_PROJ
    row = pl.BlockSpec((1, d), lambda i: (0, 0))
    return pl.pallas_call(
        _normproj_kernel,
        grid=(seq // tm,),
        in_specs=[pl.BlockSpec((tm, d), lambda i: (i, 0)), row, row, row, _const_spec(w.shape)],
        out_specs=pl.BlockSpec((tm, n), lambda i: (i, 0)),
        out_shape=jax.ShapeDtypeStruct((seq, n), out_dtype),
        compiler_params=pltpu.CompilerParams(dimension_semantics=("parallel",)),
        name="normproj",
    )(x, g.reshape(1, d), scale, shift, w)


def _proj_kernel(h_ref, w_ref, o_ref):
    for j in range(o_ref.shape[1] // TN_PROJ):
        cols = slice(j * TN_PROJ, (j + 1) * TN_PROJ)
        o_ref[:, cols] = _dot(h_ref[...], w_ref[:, cols]).astype(o_ref.dtype)


def _proj(h, w, out_dtype):
    seq, d = h.shape
    n = w.shape[1]
    tm = TM_PROJ
    return pl.pallas_call(
        _proj_kernel,
        grid=(seq // tm,),
        in_specs=[pl.BlockSpec((tm, d), lambda i: (i, 0)), _const_spec(w.shape)],
        out_specs=pl.BlockSpec((tm, n), lambda i: (i, 0)),
        out_shape=jax.ShapeDtypeStruct((seq, n), out_dtype),
        compiler_params=pltpu.CompilerParams(dimension_semantics=("parallel",)),
        name="proj",
    )(h, w)


def _diffattn_kernel(q_ref, k_ref, v_ref, bias_ref, bstat_ref, lam_ref, g_ref, o_ref,
                     qs_ref, vt_ref, kmax_ref, r_ref, m_ref, acc_ref, *s_refs, out_scale):
    blk = BLK_A
    nq = 2 * blk
    sub = SUBLANES
    dv = DV_A
    npart = len(s_refs) // 4
    wq = nq // npart
    sa_ref, sb_ref = s_refs[:2 * npart], s_refs[2 * npart:]
    i = pl.program_id(1)
    lane = lax.broadcasted_iota(jnp.int32, (blk, LANES), 1)
    same_subhead = (lax.broadcasted_iota(jnp.int32, (LANES, LANES), 0) // DK_A
                    == lax.broadcasted_iota(jnp.int32, (LANES, LANES), 1) // DK_A).astype(BF16)

    @pl.when(i == 0)
    def _():
        kmax_ref[...] = jnp.zeros(kmax_ref.shape, F32)

        def tr(b, carry):
            r0 = pl.multiple_of(b * blk, blk)
            vt_ref[0:dv, pl.ds(r0, blk)] = v_ref[pl.ds(r0, blk), :].astype(F32).T.astype(BF16)
            vt_ref[dv:dv + ONES_A, pl.ds(r0, blk)] = jnp.ones((ONES_A, blk), BF16)
            kf = k_ref[pl.ds(r0, blk), :].astype(F32)
            kn2 = _dot((kf * kf).astype(BF16), same_subhead)
            kmax_ref[...] = jnp.maximum(kmax_ref[...], jnp.max(kn2.reshape(blk // sub, sub, LANES), axis=0))
            return carry
        lax.fori_loop(0, v_ref.shape[0] // blk, tr, 0)
        kmax_ref[...] = jnp.broadcast_to(jnp.max(kmax_ref[...], axis=0, keepdims=True), kmax_ref.shape)

    q = (q_ref[...].astype(F32) * (DK_A ** -0.5 * LOG2E)).astype(BF16)
    qf = q.astype(F32)
    qs_ref[:, 0:blk] = jnp.where(lane < DK_A, qf, 0.0).T.astype(BF16)
    qs_ref[:, blk:nq] = jnp.where(lane >= DK_A, qf, 0.0).T.astype(BF16)
    acc_ref[...] = jnp.zeros(acc_ref.shape, F32)

    qsq = (qf * qf).astype(BF16)
    bound = []
    for m in range(2):
        lanes_m = (lax.broadcasted_iota(jnp.int32, (sub, LANES), 1) // DK_A == m).astype(BF16)
        qn2 = _dot_nt(lanes_m, qsq)
        bound.append(jnp.sqrt(qn2 * kmax_ref[:, m * DK_A:m * DK_A + 1]) * 1.03)
    bound = jnp.concatenate(bound, axis=1)
    bias_max, bias_span = bstat_ref[0, 0:1, 0:1], bstat_ref[0, 1:2, 0:1]
    r_ref[...] = bound + bias_max
    single_pass = jnp.max(2.0 * bound + bias_span) < SINGLE_PASS_LOG2_RANGE

    @pl.when(single_pass)
    def _():
        _diffattn_blocks(i, k_ref, bias_ref, qs_ref, vt_ref, m_ref, acc_ref, sa_ref, sb_ref, r_ref)

    @pl.when(jnp.logical_not(single_pass))
    def _():
        _diffattn_blocks(i, k_ref, bias_ref, qs_ref, vt_ref, m_ref, acc_ref, sa_ref, sb_ref, None)

    ot = acc_ref[0:dv, 0:nq] / acc_ref[dv:dv + 1, 0:nq]
    o = ot[:, 0:blk].T - lam_ref[...] * ot[:, blk:nq].T
    o = o * lax.rsqrt(jnp.mean(o * o, axis=-1, keepdims=True) + EPS) * g_ref[...]
    o_ref[...] = (o * out_scale).astype(o_ref.dtype)


def _diffattn_blocks(i, k_ref, bias_ref, qs_ref, vt_ref, m_ref, acc_ref, sa_ref, sb_ref, shift_ref):
    blk = BLK_A
    nq = 2 * blk
    sub = SUBLANES
    npart = len(sa_ref) // 2
    wq = nq // npart
    online = shift_ref is None
    if online:
        m_ref[...] = jnp.full(m_ref.shape, NEG_INF, F32)

    def scores(b, s_ref):
        k = k_ref[pl.ds(pl.multiple_of(b * blk, blk), blk), :]
        for part in range(npart):
            s = _dot(k, qs_ref[:, part * wq:(part + 1) * wq])
            s_ref[part][:, 0:wq] = s
            if online:
                s_ref[npart + part][...] = jnp.max(s.reshape(blk // sub, sub, wq), axis=0)

    def softmax_pv(b, s_ref, bias):
        vt = vt_ref[:, pl.ds(pl.multiple_of(b * blk, blk), blk)]
        for part in range(npart):
            cols = slice(part * wq, (part + 1) * wq)
            s = s_ref[part][:, 0:wq]
            if bias is not None:
                q0 = (part * wq) % blk
                s = s + bias[:, q0:q0 + wq]
            s = s.reshape(blk // sub, sub, wq)
            if not online:
                p = jnp.exp2(s - shift_ref[:, cols][None])
                acc_ref[:, cols] += _dot(vt, p.reshape(blk, wq).astype(BF16))
                continue
            m_prev = m_ref[:, cols]
            smax = jnp.max(s, axis=0) if bias is not None else s_ref[npart + part][...]
            m_cur = jnp.max(smax, axis=0, keepdims=True)
            m_new = jnp.maximum(m_prev, m_cur)
            alpha = jnp.exp2(m_prev - m_new)
            p = jnp.exp2(s - m_new[None])
            pv = _dot(vt, p.reshape(blk, wq).astype(BF16))
            acc_ref[:, cols] = acc_ref[:, cols] * alpha[0:1] + pv
            m_ref[:, cols] = m_new

    nfar = jnp.maximum(i - 1, 0)
    odd = lax.rem(nfar, 2)

    @pl.when(i == 0)
    def _():
        scores(0, sb_ref)

    @pl.when(i > 0)
    def _():
        @pl.when(odd == 1)
        def _():
            scores(0, sb_ref)
            scores(1, sa_ref)
            softmax_pv(0, sb_ref, None)

        @pl.when(odd == 0)
        def _():
            scores(0, sa_ref)

        def pair(b):
            scores(b + 1, sb_ref)
            softmax_pv(b, sa_ref, None)
            scores(b + 2, sa_ref)
            softmax_pv(b + 1, sb_ref, None)

        def quad_body(t, carry):
            pair(odd + 4 * t)
            pair(odd + 4 * t + 2)
            return carry

        npairs = nfar // 2
        lax.fori_loop(0, npairs // 2, quad_body, 0)

        @pl.when(lax.rem(npairs, 2) == 1)
        def _():
            pair(odd + 2 * (npairs - 1))
        scores(i, sb_ref)
        softmax_pv(i - 1, sa_ref, bias_ref[0, 0])

    softmax_pv(i, sb_ref, bias_ref[0, 1])


_TOEPLITZ_ROWS = 256
_TOEPLITZ_N = 2048


def _toeplitz_kernel(v_ref, o_ref, *, keep):
    rows, cols = o_ref.shape[2:]
    x = jnp.broadcast_to(v_ref[0, 0], (rows, v_ref.shape[-1]))
    tile = pltpu.roll(x, 0, 1, stride=1, stride_axis=0)[:, :cols]
    r = lax.broadcasted_iota(jnp.int32, (rows, cols), 0) + pl.program_id(1) * rows
    c = lax.broadcasted_iota(jnp.int32, (rows, cols), 1)
    for variant in range(o_ref.shape[0]):
        o_ref[variant, 0] = jnp.where(keep(r, c, variant), tile, NEG_INF)


def _toeplitz_tiles(fn, keep, heads, rows, cols, variants=1):
    n, rb = _TOEPLITZ_N, _TOEPLITZ_ROWS
    assert rows % rb == 0 and rows <= n // 2 and cols <= n // 2
    idx = jnp.arange(n, dtype=jnp.int32)
    vec = fn(jnp.where(idx < n // 2, idx, idx - n)).astype(F32)
    vecs = jnp.stack([jnp.roll(vec, k * rb, axis=1) for k in range(rows // rb)], axis=1)
    return pl.pallas_call(
        functools.partial(_toeplitz_kernel, keep=keep),
        grid=(heads, rows // rb),
        in_specs=[pl.BlockSpec((1, 1, 1, n), lambda h, k: (h, k, 0, 0))],
        out_specs=pl.BlockSpec((variants, 1, rb, cols), lambda h, k: (0, h, k, 0)),
        out_shape=jax.ShapeDtypeStruct((variants, heads, rows, cols), F32),
        name="toeplitz_tiles",
    )(vecs.reshape(heads, rows // rb, 1, n))


def _t5_bucket(rel):
    nb = NUM_BUCKETS // 2
    max_exact = nb // 2
    bucket = jnp.where(rel > 0, nb, 0)
    n = jnp.abs(rel)
    nf = jnp.maximum(n, 1).astype(F32)
    large = max_exact + (jnp.log(nf / max_exact) / math.log(MAX_DISTANCE / max_exact)
                         * (nb - max_exact)).astype(jnp.int32)
    large = jnp.minimum(large, nb - 1)
    return bucket + jnp.where(n < max_exact, n, large)


def _diff_bias_tiles(t5_table):
    blk = BLK_A
    table = t5_table.astype(F32)
    far = table[_t5_bucket(jnp.full((), -(blk + 1), jnp.int32))]
    def visible(r, c, variant):
        return jnp.floor_divide(r - blk, CHUNK) <= jnp.floor_divide(c, CHUNK)

    tiles = _toeplitz_tiles(lambda x: ((table[_t5_bucket(-x - blk)] - far) * LOG2E).T, visible,
                            N_HEADS_A, 2 * blk, blk)
    return tiles.reshape(N_HEADS_A, 2, blk, blk)


def _diff_attention(proj, t5_table, lam, subln_g, lam_init):
    seq = proj.shape[0]
    blk = BLK_A
    bias = _diff_bias_tiles(t5_table)
    ha = N_HEADS_A
    finite = bias > 0.5 * NEG_INF
    bias_max = jnp.maximum(jnp.max(jnp.where(finite, bias, NEG_INF), axis=(1, 2, 3)), 0.0)
    bias_min = jnp.minimum(jnp.min(jnp.where(finite, bias, -NEG_INF), axis=(1, 2, 3)), 0.0)
    bstat = jnp.broadcast_to(jnp.stack([bias_max, bias_max - bias_min], axis=1)[:, :, None], (ha, 2, LANES))
    kern = functools.partial(_diffattn_kernel, out_scale=1.0 - lam_init)
    return pl.pallas_call(
        kern,
        grid=(ha, seq // blk),
        in_specs=[
            pl.BlockSpec((blk, DV_A), lambda h, i: (i, h)),
            pl.BlockSpec((seq, DV_A), lambda h, i: (0, ha + h)),
            pl.BlockSpec((seq, DV_A), lambda h, i: (0, 2 * ha + h)),
            pl.BlockSpec((1, 2, blk, blk), lambda h, i: (h, 0, 0, 0)),
            pl.BlockSpec((1, 2, LANES), lambda h, i: (h, 0, 0)),
            pl.BlockSpec((1, DV_A), lambda h, i: (0, 0)),
            pl.BlockSpec((1, DV_A), lambda h, i: (0, 0)),
        ],
        out_specs=pl.BlockSpec((blk, DV_A), lambda h, i: (i, h)),
        out_shape=jax.ShapeDtypeStruct((seq, ha * DV_A), BF16),
        scratch_shapes=[
            pltpu.VMEM((DV_A, 2 * blk), BF16),
            pltpu.VMEM((DV_A + ONES_A, seq), BF16),
            pltpu.VMEM((SUBLANES, LANES), F32),
            pltpu.VMEM((SUBLANES, 2 * blk), F32),
            pltpu.VMEM((SUBLANES, 2 * blk), F32),
            pltpu.VMEM((DV_A + ONES_A, 2 * blk), F32),
        ] + 2 * ([pltpu.VMEM((blk, 2 * blk // NPART_A + SCORE_PAD), F32)] * NPART_A
                 + [pltpu.VMEM((SUBLANES, 2 * blk // NPART_A), F32)] * NPART_A),
        compiler_params=pltpu.CompilerParams(dimension_semantics=("parallel", "arbitrary")),
        name="diff_attention",
    )(proj, proj, proj, bias, bstat, jnp.full((1, DV_A), lam, F32), subln_g.reshape(1, DV_A).astype(F32))


def _band_kernel(q_ref, kp_ref, kc_ref, vp_ref, vc_ref, *refs):
    qw, band = QW_B, BAND_B
    nbias = band // qw + 1
    bias_refs, o_ref, s_refs = refs[:nbias], refs[nbias], refs[nbias + 1:]
    nk = band + qw
    sub = SUBLANES
    q = q_ref[...].astype(F32) * (DH_B ** -0.5 * LOG2E)
    lane = lax.broadcasted_iota(jnp.int32, q.shape, 1)
    qh = (jnp.where(lane < DH_B, q, 0.0).T.astype(BF16), jnp.where(lane >= DH_B, q, 0.0).T.astype(BF16))
    k_all = jnp.concatenate([kp_ref[...], kc_ref[...]], axis=0)
    vt_all = jnp.concatenate([vp_ref[...], vc_ref[...]], axis=0).astype(F32).T.astype(BF16)
    vt_all = jnp.concatenate([vt_all, jnp.ones((ONES_A, vt_all.shape[1]), BF16)], axis=0)
    ngroups = len(s_refs)
    for g in range(ngroups):
        k0 = g * qw
        qs = jnp.concatenate([qh[0][:, k0:k0 + qw], qh[1][:, k0:k0 + qw]], axis=1)
        s_refs[g][:, 0:2 * qw] = _dot(k_all[k0:k0 + nk], qs)
    for g in range(ngroups):
        k0 = g * qw
        bias_ref = bias_refs[min(g, nbias - 1)]
        bias = jnp.concatenate([bias_ref[0, 0], bias_ref[0, 1]], axis=1)
        s = (s_refs[g][:, 0:2 * qw] + bias).reshape(nk // sub, sub, 2 * qw)
        m = jnp.max(jnp.max(s, axis=0), axis=0, keepdims=True)
        p = jnp.exp2(s - m[None])
        pv = _dot(vt_all[:, k0:k0 + nk], p.reshape(nk, 2 * qw).astype(BF16))
        ot = pv[0:2 * DH_B] / pv[2 * DH_B:2 * DH_B + 1]
        o = jnp.concatenate([ot[0:DH_B, 0:qw], ot[DH_B:2 * DH_B, qw:2 * qw]], axis=0)
        o_ref[k0:k0 + qw, :] = o.T.astype(o_ref.dtype)


def _band_bias_tiles(rel_bias):
    band = BAND_B

    def valid(r, c, variant):
        qchunk = jnp.floor_divide(c, CHUNK)
        kchunk = jnp.floor_divide(r - band, CHUNK)
        missing = jnp.where(variant == 0, 0, band - (variant - 1) * QW_B)
        return (kchunk <= qchunk) & (kchunk >= qchunk - LEFT_CHUNKS) & (r >= missing)

    return _toeplitz_tiles(
        lambda x: rel_bias.astype(F32)[:, jnp.clip(-x - band, -REL_CLIP, REL_CLIP) + REL_CLIP] * LOG2E, valid,
        N_HEADS_B, band + QW_B, QW_B, variants=1 + band // QW_B)


def _band_attention(proj, rel_bias):
    seq = proj.shape[0]
    blk, band, qw = BLK_B, BAND_B, QW_B
    bias = _band_bias_tiles(rel_bias)
    npair = N_HEADS_B // 2
    qc0 = 3 * N_HEADS_A
    per = blk // band
    prev = lambda c0: (lambda hp, i: (jnp.maximum(i * per - 1, 0), c0 + hp))
    cur = lambda c0: (lambda hp, i: (i, c0 + hp))
    return pl.pallas_call(
        _band_kernel,
        grid=(npair, seq // blk),
        in_specs=[
            pl.BlockSpec((blk, LANES), cur(qc0)),
            pl.BlockSpec((band, LANES), prev(qc0 + npair)),
            pl.BlockSpec((blk, LANES), cur(qc0 + npair)),
            pl.BlockSpec((band, LANES), prev(qc0 + 2 * npair)),
            pl.BlockSpec((blk, LANES), cur(qc0 + 2 * npair)),
        ] + [
            pl.BlockSpec((1, 2, band + qw, qw), (lambda hp, i, t=t: (jnp.where(i == 0, 1 + t, 0), hp, 0, 0)))
            for t in range(band // qw)
        ] + [
            pl.BlockSpec((1, 2, band + qw, qw), lambda hp, i: (0, hp, 0, 0)),
        ],
        out_specs=pl.BlockSpec((blk, LANES), lambda hp, i: (i, hp)),
        out_shape=jax.ShapeDtypeStruct((seq, N_HEADS_B * DH_B), BF16),
        scratch_shapes=[pltpu.VMEM((band + qw, 2 * qw + SCORE_PAD), F32)] * (blk // qw),
        compiler_params=pltpu.CompilerParams(dimension_semantics=("parallel", "arbitrary")),
        name="band_attention",
    )(proj, proj, proj, proj, proj, *([bias] * bias.shape[0]))


def _retention_kernel(qk_ref, v_ref, gate_ref, cos_ref, sin_ref, qdec_ref, kdec_ref, dmat_ref,
                      sdec_ref, o_ref, state_ref):
    @pl.when(pl.program_id(0) == 0)
    def _():
        state_ref[...] = jnp.zeros(state_ref.shape, F32)

    cos = cos_ref[...]
    sin = sin_ref[...]
    lane = lax.broadcasted_iota(jnp.int32, cos.shape, 1)
    first_half = (lane % DQK_C) < (DQK_C // 2)
    qk = qk_ref[...]
    parts = []
    for j in range(qk.shape[1] // LANES):
        t = qk[:, j * LANES:(j + 1) * LANES]
        partner = jnp.where(first_half, pltpu.roll(t, LANES - DQK_C // 2, 1), pltpu.roll(t, DQK_C // 2, 1))
        parts.append(t * cos + partner * sin)
    wq = N_HEADS_C * DQK_C
    q = jnp.concatenate(parts[:wq // LANES], axis=1)
    k = jnp.concatenate(parts[wq // LANES:], axis=1) * (DQK_C ** -0.5)
    qd = (q * qdec_ref[...]).astype(BF16)
    kd = (k * kdec_ref[...]).astype(BF16)
    qb = q.astype(BF16)
    kb = k.astype(BF16)
    vb = v_ref[...].astype(BF16)
    gate = gate_ref[...]
    outs = []
    for h in range(N_HEADS_C):
        qs = slice(h * DQK_C, (h + 1) * DQK_C)
        vs = slice(h * DV_C, (h + 1) * DV_C)
        scores = _dot_nt(qb[:, qs], kb[:, qs]) * dmat_ref[h]
        state = state_ref[h]
        r = _dot(scores.astype(BF16), vb[:, vs]) + _dot(qd[:, qs], state.astype(BF16))
        state_ref[h] = state * sdec_ref[h] + _dot_tn(kd[:, qs], vb[:, vs])
        r = r * lax.rsqrt(jnp.mean(r * r, axis=-1, keepdims=True) + EPS)
        g = gate[:, vs]
        outs.append(r * (g * jax.nn.sigmoid(g)))
    o_ref[...] = jnp.concatenate(outs, axis=1).astype(o_ref.dtype)


def _retention_tables(seq):
    t = BLK_C
    half = DQK_C // 2
    inv_freq = 1.0 / (ROPE_BASE ** (jnp.arange(0, DQK_C, 2, dtype=F32) / DQK_C))
    ang = jnp.arange(seq, dtype=F32)[:, None] * inv_freq[None, :]
    reps = LANES // half
    cos = jnp.tile(jnp.cos(ang), (1, reps))
    sign = jnp.where((jnp.arange(LANES) % DQK_C) < half, -1.0, 1.0).astype(F32)
    sin = jnp.tile(jnp.sin(ang), (1, reps)) * sign[None, :]
    log_g = jnp.log(1.0 - jnp.power(2.0, -5.0 - jnp.arange(N_HEADS_C, dtype=F32)))
    pos = jnp.arange(t, dtype=F32)
    diff = pos[:, None] - pos[None, :]
    same_or_past = (jnp.arange(t)[None, :] // CHUNK) <= (jnp.arange(t)[:, None] // CHUNK)
    dmat = jnp.where(same_or_past[None], jnp.exp(log_g[:, None, None] * jnp.abs(diff)[None]), 0.0)
    qdec = jnp.repeat(jnp.exp(log_g[None, :] * (pos[:, None] + 1.0)), DQK_C, axis=1)
    kdec = jnp.repeat(jnp.exp(log_g[None, :] * (t - 1.0 - pos[:, None])), DQK_C, axis=1)
    sdec = jnp.broadcast_to(jnp.exp(log_g * t)[:, None, None], (N_HEADS_C, 1, DV_C))
    return cos, sin, qdec, kdec, dmat, sdec


def _retention(proj):
    seq = proj.shape[0]
    t = BLK_C
    cos, sin, qdec, kdec, dmat, sdec = _retention_tables(seq)
    wv = N_HEADS_C * DV_C
    return pl.pallas_call(
        _retention_kernel,
        grid=(seq // t,),
        in_specs=[
            pl.BlockSpec((t, wv), lambda i: (i, 0)),
            pl.BlockSpec((t, wv), lambda i: (i, 1)),
            pl.BlockSpec((t, wv), lambda i: (i, 2)),
            pl.BlockSpec((t, LANES), lambda i: (i, 0)),
            pl.BlockSpec((t, LANES), lambda i: (i, 0)),
            pl.BlockSpec((t, N_HEADS_C * DQK_C), lambda i: (0, 0)),
            pl.BlockSpec((t, N_HEADS_C * DQK_C), lambda i: (0, 0)),
            pl.BlockSpec((N_HEADS_C, t, t), lambda i: (0, 0, 0)),
            pl.BlockSpec((N_HEADS_C, 1, DV_C), lambda i: (0, 0, 0)),
        ],
        out_specs=pl.BlockSpec((t, wv), lambda i: (i, 0)),
        out_shape=jax.ShapeDtypeStruct((seq, wv), BF16),
        scratch_shapes=[pltpu.VMEM((N_HEADS_C, DQK_C, DV_C), F32)],
        compiler_params=pltpu.CompilerParams(dimension_semantics=("arbitrary",)),
        name="retention",
    )(proj, proj, proj, cos, sin, qdec, kdec, dmat, sdec)


def _s5_kernel(*refs):
    ncb = S5_CH // LANES
    u_refs = refs[:ncb]
    (mt_ref, bt_ref, ctr_ref, cti_ref, are_ref, aim_ref, y_ref,
     ut_ref, yt_ref, ys_ref, vr_ref, vi_ref, spr_ref, spi_ref, carry_ref) = refs[ncb:]
    tc = S5_TC
    gp = S5_GROUP
    n = S5_STATE
    ng = S5_GROUPS

    @pl.when(pl.program_id(0) == 0)
    def _():
        carry_ref[...] = jnp.zeros(carry_ref.shape, F32)

    for s in range(S5_T):
        for k in range(ncb):
            ut_ref[s, k * LANES:(k + 1) * LANES, :] = u_refs[k][pl.ds(s, tc, stride=S5_T), :].T

    unroll = 4

    def intra(it, carry):
        for k in range(unroll):
            g = it * unroll + k
            r0 = pl.multiple_of(g * gp, gp)
            ug = ut_ref[:, pl.ds(r0, gp), :].reshape(S5_T * gp, tc).astype(BF16)
            yt_ref[:, pl.ds(r0, gp), :] = _dot(mt_ref[g], ug).reshape(S5_T, gp, tc)
            vt = _dot(bt_ref[g], ug)
            n0 = pl.multiple_of(g * n, n)
            vr_ref[pl.ds(n0, n), :] = vt[0:n]
            vi_ref[pl.ds(n0, n), :] = vt[n:2 * n]
        return carry

    lax.fori_loop(0, ng // unroll, intra, 0)

    sub = SUBLANES
    nv = tc // sub
    row = lax.broadcasted_iota(jnp.int32, (tc, LANES), 0)
    in_vreg = lax.rem(row, sub)

    def rows_of(v, r):
        return jnp.broadcast_to(v[r:r + 1], (tc, LANES))

    for j in range(ng * n // LANES):
        cols = slice(j * LANES, (j + 1) * LANES)
        pwr, pwi = are_ref[:, cols], aim_ref[:, cols]
        xr = vr_ref[cols, :].T
        xi = vi_ref[cols, :].T
        for d in (1, 2, 4):
            keep = in_vreg >= d
            sr = jnp.where(keep, pltpu.roll(xr, d, 0), 0.0)
            si = jnp.where(keep, pltpu.roll(xi, d, 0), 0.0)
            fr, fi = rows_of(pwr, d - 1), rows_of(pwi, d - 1)
            xr, xi = xr + (fr * sr - fi * si), xi + (fr * si + fi * sr)
        cr, ci = carry_ref[0, :, cols], carry_ref[1, :, cols]
        cr0, ci0 = cr, ci
        outr, outi = [], []
        for v in range(nv):
            yr = xr[v * sub:(v + 1) * sub] + (pwr * cr - pwi * ci)
            yi = xi[v * sub:(v + 1) * sub] + (pwr * ci + pwi * cr)
            outr.append(yr)
            outi.append(yi)
            cr = jnp.broadcast_to(yr[sub - 1:sub], (sub, LANES))
            ci = jnp.broadcast_to(yi[sub - 1:sub], (sub, LANES))
        carry_ref[0, :, cols] = cr
        carry_ref[1, :, cols] = ci
        sr = jnp.concatenate(outr, axis=0)
        si = jnp.concatenate(outi, axis=0)
        first = row == 0
        spr_ref[j] = jnp.where(first, rows_of(cr0, 0), pltpu.roll(sr, 1, 0))
        spi_ref[j] = jnp.where(first, rows_of(ci0, 0), pltpu.roll(si, 1, 0))

    def cross(it, carry):
        for k in range(unroll):
            jp = it * unroll + k
            r0 = pl.multiple_of(jp * 2 * gp, 2 * gp)
            yc = (_dot_nt(ctr_ref[jp], spr_ref[jp].astype(BF16))
                  + _dot_nt(cti_ref[jp], spi_ref[jp].astype(BF16)))
            yt_ref[:, pl.ds(r0, 2 * gp), :] += yc.reshape(S5_T, 2 * gp, tc)
        return carry

    lax.fori_loop(0, ng // 2 // unroll, cross, 0)

    for s in range(S5_T):
        for k in range(ncb):
            ys_ref[k, pl.ds(s, tc, stride=S5_T), :] = yt_ref[s, k * LANES:(k + 1) * LANES, :].T
    for k in range(ncb):
        y_ref[:, k * LANES:(k + 1) * LANES] = ys_ref[k]


def _s5_matrices(lam_re, lam_im, log_step, b_re, b_im, c_re, c_im, d_skip):
    hi = lax.Precision.HIGHEST
    t, gp, n, ng = S5_T, S5_GROUP, S5_STATE, S5_GROUPS
    lam = lax.complex(lam_re.astype(F32), lam_im.astype(F32))
    step = jnp.exp(log_step.astype(F32))[:, None]
    ls = lam * step
    a_bar = jnp.exp(ls)
    b_bar = ((a_bar - 1.0) / lam)[..., None] * lax.complex(b_re.astype(F32), b_im.astype(F32))
    cm = lax.complex(c_re.astype(F32), c_im.astype(F32))

    def apow(k):
        kk = k.astype(F32).astype(jnp.complex64)
        return jnp.exp(ls.reshape((ng,) + (1,) * k.ndim + (n,)) * kk[None, ..., None])

    tt = jnp.arange(t)
    kmat = jnp.einsum('gpn,gln,gnq->glpq', cm, apow(tt), b_bar, precision=hi).real
    krev = jnp.transpose(kmat[:, ::-1], (0, 2, 1, 3)).reshape(ng, gp, t * gp)
    kpad = jnp.pad(krev, ((0, 0), (0, 0), (0, t * gp)))
    mt = jnp.concatenate([kpad[:, :, (t - 1 - to) * gp:(2 * t - 1 - to) * gp] for to in range(t)], axis=1)
    dvec = jnp.tile(d_skip.astype(F32).reshape(ng, 1, gp), (1, t, 1)).reshape(ng, t * gp)
    mt = mt + jnp.eye(t * gp, dtype=F32)[None] * dvec[:, :, None]
    z = jnp.swapaxes(apow(t - 1 - tt), 1, 2)[:, :, :, None] * b_bar[:, :, None, :]
    z = z.reshape(ng, n, t * gp)
    bt = jnp.concatenate([z.real, z.imag], axis=1)
    w = cm[:, None, :, :] * apow(tt + 1)[:, :, None, :]

    def pair_readout(x):
        x = x.reshape(ng // 2, 2, t, gp, n)
        first = jnp.pad(x[:, 0], ((0, 0), (0, 0), (0, 0), (0, n)))
        second = jnp.pad(x[:, 1], ((0, 0), (0, 0), (0, 0), (n, 0)))
        return jnp.stack([first, second], axis=2).reshape(ng // 2, t * 2 * gp, 2 * n).astype(BF16)

    ctr, cti = pair_readout(w.real), pair_readout(-w.imag)
    a_chunk = jnp.transpose(apow(t * (jnp.arange(SUBLANES) + 1)), (1, 0, 2)).reshape(SUBLANES, ng * n)
    return mt.astype(BF16), bt.astype(BF16), ctr, cti, a_chunk.real, a_chunk.imag


def _s5(proj, mats):
    seq, width = proj.shape
    t, tc, gp, n, ng = S5_T, S5_TC, S5_GROUP, S5_STATE, S5_GROUPS
    rows = t * tc
    ncb = S5_CH // LANES
    cb0 = (width - S5_CH) // LANES
    u_specs = [pl.BlockSpec((rows, LANES), (lambda i, k=k: (i, cb0 + k))) for k in range(ncb)]
    nsb = ng * n // LANES
    return pl.pallas_call(
        _s5_kernel,
        grid=(seq // rows,),
        in_specs=u_specs + [_const_spec(m.shape) for m in mats],
        out_specs=pl.BlockSpec((rows, S5_CH), lambda i: (i, 0)),
        out_shape=jax.ShapeDtypeStruct((seq, S5_CH), F32),
        scratch_shapes=[
            pltpu.VMEM((t, S5_CH, tc), F32),
            pltpu.VMEM((t, S5_CH, tc), F32),
            pltpu.VMEM((ncb, rows, LANES), F32),
            pltpu.VMEM((ng * n, tc), F32),
            pltpu.VMEM((ng * n, tc), F32),
            pltpu.VMEM((nsb, tc, LANES), F32),
            pltpu.VMEM((nsb, tc, LANES), F32),
            pltpu.VMEM((2, SUBLANES, ng * n), F32),
        ],
        compiler_params=pltpu.CompilerParams(dimension_semantics=("arbitrary",)),
        name="s5_scan",
    )(*([proj] * ncb), *mats)


def _mix_ffn_kernel(*refs, glu, final):
    (x_ref, a_ref, b_ref, wo_ref, g1_ref), refs = refs[:5], refs[5:]
    if glu:
        gw_ref, refs = refs[0], refs[1:]
    (g_ref, sc_ref, sh_ref, gate_ref, win_ref, cw_ref, cb_ref, wout_ref), refs = refs[:8], refs[8:]
    if final:
        fg_ref, o_ref, h_ref, act_ref, gbuf_ref, carry_ref = refs
    else:
        ng_ref, nsc_ref, nsh_ref, o_ref, hn_ref, h_ref, act_ref, gbuf_ref, carry_ref = refs
    tm = x_ref.shape[0]
    halo = gbuf_ref.shape[0] - tm

    @pl.when(pl.program_id(0) == 0)
    def _():
        carry_ref[...] = jnp.zeros(carry_ref.shape, F32)

    if glu:
        y = jax.nn.gelu(b_ref[...]).astype(BF16)
        gg = _dot(y, gw_ref[...])
        half = gg.shape[1] // 2
        b = (gg[:, :half] * jax.nn.sigmoid(gg[:, half:])).astype(BF16)
    else:
        b = b_ref[...]
    cat = jnp.concatenate([a_ref[...], b], axis=1)
    x = x_ref[...] + g1_ref[...] * _dot(cat, wo_ref[...])
    h_ref[...] = _mod_rmsnorm(x, g_ref[...], sc_ref[...], sh_ref[...]).astype(BF16)
    for f in range(D_FF // TF_FFN):
        cs = slice(f * TF_FFN, (f + 1) * TF_FFN)
        gs = slice(D_FF + f * TF_FFN, D_FF + (f + 1) * TF_FFN)
        h = h_ref[...]
        val = _dot(h, win_ref[:, cs])
        gate = _dot(h, win_ref[:, gs])
        gbuf_ref[0:halo, :] = carry_ref[:, cs]
        gbuf_ref[halo:halo + tm, :] = gate
        carry_ref[:, cs] = gate[tm - halo:tm, :]
        conv = (gate * cw_ref[2:3, cs] + gbuf_ref[halo - 1:halo - 1 + tm, :] * cw_ref[1:2, cs]
                + gbuf_ref[halo - 2:halo - 2 + tm, :] * cw_ref[0:1, cs] + cb_ref[:, cs])
        act_ref[:, cs] = (jax.nn.gelu(conv) * val).astype(BF16)
    xn = x + gate_ref[...] * _dot(act_ref[...], wout_ref[...])
    if final:
        xn = xn * lax.rsqrt(jnp.mean(xn * xn, axis=-1, keepdims=True) + EPS) * fg_ref[...]
    else:
        hn_ref[...] = _mod_rmsnorm(xn, ng_ref[...], nsc_ref[...], nsh_ref[...]).astype(BF16)
    o_ref[...] = xn


def _layer_spec(shape, layer):
    idx = (layer,) + (0,) * (len(shape) - 1)
    return pl.BlockSpec((None,) + tuple(shape[1:]), lambda *_: idx, pipeline_mode=pl.Buffered(1))


def _mix_ffn(x, a, b, wo, gate1, glu_w, g, scale, shift, gate2, w_in, conv_w, conv_b, w_out, tail, layer):
    seq, d = x.shape
    final = len(tail) == 1
    tm = TM_FFN
    halo = SUBLANES
    row = pl.BlockSpec((1, d), lambda i: (0, 0))
    rows = lambda w: pl.BlockSpec((tm, w), lambda i: (i, 0))
    conv_b = conv_b.reshape(conv_b.shape[0], 1, D_FF)
    in_specs = [rows(d), rows(a.shape[1]), rows(b.shape[1]), _const_spec(wo.shape), row]
    args = [x, a, b, wo, gate1]
    if glu_w is not None:
        in_specs.append(_const_spec(glu_w.shape))
        args.append(glu_w)
    in_specs += [
        row, row, row, row,
        _layer_spec(w_in.shape, layer),
        _layer_spec(conv_w.shape, layer),
        _layer_spec(conv_b.shape, layer),
        _layer_spec(w_out.shape, layer),
    ] + [row] * len(tail)
    args += [g.reshape(1, d), scale, shift, gate2, w_in, conv_w, conv_b, w_out]
    args += [t.reshape(1, d) for t in tail]
    out_specs = [rows(d)] if final else [rows(d), rows(d)]
    out_shape = [jax.ShapeDtypeStruct((seq, d), F32)] + ([] if final else [jax.ShapeDtypeStruct((seq, d), BF16)])
    return pl.pallas_call(
        functools.partial(_mix_ffn_kernel, glu=glu_w is not None, final=final),
        grid=(seq // tm,),
        in_specs=in_specs,
        out_specs=out_specs,
        out_shape=out_shape,
        scratch_shapes=[
            pltpu.VMEM((tm, d), BF16),
            pltpu.VMEM((tm, D_FF), BF16),
            pltpu.VMEM((tm + halo, TF_FFN), F32),
            pltpu.VMEM((halo, D_FF), F32),
        ],
        compiler_params=pltpu.CompilerParams(dimension_semantics=("arbitrary",)),
        name="mix_ffn",
    )(*args)


def kernel(x, c, t5_table, mod_w, mod_b, norm1_g, norm2_g, ffn_w_in, ffn_conv_w, ffn_conv_b, ffn_w_out,
           ev_w_in, ev_w_out, diff_lambda, diff_subln_g, band_rel_bias,
           od_w_in, od_w_out, s5_lam_re, s5_lam_im, s5_log_step, s5_b_re, s5_b_im, s5_c_re, s5_c_im,
           s5_d, s5_glu_w, final_g):
    assert x.shape[0] == 1 and x.shape[2] == D_MODEL
    seq = x.shape[1]
    assert seq % TM_PROJ == 0 and seq % (S5_T * S5_TC) == 0
    d = D_MODEL
    xs = x[0]
    mod = _modulation(c, mod_w, mod_b)
    ffn_w_in_b = ffn_w_in.astype(BF16)
    ffn_w_out_b = ffn_w_out.astype(BF16)
    mods = [[mod[i, :, k * d:(k + 1) * d] for k in range(6)] for i in range(DEPTH)]
    h = None
    for i in range(DEPTH):
        sh1, sc1, g1, sh2, sc2, g2 = mods[i]
        w_in = (ev_w_in if i % 2 == 0 else od_w_in)[i // 2].astype(BF16)
        proj_dtype = BF16 if i % 2 == 0 else F32
        if h is None:
            proj = _normproj(xs, norm1_g[i], sc1, sh1, w_in, proj_dtype)
        else:
            proj = _proj(h, w_in, proj_dtype)
        if i % 2 == 0:
            e = i // 2
            lam_init = 0.8 - 0.6 * math.exp(-0.3 * i)
            lp = diff_lambda[e].astype(F32)
            lam = jnp.exp(jnp.sum(lp[0] * lp[1])) - jnp.exp(jnp.sum(lp[2] * lp[3])) + lam_init
            mix_a = _diff_attention(proj, t5_table, lam, diff_subln_g[e], lam_init)
            mix_b = _band_attention(proj, band_rel_bias[e])
            wo, glu_w = ev_w_out[e].astype(BF16), None
        else:
            o = i // 2
            mix_a = _retention(proj)
            mats = _s5_matrices(s5_lam_re[o], s5_lam_im[o], s5_log_step[o], s5_b_re[o], s5_b_im[o],
                                s5_c_re[o], s5_c_im[o], s5_d[o])
            mix_b = _s5(proj, mats)
            wo, glu_w = od_w_out[o].astype(BF16), s5_glu_w[o].astype(BF16)
        if i == DEPTH - 1:
            tail = (final_g,)
        else:
            nsh1, nsc1 = mods[i + 1][0], mods[i + 1][1]
            tail = (norm1_g[i + 1], nsc1, nsh1)
        out = _mix_ffn(xs, mix_a, mix_b, wo, g1, glu_w, norm2_g[i], sc2, sh2, g2,
                       ffn_w_in_b, ffn_conv_w, ffn_conv_b, ffn_w_out_b, tail, layer=i)
        if i == DEPTH - 1:
            xs = out[0]
        else:
            xs, h = out
    return xs[None]
```

```python
import functools
import math

import jax
import jax.numpy as jnp
from jax import lax
from jax.experimental import pallas as pl
from jax.experimental.pallas import tpu as pltpu

F32 = jnp.float32
BF16 = jnp.bfloat16

D_MODEL = 1024
DEPTH = 2
CHUNK = 64
GROUP_WIDTH = D_MODEL // 2
DK_A = 64
DV_A = 2 * DK_A
N_HEADS_A = GROUP_WIDTH // DV_A
DH_B = 64
N_HEADS_B = GROUP_WIDTH // DH_B
LEFT_CHUNKS = 8
REL_CLIP = 2 * CHUNK
NUM_BUCKETS = 32
MAX_DISTANCE = 128
DV_C = 128
DQK_C = DV_C // 2
N_HEADS_C = GROUP_WIDTH // DV_C
ROPE_BASE = 10000.0
S5_CH = GROUP_WIDTH
S5_GROUP = 16
S5_GROUPS = S5_CH // S5_GROUP
S5_STATE = 64
D_FF = ((8 * D_MODEL // 3 + 255) // 256) * 256
CONV_W = 3
EVEN_IN = 3 * N_HEADS_A * DV_A + 3 * N_HEADS_B * DH_B
ODD_IN = 2 * N_HEADS_C * DQK_C + 2 * N_HEADS_C * DV_C + S5_CH
EPS = 1e-6
NEG_INF = -1e30
LOG2E = math.log2(math.e)

LANES = 128
SUBLANES = 8
MXU_DIM = 256

TM_PROJ = 1024
TN_PROJ = 1024
TN_MOD = 1536
TM_FFN = 512
TF_FFN = MXU_DIM
BLK_A = 512
NPART_A = 2
ONES_A = 16
SINGLE_PASS_LOG2_RANGE = 96.0
SCORE_PAD = LANES
BLK_B = 1024
BAND_B = LEFT_CHUNKS * CHUNK
QW_B = 4 * CHUNK
BLK_C = 512
S5_T = 16
S5_TC = LANES

assert BLK_B % BAND_B == 0 and BLK_B % QW_B == 0 and BAND_B % QW_B == 0
assert BLK_A >= MAX_DISTANCE, "far key blocks must sit in the saturated T5 bucket"
assert DV_A == LANES and 2 * DK_A == LANES and 2 * DH_B == LANES, "attention heads are read as 128-lane column blocks"


def _dot(a, b):
    return jnp.dot(a, b, preferred_element_type=F32)


def _dot_nt(a, b):
    return lax.dot_general(a, b, (((1,), (1,)), ((), ())), preferred_element_type=F32)


def _dot_tn(a, b):
    return lax.dot_general(a, b, (((0,), (0,)), ((), ())), preferred_element_type=F32)


def _const_spec(shape):
    zeros = (0,) * len(shape)
    return pl.BlockSpec(shape, lambda *_: zeros, pipeline_mode=pl.Buffered(1))


def _mod_rmsnorm(x, g, scale, shift):
    y = x * lax.rsqrt(jnp.mean(x * x, axis=-1, keepdims=True) + EPS)
    y = y * g
    return y * (1.0 + scale) + shift


def _mod_kernel(c_ref, w_ref, b_ref, o_ref):
    c = c_ref[...]
    cond = c * jax.nn.sigmoid(c)
    o_ref[0] = jnp.sum(cond * w_ref[0], axis=0, keepdims=True) + b_ref[0]


def _modulation(c, mod_w, mod_b):
    depth, d, n = mod_w.shape
    tn = TN_MOD
    return pl.pallas_call(
        _mod_kernel,
        grid=(depth, n // tn),
        in_specs=[
            pl.BlockSpec((d, 1), lambda i, j: (0, 0)),
            pl.BlockSpec((1, d, tn), lambda i, j: (i, 0, j)),
            pl.BlockSpec((1, 1, tn), lambda i, j: (i, 0, j)),
        ],
        out_specs=pl.BlockSpec((1, 1, tn), lambda i, j: (i, 0, j)),
        out_shape=jax.ShapeDtypeStruct((depth, 1, n), F32),
        name="modulation",
    )(c.reshape(d, 1), mod_w, mod_b.reshape(depth, 1, n))


def _normproj_kernel(x_ref, g_ref, sc_ref, sh_ref, w_ref, o_ref):
    tm, n = o_ref.shape
    half = tm // 2
    for r in range(2):
        rows = slice(r * half, (r + 1) * half)
        h = _mod_rmsnorm(x_ref[rows, :], g_ref[...], sc_ref[...], sh_ref[...]).astype(BF16)
        for j in range(n // TN_PROJ):
            cols = slice(j * TN_PROJ, (j + 1) * TN_PROJ)
            o_ref[rows, cols] = _dot(h, w_ref[:, cols]).astype(o_ref.dtype)


def _normproj(x, g, scale, shift, w, out_dtype):
    seq, d = x.shape
    n = w.shape[1]
    tm = TM_PROJ
    row = pl.BlockSpec((1, d), lambda i: (0, 0))
    return pl.pallas_call(
        _normproj_kernel,
        grid=(seq // tm,),
        in_specs=[pl.BlockSpec((tm, d), lambda i: (i, 0)), row, row, row, _const_spec(w.shape)],
        out_specs=pl.BlockSpec((tm, n), lambda i: (i, 0)),
        out_shape=jax.ShapeDtypeStruct((seq, n), out_dtype),
        compiler_params=pltpu.CompilerParams(dimension_semantics=("parallel",)),
        name="normproj",
    )(x, g.reshape(1, d), scale, shift, w)


def _proj_kernel(h_ref, w_ref, o_ref):
    for j in range(o_ref.shape[1] // TN_PROJ):
        cols = slice(j * TN_PROJ, (j + 1) * TN_PROJ)
        o_ref[:, cols] = _dot(h_ref[...], w_ref[:, cols]).astype(o_ref.dtype)


def _proj(h, w, out_dtype):
    seq, d = h.shape
    n = w.shape[1]
    tm = TM_PROJ
    return pl.pallas_call(
        _proj_kernel,
        grid=(seq // tm,),
        in_specs=[pl.BlockSpec((tm, d), lambda i: (i, 0)), _const_spec(w.shape)],
        out_specs=pl.BlockSpec((tm, n), lambda i: (i, 0)),
        out_shape=jax.ShapeDtypeStruct((seq, n), out_dtype),
        compiler_params=pltpu.CompilerParams(dimension_semantics=("parallel",)),
        name="proj",
    )(h, w)


def _diffattn_kernel(q_ref, k_ref, v_ref, bias_ref, bstat_ref, lam_ref, g_ref, o_ref,
                     qs_ref, vt_ref, kmax_ref, r_ref, m_ref, acc_ref, *s_refs, out_scale):
    blk = BLK_A
    nq = 2 * blk
    sub = SUBLANES
    dv = DV_A
    npart = NPART_A
    sa_ref, sb_ref = s_refs[:2 * npart], s_refs[2 * npart:4 * npart]
    pa_ref, pb_ref = s_refs[4 * npart:5 * npart], s_refs[5 * npart:6 * npart]
    i = pl.program_id(1)
    lane = lax.broadcasted_iota(jnp.int32, (blk, LANES), 1)
    same_subhead = (lax.broadcasted_iota(jnp.int32, (LANES, LANES), 0) // DK_A
                    == lax.broadcasted_iota(jnp.int32, (LANES, LANES), 1) // DK_A).astype(BF16)

    @pl.when(i == 0)
    def _():
        kmax_ref[...] = jnp.zeros(kmax_ref.shape, F32)

        def tr(b, carry):
            r0 = pl.multiple_of(b * blk, blk)
            vt_ref[0:dv, pl.ds(r0, blk)] = v_ref[pl.ds(r0, blk), :].astype(F32).T.astype(BF16)
            vt_ref[dv:dv + ONES_A, pl.ds(r0, blk)] = jnp.ones((ONES_A, blk), BF16)
            kf = k_ref[pl.ds(r0, blk), :].astype(F32)
            kn2 = _dot((kf * kf).astype(BF16), same_subhead)
            kmax_ref[...] = jnp.maximum(kmax_ref[...], jnp.max(kn2.reshape(blk // sub, sub, LANES), axis=0))
            return carry
        lax.fori_loop(0, v_ref.shape[0] // blk, tr, 0)
        kmax_ref[...] = jnp.broadcast_to(jnp.max(kmax_ref[...], axis=0, keepdims=True), kmax_ref.shape)

    q = (q_ref[...].astype(F32) * (DK_A ** -0.5 * LOG2E)).astype(BF16)
    qf = q.astype(F32)
    qs_ref[:, 0:blk] = jnp.where(lane < DK_A, qf, 0.0).T.astype(BF16)
    qs_ref[:, blk:nq] = jnp.where(lane >= DK_A, qf, 0.0).T.astype(BF16)
    acc_ref[...] = jnp.zeros(acc_ref.shape, F32)

    qsq = (qf * qf).astype(BF16)
    bound = []
    for m in range(2):
        lanes_m = (lax.broadcasted_iota(jnp.int32, (sub, LANES), 1) // DK_A == m).astype(BF16)
        qn2 = _dot_nt(lanes_m, qsq)
        bound.append(jnp.sqrt(qn2 * kmax_ref[:, m * DK_A:m * DK_A + 1]) * 1.03)
    bound = jnp.concatenate(bound, axis=1)
    bias_max, bias_span = bstat_ref[0, 0:1, 0:1], bstat_ref[0, 1:2, 0:1]
    r_ref[...] = bound + bias_max
    single_pass = jnp.max(2.0 * bound + bias_span) < SINGLE_PASS_LOG2_RANGE

    @pl.when(single_pass)
    def _():
        _diffattn_fixed_shift(i, k_ref, bias_ref, qs_ref, vt_ref, r_ref, acc_ref, pa_ref, pb_ref)

    @pl.when(jnp.logical_not(single_pass))
    def _():
        _diffattn_blocks(i, k_ref, bias_ref, qs_ref, vt_ref, m_ref, acc_ref, sa_ref, sb_ref, None)

    ot = acc_ref[0:dv, 0:nq] / acc_ref[dv:dv + 1, 0:nq]
    o = ot[:, 0:blk].T - lam_ref[...] * ot[:, blk:nq].T
    o = o * lax.rsqrt(jnp.mean(o * o, axis=-1, keepdims=True) + EPS) * g_ref[...]
    o_ref[...] = (o * out_scale).astype(o_ref.dtype)


def _diffattn_fixed_shift(i, k_ref, bias_ref, qs_ref, vt_ref, shift_ref, acc_ref, pa_ref, pb_ref):
    blk = BLK_A
    nq = 2 * blk
    sub = SUBLANES
    npart = len(pa_ref)
    wq = nq // npart

    def probs(b, p_ref, bias):
        k = k_ref[pl.ds(pl.multiple_of(b * blk, blk), blk), :]
        for part in range(npart):
            cols = slice(part * wq, (part + 1) * wq)
            s = _dot(k, qs_ref[:, cols])
            if bias is not None:
                q0 = (part * wq) % blk
                s = s + bias[:, q0:q0 + wq]
            p = jnp.exp2(s.reshape(blk // sub, sub, wq) - shift_ref[:, cols][None])
            p_ref[part][:, 0:wq] = p.reshape(blk, wq).astype(BF16)

    def accumulate(b, p_ref):
        vt = vt_ref[:, pl.ds(pl.multiple_of(b * blk, blk), blk)]
        for part in range(npart):
            cols = slice(part * wq, (part + 1) * wq)
            acc_ref[:, cols] += _dot(vt, p_ref[part][:, 0:wq])

    @pl.when(i == 0)
    def _():
        probs(0, pa_ref, bias_ref[0, 1])
        accumulate(0, pa_ref)

    @pl.when(i > 0)
    def _():
        nfar = i - 1
        probs(i, pa_ref, bias_ref[0, 1])
        probs(i - 1, pb_ref, bias_ref[0, 0])
        accumulate(i, pa_ref)

        def pair(t, carry):
            probs(2 * t, pa_ref, None)
            accumulate(jnp.where(t == 0, i - 1, 2 * t - 1), pb_ref)
            probs(2 * t + 1, pb_ref, None)
            accumulate(2 * t, pa_ref)
            return carry

        npairs = nfar // 2
        lax.fori_loop(0, npairs, pair, 0)
        in_pb = jnp.where(npairs == 0, i - 1, 2 * npairs - 1)

        @pl.when(lax.rem(nfar, 2) == 1)
        def _():
            probs(nfar - 1, pa_ref, None)
            accumulate(in_pb, pb_ref)
            accumulate(nfar - 1, pa_ref)

        @pl.when(lax.rem(nfar, 2) == 0)
        def _():
            accumulate(in_pb, pb_ref)


def _diffattn_blocks(i, k_ref, bias_ref, qs_ref, vt_ref, m_ref, acc_ref, sa_ref, sb_ref, shift_ref):
    blk = BLK_A
    nq = 2 * blk
    sub = SUBLANES
    npart = len(sa_ref) // 2
    wq = nq // npart
    online = shift_ref is None
    if online:
        m_ref[...] = jnp.full(m_ref.shape, NEG_INF, F32)

    def scores(b, s_ref):
        k = k_ref[pl.ds(pl.multiple_of(b * blk, blk), blk), :]
        for part in range(npart):
            s = _dot(k, qs_ref[:, part * wq:(part + 1) * wq])
            s_ref[part][:, 0:wq] = s
            if online:
                s_ref[npart + part][...] = jnp.max(s.reshape(blk // sub, sub, wq), axis=0)

    def softmax_pv(b, s_ref, bias):
        vt = vt_ref[:, pl.ds(pl.multiple_of(b * blk, blk), blk)]
        for part in range(npart):
            cols = slice(part * wq, (part + 1) * wq)
            s = s_ref[part][:, 0:wq]
            if bias is not None:
                q0 = (part * wq) % blk
                s = s + bias[:, q0:q0 + wq]
            s = s.reshape(blk // sub, sub, wq)
            if not online:
                p = jnp.exp2(s - shift_ref[:, cols][None])
                acc_ref[:, cols] += _dot(vt, p.reshape(blk, wq).astype(BF16))
                continue
            m_prev = m_ref[:, cols]
            smax = jnp.max(s, axis=0) if bias is not None else s_ref[npart + part][...]
            m_cur = jnp.max(smax, axis=0, keepdims=True)
            m_new = jnp.maximum(m_prev, m_cur)
            alpha = jnp.exp2(m_prev - m_new)
            p = jnp.exp2(s - m_new[None])
            pv = _dot(vt, p.reshape(blk, wq).astype(BF16))
            acc_ref[:, cols] = acc_ref[:, cols] * alpha[0:1] + pv
            m_ref[:, cols] = m_new

    nfar = jnp.maximum(i - 1, 0)
    odd = lax.rem(nfar, 2)

    @pl.when(i == 0)
    def _():
        scores(0, sb_ref)

    @pl.when(i > 0)
    def _():
        @pl.when(odd == 1)
        def _():
            scores(0, sb_ref)
            scores(1, sa_ref)
            softmax_pv(0, sb_ref, None)

        @pl.when(odd == 0)
        def _():
            scores(0, sa_ref)

        def pair(b):
            scores(b + 1, sb_ref)
            softmax_pv(b, sa_ref, None)
            scores(b + 2, sa_ref)
            softmax_pv(b + 1, sb_ref, None)

        def quad_body(t, carry):
            pair(odd + 4 * t)
            pair(odd + 4 * t + 2)
            return carry

        npairs = nfar // 2
        lax.fori_loop(0, npairs // 2, quad_body, 0)

        @pl.when(lax.rem(npairs, 2) == 1)
        def _():
            pair(odd + 2 * (npairs - 1))
        scores(i, sb_ref)
        softmax_pv(i - 1, sa_ref, bias_ref[0, 0])

    softmax_pv(i, sb_ref, bias_ref[0, 1])


_TOEPLITZ_ROWS = 256
_TOEPLITZ_N = 2048


def _toeplitz_kernel(v_ref, o_ref, *, keep):
    rows, cols = o_ref.shape[2:]
    x = jnp.broadcast_to(v_ref[0, 0], (rows, v_ref.shape[-1]))
    tile = pltpu.roll(x, 0, 1, stride=1, stride_axis=0)[:, :cols]
    r = lax.broadcasted_iota(jnp.int32, (rows, cols), 0) + pl.program_id(1) * rows
    c = lax.broadcasted_iota(jnp.int32, (rows, cols), 1)
    for variant in range(o_ref.shape[0]):
        o_ref[variant, 0] = jnp.where(keep(r, c, variant), tile, NEG_INF)


def _toeplitz_tiles(fn, keep, heads, rows, cols, variants=1):
    n, rb = _TOEPLITZ_N, _TOEPLITZ_ROWS
    assert rows % rb == 0 and rows <= n // 2 and cols <= n // 2
    idx = jnp.arange(n, dtype=jnp.int32)
    vec = fn(jnp.where(idx < n // 2, idx, idx - n)).astype(F32)
    vecs = jnp.stack([jnp.roll(vec, k * rb, axis=1) for k in range(rows // rb)], axis=1)
    return pl.pallas_call(
        functools.partial(_toeplitz_kernel, keep=keep),
        grid=(heads, rows // rb),
        in_specs=[pl.BlockSpec((1, 1, 1, n), lambda h, k: (h, k, 0, 0))],
        out_specs=pl.BlockSpec((variants, 1, rb, cols), lambda h, k: (0, h, k, 0)),
        out_shape=jax.ShapeDtypeStruct((variants, heads, rows, cols), F32),
        name="toeplitz_tiles",
    )(vecs.reshape(heads, rows // rb, 1, n))


def _t5_bucket(rel):
    nb = NUM_BUCKETS // 2
    max_exact = nb // 2
    bucket = jnp.where(rel > 0, nb, 0)
    n = jnp.abs(rel)
    nf = jnp.maximum(n, 1).astype(F32)
    large = max_exact + (jnp.log(nf / max_exact) / math.log(MAX_DISTANCE / max_exact)
                         * (nb - max_exact)).astype(jnp.int32)
    large = jnp.minimum(large, nb - 1)
    return bucket + jnp.where(n < max_exact, n, large)


def _diff_bias_tiles(t5_table):
    blk = BLK_A
    table = t5_table.astype(F32)
    far = table[_t5_bucket(jnp.full((), -(blk + 1), jnp.int32))]
    def visible(r, c, variant):
        return jnp.floor_divide(r - blk, CHUNK) <= jnp.floor_divide(c, CHUNK)

    tiles = _toeplitz_tiles(lambda x: ((table[_t5_bucket(-x - blk)] - far) * LOG2E).T, visible,
                            N_HEADS_A, 2 * blk, blk)
    return tiles.reshape(N_HEADS_A, 2, blk, blk)


def _diff_attention(proj, t5_table, lam, subln_g, lam_init):
    seq = proj.shape[0]
    blk = BLK_A
    bias = _diff_bias_tiles(t5_table)
    ha = N_HEADS_A
    finite = bias > 0.5 * NEG_INF
    bias_max = jnp.maximum(jnp.max(jnp.where(finite, bias, NEG_INF), axis=(1, 2, 3)), 0.0)
    bias_min = jnp.minimum(jnp.min(jnp.where(finite, bias, -NEG_INF), axis=(1, 2, 3)), 0.0)
    bstat = jnp.broadcast_to(jnp.stack([bias_max, bias_max - bias_min], axis=1)[:, :, None], (ha, 2, LANES))
    kern = functools.partial(_diffattn_kernel, out_scale=1.0 - lam_init)
    return pl.pallas_call(
        kern,
        grid=(ha, seq // blk),
        in_specs=[
            pl.BlockSpec((blk, DV_A), lambda h, i: (i, h)),
            pl.BlockSpec((seq, DV_A), lambda h, i: (0, ha + h)),
            pl.BlockSpec((seq, DV_A), lambda h, i: (0, 2 * ha + h)),
            pl.BlockSpec((1, 2, blk, blk), lambda h, i: (h, 0, 0, 0)),
            pl.BlockSpec((1, 2, LANES), lambda h, i: (h, 0, 0)),
            pl.BlockSpec((1, DV_A), lambda h, i: (0, 0)),
            pl.BlockSpec((1, DV_A), lambda h, i: (0, 0)),
        ],
        out_specs=pl.BlockSpec((blk, DV_A), lambda h, i: (i, h)),
        out_shape=jax.ShapeDtypeStruct((seq, ha * DV_A), BF16),
        scratch_shapes=[
            pltpu.VMEM((DV_A, 2 * blk), BF16),
            pltpu.VMEM((DV_A + ONES_A, seq), BF16),
            pltpu.VMEM((SUBLANES, LANES), F32),
            pltpu.VMEM((SUBLANES, 2 * blk), F32),
            pltpu.VMEM((SUBLANES, 2 * blk), F32),
            pltpu.VMEM((DV_A + ONES_A, 2 * blk), F32),
        ] + 2 * ([pltpu.VMEM((blk, 2 * blk // NPART_A + SCORE_PAD), F32)] * NPART_A
                 + [pltpu.VMEM((SUBLANES, 2 * blk // NPART_A), F32)] * NPART_A)
        + 2 * [pltpu.VMEM((blk, 2 * blk // NPART_A + SCORE_PAD), BF16)] * NPART_A,
        compiler_params=pltpu.CompilerParams(dimension_semantics=("parallel", "arbitrary")),
        name="diff_attention",
    )(proj, proj, proj, bias, bstat, jnp.full((1, DV_A), lam, F32), subln_g.reshape(1, DV_A).astype(F32))


def _band_kernel(q_ref, kp_ref, kc_ref, vp_ref, vc_ref, *refs):
    qw, band = QW_B, BAND_B
    nbias = band // qw + 1
    bias_refs, o_ref, s_refs = refs[:nbias], refs[nbias], refs[nbias + 1:]
    nk = band + qw
    sub = SUBLANES
    q = q_ref[...].astype(F32) * (DH_B ** -0.5 * LOG2E)
    lane = lax.broadcasted_iota(jnp.int32, q.shape, 1)
    qh = (jnp.where(lane < DH_B, q, 0.0).T.astype(BF16), jnp.where(lane >= DH_B, q, 0.0).T.astype(BF16))
    k_all = jnp.concatenate([kp_ref[...], kc_ref[...]], axis=0)
    vt_all = jnp.concatenate([vp_ref[...], vc_ref[...]], axis=0).astype(F32).T.astype(BF16)
    vt_all = jnp.concatenate([vt_all, jnp.ones((ONES_A, vt_all.shape[1]), BF16)], axis=0)
    ngroups = len(s_refs)
    for g in range(ngroups):
        k0 = g * qw
        qs = jnp.concatenate([qh[0][:, k0:k0 + qw], qh[1][:, k0:k0 + qw]], axis=1)
        s_refs[g][:, 0:2 * qw] = _dot(k_all[k0:k0 + nk], qs)
    for g in range(ngroups):
        k0 = g * qw
        bias_ref = bias_refs[min(g, nbias - 1)]
        bias = jnp.concatenate([bias_ref[0, 0], bias_ref[0, 1]], axis=1)
        s = (s_refs[g][:, 0:2 * qw] + bias).reshape(nk // sub, sub, 2 * qw)
        m = jnp.max(jnp.max(s, axis=0), axis=0, keepdims=True)
        p = jnp.exp2(s - m[None])
        pv = _dot(vt_all[:, k0:k0 + nk], p.reshape(nk, 2 * qw).astype(BF16))
        ot = pv[0:2 * DH_B] / pv[2 * DH_B:2 * DH_B + 1]
        o = jnp.concatenate([ot[0:DH_B, 0:qw], ot[DH_B:2 * DH_B, qw:2 * qw]], axis=0)
        o_ref[k0:k0 + qw, :] = o.T.astype(o_ref.dtype)


def _band_bias_tiles(rel_bias):
    band = BAND_B

    def valid(r, c, variant):
        qchunk = jnp.floor_divide(c, CHUNK)
        kchunk = jnp.floor_divide(r - band, CHUNK)
        missing = jnp.where(variant == 0, 0, band - (variant - 1) * QW_B)
        return (kchunk <= qchunk) & (kchunk >= qchunk - LEFT_CHUNKS) & (r >= missing)

    return _toeplitz_tiles(
        lambda x: rel_bias.astype(F32)[:, jnp.clip(-x - band, -REL_CLIP, REL_CLIP) + REL_CLIP] * LOG2E, valid,
        N_HEADS_B, band + QW_B, QW_B, variants=1 + band // QW_B)


def _band_attention(proj, rel_bias):
    seq = proj.shape[0]
    blk, band, qw = BLK_B, BAND_B, QW_B
    bias = _band_bias_tiles(rel_bias)
    npair = N_HEADS_B // 2
    qc0 = 3 * N_HEADS_A
    per = blk // band
    prev = lambda c0: (lambda hp, i: (jnp.maximum(i * per - 1, 0), c0 + hp))
    cur = lambda c0: (lambda hp, i: (i, c0 + hp))
    return pl.pallas_call(
        _band_kernel,
        grid=(npair, seq // blk),
        in_specs=[
            pl.BlockSpec((blk, LANES), cur(qc0)),
            pl.BlockSpec((band, LANES), prev(qc0 + npair)),
            pl.BlockSpec((blk, LANES), cur(qc0 + npair)),
            pl.BlockSpec((band, LANES), prev(qc0 + 2 * npair)),
            pl.BlockSpec((blk, LANES), cur(qc0 + 2 * npair)),
        ] + [
            pl.BlockSpec((1, 2, band + qw, qw), (lambda hp, i, t=t: (jnp.where(i == 0, 1 + t, 0), hp, 0, 0)))
            for t in range(band // qw)
        ] + [
            pl.BlockSpec((1, 2, band + qw, qw), lambda hp, i: (0, hp, 0, 0)),
        ],
        out_specs=pl.BlockSpec((blk, LANES), lambda hp, i: (i, hp)),
        out_shape=jax.ShapeDtypeStruct((seq, N_HEADS_B * DH_B), BF16),
        scratch_shapes=[pltpu.VMEM((band + qw, 2 * qw + SCORE_PAD), F32)] * (blk // qw),
        compiler_params=pltpu.CompilerParams(dimension_semantics=("parallel", "arbitrary")),
        name="band_attention",
    )(proj, proj, proj, proj, proj, *([bias] * bias.shape[0]))


def _retention_kernel(qk_ref, v_ref, gate_ref, cos_ref, sin_ref, qdec_ref, kdec_ref, dmat_ref,
                      sdec_ref, o_ref, state_ref):
    @pl.when(pl.program_id(0) == 0)
    def _():
        state_ref[...] = jnp.zeros(state_ref.shape, F32)

    cos = cos_ref[...]
    sin = sin_ref[...]
    lane = lax.broadcasted_iota(jnp.int32, cos.shape, 1)
    first_half = (lane % DQK_C) < (DQK_C // 2)
    qk = qk_ref[...]
    parts = []
    for j in range(qk.shape[1] // LANES):
        t = qk[:, j * LANES:(j + 1) * LANES]
        partner = jnp.where(first_half, pltpu.roll(t, LANES - DQK_C // 2, 1), pltpu.roll(t, DQK_C // 2, 1))
        parts.append(t * cos + partner * sin)
    wq = N_HEADS_C * DQK_C
    q = jnp.concatenate(parts[:wq // LANES], axis=1)
    k = jnp.concatenate(parts[wq // LANES:], axis=1) * (DQK_C ** -0.5)
    qd = (q * qdec_ref[...]).astype(BF16)
    kd = (k * kdec_ref[...]).astype(BF16)
    qb = q.astype(BF16)
    kb = k.astype(BF16)
    vb = v_ref[...].astype(BF16)
    gate = gate_ref[...]
    outs = []
    for h in range(N_HEADS_C):
        qs = slice(h * DQK_C, (h + 1) * DQK_C)
        vs = slice(h * DV_C, (h + 1) * DV_C)
        scores = _dot_nt(qb[:, qs], kb[:, qs]) * dmat_ref[h]
        state = state_ref[h]
        r = _dot(scores.astype(BF16), vb[:, vs]) + _dot(qd[:, qs], state.astype(BF16))
        state_ref[h] = state * sdec_ref[h] + _dot_tn(kd[:, qs], vb[:, vs])
        r = r * lax.rsqrt(jnp.mean(r * r, axis=-1, keepdims=True) + EPS)
        g = gate[:, vs]
        outs.append(r * (g * jax.nn.sigmoid(g)))
    o_ref[...] = jnp.concatenate(outs, axis=1).astype(o_ref.dtype)


def _retention_tables(seq):
    t = BLK_C
    half = DQK_C // 2
    inv_freq = 1.0 / (ROPE_BASE ** (jnp.arange(0, DQK_C, 2, dtype=F32) / DQK_C))
    ang = jnp.arange(seq, dtype=F32)[:, None] * inv_freq[None, :]
    reps = LANES // half
    cos = jnp.tile(jnp.cos(ang), (1, reps))
    sign = jnp.where((jnp.arange(LANES) % DQK_C) < half, -1.0, 1.0).astype(F32)
    sin = jnp.tile(jnp.sin(ang), (1, reps)) * sign[None, :]
    log_g = jnp.log(1.0 - jnp.power(2.0, -5.0 - jnp.arange(N_HEADS_C, dtype=F32)))
    pos = jnp.arange(t, dtype=F32)
    diff = pos[:, None] - pos[None, :]
    same_or_past = (jnp.arange(t)[None, :] // CHUNK) <= (jnp.arange(t)[:, None] // CHUNK)
    dmat = jnp.where(same_or_past[None], jnp.exp(log_g[:, None, None] * jnp.abs(diff)[None]), 0.0)
    qdec = jnp.repeat(jnp.exp(log_g[None, :] * (pos[:, None] + 1.0)), DQK_C, axis=1)
    kdec = jnp.repeat(jnp.exp(log_g[None, :] * (t - 1.0 - pos[:, None])), DQK_C, axis=1)
    sdec = jnp.broadcast_to(jnp.exp(log_g * t)[:, None, None], (N_HEADS_C, 1, DV_C))
    return cos, sin, qdec, kdec, dmat, sdec


def _retention(proj):
    seq = proj.shape[0]
    t = BLK_C
    cos, sin, qdec, kdec, dmat, sdec = _retention_tables(seq)
    wv = N_HEADS_C * DV_C
    return pl.pallas_call(
        _retention_kernel,
        grid=(seq // t,),
        in_specs=[
            pl.BlockSpec((t, wv), lambda i: (i, 0)),
            pl.BlockSpec((t, wv), lambda i: (i, 1)),
            pl.BlockSpec((t, wv), lambda i: (i, 2)),
            pl.BlockSpec((t, LANES), lambda i: (i, 0)),
            pl.BlockSpec((t, LANES), lambda i: (i, 0)),
            pl.BlockSpec((t, N_HEADS_C * DQK_C), lambda i: (0, 0)),
            pl.BlockSpec((t, N_HEADS_C * DQK_C), lambda i: (0, 0)),
            pl.BlockSpec((N_HEADS_C, t, t), lambda i: (0, 0, 0)),
            pl.BlockSpec((N_HEADS_C, 1, DV_C), lambda i: (0, 0, 0)),
        ],
        out_specs=pl.BlockSpec((t, wv), lambda i: (i, 0)),
        out_shape=jax.ShapeDtypeStruct((seq, wv), BF16),
        scratch_shapes=[pltpu.VMEM((N_HEADS_C, DQK_C, DV_C), F32)],
        compiler_params=pltpu.CompilerParams(dimension_semantics=("arbitrary",)),
        name="retention",
    )(proj, proj, proj, cos, sin, qdec, kdec, dmat, sdec)


def _s5_kernel(*refs):
    ncb = S5_CH // LANES
    u_refs = refs[:ncb]
    (mt_ref, bt_ref, ctr_ref, cti_ref, are_ref, aim_ref, y_ref,
     ut_ref, yt_ref, ys_ref, vr_ref, vi_ref, spr_ref, spi_ref, carry_ref) = refs[ncb:]
    tc = S5_TC
    gp = S5_GROUP
    n = S5_STATE
    ng = S5_GROUPS

    @pl.when(pl.program_id(0) == 0)
    def _():
        carry_ref[...] = jnp.zeros(carry_ref.shape, F32)

    for s in range(S5_T):
        for k in range(ncb):
            ut_ref[s, k * LANES:(k + 1) * LANES, :] = u_refs[k][pl.ds(s, tc, stride=S5_T), :].T

    unroll = 4

    def intra(it, carry):
        for k in range(unroll):
            g = it * unroll + k
            r0 = pl.multiple_of(g * gp, gp)
            ug = ut_ref[:, pl.ds(r0, gp), :].reshape(S5_T * gp, tc).astype(BF16)
            yt_ref[:, pl.ds(r0, gp), :] = _dot(mt_ref[g], ug).reshape(S5_T, gp, tc)
            vt = _dot(bt_ref[g], ug)
            n0 = pl.multiple_of(g * n, n)
            vr_ref[pl.ds(n0, n), :] = vt[0:n]
            vi_ref[pl.ds(n0, n), :] = vt[n:2 * n]
        return carry

    lax.fori_loop(0, ng // unroll, intra, 0)

    sub = SUBLANES
    nv = tc // sub
    row = lax.broadcasted_iota(jnp.int32, (tc, LANES), 0)
    in_vreg = lax.rem(row, sub)

    def rows_of(v, r):
        return jnp.broadcast_to(v[r:r + 1], (tc, LANES))

    for j in range(ng * n // LANES):
        cols = slice(j * LANES, (j + 1) * LANES)
        pwr, pwi = are_ref[:, cols], aim_ref[:, cols]
        xr = vr_ref[cols, :].T
        xi = vi_ref[cols, :].T
        for d in (1, 2, 4):
            keep = in_vreg >= d
            sr = jnp.where(keep, pltpu.roll(xr, d, 0), 0.0)
            si = jnp.where(keep, pltpu.roll(xi, d, 0), 0.0)
            fr, fi = rows_of(pwr, d - 1), rows_of(pwi, d - 1)
            xr, xi = xr + (fr * sr - fi * si), xi + (fr * si + fi * sr)
        cr, ci = carry_ref[0, :, cols], carry_ref[1, :, cols]
        cr0, ci0 = cr, ci
        outr, outi = [], []
        for v in range(nv):
            yr = xr[v * sub:(v + 1) * sub] + (pwr * cr - pwi * ci)
            yi = xi[v * sub:(v + 1) * sub] + (pwr * ci + pwi * cr)
            outr.append(yr)
            outi.append(yi)
            cr = jnp.broadcast_to(yr[sub - 1:sub], (sub, LANES))
            ci = jnp.broadcast_to(yi[sub - 1:sub], (sub, LANES))
        carry_ref[0, :, cols] = cr
        carry_ref[1, :, cols] = ci
        sr = jnp.concatenate(outr, axis=0)
        si = jnp.concatenate(outi, axis=0)
        first = row == 0
        spr_ref[j] = jnp.where(first, rows_of(cr0, 0), pltpu.roll(sr, 1, 0))
        spi_ref[j] = jnp.where(first, rows_of(ci0, 0), pltpu.roll(si, 1, 0))

    def cross(it, carry):
        for k in range(unroll):
            jp = it * unroll + k
            r0 = pl.multiple_of(jp * 2 * gp, 2 * gp)
            yc = (_dot_nt(ctr_ref[jp], spr_ref[jp].astype(BF16))
                  + _dot_nt(cti_ref[jp], spi_ref[jp].astype(BF16)))
            yt_ref[:, pl.ds(r0, 2 * gp), :] += yc.reshape(S5_T, 2 * gp, tc)
        return carry

    lax.fori_loop(0, ng // 2 // unroll, cross, 0)

    for s in range(S5_T):
        for k in range(ncb):
            ys_ref[k, pl.ds(s, tc, stride=S5_T), :] = yt_ref[s, k * LANES:(k + 1) * LANES, :].T
    for k in range(ncb):
        y_ref[:, k * LANES:(k + 1) * LANES] = ys_ref[k]


def _s5_matrices(lam_re, lam_im, log_step, b_re, b_im, c_re, c_im, d_skip):
    hi = lax.Precision.HIGHEST
    t, gp, n, ng = S5_T, S5_GROUP, S5_STATE, S5_GROUPS
    lam = lax.complex(lam_re.astype(F32), lam_im.astype(F32))
    step = jnp.exp(log_step.astype(F32))[:, None]
    ls = lam * step
    a_bar = jnp.exp(ls)
    b_bar = ((a_bar - 1.0) / lam)[..., None] * lax.complex(b_re.astype(F32), b_im.astype(F32))
    cm = lax.complex(c_re.astype(F32), c_im.astype(F32))

    def apow(k):
        kk = k.astype(F32).astype(jnp.complex64)
        return jnp.exp(ls.reshape((ng,) + (1,) * k.ndim + (n,)) * kk[None, ..., None])

    tt = jnp.arange(t)
    kmat = jnp.einsum('gpn,gln,gnq->glpq', cm, apow(tt), b_bar, precision=hi).real
    krev = jnp.transpose(kmat[:, ::-1], (0, 2, 1, 3)).reshape(ng, gp, t * gp)
    kpad = jnp.pad(krev, ((0, 0), (0, 0), (0, t * gp)))
    mt = jnp.concatenate([kpad[:, :, (t - 1 - to) * gp:(2 * t - 1 - to) * gp] for to in range(t)], axis=1)
    dvec = jnp.tile(d_skip.astype(F32).reshape(ng, 1, gp), (1, t, 1)).reshape(ng, t * gp)
    mt = mt + jnp.eye(t * gp, dtype=F32)[None] * dvec[:, :, None]
    z = jnp.swapaxes(apow(t - 1 - tt), 1, 2)[:, :, :, None] * b_bar[:, :, None, :]
    z = z.reshape(ng, n, t * gp)
    bt = jnp.concatenate([z.real, z.imag], axis=1)
    w = cm[:, None, :, :] * apow(tt + 1)[:, :, None, :]

    def pair_readout(x):
        x = x.reshape(ng // 2, 2, t, gp, n)
        first = jnp.pad(x[:, 0], ((0, 0), (0, 0), (0, 0), (0, n)))
        second = jnp.pad(x[:, 1], ((0, 0), (0, 0), (0, 0), (n, 0)))
        return jnp.stack([first, second], axis=2).reshape(ng // 2, t * 2 * gp, 2 * n).astype(BF16)

    ctr, cti = pair_readout(w.real), pair_readout(-w.imag)
    a_chunk = jnp.transpose(apow(t * (jnp.arange(SUBLANES) + 1)), (1, 0, 2)).reshape(SUBLANES, ng * n)
    return mt.astype(BF16), bt.astype(BF16), ctr, cti, a_chunk.real, a_chunk.imag


def _s5(proj, mats):
    seq, width = proj.shape
    t, tc, gp, n, ng = S5_T, S5_TC, S5_GROUP, S5_STATE, S5_GROUPS
    rows = t * tc
    ncb = S5_CH // LANES
    cb0 = (width - S5_CH) // LANES
    u_specs = [pl.BlockSpec((rows, LANES), (lambda i, k=k: (i, cb0 + k))) for k in range(ncb)]
    nsb = ng * n // LANES
    return pl.pallas_call(
        _s5_kernel,
        grid=(seq // rows,),
        in_specs=u_specs + [_const_spec(m.shape) for m in mats],
        out_specs=pl.BlockSpec((rows, S5_CH), lambda i: (i, 0)),
        out_shape=jax.ShapeDtypeStruct((seq, S5_CH), F32),
        scratch_shapes=[
            pltpu.VMEM((t, S5_CH, tc), F32),
            pltpu.VMEM((t, S5_CH, tc), F32),
            pltpu.VMEM((ncb, rows, LANES), F32),
            pltpu.VMEM((ng * n, tc), F32),
            pltpu.VMEM((ng * n, tc), F32),
            pltpu.VMEM((nsb, tc, LANES), F32),
            pltpu.VMEM((nsb, tc, LANES), F32),
            pltpu.VMEM((2, SUBLANES, ng * n), F32),
        ],
        compiler_params=pltpu.CompilerParams(dimension_semantics=("arbitrary",)),
        name="s5_scan",
    )(*([proj] * ncb), *mats)


def _mix_ffn_kernel(*refs, glu, final):
    (x_ref, a_ref, b_ref, wo_ref, g1_ref), refs = refs[:5], refs[5:]
    if glu:
        gw_ref, refs = refs[0], refs[1:]
    (g_ref, sc_ref, sh_ref, gate_ref, win_ref, cw_ref, cb_ref, wout_ref), refs = refs[:8], refs[8:]
    if final:
        fg_ref, o_ref, h_ref, act_ref, gbuf_ref, carry_ref = refs
    else:
        ng_ref, nsc_ref, nsh_ref, o_ref, hn_ref, h_ref, act_ref, gbuf_ref, carry_ref = refs
    tm = x_ref.shape[0]
    halo = gbuf_ref.shape[0] - tm

    @pl.when(pl.program_id(0) == 0)
    def _():
        carry_ref[...] = jnp.zeros(carry_ref.shape, F32)

    if glu:
        y = jax.nn.gelu(b_ref[...]).astype(BF16)
        gg = _dot(y, gw_ref[...])
        half = gg.shape[1] // 2
        b = (gg[:, :half] * jax.nn.sigmoid(gg[:, half:])).astype(BF16)
    else:
        b = b_ref[...]
    cat = jnp.concatenate([a_ref[...], b], axis=1)
    x = x_ref[...] + g1_ref[...] * _dot(cat, wo_ref[...])
    h_ref[...] = _mod_rmsnorm(x, g_ref[...], sc_ref[...], sh_ref[...]).astype(BF16)
    for f in range(D_FF // TF_FFN):
        cs = slice(f * TF_FFN, (f + 1) * TF_FFN)
        gs = slice(D_FF + f * TF_FFN, D_FF + (f + 1) * TF_FFN)
        h = h_ref[...]
        val = _dot(h, win_ref[:, cs])
        gate = _dot(h, win_ref[:, gs])
        gbuf_ref[0:halo, :] = carry_ref[:, cs]
        gbuf_ref[halo:halo + tm, :] = gate
        carry_ref[:, cs] = gate[tm - halo:tm, :]
        conv = (gate * cw_ref[2:3, cs] + gbuf_ref[halo - 1:halo - 1 + tm, :] * cw_ref[1:2, cs]
                + gbuf_ref[halo - 2:halo - 2 + tm, :] * cw_ref[0:1, cs] + cb_ref[:, cs])
        act_ref[:, cs] = (jax.nn.gelu(conv) * val).astype(BF16)
    xn = x + gate_ref[...] * _dot(act_ref[...], wout_ref[...])
    if final:
        xn = xn * lax.rsqrt(jnp.mean(xn * xn, axis=-1, keepdims=True) + EPS) * fg_ref[...]
    else:
        hn_ref[...] = _mod_rmsnorm(xn, ng_ref[...], nsc_ref[...], nsh_ref[...]).astype(BF16)
    o_ref[...] = xn


def _layer_spec(shape, layer):
    idx = (layer,) + (0,) * (len(shape) - 1)
    return pl.BlockSpec((None,) + tuple(shape[1:]), lambda *_: idx, pipeline_mode=pl.Buffered(1))


def _mix_ffn(x, a, b, wo, gate1, glu_w, g, scale, shift, gate2, w_in, conv_w, conv_b, w_out, tail, layer):
    seq, d = x.shape
    final = len(tail) == 1
    tm = TM_FFN
    halo = SUBLANES
    row = pl.BlockSpec((1, d), lambda i: (0, 0))
    rows = lambda w: pl.BlockSpec((tm, w), lambda i: (i, 0))
    conv_b = conv_b.reshape(conv_b.shape[0], 1, D_FF)
    in_specs = [rows(d), rows(a.shape[1]), rows(b.shape[1]), _const_spec(wo.shape), row]
    args = [x, a, b, wo, gate1]
    if glu_w is not None:
        in_specs.append(_const_spec(glu_w.shape))
        args.append(glu_w)
    in_specs += [
        row, row, row, row,
        _layer_spec(w_in.shape, layer),
        _layer_spec(conv_w.shape, layer),
        _layer_spec(conv_b.shape, layer),
        _layer_spec(w_out.shape, layer),
    ] + [row] * len(tail)
    args += [g.reshape(1, d), scale, shift, gate2, w_in, conv_w, conv_b, w_out]
    args += [t.reshape(1, d) for t in tail]
    out_specs = [rows(d)] if final else [rows(d), rows(d)]
    out_shape = [jax.ShapeDtypeStruct((seq, d), F32)] + ([] if final else [jax.ShapeDtypeStruct((seq, d), BF16)])
    return pl.pallas_call(
        functools.partial(_mix_ffn_kernel, glu=glu_w is not None, final=final),
        grid=(seq // tm,),
        in_specs=in_specs,
        out_specs=out_specs,
        out_shape=out_shape,
        scratch_shapes=[
            pltpu.VMEM((tm, d), BF16),
            pltpu.VMEM((tm, D_FF), BF16),
            pltpu.VMEM((tm + halo, TF_FFN), F32),
            pltpu.VMEM((halo, D_FF), F32),
        ],
        compiler_params=pltpu.CompilerParams(dimension_semantics=("arbitrary",)),
        name="mix_ffn",
    )(*args)


def kernel(x, c, t5_table, mod_w, mod_b, norm1_g, norm2_g, ffn_w_in, ffn_conv_w, ffn_conv_b, ffn_w_out,
           ev_w_in, ev_w_out, diff_lambda, diff_subln_g, band_rel_bias,
           od_w_in, od_w_out, s5_lam_re, s5_lam_im, s5_log_step, s5_b_re, s5_b_im, s5_c_re, s5_c_im,
           s5_d, s5_glu_w, final_g):
    assert x.shape[0] == 1 and x.shape[2] == D_MODEL
    seq = x.shape[1]
    assert seq % TM_PROJ == 0 and seq % (S5_T * S5_TC) == 0
    d = D_MODEL
    xs = x[0]
    mod = _modulation(c, mod_w, mod_b)
    ffn_w_in_b = ffn_w_in.astype(BF16)
    ffn_w_out_b = ffn_w_out.astype(BF16)
    mods = [[mod[i, :, k * d:(k + 1) * d] for k in range(6)] for i in range(DEPTH)]
    h = None
    for i in range(DEPTH):
        sh1, sc1, g1, sh2, sc2, g2 = mods[i]
        w_in = (ev_w_in if i % 2 == 0 else od_w_in)[i // 2].astype(BF16)
        proj_dtype = BF16 if i % 2 == 0 else F32
        if h is None:
            proj = _normproj(xs, norm1_g[i], sc1, sh1, w_in, proj_dtype)
        else:
            proj = _proj(h, w_in, proj_dtype)
        if i % 2 == 0:
            e = i // 2
            lam_init = 0.8 - 0.6 * math.exp(-0.3 * i)
            lp = diff_lambda[e].astype(F32)
            lam = jnp.exp(jnp.sum(lp[0] * lp[1])) - jnp.exp(jnp.sum(lp[2] * lp[3])) + lam_init
            mix_a = _diff_attention(proj, t5_table, lam, diff_subln_g[e], lam_init)
            mix_b = _band_attention(proj, band_rel_bias[e])
            wo, glu_w = ev_w_out[e].astype(BF16), None
        else:
            o = i // 2
            mix_a = _retention(proj)
            mats = _s5_matrices(s5_lam_re[o], s5_lam_im[o], s5_log_step[o], s5_b_re[o], s5_b_im[o],
                                s5_c_re[o], s5_c_im[o], s5_d[o])
            mix_b = _s5(proj, mats)
            wo, glu_w = od_w_out[o].astype(BF16), s5_glu_w[o].astype(BF16)
        if i == DEPTH - 1:
            tail = (final_g,)
        else:
            nsh1, nsc1 = mods[i + 1][0], mods[i + 1][1]
            tail = (norm1_g[i + 1], nsc1, nsh1)
        out = _mix_ffn(xs, mix_a, mix_b, wo, g1, glu_w, norm2_g[i], sc2, sh2, g2,
                       ffn_w_in_b, ffn_conv_w, ffn_conv_b, ffn_w_out_b, tail, layer=i)
        if i == DEPTH - 1:
            xs = out[0]
        else:
            xs, h = out
    return xs[None]
```

```python
import functools
import math

import jax
import jax.numpy as jnp
from jax import lax
from jax.experimental import pallas as pl
from jax.experimental.pallas import tpu as pltpu

F32 = jnp.float32
BF16 = jnp.bfloat16

D_MODEL = 1024
DEPTH = 2
CHUNK = 64
GROUP_WIDTH = D_MODEL // 2
DK_A = 64
DV_A = 2 * DK_A
N_HEADS_A = GROUP_WIDTH // DV_A
DH_B = 64
N_HEADS_B = GROUP_WIDTH // DH_B
LEFT_CHUNKS = 8
REL_CLIP = 2 * CHUNK
NUM_BUCKETS = 32
MAX_DISTANCE = 128
DV_C = 128
DQK_C = DV_C // 2
N_HEADS_C = GROUP_WIDTH // DV_C
ROPE_BASE = 10000.0
S5_CH = GROUP_WIDTH
S5_GROUP = 16
S5_GROUPS = S5_CH // S5_GROUP
S5_STATE = 64
D_FF = ((8 * D_MODEL // 3 + 255) // 256) * 256
CONV_W = 3
EVEN_IN = 3 * N_HEADS_A * DV_A + 3 * N_HEADS_B * DH_B
ODD_IN = 2 * N_HEADS_C * DQK_C + 2 * N_HEADS_C * DV_C + S5_CH
EPS = 1e-6
NEG_INF = -1e30
LOG2E = math.log2(math.e)

LANES = 128
SUBLANES = 8
MXU_DIM = 256

TM_PROJ = 1024
TN_PROJ = 1024
TN_MOD = 1536
TM_FFN = 512
TF_FFN = MXU_DIM
BLK_A = 512
NPART_A = 2
ONES_A = 16
SINGLE_PASS_LOG2_RANGE = 96.0
SCORE_PAD = LANES
BLK_B = 1024
BAND_B = LEFT_CHUNKS * CHUNK
QW_B = 4 * CHUNK
BLK_C = 512
S5_T = 16
S5_TC = LANES

assert BLK_B % BAND_B == 0 and BLK_B % QW_B == 0 and BAND_B % QW_B == 0
assert BLK_A >= MAX_DISTANCE, "far key blocks must sit in the saturated T5 bucket"
assert DV_A == LANES and 2 * DK_A == LANES and 2 * DH_B == LANES, "attention heads are read as 128-lane column blocks"


def _dot(a, b):
    return jnp.dot(a, b, preferred_element_type=F32)


def _dot_nt(a, b):
    return lax.dot_general(a, b, (((1,), (1,)), ((), ())), preferred_element_type=F32)


def _dot_tn(a, b):
    return lax.dot_general(a, b, (((0,), (0,)), ((), ())), preferred_element_type=F32)


def _const_spec(shape):
    zeros = (0,) * len(shape)
    return pl.BlockSpec(shape, lambda *_: zeros, pipeline_mode=pl.Buffered(1))


def _mod_rmsnorm(x, g, scale, shift):
    y = x * lax.rsqrt(jnp.mean(x * x, axis=-1, keepdims=True) + EPS)
    y = y * g
    return y * (1.0 + scale) + shift


def _mod_kernel(c_ref, w_ref, b_ref, o_ref):
    c = c_ref[...]
    cond = c * jax.nn.sigmoid(c)
    o_ref[0] = jnp.sum(cond * w_ref[0], axis=0, keepdims=True) + b_ref[0]


def _modulation(c, mod_w, mod_b):
    depth, d, n = mod_w.shape
    tn = TN_MOD
    return pl.pallas_call(
        _mod_kernel,
        grid=(depth, n // tn),
        in_specs=[
            pl.BlockSpec((d, 1), lambda i, j: (0, 0)),
            pl.BlockSpec((1, d, tn), lambda i, j: (i, 0, j)),
            pl.BlockSpec((1, 1, tn), lambda i, j: (i, 0, j)),
        ],
        out_specs=pl.BlockSpec((1, 1, tn), lambda i, j: (i, 0, j)),
        out_shape=jax.ShapeDtypeStruct((depth, 1, n), F32),
        name="modulation",
    )(c.reshape(d, 1), mod_w, mod_b.reshape(depth, 1, n))


def _normproj_kernel(x_ref, g_ref, sc_ref, sh_ref, w_ref, o_ref):
    tm, n = o_ref.shape
    half = tm // 2
    for r in range(2):
        rows = slice(r * half, (r + 1) * half)
        h = _mod_rmsnorm(x_ref[rows, :], g_ref[...], sc_ref[...], sh_ref[...]).astype(BF16)
        for j in range(n // TN_PROJ):
            cols = slice(j * TN_PROJ, (j + 1) * TN_PROJ)
            o_ref[rows, cols] = _dot(h, w_ref[:, cols]).astype(o_ref.dtype)


def _normproj(x, g, scale, shift, w, out_dtype):
    seq, d = x.shape
    n = w.shape[1]
    tm = TM_PROJ
    row = pl.BlockSpec((1, d), lambda i: (0, 0))
    return pl.pallas_call(
        _normproj_kernel,
        grid=(seq // tm,),
        in_specs=[pl.BlockSpec((tm, d), lambda i: (i, 0)), row, row, row, _const_spec(w.shape)],
        out_specs=pl.BlockSpec((tm, n), lambda i: (i, 0)),
        out_shape=jax.ShapeDtypeStruct((seq, n), out_dtype),
        compiler_params=pltpu.CompilerParams(dimension_semantics=("parallel",)),
        name="normproj",
    )(x, g.reshape(1, d), scale, shift, w)


def _proj_kernel(h_ref, w_ref, o_ref):
    for j in range(o_ref.shape[1] // TN_PROJ):
        cols = slice(j * TN_PROJ, (j + 1) * TN_PROJ)
        o_ref[:, cols] = _dot(h_ref[...], w_ref[:, cols]).astype(o_ref.dtype)


def _proj(h, w, out_dtype):
    seq, d = h.shape
    n = w.shape[1]
    tm = TM_PROJ
    return pl.pallas_call(
        _proj_kernel,
        grid=(seq // tm,),
        in_specs=[pl.BlockSpec((tm, d), lambda i: (i, 0)), _const_spec(w.shape)],
        out_specs=pl.BlockSpec((tm, n), lambda i: (i, 0)),
        out_shape=jax.ShapeDtypeStruct((seq, n), out_dtype),
        compiler_params=pltpu.CompilerParams(dimension_semantics=("parallel",)),
        name="proj",
    )(h, w)


def _diffattn_kernel(q_ref, k_ref, v_ref, bias_ref, bstat_ref, lam_ref, g_ref, o_ref,
                     qs_ref, vt_ref, kmax_ref, r_ref, m_ref, acc_ref, *s_refs, out_scale):
    blk = BLK_A
    nq = 2 * blk
    sub = SUBLANES
    dv = DV_A
    npart = NPART_A
    sa_ref, sb_ref = s_refs[:2 * npart], s_refs[2 * npart:4 * npart]
    pa_ref, pb_ref = s_refs[4 * npart:5 * npart], s_refs[5 * npart:6 * npart]
    i = pl.program_id(1)
    lane = lax.broadcasted_iota(jnp.int32, (blk, LANES), 1)
    same_subhead = (lax.broadcasted_iota(jnp.int32, (LANES, LANES), 0) // DK_A
                    == lax.broadcasted_iota(jnp.int32, (LANES, LANES), 1) // DK_A).astype(BF16)

    @pl.when(i == 0)
    def _():
        kmax_ref[...] = jnp.zeros(kmax_ref.shape, F32)

        def tr(b, carry):
            r0 = pl.multiple_of(b * blk, blk)
            vt_ref[0:dv, pl.ds(r0, blk)] = v_ref[pl.ds(r0, blk), :].astype(F32).T.astype(BF16)
            vt_ref[dv:dv + ONES_A, pl.ds(r0, blk)] = jnp.ones((ONES_A, blk), BF16)
            kf = k_ref[pl.ds(r0, blk), :].astype(F32)
            kn2 = _dot((kf * kf).astype(BF16), same_subhead)
            kmax_ref[...] = jnp.maximum(kmax_ref[...], jnp.max(kn2.reshape(blk // sub, sub, LANES), axis=0))
            return carry
        lax.fori_loop(0, v_ref.shape[0] // blk, tr, 0)
        kmax_ref[...] = jnp.broadcast_to(jnp.max(kmax_ref[...], axis=0, keepdims=True), kmax_ref.shape)

    q = (q_ref[...].astype(F32) * (DK_A ** -0.5 * LOG2E)).astype(BF16)
    qf = q.astype(F32)
    qs_ref[:, 0:blk] = jnp.where(lane < DK_A, qf, 0.0).T.astype(BF16)
    qs_ref[:, blk:nq] = jnp.where(lane >= DK_A, qf, 0.0).T.astype(BF16)
    acc_ref[...] = jnp.zeros(acc_ref.shape, F32)

    qsq = (qf * qf).astype(BF16)
    bound = []
    for m in range(2):
        lanes_m = (lax.broadcasted_iota(jnp.int32, (sub, LANES), 1) // DK_A == m).astype(BF16)
        qn2 = _dot_nt(lanes_m, qsq)
        bound.append(jnp.sqrt(qn2 * kmax_ref[:, m * DK_A:m * DK_A + 1]) * 1.03)
    bound = jnp.concatenate(bound, axis=1)
    bias_max, bias_span = bstat_ref[0, 0:1, 0:1], bstat_ref[0, 1:2, 0:1]
    r_ref[...] = bound + bias_max
    single_pass = jnp.max(2.0 * bound + bias_span) < SINGLE_PASS_LOG2_RANGE

    @pl.when(single_pass)
    def _():
        _diffattn_fixed_shift(i, k_ref, bias_ref, qs_ref, vt_ref, r_ref, acc_ref, pa_ref, pb_ref)

    @pl.when(jnp.logical_not(single_pass))
    def _():
        _diffattn_blocks(i, k_ref, bias_ref, qs_ref, vt_ref, m_ref, acc_ref, sa_ref, sb_ref, None)

    ot = acc_ref[0:dv, 0:nq] / acc_ref[dv:dv + 1, 0:nq]
    o = ot[:, 0:blk].T - lam_ref[...] * ot[:, blk:nq].T
    o = o * lax.rsqrt(jnp.mean(o * o, axis=-1, keepdims=True) + EPS) * g_ref[...]
    o_ref[...] = (o * out_scale).astype(o_ref.dtype)


def _diffattn_fixed_shift(i, k_ref, bias_ref, qs_ref, vt_ref, shift_ref, acc_ref, pa_ref, pb_ref):
    blk = BLK_A
    nq = 2 * blk
    sub = SUBLANES
    npart = len(pa_ref)
    wq = nq // npart

    def probs(b, p_ref, bias):
        k = k_ref[pl.ds(pl.multiple_of(b * blk, blk), blk), :]
        for part in range(npart):
            cols = slice(part * wq, (part + 1) * wq)
            s = _dot(k, qs_ref[:, cols])
            if bias is not None:
                q0 = (part * wq) % blk
                s = s + bias[:, q0:q0 + wq]
            p = jnp.exp2(s.reshape(blk // sub, sub, wq) - shift_ref[:, cols][None])
            p_ref[part][:, 0:wq] = p.reshape(blk, wq).astype(BF16)

    def accumulate(b, p_ref):
        vt = vt_ref[:, pl.ds(pl.multiple_of(b * blk, blk), blk)]
        for part in range(npart):
            cols = slice(part * wq, (part + 1) * wq)
            acc_ref[:, cols] += _dot(vt, p_ref[part][:, 0:wq])

    @pl.when(i == 0)
    def _():
        probs(0, pa_ref, bias_ref[0, 1])
        accumulate(0, pa_ref)

    @pl.when(i > 0)
    def _():
        nfar = i - 1
        probs(i, pa_ref, bias_ref[0, 1])
        probs(i - 1, pb_ref, bias_ref[0, 0])
        accumulate(i, pa_ref)

        def pair(t):
            probs(2 * t, pa_ref, None)
            accumulate(jnp.where(t == 0, i - 1, 2 * t - 1), pb_ref)
            probs(2 * t + 1, pb_ref, None)
            accumulate(2 * t, pa_ref)

        def two_pairs(u, carry):
            pair(2 * u)
            pair(2 * u + 1)
            return carry

        npairs = nfar // 2
        lax.fori_loop(0, npairs // 2, two_pairs, 0)

        @pl.when(lax.rem(npairs, 2) == 1)
        def _():
            pair(npairs - 1)
        in_pb = jnp.where(npairs == 0, i - 1, 2 * npairs - 1)

        @pl.when(lax.rem(nfar, 2) == 1)
        def _():
            probs(nfar - 1, pa_ref, None)
            accumulate(in_pb, pb_ref)
            accumulate(nfar - 1, pa_ref)

        @pl.when(lax.rem(nfar, 2) == 0)
        def _():
            accumulate(in_pb, pb_ref)


def _diffattn_blocks(i, k_ref, bias_ref, qs_ref, vt_ref, m_ref, acc_ref, sa_ref, sb_ref, shift_ref):
    blk = BLK_A
    nq = 2 * blk
    sub = SUBLANES
    npart = len(sa_ref) // 2
    wq = nq // npart
    online = shift_ref is None
    if online:
        m_ref[...] = jnp.full(m_ref.shape, NEG_INF, F32)

    def scores(b, s_ref):
        k = k_ref[pl.ds(pl.multiple_of(b * blk, blk), blk), :]
        for part in range(npart):
            s = _dot(k, qs_ref[:, part * wq:(part + 1) * wq])
            s_ref[part][:, 0:wq] = s
            if online:
                s_ref[npart + part][...] = jnp.max(s.reshape(blk // sub, sub, wq), axis=0)

    def softmax_pv(b, s_ref, bias):
        vt = vt_ref[:, pl.ds(pl.multiple_of(b * blk, blk), blk)]
        for part in range(npart):
            cols = slice(part * wq, (part + 1) * wq)
            s = s_ref[part][:, 0:wq]
            if bias is not None:
                q0 = (part * wq) % blk
                s = s + bias[:, q0:q0 + wq]
            s = s.reshape(blk // sub, sub, wq)
            if not online:
                p = jnp.exp2(s - shift_ref[:, cols][None])
                acc_ref[:, cols] += _dot(vt, p.reshape(blk, wq).astype(BF16))
                continue
            m_prev = m_ref[:, cols]
            smax = jnp.max(s, axis=0) if bias is not None else s_ref[npart + part][...]
            m_cur = jnp.max(smax, axis=0, keepdims=True)
            m_new = jnp.maximum(m_prev, m_cur)
            alpha = jnp.exp2(m_prev - m_new)
            p = jnp.exp2(s - m_new[None])
            pv = _dot(vt, p.reshape(blk, wq).astype(BF16))
            acc_ref[:, cols] = acc_ref[:, cols] * alpha[0:1] + pv
            m_ref[:, cols] = m_new

    nfar = jnp.maximum(i - 1, 0)
    odd = lax.rem(nfar, 2)

    @pl.when(i == 0)
    def _():
        scores(0, sb_ref)

    @pl.when(i > 0)
    def _():
        @pl.when(odd == 1)
        def _():
            scores(0, sb_ref)
            scores(1, sa_ref)
            softmax_pv(0, sb_ref, None)

        @pl.when(odd == 0)
        def _():
            scores(0, sa_ref)

        def pair(b):
            scores(b + 1, sb_ref)
            softmax_pv(b, sa_ref, None)
            scores(b + 2, sa_ref)
            softmax_pv(b + 1, sb_ref, None)

        def quad_body(t, carry):
            pair(odd + 4 * t)
            pair(odd + 4 * t + 2)
            return carry

        npairs = nfar // 2
        lax.fori_loop(0, npairs // 2, quad_body, 0)

        @pl.when(lax.rem(npairs, 2) == 1)
        def _():
            pair(odd + 2 * (npairs - 1))
        scores(i, sb_ref)
        softmax_pv(i - 1, sa_ref, bias_ref[0, 0])

    softmax_pv(i, sb_ref, bias_ref[0, 1])


_TOEPLITZ_ROWS = 256
_TOEPLITZ_N = 2048


def _toeplitz_kernel(v_ref, o_ref, *, keep):
    rows, cols = o_ref.shape[2:]
    x = jnp.broadcast_to(v_ref[0, 0], (rows, v_ref.shape[-1]))
    tile = pltpu.roll(x, 0, 1, stride=1, stride_axis=0)[:, :cols]
    r = lax.broadcasted_iota(jnp.int32, (rows, cols), 0) + pl.program_id(1) * rows
    c = lax.broadcasted_iota(jnp.int32, (rows, cols), 1)
    for variant in range(o_ref.shape[0]):
        o_ref[variant, 0] = jnp.where(keep(r, c, variant), tile, NEG_INF)


def _toeplitz_tiles(fn, keep, heads, rows, cols, variants=1):
    n, rb = _TOEPLITZ_N, _TOEPLITZ_ROWS
    assert rows % rb == 0 and rows <= n // 2 and cols <= n // 2
    idx = jnp.arange(n, dtype=jnp.int32)
    vec = fn(jnp.where(idx < n // 2, idx, idx - n)).astype(F32)
    vecs = jnp.stack([jnp.roll(vec, k * rb, axis=1) for k in range(rows // rb)], axis=1)
    return pl.pallas_call(
        functools.partial(_toeplitz_kernel, keep=keep),
        grid=(heads, rows // rb),
        in_specs=[pl.BlockSpec((1, 1, 1, n), lambda h, k: (h, k, 0, 0))],
        out_specs=pl.BlockSpec((variants, 1, rb, cols), lambda h, k: (0, h, k, 0)),
        out_shape=jax.ShapeDtypeStruct((variants, heads, rows, cols), F32),
        name="toeplitz_tiles",
    )(vecs.reshape(heads, rows // rb, 1, n))


def _t5_bucket(rel):
    nb = NUM_BUCKETS // 2
    max_exact = nb // 2
    bucket = jnp.where(rel > 0, nb, 0)
    n = jnp.abs(rel)
    nf = jnp.maximum(n, 1).astype(F32)
    large = max_exact + (jnp.log(nf / max_exact) / math.log(MAX_DISTANCE / max_exact)
                         * (nb - max_exact)).astype(jnp.int32)
    large = jnp.minimum(large, nb - 1)
    return bucket + jnp.where(n < max_exact, n, large)


def _diff_bias_tiles(t5_table):
    blk = BLK_A
    table = t5_table.astype(F32)
    far = table[_t5_bucket(jnp.full((), -(blk + 1), jnp.int32))]
    def visible(r, c, variant):
        return jnp.floor_divide(r - blk, CHUNK) <= jnp.floor_divide(c, CHUNK)

    tiles = _toeplitz_tiles(lambda x: ((table[_t5_bucket(-x - blk)] - far) * LOG2E).T, visible,
                            N_HEADS_A, 2 * blk, blk)
    return tiles.reshape(N_HEADS_A, 2, blk, blk)


def _diff_attention(proj, t5_table, lam, subln_g, lam_init):
    seq = proj.shape[0]
    blk = BLK_A
    bias = _diff_bias_tiles(t5_table)
    ha = N_HEADS_A
    finite = bias > 0.5 * NEG_INF
    bias_max = jnp.maximum(jnp.max(jnp.where(finite, bias, NEG_INF), axis=(1, 2, 3)), 0.0)
    bias_min = jnp.minimum(jnp.min(jnp.where(finite, bias, -NEG_INF), axis=(1, 2, 3)), 0.0)
    bstat = jnp.broadcast_to(jnp.stack([bias_max, bias_max - bias_min], axis=1)[:, :, None], (ha, 2, LANES))
    kern = functools.partial(_diffattn_kernel, out_scale=1.0 - lam_init)
    return pl.pallas_call(
        kern,
        grid=(ha, seq // blk),
        in_specs=[
            pl.BlockSpec((blk, DV_A), lambda h, i: (i, h)),
            pl.BlockSpec((seq, DV_A), lambda h, i: (0, ha + h)),
            pl.BlockSpec((seq, DV_A), lambda h, i: (0, 2 * ha + h)),
            pl.BlockSpec((1, 2, blk, blk), lambda h, i: (h, 0, 0, 0)),
            pl.BlockSpec((1, 2, LANES), lambda h, i: (h, 0, 0)),
            pl.BlockSpec((1, DV_A), lambda h, i: (0, 0)),
            pl.BlockSpec((1, DV_A), lambda h, i: (0, 0)),
        ],
        out_specs=pl.BlockSpec((blk, DV_A), lambda h, i: (i, h)),
        out_shape=jax.ShapeDtypeStruct((seq, ha * DV_A), BF16),
        scratch_shapes=[
            pltpu.VMEM((DV_A, 2 * blk), BF16),
            pltpu.VMEM((DV_A + ONES_A, seq), BF16),
            pltpu.VMEM((SUBLANES, LANES), F32),
            pltpu.VMEM((SUBLANES, 2 * blk), F32),
            pltpu.VMEM((SUBLANES, 2 * blk), F32),
            pltpu.VMEM((DV_A + ONES_A, 2 * blk), F32),
        ] + 2 * ([pltpu.VMEM((blk, 2 * blk // NPART_A + SCORE_PAD), F32)] * NPART_A
                 + [pltpu.VMEM((SUBLANES, 2 * blk // NPART_A), F32)] * NPART_A)
        + 2 * [pltpu.VMEM((blk, 2 * blk // NPART_A + SCORE_PAD), BF16)] * NPART_A,
        compiler_params=pltpu.CompilerParams(dimension_semantics=("parallel", "arbitrary")),
        name="diff_attention",
    )(proj, proj, proj, bias, bstat, jnp.full((1, DV_A), lam, F32), subln_g.reshape(1, DV_A).astype(F32))


def _band_kernel(q_ref, kp_ref, kc_ref, vp_ref, vc_ref, *refs):
    qw, band = QW_B, BAND_B
    nbias = band // qw + 1
    bias_refs, o_ref, s_refs = refs[:nbias], refs[nbias], refs[nbias + 1:]
    nk = band + qw
    sub = SUBLANES
    q = q_ref[...].astype(F32) * (DH_B ** -0.5 * LOG2E)
    lane = lax.broadcasted_iota(jnp.int32, q.shape, 1)
    qh = (jnp.where(lane < DH_B, q, 0.0).T.astype(BF16), jnp.where(lane >= DH_B, q, 0.0).T.astype(BF16))
    k_all = jnp.concatenate([kp_ref[...], kc_ref[...]], axis=0)
    vt_all = jnp.concatenate([vp_ref[...], vc_ref[...]], axis=0).astype(F32).T.astype(BF16)
    vt_all = jnp.concatenate([vt_all, jnp.ones((ONES_A, vt_all.shape[1]), BF16)], axis=0)
    ngroups = len(s_refs)
    for g in range(ngroups):
        k0 = g * qw
        qs = jnp.concatenate([qh[0][:, k0:k0 + qw], qh[1][:, k0:k0 + qw]], axis=1)
        s_refs[g][:, 0:2 * qw] = _dot(k_all[k0:k0 + nk], qs)
    for g in range(ngroups):
        k0 = g * qw
        bias_ref = bias_refs[min(g, nbias - 1)]
        bias = jnp.concatenate([bias_ref[0, 0], bias_ref[0, 1]], axis=1)
        s = (s_refs[g][:, 0:2 * qw] + bias).reshape(nk // sub, sub, 2 * qw)
        m = jnp.max(jnp.max(s, axis=0), axis=0, keepdims=True)
        p = jnp.exp2(s - m[None])
        pv = _dot(vt_all[:, k0:k0 + nk], p.reshape(nk, 2 * qw).astype(BF16))
        ot = pv[0:2 * DH_B] / pv[2 * DH_B:2 * DH_B + 1]
        o = jnp.concatenate([ot[0:DH_B, 0:qw], ot[DH_B:2 * DH_B, qw:2 * qw]], axis=0)
        o_ref[k0:k0 + qw, :] = o.T.astype(o_ref.dtype)


def _band_bias_tiles(rel_bias):
    band = BAND_B

    def valid(r, c, variant):
        qchunk = jnp.floor_divide(c, CHUNK)
        kchunk = jnp.floor_divide(r - band, CHUNK)
        missing = jnp.where(variant == 0, 0, band - (variant - 1) * QW_B)
        return (kchunk <= qchunk) & (kchunk >= qchunk - LEFT_CHUNKS) & (r >= missing)

    return _toeplitz_tiles(
        lambda x: rel_bias.astype(F32)[:, jnp.clip(-x - band, -REL_CLIP, REL_CLIP) + REL_CLIP] * LOG2E, valid,
        N_HEADS_B, band + QW_B, QW_B, variants=1 + band // QW_B)


def _band_attention(proj, rel_bias):
    seq = proj.shape[0]
    blk, band, qw = BLK_B, BAND_B, QW_B
    bias = _band_bias_tiles(rel_bias)
    npair = N_HEADS_B // 2
    qc0 = 3 * N_HEADS_A
    per = blk // band
    prev = lambda c0: (lambda hp, i: (jnp.maximum(i * per - 1, 0), c0 + hp))
    cur = lambda c0: (lambda hp, i: (i, c0 + hp))
    return pl.pallas_call(
        _band_kernel,
        grid=(npair, seq // blk),
        in_specs=[
            pl.BlockSpec((blk, LANES), cur(qc0)),
            pl.BlockSpec((band, LANES), prev(qc0 + npair)),
            pl.BlockSpec((blk, LANES), cur(qc0 + npair)),
            pl.BlockSpec((band, LANES), prev(qc0 + 2 * npair)),
            pl.BlockSpec((blk, LANES), cur(qc0 + 2 * npair)),
        ] + [
            pl.BlockSpec((1, 2, band + qw, qw), (lambda hp, i, t=t: (jnp.where(i == 0, 1 + t, 0), hp, 0, 0)))
            for t in range(band // qw)
        ] + [
            pl.BlockSpec((1, 2, band + qw, qw), lambda hp, i: (0, hp, 0, 0)),
        ],
        out_specs=pl.BlockSpec((blk, LANES), lambda hp, i: (i, hp)),
        out_shape=jax.ShapeDtypeStruct((seq, N_HEADS_B * DH_B), BF16),
        scratch_shapes=[pltpu.VMEM((band + qw, 2 * qw + SCORE_PAD), F32)] * (blk // qw),
        compiler_params=pltpu.CompilerParams(dimension_semantics=("parallel", "arbitrary")),
        name="band_attention",
    )(proj, proj, proj, proj, proj, *([bias] * bias.shape[0]))


def _retention_kernel(qk_ref, v_ref, gate_ref, cos_ref, sin_ref, qdec_ref, kdec_ref, dmat_ref,
                      sdec_ref, o_ref, state_ref):
    @pl.when(pl.program_id(0) == 0)
    def _():
        state_ref[...] = jnp.zeros(state_ref.shape, F32)

    cos = cos_ref[...]
    sin = sin_ref[...]
    lane = lax.broadcasted_iota(jnp.int32, cos.shape, 1)
    first_half = (lane % DQK_C) < (DQK_C // 2)
    qk = qk_ref[...]
    parts = []
    for j in range(qk.shape[1] // LANES):
        t = qk[:, j * LANES:(j + 1) * LANES]
        partner = jnp.where(first_half, pltpu.roll(t, LANES - DQK_C // 2, 1), pltpu.roll(t, DQK_C // 2, 1))
        parts.append(t * cos + partner * sin)
    wq = N_HEADS_C * DQK_C
    q = jnp.concatenate(parts[:wq // LANES], axis=1)
    k = jnp.concatenate(parts[wq // LANES:], axis=1) * (DQK_C ** -0.5)
    qd = (q * qdec_ref[...]).astype(BF16)
    kd = (k * kdec_ref[...]).astype(BF16)
    qb = q.astype(BF16)
    kb = k.astype(BF16)
    vb = v_ref[...].astype(BF16)
    gate = gate_ref[...]
    outs = []
    for h in range(N_HEADS_C):
        qs = slice(h * DQK_C, (h + 1) * DQK_C)
        vs = slice(h * DV_C, (h + 1) * DV_C)
        scores = _dot_nt(qb[:, qs], kb[:, qs]) * dmat_ref[h]
        state = state_ref[h]
        r = _dot(scores.astype(BF16), vb[:, vs]) + _dot(qd[:, qs], state.astype(BF16))
        state_ref[h] = state * sdec_ref[h] + _dot_tn(kd[:, qs], vb[:, vs])
        r = r * lax.rsqrt(jnp.mean(r * r, axis=-1, keepdims=True) + EPS)
        g = gate[:, vs]
        outs.append(r * (g * jax.nn.sigmoid(g)))
    o_ref[...] = jnp.concatenate(outs, axis=1).astype(o_ref.dtype)


def _retention_tables(seq):
    t = BLK_C
    half = DQK_C // 2
    inv_freq = 1.0 / (ROPE_BASE ** (jnp.arange(0, DQK_C, 2, dtype=F32) / DQK_C))
    ang = jnp.arange(seq, dtype=F32)[:, None] * inv_freq[None, :]
    reps = LANES // half
    cos = jnp.tile(jnp.cos(ang), (1, reps))
    sign = jnp.where((jnp.arange(LANES) % DQK_C) < half, -1.0, 1.0).astype(F32)
    sin = jnp.tile(jnp.sin(ang), (1, reps)) * sign[None, :]
    log_g = jnp.log(1.0 - jnp.power(2.0, -5.0 - jnp.arange(N_HEADS_C, dtype=F32)))
    pos = jnp.arange(t, dtype=F32)
    diff = pos[:, None] - pos[None, :]
    same_or_past = (jnp.arange(t)[None, :] // CHUNK) <= (jnp.arange(t)[:, None] // CHUNK)
    dmat = jnp.where(same_or_past[None], jnp.exp(log_g[:, None, None] * jnp.abs(diff)[None]), 0.0)
    qdec = jnp.repeat(jnp.exp(log_g[None, :] * (pos[:, None] + 1.0)), DQK_C, axis=1)
    kdec = jnp.repeat(jnp.exp(log_g[None, :] * (t - 1.0 - pos[:, None])), DQK_C, axis=1)
    sdec = jnp.broadcast_to(jnp.exp(log_g * t)[:, None, None], (N_HEADS_C, 1, DV_C))
    return cos, sin, qdec, kdec, dmat, sdec


def _retention(proj):
    seq = proj.shape[0]
    t = BLK_C
    cos, sin, qdec, kdec, dmat, sdec = _retention_tables(seq)
    wv = N_HEADS_C * DV_C
    return pl.pallas_call(
        _retention_kernel,
        grid=(seq // t,),
        in_specs=[
            pl.BlockSpec((t, wv), lambda i: (i, 0)),
            pl.BlockSpec((t, wv), lambda i: (i, 1)),
            pl.BlockSpec((t, wv), lambda i: (i, 2)),
            pl.BlockSpec((t, LANES), lambda i: (i, 0)),
            pl.BlockSpec((t, LANES), lambda i: (i, 0)),
            pl.BlockSpec((t, N_HEADS_C * DQK_C), lambda i: (0, 0)),
            pl.BlockSpec((t, N_HEADS_C * DQK_C), lambda i: (0, 0)),
            pl.BlockSpec((N_HEADS_C, t, t), lambda i: (0, 0, 0)),
            pl.BlockSpec((N_HEADS_C, 1, DV_C), lambda i: (0, 0, 0)),
        ],
        out_specs=pl.BlockSpec((t, wv), lambda i: (i, 0)),
        out_shape=jax.ShapeDtypeStruct((seq, wv), BF16),
        scratch_shapes=[pltpu.VMEM((N_HEADS_C, DQK_C, DV_C), F32)],
        compiler_params=pltpu.CompilerParams(dimension_semantics=("arbitrary",)),
        name="retention",
    )(proj, proj, proj, cos, sin, qdec, kdec, dmat, sdec)


def _s5_kernel(*refs):
    ncb = S5_CH // LANES
    u_refs = refs[:ncb]
    (mt_ref, bt_ref, ctr_ref, cti_ref, are_ref, aim_ref, y_ref,
     ut_ref, yt_ref, ys_ref, vr_ref, vi_ref, spr_ref, spi_ref, carry_ref) = refs[ncb:]
    tc = S5_TC
    gp = S5_GROUP
    n = S5_STATE
    ng = S5_GROUPS

    @pl.when(pl.program_id(0) == 0)
    def _():
        carry_ref[...] = jnp.zeros(carry_ref.shape, F32)

    for s in range(S5_T):
        for k in range(ncb):
            ut_ref[s, k * LANES:(k + 1) * LANES, :] = u_refs[k][pl.ds(s, tc, stride=S5_T), :].T

    unroll = 4

    def intra(it, carry):
        for k in range(unroll):
            g = it * unroll + k
            r0 = pl.multiple_of(g * gp, gp)
            ug = ut_ref[:, pl.ds(r0, gp), :].reshape(S5_T * gp, tc).astype(BF16)
            yt_ref[:, pl.ds(r0, gp), :] = _dot(mt_ref[g], ug).reshape(S5_T, gp, tc)
            vt = _dot(bt_ref[g], ug)
            n0 = pl.multiple_of(g * n, n)
            vr_ref[pl.ds(n0, n), :] = vt[0:n]
            vi_ref[pl.ds(n0, n), :] = vt[n:2 * n]
        return carry

    lax.fori_loop(0, ng // unroll, intra, 0)

    sub = SUBLANES
    nv = tc // sub
    row = lax.broadcasted_iota(jnp.int32, (tc, LANES), 0)
    in_vreg = lax.rem(row, sub)

    def rows_of(v, r):
        return jnp.broadcast_to(v[r:r + 1], (tc, LANES))

    for j in range(ng * n // LANES):
        cols = slice(j * LANES, (j + 1) * LANES)
        pwr, pwi = are_ref[:, cols], aim_ref[:, cols]
        xr = vr_ref[cols, :].T
        xi = vi_ref[cols, :].T
        for d in (1, 2, 4):
            keep = in_vreg >= d
            sr = jnp.where(keep, pltpu.roll(xr, d, 0), 0.0)
            si = jnp.where(keep, pltpu.roll(xi, d, 0), 0.0)
            fr, fi = rows_of(pwr, d - 1), rows_of(pwi, d - 1)
            xr, xi = xr + (fr * sr - fi * si), xi + (fr * si + fi * sr)
        cr, ci = carry_ref[0, :, cols], carry_ref[1, :, cols]
        cr0, ci0 = cr, ci
        outr, outi = [], []
        for v in range(nv):
            yr = xr[v * sub:(v + 1) * sub] + (pwr * cr - pwi * ci)
            yi = xi[v * sub:(v + 1) * sub] + (pwr * ci + pwi * cr)
            outr.append(yr)
            outi.append(yi)
            cr = jnp.broadcast_to(yr[sub - 1:sub], (sub, LANES))
            ci = jnp.broadcast_to(yi[sub - 1:sub], (sub, LANES))
        carry_ref[0, :, cols] = cr
        carry_ref[1, :, cols] = ci
        sr = jnp.concatenate(outr, axis=0)
        si = jnp.concatenate(outi, axis=0)
        first = row == 0
        spr_ref[j] = jnp.where(first, rows_of(cr0, 0), pltpu.roll(sr, 1, 0))
        spi_ref[j] = jnp.where(first, rows_of(ci0, 0), pltpu.roll(si, 1, 0))

    def cross(it, carry):
        for k in range(unroll):
            jp = it * unroll + k
            r0 = pl.multiple_of(jp * 2 * gp, 2 * gp)
            yc = (_dot_nt(ctr_ref[jp], spr_ref[jp].astype(BF16))
                  + _dot_nt(cti_ref[jp], spi_ref[jp].astype(BF16)))
            yt_ref[:, pl.ds(r0, 2 * gp), :] += yc.reshape(S5_T, 2 * gp, tc)
        return carry

    lax.fori_loop(0, ng // 2 // unroll, cross, 0)

    for s in range(S5_T):
        for k in range(ncb):
            ys_ref[k, pl.ds(s, tc, stride=S5_T), :] = yt_ref[s, k * LANES:(k + 1) * LANES, :].T
    for k in range(ncb):
        y_ref[:, k * LANES:(k + 1) * LANES] = ys_ref[k]


def _s5_matrices(lam_re, lam_im, log_step, b_re, b_im, c_re, c_im, d_skip):
    hi = lax.Precision.HIGHEST
    t, gp, n, ng = S5_T, S5_GROUP, S5_STATE, S5_GROUPS
    lam = lax.complex(lam_re.astype(F32), lam_im.astype(F32))
    step = jnp.exp(log_step.astype(F32))[:, None]
    ls = lam * step
    a_bar = jnp.exp(ls)
    b_bar = ((a_bar - 1.0) / lam)[..., None] * lax.complex(b_re.astype(F32), b_im.astype(F32))
    cm = lax.complex(c_re.astype(F32), c_im.astype(F32))

    def apow(k):
        kk = k.astype(F32).astype(jnp.complex64)
        return jnp.exp(ls.reshape((ng,) + (1,) * k.ndim + (n,)) * kk[None, ..., None])

    tt = jnp.arange(t)
    kmat = jnp.einsum('gpn,gln,gnq->glpq', cm, apow(tt), b_bar, precision=hi).real
    krev = jnp.transpose(kmat[:, ::-1], (0, 2, 1, 3)).reshape(ng, gp, t * gp)
    kpad = jnp.pad(krev, ((0, 0), (0, 0), (0, t * gp)))
    mt = jnp.concatenate([kpad[:, :, (t - 1 - to) * gp:(2 * t - 1 - to) * gp] for to in range(t)], axis=1)
    dvec = jnp.tile(d_skip.astype(F32).reshape(ng, 1, gp), (1, t, 1)).reshape(ng, t * gp)
    mt = mt + jnp.eye(t * gp, dtype=F32)[None] * dvec[:, :, None]
    z = jnp.swapaxes(apow(t - 1 - tt), 1, 2)[:, :, :, None] * b_bar[:, :, None, :]
    z = z.reshape(ng, n, t * gp)
    bt = jnp.concatenate([z.real, z.imag], axis=1)
    w = cm[:, None, :, :] * apow(tt + 1)[:, :, None, :]

    def pair_readout(x):
        x = x.reshape(ng // 2, 2, t, gp, n)
        first = jnp.pad(x[:, 0], ((0, 0), (0, 0), (0, 0), (0, n)))
        second = jnp.pad(x[:, 1], ((0, 0), (0, 0), (0, 0), (n, 0)))
        return jnp.stack([first, second], axis=2).reshape(ng // 2, t * 2 * gp, 2 * n).astype(BF16)

    ctr, cti = pair_readout(w.real), pair_readout(-w.imag)
    a_chunk = jnp.transpose(apow(t * (jnp.arange(SUBLANES) + 1)), (1, 0, 2)).reshape(SUBLANES, ng * n)
    return mt.astype(BF16), bt.astype(BF16), ctr, cti, a_chunk.real, a_chunk.imag


def _s5(proj, mats):
    seq, width = proj.shape
    t, tc, gp, n, ng = S5_T, S5_TC, S5_GROUP, S5_STATE, S5_GROUPS
    rows = t * tc
    ncb = S5_CH // LANES
    cb0 = (width - S5_CH) // LANES
    u_specs = [pl.BlockSpec((rows, LANES), (lambda i, k=k: (i, cb0 + k))) for k in range(ncb)]
    nsb = ng * n // LANES
    return pl.pallas_call(
        _s5_kernel,
        grid=(seq // rows,),
        in_specs=u_specs + [_const_spec(m.shape) for m in mats],
        out_specs=pl.BlockSpec((rows, S5_CH), lambda i: (i, 0)),
        out_shape=jax.ShapeDtypeStruct((seq, S5_CH), F32),
        scratch_shapes=[
            pltpu.VMEM((t, S5_CH, tc), F32),
            pltpu.VMEM((t, S5_CH, tc), F32),
            pltpu.VMEM((ncb, rows, LANES), F32),
            pltpu.VMEM((ng * n, tc), F32),
            pltpu.VMEM((ng * n, tc), F32),
            pltpu.VMEM((nsb, tc, LANES), F32),
            pltpu.VMEM((nsb, tc, LANES), F32),
            pltpu.VMEM((2, SUBLANES, ng * n), F32),
        ],
        compiler_params=pltpu.CompilerParams(dimension_semantics=("arbitrary",)),
        name="s5_scan",
    )(*([proj] * ncb), *mats)


def _mix_ffn_kernel(*refs, glu, final):
    (x_ref, a_ref, b_ref, wo_ref, g1_ref), refs = refs[:5], refs[5:]
    if glu:
        gw_ref, refs = refs[0], refs[1:]
    (g_ref, sc_ref, sh_ref, gate_ref, win_ref, cw_ref, cb_ref, wout_ref), refs = refs[:8], refs[8:]
    if final:
        fg_ref, o_ref, h_ref, act_ref, gbuf_ref, carry_ref = refs
    else:
        ng_ref, nsc_ref, nsh_ref, o_ref, hn_ref, h_ref, act_ref, gbuf_ref, carry_ref = refs
    tm = x_ref.shape[0]
    halo = gbuf_ref.shape[0] - tm

    @pl.when(pl.program_id(0) == 0)
    def _():
        carry_ref[...] = jnp.zeros(carry_ref.shape, F32)

    if glu:
        y = jax.nn.gelu(b_ref[...]).astype(BF16)
        gg = _dot(y, gw_ref[...])
        half = gg.shape[1] // 2
        b = (gg[:, :half] * jax.nn.sigmoid(gg[:, half:])).astype(BF16)
    else:
        b = b_ref[...]
    cat = jnp.concatenate([a_ref[...], b], axis=1)
    x = x_ref[...] + g1_ref[...] * _dot(cat, wo_ref[...])
    h_ref[...] = _mod_rmsnorm(x, g_ref[...], sc_ref[...], sh_ref[...]).astype(BF16)
    for f in range(D_FF // TF_FFN):
        cs = slice(f * TF_FFN, (f + 1) * TF_FFN)
        gs = slice(D_FF + f * TF_FFN, D_FF + (f + 1) * TF_FFN)
        h = h_ref[...]
        val = _dot(h, win_ref[:, cs])
        gate = _dot(h, win_ref[:, gs])
        gbuf_ref[0:halo, :] = carry_ref[:, cs]
        gbuf_ref[halo:halo + tm, :] = gate
        carry_ref[:, cs] = gate[tm - halo:tm, :]
        conv = (gate * cw_ref[2:3, cs] + gbuf_ref[halo - 1:halo - 1 + tm, :] * cw_ref[1:2, cs]
                + gbuf_ref[halo - 2:halo - 2 + tm, :] * cw_ref[0:1, cs] + cb_ref[:, cs])
        act_ref[:, cs] = (jax.nn.gelu(conv) * val).astype(BF16)
    xn = x + gate_ref[...] * _dot(act_ref[...], wout_ref[...])
    if final:
        xn = xn * lax.rsqrt(jnp.mean(xn * xn, axis=-1, keepdims=True) + EPS) * fg_ref[...]
    else:
        hn_ref[...] = _mod_rmsnorm(xn, ng_ref[...], nsc_ref[...], nsh_ref[...]).astype(BF16)
    o_ref[...] = xn


def _layer_spec(shape, layer):
    idx = (layer,) + (0,) * (len(shape) - 1)
    return pl.BlockSpec((None,) + tuple(shape[1:]), lambda *_: idx, pipeline_mode=pl.Buffered(1))


def _mix_ffn(x, a, b, wo, gate1, glu_w, g, scale, shift, gate2, w_in, conv_w, conv_b, w_out, tail, layer):
    seq, d = x.shape
    final = len(tail) == 1
    tm = TM_FFN
    halo = SUBLANES
    row = pl.BlockSpec((1, d), lambda i: (0, 0))
    rows = lambda w: pl.BlockSpec((tm, w), lambda i: (i, 0))
    conv_b = conv_b.reshape(conv_b.shape[0], 1, D_FF)
    in_specs = [rows(d), rows(a.shape[1]), rows(b.shape[1]), _const_spec(wo.shape), row]
    args = [x, a, b, wo, gate1]
    if glu_w is not None:
        in_specs.append(_const_spec(glu_w.shape))
        args.append(glu_w)
    in_specs += [
        row, row, row, row,
        _layer_spec(w_in.shape, layer),
        _layer_spec(conv_w.shape, layer),
        _layer_spec(conv_b.shape, layer),
        _layer_spec(w_out.shape, layer),
    ] + [row] * len(tail)
    args += [g.reshape(1, d), scale, shift, gate2, w_in, conv_w, conv_b, w_out]
    args += [t.reshape(1, d) for t in tail]
    out_specs = [rows(d)] if final else [rows(d), rows(d)]
    out_shape = [jax.ShapeDtypeStruct((seq, d), F32)] + ([] if final else [jax.ShapeDtypeStruct((seq, d), BF16)])
    return pl.pallas_call(
        functools.partial(_mix_ffn_kernel, glu=glu_w is not None, final=final),
        grid=(seq // tm,),
        in_specs=in_specs,
        out_specs=out_specs,
        out_shape=out_shape,
        scratch_shapes=[
            pltpu.VMEM((tm, d), BF16),
            pltpu.VMEM((tm, D_FF), BF16),
            pltpu.VMEM((tm + halo, TF_FFN), F32),
            pltpu.VMEM((halo, D_FF), F32),
        ],
        compiler_params=pltpu.CompilerParams(dimension_semantics=("arbitrary",)),
        name="mix_ffn",
    )(*args)


def kernel(x, c, t5_table, mod_w, mod_b, norm1_g, norm2_g, ffn_w_in, ffn_conv_w, ffn_conv_b, ffn_w_out,
           ev_w_in, ev_w_out, diff_lambda, diff_subln_g, band_rel_bias,
           od_w_in, od_w_out, s5_lam_re, s5_lam_im, s5_log_step, s5_b_re, s5_b_im, s5_c_re, s5_c_im,
           s5_d, s5_glu_w, final_g):
    assert x.shape[0] == 1 and x.shape[2] == D_MODEL
    seq = x.shape[1]
    assert seq % TM_PROJ == 0 and seq % (S5_T * S5_TC) == 0
    d = D_MODEL
    xs = x[0]
    mod = _modulation(c, mod_w, mod_b)
    ffn_w_in_b = ffn_w_in.astype(BF16)
    ffn_w_out_b = ffn_w_out.astype(BF16)
    mods = [[mod[i, :, k * d:(k + 1) * d] for k in range(6)] for i in range(DEPTH)]
    h = None
    for i in range(DEPTH):
        sh1, sc1, g1, sh2, sc2, g2 = mods[i]
        w_in = (ev_w_in if i % 2 == 0 else od_w_in)[i // 2].astype(BF16)
        proj_dtype = BF16 if i % 2 == 0 else F32
        if h is None:
            proj = _normproj(xs, norm1_g[i], sc1, sh1, w_in, proj_dtype)
        else:
            proj = _proj(h, w_in, proj_dtype)
        if i % 2 == 0:
            e = i // 2
            lam_init = 0.8 - 0.6 * math.exp(-0.3 * i)
            lp = diff_lambda[e].astype(F32)
            lam = jnp.exp(jnp.sum(lp[0] * lp[1])) - jnp.exp(jnp.sum(lp[2] * lp[3])) + lam_init
            mix_a = _diff_attention(proj, t5_table, lam, diff_subln_g[e], lam_init)
            mix_b = _band_attention(proj, band_rel_bias[e])
            wo, glu_w = ev_w_out[e].astype(BF16), None
        else:
            o = i // 2
            mix_a = _retention(proj)
            mats = _s5_matrices(s5_lam_re[o], s5_lam_im[o], s5_log_step[o], s5_b_re[o], s5_b_im[o],
                                s5_c_re[o], s5_c_im[o], s5_d[o])
            mix_b = _s5(proj, mats)
            wo, glu_w = od_w_out[o].astype(BF16), s5_glu_w[o].astype(BF16)
        if i == DEPTH - 1:
            tail = (final_g,)
        else:
            nsh1, nsc1 = mods[i + 1][0], mods[i + 1][1]
            tail = (norm1_g[i + 1], nsc1, nsh1)
        out = _mix_ffn(xs, mix_a, mix_b, wo, g1, glu_w, norm2_g[i], sc2, sh2, g2,
                       ffn_w_in_b, ffn_conv_w, ffn_conv_b, ffn_w_out_b, tail, layer=i)
        if i == DEPTH - 1:
            xs = out[0]
        else:
            xs, h = out
    return xs[None]
```

```python
import functools
import math

import jax
import jax.numpy as jnp
from jax import lax
from jax.experimental import pallas as pl
from jax.experimental.pallas import tpu as pltpu

F32 = jnp.float32
BF16 = jnp.bfloat16

D_MODEL = 1024
DEPTH = 2
CHUNK = 64
GROUP_WIDTH = D_MODEL // 2
DK_A = 64
DV_A = 2 * DK_A
N_HEADS_A = GROUP_WIDTH // DV_A
DH_B = 64
N_HEADS_B = GROUP_WIDTH // DH_B
LEFT_CHUNKS = 8
REL_CLIP = 2 * CHUNK
NUM_BUCKETS = 32
MAX_DISTANCE = 128
DV_C = 128
DQK_C = DV_C // 2
N_HEADS_C = GROUP_WIDTH // DV_C
ROPE_BASE = 10000.0
S5_CH = GROUP_WIDTH
S5_GROUP = 16
S5_GROUPS = S5_CH // S5_GROUP
S5_STATE = 64
D_FF = ((8 * D_MODEL // 3 + 255) // 256) * 256
CONV_W = 3
EVEN_IN = 3 * N_HEADS_A * DV_A + 3 * N_HEADS_B * DH_B
ODD_IN = 2 * N_HEADS_C * DQK_C + 2 * N_HEADS_C * DV_C + S5_CH
EPS = 1e-6
NEG_INF = -1e30
LOG2E = math.log2(math.e)

LANES = 128
SUBLANES = 8
MXU_DIM = 256

TM_PROJ = 1024
TN_PROJ = 1024
TN_MOD = 1536
TM_FFN = 512
TF_FFN = MXU_DIM
BLK_A = 512
NPART_A = 2
ONES_A = 16
SINGLE_PASS_LOG2_RANGE = 96.0
SCORE_PAD = LANES
BLK_B = 1024
BAND_B = LEFT_CHUNKS * CHUNK
QW_B = 4 * CHUNK
BLK_C = 512
S5_T = 16
S5_TC = LANES

assert BLK_B % BAND_B == 0 and BLK_B % QW_B == 0 and BAND_B % QW_B == 0
assert BLK_A >= MAX_DISTANCE, "far key blocks must sit in the saturated T5 bucket"
assert DV_A == LANES and 2 * DK_A == LANES and 2 * DH_B == LANES, "attention heads are read as 128-lane column blocks"


def _dot(a, b):
    return jnp.dot(a, b, preferred_element_type=F32)


def _dot_nt(a, b):
    return lax.dot_general(a, b, (((1,), (1,)), ((), ())), preferred_element_type=F32)


def _dot_tn(a, b):
    return lax.dot_general(a, b, (((0,), (0,)), ((), ())), preferred_element_type=F32)


def _const_spec(shape):
    zeros = (0,) * len(shape)
    return pl.BlockSpec(shape, lambda *_: zeros, pipeline_mode=pl.Buffered(1))


def _mod_rmsnorm(x, g, scale, shift):
    y = x * lax.rsqrt(jnp.mean(x * x, axis=-1, keepdims=True) + EPS)
    y = y * g
    return y * (1.0 + scale) + shift


def _mod_kernel(c_ref, w_ref, b_ref, o_ref):
    c = c_ref[...]
    cond = c * jax.nn.sigmoid(c)
    o_ref[0] = jnp.sum(cond * w_ref[0], axis=0, keepdims=True) + b_ref[0]


def _modulation(c, mod_w, mod_b):
    depth, d, n = mod_w.shape
    tn = TN_MOD
    return pl.pallas_call(
        _mod_kernel,
        grid=(depth, n // tn),
        in_specs=[
            pl.BlockSpec((d, 1), lambda i, j: (0, 0)),
            pl.BlockSpec((1, d, tn), lambda i, j: (i, 0, j)),
            pl.BlockSpec((1, 1, tn), lambda i, j: (i, 0, j)),
        ],
        out_specs=pl.BlockSpec((1, 1, tn), lambda i, j: (i, 0, j)),
        out_shape=jax.ShapeDtypeStruct((depth, 1, n), F32),
        name="modulation",
    )(c.reshape(d, 1), mod_w, mod_b.reshape(depth, 1, n))


def _normproj_kernel(x_ref, g_ref, sc_ref, sh_ref, w_ref, o_ref):
    tm, n = o_ref.shape
    half = tm // 2
    for r in range(2):
        rows = slice(r * half, (r + 1) * half)
        h = _mod_rmsnorm(x_ref[rows, :], g_ref[...], sc_ref[...], sh_ref[...]).astype(BF16)
        for j in range(n // TN_PROJ):
            cols = slice(j * TN_PROJ, (j + 1) * TN_PROJ)
            o_ref[rows, cols] = _dot(h, w_ref[:, cols]).astype(o_ref.dtype)


def _normproj(x, g, scale, shift, w, out_dtype):
    seq, d = x.shape
    n = w.shape[1]
    tm = TM_PROJ
    row = pl.BlockSpec((1, d), lambda i: (0, 0))
    return pl.pallas_call(
        _normproj_kernel,
        grid=(seq // tm,),
        in_specs=[pl.BlockSpec((tm, d), lambda i: (i, 0)), row, row, row, _const_spec(w.shape)],
        out_specs=pl.BlockSpec((tm, n), lambda i: (i, 0)),
        out_shape=jax.ShapeDtypeStruct((seq, n), out_dtype),
        compiler_params=pltpu.CompilerParams(dimension_semantics=("parallel",)),
        name="normproj",
    )(x, g.reshape(1, d), scale, shift, w)


def _proj_kernel(h_ref, w_ref, o_ref):
    for j in range(o_ref.shape[1] // TN_PROJ):
        cols = slice(j * TN_PROJ, (j + 1) * TN_PROJ)
        o_ref[:, cols] = _dot(h_ref[...], w_ref[:, cols]).astype(o_ref.dtype)


def _proj(h, w, out_dtype):
    seq, d = h.shape
    n = w.shape[1]
    tm = TM_PROJ
    return pl.pallas_call(
        _proj_kernel,
        grid=(seq // tm,),
        in_specs=[pl.BlockSpec((tm, d), lambda i: (i, 0)), _const_spec(w.shape)],
        out_specs=pl.BlockSpec((tm, n), lambda i: (i, 0)),
        out_shape=jax.ShapeDtypeStruct((seq, n), out_dtype),
        compiler_params=pltpu.CompilerParams(dimension_semantics=("parallel",)),
        name="proj",
    )(h, w)


def _diffattn_kernel(q_ref, k_ref, v_ref, bias_ref, bstat_ref, lam_ref, g_ref, o_ref,
                     qs_ref, vt_ref, kmax_ref, r_ref, m_ref, acc_ref, *s_refs, out_scale):
    blk = BLK_A
    nq = 2 * blk
    sub = SUBLANES
    dv = DV_A
    npart = NPART_A
    sa_ref, sb_ref = s_refs[:2 * npart], s_refs[2 * npart:4 * npart]
    pa_ref, pb_ref = s_refs[4 * npart:5 * npart], s_refs[5 * npart:6 * npart]
    i = pl.program_id(1)
    lane = lax.broadcasted_iota(jnp.int32, (blk, LANES), 1)
    same_subhead = (lax.broadcasted_iota(jnp.int32, (LANES, LANES), 0) // DK_A
                    == lax.broadcasted_iota(jnp.int32, (LANES, LANES), 1) // DK_A).astype(BF16)

    @pl.when(i == 0)
    def _():
        kmax_ref[...] = jnp.zeros(kmax_ref.shape, F32)

        def tr(b, carry):
            r0 = pl.multiple_of(b * blk, blk)
            vt_ref[0:dv, pl.ds(r0, blk)] = v_ref[pl.ds(r0, blk), :].astype(F32).T.astype(BF16)
            vt_ref[dv:dv + ONES_A, pl.ds(r0, blk)] = jnp.ones((ONES_A, blk), BF16)
            kf = k_ref[pl.ds(r0, blk), :].astype(F32)
            kn2 = _dot((kf * kf).astype(BF16), same_subhead)
            kmax_ref[...] = jnp.maximum(kmax_ref[...], jnp.max(kn2.reshape(blk // sub, sub, LANES), axis=0))
            return carry
        lax.fori_loop(0, v_ref.shape[0] // blk, tr, 0)
        kmax_ref[...] = jnp.broadcast_to(jnp.max(kmax_ref[...], axis=0, keepdims=True), kmax_ref.shape)

    q = (q_ref[...].astype(F32) * (DK_A ** -0.5 * LOG2E)).astype(BF16)
    qf = q.astype(F32)
    qs_ref[:, 0:blk] = jnp.where(lane < DK_A, qf, 0.0).T.astype(BF16)
    qs_ref[:, blk:nq] = jnp.where(lane >= DK_A, qf, 0.0).T.astype(BF16)
    acc_ref[...] = jnp.zeros(acc_ref.shape, F32)

    qsq = (qf * qf).astype(BF16)
    bound = []
    for m in range(2):
        lanes_m = (lax.broadcasted_iota(jnp.int32, (sub, LANES), 1) // DK_A == m).astype(BF16)
        qn2 = _dot_nt(lanes_m, qsq)
        bound.append(jnp.sqrt(qn2 * kmax_ref[:, m * DK_A:m * DK_A + 1]) * 1.03)
    bound = jnp.concatenate(bound, axis=1)
    bias_max, bias_span = bstat_ref[0, 0:1, 0:1], bstat_ref[0, 1:2, 0:1]
    r_ref[...] = bound + bias_max
    single_pass = jnp.max(2.0 * bound + bias_span) < SINGLE_PASS_LOG2_RANGE

    @pl.when(single_pass)
    def _():
        _diffattn_fixed_shift(i, k_ref, bias_ref, qs_ref, vt_ref, r_ref, m_ref, acc_ref, pa_ref, pb_ref)

    @pl.when(jnp.logical_not(single_pass))
    def _():
        _diffattn_blocks(i, k_ref, bias_ref, qs_ref, vt_ref, m_ref, acc_ref, sa_ref, sb_ref, None)

    ot = acc_ref[0:dv, 0:nq] / acc_ref[dv:dv + 1, 0:nq]
    o = ot[:, 0:blk].T - lam_ref[...] * ot[:, blk:nq].T
    o = o * lax.rsqrt(jnp.mean(o * o, axis=-1, keepdims=True) + EPS) * g_ref[...]
    o_ref[...] = (o * out_scale).astype(o_ref.dtype)


def _diffattn_fixed_shift(i, k_ref, bias_ref, qs_ref, vt_ref, shift_ref, l_ref, acc_ref, pa_ref, pb_ref):
    blk = BLK_A
    nq = 2 * blk
    sub = SUBLANES
    npart = len(pa_ref)
    wq = nq // npart
    l_ref[...] = jnp.zeros(l_ref.shape, F32)

    def probs(b, p_ref, bias):
        k = k_ref[pl.ds(pl.multiple_of(b * blk, blk), blk), :]
        for part in range(npart):
            cols = slice(part * wq, (part + 1) * wq)
            s = _dot(k, qs_ref[:, cols])
            if bias is not None:
                q0 = (part * wq) % blk
                s = s + bias[:, q0:q0 + wq]
            p = jnp.exp2(s.reshape(blk // sub, sub, wq) - shift_ref[:, cols][None])
            l_ref[:, cols] += jnp.sum(p, axis=0)
            p_ref[part][:, 0:wq] = p.reshape(blk, wq).astype(BF16)

    def accumulate(b, p_ref):
        vt = vt_ref[0:DV_A, pl.ds(pl.multiple_of(b * blk, blk), blk)]
        for part in range(npart):
            cols = slice(part * wq, (part + 1) * wq)
            acc_ref[0:DV_A, cols] += _dot(vt, p_ref[part][:, 0:wq])

    @pl.when(i == 0)
    def _():
        probs(0, pa_ref, bias_ref[0, 1])
        accumulate(0, pa_ref)

    @pl.when(i > 0)
    def _():
        nfar = i - 1
        probs(i, pa_ref, bias_ref[0, 1])
        probs(i - 1, pb_ref, bias_ref[0, 0])
        accumulate(i, pa_ref)

        def pair(t):
            probs(2 * t, pa_ref, None)
            accumulate(jnp.where(t == 0, i - 1, 2 * t - 1), pb_ref)
            probs(2 * t + 1, pb_ref, None)
            accumulate(2 * t, pa_ref)

        def two_pairs(u, carry):
            pair(2 * u)
            pair(2 * u + 1)
            return carry

        npairs = nfar // 2
        lax.fori_loop(0, npairs // 2, two_pairs, 0)

        @pl.when(lax.rem(npairs, 2) == 1)
        def _():
            pair(npairs - 1)
        in_pb = jnp.where(npairs == 0, i - 1, 2 * npairs - 1)

        @pl.when(lax.rem(nfar, 2) == 1)
        def _():
            probs(nfar - 1, pa_ref, None)
            accumulate(in_pb, pb_ref)
            accumulate(nfar - 1, pa_ref)

        @pl.when(lax.rem(nfar, 2) == 0)
        def _():
            accumulate(in_pb, pb_ref)

    acc_ref[DV_A:DV_A + sub, 0:nq] = jnp.broadcast_to(jnp.sum(l_ref[...], axis=0, keepdims=True), (sub, nq))


def _diffattn_blocks(i, k_ref, bias_ref, qs_ref, vt_ref, m_ref, acc_ref, sa_ref, sb_ref, shift_ref):
    blk = BLK_A
    nq = 2 * blk
    sub = SUBLANES
    npart = len(sa_ref) // 2
    wq = nq // npart
    online = shift_ref is None
    if online:
        m_ref[...] = jnp.full(m_ref.shape, NEG_INF, F32)

    def scores(b, s_ref):
        k = k_ref[pl.ds(pl.multiple_of(b * blk, blk), blk), :]
        for part in range(npart):
            s = _dot(k, qs_ref[:, part * wq:(part + 1) * wq])
            s_ref[part][:, 0:wq] = s
            if online:
                s_ref[npart + part][...] = jnp.max(s.reshape(blk // sub, sub, wq), axis=0)

    def softmax_pv(b, s_ref, bias):
        vt = vt_ref[:, pl.ds(pl.multiple_of(b * blk, blk), blk)]
        for part in range(npart):
            cols = slice(part * wq, (part + 1) * wq)
            s = s_ref[part][:, 0:wq]
            if bias is not None:
                q0 = (part * wq) % blk
                s = s + bias[:, q0:q0 + wq]
            s = s.reshape(blk // sub, sub, wq)
            if not online:
                p = jnp.exp2(s - shift_ref[:, cols][None])
                acc_ref[:, cols] += _dot(vt, p.reshape(blk, wq).astype(BF16))
                continue
            m_prev = m_ref[:, cols]
            smax = jnp.max(s, axis=0) if bias is not None else s_ref[npart + part][...]
            m_cur = jnp.max(smax, axis=0, keepdims=True)
            m_new = jnp.maximum(m_prev, m_cur)
            alpha = jnp.exp2(m_prev - m_new)
            p = jnp.exp2(s - m_new[None])
            pv = _dot(vt, p.reshape(blk, wq).astype(BF16))
            acc_ref[:, cols] = acc_ref[:, cols] * alpha[0:1] + pv
            m_ref[:, cols] = m_new

    nfar = jnp.maximum(i - 1, 0)
    odd = lax.rem(nfar, 2)

    @pl.when(i == 0)
    def _():
        scores(0, sb_ref)

    @pl.when(i > 0)
    def _():
        @pl.when(odd == 1)
        def _():
            scores(0, sb_ref)
            scores(1, sa_ref)
            softmax_pv(0, sb_ref, None)

        @pl.when(odd == 0)
        def _():
            scores(0, sa_ref)

        def pair(b):
            scores(b + 1, sb_ref)
            softmax_pv(b, sa_ref, None)
            scores(b + 2, sa_ref)
            softmax_pv(b + 1, sb_ref, None)

        def quad_body(t, carry):
            pair(odd + 4 * t)
            pair(odd + 4 * t + 2)
            return carry

        npairs = nfar // 2
        lax.fori_loop(0, npairs // 2, quad_body, 0)

        @pl.when(lax.rem(npairs, 2) == 1)
        def _():
            pair(odd + 2 * (npairs - 1))
        scores(i, sb_ref)
        softmax_pv(i - 1, sa_ref, bias_ref[0, 0])

    softmax_pv(i, sb_ref, bias_ref[0, 1])


_TOEPLITZ_ROWS = 256
_TOEPLITZ_N = 2048


def _toeplitz_kernel(v_ref, o_ref, *, keep):
    rows, cols = o_ref.shape[2:]
    x = jnp.broadcast_to(v_ref[0, 0], (rows, v_ref.shape[-1]))
    tile = pltpu.roll(x, 0, 1, stride=1, stride_axis=0)[:, :cols]
    r = lax.broadcasted_iota(jnp.int32, (rows, cols), 0) + pl.program_id(1) * rows
    c = lax.broadcasted_iota(jnp.int32, (rows, cols), 1)
    for variant in range(o_ref.shape[0]):
        o_ref[variant, 0] = jnp.where(keep(r, c, variant), tile, NEG_INF)


def _toeplitz_tiles(fn, keep, heads, rows, cols, variants=1):
    n, rb = _TOEPLITZ_N, _TOEPLITZ_ROWS
    assert rows % rb == 0 and rows <= n // 2 and cols <= n // 2
    idx = jnp.arange(n, dtype=jnp.int32)
    vec = fn(jnp.where(idx < n // 2, idx, idx - n)).astype(F32)
    vecs = jnp.stack([jnp.roll(vec, k * rb, axis=1) for k in range(rows // rb)], axis=1)
    return pl.pallas_call(
        functools.partial(_toeplitz_kernel, keep=keep),
        grid=(heads, rows // rb),
        in_specs=[pl.BlockSpec((1, 1, 1, n), lambda h, k: (h, k, 0, 0))],
        out_specs=pl.BlockSpec((variants, 1, rb, cols), lambda h, k: (0, h, k, 0)),
        out_shape=jax.ShapeDtypeStruct((variants, heads, rows, cols), F32),
        name="toeplitz_tiles",
    )(vecs.reshape(heads, rows // rb, 1, n))


def _t5_bucket(rel):
    nb = NUM_BUCKETS // 2
    max_exact = nb // 2
    bucket = jnp.where(rel > 0, nb, 0)
    n = jnp.abs(rel)
    nf = jnp.maximum(n, 1).astype(F32)
    large = max_exact + (jnp.log(nf / max_exact) / math.log(MAX_DISTANCE / max_exact)
                         * (nb - max_exact)).astype(jnp.int32)
    large = jnp.minimum(large, nb - 1)
    return bucket + jnp.where(n < max_exact, n, large)


def _diff_bias_tiles(t5_table):
    blk = BLK_A
    table = t5_table.astype(F32)
    far = table[_t5_bucket(jnp.full((), -(blk + 1), jnp.int32))]
    def visible(r, c, variant):
        return jnp.floor_divide(r - blk, CHUNK) <= jnp.floor_divide(c, CHUNK)

    tiles = _toeplitz_tiles(lambda x: ((table[_t5_bucket(-x - blk)] - far) * LOG2E).T, visible,
                            N_HEADS_A, 2 * blk, blk)
    return tiles.reshape(N_HEADS_A, 2, blk, blk)


def _diff_attention(proj, t5_table, lam, subln_g, lam_init):
    seq = proj.shape[0]
    blk = BLK_A
    bias = _diff_bias_tiles(t5_table)
    ha = N_HEADS_A
    finite = bias > 0.5 * NEG_INF
    bias_max = jnp.maximum(jnp.max(jnp.where(finite, bias, NEG_INF), axis=(1, 2, 3)), 0.0)
    bias_min = jnp.minimum(jnp.min(jnp.where(finite, bias, -NEG_INF), axis=(1, 2, 3)), 0.0)
    bstat = jnp.broadcast_to(jnp.stack([bias_max, bias_max - bias_min], axis=1)[:, :, None], (ha, 2, LANES))
    kern = functools.partial(_diffattn_kernel, out_scale=1.0 - lam_init)
    return pl.pallas_call(
        kern,
        grid=(ha, seq // blk),
        in_specs=[
            pl.BlockSpec((blk, DV_A), lambda h, i: (i, h)),
            pl.BlockSpec((seq, DV_A), lambda h, i: (0, ha + h)),
            pl.BlockSpec((seq, DV_A), lambda h, i: (0, 2 * ha + h)),
            pl.BlockSpec((1, 2, blk, blk), lambda h, i: (h, 0, 0, 0)),
            pl.BlockSpec((1, 2, LANES), lambda h, i: (h, 0, 0)),
            pl.BlockSpec((1, DV_A), lambda h, i: (0, 0)),
            pl.BlockSpec((1, DV_A), lambda h, i: (0, 0)),
        ],
        out_specs=pl.BlockSpec((blk, DV_A), lambda h, i: (i, h)),
        out_shape=jax.ShapeDtypeStruct((seq, ha * DV_A), BF16),
        scratch_shapes=[
            pltpu.VMEM((DV_A, 2 * blk), BF16),
            pltpu.VMEM((DV_A + ONES_A, seq), BF16),
            pltpu.VMEM((SUBLANES, LANES), F32),
            pltpu.VMEM((SUBLANES, 2 * blk), F32),
            pltpu.VMEM((SUBLANES, 2 * blk), F32),
            pltpu.VMEM((DV_A + ONES_A, 2 * blk), F32),
        ] + 2 * ([pltpu.VMEM((blk, 2 * blk // NPART_A + SCORE_PAD), F32)] * NPART_A
                 + [pltpu.VMEM((SUBLANES, 2 * blk // NPART_A), F32)] * NPART_A)
        + 2 * [pltpu.VMEM((blk, 2 * blk // NPART_A + SCORE_PAD), BF16)] * NPART_A,
        compiler_params=pltpu.CompilerParams(dimension_semantics=("parallel", "arbitrary")),
        name="diff_attention",
    )(proj, proj, proj, bias, bstat, jnp.full((1, DV_A), lam, F32), subln_g.reshape(1, DV_A).astype(F32))


def _band_kernel(q_ref, kp_ref, kc_ref, vp_ref, vc_ref, *refs):
    qw, band = QW_B, BAND_B
    nbias = band // qw + 1
    bias_refs, o_ref, s_refs = refs[:nbias], refs[nbias], refs[nbias + 1:]
    nk = band + qw
    sub = SUBLANES
    q = q_ref[...].astype(F32) * (DH_B ** -0.5 * LOG2E)
    lane = lax.broadcasted_iota(jnp.int32, q.shape, 1)
    qh = (jnp.where(lane < DH_B, q, 0.0).T.astype(BF16), jnp.where(lane >= DH_B, q, 0.0).T.astype(BF16))
    k_all = jnp.concatenate([kp_ref[...], kc_ref[...]], axis=0)
    vt_all = jnp.concatenate([vp_ref[...], vc_ref[...]], axis=0).astype(F32).T.astype(BF16)
    vt_all = jnp.concatenate([vt_all, jnp.ones((ONES_A, vt_all.shape[1]), BF16)], axis=0)
    ngroups = len(s_refs)
    for g in range(ngroups):
        k0 = g * qw
        qs = jnp.concatenate([qh[0][:, k0:k0 + qw], qh[1][:, k0:k0 + qw]], axis=1)
        s_refs[g][:, 0:2 * qw] = _dot(k_all[k0:k0 + nk], qs)
    for g in range(ngroups):
        k0 = g * qw
        bias_ref = bias_refs[min(g, nbias - 1)]
        bias = jnp.concatenate([bias_ref[0, 0], bias_ref[0, 1]], axis=1)
        s = (s_refs[g][:, 0:2 * qw] + bias).reshape(nk // sub, sub, 2 * qw)
        m = jnp.max(jnp.max(s, axis=0), axis=0, keepdims=True)
        p = jnp.exp2(s - m[None])
        pv = _dot(vt_all[:, k0:k0 + nk], p.reshape(nk, 2 * qw).astype(BF16))
        ot = pv[0:2 * DH_B] / pv[2 * DH_B:2 * DH_B + 1]
        o = jnp.concatenate([ot[0:DH_B, 0:qw], ot[DH_B:2 * DH_B, qw:2 * qw]], axis=0)
        o_ref[k0:k0 + qw, :] = o.T.astype(o_ref.dtype)


def _band_bias_tiles(rel_bias):
    band = BAND_B

    def valid(r, c, variant):
        qchunk = jnp.floor_divide(c, CHUNK)
        kchunk = jnp.floor_divide(r - band, CHUNK)
        missing = jnp.where(variant == 0, 0, band - (variant - 1) * QW_B)
        return (kchunk <= qchunk) & (kchunk >= qchunk - LEFT_CHUNKS) & (r >= missing)

    return _toeplitz_tiles(
        lambda x: rel_bias.astype(F32)[:, jnp.clip(-x - band, -REL_CLIP, REL_CLIP) + REL_CLIP] * LOG2E, valid,
        N_HEADS_B, band + QW_B, QW_B, variants=1 + band // QW_B)


def _band_attention(proj, rel_bias):
    seq = proj.shape[0]
    blk, band, qw = BLK_B, BAND_B, QW_B
    bias = _band_bias_tiles(rel_bias)
    npair = N_HEADS_B // 2
    qc0 = 3 * N_HEADS_A
    per = blk // band
    prev = lambda c0: (lambda hp, i: (jnp.maximum(i * per - 1, 0), c0 + hp))
    cur = lambda c0: (lambda hp, i: (i, c0 + hp))
    return pl.pallas_call(
        _band_kernel,
        grid=(npair, seq // blk),
        in_specs=[
            pl.BlockSpec((blk, LANES), cur(qc0)),
            pl.BlockSpec((band, LANES), prev(qc0 + npair)),
            pl.BlockSpec((blk, LANES), cur(qc0 + npair)),
            pl.BlockSpec((band, LANES), prev(qc0 + 2 * npair)),
            pl.BlockSpec((blk, LANES), cur(qc0 + 2 * npair)),
        ] + [
            pl.BlockSpec((1, 2, band + qw, qw), (lambda hp, i, t=t: (jnp.where(i == 0, 1 + t, 0), hp, 0, 0)))
            for t in range(band // qw)
        ] + [
            pl.BlockSpec((1, 2, band + qw, qw), lambda hp, i: (0, hp, 0, 0)),
        ],
        out_specs=pl.BlockSpec((blk, LANES), lambda hp, i: (i, hp)),
        out_shape=jax.ShapeDtypeStruct((seq, N_HEADS_B * DH_B), BF16),
        scratch_shapes=[pltpu.VMEM((band + qw, 2 * qw + SCORE_PAD), F32)] * (blk // qw),
        compiler_params=pltpu.CompilerParams(dimension_semantics=("parallel", "arbitrary")),
        name="band_attention",
    )(proj, proj, proj, proj, proj, *([bias] * bias.shape[0]))


def _retention_kernel(qk_ref, v_ref, gate_ref, cos_ref, sin_ref, qdec_ref, kdec_ref, dmat_ref,
                      sdec_ref, o_ref, state_ref):
    @pl.when(pl.program_id(0) == 0)
    def _():
        state_ref[...] = jnp.zeros(state_ref.shape, F32)

    cos = cos_ref[...]
    sin = sin_ref[...]
    lane = lax.broadcasted_iota(jnp.int32, cos.shape, 1)
    first_half = (lane % DQK_C) < (DQK_C // 2)
    qk = qk_ref[...]
    parts = []
    for j in range(qk.shape[1] // LANES):
        t = qk[:, j * LANES:(j + 1) * LANES]
        partner = jnp.where(first_half, pltpu.roll(t, LANES - DQK_C // 2, 1), pltpu.roll(t, DQK_C // 2, 1))
        parts.append(t * cos + partner * sin)
    wq = N_HEADS_C * DQK_C
    q = jnp.concatenate(parts[:wq // LANES], axis=1)
    k = jnp.concatenate(parts[wq // LANES:], axis=1) * (DQK_C ** -0.5)
    qd = (q * qdec_ref[...]).astype(BF16)
    kd = (k * kdec_ref[...]).astype(BF16)
    qb = q.astype(BF16)
    kb = k.astype(BF16)
    vb = v_ref[...].astype(BF16)
    gate = gate_ref[...]
    outs = []
    for h in range(N_HEADS_C):
        qs = slice(h * DQK_C, (h + 1) * DQK_C)
        vs = slice(h * DV_C, (h + 1) * DV_C)
        scores = _dot_nt(qb[:, qs], kb[:, qs]) * dmat_ref[h]
        state = state_ref[h]
        r = _dot(scores.astype(BF16), vb[:, vs]) + _dot(qd[:, qs], state.astype(BF16))
        state_ref[h] = state * sdec_ref[h] + _dot_tn(kd[:, qs], vb[:, vs])
        r = r * lax.rsqrt(jnp.mean(r * r, axis=-1, keepdims=True) + EPS)
        g = gate[:, vs]
        outs.append(r * (g * jax.nn.sigmoid(g)))
    o_ref[...] = jnp.concatenate(outs, axis=1).astype(o_ref.dtype)


def _retention_tables(seq):
    t = BLK_C
    half = DQK_C // 2
    inv_freq = 1.0 / (ROPE_BASE ** (jnp.arange(0, DQK_C, 2, dtype=F32) / DQK_C))
    ang = jnp.arange(seq, dtype=F32)[:, None] * inv_freq[None, :]
    reps = LANES // half
    cos = jnp.tile(jnp.cos(ang), (1, reps))
    sign = jnp.where((jnp.arange(LANES) % DQK_C) < half, -1.0, 1.0).astype(F32)
    sin = jnp.tile(jnp.sin(ang), (1, reps)) * sign[None, :]
    log_g = jnp.log(1.0 - jnp.power(2.0, -5.0 - jnp.arange(N_HEADS_C, dtype=F32)))
    pos = jnp.arange(t, dtype=F32)
    diff = pos[:, None] - pos[None, :]
    same_or_past = (jnp.arange(t)[None, :] // CHUNK) <= (jnp.arange(t)[:, None] // CHUNK)
    dmat = jnp.where(same_or_past[None], jnp.exp(log_g[:, None, None] * jnp.abs(diff)[None]), 0.0)
    qdec = jnp.repeat(jnp.exp(log_g[None, :] * (pos[:, None] + 1.0)), DQK_C, axis=1)
    kdec = jnp.repeat(jnp.exp(log_g[None, :] * (t - 1.0 - pos[:, None])), DQK_C, axis=1)
    sdec = jnp.broadcast_to(jnp.exp(log_g * t)[:, None, None], (N_HEADS_C, 1, DV_C))
    return cos, sin, qdec, kdec, dmat, sdec


def _retention(proj):
    seq = proj.shape[0]
    t = BLK_C
    cos, sin, qdec, kdec, dmat, sdec = _retention_tables(seq)
    wv = N_HEADS_C * DV_C
    return pl.pallas_call(
        _retention_kernel,
        grid=(seq // t,),
        in_specs=[
            pl.BlockSpec((t, wv), lambda i: (i, 0)),
            pl.BlockSpec((t, wv), lambda i: (i, 1)),
            pl.BlockSpec((t, wv), lambda i: (i, 2)),
            pl.BlockSpec((t, LANES), lambda i: (i, 0)),
            pl.BlockSpec((t, LANES), lambda i: (i, 0)),
            pl.BlockSpec((t, N_HEADS_C * DQK_C), lambda i: (0, 0)),
            pl.BlockSpec((t, N_HEADS_C * DQK_C), lambda i: (0, 0)),
            pl.BlockSpec((N_HEADS_C, t, t), lambda i: (0, 0, 0)),
            pl.BlockSpec((N_HEADS_C, 1, DV_C), lambda i: (0, 0, 0)),
        ],
        out_specs=pl.BlockSpec((t, wv), lambda i: (i, 0)),
        out_shape=jax.ShapeDtypeStruct((seq, wv), BF16),
        scratch_shapes=[pltpu.VMEM((N_HEADS_C, DQK_C, DV_C), F32)],
        compiler_params=pltpu.CompilerParams(dimension_semantics=("arbitrary",)),
        name="retention",
    )(proj, proj, proj, cos, sin, qdec, kdec, dmat, sdec)


def _s5_kernel(*refs):
    ncb = S5_CH // LANES
    u_refs = refs[:ncb]
    (mt_ref, bt_ref, ctr_ref, cti_ref, are_ref, aim_ref, y_ref,
     ut_ref, yt_ref, ys_ref, vr_ref, vi_ref, spr_ref, spi_ref, carry_ref) = refs[ncb:]
    tc = S5_TC
    gp = S5_GROUP
    n = S5_STATE
    ng = S5_GROUPS

    @pl.when(pl.program_id(0) == 0)
    def _():
        carry_ref[...] = jnp.zeros(carry_ref.shape, F32)

    for s in range(S5_T):
        for k in range(ncb):
            ut_ref[s, k * LANES:(k + 1) * LANES, :] = u_refs[k][pl.ds(s, tc, stride=S5_T), :].T

    unroll = 4

    def intra(it, carry):
        for k in range(unroll):
            g = it * unroll + k
            r0 = pl.multiple_of(g * gp, gp)
            ug = ut_ref[:, pl.ds(r0, gp), :].reshape(S5_T * gp, tc).astype(BF16)
            yt_ref[:, pl.ds(r0, gp), :] = _dot(mt_ref[g], ug).reshape(S5_T, gp, tc)
            vt = _dot(bt_ref[g], ug)
            n0 = pl.multiple_of(g * n, n)
            vr_ref[pl.ds(n0, n), :] = vt[0:n]
            vi_ref[pl.ds(n0, n), :] = vt[n:2 * n]
        return carry

    lax.fori_loop(0, ng // unroll, intra, 0)

    sub = SUBLANES
    nv = tc // sub
    row = lax.broadcasted_iota(jnp.int32, (tc, LANES), 0)
    in_vreg = lax.rem(row, sub)

    def rows_of(v, r):
        return jnp.broadcast_to(v[r:r + 1], (tc, LANES))

    for j in range(ng * n // LANES):
        cols = slice(j * LANES, (j + 1) * LANES)
        pwr, pwi = are_ref[:, cols], aim_ref[:, cols]
        xr = vr_ref[cols, :].T
        xi = vi_ref[cols, :].T
        for d in (1, 2, 4):
            keep = in_vreg >= d
            sr = jnp.where(keep, pltpu.roll(xr, d, 0), 0.0)
            si = jnp.where(keep, pltpu.roll(xi, d, 0), 0.0)
            fr, fi = rows_of(pwr, d - 1), rows_of(pwi, d - 1)
            xr, xi = xr + (fr * sr - fi * si), xi + (fr * si + fi * sr)
        cr, ci = carry_ref[0, :, cols], carry_ref[1, :, cols]
        cr0, ci0 = cr, ci
        outr, outi = [], []
        for v in range(nv):
            yr = xr[v * sub:(v + 1) * sub] + (pwr * cr - pwi * ci)
            yi = xi[v * sub:(v + 1) * sub] + (pwr * ci + pwi * cr)
            outr.append(yr)
            outi.append(yi)
            cr = jnp.broadcast_to(yr[sub - 1:sub], (sub, LANES))
            ci = jnp.broadcast_to(yi[sub - 1:sub], (sub, LANES))
        carry_ref[0, :, cols] = cr
        carry_ref[1, :, cols] = ci
        sr = jnp.concatenate(outr, axis=0)
        si = jnp.concatenate(outi, axis=0)
        first = row == 0
        spr_ref[j] = jnp.where(first, rows_of(cr0, 0), pltpu.roll(sr, 1, 0))
        spi_ref[j] = jnp.where(first, rows_of(ci0, 0), pltpu.roll(si, 1, 0))

    def cross(it, carry):
        for k in range(unroll):
            jp = it * unroll + k
            r0 = pl.multiple_of(jp * 2 * gp, 2 * gp)
            yc = (_dot_nt(ctr_ref[jp], spr_ref[jp].astype(BF16))
                  + _dot_nt(cti_ref[jp], spi_ref[jp].astype(BF16)))
            yt_ref[:, pl.ds(r0, 2 * gp), :] += yc.reshape(S5_T, 2 * gp, tc)
        return carry

    lax.fori_loop(0, ng // 2 // unroll, cross, 0)

    for s in range(S5_T):
        for k in range(ncb):
            ys_ref[k, pl.ds(s, tc, stride=S5_T), :] = yt_ref[s, k * LANES:(k + 1) * LANES, :].T
    for k in range(ncb):
        y_ref[:, k * LANES:(k + 1) * LANES] = ys_ref[k]


def _s5_matrices(lam_re, lam_im, log_step, b_re, b_im, c_re, c_im, d_skip):
    hi = lax.Precision.HIGHEST
    t, gp, n, ng = S5_T, S5_GROUP, S5_STATE, S5_GROUPS
    lam = lax.complex(lam_re.astype(F32), lam_im.astype(F32))
    step = jnp.exp(log_step.astype(F32))[:, None]
    ls = lam * step
    a_bar = jnp.exp(ls)
    b_bar = ((a_bar - 1.0) / lam)[..., None] * lax.complex(b_re.astype(F32), b_im.astype(F32))
    cm = lax.complex(c_re.astype(F32), c_im.astype(F32))

    def apow(k):
        kk = k.astype(F32).astype(jnp.complex64)
        return jnp.exp(ls.reshape((ng,) + (1,) * k.ndim + (n,)) * kk[None, ..., None])

    tt = jnp.arange(t)
    kmat = jnp.einsum('gpn,gln,gnq->glpq', cm, apow(tt), b_bar, precision=hi).real
    krev = jnp.transpose(kmat[:, ::-1], (0, 2, 1, 3)).reshape(ng, gp, t * gp)
    kpad = jnp.pad(krev, ((0, 0), (0, 0), (0, t * gp)))
    mt = jnp.concatenate([kpad[:, :, (t - 1 - to) * gp:(2 * t - 1 - to) * gp] for to in range(t)], axis=1)
    dvec = jnp.tile(d_skip.astype(F32).reshape(ng, 1, gp), (1, t, 1)).reshape(ng, t * gp)
    mt = mt + jnp.eye(t * gp, dtype=F32)[None] * dvec[:, :, None]
    z = jnp.swapaxes(apow(t - 1 - tt), 1, 2)[:, :, :, None] * b_bar[:, :, None, :]
    z = z.reshape(ng, n, t * gp)
    bt = jnp.concatenate([z.real, z.imag], axis=1)
    w = cm[:, None, :, :] * apow(tt + 1)[:, :, None, :]

    def pair_readout(x):
        x = x.reshape(ng // 2, 2, t, gp, n)
        first = jnp.pad(x[:, 0], ((0, 0), (0, 0), (0, 0), (0, n)))
        second = jnp.pad(x[:, 1], ((0, 0), (0, 0), (0, 0), (n, 0)))
        return jnp.stack([first, second], axis=2).reshape(ng // 2, t * 2 * gp, 2 * n).astype(BF16)

    ctr, cti = pair_readout(w.real), pair_readout(-w.imag)
    a_chunk = jnp.transpose(apow(t * (jnp.arange(SUBLANES) + 1)), (1, 0, 2)).reshape(SUBLANES, ng * n)
    return mt.astype(BF16), bt.astype(BF16), ctr, cti, a_chunk.real, a_chunk.imag


def _s5(proj, mats):
    seq, width = proj.shape
    t, tc, gp, n, ng = S5_T, S5_TC, S5_GROUP, S5_STATE, S5_GROUPS
    rows = t * tc
    ncb = S5_CH // LANES
    cb0 = (width - S5_CH) // LANES
    u_specs = [pl.BlockSpec((rows, LANES), (lambda i, k=k: (i, cb0 + k))) for k in range(ncb)]
    nsb = ng * n // LANES
    return pl.pallas_call(
        _s5_kernel,
        grid=(seq // rows,),
        in_specs=u_specs + [_const_spec(m.shape) for m in mats],
        out_specs=pl.BlockSpec((rows, S5_CH), lambda i: (i, 0)),
        out_shape=jax.ShapeDtypeStruct((seq, S5_CH), F32),
        scratch_shapes=[
            pltpu.VMEM((t, S5_CH, tc), F32),
            pltpu.VMEM((t, S5_CH, tc), F32),
            pltpu.VMEM((ncb, rows, LANES), F32),
            pltpu.VMEM((ng * n, tc), F32),
            pltpu.VMEM((ng * n, tc), F32),
            pltpu.VMEM((nsb, tc, LANES), F32),
            pltpu.VMEM((nsb, tc, LANES), F32),
            pltpu.VMEM((2, SUBLANES, ng * n), F32),
        ],
        compiler_params=pltpu.CompilerParams(dimension_semantics=("arbitrary",)),
        name="s5_scan",
    )(*([proj] * ncb), *mats)


def _mix_ffn_kernel(*refs, glu, final):
    (x_ref, a_ref, b_ref, wo_ref, g1_ref), refs = refs[:5], refs[5:]
    if glu:
        gw_ref, refs = refs[0], refs[1:]
    (g_ref, sc_ref, sh_ref, gate_ref, win_ref, cw_ref, cb_ref, wout_ref), refs = refs[:8], refs[8:]
    if final:
        fg_ref, o_ref, h_ref, act_ref, gbuf_ref, carry_ref = refs
    else:
        ng_ref, nsc_ref, nsh_ref, o_ref, hn_ref, h_ref, act_ref, gbuf_ref, carry_ref = refs
    tm = x_ref.shape[0]
    halo = gbuf_ref.shape[0] - tm

    @pl.when(pl.program_id(0) == 0)
    def _():
        carry_ref[...] = jnp.zeros(carry_ref.shape, F32)

    if glu:
        y = jax.nn.gelu(b_ref[...]).astype(BF16)
        gg = _dot(y, gw_ref[...])
        half = gg.shape[1] // 2
        b = (gg[:, :half] * jax.nn.sigmoid(gg[:, half:])).astype(BF16)
    else:
        b = b_ref[...]
    cat = jnp.concatenate([a_ref[...], b], axis=1)
    x = x_ref[...] + g1_ref[...] * _dot(cat, wo_ref[...])
    h_ref[...] = _mod_rmsnorm(x, g_ref[...], sc_ref[...], sh_ref[...]).astype(BF16)
    for f in range(D_FF // TF_FFN):
        cs = slice(f * TF_FFN, (f + 1) * TF_FFN)
        gs = slice(D_FF + f * TF_FFN, D_FF + (f + 1) * TF_FFN)
        h = h_ref[...]
        val = _dot(h, win_ref[:, cs])
        gate = _dot(h, win_ref[:, gs])
        gbuf_ref[0:halo, :] = carry_ref[:, cs]
        gbuf_ref[halo:halo + tm, :] = gate
        carry_ref[:, cs] = gate[tm - halo:tm, :]
        conv = (gate * cw_ref[2:3, cs] + gbuf_ref[halo - 1:halo - 1 + tm, :] * cw_ref[1:2, cs]
                + gbuf_ref[halo - 2:halo - 2 + tm, :] * cw_ref[0:1, cs] + cb_ref[:, cs])
        act_ref[:, cs] = (jax.nn.gelu(conv) * val).astype(BF16)
    xn = x + gate_ref[...] * _dot(act_ref[...], wout_ref[...])
    if final:
        xn = xn * lax.rsqrt(jnp.mean(xn * xn, axis=-1, keepdims=True) + EPS) * fg_ref[...]
    else:
        hn_ref[...] = _mod_rmsnorm(xn, ng_ref[...], nsc_ref[...], nsh_ref[...]).astype(BF16)
    o_ref[...] = xn


def _layer_spec(shape, layer):
    idx = (layer,) + (0,) * (len(shape) - 1)
    return pl.BlockSpec((None,) + tuple(shape[1:]), lambda *_: idx, pipeline_mode=pl.Buffered(1))


def _mix_ffn(x, a, b, wo, gate1, glu_w, g, scale, shift, gate2, w_in, conv_w, conv_b, w_out, tail, layer):
    seq, d = x.shape
    final = len(tail) == 1
    tm = TM_FFN
    halo = SUBLANES
    row = pl.BlockSpec((1, d), lambda i: (0, 0))
    rows = lambda w: pl.BlockSpec((tm, w), lambda i: (i, 0))
    conv_b = conv_b.reshape(conv_b.shape[0], 1, D_FF)
    in_specs = [rows(d), rows(a.shape[1]), rows(b.shape[1]), _const_spec(wo.shape), row]
    args = [x, a, b, wo, gate1]
    if glu_w is not None:
        in_specs.append(_const_spec(glu_w.shape))
        args.append(glu_w)
    in_specs += [
        row, row, row, row,
        _layer_spec(w_in.shape, layer),
        _layer_spec(conv_w.shape, layer),
        _layer_spec(conv_b.shape, layer),
        _layer_spec(w_out.shape, layer),
    ] + [row] * len(tail)
    args += [g.reshape(1, d), scale, shift, gate2, w_in, conv_w, conv_b, w_out]
    args += [t.reshape(1, d) for t in tail]
    out_specs = [rows(d)] if final else [rows(d), rows(d)]
    out_shape = [jax.ShapeDtypeStruct((seq, d), F32)] + ([] if final else [jax.ShapeDtypeStruct((seq, d), BF16)])
    return pl.pallas_call(
        functools.partial(_mix_ffn_kernel, glu=glu_w is not None, final=final),
        grid=(seq // tm,),
        in_specs=in_specs,
        out_specs=out_specs,
        out_shape=out_shape,
        scratch_shapes=[
            pltpu.VMEM((tm, d), BF16),
            pltpu.VMEM((tm, D_FF), BF16),
            pltpu.VMEM((tm + halo, TF_FFN), F32),
            pltpu.VMEM((halo, D_FF), F32),
        ],
        compiler_params=pltpu.CompilerParams(dimension_semantics=("arbitrary",)),
        name="mix_ffn",
    )(*args)


def kernel(x, c, t5_table, mod_w, mod_b, norm1_g, norm2_g, ffn_w_in, ffn_conv_w, ffn_conv_b, ffn_w_out,
           ev_w_in, ev_w_out, diff_lambda, diff_subln_g, band_rel_bias,
           od_w_in, od_w_out, s5_lam_re, s5_lam_im, s5_log_step, s5_b_re, s5_b_im, s5_c_re, s5_c_im,
           s5_d, s5_glu_w, final_g):
    assert x.shape[0] == 1 and x.shape[2] == D_MODEL
    seq = x.shape[1]
    assert seq % TM_PROJ == 0 and seq % (S5_T * S5_TC) == 0
    d = D_MODEL
    xs = x[0]
    mod = _modulation(c, mod_w, mod_b)
    ffn_w_in_b = ffn_w_in.astype(BF16)
    ffn_w_out_b = ffn_w_out.astype(BF16)
    mods = [[mod[i, :, k * d:(k + 1) * d] for k in range(6)] for i in range(DEPTH)]
    h = None
    for i in range(DEPTH):
        sh1, sc1, g1, sh2, sc2, g2 = mods[i]
        w_in = (ev_w_in if i % 2 == 0 else od_w_in)[i // 2].astype(BF16)
        proj_dtype = BF16 if i % 2 == 0 else F32
        if h is None:
            proj = _normproj(xs, norm1_g[i], sc1, sh1, w_in, proj_dtype)
        else:
            proj = _proj(h, w_in, proj_dtype)
        if i % 2 == 0:
            e = i // 2
            lam_init = 0.8 - 0.6 * math.exp(-0.3 * i)
            lp = diff_lambda[e].astype(F32)
            lam = jnp.exp(jnp.sum(lp[0] * lp[1])) - jnp.exp(jnp.sum(lp[2] * lp[3])) + lam_init
            mix_a = _diff_attention(proj, t5_table, lam, diff_subln_g[e], lam_init)
            mix_b = _band_attention(proj, band_rel_bias[e])
            wo, glu_w = ev_w_out[e].astype(BF16), None
        else:
            o = i // 2
            mix_a = _retention(proj)
            mats = _s5_matrices(s5_lam_re[o], s5_lam_im[o], s5_log_step[o], s5_b_re[o], s5_b_im[o],
                                s5_c_re[o], s5_c_im[o], s5_d[o])
            mix_b = _s5(proj, mats)
            wo, glu_w = od_w_out[o].astype(BF16), s5_glu_w[o].astype(BF16)
        if i == DEPTH - 1:
            tail = (final_g,)
        else:
            nsh1, nsc1 = mods[i + 1][0], mods[i + 1][1]
            tail = (norm1_g[i + 1], nsc1, nsh1)
        out = _mix_ffn(xs, mix_a, mix_b, wo, g1, glu_w, norm2_g[i], sc2, sh2, g2,
                       ffn_w_in_b, ffn_conv_w, ffn_conv_b, ffn_w_out_b, tail, layer=i)
        if i == DEPTH - 1:
            xs = out[0]
        else:
            xs, h = out
    return xs[None]
```

```python
import functools
import math

import jax
import jax.numpy as jnp
from jax import lax
from jax.experimental import pallas as pl
from jax.experimental.pallas import tpu as pltpu

F32 = jnp.float32
BF16 = jnp.bfloat16

D_MODEL = 1024
DEPTH = 2
CHUNK = 64
GROUP_WIDTH = D_MODEL // 2
DK_A = 64
DV_A = 2 * DK_A
N_HEADS_A = GROUP_WIDTH // DV_A
DH_B = 64
N_HEADS_B = GROUP_WIDTH // DH_B
LEFT_CHUNKS = 8
REL_CLIP = 2 * CHUNK
NUM_BUCKETS = 32
MAX_DISTANCE = 128
DV_C = 128
DQK_C = DV_C // 2
N_HEADS_C = GROUP_WIDTH // DV_C
ROPE_BASE = 10000.0
S5_CH = GROUP_WIDTH
S5_GROUP = 16
S5_GROUPS = S5_CH // S5_GROUP
S5_STATE = 64
D_FF = ((8 * D_MODEL // 3 + 255) // 256) * 256
CONV_W = 3
EVEN_IN = 3 * N_HEADS_A * DV_A + 3 * N_HEADS_B * DH_B
ODD_IN = 2 * N_HEADS_C * DQK_C + 2 * N_HEADS_C * DV_C + S5_CH
EPS = 1e-6
NEG_INF = -1e30
LOG2E = math.log2(math.e)

LANES = 128
SUBLANES = 8
MXU_DIM = 256

TM_PROJ = 1024
TN_PROJ = 1024
TN_MOD = 1536
TM_FFN = 512
TF_FFN = MXU_DIM
BLK_A = 512
NPART_A = 2
ONES_A = 16
SINGLE_PASS_LOG2_RANGE = 96.0
SCORE_PAD = LANES
BLK_B = 1024
BAND_B = LEFT_CHUNKS * CHUNK
QW_B = 4 * CHUNK
BLK_C = 512
S5_T = 16
S5_TC = LANES

assert BLK_B % BAND_B == 0 and BLK_B % QW_B == 0 and BAND_B % QW_B == 0
assert BLK_A >= MAX_DISTANCE, "far key blocks must sit in the saturated T5 bucket"
assert DV_A == LANES and 2 * DK_A == LANES and 2 * DH_B == LANES, "attention heads are read as 128-lane column blocks"


def _dot(a, b):
    return jnp.dot(a, b, preferred_element_type=F32)


def _dot_nt(a, b):
    return lax.dot_general(a, b, (((1,), (1,)), ((), ())), preferred_element_type=F32)


def _dot_tn(a, b):
    return lax.dot_general(a, b, (((0,), (0,)), ((), ())), preferred_element_type=F32)


def _const_spec(shape):
    zeros = (0,) * len(shape)
    return pl.BlockSpec(shape, lambda *_: zeros, pipeline_mode=pl.Buffered(1))


def _mod_rmsnorm(x, g, scale, shift):
    y = x * lax.rsqrt(jnp.mean(x * x, axis=-1, keepdims=True) + EPS)
    y = y * g
    return y * (1.0 + scale) + shift


def _mod_kernel(c_ref, w_ref, b_ref, o_ref):
    c = c_ref[...]
    cond = c * jax.nn.sigmoid(c)
    o_ref[0] = jnp.sum(cond * w_ref[0], axis=0, keepdims=True) + b_ref[0]


def _modulation(c, mod_w, mod_b):
    depth, d, n = mod_w.shape
    tn = TN_MOD
    return pl.pallas_call(
        _mod_kernel,
        grid=(depth, n // tn),
        in_specs=[
            pl.BlockSpec((d, 1), lambda i, j: (0, 0)),
            pl.BlockSpec((1, d, tn), lambda i, j: (i, 0, j)),
            pl.BlockSpec((1, 1, tn), lambda i, j: (i, 0, j)),
        ],
        out_specs=pl.BlockSpec((1, 1, tn), lambda i, j: (i, 0, j)),
        out_shape=jax.ShapeDtypeStruct((depth, 1, n), F32),
        name="modulation",
    )(c.reshape(d, 1), mod_w, mod_b.reshape(depth, 1, n))


def _normproj_kernel(x_ref, g_ref, sc_ref, sh_ref, w_ref, o_ref):
    tm, n = o_ref.shape
    half = tm // 2
    for r in range(2):
        rows = slice(r * half, (r + 1) * half)
        h = _mod_rmsnorm(x_ref[rows, :], g_ref[...], sc_ref[...], sh_ref[...]).astype(BF16)
        for j in range(n // TN_PROJ):
            cols = slice(j * TN_PROJ, (j + 1) * TN_PROJ)
            o_ref[rows, cols] = _dot(h, w_ref[:, cols]).astype(o_ref.dtype)


def _normproj(x, g, scale, shift, w, out_dtype):
    seq, d = x.shape
    n = w.shape[1]
    tm = TM_PROJ
    row = pl.BlockSpec((1, d), lambda i: (0, 0))
    return pl.pallas_call(
        _normproj_kernel,
        grid=(seq // tm,),
        in_specs=[pl.BlockSpec((tm, d), lambda i: (i, 0)), row, row, row, _const_spec(w.shape)],
        out_specs=pl.BlockSpec((tm, n), lambda i: (i, 0)),
        out_shape=jax.ShapeDtypeStruct((seq, n), out_dtype),
        compiler_params=pltpu.CompilerParams(dimension_semantics=("parallel",)),
        name="normproj",
    )(x, g.reshape(1, d), scale, shift, w)


def _proj_kernel(h_ref, w_ref, o_ref):
    for j in range(o_ref.shape[1] // TN_PROJ):
        cols = slice(j * TN_PROJ, (j + 1) * TN_PROJ)
        o_ref[:, cols] = _dot(h_ref[...], w_ref[:, cols]).astype(o_ref.dtype)


def _proj(h, w, out_dtype):
    seq, d = h.shape
    n = w.shape[1]
    tm = TM_PROJ
    return pl.pallas_call(
        _proj_kernel,
        grid=(seq // tm,),
        in_specs=[pl.BlockSpec((tm, d), lambda i: (i, 0)), _const_spec(w.shape)],
        out_specs=pl.BlockSpec((tm, n), lambda i: (i, 0)),
        out_shape=jax.ShapeDtypeStruct((seq, n), out_dtype),
        compiler_params=pltpu.CompilerParams(dimension_semantics=("parallel",)),
        name="proj",
    )(h, w)


def _diffattn_kernel(q_ref, k_ref, v_ref, bias_ref, bstat_ref, lam_ref, g_ref, o_ref,
                     qs_ref, vt_ref, kmax_ref, r_ref, m_ref, acc_ref, *s_refs, out_scale):
    blk = BLK_A
    nq = 2 * blk
    sub = SUBLANES
    dv = DV_A
    npart = NPART_A
    sa_ref, sb_ref = s_refs[:2 * npart], s_refs[2 * npart:4 * npart]
    pa_ref, pb_ref = s_refs[4 * npart:5 * npart], s_refs[5 * npart:6 * npart]
    i = pl.program_id(1)
    lane = lax.broadcasted_iota(jnp.int32, (blk, LANES), 1)
    same_subhead = (lax.broadcasted_iota(jnp.int32, (LANES, LANES), 0) // DK_A
                    == lax.broadcasted_iota(jnp.int32, (LANES, LANES), 1) // DK_A).astype(BF16)

    @pl.when(i == 0)
    def _():
        kmax_ref[...] = jnp.zeros(kmax_ref.shape, F32)

        def tr(b, carry):
            r0 = pl.multiple_of(b * blk, blk)
            vt_ref[0:dv, pl.ds(r0, blk)] = v_ref[pl.ds(r0, blk), :].astype(F32).T.astype(BF16)
            vt_ref[dv:dv + ONES_A, pl.ds(r0, blk)] = jnp.ones((ONES_A, blk), BF16)
            kf = k_ref[pl.ds(r0, blk), :].astype(F32)
            kn2 = _dot((kf * kf).astype(BF16), same_subhead)
            kmax_ref[...] = jnp.maximum(kmax_ref[...], jnp.max(kn2.reshape(blk // sub, sub, LANES), axis=0))
            return carry
        lax.fori_loop(0, v_ref.shape[0] // blk, tr, 0)
        kmax_ref[...] = jnp.broadcast_to(jnp.max(kmax_ref[...], axis=0, keepdims=True), kmax_ref.shape)

    q = (q_ref[...].astype(F32) * (DK_A ** -0.5 * LOG2E)).astype(BF16)
    qf = q.astype(F32)
    qs_ref[:, 0:blk] = jnp.where(lane < DK_A, qf, 0.0).T.astype(BF16)
    qs_ref[:, blk:nq] = jnp.where(lane >= DK_A, qf, 0.0).T.astype(BF16)
    acc_ref[...] = jnp.zeros(acc_ref.shape, F32)

    qsq = (qf * qf).astype(BF16)
    bound = []
    for m in range(2):
        lanes_m = (lax.broadcasted_iota(jnp.int32, (sub, LANES), 1) // DK_A == m).astype(BF16)
        qn2 = _dot_nt(lanes_m, qsq)
        bound.append(jnp.sqrt(qn2 * kmax_ref[:, m * DK_A:m * DK_A + 1]) * 1.03)
    bound = jnp.concatenate(bound, axis=1)
    bias_max, bias_span = bstat_ref[0, 0:1, 0:1], bstat_ref[0, 1:2, 0:1]
    r_ref[...] = bound + bias_max
    single_pass = jnp.max(2.0 * bound + bias_span) < SINGLE_PASS_LOG2_RANGE

    @pl.when(single_pass)
    def _():
        _diffattn_fixed_shift(i, k_ref, bias_ref, qs_ref, vt_ref, r_ref, m_ref, acc_ref, pa_ref, pb_ref)

    @pl.when(jnp.logical_not(single_pass))
    def _():
        _diffattn_online(i, k_ref, bias_ref, qs_ref, vt_ref, m_ref, acc_ref, sa_ref, sb_ref)

    ot = acc_ref[0:dv, 0:nq] / acc_ref[dv:dv + 1, 0:nq]
    o = ot[:, 0:blk].T - lam_ref[...] * ot[:, blk:nq].T
    o = o * lax.rsqrt(jnp.mean(o * o, axis=-1, keepdims=True) + EPS) * g_ref[...]
    o_ref[...] = (o * out_scale).astype(o_ref.dtype)


def _diffattn_fixed_shift(i, k_ref, bias_ref, qs_ref, vt_ref, shift_ref, l_ref, acc_ref, pa_ref, pb_ref):
    blk = BLK_A
    nq = 2 * blk
    sub = SUBLANES
    npart = len(pa_ref)
    wq = nq // npart
    l_ref[...] = jnp.zeros(l_ref.shape, F32)

    def probs(b, p_ref, bias):
        k = k_ref[pl.ds(pl.multiple_of(b * blk, blk), blk), :]
        for part in range(npart):
            cols = slice(part * wq, (part + 1) * wq)
            s = _dot(k, qs_ref[:, cols])
            if bias is not None:
                q0 = (part * wq) % blk
                s = s + bias[:, q0:q0 + wq]
            p = jnp.exp2(s.reshape(blk // sub, sub, wq) - shift_ref[:, cols][None])
            l_ref[:, cols] += jnp.sum(p, axis=0)
            p_ref[part][:, 0:wq] = p.reshape(blk, wq).astype(BF16)

    def accumulate(b, p_ref):
        vt = vt_ref[0:DV_A, pl.ds(pl.multiple_of(b * blk, blk), blk)]
        for part in range(npart):
            cols = slice(part * wq, (part + 1) * wq)
            acc_ref[0:DV_A, cols] += _dot(vt, p_ref[part][:, 0:wq])

    @pl.when(i == 0)
    def _():
        probs(0, pa_ref, bias_ref[0, 1])
        accumulate(0, pa_ref)

    @pl.when(i > 0)
    def _():
        nfar = i - 1
        probs(i, pa_ref, bias_ref[0, 1])
        probs(i - 1, pb_ref, bias_ref[0, 0])
        accumulate(i, pa_ref)

        def pair(t):
            probs(2 * t, pa_ref, None)
            accumulate(jnp.where(t == 0, i - 1, 2 * t - 1), pb_ref)
            probs(2 * t + 1, pb_ref, None)
            accumulate(2 * t, pa_ref)

        def two_pairs(u, carry):
            pair(2 * u)
            pair(2 * u + 1)
            return carry

        npairs = nfar // 2
        lax.fori_loop(0, npairs // 2, two_pairs, 0)

        @pl.when(lax.rem(npairs, 2) == 1)
        def _():
            pair(npairs - 1)
        in_pb = jnp.where(npairs == 0, i - 1, 2 * npairs - 1)

        @pl.when(lax.rem(nfar, 2) == 1)
        def _():
            probs(nfar - 1, pa_ref, None)
            accumulate(in_pb, pb_ref)
            accumulate(nfar - 1, pa_ref)

        @pl.when(lax.rem(nfar, 2) == 0)
        def _():
            accumulate(in_pb, pb_ref)

    acc_ref[DV_A:DV_A + sub, 0:nq] = jnp.broadcast_to(jnp.sum(l_ref[...], axis=0, keepdims=True), (sub, nq))


def _diffattn_online(i, k_ref, bias_ref, qs_ref, vt_ref, m_ref, acc_ref, sa_ref, sb_ref):
    blk = BLK_A
    nq = 2 * blk
    sub = SUBLANES
    npart = len(sa_ref) // 2
    wq = nq // npart
    m_ref[...] = jnp.full(m_ref.shape, NEG_INF, F32)

    def scores(b, s_ref):
        k = k_ref[pl.ds(pl.multiple_of(b * blk, blk), blk), :]
        for part in range(npart):
            s = _dot(k, qs_ref[:, part * wq:(part + 1) * wq])
            s_ref[part][:, 0:wq] = s
            s_ref[npart + part][...] = jnp.max(s.reshape(blk // sub, sub, wq), axis=0)

    def softmax_pv(b, s_ref, bias):
        vt = vt_ref[:, pl.ds(pl.multiple_of(b * blk, blk), blk)]
        for part in range(npart):
            cols = slice(part * wq, (part + 1) * wq)
            s = s_ref[part][:, 0:wq]
            if bias is not None:
                q0 = (part * wq) % blk
                s = s + bias[:, q0:q0 + wq]
            s = s.reshape(blk // sub, sub, wq)
            m_prev = m_ref[:, cols]
            smax = jnp.max(s, axis=0) if bias is not None else s_ref[npart + part][...]
            m_cur = jnp.max(smax, axis=0, keepdims=True)
            m_new = jnp.maximum(m_prev, m_cur)
            alpha = jnp.exp2(m_prev - m_new)
            p = jnp.exp2(s - m_new[None])
            pv = _dot(vt, p.reshape(blk, wq).astype(BF16))
            acc_ref[:, cols] = acc_ref[:, cols] * alpha[0:1] + pv
            m_ref[:, cols] = m_new

    nfar = jnp.maximum(i - 1, 0)
    odd = lax.rem(nfar, 2)

    @pl.when(i == 0)
    def _():
        scores(0, sb_ref)

    @pl.when(i > 0)
    def _():
        @pl.when(odd == 1)
        def _():
            scores(0, sb_ref)
            scores(1, sa_ref)
            softmax_pv(0, sb_ref, None)

        @pl.when(odd == 0)
        def _():
            scores(0, sa_ref)

        def pair(b):
            scores(b + 1, sb_ref)
            softmax_pv(b, sa_ref, None)
            scores(b + 2, sa_ref)
            softmax_pv(b + 1, sb_ref, None)

        def quad_body(t, carry):
            pair(odd + 4 * t)
            pair(odd + 4 * t + 2)
            return carry

        npairs = nfar // 2
        lax.fori_loop(0, npairs // 2, quad_body, 0)

        @pl.when(lax.rem(npairs, 2) == 1)
        def _():
            pair(odd + 2 * (npairs - 1))
        scores(i, sb_ref)
        softmax_pv(i - 1, sa_ref, bias_ref[0, 0])

    softmax_pv(i, sb_ref, bias_ref[0, 1])


_TOEPLITZ_ROWS = 256
_TOEPLITZ_N = 2048


def _toeplitz_kernel(v_ref, o_ref, *, keep):
    rows, cols = o_ref.shape[2:]
    x = jnp.broadcast_to(v_ref[0, 0], (rows, v_ref.shape[-1]))
    tile = pltpu.roll(x, 0, 1, stride=1, stride_axis=0)[:, :cols]
    r = lax.broadcasted_iota(jnp.int32, (rows, cols), 0) + pl.program_id(1) * rows
    c = lax.broadcasted_iota(jnp.int32, (rows, cols), 1)
    for variant in range(o_ref.shape[0]):
        o_ref[variant, 0] = jnp.where(keep(r, c, variant), tile, NEG_INF)


def _toeplitz_tiles(fn, keep, heads, rows, cols, variants=1):
    n, rb = _TOEPLITZ_N, _TOEPLITZ_ROWS
    assert rows % rb == 0 and rows <= n // 2 and cols <= n // 2
    idx = jnp.arange(n, dtype=jnp.int32)
    vec = fn(jnp.where(idx < n // 2, idx, idx - n)).astype(F32)
    vecs = jnp.stack([jnp.roll(vec, k * rb, axis=1) for k in range(rows // rb)], axis=1)
    return pl.pallas_call(
        functools.partial(_toeplitz_kernel, keep=keep),
        grid=(heads, rows // rb),
        in_specs=[pl.BlockSpec((1, 1, 1, n), lambda h, k: (h, k, 0, 0))],
        out_specs=pl.BlockSpec((variants, 1, rb, cols), lambda h, k: (0, h, k, 0)),
        out_shape=jax.ShapeDtypeStruct((variants, heads, rows, cols), F32),
        name="toeplitz_tiles",
    )(vecs.reshape(heads, rows // rb, 1, n))


def _t5_bucket(rel):
    nb = NUM_BUCKETS // 2
    max_exact = nb // 2
    bucket = jnp.where(rel > 0, nb, 0)
    n = jnp.abs(rel)
    nf = jnp.maximum(n, 1).astype(F32)
    large = max_exact + (jnp.log(nf / max_exact) / math.log(MAX_DISTANCE / max_exact)
                         * (nb - max_exact)).astype(jnp.int32)
    large = jnp.minimum(large, nb - 1)
    return bucket + jnp.where(n < max_exact, n, large)


def _diff_bias_tiles(t5_table):
    blk = BLK_A
    table = t5_table.astype(F32)
    far = table[_t5_bucket(jnp.full((), -(blk + 1), jnp.int32))]
    def visible(r, c, variant):
        return jnp.floor_divide(r - blk, CHUNK) <= jnp.floor_divide(c, CHUNK)

    tiles = _toeplitz_tiles(lambda x: ((table[_t5_bucket(-x - blk)] - far) * LOG2E).T, visible,
                            N_HEADS_A, 2 * blk, blk)
    return tiles.reshape(N_HEADS_A, 2, blk, blk)


def _diff_attention(proj, t5_table, lam, subln_g, lam_init):
    seq = proj.shape[0]
    blk = BLK_A
    bias = _diff_bias_tiles(t5_table)
    ha = N_HEADS_A
    finite = bias > 0.5 * NEG_INF
    bias_max = jnp.maximum(jnp.max(jnp.where(finite, bias, NEG_INF), axis=(1, 2, 3)), 0.0)
    bias_min = jnp.minimum(jnp.min(jnp.where(finite, bias, -NEG_INF), axis=(1, 2, 3)), 0.0)
    bstat = jnp.broadcast_to(jnp.stack([bias_max, bias_max - bias_min], axis=1)[:, :, None], (ha, 2, LANES))
    kern = functools.partial(_diffattn_kernel, out_scale=1.0 - lam_init)
    return pl.pallas_call(
        kern,
        grid=(ha, seq // blk),
        in_specs=[
            pl.BlockSpec((blk, DV_A), lambda h, i: (i, h)),
            pl.BlockSpec((seq, DV_A), lambda h, i: (0, ha + h)),
            pl.BlockSpec((seq, DV_A), lambda h, i: (0, 2 * ha + h)),
            pl.BlockSpec((1, 2, blk, blk), lambda h, i: (h, 0, 0, 0)),
            pl.BlockSpec((1, 2, LANES), lambda h, i: (h, 0, 0)),
            pl.BlockSpec((1, DV_A), lambda h, i: (0, 0)),
            pl.BlockSpec((1, DV_A), lambda h, i: (0, 0)),
        ],
        out_specs=pl.BlockSpec((blk, DV_A), lambda h, i: (i, h)),
        out_shape=jax.ShapeDtypeStruct((seq, ha * DV_A), BF16),
        scratch_shapes=[
            pltpu.VMEM((DV_A, 2 * blk), BF16),
            pltpu.VMEM((DV_A + ONES_A, seq), BF16),
            pltpu.VMEM((SUBLANES, LANES), F32),
            pltpu.VMEM((SUBLANES, 2 * blk), F32),
            pltpu.VMEM((SUBLANES, 2 * blk), F32),
            pltpu.VMEM((DV_A + ONES_A, 2 * blk), F32),
        ] + 2 * ([pltpu.VMEM((blk, 2 * blk // NPART_A + SCORE_PAD), F32)] * NPART_A
                 + [pltpu.VMEM((SUBLANES, 2 * blk // NPART_A), F32)] * NPART_A)
        + 2 * [pltpu.VMEM((blk, 2 * blk // NPART_A + SCORE_PAD), BF16)] * NPART_A,
        compiler_params=pltpu.CompilerParams(dimension_semantics=("parallel", "arbitrary")),
        name="diff_attention",
    )(proj, proj, proj, bias, bstat, jnp.full((1, DV_A), lam, F32), subln_g.reshape(1, DV_A).astype(F32))


def _band_kernel(q_ref, kp_ref, kc_ref, vp_ref, vc_ref, *refs):
    qw, band = QW_B, BAND_B
    nbias = band // qw + 1
    bias_refs, o_ref, s_refs = refs[:nbias], refs[nbias], refs[nbias + 1:]
    nk = band + qw
    sub = SUBLANES
    q = q_ref[...].astype(F32) * (DH_B ** -0.5 * LOG2E)
    lane = lax.broadcasted_iota(jnp.int32, q.shape, 1)
    qh = (jnp.where(lane < DH_B, q, 0.0).T.astype(BF16), jnp.where(lane >= DH_B, q, 0.0).T.astype(BF16))
    k_all = jnp.concatenate([kp_ref[...], kc_ref[...]], axis=0)
    vt_all = jnp.concatenate([vp_ref[...], vc_ref[...]], axis=0).astype(F32).T.astype(BF16)
    vt_all = jnp.concatenate([vt_all, jnp.ones((ONES_A, vt_all.shape[1]), BF16)], axis=0)
    ngroups = len(s_refs)
    for g in range(ngroups):
        k0 = g * qw
        qs = jnp.concatenate([qh[0][:, k0:k0 + qw], qh[1][:, k0:k0 + qw]], axis=1)
        s_refs[g][:, 0:2 * qw] = _dot(k_all[k0:k0 + nk], qs)
    for g in range(ngroups):
        k0 = g * qw
        bias_ref = bias_refs[min(g, nbias - 1)]
        bias = jnp.concatenate([bias_ref[0, 0], bias_ref[0, 1]], axis=1)
        s = (s_refs[g][:, 0:2 * qw] + bias).reshape(nk // sub, sub, 2 * qw)
        m = jnp.max(jnp.max(s, axis=0), axis=0, keepdims=True)
        p = jnp.exp2(s - m[None])
        pv = _dot(vt_all[:, k0:k0 + nk], p.reshape(nk, 2 * qw).astype(BF16))
        ot = pv[0:2 * DH_B] / pv[2 * DH_B:2 * DH_B + 1]
        o = jnp.concatenate([ot[0:DH_B, 0:qw], ot[DH_B:2 * DH_B, qw:2 * qw]], axis=0)
        o_ref[k0:k0 + qw, :] = o.T.astype(o_ref.dtype)


def _band_bias_tiles(rel_bias):
    band = BAND_B

    def valid(r, c, variant):
        qchunk = jnp.floor_divide(c, CHUNK)
        kchunk = jnp.floor_divide(r - band, CHUNK)
        missing = jnp.where(variant == 0, 0, band - (variant - 1) * QW_B)
        return (kchunk <= qchunk) & (kchunk >= qchunk - LEFT_CHUNKS) & (r >= missing)

    return _toeplitz_tiles(
        lambda x: rel_bias.astype(F32)[:, jnp.clip(-x - band, -REL_CLIP, REL_CLIP) + REL_CLIP] * LOG2E, valid,
        N_HEADS_B, band + QW_B, QW_B, variants=1 + band // QW_B)


def _band_attention(proj, rel_bias):
    seq = proj.shape[0]
    blk, band, qw = BLK_B, BAND_B, QW_B
    bias = _band_bias_tiles(rel_bias)
    npair = N_HEADS_B // 2
    qc0 = 3 * N_HEADS_A
    per = blk // band
    prev = lambda c0: (lambda hp, i: (jnp.maximum(i * per - 1, 0), c0 + hp))
    cur = lambda c0: (lambda hp, i: (i, c0 + hp))
    return pl.pallas_call(
        _band_kernel,
        grid=(npair, seq // blk),
        in_specs=[
            pl.BlockSpec((blk, LANES), cur(qc0)),
            pl.BlockSpec((band, LANES), prev(qc0 + npair)),
            pl.BlockSpec((blk, LANES), cur(qc0 + npair)),
            pl.BlockSpec((band, LANES), prev(qc0 + 2 * npair)),
            pl.BlockSpec((blk, LANES), cur(qc0 + 2 * npair)),
        ] + [
            pl.BlockSpec((1, 2, band + qw, qw), (lambda hp, i, t=t: (jnp.where(i == 0, 1 + t, 0), hp, 0, 0)))
            for t in range(band // qw)
        ] + [
            pl.BlockSpec((1, 2, band + qw, qw), lambda hp, i: (0, hp, 0, 0)),
        ],
        out_specs=pl.BlockSpec((blk, LANES), lambda hp, i: (i, hp)),
        out_shape=jax.ShapeDtypeStruct((seq, N_HEADS_B * DH_B), BF16),
        scratch_shapes=[pltpu.VMEM((band + qw, 2 * qw + SCORE_PAD), F32)] * (blk // qw),
        compiler_params=pltpu.CompilerParams(dimension_semantics=("parallel", "arbitrary")),
        name="band_attention",
    )(proj, proj, proj, proj, proj, *([bias] * bias.shape[0]))


def _retention_kernel(qk_ref, v_ref, gate_ref, cos_ref, sin_ref, qdec_ref, kdec_ref, dmat_ref,
                      sdec_ref, o_ref, state_ref):
    @pl.when(pl.program_id(0) == 0)
    def _():
        state_ref[...] = jnp.zeros(state_ref.shape, F32)

    cos = cos_ref[...]
    sin = sin_ref[...]
    lane = lax.broadcasted_iota(jnp.int32, cos.shape, 1)
    first_half = (lane % DQK_C) < (DQK_C // 2)
    qk = qk_ref[...]
    parts = []
    for j in range(qk.shape[1] // LANES):
        t = qk[:, j * LANES:(j + 1) * LANES]
        partner = jnp.where(first_half, pltpu.roll(t, LANES - DQK_C // 2, 1), pltpu.roll(t, DQK_C // 2, 1))
        parts.append(t * cos + partner * sin)
    wq = N_HEADS_C * DQK_C
    q = jnp.concatenate(parts[:wq // LANES], axis=1)
    k = jnp.concatenate(parts[wq // LANES:], axis=1) * (DQK_C ** -0.5)
    qd = (q * qdec_ref[...]).astype(BF16)
    kd = (k * kdec_ref[...]).astype(BF16)
    qb = q.astype(BF16)
    kb = k.astype(BF16)
    vb = v_ref[...].astype(BF16)
    gate = gate_ref[...]
    outs = []
    for h in range(N_HEADS_C):
        qs = slice(h * DQK_C, (h + 1) * DQK_C)
        vs = slice(h * DV_C, (h + 1) * DV_C)
        scores = _dot_nt(qb[:, qs], kb[:, qs]) * dmat_ref[h]
        state = state_ref[h]
        r = _dot(scores.astype(BF16), vb[:, vs]) + _dot(qd[:, qs], state.astype(BF16))
        state_ref[h] = state * sdec_ref[h] + _dot_tn(kd[:, qs], vb[:, vs])
        r = r * lax.rsqrt(jnp.mean(r * r, axis=-1, keepdims=True) + EPS)
        g = gate[:, vs]
        outs.append(r * (g * jax.nn.sigmoid(g)))
    o_ref[...] = jnp.concatenate(outs, axis=1).astype(o_ref.dtype)


def _retention_tables(seq):
    t = BLK_C
    half = DQK_C // 2
    inv_freq = 1.0 / (ROPE_BASE ** (jnp.arange(0, DQK_C, 2, dtype=F32) / DQK_C))
    ang = jnp.arange(seq, dtype=F32)[:, None] * inv_freq[None, :]
    reps = LANES // half
    cos = jnp.tile(jnp.cos(ang), (1, reps))
    sign = jnp.where((jnp.arange(LANES) % DQK_C) < half, -1.0, 1.0).astype(F32)
    sin = jnp.tile(jnp.sin(ang), (1, reps)) * sign[None, :]
    log_g = jnp.log(1.0 - jnp.power(2.0, -5.0 - jnp.arange(N_HEADS_C, dtype=F32)))
    pos = jnp.arange(t, dtype=F32)
    diff = pos[:, None] - pos[None, :]
    same_or_past = (jnp.arange(t)[None, :] // CHUNK) <= (jnp.arange(t)[:, None] // CHUNK)
    dmat = jnp.where(same_or_past[None], jnp.exp(log_g[:, None, None] * jnp.abs(diff)[None]), 0.0)
    qdec = jnp.repeat(jnp.exp(log_g[None, :] * (pos[:, None] + 1.0)), DQK_C, axis=1)
    kdec = jnp.repeat(jnp.exp(log_g[None, :] * (t - 1.0 - pos[:, None])), DQK_C, axis=1)
    sdec = jnp.broadcast_to(jnp.exp(log_g * t)[:, None, None], (N_HEADS_C, 1, DV_C))
    return cos, sin, qdec, kdec, dmat, sdec


def _retention(proj):
    seq = proj.shape[0]
    t = BLK_C
    cos, sin, qdec, kdec, dmat, sdec = _retention_tables(seq)
    wv = N_HEADS_C * DV_C
    return pl.pallas_call(
        _retention_kernel,
        grid=(seq // t,),
        in_specs=[
            pl.BlockSpec((t, wv), lambda i: (i, 0)),
            pl.BlockSpec((t, wv), lambda i: (i, 1)),
            pl.BlockSpec((t, wv), lambda i: (i, 2)),
            pl.BlockSpec((t, LANES), lambda i: (i, 0)),
            pl.BlockSpec((t, LANES), lambda i: (i, 0)),
            pl.BlockSpec((t, N_HEADS_C * DQK_C), lambda i: (0, 0)),
            pl.BlockSpec((t, N_HEADS_C * DQK_C), lambda i: (0, 0)),
            pl.BlockSpec((N_HEADS_C, t, t), lambda i: (0, 0, 0)),
            pl.BlockSpec((N_HEADS_C, 1, DV_C), lambda i: (0, 0, 0)),
        ],
        out_specs=pl.BlockSpec((t, wv), lambda i: (i, 0)),
        out_shape=jax.ShapeDtypeStruct((seq, wv), BF16),
        scratch_shapes=[pltpu.VMEM((N_HEADS_C, DQK_C, DV_C), F32)],
        compiler_params=pltpu.CompilerParams(dimension_semantics=("arbitrary",)),
        name="retention",
    )(proj, proj, proj, cos, sin, qdec, kdec, dmat, sdec)


def _s5_kernel(*refs):
    ncb = S5_CH // LANES
    u_refs = refs[:ncb]
    (mt_ref, bt_ref, ctr_ref, cti_ref, are_ref, aim_ref, y_ref,
     ut_ref, yt_ref, ys_ref, vr_ref, vi_ref, spr_ref, spi_ref, carry_ref) = refs[ncb:]
    tc = S5_TC
    gp = S5_GROUP
    n = S5_STATE
    ng = S5_GROUPS

    @pl.when(pl.program_id(0) == 0)
    def _():
        carry_ref[...] = jnp.zeros(carry_ref.shape, F32)

    for s in range(S5_T):
        for k in range(ncb):
            ut_ref[s, k * LANES:(k + 1) * LANES, :] = u_refs[k][pl.ds(s, tc, stride=S5_T), :].T

    unroll = 4

    def intra(it, carry):
        for k in range(unroll):
            g = it * unroll + k
            r0 = pl.multiple_of(g * gp, gp)
            ug = ut_ref[:, pl.ds(r0, gp), :].reshape(S5_T * gp, tc).astype(BF16)
            yt_ref[:, pl.ds(r0, gp), :] = _dot(mt_ref[g], ug).reshape(S5_T, gp, tc)
            vt = _dot(bt_ref[g], ug)
            n0 = pl.multiple_of(g * n, n)
            vr_ref[pl.ds(n0, n), :] = vt[0:n]
            vi_ref[pl.ds(n0, n), :] = vt[n:2 * n]
        return carry

    lax.fori_loop(0, ng // unroll, intra, 0)

    sub = SUBLANES
    nv = tc // sub
    row = lax.broadcasted_iota(jnp.int32, (tc, LANES), 0)
    in_vreg = lax.rem(row, sub)

    def rows_of(v, r):
        return jnp.broadcast_to(v[r:r + 1], (tc, LANES))

    for j in range(ng * n // LANES):
        cols = slice(j * LANES, (j + 1) * LANES)
        pwr, pwi = are_ref[:, cols], aim_ref[:, cols]
        xr = vr_ref[cols, :].T
        xi = vi_ref[cols, :].T
        for d in (1, 2, 4):
            keep = in_vreg >= d
            sr = jnp.where(keep, pltpu.roll(xr, d, 0), 0.0)
            si = jnp.where(keep, pltpu.roll(xi, d, 0), 0.0)
            fr, fi = rows_of(pwr, d - 1), rows_of(pwi, d - 1)
            xr, xi = xr + (fr * sr - fi * si), xi + (fr * si + fi * sr)
        cr, ci = carry_ref[0, :, cols], carry_ref[1, :, cols]
        cr0, ci0 = cr, ci
        outr, outi = [], []
        for v in range(nv):
            yr = xr[v * sub:(v + 1) * sub] + (pwr * cr - pwi * ci)
            yi = xi[v * sub:(v + 1) * sub] + (pwr * ci + pwi * cr)
            outr.append(yr)
            outi.append(yi)
            cr = jnp.broadcast_to(yr[sub - 1:sub], (sub, LANES))
            ci = jnp.broadcast_to(yi[sub - 1:sub], (sub, LANES))
        carry_ref[0, :, cols] = cr
        carry_ref[1, :, cols] = ci
        sr = jnp.concatenate(outr, axis=0)
        si = jnp.concatenate(outi, axis=0)
        first = row == 0
        spr_ref[j] = jnp.where(first, rows_of(cr0, 0), pltpu.roll(sr, 1, 0))
        spi_ref[j] = jnp.where(first, rows_of(ci0, 0), pltpu.roll(si, 1, 0))

    def cross(it, carry):
        for k in range(unroll):
            jp = it * unroll + k
            r0 = pl.multiple_of(jp * 2 * gp, 2 * gp)
            yc = (_dot_nt(ctr_ref[jp], spr_ref[jp].astype(BF16))
                  + _dot_nt(cti_ref[jp], spi_ref[jp].astype(BF16)))
            yt_ref[:, pl.ds(r0, 2 * gp), :] += yc.reshape(S5_T, 2 * gp, tc)
        return carry

    lax.fori_loop(0, ng // 2 // unroll, cross, 0)

    for s in range(S5_T):
        for k in range(ncb):
            ys_ref[k, pl.ds(s, tc, stride=S5_T), :] = yt_ref[s, k * LANES:(k + 1) * LANES, :].T
    for k in range(ncb):
        y_ref[:, k * LANES:(k + 1) * LANES] = ys_ref[k]


def _s5_matrices(lam_re, lam_im, log_step, b_re, b_im, c_re, c_im, d_skip):
    hi = lax.Precision.HIGHEST
    t, gp, n, ng = S5_T, S5_GROUP, S5_STATE, S5_GROUPS
    lam = lax.complex(lam_re.astype(F32), lam_im.astype(F32))
    step = jnp.exp(log_step.astype(F32))[:, None]
    ls = lam * step
    a_bar = jnp.exp(ls)
    b_bar = ((a_bar - 1.0) / lam)[..., None] * lax.complex(b_re.astype(F32), b_im.astype(F32))
    cm = lax.complex(c_re.astype(F32), c_im.astype(F32))

    def apow(k):
        kk = k.astype(F32).astype(jnp.complex64)
        return jnp.exp(ls.reshape((ng,) + (1,) * k.ndim + (n,)) * kk[None, ..., None])

    tt = jnp.arange(t)
    kmat = jnp.einsum('gpn,gln,gnq->glpq', cm, apow(tt), b_bar, precision=hi).real
    krev = jnp.transpose(kmat[:, ::-1], (0, 2, 1, 3)).reshape(ng, gp, t * gp)
    kpad = jnp.pad(krev, ((0, 0), (0, 0), (0, t * gp)))
    mt = jnp.concatenate([kpad[:, :, (t - 1 - to) * gp:(2 * t - 1 - to) * gp] for to in range(t)], axis=1)
    dvec = jnp.tile(d_skip.astype(F32).reshape(ng, 1, gp), (1, t, 1)).reshape(ng, t * gp)
    mt = mt + jnp.eye(t * gp, dtype=F32)[None] * dvec[:, :, None]
    z = jnp.swapaxes(apow(t - 1 - tt), 1, 2)[:, :, :, None] * b_bar[:, :, None, :]
    z = z.reshape(ng, n, t * gp)
    bt = jnp.concatenate([z.real, z.imag], axis=1)
    w = cm[:, None, :, :] * apow(tt + 1)[:, :, None, :]

    def pair_readout(x):
        x = x.reshape(ng // 2, 2, t, gp, n)
        first = jnp.pad(x[:, 0], ((0, 0), (0, 0), (0, 0), (0, n)))
        second = jnp.pad(x[:, 1], ((0, 0), (0, 0), (0, 0), (n, 0)))
        return jnp.stack([first, second], axis=2).reshape(ng // 2, t * 2 * gp, 2 * n).astype(BF16)

    ctr, cti = pair_readout(w.real), pair_readout(-w.imag)
    a_chunk = jnp.transpose(apow(t * (jnp.arange(SUBLANES) + 1)), (1, 0, 2)).reshape(SUBLANES, ng * n)
    return mt.astype(BF16), bt.astype(BF16), ctr, cti, a_chunk.real, a_chunk.imag


def _s5(proj, mats):
    seq, width = proj.shape
    t, tc, gp, n, ng = S5_T, S5_TC, S5_GROUP, S5_STATE, S5_GROUPS
    rows = t * tc
    ncb = S5_CH // LANES
    cb0 = (width - S5_CH) // LANES
    u_specs = [pl.BlockSpec((rows, LANES), (lambda i, k=k: (i, cb0 + k))) for k in range(ncb)]
    nsb = ng * n // LANES
    return pl.pallas_call(
        _s5_kernel,
        grid=(seq // rows,),
        in_specs=u_specs + [_const_spec(m.shape) for m in mats],
        out_specs=pl.BlockSpec((rows, S5_CH), lambda i: (i, 0)),
        out_shape=jax.ShapeDtypeStruct((seq, S5_CH), F32),
        scratch_shapes=[
            pltpu.VMEM((t, S5_CH, tc), F32),
            pltpu.VMEM((t, S5_CH, tc), F32),
            pltpu.VMEM((ncb, rows, LANES), F32),
            pltpu.VMEM((ng * n, tc), F32),
            pltpu.VMEM((ng * n, tc), F32),
            pltpu.VMEM((nsb, tc, LANES), F32),
            pltpu.VMEM((nsb, tc, LANES), F32),
            pltpu.VMEM((2, SUBLANES, ng * n), F32),
        ],
        compiler_params=pltpu.CompilerParams(dimension_semantics=("arbitrary",)),
        name="s5_scan",
    )(*([proj] * ncb), *mats)


def _mix_ffn_kernel(*refs, glu, final):
    (x_ref, a_ref, b_ref, wo_ref, g1_ref), refs = refs[:5], refs[5:]
    if glu:
        gw_ref, refs = refs[0], refs[1:]
    (g_ref, sc_ref, sh_ref, gate_ref, win_ref, cw_ref, cb_ref, wout_ref), refs = refs[:8], refs[8:]
    if final:
        fg_ref, o_ref, h_ref, act_ref, gbuf_ref, carry_ref = refs
    else:
        ng_ref, nsc_ref, nsh_ref, o_ref, hn_ref, h_ref, act_ref, gbuf_ref, carry_ref = refs
    tm = x_ref.shape[0]
    halo = gbuf_ref.shape[0] - tm

    @pl.when(pl.program_id(0) == 0)
    def _():
        carry_ref[...] = jnp.zeros(carry_ref.shape, F32)

    if glu:
        y = jax.nn.gelu(b_ref[...]).astype(BF16)
        gg = _dot(y, gw_ref[...])
        half = gg.shape[1] // 2
        b = (gg[:, :half] * jax.nn.sigmoid(gg[:, half:])).astype(BF16)
    else:
        b = b_ref[...]
    cat = jnp.concatenate([a_ref[...], b], axis=1)
    x = x_ref[...] + g1_ref[...] * _dot(cat, wo_ref[...])
    h_ref[...] = _mod_rmsnorm(x, g_ref[...], sc_ref[...], sh_ref[...]).astype(BF16)
    for f in range(D_FF // TF_FFN):
        cs = slice(f * TF_FFN, (f + 1) * TF_FFN)
        gs = slice(D_FF + f * TF_FFN, D_FF + (f + 1) * TF_FFN)
        h = h_ref[...]
        val = _dot(h, win_ref[:, cs])
        gate = _dot(h, win_ref[:, gs])
        gbuf_ref[0:halo, :] = carry_ref[:, cs]
        gbuf_ref[halo:halo + tm, :] = gate
        carry_ref[:, cs] = gate[tm - halo:tm, :]
        conv = (gate * cw_ref[2:3, cs] + gbuf_ref[halo - 1:halo - 1 + tm, :] * cw_ref[1:2, cs]
                + gbuf_ref[halo - 2:halo - 2 + tm, :] * cw_ref[0:1, cs] + cb_ref[:, cs])
        act_ref[:, cs] = (jax.nn.gelu(conv) * val).astype(BF16)
    xn = x + gate_ref[...] * _dot(act_ref[...], wout_ref[...])
    if final:
        xn = xn * lax.rsqrt(jnp.mean(xn * xn, axis=-1, keepdims=True) + EPS) * fg_ref[...]
    else:
        hn_ref[...] = _mod_rmsnorm(xn, ng_ref[...], nsc_ref[...], nsh_ref[...]).astype(BF16)
    o_ref[...] = xn


def _layer_spec(shape, layer):
    idx = (layer,) + (0,) * (len(shape) - 1)
    return pl.BlockSpec((None,) + tuple(shape[1:]), lambda *_: idx, pipeline_mode=pl.Buffered(1))


def _mix_ffn(x, a, b, wo, gate1, glu_w, g, scale, shift, gate2, w_in, conv_w, conv_b, w_out, tail, layer):
    seq, d = x.shape
    final = len(tail) == 1
    tm = TM_FFN
    halo = SUBLANES
    row = pl.BlockSpec((1, d), lambda i: (0, 0))
    rows = lambda w: pl.BlockSpec((tm, w), lambda i: (i, 0))
    conv_b = conv_b.reshape(conv_b.shape[0], 1, D_FF)
    in_specs = [rows(d), rows(a.shape[1]), rows(b.shape[1]), _const_spec(wo.shape), row]
    args = [x, a, b, wo, gate1]
    if glu_w is not None:
        in_specs.append(_const_spec(glu_w.shape))
        args.append(glu_w)
    in_specs += [
        row, row, row, row,
        _layer_spec(w_in.shape, layer),
        _layer_spec(conv_w.shape, layer),
        _layer_spec(conv_b.shape, layer),
        _layer_spec(w_out.shape, layer),
    ] + [row] * len(tail)
    args += [g.reshape(1, d), scale, shift, gate2, w_in, conv_w, conv_b, w_out]
    args += [t.reshape(1, d) for t in tail]
    out_specs = [rows(d)] if final else [rows(d), rows(d)]
    out_shape = [jax.ShapeDtypeStruct((seq, d), F32)] + ([] if final else [jax.ShapeDtypeStruct((seq, d), BF16)])
    return pl.pallas_call(
        functools.partial(_mix_ffn_kernel, glu=glu_w is not None, final=final),
        grid=(seq // tm,),
        in_specs=in_specs,
        out_specs=out_specs,
        out_shape=out_shape,
        scratch_shapes=[
            pltpu.VMEM((tm, d), BF16),
            pltpu.VMEM((tm, D_FF), BF16),
            pltpu.VMEM((tm + halo, TF_FFN), F32),
            pltpu.VMEM((halo, D_FF), F32),
        ],
        compiler_params=pltpu.CompilerParams(dimension_semantics=("arbitrary",)),
        name="mix_ffn",
    )(*args)


def kernel(x, c, t5_table, mod_w, mod_b, norm1_g, norm2_g, ffn_w_in, ffn_conv_w, ffn_conv_b, ffn_w_out,
           ev_w_in, ev_w_out, diff_lambda, diff_subln_g, band_rel_bias,
           od_w_in, od_w_out, s5_lam_re, s5_lam_im, s5_log_step, s5_b_re, s5_b_im, s5_c_re, s5_c_im,
           s5_d, s5_glu_w, final_g):
    assert x.shape[0] == 1 and x.shape[2] == D_MODEL
    seq = x.shape[1]
    assert seq % TM_PROJ == 0 and seq % (S5_T * S5_TC) == 0
    d = D_MODEL
    xs = x[0]
    mod = _modulation(c, mod_w, mod_b)
    ffn_w_in_b = ffn_w_in.astype(BF16)
    ffn_w_out_b = ffn_w_out.astype(BF16)
    mods = [[mod[i, :, k * d:(k + 1) * d] for k in range(6)] for i in range(DEPTH)]
    h = None
    for i in range(DEPTH):
        sh1, sc1, g1, sh2, sc2, g2 = mods[i]
        w_in = (ev_w_in if i % 2 == 0 else od_w_in)[i // 2].astype(BF16)
        proj_dtype = BF16 if i % 2 == 0 else F32
        if h is None:
            proj = _normproj(xs, norm1_g[i], sc1, sh1, w_in, proj_dtype)
        else:
            proj = _proj(h, w_in, proj_dtype)
        if i % 2 == 0:
            e = i // 2
            lam_init = 0.8 - 0.6 * math.exp(-0.3 * i)
            lp = diff_lambda[e].astype(F32)
            lam = jnp.exp(jnp.sum(lp[0] * lp[1])) - jnp.exp(jnp.sum(lp[2] * lp[3])) + lam_init
            mix_a = _diff_attention(proj, t5_table, lam, diff_subln_g[e], lam_init)
            mix_b = _band_attention(proj, band_rel_bias[e])
            wo, glu_w = ev_w_out[e].astype(BF16), None
        else:
            o = i // 2
            mix_a = _retention(proj)
            mats = _s5_matrices(s5_lam_re[o], s5_lam_im[o], s5_log_step[o], s5_b_re[o], s5_b_im[o],
                                s5_c_re[o], s5_c_im[o], s5_d[o])
            mix_b = _s5(proj, mats)
            wo, glu_w = od_w_out[o].astype(BF16), s5_glu_w[o].astype(BF16)
        if i == DEPTH - 1:
            tail = (final_g,)
        else:
            nsh1, nsc1 = mods[i + 1][0], mods[i + 1][1]
            tail = (norm1_g[i + 1], nsc1, nsh1)
        out = _mix_ffn(xs, mix_a, mix_b, wo, g1, glu_w, norm2_g[i], sc2, sh2, g2,
                       ffn_w_in_b, ffn_conv_w, ffn_conv_b, ffn_w_out_b, tail, layer=i)
        if i == DEPTH - 1:
            xs = out[0]
        else:
            xs, h = out
    return xs[None]
```

```python
import functools
import math

import jax
import jax.numpy as jnp
from jax import lax
from jax.experimental import pallas as pl
from jax.experimental.pallas import tpu as pltpu

F32 = jnp.float32
BF16 = jnp.bfloat16

D_MODEL = 1024
DEPTH = 2
CHUNK = 64
GROUP_WIDTH = D_MODEL // 2
DK_A = 64
DV_A = 2 * DK_A
N_HEADS_A = GROUP_WIDTH // DV_A
DH_B = 64
N_HEADS_B = GROUP_WIDTH // DH_B
LEFT_CHUNKS = 8
REL_CLIP = 2 * CHUNK
NUM_BUCKETS = 32
MAX_DISTANCE = 128
DV_C = 128
DQK_C = DV_C // 2
N_HEADS_C = GROUP_WIDTH // DV_C
ROPE_BASE = 10000.0
S5_CH = GROUP_WIDTH
S5_GROUP = 16
S5_GROUPS = S5_CH // S5_GROUP
S5_STATE = 64
D_FF = ((8 * D_MODEL // 3 + 255) // 256) * 256
CONV_W = 3
EVEN_IN = 3 * N_HEADS_A * DV_A + 3 * N_HEADS_B * DH_B
ODD_IN = 2 * N_HEADS_C * DQK_C + 2 * N_HEADS_C * DV_C + S5_CH
EPS = 1e-6
NEG_INF = -1e30
LOG2E = math.log2(math.e)

LANES = 128
SUBLANES = 8
MXU_DIM = 256

TM_PROJ = 1024
TN_PROJ = 1024
TN_MOD = 1536
TM_FFN = 512
TF_FFN = MXU_DIM
BLK_A = 512
NPART_A = 2
ONES_A = 16
SINGLE_PASS_LOG2_RANGE = 96.0
SCORE_PAD = LANES
BLK_B = 1024
BAND_B = LEFT_CHUNKS * CHUNK
QW_B = 4 * CHUNK
BLK_C = 512
S5_T = 16
S5_TC = LANES

assert BLK_B % BAND_B == 0 and BLK_B % QW_B == 0 and BAND_B % QW_B == 0
assert BLK_A >= MAX_DISTANCE, "far key blocks must sit in the saturated T5 bucket"
assert DV_A == LANES and 2 * DK_A == LANES and 2 * DH_B == LANES, "attention heads are read as 128-lane column blocks"


def _dot(a, b):
    return jnp.dot(a, b, preferred_element_type=F32)


def _dot_nt(a, b):
    return lax.dot_general(a, b, (((1,), (1,)), ((), ())), preferred_element_type=F32)


def _dot_tn(a, b):
    return lax.dot_general(a, b, (((0,), (0,)), ((), ())), preferred_element_type=F32)


def _const_spec(shape):
    zeros = (0,) * len(shape)
    return pl.BlockSpec(shape, lambda *_: zeros, pipeline_mode=pl.Buffered(1))


def _mod_rmsnorm(x, g, scale, shift):
    y = x * lax.rsqrt(jnp.mean(x * x, axis=-1, keepdims=True) + EPS)
    y = y * g
    return y * (1.0 + scale) + shift


def _mod_kernel(c_ref, w_ref, b_ref, o_ref):
    c = c_ref[...]
    cond = c * jax.nn.sigmoid(c)
    o_ref[0] = jnp.sum(cond * w_ref[0], axis=0, keepdims=True) + b_ref[0]


def _modulation(c, mod_w, mod_b):
    depth, d, n = mod_w.shape
    tn = TN_MOD
    return pl.pallas_call(
        _mod_kernel,
        grid=(depth, n // tn),
        in_specs=[
            pl.BlockSpec((d, 1), lambda i, j: (0, 0)),
            pl.BlockSpec((1, d, tn), lambda i, j: (i, 0, j)),
            pl.BlockSpec((1, 1, tn), lambda i, j: (i, 0, j)),
        ],
        out_specs=pl.BlockSpec((1, 1, tn), lambda i, j: (i, 0, j)),
        out_shape=jax.ShapeDtypeStruct((depth, 1, n), F32),
        name="modulation",
    )(c.reshape(d, 1), mod_w, mod_b.reshape(depth, 1, n))


def _normproj_kernel(x_ref, g_ref, sc_ref, sh_ref, w_ref, o_ref):
    tm, n = o_ref.shape
    half = tm // 2
    for r in range(2):
        rows = slice(r * half, (r + 1) * half)
        h = _mod_rmsnorm(x_ref[rows, :], g_ref[...], sc_ref[...], sh_ref[...]).astype(BF16)
        for j in range(n // TN_PROJ):
            cols = slice(j * TN_PROJ, (j + 1) * TN_PROJ)
            o_ref[rows, cols] = _dot(h, w_ref[:, cols]).astype(o_ref.dtype)


def _normproj(x, g, scale, shift, w, out_dtype):
    seq, d = x.shape
    n = w.shape[1]
    tm = TM_PROJ
    row = pl.BlockSpec((1, d), lambda i: (0, 0))
    return pl.pallas_call(
        _normproj_kernel,
        grid=(seq // tm,),
        in_specs=[pl.BlockSpec((tm, d), lambda i: (i, 0)), row, row, row, _const_spec(w.shape)],
        out_specs=pl.BlockSpec((tm, n), lambda i: (i, 0)),
        out_shape=jax.ShapeDtypeStruct((seq, n), out_dtype),
        compiler_params=pltpu.CompilerParams(dimension_semantics=("parallel",)),
        name="normproj",
    )(x, g.reshape(1, d), scale, shift, w)


def _proj_kernel(h_ref, w_ref, o_ref):
    for j in range(o_ref.shape[1] // TN_PROJ):
        cols = slice(j * TN_PROJ, (j + 1) * TN_PROJ)
        o_ref[:, cols] = _dot(h_ref[...], w_ref[:, cols]).astype(o_ref.dtype)


def _proj(h, w, out_dtype):
    seq, d = h.shape
    n = w.shape[1]
    tm = TM_PROJ
    return pl.pallas_call(
        _proj_kernel,
        grid=(seq // tm,),
        in_specs=[pl.BlockSpec((tm, d), lambda i: (i, 0)), _const_spec(w.shape)],
        out_specs=pl.BlockSpec((tm, n), lambda i: (i, 0)),
        out_shape=jax.ShapeDtypeStruct((seq, n), out_dtype),
        compiler_params=pltpu.CompilerParams(dimension_semantics=("parallel",)),
        name="proj",
    )(h, w)


def _diffattn_kernel(q_ref, k_ref, v_ref, bias_ref, bstat_ref, lam_ref, g_ref, o_ref,
                     qs_ref, vt_ref, kmax_ref, r_ref, m_ref, acc_ref, *s_refs, out_scale):
    blk = BLK_A
    nq = 2 * blk
    sub = SUBLANES
    dv = DV_A
    npart = NPART_A
    sa_ref, sb_ref = s_refs[:2 * npart], s_refs[2 * npart:4 * npart]
    pa_ref, pb_ref = s_refs[4 * npart:5 * npart], s_refs[5 * npart:6 * npart]
    i = pl.program_id(1)
    lane = lax.broadcasted_iota(jnp.int32, (blk, LANES), 1)
    same_subhead = (lax.broadcasted_iota(jnp.int32, (LANES, LANES), 0) // DK_A
                    == lax.broadcasted_iota(jnp.int32, (LANES, LANES), 1) // DK_A).astype(BF16)

    @pl.when(i == 0)
    def _():
        kmax_ref[...] = jnp.zeros(kmax_ref.shape, F32)

        def tr(b, carry):
            r0 = pl.multiple_of(b * blk, blk)
            vt_ref[0:dv, pl.ds(r0, blk)] = v_ref[pl.ds(r0, blk), :].astype(F32).T.astype(BF16)
            vt_ref[dv:dv + ONES_A, pl.ds(r0, blk)] = jnp.ones((ONES_A, blk), BF16)
            kf = k_ref[pl.ds(r0, blk), :].astype(F32)
            kn2 = _dot((kf * kf).astype(BF16), same_subhead)
            kmax_ref[...] = jnp.maximum(kmax_ref[...], jnp.max(kn2.reshape(blk // sub, sub, LANES), axis=0))
            return carry
        lax.fori_loop(0, v_ref.shape[0] // blk, tr, 0)
        kmax_ref[...] = jnp.broadcast_to(jnp.max(kmax_ref[...], axis=0, keepdims=True), kmax_ref.shape)

    q = (q_ref[...].astype(F32) * (DK_A ** -0.5 * LOG2E)).astype(BF16)
    qf = q.astype(F32)
    qs_ref[:, 0:blk] = jnp.where(lane < DK_A, qf, 0.0).T.astype(BF16)
    qs_ref[:, blk:nq] = jnp.where(lane >= DK_A, qf, 0.0).T.astype(BF16)
    acc_ref[...] = jnp.zeros(acc_ref.shape, F32)

    qt = qs_ref[...].astype(F32)
    qn2 = jnp.sum((qt * qt).reshape(LANES // sub, sub, nq), axis=0)
    qn2 = jnp.broadcast_to(jnp.sum(qn2, axis=0, keepdims=True), (sub, nq))
    kmax2 = jnp.concatenate([jnp.broadcast_to(kmax_ref[:, m * DK_A:m * DK_A + 1], (sub, blk)) for m in range(2)],
                            axis=1)
    bound = jnp.sqrt(qn2 * kmax2) * 1.03
    bias_max, bias_span = bstat_ref[0, 0:1, 0:1], bstat_ref[0, 1:2, 0:1]
    r_ref[...] = bound + bias_max
    single_pass = jnp.max(2.0 * bound + bias_span) < SINGLE_PASS_LOG2_RANGE

    @pl.when(single_pass)
    def _():
        _diffattn_fixed_shift(i, k_ref, bias_ref, qs_ref, vt_ref, r_ref, m_ref, acc_ref, pa_ref, pb_ref)

    @pl.when(jnp.logical_not(single_pass))
    def _():
        _diffattn_online(i, k_ref, bias_ref, qs_ref, vt_ref, m_ref, acc_ref, sa_ref, sb_ref)

    ot = acc_ref[0:dv, 0:nq] / acc_ref[dv:dv + 1, 0:nq]
    o = ot[:, 0:blk].T - lam_ref[...] * ot[:, blk:nq].T
    o = o * lax.rsqrt(jnp.mean(o * o, axis=-1, keepdims=True) + EPS) * g_ref[...]
    o_ref[...] = (o * out_scale).astype(o_ref.dtype)


def _diffattn_fixed_shift(i, k_ref, bias_ref, qs_ref, vt_ref, shift_ref, l_ref, acc_ref, pa_ref, pb_ref):
    blk = BLK_A
    nq = 2 * blk
    sub = SUBLANES
    npart = len(pa_ref)
    wq = nq // npart
    l_ref[...] = jnp.zeros(l_ref.shape, F32)

    def probs(b, p_ref, bias):
        k = k_ref[pl.ds(pl.multiple_of(b * blk, blk), blk), :]
        for part in range(npart):
            cols = slice(part * wq, (part + 1) * wq)
            s = _dot(k, qs_ref[:, cols])
            if bias is not None:
                q0 = (part * wq) % blk
                s = s + bias[:, q0:q0 + wq]
            p = jnp.exp2(s.reshape(blk // sub, sub, wq) - shift_ref[:, cols][None])
            l_ref[:, cols] += jnp.sum(p, axis=0)
            p_ref[part][:, 0:wq] = p.reshape(blk, wq).astype(BF16)

    def accumulate(b, p_ref):
        vt = vt_ref[0:DV_A, pl.ds(pl.multiple_of(b * blk, blk), blk)]
        for part in range(npart):
            cols = slice(part * wq, (part + 1) * wq)
            acc_ref[0:DV_A, cols] += _dot(vt, p_ref[part][:, 0:wq])

    @pl.when(i == 0)
    def _():
        probs(0, pa_ref, bias_ref[0, 1])
        accumulate(0, pa_ref)

    @pl.when(i > 0)
    def _():
        nfar = i - 1
        probs(i, pa_ref, bias_ref[0, 1])
        probs(i - 1, pb_ref, bias_ref[0, 0])
        accumulate(i, pa_ref)

        def pair(t):
            probs(2 * t, pa_ref, None)
            accumulate(jnp.where(t == 0, i - 1, 2 * t - 1), pb_ref)
            probs(2 * t + 1, pb_ref, None)
            accumulate(2 * t, pa_ref)

        def two_pairs(u, carry):
            pair(2 * u)
            pair(2 * u + 1)
            return carry

        npairs = nfar // 2
        lax.fori_loop(0, npairs // 2, two_pairs, 0)

        @pl.when(lax.rem(npairs, 2) == 1)
        def _():
            pair(npairs - 1)
        in_pb = jnp.where(npairs == 0, i - 1, 2 * npairs - 1)

        @pl.when(lax.rem(nfar, 2) == 1)
        def _():
            probs(nfar - 1, pa_ref, None)
            accumulate(in_pb, pb_ref)
            accumulate(nfar - 1, pa_ref)

        @pl.when(lax.rem(nfar, 2) == 0)
        def _():
            accumulate(in_pb, pb_ref)

    acc_ref[DV_A:DV_A + sub, 0:nq] = jnp.broadcast_to(jnp.sum(l_ref[...], axis=0, keepdims=True), (sub, nq))


def _diffattn_online(i, k_ref, bias_ref, qs_ref, vt_ref, m_ref, acc_ref, sa_ref, sb_ref):
    blk = BLK_A
    nq = 2 * blk
    sub = SUBLANES
    npart = len(sa_ref) // 2
    wq = nq // npart
    m_ref[...] = jnp.full(m_ref.shape, NEG_INF, F32)

    def scores(b, s_ref):
        k = k_ref[pl.ds(pl.multiple_of(b * blk, blk), blk), :]
        for part in range(npart):
            s = _dot(k, qs_ref[:, part * wq:(part + 1) * wq])
            s_ref[part][:, 0:wq] = s
            s_ref[npart + part][...] = jnp.max(s.reshape(blk // sub, sub, wq), axis=0)

    def softmax_pv(b, s_ref, bias):
        vt = vt_ref[:, pl.ds(pl.multiple_of(b * blk, blk), blk)]
        for part in range(npart):
            cols = slice(part * wq, (part + 1) * wq)
            s = s_ref[part][:, 0:wq]
            if bias is not None:
                q0 = (part * wq) % blk
                s = s + bias[:, q0:q0 + wq]
            s = s.reshape(blk // sub, sub, wq)
            m_prev = m_ref[:, cols]
            smax = jnp.max(s, axis=0) if bias is not None else s_ref[npart + part][...]
            m_cur = jnp.max(smax, axis=0, keepdims=True)
            m_new = jnp.maximum(m_prev, m_cur)
            alpha = jnp.exp2(m_prev - m_new)
            p = jnp.exp2(s - m_new[None])
            pv = _dot(vt, p.reshape(blk, wq).astype(BF16))
            acc_ref[:, cols] = acc_ref[:, cols] * alpha[0:1] + pv
            m_ref[:, cols] = m_new

    nfar = jnp.maximum(i - 1, 0)
    odd = lax.rem(nfar, 2)

    @pl.when(i == 0)
    def _():
        scores(0, sb_ref)

    @pl.when(i > 0)
    def _():
        @pl.when(odd == 1)
        def _():
            scores(0, sb_ref)
            scores(1, sa_ref)
            softmax_pv(0, sb_ref, None)

        @pl.when(odd == 0)
        def _():
            scores(0, sa_ref)

        def pair(b):
            scores(b + 1, sb_ref)
            softmax_pv(b, sa_ref, None)
            scores(b + 2, sa_ref)
            softmax_pv(b + 1, sb_ref, None)

        def quad_body(t, carry):
            pair(odd + 4 * t)
            pair(odd + 4 * t + 2)
            return carry

        npairs = nfar // 2
        lax.fori_loop(0, npairs // 2, quad_body, 0)

        @pl.when(lax.rem(npairs, 2) == 1)
        def _():
            pair(odd + 2 * (npairs - 1))
        scores(i, sb_ref)
        softmax_pv(i - 1, sa_ref, bias_ref[0, 0])

    softmax_pv(i, sb_ref, bias_ref[0, 1])


_TOEPLITZ_ROWS = 256
_TOEPLITZ_N = 2048


def _toeplitz_kernel(v_ref, o_ref, *, keep):
    rows, cols = o_ref.shape[2:]
    x = jnp.broadcast_to(v_ref[0, 0], (rows, v_ref.shape[-1]))
    tile = pltpu.roll(x, 0, 1, stride=1, stride_axis=0)[:, :cols]
    r = lax.broadcasted_iota(jnp.int32, (rows, cols), 0) + pl.program_id(1) * rows
    c = lax.broadcasted_iota(jnp.int32, (rows, cols), 1)
    for variant in range(o_ref.shape[0]):
        o_ref[variant, 0] = jnp.where(keep(r, c, variant), tile, NEG_INF)


def _toeplitz_tiles(fn, keep, heads, rows, cols, variants=1):
    n, rb = _TOEPLITZ_N, _TOEPLITZ_ROWS
    assert rows % rb == 0 and rows <= n // 2 and cols <= n // 2
    idx = jnp.arange(n, dtype=jnp.int32)
    vec = fn(jnp.where(idx < n // 2, idx, idx - n)).astype(F32)
    vecs = jnp.stack([jnp.roll(vec, k * rb, axis=1) for k in range(rows // rb)], axis=1)
    return pl.pallas_call(
        functools.partial(_toeplitz_kernel, keep=keep),
        grid=(heads, rows // rb),
        in_specs=[pl.BlockSpec((1, 1, 1, n), lambda h, k: (h, k, 0, 0))],
        out_specs=pl.BlockSpec((variants, 1, rb, cols), lambda h, k: (0, h, k, 0)),
        out_shape=jax.ShapeDtypeStruct((variants, heads, rows, cols), F32),
        name="toeplitz_tiles",
    )(vecs.reshape(heads, rows // rb, 1, n))


def _t5_bucket(rel):
    nb = NUM_BUCKETS // 2
    max_exact = nb // 2
    bucket = jnp.where(rel > 0, nb, 0)
    n = jnp.abs(rel)
    nf = jnp.maximum(n, 1).astype(F32)
    large = max_exact + (jnp.log(nf / max_exact) / math.log(MAX_DISTANCE / max_exact)
                         * (nb - max_exact)).astype(jnp.int32)
    large = jnp.minimum(large, nb - 1)
    return bucket + jnp.where(n < max_exact, n, large)


def _diff_bias_tiles(t5_table):
    blk = BLK_A
    table = t5_table.astype(F32)
    far = table[_t5_bucket(jnp.full((), -(blk + 1), jnp.int32))]
    def visible(r, c, variant):
        return jnp.floor_divide(r - blk, CHUNK) <= jnp.floor_divide(c, CHUNK)

    tiles = _toeplitz_tiles(lambda x: ((table[_t5_bucket(-x - blk)] - far) * LOG2E).T, visible,
                            N_HEADS_A, 2 * blk, blk)
    return tiles.reshape(N_HEADS_A, 2, blk, blk)


def _diff_attention(proj, t5_table, lam, subln_g, lam_init):
    seq = proj.shape[0]
    blk = BLK_A
    bias = _diff_bias_tiles(t5_table)
    ha = N_HEADS_A
    finite = bias > 0.5 * NEG_INF
    bias_max = jnp.maximum(jnp.max(jnp.where(finite, bias, NEG_INF), axis=(1, 2, 3)), 0.0)
    bias_min = jnp.minimum(jnp.min(jnp.where(finite, bias, -NEG_INF), axis=(1, 2, 3)), 0.0)
    bstat = jnp.broadcast_to(jnp.stack([bias_max, bias_max - bias_min], axis=1)[:, :, None], (ha, 2, LANES))
    kern = functools.partial(_diffattn_kernel, out_scale=1.0 - lam_init)
    return pl.pallas_call(
        kern,
        grid=(ha, seq // blk),
        in_specs=[
            pl.BlockSpec((blk, DV_A), lambda h, i: (i, h)),
            pl.BlockSpec((seq, DV_A), lambda h, i: (0, ha + h)),
            pl.BlockSpec((seq, DV_A), lambda h, i: (0, 2 * ha + h)),
            pl.BlockSpec((1, 2, blk, blk), lambda h, i: (h, 0, 0, 0)),
            pl.BlockSpec((1, 2, LANES), lambda h, i: (h, 0, 0)),
            pl.BlockSpec((1, DV_A), lambda h, i: (0, 0)),
            pl.BlockSpec((1, DV_A), lambda h, i: (0, 0)),
        ],
        out_specs=pl.BlockSpec((blk, DV_A), lambda h, i: (i, h)),
        out_shape=jax.ShapeDtypeStruct((seq, ha * DV_A), BF16),
        scratch_shapes=[
            pltpu.VMEM((DV_A, 2 * blk), BF16),
            pltpu.VMEM((DV_A + ONES_A, seq), BF16),
            pltpu.VMEM((SUBLANES, LANES), F32),
            pltpu.VMEM((SUBLANES, 2 * blk), F32),
            pltpu.VMEM((SUBLANES, 2 * blk), F32),
            pltpu.VMEM((DV_A + ONES_A, 2 * blk), F32),
        ] + 2 * ([pltpu.VMEM((blk, 2 * blk // NPART_A + SCORE_PAD), F32)] * NPART_A
                 + [pltpu.VMEM((SUBLANES, 2 * blk // NPART_A), F32)] * NPART_A)
        + 2 * [pltpu.VMEM((blk, 2 * blk // NPART_A + SCORE_PAD), BF16)] * NPART_A,
        compiler_params=pltpu.CompilerParams(dimension_semantics=("parallel", "arbitrary")),
        name="diff_attention",
    )(proj, proj, proj, bias, bstat, jnp.full((1, DV_A), lam, F32), subln_g.reshape(1, DV_A).astype(F32))


def _band_kernel(q_ref, kp_ref, kc_ref, vp_ref, vc_ref, *refs):
    qw, band = QW_B, BAND_B
    nbias = band // qw + 1
    bias_refs, o_ref, s_refs = refs[:nbias], refs[nbias], refs[nbias + 1:]
    nk = band + qw
    sub = SUBLANES
    q = q_ref[...].astype(F32) * (DH_B ** -0.5 * LOG2E)
    lane = lax.broadcasted_iota(jnp.int32, q.shape, 1)
    qh = (jnp.where(lane < DH_B, q, 0.0).T.astype(BF16), jnp.where(lane >= DH_B, q, 0.0).T.astype(BF16))
    k_all = jnp.concatenate([kp_ref[...], kc_ref[...]], axis=0)
    vt_all = jnp.concatenate([vp_ref[...], vc_ref[...]], axis=0).astype(F32).T.astype(BF16)
    vt_all = jnp.concatenate([vt_all, jnp.ones((ONES_A, vt_all.shape[1]), BF16)], axis=0)
    ngroups = len(s_refs)
    for g in range(ngroups):
        k0 = g * qw
        qs = jnp.concatenate([qh[0][:, k0:k0 + qw], qh[1][:, k0:k0 + qw]], axis=1)
        s_refs[g][:, 0:2 * qw] = _dot(k_all[k0:k0 + nk], qs)
    for g in range(ngroups):
        k0 = g * qw
        bias_ref = bias_refs[min(g, nbias - 1)]
        bias = jnp.concatenate([bias_ref[0, 0], bias_ref[0, 1]], axis=1)
        s = (s_refs[g][:, 0:2 * qw] + bias).reshape(nk // sub, sub, 2 * qw)
        m = jnp.max(jnp.max(s, axis=0), axis=0, keepdims=True)
        p = jnp.exp2(s - m[None])
        pv = _dot(vt_all[:, k0:k0 + nk], p.reshape(nk, 2 * qw).astype(BF16))
        ot = pv[0:2 * DH_B] / pv[2 * DH_B:2 * DH_B + 1]
        o = jnp.concatenate([ot[0:DH_B, 0:qw], ot[DH_B:2 * DH_B, qw:2 * qw]], axis=0)
        o_ref[k0:k0 + qw, :] = o.T.astype(o_ref.dtype)


def _band_bias_tiles(rel_bias):
    band = BAND_B

    def valid(r, c, variant):
        qchunk = jnp.floor_divide(c, CHUNK)
        kchunk = jnp.floor_divide(r - band, CHUNK)
        missing = jnp.where(variant == 0, 0, band - (variant - 1) * QW_B)
        return (kchunk <= qchunk) & (kchunk >= qchunk - LEFT_CHUNKS) & (r >= missing)

    return _toeplitz_tiles(
        lambda x: rel_bias.astype(F32)[:, jnp.clip(-x - band, -REL_CLIP, REL_CLIP) + REL_CLIP] * LOG2E, valid,
        N_HEADS_B, band + QW_B, QW_B, variants=1 + band // QW_B)


def _band_attention(proj, rel_bias):
    seq = proj.shape[0]
    blk, band, qw = BLK_B, BAND_B, QW_B
    bias = _band_bias_tiles(rel_bias)
    npair = N_HEADS_B // 2
    qc0 = 3 * N_HEADS_A
    per = blk // band
    prev = lambda c0: (lambda hp, i: (jnp.maximum(i * per - 1, 0), c0 + hp))
    cur = lambda c0: (lambda hp, i: (i, c0 + hp))
    return pl.pallas_call(
        _band_kernel,
        grid=(npair, seq // blk),
        in_specs=[
            pl.BlockSpec((blk, LANES), cur(qc0)),
            pl.BlockSpec((band, LANES), prev(qc0 + npair)),
            pl.BlockSpec((blk, LANES), cur(qc0 + npair)),
            pl.BlockSpec((band, LANES), prev(qc0 + 2 * npair)),
            pl.BlockSpec((blk, LANES), cur(qc0 + 2 * npair)),
        ] + [
            pl.BlockSpec((1, 2, band + qw, qw), (lambda hp, i, t=t: (jnp.where(i == 0, 1 + t, 0), hp, 0, 0)))
            for t in range(band // qw)
        ] + [
            pl.BlockSpec((1, 2, band + qw, qw), lambda hp, i: (0, hp, 0, 0)),
        ],
        out_specs=pl.BlockSpec((blk, LANES), lambda hp, i: (i, hp)),
        out_shape=jax.ShapeDtypeStruct((seq, N_HEADS_B * DH_B), BF16),
        scratch_shapes=[pltpu.VMEM((band + qw, 2 * qw + SCORE_PAD), F32)] * (blk // qw),
        compiler_params=pltpu.CompilerParams(dimension_semantics=("parallel", "arbitrary")),
        name="band_attention",
    )(proj, proj, proj, proj, proj, *([bias] * bias.shape[0]))


def _retention_kernel(qk_ref, v_ref, gate_ref, cos_ref, sin_ref, qdec_ref, kdec_ref, dmat_ref,
                      sdec_ref, o_ref, state_ref):
    @pl.when(pl.program_id(0) == 0)
    def _():
        state_ref[...] = jnp.zeros(state_ref.shape, F32)

    cos = cos_ref[...]
    sin = sin_ref[...]
    lane = lax.broadcasted_iota(jnp.int32, cos.shape, 1)
    first_half = (lane % DQK_C) < (DQK_C // 2)
    qk = qk_ref[...]
    parts = []
    for j in range(qk.shape[1] // LANES):
        t = qk[:, j * LANES:(j + 1) * LANES]
        partner = jnp.where(first_half, pltpu.roll(t, LANES - DQK_C // 2, 1), pltpu.roll(t, DQK_C // 2, 1))
        parts.append(t * cos + partner * sin)
    wq = N_HEADS_C * DQK_C
    q = jnp.concatenate(parts[:wq // LANES], axis=1)
    k = jnp.concatenate(parts[wq // LANES:], axis=1) * (DQK_C ** -0.5)
    qd = (q * qdec_ref[...]).astype(BF16)
    kd = (k * kdec_ref[...]).astype(BF16)
    qb = q.astype(BF16)
    kb = k.astype(BF16)
    vb = v_ref[...].astype(BF16)
    gate = gate_ref[...]
    outs = []
    for h in range(N_HEADS_C):
        qs = slice(h * DQK_C, (h + 1) * DQK_C)
        vs = slice(h * DV_C, (h + 1) * DV_C)
        scores = _dot_nt(qb[:, qs], kb[:, qs]) * dmat_ref[h]
        state = state_ref[h]
        r = _dot(scores.astype(BF16), vb[:, vs]) + _dot(qd[:, qs], state.astype(BF16))
        state_ref[h] = state * sdec_ref[h] + _dot_tn(kd[:, qs], vb[:, vs])
        r = r * lax.rsqrt(jnp.mean(r * r, axis=-1, keepdims=True) + EPS)
        g = gate[:, vs]
        outs.append(r * (g * jax.nn.sigmoid(g)))
    o_ref[...] = jnp.concatenate(outs, axis=1).astype(o_ref.dtype)


def _retention_tables(seq):
    t = BLK_C
    half = DQK_C // 2
    inv_freq = 1.0 / (ROPE_BASE ** (jnp.arange(0, DQK_C, 2, dtype=F32) / DQK_C))
    ang = jnp.arange(seq, dtype=F32)[:, None] * inv_freq[None, :]
    reps = LANES // half
    cos = jnp.tile(jnp.cos(ang), (1, reps))
    sign = jnp.where((jnp.arange(LANES) % DQK_C) < half, -1.0, 1.0).astype(F32)
    sin = jnp.tile(jnp.sin(ang), (1, reps)) * sign[None, :]
    log_g = jnp.log(1.0 - jnp.power(2.0, -5.0 - jnp.arange(N_HEADS_C, dtype=F32)))
    pos = jnp.arange(t, dtype=F32)
    diff = pos[:, None] - pos[None, :]
    same_or_past = (jnp.arange(t)[None, :] // CHUNK) <= (jnp.arange(t)[:, None] // CHUNK)
    dmat = jnp.where(same_or_past[None], jnp.exp(log_g[:, None, None] * jnp.abs(diff)[None]), 0.0)
    qdec = jnp.repeat(jnp.exp(log_g[None, :] * (pos[:, None] + 1.0)), DQK_C, axis=1)
    kdec = jnp.repeat(jnp.exp(log_g[None, :] * (t - 1.0 - pos[:, None])), DQK_C, axis=1)
    sdec = jnp.broadcast_to(jnp.exp(log_g * t)[:, None, None], (N_HEADS_C, 1, DV_C))
    return cos, sin, qdec, kdec, dmat, sdec


def _retention(proj):
    seq = proj.shape[0]
    t = BLK_C
    cos, sin, qdec, kdec, dmat, sdec = _retention_tables(seq)
    wv = N_HEADS_C * DV_C
    return pl.pallas_call(
        _retention_kernel,
        grid=(seq // t,),
        in_specs=[
            pl.BlockSpec((t, wv), lambda i: (i, 0)),
            pl.BlockSpec((t, wv), lambda i: (i, 1)),
            pl.BlockSpec((t, wv), lambda i: (i, 2)),
            pl.BlockSpec((t, LANES), lambda i: (i, 0)),
            pl.BlockSpec((t, LANES), lambda i: (i, 0)),
            pl.BlockSpec((t, N_HEADS_C * DQK_C), lambda i: (0, 0)),
            pl.BlockSpec((t, N_HEADS_C * DQK_C), lambda i: (0, 0)),
            pl.BlockSpec((N_HEADS_C, t, t), lambda i: (0, 0, 0)),
            pl.BlockSpec((N_HEADS_C, 1, DV_C), lambda i: (0, 0, 0)),
        ],
        out_specs=pl.BlockSpec((t, wv), lambda i: (i, 0)),
        out_shape=jax.ShapeDtypeStruct((seq, wv), BF16),
        scratch_shapes=[pltpu.VMEM((N_HEADS_C, DQK_C, DV_C), F32)],
        compiler_params=pltpu.CompilerParams(dimension_semantics=("arbitrary",)),
        name="retention",
    )(proj, proj, proj, cos, sin, qdec, kdec, dmat, sdec)


def _s5_kernel(*refs):
    ncb = S5_CH // LANES
    u_refs = refs[:ncb]
    (mt_ref, bt_ref, ctr_ref, cti_ref, are_ref, aim_ref, y_ref,
     ut_ref, yt_ref, ys_ref, vr_ref, vi_ref, spr_ref, spi_ref, carry_ref) = refs[ncb:]
    tc = S5_TC
    gp = S5_GROUP
    n = S5_STATE
    ng = S5_GROUPS

    @pl.when(pl.program_id(0) == 0)
    def _():
        carry_ref[...] = jnp.zeros(carry_ref.shape, F32)

    for s in range(S5_T):
        for k in range(ncb):
            ut_ref[s, k * LANES:(k + 1) * LANES, :] = u_refs[k][pl.ds(s, tc, stride=S5_T), :].T

    unroll = 4

    def intra(it, carry):
        for k in range(unroll):
            g = it * unroll + k
            r0 = pl.multiple_of(g * gp, gp)
            ug = ut_ref[:, pl.ds(r0, gp), :].reshape(S5_T * gp, tc).astype(BF16)
            yt_ref[:, pl.ds(r0, gp), :] = _dot(mt_ref[g], ug).reshape(S5_T, gp, tc)
            vt = _dot(bt_ref[g], ug)
            n0 = pl.multiple_of(g * n, n)
            vr_ref[pl.ds(n0, n), :] = vt[0:n]
            vi_ref[pl.ds(n0, n), :] = vt[n:2 * n]
        return carry

    lax.fori_loop(0, ng // unroll, intra, 0)

    sub = SUBLANES
    nv = tc // sub
    row = lax.broadcasted_iota(jnp.int32, (tc, LANES), 0)
    in_vreg = lax.rem(row, sub)

    def rows_of(v, r):
        return jnp.broadcast_to(v[r:r + 1], (tc, LANES))

    for j in range(ng * n // LANES):
        cols = slice(j * LANES, (j + 1) * LANES)
        pwr, pwi = are_ref[:, cols], aim_ref[:, cols]
        xr = vr_ref[cols, :].T
        xi = vi_ref[cols, :].T
        for d in (1, 2, 4):
            keep = in_vreg >= d
            sr = jnp.where(keep, pltpu.roll(xr, d, 0), 0.0)
            si = jnp.where(keep, pltpu.roll(xi, d, 0), 0.0)
            fr, fi = rows_of(pwr, d - 1), rows_of(pwi, d - 1)
            xr, xi = xr + (fr * sr - fi * si), xi + (fr * si + fi * sr)
        cr, ci = carry_ref[0, :, cols], carry_ref[1, :, cols]
        cr0, ci0 = cr, ci
        outr, outi = [], []
        for v in range(nv):
            yr = xr[v * sub:(v + 1) * sub] + (pwr * cr - pwi * ci)
            yi = xi[v * sub:(v + 1) * sub] + (pwr * ci + pwi * cr)
            outr.append(yr)
            outi.append(yi)
            cr = jnp.broadcast_to(yr[sub - 1:sub], (sub, LANES))
            ci = jnp.broadcast_to(yi[sub - 1:sub], (sub, LANES))
        carry_ref[0, :, cols] = cr
        carry_ref[1, :, cols] = ci
        sr = jnp.concatenate(outr, axis=0)
        si = jnp.concatenate(outi, axis=0)
        first = row == 0
        spr_ref[j] = jnp.where(first, rows_of(cr0, 0), pltpu.roll(sr, 1, 0))
        spi_ref[j] = jnp.where(first, rows_of(ci0, 0), pltpu.roll(si, 1, 0))

    def cross(it, carry):
        for k in range(unroll):
            jp = it * unroll + k
            r0 = pl.multiple_of(jp * 2 * gp, 2 * gp)
            yc = (_dot_nt(ctr_ref[jp], spr_ref[jp].astype(BF16))
                  + _dot_nt(cti_ref[jp], spi_ref[jp].astype(BF16)))
            yt_ref[:, pl.ds(r0, 2 * gp), :] += yc.reshape(S5_T, 2 * gp, tc)
        return carry

    lax.fori_loop(0, ng // 2 // unroll, cross, 0)

    for s in range(S5_T):
        for k in range(ncb):
            ys_ref[k, pl.ds(s, tc, stride=S5_T), :] = yt_ref[s, k * LANES:(k + 1) * LANES, :].T
    for k in range(ncb):
        y_ref[:, k * LANES:(k + 1) * LANES] = ys_ref[k]


def _s5_matrices(lam_re, lam_im, log_step, b_re, b_im, c_re, c_im, d_skip):
    hi = lax.Precision.HIGHEST
    t, gp, n, ng = S5_T, S5_GROUP, S5_STATE, S5_GROUPS
    lam = lax.complex(lam_re.astype(F32), lam_im.astype(F32))
    step = jnp.exp(log_step.astype(F32))[:, None]
    ls = lam * step
    a_bar = jnp.exp(ls)
    b_bar = ((a_bar - 1.0) / lam)[..., None] * lax.complex(b_re.astype(F32), b_im.astype(F32))
    cm = lax.complex(c_re.astype(F32), c_im.astype(F32))

    def apow(k):
        kk = k.astype(F32).astype(jnp.complex64)
        return jnp.exp(ls.reshape((ng,) + (1,) * k.ndim + (n,)) * kk[None, ..., None])

    tt = jnp.arange(t)
    kmat = jnp.einsum('gpn,gln,gnq->glpq', cm, apow(tt), b_bar, precision=hi).real
    krev = jnp.transpose(kmat[:, ::-1], (0, 2, 1, 3)).reshape(ng, gp, t * gp)
    kpad = jnp.pad(krev, ((0, 0), (0, 0), (0, t * gp)))
    mt = jnp.concatenate([kpad[:, :, (t - 1 - to) * gp:(2 * t - 1 - to) * gp] for to in range(t)], axis=1)
    dvec = jnp.tile(d_skip.astype(F32).reshape(ng, 1, gp), (1, t, 1)).reshape(ng, t * gp)
    mt = mt + jnp.eye(t * gp, dtype=F32)[None] * dvec[:, :, None]
    z = jnp.swapaxes(apow(t - 1 - tt), 1, 2)[:, :, :, None] * b_bar[:, :, None, :]
    z = z.reshape(ng, n, t * gp)
    bt = jnp.concatenate([z.real, z.imag], axis=1)
    w = cm[:, None, :, :] * apow(tt + 1)[:, :, None, :]

    def pair_readout(x):
        x = x.reshape(ng // 2, 2, t, gp, n)
        first = jnp.pad(x[:, 0], ((0, 0), (0, 0), (0, 0), (0, n)))
        second = jnp.pad(x[:, 1], ((0, 0), (0, 0), (0, 0), (n, 0)))
        return jnp.stack([first, second], axis=2).reshape(ng // 2, t * 2 * gp, 2 * n).astype(BF16)

    ctr, cti = pair_readout(w.real), pair_readout(-w.imag)
    a_chunk = jnp.transpose(apow(t * (jnp.arange(SUBLANES) + 1)), (1, 0, 2)).reshape(SUBLANES, ng * n)
    return mt.astype(BF16), bt.astype(BF16), ctr, cti, a_chunk.real, a_chunk.imag


def _s5(proj, mats):
    seq, width = proj.shape
    t, tc, gp, n, ng = S5_T, S5_TC, S5_GROUP, S5_STATE, S5_GROUPS
    rows = t * tc
    ncb = S5_CH // LANES
    cb0 = (width - S5_CH) // LANES
    u_specs = [pl.BlockSpec((rows, LANES), (lambda i, k=k: (i, cb0 + k))) for k in range(ncb)]
    nsb = ng * n // LANES
    return pl.pallas_call(
        _s5_kernel,
        grid=(seq // rows,),
        in_specs=u_specs + [_const_spec(m.shape) for m in mats],
        out_specs=pl.BlockSpec((rows, S5_CH), lambda i: (i, 0)),
        out_shape=jax.ShapeDtypeStruct((seq, S5_CH), F32),
        scratch_shapes=[
            pltpu.VMEM((t, S5_CH, tc), F32),
            pltpu.VMEM((t, S5_CH, tc), F32),
            pltpu.VMEM((ncb, rows, LANES), F32),
            pltpu.VMEM((ng * n, tc), F32),
            pltpu.VMEM((ng * n, tc), F32),
            pltpu.VMEM((nsb, tc, LANES), F32),
            pltpu.VMEM((nsb, tc, LANES), F32),
            pltpu.VMEM((2, SUBLANES, ng * n), F32),
        ],
        compiler_params=pltpu.CompilerParams(dimension_semantics=("arbitrary",)),
        name="s5_scan",
    )(*([proj] * ncb), *mats)


def _mix_ffn_kernel(*refs, glu, final):
    (x_ref, a_ref, b_ref, wo_ref, g1_ref), refs = refs[:5], refs[5:]
    if glu:
        gw_ref, refs = refs[0], refs[1:]
    (g_ref, sc_ref, sh_ref, gate_ref, win_ref, cw_ref, cb_ref, wout_ref), refs = refs[:8], refs[8:]
    if final:
        fg_ref, o_ref, h_ref, act_ref, gbuf_ref, carry_ref = refs
    else:
        ng_ref, nsc_ref, nsh_ref, o_ref, hn_ref, h_ref, act_ref, gbuf_ref, carry_ref = refs
    tm = x_ref.shape[0]
    halo = gbuf_ref.shape[0] - tm

    @pl.when(pl.program_id(0) == 0)
    def _():
        carry_ref[...] = jnp.zeros(carry_ref.shape, F32)

    if glu:
        y = jax.nn.gelu(b_ref[...]).astype(BF16)
        gg = _dot(y, gw_ref[...])
        half = gg.shape[1] // 2
        b = (gg[:, :half] * jax.nn.sigmoid(gg[:, half:])).astype(BF16)
    else:
        b = b_ref[...]
    cat = jnp.concatenate([a_ref[...], b], axis=1)
    x = x_ref[...] + g1_ref[...] * _dot(cat, wo_ref[...])
    h_ref[...] = _mod_rmsnorm(x, g_ref[...], sc_ref[...], sh_ref[...]).astype(BF16)
    for f in range(D_FF // TF_FFN):
        cs = slice(f * TF_FFN, (f + 1) * TF_FFN)
        gs = slice(D_FF + f * TF_FFN, D_FF + (f + 1) * TF_FFN)
        h = h_ref[...]
        val = _dot(h, win_ref[:, cs])
        gate = _dot(h, win_ref[:, gs])
        gbuf_ref[0:halo, :] = carry_ref[:, cs]
        gbuf_ref[halo:halo + tm, :] = gate
        carry_ref[:, cs] = gate[tm - halo:tm, :]
        conv = (gate * cw_ref[2:3, cs] + gbuf_ref[halo - 1:halo - 1 + tm, :] * cw_ref[1:2, cs]
                + gbuf_ref[halo - 2:halo - 2 + tm, :] * cw_ref[0:1, cs] + cb_ref[:, cs])
        act_ref[:, cs] = (jax.nn.gelu(conv) * val).astype(BF16)
    xn = x + gate_ref[...] * _dot(act_ref[...], wout_ref[...])
    if final:
        xn = xn * lax.rsqrt(jnp.mean(xn * xn, axis=-1, keepdims=True) + EPS) * fg_ref[...]
    else:
        hn_ref[...] = _mod_rmsnorm(xn, ng_ref[...], nsc_ref[...], nsh_ref[...]).astype(BF16)
    o_ref[...] = xn


def _layer_spec(shape, layer):
    idx = (layer,) + (0,) * (len(shape) - 1)
    return pl.BlockSpec((None,) + tuple(shape[1:]), lambda *_: idx, pipeline_mode=pl.Buffered(1))


def _mix_ffn(x, a, b, wo, gate1, glu_w, g, scale, shift, gate2, w_in, conv_w, conv_b, w_out, tail, layer):
    seq, d = x.shape
    final = len(tail) == 1
    tm = TM_FFN
    halo = SUBLANES
    row = pl.BlockSpec((1, d), lambda i: (0, 0))
    rows = lambda w: pl.BlockSpec((tm, w), lambda i: (i, 0))
    conv_b = conv_b.reshape(conv_b.shape[0], 1, D_FF)
    in_specs = [rows(d), rows(a.shape[1]), rows(b.shape[1]), _const_spec(wo.shape), row]
    args = [x, a, b, wo, gate1]
    if glu_w is not None:
        in_specs.append(_const_spec(glu_w.shape))
        args.append(glu_w)
    in_specs += [
        row, row, row, row,
        _layer_spec(w_in.shape, layer),
        _layer_spec(conv_w.shape, layer),
        _layer_spec(conv_b.shape, layer),
        _layer_spec(w_out.shape, layer),
    ] + [row] * len(tail)
    args += [g.reshape(1, d), scale, shift, gate2, w_in, conv_w, conv_b, w_out]
    args += [t.reshape(1, d) for t in tail]
    out_specs = [rows(d)] if final else [rows(d), rows(d)]
    out_shape = [jax.ShapeDtypeStruct((seq, d), F32)] + ([] if final else [jax.ShapeDtypeStruct((seq, d), BF16)])
    return pl.pallas_call(
        functools.partial(_mix_ffn_kernel, glu=glu_w is not None, final=final),
        grid=(seq // tm,),
        in_specs=in_specs,
        out_specs=out_specs,
        out_shape=out_shape,
        scratch_shapes=[
            pltpu.VMEM((tm, d), BF16),
            pltpu.VMEM((tm, D_FF), BF16),
            pltpu.VMEM((tm + halo, TF_FFN), F32),
            pltpu.VMEM((halo, D_FF), F32),
        ],
        compiler_params=pltpu.CompilerParams(dimension_semantics=("arbitrary",)),
        name="mix_ffn",
    )(*args)


def kernel(x, c, t5_table, mod_w, mod_b, norm1_g, norm2_g, ffn_w_in, ffn_conv_w, ffn_conv_b, ffn_w_out,
           ev_w_in, ev_w_out, diff_lambda, diff_subln_g, band_rel_bias,
           od_w_in, od_w_out, s5_lam_re, s5_lam_im, s5_log_step, s5_b_re, s5_b_im, s5_c_re, s5_c_im,
           s5_d, s5_glu_w, final_g):
    assert x.shape[0] == 1 and x.shape[2] == D_MODEL
    seq = x.shape[1]
    assert seq % TM_PROJ == 0 and seq % (S5_T * S5_TC) == 0
    d = D_MODEL
    xs = x[0]
    mod = _modulation(c, mod_w, mod_b)
    ffn_w_in_b = ffn_w_in.astype(BF16)
    ffn_w_out_b = ffn_w_out.astype(BF16)
    mods = [[mod[i, :, k * d:(k + 1) * d] for k in range(6)] for i in range(DEPTH)]
    h = None
    for i in range(DEPTH):
        sh1, sc1, g1, sh2, sc2, g2 = mods[i]
        w_in = (ev_w_in if i % 2 == 0 else od_w_in)[i // 2].astype(BF16)
        proj_dtype = BF16 if i % 2 == 0 else F32
        if h is None:
            proj = _normproj(xs, norm1_g[i], sc1, sh1, w_in, proj_dtype)
        else:
            proj = _proj(h, w_in, proj_dtype)
        if i % 2 == 0:
            e = i // 2
            lam_init = 0.8 - 0.6 * math.exp(-0.3 * i)
            lp = diff_lambda[e].astype(F32)
            lam = jnp.exp(jnp.sum(lp[0] * lp[1])) - jnp.exp(jnp.sum(lp[2] * lp[3])) + lam_init
            mix_a = _diff_attention(proj, t5_table, lam, diff_subln_g[e], lam_init)
            mix_b = _band_attention(proj, band_rel_bias[e])
            wo, glu_w = ev_w_out[e].astype(BF16), None
        else:
            o = i // 2
            mix_a = _retention(proj)
            mats = _s5_matrices(s5_lam_re[o], s5_lam_im[o], s5_log_step[o], s5_b_re[o], s5_b_im[o],
                                s5_c_re[o], s5_c_im[o], s5_d[o])
            mix_b = _s5(proj, mats)
            wo, glu_w = od_w_out[o].astype(BF16), s5_glu_w[o].astype(BF16)
        if i == DEPTH - 1:
            tail = (final_g,)
        else:
            nsh1, nsc1 = mods[i + 1][0], mods[i + 1][1]
            tail = (norm1_g[i + 1], nsc1, nsh1)
        out = _mix_ffn(xs, mix_a, mix_b, wo, g1, glu_w, norm2_g[i], sc2, sh2, g2,
                       ffn_w_in_b, ffn_conv_w, ffn_conv_b, ffn_w_out_b, tail, layer=i)
        if i == DEPTH - 1:
            xs = out[0]
        else:
            xs, h = out
    return xs[None]
```

```python
import functools
import math

import jax
import jax.numpy as jnp
from jax import lax
from jax.experimental import pallas as pl
from jax.experimental.pallas import tpu as pltpu

F32 = jnp.float32
BF16 = jnp.bfloat16

D_MODEL = 1024
DEPTH = 2
CHUNK = 64
GROUP_WIDTH = D_MODEL // 2
DK_A = 64
DV_A = 2 * DK_A
N_HEADS_A = GROUP_WIDTH // DV_A
DH_B = 64
N_HEADS_B = GROUP_WIDTH // DH_B
LEFT_CHUNKS = 8
REL_CLIP = 2 * CHUNK
NUM_BUCKETS = 32
MAX_DISTANCE = 128
DV_C = 128
DQK_C = DV_C // 2
N_HEADS_C = GROUP_WIDTH // DV_C
ROPE_BASE = 10000.0
S5_CH = GROUP_WIDTH
S5_GROUP = 16
S5_GROUPS = S5_CH // S5_GROUP
S5_STATE = 64
D_FF = ((8 * D_MODEL // 3 + 255) // 256) * 256
CONV_W = 3
EVEN_IN = 3 * N_HEADS_A * DV_A + 3 * N_HEADS_B * DH_B
ODD_IN = 2 * N_HEADS_C * DQK_C + 2 * N_HEADS_C * DV_C + S5_CH
EPS = 1e-6
NEG_INF = -1e30
LOG2E = math.log2(math.e)

LANES = 128
SUBLANES = 8
MXU_DIM = 256

TM_PROJ = 1024
TN_PROJ = 1024
TN_MOD = 1536
TM_FFN = 512
TF_FFN = MXU_DIM
BLK_A = 512
NPART_A = 2
ONES_A = 16
SINGLE_PASS_LOG2_RANGE = 96.0
SCORE_PAD = LANES
BLK_B = 1024
BAND_B = LEFT_CHUNKS * CHUNK
QW_B = 4 * CHUNK
BLK_C = 512
S5_T = 16
S5_TC = LANES

assert BLK_B % BAND_B == 0 and BLK_B % QW_B == 0 and BAND_B % QW_B == 0
assert BLK_A >= MAX_DISTANCE, "far key blocks must sit in the saturated T5 bucket"
assert DV_A == LANES and 2 * DK_A == LANES and 2 * DH_B == LANES, "attention heads are read as 128-lane column blocks"


def _dot(a, b):
    return jnp.dot(a, b, preferred_element_type=F32)


def _dot_nt(a, b):
    return lax.dot_general(a, b, (((1,), (1,)), ((), ())), preferred_element_type=F32)


def _dot_tn(a, b):
    return lax.dot_general(a, b, (((0,), (0,)), ((), ())), preferred_element_type=F32)


def _const_spec(shape):
    zeros = (0,) * len(shape)
    return pl.BlockSpec(shape, lambda *_: zeros, pipeline_mode=pl.Buffered(1))


def _mod_rmsnorm(x, g, scale, shift):
    y = x * lax.rsqrt(jnp.mean(x * x, axis=-1, keepdims=True) + EPS)
    y = y * g
    return y * (1.0 + scale) + shift


def _mod_kernel(c_ref, w_ref, b_ref, o_ref):
    c = c_ref[...]
    cond = c * jax.nn.sigmoid(c)
    o_ref[0] = jnp.sum(cond * w_ref[0], axis=0, keepdims=True) + b_ref[0]


def _modulation(c, mod_w, mod_b):
    depth, d, n = mod_w.shape
    tn = TN_MOD
    return pl.pallas_call(
        _mod_kernel,
        grid=(depth, n // tn),
        in_specs=[
            pl.BlockSpec((d, 1), lambda i, j: (0, 0)),
            pl.BlockSpec((1, d, tn), lambda i, j: (i, 0, j)),
            pl.BlockSpec((1, 1, tn), lambda i, j: (i, 0, j)),
        ],
        out_specs=pl.BlockSpec((1, 1, tn), lambda i, j: (i, 0, j)),
        out_shape=jax.ShapeDtypeStruct((depth, 1, n), F32),
        name="modulation",
    )(c.reshape(d, 1), mod_w, mod_b.reshape(depth, 1, n))


def _normproj_kernel(x_ref, g_ref, sc_ref, sh_ref, w_ref, o_ref):
    tm, n = o_ref.shape
    half = tm // 2
    for r in range(2):
        rows = slice(r * half, (r + 1) * half)
        h = _mod_rmsnorm(x_ref[rows, :], g_ref[...], sc_ref[...], sh_ref[...]).astype(BF16)
        for j in range(n // TN_PROJ):
            cols = slice(j * TN_PROJ, (j + 1) * TN_PROJ)
            o_ref[rows, cols] = _dot(h, w_ref[:, cols]).astype(o_ref.dtype)


def _normproj(x, g, scale, shift, w, out_dtype):
    seq, d = x.shape
    n = w.shape[1]
    tm = TM_PROJ
    row = pl.BlockSpec((1, d), lambda i: (0, 0))
    return pl.pallas_call(
        _normproj_kernel,
        grid=(seq // tm,),
        in_specs=[pl.BlockSpec((tm, d), lambda i: (i, 0)), row, row, row, _const_spec(w.shape)],
        out_specs=pl.BlockSpec((tm, n), lambda i: (i, 0)),
        out_shape=jax.ShapeDtypeStruct((seq, n), out_dtype),
        compiler_params=pltpu.CompilerParams(dimension_semantics=("parallel",)),
        name="normproj",
    )(x, g.reshape(1, d), scale, shift, w)


def _proj_kernel(h_ref, w_ref, o_ref):
    for j in range(o_ref.shape[1] // TN_PROJ):
        cols = slice(j * TN_PROJ, (j + 1) * TN_PROJ)
        o_ref[:, cols] = _dot(h_ref[...], w_ref[:, cols]).astype(o_ref.dtype)


def _proj(h, w, out_dtype):
    seq, d = h.shape
    n = w.shape[1]
    tm = TM_PROJ
    return pl.pallas_call(
        _proj_kernel,
        grid=(seq // tm,),
        in_specs=[pl.BlockSpec((tm, d), lambda i: (i, 0)), _const_spec(w.shape)],
        out_specs=pl.BlockSpec((tm, n), lambda i: (i, 0)),
        out_shape=jax.ShapeDtypeStruct((seq, n), out_dtype),
        compiler_params=pltpu.CompilerParams(dimension_semantics=("parallel",)),
        name="proj",
    )(h, w)


def _diffattn_kernel(q_ref, k_ref, v_ref, bias_ref, bstat_ref, lam_ref, g_ref, o_ref,
                     qs_ref, vt_ref, kmax_ref, r_ref, m_ref, acc_ref, *s_refs, out_scale):
    blk = BLK_A
    nq = 2 * blk
    sub = SUBLANES
    dv = DV_A
    npart = NPART_A
    sa_ref, sb_ref = s_refs[:2 * npart], s_refs[2 * npart:4 * npart]
    pa_ref, pb_ref = s_refs[4 * npart:5 * npart], s_refs[5 * npart:6 * npart]
    i = pl.program_id(1)
    lane = lax.broadcasted_iota(jnp.int32, (blk, LANES), 1)
    same_subhead = (lax.broadcasted_iota(jnp.int32, (LANES, LANES), 0) // DK_A
                    == lax.broadcasted_iota(jnp.int32, (LANES, LANES), 1) // DK_A).astype(BF16)

    @pl.when(i == 0)
    def _():
        kmax_ref[...] = jnp.zeros(kmax_ref.shape, F32)

        def tr(b, carry):
            r0 = pl.multiple_of(b * blk, blk)
            vt_ref[0:dv, pl.ds(r0, blk)] = v_ref[pl.ds(r0, blk), :].astype(F32).T.astype(BF16)
            vt_ref[dv:dv + ONES_A, pl.ds(r0, blk)] = jnp.ones((ONES_A, blk), BF16)
            kf = k_ref[pl.ds(r0, blk), :].astype(F32)
            kn2 = _dot((kf * kf).astype(BF16), same_subhead)
            kmax_ref[...] = jnp.maximum(kmax_ref[...], jnp.max(kn2.reshape(blk // sub, sub, LANES), axis=0))
            return carry
        lax.fori_loop(0, v_ref.shape[0] // blk, tr, 0)
        kmax_ref[...] = jnp.broadcast_to(jnp.max(kmax_ref[...], axis=0, keepdims=True), kmax_ref.shape)

    q = (q_ref[...].astype(F32) * (DK_A ** -0.5 * LOG2E)).astype(BF16)
    qf = q.astype(F32)
    qs_ref[:, 0:blk] = jnp.where(lane < DK_A, qf, 0.0).T.astype(BF16)
    qs_ref[:, blk:nq] = jnp.where(lane >= DK_A, qf, 0.0).T.astype(BF16)
    acc_ref[...] = jnp.zeros(acc_ref.shape, F32)

    qt = qs_ref[...].astype(F32)
    qn2 = jnp.sum((qt * qt).reshape(LANES // sub, sub, nq), axis=0)
    qn2 = jnp.broadcast_to(jnp.sum(qn2, axis=0, keepdims=True), (sub, nq))
    kmax2 = jnp.concatenate([jnp.broadcast_to(kmax_ref[:, m * DK_A:m * DK_A + 1], (sub, blk)) for m in range(2)],
                            axis=1)
    bound = jnp.sqrt(qn2 * kmax2) * 1.03
    bias_max, bias_span = bstat_ref[0, 0:1, 0:1], bstat_ref[0, 1:2, 0:1]
    r_ref[...] = bound + bias_max
    single_pass = jnp.max(2.0 * bound + bias_span) < SINGLE_PASS_LOG2_RANGE

    @pl.when(single_pass)
    def _():
        _diffattn_fixed_shift(i, k_ref, bias_ref, qs_ref, vt_ref, r_ref, m_ref, acc_ref, pa_ref, pb_ref)

    @pl.when(jnp.logical_not(single_pass))
    def _():
        _diffattn_online(i, k_ref, bias_ref, qs_ref, vt_ref, m_ref, acc_ref, sa_ref, sb_ref)

    ot = acc_ref[0:dv, 0:nq] / acc_ref[dv:dv + 1, 0:nq]
    o = ot[:, 0:blk].T - lam_ref[...] * ot[:, blk:nq].T
    o = o * lax.rsqrt(jnp.mean(o * o, axis=-1, keepdims=True) + EPS) * g_ref[...]
    o_ref[...] = (o * out_scale).astype(o_ref.dtype)


def _diffattn_fixed_shift(i, k_ref, bias_ref, qs_ref, vt_ref, shift_ref, l_ref, acc_ref, pa_ref, pb_ref):
    blk = BLK_A
    nq = 2 * blk
    sub = SUBLANES
    npart = len(pa_ref)
    wq = nq // npart
    l_ref[...] = jnp.zeros(l_ref.shape, F32)

    def probs(b, p_ref, bias):
        k = k_ref[pl.ds(pl.multiple_of(b * blk, blk), blk), :]
        for part in range(npart):
            cols = slice(part * wq, (part + 1) * wq)
            s = _dot(k, qs_ref[:, cols])
            if bias is not None:
                q0 = (part * wq) % blk
                s = s + bias[:, q0:q0 + wq]
            p = jnp.exp2(s.reshape(blk // sub, sub, wq) - shift_ref[:, cols][None])
            l_ref[:, cols] += jnp.sum(p, axis=0)
            p_ref[part][:, 0:wq] = p.reshape(blk, wq).astype(BF16)

    def accumulate(b, p_ref):
        vt = vt_ref[0:DV_A, pl.ds(pl.multiple_of(b * blk, blk), blk)]
        for part in range(npart):
            cols = slice(part * wq, (part + 1) * wq)
            acc_ref[0:DV_A, cols] += _dot(vt, p_ref[part][:, 0:wq])

    @pl.when(i == 0)
    def _():
        probs(0, pa_ref, bias_ref[0, 1])
        accumulate(0, pa_ref)

    @pl.when(i > 0)
    def _():
        nfar = i - 1
        probs(i, pa_ref, bias_ref[0, 1])
        probs(i - 1, pb_ref, bias_ref[0, 0])
        accumulate(i, pa_ref)

        def pair(t):
            probs(2 * t, pa_ref, None)
            accumulate(jnp.where(t == 0, i - 1, 2 * t - 1), pb_ref)
            probs(2 * t + 1, pb_ref, None)
            accumulate(2 * t, pa_ref)

        def four_pairs(u, carry):
            for v in range(4):
                pair(4 * u + v)
            return carry

        npairs = nfar // 2
        lax.fori_loop(0, npairs // 4, four_pairs, 0)

        def one_pair(t, carry):
            pair(t)
            return carry
        lax.fori_loop(4 * (npairs // 4), npairs, one_pair, 0)
        in_pb = jnp.where(npairs == 0, i - 1, 2 * npairs - 1)

        @pl.when(lax.rem(nfar, 2) == 1)
        def _():
            probs(nfar - 1, pa_ref, None)
            accumulate(in_pb, pb_ref)
            accumulate(nfar - 1, pa_ref)

        @pl.when(lax.rem(nfar, 2) == 0)
        def _():
            accumulate(in_pb, pb_ref)

    acc_ref[DV_A:DV_A + sub, 0:nq] = jnp.broadcast_to(jnp.sum(l_ref[...], axis=0, keepdims=True), (sub, nq))


def _diffattn_online(i, k_ref, bias_ref, qs_ref, vt_ref, m_ref, acc_ref, sa_ref, sb_ref):
    blk = BLK_A
    nq = 2 * blk
    sub = SUBLANES
    npart = len(sa_ref) // 2
    wq = nq // npart
    m_ref[...] = jnp.full(m_ref.shape, NEG_INF, F32)

    def scores(b, s_ref):
        k = k_ref[pl.ds(pl.multiple_of(b * blk, blk), blk), :]
        for part in range(npart):
            s = _dot(k, qs_ref[:, part * wq:(part + 1) * wq])
            s_ref[part][:, 0:wq] = s
            s_ref[npart + part][...] = jnp.max(s.reshape(blk // sub, sub, wq), axis=0)

    def softmax_pv(b, s_ref, bias):
        vt = vt_ref[:, pl.ds(pl.multiple_of(b * blk, blk), blk)]
        for part in range(npart):
            cols = slice(part * wq, (part + 1) * wq)
            s = s_ref[part][:, 0:wq]
            if bias is not None:
                q0 = (part * wq) % blk
                s = s + bias[:, q0:q0 + wq]
            s = s.reshape(blk // sub, sub, wq)
            m_prev = m_ref[:, cols]
            smax = jnp.max(s, axis=0) if bias is not None else s_ref[npart + part][...]
            m_cur = jnp.max(smax, axis=0, keepdims=True)
            m_new = jnp.maximum(m_prev, m_cur)
            alpha = jnp.exp2(m_prev - m_new)
            p = jnp.exp2(s - m_new[None])
            pv = _dot(vt, p.reshape(blk, wq).astype(BF16))
            acc_ref[:, cols] = acc_ref[:, cols] * alpha[0:1] + pv
            m_ref[:, cols] = m_new

    nfar = jnp.maximum(i - 1, 0)
    odd = lax.rem(nfar, 2)

    @pl.when(i == 0)
    def _():
        scores(0, sb_ref)

    @pl.when(i > 0)
    def _():
        @pl.when(odd == 1)
        def _():
            scores(0, sb_ref)
            scores(1, sa_ref)
            softmax_pv(0, sb_ref, None)

        @pl.when(odd == 0)
        def _():
            scores(0, sa_ref)

        def pair(b):
            scores(b + 1, sb_ref)
            softmax_pv(b, sa_ref, None)
            scores(b + 2, sa_ref)
            softmax_pv(b + 1, sb_ref, None)

        def quad_body(t, carry):
            pair(odd + 4 * t)
            pair(odd + 4 * t + 2)
            return carry

        npairs = nfar // 2
        lax.fori_loop(0, npairs // 2, quad_body, 0)

        @pl.when(lax.rem(npairs, 2) == 1)
        def _():
            pair(odd + 2 * (npairs - 1))
        scores(i, sb_ref)
        softmax_pv(i - 1, sa_ref, bias_ref[0, 0])

    softmax_pv(i, sb_ref, bias_ref[0, 1])


_TOEPLITZ_ROWS = 256
_TOEPLITZ_N = 2048


def _toeplitz_kernel(v_ref, o_ref, *, keep):
    rows, cols = o_ref.shape[2:]
    x = jnp.broadcast_to(v_ref[0, 0], (rows, v_ref.shape[-1]))
    tile = pltpu.roll(x, 0, 1, stride=1, stride_axis=0)[:, :cols]
    r = lax.broadcasted_iota(jnp.int32, (rows, cols), 0) + pl.program_id(1) * rows
    c = lax.broadcasted_iota(jnp.int32, (rows, cols), 1)
    for variant in range(o_ref.shape[0]):
        o_ref[variant, 0] = jnp.where(keep(r, c, variant), tile, NEG_INF)


def _toeplitz_tiles(fn, keep, heads, rows, cols, variants=1):
    n, rb = _TOEPLITZ_N, _TOEPLITZ_ROWS
    assert rows % rb == 0 and rows <= n // 2 and cols <= n // 2
    idx = jnp.arange(n, dtype=jnp.int32)
    vec = fn(jnp.where(idx < n // 2, idx, idx - n)).astype(F32)
    vecs = jnp.stack([jnp.roll(vec, k * rb, axis=1) for k in range(rows // rb)], axis=1)
    return pl.pallas_call(
        functools.partial(_toeplitz_kernel, keep=keep),
        grid=(heads, rows // rb),
        in_specs=[pl.BlockSpec((1, 1, 1, n), lambda h, k: (h, k, 0, 0))],
        out_specs=pl.BlockSpec((variants, 1, rb, cols), lambda h, k: (0, h, k, 0)),
        out_shape=jax.ShapeDtypeStruct((variants, heads, rows, cols), F32),
        name="toeplitz_tiles",
    )(vecs.reshape(heads, rows // rb, 1, n))


def _t5_bucket(rel):
    nb = NUM_BUCKETS // 2
    max_exact = nb // 2
    bucket = jnp.where(rel > 0, nb, 0)
    n = jnp.abs(rel)
    nf = jnp.maximum(n, 1).astype(F32)
    large = max_exact + (jnp.log(nf / max_exact) / math.log(MAX_DISTANCE / max_exact)
                         * (nb - max_exact)).astype(jnp.int32)
    large = jnp.minimum(large, nb - 1)
    return bucket + jnp.where(n < max_exact, n, large)


def _diff_bias_tiles(t5_table):
    blk = BLK_A
    table = t5_table.astype(F32)
    far = table[_t5_bucket(jnp.full((), -(blk + 1), jnp.int32))]
    def visible(r, c, variant):
        return jnp.floor_divide(r - blk, CHUNK) <= jnp.floor_divide(c, CHUNK)

    tiles = _toeplitz_tiles(lambda x: ((table[_t5_bucket(-x - blk)] - far) * LOG2E).T, visible,
                            N_HEADS_A, 2 * blk, blk)
    return tiles.reshape(N_HEADS_A, 2, blk, blk)


def _diff_attention(proj, t5_table, lam, subln_g, lam_init):
    seq = proj.shape[0]
    blk = BLK_A
    bias = _diff_bias_tiles(t5_table)
    ha = N_HEADS_A
    finite = bias > 0.5 * NEG_INF
    bias_max = jnp.maximum(jnp.max(jnp.where(finite, bias, NEG_INF), axis=(1, 2, 3)), 0.0)
    bias_min = jnp.minimum(jnp.min(jnp.where(finite, bias, -NEG_INF), axis=(1, 2, 3)), 0.0)
    bstat = jnp.broadcast_to(jnp.stack([bias_max, bias_max - bias_min], axis=1)[:, :, None], (ha, 2, LANES))
    kern = functools.partial(_diffattn_kernel, out_scale=1.0 - lam_init)
    return pl.pallas_call(
        kern,
        grid=(ha, seq // blk),
        in_specs=[
            pl.BlockSpec((blk, DV_A), lambda h, i: (i, h)),
            pl.BlockSpec((seq, DV_A), lambda h, i: (0, ha + h)),
            pl.BlockSpec((seq, DV_A), lambda h, i: (0, 2 * ha + h)),
            pl.BlockSpec((1, 2, blk, blk), lambda h, i: (h, 0, 0, 0)),
            pl.BlockSpec((1, 2, LANES), lambda h, i: (h, 0, 0)),
            pl.BlockSpec((1, DV_A), lambda h, i: (0, 0)),
            pl.BlockSpec((1, DV_A), lambda h, i: (0, 0)),
        ],
        out_specs=pl.BlockSpec((blk, DV_A), lambda h, i: (i, h)),
        out_shape=jax.ShapeDtypeStruct((seq, ha * DV_A), BF16),
        scratch_shapes=[
            pltpu.VMEM((DV_A, 2 * blk), BF16),
            pltpu.VMEM((DV_A + ONES_A, seq), BF16),
            pltpu.VMEM((SUBLANES, LANES), F32),
            pltpu.VMEM((SUBLANES, 2 * blk), F32),
            pltpu.VMEM((SUBLANES, 2 * blk), F32),
            pltpu.VMEM((DV_A + ONES_A, 2 * blk), F32),
        ] + 2 * ([pltpu.VMEM((blk, 2 * blk // NPART_A + SCORE_PAD), F32)] * NPART_A
                 + [pltpu.VMEM((SUBLANES, 2 * blk // NPART_A), F32)] * NPART_A)
        + 2 * [pltpu.VMEM((blk, 2 * blk // NPART_A + SCORE_PAD), BF16)] * NPART_A,
        compiler_params=pltpu.CompilerParams(dimension_semantics=("parallel", "arbitrary")),
        name="diff_attention",
    )(proj, proj, proj, bias, bstat, jnp.full((1, DV_A), lam, F32), subln_g.reshape(1, DV_A).astype(F32))


def _band_kernel(q_ref, kp_ref, kc_ref, vp_ref, vc_ref, *refs):
    qw, band = QW_B, BAND_B
    nbias = band // qw + 1
    bias_refs, o_ref, s_refs = refs[:nbias], refs[nbias], refs[nbias + 1:]
    nk = band + qw
    sub = SUBLANES
    q = q_ref[...].astype(F32) * (DH_B ** -0.5 * LOG2E)
    lane = lax.broadcasted_iota(jnp.int32, q.shape, 1)
    qh = (jnp.where(lane < DH_B, q, 0.0).T.astype(BF16), jnp.where(lane >= DH_B, q, 0.0).T.astype(BF16))
    k_all = jnp.concatenate([kp_ref[...], kc_ref[...]], axis=0)
    vt_all = jnp.concatenate([vp_ref[...], vc_ref[...]], axis=0).astype(F32).T.astype(BF16)
    vt_all = jnp.concatenate([vt_all, jnp.ones((ONES_A, vt_all.shape[1]), BF16)], axis=0)
    ngroups = len(s_refs)
    for g in range(ngroups):
        k0 = g * qw
        qs = jnp.concatenate([qh[0][:, k0:k0 + qw], qh[1][:, k0:k0 + qw]], axis=1)
        s_refs[g][:, 0:2 * qw] = _dot(k_all[k0:k0 + nk], qs)
    for g in range(ngroups):
        k0 = g * qw
        bias_ref = bias_refs[min(g, nbias - 1)]
        bias = jnp.concatenate([bias_ref[0, 0], bias_ref[0, 1]], axis=1)
        s = (s_refs[g][:, 0:2 * qw] + bias).reshape(nk // sub, sub, 2 * qw)
        m = jnp.max(jnp.max(s, axis=0), axis=0, keepdims=True)
        p = jnp.exp2(s - m[None])
        pv = _dot(vt_all[:, k0:k0 + nk], p.reshape(nk, 2 * qw).astype(BF16))
        ot = pv[0:2 * DH_B] / pv[2 * DH_B:2 * DH_B + 1]
        o = jnp.concatenate([ot[0:DH_B, 0:qw], ot[DH_B:2 * DH_B, qw:2 * qw]], axis=0)
        o_ref[k0:k0 + qw, :] = o.T.astype(o_ref.dtype)


def _band_bias_tiles(rel_bias):
    band = BAND_B

    def valid(r, c, variant):
        qchunk = jnp.floor_divide(c, CHUNK)
        kchunk = jnp.floor_divide(r - band, CHUNK)
        missing = jnp.where(variant == 0, 0, band - (variant - 1) * QW_B)
        return (kchunk <= qchunk) & (kchunk >= qchunk - LEFT_CHUNKS) & (r >= missing)

    return _toeplitz_tiles(
        lambda x: rel_bias.astype(F32)[:, jnp.clip(-x - band, -REL_CLIP, REL_CLIP) + REL_CLIP] * LOG2E, valid,
        N_HEADS_B, band + QW_B, QW_B, variants=1 + band // QW_B)


def _band_attention(proj, rel_bias):
    seq = proj.shape[0]
    blk, band, qw = BLK_B, BAND_B, QW_B
    bias = _band_bias_tiles(rel_bias)
    npair = N_HEADS_B // 2
    qc0 = 3 * N_HEADS_A
    per = blk // band
    prev = lambda c0: (lambda hp, i: (jnp.maximum(i * per - 1, 0), c0 + hp))
    cur = lambda c0: (lambda hp, i: (i, c0 + hp))
    return pl.pallas_call(
        _band_kernel,
        grid=(npair, seq // blk),
        in_specs=[
            pl.BlockSpec((blk, LANES), cur(qc0)),
            pl.BlockSpec((band, LANES), prev(qc0 + npair)),
            pl.BlockSpec((blk, LANES), cur(qc0 + npair)),
            pl.BlockSpec((band, LANES), prev(qc0 + 2 * npair)),
            pl.BlockSpec((blk, LANES), cur(qc0 + 2 * npair)),
        ] + [
            pl.BlockSpec((1, 2, band + qw, qw), (lambda hp, i, t=t: (jnp.where(i == 0, 1 + t, 0), hp, 0, 0)))
            for t in range(band // qw)
        ] + [
            pl.BlockSpec((1, 2, band + qw, qw), lambda hp, i: (0, hp, 0, 0)),
        ],
        out_specs=pl.BlockSpec((blk, LANES), lambda hp, i: (i, hp)),
        out_shape=jax.ShapeDtypeStruct((seq, N_HEADS_B * DH_B), BF16),
        scratch_shapes=[pltpu.VMEM((band + qw, 2 * qw + SCORE_PAD), F32)] * (blk // qw),
        compiler_params=pltpu.CompilerParams(dimension_semantics=("parallel", "arbitrary")),
        name="band_attention",
    )(proj, proj, proj, proj, proj, *([bias] * bias.shape[0]))


def _retention_kernel(qk_ref, v_ref, gate_ref, cos_ref, sin_ref, qdec_ref, kdec_ref, dmat_ref,
                      sdec_ref, o_ref, state_ref):
    @pl.when(pl.program_id(0) == 0)
    def _():
        state_ref[...] = jnp.zeros(state_ref.shape, F32)

    cos = cos_ref[...]
    sin = sin_ref[...]
    lane = lax.broadcasted_iota(jnp.int32, cos.shape, 1)
    first_half = (lane % DQK_C) < (DQK_C // 2)
    qk = qk_ref[...]
    parts = []
    for j in range(qk.shape[1] // LANES):
        t = qk[:, j * LANES:(j + 1) * LANES]
        partner = jnp.where(first_half, pltpu.roll(t, LANES - DQK_C // 2, 1), pltpu.roll(t, DQK_C // 2, 1))
        parts.append(t * cos + partner * sin)
    wq = N_HEADS_C * DQK_C
    q = jnp.concatenate(parts[:wq // LANES], axis=1)
    k = jnp.concatenate(parts[wq // LANES:], axis=1) * (DQK_C ** -0.5)
    qd = (q * qdec_ref[...]).astype(BF16)
    kd = (k * kdec_ref[...]).astype(BF16)
    qb = q.astype(BF16)
    kb = k.astype(BF16)
    vb = v_ref[...].astype(BF16)
    gate = gate_ref[...]
    outs = []
    for h in range(N_HEADS_C):
        qs = slice(h * DQK_C, (h + 1) * DQK_C)
        vs = slice(h * DV_C, (h + 1) * DV_C)
        scores = _dot_nt(qb[:, qs], kb[:, qs]) * dmat_ref[h]
        state = state_ref[h]
        r = _dot(scores.astype(BF16), vb[:, vs]) + _dot(qd[:, qs], state.astype(BF16))
        state_ref[h] = state * sdec_ref[h] + _dot_tn(kd[:, qs], vb[:, vs])
        r = r * lax.rsqrt(jnp.mean(r * r, axis=-1, keepdims=True) + EPS)
        g = gate[:, vs]
        outs.append(r * (g * jax.nn.sigmoid(g)))
    o_ref[...] = jnp.concatenate(outs, axis=1).astype(o_ref.dtype)


def _retention_tables(seq):
    t = BLK_C
    half = DQK_C // 2
    inv_freq = 1.0 / (ROPE_BASE ** (jnp.arange(0, DQK_C, 2, dtype=F32) / DQK_C))
    ang = jnp.arange(seq, dtype=F32)[:, None] * inv_freq[None, :]
    reps = LANES // half
    cos = jnp.tile(jnp.cos(ang), (1, reps))
    sign = jnp.where((jnp.arange(LANES) % DQK_C) < half, -1.0, 1.0).astype(F32)
    sin = jnp.tile(jnp.sin(ang), (1, reps)) * sign[None, :]
    log_g = jnp.log(1.0 - jnp.power(2.0, -5.0 - jnp.arange(N_HEADS_C, dtype=F32)))
    pos = jnp.arange(t, dtype=F32)
    diff = pos[:, None] - pos[None, :]
    same_or_past = (jnp.arange(t)[None, :] // CHUNK) <= (jnp.arange(t)[:, None] // CHUNK)
    dmat = jnp.where(same_or_past[None], jnp.exp(log_g[:, None, None] * jnp.abs(diff)[None]), 0.0)
    qdec = jnp.repeat(jnp.exp(log_g[None, :] * (pos[:, None] + 1.0)), DQK_C, axis=1)
    kdec = jnp.repeat(jnp.exp(log_g[None, :] * (t - 1.0 - pos[:, None])), DQK_C, axis=1)
    sdec = jnp.broadcast_to(jnp.exp(log_g * t)[:, None, None], (N_HEADS_C, 1, DV_C))
    return cos, sin, qdec, kdec, dmat, sdec


def _retention(proj):
    seq = proj.shape[0]
    t = BLK_C
    cos, sin, qdec, kdec, dmat, sdec = _retention_tables(seq)
    wv = N_HEADS_C * DV_C
    return pl.pallas_call(
        _retention_kernel,
        grid=(seq // t,),
        in_specs=[
            pl.BlockSpec((t, wv), lambda i: (i, 0)),
            pl.BlockSpec((t, wv), lambda i: (i, 1)),
            pl.BlockSpec((t, wv), lambda i: (i, 2)),
            pl.BlockSpec((t, LANES), lambda i: (i, 0)),
            pl.BlockSpec((t, LANES), lambda i: (i, 0)),
            pl.BlockSpec((t, N_HEADS_C * DQK_C), lambda i: (0, 0)),
            pl.BlockSpec((t, N_HEADS_C * DQK_C), lambda i: (0, 0)),
            pl.BlockSpec((N_HEADS_C, t, t), lambda i: (0, 0, 0)),
            pl.BlockSpec((N_HEADS_C, 1, DV_C), lambda i: (0, 0, 0)),
        ],
        out_specs=pl.BlockSpec((t, wv), lambda i: (i, 0)),
        out_shape=jax.ShapeDtypeStruct((seq, wv), BF16),
        scratch_shapes=[pltpu.VMEM((N_HEADS_C, DQK_C, DV_C), F32)],
        compiler_params=pltpu.CompilerParams(dimension_semantics=("arbitrary",)),
        name="retention",
    )(proj, proj, proj, cos, sin, qdec, kdec, dmat, sdec)


def _s5_kernel(*refs):
    ncb = S5_CH // LANES
    u_refs = refs[:ncb]
    (mt_ref, bt_ref, ctr_ref, cti_ref, are_ref, aim_ref, y_ref,
     ut_ref, yt_ref, ys_ref, vr_ref, vi_ref, spr_ref, spi_ref, carry_ref) = refs[ncb:]
    tc = S5_TC
    gp = S5_GROUP
    n = S5_STATE
    ng = S5_GROUPS

    @pl.when(pl.program_id(0) == 0)
    def _():
        carry_ref[...] = jnp.zeros(carry_ref.shape, F32)

    for s in range(S5_T):
        for k in range(ncb):
            ut_ref[s, k * LANES:(k + 1) * LANES, :] = u_refs[k][pl.ds(s, tc, stride=S5_T), :].T

    unroll = 4

    def intra(it, carry):
        for k in range(unroll):
            g = it * unroll + k
            r0 = pl.multiple_of(g * gp, gp)
            ug = ut_ref[:, pl.ds(r0, gp), :].reshape(S5_T * gp, tc).astype(BF16)
            yt_ref[:, pl.ds(r0, gp), :] = _dot(mt_ref[g], ug).reshape(S5_T, gp, tc)
            vt = _dot(bt_ref[g], ug)
            n0 = pl.multiple_of(g * n, n)
            vr_ref[pl.ds(n0, n), :] = vt[0:n]
            vi_ref[pl.ds(n0, n), :] = vt[n:2 * n]
        return carry

    lax.fori_loop(0, ng // unroll, intra, 0)

    sub = SUBLANES
    nv = tc // sub
    row = lax.broadcasted_iota(jnp.int32, (tc, LANES), 0)
    in_vreg = lax.rem(row, sub)

    def rows_of(v, r):
        return jnp.broadcast_to(v[r:r + 1], (tc, LANES))

    for j in range(ng * n // LANES):
        cols = slice(j * LANES, (j + 1) * LANES)
        pwr, pwi = are_ref[:, cols], aim_ref[:, cols]
        xr = vr_ref[cols, :].T
        xi = vi_ref[cols, :].T
        for d in (1, 2, 4):
            keep = in_vreg >= d
            sr = jnp.where(keep, pltpu.roll(xr, d, 0), 0.0)
            si = jnp.where(keep, pltpu.roll(xi, d, 0), 0.0)
            fr, fi = rows_of(pwr, d - 1), rows_of(pwi, d - 1)
            xr, xi = xr + (fr * sr - fi * si), xi + (fr * si + fi * sr)
        cr, ci = carry_ref[0, :, cols], carry_ref[1, :, cols]
        cr0, ci0 = cr, ci
        outr, outi = [], []
        for v in range(nv):
            yr = xr[v * sub:(v + 1) * sub] + (pwr * cr - pwi * ci)
            yi = xi[v * sub:(v + 1) * sub] + (pwr * ci + pwi * cr)
            outr.append(yr)
            outi.append(yi)
            cr = jnp.broadcast_to(yr[sub - 1:sub], (sub, LANES))
            ci = jnp.broadcast_to(yi[sub - 1:sub], (sub, LANES))
        carry_ref[0, :, cols] = cr
        carry_ref[1, :, cols] = ci
        sr = jnp.concatenate(outr, axis=0)
        si = jnp.concatenate(outi, axis=0)
        first = row == 0
        spr_ref[j] = jnp.where(first, rows_of(cr0, 0), pltpu.roll(sr, 1, 0))
        spi_ref[j] = jnp.where(first, rows_of(ci0, 0), pltpu.roll(si, 1, 0))

    def cross(it, carry):
        for k in range(unroll):
            jp = it * unroll + k
            r0 = pl.multiple_of(jp * 2 * gp, 2 * gp)
            yc = (_dot_nt(ctr_ref[jp], spr_ref[jp].astype(BF16))
                  + _dot_nt(cti_ref[jp], spi_ref[jp].astype(BF16)))
            yt_ref[:, pl.ds(r0, 2 * gp), :] += yc.reshape(S5_T, 2 * gp, tc)
        return carry

    lax.fori_loop(0, ng // 2 // unroll, cross, 0)

    for s in range(S5_T):
        for k in range(ncb):
            ys_ref[k, pl.ds(s, tc, stride=S5_T), :] = yt_ref[s, k * LANES:(k + 1) * LANES, :].T
    for k in range(ncb):
        y_ref[:, k * LANES:(k + 1) * LANES] = ys_ref[k]


def _s5_matrices(lam_re, lam_im, log_step, b_re, b_im, c_re, c_im, d_skip):
    hi = lax.Precision.HIGHEST
    t, gp, n, ng = S5_T, S5_GROUP, S5_STATE, S5_GROUPS
    lam = lax.complex(lam_re.astype(F32), lam_im.astype(F32))
    step = jnp.exp(log_step.astype(F32))[:, None]
    ls = lam * step
    a_bar = jnp.exp(ls)
    b_bar = ((a_bar - 1.0) / lam)[..., None] * lax.complex(b_re.astype(F32), b_im.astype(F32))
    cm = lax.complex(c_re.astype(F32), c_im.astype(F32))

    def apow(k):
        kk = k.astype(F32).astype(jnp.complex64)
        return jnp.exp(ls.reshape((ng,) + (1,) * k.ndim + (n,)) * kk[None, ..., None])

    tt = jnp.arange(t)
    kmat = jnp.einsum('gpn,gln,gnq->glpq', cm, apow(tt), b_bar, precision=hi).real
    krev = jnp.transpose(kmat[:, ::-1], (0, 2, 1, 3)).reshape(ng, gp, t * gp)
    kpad = jnp.pad(krev, ((0, 0), (0, 0), (0, t * gp)))
    mt = jnp.concatenate([kpad[:, :, (t - 1 - to) * gp:(2 * t - 1 - to) * gp] for to in range(t)], axis=1)
    dvec = jnp.tile(d_skip.astype(F32).reshape(ng, 1, gp), (1, t, 1)).reshape(ng, t * gp)
    mt = mt + jnp.eye(t * gp, dtype=F32)[None] * dvec[:, :, None]
    z = jnp.swapaxes(apow(t - 1 - tt), 1, 2)[:, :, :, None] * b_bar[:, :, None, :]
    z = z.reshape(ng, n, t * gp)
    bt = jnp.concatenate([z.real, z.imag], axis=1)
    w = cm[:, None, :, :] * apow(tt + 1)[:, :, None, :]

    def pair_readout(x):
        x = x.reshape(ng // 2, 2, t, gp, n)
        first = jnp.pad(x[:, 0], ((0, 0), (0, 0), (0, 0), (0, n)))
        second = jnp.pad(x[:, 1], ((0, 0), (0, 0), (0, 0), (n, 0)))
        return jnp.stack([first, second], axis=2).reshape(ng // 2, t * 2 * gp, 2 * n).astype(BF16)

    ctr, cti = pair_readout(w.real), pair_readout(-w.imag)
    a_chunk = jnp.transpose(apow(t * (jnp.arange(SUBLANES) + 1)), (1, 0, 2)).reshape(SUBLANES, ng * n)
    return mt.astype(BF16), bt.astype(BF16), ctr, cti, a_chunk.real, a_chunk.imag


def _s5(proj, mats):
    seq, width = proj.shape
    t, tc, gp, n, ng = S5_T, S5_TC, S5_GROUP, S5_STATE, S5_GROUPS
    rows = t * tc
    ncb = S5_CH // LANES
    cb0 = (width - S5_CH) // LANES
    u_specs = [pl.BlockSpec((rows, LANES), (lambda i, k=k: (i, cb0 + k))) for k in range(ncb)]
    nsb = ng * n // LANES
    return pl.pallas_call(
        _s5_kernel,
        grid=(seq // rows,),
        in_specs=u_specs + [_const_spec(m.shape) for m in mats],
        out_specs=pl.BlockSpec((rows, S5_CH), lambda i: (i, 0)),
        out_shape=jax.ShapeDtypeStruct((seq, S5_CH), F32),
        scratch_shapes=[
            pltpu.VMEM((t, S5_CH, tc), F32),
            pltpu.VMEM((t, S5_CH, tc), F32),
            pltpu.VMEM((ncb, rows, LANES), F32),
            pltpu.VMEM((ng * n, tc), F32),
            pltpu.VMEM((ng * n, tc), F32),
            pltpu.VMEM((nsb, tc, LANES), F32),
            pltpu.VMEM((nsb, tc, LANES), F32),
            pltpu.VMEM((2, SUBLANES, ng * n), F32),
        ],
        compiler_params=pltpu.CompilerParams(dimension_semantics=("arbitrary",)),
        name="s5_scan",
    )(*([proj] * ncb), *mats)


def _mix_ffn_kernel(*refs, glu, final):
    (x_ref, a_ref, b_ref, wo_ref, g1_ref), refs = refs[:5], refs[5:]
    if glu:
        gw_ref, refs = refs[0], refs[1:]
    (g_ref, sc_ref, sh_ref, gate_ref, win_ref, cw_ref, cb_ref, wout_ref), refs = refs[:8], refs[8:]
    if final:
        fg_ref, o_ref, h_ref, act_ref, gbuf_ref, carry_ref = refs
    else:
        ng_ref, nsc_ref, nsh_ref, o_ref, hn_ref, h_ref, act_ref, gbuf_ref, carry_ref = refs
    tm = x_ref.shape[0]
    halo = gbuf_ref.shape[0] - tm

    @pl.when(pl.program_id(0) == 0)
    def _():
        carry_ref[...] = jnp.zeros(carry_ref.shape, F32)

    if glu:
        y = jax.nn.gelu(b_ref[...]).astype(BF16)
        gg = _dot(y, gw_ref[...])
        half = gg.shape[1] // 2
        b = (gg[:, :half] * jax.nn.sigmoid(gg[:, half:])).astype(BF16)
    else:
        b = b_ref[...]
    cat = jnp.concatenate([a_ref[...], b], axis=1)
    x = x_ref[...] + g1_ref[...] * _dot(cat, wo_ref[...])
    h_ref[...] = _mod_rmsnorm(x, g_ref[...], sc_ref[...], sh_ref[...]).astype(BF16)
    for f in range(D_FF // TF_FFN):
        cs = slice(f * TF_FFN, (f + 1) * TF_FFN)
        gs = slice(D_FF + f * TF_FFN, D_FF + (f + 1) * TF_FFN)
        h = h_ref[...]
        val = _dot(h, win_ref[:, cs])
        gate = _dot(h, win_ref[:, gs])
        gbuf_ref[0:halo, :] = carry_ref[:, cs]
        gbuf_ref[halo:halo + tm, :] = gate
        carry_ref[:, cs] = gate[tm - halo:tm, :]
        conv = (gate * cw_ref[2:3, cs] + gbuf_ref[halo - 1:halo - 1 + tm, :] * cw_ref[1:2, cs]
                + gbuf_ref[halo - 2:halo - 2 + tm, :] * cw_ref[0:1, cs] + cb_ref[:, cs])
        act_ref[:, cs] = (jax.nn.gelu(conv) * val).astype(BF16)
    xn = x + gate_ref[...] * _dot(act_ref[...], wout_ref[...])
    if final:
        xn = xn * lax.rsqrt(jnp.mean(xn * xn, axis=-1, keepdims=True) + EPS) * fg_ref[...]
    else:
        hn_ref[...] = _mod_rmsnorm(xn, ng_ref[...], nsc_ref[...], nsh_ref[...]).astype(BF16)
    o_ref[...] = xn


def _layer_spec(shape, layer):
    idx = (layer,) + (0,) * (len(shape) - 1)
    return pl.BlockSpec((None,) + tuple(shape[1:]), lambda *_: idx, pipeline_mode=pl.Buffered(1))


def _mix_ffn(x, a, b, wo, gate1, glu_w, g, scale, shift, gate2, w_in, conv_w, conv_b, w_out, tail, layer):
    seq, d = x.shape
    final = len(tail) == 1
    tm = TM_FFN
    halo = SUBLANES
    row = pl.BlockSpec((1, d), lambda i: (0, 0))
    rows = lambda w: pl.BlockSpec((tm, w), lambda i: (i, 0))
    conv_b = conv_b.reshape(conv_b.shape[0], 1, D_FF)
    in_specs = [rows(d), rows(a.shape[1]), rows(b.shape[1]), _const_spec(wo.shape), row]
    args = [x, a, b, wo, gate1]
    if glu_w is not None:
        in_specs.append(_const_spec(glu_w.shape))
        args.append(glu_w)
    in_specs += [
        row, row, row, row,
        _layer_spec(w_in.shape, layer),
        _layer_spec(conv_w.shape, layer),
        _layer_spec(conv_b.shape, layer),
        _layer_spec(w_out.shape, layer),
    ] + [row] * len(tail)
    args += [g.reshape(1, d), scale, shift, gate2, w_in, conv_w, conv_b, w_out]
    args += [t.reshape(1, d) for t in tail]
    out_specs = [rows(d)] if final else [rows(d), rows(d)]
    out_shape = [jax.ShapeDtypeStruct((seq, d), F32)] + ([] if final else [jax.ShapeDtypeStruct((seq, d), BF16)])
    return pl.pallas_call(
        functools.partial(_mix_ffn_kernel, glu=glu_w is not None, final=final),
        grid=(seq // tm,),
        in_specs=in_specs,
        out_specs=out_specs,
        out_shape=out_shape,
        scratch_shapes=[
            pltpu.VMEM((tm, d), BF16),
            pltpu.VMEM((tm, D_FF), BF16),
            pltpu.VMEM((tm + halo, TF_FFN), F32),
            pltpu.VMEM((halo, D_FF), F32),
        ],
        compiler_params=pltpu.CompilerParams(dimension_semantics=("arbitrary",)),
        name="mix_ffn",
    )(*args)


def kernel(x, c, t5_table, mod_w, mod_b, norm1_g, norm2_g, ffn_w_in, ffn_conv_w, ffn_conv_b, ffn_w_out,
           ev_w_in, ev_w_out, diff_lambda, diff_subln_g, band_rel_bias,
           od_w_in, od_w_out, s5_lam_re, s5_lam_im, s5_log_step, s5_b_re, s5_b_im, s5_c_re, s5_c_im,
           s5_d, s5_glu_w, final_g):
    assert x.shape[0] == 1 and x.shape[2] == D_MODEL
    seq = x.shape[1]
    assert seq % TM_PROJ == 0 and seq % (S5_T * S5_TC) == 0
    d = D_MODEL
    xs = x[0]
    mod = _modulation(c, mod_w, mod_b)
    ffn_w_in_b = ffn_w_in.astype(BF16)
    ffn_w_out_b = ffn_w_out.astype(BF16)
    mods = [[mod[i, :, k * d:(k + 1) * d] for k in range(6)] for i in range(DEPTH)]
    h = None
    for i in range(DEPTH):
        sh1, sc1, g1, sh2, sc2, g2 = mods[i]
        w_in = (ev_w_in if i % 2 == 0 else od_w_in)[i // 2].astype(BF16)
        proj_dtype = BF16 if i % 2 == 0 else F32
        if h is None:
            proj = _normproj(xs, norm1_g[i], sc1, sh1, w_in, proj_dtype)
        else:
            proj = _proj(h, w_in, proj_dtype)
        if i % 2 == 0:
            e = i // 2
            lam_init = 0.8 - 0.6 * math.exp(-0.3 * i)
            lp = diff_lambda[e].astype(F32)
            lam = jnp.exp(jnp.sum(lp[0] * lp[1])) - jnp.exp(jnp.sum(lp[2] * lp[3])) + lam_init
            mix_a = _diff_attention(proj, t5_table, lam, diff_subln_g[e], lam_init)
            mix_b = _band_attention(proj, band_rel_bias[e])
            wo, glu_w = ev_w_out[e].astype(BF16), None
        else:
            o = i // 2
            mix_a = _retention(proj)
            mats = _s5_matrices(s5_lam_re[o], s5_lam_im[o], s5_log_step[o], s5_b_re[o], s5_b_im[o],
                                s5_c_re[o], s5_c_im[o], s5_d[o])
            mix_b = _s5(proj, mats)
            wo, glu_w = od_w_out[o].astype(BF16), s5_glu_w[o].astype(BF16)
        if i == DEPTH - 1:
            tail = (final_g,)
        else:
            nsh1, nsc1 = mods[i + 1][0], mods[i + 1][1]
            tail = (norm1_g[i + 1], nsc1, nsh1)
        out = _mix_ffn(xs, mix_a, mix_b, wo, g1, glu_w, norm2_g[i], sc2, sh2, g2,
                       ffn_w_in_b, ffn_conv_w, ffn_conv_b, ffn_w_out_b, tail, layer=i)
        if i == DEPTH - 1:
            xs = out[0]
        else:
            xs, h = out
    return xs[None]
```

```python
import functools
import math

import jax
import jax.numpy as jnp
from jax import lax
from jax.experimental import pallas as pl
from jax.experimental.pallas import tpu as pltpu

F32 = jnp.float32
BF16 = jnp.bfloat16

D_MODEL = 1024
DEPTH = 2
CHUNK = 64
GROUP_WIDTH = D_MODEL // 2
DK_A = 64
DV_A = 2 * DK_A
N_HEADS_A = GROUP_WIDTH // DV_A
DH_B = 64
N_HEADS_B = GROUP_WIDTH // DH_B
LEFT_CHUNKS = 8
REL_CLIP = 2 * CHUNK
NUM_BUCKETS = 32
MAX_DISTANCE = 128
DV_C = 128
DQK_C = DV_C // 2
N_HEADS_C = GROUP_WIDTH // DV_C
ROPE_BASE = 10000.0
S5_CH = GROUP_WIDTH
S5_GROUP = 16
S5_GROUPS = S5_CH // S5_GROUP
S5_STATE = 64
D_FF = ((8 * D_MODEL // 3 + 255) // 256) * 256
CONV_W = 3
EVEN_IN = 3 * N_HEADS_A * DV_A + 3 * N_HEADS_B * DH_B
ODD_IN = 2 * N_HEADS_C * DQK_C + 2 * N_HEADS_C * DV_C + S5_CH
EPS = 1e-6
NEG_INF = -1e30
LOG2E = math.log2(math.e)

LANES = 128
SUBLANES = 8
MXU_DIM = 256

TM_PROJ = 1024
TN_PROJ = 1024
TN_MOD = 1536
TM_FFN = 512
TF_FFN = MXU_DIM
BLK_A = 512
NPART_A = 2
ONES_A = 16
SINGLE_PASS_LOG2_RANGE = 96.0
SCORE_PAD = LANES
BLK_B = 1024
BAND_B = LEFT_CHUNKS * CHUNK
QW_B = 4 * CHUNK
BLK_C = 512
S5_T = 16
S5_TC = LANES

assert BLK_B % BAND_B == 0 and BLK_B % QW_B == 0 and BAND_B % QW_B == 0
assert BLK_A >= MAX_DISTANCE, "far key blocks must sit in the saturated T5 bucket"
assert DV_A == LANES and 2 * DK_A == LANES and 2 * DH_B == LANES, "attention heads are read as 128-lane column blocks"


def _dot(a, b):
    return jnp.dot(a, b, preferred_element_type=F32)


def _dot_nt(a, b):
    return lax.dot_general(a, b, (((1,), (1,)), ((), ())), preferred_element_type=F32)


def _dot_tn(a, b):
    return lax.dot_general(a, b, (((0,), (0,)), ((), ())), preferred_element_type=F32)


def _const_spec(shape):
    zeros = (0,) * len(shape)
    return pl.BlockSpec(shape, lambda *_: zeros, pipeline_mode=pl.Buffered(1))


def _mod_rmsnorm(x, g, scale, shift):
    y = x * lax.rsqrt(jnp.mean(x * x, axis=-1, keepdims=True) + EPS)
    y = y * g
    return y * (1.0 + scale) + shift


def _mod_kernel(c_ref, w_ref, b_ref, o_ref):
    c = c_ref[...]
    cond = c * jax.nn.sigmoid(c)
    o_ref[0] = jnp.sum(cond * w_ref[0], axis=0, keepdims=True) + b_ref[0]


def _modulation(c, mod_w, mod_b):
    depth, d, n = mod_w.shape
    tn = TN_MOD
    return pl.pallas_call(
        _mod_kernel,
        grid=(depth, n // tn),
        in_specs=[
            pl.BlockSpec((d, 1), lambda i, j: (0, 0)),
            pl.BlockSpec((1, d, tn), lambda i, j: (i, 0, j)),
            pl.BlockSpec((1, 1, tn), lambda i, j: (i, 0, j)),
        ],
        out_specs=pl.BlockSpec((1, 1, tn), lambda i, j: (i, 0, j)),
        out_shape=jax.ShapeDtypeStruct((depth, 1, n), F32),
        name="modulation",
    )(c.reshape(d, 1), mod_w, mod_b.reshape(depth, 1, n))


def _normproj_kernel(x_ref, g_ref, sc_ref, sh_ref, w_ref, o_ref):
    tm, n = o_ref.shape
    half = tm // 2
    for r in range(2):
        rows = slice(r * half, (r + 1) * half)
        h = _mod_rmsnorm(x_ref[rows, :], g_ref[...], sc_ref[...], sh_ref[...]).astype(BF16)
        for j in range(n // TN_PROJ):
            cols = slice(j * TN_PROJ, (j + 1) * TN_PROJ)
            o_ref[rows, cols] = _dot(h, w_ref[:, cols]).astype(o_ref.dtype)


def _normproj(x, g, scale, shift, w, out_dtype):
    seq, d = x.shape
    n = w.shape[1]
    tm = TM_PROJ
    row = pl.BlockSpec((1, d), lambda i: (0, 0))
    return pl.pallas_call(
        _normproj_kernel,
        grid=(seq // tm,),
        in_specs=[pl.BlockSpec((tm, d), lambda i: (i, 0)), row, row, row, _const_spec(w.shape)],
        out_specs=pl.BlockSpec((tm, n), lambda i: (i, 0)),
        out_shape=jax.ShapeDtypeStruct((seq, n), out_dtype),
        compiler_params=pltpu.CompilerParams(dimension_semantics=("parallel",)),
        name="normproj",
    )(x, g.reshape(1, d), scale, shift, w)


def _proj_kernel(h_ref, w_ref, o_ref):
    for j in range(o_ref.shape[1] // TN_PROJ):
        cols = slice(j * TN_PROJ, (j + 1) * TN_PROJ)
        o_ref[:, cols] = _dot(h_ref[...], w_ref[:, cols]).astype(o_ref.dtype)


def _proj(h, w, out_dtype):
    seq, d = h.shape
    n = w.shape[1]
    tm = TM_PROJ
    return pl.pallas_call(
        _proj_kernel,
        grid=(seq // tm,),
        in_specs=[pl.BlockSpec((tm, d), lambda i: (i, 0)), _const_spec(w.shape)],
        out_specs=pl.BlockSpec((tm, n), lambda i: (i, 0)),
        out_shape=jax.ShapeDtypeStruct((seq, n), out_dtype),
        compiler_params=pltpu.CompilerParams(dimension_semantics=("parallel",)),
        name="proj",
    )(h, w)


def _diffattn_kernel(q_ref, k_ref, v_ref, bias_ref, bstat_ref, lam_ref, g_ref, o_ref,
                     qs_ref, vt_ref, kmax_ref, r_ref, m_ref, acc_ref, *s_refs, out_scale):
    blk = BLK_A
    nq = 2 * blk
    sub = SUBLANES
    dv = DV_A
    npart = NPART_A
    sa_ref, sb_ref = s_refs[:2 * npart], s_refs[2 * npart:4 * npart]
    pa_ref, pb_ref = s_refs[4 * npart:5 * npart], s_refs[5 * npart:6 * npart]
    i = pl.program_id(1)
    lane = lax.broadcasted_iota(jnp.int32, (blk, LANES), 1)
    same_subhead = (lax.broadcasted_iota(jnp.int32, (LANES, LANES), 0) // DK_A
                    == lax.broadcasted_iota(jnp.int32, (LANES, LANES), 1) // DK_A).astype(BF16)

    @pl.when(i == 0)
    def _():
        kmax_ref[...] = jnp.zeros(kmax_ref.shape, F32)

        def tr(b, carry):
            r0 = pl.multiple_of(b * blk, blk)
            vt_ref[0:dv, pl.ds(r0, blk)] = v_ref[pl.ds(r0, blk), :].astype(F32).T.astype(BF16)
            vt_ref[dv:dv + ONES_A, pl.ds(r0, blk)] = jnp.ones((ONES_A, blk), BF16)
            kf = k_ref[pl.ds(r0, blk), :].astype(F32)
            kn2 = _dot((kf * kf).astype(BF16), same_subhead)
            kmax_ref[...] = jnp.maximum(kmax_ref[...], jnp.max(kn2.reshape(blk // sub, sub, LANES), axis=0))
            return carry
        lax.fori_loop(0, v_ref.shape[0] // blk, tr, 0)
        kmax_ref[...] = jnp.broadcast_to(jnp.max(kmax_ref[...], axis=0, keepdims=True), kmax_ref.shape)

    q = (q_ref[...].astype(F32) * (DK_A ** -0.5 * LOG2E)).astype(BF16)
    qf = q.astype(F32)
    qs_ref[:, 0:blk] = jnp.where(lane < DK_A, qf, 0.0).T.astype(BF16)
    qs_ref[:, blk:nq] = jnp.where(lane >= DK_A, qf, 0.0).T.astype(BF16)
    acc_ref[...] = jnp.zeros(acc_ref.shape, F32)

    qt = qs_ref[...].astype(F32)
    qn2 = jnp.sum((qt * qt).reshape(LANES // sub, sub, nq), axis=0)
    qn2 = jnp.broadcast_to(jnp.sum(qn2, axis=0, keepdims=True), (sub, nq))
    kmax2 = jnp.concatenate([jnp.broadcast_to(kmax_ref[:, m * DK_A:m * DK_A + 1], (sub, blk)) for m in range(2)],
                            axis=1)
    bound = jnp.sqrt(qn2 * kmax2) * 1.03
    bias_max, bias_span = bstat_ref[0, 0:1, 0:1], bstat_ref[0, 1:2, 0:1]
    r_ref[...] = bound + bias_max
    single_pass = jnp.max(2.0 * bound + bias_span) < SINGLE_PASS_LOG2_RANGE

    @pl.when(single_pass)
    def _():
        _diffattn_fixed_shift(i, k_ref, bias_ref, qs_ref, vt_ref, r_ref, m_ref, acc_ref, pa_ref, pb_ref)

    @pl.when(jnp.logical_not(single_pass))
    def _():
        _diffattn_online(i, k_ref, bias_ref, qs_ref, vt_ref, m_ref, acc_ref, sa_ref, sb_ref)

    ot = acc_ref[0:dv, 0:nq] / acc_ref[dv:dv + 1, 0:nq]
    o = ot[:, 0:blk].T - lam_ref[...] * ot[:, blk:nq].T
    o = o * lax.rsqrt(jnp.mean(o * o, axis=-1, keepdims=True) + EPS) * g_ref[...]
    o_ref[...] = (o * out_scale).astype(o_ref.dtype)


def _diffattn_fixed_shift(i, k_ref, bias_ref, qs_ref, vt_ref, shift_ref, l_ref, acc_ref, pa_ref, pb_ref):
    blk = BLK_A
    nq = 2 * blk
    sub = SUBLANES
    npart = len(pa_ref)
    wq = nq // npart
    l_ref[...] = jnp.zeros(l_ref.shape, F32)

    def probs(b, p_ref, bias):
        k = k_ref[pl.ds(pl.multiple_of(b * blk, blk), blk), :]
        for part in range(npart):
            cols = slice(part * wq, (part + 1) * wq)
            s = _dot(k, qs_ref[:, cols])
            if bias is not None:
                q0 = (part * wq) % blk
                s = s + bias[:, q0:q0 + wq]
            p = jnp.exp2(s.reshape(blk // sub, sub, wq) - shift_ref[:, cols][None])
            l_ref[:, cols] += jnp.sum(p, axis=0)
            p_ref[part][:, 0:wq] = p.reshape(blk, wq).astype(BF16)

    def accumulate(b, p_ref):
        vt = vt_ref[0:DV_A, pl.ds(pl.multiple_of(b * blk, blk), blk)]
        for part in range(npart):
            cols = slice(part * wq, (part + 1) * wq)
            acc_ref[0:DV_A, cols] += _dot(vt, p_ref[part][:, 0:wq])

    @pl.when(i == 0)
    def _():
        probs(0, pa_ref, bias_ref[0, 1])
        accumulate(0, pa_ref)

    @pl.when(i > 0)
    def _():
        nfar = i - 1
        probs(i, pa_ref, bias_ref[0, 1])
        probs(i - 1, pb_ref, bias_ref[0, 0])
        accumulate(i, pa_ref)

        def pair(t):
            probs(2 * t, pa_ref, None)
            accumulate(jnp.where(t == 0, i - 1, 2 * t - 1), pb_ref)
            probs(2 * t + 1, pb_ref, None)
            accumulate(2 * t, pa_ref)

        def four_pairs(u, carry):
            for v in range(4):
                pair(4 * u + v)
            return carry

        npairs = nfar // 2
        lax.fori_loop(0, npairs // 4, four_pairs, 0)

        def one_pair(t, carry):
            pair(t)
            return carry
        lax.fori_loop(4 * (npairs // 4), npairs, one_pair, 0)
        in_pb = jnp.where(npairs == 0, i - 1, 2 * npairs - 1)

        @pl.when(lax.rem(nfar, 2) == 1)
        def _():
            probs(nfar - 1, pa_ref, None)
            accumulate(in_pb, pb_ref)
            accumulate(nfar - 1, pa_ref)

        @pl.when(lax.rem(nfar, 2) == 0)
        def _():
            accumulate(in_pb, pb_ref)

    acc_ref[DV_A:DV_A + sub, 0:nq] = jnp.broadcast_to(jnp.sum(l_ref[...], axis=0, keepdims=True), (sub, nq))


def _diffattn_online(i, k_ref, bias_ref, qs_ref, vt_ref, m_ref, acc_ref, sa_ref, sb_ref):
    blk = BLK_A
    nq = 2 * blk
    sub = SUBLANES
    npart = len(sa_ref) // 2
    wq = nq // npart
    m_ref[...] = jnp.full(m_ref.shape, NEG_INF, F32)

    def scores(b, s_ref):
        k = k_ref[pl.ds(pl.multiple_of(b * blk, blk), blk), :]
        for part in range(npart):
            s = _dot(k, qs_ref[:, part * wq:(part + 1) * wq])
            s_ref[part][:, 0:wq] = s
            s_ref[npart + part][...] = jnp.max(s.reshape(blk // sub, sub, wq), axis=0)

    def softmax_pv(b, s_ref, bias):
        vt = vt_ref[:, pl.ds(pl.multiple_of(b * blk, blk), blk)]
        for part in range(npart):
            cols = slice(part * wq, (part + 1) * wq)
            s = s_ref[part][:, 0:wq]
            if bias is not None:
                q0 = (part * wq) % blk
                s = s + bias[:, q0:q0 + wq]
            s = s.reshape(blk // sub, sub, wq)
            m_prev = m_ref[:, cols]
            smax = jnp.max(s, axis=0) if bias is not None else s_ref[npart + part][...]
            m_cur = jnp.max(smax, axis=0, keepdims=True)
            m_new = jnp.maximum(m_prev, m_cur)
            alpha = jnp.exp2(m_prev - m_new)
            p = jnp.exp2(s - m_new[None])
            pv = _dot(vt, p.reshape(blk, wq).astype(BF16))
            acc_ref[:, cols] = acc_ref[:, cols] * alpha[0:1] + pv
            m_ref[:, cols] = m_new

    nfar = jnp.maximum(i - 1, 0)
    odd = lax.rem(nfar, 2)

    @pl.when(i == 0)
    def _():
        scores(0, sb_ref)

    @pl.when(i > 0)
    def _():
        @pl.when(odd == 1)
        def _():
            scores(0, sb_ref)
            scores(1, sa_ref)
            softmax_pv(0, sb_ref, None)

        @pl.when(odd == 0)
        def _():
            scores(0, sa_ref)

        def pair(b):
            scores(b + 1, sb_ref)
            softmax_pv(b, sa_ref, None)
            scores(b + 2, sa_ref)
            softmax_pv(b + 1, sb_ref, None)

        def quad_body(t, carry):
            pair(odd + 4 * t)
            pair(odd + 4 * t + 2)
            return carry

        npairs = nfar // 2
        lax.fori_loop(0, npairs // 2, quad_body, 0)

        @pl.when(lax.rem(npairs, 2) == 1)
        def _():
            pair(odd + 2 * (npairs - 1))
        scores(i, sb_ref)
        softmax_pv(i - 1, sa_ref, bias_ref[0, 0])

    softmax_pv(i, sb_ref, bias_ref[0, 1])


_TOEPLITZ_ROWS = 256
_TOEPLITZ_N = 2048


def _toeplitz_kernel(v_ref, o_ref, *, keep):
    rows, cols = o_ref.shape[2:]
    x = jnp.broadcast_to(v_ref[0, 0], (rows, v_ref.shape[-1]))
    tile = pltpu.roll(x, 0, 1, stride=1, stride_axis=0)[:, :cols]
    r = lax.broadcasted_iota(jnp.int32, (rows, cols), 0) + pl.program_id(1) * rows
    c = lax.broadcasted_iota(jnp.int32, (rows, cols), 1)
    for variant in range(o_ref.shape[0]):
        o_ref[variant, 0] = jnp.where(keep(r, c, variant), tile, NEG_INF)


def _toeplitz_tiles(fn, keep, heads, rows, cols, variants=1):
    n, rb = _TOEPLITZ_N, _TOEPLITZ_ROWS
    assert rows % rb == 0 and rows <= n // 2 and cols <= n // 2
    idx = jnp.arange(n, dtype=jnp.int32)
    vec = fn(jnp.where(idx < n // 2, idx, idx - n)).astype(F32)
    vecs = jnp.stack([jnp.roll(vec, k * rb, axis=1) for k in range(rows // rb)], axis=1)
    return pl.pallas_call(
        functools.partial(_toeplitz_kernel, keep=keep),
        grid=(heads, rows // rb),
        in_specs=[pl.BlockSpec((1, 1, 1, n), lambda h, k: (h, k, 0, 0))],
        out_specs=pl.BlockSpec((variants, 1, rb, cols), lambda h, k: (0, h, k, 0)),
        out_shape=jax.ShapeDtypeStruct((variants, heads, rows, cols), F32),
        name="toeplitz_tiles",
    )(vecs.reshape(heads, rows // rb, 1, n))


def _bias_stats(tiles):
    finite = tiles > 0.5 * NEG_INF
    bias_max = jnp.maximum(jnp.max(jnp.where(finite, tiles, NEG_INF), axis=(1, 2)), 0.0)
    bias_min = jnp.minimum(jnp.min(jnp.where(finite, tiles, -NEG_INF), axis=(1, 2)), 0.0)
    return jnp.broadcast_to(jnp.stack([bias_max, bias_max - bias_min], axis=1)[:, :, None],
                            (tiles.shape[0], 2, LANES))


def _t5_bucket(rel):
    nb = NUM_BUCKETS // 2
    max_exact = nb // 2
    bucket = jnp.where(rel > 0, nb, 0)
    n = jnp.abs(rel)
    nf = jnp.maximum(n, 1).astype(F32)
    large = max_exact + (jnp.log(nf / max_exact) / math.log(MAX_DISTANCE / max_exact)
                         * (nb - max_exact)).astype(jnp.int32)
    large = jnp.minimum(large, nb - 1)
    return bucket + jnp.where(n < max_exact, n, large)


def _diff_bias_tiles(t5_table):
    blk = BLK_A
    table = t5_table.astype(F32)
    far = table[_t5_bucket(jnp.full((), -(blk + 1), jnp.int32))]
    def visible(r, c, variant):
        return jnp.floor_divide(r - blk, CHUNK) <= jnp.floor_divide(c, CHUNK)

    tiles = _toeplitz_tiles(lambda x: ((table[_t5_bucket(-x - blk)] - far) * LOG2E).T, visible,
                            N_HEADS_A, 2 * blk, blk)
    return tiles.reshape(N_HEADS_A, 2, blk, blk)


def _diff_attention(proj, t5_table, lam, subln_g, lam_init):
    seq = proj.shape[0]
    blk = BLK_A
    bias = _diff_bias_tiles(t5_table)
    ha = N_HEADS_A
    bstat = _bias_stats(bias.reshape(ha, 2 * blk, blk))
    kern = functools.partial(_diffattn_kernel, out_scale=1.0 - lam_init)
    return pl.pallas_call(
        kern,
        grid=(ha, seq // blk),
        in_specs=[
            pl.BlockSpec((blk, DV_A), lambda h, i: (i, h)),
            pl.BlockSpec((seq, DV_A), lambda h, i: (0, ha + h)),
            pl.BlockSpec((seq, DV_A), lambda h, i: (0, 2 * ha + h)),
            pl.BlockSpec((1, 2, blk, blk), lambda h, i: (h, 0, 0, 0)),
            pl.BlockSpec((1, 2, LANES), lambda h, i: (h, 0, 0)),
            pl.BlockSpec((1, DV_A), lambda h, i: (0, 0)),
            pl.BlockSpec((1, DV_A), lambda h, i: (0, 0)),
        ],
        out_specs=pl.BlockSpec((blk, DV_A), lambda h, i: (i, h)),
        out_shape=jax.ShapeDtypeStruct((seq, ha * DV_A), BF16),
        scratch_shapes=[
            pltpu.VMEM((DV_A, 2 * blk), BF16),
            pltpu.VMEM((DV_A + ONES_A, seq), BF16),
            pltpu.VMEM((SUBLANES, LANES), F32),
            pltpu.VMEM((SUBLANES, 2 * blk), F32),
            pltpu.VMEM((SUBLANES, 2 * blk), F32),
            pltpu.VMEM((DV_A + ONES_A, 2 * blk), F32),
        ] + 2 * ([pltpu.VMEM((blk, 2 * blk // NPART_A + SCORE_PAD), F32)] * NPART_A
                 + [pltpu.VMEM((SUBLANES, 2 * blk // NPART_A), F32)] * NPART_A)
        + 2 * [pltpu.VMEM((blk, 2 * blk // NPART_A + SCORE_PAD), BF16)] * NPART_A,
        compiler_params=pltpu.CompilerParams(dimension_semantics=("parallel", "arbitrary")),
        name="diff_attention",
    )(proj, proj, proj, bias, bstat, jnp.full((1, DV_A), lam, F32), subln_g.reshape(1, DV_A).astype(F32))


def _band_kernel(q_ref, kp_ref, kc_ref, vp_ref, vc_ref, *refs):
    qw, band = QW_B, BAND_B
    nbias = band // qw + 1
    bias_refs, bstat_ref, o_ref = refs[:nbias], refs[nbias], refs[nbias + 1]
    ngroups = (len(refs) - nbias - 2) // 2
    s_refs, p_refs = refs[nbias + 2:nbias + 2 + ngroups], refs[nbias + 2 + ngroups:]
    nk = band + qw
    sub = SUBLANES
    q = (q_ref[...].astype(F32) * (DH_B ** -0.5 * LOG2E)).astype(BF16).astype(F32)
    lane = lax.broadcasted_iota(jnp.int32, q.shape, 1)
    qt = (jnp.where(lane < DH_B, q, 0.0).T, jnp.where(lane >= DH_B, q, 0.0).T)
    qh = (qt[0].astype(BF16), qt[1].astype(BF16))
    k_all = jnp.concatenate([kp_ref[...], kc_ref[...]], axis=0)
    vt_all = jnp.concatenate([vp_ref[...], vc_ref[...]], axis=0).astype(F32).T.astype(BF16)

    def group_operands(g):
        k0 = g * qw
        qs = jnp.concatenate([qh[0][:, k0:k0 + qw], qh[1][:, k0:k0 + qw]], axis=1)
        bias_ref = bias_refs[min(g, nbias - 1)]
        bias = jnp.concatenate([bias_ref[0, 0], bias_ref[0, 1]], axis=1)
        return k0, qs, bias

    def store_group(g, ot):
        o = jnp.concatenate([ot[0:DH_B, 0:qw], ot[DH_B:2 * DH_B, qw:2 * qw]], axis=0)
        o_ref[g * qw:(g + 1) * qw, :] = o.T.astype(o_ref.dtype)

    same_head = (lax.broadcasted_iota(jnp.int32, (LANES, LANES), 0) // DH_B
                 == lax.broadcasted_iota(jnp.int32, (LANES, LANES), 1) // DH_B).astype(BF16)
    kf = k_all.astype(F32)
    kn2 = _dot((kf * kf).astype(BF16), same_head)
    kmax2 = jnp.max(jnp.max(kn2.reshape(kn2.shape[0] // sub, sub, LANES), axis=0), axis=0, keepdims=True)
    shifts, worst = [], None
    for m in range(2):
        qn2 = jnp.sum(qt[m] * qt[m], axis=0, keepdims=True)
        bound = jnp.sqrt(qn2 * kmax2[:, m * DH_B:m * DH_B + 1]) * 1.03
        shifts.append(bound + bstat_ref[m, 0:1, 0:1])
        spread = jnp.max(2.0 * bound + bstat_ref[m, 1:2, 0:1])
        worst = spread if worst is None else jnp.maximum(worst, spread)
    fixed_shift = worst < SINGLE_PASS_LOG2_RANGE

    @pl.when(fixed_shift)
    def _():
        sums = []
        for g in range(ngroups):
            k0, qs, bias = group_operands(g)
            r = jnp.concatenate([shifts[0][:, k0:k0 + qw], shifts[1][:, k0:k0 + qw]], axis=1)
            s = _dot(k_all[k0:k0 + nk], qs) + (bias - r)
            p = jnp.exp2(s).reshape(nk // sub, sub, 2 * qw)
            sums.append(jnp.sum(jnp.sum(p, axis=0), axis=0, keepdims=True))
            p_refs[g][:, 0:2 * qw] = p.reshape(nk, 2 * qw).astype(BF16)
        for g in range(ngroups):
            k0 = g * qw
            store_group(g, _dot(vt_all[:, k0:k0 + nk], p_refs[g][:, 0:2 * qw]) / sums[g])

    @pl.when(jnp.logical_not(fixed_shift))
    def _():
        vt_ones = jnp.concatenate([vt_all, jnp.ones((ONES_A, vt_all.shape[1]), BF16)], axis=0)
        for g in range(ngroups):
            k0, qs, _ = group_operands(g)
            s_refs[g][:, 0:2 * qw] = _dot(k_all[k0:k0 + nk], qs)
        for g in range(ngroups):
            k0, _, bias = group_operands(g)
            s = (s_refs[g][:, 0:2 * qw] + bias).reshape(nk // sub, sub, 2 * qw)
            m = jnp.max(jnp.max(s, axis=0), axis=0, keepdims=True)
            p = jnp.exp2(s - m[None])
            pv = _dot(vt_ones[:, k0:k0 + nk], p.reshape(nk, 2 * qw).astype(BF16))
            store_group(g, pv[0:2 * DH_B] / pv[2 * DH_B:2 * DH_B + 1])


def _band_bias_tiles(rel_bias):
    band = BAND_B

    def valid(r, c, variant):
        qchunk = jnp.floor_divide(c, CHUNK)
        kchunk = jnp.floor_divide(r - band, CHUNK)
        missing = jnp.where(variant == 0, 0, band - (variant - 1) * QW_B)
        return (kchunk <= qchunk) & (kchunk >= qchunk - LEFT_CHUNKS) & (r >= missing)

    return _toeplitz_tiles(
        lambda x: rel_bias.astype(F32)[:, jnp.clip(-x - band, -REL_CLIP, REL_CLIP) + REL_CLIP] * LOG2E, valid,
        N_HEADS_B, band + QW_B, QW_B, variants=1 + band // QW_B)


def _band_attention(proj, rel_bias):
    seq = proj.shape[0]
    blk, band, qw = BLK_B, BAND_B, QW_B
    bias = _band_bias_tiles(rel_bias)
    npair = N_HEADS_B // 2
    qc0 = 3 * N_HEADS_A
    per = blk // band
    prev = lambda c0: (lambda hp, i: (jnp.maximum(i * per - 1, 0), c0 + hp))
    cur = lambda c0: (lambda hp, i: (i, c0 + hp))
    return pl.pallas_call(
        _band_kernel,
        grid=(npair, seq // blk),
        in_specs=[
            pl.BlockSpec((blk, LANES), cur(qc0)),
            pl.BlockSpec((band, LANES), prev(qc0 + npair)),
            pl.BlockSpec((blk, LANES), cur(qc0 + npair)),
            pl.BlockSpec((band, LANES), prev(qc0 + 2 * npair)),
            pl.BlockSpec((blk, LANES), cur(qc0 + 2 * npair)),
        ] + [
            pl.BlockSpec((1, 2, band + qw, qw), (lambda hp, i, t=t: (jnp.where(i == 0, 1 + t, 0), hp, 0, 0)))
            for t in range(band // qw)
        ] + [
            pl.BlockSpec((1, 2, band + qw, qw), lambda hp, i: (0, hp, 0, 0)),
            pl.BlockSpec((2, 2, LANES), lambda hp, i: (hp, 0, 0)),
        ],
        out_specs=pl.BlockSpec((blk, LANES), lambda hp, i: (i, hp)),
        out_shape=jax.ShapeDtypeStruct((seq, N_HEADS_B * DH_B), BF16),
        scratch_shapes=([pltpu.VMEM((band + qw, 2 * qw + SCORE_PAD), F32)] * (blk // qw)
                        + [pltpu.VMEM((band + qw, 2 * qw + SCORE_PAD), BF16)] * (blk // qw)),
        compiler_params=pltpu.CompilerParams(dimension_semantics=("parallel", "arbitrary")),
        name="band_attention",
    )(proj, proj, proj, proj, proj, *([bias] * bias.shape[0]), _bias_stats(bias[0]))


def _retention_kernel(qk_ref, v_ref, gate_ref, cos_ref, sin_ref, qdec_ref, kdec_ref, dmat_ref,
                      sdec_ref, o_ref, state_ref):
    @pl.when(pl.program_id(0) == 0)
    def _():
        state_ref[...] = jnp.zeros(state_ref.shape, F32)

    cos = cos_ref[...]
    sin = sin_ref[...]
    lane = lax.broadcasted_iota(jnp.int32, cos.shape, 1)
    first_half = (lane % DQK_C) < (DQK_C // 2)
    qk = qk_ref[...]
    parts = []
    for j in range(qk.shape[1] // LANES):
        t = qk[:, j * LANES:(j + 1) * LANES]
        partner = jnp.where(first_half, pltpu.roll(t, LANES - DQK_C // 2, 1), pltpu.roll(t, DQK_C // 2, 1))
        parts.append(t * cos + partner * sin)
    wq = N_HEADS_C * DQK_C
    q = jnp.concatenate(parts[:wq // LANES], axis=1)
    k = jnp.concatenate(parts[wq // LANES:], axis=1) * (DQK_C ** -0.5)
    qd = (q * qdec_ref[...]).astype(BF16)
    kd = (k * kdec_ref[...]).astype(BF16)
    qb = q.astype(BF16)
    kb = k.astype(BF16)
    vb = v_ref[...].astype(BF16)
    gate = gate_ref[...]
    outs = []
    for h in range(N_HEADS_C):
        qs = slice(h * DQK_C, (h + 1) * DQK_C)
        vs = slice(h * DV_C, (h + 1) * DV_C)
        scores = _dot_nt(qb[:, qs], kb[:, qs]) * dmat_ref[h]
        state = state_ref[h]
        r = _dot(scores.astype(BF16), vb[:, vs]) + _dot(qd[:, qs], state.astype(BF16))
        state_ref[h] = state * sdec_ref[h] + _dot_tn(kd[:, qs], vb[:, vs])
        r = r * lax.rsqrt(jnp.mean(r * r, axis=-1, keepdims=True) + EPS)
        g = gate[:, vs]
        outs.append(r * (g * jax.nn.sigmoid(g)))
    o_ref[...] = jnp.concatenate(outs, axis=1).astype(o_ref.dtype)


def _retention_tables(seq):
    t = BLK_C
    half = DQK_C // 2
    inv_freq = 1.0 / (ROPE_BASE ** (jnp.arange(0, DQK_C, 2, dtype=F32) / DQK_C))
    ang = jnp.arange(seq, dtype=F32)[:, None] * inv_freq[None, :]
    reps = LANES // half
    cos = jnp.tile(jnp.cos(ang), (1, reps))
    sign = jnp.where((jnp.arange(LANES) % DQK_C) < half, -1.0, 1.0).astype(F32)
    sin = jnp.tile(jnp.sin(ang), (1, reps)) * sign[None, :]
    log_g = jnp.log(1.0 - jnp.power(2.0, -5.0 - jnp.arange(N_HEADS_C, dtype=F32)))
    pos = jnp.arange(t, dtype=F32)
    diff = pos[:, None] - pos[None, :]
    same_or_past = (jnp.arange(t)[None, :] // CHUNK) <= (jnp.arange(t)[:, None] // CHUNK)
    dmat = jnp.where(same_or_past[None], jnp.exp(log_g[:, None, None] * jnp.abs(diff)[None]), 0.0)
    qdec = jnp.repeat(jnp.exp(log_g[None, :] * (pos[:, None] + 1.0)), DQK_C, axis=1)
    kdec = jnp.repeat(jnp.exp(log_g[None, :] * (t - 1.0 - pos[:, None])), DQK_C, axis=1)
    sdec = jnp.broadcast_to(jnp.exp(log_g * t)[:, None, None], (N_HEADS_C, 1, DV_C))
    return cos, sin, qdec, kdec, dmat, sdec


def _retention(proj):
    seq = proj.shape[0]
    t = BLK_C
    cos, sin, qdec, kdec, dmat, sdec = _retention_tables(seq)
    wv = N_HEADS_C * DV_C
    return pl.pallas_call(
        _retention_kernel,
        grid=(seq // t,),
        in_specs=[
            pl.BlockSpec((t, wv), lambda i: (i, 0)),
            pl.BlockSpec((t, wv), lambda i: (i, 1)),
            pl.BlockSpec((t, wv), lambda i: (i, 2)),
            pl.BlockSpec((t, LANES), lambda i: (i, 0)),
            pl.BlockSpec((t, LANES), lambda i: (i, 0)),
            pl.BlockSpec((t, N_HEADS_C * DQK_C), lambda i: (0, 0)),
            pl.BlockSpec((t, N_HEADS_C * DQK_C), lambda i: (0, 0)),
            pl.BlockSpec((N_HEADS_C, t, t), lambda i: (0, 0, 0)),
            pl.BlockSpec((N_HEADS_C, 1, DV_C), lambda i: (0, 0, 0)),
        ],
        out_specs=pl.BlockSpec((t, wv), lambda i: (i, 0)),
        out_shape=jax.ShapeDtypeStruct((seq, wv), BF16),
        scratch_shapes=[pltpu.VMEM((N_HEADS_C, DQK_C, DV_C), F32)],
        compiler_params=pltpu.CompilerParams(dimension_semantics=("arbitrary",)),
        name="retention",
    )(proj, proj, proj, cos, sin, qdec, kdec, dmat, sdec)


def _s5_kernel(*refs):
    ncb = S5_CH // LANES
    u_refs = refs[:ncb]
    (mt_ref, bt_ref, ctr_ref, cti_ref, are_ref, aim_ref, y_ref,
     ut_ref, yt_ref, ys_ref, vr_ref, vi_ref, spr_ref, spi_ref, carry_ref) = refs[ncb:]
    tc = S5_TC
    gp = S5_GROUP
    n = S5_STATE
    ng = S5_GROUPS

    @pl.when(pl.program_id(0) == 0)
    def _():
        carry_ref[...] = jnp.zeros(carry_ref.shape, F32)

    for s in range(S5_T):
        for k in range(ncb):
            ut_ref[s, k * LANES:(k + 1) * LANES, :] = u_refs[k][pl.ds(s, tc, stride=S5_T), :].T

    unroll = 4

    def intra(it, carry):
        for k in range(unroll):
            g = it * unroll + k
            r0 = pl.multiple_of(g * gp, gp)
            ug = ut_ref[:, pl.ds(r0, gp), :].reshape(S5_T * gp, tc).astype(BF16)
            yt_ref[:, pl.ds(r0, gp), :] = _dot(mt_ref[g], ug).reshape(S5_T, gp, tc)
            vt = _dot(bt_ref[g], ug)
            n0 = pl.multiple_of(g * n, n)
            vr_ref[pl.ds(n0, n), :] = vt[0:n]
            vi_ref[pl.ds(n0, n), :] = vt[n:2 * n]
        return carry

    lax.fori_loop(0, ng // unroll, intra, 0)

    sub = SUBLANES
    nv = tc // sub
    row = lax.broadcasted_iota(jnp.int32, (tc, LANES), 0)
    in_vreg = lax.rem(row, sub)

    def rows_of(v, r):
        return jnp.broadcast_to(v[r:r + 1], (tc, LANES))

    for j in range(ng * n // LANES):
        cols = slice(j * LANES, (j + 1) * LANES)
        pwr, pwi = are_ref[:, cols], aim_ref[:, cols]
        xr = vr_ref[cols, :].T
        xi = vi_ref[cols, :].T
        for d in (1, 2, 4):
            keep = in_vreg >= d
            sr = jnp.where(keep, pltpu.roll(xr, d, 0), 0.0)
            si = jnp.where(keep, pltpu.roll(xi, d, 0), 0.0)
            fr, fi = rows_of(pwr, d - 1), rows_of(pwi, d - 1)
            xr, xi = xr + (fr * sr - fi * si), xi + (fr * si + fi * sr)
        cr, ci = carry_ref[0, :, cols], carry_ref[1, :, cols]
        cr0, ci0 = cr, ci
        outr, outi = [], []
        for v in range(nv):
            yr = xr[v * sub:(v + 1) * sub] + (pwr * cr - pwi * ci)
            yi = xi[v * sub:(v + 1) * sub] + (pwr * ci + pwi * cr)
            outr.append(yr)
            outi.append(yi)
            cr = jnp.broadcast_to(yr[sub - 1:sub], (sub, LANES))
            ci = jnp.broadcast_to(yi[sub - 1:sub], (sub, LANES))
        carry_ref[0, :, cols] = cr
        carry_ref[1, :, cols] = ci
        sr = jnp.concatenate(outr, axis=0)
        si = jnp.concatenate(outi, axis=0)
        first = row == 0
        spr_ref[j] = jnp.where(first, rows_of(cr0, 0), pltpu.roll(sr, 1, 0))
        spi_ref[j] = jnp.where(first, rows_of(ci0, 0), pltpu.roll(si, 1, 0))

    def cross(it, carry):
        for k in range(unroll):
            jp = it * unroll + k
            r0 = pl.multiple_of(jp * 2 * gp, 2 * gp)
            yc = (_dot_nt(ctr_ref[jp], spr_ref[jp].astype(BF16))
                  + _dot_nt(cti_ref[jp], spi_ref[jp].astype(BF16)))
            yt_ref[:, pl.ds(r0, 2 * gp), :] += yc.reshape(S5_T, 2 * gp, tc)
        return carry

    lax.fori_loop(0, ng // 2 // unroll, cross, 0)

    for s in range(S5_T):
        for k in range(ncb):
            ys_ref[k, pl.ds(s, tc, stride=S5_T), :] = yt_ref[s, k * LANES:(k + 1) * LANES, :].T
    for k in range(ncb):
        y_ref[:, k * LANES:(k + 1) * LANES] = ys_ref[k]


def _s5_matrices(lam_re, lam_im, log_step, b_re, b_im, c_re, c_im, d_skip):
    hi = lax.Precision.HIGHEST
    t, gp, n, ng = S5_T, S5_GROUP, S5_STATE, S5_GROUPS
    lam = lax.complex(lam_re.astype(F32), lam_im.astype(F32))
    step = jnp.exp(log_step.astype(F32))[:, None]
    ls = lam * step
    a_bar = jnp.exp(ls)
    b_bar = ((a_bar - 1.0) / lam)[..., None] * lax.complex(b_re.astype(F32), b_im.astype(F32))
    cm = lax.complex(c_re.astype(F32), c_im.astype(F32))

    def apow(k):
        kk = k.astype(F32).astype(jnp.complex64)
        return jnp.exp(ls.reshape((ng,) + (1,) * k.ndim + (n,)) * kk[None, ..., None])

    tt = jnp.arange(t)
    kmat = jnp.einsum('gpn,gln,gnq->glpq', cm, apow(tt), b_bar, precision=hi).real
    krev = jnp.transpose(kmat[:, ::-1], (0, 2, 1, 3)).reshape(ng, gp, t * gp)
    kpad = jnp.pad(krev, ((0, 0), (0, 0), (0, t * gp)))
    mt = jnp.concatenate([kpad[:, :, (t - 1 - to) * gp:(2 * t - 1 - to) * gp] for to in range(t)], axis=1)
    dvec = jnp.tile(d_skip.astype(F32).reshape(ng, 1, gp), (1, t, 1)).reshape(ng, t * gp)
    mt = mt + jnp.eye(t * gp, dtype=F32)[None] * dvec[:, :, None]
    z = jnp.swapaxes(apow(t - 1 - tt), 1, 2)[:, :, :, None] * b_bar[:, :, None, :]
    z = z.reshape(ng, n, t * gp)
    bt = jnp.concatenate([z.real, z.imag], axis=1)
    w = cm[:, None, :, :] * apow(tt + 1)[:, :, None, :]

    def pair_readout(x):
        x = x.reshape(ng // 2, 2, t, gp, n)
        first = jnp.pad(x[:, 0], ((0, 0), (0, 0), (0, 0), (0, n)))
        second = jnp.pad(x[:, 1], ((0, 0), (0, 0), (0, 0), (n, 0)))
        return jnp.stack([first, second], axis=2).reshape(ng // 2, t * 2 * gp, 2 * n).astype(BF16)

    ctr, cti = pair_readout(w.real), pair_readout(-w.imag)
    a_chunk = jnp.transpose(apow(t * (jnp.arange(SUBLANES) + 1)), (1, 0, 2)).reshape(SUBLANES, ng * n)
    return mt.astype(BF16), bt.astype(BF16), ctr, cti, a_chunk.real, a_chunk.imag


def _s5(proj, mats):
    seq, width = proj.shape
    t, tc, gp, n, ng = S5_T, S5_TC, S5_GROUP, S5_STATE, S5_GROUPS
    rows = t * tc
    ncb = S5_CH // LANES
    cb0 = (width - S5_CH) // LANES
    u_specs = [pl.BlockSpec((rows, LANES), (lambda i, k=k: (i, cb0 + k))) for k in range(ncb)]
    nsb = ng * n // LANES
    return pl.pallas_call(
        _s5_kernel,
        grid=(seq // rows,),
        in_specs=u_specs + [_const_spec(m.shape) for m in mats],
        out_specs=pl.BlockSpec((rows, S5_CH), lambda i: (i, 0)),
        out_shape=jax.ShapeDtypeStruct((seq, S5_CH), F32),
        scratch_shapes=[
            pltpu.VMEM((t, S5_CH, tc), F32),
            pltpu.VMEM((t, S5_CH, tc), F32),
            pltpu.VMEM((ncb, rows, LANES), F32),
            pltpu.VMEM((ng * n, tc), F32),
            pltpu.VMEM((ng * n, tc), F32),
            pltpu.VMEM((nsb, tc, LANES), F32),
            pltpu.VMEM((nsb, tc, LANES), F32),
            pltpu.VMEM((2, SUBLANES, ng * n), F32),
        ],
        compiler_params=pltpu.CompilerParams(dimension_semantics=("arbitrary",)),
        name="s5_scan",
    )(*([proj] * ncb), *mats)


def _mix_ffn_kernel(*refs, glu, final):
    (x_ref, a_ref, b_ref, wo_ref, g1_ref), refs = refs[:5], refs[5:]
    if glu:
        gw_ref, refs = refs[0], refs[1:]
    (g_ref, sc_ref, sh_ref, gate_ref, win_ref, cw_ref, cb_ref, wout_ref), refs = refs[:8], refs[8:]
    if final:
        fg_ref, o_ref, h_ref, act_ref, gbuf_ref, carry_ref = refs
    else:
        ng_ref, nsc_ref, nsh_ref, o_ref, hn_ref, h_ref, act_ref, gbuf_ref, carry_ref = refs
    tm = x_ref.shape[0]
    halo = gbuf_ref.shape[0] - tm

    @pl.when(pl.program_id(0) == 0)
    def _():
        carry_ref[...] = jnp.zeros(carry_ref.shape, F32)

    if glu:
        y = jax.nn.gelu(b_ref[...]).astype(BF16)
        gg = _dot(y, gw_ref[...])
        half = gg.shape[1] // 2
        b = (gg[:, :half] * jax.nn.sigmoid(gg[:, half:])).astype(BF16)
    else:
        b = b_ref[...]
    cat = jnp.concatenate([a_ref[...], b], axis=1)
    x = x_ref[...] + g1_ref[...] * _dot(cat, wo_ref[...])
    h_ref[...] = _mod_rmsnorm(x, g_ref[...], sc_ref[...], sh_ref[...]).astype(BF16)
    for f in range(D_FF // TF_FFN):
        cs = slice(f * TF_FFN, (f + 1) * TF_FFN)
        gs = slice(D_FF + f * TF_FFN, D_FF + (f + 1) * TF_FFN)
        h = h_ref[...]
        val = _dot(h, win_ref[:, cs])
        gate = _dot(h, win_ref[:, gs])
        gbuf_ref[0:halo, :] = carry_ref[:, cs]
        gbuf_ref[halo:halo + tm, :] = gate
        carry_ref[:, cs] = gate[tm - halo:tm, :]
        conv = (gate * cw_ref[2:3, cs] + gbuf_ref[halo - 1:halo - 1 + tm, :] * cw_ref[1:2, cs]
                + gbuf_ref[halo - 2:halo - 2 + tm, :] * cw_ref[0:1, cs] + cb_ref[:, cs])
        act_ref[:, cs] = (jax.nn.gelu(conv) * val).astype(BF16)
    xn = x + gate_ref[...] * _dot(act_ref[...], wout_ref[...])
    if final:
        xn = xn * lax.rsqrt(jnp.mean(xn * xn, axis=-1, keepdims=True) + EPS) * fg_ref[...]
    else:
        hn_ref[...] = _mod_rmsnorm(xn, ng_ref[...], nsc_ref[...], nsh_ref[...]).astype(BF16)
    o_ref[...] = xn


def _layer_spec(shape, layer):
    idx = (layer,) + (0,) * (len(shape) - 1)
    return pl.BlockSpec((None,) + tuple(shape[1:]), lambda *_: idx, pipeline_mode=pl.Buffered(1))


def _mix_ffn(x, a, b, wo, gate1, glu_w, g, scale, shift, gate2, w_in, conv_w, conv_b, w_out, tail, layer):
    seq, d = x.shape
    final = len(tail) == 1
    tm = TM_FFN
    halo = SUBLANES
    row = pl.BlockSpec((1, d), lambda i: (0, 0))
    rows = lambda w: pl.BlockSpec((tm, w), lambda i: (i, 0))
    conv_b = conv_b.reshape(conv_b.shape[0], 1, D_FF)
    in_specs = [rows(d), rows(a.shape[1]), rows(b.shape[1]), _const_spec(wo.shape), row]
    args = [x, a, b, wo, gate1]
    if glu_w is not None:
        in_specs.append(_const_spec(glu_w.shape))
        args.append(glu_w)
    in_specs += [
        row, row, row, row,
        _layer_spec(w_in.shape, layer),
        _layer_spec(conv_w.shape, layer),
        _layer_spec(conv_b.shape, layer),
        _layer_spec(w_out.shape, layer),
    ] + [row] * len(tail)
    args += [g.reshape(1, d), scale, shift, gate2, w_in, conv_w, conv_b, w_out]
    args += [t.reshape(1, d) for t in tail]
    out_specs = [rows(d)] if final else [rows(d), rows(d)]
    out_shape = [jax.ShapeDtypeStruct((seq, d), F32)] + ([] if final else [jax.ShapeDtypeStruct((seq, d), BF16)])
    return pl.pallas_call(
        functools.partial(_mix_ffn_kernel, glu=glu_w is not None, final=final),
        grid=(seq // tm,),
        in_specs=in_specs,
        out_specs=out_specs,
        out_shape=out_shape,
        scratch_shapes=[
            pltpu.VMEM((tm, d), BF16),
            pltpu.VMEM((tm, D_FF), BF16),
            pltpu.VMEM((tm + halo, TF_FFN), F32),
            pltpu.VMEM((halo, D_FF), F32),
        ],
        compiler_params=pltpu.CompilerParams(dimension_semantics=("arbitrary",)),
        name="mix_ffn",
    )(*args)


def kernel(x, c, t5_table, mod_w, mod_b, norm1_g, norm2_g, ffn_w_in, ffn_conv_w, ffn_conv_b, ffn_w_out,
           ev_w_in, ev_w_out, diff_lambda, diff_subln_g, band_rel_bias,
           od_w_in, od_w_out, s5_lam_re, s5_lam_im, s5_log_step, s5_b_re, s5_b_im, s5_c_re, s5_c_im,
           s5_d, s5_glu_w, final_g):
    assert x.shape[0] == 1 and x.shape[2] == D_MODEL
    seq = x.shape[1]
    assert seq % TM_PROJ == 0 and seq % (S5_T * S5_TC) == 0
    d = D_MODEL
    xs = x[0]
    mod = _modulation(c, mod_w, mod_b)
    ffn_w_in_b = ffn_w_in.astype(BF16)
    ffn_w_out_b = ffn_w_out.astype(BF16)
    mods = [[mod[i, :, k * d:(k + 1) * d] for k in range(6)] for i in range(DEPTH)]
    h = None
    for i in range(DEPTH):
        sh1, sc1, g1, sh2, sc2, g2 = mods[i]
        w_in = (ev_w_in if i % 2 == 0 else od_w_in)[i // 2].astype(BF16)
        proj_dtype = BF16 if i % 2 == 0 else F32
        if h is None:
            proj = _normproj(xs, norm1_g[i], sc1, sh1, w_in, proj_dtype)
        else:
            proj = _proj(h, w_in, proj_dtype)
        if i % 2 == 0:
            e = i // 2
            lam_init = 0.8 - 0.6 * math.exp(-0.3 * i)
            lp = diff_lambda[e].astype(F32)
            lam = jnp.exp(jnp.sum(lp[0] * lp[1])) - jnp.exp(jnp.sum(lp[2] * lp[3])) + lam_init
            mix_a = _diff_attention(proj, t5_table, lam, diff_subln_g[e], lam_init)
            mix_b = _band_attention(proj, band_rel_bias[e])
            wo, glu_w = ev_w_out[e].astype(BF16), None
        else:
            o = i // 2
            mix_a = _retention(proj)
            mats = _s5_matrices(s5_lam_re[o], s5_lam_im[o], s5_log_step[o], s5_b_re[o], s5_b_im[o],
                                s5_c_re[o], s5_c_im[o], s5_d[o])
            mix_b = _s5(proj, mats)
            wo, glu_w = od_w_out[o].astype(BF16), s5_glu_w[o].astype(BF16)
        if i == DEPTH - 1:
            tail = (final_g,)
        else:
            nsh1, nsc1 = mods[i + 1][0], mods[i + 1][1]
            tail = (norm1_g[i + 1], nsc1, nsh1)
        out = _mix_ffn(xs, mix_a, mix_b, wo, g1, glu_w, norm2_g[i], sc2, sh2, g2,
                       ffn_w_in_b, ffn_conv_w, ffn_conv_b, ffn_w_out_b, tail, layer=i)
        if i == DEPTH - 1:
            xs = out[0]
        else:
            xs, h = out
    return xs[None]
```

```python
import functools
import math

import jax
import jax.numpy as jnp
from jax import lax
from jax.experimental import pallas as pl
from jax.experimental.pallas import tpu as pltpu

F32 = jnp.float32
BF16 = jnp.bfloat16

D_MODEL = 1024
DEPTH = 2
CHUNK = 64
GROUP_WIDTH = D_MODEL // 2
DK_A = 64
DV_A = 2 * DK_A
N_HEADS_A = GROUP_WIDTH // DV_A
DH_B = 64
N_HEADS_B = GROUP_WIDTH // DH_B
LEFT_CHUNKS = 8
REL_CLIP = 2 * CHUNK
NUM_BUCKETS = 32
MAX_DISTANCE = 128
DV_C = 128
DQK_C = DV_C // 2
N_HEADS_C = GROUP_WIDTH // DV_C
ROPE_BASE = 10000.0
S5_CH = GROUP_WIDTH
S5_GROUP = 16
S5_GROUPS = S5_CH // S5_GROUP
S5_STATE = 64
D_FF = ((8 * D_MODEL // 3 + 255) // 256) * 256
CONV_W = 3
EVEN_IN = 3 * N_HEADS_A * DV_A + 3 * N_HEADS_B * DH_B
ODD_IN = 2 * N_HEADS_C * DQK_C + 2 * N_HEADS_C * DV_C + S5_CH
EPS = 1e-6
NEG_INF = -1e30
LOG2E = math.log2(math.e)

LANES = 128
SUBLANES = 8
MXU_DIM = 256

TM_PROJ = 1024
TN_PROJ = 1024
TN_MOD = 1536
TM_FFN = 512
TF_FFN = MXU_DIM
BLK_A = 512
NPART_A = 2
ONES_A = 16
SINGLE_PASS_LOG2_RANGE = 96.0
SCORE_PAD = LANES
BLK_B = 1024
BAND_B = LEFT_CHUNKS * CHUNK
QW_B = 4 * CHUNK
BLK_C = 512
S5_T = 16
S5_TC = LANES

assert BLK_B % BAND_B == 0 and BLK_B % QW_B == 0 and BAND_B % QW_B == 0
assert BLK_A >= MAX_DISTANCE, "far key blocks must sit in the saturated T5 bucket"
assert DV_A == LANES and 2 * DK_A == LANES and 2 * DH_B == LANES, "attention heads are read as 128-lane column blocks"


def _dot(a, b):
    return jnp.dot(a, b, preferred_element_type=F32)


def _dot_nt(a, b):
    return lax.dot_general(a, b, (((1,), (1,)), ((), ())), preferred_element_type=F32)


def _dot_tn(a, b):
    return lax.dot_general(a, b, (((0,), (0,)), ((), ())), preferred_element_type=F32)


def _const_spec(shape):
    zeros = (0,) * len(shape)
    return pl.BlockSpec(shape, lambda *_: zeros, pipeline_mode=pl.Buffered(1))


def _mod_rmsnorm(x, g, scale, shift):
    y = x * lax.rsqrt(jnp.mean(x * x, axis=-1, keepdims=True) + EPS)
    y = y * g
    return y * (1.0 + scale) + shift


def _mod_kernel(c_ref, w_ref, b_ref, o_ref):
    c = c_ref[...]
    cond = c * jax.nn.sigmoid(c)
    o_ref[0] = jnp.sum(cond * w_ref[0], axis=0, keepdims=True) + b_ref[0]


def _modulation(c, mod_w, mod_b):
    depth, d, n = mod_w.shape
    tn = TN_MOD
    return pl.pallas_call(
        _mod_kernel,
        grid=(depth, n // tn),
        in_specs=[
            pl.BlockSpec((d, 1), lambda i, j: (0, 0)),
            pl.BlockSpec((1, d, tn), lambda i, j: (i, 0, j)),
            pl.BlockSpec((1, 1, tn), lambda i, j: (i, 0, j)),
        ],
        out_specs=pl.BlockSpec((1, 1, tn), lambda i, j: (i, 0, j)),
        out_shape=jax.ShapeDtypeStruct((depth, 1, n), F32),
        name="modulation",
    )(c.reshape(d, 1), mod_w, mod_b.reshape(depth, 1, n))


def _normproj_kernel(x_ref, g_ref, sc_ref, sh_ref, w_ref, o_ref):
    tm, n = o_ref.shape
    half = tm // 2
    for r in range(2):
        rows = slice(r * half, (r + 1) * half)
        h = _mod_rmsnorm(x_ref[rows, :], g_ref[...], sc_ref[...], sh_ref[...]).astype(BF16)
        for j in range(n // TN_PROJ):
            cols = slice(j * TN_PROJ, (j + 1) * TN_PROJ)
            o_ref[rows, cols] = _dot(h, w_ref[:, cols]).astype(o_ref.dtype)


def _normproj(x, g, scale, shift, w, out_dtype):
    seq, d = x.shape
    n = w.shape[1]
    tm = TM_PROJ
    row = pl.BlockSpec((1, d), lambda i: (0, 0))
    return pl.pallas_call(
        _normproj_kernel,
        grid=(seq // tm,),
        in_specs=[pl.BlockSpec((tm, d), lambda i: (i, 0)), row, row, row, _const_spec(w.shape)],
        out_specs=pl.BlockSpec((tm, n), lambda i: (i, 0)),
        out_shape=jax.ShapeDtypeStruct((seq, n), out_dtype),
        compiler_params=pltpu.CompilerParams(dimension_semantics=("parallel",)),
        name="normproj",
    )(x, g.reshape(1, d), scale, shift, w)


def _proj_kernel(h_ref, w_ref, o_ref):
    for j in range(o_ref.shape[1] // TN_PROJ):
        cols = slice(j * TN_PROJ, (j + 1) * TN_PROJ)
        o_ref[:, cols] = _dot(h_ref[...], w_ref[:, cols]).astype(o_ref.dtype)


def _proj(h, w, out_dtype):
    seq, d = h.shape
    n = w.shape[1]
    tm = TM_PROJ
    return pl.pallas_call(
        _proj_kernel,
        grid=(seq // tm,),
        in_specs=[pl.BlockSpec((tm, d), lambda i: (i, 0)), _const_spec(w.shape)],
        out_specs=pl.BlockSpec((tm, n), lambda i: (i, 0)),
        out_shape=jax.ShapeDtypeStruct((seq, n), out_dtype),
        compiler_params=pltpu.CompilerParams(dimension_semantics=("parallel",)),
        name="proj",
    )(h, w)


def _diffattn_kernel(q_ref, qall_ref, k_ref, v_ref, bias_ref, bstat_ref, lam_ref, g_ref, o_ref,
                     flag_ref, qs_ref, vt_ref, kmax_ref, r_ref, m_ref, acc_ref, *s_refs, out_scale):
    blk = BLK_A
    nq = 2 * blk
    sub = SUBLANES
    dv = DV_A
    npart = NPART_A
    sa_ref, sb_ref = s_refs[:2 * npart], s_refs[2 * npart:4 * npart]
    pa_ref, pb_ref = s_refs[4 * npart:5 * npart], s_refs[5 * npart:6 * npart]
    i = pl.program_id(1)
    lane = lax.broadcasted_iota(jnp.int32, (blk, LANES), 1)
    same_subhead = (lax.broadcasted_iota(jnp.int32, (LANES, LANES), 0) // DK_A
                    == lax.broadcasted_iota(jnp.int32, (LANES, LANES), 1) // DK_A).astype(BF16)

    bias_max, bias_span = bstat_ref[0, 0:1, 0:1], bstat_ref[0, 1:2, 0:1]
    q_scale = DK_A ** -0.5 * LOG2E

    @pl.when(i == 0)
    def _():
        kmax_ref[...] = jnp.zeros(kmax_ref.shape, F32)

        def tr(b, qmax):
            r0 = pl.multiple_of(b * blk, blk)
            vt_ref[0:dv, pl.ds(r0, blk)] = v_ref[pl.ds(r0, blk), :].astype(F32).T.astype(BF16)
            vt_ref[dv:dv + ONES_A, pl.ds(r0, blk)] = jnp.ones((ONES_A, blk), BF16)
            kf = k_ref[pl.ds(r0, blk), :].astype(F32)
            kn2 = _dot((kf * kf).astype(BF16), same_subhead)
            kmax_ref[...] = jnp.maximum(kmax_ref[...], jnp.max(kn2.reshape(blk // sub, sub, LANES), axis=0))
            qa = (qall_ref[pl.ds(r0, blk), :].astype(F32) * q_scale).astype(BF16).astype(F32)
            qn2 = _dot((qa * qa).astype(BF16), same_subhead)
            return jnp.maximum(qmax, jnp.max(qn2.reshape(blk // sub, sub, LANES), axis=0))
        qmax2 = lax.fori_loop(0, v_ref.shape[0] // blk, tr, jnp.zeros((sub, LANES), F32))
        kmax2 = jnp.max(kmax_ref[...], axis=0, keepdims=True)
        kmax_ref[...] = jnp.broadcast_to(kmax2, kmax_ref.shape)
        worst = 2.0 * jnp.sqrt(jnp.max(qmax2, axis=0, keepdims=True) * kmax2) * 1.03 * 1.03 + bias_span
        flag_ref[0] = (jnp.max(worst) < SINGLE_PASS_LOG2_RANGE).astype(jnp.int32)

    q = (q_ref[...].astype(F32) * q_scale).astype(BF16)
    qf = q.astype(F32)
    qs_ref[:, 0:blk] = jnp.where(lane < DK_A, qf, 0.0).T.astype(BF16)
    qs_ref[:, blk:nq] = jnp.where(lane >= DK_A, qf, 0.0).T.astype(BF16)
    acc_ref[...] = jnp.zeros(acc_ref.shape, F32)

    qt = qs_ref[...].astype(F32)
    qn2 = jnp.sum((qt * qt).reshape(LANES // sub, sub, nq), axis=0)
    qn2 = jnp.broadcast_to(jnp.sum(qn2, axis=0, keepdims=True), (sub, nq))
    kmax2 = jnp.concatenate([jnp.broadcast_to(kmax_ref[:, m * DK_A:m * DK_A + 1], (sub, blk)) for m in range(2)],
                            axis=1)
    bound = jnp.sqrt(qn2 * kmax2) * 1.03
    r_ref[...] = bound + bias_max
    single_pass = flag_ref[0] == 1

    @pl.when(single_pass)
    def _():
        _diffattn_fixed_shift(i, k_ref, bias_ref, qs_ref, vt_ref, r_ref, m_ref, acc_ref, pa_ref, pb_ref)

    @pl.when(jnp.logical_not(single_pass))
    def _():
        _diffattn_online(i, k_ref, bias_ref, qs_ref, vt_ref, m_ref, acc_ref, sa_ref, sb_ref)

    ot = acc_ref[0:dv, 0:nq] / acc_ref[dv:dv + 1, 0:nq]
    o = ot[:, 0:blk].T - lam_ref[...] * ot[:, blk:nq].T
    o = o * lax.rsqrt(jnp.mean(o * o, axis=-1, keepdims=True) + EPS) * g_ref[...]
    o_ref[...] = (o * out_scale).astype(o_ref.dtype)


def _diffattn_fixed_shift(i, k_ref, bias_ref, qs_ref, vt_ref, shift_ref, l_ref, acc_ref, pa_ref, pb_ref):
    blk = BLK_A
    nq = 2 * blk
    sub = SUBLANES
    npart = len(pa_ref)
    wq = nq // npart
    l_ref[...] = jnp.zeros(l_ref.shape, F32)

    def probs(b, p_ref, bias):
        k = k_ref[pl.ds(pl.multiple_of(b * blk, blk), blk), :]
        for part in range(npart):
            cols = slice(part * wq, (part + 1) * wq)
            s = _dot(k, qs_ref[:, cols])
            if bias is not None:
                q0 = (part * wq) % blk
                s = s + bias[:, q0:q0 + wq]
            p = jnp.exp2(s.reshape(blk // sub, sub, wq) - shift_ref[:, cols][None])
            l_ref[:, cols] += jnp.sum(p, axis=0)
            p_ref[part][:, 0:wq] = p.reshape(blk, wq).astype(BF16)

    def accumulate(b, p_ref):
        vt = vt_ref[0:DV_A, pl.ds(pl.multiple_of(b * blk, blk), blk)]
        for part in range(npart):
            cols = slice(part * wq, (part + 1) * wq)
            acc_ref[0:DV_A, cols] += _dot(vt, p_ref[part][:, 0:wq])

    @pl.when(i == 0)
    def _():
        probs(0, pa_ref, bias_ref[0, 1])
        accumulate(0, pa_ref)

    @pl.when(i > 0)
    def _():
        nfar = i - 1
        probs(i, pa_ref, bias_ref[0, 1])
        probs(i - 1, pb_ref, bias_ref[0, 0])
        accumulate(i, pa_ref)

        def pair(t):
            probs(2 * t, pa_ref, None)
            accumulate(jnp.where(t == 0, i - 1, 2 * t - 1), pb_ref)
            probs(2 * t + 1, pb_ref, None)
            accumulate(2 * t, pa_ref)

        def four_pairs(u, carry):
            for v in range(4):
                pair(4 * u + v)
            return carry

        npairs = nfar // 2
        lax.fori_loop(0, npairs // 4, four_pairs, 0)

        def one_pair(t, carry):
            pair(t)
            return carry
        lax.fori_loop(4 * (npairs // 4), npairs, one_pair, 0)
        in_pb = jnp.where(npairs == 0, i - 1, 2 * npairs - 1)

        @pl.when(lax.rem(nfar, 2) == 1)
        def _():
            probs(nfar - 1, pa_ref, None)
            accumulate(in_pb, pb_ref)
            accumulate(nfar - 1, pa_ref)

        @pl.when(lax.rem(nfar, 2) == 0)
        def _():
            accumulate(in_pb, pb_ref)

    acc_ref[DV_A:DV_A + sub, 0:nq] = jnp.broadcast_to(jnp.sum(l_ref[...], axis=0, keepdims=True), (sub, nq))


def _diffattn_online(i, k_ref, bias_ref, qs_ref, vt_ref, m_ref, acc_ref, sa_ref, sb_ref):
    blk = BLK_A
    nq = 2 * blk
    sub = SUBLANES
    npart = len(sa_ref) // 2
    wq = nq // npart
    m_ref[...] = jnp.full(m_ref.shape, NEG_INF, F32)

    def scores(b, s_ref):
        k = k_ref[pl.ds(pl.multiple_of(b * blk, blk), blk), :]
        for part in range(npart):
            s = _dot(k, qs_ref[:, part * wq:(part + 1) * wq])
            s_ref[part][:, 0:wq] = s
            s_ref[npart + part][...] = jnp.max(s.reshape(blk // sub, sub, wq), axis=0)

    def softmax_pv(b, s_ref, bias):
        vt = vt_ref[:, pl.ds(pl.multiple_of(b * blk, blk), blk)]
        for part in range(npart):
            cols = slice(part * wq, (part + 1) * wq)
            s = s_ref[part][:, 0:wq]
            if bias is not None:
                q0 = (part * wq) % blk
                s = s + bias[:, q0:q0 + wq]
            s = s.reshape(blk // sub, sub, wq)
            m_prev = m_ref[:, cols]
            smax = jnp.max(s, axis=0) if bias is not None else s_ref[npart + part][...]
            m_cur = jnp.max(smax, axis=0, keepdims=True)
            m_new = jnp.maximum(m_prev, m_cur)
            alpha = jnp.exp2(m_prev - m_new)
            p = jnp.exp2(s - m_new[None])
            pv = _dot(vt, p.reshape(blk, wq).astype(BF16))
            acc_ref[:, cols] = acc_ref[:, cols] * alpha[0:1] + pv
            m_ref[:, cols] = m_new

    nfar = jnp.maximum(i - 1, 0)
    odd = lax.rem(nfar, 2)

    @pl.when(i == 0)
    def _():
        scores(0, sb_ref)

    @pl.when(i > 0)
    def _():
        @pl.when(odd == 1)
        def _():
            scores(0, sb_ref)
            scores(1, sa_ref)
            softmax_pv(0, sb_ref, None)

        @pl.when(odd == 0)
        def _():
            scores(0, sa_ref)

        def pair(b):
            scores(b + 1, sb_ref)
            softmax_pv(b, sa_ref, None)
            scores(b + 2, sa_ref)
            softmax_pv(b + 1, sb_ref, None)

        def quad_body(t, carry):
            pair(odd + 4 * t)
            pair(odd + 4 * t + 2)
            return carry

        npairs = nfar // 2
        lax.fori_loop(0, npairs // 2, quad_body, 0)

        @pl.when(lax.rem(npairs, 2) == 1)
        def _():
            pair(odd + 2 * (npairs - 1))
        scores(i, sb_ref)
        softmax_pv(i - 1, sa_ref, bias_ref[0, 0])

    softmax_pv(i, sb_ref, bias_ref[0, 1])


_TOEPLITZ_ROWS = 256
_TOEPLITZ_N = 2048


def _toeplitz_kernel(v_ref, o_ref, *, keep):
    rows, cols = o_ref.shape[2:]
    x = jnp.broadcast_to(v_ref[0, 0], (rows, v_ref.shape[-1]))
    tile = pltpu.roll(x, 0, 1, stride=1, stride_axis=0)[:, :cols]
    r = lax.broadcasted_iota(jnp.int32, (rows, cols), 0) + pl.program_id(1) * rows
    c = lax.broadcasted_iota(jnp.int32, (rows, cols), 1)
    for variant in range(o_ref.shape[0]):
        o_ref[variant, 0] = jnp.where(keep(r, c, variant), tile, NEG_INF)


def _toeplitz_tiles(fn, keep, heads, rows, cols, variants=1):
    n, rb = _TOEPLITZ_N, _TOEPLITZ_ROWS
    assert rows % rb == 0 and rows <= n // 2 and cols <= n // 2
    idx = jnp.arange(n, dtype=jnp.int32)
    vec = fn(jnp.where(idx < n // 2, idx, idx - n)).astype(F32)
    vecs = jnp.stack([jnp.roll(vec, k * rb, axis=1) for k in range(rows // rb)], axis=1)
    return pl.pallas_call(
        functools.partial(_toeplitz_kernel, keep=keep),
        grid=(heads, rows // rb),
        in_specs=[pl.BlockSpec((1, 1, 1, n), lambda h, k: (h, k, 0, 0))],
        out_specs=pl.BlockSpec((variants, 1, rb, cols), lambda h, k: (0, h, k, 0)),
        out_shape=jax.ShapeDtypeStruct((variants, heads, rows, cols), F32),
        name="toeplitz_tiles",
    )(vecs.reshape(heads, rows // rb, 1, n))


def _bias_stats(tiles):
    finite = tiles > 0.5 * NEG_INF
    bias_max = jnp.maximum(jnp.max(jnp.where(finite, tiles, NEG_INF), axis=(1, 2)), 0.0)
    bias_min = jnp.minimum(jnp.min(jnp.where(finite, tiles, -NEG_INF), axis=(1, 2)), 0.0)
    return jnp.broadcast_to(jnp.stack([bias_max, bias_max - bias_min], axis=1)[:, :, None],
                            (tiles.shape[0], 2, LANES))


def _t5_bucket(rel):
    nb = NUM_BUCKETS // 2
    max_exact = nb // 2
    bucket = jnp.where(rel > 0, nb, 0)
    n = jnp.abs(rel)
    nf = jnp.maximum(n, 1).astype(F32)
    large = max_exact + (jnp.log(nf / max_exact) / math.log(MAX_DISTANCE / max_exact)
                         * (nb - max_exact)).astype(jnp.int32)
    large = jnp.minimum(large, nb - 1)
    return bucket + jnp.where(n < max_exact, n, large)


def _diff_bias_tiles(t5_table):
    blk = BLK_A
    table = t5_table.astype(F32)
    far = table[_t5_bucket(jnp.full((), -(blk + 1), jnp.int32))]
    def visible(r, c, variant):
        return jnp.floor_divide(r - blk, CHUNK) <= jnp.floor_divide(c, CHUNK)

    tiles = _toeplitz_tiles(lambda x: ((table[_t5_bucket(-x - blk)] - far) * LOG2E).T, visible,
                            N_HEADS_A, 2 * blk, blk)
    return tiles.reshape(N_HEADS_A, 2, blk, blk)


def _diff_attention(proj, t5_table, lam, subln_g, lam_init):
    seq = proj.shape[0]
    blk = BLK_A
    bias = _diff_bias_tiles(t5_table)
    ha = N_HEADS_A
    bstat = _bias_stats(bias.reshape(ha, 2 * blk, blk))
    kern = functools.partial(_diffattn_kernel, out_scale=1.0 - lam_init)
    return pl.pallas_call(
        kern,
        grid=(ha, seq // blk),
        in_specs=[
            pl.BlockSpec((blk, DV_A), lambda h, i: (i, h)),
            pl.BlockSpec((seq, DV_A), lambda h, i: (0, h)),
            pl.BlockSpec((seq, DV_A), lambda h, i: (0, ha + h)),
            pl.BlockSpec((seq, DV_A), lambda h, i: (0, 2 * ha + h)),
            pl.BlockSpec((1, 2, blk, blk), lambda h, i: (h, 0, 0, 0)),
            pl.BlockSpec((1, 2, LANES), lambda h, i: (h, 0, 0)),
            pl.BlockSpec((1, DV_A), lambda h, i: (0, 0)),
            pl.BlockSpec((1, DV_A), lambda h, i: (0, 0)),
        ],
        out_specs=pl.BlockSpec((blk, DV_A), lambda h, i: (i, h)),
        out_shape=jax.ShapeDtypeStruct((seq, ha * DV_A), BF16),
        scratch_shapes=[
            pltpu.SMEM((1,), jnp.int32),
            pltpu.VMEM((DV_A, 2 * blk), BF16),
            pltpu.VMEM((DV_A + ONES_A, seq), BF16),
            pltpu.VMEM((SUBLANES, LANES), F32),
            pltpu.VMEM((SUBLANES, 2 * blk), F32),
            pltpu.VMEM((SUBLANES, 2 * blk), F32),
            pltpu.VMEM((DV_A + ONES_A, 2 * blk), F32),
        ] + 2 * ([pltpu.VMEM((blk, 2 * blk // NPART_A + SCORE_PAD), F32)] * NPART_A
                 + [pltpu.VMEM((SUBLANES, 2 * blk // NPART_A), F32)] * NPART_A)
        + 2 * [pltpu.VMEM((blk, 2 * blk // NPART_A + SCORE_PAD), BF16)] * NPART_A,
        compiler_params=pltpu.CompilerParams(dimension_semantics=("parallel", "arbitrary")),
        name="diff_attention",
    )(proj, proj, proj, proj, bias, bstat, jnp.full((1, DV_A), lam, F32), subln_g.reshape(1, DV_A).astype(F32))


def _band_kernel(q_ref, qall_ref, kall_ref, kp_ref, kc_ref, vp_ref, vc_ref, *refs):
    qw, band = QW_B, BAND_B
    nbias = band // qw + 1
    bias_refs, bstat_ref, o_ref, flag_ref, kmax_ref = refs[:nbias], *refs[nbias:nbias + 4]
    refs = refs[nbias + 4:]
    ngroups = len(refs) // 2
    s_refs, p_refs = refs[:ngroups], refs[ngroups:]
    nk = band + qw
    sub = SUBLANES
    q = (q_ref[...].astype(F32) * (DH_B ** -0.5 * LOG2E)).astype(BF16).astype(F32)
    lane = lax.broadcasted_iota(jnp.int32, q.shape, 1)
    qt = (jnp.where(lane < DH_B, q, 0.0).T, jnp.where(lane >= DH_B, q, 0.0).T)
    qh = (qt[0].astype(BF16), qt[1].astype(BF16))
    k_all = jnp.concatenate([kp_ref[...], kc_ref[...]], axis=0)
    vt_all = jnp.concatenate([vp_ref[...], vc_ref[...]], axis=0).astype(F32).T.astype(BF16)

    def group_operands(g):
        k0 = g * qw
        qs = jnp.concatenate([qh[0][:, k0:k0 + qw], qh[1][:, k0:k0 + qw]], axis=1)
        bias_ref = bias_refs[min(g, nbias - 1)]
        bias = jnp.concatenate([bias_ref[0, 0], bias_ref[0, 1]], axis=1)
        return k0, qs, bias

    def store_group(g, ot):
        o = jnp.concatenate([ot[0:DH_B, 0:qw], ot[DH_B:2 * DH_B, qw:2 * qw]], axis=0)
        o_ref[g * qw:(g + 1) * qw, :] = o.T.astype(o_ref.dtype)

    @pl.when(pl.program_id(1) == 0)
    def _():
        same_head = (lax.broadcasted_iota(jnp.int32, (LANES, LANES), 0) // DH_B
                     == lax.broadcasted_iota(jnp.int32, (LANES, LANES), 1) // DH_B).astype(BF16)
        rows = BLK_B

        def norms(b, carry):
            qmax, kmax = carry
            r0 = pl.multiple_of(b * rows, rows)
            kf = kall_ref[pl.ds(r0, rows), :].astype(F32)
            qa = (qall_ref[pl.ds(r0, rows), :].astype(F32) * (DH_B ** -0.5 * LOG2E)).astype(BF16).astype(F32)
            kn2 = _dot((kf * kf).astype(BF16), same_head).reshape(rows // sub, sub, LANES)
            qn2 = _dot((qa * qa).astype(BF16), same_head).reshape(rows // sub, sub, LANES)
            return jnp.maximum(qmax, jnp.max(qn2, axis=0)), jnp.maximum(kmax, jnp.max(kn2, axis=0))
        zero = jnp.zeros((sub, LANES), F32)
        qmax, kmax = lax.fori_loop(0, qall_ref.shape[0] // rows, norms, (zero, zero))
        qmax, kmax = jnp.max(qmax, axis=0, keepdims=True), jnp.max(kmax, axis=0, keepdims=True)
        kmax_ref[...] = jnp.broadcast_to(kmax, kmax_ref.shape)
        span = jnp.concatenate([jnp.broadcast_to(bstat_ref[m, 1:2, 0:1], (1, DH_B)) for m in range(2)], axis=1)
        worst = 2.0 * jnp.sqrt(qmax * kmax) * 1.03 * 1.03 + span
        flag_ref[0] = (jnp.max(worst) < SINGLE_PASS_LOG2_RANGE).astype(jnp.int32)

    shifts = []
    for m in range(2):
        qn2 = jnp.sum(qt[m] * qt[m], axis=0, keepdims=True)
        shifts.append(jnp.sqrt(qn2 * kmax_ref[0:1, m * DH_B:m * DH_B + 1]) * 1.03 + bstat_ref[m, 0:1, 0:1])
    fixed_shift = flag_ref[0] == 1

    @pl.when(fixed_shift)
    def _():
        sums = []
        for g in range(ngroups):
            k0, qs, bias = group_operands(g)
            r = jnp.concatenate([shifts[0][:, k0:k0 + qw], shifts[1][:, k0:k0 + qw]], axis=1)
            s = _dot(k_all[k0:k0 + nk], qs) + (bias - r)
            p = jnp.exp2(s).reshape(nk // sub, sub, 2 * qw)
            sums.append(jnp.sum(jnp.sum(p, axis=0), axis=0, keepdims=True))
            p_refs[g][:, 0:2 * qw] = p.reshape(nk, 2 * qw).astype(BF16)
        for g in range(ngroups):
            k0 = g * qw
            store_group(g, _dot(vt_all[:, k0:k0 + nk], p_refs[g][:, 0:2 * qw]) / sums[g])

    @pl.when(jnp.logical_not(fixed_shift))
    def _():
        vt_ones = jnp.concatenate([vt_all, jnp.ones((ONES_A, vt_all.shape[1]), BF16)], axis=0)
        for g in range(ngroups):
            k0, qs, _ = group_operands(g)
            s_refs[g][:, 0:2 * qw] = _dot(k_all[k0:k0 + nk], qs)
        for g in range(ngroups):
            k0, _, bias = group_operands(g)
            s = (s_refs[g][:, 0:2 * qw] + bias).reshape(nk // sub, sub, 2 * qw)
            m = jnp.max(jnp.max(s, axis=0), axis=0, keepdims=True)
            p = jnp.exp2(s - m[None])
            pv = _dot(vt_ones[:, k0:k0 + nk], p.reshape(nk, 2 * qw).astype(BF16))
            store_group(g, pv[0:2 * DH_B] / pv[2 * DH_B:2 * DH_B + 1])


def _band_bias_tiles(rel_bias):
    band = BAND_B

    def valid(r, c, variant):
        qchunk = jnp.floor_divide(c, CHUNK)
        kchunk = jnp.floor_divide(r - band, CHUNK)
        missing = jnp.where(variant == 0, 0, band - (variant - 1) * QW_B)
        return (kchunk <= qchunk) & (kchunk >= qchunk - LEFT_CHUNKS) & (r >= missing)

    return _toeplitz_tiles(
        lambda x: rel_bias.astype(F32)[:, jnp.clip(-x - band, -REL_CLIP, REL_CLIP) + REL_CLIP] * LOG2E, valid,
        N_HEADS_B, band + QW_B, QW_B, variants=1 + band // QW_B)


def _band_attention(proj, rel_bias):
    seq = proj.shape[0]
    blk, band, qw = BLK_B, BAND_B, QW_B
    bias = _band_bias_tiles(rel_bias)
    npair = N_HEADS_B // 2
    qc0 = 3 * N_HEADS_A
    per = blk // band
    prev = lambda c0: (lambda hp, i: (jnp.maximum(i * per - 1, 0), c0 + hp))
    cur = lambda c0: (lambda hp, i: (i, c0 + hp))
    return pl.pallas_call(
        _band_kernel,
        grid=(npair, seq // blk),
        in_specs=[
            pl.BlockSpec((blk, LANES), cur(qc0)),
            pl.BlockSpec((seq, LANES), lambda hp, i: (0, qc0 + hp)),
            pl.BlockSpec((seq, LANES), lambda hp, i: (0, qc0 + npair + hp)),
            pl.BlockSpec((band, LANES), prev(qc0 + npair)),
            pl.BlockSpec((blk, LANES), cur(qc0 + npair)),
            pl.BlockSpec((band, LANES), prev(qc0 + 2 * npair)),
            pl.BlockSpec((blk, LANES), cur(qc0 + 2 * npair)),
        ] + [
            pl.BlockSpec((1, 2, band + qw, qw), (lambda hp, i, t=t: (jnp.where(i == 0, 1 + t, 0), hp, 0, 0)))
            for t in range(band // qw)
        ] + [
            pl.BlockSpec((1, 2, band + qw, qw), lambda hp, i: (0, hp, 0, 0)),
            pl.BlockSpec((2, 2, LANES), lambda hp, i: (hp, 0, 0)),
        ],
        out_specs=pl.BlockSpec((blk, LANES), lambda hp, i: (i, hp)),
        out_shape=jax.ShapeDtypeStruct((seq, N_HEADS_B * DH_B), BF16),
        scratch_shapes=([pltpu.SMEM((1,), jnp.int32),
                         pltpu.VMEM((SUBLANES, LANES), F32)]
                        + [pltpu.VMEM((band + qw, 2 * qw + SCORE_PAD), F32)] * (blk // qw)
                        + [pltpu.VMEM((band + qw, 2 * qw + SCORE_PAD), BF16)] * (blk // qw)),
        compiler_params=pltpu.CompilerParams(dimension_semantics=("parallel", "arbitrary")),
        name="band_attention",
    )(proj, proj, proj, proj, proj, proj, proj, *([bias] * bias.shape[0]), _bias_stats(bias[0]))


def _retention_kernel(qk_ref, v_ref, gate_ref, cos_ref, sin_ref, qdec_ref, kdec_ref, dmat_ref,
                      sdec_ref, o_ref, state_ref):
    @pl.when(pl.program_id(0) == 0)
    def _():
        state_ref[...] = jnp.zeros(state_ref.shape, F32)

    cos = cos_ref[...]
    sin = sin_ref[...]
    lane = lax.broadcasted_iota(jnp.int32, cos.shape, 1)
    first_half = (lane % DQK_C) < (DQK_C // 2)
    qk = qk_ref[...]
    parts = []
    for j in range(qk.shape[1] // LANES):
        t = qk[:, j * LANES:(j + 1) * LANES]
        partner = jnp.where(first_half, pltpu.roll(t, LANES - DQK_C // 2, 1), pltpu.roll(t, DQK_C // 2, 1))
        parts.append(t * cos + partner * sin)
    wq = N_HEADS_C * DQK_C
    q = jnp.concatenate(parts[:wq // LANES], axis=1)
    k = jnp.concatenate(parts[wq // LANES:], axis=1) * (DQK_C ** -0.5)
    qd = (q * qdec_ref[...]).astype(BF16)
    kd = (k * kdec_ref[...]).astype(BF16)
    qb = q.astype(BF16)
    kb = k.astype(BF16)
    vb = v_ref[...].astype(BF16)
    gate = gate_ref[...]
    outs = []
    for h in range(N_HEADS_C):
        qs = slice(h * DQK_C, (h + 1) * DQK_C)
        vs = slice(h * DV_C, (h + 1) * DV_C)
        scores = _dot_nt(qb[:, qs], kb[:, qs]) * dmat_ref[h]
        state = state_ref[h]
        r = _dot(scores.astype(BF16), vb[:, vs]) + _dot(qd[:, qs], state.astype(BF16))
        state_ref[h] = state * sdec_ref[h] + _dot_tn(kd[:, qs], vb[:, vs])
        r = r * lax.rsqrt(jnp.mean(r * r, axis=-1, keepdims=True) + EPS)
        g = gate[:, vs]
        outs.append(r * (g * jax.nn.sigmoid(g)))
    o_ref[...] = jnp.concatenate(outs, axis=1).astype(o_ref.dtype)


def _retention_tables(seq):
    t = BLK_C
    half = DQK_C // 2
    inv_freq = 1.0 / (ROPE_BASE ** (jnp.arange(0, DQK_C, 2, dtype=F32) / DQK_C))
    ang = jnp.arange(seq, dtype=F32)[:, None] * inv_freq[None, :]
    reps = LANES // half
    cos = jnp.tile(jnp.cos(ang), (1, reps))
    sign = jnp.where((jnp.arange(LANES) % DQK_C) < half, -1.0, 1.0).astype(F32)
    sin = jnp.tile(jnp.sin(ang), (1, reps)) * sign[None, :]
    log_g = jnp.log(1.0 - jnp.power(2.0, -5.0 - jnp.arange(N_HEADS_C, dtype=F32)))
    pos = jnp.arange(t, dtype=F32)
    diff = pos[:, None] - pos[None, :]
    same_or_past = (jnp.arange(t)[None, :] // CHUNK) <= (jnp.arange(t)[:, None] // CHUNK)
    dmat = jnp.where(same_or_past[None], jnp.exp(log_g[:, None, None] * jnp.abs(diff)[None]), 0.0)
    qdec = jnp.repeat(jnp.exp(log_g[None, :] * (pos[:, None] + 1.0)), DQK_C, axis=1)
    kdec = jnp.repeat(jnp.exp(log_g[None, :] * (t - 1.0 - pos[:, None])), DQK_C, axis=1)
    sdec = jnp.broadcast_to(jnp.exp(log_g * t)[:, None, None], (N_HEADS_C, 1, DV_C))
    return cos, sin, qdec, kdec, dmat, sdec


def _retention(proj):
    seq = proj.shape[0]
    t = BLK_C
    cos, sin, qdec, kdec, dmat, sdec = _retention_tables(seq)
    wv = N_HEADS_C * DV_C
    return pl.pallas_call(
        _retention_kernel,
        grid=(seq // t,),
        in_specs=[
            pl.BlockSpec((t, wv), lambda i: (i, 0)),
            pl.BlockSpec((t, wv), lambda i: (i, 1)),
            pl.BlockSpec((t, wv), lambda i: (i, 2)),
            pl.BlockSpec((t, LANES), lambda i: (i, 0)),
            pl.BlockSpec((t, LANES), lambda i: (i, 0)),
            pl.BlockSpec((t, N_HEADS_C * DQK_C), lambda i: (0, 0)),
            pl.BlockSpec((t, N_HEADS_C * DQK_C), lambda i: (0, 0)),
            pl.BlockSpec((N_HEADS_C, t, t), lambda i: (0, 0, 0)),
            pl.BlockSpec((N_HEADS_C, 1, DV_C), lambda i: (0, 0, 0)),
        ],
        out_specs=pl.BlockSpec((t, wv), lambda i: (i, 0)),
        out_shape=jax.ShapeDtypeStruct((seq, wv), BF16),
        scratch_shapes=[pltpu.VMEM((N_HEADS_C, DQK_C, DV_C), F32)],
        compiler_params=pltpu.CompilerParams(dimension_semantics=("arbitrary",)),
        name="retention",
    )(proj, proj, proj, cos, sin, qdec, kdec, dmat, sdec)


def _s5_kernel(*refs):
    ncb = S5_CH // LANES
    u_refs = refs[:ncb]
    (mt_ref, bt_ref, ctr_ref, cti_ref, are_ref, aim_ref, y_ref,
     ut_ref, yt_ref, ys_ref, vr_ref, vi_ref, spr_ref, spi_ref, carry_ref) = refs[ncb:]
    tc = S5_TC
    gp = S5_GROUP
    n = S5_STATE
    ng = S5_GROUPS

    @pl.when(pl.program_id(0) == 0)
    def _():
        carry_ref[...] = jnp.zeros(carry_ref.shape, F32)

    for s in range(S5_T):
        for k in range(ncb):
            ut_ref[s, k * LANES:(k + 1) * LANES, :] = u_refs[k][pl.ds(s, tc, stride=S5_T), :].T

    unroll = 4

    def intra(it, carry):
        for k in range(unroll):
            g = it * unroll + k
            r0 = pl.multiple_of(g * gp, gp)
            ug = ut_ref[:, pl.ds(r0, gp), :].reshape(S5_T * gp, tc).astype(BF16)
            yt_ref[:, pl.ds(r0, gp), :] = _dot(mt_ref[g], ug).reshape(S5_T, gp, tc)
            vt = _dot(bt_ref[g], ug)
            n0 = pl.multiple_of(g * n, n)
            vr_ref[pl.ds(n0, n), :] = vt[0:n]
            vi_ref[pl.ds(n0, n), :] = vt[n:2 * n]
        return carry

    lax.fori_loop(0, ng // unroll, intra, 0)

    sub = SUBLANES
    nv = tc // sub
    row = lax.broadcasted_iota(jnp.int32, (tc, LANES), 0)
    in_vreg = lax.rem(row, sub)

    def rows_of(v, r):
        return jnp.broadcast_to(v[r:r + 1], (tc, LANES))

    for j in range(ng * n // LANES):
        cols = slice(j * LANES, (j + 1) * LANES)
        pwr, pwi = are_ref[:, cols], aim_ref[:, cols]
        xr = vr_ref[cols, :].T
        xi = vi_ref[cols, :].T
        for d in (1, 2, 4):
            keep = in_vreg >= d
            sr = jnp.where(keep, pltpu.roll(xr, d, 0), 0.0)
            si = jnp.where(keep, pltpu.roll(xi, d, 0), 0.0)
            fr, fi = rows_of(pwr, d - 1), rows_of(pwi, d - 1)
            xr, xi = xr + (fr * sr - fi * si), xi + (fr * si + fi * sr)
        cr, ci = carry_ref[0, :, cols], carry_ref[1, :, cols]
        cr0, ci0 = cr, ci
        outr, outi = [], []
        for v in range(nv):
            yr = xr[v * sub:(v + 1) * sub] + (pwr * cr - pwi * ci)
            yi = xi[v * sub:(v + 1) * sub] + (pwr * ci + pwi * cr)
            outr.append(yr)
            outi.append(yi)
            cr = jnp.broadcast_to(yr[sub - 1:sub], (sub, LANES))
            ci = jnp.broadcast_to(yi[sub - 1:sub], (sub, LANES))
        carry_ref[0, :, cols] = cr
        carry_ref[1, :, cols] = ci
        sr = jnp.concatenate(outr, axis=0)
        si = jnp.concatenate(outi, axis=0)
        first = row == 0
        spr_ref[j] = jnp.where(first, rows_of(cr0, 0), pltpu.roll(sr, 1, 0))
        spi_ref[j] = jnp.where(first, rows_of(ci0, 0), pltpu.roll(si, 1, 0))

    def cross(it, carry):
        for k in range(unroll):
            jp = it * unroll + k
            r0 = pl.multiple_of(jp * 2 * gp, 2 * gp)
            yc = (_dot_nt(ctr_ref[jp], spr_ref[jp].astype(BF16))
                  + _dot_nt(cti_ref[jp], spi_ref[jp].astype(BF16)))
            yt_ref[:, pl.ds(r0, 2 * gp), :] += yc.reshape(S5_T, 2 * gp, tc)
        return carry

    lax.fori_loop(0, ng // 2 // unroll, cross, 0)

    for s in range(S5_T):
        for k in range(ncb):
            ys_ref[k, pl.ds(s, tc, stride=S5_T), :] = yt_ref[s, k * LANES:(k + 1) * LANES, :].T
    for k in range(ncb):
        y_ref[:, k * LANES:(k + 1) * LANES] = ys_ref[k]


def _s5_matrices(lam_re, lam_im, log_step, b_re, b_im, c_re, c_im, d_skip):
    hi = lax.Precision.HIGHEST
    t, gp, n, ng = S5_T, S5_GROUP, S5_STATE, S5_GROUPS
    lam = lax.complex(lam_re.astype(F32), lam_im.astype(F32))
    step = jnp.exp(log_step.astype(F32))[:, None]
    ls = lam * step
    a_bar = jnp.exp(ls)
    b_bar = ((a_bar - 1.0) / lam)[..., None] * lax.complex(b_re.astype(F32), b_im.astype(F32))
    cm = lax.complex(c_re.astype(F32), c_im.astype(F32))

    def apow(k):
        kk = k.astype(F32).astype(jnp.complex64)
        return jnp.exp(ls.reshape((ng,) + (1,) * k.ndim + (n,)) * kk[None, ..., None])

    tt = jnp.arange(t)
    kmat = jnp.einsum('gpn,gln,gnq->glpq', cm, apow(tt), b_bar, precision=hi).real
    krev = jnp.transpose(kmat[:, ::-1], (0, 2, 1, 3)).reshape(ng, gp, t * gp)
    kpad = jnp.pad(krev, ((0, 0), (0, 0), (0, t * gp)))
    mt = jnp.concatenate([kpad[:, :, (t - 1 - to) * gp:(2 * t - 1 - to) * gp] for to in range(t)], axis=1)
    dvec = jnp.tile(d_skip.astype(F32).reshape(ng, 1, gp), (1, t, 1)).reshape(ng, t * gp)
    mt = mt + jnp.eye(t * gp, dtype=F32)[None] * dvec[:, :, None]
    z = jnp.swapaxes(apow(t - 1 - tt), 1, 2)[:, :, :, None] * b_bar[:, :, None, :]
    z = z.reshape(ng, n, t * gp)
    bt = jnp.concatenate([z.real, z.imag], axis=1)
    w = cm[:, None, :, :] * apow(tt + 1)[:, :, None, :]

    def pair_readout(x):
        x = x.reshape(ng // 2, 2, t, gp, n)
        first = jnp.pad(x[:, 0], ((0, 0), (0, 0), (0, 0), (0, n)))
        second = jnp.pad(x[:, 1], ((0, 0), (0, 0), (0, 0), (n, 0)))
        return jnp.stack([first, second], axis=2).reshape(ng // 2, t * 2 * gp, 2 * n).astype(BF16)

    ctr, cti = pair_readout(w.real), pair_readout(-w.imag)
    a_chunk = jnp.transpose(apow(t * (jnp.arange(SUBLANES) + 1)), (1, 0, 2)).reshape(SUBLANES, ng * n)
    return mt.astype(BF16), bt.astype(BF16), ctr, cti, a_chunk.real, a_chunk.imag


def _s5(proj, mats):
    seq, width = proj.shape
    t, tc, gp, n, ng = S5_T, S5_TC, S5_GROUP, S5_STATE, S5_GROUPS
    rows = t * tc
    ncb = S5_CH // LANES
    cb0 = (width - S5_CH) // LANES
    u_specs = [pl.BlockSpec((rows, LANES), (lambda i, k=k: (i, cb0 + k))) for k in range(ncb)]
    nsb = ng * n // LANES
    return pl.pallas_call(
        _s5_kernel,
        grid=(seq // rows,),
        in_specs=u_specs + [_const_spec(m.shape) for m in mats],
        out_specs=pl.BlockSpec((rows, S5_CH), lambda i: (i, 0)),
        out_shape=jax.ShapeDtypeStruct((seq, S5_CH), F32),
        scratch_shapes=[
            pltpu.VMEM((t, S5_CH, tc), F32),
            pltpu.VMEM((t, S5_CH, tc), F32),
            pltpu.VMEM((ncb, rows, LANES), F32),
            pltpu.VMEM((ng * n, tc), F32),
            pltpu.VMEM((ng * n, tc), F32),
            pltpu.VMEM((nsb, tc, LANES), F32),
            pltpu.VMEM((nsb, tc, LANES), F32),
            pltpu.VMEM((2, SUBLANES, ng * n), F32),
        ],
        compiler_params=pltpu.CompilerParams(dimension_semantics=("arbitrary",)),
        name="s5_scan",
    )(*([proj] * ncb), *mats)


def _mix_ffn_kernel(*refs, glu, final):
    (x_ref, a_ref, b_ref, wo_ref, g1_ref), refs = refs[:5], refs[5:]
    if glu:
        gw_ref, refs = refs[0], refs[1:]
    (g_ref, sc_ref, sh_ref, gate_ref, win_ref, cw_ref, cb_ref, wout_ref), refs = refs[:8], refs[8:]
    if final:
        fg_ref, o_ref, h_ref, act_ref, gbuf_ref, carry_ref = refs
    else:
        ng_ref, nsc_ref, nsh_ref, o_ref, hn_ref, h_ref, act_ref, gbuf_ref, carry_ref = refs
    tm = x_ref.shape[0]
    halo = gbuf_ref.shape[0] - tm

    @pl.when(pl.program_id(0) == 0)
    def _():
        carry_ref[...] = jnp.zeros(carry_ref.shape, F32)

    if glu:
        y = jax.nn.gelu(b_ref[...]).astype(BF16)
        gg = _dot(y, gw_ref[...])
        half = gg.shape[1] // 2
        b = (gg[:, :half] * jax.nn.sigmoid(gg[:, half:])).astype(BF16)
    else:
        b = b_ref[...]
    cat = jnp.concatenate([a_ref[...], b], axis=1)
    x = x_ref[...] + g1_ref[...] * _dot(cat, wo_ref[...])
    h_ref[...] = _mod_rmsnorm(x, g_ref[...], sc_ref[...], sh_ref[...]).astype(BF16)
    for f in range(D_FF // TF_FFN):
        cs = slice(f * TF_FFN, (f + 1) * TF_FFN)
        gs = slice(D_FF + f * TF_FFN, D_FF + (f + 1) * TF_FFN)
        h = h_ref[...]
        val = _dot(h, win_ref[:, cs])
        gate = _dot(h, win_ref[:, gs])
        gbuf_ref[0:halo, :] = carry_ref[:, cs]
        gbuf_ref[halo:halo + tm, :] = gate
        carry_ref[:, cs] = gate[tm - halo:tm, :]
        conv = (gate * cw_ref[2:3, cs] + gbuf_ref[halo - 1:halo - 1 + tm, :] * cw_ref[1:2, cs]
                + gbuf_ref[halo - 2:halo - 2 + tm, :] * cw_ref[0:1, cs] + cb_ref[:, cs])
        act_ref[:, cs] = (jax.nn.gelu(conv) * val).astype(BF16)
    xn = x + gate_ref[...] * _dot(act_ref[...], wout_ref[...])
    if final:
        xn = xn * lax.rsqrt(jnp.mean(xn * xn, axis=-1, keepdims=True) + EPS) * fg_ref[...]
    else:
        hn_ref[...] = _mod_rmsnorm(xn, ng_ref[...], nsc_ref[...], nsh_ref[...]).astype(BF16)
    o_ref[...] = xn


def _layer_spec(shape, layer):
    idx = (layer,) + (0,) * (len(shape) - 1)
    return pl.BlockSpec((None,) + tuple(shape[1:]), lambda *_: idx, pipeline_mode=pl.Buffered(1))


def _mix_ffn(x, a, b, wo, gate1, glu_w, g, scale, shift, gate2, w_in, conv_w, conv_b, w_out, tail, layer):
    seq, d = x.shape
    final = len(tail) == 1
    tm = TM_FFN
    halo = SUBLANES
    row = pl.BlockSpec((1, d), lambda i: (0, 0))
    rows = lambda w: pl.BlockSpec((tm, w), lambda i: (i, 0))
    conv_b = conv_b.reshape(conv_b.shape[0], 1, D_FF)
    in_specs = [rows(d), rows(a.shape[1]), rows(b.shape[1]), _const_spec(wo.shape), row]
    args = [x, a, b, wo, gate1]
    if glu_w is not None:
        in_specs.append(_const_spec(glu_w.shape))
        args.append(glu_w)
    in_specs += [
        row, row, row, row,
        _layer_spec(w_in.shape, layer),
        _layer_spec(conv_w.shape, layer),
        _layer_spec(conv_b.shape, layer),
        _layer_spec(w_out.shape, layer),
    ] + [row] * len(tail)
    args += [g.reshape(1, d), scale, shift, gate2, w_in, conv_w, conv_b, w_out]
    args += [t.reshape(1, d) for t in tail]
    out_specs = [rows(d)] if final else [rows(d), rows(d)]
    out_shape = [jax.ShapeDtypeStruct((seq, d), F32)] + ([] if final else [jax.ShapeDtypeStruct((seq, d), BF16)])
    return pl.pallas_call(
        functools.partial(_mix_ffn_kernel, glu=glu_w is not None, final=final),
        grid=(seq // tm,),
        in_specs=in_specs,
        out_specs=out_specs,
        out_shape=out_shape,
        scratch_shapes=[
            pltpu.VMEM((tm, d), BF16),
            pltpu.VMEM((tm, D_FF), BF16),
            pltpu.VMEM((tm + halo, TF_FFN), F32),
            pltpu.VMEM((halo, D_FF), F32),
        ],
        compiler_params=pltpu.CompilerParams(dimension_semantics=("arbitrary",)),
        name="mix_ffn",
    )(*args)


def kernel(x, c, t5_table, mod_w, mod_b, norm1_g, norm2_g, ffn_w_in, ffn_conv_w, ffn_conv_b, ffn_w_out,
           ev_w_in, ev_w_out, diff_lambda, diff_subln_g, band_rel_bias,
           od_w_in, od_w_out, s5_lam_re, s5_lam_im, s5_log_step, s5_b_re, s5_b_im, s5_c_re, s5_c_im,
           s5_d, s5_glu_w, final_g):
    assert x.shape[0] == 1 and x.shape[2] == D_MODEL
    seq = x.shape[1]
    assert seq % TM_PROJ == 0 and seq % (S5_T * S5_TC) == 0
    d = D_MODEL
    xs = x[0]
    mod = _modulation(c, mod_w, mod_b)
    ffn_w_in_b = ffn_w_in.astype(BF16)
    ffn_w_out_b = ffn_w_out.astype(BF16)
    mods = [[mod[i, :, k * d:(k + 1) * d] for k in range(6)] for i in range(DEPTH)]
    h = None
    for i in range(DEPTH):
        sh1, sc1, g1, sh2, sc2, g2 = mods[i]
        w_in = (ev_w_in if i % 2 == 0 else od_w_in)[i // 2].astype(BF16)
        proj_dtype = BF16 if i % 2 == 0 else F32
        if h is None:
            proj = _normproj(xs, norm1_g[i], sc1, sh1, w_in, proj_dtype)
        else:
            proj = _proj(h, w_in, proj_dtype)
        if i % 2 == 0:
            e = i // 2
            lam_init = 0.8 - 0.6 * math.exp(-0.3 * i)
            lp = diff_lambda[e].astype(F32)
            lam = jnp.exp(jnp.sum(lp[0] * lp[1])) - jnp.exp(jnp.sum(lp[2] * lp[3])) + lam_init
            mix_a = _diff_attention(proj, t5_table, lam, diff_subln_g[e], lam_init)
            mix_b = _band_attention(proj, band_rel_bias[e])
            wo, glu_w = ev_w_out[e].astype(BF16), None
        else:
            o = i // 2
            mix_a = _retention(proj)
            mats = _s5_matrices(s5_lam_re[o], s5_lam_im[o], s5_log_step[o], s5_b_re[o], s5_b_im[o],
                                s5_c_re[o], s5_c_im[o], s5_d[o])
            mix_b = _s5(proj, mats)
            wo, glu_w = od_w_out[o].astype(BF16), s5_glu_w[o].astype(BF16)
        if i == DEPTH - 1:
            tail = (final_g,)
        else:
            nsh1, nsc1 = mods[i + 1][0], mods[i + 1][1]
            tail = (norm1_g[i + 1], nsc1, nsh1)
        out = _mix_ffn(xs, mix_a, mix_b, wo, g1, glu_w, norm2_g[i], sc2, sh2, g2,
                       ffn_w_in_b, ffn_conv_w, ffn_conv_b, ffn_w_out_b, tail, layer=i)
        if i == DEPTH - 1:
            xs = out[0]
        else:
            xs, h = out
    return xs[None]
```

```python
import functools
import math

import jax
import jax.numpy as jnp
from jax import lax
from jax.experimental import pallas as pl
from jax.experimental.pallas import tpu as pltpu

F32 = jnp.float32
BF16 = jnp.bfloat16

D_MODEL = 1024
DEPTH = 2
CHUNK = 64
GROUP_WIDTH = D_MODEL // 2
DK_A = 64
DV_A = 2 * DK_A
N_HEADS_A = GROUP_WIDTH // DV_A
DH_B = 64
N_HEADS_B = GROUP_WIDTH // DH_B
LEFT_CHUNKS = 8
REL_CLIP = 2 * CHUNK
NUM_BUCKETS = 32
MAX_DISTANCE = 128
DV_C = 128
DQK_C = DV_C // 2
N_HEADS_C = GROUP_WIDTH // DV_C
ROPE_BASE = 10000.0
S5_CH = GROUP_WIDTH
S5_GROUP = 16
S5_GROUPS = S5_CH // S5_GROUP
S5_STATE = 64
D_FF = ((8 * D_MODEL // 3 + 255) // 256) * 256
CONV_W = 3
EVEN_IN = 3 * N_HEADS_A * DV_A + 3 * N_HEADS_B * DH_B
ODD_IN = 2 * N_HEADS_C * DQK_C + 2 * N_HEADS_C * DV_C + S5_CH
EPS = 1e-6
NEG_INF = -1e30
LOG2E = math.log2(math.e)

LANES = 128
SUBLANES = 8
MXU_DIM = 256

TM_PROJ = 1024
TN_PROJ = 1024
TN_MOD = 1536
TM_FFN = 512
TF_FFN = MXU_DIM
BLK_A = 512
NPART_A = 2
ONES_A = 16
SINGLE_PASS_LOG2_RANGE = 96.0
SCORE_PAD = LANES
BLK_B = 1024
BAND_B = LEFT_CHUNKS * CHUNK
QW_B = 4 * CHUNK
BLK_C = 512
S5_T = 16
S5_TC = LANES

assert BLK_B % BAND_B == 0 and BLK_B % QW_B == 0 and BAND_B % QW_B == 0
assert BLK_A >= MAX_DISTANCE, "far key blocks must sit in the saturated T5 bucket"
assert DV_A == LANES and 2 * DK_A == LANES and 2 * DH_B == LANES, "attention heads are read as 128-lane column blocks"


def _dot(a, b):
    return jnp.dot(a, b, preferred_element_type=F32)


def _dot_nt(a, b):
    return lax.dot_general(a, b, (((1,), (1,)), ((), ())), preferred_element_type=F32)


def _dot_tn(a, b):
    return lax.dot_general(a, b, (((0,), (0,)), ((), ())), preferred_element_type=F32)


def _const_spec(shape):
    zeros = (0,) * len(shape)
    return pl.BlockSpec(shape, lambda *_: zeros, pipeline_mode=pl.Buffered(1))


def _mod_rmsnorm(x, g, scale, shift):
    y = x * lax.rsqrt(jnp.mean(x * x, axis=-1, keepdims=True) + EPS)
    y = y * g
    return y * (1.0 + scale) + shift


def _mod_kernel(c_ref, w_ref, b_ref, o_ref):
    c = c_ref[...]
    cond = c * jax.nn.sigmoid(c)
    o_ref[0] = jnp.sum(cond * w_ref[0], axis=0, keepdims=True) + b_ref[0]


def _modulation(c, mod_w, mod_b):
    depth, d, n = mod_w.shape
    tn = TN_MOD
    return pl.pallas_call(
        _mod_kernel,
        grid=(depth, n // tn),
        in_specs=[
            pl.BlockSpec((d, 1), lambda i, j: (0, 0)),
            pl.BlockSpec((1, d, tn), lambda i, j: (i, 0, j)),
            pl.BlockSpec((1, 1, tn), lambda i, j: (i, 0, j)),
        ],
        out_specs=pl.BlockSpec((1, 1, tn), lambda i, j: (i, 0, j)),
        out_shape=jax.ShapeDtypeStruct((depth, 1, n), F32),
        name="modulation",
    )(c.reshape(d, 1), mod_w, mod_b.reshape(depth, 1, n))


def _normproj_kernel(x_ref, g_ref, sc_ref, sh_ref, w_ref, o_ref):
    tm, n = o_ref.shape
    half = tm // 2
    for r in range(2):
        rows = slice(r * half, (r + 1) * half)
        h = _mod_rmsnorm(x_ref[rows, :], g_ref[...], sc_ref[...], sh_ref[...]).astype(BF16)
        for j in range(n // TN_PROJ):
            cols = slice(j * TN_PROJ, (j + 1) * TN_PROJ)
            o_ref[rows, cols] = _dot(h, w_ref[:, cols]).astype(o_ref.dtype)


def _normproj(x, g, scale, shift, w, out_dtype):
    seq, d = x.shape
    n = w.shape[1]
    tm = TM_PROJ
    row = pl.BlockSpec((1, d), lambda i: (0, 0))
    return pl.pallas_call(
        _normproj_kernel,
        grid=(seq // tm,),
        in_specs=[pl.BlockSpec((tm, d), lambda i: (i, 0)), row, row, row, _const_spec(w.shape)],
        out_specs=pl.BlockSpec((tm, n), lambda i: (i, 0)),
        out_shape=jax.ShapeDtypeStruct((seq, n), out_dtype),
        compiler_params=pltpu.CompilerParams(dimension_semantics=("parallel",)),
        name="normproj",
    )(x, g.reshape(1, d), scale, shift, w)


def _proj_kernel(h_ref, w_ref, o_ref):
    for j in range(o_ref.shape[1] // TN_PROJ):
        cols = slice(j * TN_PROJ, (j + 1) * TN_PROJ)
        o_ref[:, cols] = _dot(h_ref[...], w_ref[:, cols]).astype(o_ref.dtype)


def _proj(h, w, out_dtype):
    seq, d = h.shape
    n = w.shape[1]
    tm = TM_PROJ
    return pl.pallas_call(
        _proj_kernel,
        grid=(seq // tm,),
        in_specs=[pl.BlockSpec((tm, d), lambda i: (i, 0)), _const_spec(w.shape)],
        out_specs=pl.BlockSpec((tm, n), lambda i: (i, 0)),
        out_shape=jax.ShapeDtypeStruct((seq, n), out_dtype),
        compiler_params=pltpu.CompilerParams(dimension_semantics=("parallel",)),
        name="proj",
    )(h, w)


def _diffattn_kernel(q_ref, qall_ref, k_ref, v_ref, bias_ref, bstat_ref, lam_ref, g_ref, o_ref,
                     flag_ref, qs_ref, vt_ref, kmax_ref, r_ref, m_ref, acc_ref, *s_refs, out_scale):
    blk = BLK_A
    nq = 2 * blk
    sub = SUBLANES
    dv = DV_A
    npart = NPART_A
    sa_ref, sb_ref = s_refs[:2 * npart], s_refs[2 * npart:4 * npart]
    pa_ref, pb_ref = s_refs[4 * npart:5 * npart], s_refs[5 * npart:6 * npart]
    i = pl.program_id(1)
    lane = lax.broadcasted_iota(jnp.int32, (blk, LANES), 1)
    same_subhead = (lax.broadcasted_iota(jnp.int32, (LANES, LANES), 0) // DK_A
                    == lax.broadcasted_iota(jnp.int32, (LANES, LANES), 1) // DK_A).astype(BF16)

    bias_max, bias_span = bstat_ref[0, 0:1, 0:1], bstat_ref[0, 1:2, 0:1]
    q_scale = DK_A ** -0.5 * LOG2E

    @pl.when(i == 0)
    def _():
        kmax_ref[...] = jnp.zeros(kmax_ref.shape, F32)

        def tr(b, qmax):
            r0 = pl.multiple_of(b * blk, blk)
            vt_ref[0:dv, pl.ds(r0, blk)] = v_ref[pl.ds(r0, blk), :].astype(F32).T.astype(BF16)
            vt_ref[dv:dv + ONES_A, pl.ds(r0, blk)] = jnp.ones((ONES_A, blk), BF16)
            kf = k_ref[pl.ds(r0, blk), :].astype(F32)
            kn2 = _dot((kf * kf).astype(BF16), same_subhead)
            kmax_ref[...] = jnp.maximum(kmax_ref[...], jnp.max(kn2.reshape(blk // sub, sub, LANES), axis=0))
            qa = (qall_ref[pl.ds(r0, blk), :].astype(F32) * q_scale).astype(BF16).astype(F32)
            qn2 = _dot((qa * qa).astype(BF16), same_subhead)
            return jnp.maximum(qmax, jnp.max(qn2.reshape(blk // sub, sub, LANES), axis=0))
        qmax2 = lax.fori_loop(0, v_ref.shape[0] // blk, tr, jnp.zeros((sub, LANES), F32))
        kmax2 = jnp.max(kmax_ref[...], axis=0, keepdims=True)
        kmax_ref[...] = jnp.broadcast_to(kmax2, kmax_ref.shape)
        worst = 2.0 * jnp.sqrt(jnp.max(qmax2, axis=0, keepdims=True) * kmax2) * 1.03 * 1.03 + bias_span
        flag_ref[0] = (jnp.max(worst) < SINGLE_PASS_LOG2_RANGE).astype(jnp.int32)

    q = (q_ref[...].astype(F32) * q_scale).astype(BF16)
    qf = q.astype(F32)
    qs_ref[:, 0:blk] = jnp.where(lane < DK_A, qf, 0.0).T.astype(BF16)
    qs_ref[:, blk:nq] = jnp.where(lane >= DK_A, qf, 0.0).T.astype(BF16)
    acc_ref[...] = jnp.zeros(acc_ref.shape, F32)

    qt = qs_ref[...].astype(F32)
    qn2 = jnp.sum((qt * qt).reshape(LANES // sub, sub, nq), axis=0)
    qn2 = jnp.broadcast_to(jnp.sum(qn2, axis=0, keepdims=True), (sub, nq))
    kmax2 = jnp.concatenate([jnp.broadcast_to(kmax_ref[:, m * DK_A:m * DK_A + 1], (sub, blk)) for m in range(2)],
                            axis=1)
    bound = jnp.sqrt(qn2 * kmax2) * 1.03
    r_ref[...] = bound + bias_max
    single_pass = flag_ref[0] == 1

    @pl.when(single_pass)
    def _():
        _diffattn_fixed_shift(i, k_ref, bias_ref, qs_ref, vt_ref, r_ref, m_ref, acc_ref, pa_ref, pb_ref)

    @pl.when(jnp.logical_not(single_pass))
    def _():
        _diffattn_online(i, k_ref, bias_ref, qs_ref, vt_ref, m_ref, acc_ref, sa_ref, sb_ref)

    ot = acc_ref[0:dv, 0:nq] / acc_ref[dv:dv + 1, 0:nq]
    o = ot[:, 0:blk].T - lam_ref[...] * ot[:, blk:nq].T
    o = o * lax.rsqrt(jnp.mean(o * o, axis=-1, keepdims=True) + EPS) * g_ref[...]
    o_ref[...] = (o * out_scale).astype(o_ref.dtype)


def _diffattn_fixed_shift(i, k_ref, bias_ref, qs_ref, vt_ref, shift_ref, l_ref, acc_ref, pa_ref, pb_ref):
    blk = BLK_A
    nq = 2 * blk
    sub = SUBLANES
    npart = len(pa_ref)
    wq = nq // npart
    l_ref[...] = jnp.zeros(l_ref.shape, F32)

    def probs(b, p_ref, bias):
        k = k_ref[pl.ds(pl.multiple_of(b * blk, blk), blk), :]
        for part in range(npart):
            cols = slice(part * wq, (part + 1) * wq)
            s = _dot(k, qs_ref[:, cols])
            if bias is not None:
                q0 = (part * wq) % blk
                s = s + bias[:, q0:q0 + wq]
            p = jnp.exp2(s.reshape(blk // sub, sub, wq) - shift_ref[:, cols][None])
            l_ref[:, cols] += jnp.sum(p, axis=0)
            p_ref[part][:, 0:wq] = p.reshape(blk, wq).astype(BF16)

    def accumulate(b, p_ref):
        vt = vt_ref[0:DV_A, pl.ds(pl.multiple_of(b * blk, blk), blk)]
        for part in range(npart):
            cols = slice(part * wq, (part + 1) * wq)
            acc_ref[0:DV_A, cols] += _dot(vt, p_ref[part][:, 0:wq])

    @pl.when(i == 0)
    def _():
        probs(0, pa_ref, bias_ref[0, 1])
        accumulate(0, pa_ref)

    @pl.when(i > 0)
    def _():
        nfar = i - 1
        probs(i, pa_ref, bias_ref[0, 1])
        probs(i - 1, pb_ref, bias_ref[0, 0])
        accumulate(i, pa_ref)

        def pair(t):
            probs(2 * t, pa_ref, None)
            accumulate(jnp.where(t == 0, i - 1, 2 * t - 1), pb_ref)
            probs(2 * t + 1, pb_ref, None)
            accumulate(2 * t, pa_ref)

        def four_pairs(u, carry):
            for v in range(4):
                pair(4 * u + v)
            return carry

        npairs = nfar // 2
        lax.fori_loop(0, npairs // 4, four_pairs, 0)

        def one_pair(t, carry):
            pair(t)
            return carry
        lax.fori_loop(4 * (npairs // 4), npairs, one_pair, 0)
        in_pb = jnp.where(npairs == 0, i - 1, 2 * npairs - 1)

        @pl.when(lax.rem(nfar, 2) == 1)
        def _():
            probs(nfar - 1, pa_ref, None)
            accumulate(in_pb, pb_ref)
            accumulate(nfar - 1, pa_ref)

        @pl.when(lax.rem(nfar, 2) == 0)
        def _():
            accumulate(in_pb, pb_ref)

    acc_ref[DV_A:DV_A + sub, 0:nq] = jnp.broadcast_to(jnp.sum(l_ref[...], axis=0, keepdims=True), (sub, nq))


def _diffattn_online(i, k_ref, bias_ref, qs_ref, vt_ref, m_ref, acc_ref, sa_ref, sb_ref):
    blk = BLK_A
    nq = 2 * blk
    sub = SUBLANES
    npart = len(sa_ref) // 2
    wq = nq // npart
    m_ref[...] = jnp.full(m_ref.shape, NEG_INF, F32)

    def scores(b, s_ref):
        k = k_ref[pl.ds(pl.multiple_of(b * blk, blk), blk), :]
        for part in range(npart):
            s = _dot(k, qs_ref[:, part * wq:(part + 1) * wq])
            s_ref[part][:, 0:wq] = s
            s_ref[npart + part][...] = jnp.max(s.reshape(blk // sub, sub, wq), axis=0)

    def softmax_pv(b, s_ref, bias):
        vt = vt_ref[:, pl.ds(pl.multiple_of(b * blk, blk), blk)]
        for part in range(npart):
            cols = slice(part * wq, (part + 1) * wq)
            s = s_ref[part][:, 0:wq]
            if bias is not None:
                q0 = (part * wq) % blk
                s = s + bias[:, q0:q0 + wq]
            s = s.reshape(blk // sub, sub, wq)
            m_prev = m_ref[:, cols]
            smax = jnp.max(s, axis=0) if bias is not None else s_ref[npart + part][...]
            m_cur = jnp.max(smax, axis=0, keepdims=True)
            m_new = jnp.maximum(m_prev, m_cur)
            alpha = jnp.exp2(m_prev - m_new)
            p = jnp.exp2(s - m_new[None])
            pv = _dot(vt, p.reshape(blk, wq).astype(BF16))
            acc_ref[:, cols] = acc_ref[:, cols] * alpha[0:1] + pv
            m_ref[:, cols] = m_new

    nfar = jnp.maximum(i - 1, 0)
    odd = lax.rem(nfar, 2)

    @pl.when(i == 0)
    def _():
        scores(0, sb_ref)

    @pl.when(i > 0)
    def _():
        @pl.when(odd == 1)
        def _():
            scores(0, sb_ref)
            scores(1, sa_ref)
            softmax_pv(0, sb_ref, None)

        @pl.when(odd == 0)
        def _():
            scores(0, sa_ref)

        def pair(b):
            scores(b + 1, sb_ref)
            softmax_pv(b, sa_ref, None)
            scores(b + 2, sa_ref)
            softmax_pv(b + 1, sb_ref, None)

        def quad_body(t, carry):
            pair(odd + 4 * t)
            pair(odd + 4 * t + 2)
            return carry

        npairs = nfar // 2
        lax.fori_loop(0, npairs // 2, quad_body, 0)

        @pl.when(lax.rem(npairs, 2) == 1)
        def _():
            pair(odd + 2 * (npairs - 1))
        scores(i, sb_ref)
        softmax_pv(i - 1, sa_ref, bias_ref[0, 0])

    softmax_pv(i, sb_ref, bias_ref[0, 1])


_TOEPLITZ_ROWS = 256
_TOEPLITZ_N = 2048


def _toeplitz_kernel(v_ref, o_ref, *, keep):
    rows, cols = o_ref.shape[2:]
    x = jnp.broadcast_to(v_ref[0, 0], (rows, v_ref.shape[-1]))
    tile = pltpu.roll(x, 0, 1, stride=1, stride_axis=0)[:, :cols]
    r = lax.broadcasted_iota(jnp.int32, (rows, cols), 0) + pl.program_id(1) * rows
    c = lax.broadcasted_iota(jnp.int32, (rows, cols), 1)
    for variant in range(o_ref.shape[0]):
        o_ref[variant, 0] = jnp.where(keep(r, c, variant), tile, NEG_INF)


def _toeplitz_tiles(fn, keep, heads, rows, cols, variants=1):
    n, rb = _TOEPLITZ_N, _TOEPLITZ_ROWS
    assert rows % rb == 0 and rows <= n // 2 and cols <= n // 2
    idx = jnp.arange(n, dtype=jnp.int32)
    vec = fn(jnp.where(idx < n // 2, idx, idx - n)).astype(F32)
    vecs = jnp.stack([jnp.roll(vec, k * rb, axis=1) for k in range(rows // rb)], axis=1)
    return pl.pallas_call(
        functools.partial(_toeplitz_kernel, keep=keep),
        grid=(heads, rows // rb),
        in_specs=[pl.BlockSpec((1, 1, 1, n), lambda h, k: (h, k, 0, 0))],
        out_specs=pl.BlockSpec((variants, 1, rb, cols), lambda h, k: (0, h, k, 0)),
        out_shape=jax.ShapeDtypeStruct((variants, heads, rows, cols), F32),
        name="toeplitz_tiles",
    )(vecs.reshape(heads, rows // rb, 1, n))


def _bias_stats(tiles):
    finite = tiles > 0.5 * NEG_INF
    bias_max = jnp.maximum(jnp.max(jnp.where(finite, tiles, NEG_INF), axis=(1, 2)), 0.0)
    bias_min = jnp.minimum(jnp.min(jnp.where(finite, tiles, -NEG_INF), axis=(1, 2)), 0.0)
    return jnp.broadcast_to(jnp.stack([bias_max, bias_max - bias_min], axis=1)[:, :, None],
                            (tiles.shape[0], 2, LANES))


def _t5_bucket(rel):
    nb = NUM_BUCKETS // 2
    max_exact = nb // 2
    bucket = jnp.where(rel > 0, nb, 0)
    n = jnp.abs(rel)
    nf = jnp.maximum(n, 1).astype(F32)
    large = max_exact + (jnp.log(nf / max_exact) / math.log(MAX_DISTANCE / max_exact)
                         * (nb - max_exact)).astype(jnp.int32)
    large = jnp.minimum(large, nb - 1)
    return bucket + jnp.where(n < max_exact, n, large)


def _diff_bias_tiles(t5_table):
    blk = BLK_A
    table = t5_table.astype(F32)
    far = table[_t5_bucket(jnp.full((), -(blk + 1), jnp.int32))]
    def visible(r, c, variant):
        return jnp.floor_divide(r - blk, CHUNK) <= jnp.floor_divide(c, CHUNK)

    tiles = _toeplitz_tiles(lambda x: ((table[_t5_bucket(-x - blk)] - far) * LOG2E).T, visible,
                            N_HEADS_A, 2 * blk, blk)
    return tiles.reshape(N_HEADS_A, 2, blk, blk)


def _diff_attention(proj, t5_table, lam, subln_g, lam_init):
    seq = proj.shape[0]
    blk = BLK_A
    bias = _diff_bias_tiles(t5_table)
    ha = N_HEADS_A
    bstat = _bias_stats(bias.reshape(ha, 2 * blk, blk))
    kern = functools.partial(_diffattn_kernel, out_scale=1.0 - lam_init)
    return pl.pallas_call(
        kern,
        grid=(ha, seq // blk),
        in_specs=[
            pl.BlockSpec((blk, DV_A), lambda h, i: (i, h)),
            pl.BlockSpec((seq, DV_A), lambda h, i: (0, h)),
            pl.BlockSpec((seq, DV_A), lambda h, i: (0, ha + h)),
            pl.BlockSpec((seq, DV_A), lambda h, i: (0, 2 * ha + h)),
            pl.BlockSpec((1, 2, blk, blk), lambda h, i: (h, 0, 0, 0)),
            pl.BlockSpec((1, 2, LANES), lambda h, i: (h, 0, 0)),
            pl.BlockSpec((1, DV_A), lambda h, i: (0, 0)),
            pl.BlockSpec((1, DV_A), lambda h, i: (0, 0)),
        ],
        out_specs=pl.BlockSpec((blk, DV_A), lambda h, i: (i, h)),
        out_shape=jax.ShapeDtypeStruct((seq, ha * DV_A), BF16),
        scratch_shapes=[
            pltpu.SMEM((1,), jnp.int32),
            pltpu.VMEM((DV_A, 2 * blk), BF16),
            pltpu.VMEM((DV_A + ONES_A, seq), BF16),
            pltpu.VMEM((SUBLANES, LANES), F32),
            pltpu.VMEM((SUBLANES, 2 * blk), F32),
            pltpu.VMEM((SUBLANES, 2 * blk), F32),
            pltpu.VMEM((DV_A + ONES_A, 2 * blk), F32),
        ] + 2 * ([pltpu.VMEM((blk, 2 * blk // NPART_A + SCORE_PAD), F32)] * NPART_A
                 + [pltpu.VMEM((SUBLANES, 2 * blk // NPART_A), F32)] * NPART_A)
        + 2 * [pltpu.VMEM((blk, 2 * blk // NPART_A + SCORE_PAD), BF16)] * NPART_A,
        compiler_params=pltpu.CompilerParams(dimension_semantics=("parallel", "arbitrary")),
        name="diff_attention",
    )(proj, proj, proj, proj, bias, bstat, jnp.full((1, DV_A), lam, F32), subln_g.reshape(1, DV_A).astype(F32))


def _band_kernel(q_ref, kp_ref, kc_ref, vp_ref, vc_ref, *refs):
    qw, band = QW_B, BAND_B
    nbias = band // qw + 1
    bias_refs, bstat_ref, o_ref = refs[:nbias], refs[nbias], refs[nbias + 1]
    ngroups = (len(refs) - nbias - 2) // 2
    s_refs, p_refs = refs[nbias + 2:nbias + 2 + ngroups], refs[nbias + 2 + ngroups:]
    nk = band + qw
    sub = SUBLANES
    q = (q_ref[...].astype(F32) * (DH_B ** -0.5 * LOG2E)).astype(BF16).astype(F32)
    lane = lax.broadcasted_iota(jnp.int32, q.shape, 1)
    qt = (jnp.where(lane < DH_B, q, 0.0).T, jnp.where(lane >= DH_B, q, 0.0).T)
    qh = (qt[0].astype(BF16), qt[1].astype(BF16))
    k_all = jnp.concatenate([kp_ref[...], kc_ref[...]], axis=0)
    vt_all = jnp.concatenate([vp_ref[...], vc_ref[...]], axis=0).astype(F32).T.astype(BF16)

    def group_operands(g):
        k0 = g * qw
        qs = jnp.concatenate([qh[0][:, k0:k0 + qw], qh[1][:, k0:k0 + qw]], axis=1)
        bias_ref = bias_refs[min(g, nbias - 1)]
        bias = jnp.concatenate([bias_ref[0, 0], bias_ref[0, 1]], axis=1)
        return k0, qs, bias

    def store_group(g, ot):
        o = jnp.concatenate([ot[0:DH_B, 0:qw], ot[DH_B:2 * DH_B, qw:2 * qw]], axis=0)
        o_ref[g * qw:(g + 1) * qw, :] = o.T.astype(o_ref.dtype)

    same_head = (lax.broadcasted_iota(jnp.int32, (LANES, LANES), 0) // DH_B
                 == lax.broadcasted_iota(jnp.int32, (LANES, LANES), 1) // DH_B).astype(BF16)
    kf = k_all.astype(F32)
    kn2 = _dot((kf * kf).astype(BF16), same_head)
    kmax2 = jnp.max(jnp.max(kn2.reshape(kn2.shape[0] // sub, sub, LANES), axis=0), axis=0, keepdims=True)
    shifts, worst = [], None
    for m in range(2):
        qn2 = jnp.sum(qt[m] * qt[m], axis=0, keepdims=True)
        bound = jnp.sqrt(qn2 * kmax2[:, m * DH_B:m * DH_B + 1]) * 1.03
        shifts.append(bound + bstat_ref[m, 0:1, 0:1])
        spread = jnp.max(2.0 * bound + bstat_ref[m, 1:2, 0:1])
        worst = spread if worst is None else jnp.maximum(worst, spread)
    fixed_shift = worst < SINGLE_PASS_LOG2_RANGE

    @pl.when(fixed_shift)
    def _():
        sums = []
        for g in range(ngroups):
            k0, qs, bias = group_operands(g)
            r = jnp.concatenate([shifts[0][:, k0:k0 + qw], shifts[1][:, k0:k0 + qw]], axis=1)
            s = _dot(k_all[k0:k0 + nk], qs) + (bias - r)
            p = jnp.exp2(s).reshape(nk // sub, sub, 2 * qw)
            sums.append(jnp.sum(jnp.sum(p, axis=0), axis=0, keepdims=True))
            p_refs[g][:, 0:2 * qw] = p.reshape(nk, 2 * qw).astype(BF16)
        for g in range(ngroups):
            k0 = g * qw
            store_group(g, _dot(vt_all[:, k0:k0 + nk], p_refs[g][:, 0:2 * qw]) / sums[g])

    @pl.when(jnp.logical_not(fixed_shift))
    def _():
        vt_ones = jnp.concatenate([vt_all, jnp.ones((ONES_A, vt_all.shape[1]), BF16)], axis=0)
        for g in range(ngroups):
            k0, qs, _ = group_operands(g)
            s_refs[g][:, 0:2 * qw] = _dot(k_all[k0:k0 + nk], qs)
        for g in range(ngroups):
            k0, _, bias = group_operands(g)
            s = (s_refs[g][:, 0:2 * qw] + bias).reshape(nk // sub, sub, 2 * qw)
            m = jnp.max(jnp.max(s, axis=0), axis=0, keepdims=True)
            p = jnp.exp2(s - m[None])
            pv = _dot(vt_ones[:, k0:k0 + nk], p.reshape(nk, 2 * qw).astype(BF16))
            store_group(g, pv[0:2 * DH_B] / pv[2 * DH_B:2 * DH_B + 1])


def _band_bias_tiles(rel_bias):
    band = BAND_B

    def valid(r, c, variant):
        qchunk = jnp.floor_divide(c, CHUNK)
        kchunk = jnp.floor_divide(r - band, CHUNK)
        missing = jnp.where(variant == 0, 0, band - (variant - 1) * QW_B)
        return (kchunk <= qchunk) & (kchunk >= qchunk - LEFT_CHUNKS) & (r >= missing)

    return _toeplitz_tiles(
        lambda x: rel_bias.astype(F32)[:, jnp.clip(-x - band, -REL_CLIP, REL_CLIP) + REL_CLIP] * LOG2E, valid,
        N_HEADS_B, band + QW_B, QW_B, variants=1 + band // QW_B)


def _band_attention(proj, rel_bias):
    seq = proj.shape[0]
    blk, band, qw = BLK_B, BAND_B, QW_B
    bias = _band_bias_tiles(rel_bias)
    npair = N_HEADS_B // 2
    qc0 = 3 * N_HEADS_A
    per = blk // band
    prev = lambda c0: (lambda hp, i: (jnp.maximum(i * per - 1, 0), c0 + hp))
    cur = lambda c0: (lambda hp, i: (i, c0 + hp))
    return pl.pallas_call(
        _band_kernel,
        grid=(npair, seq // blk),
        in_specs=[
            pl.BlockSpec((blk, LANES), cur(qc0)),
            pl.BlockSpec((band, LANES), prev(qc0 + npair)),
            pl.BlockSpec((blk, LANES), cur(qc0 + npair)),
            pl.BlockSpec((band, LANES), prev(qc0 + 2 * npair)),
            pl.BlockSpec((blk, LANES), cur(qc0 + 2 * npair)),
        ] + [
            pl.BlockSpec((1, 2, band + qw, qw), (lambda hp, i, t=t: (jnp.where(i == 0, 1 + t, 0), hp, 0, 0)))
            for t in range(band // qw)
        ] + [
            pl.BlockSpec((1, 2, band + qw, qw), lambda hp, i: (0, hp, 0, 0)),
            pl.BlockSpec((2, 2, LANES), lambda hp, i: (hp, 0, 0)),
        ],
        out_specs=pl.BlockSpec((blk, LANES), lambda hp, i: (i, hp)),
        out_shape=jax.ShapeDtypeStruct((seq, N_HEADS_B * DH_B), BF16),
        scratch_shapes=([pltpu.VMEM((band + qw, 2 * qw + SCORE_PAD), F32)] * (blk // qw)
                        + [pltpu.VMEM((band + qw, 2 * qw + SCORE_PAD), BF16)] * (blk // qw)),
        compiler_params=pltpu.CompilerParams(dimension_semantics=("parallel", "arbitrary")),
        name="band_attention",
    )(proj, proj, proj, proj, proj, *([bias] * bias.shape[0]), _bias_stats(bias[0]))


def _retention_kernel(qk_ref, v_ref, gate_ref, cos_ref, sin_ref, qdec_ref, kdec_ref, dmat_ref,
                      sdec_ref, o_ref, state_ref):
    @pl.when(pl.program_id(0) == 0)
    def _():
        state_ref[...] = jnp.zeros(state_ref.shape, F32)

    cos = cos_ref[...]
    sin = sin_ref[...]
    lane = lax.broadcasted_iota(jnp.int32, cos.shape, 1)
    first_half = (lane % DQK_C) < (DQK_C // 2)
    qk = qk_ref[...]
    parts = []
    for j in range(qk.shape[1] // LANES):
        t = qk[:, j * LANES:(j + 1) * LANES]
        partner = jnp.where(first_half, pltpu.roll(t, LANES - DQK_C // 2, 1), pltpu.roll(t, DQK_C // 2, 1))
        parts.append(t * cos + partner * sin)
    wq = N_HEADS_C * DQK_C
    q = jnp.concatenate(parts[:wq // LANES], axis=1)
    k = jnp.concatenate(parts[wq // LANES:], axis=1) * (DQK_C ** -0.5)
    qd = (q * qdec_ref[...]).astype(BF16)
    kd = (k * kdec_ref[...]).astype(BF16)
    qb = q.astype(BF16)
    kb = k.astype(BF16)
    vb = v_ref[...].astype(BF16)
    gate = gate_ref[...]
    outs = []
    for h in range(N_HEADS_C):
        qs = slice(h * DQK_C, (h + 1) * DQK_C)
        vs = slice(h * DV_C, (h + 1) * DV_C)
        scores = _dot_nt(qb[:, qs], kb[:, qs]) * dmat_ref[h]
        state = state_ref[h]
        r = _dot(scores.astype(BF16), vb[:, vs]) + _dot(qd[:, qs], state.astype(BF16))
        state_ref[h] = state * sdec_ref[h] + _dot_tn(kd[:, qs], vb[:, vs])
        r = r * lax.rsqrt(jnp.mean(r * r, axis=-1, keepdims=True) + EPS)
        g = gate[:, vs]
        outs.append(r * (g * jax.nn.sigmoid(g)))
    o_ref[...] = jnp.concatenate(outs, axis=1).astype(o_ref.dtype)


def _retention_tables(seq):
    t = BLK_C
    half = DQK_C // 2
    inv_freq = 1.0 / (ROPE_BASE ** (jnp.arange(0, DQK_C, 2, dtype=F32) / DQK_C))
    ang = jnp.arange(seq, dtype=F32)[:, None] * inv_freq[None, :]
    reps = LANES // half
    cos = jnp.tile(jnp.cos(ang), (1, reps))
    sign = jnp.where((jnp.arange(LANES) % DQK_C) < half, -1.0, 1.0).astype(F32)
    sin = jnp.tile(jnp.sin(ang), (1, reps)) * sign[None, :]
    log_g = jnp.log(1.0 - jnp.power(2.0, -5.0 - jnp.arange(N_HEADS_C, dtype=F32)))
    pos = jnp.arange(t, dtype=F32)
    diff = pos[:, None] - pos[None, :]
    same_or_past = (jnp.arange(t)[None, :] // CHUNK) <= (jnp.arange(t)[:, None] // CHUNK)
    dmat = jnp.where(same_or_past[None], jnp.exp(log_g[:, None, None] * jnp.abs(diff)[None]), 0.0)
    qdec = jnp.repeat(jnp.exp(log_g[None, :] * (pos[:, None] + 1.0)), DQK_C, axis=1)
    kdec = jnp.repeat(jnp.exp(log_g[None, :] * (t - 1.0 - pos[:, None])), DQK_C, axis=1)
    sdec = jnp.broadcast_to(jnp.exp(log_g * t)[:, None, None], (N_HEADS_C, 1, DV_C))
    return cos, sin, qdec, kdec, dmat, sdec


def _retention(proj):
    seq = proj.shape[0]
    t = BLK_C
    cos, sin, qdec, kdec, dmat, sdec = _retention_tables(seq)
    wv = N_HEADS_C * DV_C
    return pl.pallas_call(
        _retention_kernel,
        grid=(seq // t,),
        in_specs=[
            pl.BlockSpec((t, wv), lambda i: (i, 0)),
            pl.BlockSpec((t, wv), lambda i: (i, 1)),
            pl.BlockSpec((t, wv), lambda i: (i, 2)),
            pl.BlockSpec((t, LANES), lambda i: (i, 0)),
            pl.BlockSpec((t, LANES), lambda i: (i, 0)),
            pl.BlockSpec((t, N_HEADS_C * DQK_C), lambda i: (0, 0)),
            pl.BlockSpec((t, N_HEADS_C * DQK_C), lambda i: (0, 0)),
            pl.BlockSpec((N_HEADS_C, t, t), lambda i: (0, 0, 0)),
            pl.BlockSpec((N_HEADS_C, 1, DV_C), lambda i: (0, 0, 0)),
        ],
        out_specs=pl.BlockSpec((t, wv), lambda i: (i, 0)),
        out_shape=jax.ShapeDtypeStruct((seq, wv), BF16),
        scratch_shapes=[pltpu.VMEM((N_HEADS_C, DQK_C, DV_C), F32)],
        compiler_params=pltpu.CompilerParams(dimension_semantics=("arbitrary",)),
        name="retention",
    )(proj, proj, proj, cos, sin, qdec, kdec, dmat, sdec)


def _s5_kernel(*refs):
    ncb = S5_CH // LANES
    u_refs = refs[:ncb]
    (mt_ref, bt_ref, ctr_ref, cti_ref, are_ref, aim_ref, y_ref,
     ut_ref, yt_ref, ys_ref, vr_ref, vi_ref, spr_ref, spi_ref, carry_ref) = refs[ncb:]
    tc = S5_TC
    gp = S5_GROUP
    n = S5_STATE
    ng = S5_GROUPS

    @pl.when(pl.program_id(0) == 0)
    def _():
        carry_ref[...] = jnp.zeros(carry_ref.shape, F32)

    for s in range(S5_T):
        for k in range(ncb):
            ut_ref[s, k * LANES:(k + 1) * LANES, :] = u_refs[k][pl.ds(s, tc, stride=S5_T), :].T

    unroll = 4

    def intra(it, carry):
        for k in range(unroll):
            g = it * unroll + k
            r0 = pl.multiple_of(g * gp, gp)
            ug = ut_ref[:, pl.ds(r0, gp), :].reshape(S5_T * gp, tc).astype(BF16)
            yt_ref[:, pl.ds(r0, gp), :] = _dot(mt_ref[g], ug).reshape(S5_T, gp, tc)
            vt = _dot(bt_ref[g], ug)
            n0 = pl.multiple_of(g * n, n)
            vr_ref[pl.ds(n0, n), :] = vt[0:n]
            vi_ref[pl.ds(n0, n), :] = vt[n:2 * n]
        return carry

    lax.fori_loop(0, ng // unroll, intra, 0)

    sub = SUBLANES
    nv = tc // sub
    row = lax.broadcasted_iota(jnp.int32, (tc, LANES), 0)
    in_vreg = lax.rem(row, sub)

    def rows_of(v, r):
        return jnp.broadcast_to(v[r:r + 1], (tc, LANES))

    for j in range(ng * n // LANES):
        cols = slice(j * LANES, (j + 1) * LANES)
        pwr, pwi = are_ref[:, cols], aim_ref[:, cols]
        xr = vr_ref[cols, :].T
        xi = vi_ref[cols, :].T
        for d in (1, 2, 4):
            keep = in_vreg >= d
            sr = jnp.where(keep, pltpu.roll(xr, d, 0), 0.0)
            si = jnp.where(keep, pltpu.roll(xi, d, 0), 0.0)
            fr, fi = rows_of(pwr, d - 1), rows_of(pwi, d - 1)
            xr, xi = xr + (fr * sr - fi * si), xi + (fr * si + fi * sr)
        cr, ci = carry_ref[0, :, cols], carry_ref[1, :, cols]
        cr0, ci0 = cr, ci
        outr, outi = [], []
        for v in range(nv):
            yr = xr[v * sub:(v + 1) * sub] + (pwr * cr - pwi * ci)
            yi = xi[v * sub:(v + 1) * sub] + (pwr * ci + pwi * cr)
            outr.append(yr)
            outi.append(yi)
            cr = jnp.broadcast_to(yr[sub - 1:sub], (sub, LANES))
            ci = jnp.broadcast_to(yi[sub - 1:sub], (sub, LANES))
        carry_ref[0, :, cols] = cr
        carry_ref[1, :, cols] = ci
        sr = jnp.concatenate(outr, axis=0)
        si = jnp.concatenate(outi, axis=0)
        first = row == 0
        spr_ref[j] = jnp.where(first, rows_of(cr0, 0), pltpu.roll(sr, 1, 0))
        spi_ref[j] = jnp.where(first, rows_of(ci0, 0), pltpu.roll(si, 1, 0))

    def cross(it, carry):
        for k in range(unroll):
            jp = it * unroll + k
            r0 = pl.multiple_of(jp * 2 * gp, 2 * gp)
            yc = (_dot_nt(ctr_ref[jp], spr_ref[jp].astype(BF16))
                  + _dot_nt(cti_ref[jp], spi_ref[jp].astype(BF16)))
            yt_ref[:, pl.ds(r0, 2 * gp), :] += yc.reshape(S5_T, 2 * gp, tc)
        return carry

    lax.fori_loop(0, ng // 2 // unroll, cross, 0)

    for s in range(S5_T):
        for k in range(ncb):
            ys_ref[k, pl.ds(s, tc, stride=S5_T), :] = yt_ref[s, k * LANES:(k + 1) * LANES, :].T
    for k in range(ncb):
        y_ref[:, k * LANES:(k + 1) * LANES] = ys_ref[k]


def _s5_matrices(lam_re, lam_im, log_step, b_re, b_im, c_re, c_im, d_skip):
    hi = lax.Precision.HIGHEST
    t, gp, n, ng = S5_T, S5_GROUP, S5_STATE, S5_GROUPS
    lam = lax.complex(lam_re.astype(F32), lam_im.astype(F32))
    step = jnp.exp(log_step.astype(F32))[:, None]
    ls = lam * step
    a_bar = jnp.exp(ls)
    b_bar = ((a_bar - 1.0) / lam)[..., None] * lax.complex(b_re.astype(F32), b_im.astype(F32))
    cm = lax.complex(c_re.astype(F32), c_im.astype(F32))

    def apow(k):
        kk = k.astype(F32).astype(jnp.complex64)
        return jnp.exp(ls.reshape((ng,) + (1,) * k.ndim + (n,)) * kk[None, ..., None])

    tt = jnp.arange(t)
    kmat = jnp.einsum('gpn,gln,gnq->glpq', cm, apow(tt), b_bar, precision=hi).real
    krev = jnp.transpose(kmat[:, ::-1], (0, 2, 1, 3)).reshape(ng, gp, t * gp)
    kpad = jnp.pad(krev, ((0, 0), (0, 0), (0, t * gp)))
    mt = jnp.concatenate([kpad[:, :, (t - 1 - to) * gp:(2 * t - 1 - to) * gp] for to in range(t)], axis=1)
    dvec = jnp.tile(d_skip.astype(F32).reshape(ng, 1, gp), (1, t, 1)).reshape(ng, t * gp)
    mt = mt + jnp.eye(t * gp, dtype=F32)[None] * dvec[:, :, None]
    z = jnp.swapaxes(apow(t - 1 - tt), 1, 2)[:, :, :, None] * b_bar[:, :, None, :]
    z = z.reshape(ng, n, t * gp)
    bt = jnp.concatenate([z.real, z.imag], axis=1)
    w = cm[:, None, :, :] * apow(tt + 1)[:, :, None, :]

    def pair_readout(x):
        x = x.reshape(ng // 2, 2, t, gp, n)
        first = jnp.pad(x[:, 0], ((0, 0), (0, 0), (0, 0), (0, n)))
        second = jnp.pad(x[:, 1], ((0, 0), (0, 0), (0, 0), (n, 0)))
        return jnp.stack([first, second], axis=2).reshape(ng // 2, t * 2 * gp, 2 * n).astype(BF16)

    ctr, cti = pair_readout(w.real), pair_readout(-w.imag)
    a_chunk = jnp.transpose(apow(t * (jnp.arange(SUBLANES) + 1)), (1, 0, 2)).reshape(SUBLANES, ng * n)
    return mt.astype(BF16), bt.astype(BF16), ctr, cti, a_chunk.real, a_chunk.imag


def _s5(proj, mats):
    seq, width = proj.shape
    t, tc, gp, n, ng = S5_T, S5_TC, S5_GROUP, S5_STATE, S5_GROUPS
    rows = t * tc
    ncb = S5_CH // LANES
    cb0 = (width - S5_CH) // LANES
    u_specs = [pl.BlockSpec((rows, LANES), (lambda i, k=k: (i, cb0 + k))) for k in range(ncb)]
    nsb = ng * n // LANES
    return pl.pallas_call(
        _s5_kernel,
        grid=(seq // rows,),
        in_specs=u_specs + [_const_spec(m.shape) for m in mats],
        out_specs=pl.BlockSpec((rows, S5_CH), lambda i: (i, 0)),
        out_shape=jax.ShapeDtypeStruct((seq, S5_CH), F32),
        scratch_shapes=[
            pltpu.VMEM((t, S5_CH, tc), F32),
            pltpu.VMEM((t, S5_CH, tc), F32),
            pltpu.VMEM((ncb, rows, LANES), F32),
            pltpu.VMEM((ng * n, tc), F32),
            pltpu.VMEM((ng * n, tc), F32),
            pltpu.VMEM((nsb, tc, LANES), F32),
            pltpu.VMEM((nsb, tc, LANES), F32),
            pltpu.VMEM((2, SUBLANES, ng * n), F32),
        ],
        compiler_params=pltpu.CompilerParams(dimension_semantics=("arbitrary",)),
        name="s5_scan",
    )(*([proj] * ncb), *mats)


def _mix_ffn_kernel(*refs, glu, final):
    (x_ref, a_ref, b_ref, wo_ref, g1_ref), refs = refs[:5], refs[5:]
    if glu:
        gw_ref, refs = refs[0], refs[1:]
    (g_ref, sc_ref, sh_ref, gate_ref, win_ref, cw_ref, cb_ref, wout_ref), refs = refs[:8], refs[8:]
    if final:
        fg_ref, o_ref, h_ref, act_ref, gbuf_ref, carry_ref = refs
    else:
        ng_ref, nsc_ref, nsh_ref, o_ref, hn_ref, h_ref, act_ref, gbuf_ref, carry_ref = refs
    tm = x_ref.shape[0]
    halo = gbuf_ref.shape[0] - tm

    @pl.when(pl.program_id(0) == 0)
    def _():
        carry_ref[...] = jnp.zeros(carry_ref.shape, F32)

    if glu:
        y = jax.nn.gelu(b_ref[...]).astype(BF16)
        gg = _dot(y, gw_ref[...])
        half = gg.shape[1] // 2
        b = (gg[:, :half] * jax.nn.sigmoid(gg[:, half:])).astype(BF16)
    else:
        b = b_ref[...]
    cat = jnp.concatenate([a_ref[...], b], axis=1)
    x = x_ref[...] + g1_ref[...] * _dot(cat, wo_ref[...])
    h_ref[...] = _mod_rmsnorm(x, g_ref[...], sc_ref[...], sh_ref[...]).astype(BF16)
    for f in range(D_FF // TF_FFN):
        cs = slice(f * TF_FFN, (f + 1) * TF_FFN)
        gs = slice(D_FF + f * TF_FFN, D_FF + (f + 1) * TF_FFN)
        h = h_ref[...]
        val = _dot(h, win_ref[:, cs])
        gate = _dot(h, win_ref[:, gs])
        gbuf_ref[0:halo, :] = carry_ref[:, cs]
        gbuf_ref[halo:halo + tm, :] = gate
        carry_ref[:, cs] = gate[tm - halo:tm, :]
        conv = (gate * cw_ref[2:3, cs] + gbuf_ref[halo - 1:halo - 1 + tm, :] * cw_ref[1:2, cs]
                + gbuf_ref[halo - 2:halo - 2 + tm, :] * cw_ref[0:1, cs] + cb_ref[:, cs])
        act_ref[:, cs] = (jax.nn.gelu(conv) * val).astype(BF16)
    xn = x + gate_ref[...] * _dot(act_ref[...], wout_ref[...])
    if final:
        xn = xn * lax.rsqrt(jnp.mean(xn * xn, axis=-1, keepdims=True) + EPS) * fg_ref[...]
    else:
        hn_ref[...] = _mod_rmsnorm(xn, ng_ref[...], nsc_ref[...], nsh_ref[...]).astype(BF16)
    o_ref[...] = xn


def _layer_spec(shape, layer):
    idx = (layer,) + (0,) * (len(shape) - 1)
    return pl.BlockSpec((None,) + tuple(shape[1:]), lambda *_: idx, pipeline_mode=pl.Buffered(1))


def _mix_ffn(x, a, b, wo, gate1, glu_w, g, scale, shift, gate2, w_in, conv_w, conv_b, w_out, tail, layer):
    seq, d = x.shape
    final = len(tail) == 1
    tm = TM_FFN
    halo = SUBLANES
    row = pl.BlockSpec((1, d), lambda i: (0, 0))
    rows = lambda w: pl.BlockSpec((tm, w), lambda i: (i, 0))
    conv_b = conv_b.reshape(conv_b.shape[0], 1, D_FF)
    in_specs = [rows(d), rows(a.shape[1]), rows(b.shape[1]), _const_spec(wo.shape), row]
    args = [x, a, b, wo, gate1]
    if glu_w is not None:
        in_specs.append(_const_spec(glu_w.shape))
        args.append(glu_w)
    in_specs += [
        row, row, row, row,
        _layer_spec(w_in.shape, layer),
        _layer_spec(conv_w.shape, layer),
        _layer_spec(conv_b.shape, layer),
        _layer_spec(w_out.shape, layer),
    ] + [row] * len(tail)
    args += [g.reshape(1, d), scale, shift, gate2, w_in, conv_w, conv_b, w_out]
    args += [t.reshape(1, d) for t in tail]
    out_specs = [rows(d)] if final else [rows(d), rows(d)]
    out_shape = [jax.ShapeDtypeStruct((seq, d), F32)] + ([] if final else [jax.ShapeDtypeStruct((seq, d), BF16)])
    return pl.pallas_call(
        functools.partial(_mix_ffn_kernel, glu=glu_w is not None, final=final),
        grid=(seq // tm,),
        in_specs=in_specs,
        out_specs=out_specs,
        out_shape=out_shape,
        scratch_shapes=[
            pltpu.VMEM((tm, d), BF16),
            pltpu.VMEM((tm, D_FF), BF16),
            pltpu.VMEM((tm + halo, TF_FFN), F32),
            pltpu.VMEM((halo, D_FF), F32),
        ],
        compiler_params=pltpu.CompilerParams(dimension_semantics=("arbitrary",)),
        name="mix_ffn",
    )(*args)


def kernel(x, c, t5_table, mod_w, mod_b, norm1_g, norm2_g, ffn_w_in, ffn_conv_w, ffn_conv_b, ffn_w_out,
           ev_w_in, ev_w_out, diff_lambda, diff_subln_g, band_rel_bias,
           od_w_in, od_w_out, s5_lam_re, s5_lam_im, s5_log_step, s5_b_re, s5_b_im, s5_c_re, s5_c_im,
           s5_d, s5_glu_w, final_g):
    assert x.shape[0] == 1 and x.shape[2] == D_MODEL
    seq = x.shape[1]
    assert seq % TM_PROJ == 0 and seq % (S5_T * S5_TC) == 0
    d = D_MODEL
    xs = x[0]
    mod = _modulation(c, mod_w, mod_b)
    ffn_w_in_b = ffn_w_in.astype(BF16)
    ffn_w_out_b = ffn_w_out.astype(BF16)
    mods = [[mod[i, :, k * d:(k + 1) * d] for k in range(6)] for i in range(DEPTH)]
    h = None
    for i in range(DEPTH):
        sh1, sc1, g1, sh2, sc2, g2 = mods[i]
        w_in = (ev_w_in if i % 2 == 0 else od_w_in)[i // 2].astype(BF16)
        proj_dtype = BF16 if i % 2 == 0 else F32
        if h is None:
            proj = _normproj(xs, norm1_g[i], sc1, sh1, w_in, proj_dtype)
        else:
            proj = _proj(h, w_in, proj_dtype)
        if i % 2 == 0:
            e = i // 2
            lam_init = 0.8 - 0.6 * math.exp(-0.3 * i)
            lp = diff_lambda[e].astype(F32)
            lam = jnp.exp(jnp.sum(lp[0] * lp[1])) - jnp.exp(jnp.sum(lp[2] * lp[3])) + lam_init
            mix_a = _diff_attention(proj, t5_table, lam, diff_subln_g[e], lam_init)
            mix_b = _band_attention(proj, band_rel_bias[e])
            wo, glu_w = ev_w_out[e].astype(BF16), None
        else:
            o = i // 2
            mix_a = _retention(proj)
            mats = _s5_matrices(s5_lam_re[o], s5_lam_im[o], s5_log_step[o], s5_b_re[o], s5_b_im[o],
                                s5_c_re[o], s5_c_im[o], s5_d[o])
            mix_b = _s5(proj, mats)
            wo, glu_w = od_w_out[o].astype(BF16), s5_glu_w[o].astype(BF16)
        if i == DEPTH - 1:
            tail = (final_g,)
        else:
            nsh1, nsc1 = mods[i + 1][0], mods[i + 1][1]
            tail = (norm1_g[i + 1], nsc1, nsh1)
        out = _mix_ffn(xs, mix_a, mix_b, wo, g1, glu_w, norm2_g[i], sc2, sh2, g2,
                       ffn_w_in_b, ffn_conv_w, ffn_conv_b, ffn_w_out_b, tail, layer=i)
        if i == DEPTH - 1:
            xs = out[0]
        else:
            xs, h = out
    return xs[None]
```

```python
import functools
import math

import jax
import jax.numpy as jnp
import numpy as np
from jax import lax
from jax.experimental import pallas as pl
from jax.experimental.pallas import tpu as pltpu

F32 = jnp.float32
BF16 = jnp.bfloat16

D_MODEL = 1024
DEPTH = 2
CHUNK = 64
GROUP_WIDTH = D_MODEL // 2
DK_A = 64
DV_A = 2 * DK_A
N_HEADS_A = GROUP_WIDTH // DV_A
DH_B = 64
N_HEADS_B = GROUP_WIDTH // DH_B
LEFT_CHUNKS = 8
REL_CLIP = 2 * CHUNK
NUM_BUCKETS = 32
MAX_DISTANCE = 128
DV_C = 128
DQK_C = DV_C // 2
N_HEADS_C = GROUP_WIDTH // DV_C
ROPE_BASE = 10000.0
S5_CH = GROUP_WIDTH
S5_GROUP = 16
S5_GROUPS = S5_CH // S5_GROUP
S5_STATE = 64
D_FF = ((8 * D_MODEL // 3 + 255) // 256) * 256
CONV_W = 3
EVEN_IN = 3 * N_HEADS_A * DV_A + 3 * N_HEADS_B * DH_B
ODD_IN = 2 * N_HEADS_C * DQK_C + 2 * N_HEADS_C * DV_C + S5_CH
EPS = 1e-6
NEG_INF = -1e30
LOG2E = math.log2(math.e)

LANES = 128
SUBLANES = 8
MXU_DIM = 256

TM_PROJ = 1024
TN_PROJ = 1024
TN_MOD = 1536
TM_FFN = 512
TF_FFN = MXU_DIM
BLK_A = 512
NPART_A = 2
ONES_A = 16
SINGLE_PASS_LOG2_RANGE = 96.0
SCORE_PAD = LANES
BLK_B = 1024
BAND_B = LEFT_CHUNKS * CHUNK
QW_B = 4 * CHUNK
BLK_C = 512
S5_T = 16
S5_TC = LANES

assert BLK_B % BAND_B == 0 and BLK_B % QW_B == 0 and BAND_B % QW_B == 0
assert BLK_A >= MAX_DISTANCE, "far key blocks must sit in the saturated T5 bucket"
assert DV_A == LANES and 2 * DK_A == LANES and 2 * DH_B == LANES, "attention heads are read as 128-lane column blocks"


def _dot(a, b):
    return jnp.dot(a, b, preferred_element_type=F32)


def _dot_nt(a, b):
    return lax.dot_general(a, b, (((1,), (1,)), ((), ())), preferred_element_type=F32)


def _dot_tn(a, b):
    return lax.dot_general(a, b, (((0,), (0,)), ((), ())), preferred_element_type=F32)


def _const_spec(shape):
    zeros = (0,) * len(shape)
    return pl.BlockSpec(shape, lambda *_: zeros, pipeline_mode=pl.Buffered(1))


def _mod_rmsnorm(x, g, scale, shift):
    y = x * lax.rsqrt(jnp.mean(x * x, axis=-1, keepdims=True) + EPS)
    y = y * g
    return y * (1.0 + scale) + shift


def _mod_kernel(c_ref, w_ref, b_ref, o_ref):
    c = c_ref[...]
    cond = c * jax.nn.sigmoid(c)
    o_ref[0] = jnp.sum(cond * w_ref[0], axis=0, keepdims=True) + b_ref[0]


def _modulation(c, mod_w, mod_b):
    depth, d, n = mod_w.shape
    tn = TN_MOD
    return pl.pallas_call(
        _mod_kernel,
        grid=(depth, n // tn),
        in_specs=[
            pl.BlockSpec((d, 1), lambda i, j: (0, 0)),
            pl.BlockSpec((1, d, tn), lambda i, j: (i, 0, j)),
            pl.BlockSpec((1, 1, tn), lambda i, j: (i, 0, j)),
        ],
        out_specs=pl.BlockSpec((1, 1, tn), lambda i, j: (i, 0, j)),
        out_shape=jax.ShapeDtypeStruct((depth, 1, n), F32),
        name="modulation",
    )(c.reshape(d, 1), mod_w, mod_b.reshape(depth, 1, n))


def _normproj_kernel(x_ref, g_ref, sc_ref, sh_ref, w_ref, o_ref):
    tm, n = o_ref.shape
    half = tm // 2
    for r in range(2):
        rows = slice(r * half, (r + 1) * half)
        h = _mod_rmsnorm(x_ref[rows, :], g_ref[...], sc_ref[...], sh_ref[...]).astype(BF16)
        for j in range(n // TN_PROJ):
            cols = slice(j * TN_PROJ, (j + 1) * TN_PROJ)
            o_ref[rows, cols] = _dot(h, w_ref[:, cols]).astype(o_ref.dtype)


def _normproj(x, g, scale, shift, w, out_dtype):
    seq, d = x.shape
    n = w.shape[1]
    tm = TM_PROJ
    row = pl.BlockSpec((1, d), lambda i: (0, 0))
    return pl.pallas_call(
        _normproj_kernel,
        grid=(seq // tm,),
        in_specs=[pl.BlockSpec((tm, d), lambda i: (i, 0)), row, row, row, _const_spec(w.shape)],
        out_specs=pl.BlockSpec((tm, n), lambda i: (i, 0)),
        out_shape=jax.ShapeDtypeStruct((seq, n), out_dtype),
        compiler_params=pltpu.CompilerParams(dimension_semantics=("parallel",)),
        name="normproj",
    )(x, g.reshape(1, d), scale, shift, w)


def _proj_kernel(h_ref, w_ref, o_ref):
    for j in range(o_ref.shape[1] // TN_PROJ):
        cols = slice(j * TN_PROJ, (j + 1) * TN_PROJ)
        o_ref[:, cols] = _dot(h_ref[...], w_ref[:, cols]).astype(o_ref.dtype)


def _proj(h, w, out_dtype):
    seq, d = h.shape
    n = w.shape[1]
    tm = TM_PROJ
    return pl.pallas_call(
        _proj_kernel,
        grid=(seq // tm,),
        in_specs=[pl.BlockSpec((tm, d), lambda i: (i, 0)), _const_spec(w.shape)],
        out_specs=pl.BlockSpec((tm, n), lambda i: (i, 0)),
        out_shape=jax.ShapeDtypeStruct((seq, n), out_dtype),
        compiler_params=pltpu.CompilerParams(dimension_semantics=("parallel",)),
        name="proj",
    )(h, w)


def _diffattn_kernel(q_ref, qall_ref, k_ref, v_ref, bias_ref, bstat_ref, lam_ref, g_ref, o_ref,
                     flag_ref, qs_ref, vt_ref, kmax_ref, r_ref, m_ref, acc_ref, *s_refs, out_scale):
    blk = BLK_A
    nq = 2 * blk
    sub = SUBLANES
    dv = DV_A
    npart = NPART_A
    sa_ref, sb_ref = s_refs[:2 * npart], s_refs[2 * npart:4 * npart]
    pa_ref, pb_ref = s_refs[4 * npart:5 * npart], s_refs[5 * npart:6 * npart]
    i = pl.program_id(1)
    lane = lax.broadcasted_iota(jnp.int32, (blk, LANES), 1)
    same_subhead = (lax.broadcasted_iota(jnp.int32, (LANES, LANES), 0) // DK_A
                    == lax.broadcasted_iota(jnp.int32, (LANES, LANES), 1) // DK_A).astype(BF16)

    bias_max, bias_span = bstat_ref[0, 0:1, 0:1], bstat_ref[0, 1:2, 0:1]
    q_scale = DK_A ** -0.5 * LOG2E

    @pl.when(i == 0)
    def _():
        kmax_ref[...] = jnp.zeros(kmax_ref.shape, F32)

        def tr(b, qmax):
            r0 = pl.multiple_of(b * blk, blk)
            vt_ref[0:dv, pl.ds(r0, blk)] = v_ref[pl.ds(r0, blk), :].astype(F32).T.astype(BF16)
            vt_ref[dv:dv + ONES_A, pl.ds(r0, blk)] = jnp.ones((ONES_A, blk), BF16)
            kf = k_ref[pl.ds(r0, blk), :].astype(F32)
            kn2 = _dot((kf * kf).astype(BF16), same_subhead)
            kmax_ref[...] = jnp.maximum(kmax_ref[...], jnp.max(kn2.reshape(blk // sub, sub, LANES), axis=0))
            qa = (qall_ref[pl.ds(r0, blk), :].astype(F32) * q_scale).astype(BF16).astype(F32)
            qn2 = _dot((qa * qa).astype(BF16), same_subhead)
            return jnp.maximum(qmax, jnp.max(qn2.reshape(blk // sub, sub, LANES), axis=0))
        qmax2 = lax.fori_loop(0, v_ref.shape[0] // blk, tr, jnp.zeros((sub, LANES), F32))
        kmax2 = jnp.max(kmax_ref[...], axis=0, keepdims=True)
        kmax_ref[...] = jnp.broadcast_to(kmax2, kmax_ref.shape)
        worst = 2.0 * jnp.sqrt(jnp.max(qmax2, axis=0, keepdims=True) * kmax2) * 1.03 * 1.03 + bias_span
        flag_ref[0] = (jnp.max(worst) < SINGLE_PASS_LOG2_RANGE).astype(jnp.int32)

    q = (q_ref[...].astype(F32) * q_scale).astype(BF16)
    qf = q.astype(F32)
    qs_ref[:, 0:blk] = jnp.where(lane < DK_A, qf, 0.0).T.astype(BF16)
    qs_ref[:, blk:nq] = jnp.where(lane >= DK_A, qf, 0.0).T.astype(BF16)
    acc_ref[...] = jnp.zeros(acc_ref.shape, F32)

    qt = qs_ref[...].astype(F32)
    qn2 = jnp.sum((qt * qt).reshape(LANES // sub, sub, nq), axis=0)
    qn2 = jnp.broadcast_to(jnp.sum(qn2, axis=0, keepdims=True), (sub, nq))
    kmax2 = jnp.concatenate([jnp.broadcast_to(kmax_ref[:, m * DK_A:m * DK_A + 1], (sub, blk)) for m in range(2)],
                            axis=1)
    bound = jnp.sqrt(qn2 * kmax2) * 1.03
    r_ref[...] = bound + bias_max
    single_pass = flag_ref[0] == 1

    @pl.when(single_pass)
    def _():
        _diffattn_fixed_shift(i, k_ref, bias_ref, qs_ref, vt_ref, r_ref, m_ref, acc_ref, pa_ref, pb_ref)

    @pl.when(jnp.logical_not(single_pass))
    def _():
        _diffattn_online(i, k_ref, bias_ref, qs_ref, vt_ref, m_ref, acc_ref, sa_ref, sb_ref)

    ot = acc_ref[0:dv, 0:nq] / acc_ref[dv:dv + 1, 0:nq]
    o = ot[:, 0:blk].T - lam_ref[...] * ot[:, blk:nq].T
    o = o * lax.rsqrt(jnp.mean(o * o, axis=-1, keepdims=True) + EPS) * g_ref[...]
    o_ref[...] = (o * out_scale).astype(o_ref.dtype)


def _diffattn_fixed_shift(i, k_ref, bias_ref, qs_ref, vt_ref, shift_ref, l_ref, acc_ref, pa_ref, pb_ref):
    blk = BLK_A
    nq = 2 * blk
    sub = SUBLANES
    npart = len(pa_ref)
    wq = nq // npart
    l_ref[...] = jnp.zeros(l_ref.shape, F32)

    def probs(b, p_ref, bias):
        k = k_ref[pl.ds(pl.multiple_of(b * blk, blk), blk), :]
        for part in range(npart):
            cols = slice(part * wq, (part + 1) * wq)
            s = _dot(k, qs_ref[:, cols])
            if bias is not None:
                q0 = (part * wq) % blk
                s = s + bias[:, q0:q0 + wq]
            p = jnp.exp2(s.reshape(blk // sub, sub, wq) - shift_ref[:, cols][None])
            l_ref[:, cols] += jnp.sum(p, axis=0)
            p_ref[part][:, 0:wq] = p.reshape(blk, wq).astype(BF16)

    def accumulate(b, p_ref):
        vt = vt_ref[0:DV_A, pl.ds(pl.multiple_of(b * blk, blk), blk)]
        for part in range(npart):
            cols = slice(part * wq, (part + 1) * wq)
            acc_ref[0:DV_A, cols] += _dot(vt, p_ref[part][:, 0:wq])

    @pl.when(i == 0)
    def _():
        probs(0, pa_ref, bias_ref[0, 1])
        accumulate(0, pa_ref)

    @pl.when(i > 0)
    def _():
        nfar = i - 1
        probs(i, pa_ref, bias_ref[0, 1])
        probs(i - 1, pb_ref, bias_ref[0, 0])
        accumulate(i, pa_ref)

        def pair(t):
            probs(2 * t, pa_ref, None)
            accumulate(jnp.where(t == 0, i - 1, 2 * t - 1), pb_ref)
            probs(2 * t + 1, pb_ref, None)
            accumulate(2 * t, pa_ref)

        def four_pairs(u, carry):
            for v in range(4):
                pair(4 * u + v)
            return carry

        npairs = nfar // 2
        lax.fori_loop(0, npairs // 4, four_pairs, 0)

        def one_pair(t, carry):
            pair(t)
            return carry
        lax.fori_loop(4 * (npairs // 4), npairs, one_pair, 0)
        in_pb = jnp.where(npairs == 0, i - 1, 2 * npairs - 1)

        @pl.when(lax.rem(nfar, 2) == 1)
        def _():
            probs(nfar - 1, pa_ref, None)
            accumulate(in_pb, pb_ref)
            accumulate(nfar - 1, pa_ref)

        @pl.when(lax.rem(nfar, 2) == 0)
        def _():
            accumulate(in_pb, pb_ref)

    acc_ref[DV_A:DV_A + sub, 0:nq] = jnp.broadcast_to(jnp.sum(l_ref[...], axis=0, keepdims=True), (sub, nq))


def _diffattn_online(i, k_ref, bias_ref, qs_ref, vt_ref, m_ref, acc_ref, sa_ref, sb_ref):
    blk = BLK_A
    nq = 2 * blk
    sub = SUBLANES
    npart = len(sa_ref) // 2
    wq = nq // npart
    m_ref[...] = jnp.full(m_ref.shape, NEG_INF, F32)

    def scores(b, s_ref):
        k = k_ref[pl.ds(pl.multiple_of(b * blk, blk), blk), :]
        for part in range(npart):
            s = _dot(k, qs_ref[:, part * wq:(part + 1) * wq])
            s_ref[part][:, 0:wq] = s
            s_ref[npart + part][...] = jnp.max(s.reshape(blk // sub, sub, wq), axis=0)

    def softmax_pv(b, s_ref, bias):
        vt = vt_ref[:, pl.ds(pl.multiple_of(b * blk, blk), blk)]
        for part in range(npart):
            cols = slice(part * wq, (part + 1) * wq)
            s = s_ref[part][:, 0:wq]
            if bias is not None:
                q0 = (part * wq) % blk
                s = s + bias[:, q0:q0 + wq]
            s = s.reshape(blk // sub, sub, wq)
            m_prev = m_ref[:, cols]
            smax = jnp.max(s, axis=0) if bias is not None else s_ref[npart + part][...]
            m_cur = jnp.max(smax, axis=0, keepdims=True)
            m_new = jnp.maximum(m_prev, m_cur)
            alpha = jnp.exp2(m_prev - m_new)
            p = jnp.exp2(s - m_new[None])
            pv = _dot(vt, p.reshape(blk, wq).astype(BF16))
            acc_ref[:, cols] = acc_ref[:, cols] * alpha[0:1] + pv
            m_ref[:, cols] = m_new

    nfar = jnp.maximum(i - 1, 0)
    odd = lax.rem(nfar, 2)

    @pl.when(i == 0)
    def _():
        scores(0, sb_ref)

    @pl.when(i > 0)
    def _():
        @pl.when(odd == 1)
        def _():
            scores(0, sb_ref)
            scores(1, sa_ref)
            softmax_pv(0, sb_ref, None)

        @pl.when(odd == 0)
        def _():
            scores(0, sa_ref)

        def pair(b):
            scores(b + 1, sb_ref)
            softmax_pv(b, sa_ref, None)
            scores(b + 2, sa_ref)
            softmax_pv(b + 1, sb_ref, None)

        def quad_body(t, carry):
            pair(odd + 4 * t)
            pair(odd + 4 * t + 2)
            return carry

        npairs = nfar // 2
        lax.fori_loop(0, npairs // 2, quad_body, 0)

        @pl.when(lax.rem(npairs, 2) == 1)
        def _():
            pair(odd + 2 * (npairs - 1))
        scores(i, sb_ref)
        softmax_pv(i - 1, sa_ref, bias_ref[0, 0])

    softmax_pv(i, sb_ref, bias_ref[0, 1])


_TOEPLITZ_ROWS = 256
_TOEPLITZ_N = 2048


def _toeplitz_kernel(v_ref, o_ref, *, keep):
    rows, cols = o_ref.shape[2:]
    x = jnp.broadcast_to(v_ref[0, 0], (rows, v_ref.shape[-1]))
    tile = pltpu.roll(x, 0, 1, stride=1, stride_axis=0)[:, :cols]
    r = lax.broadcasted_iota(jnp.int32, (rows, cols), 0) + pl.program_id(1) * rows
    c = lax.broadcasted_iota(jnp.int32, (rows, cols), 1)
    for variant in range(o_ref.shape[0]):
        o_ref[variant, 0] = jnp.where(keep(r, c, variant), tile, NEG_INF)


def _toeplitz_tiles(fn, keep, heads, rows, cols, variants=1):
    n, rb = _TOEPLITZ_N, _TOEPLITZ_ROWS
    assert rows % rb == 0 and rows <= n // 2 and cols <= n // 2
    idx = jnp.arange(n, dtype=jnp.int32)
    vec = fn(jnp.where(idx < n // 2, idx, idx - n)).astype(F32)
    vecs = jnp.stack([jnp.roll(vec, k * rb, axis=1) for k in range(rows // rb)], axis=1)
    return pl.pallas_call(
        functools.partial(_toeplitz_kernel, keep=keep),
        grid=(heads, rows // rb),
        in_specs=[pl.BlockSpec((1, 1, 1, n), lambda h, k: (h, k, 0, 0))],
        out_specs=pl.BlockSpec((variants, 1, rb, cols), lambda h, k: (0, h, k, 0)),
        out_shape=jax.ShapeDtypeStruct((variants, heads, rows, cols), F32),
        name="toeplitz_tiles",
    )(vecs.reshape(heads, rows // rb, 1, n))


def _bias_stats(tiles):
    finite = tiles > 0.5 * NEG_INF
    bias_max = jnp.maximum(jnp.max(jnp.where(finite, tiles, NEG_INF), axis=(1, 2)), 0.0)
    bias_min = jnp.minimum(jnp.min(jnp.where(finite, tiles, -NEG_INF), axis=(1, 2)), 0.0)
    return jnp.broadcast_to(jnp.stack([bias_max, bias_max - bias_min], axis=1)[:, :, None],
                            (tiles.shape[0], 2, LANES))


def _t5_bucket(rel):
    nb = NUM_BUCKETS // 2
    max_exact = nb // 2
    bucket = jnp.where(rel > 0, nb, 0)
    n = jnp.abs(rel)
    nf = jnp.maximum(n, 1).astype(F32)
    large = max_exact + (jnp.log(nf / max_exact) / math.log(MAX_DISTANCE / max_exact)
                         * (nb - max_exact)).astype(jnp.int32)
    large = jnp.minimum(large, nb - 1)
    return bucket + jnp.where(n < max_exact, n, large)


def _diff_bias_tiles(t5_table):
    blk = BLK_A
    table = t5_table.astype(F32)
    far = table[_t5_bucket(jnp.full((), -(blk + 1), jnp.int32))]
    def visible(r, c, variant):
        return jnp.floor_divide(r - blk, CHUNK) <= jnp.floor_divide(c, CHUNK)

    tiles = _toeplitz_tiles(lambda x: ((table[_t5_bucket(-x - blk)] - far) * LOG2E).T, visible,
                            N_HEADS_A, 2 * blk, blk)
    return tiles.reshape(N_HEADS_A, 2, blk, blk)


def _diff_attention(proj, t5_table, lam, subln_g, lam_init):
    seq = proj.shape[0]
    blk = BLK_A
    bias = _diff_bias_tiles(t5_table)
    ha = N_HEADS_A
    bstat = _bias_stats(bias.reshape(ha, 2 * blk, blk))
    kern = functools.partial(_diffattn_kernel, out_scale=1.0 - lam_init)
    return pl.pallas_call(
        kern,
        grid=(ha, seq // blk),
        in_specs=[
            pl.BlockSpec((blk, DV_A), lambda h, i: (i, h)),
            pl.BlockSpec((seq, DV_A), lambda h, i: (0, h)),
            pl.BlockSpec((seq, DV_A), lambda h, i: (0, ha + h)),
            pl.BlockSpec((seq, DV_A), lambda h, i: (0, 2 * ha + h)),
            pl.BlockSpec((1, 2, blk, blk), lambda h, i: (h, 0, 0, 0)),
            pl.BlockSpec((1, 2, LANES), lambda h, i: (h, 0, 0)),
            pl.BlockSpec((1, DV_A), lambda h, i: (0, 0)),
            pl.BlockSpec((1, DV_A), lambda h, i: (0, 0)),
        ],
        out_specs=pl.BlockSpec((blk, DV_A), lambda h, i: (i, h)),
        out_shape=jax.ShapeDtypeStruct((seq, ha * DV_A), BF16),
        scratch_shapes=[
            pltpu.SMEM((1,), jnp.int32),
            pltpu.VMEM((DV_A, 2 * blk), BF16),
            pltpu.VMEM((DV_A + ONES_A, seq), BF16),
            pltpu.VMEM((SUBLANES, LANES), F32),
            pltpu.VMEM((SUBLANES, 2 * blk), F32),
            pltpu.VMEM((SUBLANES, 2 * blk), F32),
            pltpu.VMEM((DV_A + ONES_A, 2 * blk), F32),
        ] + 2 * ([pltpu.VMEM((blk, 2 * blk // NPART_A + SCORE_PAD), F32)] * NPART_A
                 + [pltpu.VMEM((SUBLANES, 2 * blk // NPART_A), F32)] * NPART_A)
        + 2 * [pltpu.VMEM((blk, 2 * blk // NPART_A + SCORE_PAD), BF16)] * NPART_A,
        compiler_params=pltpu.CompilerParams(dimension_semantics=("parallel", "arbitrary")),
        name="diff_attention",
    )(proj, proj, proj, proj, bias, bstat, jnp.full((1, DV_A), lam, F32), subln_g.reshape(1, DV_A).astype(F32))


def _band_kernel(q_ref, kp_ref, kc_ref, vp_ref, vc_ref, *refs):
    qw, band = QW_B, BAND_B
    nbias = band // qw + 1
    bias_refs, bstat_ref, o_ref = refs[:nbias], refs[nbias], refs[nbias + 1]
    ngroups = (len(refs) - nbias - 2) // 2
    s_refs, p_refs = refs[nbias + 2:nbias + 2 + ngroups], refs[nbias + 2 + ngroups:]
    nk = band + qw
    sub = SUBLANES
    q = (q_ref[...].astype(F32) * (DH_B ** -0.5 * LOG2E)).astype(BF16).astype(F32)
    lane = lax.broadcasted_iota(jnp.int32, q.shape, 1)
    qt = (jnp.where(lane < DH_B, q, 0.0).T, jnp.where(lane >= DH_B, q, 0.0).T)
    qh = (qt[0].astype(BF16), qt[1].astype(BF16))
    k_all = jnp.concatenate([kp_ref[...], kc_ref[...]], axis=0)
    vt_all = jnp.concatenate([vp_ref[...], vc_ref[...]], axis=0).astype(F32).T.astype(BF16)

    def group_operands(g):
        k0 = g * qw
        qs = jnp.concatenate([qh[0][:, k0:k0 + qw], qh[1][:, k0:k0 + qw]], axis=1)
        bias_ref = bias_refs[min(g, nbias - 1)]
        bias = jnp.concatenate([bias_ref[0, 0], bias_ref[0, 1]], axis=1)
        return k0, qs, bias

    def store_group(g, ot):
        o = jnp.concatenate([ot[0:DH_B, 0:qw], ot[DH_B:2 * DH_B, qw:2 * qw]], axis=0)
        o_ref[g * qw:(g + 1) * qw, :] = o.T.astype(o_ref.dtype)

    same_head = (lax.broadcasted_iota(jnp.int32, (LANES, LANES), 0) // DH_B
                 == lax.broadcasted_iota(jnp.int32, (LANES, LANES), 1) // DH_B).astype(BF16)
    kf = k_all.astype(F32)
    kn2 = _dot((kf * kf).astype(BF16), same_head)
    kmax2 = jnp.max(jnp.max(kn2.reshape(kn2.shape[0] // sub, sub, LANES), axis=0), axis=0, keepdims=True)
    shifts, worst = [], None
    for m in range(2):
        qn2 = jnp.sum(qt[m] * qt[m], axis=0, keepdims=True)
        bound = jnp.sqrt(qn2 * kmax2[:, m * DH_B:m * DH_B + 1]) * 1.03
        shifts.append(bound + bstat_ref[m, 0:1, 0:1])
        spread = jnp.max(2.0 * bound + bstat_ref[m, 1:2, 0:1])
        worst = spread if worst is None else jnp.maximum(worst, spread)
    fixed_shift = worst < SINGLE_PASS_LOG2_RANGE

    @pl.when(fixed_shift)
    def _():
        sums = []
        for g in range(ngroups):
            k0, qs, bias = group_operands(g)
            r = jnp.concatenate([shifts[0][:, k0:k0 + qw], shifts[1][:, k0:k0 + qw]], axis=1)
            s = _dot(k_all[k0:k0 + nk], qs) + (bias - r)
            p = jnp.exp2(s).reshape(nk // sub, sub, 2 * qw)
            sums.append(jnp.sum(jnp.sum(p, axis=0), axis=0, keepdims=True))
            p_refs[g][:, 0:2 * qw] = p.reshape(nk, 2 * qw).astype(BF16)
        for g in range(ngroups):
            k0 = g * qw
            store_group(g, _dot(vt_all[:, k0:k0 + nk], p_refs[g][:, 0:2 * qw]) / sums[g])

    @pl.when(jnp.logical_not(fixed_shift))
    def _():
        vt_ones = jnp.concatenate([vt_all, jnp.ones((ONES_A, vt_all.shape[1]), BF16)], axis=0)
        for g in range(ngroups):
            k0, qs, _ = group_operands(g)
            s_refs[g][:, 0:2 * qw] = _dot(k_all[k0:k0 + nk], qs)
        for g in range(ngroups):
            k0, _, bias = group_operands(g)
            s = (s_refs[g][:, 0:2 * qw] + bias).reshape(nk // sub, sub, 2 * qw)
            m = jnp.max(jnp.max(s, axis=0), axis=0, keepdims=True)
            p = jnp.exp2(s - m[None])
            pv = _dot(vt_ones[:, k0:k0 + nk], p.reshape(nk, 2 * qw).astype(BF16))
            store_group(g, pv[0:2 * DH_B] / pv[2 * DH_B:2 * DH_B + 1])


def _band_bias_tiles(rel_bias):
    band = BAND_B

    def valid(r, c, variant):
        qchunk = jnp.floor_divide(c, CHUNK)
        kchunk = jnp.floor_divide(r - band, CHUNK)
        missing = jnp.where(variant == 0, 0, band - (variant - 1) * QW_B)
        return (kchunk <= qchunk) & (kchunk >= qchunk - LEFT_CHUNKS) & (r >= missing)

    return _toeplitz_tiles(
        lambda x: rel_bias.astype(F32)[:, jnp.clip(-x - band, -REL_CLIP, REL_CLIP) + REL_CLIP] * LOG2E, valid,
        N_HEADS_B, band + QW_B, QW_B, variants=1 + band // QW_B)


def _band_attention(proj, rel_bias):
    seq = proj.shape[0]
    blk, band, qw = BLK_B, BAND_B, QW_B
    bias = _band_bias_tiles(rel_bias)
    npair = N_HEADS_B // 2
    qc0 = 3 * N_HEADS_A
    per = blk // band
    prev = lambda c0: (lambda hp, i: (jnp.maximum(i * per - 1, 0), c0 + hp))
    cur = lambda c0: (lambda hp, i: (i, c0 + hp))
    return pl.pallas_call(
        _band_kernel,
        grid=(npair, seq // blk),
        in_specs=[
            pl.BlockSpec((blk, LANES), cur(qc0)),
            pl.BlockSpec((band, LANES), prev(qc0 + npair)),
            pl.BlockSpec((blk, LANES), cur(qc0 + npair)),
            pl.BlockSpec((band, LANES), prev(qc0 + 2 * npair)),
            pl.BlockSpec((blk, LANES), cur(qc0 + 2 * npair)),
        ] + [
            pl.BlockSpec((1, 2, band + qw, qw), (lambda hp, i, t=t: (jnp.where(i == 0, 1 + t, 0), hp, 0, 0)))
            for t in range(band // qw)
        ] + [
            pl.BlockSpec((1, 2, band + qw, qw), lambda hp, i: (0, hp, 0, 0)),
            pl.BlockSpec((2, 2, LANES), lambda hp, i: (hp, 0, 0)),
        ],
        out_specs=pl.BlockSpec((blk, LANES), lambda hp, i: (i, hp)),
        out_shape=jax.ShapeDtypeStruct((seq, N_HEADS_B * DH_B), BF16),
        scratch_shapes=([pltpu.VMEM((band + qw, 2 * qw + SCORE_PAD), F32)] * (blk // qw)
                        + [pltpu.VMEM((band + qw, 2 * qw + SCORE_PAD), BF16)] * (blk // qw)),
        compiler_params=pltpu.CompilerParams(dimension_semantics=("parallel", "arbitrary")),
        name="band_attention",
    )(proj, proj, proj, proj, proj, *([bias] * bias.shape[0]), _bias_stats(bias[0]))


def _retention_kernel(qk_ref, v_ref, gate_ref, cos_ref, sin_ref, qdec_ref, kdec_ref, dmat_ref,
                      sdec_ref, o_ref, state_ref):
    @pl.when(pl.program_id(0) == 0)
    def _():
        state_ref[...] = jnp.zeros(state_ref.shape, F32)

    cos = cos_ref[...]
    sin = sin_ref[...]
    lane = lax.broadcasted_iota(jnp.int32, cos.shape, 1)
    first_half = (lane % DQK_C) < (DQK_C // 2)
    qk = qk_ref[...]
    parts = []
    for j in range(qk.shape[1] // LANES):
        t = qk[:, j * LANES:(j + 1) * LANES]
        partner = jnp.where(first_half, pltpu.roll(t, LANES - DQK_C // 2, 1), pltpu.roll(t, DQK_C // 2, 1))
        parts.append(t * cos + partner * sin)
    wq = N_HEADS_C * DQK_C
    q = jnp.concatenate(parts[:wq // LANES], axis=1)
    k = jnp.concatenate(parts[wq // LANES:], axis=1) * (DQK_C ** -0.5)
    qd = (q * qdec_ref[...]).astype(BF16)
    kd = (k * kdec_ref[...]).astype(BF16)
    qb = q.astype(BF16)
    kb = k.astype(BF16)
    vb = v_ref[...].astype(BF16)
    gate = gate_ref[...]
    outs = []
    for h in range(N_HEADS_C):
        qs = slice(h * DQK_C, (h + 1) * DQK_C)
        vs = slice(h * DV_C, (h + 1) * DV_C)
        scores = _dot_nt(qb[:, qs], kb[:, qs]) * dmat_ref[h]
        state = state_ref[h]
        r = _dot(scores.astype(BF16), vb[:, vs]) + _dot(qd[:, qs], state.astype(BF16))
        state_ref[h] = state * sdec_ref[h] + _dot_tn(kd[:, qs], vb[:, vs])
        r = r * lax.rsqrt(jnp.mean(r * r, axis=-1, keepdims=True) + EPS)
        g = gate[:, vs]
        outs.append(r * (g * jax.nn.sigmoid(g)))
    o_ref[...] = jnp.concatenate(outs, axis=1).astype(o_ref.dtype)


def _retention_tables(seq):
    t = BLK_C
    half = DQK_C // 2
    inv_freq = 1.0 / np.power(ROPE_BASE, np.arange(0, DQK_C, 2, dtype=np.float64) / DQK_C)
    ang = np.arange(seq, dtype=np.float64)[:, None] * inv_freq[None, :]
    reps = LANES // half
    cos = np.tile(np.cos(ang), (1, reps))
    sign = np.where((np.arange(LANES) % DQK_C) < half, -1.0, 1.0)
    sin = np.tile(np.sin(ang), (1, reps)) * sign[None, :]
    log_g = np.log(1.0 - np.power(2.0, -5.0 - np.arange(N_HEADS_C, dtype=np.float64)))
    pos = np.arange(t, dtype=np.float64)
    diff = pos[:, None] - pos[None, :]
    same_or_past = (np.arange(t)[None, :] // CHUNK) <= (np.arange(t)[:, None] // CHUNK)
    dmat = np.where(same_or_past[None], np.exp(log_g[:, None, None] * np.abs(diff)[None]), 0.0)
    qdec = np.repeat(np.exp(log_g[None, :] * (pos[:, None] + 1.0)), DQK_C, axis=1)
    kdec = np.repeat(np.exp(log_g[None, :] * (t - 1.0 - pos[:, None])), DQK_C, axis=1)
    sdec = np.broadcast_to(np.exp(log_g * t)[:, None, None], (N_HEADS_C, 1, DV_C))
    return tuple(jnp.asarray(a.astype(np.float32)) for a in (cos, sin, qdec, kdec, dmat, sdec))


def _retention(proj):
    seq = proj.shape[0]
    t = BLK_C
    cos, sin, qdec, kdec, dmat, sdec = _retention_tables(seq)
    wv = N_HEADS_C * DV_C
    return pl.pallas_call(
        _retention_kernel,
        grid=(seq // t,),
        in_specs=[
            pl.BlockSpec((t, wv), lambda i: (i, 0)),
            pl.BlockSpec((t, wv), lambda i: (i, 1)),
            pl.BlockSpec((t, wv), lambda i: (i, 2)),
            pl.BlockSpec((t, LANES), lambda i: (i, 0)),
            pl.BlockSpec((t, LANES), lambda i: (i, 0)),
            pl.BlockSpec((t, N_HEADS_C * DQK_C), lambda i: (0, 0)),
            pl.BlockSpec((t, N_HEADS_C * DQK_C), lambda i: (0, 0)),
            pl.BlockSpec((N_HEADS_C, t, t), lambda i: (0, 0, 0)),
            pl.BlockSpec((N_HEADS_C, 1, DV_C), lambda i: (0, 0, 0)),
        ],
        out_specs=pl.BlockSpec((t, wv), lambda i: (i, 0)),
        out_shape=jax.ShapeDtypeStruct((seq, wv), BF16),
        scratch_shapes=[pltpu.VMEM((N_HEADS_C, DQK_C, DV_C), F32)],
        compiler_params=pltpu.CompilerParams(dimension_semantics=("arbitrary",)),
        name="retention",
    )(proj, proj, proj, cos, sin, qdec, kdec, dmat, sdec)


def _s5_kernel(*refs):
    ncb = S5_CH // LANES
    u_refs = refs[:ncb]
    (mt_ref, bt_ref, ctr_ref, cti_ref, are_ref, aim_ref, y_ref,
     ut_ref, yt_ref, ys_ref, vr_ref, vi_ref, spr_ref, spi_ref, carry_ref) = refs[ncb:]
    tc = S5_TC
    gp = S5_GROUP
    n = S5_STATE
    ng = S5_GROUPS

    @pl.when(pl.program_id(0) == 0)
    def _():
        carry_ref[...] = jnp.zeros(carry_ref.shape, F32)

    for s in range(S5_T):
        for k in range(ncb):
            ut_ref[s, k * LANES:(k + 1) * LANES, :] = u_refs[k][pl.ds(s, tc, stride=S5_T), :].T

    unroll = 4

    def intra(it, carry):
        for k in range(unroll):
            g = it * unroll + k
            r0 = pl.multiple_of(g * gp, gp)
            ug = ut_ref[:, pl.ds(r0, gp), :].reshape(S5_T * gp, tc).astype(BF16)
            yt_ref[:, pl.ds(r0, gp), :] = _dot(mt_ref[g], ug).reshape(S5_T, gp, tc)
            vt = _dot(bt_ref[g], ug)
            n0 = pl.multiple_of(g * n, n)
            vr_ref[pl.ds(n0, n), :] = vt[0:n]
            vi_ref[pl.ds(n0, n), :] = vt[n:2 * n]
        return carry

    lax.fori_loop(0, ng // unroll, intra, 0)

    sub = SUBLANES
    nv = tc // sub
    row = lax.broadcasted_iota(jnp.int32, (tc, LANES), 0)
    in_vreg = lax.rem(row, sub)

    def rows_of(v, r):
        return jnp.broadcast_to(v[r:r + 1], (tc, LANES))

    for j in range(ng * n // LANES):
        cols = slice(j * LANES, (j + 1) * LANES)
        pwr, pwi = are_ref[:, cols], aim_ref[:, cols]
        xr = vr_ref[cols, :].T
        xi = vi_ref[cols, :].T
        for d in (1, 2, 4):
            keep = in_vreg >= d
            sr = jnp.where(keep, pltpu.roll(xr, d, 0), 0.0)
            si = jnp.where(keep, pltpu.roll(xi, d, 0), 0.0)
            fr, fi = rows_of(pwr, d - 1), rows_of(pwi, d - 1)
            xr, xi = xr + (fr * sr - fi * si), xi + (fr * si + fi * sr)
        cr, ci = carry_ref[0, :, cols], carry_ref[1, :, cols]
        cr0, ci0 = cr, ci
        outr, outi = [], []
        for v in range(nv):
            yr = xr[v * sub:(v + 1) * sub] + (pwr * cr - pwi * ci)
            yi = xi[v * sub:(v + 1) * sub] + (pwr * ci + pwi * cr)
            outr.append(yr)
            outi.append(yi)
            cr = jnp.broadcast_to(yr[sub - 1:sub], (sub, LANES))
            ci = jnp.broadcast_to(yi[sub - 1:sub], (sub, LANES))
        carry_ref[0, :, cols] = cr
        carry_ref[1, :, cols] = ci
        sr = jnp.concatenate(outr, axis=0)
        si = jnp.concatenate(outi, axis=0)
        first = row == 0
        spr_ref[j] = jnp.where(first, rows_of(cr0, 0), pltpu.roll(sr, 1, 0))
        spi_ref[j] = jnp.where(first, rows_of(ci0, 0), pltpu.roll(si, 1, 0))

    def cross(it, carry):
        for k in range(unroll):
            jp = it * unroll + k
            r0 = pl.multiple_of(jp * 2 * gp, 2 * gp)
            yc = (_dot_nt(ctr_ref[jp], spr_ref[jp].astype(BF16))
                  + _dot_nt(cti_ref[jp], spi_ref[jp].astype(BF16)))
            yt_ref[:, pl.ds(r0, 2 * gp), :] += yc.reshape(S5_T, 2 * gp, tc)
        return carry

    lax.fori_loop(0, ng // 2 // unroll, cross, 0)

    for s in range(S5_T):
        for k in range(ncb):
            ys_ref[k, pl.ds(s, tc, stride=S5_T), :] = yt_ref[s, k * LANES:(k + 1) * LANES, :].T
    for k in range(ncb):
        y_ref[:, k * LANES:(k + 1) * LANES] = ys_ref[k]


def _s5_matrices(lam_re, lam_im, log_step, b_re, b_im, c_re, c_im, d_skip):
    hi = lax.Precision.HIGHEST
    t, gp, n, ng = S5_T, S5_GROUP, S5_STATE, S5_GROUPS
    lam = lax.complex(lam_re.astype(F32), lam_im.astype(F32))
    step = jnp.exp(log_step.astype(F32))[:, None]
    ls = lam * step
    a_bar = jnp.exp(ls)
    b_bar = ((a_bar - 1.0) / lam)[..., None] * lax.complex(b_re.astype(F32), b_im.astype(F32))
    cm = lax.complex(c_re.astype(F32), c_im.astype(F32))

    def apow(k):
        kk = k.astype(F32).astype(jnp.complex64)
        return jnp.exp(ls.reshape((ng,) + (1,) * k.ndim + (n,)) * kk[None, ..., None])

    tt = jnp.arange(t)
    kmat = jnp.einsum('gpn,gln,gnq->glpq', cm, apow(tt), b_bar, precision=hi).real
    krev = jnp.transpose(kmat[:, ::-1], (0, 2, 1, 3)).reshape(ng, gp, t * gp)
    kpad = jnp.pad(krev, ((0, 0), (0, 0), (0, t * gp)))
    mt = jnp.concatenate([kpad[:, :, (t - 1 - to) * gp:(2 * t - 1 - to) * gp] for to in range(t)], axis=1)
    dvec = jnp.tile(d_skip.astype(F32).reshape(ng, 1, gp), (1, t, 1)).reshape(ng, t * gp)
    mt = mt + jnp.eye(t * gp, dtype=F32)[None] * dvec[:, :, None]
    z = jnp.swapaxes(apow(t - 1 - tt), 1, 2)[:, :, :, None] * b_bar[:, :, None, :]
    z = z.reshape(ng, n, t * gp)
    bt = jnp.concatenate([z.real, z.imag], axis=1)
    w = cm[:, None, :, :] * apow(tt + 1)[:, :, None, :]

    def pair_readout(x):
        x = x.reshape(ng // 2, 2, t, gp, n)
        first = jnp.pad(x[:, 0], ((0, 0), (0, 0), (0, 0), (0, n)))
        second = jnp.pad(x[:, 1], ((0, 0), (0, 0), (0, 0), (n, 0)))
        return jnp.stack([first, second], axis=2).reshape(ng // 2, t * 2 * gp, 2 * n).astype(BF16)

    ctr, cti = pair_readout(w.real), pair_readout(-w.imag)
    a_chunk = jnp.transpose(apow(t * (jnp.arange(SUBLANES) + 1)), (1, 0, 2)).reshape(SUBLANES, ng * n)
    return mt.astype(BF16), bt.astype(BF16), ctr, cti, a_chunk.real, a_chunk.imag


def _s5(proj, mats):
    seq, width = proj.shape
    t, tc, gp, n, ng = S5_T, S5_TC, S5_GROUP, S5_STATE, S5_GROUPS
    rows = t * tc
    ncb = S5_CH // LANES
    cb0 = (width - S5_CH) // LANES
    u_specs = [pl.BlockSpec((rows, LANES), (lambda i, k=k: (i, cb0 + k))) for k in range(ncb)]
    nsb = ng * n // LANES
    return pl.pallas_call(
        _s5_kernel,
        grid=(seq // rows,),
        in_specs=u_specs + [_const_spec(m.shape) for m in mats],
        out_specs=pl.BlockSpec((rows, S5_CH), lambda i: (i, 0)),
        out_shape=jax.ShapeDtypeStruct((seq, S5_CH), F32),
        scratch_shapes=[
            pltpu.VMEM((t, S5_CH, tc), F32),
            pltpu.VMEM((t, S5_CH, tc), F32),
            pltpu.VMEM((ncb, rows, LANES), F32),
            pltpu.VMEM((ng * n, tc), F32),
            pltpu.VMEM((ng * n, tc), F32),
            pltpu.VMEM((nsb, tc, LANES), F32),
            pltpu.VMEM((nsb, tc, LANES), F32),
            pltpu.VMEM((2, SUBLANES, ng * n), F32),
        ],
        compiler_params=pltpu.CompilerParams(dimension_semantics=("arbitrary",)),
        name="s5_scan",
    )(*([proj] * ncb), *mats)


def _mix_ffn_kernel(*refs, glu, final):
    (x_ref, a_ref, b_ref, wo_ref, g1_ref), refs = refs[:5], refs[5:]
    if glu:
        gw_ref, refs = refs[0], refs[1:]
    (g_ref, sc_ref, sh_ref, gate_ref, win_ref, cw_ref, cb_ref, wout_ref), refs = refs[:8], refs[8:]
    if final:
        fg_ref, o_ref, h_ref, act_ref, gbuf_ref, carry_ref = refs
    else:
        ng_ref, nsc_ref, nsh_ref, o_ref, hn_ref, h_ref, act_ref, gbuf_ref, carry_ref = refs
    tm = x_ref.shape[0]
    halo = gbuf_ref.shape[0] - tm

    @pl.when(pl.program_id(0) == 0)
    def _():
        carry_ref[...] = jnp.zeros(carry_ref.shape, F32)

    if glu:
        y = jax.nn.gelu(b_ref[...]).astype(BF16)
        gg = _dot(y, gw_ref[...])
        half = gg.shape[1] // 2
        b = (gg[:, :half] * jax.nn.sigmoid(gg[:, half:])).astype(BF16)
    else:
        b = b_ref[...]
    cat = jnp.concatenate([a_ref[...], b], axis=1)
    x = x_ref[...] + g1_ref[...] * _dot(cat, wo_ref[...])
    h_ref[...] = _mod_rmsnorm(x, g_ref[...], sc_ref[...], sh_ref[...]).astype(BF16)
    for f in range(D_FF // TF_FFN):
        cs = slice(f * TF_FFN, (f + 1) * TF_FFN)
        gs = slice(D_FF + f * TF_FFN, D_FF + (f + 1) * TF_FFN)
        h = h_ref[...]
        val = _dot(h, win_ref[:, cs])
        gate = _dot(h, win_ref[:, gs])
        gbuf_ref[0:halo, :] = carry_ref[:, cs]
        gbuf_ref[halo:halo + tm, :] = gate
        carry_ref[:, cs] = gate[tm - halo:tm, :]
        conv = (gate * cw_ref[2:3, cs] + gbuf_ref[halo - 1:halo - 1 + tm, :] * cw_ref[1:2, cs]
                + gbuf_ref[halo - 2:halo - 2 + tm, :] * cw_ref[0:1, cs] + cb_ref[:, cs])
        act_ref[:, cs] = (jax.nn.gelu(conv) * val).astype(BF16)
    xn = x + gate_ref[...] * _dot(act_ref[...], wout_ref[...])
    if final:
        xn = xn * lax.rsqrt(jnp.mean(xn * xn, axis=-1, keepdims=True) + EPS) * fg_ref[...]
    else:
        hn_ref[...] = _mod_rmsnorm(xn, ng_ref[...], nsc_ref[...], nsh_ref[...]).astype(BF16)
    o_ref[...] = xn


def _layer_spec(shape, layer):
    idx = (layer,) + (0,) * (len(shape) - 1)
    return pl.BlockSpec((None,) + tuple(shape[1:]), lambda *_: idx, pipeline_mode=pl.Buffered(1))


def _mix_ffn(x, a, b, wo, gate1, glu_w, g, scale, shift, gate2, w_in, conv_w, conv_b, w_out, tail, layer):
    seq, d = x.shape
    final = len(tail) == 1
    tm = TM_FFN
    halo = SUBLANES
    row = pl.BlockSpec((1, d), lambda i: (0, 0))
    rows = lambda w: pl.BlockSpec((tm, w), lambda i: (i, 0))
    conv_b = conv_b.reshape(conv_b.shape[0], 1, D_FF)
    in_specs = [rows(d), rows(a.shape[1]), rows(b.shape[1]), _const_spec(wo.shape), row]
    args = [x, a, b, wo, gate1]
    if glu_w is not None:
        in_specs.append(_const_spec(glu_w.shape))
        args.append(glu_w)
    in_specs += [
        row, row, row, row,
        _layer_spec(w_in.shape, layer),
        _layer_spec(conv_w.shape, layer),
        _layer_spec(conv_b.shape, layer),
        _layer_spec(w_out.shape, layer),
    ] + [row] * len(tail)
    args += [g.reshape(1, d), scale, shift, gate2, w_in, conv_w, conv_b, w_out]
    args += [t.reshape(1, d) for t in tail]
    out_specs = [rows(d)] if final else [rows(d), rows(d)]
    out_shape = [jax.ShapeDtypeStruct((seq, d), F32)] + ([] if final else [jax.ShapeDtypeStruct((seq, d), BF16)])
    return pl.pallas_call(
        functools.partial(_mix_ffn_kernel, glu=glu_w is not None, final=final),
        grid=(seq // tm,),
        in_specs=in_specs,
        out_specs=out_specs,
        out_shape=out_shape,
        scratch_shapes=[
            pltpu.VMEM((tm, d), BF16),
            pltpu.VMEM((tm, D_FF), BF16),
            pltpu.VMEM((tm + halo, TF_FFN), F32),
            pltpu.VMEM((halo, D_FF), F32),
        ],
        compiler_params=pltpu.CompilerParams(dimension_semantics=("arbitrary",)),
        name="mix_ffn",
    )(*args)


def kernel(x, c, t5_table, mod_w, mod_b, norm1_g, norm2_g, ffn_w_in, ffn_conv_w, ffn_conv_b, ffn_w_out,
           ev_w_in, ev_w_out, diff_lambda, diff_subln_g, band_rel_bias,
           od_w_in, od_w_out, s5_lam_re, s5_lam_im, s5_log_step, s5_b_re, s5_b_im, s5_c_re, s5_c_im,
           s5_d, s5_glu_w, final_g):
    assert x.shape[0] == 1 and x.shape[2] == D_MODEL
    seq = x.shape[1]
    assert seq % TM_PROJ == 0 and seq % (S5_T * S5_TC) == 0
    d = D_MODEL
    xs = x[0]
    mod = _modulation(c, mod_w, mod_b)
    ffn_w_in_b = ffn_w_in.astype(BF16)
    ffn_w_out_b = ffn_w_out.astype(BF16)
    mods = [[mod[i, :, k * d:(k + 1) * d] for k in range(6)] for i in range(DEPTH)]
    h = None
    for i in range(DEPTH):
        sh1, sc1, g1, sh2, sc2, g2 = mods[i]
        w_in = (ev_w_in if i % 2 == 0 else od_w_in)[i // 2].astype(BF16)
        proj_dtype = BF16 if i % 2 == 0 else F32
        if h is None:
            proj = _normproj(xs, norm1_g[i], sc1, sh1, w_in, proj_dtype)
        else:
            proj = _proj(h, w_in, proj_dtype)
        if i % 2 == 0:
            e = i // 2
            lam_init = 0.8 - 0.6 * math.exp(-0.3 * i)
            lp = diff_lambda[e].astype(F32)
            lam = jnp.exp(jnp.sum(lp[0] * lp[1])) - jnp.exp(jnp.sum(lp[2] * lp[3])) + lam_init
            mix_a = _diff_attention(proj, t5_table, lam, diff_subln_g[e], lam_init)
            mix_b = _band_attention(proj, band_rel_bias[e])
            wo, glu_w = ev_w_out[e].astype(BF16), None
        else:
            o = i // 2
            mix_a = _retention(proj)
            mats = _s5_matrices(s5_lam_re[o], s5_lam_im[o], s5_log_step[o], s5_b_re[o], s5_b_im[o],
                                s5_c_re[o], s5_c_im[o], s5_d[o])
            mix_b = _s5(proj, mats)
            wo, glu_w = od_w_out[o].astype(BF16), s5_glu_w[o].astype(BF16)
        if i == DEPTH - 1:
            tail = (final_g,)
        else:
            nsh1, nsc1 = mods[i + 1][0], mods[i + 1][1]
            tail = (norm1_g[i + 1], nsc1, nsh1)
        out = _mix_ffn(xs, mix_a, mix_b, wo, g1, glu_w, norm2_g[i], sc2, sh2, g2,
                       ffn_w_in_b, ffn_conv_w, ffn_conv_b, ffn_w_out_b, tail, layer=i)
        if i == DEPTH - 1:
            xs = out[0]
        else:
            xs, h = out
    return xs[None]
```

```python
import functools
import math

import jax
import jax.numpy as jnp
import numpy as np
from jax import lax
from jax.experimental import pallas as pl
from jax.experimental.pallas import tpu as pltpu

F32 = jnp.float32
BF16 = jnp.bfloat16

D_MODEL = 1024
DEPTH = 2
CHUNK = 64
GROUP_WIDTH = D_MODEL // 2
DK_A = 64
DV_A = 2 * DK_A
N_HEADS_A = GROUP_WIDTH // DV_A
DH_B = 64
N_HEADS_B = GROUP_WIDTH // DH_B
LEFT_CHUNKS = 8
REL_CLIP = 2 * CHUNK
NUM_BUCKETS = 32
MAX_DISTANCE = 128
DV_C = 128
DQK_C = DV_C // 2
N_HEADS_C = GROUP_WIDTH // DV_C
ROPE_BASE = 10000.0
S5_CH = GROUP_WIDTH
S5_GROUP = 16
S5_GROUPS = S5_CH // S5_GROUP
S5_STATE = 64
D_FF = ((8 * D_MODEL // 3 + 255) // 256) * 256
CONV_W = 3
EVEN_IN = 3 * N_HEADS_A * DV_A + 3 * N_HEADS_B * DH_B
ODD_IN = 2 * N_HEADS_C * DQK_C + 2 * N_HEADS_C * DV_C + S5_CH
EPS = 1e-6
NEG_INF = -1e30
LOG2E = math.log2(math.e)

LANES = 128
SUBLANES = 8
MXU_DIM = 256

TM_PROJ = 1024
TN_PROJ = 1024
TN_MOD = 1536
TM_FFN = 512
TF_FFN = MXU_DIM
BLK_A = 512
NPART_A = 2
ONES_A = 16
SINGLE_PASS_LOG2_RANGE = 96.0
SCORE_PAD = LANES
BLK_B = 1024
BAND_B = LEFT_CHUNKS * CHUNK
QW_B = 4 * CHUNK
BLK_C = 512
S5_T = 16
S5_TC = LANES

assert BLK_B % BAND_B == 0 and BLK_B % QW_B == 0 and BAND_B % QW_B == 0
assert BLK_A >= MAX_DISTANCE, "far key blocks must sit in the saturated T5 bucket"
assert DV_A == LANES and 2 * DK_A == LANES and 2 * DH_B == LANES, "attention heads are read as 128-lane column blocks"


def _dot(a, b):
    return jnp.dot(a, b, preferred_element_type=F32)


def _dot_nt(a, b):
    return lax.dot_general(a, b, (((1,), (1,)), ((), ())), preferred_element_type=F32)


def _dot_tn(a, b):
    return lax.dot_general(a, b, (((0,), (0,)), ((), ())), preferred_element_type=F32)


def _const_spec(shape):
    zeros = (0,) * len(shape)
    return pl.BlockSpec(shape, lambda *_: zeros, pipeline_mode=pl.Buffered(1))


def _mod_rmsnorm(x, g, scale, shift):
    y = x * lax.rsqrt(jnp.mean(x * x, axis=-1, keepdims=True) + EPS)
    y = y * g
    return y * (1.0 + scale) + shift


def _mod_kernel(c_ref, w_ref, b_ref, o_ref):
    c = c_ref[...]
    cond = c * jax.nn.sigmoid(c)
    o_ref[0] = jnp.sum(cond * w_ref[0], axis=0, keepdims=True) + b_ref[0]


def _modulation(c, mod_w, mod_b):
    depth, d, n = mod_w.shape
    tn = TN_MOD
    return pl.pallas_call(
        _mod_kernel,
        grid=(depth, n // tn),
        in_specs=[
            pl.BlockSpec((d, 1), lambda i, j: (0, 0)),
            pl.BlockSpec((1, d, tn), lambda i, j: (i, 0, j)),
            pl.BlockSpec((1, 1, tn), lambda i, j: (i, 0, j)),
        ],
        out_specs=pl.BlockSpec((1, 1, tn), lambda i, j: (i, 0, j)),
        out_shape=jax.ShapeDtypeStruct((depth, 1, n), F32),
        name="modulation",
    )(c.reshape(d, 1), mod_w, mod_b.reshape(depth, 1, n))


def _normproj_kernel(x_ref, g_ref, sc_ref, sh_ref, w_ref, o_ref):
    tm, n = o_ref.shape
    half = tm // 2
    for r in range(2):
        rows = slice(r * half, (r + 1) * half)
        h = _mod_rmsnorm(x_ref[rows, :], g_ref[...], sc_ref[...], sh_ref[...]).astype(BF16)
        for j in range(n // TN_PROJ):
            cols = slice(j * TN_PROJ, (j + 1) * TN_PROJ)
            o_ref[rows, cols] = _dot(h, w_ref[:, cols]).astype(o_ref.dtype)


def _normproj(x, g, scale, shift, w, out_dtype):
    seq, d = x.shape
    n = w.shape[1]
    tm = TM_PROJ
    row = pl.BlockSpec((1, d), lambda i: (0, 0))
    return pl.pallas_call(
        _normproj_kernel,
        grid=(seq // tm,),
        in_specs=[pl.BlockSpec((tm, d), lambda i: (i, 0)), row, row, row, _const_spec(w.shape)],
        out_specs=pl.BlockSpec((tm, n), lambda i: (i, 0)),
        out_shape=jax.ShapeDtypeStruct((seq, n), out_dtype),
        compiler_params=pltpu.CompilerParams(dimension_semantics=("parallel",)),
        name="normproj",
    )(x, g.reshape(1, d), scale, shift, w)


def _proj_kernel(h_ref, w_ref, o_ref):
    for j in range(o_ref.shape[1] // TN_PROJ):
        cols = slice(j * TN_PROJ, (j + 1) * TN_PROJ)
        o_ref[:, cols] = _dot(h_ref[...], w_ref[:, cols]).astype(o_ref.dtype)


def _proj(h, w, out_dtype):
    seq, d = h.shape
    n = w.shape[1]
    tm = TM_PROJ
    return pl.pallas_call(
        _proj_kernel,
        grid=(seq // tm,),
        in_specs=[pl.BlockSpec((tm, d), lambda i: (i, 0)), _const_spec(w.shape)],
        out_specs=pl.BlockSpec((tm, n), lambda i: (i, 0)),
        out_shape=jax.ShapeDtypeStruct((seq, n), out_dtype),
        compiler_params=pltpu.CompilerParams(dimension_semantics=("parallel",)),
        name="proj",
    )(h, w)


def _diffattn_kernel(q_ref, qall_ref, k_ref, v_ref, bias_ref, bstat_ref, lam_ref, g_ref, o_ref,
                     flag_ref, qs_ref, vt_ref, kmax_ref, r_ref, m_ref, acc_ref, *s_refs, out_scale):
    blk = BLK_A
    nq = 2 * blk
    sub = SUBLANES
    dv = DV_A
    npart = NPART_A
    sa_ref, sb_ref = s_refs[:2 * npart], s_refs[2 * npart:4 * npart]
    pa_ref, pb_ref = s_refs[4 * npart:5 * npart], s_refs[5 * npart:6 * npart]
    i = pl.program_id(1)
    lane = lax.broadcasted_iota(jnp.int32, (blk, LANES), 1)
    same_subhead = (lax.broadcasted_iota(jnp.int32, (LANES, LANES), 0) // DK_A
                    == lax.broadcasted_iota(jnp.int32, (LANES, LANES), 1) // DK_A).astype(BF16)

    bias_max, bias_span = bstat_ref[0, 0:1, 0:1], bstat_ref[0, 1:2, 0:1]
    q_scale = DK_A ** -0.5 * LOG2E

    @pl.when(i == 0)
    def _():
        kmax_ref[...] = jnp.zeros(kmax_ref.shape, F32)

        def tr(b, qmax):
            r0 = pl.multiple_of(b * blk, blk)
            vt_ref[0:dv, pl.ds(r0, blk)] = v_ref[pl.ds(r0, blk), :].astype(F32).T.astype(BF16)
            vt_ref[dv:dv + ONES_A, pl.ds(r0, blk)] = jnp.ones((ONES_A, blk), BF16)
            kf = k_ref[pl.ds(r0, blk), :].astype(F32)
            kn2 = _dot((kf * kf).astype(BF16), same_subhead)
            kmax_ref[...] = jnp.maximum(kmax_ref[...], jnp.max(kn2.reshape(blk // sub, sub, LANES), axis=0))
            qa = (qall_ref[pl.ds(r0, blk), :].astype(F32) * q_scale).astype(BF16).astype(F32)
            qn2 = _dot((qa * qa).astype(BF16), same_subhead)
            return jnp.maximum(qmax, jnp.max(qn2.reshape(blk // sub, sub, LANES), axis=0))
        qmax2 = lax.fori_loop(0, v_ref.shape[0] // blk, tr, jnp.zeros((sub, LANES), F32))
        kmax2 = jnp.max(kmax_ref[...], axis=0, keepdims=True)
        kmax_ref[...] = jnp.broadcast_to(kmax2, kmax_ref.shape)
        worst = 2.0 * jnp.sqrt(jnp.max(qmax2, axis=0, keepdims=True) * kmax2) * 1.03 * 1.03 + bias_span
        flag_ref[0] = (jnp.max(worst) < SINGLE_PASS_LOG2_RANGE).astype(jnp.int32)

    q = (q_ref[...].astype(F32) * q_scale).astype(BF16)
    qf = q.astype(F32)
    qs_ref[:, 0:blk] = jnp.where(lane < DK_A, qf, 0.0).T.astype(BF16)
    qs_ref[:, blk:nq] = jnp.where(lane >= DK_A, qf, 0.0).T.astype(BF16)
    acc_ref[...] = jnp.zeros(acc_ref.shape, F32)

    qt = qs_ref[...].astype(F32)
    qn2 = jnp.sum((qt * qt).reshape(LANES // sub, sub, nq), axis=0)
    qn2 = jnp.broadcast_to(jnp.sum(qn2, axis=0, keepdims=True), (sub, nq))
    kmax2 = jnp.concatenate([jnp.broadcast_to(kmax_ref[:, m * DK_A:m * DK_A + 1], (sub, blk)) for m in range(2)],
                            axis=1)
    bound = jnp.sqrt(qn2 * kmax2) * 1.03
    r_ref[...] = bound + bias_max
    single_pass = flag_ref[0] == 1

    @pl.when(single_pass)
    def _():
        _diffattn_fixed_shift(i, k_ref, bias_ref, qs_ref, vt_ref, r_ref, m_ref, acc_ref, pa_ref, pb_ref)

    @pl.when(jnp.logical_not(single_pass))
    def _():
        _diffattn_online(i, k_ref, bias_ref, qs_ref, vt_ref, m_ref, acc_ref, sa_ref, sb_ref)

    ot = acc_ref[0:dv, 0:nq] / acc_ref[dv:dv + 1, 0:nq]
    o = ot[:, 0:blk].T - lam_ref[...] * ot[:, blk:nq].T
    o = o * lax.rsqrt(jnp.mean(o * o, axis=-1, keepdims=True) + EPS) * g_ref[...]
    o_ref[...] = (o * out_scale).astype(o_ref.dtype)


def _diffattn_fixed_shift(i, k_ref, bias_ref, qs_ref, vt_ref, shift_ref, l_ref, acc_ref, pa_ref, pb_ref):
    blk = BLK_A
    nq = 2 * blk
    sub = SUBLANES
    npart = len(pa_ref)
    wq = nq // npart
    l_ref[...] = jnp.zeros(l_ref.shape, F32)

    def probs(b, p_ref, bias):
        k = k_ref[pl.ds(pl.multiple_of(b * blk, blk), blk), :]
        for part in range(npart):
            cols = slice(part * wq, (part + 1) * wq)
            s = _dot(k, qs_ref[:, cols])
            if bias is not None:
                q0 = (part * wq) % blk
                s = s + bias[:, q0:q0 + wq]
            p = jnp.exp2(s.reshape(blk // sub, sub, wq) - shift_ref[:, cols][None])
            l_ref[:, cols] += jnp.sum(p, axis=0)
            p_ref[part][:, 0:wq] = p.reshape(blk, wq).astype(BF16)

    def accumulate(b, p_ref):
        vt = vt_ref[0:DV_A, pl.ds(pl.multiple_of(b * blk, blk), blk)]
        for part in range(npart):
            cols = slice(part * wq, (part + 1) * wq)
            acc_ref[0:DV_A, cols] += _dot(vt, p_ref[part][:, 0:wq])

    @pl.when(i == 0)
    def _():
        probs(0, pa_ref, bias_ref[0, 1])
        accumulate(0, pa_ref)

    @pl.when(i > 0)
    def _():
        nfar = i - 1
        probs(i, pa_ref, bias_ref[0, 1])
        probs(i - 1, pb_ref, bias_ref[0, 0])
        accumulate(i, pa_ref)

        def pair(t):
            probs(2 * t, pa_ref, None)
            accumulate(jnp.where(t == 0, i - 1, 2 * t - 1), pb_ref)
            probs(2 * t + 1, pb_ref, None)
            accumulate(2 * t, pa_ref)

        def four_pairs(u, carry):
            for v in range(4):
                pair(4 * u + v)
            return carry

        npairs = nfar // 2
        lax.fori_loop(0, npairs // 4, four_pairs, 0)

        def one_pair(t, carry):
            pair(t)
            return carry
        lax.fori_loop(4 * (npairs // 4), npairs, one_pair, 0)
        in_pb = jnp.where(npairs == 0, i - 1, 2 * npairs - 1)

        @pl.when(lax.rem(nfar, 2) == 1)
        def _():
            probs(nfar - 1, pa_ref, None)
            accumulate(in_pb, pb_ref)
            accumulate(nfar - 1, pa_ref)

        @pl.when(lax.rem(nfar, 2) == 0)
        def _():
            accumulate(in_pb, pb_ref)

    acc_ref[DV_A:DV_A + sub, 0:nq] = jnp.broadcast_to(jnp.sum(l_ref[...], axis=0, keepdims=True), (sub, nq))


def _diffattn_online(i, k_ref, bias_ref, qs_ref, vt_ref, m_ref, acc_ref, sa_ref, sb_ref):
    blk = BLK_A
    nq = 2 * blk
    sub = SUBLANES
    npart = len(sa_ref) // 2
    wq = nq // npart
    m_ref[...] = jnp.full(m_ref.shape, NEG_INF, F32)

    def scores(b, s_ref):
        k = k_ref[pl.ds(pl.multiple_of(b * blk, blk), blk), :]
        for part in range(npart):
            s = _dot(k, qs_ref[:, part * wq:(part + 1) * wq])
            s_ref[part][:, 0:wq] = s
            s_ref[npart + part][...] = jnp.max(s.reshape(blk // sub, sub, wq), axis=0)

    def softmax_pv(b, s_ref, bias):
        vt = vt_ref[:, pl.ds(pl.multiple_of(b * blk, blk), blk)]
        for part in range(npart):
            cols = slice(part * wq, (part + 1) * wq)
            s = s_ref[part][:, 0:wq]
            if bias is not None:
                q0 = (part * wq) % blk
                s = s + bias[:, q0:q0 + wq]
            s = s.reshape(blk // sub, sub, wq)
            m_prev = m_ref[:, cols]
            smax = jnp.max(s, axis=0) if bias is not None else s_ref[npart + part][...]
            m_cur = jnp.max(smax, axis=0, keepdims=True)
            m_new = jnp.maximum(m_prev, m_cur)
            alpha = jnp.exp2(m_prev - m_new)
            p = jnp.exp2(s - m_new[None])
            pv = _dot(vt, p.reshape(blk, wq).astype(BF16))
            acc_ref[:, cols] = acc_ref[:, cols] * alpha[0:1] + pv
            m_ref[:, cols] = m_new

    nfar = jnp.maximum(i - 1, 0)
    odd = lax.rem(nfar, 2)

    @pl.when(i == 0)
    def _():
        scores(0, sb_ref)

    @pl.when(i > 0)
    def _():
        @pl.when(odd == 1)
        def _():
            scores(0, sb_ref)
            scores(1, sa_ref)
            softmax_pv(0, sb_ref, None)

        @pl.when(odd == 0)
        def _():
            scores(0, sa_ref)

        def pair(b):
            scores(b + 1, sb_ref)
            softmax_pv(b, sa_ref, None)
            scores(b + 2, sa_ref)
            softmax_pv(b + 1, sb_ref, None)

        def quad_body(t, carry):
            pair(odd + 4 * t)
            pair(odd + 4 * t + 2)
            return carry

        npairs = nfar // 2
        lax.fori_loop(0, npairs // 2, quad_body, 0)

        @pl.when(lax.rem(npairs, 2) == 1)
        def _():
            pair(odd + 2 * (npairs - 1))
        scores(i, sb_ref)
        softmax_pv(i - 1, sa_ref, bias_ref[0, 0])

    softmax_pv(i, sb_ref, bias_ref[0, 1])


_TOEPLITZ_ROWS = 256


def _toeplitz_kernel(v_ref, o_ref, *, keep):
    rows, cols = o_ref.shape[2:]
    x = jnp.broadcast_to(v_ref[0, 0], (rows, v_ref.shape[-1]))
    tile = pltpu.roll(x, 0, 1, stride=1, stride_axis=0)[:, :cols]
    r = lax.broadcasted_iota(jnp.int32, (rows, cols), 0) + pl.program_id(1) * rows
    c = lax.broadcasted_iota(jnp.int32, (rows, cols), 1)
    for variant in range(o_ref.shape[0]):
        o_ref[variant, 0] = jnp.where(keep(r, c, variant), tile, NEG_INF)


def _toeplitz_tiles(fn, keep, heads, rows, cols, variants=1):
    n, rb = rows + cols, _TOEPLITZ_ROWS
    assert rows % rb == 0 and n % LANES == 0
    idx = jnp.arange(n, dtype=jnp.int32)
    vec = fn(jnp.where(idx < cols, idx, idx - n)).astype(F32)
    vecs = jnp.stack([jnp.roll(vec, k * rb, axis=1) for k in range(rows // rb)], axis=1)
    return pl.pallas_call(
        functools.partial(_toeplitz_kernel, keep=keep),
        grid=(heads, rows // rb),
        in_specs=[pl.BlockSpec((1, 1, 1, n), lambda h, k: (h, k, 0, 0))],
        out_specs=pl.BlockSpec((variants, 1, rb, cols), lambda h, k: (0, h, k, 0)),
        out_shape=jax.ShapeDtypeStruct((variants, heads, rows, cols), F32),
        name="toeplitz_tiles",
    )(vecs.reshape(heads, rows // rb, 1, n))


def _bias_stats(tiles):
    finite = tiles > 0.5 * NEG_INF
    bias_max = jnp.maximum(jnp.max(jnp.where(finite, tiles, NEG_INF), axis=(1, 2)), 0.0)
    bias_min = jnp.minimum(jnp.min(jnp.where(finite, tiles, -NEG_INF), axis=(1, 2)), 0.0)
    return jnp.broadcast_to(jnp.stack([bias_max, bias_max - bias_min], axis=1)[:, :, None],
                            (tiles.shape[0], 2, LANES))


def _t5_bucket(rel):
    nb = NUM_BUCKETS // 2
    max_exact = nb // 2
    bucket = jnp.where(rel > 0, nb, 0)
    n = jnp.abs(rel)
    nf = jnp.maximum(n, 1).astype(F32)
    large = max_exact + (jnp.log(nf / max_exact) / math.log(MAX_DISTANCE / max_exact)
                         * (nb - max_exact)).astype(jnp.int32)
    large = jnp.minimum(large, nb - 1)
    return bucket + jnp.where(n < max_exact, n, large)


def _diff_bias_tiles(t5_table):
    blk = BLK_A
    table = t5_table.astype(F32)
    far = table[_t5_bucket(jnp.full((), -(blk + 1), jnp.int32))]
    def visible(r, c, variant):
        return jnp.floor_divide(r - blk, CHUNK) <= jnp.floor_divide(c, CHUNK)

    tiles = _toeplitz_tiles(lambda x: ((table[_t5_bucket(-x - blk)] - far) * LOG2E).T, visible,
                            N_HEADS_A, 2 * blk, blk)
    return tiles.reshape(N_HEADS_A, 2, blk, blk)


def _diff_attention(proj, t5_table, lam, subln_g, lam_init):
    seq = proj.shape[0]
    blk = BLK_A
    bias = _diff_bias_tiles(t5_table)
    ha = N_HEADS_A
    bstat = _bias_stats(bias.reshape(ha, 2 * blk, blk))
    kern = functools.partial(_diffattn_kernel, out_scale=1.0 - lam_init)
    return pl.pallas_call(
        kern,
        grid=(ha, seq // blk),
        in_specs=[
            pl.BlockSpec((blk, DV_A), lambda h, i: (i, h)),
            pl.BlockSpec((seq, DV_A), lambda h, i: (0, h)),
            pl.BlockSpec((seq, DV_A), lambda h, i: (0, ha + h)),
            pl.BlockSpec((seq, DV_A), lambda h, i: (0, 2 * ha + h)),
            pl.BlockSpec((1, 2, blk, blk), lambda h, i: (h, 0, 0, 0)),
            pl.BlockSpec((1, 2, LANES), lambda h, i: (h, 0, 0)),
            pl.BlockSpec((1, DV_A), lambda h, i: (0, 0)),
            pl.BlockSpec((1, DV_A), lambda h, i: (0, 0)),
        ],
        out_specs=pl.BlockSpec((blk, DV_A), lambda h, i: (i, h)),
        out_shape=jax.ShapeDtypeStruct((seq, ha * DV_A), BF16),
        scratch_shapes=[
            pltpu.SMEM((1,), jnp.int32),
            pltpu.VMEM((DV_A, 2 * blk), BF16),
            pltpu.VMEM((DV_A + ONES_A, seq), BF16),
            pltpu.VMEM((SUBLANES, LANES), F32),
            pltpu.VMEM((SUBLANES, 2 * blk), F32),
            pltpu.VMEM((SUBLANES, 2 * blk), F32),
            pltpu.VMEM((DV_A + ONES_A, 2 * blk), F32),
        ] + 2 * ([pltpu.VMEM((blk, 2 * blk // NPART_A + SCORE_PAD), F32)] * NPART_A
                 + [pltpu.VMEM((SUBLANES, 2 * blk // NPART_A), F32)] * NPART_A)
        + 2 * [pltpu.VMEM((blk, 2 * blk // NPART_A + SCORE_PAD), BF16)] * NPART_A,
        compiler_params=pltpu.CompilerParams(dimension_semantics=("parallel", "arbitrary")),
        name="diff_attention",
    )(proj, proj, proj, proj, bias, bstat, jnp.full((1, DV_A), lam, F32), subln_g.reshape(1, DV_A).astype(F32))


def _band_kernel(q_ref, kp_ref, kc_ref, vp_ref, vc_ref, *refs):
    qw, band = QW_B, BAND_B
    nbias = band // qw + 1
    bias_refs, bstat_ref, o_ref = refs[:nbias], refs[nbias], refs[nbias + 1]
    ngroups = (len(refs) - nbias - 2) // 2
    s_refs, p_refs = refs[nbias + 2:nbias + 2 + ngroups], refs[nbias + 2 + ngroups:]
    nk = band + qw
    sub = SUBLANES
    q = (q_ref[...].astype(F32) * (DH_B ** -0.5 * LOG2E)).astype(BF16).astype(F32)
    lane = lax.broadcasted_iota(jnp.int32, q.shape, 1)
    qt = (jnp.where(lane < DH_B, q, 0.0).T, jnp.where(lane >= DH_B, q, 0.0).T)
    qh = (qt[0].astype(BF16), qt[1].astype(BF16))
    k_all = jnp.concatenate([kp_ref[...], kc_ref[...]], axis=0)
    vt_all = jnp.concatenate([vp_ref[...], vc_ref[...]], axis=0).astype(F32).T.astype(BF16)

    def group_operands(g):
        k0 = g * qw
        qs = jnp.concatenate([qh[0][:, k0:k0 + qw], qh[1][:, k0:k0 + qw]], axis=1)
        bias_ref = bias_refs[min(g, nbias - 1)]
        bias = jnp.concatenate([bias_ref[0, 0], bias_ref[0, 1]], axis=1)
        return k0, qs, bias

    def store_group(g, ot):
        o = jnp.concatenate([ot[0:DH_B, 0:qw], ot[DH_B:2 * DH_B, qw:2 * qw]], axis=0)
        o_ref[g * qw:(g + 1) * qw, :] = o.T.astype(o_ref.dtype)

    same_head = (lax.broadcasted_iota(jnp.int32, (LANES, LANES), 0) // DH_B
                 == lax.broadcasted_iota(jnp.int32, (LANES, LANES), 1) // DH_B).astype(BF16)
    kf = k_all.astype(F32)
    kn2 = _dot((kf * kf).astype(BF16), same_head)
    kmax2 = jnp.max(jnp.max(kn2.reshape(kn2.shape[0] // sub, sub, LANES), axis=0), axis=0, keepdims=True)
    shifts, worst = [], None
    for m in range(2):
        qn2 = jnp.sum(qt[m] * qt[m], axis=0, keepdims=True)
        bound = jnp.sqrt(qn2 * kmax2[:, m * DH_B:m * DH_B + 1]) * 1.03
        shifts.append(bound + bstat_ref[m, 0:1, 0:1])
        spread = jnp.max(2.0 * bound + bstat_ref[m, 1:2, 0:1])
        worst = spread if worst is None else jnp.maximum(worst, spread)
    fixed_shift = worst < SINGLE_PASS_LOG2_RANGE

    @pl.when(fixed_shift)
    def _():
        sums = []
        for g in range(ngroups):
            k0, qs, bias = group_operands(g)
            r = jnp.concatenate([shifts[0][:, k0:k0 + qw], shifts[1][:, k0:k0 + qw]], axis=1)
            s = _dot(k_all[k0:k0 + nk], qs) + (bias - r)
            p = jnp.exp2(s).reshape(nk // sub, sub, 2 * qw)
            sums.append(jnp.sum(jnp.sum(p, axis=0), axis=0, keepdims=True))
            p_refs[g][:, 0:2 * qw] = p.reshape(nk, 2 * qw).astype(BF16)
        for g in range(ngroups):
            k0 = g * qw
            store_group(g, _dot(vt_all[:, k0:k0 + nk], p_refs[g][:, 0:2 * qw]) / sums[g])

    @pl.when(jnp.logical_not(fixed_shift))
    def _():
        vt_ones = jnp.concatenate([vt_all, jnp.ones((ONES_A, vt_all.shape[1]), BF16)], axis=0)
        for g in range(ngroups):
            k0, qs, _ = group_operands(g)
            s_refs[g][:, 0:2 * qw] = _dot(k_all[k0:k0 + nk], qs)
        for g in range(ngroups):
            k0, _, bias = group_operands(g)
            s = (s_refs[g][:, 0:2 * qw] + bias).reshape(nk // sub, sub, 2 * qw)
            m = jnp.max(jnp.max(s, axis=0), axis=0, keepdims=True)
            p = jnp.exp2(s - m[None])
            pv = _dot(vt_ones[:, k0:k0 + nk], p.reshape(nk, 2 * qw).astype(BF16))
            store_group(g, pv[0:2 * DH_B] / pv[2 * DH_B:2 * DH_B + 1])


def _band_bias_tiles(rel_bias):
    band = BAND_B

    def valid(r, c, variant):
        qchunk = jnp.floor_divide(c, CHUNK)
        kchunk = jnp.floor_divide(r - band, CHUNK)
        missing = jnp.where(variant == 0, 0, band - (variant - 1) * QW_B)
        return (kchunk <= qchunk) & (kchunk >= qchunk - LEFT_CHUNKS) & (r >= missing)

    return _toeplitz_tiles(
        lambda x: rel_bias.astype(F32)[:, jnp.clip(-x - band, -REL_CLIP, REL_CLIP) + REL_CLIP] * LOG2E, valid,
        N_HEADS_B, band + QW_B, QW_B, variants=1 + band // QW_B)


def _band_attention(proj, rel_bias):
    seq = proj.shape[0]
    blk, band, qw = BLK_B, BAND_B, QW_B
    bias = _band_bias_tiles(rel_bias)
    npair = N_HEADS_B // 2
    qc0 = 3 * N_HEADS_A
    per = blk // band
    prev = lambda c0: (lambda hp, i: (jnp.maximum(i * per - 1, 0), c0 + hp))
    cur = lambda c0: (lambda hp, i: (i, c0 + hp))
    return pl.pallas_call(
        _band_kernel,
        grid=(npair, seq // blk),
        in_specs=[
            pl.BlockSpec((blk, LANES), cur(qc0)),
            pl.BlockSpec((band, LANES), prev(qc0 + npair)),
            pl.BlockSpec((blk, LANES), cur(qc0 + npair)),
            pl.BlockSpec((band, LANES), prev(qc0 + 2 * npair)),
            pl.BlockSpec((blk, LANES), cur(qc0 + 2 * npair)),
        ] + [
            pl.BlockSpec((1, 2, band + qw, qw), (lambda hp, i, t=t: (jnp.where(i == 0, 1 + t, 0), hp, 0, 0)))
            for t in range(band // qw)
        ] + [
            pl.BlockSpec((1, 2, band + qw, qw), lambda hp, i: (0, hp, 0, 0)),
            pl.BlockSpec((2, 2, LANES), lambda hp, i: (hp, 0, 0)),
        ],
        out_specs=pl.BlockSpec((blk, LANES), lambda hp, i: (i, hp)),
        out_shape=jax.ShapeDtypeStruct((seq, N_HEADS_B * DH_B), BF16),
        scratch_shapes=([pltpu.VMEM((band + qw, 2 * qw + SCORE_PAD), F32)] * (blk // qw)
                        + [pltpu.VMEM((band + qw, 2 * qw + SCORE_PAD), BF16)] * (blk // qw)),
        compiler_params=pltpu.CompilerParams(dimension_semantics=("parallel", "arbitrary")),
        name="band_attention",
    )(proj, proj, proj, proj, proj, *([bias] * bias.shape[0]), _bias_stats(bias[0]))


def _retention_kernel(qk_ref, v_ref, gate_ref, cos_ref, sin_ref, qdec_ref, kdec_ref, dmat_ref,
                      sdec_ref, o_ref, state_ref):
    @pl.when(pl.program_id(0) == 0)
    def _():
        state_ref[...] = jnp.zeros(state_ref.shape, F32)

    cos = cos_ref[...]
    sin = sin_ref[...]
    lane = lax.broadcasted_iota(jnp.int32, cos.shape, 1)
    first_half = (lane % DQK_C) < (DQK_C // 2)
    qk = qk_ref[...]
    parts = []
    for j in range(qk.shape[1] // LANES):
        t = qk[:, j * LANES:(j + 1) * LANES]
        partner = jnp.where(first_half, pltpu.roll(t, LANES - DQK_C // 2, 1), pltpu.roll(t, DQK_C // 2, 1))
        parts.append(t * cos + partner * sin)
    wq = N_HEADS_C * DQK_C
    q = jnp.concatenate(parts[:wq // LANES], axis=1)
    k = jnp.concatenate(parts[wq // LANES:], axis=1) * (DQK_C ** -0.5)
    qd = (q * qdec_ref[...]).astype(BF16)
    kd = (k * kdec_ref[...]).astype(BF16)
    qb = q.astype(BF16)
    kb = k.astype(BF16)
    vb = v_ref[...].astype(BF16)
    gate = gate_ref[...]
    outs = []
    for h in range(N_HEADS_C):
        qs = slice(h * DQK_C, (h + 1) * DQK_C)
        vs = slice(h * DV_C, (h + 1) * DV_C)
        scores = _dot_nt(qb[:, qs], kb[:, qs]) * dmat_ref[h]
        state = state_ref[h]
        r = _dot(scores.astype(BF16), vb[:, vs]) + _dot(qd[:, qs], state.astype(BF16))
        state_ref[h] = state * sdec_ref[h] + _dot_tn(kd[:, qs], vb[:, vs])
        r = r * lax.rsqrt(jnp.mean(r * r, axis=-1, keepdims=True) + EPS)
        g = gate[:, vs]
        outs.append(r * (g * jax.nn.sigmoid(g)))
    o_ref[...] = jnp.concatenate(outs, axis=1).astype(o_ref.dtype)


def _retention_tables(seq):
    t = BLK_C
    half = DQK_C // 2
    inv_freq = 1.0 / np.power(ROPE_BASE, np.arange(0, DQK_C, 2, dtype=np.float64) / DQK_C)
    ang = np.arange(seq, dtype=np.float64)[:, None] * inv_freq[None, :]
    reps = LANES // half
    cos = np.tile(np.cos(ang), (1, reps))
    sign = np.where((np.arange(LANES) % DQK_C) < half, -1.0, 1.0)
    sin = np.tile(np.sin(ang), (1, reps)) * sign[None, :]
    log_g = np.log(1.0 - np.power(2.0, -5.0 - np.arange(N_HEADS_C, dtype=np.float64)))
    pos = np.arange(t, dtype=np.float64)
    diff = pos[:, None] - pos[None, :]
    same_or_past = (np.arange(t)[None, :] // CHUNK) <= (np.arange(t)[:, None] // CHUNK)
    dmat = np.where(same_or_past[None], np.exp(log_g[:, None, None] * np.abs(diff)[None]), 0.0)
    qdec = np.repeat(np.exp(log_g[None, :] * (pos[:, None] + 1.0)), DQK_C, axis=1)
    kdec = np.repeat(np.exp(log_g[None, :] * (t - 1.0 - pos[:, None])), DQK_C, axis=1)
    sdec = np.broadcast_to(np.exp(log_g * t)[:, None, None], (N_HEADS_C, 1, DV_C))
    return tuple(jnp.asarray(a.astype(np.float32)) for a in (cos, sin, qdec, kdec, dmat, sdec))


def _retention(proj):
    seq = proj.shape[0]
    t = BLK_C
    cos, sin, qdec, kdec, dmat, sdec = _retention_tables(seq)
    wv = N_HEADS_C * DV_C
    return pl.pallas_call(
        _retention_kernel,
        grid=(seq // t,),
        in_specs=[
            pl.BlockSpec((t, wv), lambda i: (i, 0)),
            pl.BlockSpec((t, wv), lambda i: (i, 1)),
            pl.BlockSpec((t, wv), lambda i: (i, 2)),
            pl.BlockSpec((t, LANES), lambda i: (i, 0)),
            pl.BlockSpec((t, LANES), lambda i: (i, 0)),
            pl.BlockSpec((t, N_HEADS_C * DQK_C), lambda i: (0, 0)),
            pl.BlockSpec((t, N_HEADS_C * DQK_C), lambda i: (0, 0)),
            pl.BlockSpec((N_HEADS_C, t, t), lambda i: (0, 0, 0)),
            pl.BlockSpec((N_HEADS_C, 1, DV_C), lambda i: (0, 0, 0)),
        ],
        out_specs=pl.BlockSpec((t, wv), lambda i: (i, 0)),
        out_shape=jax.ShapeDtypeStruct((seq, wv), BF16),
        scratch_shapes=[pltpu.VMEM((N_HEADS_C, DQK_C, DV_C), F32)],
        compiler_params=pltpu.CompilerParams(dimension_semantics=("arbitrary",)),
        name="retention",
    )(proj, proj, proj, cos, sin, qdec, kdec, dmat, sdec)


def _s5_kernel(*refs):
    ncb = S5_CH // LANES
    u_refs = refs[:ncb]
    (mt_ref, bt_ref, ctr_ref, cti_ref, are_ref, aim_ref, y_ref,
     ut_ref, yt_ref, ys_ref, vr_ref, vi_ref, spr_ref, spi_ref, carry_ref) = refs[ncb:]
    tc = S5_TC
    gp = S5_GROUP
    n = S5_STATE
    ng = S5_GROUPS

    @pl.when(pl.program_id(0) == 0)
    def _():
        carry_ref[...] = jnp.zeros(carry_ref.shape, F32)

    for s in range(S5_T):
        for k in range(ncb):
            ut_ref[s, k * LANES:(k + 1) * LANES, :] = u_refs[k][pl.ds(s, tc, stride=S5_T), :].T

    unroll = 4

    def intra(it, carry):
        for k in range(unroll):
            g = it * unroll + k
            r0 = pl.multiple_of(g * gp, gp)
            ug = ut_ref[:, pl.ds(r0, gp), :].reshape(S5_T * gp, tc).astype(BF16)
            yt_ref[:, pl.ds(r0, gp), :] = _dot(mt_ref[g], ug).reshape(S5_T, gp, tc)
            vt = _dot(bt_ref[g], ug)
            n0 = pl.multiple_of(g * n, n)
            vr_ref[pl.ds(n0, n), :] = vt[0:n]
            vi_ref[pl.ds(n0, n), :] = vt[n:2 * n]
        return carry

    lax.fori_loop(0, ng // unroll, intra, 0)

    sub = SUBLANES
    nv = tc // sub
    row = lax.broadcasted_iota(jnp.int32, (tc, LANES), 0)
    in_vreg = lax.rem(row, sub)

    def rows_of(v, r):
        return jnp.broadcast_to(v[r:r + 1], (tc, LANES))

    for j in range(ng * n // LANES):
        cols = slice(j * LANES, (j + 1) * LANES)
        pwr, pwi = are_ref[:, cols], aim_ref[:, cols]
        xr = vr_ref[cols, :].T
        xi = vi_ref[cols, :].T
        for d in (1, 2, 4):
            keep = in_vreg >= d
            sr = jnp.where(keep, pltpu.roll(xr, d, 0), 0.0)
            si = jnp.where(keep, pltpu.roll(xi, d, 0), 0.0)
            fr, fi = rows_of(pwr, d - 1), rows_of(pwi, d - 1)
            xr, xi = xr + (fr * sr - fi * si), xi + (fr * si + fi * sr)
        cr, ci = carry_ref[0, :, cols], carry_ref[1, :, cols]
        cr0, ci0 = cr, ci
        outr, outi = [], []
        for v in range(nv):
            yr = xr[v * sub:(v + 1) * sub] + (pwr * cr - pwi * ci)
            yi = xi[v * sub:(v + 1) * sub] + (pwr * ci + pwi * cr)
            outr.append(yr)
            outi.append(yi)
            cr = jnp.broadcast_to(yr[sub - 1:sub], (sub, LANES))
            ci = jnp.broadcast_to(yi[sub - 1:sub], (sub, LANES))
        carry_ref[0, :, cols] = cr
        carry_ref[1, :, cols] = ci
        sr = jnp.concatenate(outr, axis=0)
        si = jnp.concatenate(outi, axis=0)
        first = row == 0
        spr_ref[j] = jnp.where(first, rows_of(cr0, 0), pltpu.roll(sr, 1, 0))
        spi_ref[j] = jnp.where(first, rows_of(ci0, 0), pltpu.roll(si, 1, 0))

    def cross(it, carry):
        for k in range(unroll):
            jp = it * unroll + k
            r0 = pl.multiple_of(jp * 2 * gp, 2 * gp)
            yc = (_dot_nt(ctr_ref[jp], spr_ref[jp].astype(BF16))
                  + _dot_nt(cti_ref[jp], spi_ref[jp].astype(BF16)))
            yt_ref[:, pl.ds(r0, 2 * gp), :] += yc.reshape(S5_T, 2 * gp, tc)
        return carry

    lax.fori_loop(0, ng // 2 // unroll, cross, 0)

    for s in range(S5_T):
        for k in range(ncb):
            ys_ref[k, pl.ds(s, tc, stride=S5_T), :] = yt_ref[s, k * LANES:(k + 1) * LANES, :].T
    for k in range(ncb):
        y_ref[:, k * LANES:(k + 1) * LANES] = ys_ref[k]


def _s5_matrices(lam_re, lam_im, log_step, b_re, b_im, c_re, c_im, d_skip):
    hi = lax.Precision.HIGHEST
    t, gp, n, ng = S5_T, S5_GROUP, S5_STATE, S5_GROUPS
    lam = lax.complex(lam_re.astype(F32), lam_im.astype(F32))
    step = jnp.exp(log_step.astype(F32))[:, None]
    ls = lam * step
    a_bar = jnp.exp(ls)
    b_bar = ((a_bar - 1.0) / lam)[..., None] * lax.complex(b_re.astype(F32), b_im.astype(F32))
    cm = lax.complex(c_re.astype(F32), c_im.astype(F32))

    def apow(k):
        kk = k.astype(F32).astype(jnp.complex64)
        return jnp.exp(ls.reshape((ng,) + (1,) * k.ndim + (n,)) * kk[None, ..., None])

    tt = jnp.arange(t)
    kmat = jnp.einsum('gpn,gln,gnq->glpq', cm, apow(tt), b_bar, precision=hi).real
    krev = jnp.transpose(kmat[:, ::-1], (0, 2, 1, 3)).reshape(ng, gp, t * gp)
    kpad = jnp.pad(krev, ((0, 0), (0, 0), (0, t * gp)))
    mt = jnp.concatenate([kpad[:, :, (t - 1 - to) * gp:(2 * t - 1 - to) * gp] for to in range(t)], axis=1)
    dvec = jnp.tile(d_skip.astype(F32).reshape(ng, 1, gp), (1, t, 1)).reshape(ng, t * gp)
    mt = mt + jnp.eye(t * gp, dtype=F32)[None] * dvec[:, :, None]
    z = jnp.swapaxes(apow(t - 1 - tt), 1, 2)[:, :, :, None] * b_bar[:, :, None, :]
    z = z.reshape(ng, n, t * gp)
    bt = jnp.concatenate([z.real, z.imag], axis=1)
    w = cm[:, None, :, :] * apow(tt + 1)[:, :, None, :]

    def pair_readout(x):
        x = x.reshape(ng // 2, 2, t, gp, n)
        first = jnp.pad(x[:, 0], ((0, 0), (0, 0), (0, 0), (0, n)))
        second = jnp.pad(x[:, 1], ((0, 0), (0, 0), (0, 0), (n, 0)))
        return jnp.stack([first, second], axis=2).reshape(ng // 2, t * 2 * gp, 2 * n).astype(BF16)

    ctr, cti = pair_readout(w.real), pair_readout(-w.imag)
    a_chunk = jnp.transpose(apow(t * (jnp.arange(SUBLANES) + 1)), (1, 0, 2)).reshape(SUBLANES, ng * n)
    return mt.astype(BF16), bt.astype(BF16), ctr, cti, a_chunk.real, a_chunk.imag


def _s5(proj, mats):
    seq, width = proj.shape
    t, tc, gp, n, ng = S5_T, S5_TC, S5_GROUP, S5_STATE, S5_GROUPS
    rows = t * tc
    ncb = S5_CH // LANES
    cb0 = (width - S5_CH) // LANES
    u_specs = [pl.BlockSpec((rows, LANES), (lambda i, k=k: (i, cb0 + k))) for k in range(ncb)]
    nsb = ng * n // LANES
    return pl.pallas_call(
        _s5_kernel,
        grid=(seq // rows,),
        in_specs=u_specs + [_const_spec(m.shape) for m in mats],
        out_specs=pl.BlockSpec((rows, S5_CH), lambda i: (i, 0)),
        out_shape=jax.ShapeDtypeStruct((seq, S5_CH), F32),
        scratch_shapes=[
            pltpu.VMEM((t, S5_CH, tc), F32),
            pltpu.VMEM((t, S5_CH, tc), F32),
            pltpu.VMEM((ncb, rows, LANES), F32),
            pltpu.VMEM((ng * n, tc), F32),
            pltpu.VMEM((ng * n, tc), F32),
            pltpu.VMEM((nsb, tc, LANES), F32),
            pltpu.VMEM((nsb, tc, LANES), F32),
            pltpu.VMEM((2, SUBLANES, ng * n), F32),
        ],
        compiler_params=pltpu.CompilerParams(dimension_semantics=("arbitrary",)),
        name="s5_scan",
    )(*([proj] * ncb), *mats)


def _mix_ffn_kernel(*refs, glu, final):
    (x_ref, a_ref, b_ref, wo_ref, g1_ref), refs = refs[:5], refs[5:]
    if glu:
        gw_ref, refs = refs[0], refs[1:]
    (g_ref, sc_ref, sh_ref, gate_ref, win_ref, cw_ref, cb_ref, wout_ref), refs = refs[:8], refs[8:]
    if final:
        fg_ref, o_ref, h_ref, act_ref, gbuf_ref, carry_ref = refs
    else:
        ng_ref, nsc_ref, nsh_ref, o_ref, hn_ref, h_ref, act_ref, gbuf_ref, carry_ref = refs
    tm = x_ref.shape[0]
    halo = gbuf_ref.shape[0] - tm

    @pl.when(pl.program_id(0) == 0)
    def _():
        carry_ref[...] = jnp.zeros(carry_ref.shape, F32)

    if glu:
        y = jax.nn.gelu(b_ref[...]).astype(BF16)
        gg = _dot(y, gw_ref[...])
        half = gg.shape[1] // 2
        b = (gg[:, :half] * jax.nn.sigmoid(gg[:, half:])).astype(BF16)
    else:
        b = b_ref[...]
    cat = jnp.concatenate([a_ref[...], b], axis=1)
    x = x_ref[...] + g1_ref[...] * _dot(cat, wo_ref[...])
    h_ref[...] = _mod_rmsnorm(x, g_ref[...], sc_ref[...], sh_ref[...]).astype(BF16)
    for f in range(D_FF // TF_FFN):
        cs = slice(f * TF_FFN, (f + 1) * TF_FFN)
        gs = slice(D_FF + f * TF_FFN, D_FF + (f + 1) * TF_FFN)
        h = h_ref[...]
        val = _dot(h, win_ref[:, cs])
        gate = _dot(h, win_ref[:, gs])
        gbuf_ref[0:halo, :] = carry_ref[:, cs]
        gbuf_ref[halo:halo + tm, :] = gate
        carry_ref[:, cs] = gate[tm - halo:tm, :]
        conv = (gate * cw_ref[2:3, cs] + gbuf_ref[halo - 1:halo - 1 + tm, :] * cw_ref[1:2, cs]
                + gbuf_ref[halo - 2:halo - 2 + tm, :] * cw_ref[0:1, cs] + cb_ref[:, cs])
        act_ref[:, cs] = (jax.nn.gelu(conv) * val).astype(BF16)
    xn = x + gate_ref[...] * _dot(act_ref[...], wout_ref[...])
    if final:
        xn = xn * lax.rsqrt(jnp.mean(xn * xn, axis=-1, keepdims=True) + EPS) * fg_ref[...]
    else:
        hn_ref[...] = _mod_rmsnorm(xn, ng_ref[...], nsc_ref[...], nsh_ref[...]).astype(BF16)
    o_ref[...] = xn


def _layer_spec(shape, layer):
    idx = (layer,) + (0,) * (len(shape) - 1)
    return pl.BlockSpec((None,) + tuple(shape[1:]), lambda *_: idx, pipeline_mode=pl.Buffered(1))


def _mix_ffn(x, a, b, wo, gate1, glu_w, g, scale, shift, gate2, w_in, conv_w, conv_b, w_out, tail, layer):
    seq, d = x.shape
    final = len(tail) == 1
    tm = TM_FFN
    halo = SUBLANES
    row = pl.BlockSpec((1, d), lambda i: (0, 0))
    rows = lambda w: pl.BlockSpec((tm, w), lambda i: (i, 0))
    conv_b = conv_b.reshape(conv_b.shape[0], 1, D_FF)
    in_specs = [rows(d), rows(a.shape[1]), rows(b.shape[1]), _const_spec(wo.shape), row]
    args = [x, a, b, wo, gate1]
    if glu_w is not None:
        in_specs.append(_const_spec(glu_w.shape))
        args.append(glu_w)
    in_specs += [
        row, row, row, row,
        _layer_spec(w_in.shape, layer),
        _layer_spec(conv_w.shape, layer),
        _layer_spec(conv_b.shape, layer),
        _layer_spec(w_out.shape, layer),
    ] + [row] * len(tail)
    args += [g.reshape(1, d), scale, shift, gate2, w_in, conv_w, conv_b, w_out]
    args += [t.reshape(1, d) for t in tail]
    out_specs = [rows(d)] if final else [rows(d), rows(d)]
    out_shape = [jax.ShapeDtypeStruct((seq, d), F32)] + ([] if final else [jax.ShapeDtypeStruct((seq, d), BF16)])
    return pl.pallas_call(
        functools.partial(_mix_ffn_kernel, glu=glu_w is not None, final=final),
        grid=(seq // tm,),
        in_specs=in_specs,
        out_specs=out_specs,
        out_shape=out_shape,
        scratch_shapes=[
            pltpu.VMEM((tm, d), BF16),
            pltpu.VMEM((tm, D_FF), BF16),
            pltpu.VMEM((tm + halo, TF_FFN), F32),
            pltpu.VMEM((halo, D_FF), F32),
        ],
        compiler_params=pltpu.CompilerParams(dimension_semantics=("arbitrary",)),
        name="mix_ffn",
    )(*args)


def kernel(x, c, t5_table, mod_w, mod_b, norm1_g, norm2_g, ffn_w_in, ffn_conv_w, ffn_conv_b, ffn_w_out,
           ev_w_in, ev_w_out, diff_lambda, diff_subln_g, band_rel_bias,
           od_w_in, od_w_out, s5_lam_re, s5_lam_im, s5_log_step, s5_b_re, s5_b_im, s5_c_re, s5_c_im,
           s5_d, s5_glu_w, final_g):
    assert x.shape[0] == 1 and x.shape[2] == D_MODEL
    seq = x.shape[1]
    assert seq % TM_PROJ == 0 and seq % (S5_T * S5_TC) == 0
    d = D_MODEL
    xs = x[0]
    mod = _modulation(c, mod_w, mod_b)
    ffn_w_in_b = ffn_w_in.astype(BF16)
    ffn_w_out_b = ffn_w_out.astype(BF16)
    mods = [[mod[i, :, k * d:(k + 1) * d] for k in range(6)] for i in range(DEPTH)]
    h = None
    for i in range(DEPTH):
        sh1, sc1, g1, sh2, sc2, g2 = mods[i]
        w_in = (ev_w_in if i % 2 == 0 else od_w_in)[i // 2].astype(BF16)
        proj_dtype = BF16 if i % 2 == 0 else F32
        if h is None:
            proj = _normproj(xs, norm1_g[i], sc1, sh1, w_in, proj_dtype)
        else:
            proj = _proj(h, w_in, proj_dtype)
        if i % 2 == 0:
            e = i // 2
            lam_init = 0.8 - 0.6 * math.exp(-0.3 * i)
            lp = diff_lambda[e].astype(F32)
            lam = jnp.exp(jnp.sum(lp[0] * lp[1])) - jnp.exp(jnp.sum(lp[2] * lp[3])) + lam_init
            mix_a = _diff_attention(proj, t5_table, lam, diff_subln_g[e], lam_init)
            mix_b = _band_attention(proj, band_rel_bias[e])
            wo, glu_w = ev_w_out[e].astype(BF16), None
        else:
            o = i // 2
            mix_a = _retention(proj)
            mats = _s5_matrices(s5_lam_re[o], s5_lam_im[o], s5_log_step[o], s5_b_re[o], s5_b_im[o],
                                s5_c_re[o], s5_c_im[o], s5_d[o])
            mix_b = _s5(proj, mats)
            wo, glu_w = od_w_out[o].astype(BF16), s5_glu_w[o].astype(BF16)
        if i == DEPTH - 1:
            tail = (final_g,)
        else:
            nsh1, nsc1 = mods[i + 1][0], mods[i + 1][1]
            tail = (norm1_g[i + 1], nsc1, nsh1)
        out = _mix_ffn(xs, mix_a, mix_b, wo, g1, glu_w, norm2_g[i], sc2, sh2, g2,
                       ffn_w_in_b, ffn_conv_w, ffn_conv_b, ffn_w_out_b, tail, layer=i)
        if i == DEPTH - 1:
            xs = out[0]
        else:
            xs, h = out
    return xs[None]
```

```python
import functools
import math

import jax
import jax.numpy as jnp
import numpy as np
from jax import lax
from jax.experimental import pallas as pl
from jax.experimental.pallas import tpu as pltpu

F32 = jnp.float32
BF16 = jnp.bfloat16

D_MODEL = 1024
DEPTH = 2
CHUNK = 64
GROUP_WIDTH = D_MODEL // 2
DK_A = 64
DV_A = 2 * DK_A
N_HEADS_A = GROUP_WIDTH // DV_A
DH_B = 64
N_HEADS_B = GROUP_WIDTH // DH_B
LEFT_CHUNKS = 8
REL_CLIP = 2 * CHUNK
NUM_BUCKETS = 32
MAX_DISTANCE = 128
DV_C = 128
DQK_C = DV_C // 2
N_HEADS_C = GROUP_WIDTH // DV_C
ROPE_BASE = 10000.0
S5_CH = GROUP_WIDTH
S5_GROUP = 16
S5_GROUPS = S5_CH // S5_GROUP
S5_STATE = 64
D_FF = ((8 * D_MODEL // 3 + 255) // 256) * 256
CONV_W = 3
EVEN_IN = 3 * N_HEADS_A * DV_A + 3 * N_HEADS_B * DH_B
ODD_IN = 2 * N_HEADS_C * DQK_C + 2 * N_HEADS_C * DV_C + S5_CH
EPS = 1e-6
NEG_INF = -1e30
LOG2E = math.log2(math.e)

LANES = 128
SUBLANES = 8
MXU_DIM = 256

TM_PROJ = 1024
TN_PROJ = 1024
TN_MOD = 1536
TM_FFN = 512
TF_FFN = MXU_DIM
BLK_A = 512
NPART_A = 2
ONES_A = 16
SINGLE_PASS_LOG2_RANGE = 96.0
SCORE_PAD = LANES
BLK_B = 1024
BAND_B = LEFT_CHUNKS * CHUNK
QW_B = 4 * CHUNK
BLK_C = 512
S5_T = 16
S5_TC = LANES

assert BLK_B % BAND_B == 0 and BLK_B % QW_B == 0 and BAND_B % QW_B == 0
assert BLK_A >= MAX_DISTANCE, "far key blocks must sit in the saturated T5 bucket"
assert DV_A == LANES and 2 * DK_A == LANES and 2 * DH_B == LANES, "attention heads are read as 128-lane column blocks"


def _dot(a, b):
    return jnp.dot(a, b, preferred_element_type=F32)


def _dot_nt(a, b):
    return lax.dot_general(a, b, (((1,), (1,)), ((), ())), preferred_element_type=F32)


def _dot_tn(a, b):
    return lax.dot_general(a, b, (((0,), (0,)), ((), ())), preferred_element_type=F32)


def _const_spec(shape):
    zeros = (0,) * len(shape)
    return pl.BlockSpec(shape, lambda *_: zeros, pipeline_mode=pl.Buffered(1))


def _mod_rmsnorm(x, g, scale, shift):
    y = x * lax.rsqrt(jnp.mean(x * x, axis=-1, keepdims=True) + EPS)
    y = y * g
    return y * (1.0 + scale) + shift


def _mod_kernel(c_ref, w_ref, b_ref, o_ref):
    c = c_ref[...]
    cond = c * jax.nn.sigmoid(c)
    o_ref[0] = jnp.sum(cond * w_ref[0], axis=0, keepdims=True) + b_ref[0]


def _modulation(c, mod_w, mod_b):
    depth, d, n = mod_w.shape
    tn = TN_MOD
    return pl.pallas_call(
        _mod_kernel,
        grid=(depth, n // tn),
        in_specs=[
            pl.BlockSpec((d, 1), lambda i, j: (0, 0)),
            pl.BlockSpec((1, d, tn), lambda i, j: (i, 0, j)),
            pl.BlockSpec((1, 1, tn), lambda i, j: (i, 0, j)),
        ],
        out_specs=pl.BlockSpec((1, 1, tn), lambda i, j: (i, 0, j)),
        out_shape=jax.ShapeDtypeStruct((depth, 1, n), F32),
        name="modulation",
    )(c.reshape(d, 1), mod_w, mod_b.reshape(depth, 1, n))


def _normproj_kernel(x_ref, g_ref, sc_ref, sh_ref, w_ref, o_ref):
    tm, n = o_ref.shape
    half = tm // 2
    for r in range(2):
        rows = slice(r * half, (r + 1) * half)
        h = _mod_rmsnorm(x_ref[rows, :], g_ref[...], sc_ref[...], sh_ref[...]).astype(BF16)
        for j in range(n // TN_PROJ):
            cols = slice(j * TN_PROJ, (j + 1) * TN_PROJ)
            o_ref[rows, cols] = _dot(h, w_ref[:, cols]).astype(o_ref.dtype)


def _normproj(x, g, scale, shift, w, out_dtype):
    seq, d = x.shape
    n = w.shape[1]
    tm = TM_PROJ
    row = pl.BlockSpec((1, d), lambda i: (0, 0))
    return pl.pallas_call(
        _normproj_kernel,
        grid=(seq // tm,),
        in_specs=[pl.BlockSpec((tm, d), lambda i: (i, 0)), row, row, row, _const_spec(w.shape)],
        out_specs=pl.BlockSpec((tm, n), lambda i: (i, 0)),
        out_shape=jax.ShapeDtypeStruct((seq, n), out_dtype),
        compiler_params=pltpu.CompilerParams(dimension_semantics=("parallel",)),
        name="normproj",
    )(x, g.reshape(1, d), scale, shift, w)


def _proj_kernel(h_ref, w_ref, o_ref):
    for j in range(o_ref.shape[1] // TN_PROJ):
        cols = slice(j * TN_PROJ, (j + 1) * TN_PROJ)
        o_ref[:, cols] = _dot(h_ref[...], w_ref[:, cols]).astype(o_ref.dtype)


def _proj(h, w, out_dtype):
    seq, d = h.shape
    n = w.shape[1]
    tm = TM_PROJ
    return pl.pallas_call(
        _proj_kernel,
        grid=(seq // tm,),
        in_specs=[pl.BlockSpec((tm, d), lambda i: (i, 0)), _const_spec(w.shape)],
        out_specs=pl.BlockSpec((tm, n), lambda i: (i, 0)),
        out_shape=jax.ShapeDtypeStruct((seq, n), out_dtype),
        compiler_params=pltpu.CompilerParams(dimension_semantics=("parallel",)),
        name="proj",
    )(h, w)


def _diffattn_kernel(q_ref, qall_ref, k_ref, v_ref, bias_ref, bstat_ref, lam_ref, g_ref, o_ref,
                     flag_ref, qs_ref, vt_ref, kmax_ref, r_ref, m_ref, acc_ref, *s_refs, out_scale):
    blk = BLK_A
    nq = 2 * blk
    sub = SUBLANES
    dv = DV_A
    npart = NPART_A
    sa_ref, sb_ref = s_refs[:2 * npart], s_refs[2 * npart:4 * npart]
    pa_ref, pb_ref = s_refs[4 * npart:5 * npart], s_refs[5 * npart:6 * npart]
    i = pl.program_id(1)
    lane = lax.broadcasted_iota(jnp.int32, (blk, LANES), 1)
    same_subhead = (lax.broadcasted_iota(jnp.int32, (LANES, LANES), 0) // DK_A
                    == lax.broadcasted_iota(jnp.int32, (LANES, LANES), 1) // DK_A).astype(BF16)

    bias_max, bias_span = bstat_ref[0, 0:1, 0:1], bstat_ref[0, 1:2, 0:1]
    q_scale = DK_A ** -0.5 * LOG2E

    @pl.when(i == 0)
    def _():
        kmax_ref[...] = jnp.zeros(kmax_ref.shape, F32)

        def tr(b, qmax):
            r0 = pl.multiple_of(b * blk, blk)
            vt_ref[0:dv, pl.ds(r0, blk)] = v_ref[pl.ds(r0, blk), :].astype(F32).T.astype(BF16)
            vt_ref[dv:dv + ONES_A, pl.ds(r0, blk)] = jnp.ones((ONES_A, blk), BF16)
            kf = k_ref[pl.ds(r0, blk), :].astype(F32)
            kn2 = _dot((kf * kf).astype(BF16), same_subhead)
            kmax_ref[...] = jnp.maximum(kmax_ref[...], jnp.max(kn2.reshape(blk // sub, sub, LANES), axis=0))
            qa = (qall_ref[pl.ds(r0, blk), :].astype(F32) * q_scale).astype(BF16).astype(F32)
            qn2 = _dot((qa * qa).astype(BF16), same_subhead)
            return jnp.maximum(qmax, jnp.max(qn2.reshape(blk // sub, sub, LANES), axis=0))
        qmax2 = lax.fori_loop(0, v_ref.shape[0] // blk, tr, jnp.zeros((sub, LANES), F32))
        kmax2 = jnp.max(kmax_ref[...], axis=0, keepdims=True)
        kmax_ref[...] = jnp.broadcast_to(kmax2, kmax_ref.shape)
        worst = 2.0 * jnp.sqrt(jnp.max(qmax2, axis=0, keepdims=True) * kmax2) * 1.03 * 1.03 + bias_span
        flag_ref[0] = (jnp.max(worst) < SINGLE_PASS_LOG2_RANGE).astype(jnp.int32)

    q = (q_ref[...].astype(F32) * q_scale).astype(BF16)
    qf = q.astype(F32)
    qs_ref[:, 0:blk] = jnp.where(lane < DK_A, qf, 0.0).T.astype(BF16)
    qs_ref[:, blk:nq] = jnp.where(lane >= DK_A, qf, 0.0).T.astype(BF16)
    acc_ref[...] = jnp.zeros(acc_ref.shape, F32)

    qt = qs_ref[...].astype(F32)
    qn2 = jnp.sum((qt * qt).reshape(LANES // sub, sub, nq), axis=0)
    qn2 = jnp.broadcast_to(jnp.sum(qn2, axis=0, keepdims=True), (sub, nq))
    kmax2 = jnp.concatenate([jnp.broadcast_to(kmax_ref[:, m * DK_A:m * DK_A + 1], (sub, blk)) for m in range(2)],
                            axis=1)
    bound = jnp.sqrt(qn2 * kmax2) * 1.03
    r_ref[...] = bound + bias_max
    single_pass = flag_ref[0] == 1

    @pl.when(single_pass)
    def _():
        _diffattn_fixed_shift(i, k_ref, bias_ref, qs_ref, vt_ref, r_ref, m_ref, acc_ref, pa_ref, pb_ref)

    @pl.when(jnp.logical_not(single_pass))
    def _():
        _diffattn_online(i, k_ref, bias_ref, qs_ref, vt_ref, m_ref, acc_ref, sa_ref, sb_ref)

    ot = acc_ref[0:dv, 0:nq] / acc_ref[dv:dv + 1, 0:nq]
    o = ot[:, 0:blk].T - lam_ref[...] * ot[:, blk:nq].T
    o = o * lax.rsqrt(jnp.mean(o * o, axis=-1, keepdims=True) + EPS) * g_ref[...]
    o_ref[...] = (o * out_scale).astype(o_ref.dtype)


def _diffattn_fixed_shift(i, k_ref, bias_ref, qs_ref, vt_ref, shift_ref, l_ref, acc_ref, pa_ref, pb_ref):
    blk = BLK_A
    nq = 2 * blk
    sub = SUBLANES
    npart = len(pa_ref)
    wq = nq // npart
    l_ref[...] = jnp.zeros(l_ref.shape, F32)

    def probs(b, p_ref, bias):
        k = k_ref[pl.ds(pl.multiple_of(b * blk, blk), blk), :]
        for part in range(npart):
            cols = slice(part * wq, (part + 1) * wq)
            s = _dot(k, qs_ref[:, cols])
            if bias is not None:
                q0 = (part * wq) % blk
                s = s + bias[:, q0:q0 + wq]
            p = jnp.exp2(s.reshape(blk // sub, sub, wq) - shift_ref[:, cols][None])
            l_ref[:, cols] += jnp.sum(p, axis=0)
            p_ref[part][:, 0:wq] = p.reshape(blk, wq).astype(BF16)

    def accumulate(b, p_ref):
        vt = vt_ref[0:DV_A, pl.ds(pl.multiple_of(b * blk, blk), blk)]
        for part in range(npart):
            cols = slice(part * wq, (part + 1) * wq)
            acc_ref[0:DV_A, cols] += _dot(vt, p_ref[part][:, 0:wq])

    @pl.when(i == 0)
    def _():
        probs(0, pa_ref, bias_ref[0, 1])
        accumulate(0, pa_ref)

    @pl.when(i > 0)
    def _():
        nfar = i - 1
        probs(i, pa_ref, bias_ref[0, 1])
        probs(i - 1, pb_ref, bias_ref[0, 0])
        accumulate(i, pa_ref)

        def pair(t):
            probs(2 * t, pa_ref, None)
            accumulate(jnp.where(t == 0, i - 1, 2 * t - 1), pb_ref)
            probs(2 * t + 1, pb_ref, None)
            accumulate(2 * t, pa_ref)

        def four_pairs(u, carry):
            for v in range(4):
                pair(4 * u + v)
            return carry

        npairs = nfar // 2
        lax.fori_loop(0, npairs // 4, four_pairs, 0)

        def one_pair(t, carry):
            pair(t)
            return carry
        lax.fori_loop(4 * (npairs // 4), npairs, one_pair, 0)
        in_pb = jnp.where(npairs == 0, i - 1, 2 * npairs - 1)

        @pl.when(lax.rem(nfar, 2) == 1)
        def _():
            probs(nfar - 1, pa_ref, None)
            accumulate(in_pb, pb_ref)
            accumulate(nfar - 1, pa_ref)

        @pl.when(lax.rem(nfar, 2) == 0)
        def _():
            accumulate(in_pb, pb_ref)

    acc_ref[DV_A:DV_A + sub, 0:nq] = jnp.broadcast_to(jnp.sum(l_ref[...], axis=0, keepdims=True), (sub, nq))


def _diffattn_online(i, k_ref, bias_ref, qs_ref, vt_ref, m_ref, acc_ref, sa_ref, sb_ref):
    blk = BLK_A
    nq = 2 * blk
    sub = SUBLANES
    npart = len(sa_ref) // 2
    wq = nq // npart
    m_ref[...] = jnp.full(m_ref.shape, NEG_INF, F32)

    def scores(b, s_ref):
        k = k_ref[pl.ds(pl.multiple_of(b * blk, blk), blk), :]
        for part in range(npart):
            s = _dot(k, qs_ref[:, part * wq:(part + 1) * wq])
            s_ref[part][:, 0:wq] = s
            s_ref[npart + part][...] = jnp.max(s.reshape(blk // sub, sub, wq), axis=0)

    def softmax_pv(b, s_ref, bias):
        vt = vt_ref[:, pl.ds(pl.multiple_of(b * blk, blk), blk)]
        for part in range(npart):
            cols = slice(part * wq, (part + 1) * wq)
            s = s_ref[part][:, 0:wq]
            if bias is not None:
                q0 = (part * wq) % blk
                s = s + bias[:, q0:q0 + wq]
            s = s.reshape(blk // sub, sub, wq)
            m_prev = m_ref[:, cols]
            smax = jnp.max(s, axis=0) if bias is not None else s_ref[npart + part][...]
            m_cur = jnp.max(smax, axis=0, keepdims=True)
            m_new = jnp.maximum(m_prev, m_cur)
            alpha = jnp.exp2(m_prev - m_new)
            p = jnp.exp2(s - m_new[None])
            pv = _dot(vt, p.reshape(blk, wq).astype(BF16))
            acc_ref[:, cols] = acc_ref[:, cols] * alpha[0:1] + pv
            m_ref[:, cols] = m_new

    nfar = jnp.maximum(i - 1, 0)
    odd = lax.rem(nfar, 2)

    @pl.when(i == 0)
    def _():
        scores(0, sb_ref)

    @pl.when(i > 0)
    def _():
        @pl.when(odd == 1)
        def _():
            scores(0, sb_ref)
            scores(1, sa_ref)
            softmax_pv(0, sb_ref, None)

        @pl.when(odd == 0)
        def _():
            scores(0, sa_ref)

        def pair(b):
            scores(b + 1, sb_ref)
            softmax_pv(b, sa_ref, None)
            scores(b + 2, sa_ref)
            softmax_pv(b + 1, sb_ref, None)

        def quad_body(t, carry):
            pair(odd + 4 * t)
            pair(odd + 4 * t + 2)
            return carry

        npairs = nfar // 2
        lax.fori_loop(0, npairs // 2, quad_body, 0)

        @pl.when(lax.rem(npairs, 2) == 1)
        def _():
            pair(odd + 2 * (npairs - 1))
        scores(i, sb_ref)
        softmax_pv(i - 1, sa_ref, bias_ref[0, 0])

    softmax_pv(i, sb_ref, bias_ref[0, 1])


_TOEPLITZ_ROWS = 512


def _toeplitz_kernel(v_ref, o_ref, *, keep):
    rows, cols = o_ref.shape[2:]
    x = jnp.broadcast_to(v_ref[0, 0], (rows, v_ref.shape[-1]))
    tile = pltpu.roll(x, 0, 1, stride=1, stride_axis=0)[:, :cols]
    r = lax.broadcasted_iota(jnp.int32, (rows, cols), 0) + pl.program_id(1) * rows
    c = lax.broadcasted_iota(jnp.int32, (rows, cols), 1)
    for variant in range(o_ref.shape[0]):
        o_ref[variant, 0] = jnp.where(keep(r, c, variant), tile, NEG_INF)


def _toeplitz_tiles(fn, keep, heads, rows, cols, variants=1):
    n = rows + cols
    rb = rows if rows % _TOEPLITZ_ROWS else _TOEPLITZ_ROWS
    assert rows % rb == 0 and n % LANES == 0
    idx = jnp.arange(n, dtype=jnp.int32)
    vec = fn(jnp.where(idx < cols, idx, idx - n)).astype(F32)
    vecs = jnp.stack([jnp.roll(vec, k * rb, axis=1) for k in range(rows // rb)], axis=1)
    return pl.pallas_call(
        functools.partial(_toeplitz_kernel, keep=keep),
        grid=(heads, rows // rb),
        in_specs=[pl.BlockSpec((1, 1, 1, n), lambda h, k: (h, k, 0, 0))],
        out_specs=pl.BlockSpec((variants, 1, rb, cols), lambda h, k: (0, h, k, 0)),
        out_shape=jax.ShapeDtypeStruct((variants, heads, rows, cols), F32),
        name="toeplitz_tiles",
    )(vecs.reshape(heads, rows // rb, 1, n))


def _bias_stats(tiles):
    finite = tiles > 0.5 * NEG_INF
    bias_max = jnp.maximum(jnp.max(jnp.where(finite, tiles, NEG_INF), axis=(1, 2)), 0.0)
    bias_min = jnp.minimum(jnp.min(jnp.where(finite, tiles, -NEG_INF), axis=(1, 2)), 0.0)
    return jnp.broadcast_to(jnp.stack([bias_max, bias_max - bias_min], axis=1)[:, :, None],
                            (tiles.shape[0], 2, LANES))


def _t5_bucket(rel):
    nb = NUM_BUCKETS // 2
    max_exact = nb // 2
    bucket = jnp.where(rel > 0, nb, 0)
    n = jnp.abs(rel)
    nf = jnp.maximum(n, 1).astype(F32)
    large = max_exact + (jnp.log(nf / max_exact) / math.log(MAX_DISTANCE / max_exact)
                         * (nb - max_exact)).astype(jnp.int32)
    large = jnp.minimum(large, nb - 1)
    return bucket + jnp.where(n < max_exact, n, large)


def _diff_bias_tiles(t5_table):
    blk = BLK_A
    table = t5_table.astype(F32)
    far = table[_t5_bucket(jnp.full((), -(blk + 1), jnp.int32))]
    def visible(r, c, variant):
        return jnp.floor_divide(r - blk, CHUNK) <= jnp.floor_divide(c, CHUNK)

    tiles = _toeplitz_tiles(lambda x: ((table[_t5_bucket(-x - blk)] - far) * LOG2E).T, visible,
                            N_HEADS_A, 2 * blk, blk)
    return tiles.reshape(N_HEADS_A, 2, blk, blk)


def _diff_attention(proj, t5_table, lam, subln_g, lam_init):
    seq = proj.shape[0]
    blk = BLK_A
    bias = _diff_bias_tiles(t5_table)
    ha = N_HEADS_A
    bstat = _bias_stats(bias.reshape(ha, 2 * blk, blk))
    kern = functools.partial(_diffattn_kernel, out_scale=1.0 - lam_init)
    return pl.pallas_call(
        kern,
        grid=(ha, seq // blk),
        in_specs=[
            pl.BlockSpec((blk, DV_A), lambda h, i: (i, h)),
            pl.BlockSpec((seq, DV_A), lambda h, i: (0, h)),
            pl.BlockSpec((seq, DV_A), lambda h, i: (0, ha + h)),
            pl.BlockSpec((seq, DV_A), lambda h, i: (0, 2 * ha + h)),
            pl.BlockSpec((1, 2, blk, blk), lambda h, i: (h, 0, 0, 0)),
            pl.BlockSpec((1, 2, LANES), lambda h, i: (h, 0, 0)),
            pl.BlockSpec((1, DV_A), lambda h, i: (0, 0)),
            pl.BlockSpec((1, DV_A), lambda h, i: (0, 0)),
        ],
        out_specs=pl.BlockSpec((blk, DV_A), lambda h, i: (i, h)),
        out_shape=jax.ShapeDtypeStruct((seq, ha * DV_A), BF16),
        scratch_shapes=[
            pltpu.SMEM((1,), jnp.int32),
            pltpu.VMEM((DV_A, 2 * blk), BF16),
            pltpu.VMEM((DV_A + ONES_A, seq), BF16),
            pltpu.VMEM((SUBLANES, LANES), F32),
            pltpu.VMEM((SUBLANES, 2 * blk), F32),
            pltpu.VMEM((SUBLANES, 2 * blk), F32),
            pltpu.VMEM((DV_A + ONES_A, 2 * blk), F32),
        ] + 2 * ([pltpu.VMEM((blk, 2 * blk // NPART_A + SCORE_PAD), F32)] * NPART_A
                 + [pltpu.VMEM((SUBLANES, 2 * blk // NPART_A), F32)] * NPART_A)
        + 2 * [pltpu.VMEM((blk, 2 * blk // NPART_A + SCORE_PAD), BF16)] * NPART_A,
        compiler_params=pltpu.CompilerParams(dimension_semantics=("parallel", "arbitrary")),
        name="diff_attention",
    )(proj, proj, proj, proj, bias, bstat, jnp.full((1, DV_A), lam, F32), subln_g.reshape(1, DV_A).astype(F32))


def _band_kernel(q_ref, kp_ref, kc_ref, vp_ref, vc_ref, *refs):
    qw, band = QW_B, BAND_B
    nbias = band // qw + 1
    bias_refs, bstat_ref, o_ref = refs[:nbias], refs[nbias], refs[nbias + 1]
    ngroups = (len(refs) - nbias - 2) // 2
    s_refs, p_refs = refs[nbias + 2:nbias + 2 + ngroups], refs[nbias + 2 + ngroups:]
    nk = band + qw
    sub = SUBLANES
    q = (q_ref[...].astype(F32) * (DH_B ** -0.5 * LOG2E)).astype(BF16).astype(F32)
    lane = lax.broadcasted_iota(jnp.int32, q.shape, 1)
    qt = (jnp.where(lane < DH_B, q, 0.0).T, jnp.where(lane >= DH_B, q, 0.0).T)
    qh = (qt[0].astype(BF16), qt[1].astype(BF16))
    k_all = jnp.concatenate([kp_ref[...], kc_ref[...]], axis=0)
    vt_all = jnp.concatenate([vp_ref[...], vc_ref[...]], axis=0).astype(F32).T.astype(BF16)

    def group_operands(g):
        k0 = g * qw
        qs = jnp.concatenate([qh[0][:, k0:k0 + qw], qh[1][:, k0:k0 + qw]], axis=1)
        bias_ref = bias_refs[min(g, nbias - 1)]
        bias = jnp.concatenate([bias_ref[0, 0], bias_ref[0, 1]], axis=1)
        return k0, qs, bias

    def store_group(g, ot):
        o = jnp.concatenate([ot[0:DH_B, 0:qw], ot[DH_B:2 * DH_B, qw:2 * qw]], axis=0)
        o_ref[g * qw:(g + 1) * qw, :] = o.T.astype(o_ref.dtype)

    same_head = (lax.broadcasted_iota(jnp.int32, (LANES, LANES), 0) // DH_B
                 == lax.broadcasted_iota(jnp.int32, (LANES, LANES), 1) // DH_B).astype(BF16)
    kf = k_all.astype(F32)
    kn2 = _dot((kf * kf).astype(BF16), same_head)
    kmax2 = jnp.max(jnp.max(kn2.reshape(kn2.shape[0] // sub, sub, LANES), axis=0), axis=0, keepdims=True)
    shifts, worst = [], None
    for m in range(2):
        qn2 = jnp.sum(qt[m] * qt[m], axis=0, keepdims=True)
        bound = jnp.sqrt(qn2 * kmax2[:, m * DH_B:m * DH_B + 1]) * 1.03
        shifts.append(bound + bstat_ref[m, 0:1, 0:1])
        spread = jnp.max(2.0 * bound + bstat_ref[m, 1:2, 0:1])
        worst = spread if worst is None else jnp.maximum(worst, spread)
    fixed_shift = worst < SINGLE_PASS_LOG2_RANGE

    @pl.when(fixed_shift)
    def _():
        sums = []
        for g in range(ngroups):
            k0, qs, bias = group_operands(g)
            r = jnp.concatenate([shifts[0][:, k0:k0 + qw], shifts[1][:, k0:k0 + qw]], axis=1)
            s = _dot(k_all[k0:k0 + nk], qs) + (bias - r)
            p = jnp.exp2(s).reshape(nk // sub, sub, 2 * qw)
            sums.append(jnp.sum(jnp.sum(p, axis=0), axis=0, keepdims=True))
            p_refs[g][:, 0:2 * qw] = p.reshape(nk, 2 * qw).astype(BF16)
        for g in range(ngroups):
            k0 = g * qw
            store_group(g, _dot(vt_all[:, k0:k0 + nk], p_refs[g][:, 0:2 * qw]) / sums[g])

    @pl.when(jnp.logical_not(fixed_shift))
    def _():
        vt_ones = jnp.concatenate([vt_all, jnp.ones((ONES_A, vt_all.shape[1]), BF16)], axis=0)
        for g in range(ngroups):
            k0, qs, _ = group_operands(g)
            s_refs[g][:, 0:2 * qw] = _dot(k_all[k0:k0 + nk], qs)
        for g in range(ngroups):
            k0, _, bias = group_operands(g)
            s = (s_refs[g][:, 0:2 * qw] + bias).reshape(nk // sub, sub, 2 * qw)
            m = jnp.max(jnp.max(s, axis=0), axis=0, keepdims=True)
            p = jnp.exp2(s - m[None])
            pv = _dot(vt_ones[:, k0:k0 + nk], p.reshape(nk, 2 * qw).astype(BF16))
            store_group(g, pv[0:2 * DH_B] / pv[2 * DH_B:2 * DH_B + 1])


def _band_bias_tiles(rel_bias):
    band = BAND_B

    def valid(r, c, variant):
        qchunk = jnp.floor_divide(c, CHUNK)
        kchunk = jnp.floor_divide(r - band, CHUNK)
        missing = jnp.where(variant == 0, 0, band - (variant - 1) * QW_B)
        return (kchunk <= qchunk) & (kchunk >= qchunk - LEFT_CHUNKS) & (r >= missing)

    return _toeplitz_tiles(
        lambda x: rel_bias.astype(F32)[:, jnp.clip(-x - band, -REL_CLIP, REL_CLIP) + REL_CLIP] * LOG2E, valid,
        N_HEADS_B, band + QW_B, QW_B, variants=1 + band // QW_B)


def _band_attention(proj, rel_bias):
    seq = proj.shape[0]
    blk, band, qw = BLK_B, BAND_B, QW_B
    bias = _band_bias_tiles(rel_bias)
    npair = N_HEADS_B // 2
    qc0 = 3 * N_HEADS_A
    per = blk // band
    prev = lambda c0: (lambda hp, i: (jnp.maximum(i * per - 1, 0), c0 + hp))
    cur = lambda c0: (lambda hp, i: (i, c0 + hp))
    return pl.pallas_call(
        _band_kernel,
        grid=(npair, seq // blk),
        in_specs=[
            pl.BlockSpec((blk, LANES), cur(qc0)),
            pl.BlockSpec((band, LANES), prev(qc0 + npair)),
            pl.BlockSpec((blk, LANES), cur(qc0 + npair)),
            pl.BlockSpec((band, LANES), prev(qc0 + 2 * npair)),
            pl.BlockSpec((blk, LANES), cur(qc0 + 2 * npair)),
        ] + [
            pl.BlockSpec((1, 2, band + qw, qw), (lambda hp, i, t=t: (jnp.where(i == 0, 1 + t, 0), hp, 0, 0)))
            for t in range(band // qw)
        ] + [
            pl.BlockSpec((1, 2, band + qw, qw), lambda hp, i: (0, hp, 0, 0)),
            pl.BlockSpec((2, 2, LANES), lambda hp, i: (hp, 0, 0)),
        ],
        out_specs=pl.BlockSpec((blk, LANES), lambda hp, i: (i, hp)),
        out_shape=jax.ShapeDtypeStruct((seq, N_HEADS_B * DH_B), BF16),
        scratch_shapes=([pltpu.VMEM((band + qw, 2 * qw + SCORE_PAD), F32)] * (blk // qw)
                        + [pltpu.VMEM((band + qw, 2 * qw + SCORE_PAD), BF16)] * (blk // qw)),
        compiler_params=pltpu.CompilerParams(dimension_semantics=("parallel", "arbitrary")),
        name="band_attention",
    )(proj, proj, proj, proj, proj, *([bias] * bias.shape[0]), _bias_stats(bias[0]))


def _retention_kernel(qk_ref, v_ref, gate_ref, cos_ref, sin_ref, qdec_ref, kdec_ref, dmat_ref,
                      sdec_ref, o_ref, state_ref):
    @pl.when(pl.program_id(0) == 0)
    def _():
        state_ref[...] = jnp.zeros(state_ref.shape, F32)

    cos = cos_ref[...]
    sin = sin_ref[...]
    lane = lax.broadcasted_iota(jnp.int32, cos.shape, 1)
    first_half = (lane % DQK_C) < (DQK_C // 2)
    qk = qk_ref[...]
    parts = []
    for j in range(qk.shape[1] // LANES):
        t = qk[:, j * LANES:(j + 1) * LANES]
        partner = jnp.where(first_half, pltpu.roll(t, LANES - DQK_C // 2, 1), pltpu.roll(t, DQK_C // 2, 1))
        parts.append(t * cos + partner * sin)
    wq = N_HEADS_C * DQK_C
    q = jnp.concatenate(parts[:wq // LANES], axis=1)
    k = jnp.concatenate(parts[wq // LANES:], axis=1) * (DQK_C ** -0.5)
    qd = (q * qdec_ref[...]).astype(BF16)
    kd = (k * kdec_ref[...]).astype(BF16)
    qb = q.astype(BF16)
    kb = k.astype(BF16)
    vb = v_ref[...].astype(BF16)
    gate = gate_ref[...]
    outs = []
    for h in range(N_HEADS_C):
        qs = slice(h * DQK_C, (h + 1) * DQK_C)
        vs = slice(h * DV_C, (h + 1) * DV_C)
        scores = _dot_nt(qb[:, qs], kb[:, qs]) * dmat_ref[h]
        state = state_ref[h]
        r = _dot(scores.astype(BF16), vb[:, vs]) + _dot(qd[:, qs], state.astype(BF16))
        state_ref[h] = state * sdec_ref[h] + _dot_tn(kd[:, qs], vb[:, vs])
        r = r * lax.rsqrt(jnp.mean(r * r, axis=-1, keepdims=True) + EPS)
        g = gate[:, vs]
        outs.append(r * (g * jax.nn.sigmoid(g)))
    o_ref[...] = jnp.concatenate(outs, axis=1).astype(o_ref.dtype)


def _retention_tables(seq):
    t = BLK_C
    half = DQK_C // 2
    inv_freq = 1.0 / np.power(ROPE_BASE, np.arange(0, DQK_C, 2, dtype=np.float64) / DQK_C)
    ang = np.arange(seq, dtype=np.float64)[:, None] * inv_freq[None, :]
    reps = LANES // half
    cos = np.tile(np.cos(ang), (1, reps))
    sign = np.where((np.arange(LANES) % DQK_C) < half, -1.0, 1.0)
    sin = np.tile(np.sin(ang), (1, reps)) * sign[None, :]
    log_g = np.log(1.0 - np.power(2.0, -5.0 - np.arange(N_HEADS_C, dtype=np.float64)))
    pos = np.arange(t, dtype=np.float64)
    diff = pos[:, None] - pos[None, :]
    same_or_past = (np.arange(t)[None, :] // CHUNK) <= (np.arange(t)[:, None] // CHUNK)
    dmat = np.where(same_or_past[None], np.exp(log_g[:, None, None] * np.abs(diff)[None]), 0.0)
    qdec = np.repeat(np.exp(log_g[None, :] * (pos[:, None] + 1.0)), DQK_C, axis=1)
    kdec = np.repeat(np.exp(log_g[None, :] * (t - 1.0 - pos[:, None])), DQK_C, axis=1)
    sdec = np.broadcast_to(np.exp(log_g * t)[:, None, None], (N_HEADS_C, 1, DV_C))
    return tuple(jnp.asarray(a.astype(np.float32)) for a in (cos, sin, qdec, kdec, dmat, sdec))


def _retention(proj):
    seq = proj.shape[0]
    t = BLK_C
    cos, sin, qdec, kdec, dmat, sdec = _retention_tables(seq)
    wv = N_HEADS_C * DV_C
    return pl.pallas_call(
        _retention_kernel,
        grid=(seq // t,),
        in_specs=[
            pl.BlockSpec((t, wv), lambda i: (i, 0)),
            pl.BlockSpec((t, wv), lambda i: (i, 1)),
            pl.BlockSpec((t, wv), lambda i: (i, 2)),
            pl.BlockSpec((t, LANES), lambda i: (i, 0)),
            pl.BlockSpec((t, LANES), lambda i: (i, 0)),
            pl.BlockSpec((t, N_HEADS_C * DQK_C), lambda i: (0, 0)),
            pl.BlockSpec((t, N_HEADS_C * DQK_C), lambda i: (0, 0)),
            pl.BlockSpec((N_HEADS_C, t, t), lambda i: (0, 0, 0)),
            pl.BlockSpec((N_HEADS_C, 1, DV_C), lambda i: (0, 0, 0)),
        ],
        out_specs=pl.BlockSpec((t, wv), lambda i: (i, 0)),
        out_shape=jax.ShapeDtypeStruct((seq, wv), BF16),
        scratch_shapes=[pltpu.VMEM((N_HEADS_C, DQK_C, DV_C), F32)],
        compiler_params=pltpu.CompilerParams(dimension_semantics=("arbitrary",)),
        name="retention",
    )(proj, proj, proj, cos, sin, qdec, kdec, dmat, sdec)


def _s5_kernel(*refs):
    ncb = S5_CH // LANES
    u_refs = refs[:ncb]
    (mt_ref, bt_ref, ctr_ref, cti_ref, are_ref, aim_ref, y_ref,
     ut_ref, yt_ref, ys_ref, vr_ref, vi_ref, spr_ref, spi_ref, carry_ref) = refs[ncb:]
    tc = S5_TC
    gp = S5_GROUP
    n = S5_STATE
    ng = S5_GROUPS

    @pl.when(pl.program_id(0) == 0)
    def _():
        carry_ref[...] = jnp.zeros(carry_ref.shape, F32)

    for s in range(S5_T):
        for k in range(ncb):
            ut_ref[s, k * LANES:(k + 1) * LANES, :] = u_refs[k][pl.ds(s, tc, stride=S5_T), :].T

    unroll = 4

    def intra(it, carry):
        for k in range(unroll):
            g = it * unroll + k
            r0 = pl.multiple_of(g * gp, gp)
            ug = ut_ref[:, pl.ds(r0, gp), :].reshape(S5_T * gp, tc).astype(BF16)
            yt_ref[:, pl.ds(r0, gp), :] = _dot(mt_ref[g], ug).reshape(S5_T, gp, tc)
            vt = _dot(bt_ref[g], ug)
            n0 = pl.multiple_of(g * n, n)
            vr_ref[pl.ds(n0, n), :] = vt[0:n]
            vi_ref[pl.ds(n0, n), :] = vt[n:2 * n]
        return carry

    lax.fori_loop(0, ng // unroll, intra, 0)

    sub = SUBLANES
    nv = tc // sub
    row = lax.broadcasted_iota(jnp.int32, (tc, LANES), 0)
    in_vreg = lax.rem(row, sub)

    def rows_of(v, r):
        return jnp.broadcast_to(v[r:r + 1], (tc, LANES))

    for j in range(ng * n // LANES):
        cols = slice(j * LANES, (j + 1) * LANES)
        pwr, pwi = are_ref[:, cols], aim_ref[:, cols]
        xr = vr_ref[cols, :].T
        xi = vi_ref[cols, :].T
        for d in (1, 2, 4):
            keep = in_vreg >= d
            sr = jnp.where(keep, pltpu.roll(xr, d, 0), 0.0)
            si = jnp.where(keep, pltpu.roll(xi, d, 0), 0.0)
            fr, fi = rows_of(pwr, d - 1), rows_of(pwi, d - 1)
            xr, xi = xr + (fr * sr - fi * si), xi + (fr * si + fi * sr)
        cr, ci = carry_ref[0, :, cols], carry_ref[1, :, cols]
        cr0, ci0 = cr, ci
        outr, outi = [], []
        for v in range(nv):
            yr = xr[v * sub:(v + 1) * sub] + (pwr * cr - pwi * ci)
            yi = xi[v * sub:(v + 1) * sub] + (pwr * ci + pwi * cr)
            outr.append(yr)
            outi.append(yi)
            cr = jnp.broadcast_to(yr[sub - 1:sub], (sub, LANES))
            ci = jnp.broadcast_to(yi[sub - 1:sub], (sub, LANES))
        carry_ref[0, :, cols] = cr
        carry_ref[1, :, cols] = ci
        sr = jnp.concatenate(outr, axis=0)
        si = jnp.concatenate(outi, axis=0)
        first = row == 0
        spr_ref[j] = jnp.where(first, rows_of(cr0, 0), pltpu.roll(sr, 1, 0))
        spi_ref[j] = jnp.where(first, rows_of(ci0, 0), pltpu.roll(si, 1, 0))

    def cross(it, carry):
        for k in range(unroll):
            jp = it * unroll + k
            r0 = pl.multiple_of(jp * 2 * gp, 2 * gp)
            yc = (_dot_nt(ctr_ref[jp], spr_ref[jp].astype(BF16))
                  + _dot_nt(cti_ref[jp], spi_ref[jp].astype(BF16)))
            yt_ref[:, pl.ds(r0, 2 * gp), :] += yc.reshape(S5_T, 2 * gp, tc)
        return carry

    lax.fori_loop(0, ng // 2 // unroll, cross, 0)

    for s in range(S5_T):
        for k in range(ncb):
            ys_ref[k, pl.ds(s, tc, stride=S5_T), :] = yt_ref[s, k * LANES:(k + 1) * LANES, :].T
    for k in range(ncb):
        y_ref[:, k * LANES:(k + 1) * LANES] = ys_ref[k]


def _s5_matrices(lam_re, lam_im, log_step, b_re, b_im, c_re, c_im, d_skip):
    hi = lax.Precision.HIGHEST
    t, gp, n, ng = S5_T, S5_GROUP, S5_STATE, S5_GROUPS
    lam = lax.complex(lam_re.astype(F32), lam_im.astype(F32))
    step = jnp.exp(log_step.astype(F32))[:, None]
    ls = lam * step
    a_bar = jnp.exp(ls)
    b_bar = ((a_bar - 1.0) / lam)[..., None] * lax.complex(b_re.astype(F32), b_im.astype(F32))
    cm = lax.complex(c_re.astype(F32), c_im.astype(F32))

    def apow(k):
        kk = k.astype(F32).astype(jnp.complex64)
        return jnp.exp(ls.reshape((ng,) + (1,) * k.ndim + (n,)) * kk[None, ..., None])

    tt = jnp.arange(t)
    kmat = jnp.einsum('gpn,gln,gnq->glpq', cm, apow(tt), b_bar, precision=hi).real
    krev = jnp.transpose(kmat[:, ::-1], (0, 2, 1, 3)).reshape(ng, gp, t * gp)
    kpad = jnp.pad(krev, ((0, 0), (0, 0), (0, t * gp)))
    mt = jnp.concatenate([kpad[:, :, (t - 1 - to) * gp:(2 * t - 1 - to) * gp] for to in range(t)], axis=1)
    dvec = jnp.tile(d_skip.astype(F32).reshape(ng, 1, gp), (1, t, 1)).reshape(ng, t * gp)
    mt = mt + jnp.eye(t * gp, dtype=F32)[None] * dvec[:, :, None]
    z = jnp.swapaxes(apow(t - 1 - tt), 1, 2)[:, :, :, None] * b_bar[:, :, None, :]
    z = z.reshape(ng, n, t * gp)
    bt = jnp.concatenate([z.real, z.imag], axis=1)
    w = cm[:, None, :, :] * apow(tt + 1)[:, :, None, :]

    def pair_readout(x):
        x = x.reshape(ng // 2, 2, t, gp, n)
        first = jnp.pad(x[:, 0], ((0, 0), (0, 0), (0, 0), (0, n)))
        second = jnp.pad(x[:, 1], ((0, 0), (0, 0), (0, 0), (n, 0)))
        return jnp.stack([first, second], axis=2).reshape(ng // 2, t * 2 * gp, 2 * n).astype(BF16)

    ctr, cti = pair_readout(w.real), pair_readout(-w.imag)
    a_chunk = jnp.transpose(apow(t * (jnp.arange(SUBLANES) + 1)), (1, 0, 2)).reshape(SUBLANES, ng * n)
    return mt.astype(BF16), bt.astype(BF16), ctr, cti, a_chunk.real, a_chunk.imag


def _s5(proj, mats):
    seq, width = proj.shape
    t, tc, gp, n, ng = S5_T, S5_TC, S5_GROUP, S5_STATE, S5_GROUPS
    rows = t * tc
    ncb = S5_CH // LANES
    cb0 = (width - S5_CH) // LANES
    u_specs = [pl.BlockSpec((rows, LANES), (lambda i, k=k: (i, cb0 + k))) for k in range(ncb)]
    nsb = ng * n // LANES
    return pl.pallas_call(
        _s5_kernel,
        grid=(seq // rows,),
        in_specs=u_specs + [_const_spec(m.shape) for m in mats],
        out_specs=pl.BlockSpec((rows, S5_CH), lambda i: (i, 0)),
        out_shape=jax.ShapeDtypeStruct((seq, S5_CH), F32),
        scratch_shapes=[
            pltpu.VMEM((t, S5_CH, tc), F32),
            pltpu.VMEM((t, S5_CH, tc), F32),
            pltpu.VMEM((ncb, rows, LANES), F32),
            pltpu.VMEM((ng * n, tc), F32),
            pltpu.VMEM((ng * n, tc), F32),
            pltpu.VMEM((nsb, tc, LANES), F32),
            pltpu.VMEM((nsb, tc, LANES), F32),
            pltpu.VMEM((2, SUBLANES, ng * n), F32),
        ],
        compiler_params=pltpu.CompilerParams(dimension_semantics=("arbitrary",)),
        name="s5_scan",
    )(*([proj] * ncb), *mats)


def _mix_ffn_kernel(*refs, glu, final):
    (x_ref, a_ref, b_ref, wo_ref, g1_ref), refs = refs[:5], refs[5:]
    if glu:
        gw_ref, refs = refs[0], refs[1:]
    (g_ref, sc_ref, sh_ref, gate_ref, win_ref, cw_ref, cb_ref, wout_ref), refs = refs[:8], refs[8:]
    if final:
        fg_ref, o_ref, h_ref, act_ref, gbuf_ref, carry_ref = refs
    else:
        ng_ref, nsc_ref, nsh_ref, o_ref, hn_ref, h_ref, act_ref, gbuf_ref, carry_ref = refs
    tm = x_ref.shape[0]
    halo = gbuf_ref.shape[0] - tm

    @pl.when(pl.program_id(0) == 0)
    def _():
        carry_ref[...] = jnp.zeros(carry_ref.shape, F32)

    if glu:
        y = jax.nn.gelu(b_ref[...]).astype(BF16)
        gg = _dot(y, gw_ref[...])
        half = gg.shape[1] // 2
        b = (gg[:, :half] * jax.nn.sigmoid(gg[:, half:])).astype(BF16)
    else:
        b = b_ref[...]
    cat = jnp.concatenate([a_ref[...], b], axis=1)
    x = x_ref[...] + g1_ref[...] * _dot(cat, wo_ref[...])
    h_ref[...] = _mod_rmsnorm(x, g_ref[...], sc_ref[...], sh_ref[...]).astype(BF16)
    for f in range(D_FF // TF_FFN):
        cs = slice(f * TF_FFN, (f + 1) * TF_FFN)
        gs = slice(D_FF + f * TF_FFN, D_FF + (f + 1) * TF_FFN)
        h = h_ref[...]
        val = _dot(h, win_ref[:, cs])
        gate = _dot(h, win_ref[:, gs])
        gbuf_ref[0:halo, :] = carry_ref[:, cs]
        gbuf_ref[halo:halo + tm, :] = gate
        carry_ref[:, cs] = gate[tm - halo:tm, :]
        conv = (gate * cw_ref[2:3, cs] + gbuf_ref[halo - 1:halo - 1 + tm, :] * cw_ref[1:2, cs]
                + gbuf_ref[halo - 2:halo - 2 + tm, :] * cw_ref[0:1, cs] + cb_ref[:, cs])
        act_ref[:, cs] = (jax.nn.gelu(conv) * val).astype(BF16)
    xn = x + gate_ref[...] * _dot(act_ref[...], wout_ref[...])
    if final:
        xn = xn * lax.rsqrt(jnp.mean(xn * xn, axis=-1, keepdims=True) + EPS) * fg_ref[...]
    else:
        hn_ref[...] = _mod_rmsnorm(xn, ng_ref[...], nsc_ref[...], nsh_ref[...]).astype(BF16)
    o_ref[...] = xn


def _layer_spec(shape, layer):
    idx = (layer,) + (0,) * (len(shape) - 1)
    return pl.BlockSpec((None,) + tuple(shape[1:]), lambda *_: idx, pipeline_mode=pl.Buffered(1))


def _mix_ffn(x, a, b, wo, gate1, glu_w, g, scale, shift, gate2, w_in, conv_w, conv_b, w_out, tail, layer):
    seq, d = x.shape
    final = len(tail) == 1
    tm = TM_FFN
    halo = SUBLANES
    row = pl.BlockSpec((1, d), lambda i: (0, 0))
    rows = lambda w: pl.BlockSpec((tm, w), lambda i: (i, 0))
    conv_b = conv_b.reshape(conv_b.shape[0], 1, D_FF)
    in_specs = [rows(d), rows(a.shape[1]), rows(b.shape[1]), _const_spec(wo.shape), row]
    args = [x, a, b, wo, gate1]
    if glu_w is not None:
        in_specs.append(_const_spec(glu_w.shape))
        args.append(glu_w)
    in_specs += [
        row, row, row, row,
        _layer_spec(w_in.shape, layer),
        _layer_spec(conv_w.shape, layer),
        _layer_spec(conv_b.shape, layer),
        _layer_spec(w_out.shape, layer),
    ] + [row] * len(tail)
    args += [g.reshape(1, d), scale, shift, gate2, w_in, conv_w, conv_b, w_out]
    args += [t.reshape(1, d) for t in tail]
    out_specs = [rows(d)] if final else [rows(d), rows(d)]
    out_shape = [jax.ShapeDtypeStruct((seq, d), F32)] + ([] if final else [jax.ShapeDtypeStruct((seq, d), BF16)])
    return pl.pallas_call(
        functools.partial(_mix_ffn_kernel, glu=glu_w is not None, final=final),
        grid=(seq // tm,),
        in_specs=in_specs,
        out_specs=out_specs,
        out_shape=out_shape,
        scratch_shapes=[
            pltpu.VMEM((tm, d), BF16),
            pltpu.VMEM((tm, D_FF), BF16),
            pltpu.VMEM((tm + halo, TF_FFN), F32),
            pltpu.VMEM((halo, D_FF), F32),
        ],
        compiler_params=pltpu.CompilerParams(dimension_semantics=("arbitrary",)),
        name="mix_ffn",
    )(*args)


def kernel(x, c, t5_table, mod_w, mod_b, norm1_g, norm2_g, ffn_w_in, ffn_conv_w, ffn_conv_b, ffn_w_out,
           ev_w_in, ev_w_out, diff_lambda, diff_subln_g, band_rel_bias,
           od_w_in, od_w_out, s5_lam_re, s5_lam_im, s5_log_step, s5_b_re, s5_b_im, s5_c_re, s5_c_im,
           s5_d, s5_glu_w, final_g):
    assert x.shape[0] == 1 and x.shape[2] == D_MODEL
    seq = x.shape[1]
    assert seq % TM_PROJ == 0 and seq % (S5_T * S5_TC) == 0
    d = D_MODEL
    xs = x[0]
    mod = _modulation(c, mod_w, mod_b)
    ffn_w_in_b = ffn_w_in.astype(BF16)
    ffn_w_out_b = ffn_w_out.astype(BF16)
    mods = [[mod[i, :, k * d:(k + 1) * d] for k in range(6)] for i in range(DEPTH)]
    h = None
    for i in range(DEPTH):
        sh1, sc1, g1, sh2, sc2, g2 = mods[i]
        w_in = (ev_w_in if i % 2 == 0 else od_w_in)[i // 2].astype(BF16)
        proj_dtype = BF16 if i % 2 == 0 else F32
        if h is None:
            proj = _normproj(xs, norm1_g[i], sc1, sh1, w_in, proj_dtype)
        else:
            proj = _proj(h, w_in, proj_dtype)
        if i % 2 == 0:
            e = i // 2
            lam_init = 0.8 - 0.6 * math.exp(-0.3 * i)
            lp = diff_lambda[e].astype(F32)
            lam = jnp.exp(jnp.sum(lp[0] * lp[1])) - jnp.exp(jnp.sum(lp[2] * lp[3])) + lam_init
            mix_a = _diff_attention(proj, t5_table, lam, diff_subln_g[e], lam_init)
            mix_b = _band_attention(proj, band_rel_bias[e])
            wo, glu_w = ev_w_out[e].astype(BF16), None
        else:
            o = i // 2
            mix_a = _retention(proj)
            mats = _s5_matrices(s5_lam_re[o], s5_lam_im[o], s5_log_step[o], s5_b_re[o], s5_b_im[o],
                                s5_c_re[o], s5_c_im[o], s5_d[o])
            mix_b = _s5(proj, mats)
            wo, glu_w = od_w_out[o].astype(BF16), s5_glu_w[o].astype(BF16)
        if i == DEPTH - 1:
            tail = (final_g,)
        else:
            nsh1, nsc1 = mods[i + 1][0], mods[i + 1][1]
            tail = (norm1_g[i + 1], nsc1, nsh1)
        out = _mix_ffn(xs, mix_a, mix_b, wo, g1, glu_w, norm2_g[i], sc2, sh2, g2,
                       ffn_w_in_b, ffn_conv_w, ffn_conv_b, ffn_w_out_b, tail, layer=i)
        if i == DEPTH - 1:
            xs = out[0]
        else:
            xs, h = out
    return xs[None]
```

```python
import functools
import math

import jax
import jax.numpy as jnp
import numpy as np
from jax import lax
from jax.experimental import pallas as pl
from jax.experimental.pallas import tpu as pltpu

F32 = jnp.float32
BF16 = jnp.bfloat16

D_MODEL = 1024
DEPTH = 2
CHUNK = 64
GROUP_WIDTH = D_MODEL // 2
DK_A = 64
DV_A = 2 * DK_A
N_HEADS_A = GROUP_WIDTH // DV_A
DH_B = 64
N_HEADS_B = GROUP_WIDTH // DH_B
LEFT_CHUNKS = 8
REL_CLIP = 2 * CHUNK
NUM_BUCKETS = 32
MAX_DISTANCE = 128
DV_C = 128
DQK_C = DV_C // 2
N_HEADS_C = GROUP_WIDTH // DV_C
ROPE_BASE = 10000.0
S5_CH = GROUP_WIDTH
S5_GROUP = 16
S5_GROUPS = S5_CH // S5_GROUP
S5_STATE = 64
D_FF = ((8 * D_MODEL // 3 + 255) // 256) * 256
CONV_W = 3
EVEN_IN = 3 * N_HEADS_A * DV_A + 3 * N_HEADS_B * DH_B
ODD_IN = 2 * N_HEADS_C * DQK_C + 2 * N_HEADS_C * DV_C + S5_CH
EPS = 1e-6
NEG_INF = -1e30
LOG2E = math.log2(math.e)

LANES = 128
SUBLANES = 8
MXU_DIM = 256

TM_PROJ = 1024
TN_PROJ = 1024
TN_MOD = 1536
TM_FFN = 512
TF_FFN = MXU_DIM
BLK_A = 512
NPART_A = 2
ONES_A = 16
SINGLE_PASS_LOG2_RANGE = 96.0
SCORE_PAD = LANES
BLK_B = 1024
BAND_B = LEFT_CHUNKS * CHUNK
QW_B = 4 * CHUNK
BLK_C = 512
S5_T = 16
S5_TC = LANES

assert BLK_B % BAND_B == 0 and BLK_B % QW_B == 0 and BAND_B % QW_B == 0
assert BLK_A >= MAX_DISTANCE, "far key blocks must sit in the saturated T5 bucket"
assert DV_A == LANES and 2 * DK_A == LANES and 2 * DH_B == LANES, "attention heads are read as 128-lane column blocks"


def _dot(a, b):
    return jnp.dot(a, b, preferred_element_type=F32)


def _dot_nt(a, b):
    return lax.dot_general(a, b, (((1,), (1,)), ((), ())), preferred_element_type=F32)


def _dot_tn(a, b):
    return lax.dot_general(a, b, (((0,), (0,)), ((), ())), preferred_element_type=F32)


def _const_spec(shape):
    zeros = (0,) * len(shape)
    return pl.BlockSpec(shape, lambda *_: zeros, pipeline_mode=pl.Buffered(1))


def _mod_rmsnorm(x, g, scale, shift):
    y = x * lax.rsqrt(jnp.mean(x * x, axis=-1, keepdims=True) + EPS)
    y = y * g
    return y * (1.0 + scale) + shift


def _mod_kernel(c_ref, w_ref, b_ref, o_ref):
    c = c_ref[...]
    cond = c * jax.nn.sigmoid(c)
    o_ref[0] = jnp.sum(cond * w_ref[0], axis=0, keepdims=True) + b_ref[0]


def _modulation(c, mod_w, mod_b):
    depth, d, n = mod_w.shape
    tn = TN_MOD
    return pl.pallas_call(
        _mod_kernel,
        grid=(depth, n // tn),
        in_specs=[
            pl.BlockSpec((d, 1), lambda i, j: (0, 0)),
            pl.BlockSpec((1, d, tn), lambda i, j: (i, 0, j)),
            pl.BlockSpec((1, 1, tn), lambda i, j: (i, 0, j)),
        ],
        out_specs=pl.BlockSpec((1, 1, tn), lambda i, j: (i, 0, j)),
        out_shape=jax.ShapeDtypeStruct((depth, 1, n), F32),
        name="modulation",
    )(c.reshape(d, 1), mod_w, mod_b.reshape(depth, 1, n))


def _normproj_kernel(x_ref, g_ref, sc_ref, sh_ref, w_ref, o_ref):
    tm, n = o_ref.shape
    half = tm // 2
    for r in range(2):
        rows = slice(r * half, (r + 1) * half)
        h = _mod_rmsnorm(x_ref[rows, :], g_ref[...], sc_ref[...], sh_ref[...]).astype(BF16)
        for j in range(n // TN_PROJ):
            cols = slice(j * TN_PROJ, (j + 1) * TN_PROJ)
            o_ref[rows, cols] = _dot(h, w_ref[:, cols]).astype(o_ref.dtype)


def _normproj(x, g, scale, shift, w, out_dtype):
    seq, d = x.shape
    n = w.shape[1]
    tm = TM_PROJ
    row = pl.BlockSpec((1, d), lambda i: (0, 0))
    return pl.pallas_call(
        _normproj_kernel,
        grid=(seq // tm,),
        in_specs=[pl.BlockSpec((tm, d), lambda i: (i, 0)), row, row, row, _const_spec(w.shape)],
        out_specs=pl.BlockSpec((tm, n), lambda i: (i, 0)),
        out_shape=jax.ShapeDtypeStruct((seq, n), out_dtype),
        compiler_params=pltpu.CompilerParams(dimension_semantics=("parallel",)),
        name="normproj",
    )(x, g.reshape(1, d), scale, shift, w)


def _proj_kernel(h_ref, w_ref, o_ref):
    for j in range(o_ref.shape[1] // TN_PROJ):
        cols = slice(j * TN_PROJ, (j + 1) * TN_PROJ)
        o_ref[:, cols] = _dot(h_ref[...], w_ref[:, cols]).astype(o_ref.dtype)


def _proj(h, w, out_dtype):
    seq, d = h.shape
    n = w.shape[1]
    tm = TM_PROJ
    return pl.pallas_call(
        _proj_kernel,
        grid=(seq // tm,),
        in_specs=[pl.BlockSpec((tm, d), lambda i: (i, 0)), _const_spec(w.shape)],
        out_specs=pl.BlockSpec((tm, n), lambda i: (i, 0)),
        out_shape=jax.ShapeDtypeStruct((seq, n), out_dtype),
        compiler_params=pltpu.CompilerParams(dimension_semantics=("parallel",)),
        name="proj",
    )(h, w)


def _diffattn_kernel(qall_ref, k_ref, v_ref, bias_ref, bstat_ref, lam_ref, g_ref, o_ref,
                     flag_ref, qs_ref, vt_ref, kmax_ref, r_ref, m_ref, acc_ref, *s_refs, out_scale):
    blk = BLK_A
    nq = 2 * blk
    sub = SUBLANES
    dv = DV_A
    npart = NPART_A
    sa_ref, sb_ref = s_refs[:2 * npart], s_refs[2 * npart:4 * npart]
    pa_ref, pb_ref = s_refs[4 * npart:5 * npart], s_refs[5 * npart:6 * npart]
    i = pl.program_id(1)
    lane = lax.broadcasted_iota(jnp.int32, (blk, LANES), 1)
    same_subhead = (lax.broadcasted_iota(jnp.int32, (LANES, LANES), 0) // DK_A
                    == lax.broadcasted_iota(jnp.int32, (LANES, LANES), 1) // DK_A).astype(BF16)

    bias_max, bias_span = bstat_ref[0, 0:1, 0:1], bstat_ref[0, 1:2, 0:1]
    q_scale = DK_A ** -0.5 * LOG2E

    @pl.when(i == 0)
    def _():
        kmax_ref[...] = jnp.zeros(kmax_ref.shape, F32)

        def tr(b, qmax):
            r0 = pl.multiple_of(b * blk, blk)
            c0 = pl.multiple_of(b * nq, nq)
            vt_ref[0:dv, pl.ds(r0, blk)] = v_ref[pl.ds(r0, blk), :].astype(F32).T.astype(BF16)
            vt_ref[dv:dv + ONES_A, pl.ds(r0, blk)] = jnp.ones((ONES_A, blk), BF16)
            kf = k_ref[pl.ds(r0, blk), :].astype(F32)
            kn2 = _dot((kf * kf).astype(BF16), same_subhead)
            kmax_ref[...] = jnp.maximum(kmax_ref[...], jnp.max(kn2.reshape(blk // sub, sub, LANES), axis=0))
            qa = (qall_ref[pl.ds(r0, blk), :].astype(F32) * q_scale).astype(BF16).astype(F32)
            for m in range(2):
                qt = jnp.where((lane < DK_A) if m == 0 else (lane >= DK_A), qa, 0.0).T
                qs_ref[:, pl.ds(c0 + m * blk, blk)] = qt.astype(BF16)
                qn2 = jnp.sum(jnp.sum((qt * qt).reshape(LANES // sub, sub, blk), axis=0), axis=0, keepdims=True)
                qn2 = jnp.broadcast_to(qn2, (sub, blk))
                r_ref[:, pl.ds(c0 + m * blk, blk)] = qn2
                qmax = jnp.maximum(qmax, qn2)
            return qmax
        nblocks = v_ref.shape[0] // blk
        qmax2 = jnp.max(lax.fori_loop(0, nblocks, tr, jnp.zeros((sub, blk), F32)))
        kmax2 = jnp.max(kmax_ref[...], axis=0, keepdims=True)
        worst = 2.0 * jnp.sqrt(qmax2 * kmax2) * 1.03 + bias_span
        flag_ref[0] = (jnp.max(worst) < SINGLE_PASS_LOG2_RANGE).astype(jnp.int32)

        kmax_nq = jnp.concatenate([jnp.broadcast_to(kmax2[:, m * DK_A:m * DK_A + 1], (sub, blk)) for m in range(2)],
                                  axis=1)

        def shifts(b, carry):
            c0 = pl.multiple_of(b * nq, nq)
            r_ref[:, pl.ds(c0, nq)] = jnp.sqrt(r_ref[:, pl.ds(c0, nq)] * kmax_nq) * 1.03 + bias_max
            return carry
        lax.fori_loop(0, nblocks, shifts, 0)

    acc_ref[...] = jnp.zeros(acc_ref.shape, F32)
    q0 = pl.multiple_of(i * nq, nq)
    single_pass = flag_ref[0] == 1

    @pl.when(single_pass)
    def _():
        _diffattn_fixed_shift(i, q0, k_ref, bias_ref, qs_ref, vt_ref, r_ref, m_ref, acc_ref, pa_ref, pb_ref)

    @pl.when(jnp.logical_not(single_pass))
    def _():
        _diffattn_online(i, q0, k_ref, bias_ref, qs_ref, vt_ref, m_ref, acc_ref, sa_ref, sb_ref)

    ot = acc_ref[0:dv, 0:nq] / acc_ref[dv:dv + 1, 0:nq]
    o = ot[:, 0:blk].T - lam_ref[...] * ot[:, blk:nq].T
    o = o * lax.rsqrt(jnp.mean(o * o, axis=-1, keepdims=True) + EPS) * g_ref[...]
    o_ref[...] = (o * out_scale).astype(o_ref.dtype)


def _diffattn_fixed_shift(i, q0, k_ref, bias_ref, qs_ref, vt_ref, shift_ref, l_ref, acc_ref, pa_ref, pb_ref):
    blk = BLK_A
    nq = 2 * blk
    sub = SUBLANES
    npart = len(pa_ref)
    wq = nq // npart
    l_ref[...] = jnp.zeros(l_ref.shape, F32)

    def probs(b, p_ref, bias):
        k = k_ref[pl.ds(pl.multiple_of(b * blk, blk), blk), :]
        for part in range(npart):
            cols = slice(part * wq, (part + 1) * wq)
            qcols = pl.ds(pl.multiple_of(q0 + part * wq, wq), wq)
            s = _dot(k, qs_ref[:, qcols])
            if bias is not None:
                b0 = (part * wq) % blk
                s = s + bias[:, b0:b0 + wq]
            p = jnp.exp2(s.reshape(blk // sub, sub, wq) - shift_ref[:, qcols][None])
            l_ref[:, cols] += jnp.sum(p, axis=0)
            p_ref[part][:, 0:wq] = p.reshape(blk, wq).astype(BF16)

    def accumulate(b, p_ref):
        vt = vt_ref[0:DV_A, pl.ds(pl.multiple_of(b * blk, blk), blk)]
        for part in range(npart):
            cols = slice(part * wq, (part + 1) * wq)
            acc_ref[0:DV_A, cols] += _dot(vt, p_ref[part][:, 0:wq])

    @pl.when(i == 0)
    def _():
        probs(0, pa_ref, bias_ref[0, 1])
        accumulate(0, pa_ref)

    @pl.when(i > 0)
    def _():
        nfar = i - 1
        probs(i, pa_ref, bias_ref[0, 1])
        probs(i - 1, pb_ref, bias_ref[0, 0])
        accumulate(i, pa_ref)

        def pair(t):
            probs(2 * t, pa_ref, None)
            accumulate(jnp.where(t == 0, i - 1, 2 * t - 1), pb_ref)
            probs(2 * t + 1, pb_ref, None)
            accumulate(2 * t, pa_ref)

        def four_pairs(u, carry):
            for v in range(4):
                pair(4 * u + v)
            return carry

        npairs = nfar // 2
        lax.fori_loop(0, npairs // 4, four_pairs, 0)

        def one_pair(t, carry):
            pair(t)
            return carry
        lax.fori_loop(4 * (npairs // 4), npairs, one_pair, 0)
        in_pb = jnp.where(npairs == 0, i - 1, 2 * npairs - 1)

        @pl.when(lax.rem(nfar, 2) == 1)
        def _():
            probs(nfar - 1, pa_ref, None)
            accumulate(in_pb, pb_ref)
            accumulate(nfar - 1, pa_ref)

        @pl.when(lax.rem(nfar, 2) == 0)
        def _():
            accumulate(in_pb, pb_ref)

    acc_ref[DV_A:DV_A + sub, 0:nq] = jnp.broadcast_to(jnp.sum(l_ref[...], axis=0, keepdims=True), (sub, nq))


def _diffattn_online(i, q0, k_ref, bias_ref, qs_ref, vt_ref, m_ref, acc_ref, sa_ref, sb_ref):
    blk = BLK_A
    nq = 2 * blk
    sub = SUBLANES
    npart = len(sa_ref) // 2
    wq = nq // npart
    m_ref[...] = jnp.full(m_ref.shape, NEG_INF, F32)

    def scores(b, s_ref):
        k = k_ref[pl.ds(pl.multiple_of(b * blk, blk), blk), :]
        for part in range(npart):
            s = _dot(k, qs_ref[:, pl.ds(pl.multiple_of(q0 + part * wq, wq), wq)])
            s_ref[part][:, 0:wq] = s
            s_ref[npart + part][...] = jnp.max(s.reshape(blk // sub, sub, wq), axis=0)

    def softmax_pv(b, s_ref, bias):
        vt = vt_ref[:, pl.ds(pl.multiple_of(b * blk, blk), blk)]
        for part in range(npart):
            cols = slice(part * wq, (part + 1) * wq)
            s = s_ref[part][:, 0:wq]
            if bias is not None:
                b0 = (part * wq) % blk
                s = s + bias[:, b0:b0 + wq]
            s = s.reshape(blk // sub, sub, wq)
            m_prev = m_ref[:, cols]
            smax = jnp.max(s, axis=0) if bias is not None else s_ref[npart + part][...]
            m_cur = jnp.max(smax, axis=0, keepdims=True)
            m_new = jnp.maximum(m_prev, m_cur)
            alpha = jnp.exp2(m_prev - m_new)
            p = jnp.exp2(s - m_new[None])
            pv = _dot(vt, p.reshape(blk, wq).astype(BF16))
            acc_ref[:, cols] = acc_ref[:, cols] * alpha[0:1] + pv
            m_ref[:, cols] = m_new

    nfar = jnp.maximum(i - 1, 0)
    odd = lax.rem(nfar, 2)

    @pl.when(i == 0)
    def _():
        scores(0, sb_ref)

    @pl.when(i > 0)
    def _():
        @pl.when(odd == 1)
        def _():
            scores(0, sb_ref)
            scores(1, sa_ref)
            softmax_pv(0, sb_ref, None)

        @pl.when(odd == 0)
        def _():
            scores(0, sa_ref)

        def pair(b):
            scores(b + 1, sb_ref)
            softmax_pv(b, sa_ref, None)
            scores(b + 2, sa_ref)
            softmax_pv(b + 1, sb_ref, None)

        def quad_body(t, carry):
            pair(odd + 4 * t)
            pair(odd + 4 * t + 2)
            return carry

        npairs = nfar // 2
        lax.fori_loop(0, npairs // 2, quad_body, 0)

        @pl.when(lax.rem(npairs, 2) == 1)
        def _():
            pair(odd + 2 * (npairs - 1))
        scores(i, sb_ref)
        softmax_pv(i - 1, sa_ref, bias_ref[0, 0])

    softmax_pv(i, sb_ref, bias_ref[0, 1])


_TOEPLITZ_ROWS = 512


def _toeplitz_kernel(v_ref, o_ref, *, keep):
    rows, cols = o_ref.shape[2:]
    x = jnp.broadcast_to(v_ref[0, 0], (rows, v_ref.shape[-1]))
    tile = pltpu.roll(x, 0, 1, stride=1, stride_axis=0)[:, :cols]
    r = lax.broadcasted_iota(jnp.int32, (rows, cols), 0) + pl.program_id(1) * rows
    c = lax.broadcasted_iota(jnp.int32, (rows, cols), 1)
    for variant in range(o_ref.shape[0]):
        o_ref[variant, 0] = jnp.where(keep(r, c, variant), tile, NEG_INF)


def _toeplitz_tiles(fn, keep, heads, rows, cols, variants=1):
    n = rows + cols
    rb = rows if rows % _TOEPLITZ_ROWS else _TOEPLITZ_ROWS
    assert rows % rb == 0 and n % LANES == 0
    idx = jnp.arange(n, dtype=jnp.int32)
    vec = fn(jnp.where(idx < cols, idx, idx - n)).astype(F32)
    vecs = jnp.stack([jnp.roll(vec, k * rb, axis=1) for k in range(rows // rb)], axis=1)
    return pl.pallas_call(
        functools.partial(_toeplitz_kernel, keep=keep),
        grid=(heads, rows // rb),
        in_specs=[pl.BlockSpec((1, 1, 1, n), lambda h, k: (h, k, 0, 0))],
        out_specs=pl.BlockSpec((variants, 1, rb, cols), lambda h, k: (0, h, k, 0)),
        out_shape=jax.ShapeDtypeStruct((variants, heads, rows, cols), F32),
        name="toeplitz_tiles",
    )(vecs.reshape(heads, rows // rb, 1, n))


def _bias_stats(tiles):
    finite = tiles > 0.5 * NEG_INF
    bias_max = jnp.maximum(jnp.max(jnp.where(finite, tiles, NEG_INF), axis=(1, 2)), 0.0)
    bias_min = jnp.minimum(jnp.min(jnp.where(finite, tiles, -NEG_INF), axis=(1, 2)), 0.0)
    return jnp.broadcast_to(jnp.stack([bias_max, bias_max - bias_min], axis=1)[:, :, None],
                            (tiles.shape[0], 2, LANES))


def _t5_bucket(rel):
    nb = NUM_BUCKETS // 2
    max_exact = nb // 2
    bucket = jnp.where(rel > 0, nb, 0)
    n = jnp.abs(rel)
    nf = jnp.maximum(n, 1).astype(F32)
    large = max_exact + (jnp.log(nf / max_exact) / math.log(MAX_DISTANCE / max_exact)
                         * (nb - max_exact)).astype(jnp.int32)
    large = jnp.minimum(large, nb - 1)
    return bucket + jnp.where(n < max_exact, n, large)


def _diff_bias_tiles(t5_table):
    blk = BLK_A
    table = t5_table.astype(F32)
    far = table[_t5_bucket(jnp.full((), -(blk + 1), jnp.int32))]
    def visible(r, c, variant):
        return jnp.floor_divide(r - blk, CHUNK) <= jnp.floor_divide(c, CHUNK)

    tiles = _toeplitz_tiles(lambda x: ((table[_t5_bucket(-x - blk)] - far) * LOG2E).T, visible,
                            N_HEADS_A, 2 * blk, blk)
    return tiles.reshape(N_HEADS_A, 2, blk, blk)


def _diff_attention(proj, t5_table, lam, subln_g, lam_init):
    seq = proj.shape[0]
    blk = BLK_A
    bias = _diff_bias_tiles(t5_table)
    ha = N_HEADS_A
    bstat = _bias_stats(bias.reshape(ha, 2 * blk, blk))
    kern = functools.partial(_diffattn_kernel, out_scale=1.0 - lam_init)
    return pl.pallas_call(
        kern,
        grid=(ha, seq // blk),
        in_specs=[
            pl.BlockSpec((seq, DV_A), lambda h, i: (0, h)),
            pl.BlockSpec((seq, DV_A), lambda h, i: (0, ha + h)),
            pl.BlockSpec((seq, DV_A), lambda h, i: (0, 2 * ha + h)),
            pl.BlockSpec((1, 2, blk, blk), lambda h, i: (h, 0, 0, 0)),
            pl.BlockSpec((1, 2, LANES), lambda h, i: (h, 0, 0)),
            pl.BlockSpec((1, DV_A), lambda h, i: (0, 0)),
            pl.BlockSpec((1, DV_A), lambda h, i: (0, 0)),
        ],
        out_specs=pl.BlockSpec((blk, DV_A), lambda h, i: (i, h)),
        out_shape=jax.ShapeDtypeStruct((seq, ha * DV_A), BF16),
        scratch_shapes=[
            pltpu.SMEM((1,), jnp.int32),
            pltpu.VMEM((DV_A, 2 * seq), BF16),
            pltpu.VMEM((DV_A + ONES_A, seq), BF16),
            pltpu.VMEM((SUBLANES, LANES), F32),
            pltpu.VMEM((SUBLANES, 2 * seq), F32),
            pltpu.VMEM((SUBLANES, 2 * blk), F32),
            pltpu.VMEM((DV_A + ONES_A, 2 * blk), F32),
        ] + 2 * ([pltpu.VMEM((blk, 2 * blk // NPART_A + SCORE_PAD), F32)] * NPART_A
                 + [pltpu.VMEM((SUBLANES, 2 * blk // NPART_A), F32)] * NPART_A)
        + 2 * [pltpu.VMEM((blk, 2 * blk // NPART_A + SCORE_PAD), BF16)] * NPART_A,
        compiler_params=pltpu.CompilerParams(dimension_semantics=("parallel", "arbitrary")),
        name="diff_attention",
    )(proj, proj, proj, bias, bstat, jnp.full((1, DV_A), lam, F32), subln_g.reshape(1, DV_A).astype(F32))


def _band_kernel(q_ref, kp_ref, kc_ref, vp_ref, vc_ref, *refs):
    qw, band = QW_B, BAND_B
    nbias = band // qw + 1
    bias_refs, bstat_ref, o_ref = refs[:nbias], refs[nbias], refs[nbias + 1]
    ngroups = (len(refs) - nbias - 2) // 2
    s_refs, p_refs = refs[nbias + 2:nbias + 2 + ngroups], refs[nbias + 2 + ngroups:]
    nk = band + qw
    sub = SUBLANES
    q = (q_ref[...].astype(F32) * (DH_B ** -0.5 * LOG2E)).astype(BF16).astype(F32)
    lane = lax.broadcasted_iota(jnp.int32, q.shape, 1)
    qt = (jnp.where(lane < DH_B, q, 0.0).T, jnp.where(lane >= DH_B, q, 0.0).T)
    qh = (qt[0].astype(BF16), qt[1].astype(BF16))
    k_all = jnp.concatenate([kp_ref[...], kc_ref[...]], axis=0)
    vt_all = jnp.concatenate([vp_ref[...], vc_ref[...]], axis=0).astype(F32).T.astype(BF16)

    def group_operands(g):
        k0 = g * qw
        qs = jnp.concatenate([qh[0][:, k0:k0 + qw], qh[1][:, k0:k0 + qw]], axis=1)
        bias_ref = bias_refs[min(g, nbias - 1)]
        bias = jnp.concatenate([bias_ref[0, 0], bias_ref[0, 1]], axis=1)
        return k0, qs, bias

    def store_group(g, ot):
        o = jnp.concatenate([ot[0:DH_B, 0:qw], ot[DH_B:2 * DH_B, qw:2 * qw]], axis=0)
        o_ref[g * qw:(g + 1) * qw, :] = o.T.astype(o_ref.dtype)

    same_head = (lax.broadcasted_iota(jnp.int32, (LANES, LANES), 0) // DH_B
                 == lax.broadcasted_iota(jnp.int32, (LANES, LANES), 1) // DH_B).astype(BF16)
    kf = k_all.astype(F32)
    kn2 = _dot((kf * kf).astype(BF16), same_head)
    kmax2 = jnp.max(jnp.max(kn2.reshape(kn2.shape[0] // sub, sub, LANES), axis=0), axis=0, keepdims=True)
    shifts, worst = [], None
    for m in range(2):
        qn2 = jnp.sum(qt[m] * qt[m], axis=0, keepdims=True)
        bound = jnp.sqrt(qn2 * kmax2[:, m * DH_B:m * DH_B + 1]) * 1.03
        shifts.append(bound + bstat_ref[m, 0:1, 0:1])
        spread = jnp.max(2.0 * bound + bstat_ref[m, 1:2, 0:1])
        worst = spread if worst is None else jnp.maximum(worst, spread)
    fixed_shift = worst < SINGLE_PASS_LOG2_RANGE

    @pl.when(fixed_shift)
    def _():
        sums = []
        for g in range(ngroups):
            k0, qs, bias = group_operands(g)
            r = jnp.concatenate([shifts[0][:, k0:k0 + qw], shifts[1][:, k0:k0 + qw]], axis=1)
            s = _dot(k_all[k0:k0 + nk], qs) + (bias - r)
            p = jnp.exp2(s).reshape(nk // sub, sub, 2 * qw)
            sums.append(jnp.sum(jnp.sum(p, axis=0), axis=0, keepdims=True))
            p_refs[g][:, 0:2 * qw] = p.reshape(nk, 2 * qw).astype(BF16)
        for g in range(ngroups):
            k0 = g * qw
            store_group(g, _dot(vt_all[:, k0:k0 + nk], p_refs[g][:, 0:2 * qw]) / sums[g])

    @pl.when(jnp.logical_not(fixed_shift))
    def _():
        vt_ones = jnp.concatenate([vt_all, jnp.ones((ONES_A, vt_all.shape[1]), BF16)], axis=0)
        for g in range(ngroups):
            k0, qs, _ = group_operands(g)
            s_refs[g][:, 0:2 * qw] = _dot(k_all[k0:k0 + nk], qs)
        for g in range(ngroups):
            k0, _, bias = group_operands(g)
            s = (s_refs[g][:, 0:2 * qw] + bias).reshape(nk // sub, sub, 2 * qw)
            m = jnp.max(jnp.max(s, axis=0), axis=0, keepdims=True)
            p = jnp.exp2(s - m[None])
            pv = _dot(vt_ones[:, k0:k0 + nk], p.reshape(nk, 2 * qw).astype(BF16))
            store_group(g, pv[0:2 * DH_B] / pv[2 * DH_B:2 * DH_B + 1])


def _band_bias_tiles(rel_bias):
    band = BAND_B

    def valid(r, c, variant):
        qchunk = jnp.floor_divide(c, CHUNK)
        kchunk = jnp.floor_divide(r - band, CHUNK)
        missing = jnp.where(variant == 0, 0, band - (variant - 1) * QW_B)
        return (kchunk <= qchunk) & (kchunk >= qchunk - LEFT_CHUNKS) & (r >= missing)

    return _toeplitz_tiles(
        lambda x: rel_bias.astype(F32)[:, jnp.clip(-x - band, -REL_CLIP, REL_CLIP) + REL_CLIP] * LOG2E, valid,
        N_HEADS_B, band + QW_B, QW_B, variants=1 + band // QW_B)


def _band_attention(proj, rel_bias):
    seq = proj.shape[0]
    blk, band, qw = BLK_B, BAND_B, QW_B
    bias = _band_bias_tiles(rel_bias)
    npair = N_HEADS_B // 2
    qc0 = 3 * N_HEADS_A
    per = blk // band
    prev = lambda c0: (lambda hp, i: (jnp.maximum(i * per - 1, 0), c0 + hp))
    cur = lambda c0: (lambda hp, i: (i, c0 + hp))
    return pl.pallas_call(
        _band_kernel,
        grid=(npair, seq // blk),
        in_specs=[
            pl.BlockSpec((blk, LANES), cur(qc0)),
            pl.BlockSpec((band, LANES), prev(qc0 + npair)),
            pl.BlockSpec((blk, LANES), cur(qc0 + npair)),
            pl.BlockSpec((band, LANES), prev(qc0 + 2 * npair)),
            pl.BlockSpec((blk, LANES), cur(qc0 + 2 * npair)),
        ] + [
            pl.BlockSpec((1, 2, band + qw, qw), (lambda hp, i, t=t: (jnp.where(i == 0, 1 + t, 0), hp, 0, 0)))
            for t in range(band // qw)
        ] + [
            pl.BlockSpec((1, 2, band + qw, qw), lambda hp, i: (0, hp, 0, 0)),
            pl.BlockSpec((2, 2, LANES), lambda hp, i: (hp, 0, 0)),
        ],
        out_specs=pl.BlockSpec((blk, LANES), lambda hp, i: (i, hp)),
        out_shape=jax.ShapeDtypeStruct((seq, N_HEADS_B * DH_B), BF16),
        scratch_shapes=([pltpu.VMEM((band + qw, 2 * qw + SCORE_PAD), F32)] * (blk // qw)
                        + [pltpu.VMEM((band + qw, 2 * qw + SCORE_PAD), BF16)] * (blk // qw)),
        compiler_params=pltpu.CompilerParams(dimension_semantics=("parallel", "arbitrary")),
        name="band_attention",
    )(proj, proj, proj, proj, proj, *([bias] * bias.shape[0]), _bias_stats(bias[0]))


def _retention_kernel(qk_ref, v_ref, gate_ref, cos_ref, sin_ref, qdec_ref, kdec_ref, dmat_ref,
                      sdec_ref, o_ref, state_ref):
    @pl.when(pl.program_id(0) == 0)
    def _():
        state_ref[...] = jnp.zeros(state_ref.shape, F32)

    cos = cos_ref[...]
    sin = sin_ref[...]
    lane = lax.broadcasted_iota(jnp.int32, cos.shape, 1)
    first_half = (lane % DQK_C) < (DQK_C // 2)
    qk = qk_ref[...]
    parts = []
    for j in range(qk.shape[1] // LANES):
        t = qk[:, j * LANES:(j + 1) * LANES]
        partner = jnp.where(first_half, pltpu.roll(t, LANES - DQK_C // 2, 1), pltpu.roll(t, DQK_C // 2, 1))
        parts.append(t * cos + partner * sin)
    wq = N_HEADS_C * DQK_C
    q = jnp.concatenate(parts[:wq // LANES], axis=1)
    k = jnp.concatenate(parts[wq // LANES:], axis=1) * (DQK_C ** -0.5)
    qd = (q * qdec_ref[...]).astype(BF16)
    kd = (k * kdec_ref[...]).astype(BF16)
    qb = q.astype(BF16)
    kb = k.astype(BF16)
    vb = v_ref[...].astype(BF16)
    gate = gate_ref[...]
    outs = []
    for h in range(N_HEADS_C):
        qs = slice(h * DQK_C, (h + 1) * DQK_C)
        vs = slice(h * DV_C, (h + 1) * DV_C)
        scores = _dot_nt(qb[:, qs], kb[:, qs]) * dmat_ref[h]
        state = state_ref[h]
        r = _dot(scores.astype(BF16), vb[:, vs]) + _dot(qd[:, qs], state.astype(BF16))
        state_ref[h] = state * sdec_ref[h] + _dot_tn(kd[:, qs], vb[:, vs])
        r = r * lax.rsqrt(jnp.mean(r * r, axis=-1, keepdims=True) + EPS)
        g = gate[:, vs]
        outs.append(r * (g * jax.nn.sigmoid(g)))
    o_ref[...] = jnp.concatenate(outs, axis=1).astype(o_ref.dtype)


def _retention_tables(seq):
    t = BLK_C
    half = DQK_C // 2
    inv_freq = 1.0 / np.power(ROPE_BASE, np.arange(0, DQK_C, 2, dtype=np.float64) / DQK_C)
    ang = np.arange(seq, dtype=np.float64)[:, None] * inv_freq[None, :]
    reps = LANES // half
    cos = np.tile(np.cos(ang), (1, reps))
    sign = np.where((np.arange(LANES) % DQK_C) < half, -1.0, 1.0)
    sin = np.tile(np.sin(ang), (1, reps)) * sign[None, :]
    log_g = np.log(1.0 - np.power(2.0, -5.0 - np.arange(N_HEADS_C, dtype=np.float64)))
    pos = np.arange(t, dtype=np.float64)
    diff = pos[:, None] - pos[None, :]
    same_or_past = (np.arange(t)[None, :] // CHUNK) <= (np.arange(t)[:, None] // CHUNK)
    dmat = np.where(same_or_past[None], np.exp(log_g[:, None, None] * np.abs(diff)[None]), 0.0)
    qdec = np.repeat(np.exp(log_g[None, :] * (pos[:, None] + 1.0)), DQK_C, axis=1)
    kdec = np.repeat(np.exp(log_g[None, :] * (t - 1.0 - pos[:, None])), DQK_C, axis=1)
    sdec = np.broadcast_to(np.exp(log_g * t)[:, None, None], (N_HEADS_C, 1, DV_C))
    return tuple(jnp.asarray(a.astype(np.float32)) for a in (cos, sin, qdec, kdec, dmat, sdec))


def _retention(proj):
    seq = proj.shape[0]
    t = BLK_C
    cos, sin, qdec, kdec, dmat, sdec = _retention_tables(seq)
    wv = N_HEADS_C * DV_C
    return pl.pallas_call(
        _retention_kernel,
        grid=(seq // t,),
        in_specs=[
            pl.BlockSpec((t, wv), lambda i: (i, 0)),
            pl.BlockSpec((t, wv), lambda i: (i, 1)),
            pl.BlockSpec((t, wv), lambda i: (i, 2)),
            pl.BlockSpec((t, LANES), lambda i: (i, 0)),
            pl.BlockSpec((t, LANES), lambda i: (i, 0)),
            pl.BlockSpec((t, N_HEADS_C * DQK_C), lambda i: (0, 0)),
            pl.BlockSpec((t, N_HEADS_C * DQK_C), lambda i: (0, 0)),
            pl.BlockSpec((N_HEADS_C, t, t), lambda i: (0, 0, 0)),
            pl.BlockSpec((N_HEADS_C, 1, DV_C), lambda i: (0, 0, 0)),
        ],
        out_specs=pl.BlockSpec((t, wv), lambda i: (i, 0)),
        out_shape=jax.ShapeDtypeStruct((seq, wv), BF16),
        scratch_shapes=[pltpu.VMEM((N_HEADS_C, DQK_C, DV_C), F32)],
        compiler_params=pltpu.CompilerParams(dimension_semantics=("arbitrary",)),
        name="retention",
    )(proj, proj, proj, cos, sin, qdec, kdec, dmat, sdec)


def _s5_kernel(*refs):
    ncb = S5_CH // LANES
    u_refs = refs[:ncb]
    (mt_ref, bt_ref, ctr_ref, cti_ref, are_ref, aim_ref, y_ref,
     ut_ref, yt_ref, ys_ref, vr_ref, vi_ref, spr_ref, spi_ref, carry_ref) = refs[ncb:]
    tc = S5_TC
    gp = S5_GROUP
    n = S5_STATE
    ng = S5_GROUPS

    @pl.when(pl.program_id(0) == 0)
    def _():
        carry_ref[...] = jnp.zeros(carry_ref.shape, F32)

    for s in range(S5_T):
        for k in range(ncb):
            ut_ref[s, k * LANES:(k + 1) * LANES, :] = u_refs[k][pl.ds(s, tc, stride=S5_T), :].T

    unroll = 4

    def intra(it, carry):
        for k in range(unroll):
            g = it * unroll + k
            r0 = pl.multiple_of(g * gp, gp)
            ug = ut_ref[:, pl.ds(r0, gp), :].reshape(S5_T * gp, tc).astype(BF16)
            yt_ref[:, pl.ds(r0, gp), :] = _dot(mt_ref[g], ug).reshape(S5_T, gp, tc)
            vt = _dot(bt_ref[g], ug)
            n0 = pl.multiple_of(g * n, n)
            vr_ref[pl.ds(n0, n), :] = vt[0:n]
            vi_ref[pl.ds(n0, n), :] = vt[n:2 * n]
        return carry

    lax.fori_loop(0, ng // unroll, intra, 0)

    sub = SUBLANES
    nv = tc // sub
    row = lax.broadcasted_iota(jnp.int32, (tc, LANES), 0)
    in_vreg = lax.rem(row, sub)

    def rows_of(v, r):
        return jnp.broadcast_to(v[r:r + 1], (tc, LANES))

    for j in range(ng * n // LANES):
        cols = slice(j * LANES, (j + 1) * LANES)
        pwr, pwi = are_ref[:, cols], aim_ref[:, cols]
        xr = vr_ref[cols, :].T
        xi = vi_ref[cols, :].T
        for d in (1, 2, 4):
            keep = in_vreg >= d
            sr = jnp.where(keep, pltpu.roll(xr, d, 0), 0.0)
            si = jnp.where(keep, pltpu.roll(xi, d, 0), 0.0)
            fr, fi = rows_of(pwr, d - 1), rows_of(pwi, d - 1)
            xr, xi = xr + (fr * sr - fi * si), xi + (fr * si + fi * sr)
        cr, ci = carry_ref[0, :, cols], carry_ref[1, :, cols]
        cr0, ci0 = cr, ci
        outr, outi = [], []
        for v in range(nv):
            yr = xr[v * sub:(v + 1) * sub] + (pwr * cr - pwi * ci)
            yi = xi[v * sub:(v + 1) * sub] + (pwr * ci + pwi * cr)
            outr.append(yr)
            outi.append(yi)
            cr = jnp.broadcast_to(yr[sub - 1:sub], (sub, LANES))
            ci = jnp.broadcast_to(yi[sub - 1:sub], (sub, LANES))
        carry_ref[0, :, cols] = cr
        carry_ref[1, :, cols] = ci
        sr = jnp.concatenate(outr, axis=0)
        si = jnp.concatenate(outi, axis=0)
        first = row == 0
        spr_ref[j] = jnp.where(first, rows_of(cr0, 0), pltpu.roll(sr, 1, 0))
        spi_ref[j] = jnp.where(first, rows_of(ci0, 0), pltpu.roll(si, 1, 0))

    def cross(it, carry):
        for k in range(unroll):
            jp = it * unroll + k
            r0 = pl.multiple_of(jp * 2 * gp, 2 * gp)
            yc = (_dot_nt(ctr_ref[jp], spr_ref[jp].astype(BF16))
                  + _dot_nt(cti_ref[jp], spi_ref[jp].astype(BF16)))
            yt_ref[:, pl.ds(r0, 2 * gp), :] += yc.reshape(S5_T, 2 * gp, tc)
        return carry

    lax.fori_loop(0, ng // 2 // unroll, cross, 0)

    for s in range(S5_T):
        for k in range(ncb):
            ys_ref[k, pl.ds(s, tc, stride=S5_T), :] = yt_ref[s, k * LANES:(k + 1) * LANES, :].T
    for k in range(ncb):
        y_ref[:, k * LANES:(k + 1) * LANES] = ys_ref[k]


def _s5_matrices(lam_re, lam_im, log_step, b_re, b_im, c_re, c_im, d_skip):
    hi = lax.Precision.HIGHEST
    t, gp, n, ng = S5_T, S5_GROUP, S5_STATE, S5_GROUPS
    lam = lax.complex(lam_re.astype(F32), lam_im.astype(F32))
    step = jnp.exp(log_step.astype(F32))[:, None]
    ls = lam * step
    a_bar = jnp.exp(ls)
    b_bar = ((a_bar - 1.0) / lam)[..., None] * lax.complex(b_re.astype(F32), b_im.astype(F32))
    cm = lax.complex(c_re.astype(F32), c_im.astype(F32))

    def apow(k):
        kk = k.astype(F32).astype(jnp.complex64)
        return jnp.exp(ls.reshape((ng,) + (1,) * k.ndim + (n,)) * kk[None, ..., None])

    tt = jnp.arange(t)
    kmat = jnp.einsum('gpn,gln,gnq->glpq', cm, apow(tt), b_bar, precision=hi).real
    krev = jnp.transpose(kmat[:, ::-1], (0, 2, 1, 3)).reshape(ng, gp, t * gp)
    kpad = jnp.pad(krev, ((0, 0), (0, 0), (0, t * gp)))
    mt = jnp.concatenate([kpad[:, :, (t - 1 - to) * gp:(2 * t - 1 - to) * gp] for to in range(t)], axis=1)
    dvec = jnp.tile(d_skip.astype(F32).reshape(ng, 1, gp), (1, t, 1)).reshape(ng, t * gp)
    mt = mt + jnp.eye(t * gp, dtype=F32)[None] * dvec[:, :, None]
    z = jnp.swapaxes(apow(t - 1 - tt), 1, 2)[:, :, :, None] * b_bar[:, :, None, :]
    z = z.reshape(ng, n, t * gp)
    bt = jnp.concatenate([z.real, z.imag], axis=1)
    w = cm[:, None, :, :] * apow(tt + 1)[:, :, None, :]

    def pair_readout(x):
        x = x.reshape(ng // 2, 2, t, gp, n)
        first = jnp.pad(x[:, 0], ((0, 0), (0, 0), (0, 0), (0, n)))
        second = jnp.pad(x[:, 1], ((0, 0), (0, 0), (0, 0), (n, 0)))
        return jnp.stack([first, second], axis=2).reshape(ng // 2, t * 2 * gp, 2 * n).astype(BF16)

    ctr, cti = pair_readout(w.real), pair_readout(-w.imag)
    a_chunk = jnp.transpose(apow(t * (jnp.arange(SUBLANES) + 1)), (1, 0, 2)).reshape(SUBLANES, ng * n)
    return mt.astype(BF16), bt.astype(BF16), ctr, cti, a_chunk.real, a_chunk.imag


def _s5(proj, mats):
    seq, width = proj.shape
    t, tc, gp, n, ng = S5_T, S5_TC, S5_GROUP, S5_STATE, S5_GROUPS
    rows = t * tc
    ncb = S5_CH // LANES
    cb0 = (width - S5_CH) // LANES
    u_specs = [pl.BlockSpec((rows, LANES), (lambda i, k=k: (i, cb0 + k))) for k in range(ncb)]
    nsb = ng * n // LANES
    return pl.pallas_call(
        _s5_kernel,
        grid=(seq // rows,),
        in_specs=u_specs + [_const_spec(m.shape) for m in mats],
        out_specs=pl.BlockSpec((rows, S5_CH), lambda i: (i, 0)),
        out_shape=jax.ShapeDtypeStruct((seq, S5_CH), F32),
        scratch_shapes=[
            pltpu.VMEM((t, S5_CH, tc), F32),
            pltpu.VMEM((t, S5_CH, tc), F32),
            pltpu.VMEM((ncb, rows, LANES), F32),
            pltpu.VMEM((ng * n, tc), F32),
            pltpu.VMEM((ng * n, tc), F32),
            pltpu.VMEM((nsb, tc, LANES), F32),
            pltpu.VMEM((nsb, tc, LANES), F32),
            pltpu.VMEM((2, SUBLANES, ng * n), F32),
        ],
        compiler_params=pltpu.CompilerParams(dimension_semantics=("arbitrary",)),
        name="s5_scan",
    )(*([proj] * ncb), *mats)


def _mix_ffn_kernel(*refs, glu, final):
    (x_ref, a_ref, b_ref, wo_ref, g1_ref), refs = refs[:5], refs[5:]
    if glu:
        gw_ref, refs = refs[0], refs[1:]
    (g_ref, sc_ref, sh_ref, gate_ref, win_ref, cw_ref, cb_ref, wout_ref), refs = refs[:8], refs[8:]
    if final:
        fg_ref, o_ref, h_ref, act_ref, gbuf_ref, carry_ref = refs
    else:
        ng_ref, nsc_ref, nsh_ref, o_ref, hn_ref, h_ref, act_ref, gbuf_ref, carry_ref = refs
    tm = x_ref.shape[0]
    halo = gbuf_ref.shape[0] - tm

    @pl.when(pl.program_id(0) == 0)
    def _():
        carry_ref[...] = jnp.zeros(carry_ref.shape, F32)

    if glu:
        y = jax.nn.gelu(b_ref[...]).astype(BF16)
        gg = _dot(y, gw_ref[...])
        half = gg.shape[1] // 2
        b = (gg[:, :half] * jax.nn.sigmoid(gg[:, half:])).astype(BF16)
    else:
        b = b_ref[...]
    cat = jnp.concatenate([a_ref[...], b], axis=1)
    x = x_ref[...] + g1_ref[...] * _dot(cat, wo_ref[...])
    h_ref[...] = _mod_rmsnorm(x, g_ref[...], sc_ref[...], sh_ref[...]).astype(BF16)
    for f in range(D_FF // TF_FFN):
        cs = slice(f * TF_FFN, (f + 1) * TF_FFN)
        gs = slice(D_FF + f * TF_FFN, D_FF + (f + 1) * TF_FFN)
        h = h_ref[...]
        val = _dot(h, win_ref[:, cs])
        gate = _dot(h, win_ref[:, gs])
        gbuf_ref[0:halo, :] = carry_ref[:, cs]
        gbuf_ref[halo:halo + tm, :] = gate
        carry_ref[:, cs] = gate[tm - halo:tm, :]
        conv = (gate * cw_ref[2:3, cs] + gbuf_ref[halo - 1:halo - 1 + tm, :] * cw_ref[1:2, cs]
                + gbuf_ref[halo - 2:halo - 2 + tm, :] * cw_ref[0:1, cs] + cb_ref[:, cs])
        act_ref[:, cs] = (jax.nn.gelu(conv) * val).astype(BF16)
    xn = x + gate_ref[...] * _dot(act_ref[...], wout_ref[...])
    if final:
        xn = xn * lax.rsqrt(jnp.mean(xn * xn, axis=-1, keepdims=True) + EPS) * fg_ref[...]
    else:
        hn_ref[...] = _mod_rmsnorm(xn, ng_ref[...], nsc_ref[...], nsh_ref[...]).astype(BF16)
    o_ref[...] = xn


def _layer_spec(shape, layer):
    idx = (layer,) + (0,) * (len(shape) - 1)
    return pl.BlockSpec((None,) + tuple(shape[1:]), lambda *_: idx, pipeline_mode=pl.Buffered(1))


def _mix_ffn(x, a, b, wo, gate1, glu_w, g, scale, shift, gate2, w_in, conv_w, conv_b, w_out, tail, layer):
    seq, d = x.shape
    final = len(tail) == 1
    tm = TM_FFN
    halo = SUBLANES
    row = pl.BlockSpec((1, d), lambda i: (0, 0))
    rows = lambda w: pl.BlockSpec((tm, w), lambda i: (i, 0))
    conv_b = conv_b.reshape(conv_b.shape[0], 1, D_FF)
    in_specs = [rows(d), rows(a.shape[1]), rows(b.shape[1]), _const_spec(wo.shape), row]
    args = [x, a, b, wo, gate1]
    if glu_w is not None:
        in_specs.append(_const_spec(glu_w.shape))
        args.append(glu_w)
    in_specs += [
        row, row, row, row,
        _layer_spec(w_in.shape, layer),
        _layer_spec(conv_w.shape, layer),
        _layer_spec(conv_b.shape, layer),
        _layer_spec(w_out.shape, layer),
    ] + [row] * len(tail)
    args += [g.reshape(1, d), scale, shift, gate2, w_in, conv_w, conv_b, w_out]
    args += [t.reshape(1, d) for t in tail]
    out_specs = [rows(d)] if final else [rows(d), rows(d)]
    out_shape = [jax.ShapeDtypeStruct((seq, d), F32)] + ([] if final else [jax.ShapeDtypeStruct((seq, d), BF16)])
    return pl.pallas_call(
        functools.partial(_mix_ffn_kernel, glu=glu_w is not None, final=final),
        grid=(seq // tm,),
        in_specs=in_specs,
        out_specs=out_specs,
        out_shape=out_shape,
        scratch_shapes=[
            pltpu.VMEM((tm, d), BF16),
            pltpu.VMEM((tm, D_FF), BF16),
            pltpu.VMEM((tm + halo, TF_FFN), F32),
            pltpu.VMEM((halo, D_FF), F32),
        ],
        compiler_params=pltpu.CompilerParams(dimension_semantics=("arbitrary",)),
        name="mix_ffn",
    )(*args)


def kernel(x, c, t5_table, mod_w, mod_b, norm1_g, norm2_g, ffn_w_in, ffn_conv_w, ffn_conv_b, ffn_w_out,
           ev_w_in, ev_w_out, diff_lambda, diff_subln_g, band_rel_bias,
           od_w_in, od_w_out, s5_lam_re, s5_lam_im, s5_log_step, s5_b_re, s5_b_im, s5_c_re, s5_c_im,
           s5_d, s5_glu_w, final_g):
    assert x.shape[0] == 1 and x.shape[2] == D_MODEL
    seq = x.shape[1]
    assert seq % TM_PROJ == 0 and seq % (S5_T * S5_TC) == 0
    d = D_MODEL
    xs = x[0]
    mod = _modulation(c, mod_w, mod_b)
    ffn_w_in_b = ffn_w_in.astype(BF16)
    ffn_w_out_b = ffn_w_out.astype(BF16)
    mods = [[mod[i, :, k * d:(k + 1) * d] for k in range(6)] for i in range(DEPTH)]
    h = None
    for i in range(DEPTH):
        sh1, sc1, g1, sh2, sc2, g2 = mods[i]
        w_in = (ev_w_in if i % 2 == 0 else od_w_in)[i // 2].astype(BF16)
        proj_dtype = BF16 if i % 2 == 0 else F32
        if h is None:
            proj = _normproj(xs, norm1_g[i], sc1, sh1, w_in, proj_dtype)
        else:
            proj = _proj(h, w_in, proj_dtype)
        if i % 2 == 0:
            e = i // 2
            lam_init = 0.8 - 0.6 * math.exp(-0.3 * i)
            lp = diff_lambda[e].astype(F32)
            lam = jnp.exp(jnp.sum(lp[0] * lp[1])) - jnp.exp(jnp.sum(lp[2] * lp[3])) + lam_init
            mix_a = _diff_attention(proj, t5_table, lam, diff_subln_g[e], lam_init)
            mix_b = _band_attention(proj, band_rel_bias[e])
            wo, glu_w = ev_w_out[e].astype(BF16), None
        else:
            o = i // 2
            mix_a = _retention(proj)
            mats = _s5_matrices(s5_lam_re[o], s5_lam_im[o], s5_log_step[o], s5_b_re[o], s5_b_im[o],
                                s5_c_re[o], s5_c_im[o], s5_d[o])
            mix_b = _s5(proj, mats)
            wo, glu_w = od_w_out[o].astype(BF16), s5_glu_w[o].astype(BF16)
        if i == DEPTH - 1:
            tail = (final_g,)
        else:
            nsh1, nsc1 = mods[i + 1][0], mods[i + 1][1]
            tail = (norm1_g[i + 1], nsc1, nsh1)
        out = _mix_ffn(xs, mix_a, mix_b, wo, g1, glu_w, norm2_g[i], sc2, sh2, g2,
                       ffn_w_in_b, ffn_conv_w, ffn_conv_b, ffn_w_out_b, tail, layer=i)
        if i == DEPTH - 1:
            xs = out[0]
        else:
            xs, h = out
    return xs[None]
```

```python
import functools
import math

import jax
import jax.numpy as jnp
import numpy as np
from jax import lax
from jax.experimental import pallas as pl
from jax.experimental.pallas import tpu as pltpu

F32 = jnp.float32
BF16 = jnp.bfloat16

D_MODEL = 1024
DEPTH = 2
CHUNK = 64
GROUP_WIDTH = D_MODEL // 2
DK_A = 64
DV_A = 2 * DK_A
N_HEADS_A = GROUP_WIDTH // DV_A
DH_B = 64
N_HEADS_B = GROUP_WIDTH // DH_B
LEFT_CHUNKS = 8
REL_CLIP = 2 * CHUNK
NUM_BUCKETS = 32
MAX_DISTANCE = 128
DV_C = 128
DQK_C = DV_C // 2
N_HEADS_C = GROUP_WIDTH // DV_C
ROPE_BASE = 10000.0
S5_CH = GROUP_WIDTH
S5_GROUP = 16
S5_GROUPS = S5_CH // S5_GROUP
S5_STATE = 64
D_FF = ((8 * D_MODEL // 3 + 255) // 256) * 256
CONV_W = 3
EVEN_IN = 3 * N_HEADS_A * DV_A + 3 * N_HEADS_B * DH_B
ODD_IN = 2 * N_HEADS_C * DQK_C + 2 * N_HEADS_C * DV_C + S5_CH
EPS = 1e-6
NEG_INF = -1e30
LOG2E = math.log2(math.e)

LANES = 128
SUBLANES = 8
MXU_DIM = 256

TM_PROJ = 1024
TN_PROJ = 1024
TN_MOD = 1536
TM_FFN = 512
TF_FFN = MXU_DIM
BLK_A = 512
NPART_A = 2
ONES_A = 16
SINGLE_PASS_LOG2_RANGE = 96.0
SCORE_PAD = LANES
BLK_B = 2048
BAND_B = LEFT_CHUNKS * CHUNK
QW_B = 4 * CHUNK
BLK_C = 512
S5_T = 16
S5_TC = LANES

assert BLK_B % BAND_B == 0 and BLK_B % QW_B == 0 and BAND_B % QW_B == 0
assert BLK_A >= MAX_DISTANCE, "far key blocks must sit in the saturated T5 bucket"
assert DV_A == LANES and 2 * DK_A == LANES and 2 * DH_B == LANES, "attention heads are read as 128-lane column blocks"


def _dot(a, b):
    return jnp.dot(a, b, preferred_element_type=F32)


def _dot_nt(a, b):
    return lax.dot_general(a, b, (((1,), (1,)), ((), ())), preferred_element_type=F32)


def _dot_tn(a, b):
    return lax.dot_general(a, b, (((0,), (0,)), ((), ())), preferred_element_type=F32)


def _const_spec(shape):
    zeros = (0,) * len(shape)
    return pl.BlockSpec(shape, lambda *_: zeros, pipeline_mode=pl.Buffered(1))


def _mod_rmsnorm(x, g, scale, shift):
    y = x * lax.rsqrt(jnp.mean(x * x, axis=-1, keepdims=True) + EPS)
    y = y * g
    return y * (1.0 + scale) + shift


def _mod_kernel(c_ref, w_ref, b_ref, o_ref):
    c = c_ref[...]
    cond = c * jax.nn.sigmoid(c)
    o_ref[0] = jnp.sum(cond * w_ref[0], axis=0, keepdims=True) + b_ref[0]


def _modulation(c, mod_w, mod_b):
    depth, d, n = mod_w.shape
    tn = TN_MOD
    return pl.pallas_call(
        _mod_kernel,
        grid=(depth, n // tn),
        in_specs=[
            pl.BlockSpec((d, 1), lambda i, j: (0, 0)),
            pl.BlockSpec((1, d, tn), lambda i, j: (i, 0, j)),
            pl.BlockSpec((1, 1, tn), lambda i, j: (i, 0, j)),
        ],
        out_specs=pl.BlockSpec((1, 1, tn), lambda i, j: (i, 0, j)),
        out_shape=jax.ShapeDtypeStruct((depth, 1, n), F32),
        name="modulation",
    )(c.reshape(d, 1), mod_w, mod_b.reshape(depth, 1, n))


def _normproj_kernel(x_ref, g_ref, sc_ref, sh_ref, w_ref, o_ref):
    tm, n = o_ref.shape
    half = tm // 2
    for r in range(2):
        rows = slice(r * half, (r + 1) * half)
        h = _mod_rmsnorm(x_ref[rows, :], g_ref[...], sc_ref[...], sh_ref[...]).astype(BF16)
        for j in range(n // TN_PROJ):
            cols = slice(j * TN_PROJ, (j + 1) * TN_PROJ)
            o_ref[rows, cols] = _dot(h, w_ref[:, cols]).astype(o_ref.dtype)


def _normproj(x, g, scale, shift, w, out_dtype):
    seq, d = x.shape
    n = w.shape[1]
    tm = TM_PROJ
    row = pl.BlockSpec((1, d), lambda i: (0, 0))
    return pl.pallas_call(
        _normproj_kernel,
        grid=(seq // tm,),
        in_specs=[pl.BlockSpec((tm, d), lambda i: (i, 0)), row, row, row, _const_spec(w.shape)],
        out_specs=pl.BlockSpec((tm, n), lambda i: (i, 0)),
        out_shape=jax.ShapeDtypeStruct((seq, n), out_dtype),
        compiler_params=pltpu.CompilerParams(dimension_semantics=("parallel",)),
        name="normproj",
    )(x, g.reshape(1, d), scale, shift, w)


def _proj_kernel(h_ref, w_ref, o_ref):
    for j in range(o_ref.shape[1] // TN_PROJ):
        cols = slice(j * TN_PROJ, (j + 1) * TN_PROJ)
        o_ref[:, cols] = _dot(h_ref[...], w_ref[:, cols]).astype(o_ref.dtype)


def _proj(h, w, out_dtype):
    seq, d = h.shape
    n = w.shape[1]
    tm = TM_PROJ
    return pl.pallas_call(
        _proj_kernel,
        grid=(seq // tm,),
        in_specs=[pl.BlockSpec((tm, d), lambda i: (i, 0)), _const_spec(w.shape)],
        out_specs=pl.BlockSpec((tm, n), lambda i: (i, 0)),
        out_shape=jax.ShapeDtypeStruct((seq, n), out_dtype),
        compiler_params=pltpu.CompilerParams(dimension_semantics=("parallel",)),
        name="proj",
    )(h, w)


def _diffattn_kernel(qall_ref, k_ref, v_ref, bias_ref, bstat_ref, lam_ref, g_ref, o_ref,
                     flag_ref, qs_ref, vt_ref, kmax_ref, r_ref, m_ref, acc_ref, *s_refs, out_scale):
    blk = BLK_A
    nq = 2 * blk
    sub = SUBLANES
    dv = DV_A
    npart = NPART_A
    sa_ref, sb_ref = s_refs[:2 * npart], s_refs[2 * npart:4 * npart]
    pa_ref, pb_ref = s_refs[4 * npart:5 * npart], s_refs[5 * npart:6 * npart]
    i = pl.program_id(1)
    lane = lax.broadcasted_iota(jnp.int32, (blk, LANES), 1)
    same_subhead = (lax.broadcasted_iota(jnp.int32, (LANES, LANES), 0) // DK_A
                    == lax.broadcasted_iota(jnp.int32, (LANES, LANES), 1) // DK_A).astype(BF16)

    bias_max, bias_span = bstat_ref[0, 0:1, 0:1], bstat_ref[0, 1:2, 0:1]
    q_scale = DK_A ** -0.5 * LOG2E

    @pl.when(i == 0)
    def _():
        kmax_ref[...] = jnp.zeros(kmax_ref.shape, F32)

        def tr(b, qmax):
            r0 = pl.multiple_of(b * blk, blk)
            c0 = pl.multiple_of(b * nq, nq)
            vt_ref[0:dv, pl.ds(r0, blk)] = v_ref[pl.ds(r0, blk), :].astype(F32).T.astype(BF16)
            vt_ref[dv:dv + ONES_A, pl.ds(r0, blk)] = jnp.ones((ONES_A, blk), BF16)
            kf = k_ref[pl.ds(r0, blk), :].astype(F32)
            kn2 = _dot((kf * kf).astype(BF16), same_subhead)
            kmax_ref[...] = jnp.maximum(kmax_ref[...], jnp.max(kn2.reshape(blk // sub, sub, LANES), axis=0))
            qa = (qall_ref[pl.ds(r0, blk), :].astype(F32) * q_scale).astype(BF16).astype(F32)
            for m in range(2):
                qt = jnp.where((lane < DK_A) if m == 0 else (lane >= DK_A), qa, 0.0).T
                qs_ref[:, pl.ds(c0 + m * blk, blk)] = qt.astype(BF16)
                qn2 = jnp.sum(jnp.sum((qt * qt).reshape(LANES // sub, sub, blk), axis=0), axis=0, keepdims=True)
                qn2 = jnp.broadcast_to(qn2, (sub, blk))
                r_ref[:, pl.ds(c0 + m * blk, blk)] = qn2
                qmax = jnp.maximum(qmax, qn2)
            return qmax
        nblocks = v_ref.shape[0] // blk
        qmax2 = jnp.max(lax.fori_loop(0, nblocks, tr, jnp.zeros((sub, blk), F32)))
        kmax2 = jnp.max(kmax_ref[...], axis=0, keepdims=True)
        worst = 2.0 * jnp.sqrt(qmax2 * kmax2) * 1.03 + bias_span
        flag_ref[0] = (jnp.max(worst) < SINGLE_PASS_LOG2_RANGE).astype(jnp.int32)

        kmax_nq = jnp.concatenate([jnp.broadcast_to(kmax2[:, m * DK_A:m * DK_A + 1], (sub, blk)) for m in range(2)],
                                  axis=1)

        def shifts(b, carry):
            c0 = pl.multiple_of(b * nq, nq)
            r_ref[:, pl.ds(c0, nq)] = jnp.sqrt(r_ref[:, pl.ds(c0, nq)] * kmax_nq) * 1.03 + bias_max
            return carry
        lax.fori_loop(0, nblocks, shifts, 0)

    acc_ref[...] = jnp.zeros(acc_ref.shape, F32)
    q0 = pl.multiple_of(i * nq, nq)
    single_pass = flag_ref[0] == 1

    @pl.when(single_pass)
    def _():
        _diffattn_fixed_shift(i, q0, k_ref, bias_ref, qs_ref, vt_ref, r_ref, m_ref, acc_ref, pa_ref, pb_ref)

    @pl.when(jnp.logical_not(single_pass))
    def _():
        _diffattn_online(i, q0, k_ref, bias_ref, qs_ref, vt_ref, m_ref, acc_ref, sa_ref, sb_ref)

    ot = acc_ref[0:dv, 0:nq] / acc_ref[dv:dv + 1, 0:nq]
    o = ot[:, 0:blk].T - lam_ref[...] * ot[:, blk:nq].T
    o = o * lax.rsqrt(jnp.mean(o * o, axis=-1, keepdims=True) + EPS) * g_ref[...]
    o_ref[...] = (o * out_scale).astype(o_ref.dtype)


def _diffattn_fixed_shift(i, q0, k_ref, bias_ref, qs_ref, vt_ref, shift_ref, l_ref, acc_ref, pa_ref, pb_ref):
    blk = BLK_A
    nq = 2 * blk
    sub = SUBLANES
    npart = len(pa_ref)
    wq = nq // npart
    l_ref[...] = jnp.zeros(l_ref.shape, F32)

    def probs(b, p_ref, bias):
        k = k_ref[pl.ds(pl.multiple_of(b * blk, blk), blk), :]
        for part in range(npart):
            cols = slice(part * wq, (part + 1) * wq)
            qcols = pl.ds(pl.multiple_of(q0 + part * wq, wq), wq)
            s = _dot(k, qs_ref[:, qcols])
            if bias is not None:
                b0 = (part * wq) % blk
                s = s + bias[:, b0:b0 + wq]
            p = jnp.exp2(s.reshape(blk // sub, sub, wq) - shift_ref[:, qcols][None])
            l_ref[:, cols] += jnp.sum(p, axis=0)
            p_ref[part][:, 0:wq] = p.reshape(blk, wq).astype(BF16)

    def accumulate(b, p_ref):
        vt = vt_ref[0:DV_A, pl.ds(pl.multiple_of(b * blk, blk), blk)]
        for part in range(npart):
            cols = slice(part * wq, (part + 1) * wq)
            acc_ref[0:DV_A, cols] += _dot(vt, p_ref[part][:, 0:wq])

    @pl.when(i == 0)
    def _():
        probs(0, pa_ref, bias_ref[0, 1])
        accumulate(0, pa_ref)

    @pl.when(i > 0)
    def _():
        nfar = i - 1
        probs(i, pa_ref, bias_ref[0, 1])
        probs(i - 1, pb_ref, bias_ref[0, 0])
        accumulate(i, pa_ref)

        def pair(t):
            probs(2 * t, pa_ref, None)
            accumulate(jnp.where(t == 0, i - 1, 2 * t - 1), pb_ref)
            probs(2 * t + 1, pb_ref, None)
            accumulate(2 * t, pa_ref)

        def four_pairs(u, carry):
            for v in range(4):
                pair(4 * u + v)
            return carry

        npairs = nfar // 2
        lax.fori_loop(0, npairs // 4, four_pairs, 0)

        def one_pair(t, carry):
            pair(t)
            return carry
        lax.fori_loop(4 * (npairs // 4), npairs, one_pair, 0)
        in_pb = jnp.where(npairs == 0, i - 1, 2 * npairs - 1)

        @pl.when(lax.rem(nfar, 2) == 1)
        def _():
            probs(nfar - 1, pa_ref, None)
            accumulate(in_pb, pb_ref)
            accumulate(nfar - 1, pa_ref)

        @pl.when(lax.rem(nfar, 2) == 0)
        def _():
            accumulate(in_pb, pb_ref)

    acc_ref[DV_A:DV_A + sub, 0:nq] = jnp.broadcast_to(jnp.sum(l_ref[...], axis=0, keepdims=True), (sub, nq))


def _diffattn_online(i, q0, k_ref, bias_ref, qs_ref, vt_ref, m_ref, acc_ref, sa_ref, sb_ref):
    blk = BLK_A
    nq = 2 * blk
    sub = SUBLANES
    npart = len(sa_ref) // 2
    wq = nq // npart
    m_ref[...] = jnp.full(m_ref.shape, NEG_INF, F32)

    def scores(b, s_ref):
        k = k_ref[pl.ds(pl.multiple_of(b * blk, blk), blk), :]
        for part in range(npart):
            s = _dot(k, qs_ref[:, pl.ds(pl.multiple_of(q0 + part * wq, wq), wq)])
            s_ref[part][:, 0:wq] = s
            s_ref[npart + part][...] = jnp.max(s.reshape(blk // sub, sub, wq), axis=0)

    def softmax_pv(b, s_ref, bias):
        vt = vt_ref[:, pl.ds(pl.multiple_of(b * blk, blk), blk)]
        for part in range(npart):
            cols = slice(part * wq, (part + 1) * wq)
            s = s_ref[part][:, 0:wq]
            if bias is not None:
                b0 = (part * wq) % blk
                s = s + bias[:, b0:b0 + wq]
            s = s.reshape(blk // sub, sub, wq)
            m_prev = m_ref[:, cols]
            smax = jnp.max(s, axis=0) if bias is not None else s_ref[npart + part][...]
            m_cur = jnp.max(smax, axis=0, keepdims=True)
            m_new = jnp.maximum(m_prev, m_cur)
            alpha = jnp.exp2(m_prev - m_new)
            p = jnp.exp2(s - m_new[None])
            pv = _dot(vt, p.reshape(blk, wq).astype(BF16))
            acc_ref[:, cols] = acc_ref[:, cols] * alpha[0:1] + pv
            m_ref[:, cols] = m_new

    nfar = jnp.maximum(i - 1, 0)
    odd = lax.rem(nfar, 2)

    @pl.when(i == 0)
    def _():
        scores(0, sb_ref)

    @pl.when(i > 0)
    def _():
        @pl.when(odd == 1)
        def _():
            scores(0, sb_ref)
            scores(1, sa_ref)
            softmax_pv(0, sb_ref, None)

        @pl.when(odd == 0)
        def _():
            scores(0, sa_ref)

        def pair(b):
            scores(b + 1, sb_ref)
            softmax_pv(b, sa_ref, None)
            scores(b + 2, sa_ref)
            softmax_pv(b + 1, sb_ref, None)

        def quad_body(t, carry):
            pair(odd + 4 * t)
            pair(odd + 4 * t + 2)
            return carry

        npairs = nfar // 2
        lax.fori_loop(0, npairs // 2, quad_body, 0)

        @pl.when(lax.rem(npairs, 2) == 1)
        def _():
            pair(odd + 2 * (npairs - 1))
        scores(i, sb_ref)
        softmax_pv(i - 1, sa_ref, bias_ref[0, 0])

    softmax_pv(i, sb_ref, bias_ref[0, 1])


_TOEPLITZ_ROWS = 512


def _toeplitz_kernel(v_ref, o_ref, *, keep):
    rows, cols = o_ref.shape[2:]
    x = jnp.broadcast_to(v_ref[0, 0], (rows, v_ref.shape[-1]))
    tile = pltpu.roll(x, 0, 1, stride=1, stride_axis=0)[:, :cols]
    r = lax.broadcasted_iota(jnp.int32, (rows, cols), 0) + pl.program_id(1) * rows
    c = lax.broadcasted_iota(jnp.int32, (rows, cols), 1)
    for variant in range(o_ref.shape[0]):
        o_ref[variant, 0] = jnp.where(keep(r, c, variant), tile, NEG_INF)


def _toeplitz_tiles(fn, keep, heads, rows, cols, variants=1):
    n = rows + cols
    rb = rows if rows % _TOEPLITZ_ROWS else _TOEPLITZ_ROWS
    assert rows % rb == 0 and n % LANES == 0
    idx = jnp.arange(n, dtype=jnp.int32)
    vec = fn(jnp.where(idx < cols, idx, idx - n)).astype(F32)
    vecs = jnp.stack([jnp.roll(vec, k * rb, axis=1) for k in range(rows // rb)], axis=1)
    return pl.pallas_call(
        functools.partial(_toeplitz_kernel, keep=keep),
        grid=(heads, rows // rb),
        in_specs=[pl.BlockSpec((1, 1, 1, n), lambda h, k: (h, k, 0, 0))],
        out_specs=pl.BlockSpec((variants, 1, rb, cols), lambda h, k: (0, h, k, 0)),
        out_shape=jax.ShapeDtypeStruct((variants, heads, rows, cols), F32),
        name="toeplitz_tiles",
    )(vecs.reshape(heads, rows // rb, 1, n))


def _bias_stats(tiles):
    finite = tiles > 0.5 * NEG_INF
    bias_max = jnp.maximum(jnp.max(jnp.where(finite, tiles, NEG_INF), axis=(1, 2)), 0.0)
    bias_min = jnp.minimum(jnp.min(jnp.where(finite, tiles, -NEG_INF), axis=(1, 2)), 0.0)
    return jnp.broadcast_to(jnp.stack([bias_max, bias_max - bias_min], axis=1)[:, :, None],
                            (tiles.shape[0], 2, LANES))


def _t5_bucket(rel):
    nb = NUM_BUCKETS // 2
    max_exact = nb // 2
    bucket = jnp.where(rel > 0, nb, 0)
    n = jnp.abs(rel)
    nf = jnp.maximum(n, 1).astype(F32)
    large = max_exact + (jnp.log(nf / max_exact) / math.log(MAX_DISTANCE / max_exact)
                         * (nb - max_exact)).astype(jnp.int32)
    large = jnp.minimum(large, nb - 1)
    return bucket + jnp.where(n < max_exact, n, large)


def _diff_bias_tiles(t5_table):
    blk = BLK_A
    table = t5_table.astype(F32)
    far = table[_t5_bucket(jnp.full((), -(blk + 1), jnp.int32))]
    def visible(r, c, variant):
        return jnp.floor_divide(r - blk, CHUNK) <= jnp.floor_divide(c, CHUNK)

    tiles = _toeplitz_tiles(lambda x: ((table[_t5_bucket(-x - blk)] - far) * LOG2E).T, visible,
                            N_HEADS_A, 2 * blk, blk)
    return tiles.reshape(N_HEADS_A, 2, blk, blk)


def _diff_attention(proj, t5_table, lam, subln_g, lam_init):
    seq = proj.shape[0]
    blk = BLK_A
    bias = _diff_bias_tiles(t5_table)
    ha = N_HEADS_A
    bstat = _bias_stats(bias.reshape(ha, 2 * blk, blk))
    kern = functools.partial(_diffattn_kernel, out_scale=1.0 - lam_init)
    return pl.pallas_call(
        kern,
        grid=(ha, seq // blk),
        in_specs=[
            pl.BlockSpec((seq, DV_A), lambda h, i: (0, h)),
            pl.BlockSpec((seq, DV_A), lambda h, i: (0, ha + h)),
            pl.BlockSpec((seq, DV_A), lambda h, i: (0, 2 * ha + h)),
            pl.BlockSpec((1, 2, blk, blk), lambda h, i: (h, 0, 0, 0)),
            pl.BlockSpec((1, 2, LANES), lambda h, i: (h, 0, 0)),
            pl.BlockSpec((1, DV_A), lambda h, i: (0, 0)),
            pl.BlockSpec((1, DV_A), lambda h, i: (0, 0)),
        ],
        out_specs=pl.BlockSpec((blk, DV_A), lambda h, i: (i, h)),
        out_shape=jax.ShapeDtypeStruct((seq, ha * DV_A), BF16),
        scratch_shapes=[
            pltpu.SMEM((1,), jnp.int32),
            pltpu.VMEM((DV_A, 2 * seq), BF16),
            pltpu.VMEM((DV_A + ONES_A, seq), BF16),
            pltpu.VMEM((SUBLANES, LANES), F32),
            pltpu.VMEM((SUBLANES, 2 * seq), F32),
            pltpu.VMEM((SUBLANES, 2 * blk), F32),
            pltpu.VMEM((DV_A + ONES_A, 2 * blk), F32),
        ] + 2 * ([pltpu.VMEM((blk, 2 * blk // NPART_A + SCORE_PAD), F32)] * NPART_A
                 + [pltpu.VMEM((SUBLANES, 2 * blk // NPART_A), F32)] * NPART_A)
        + 2 * [pltpu.VMEM((blk, 2 * blk // NPART_A + SCORE_PAD), BF16)] * NPART_A,
        compiler_params=pltpu.CompilerParams(dimension_semantics=("parallel", "arbitrary")),
        name="diff_attention",
    )(proj, proj, proj, bias, bstat, jnp.full((1, DV_A), lam, F32), subln_g.reshape(1, DV_A).astype(F32))


def _band_kernel(q_ref, kp_ref, kc_ref, vp_ref, vc_ref, *refs):
    qw, band = QW_B, BAND_B
    nbias = band // qw + 1
    bias_refs, bstat_ref, o_ref = refs[:nbias], refs[nbias], refs[nbias + 1]
    ngroups = (len(refs) - nbias - 2) // 2
    s_refs, p_refs = refs[nbias + 2:nbias + 2 + ngroups], refs[nbias + 2 + ngroups:]
    nk = band + qw
    sub = SUBLANES
    q = (q_ref[...].astype(F32) * (DH_B ** -0.5 * LOG2E)).astype(BF16).astype(F32)
    lane = lax.broadcasted_iota(jnp.int32, q.shape, 1)
    qt = (jnp.where(lane < DH_B, q, 0.0).T, jnp.where(lane >= DH_B, q, 0.0).T)
    qh = (qt[0].astype(BF16), qt[1].astype(BF16))
    k_all = jnp.concatenate([kp_ref[...], kc_ref[...]], axis=0)
    vt_all = jnp.concatenate([vp_ref[...], vc_ref[...]], axis=0).astype(F32).T.astype(BF16)

    def group_operands(g):
        k0 = g * qw
        qs = jnp.concatenate([qh[0][:, k0:k0 + qw], qh[1][:, k0:k0 + qw]], axis=1)
        bias_ref = bias_refs[min(g, nbias - 1)]
        bias = jnp.concatenate([bias_ref[0, 0], bias_ref[0, 1]], axis=1)
        return k0, qs, bias

    def store_group(g, ot):
        o = jnp.concatenate([ot[0:DH_B, 0:qw], ot[DH_B:2 * DH_B, qw:2 * qw]], axis=0)
        o_ref[g * qw:(g + 1) * qw, :] = o.T.astype(o_ref.dtype)

    same_head = (lax.broadcasted_iota(jnp.int32, (LANES, LANES), 0) // DH_B
                 == lax.broadcasted_iota(jnp.int32, (LANES, LANES), 1) // DH_B).astype(BF16)
    kf = k_all.astype(F32)
    kn2 = _dot((kf * kf).astype(BF16), same_head)
    kmax2 = jnp.max(jnp.max(kn2.reshape(kn2.shape[0] // sub, sub, LANES), axis=0), axis=0, keepdims=True)
    shifts, worst = [], None
    for m in range(2):
        qn2 = jnp.sum(qt[m] * qt[m], axis=0, keepdims=True)
        bound = jnp.sqrt(qn2 * kmax2[:, m * DH_B:m * DH_B + 1]) * 1.03
        shifts.append(bound + bstat_ref[m, 0:1, 0:1])
        spread = jnp.max(2.0 * bound + bstat_ref[m, 1:2, 0:1])
        worst = spread if worst is None else jnp.maximum(worst, spread)
    fixed_shift = worst < SINGLE_PASS_LOG2_RANGE

    @pl.when(fixed_shift)
    def _():
        sums = []
        for g in range(ngroups):
            k0, qs, bias = group_operands(g)
            r = jnp.concatenate([shifts[0][:, k0:k0 + qw], shifts[1][:, k0:k0 + qw]], axis=1)
            s = _dot(k_all[k0:k0 + nk], qs) + (bias - r)
            p = jnp.exp2(s).reshape(nk // sub, sub, 2 * qw)
            sums.append(jnp.sum(jnp.sum(p, axis=0), axis=0, keepdims=True))
            p_refs[g][:, 0:2 * qw] = p.reshape(nk, 2 * qw).astype(BF16)
        for g in range(ngroups):
            k0 = g * qw
            store_group(g, _dot(vt_all[:, k0:k0 + nk], p_refs[g][:, 0:2 * qw]) / sums[g])

    @pl.when(jnp.logical_not(fixed_shift))
    def _():
        vt_ones = jnp.concatenate([vt_all, jnp.ones((ONES_A, vt_all.shape[1]), BF16)], axis=0)
        for g in range(ngroups):
            k0, qs, _ = group_operands(g)
            s_refs[g][:, 0:2 * qw] = _dot(k_all[k0:k0 + nk], qs)
        for g in range(ngroups):
            k0, _, bias = group_operands(g)
            s = (s_refs[g][:, 0:2 * qw] + bias).reshape(nk // sub, sub, 2 * qw)
            m = jnp.max(jnp.max(s, axis=0), axis=0, keepdims=True)
            p = jnp.exp2(s - m[None])
            pv = _dot(vt_ones[:, k0:k0 + nk], p.reshape(nk, 2 * qw).astype(BF16))
            store_group(g, pv[0:2 * DH_B] / pv[2 * DH_B:2 * DH_B + 1])


def _band_bias_tiles(rel_bias):
    band = BAND_B

    def valid(r, c, variant):
        qchunk = jnp.floor_divide(c, CHUNK)
        kchunk = jnp.floor_divide(r - band, CHUNK)
        missing = jnp.where(variant == 0, 0, band - (variant - 1) * QW_B)
        return (kchunk <= qchunk) & (kchunk >= qchunk - LEFT_CHUNKS) & (r >= missing)

    return _toeplitz_tiles(
        lambda x: rel_bias.astype(F32)[:, jnp.clip(-x - band, -REL_CLIP, REL_CLIP) + REL_CLIP] * LOG2E, valid,
        N_HEADS_B, band + QW_B, QW_B, variants=1 + band // QW_B)


def _band_attention(proj, rel_bias):
    seq = proj.shape[0]
    blk, band, qw = BLK_B, BAND_B, QW_B
    bias = _band_bias_tiles(rel_bias)
    npair = N_HEADS_B // 2
    qc0 = 3 * N_HEADS_A
    per = blk // band
    prev = lambda c0: (lambda hp, i: (jnp.maximum(i * per - 1, 0), c0 + hp))
    cur = lambda c0: (lambda hp, i: (i, c0 + hp))
    return pl.pallas_call(
        _band_kernel,
        grid=(npair, seq // blk),
        in_specs=[
            pl.BlockSpec((blk, LANES), cur(qc0)),
            pl.BlockSpec((band, LANES), prev(qc0 + npair)),
            pl.BlockSpec((blk, LANES), cur(qc0 + npair)),
            pl.BlockSpec((band, LANES), prev(qc0 + 2 * npair)),
            pl.BlockSpec((blk, LANES), cur(qc0 + 2 * npair)),
        ] + [
            pl.BlockSpec((1, 2, band + qw, qw), (lambda hp, i, t=t: (jnp.where(i == 0, 1 + t, 0), hp, 0, 0)))
            for t in range(band // qw)
        ] + [
            pl.BlockSpec((1, 2, band + qw, qw), lambda hp, i: (0, hp, 0, 0)),
            pl.BlockSpec((2, 2, LANES), lambda hp, i: (hp, 0, 0)),
        ],
        out_specs=pl.BlockSpec((blk, LANES), lambda hp, i: (i, hp)),
        out_shape=jax.ShapeDtypeStruct((seq, N_HEADS_B * DH_B), BF16),
        scratch_shapes=([pltpu.VMEM((band + qw, 2 * qw + SCORE_PAD), F32)] * (blk // qw)
                        + [pltpu.VMEM((band + qw, 2 * qw + SCORE_PAD), BF16)] * (blk // qw)),
        compiler_params=pltpu.CompilerParams(dimension_semantics=("parallel", "arbitrary")),
        name="band_attention",
    )(proj, proj, proj, proj, proj, *([bias] * bias.shape[0]), _bias_stats(bias[0]))


def _retention_kernel(qk_ref, v_ref, gate_ref, cos_ref, sin_ref, qdec_ref, kdec_ref, dmat_ref,
                      sdec_ref, o_ref, state_ref):
    @pl.when(pl.program_id(0) == 0)
    def _():
        state_ref[...] = jnp.zeros(state_ref.shape, F32)

    cos = cos_ref[...]
    sin = sin_ref[...]
    lane = lax.broadcasted_iota(jnp.int32, cos.shape, 1)
    first_half = (lane % DQK_C) < (DQK_C // 2)
    qk = qk_ref[...]
    parts = []
    for j in range(qk.shape[1] // LANES):
        t = qk[:, j * LANES:(j + 1) * LANES]
        partner = jnp.where(first_half, pltpu.roll(t, LANES - DQK_C // 2, 1), pltpu.roll(t, DQK_C // 2, 1))
        parts.append(t * cos + partner * sin)
    wq = N_HEADS_C * DQK_C
    q = jnp.concatenate(parts[:wq // LANES], axis=1)
    k = jnp.concatenate(parts[wq // LANES:], axis=1) * (DQK_C ** -0.5)
    qd = (q * qdec_ref[...]).astype(BF16)
    kd = (k * kdec_ref[...]).astype(BF16)
    qb = q.astype(BF16)
    kb = k.astype(BF16)
    vb = v_ref[...].astype(BF16)
    gate = gate_ref[...]
    outs = []
    for h in range(N_HEADS_C):
        qs = slice(h * DQK_C, (h + 1) * DQK_C)
        vs = slice(h * DV_C, (h + 1) * DV_C)
        scores = _dot_nt(qb[:, qs], kb[:, qs]) * dmat_ref[h]
        state = state_ref[h]
        r = _dot(scores.astype(BF16), vb[:, vs]) + _dot(qd[:, qs], state.astype(BF16))
        state_ref[h] = state * sdec_ref[h] + _dot_tn(kd[:, qs], vb[:, vs])
        r = r * lax.rsqrt(jnp.mean(r * r, axis=-1, keepdims=True) + EPS)
        g = gate[:, vs]
        outs.append(r * (g * jax.nn.sigmoid(g)))
    o_ref[...] = jnp.concatenate(outs, axis=1).astype(o_ref.dtype)


def _retention_tables(seq):
    t = BLK_C
    half = DQK_C // 2
    inv_freq = 1.0 / np.power(ROPE_BASE, np.arange(0, DQK_C, 2, dtype=np.float64) / DQK_C)
    ang = np.arange(seq, dtype=np.float64)[:, None] * inv_freq[None, :]
    reps = LANES // half
    cos = np.tile(np.cos(ang), (1, reps))
    sign = np.where((np.arange(LANES) % DQK_C) < half, -1.0, 1.0)
    sin = np.tile(np.sin(ang), (1, reps)) * sign[None, :]
    log_g = np.log(1.0 - np.power(2.0, -5.0 - np.arange(N_HEADS_C, dtype=np.float64)))
    pos = np.arange(t, dtype=np.float64)
    diff = pos[:, None] - pos[None, :]
    same_or_past = (np.arange(t)[None, :] // CHUNK) <= (np.arange(t)[:, None] // CHUNK)
    dmat = np.where(same_or_past[None], np.exp(log_g[:, None, None] * np.abs(diff)[None]), 0.0)
    qdec = np.repeat(np.exp(log_g[None, :] * (pos[:, None] + 1.0)), DQK_C, axis=1)
    kdec = np.repeat(np.exp(log_g[None, :] * (t - 1.0 - pos[:, None])), DQK_C, axis=1)
    sdec = np.broadcast_to(np.exp(log_g * t)[:, None, None], (N_HEADS_C, 1, DV_C))
    return tuple(jnp.asarray(a.astype(np.float32)) for a in (cos, sin, qdec, kdec, dmat, sdec))


def _retention(proj):
    seq = proj.shape[0]
    t = BLK_C
    cos, sin, qdec, kdec, dmat, sdec = _retention_tables(seq)
    wv = N_HEADS_C * DV_C
    return pl.pallas_call(
        _retention_kernel,
        grid=(seq // t,),
        in_specs=[
            pl.BlockSpec((t, wv), lambda i: (i, 0)),
            pl.BlockSpec((t, wv), lambda i: (i, 1)),
            pl.BlockSpec((t, wv), lambda i: (i, 2)),
            pl.BlockSpec((t, LANES), lambda i: (i, 0)),
            pl.BlockSpec((t, LANES), lambda i: (i, 0)),
            pl.BlockSpec((t, N_HEADS_C * DQK_C), lambda i: (0, 0)),
            pl.BlockSpec((t, N_HEADS_C * DQK_C), lambda i: (0, 0)),
            pl.BlockSpec((N_HEADS_C, t, t), lambda i: (0, 0, 0)),
            pl.BlockSpec((N_HEADS_C, 1, DV_C), lambda i: (0, 0, 0)),
        ],
        out_specs=pl.BlockSpec((t, wv), lambda i: (i, 0)),
        out_shape=jax.ShapeDtypeStruct((seq, wv), BF16),
        scratch_shapes=[pltpu.VMEM((N_HEADS_C, DQK_C, DV_C), F32)],
        compiler_params=pltpu.CompilerParams(dimension_semantics=("arbitrary",)),
        name="retention",
    )(proj, proj, proj, cos, sin, qdec, kdec, dmat, sdec)


def _s5_kernel(*refs):
    ncb = S5_CH // LANES
    u_refs = refs[:ncb]
    (mt_ref, bt_ref, ctr_ref, cti_ref, are_ref, aim_ref, y_ref,
     ut_ref, yt_ref, ys_ref, vr_ref, vi_ref, spr_ref, spi_ref, carry_ref) = refs[ncb:]
    tc = S5_TC
    gp = S5_GROUP
    n = S5_STATE
    ng = S5_GROUPS

    @pl.when(pl.program_id(0) == 0)
    def _():
        carry_ref[...] = jnp.zeros(carry_ref.shape, F32)

    for s in range(S5_T):
        for k in range(ncb):
            ut_ref[s, k * LANES:(k + 1) * LANES, :] = u_refs[k][pl.ds(s, tc, stride=S5_T), :].T

    unroll = 4

    def intra(it, carry):
        for k in range(unroll):
            g = it * unroll + k
            r0 = pl.multiple_of(g * gp, gp)
            ug = ut_ref[:, pl.ds(r0, gp), :].reshape(S5_T * gp, tc).astype(BF16)
            yt_ref[:, pl.ds(r0, gp), :] = _dot(mt_ref[g], ug).reshape(S5_T, gp, tc)
            vt = _dot(bt_ref[g], ug)
            n0 = pl.multiple_of(g * n, n)
            vr_ref[pl.ds(n0, n), :] = vt[0:n]
            vi_ref[pl.ds(n0, n), :] = vt[n:2 * n]
        return carry

    lax.fori_loop(0, ng // unroll, intra, 0)

    sub = SUBLANES
    nv = tc // sub
    row = lax.broadcasted_iota(jnp.int32, (tc, LANES), 0)
    in_vreg = lax.rem(row, sub)

    def rows_of(v, r):
        return jnp.broadcast_to(v[r:r + 1], (tc, LANES))

    for j in range(ng * n // LANES):
        cols = slice(j * LANES, (j + 1) * LANES)
        pwr, pwi = are_ref[:, cols], aim_ref[:, cols]
        xr = vr_ref[cols, :].T
        xi = vi_ref[cols, :].T
        for d in (1, 2, 4):
            keep = in_vreg >= d
            sr = jnp.where(keep, pltpu.roll(xr, d, 0), 0.0)
            si = jnp.where(keep, pltpu.roll(xi, d, 0), 0.0)
            fr, fi = rows_of(pwr, d - 1), rows_of(pwi, d - 1)
            xr, xi = xr + (fr * sr - fi * si), xi + (fr * si + fi * sr)
        cr, ci = carry_ref[0, :, cols], carry_ref[1, :, cols]
        cr0, ci0 = cr, ci
        outr, outi = [], []
        for v in range(nv):
            yr = xr[v * sub:(v + 1) * sub] + (pwr * cr - pwi * ci)
            yi = xi[v * sub:(v + 1) * sub] + (pwr * ci + pwi * cr)
            outr.append(yr)
            outi.append(yi)
            cr = jnp.broadcast_to(yr[sub - 1:sub], (sub, LANES))
            ci = jnp.broadcast_to(yi[sub - 1:sub], (sub, LANES))
        carry_ref[0, :, cols] = cr
        carry_ref[1, :, cols] = ci
        sr = jnp.concatenate(outr, axis=0)
        si = jnp.concatenate(outi, axis=0)
        first = row == 0
        spr_ref[j] = jnp.where(first, rows_of(cr0, 0), pltpu.roll(sr, 1, 0))
        spi_ref[j] = jnp.where(first, rows_of(ci0, 0), pltpu.roll(si, 1, 0))

    def cross(it, carry):
        for k in range(unroll):
            jp = it * unroll + k
            r0 = pl.multiple_of(jp * 2 * gp, 2 * gp)
            yc = (_dot_nt(ctr_ref[jp], spr_ref[jp].astype(BF16))
                  + _dot_nt(cti_ref[jp], spi_ref[jp].astype(BF16)))
            yt_ref[:, pl.ds(r0, 2 * gp), :] += yc.reshape(S5_T, 2 * gp, tc)
        return carry

    lax.fori_loop(0, ng // 2 // unroll, cross, 0)

    for s in range(S5_T):
        for k in range(ncb):
            ys_ref[k, pl.ds(s, tc, stride=S5_T), :] = yt_ref[s, k * LANES:(k + 1) * LANES, :].T
    for k in range(ncb):
        y_ref[:, k * LANES:(k + 1) * LANES] = ys_ref[k]


def _s5_matrices(lam_re, lam_im, log_step, b_re, b_im, c_re, c_im, d_skip):
    hi = lax.Precision.HIGHEST
    t, gp, n, ng = S5_T, S5_GROUP, S5_STATE, S5_GROUPS
    lam = lax.complex(lam_re.astype(F32), lam_im.astype(F32))
    step = jnp.exp(log_step.astype(F32))[:, None]
    ls = lam * step
    a_bar = jnp.exp(ls)
    b_bar = ((a_bar - 1.0) / lam)[..., None] * lax.complex(b_re.astype(F32), b_im.astype(F32))
    cm = lax.complex(c_re.astype(F32), c_im.astype(F32))

    def apow(k):
        kk = k.astype(F32).astype(jnp.complex64)
        return jnp.exp(ls.reshape((ng,) + (1,) * k.ndim + (n,)) * kk[None, ..., None])

    tt = jnp.arange(t)
    kmat = jnp.einsum('gpn,gln,gnq->glpq', cm, apow(tt), b_bar, precision=hi).real
    krev = jnp.transpose(kmat[:, ::-1], (0, 2, 1, 3)).reshape(ng, gp, t * gp)
    kpad = jnp.pad(krev, ((0, 0), (0, 0), (0, t * gp)))
    mt = jnp.concatenate([kpad[:, :, (t - 1 - to) * gp:(2 * t - 1 - to) * gp] for to in range(t)], axis=1)
    dvec = jnp.tile(d_skip.astype(F32).reshape(ng, 1, gp), (1, t, 1)).reshape(ng, t * gp)
    mt = mt + jnp.eye(t * gp, dtype=F32)[None] * dvec[:, :, None]
    z = jnp.swapaxes(apow(t - 1 - tt), 1, 2)[:, :, :, None] * b_bar[:, :, None, :]
    z = z.reshape(ng, n, t * gp)
    bt = jnp.concatenate([z.real, z.imag], axis=1)
    w = cm[:, None, :, :] * apow(tt + 1)[:, :, None, :]

    def pair_readout(x):
        x = x.reshape(ng // 2, 2, t, gp, n)
        first = jnp.pad(x[:, 0], ((0, 0), (0, 0), (0, 0), (0, n)))
        second = jnp.pad(x[:, 1], ((0, 0), (0, 0), (0, 0), (n, 0)))
        return jnp.stack([first, second], axis=2).reshape(ng // 2, t * 2 * gp, 2 * n).astype(BF16)

    ctr, cti = pair_readout(w.real), pair_readout(-w.imag)
    a_chunk = jnp.transpose(apow(t * (jnp.arange(SUBLANES) + 1)), (1, 0, 2)).reshape(SUBLANES, ng * n)
    return mt.astype(BF16), bt.astype(BF16), ctr, cti, a_chunk.real, a_chunk.imag


def _s5(proj, mats):
    seq, width = proj.shape
    t, tc, gp, n, ng = S5_T, S5_TC, S5_GROUP, S5_STATE, S5_GROUPS
    rows = t * tc
    ncb = S5_CH // LANES
    cb0 = (width - S5_CH) // LANES
    u_specs = [pl.BlockSpec((rows, LANES), (lambda i, k=k: (i, cb0 + k))) for k in range(ncb)]
    nsb = ng * n // LANES
    return pl.pallas_call(
        _s5_kernel,
        grid=(seq // rows,),
        in_specs=u_specs + [_const_spec(m.shape) for m in mats],
        out_specs=pl.BlockSpec((rows, S5_CH), lambda i: (i, 0)),
        out_shape=jax.ShapeDtypeStruct((seq, S5_CH), F32),
        scratch_shapes=[
            pltpu.VMEM((t, S5_CH, tc), F32),
            pltpu.VMEM((t, S5_CH, tc), F32),
            pltpu.VMEM((ncb, rows, LANES), F32),
            pltpu.VMEM((ng * n, tc), F32),
            pltpu.VMEM((ng * n, tc), F32),
            pltpu.VMEM((nsb, tc, LANES), F32),
            pltpu.VMEM((nsb, tc, LANES), F32),
            pltpu.VMEM((2, SUBLANES, ng * n), F32),
        ],
        compiler_params=pltpu.CompilerParams(dimension_semantics=("arbitrary",)),
        name="s5_scan",
    )(*([proj] * ncb), *mats)


def _mix_ffn_kernel(*refs, glu, final):
    (x_ref, a_ref, b_ref, wo_ref, g1_ref), refs = refs[:5], refs[5:]
    if glu:
        gw_ref, refs = refs[0], refs[1:]
    (g_ref, sc_ref, sh_ref, gate_ref, win_ref, cw_ref, cb_ref, wout_ref), refs = refs[:8], refs[8:]
    if final:
        fg_ref, o_ref, h_ref, act_ref, gbuf_ref, carry_ref = refs
    else:
        ng_ref, nsc_ref, nsh_ref, o_ref, hn_ref, h_ref, act_ref, gbuf_ref, carry_ref = refs
    tm = x_ref.shape[0]
    halo = gbuf_ref.shape[0] - tm

    @pl.when(pl.program_id(0) == 0)
    def _():
        carry_ref[...] = jnp.zeros(carry_ref.shape, F32)

    if glu:
        y = jax.nn.gelu(b_ref[...]).astype(BF16)
        gg = _dot(y, gw_ref[...])
        half = gg.shape[1] // 2
        b = (gg[:, :half] * jax.nn.sigmoid(gg[:, half:])).astype(BF16)
    else:
        b = b_ref[...]
    cat = jnp.concatenate([a_ref[...], b], axis=1)
    x = x_ref[...] + g1_ref[...] * _dot(cat, wo_ref[...])
    h_ref[...] = _mod_rmsnorm(x, g_ref[...], sc_ref[...], sh_ref[...]).astype(BF16)
    for f in range(D_FF // TF_FFN):
        cs = slice(f * TF_FFN, (f + 1) * TF_FFN)
        gs = slice(D_FF + f * TF_FFN, D_FF + (f + 1) * TF_FFN)
        h = h_ref[...]
        val = _dot(h, win_ref[:, cs])
        gate = _dot(h, win_ref[:, gs])
        gbuf_ref[0:halo, :] = carry_ref[:, cs]
        gbuf_ref[halo:halo + tm, :] = gate
        carry_ref[:, cs] = gate[tm - halo:tm, :]
        conv = (gate * cw_ref[2:3, cs] + gbuf_ref[halo - 1:halo - 1 + tm, :] * cw_ref[1:2, cs]
                + gbuf_ref[halo - 2:halo - 2 + tm, :] * cw_ref[0:1, cs] + cb_ref[:, cs])
        act_ref[:, cs] = (jax.nn.gelu(conv) * val).astype(BF16)
    xn = x + gate_ref[...] * _dot(act_ref[...], wout_ref[...])
    if final:
        xn = xn * lax.rsqrt(jnp.mean(xn * xn, axis=-1, keepdims=True) + EPS) * fg_ref[...]
    else:
        hn_ref[...] = _mod_rmsnorm(xn, ng_ref[...], nsc_ref[...], nsh_ref[...]).astype(BF16)
    o_ref[...] = xn


def _layer_spec(shape, layer):
    idx = (layer,) + (0,) * (len(shape) - 1)
    return pl.BlockSpec((None,) + tuple(shape[1:]), lambda *_: idx, pipeline_mode=pl.Buffered(1))


def _mix_ffn(x, a, b, wo, gate1, glu_w, g, scale, shift, gate2, w_in, conv_w, conv_b, w_out, tail, layer):
    seq, d = x.shape
    final = len(tail) == 1
    tm = TM_FFN
    halo = SUBLANES
    row = pl.BlockSpec((1, d), lambda i: (0, 0))
    rows = lambda w: pl.BlockSpec((tm, w), lambda i: (i, 0))
    conv_b = conv_b.reshape(conv_b.shape[0], 1, D_FF)
    in_specs = [rows(d), rows(a.shape[1]), rows(b.shape[1]), _const_spec(wo.shape), row]
    args = [x, a, b, wo, gate1]
    if glu_w is not None:
        in_specs.append(_const_spec(glu_w.shape))
        args.append(glu_w)
    in_specs += [
        row, row, row, row,
        _layer_spec(w_in.shape, layer),
        _layer_spec(conv_w.shape, layer),
        _layer_spec(conv_b.shape, layer),
        _layer_spec(w_out.shape, layer),
    ] + [row] * len(tail)
    args += [g.reshape(1, d), scale, shift, gate2, w_in, conv_w, conv_b, w_out]
    args += [t.reshape(1, d) for t in tail]
    out_specs = [rows(d)] if final else [rows(d), rows(d)]
    out_shape = [jax.ShapeDtypeStruct((seq, d), F32)] + ([] if final else [jax.ShapeDtypeStruct((seq, d), BF16)])
    return pl.pallas_call(
        functools.partial(_mix_ffn_kernel, glu=glu_w is not None, final=final),
        grid=(seq // tm,),
        in_specs=in_specs,
        out_specs=out_specs,
        out_shape=out_shape,
        scratch_shapes=[
            pltpu.VMEM((tm, d), BF16),
            pltpu.VMEM((tm, D_FF), BF16),
            pltpu.VMEM((tm + halo, TF_FFN), F32),
            pltpu.VMEM((halo, D_FF), F32),
        ],
        compiler_params=pltpu.CompilerParams(dimension_semantics=("arbitrary",)),
        name="mix_ffn",
    )(*args)


def kernel(x, c, t5_table, mod_w, mod_b, norm1_g, norm2_g, ffn_w_in, ffn_conv_w, ffn_conv_b, ffn_w_out,
           ev_w_in, ev_w_out, diff_lambda, diff_subln_g, band_rel_bias,
           od_w_in, od_w_out, s5_lam_re, s5_lam_im, s5_log_step, s5_b_re, s5_b_im, s5_c_re, s5_c_im,
           s5_d, s5_glu_w, final_g):
    assert x.shape[0] == 1 and x.shape[2] == D_MODEL
    seq = x.shape[1]
    assert seq % TM_PROJ == 0 and seq % (S5_T * S5_TC) == 0
    d = D_MODEL
    xs = x[0]
    mod = _modulation(c, mod_w, mod_b)
    ffn_w_in_b = ffn_w_in.astype(BF16)
    ffn_w_out_b = ffn_w_out.astype(BF16)
    mods = [[mod[i, :, k * d:(k + 1) * d] for k in range(6)] for i in range(DEPTH)]
    h = None
    for i in range(DEPTH):
        sh1, sc1, g1, sh2, sc2, g2 = mods[i]
        w_in = (ev_w_in if i % 2 == 0 else od_w_in)[i // 2].astype(BF16)
        proj_dtype = BF16 if i % 2 == 0 else F32
        if h is None:
            proj = _normproj(xs, norm1_g[i], sc1, sh1, w_in, proj_dtype)
        else:
            proj = _proj(h, w_in, proj_dtype)
        if i % 2 == 0:
            e = i // 2
            lam_init = 0.8 - 0.6 * math.exp(-0.3 * i)
            lp = diff_lambda[e].astype(F32)
            lam = jnp.exp(jnp.sum(lp[0] * lp[1])) - jnp.exp(jnp.sum(lp[2] * lp[3])) + lam_init
            mix_a = _diff_attention(proj, t5_table, lam, diff_subln_g[e], lam_init)
            mix_b = _band_attention(proj, band_rel_bias[e])
            wo, glu_w = ev_w_out[e].astype(BF16), None
        else:
            o = i // 2
            mix_a = _retention(proj)
            mats = _s5_matrices(s5_lam_re[o], s5_lam_im[o], s5_log_step[o], s5_b_re[o], s5_b_im[o],
                                s5_c_re[o], s5_c_im[o], s5_d[o])
            mix_b = _s5(proj, mats)
            wo, glu_w = od_w_out[o].astype(BF16), s5_glu_w[o].astype(BF16)
        if i == DEPTH - 1:
            tail = (final_g,)
        else:
            nsh1, nsc1 = mods[i + 1][0], mods[i + 1][1]
            tail = (norm1_g[i + 1], nsc1, nsh1)
        out = _mix_ffn(xs, mix_a, mix_b, wo, g1, glu_w, norm2_g[i], sc2, sh2, g2,
                       ffn_w_in_b, ffn_conv_w, ffn_conv_b, ffn_w_out_b, tail, layer=i)
        if i == DEPTH - 1:
            xs = out[0]
        else:
            xs, h = out
    return xs[None]
```

```python
import functools
import math

import jax
import jax.numpy as jnp
import numpy as np
from jax import lax
from jax.experimental import pallas as pl
from jax.experimental.pallas import tpu as pltpu

F32 = jnp.float32
BF16 = jnp.bfloat16

D_MODEL = 1024
DEPTH = 2
CHUNK = 64
GROUP_WIDTH = D_MODEL // 2
DK_A = 64
DV_A = 2 * DK_A
N_HEADS_A = GROUP_WIDTH // DV_A
DH_B = 64
N_HEADS_B = GROUP_WIDTH // DH_B
LEFT_CHUNKS = 8
REL_CLIP = 2 * CHUNK
NUM_BUCKETS = 32
MAX_DISTANCE = 128
DV_C = 128
DQK_C = DV_C // 2
N_HEADS_C = GROUP_WIDTH // DV_C
ROPE_BASE = 10000.0
S5_CH = GROUP_WIDTH
S5_GROUP = 16
S5_GROUPS = S5_CH // S5_GROUP
S5_STATE = 64
D_FF = ((8 * D_MODEL // 3 + 255) // 256) * 256
CONV_W = 3
EVEN_IN = 3 * N_HEADS_A * DV_A + 3 * N_HEADS_B * DH_B
ODD_IN = 2 * N_HEADS_C * DQK_C + 2 * N_HEADS_C * DV_C + S5_CH
EPS = 1e-6
NEG_INF = -1e30
LOG2E = math.log2(math.e)

LANES = 128
SUBLANES = 8
MXU_DIM = 256

TM_PROJ = 1024
TN_PROJ = 1024
TN_MOD = 1536
TM_FFN = 512
TF_FFN = MXU_DIM
BLK_A = 512
NPART_A = 2
ONES_A = 16
SINGLE_PASS_LOG2_RANGE = 96.0
SCORE_PAD = LANES
BLK_B = 4096
BAND_B = LEFT_CHUNKS * CHUNK
QW_B = 4 * CHUNK
BLK_C = 512
S5_T = 16
S5_TC = LANES

assert BLK_B % BAND_B == 0 and BLK_B % QW_B == 0 and BAND_B % QW_B == 0
assert BLK_A >= MAX_DISTANCE, "far key blocks must sit in the saturated T5 bucket"
assert DV_A == LANES and 2 * DK_A == LANES and 2 * DH_B == LANES, "attention heads are read as 128-lane column blocks"


def _dot(a, b):
    return jnp.dot(a, b, preferred_element_type=F32)


def _dot_nt(a, b):
    return lax.dot_general(a, b, (((1,), (1,)), ((), ())), preferred_element_type=F32)


def _dot_tn(a, b):
    return lax.dot_general(a, b, (((0,), (0,)), ((), ())), preferred_element_type=F32)


def _const_spec(shape):
    zeros = (0,) * len(shape)
    return pl.BlockSpec(shape, lambda *_: zeros, pipeline_mode=pl.Buffered(1))


def _mod_rmsnorm(x, g, scale, shift):
    y = x * lax.rsqrt(jnp.mean(x * x, axis=-1, keepdims=True) + EPS)
    y = y * g
    return y * (1.0 + scale) + shift


def _mod_kernel(c_ref, w_ref, b_ref, o_ref):
    c = c_ref[...]
    cond = c * jax.nn.sigmoid(c)
    o_ref[0] = jnp.sum(cond * w_ref[0], axis=0, keepdims=True) + b_ref[0]


def _modulation(c, mod_w, mod_b):
    depth, d, n = mod_w.shape
    tn = TN_MOD
    return pl.pallas_call(
        _mod_kernel,
        grid=(depth, n // tn),
        in_specs=[
            pl.BlockSpec((d, 1), lambda i, j: (0, 0)),
            pl.BlockSpec((1, d, tn), lambda i, j: (i, 0, j)),
            pl.BlockSpec((1, 1, tn), lambda i, j: (i, 0, j)),
        ],
        out_specs=pl.BlockSpec((1, 1, tn), lambda i, j: (i, 0, j)),
        out_shape=jax.ShapeDtypeStruct((depth, 1, n), F32),
        name="modulation",
    )(c.reshape(d, 1), mod_w, mod_b.reshape(depth, 1, n))


def _normproj_kernel(x_ref, g_ref, sc_ref, sh_ref, w_ref, o_ref):
    tm, n = o_ref.shape
    half = tm // 2
    for r in range(2):
        rows = slice(r * half, (r + 1) * half)
        h = _mod_rmsnorm(x_ref[rows, :], g_ref[...], sc_ref[...], sh_ref[...]).astype(BF16)
        for j in range(n // TN_PROJ):
            cols = slice(j * TN_PROJ, (j + 1) * TN_PROJ)
            o_ref[rows, cols] = _dot(h, w_ref[:, cols]).astype(o_ref.dtype)


def _normproj(x, g, scale, shift, w, out_dtype):
    seq, d = x.shape
    n = w.shape[1]
    tm = TM_PROJ
    row = pl.BlockSpec((1, d), lambda i: (0, 0))
    return pl.pallas_call(
        _normproj_kernel,
        grid=(seq // tm,),
        in_specs=[pl.BlockSpec((tm, d), lambda i: (i, 0)), row, row, row, _const_spec(w.shape)],
        out_specs=pl.BlockSpec((tm, n), lambda i: (i, 0)),
        out_shape=jax.ShapeDtypeStruct((seq, n), out_dtype),
        compiler_params=pltpu.CompilerParams(dimension_semantics=("parallel",)),
        name="normproj",
    )(x, g.reshape(1, d), scale, shift, w)


def _proj_kernel(h_ref, w_ref, o_ref):
    for j in range(o_ref.shape[1] // TN_PROJ):
        cols = slice(j * TN_PROJ, (j + 1) * TN_PROJ)
        o_ref[:, cols] = _dot(h_ref[...], w_ref[:, cols]).astype(o_ref.dtype)


def _proj(h, w, out_dtype):
    seq, d = h.shape
    n = w.shape[1]
    tm = TM_PROJ
    return pl.pallas_call(
        _proj_kernel,
        grid=(seq // tm,),
        in_specs=[pl.BlockSpec((tm, d), lambda i: (i, 0)), _const_spec(w.shape)],
        out_specs=pl.BlockSpec((tm, n), lambda i: (i, 0)),
        out_shape=jax.ShapeDtypeStruct((seq, n), out_dtype),
        compiler_params=pltpu.CompilerParams(dimension_semantics=("parallel",)),
        name="proj",
    )(h, w)


def _diffattn_kernel(qall_ref, k_ref, v_ref, bias_ref, bstat_ref, lam_ref, g_ref, o_ref,
                     flag_ref, qs_ref, vt_ref, kmax_ref, r_ref, m_ref, acc_ref, *s_refs, out_scale):
    blk = BLK_A
    nq = 2 * blk
    sub = SUBLANES
    dv = DV_A
    npart = NPART_A
    sa_ref, sb_ref = s_refs[:2 * npart], s_refs[2 * npart:4 * npart]
    pa_ref, pb_ref = s_refs[4 * npart:5 * npart], s_refs[5 * npart:6 * npart]
    i = pl.program_id(1)
    lane = lax.broadcasted_iota(jnp.int32, (blk, LANES), 1)
    same_subhead = (lax.broadcasted_iota(jnp.int32, (LANES, LANES), 0) // DK_A
                    == lax.broadcasted_iota(jnp.int32, (LANES, LANES), 1) // DK_A).astype(BF16)

    bias_max, bias_span = bstat_ref[0, 0:1, 0:1], bstat_ref[0, 1:2, 0:1]
    q_scale = DK_A ** -0.5 * LOG2E

    @pl.when(i == 0)
    def _():
        kmax_ref[...] = jnp.zeros(kmax_ref.shape, F32)

        def tr(b, qmax):
            r0 = pl.multiple_of(b * blk, blk)
            c0 = pl.multiple_of(b * nq, nq)
            vt_ref[0:dv, pl.ds(r0, blk)] = v_ref[pl.ds(r0, blk), :].astype(F32).T.astype(BF16)
            vt_ref[dv:dv + ONES_A, pl.ds(r0, blk)] = jnp.ones((ONES_A, blk), BF16)
            kf = k_ref[pl.ds(r0, blk), :].astype(F32)
            kn2 = _dot((kf * kf).astype(BF16), same_subhead)
            kmax_ref[...] = jnp.maximum(kmax_ref[...], jnp.max(kn2.reshape(blk // sub, sub, LANES), axis=0))
            qa = (qall_ref[pl.ds(r0, blk), :].astype(F32) * q_scale).astype(BF16).astype(F32)
            for m in range(2):
                qt = jnp.where((lane < DK_A) if m == 0 else (lane >= DK_A), qa, 0.0).T
                qs_ref[:, pl.ds(c0 + m * blk, blk)] = qt.astype(BF16)
                qn2 = jnp.sum(jnp.sum((qt * qt).reshape(LANES // sub, sub, blk), axis=0), axis=0, keepdims=True)
                qn2 = jnp.broadcast_to(qn2, (sub, blk))
                r_ref[:, pl.ds(c0 + m * blk, blk)] = qn2
                qmax = jnp.maximum(qmax, qn2)
            return qmax
        nblocks = v_ref.shape[0] // blk
        qmax2 = jnp.max(lax.fori_loop(0, nblocks, tr, jnp.zeros((sub, blk), F32)))
        kmax2 = jnp.max(kmax_ref[...], axis=0, keepdims=True)
        worst = 2.0 * jnp.sqrt(qmax2 * kmax2) * 1.03 + bias_span
        flag_ref[0] = (jnp.max(worst) < SINGLE_PASS_LOG2_RANGE).astype(jnp.int32)

        kmax_nq = jnp.concatenate([jnp.broadcast_to(kmax2[:, m * DK_A:m * DK_A + 1], (sub, blk)) for m in range(2)],
                                  axis=1)

        def shifts(b, carry):
            c0 = pl.multiple_of(b * nq, nq)
            r_ref[:, pl.ds(c0, nq)] = jnp.sqrt(r_ref[:, pl.ds(c0, nq)] * kmax_nq) * 1.03 + bias_max
            return carry
        lax.fori_loop(0, nblocks, shifts, 0)

    acc_ref[...] = jnp.zeros(acc_ref.shape, F32)
    q0 = pl.multiple_of(i * nq, nq)
    single_pass = flag_ref[0] == 1

    @pl.when(single_pass)
    def _():
        _diffattn_fixed_shift(i, q0, k_ref, bias_ref, qs_ref, vt_ref, r_ref, m_ref, acc_ref, pa_ref, pb_ref)

    @pl.when(jnp.logical_not(single_pass))
    def _():
        _diffattn_online(i, q0, k_ref, bias_ref, qs_ref, vt_ref, m_ref, acc_ref, sa_ref, sb_ref)

    ot = acc_ref[0:dv, 0:nq] / acc_ref[dv:dv + 1, 0:nq]
    o = ot[:, 0:blk].T - lam_ref[...] * ot[:, blk:nq].T
    o = o * lax.rsqrt(jnp.mean(o * o, axis=-1, keepdims=True) + EPS) * g_ref[...]
    o_ref[...] = (o * out_scale).astype(o_ref.dtype)


def _diffattn_fixed_shift(i, q0, k_ref, bias_ref, qs_ref, vt_ref, shift_ref, l_ref, acc_ref, pa_ref, pb_ref):
    blk = BLK_A
    nq = 2 * blk
    sub = SUBLANES
    npart = len(pa_ref)
    wq = nq // npart
    l_ref[...] = jnp.zeros(l_ref.shape, F32)

    def probs(b, p_ref, bias):
        k = k_ref[pl.ds(pl.multiple_of(b * blk, blk), blk), :]
        for part in range(npart):
            cols = slice(part * wq, (part + 1) * wq)
            qcols = pl.ds(pl.multiple_of(q0 + part * wq, wq), wq)
            s = _dot(k, qs_ref[:, qcols])
            if bias is not None:
                b0 = (part * wq) % blk
                s = s + bias[:, b0:b0 + wq]
            p = jnp.exp2(s.reshape(blk // sub, sub, wq) - shift_ref[:, qcols][None])
            l_ref[:, cols] += jnp.sum(p, axis=0)
            p_ref[part][:, 0:wq] = p.reshape(blk, wq).astype(BF16)

    def accumulate(b, p_ref):
        vt = vt_ref[0:DV_A, pl.ds(pl.multiple_of(b * blk, blk), blk)]
        for part in range(npart):
            cols = slice(part * wq, (part + 1) * wq)
            acc_ref[0:DV_A, cols] += _dot(vt, p_ref[part][:, 0:wq])

    @pl.when(i == 0)
    def _():
        probs(0, pa_ref, bias_ref[0, 1])
        accumulate(0, pa_ref)

    @pl.when(i > 0)
    def _():
        nfar = i - 1
        probs(i, pa_ref, bias_ref[0, 1])
        probs(i - 1, pb_ref, bias_ref[0, 0])
        accumulate(i, pa_ref)

        def pair(t):
            probs(2 * t, pa_ref, None)
            accumulate(jnp.where(t == 0, i - 1, 2 * t - 1), pb_ref)
            probs(2 * t + 1, pb_ref, None)
            accumulate(2 * t, pa_ref)

        def four_pairs(u, carry):
            for v in range(4):
                pair(4 * u + v)
            return carry

        npairs = nfar // 2
        lax.fori_loop(0, npairs // 4, four_pairs, 0)

        def one_pair(t, carry):
            pair(t)
            return carry
        lax.fori_loop(4 * (npairs // 4), npairs, one_pair, 0)
        in_pb = jnp.where(npairs == 0, i - 1, 2 * npairs - 1)

        @pl.when(lax.rem(nfar, 2) == 1)
        def _():
            probs(nfar - 1, pa_ref, None)
            accumulate(in_pb, pb_ref)
            accumulate(nfar - 1, pa_ref)

        @pl.when(lax.rem(nfar, 2) == 0)
        def _():
            accumulate(in_pb, pb_ref)

    acc_ref[DV_A:DV_A + sub, 0:nq] = jnp.broadcast_to(jnp.sum(l_ref[...], axis=0, keepdims=True), (sub, nq))


def _diffattn_online(i, q0, k_ref, bias_ref, qs_ref, vt_ref, m_ref, acc_ref, sa_ref, sb_ref):
    blk = BLK_A
    nq = 2 * blk
    sub = SUBLANES
    npart = len(sa_ref) // 2
    wq = nq // npart
    m_ref[...] = jnp.full(m_ref.shape, NEG_INF, F32)

    def scores(b, s_ref):
        k = k_ref[pl.ds(pl.multiple_of(b * blk, blk), blk), :]
        for part in range(npart):
            s = _dot(k, qs_ref[:, pl.ds(pl.multiple_of(q0 + part * wq, wq), wq)])
            s_ref[part][:, 0:wq] = s
            s_ref[npart + part][...] = jnp.max(s.reshape(blk // sub, sub, wq), axis=0)

    def softmax_pv(b, s_ref, bias):
        vt = vt_ref[:, pl.ds(pl.multiple_of(b * blk, blk), blk)]
        for part in range(npart):
            cols = slice(part * wq, (part + 1) * wq)
            s = s_ref[part][:, 0:wq]
            if bias is not None:
                b0 = (part * wq) % blk
                s = s + bias[:, b0:b0 + wq]
            s = s.reshape(blk // sub, sub, wq)
            m_prev = m_ref[:, cols]
            smax = jnp.max(s, axis=0) if bias is not None else s_ref[npart + part][...]
            m_cur = jnp.max(smax, axis=0, keepdims=True)
            m_new = jnp.maximum(m_prev, m_cur)
            alpha = jnp.exp2(m_prev - m_new)
            p = jnp.exp2(s - m_new[None])
            pv = _dot(vt, p.reshape(blk, wq).astype(BF16))
            acc_ref[:, cols] = acc_ref[:, cols] * alpha[0:1] + pv
            m_ref[:, cols] = m_new

    nfar = jnp.maximum(i - 1, 0)
    odd = lax.rem(nfar, 2)

    @pl.when(i == 0)
    def _():
        scores(0, sb_ref)

    @pl.when(i > 0)
    def _():
        @pl.when(odd == 1)
        def _():
            scores(0, sb_ref)
            scores(1, sa_ref)
            softmax_pv(0, sb_ref, None)

        @pl.when(odd == 0)
        def _():
            scores(0, sa_ref)

        def pair(b):
            scores(b + 1, sb_ref)
            softmax_pv(b, sa_ref, None)
            scores(b + 2, sa_ref)
            softmax_pv(b + 1, sb_ref, None)

        def quad_body(t, carry):
            pair(odd + 4 * t)
            pair(odd + 4 * t + 2)
            return carry

        npairs = nfar // 2
        lax.fori_loop(0, npairs // 2, quad_body, 0)

        @pl.when(lax.rem(npairs, 2) == 1)
        def _():
            pair(odd + 2 * (npairs - 1))
        scores(i, sb_ref)
        softmax_pv(i - 1, sa_ref, bias_ref[0, 0])

    softmax_pv(i, sb_ref, bias_ref[0, 1])


_TOEPLITZ_ROWS = 512


def _toeplitz_kernel(v_ref, o_ref, *, keep):
    rows, cols = o_ref.shape[2:]
    x = jnp.broadcast_to(v_ref[0, 0], (rows, v_ref.shape[-1]))
    tile = pltpu.roll(x, 0, 1, stride=1, stride_axis=0)[:, :cols]
    r = lax.broadcasted_iota(jnp.int32, (rows, cols), 0) + pl.program_id(1) * rows
    c = lax.broadcasted_iota(jnp.int32, (rows, cols), 1)
    for variant in range(o_ref.shape[0]):
        o_ref[variant, 0] = jnp.where(keep(r, c, variant), tile, NEG_INF)


def _toeplitz_tiles(fn, keep, heads, rows, cols, variants=1):
    n = rows + cols
    rb = rows if rows % _TOEPLITZ_ROWS else _TOEPLITZ_ROWS
    assert rows % rb == 0 and n % LANES == 0
    idx = jnp.arange(n, dtype=jnp.int32)
    vec = fn(jnp.where(idx < cols, idx, idx - n)).astype(F32)
    vecs = jnp.stack([jnp.roll(vec, k * rb, axis=1) for k in range(rows // rb)], axis=1)
    return pl.pallas_call(
        functools.partial(_toeplitz_kernel, keep=keep),
        grid=(heads, rows // rb),
        in_specs=[pl.BlockSpec((1, 1, 1, n), lambda h, k: (h, k, 0, 0))],
        out_specs=pl.BlockSpec((variants, 1, rb, cols), lambda h, k: (0, h, k, 0)),
        out_shape=jax.ShapeDtypeStruct((variants, heads, rows, cols), F32),
        name="toeplitz_tiles",
    )(vecs.reshape(heads, rows // rb, 1, n))


def _bias_stats(tiles):
    finite = tiles > 0.5 * NEG_INF
    bias_max = jnp.maximum(jnp.max(jnp.where(finite, tiles, NEG_INF), axis=(1, 2)), 0.0)
    bias_min = jnp.minimum(jnp.min(jnp.where(finite, tiles, -NEG_INF), axis=(1, 2)), 0.0)
    return jnp.broadcast_to(jnp.stack([bias_max, bias_max - bias_min], axis=1)[:, :, None],
                            (tiles.shape[0], 2, LANES))


def _t5_bucket(rel):
    nb = NUM_BUCKETS // 2
    max_exact = nb // 2
    bucket = jnp.where(rel > 0, nb, 0)
    n = jnp.abs(rel)
    nf = jnp.maximum(n, 1).astype(F32)
    large = max_exact + (jnp.log(nf / max_exact) / math.log(MAX_DISTANCE / max_exact)
                         * (nb - max_exact)).astype(jnp.int32)
    large = jnp.minimum(large, nb - 1)
    return bucket + jnp.where(n < max_exact, n, large)


def _diff_bias_tiles(t5_table):
    blk = BLK_A
    table = t5_table.astype(F32)
    far = table[_t5_bucket(jnp.full((), -(blk + 1), jnp.int32))]
    def visible(r, c, variant):
        return jnp.floor_divide(r - blk, CHUNK) <= jnp.floor_divide(c, CHUNK)

    tiles = _toeplitz_tiles(lambda x: ((table[_t5_bucket(-x - blk)] - far) * LOG2E).T, visible,
                            N_HEADS_A, 2 * blk, blk)
    return tiles.reshape(N_HEADS_A, 2, blk, blk)


def _diff_attention(proj, t5_table, lam, subln_g, lam_init):
    seq = proj.shape[0]
    blk = BLK_A
    bias = _diff_bias_tiles(t5_table)
    ha = N_HEADS_A
    bstat = _bias_stats(bias.reshape(ha, 2 * blk, blk))
    kern = functools.partial(_diffattn_kernel, out_scale=1.0 - lam_init)
    return pl.pallas_call(
        kern,
        grid=(ha, seq // blk),
        in_specs=[
            pl.BlockSpec((seq, DV_A), lambda h, i: (0, h)),
            pl.BlockSpec((seq, DV_A), lambda h, i: (0, ha + h)),
            pl.BlockSpec((seq, DV_A), lambda h, i: (0, 2 * ha + h)),
            pl.BlockSpec((1, 2, blk, blk), lambda h, i: (h, 0, 0, 0)),
            pl.BlockSpec((1, 2, LANES), lambda h, i: (h, 0, 0)),
            pl.BlockSpec((1, DV_A), lambda h, i: (0, 0)),
            pl.BlockSpec((1, DV_A), lambda h, i: (0, 0)),
        ],
        out_specs=pl.BlockSpec((blk, DV_A), lambda h, i: (i, h)),
        out_shape=jax.ShapeDtypeStruct((seq, ha * DV_A), BF16),
        scratch_shapes=[
            pltpu.SMEM((1,), jnp.int32),
            pltpu.VMEM((DV_A, 2 * seq), BF16),
            pltpu.VMEM((DV_A + ONES_A, seq), BF16),
            pltpu.VMEM((SUBLANES, LANES), F32),
            pltpu.VMEM((SUBLANES, 2 * seq), F32),
            pltpu.VMEM((SUBLANES, 2 * blk), F32),
            pltpu.VMEM((DV_A + ONES_A, 2 * blk), F32),
        ] + 2 * ([pltpu.VMEM((blk, 2 * blk // NPART_A + SCORE_PAD), F32)] * NPART_A
                 + [pltpu.VMEM((SUBLANES, 2 * blk // NPART_A), F32)] * NPART_A)
        + 2 * [pltpu.VMEM((blk, 2 * blk // NPART_A + SCORE_PAD), BF16)] * NPART_A,
        compiler_params=pltpu.CompilerParams(dimension_semantics=("parallel", "arbitrary")),
        name="diff_attention",
    )(proj, proj, proj, bias, bstat, jnp.full((1, DV_A), lam, F32), subln_g.reshape(1, DV_A).astype(F32))


def _band_kernel(q_ref, kp_ref, kc_ref, vp_ref, vc_ref, *refs):
    qw, band = QW_B, BAND_B
    nbias = band // qw + 1
    bias_refs, bstat_ref, o_ref = refs[:nbias], refs[nbias], refs[nbias + 1]
    p_refs = refs[nbias + 2:]
    ngroups = len(p_refs)
    nk = band + qw
    sub = SUBLANES
    q = (q_ref[...].astype(F32) * (DH_B ** -0.5 * LOG2E)).astype(BF16).astype(F32)
    lane = lax.broadcasted_iota(jnp.int32, q.shape, 1)
    qt = (jnp.where(lane < DH_B, q, 0.0).T, jnp.where(lane >= DH_B, q, 0.0).T)
    qh = (qt[0].astype(BF16), qt[1].astype(BF16))
    k_all = jnp.concatenate([kp_ref[...], kc_ref[...]], axis=0)
    vt_all = jnp.concatenate([vp_ref[...], vc_ref[...]], axis=0).astype(F32).T.astype(BF16)

    def group_operands(g):
        k0 = g * qw
        qs = jnp.concatenate([qh[0][:, k0:k0 + qw], qh[1][:, k0:k0 + qw]], axis=1)
        bias_ref = bias_refs[min(g, nbias - 1)]
        bias = jnp.concatenate([bias_ref[0, 0], bias_ref[0, 1]], axis=1)
        return k0, qs, bias

    def store_group(g, ot):
        o = jnp.concatenate([ot[0:DH_B, 0:qw], ot[DH_B:2 * DH_B, qw:2 * qw]], axis=0)
        o_ref[g * qw:(g + 1) * qw, :] = o.T.astype(o_ref.dtype)

    same_head = (lax.broadcasted_iota(jnp.int32, (LANES, LANES), 0) // DH_B
                 == lax.broadcasted_iota(jnp.int32, (LANES, LANES), 1) // DH_B).astype(BF16)
    kf = k_all.astype(F32)
    kn2 = _dot((kf * kf).astype(BF16), same_head)
    kmax2 = jnp.max(jnp.max(kn2.reshape(kn2.shape[0] // sub, sub, LANES), axis=0), axis=0, keepdims=True)
    shifts, worst = [], None
    for m in range(2):
        qn2 = jnp.sum(qt[m] * qt[m], axis=0, keepdims=True)
        bound = jnp.sqrt(qn2 * kmax2[:, m * DH_B:m * DH_B + 1]) * 1.03
        shifts.append(bound + bstat_ref[m, 0:1, 0:1])
        spread = jnp.max(2.0 * bound + bstat_ref[m, 1:2, 0:1])
        worst = spread if worst is None else jnp.maximum(worst, spread)
    fixed_shift = worst < SINGLE_PASS_LOG2_RANGE

    @pl.when(fixed_shift)
    def _():
        sums = []
        for g in range(ngroups):
            k0, qs, bias = group_operands(g)
            r = jnp.concatenate([shifts[0][:, k0:k0 + qw], shifts[1][:, k0:k0 + qw]], axis=1)
            s = _dot(k_all[k0:k0 + nk], qs) + (bias - r)
            p = jnp.exp2(s).reshape(nk // sub, sub, 2 * qw)
            sums.append(jnp.sum(jnp.sum(p, axis=0), axis=0, keepdims=True))
            p_refs[g][:, 0:2 * qw] = p.reshape(nk, 2 * qw).astype(BF16)
        for g in range(ngroups):
            k0 = g * qw
            store_group(g, _dot(vt_all[:, k0:k0 + nk], p_refs[g][:, 0:2 * qw]) / sums[g])

    @pl.when(jnp.logical_not(fixed_shift))
    def _():
        vt_ones = jnp.concatenate([vt_all, jnp.ones((ONES_A, vt_all.shape[1]), BF16)], axis=0)
        for g in range(ngroups):
            k0, qs, bias = group_operands(g)
            s = (_dot(k_all[k0:k0 + nk], qs) + bias).reshape(nk // sub, sub, 2 * qw)
            m = jnp.max(jnp.max(s, axis=0), axis=0, keepdims=True)
            p = jnp.exp2(s - m[None])
            pv = _dot(vt_ones[:, k0:k0 + nk], p.reshape(nk, 2 * qw).astype(BF16))
            store_group(g, pv[0:2 * DH_B] / pv[2 * DH_B:2 * DH_B + 1])


def _band_bias_tiles(rel_bias):
    band = BAND_B

    def valid(r, c, variant):
        qchunk = jnp.floor_divide(c, CHUNK)
        kchunk = jnp.floor_divide(r - band, CHUNK)
        missing = jnp.where(variant == 0, 0, band - (variant - 1) * QW_B)
        return (kchunk <= qchunk) & (kchunk >= qchunk - LEFT_CHUNKS) & (r >= missing)

    return _toeplitz_tiles(
        lambda x: rel_bias.astype(F32)[:, jnp.clip(-x - band, -REL_CLIP, REL_CLIP) + REL_CLIP] * LOG2E, valid,
        N_HEADS_B, band + QW_B, QW_B, variants=1 + band // QW_B)


def _band_attention(proj, rel_bias):
    seq = proj.shape[0]
    blk, band, qw = BLK_B, BAND_B, QW_B
    bias = _band_bias_tiles(rel_bias)
    npair = N_HEADS_B // 2
    qc0 = 3 * N_HEADS_A
    per = blk // band
    prev = lambda c0: (lambda hp, i: (jnp.maximum(i * per - 1, 0), c0 + hp))
    cur = lambda c0: (lambda hp, i: (i, c0 + hp))
    return pl.pallas_call(
        _band_kernel,
        grid=(npair, seq // blk),
        in_specs=[
            pl.BlockSpec((blk, LANES), cur(qc0)),
            pl.BlockSpec((band, LANES), prev(qc0 + npair)),
            pl.BlockSpec((blk, LANES), cur(qc0 + npair)),
            pl.BlockSpec((band, LANES), prev(qc0 + 2 * npair)),
            pl.BlockSpec((blk, LANES), cur(qc0 + 2 * npair)),
        ] + [
            pl.BlockSpec((1, 2, band + qw, qw), (lambda hp, i, t=t: (jnp.where(i == 0, 1 + t, 0), hp, 0, 0)))
            for t in range(band // qw)
        ] + [
            pl.BlockSpec((1, 2, band + qw, qw), lambda hp, i: (0, hp, 0, 0)),
            pl.BlockSpec((2, 2, LANES), lambda hp, i: (hp, 0, 0)),
        ],
        out_specs=pl.BlockSpec((blk, LANES), lambda hp, i: (i, hp)),
        out_shape=jax.ShapeDtypeStruct((seq, N_HEADS_B * DH_B), BF16),
        scratch_shapes=[pltpu.VMEM((band + qw, 2 * qw + SCORE_PAD), BF16)] * (blk // qw),
        compiler_params=pltpu.CompilerParams(dimension_semantics=("parallel", "arbitrary")),
        name="band_attention",
    )(proj, proj, proj, proj, proj, *([bias] * bias.shape[0]), _bias_stats(bias[0]))


def _retention_kernel(qk_ref, v_ref, gate_ref, cos_ref, sin_ref, qdec_ref, kdec_ref, dmat_ref,
                      sdec_ref, o_ref, state_ref):
    @pl.when(pl.program_id(0) == 0)
    def _():
        state_ref[...] = jnp.zeros(state_ref.shape, F32)

    cos = cos_ref[...]
    sin = sin_ref[...]
    lane = lax.broadcasted_iota(jnp.int32, cos.shape, 1)
    first_half = (lane % DQK_C) < (DQK_C // 2)
    qk = qk_ref[...]
    parts = []
    for j in range(qk.shape[1] // LANES):
        t = qk[:, j * LANES:(j + 1) * LANES]
        partner = jnp.where(first_half, pltpu.roll(t, LANES - DQK_C // 2, 1), pltpu.roll(t, DQK_C // 2, 1))
        parts.append(t * cos + partner * sin)
    wq = N_HEADS_C * DQK_C
    q = jnp.concatenate(parts[:wq // LANES], axis=1)
    k = jnp.concatenate(parts[wq // LANES:], axis=1) * (DQK_C ** -0.5)
    qd = (q * qdec_ref[...]).astype(BF16)
    kd = (k * kdec_ref[...]).astype(BF16)
    qb = q.astype(BF16)
    kb = k.astype(BF16)
    vb = v_ref[...].astype(BF16)
    gate = gate_ref[...]
    outs = []
    for h in range(N_HEADS_C):
        qs = slice(h * DQK_C, (h + 1) * DQK_C)
        vs = slice(h * DV_C, (h + 1) * DV_C)
        scores = _dot_nt(qb[:, qs], kb[:, qs]) * dmat_ref[h]
        state = state_ref[h]
        r = _dot(scores.astype(BF16), vb[:, vs]) + _dot(qd[:, qs], state.astype(BF16))
        state_ref[h] = state * sdec_ref[h] + _dot_tn(kd[:, qs], vb[:, vs])
        r = r * lax.rsqrt(jnp.mean(r * r, axis=-1, keepdims=True) + EPS)
        g = gate[:, vs]
        outs.append(r * (g * jax.nn.sigmoid(g)))
    o_ref[...] = jnp.concatenate(outs, axis=1).astype(o_ref.dtype)


def _retention_tables(seq):
    t = BLK_C
    half = DQK_C // 2
    inv_freq = 1.0 / np.power(ROPE_BASE, np.arange(0, DQK_C, 2, dtype=np.float64) / DQK_C)
    ang = np.arange(seq, dtype=np.float64)[:, None] * inv_freq[None, :]
    reps = LANES // half
    cos = np.tile(np.cos(ang), (1, reps))
    sign = np.where((np.arange(LANES) % DQK_C) < half, -1.0, 1.0)
    sin = np.tile(np.sin(ang), (1, reps)) * sign[None, :]
    log_g = np.log(1.0 - np.power(2.0, -5.0 - np.arange(N_HEADS_C, dtype=np.float64)))
    pos = np.arange(t, dtype=np.float64)
    diff = pos[:, None] - pos[None, :]
    same_or_past = (np.arange(t)[None, :] // CHUNK) <= (np.arange(t)[:, None] // CHUNK)
    dmat = np.where(same_or_past[None], np.exp(log_g[:, None, None] * np.abs(diff)[None]), 0.0)
    qdec = np.repeat(np.exp(log_g[None, :] * (pos[:, None] + 1.0)), DQK_C, axis=1)
    kdec = np.repeat(np.exp(log_g[None, :] * (t - 1.0 - pos[:, None])), DQK_C, axis=1)
    sdec = np.broadcast_to(np.exp(log_g * t)[:, None, None], (N_HEADS_C, 1, DV_C))
    return tuple(jnp.asarray(a.astype(np.float32)) for a in (cos, sin, qdec, kdec, dmat, sdec))


def _retention(proj):
    seq = proj.shape[0]
    t = BLK_C
    cos, sin, qdec, kdec, dmat, sdec = _retention_tables(seq)
    wv = N_HEADS_C * DV_C
    return pl.pallas_call(
        _retention_kernel,
        grid=(seq // t,),
        in_specs=[
            pl.BlockSpec((t, wv), lambda i: (i, 0)),
            pl.BlockSpec((t, wv), lambda i: (i, 1)),
            pl.BlockSpec((t, wv), lambda i: (i, 2)),
            pl.BlockSpec((t, LANES), lambda i: (i, 0)),
            pl.BlockSpec((t, LANES), lambda i: (i, 0)),
            pl.BlockSpec((t, N_HEADS_C * DQK_C), lambda i: (0, 0)),
            pl.BlockSpec((t, N_HEADS_C * DQK_C), lambda i: (0, 0)),
            pl.BlockSpec((N_HEADS_C, t, t), lambda i: (0, 0, 0)),
            pl.BlockSpec((N_HEADS_C, 1, DV_C), lambda i: (0, 0, 0)),
        ],
        out_specs=pl.BlockSpec((t, wv), lambda i: (i, 0)),
        out_shape=jax.ShapeDtypeStruct((seq, wv), BF16),
        scratch_shapes=[pltpu.VMEM((N_HEADS_C, DQK_C, DV_C), F32)],
        compiler_params=pltpu.CompilerParams(dimension_semantics=("arbitrary",)),
        name="retention",
    )(proj, proj, proj, cos, sin, qdec, kdec, dmat, sdec)


def _s5_kernel(*refs):
    ncb = S5_CH // LANES
    u_refs = refs[:ncb]
    (mt_ref, bt_ref, ctr_ref, cti_ref, are_ref, aim_ref, y_ref,
     ut_ref, yt_ref, ys_ref, vr_ref, vi_ref, spr_ref, spi_ref, carry_ref) = refs[ncb:]
    tc = S5_TC
    gp = S5_GROUP
    n = S5_STATE
    ng = S5_GROUPS

    @pl.when(pl.program_id(0) == 0)
    def _():
        carry_ref[...] = jnp.zeros(carry_ref.shape, F32)

    for s in range(S5_T):
        for k in range(ncb):
            ut_ref[s, k * LANES:(k + 1) * LANES, :] = u_refs[k][pl.ds(s, tc, stride=S5_T), :].T

    unroll = 4

    def intra(it, carry):
        for k in range(unroll):
            g = it * unroll + k
            r0 = pl.multiple_of(g * gp, gp)
            ug = ut_ref[:, pl.ds(r0, gp), :].reshape(S5_T * gp, tc).astype(BF16)
            yt_ref[:, pl.ds(r0, gp), :] = _dot(mt_ref[g], ug).reshape(S5_T, gp, tc)
            vt = _dot(bt_ref[g], ug)
            n0 = pl.multiple_of(g * n, n)
            vr_ref[pl.ds(n0, n), :] = vt[0:n]
            vi_ref[pl.ds(n0, n), :] = vt[n:2 * n]
        return carry

    lax.fori_loop(0, ng // unroll, intra, 0)

    sub = SUBLANES
    nv = tc // sub
    row = lax.broadcasted_iota(jnp.int32, (tc, LANES), 0)
    in_vreg = lax.rem(row, sub)

    def rows_of(v, r):
        return jnp.broadcast_to(v[r:r + 1], (tc, LANES))

    for j in range(ng * n // LANES):
        cols = slice(j * LANES, (j + 1) * LANES)
        pwr, pwi = are_ref[:, cols], aim_ref[:, cols]
        xr = vr_ref[cols, :].T
        xi = vi_ref[cols, :].T
        for d in (1, 2, 4):
            keep = in_vreg >= d
            sr = jnp.where(keep, pltpu.roll(xr, d, 0), 0.0)
            si = jnp.where(keep, pltpu.roll(xi, d, 0), 0.0)
            fr, fi = rows_of(pwr, d - 1), rows_of(pwi, d - 1)
            xr, xi = xr + (fr * sr - fi * si), xi + (fr * si + fi * sr)
        cr, ci = carry_ref[0, :, cols], carry_ref[1, :, cols]
        cr0, ci0 = cr, ci
        outr, outi = [], []
        for v in range(nv):
            yr = xr[v * sub:(v + 1) * sub] + (pwr * cr - pwi * ci)
            yi = xi[v * sub:(v + 1) * sub] + (pwr * ci + pwi * cr)
            outr.append(yr)
            outi.append(yi)
            cr = jnp.broadcast_to(yr[sub - 1:sub], (sub, LANES))
            ci = jnp.broadcast_to(yi[sub - 1:sub], (sub, LANES))
        carry_ref[0, :, cols] = cr
        carry_ref[1, :, cols] = ci
        sr = jnp.concatenate(outr, axis=0)
        si = jnp.concatenate(outi, axis=0)
        first = row == 0
        spr_ref[j] = jnp.where(first, rows_of(cr0, 0), pltpu.roll(sr, 1, 0))
        spi_ref[j] = jnp.where(first, rows_of(ci0, 0), pltpu.roll(si, 1, 0))

    def cross(it, carry):
        for k in range(unroll):
            jp = it * unroll + k
            r0 = pl.multiple_of(jp * 2 * gp, 2 * gp)
            yc = (_dot_nt(ctr_ref[jp], spr_ref[jp].astype(BF16))
                  + _dot_nt(cti_ref[jp], spi_ref[jp].astype(BF16)))
            yt_ref[:, pl.ds(r0, 2 * gp), :] += yc.reshape(S5_T, 2 * gp, tc)
        return carry

    lax.fori_loop(0, ng // 2 // unroll, cross, 0)

    for s in range(S5_T):
        for k in range(ncb):
            ys_ref[k, pl.ds(s, tc, stride=S5_T), :] = yt_ref[s, k * LANES:(k + 1) * LANES, :].T
    for k in range(ncb):
        y_ref[:, k * LANES:(k + 1) * LANES] = ys_ref[k]


def _s5_matrices(lam_re, lam_im, log_step, b_re, b_im, c_re, c_im, d_skip):
    hi = lax.Precision.HIGHEST
    t, gp, n, ng = S5_T, S5_GROUP, S5_STATE, S5_GROUPS
    lam = lax.complex(lam_re.astype(F32), lam_im.astype(F32))
    step = jnp.exp(log_step.astype(F32))[:, None]
    ls = lam * step
    a_bar = jnp.exp(ls)
    b_bar = ((a_bar - 1.0) / lam)[..., None] * lax.complex(b_re.astype(F32), b_im.astype(F32))
    cm = lax.complex(c_re.astype(F32), c_im.astype(F32))

    def apow(k):
        kk = k.astype(F32).astype(jnp.complex64)
        return jnp.exp(ls.reshape((ng,) + (1,) * k.ndim + (n,)) * kk[None, ..., None])

    tt = jnp.arange(t)
    kmat = jnp.einsum('gpn,gln,gnq->glpq', cm, apow(tt), b_bar, precision=hi).real
    krev = jnp.transpose(kmat[:, ::-1], (0, 2, 1, 3)).reshape(ng, gp, t * gp)
    kpad = jnp.pad(krev, ((0, 0), (0, 0), (0, t * gp)))
    mt = jnp.concatenate([kpad[:, :, (t - 1 - to) * gp:(2 * t - 1 - to) * gp] for to in range(t)], axis=1)
    dvec = jnp.tile(d_skip.astype(F32).reshape(ng, 1, gp), (1, t, 1)).reshape(ng, t * gp)
    mt = mt + jnp.eye(t * gp, dtype=F32)[None] * dvec[:, :, None]
    z = jnp.swapaxes(apow(t - 1 - tt), 1, 2)[:, :, :, None] * b_bar[:, :, None, :]
    z = z.reshape(ng, n, t * gp)
    bt = jnp.concatenate([z.real, z.imag], axis=1)
    w = cm[:, None, :, :] * apow(tt + 1)[:, :, None, :]

    def pair_readout(x):
        x = x.reshape(ng // 2, 2, t, gp, n)
        first = jnp.pad(x[:, 0], ((0, 0), (0, 0), (0, 0), (0, n)))
        second = jnp.pad(x[:, 1], ((0, 0), (0, 0), (0, 0), (n, 0)))
        return jnp.stack([first, second], axis=2).reshape(ng // 2, t * 2 * gp, 2 * n).astype(BF16)

    ctr, cti = pair_readout(w.real), pair_readout(-w.imag)
    a_chunk = jnp.transpose(apow(t * (jnp.arange(SUBLANES) + 1)), (1, 0, 2)).reshape(SUBLANES, ng * n)
    return mt.astype(BF16), bt.astype(BF16), ctr, cti, a_chunk.real, a_chunk.imag


def _s5(proj, mats):
    seq, width = proj.shape
    t, tc, gp, n, ng = S5_T, S5_TC, S5_GROUP, S5_STATE, S5_GROUPS
    rows = t * tc
    ncb = S5_CH // LANES
    cb0 = (width - S5_CH) // LANES
    u_specs = [pl.BlockSpec((rows, LANES), (lambda i, k=k: (i, cb0 + k))) for k in range(ncb)]
    nsb = ng * n // LANES
    return pl.pallas_call(
        _s5_kernel,
        grid=(seq // rows,),
        in_specs=u_specs + [_const_spec(m.shape) for m in mats],
        out_specs=pl.BlockSpec((rows, S5_CH), lambda i: (i, 0)),
        out_shape=jax.ShapeDtypeStruct((seq, S5_CH), F32),
        scratch_shapes=[
            pltpu.VMEM((t, S5_CH, tc), F32),
            pltpu.VMEM((t, S5_CH, tc), F32),
            pltpu.VMEM((ncb, rows, LANES), F32),
            pltpu.VMEM((ng * n, tc), F32),
            pltpu.VMEM((ng * n, tc), F32),
            pltpu.VMEM((nsb, tc, LANES), F32),
            pltpu.VMEM((nsb, tc, LANES), F32),
            pltpu.VMEM((2, SUBLANES, ng * n), F32),
        ],
        compiler_params=pltpu.CompilerParams(dimension_semantics=("arbitrary",)),
        name="s5_scan",
    )(*([proj] * ncb), *mats)


def _mix_ffn_kernel(*refs, glu, final):
    (x_ref, a_ref, b_ref, wo_ref, g1_ref), refs = refs[:5], refs[5:]
    if glu:
        gw_ref, refs = refs[0], refs[1:]
    (g_ref, sc_ref, sh_ref, gate_ref, win_ref, cw_ref, cb_ref, wout_ref), refs = refs[:8], refs[8:]
    if final:
        fg_ref, o_ref, h_ref, act_ref, gbuf_ref, carry_ref = refs
    else:
        ng_ref, nsc_ref, nsh_ref, o_ref, hn_ref, h_ref, act_ref, gbuf_ref, carry_ref = refs
    tm = x_ref.shape[0]
    halo = gbuf_ref.shape[0] - tm

    @pl.when(pl.program_id(0) == 0)
    def _():
        carry_ref[...] = jnp.zeros(carry_ref.shape, F32)

    if glu:
        y = jax.nn.gelu(b_ref[...]).astype(BF16)
        gg = _dot(y, gw_ref[...])
        half = gg.shape[1] // 2
        b = (gg[:, :half] * jax.nn.sigmoid(gg[:, half:])).astype(BF16)
    else:
        b = b_ref[...]
    cat = jnp.concatenate([a_ref[...], b], axis=1)
    x = x_ref[...] + g1_ref[...] * _dot(cat, wo_ref[...])
    h_ref[...] = _mod_rmsnorm(x, g_ref[...], sc_ref[...], sh_ref[...]).astype(BF16)
    for f in range(D_FF // TF_FFN):
        cs = slice(f * TF_FFN, (f + 1) * TF_FFN)
        gs = slice(D_FF + f * TF_FFN, D_FF + (f + 1) * TF_FFN)
        h = h_ref[...]
        val = _dot(h, win_ref[:, cs])
        gate = _dot(h, win_ref[:, gs])
        gbuf_ref[0:halo, :] = carry_ref[:, cs]
        gbuf_ref[halo:halo + tm, :] = gate
        carry_ref[:, cs] = gate[tm - halo:tm, :]
        conv = (gate * cw_ref[2:3, cs] + gbuf_ref[halo - 1:halo - 1 + tm, :] * cw_ref[1:2, cs]
                + gbuf_ref[halo - 2:halo - 2 + tm, :] * cw_ref[0:1, cs] + cb_ref[:, cs])
        act_ref[:, cs] = (jax.nn.gelu(conv) * val).astype(BF16)
    xn = x + gate_ref[...] * _dot(act_ref[...], wout_ref[...])
    if final:
        xn = xn * lax.rsqrt(jnp.mean(xn * xn, axis=-1, keepdims=True) + EPS) * fg_ref[...]
    else:
        hn_ref[...] = _mod_rmsnorm(xn, ng_ref[...], nsc_ref[...], nsh_ref[...]).astype(BF16)
    o_ref[...] = xn


def _layer_spec(shape, layer):
    idx = (layer,) + (0,) * (len(shape) - 1)
    return pl.BlockSpec((None,) + tuple(shape[1:]), lambda *_: idx, pipeline_mode=pl.Buffered(1))


def _mix_ffn(x, a, b, wo, gate1, glu_w, g, scale, shift, gate2, w_in, conv_w, conv_b, w_out, tail, layer):
    seq, d = x.shape
    final = len(tail) == 1
    tm = TM_FFN
    halo = SUBLANES
    row = pl.BlockSpec((1, d), lambda i: (0, 0))
    rows = lambda w: pl.BlockSpec((tm, w), lambda i: (i, 0))
    conv_b = conv_b.reshape(conv_b.shape[0], 1, D_FF)
    in_specs = [rows(d), rows(a.shape[1]), rows(b.shape[1]), _const_spec(wo.shape), row]
    args = [x, a, b, wo, gate1]
    if glu_w is not None:
        in_specs.append(_const_spec(glu_w.shape))
        args.append(glu_w)
    in_specs += [
        row, row, row, row,
        _layer_spec(w_in.shape, layer),
        _layer_spec(conv_w.shape, layer),
        _layer_spec(conv_b.shape, layer),
        _layer_spec(w_out.shape, layer),
    ] + [row] * len(tail)
    args += [g.reshape(1, d), scale, shift, gate2, w_in, conv_w, conv_b, w_out]
    args += [t.reshape(1, d) for t in tail]
    out_specs = [rows(d)] if final else [rows(d), rows(d)]
    out_shape = [jax.ShapeDtypeStruct((seq, d), F32)] + ([] if final else [jax.ShapeDtypeStruct((seq, d), BF16)])
    return pl.pallas_call(
        functools.partial(_mix_ffn_kernel, glu=glu_w is not None, final=final),
        grid=(seq // tm,),
        in_specs=in_specs,
        out_specs=out_specs,
        out_shape=out_shape,
        scratch_shapes=[
            pltpu.VMEM((tm, d), BF16),
            pltpu.VMEM((tm, D_FF), BF16),
            pltpu.VMEM((tm + halo, TF_FFN), F32),
            pltpu.VMEM((halo, D_FF), F32),
        ],
        compiler_params=pltpu.CompilerParams(dimension_semantics=("arbitrary",)),
        name="mix_ffn",
    )(*args)


def kernel(x, c, t5_table, mod_w, mod_b, norm1_g, norm2_g, ffn_w_in, ffn_conv_w, ffn_conv_b, ffn_w_out,
           ev_w_in, ev_w_out, diff_lambda, diff_subln_g, band_rel_bias,
           od_w_in, od_w_out, s5_lam_re, s5_lam_im, s5_log_step, s5_b_re, s5_b_im, s5_c_re, s5_c_im,
           s5_d, s5_glu_w, final_g):
    assert x.shape[0] == 1 and x.shape[2] == D_MODEL
    seq = x.shape[1]
    assert seq % TM_PROJ == 0 and seq % (S5_T * S5_TC) == 0
    d = D_MODEL
    xs = x[0]
    mod = _modulation(c, mod_w, mod_b)
    ffn_w_in_b = ffn_w_in.astype(BF16)
    ffn_w_out_b = ffn_w_out.astype(BF16)
    mods = [[mod[i, :, k * d:(k + 1) * d] for k in range(6)] for i in range(DEPTH)]
    h = None
    for i in range(DEPTH):
        sh1, sc1, g1, sh2, sc2, g2 = mods[i]
        w_in = (ev_w_in if i % 2 == 0 else od_w_in)[i // 2].astype(BF16)
        proj_dtype = BF16 if i % 2 == 0 else F32
        if h is None:
            proj = _normproj(xs, norm1_g[i], sc1, sh1, w_in, proj_dtype)
        else:
            proj = _proj(h, w_in, proj_dtype)
        if i % 2 == 0:
            e = i // 2
            lam_init = 0.8 - 0.6 * math.exp(-0.3 * i)
            lp = diff_lambda[e].astype(F32)
            lam = jnp.exp(jnp.sum(lp[0] * lp[1])) - jnp.exp(jnp.sum(lp[2] * lp[3])) + lam_init
            mix_a = _diff_attention(proj, t5_table, lam, diff_subln_g[e], lam_init)
            mix_b = _band_attention(proj, band_rel_bias[e])
            wo, glu_w = ev_w_out[e].astype(BF16), None
        else:
            o = i // 2
            mix_a = _retention(proj)
            mats = _s5_matrices(s5_lam_re[o], s5_lam_im[o], s5_log_step[o], s5_b_re[o], s5_b_im[o],
                                s5_c_re[o], s5_c_im[o], s5_d[o])
            mix_b = _s5(proj, mats)
            wo, glu_w = od_w_out[o].astype(BF16), s5_glu_w[o].astype(BF16)
        if i == DEPTH - 1:
            tail = (final_g,)
        else:
            nsh1, nsc1 = mods[i + 1][0], mods[i + 1][1]
            tail = (norm1_g[i + 1], nsc1, nsh1)
        out = _mix_ffn(xs, mix_a, mix_b, wo, g1, glu_w, norm2_g[i], sc2, sh2, g2,
                       ffn_w_in_b, ffn_conv_w, ffn_conv_b, ffn_w_out_b, tail, layer=i)
        if i == DEPTH - 1:
            xs = out[0]
        else:
            xs, h = out
    return xs[None]
```

```python
import functools
import math

import jax
import jax.numpy as jnp
import numpy as np
from jax import lax
from jax.experimental import pallas as pl
from jax.experimental.pallas import tpu as pltpu

F32 = jnp.float32
BF16 = jnp.bfloat16

D_MODEL = 1024
DEPTH = 2
CHUNK = 64
GROUP_WIDTH = D_MODEL // 2
DK_A = 64
DV_A = 2 * DK_A
N_HEADS_A = GROUP_WIDTH // DV_A
DH_B = 64
N_HEADS_B = GROUP_WIDTH // DH_B
LEFT_CHUNKS = 8
REL_CLIP = 2 * CHUNK
NUM_BUCKETS = 32
MAX_DISTANCE = 128
DV_C = 128
DQK_C = DV_C // 2
N_HEADS_C = GROUP_WIDTH // DV_C
ROPE_BASE = 10000.0
S5_CH = GROUP_WIDTH
S5_GROUP = 16
S5_GROUPS = S5_CH // S5_GROUP
S5_STATE = 64
D_FF = ((8 * D_MODEL // 3 + 255) // 256) * 256
CONV_W = 3
EVEN_IN = 3 * N_HEADS_A * DV_A + 3 * N_HEADS_B * DH_B
ODD_IN = 2 * N_HEADS_C * DQK_C + 2 * N_HEADS_C * DV_C + S5_CH
EPS = 1e-6
NEG_INF = -1e30
LOG2E = math.log2(math.e)

LANES = 128
SUBLANES = 8
MXU_DIM = 256

TM_PROJ = 1024
TN_PROJ = 1024
TN_MOD = 1536
TM_FFN = 512
TF_FFN = MXU_DIM
BLK_A = 512
NPART_A = 2
ONES_A = 16
SINGLE_PASS_LOG2_RANGE = 96.0
SCORE_PAD = LANES
BLK_B = 4096
BAND_B = LEFT_CHUNKS * CHUNK
QW_B = 4 * CHUNK
BLK_C = 512
S5_T = 16
S5_TC = LANES

assert BLK_B % BAND_B == 0 and BLK_B % QW_B == 0 and BAND_B % QW_B == 0
assert BLK_A >= MAX_DISTANCE, "far key blocks must sit in the saturated T5 bucket"
assert DV_A == LANES and 2 * DK_A == LANES and 2 * DH_B == LANES, "attention heads are read as 128-lane column blocks"


def _dot(a, b):
    return jnp.dot(a, b, preferred_element_type=F32)


def _dot_nt(a, b):
    return lax.dot_general(a, b, (((1,), (1,)), ((), ())), preferred_element_type=F32)


def _dot_tn(a, b):
    return lax.dot_general(a, b, (((0,), (0,)), ((), ())), preferred_element_type=F32)


def _const_spec(shape):
    zeros = (0,) * len(shape)
    return pl.BlockSpec(shape, lambda *_: zeros, pipeline_mode=pl.Buffered(1))


def _mod_rmsnorm(x, g, scale, shift):
    y = x * lax.rsqrt(jnp.mean(x * x, axis=-1, keepdims=True) + EPS)
    y = y * g
    return y * (1.0 + scale) + shift


def _mod_kernel(c_ref, w_ref, b_ref, o_ref):
    c = c_ref[...]
    cond = c * jax.nn.sigmoid(c)
    o_ref[0] = jnp.sum(cond * w_ref[0], axis=0, keepdims=True) + b_ref[0]


def _modulation(c, mod_w, mod_b):
    depth, d, n = mod_w.shape
    tn = TN_MOD
    return pl.pallas_call(
        _mod_kernel,
        grid=(depth, n // tn),
        in_specs=[
            pl.BlockSpec((d, 1), lambda i, j: (0, 0)),
            pl.BlockSpec((1, d, tn), lambda i, j: (i, 0, j)),
            pl.BlockSpec((1, 1, tn), lambda i, j: (i, 0, j)),
        ],
        out_specs=pl.BlockSpec((1, 1, tn), lambda i, j: (i, 0, j)),
        out_shape=jax.ShapeDtypeStruct((depth, 1, n), F32),
        name="modulation",
    )(c.reshape(d, 1), mod_w, mod_b.reshape(depth, 1, n))


def _normproj_kernel(x_ref, g_ref, sc_ref, sh_ref, w_ref, o_ref):
    tm, n = o_ref.shape
    half = tm // 2
    for r in range(2):
        rows = slice(r * half, (r + 1) * half)
        h = _mod_rmsnorm(x_ref[rows, :], g_ref[...], sc_ref[...], sh_ref[...]).astype(BF16)
        for j in range(n // TN_PROJ):
            cols = slice(j * TN_PROJ, (j + 1) * TN_PROJ)
            o_ref[rows, cols] = _dot(h, w_ref[:, cols]).astype(o_ref.dtype)


def _normproj(x, g, scale, shift, w, out_dtype):
    seq, d = x.shape
    n = w.shape[1]
    tm = TM_PROJ
    row = pl.BlockSpec((1, d), lambda i: (0, 0))
    return pl.pallas_call(
        _normproj_kernel,
        grid=(seq // tm,),
        in_specs=[pl.BlockSpec((tm, d), lambda i: (i, 0)), row, row, row, _const_spec(w.shape)],
        out_specs=pl.BlockSpec((tm, n), lambda i: (i, 0)),
        out_shape=jax.ShapeDtypeStruct((seq, n), out_dtype),
        compiler_params=pltpu.CompilerParams(dimension_semantics=("parallel",)),
        name="normproj",
    )(x, g.reshape(1, d), scale, shift, w)


def _proj_kernel(h_ref, w_ref, o_ref):
    for j in range(o_ref.shape[1] // TN_PROJ):
        cols = slice(j * TN_PROJ, (j + 1) * TN_PROJ)
        o_ref[:, cols] = _dot(h_ref[...], w_ref[:, cols]).astype(o_ref.dtype)


def _proj(h, w, out_dtype):
    seq, d = h.shape
    n = w.shape[1]
    tm = TM_PROJ
    return pl.pallas_call(
        _proj_kernel,
        grid=(seq // tm,),
        in_specs=[pl.BlockSpec((tm, d), lambda i: (i, 0)), _const_spec(w.shape)],
        out_specs=pl.BlockSpec((tm, n), lambda i: (i, 0)),
        out_shape=jax.ShapeDtypeStruct((seq, n), out_dtype),
        compiler_params=pltpu.CompilerParams(dimension_semantics=("parallel",)),
        name="proj",
    )(h, w)


def _diffattn_kernel(qall_ref, k_ref, v_ref, bias_ref, bstat_ref, lam_ref, g_ref, o_ref,
                     flag_ref, qs_ref, vt_ref, kmax_ref, r_ref, m_ref, acc_ref, *s_refs, out_scale):
    blk = BLK_A
    nq = 2 * blk
    sub = SUBLANES
    dv = DV_A
    npart = NPART_A
    sa_ref, sb_ref = s_refs[:2 * npart], s_refs[2 * npart:4 * npart]
    pa_ref, pb_ref = s_refs[4 * npart:5 * npart], s_refs[5 * npart:6 * npart]
    i = pl.program_id(1)
    lane = lax.broadcasted_iota(jnp.int32, (blk, LANES), 1)
    same_subhead = (lax.broadcasted_iota(jnp.int32, (LANES, LANES), 0) // DK_A
                    == lax.broadcasted_iota(jnp.int32, (LANES, LANES), 1) // DK_A).astype(BF16)

    bias_max, bias_span = bstat_ref[0, 0:1, 0:1], bstat_ref[0, 1:2, 0:1]
    q_scale = DK_A ** -0.5 * LOG2E

    @pl.when(i == 0)
    def _():
        kmax_ref[...] = jnp.zeros(kmax_ref.shape, F32)

        def tr(b, qmax):
            r0 = pl.multiple_of(b * blk, blk)
            c0 = pl.multiple_of(b * nq, nq)
            vt_ref[0:dv, pl.ds(r0, blk)] = v_ref[pl.ds(r0, blk), :].astype(F32).T.astype(BF16)
            vt_ref[dv:dv + ONES_A, pl.ds(r0, blk)] = jnp.ones((ONES_A, blk), BF16)
            kf = k_ref[pl.ds(r0, blk), :].astype(F32)
            kn2 = _dot((kf * kf).astype(BF16), same_subhead)
            kmax_ref[...] = jnp.maximum(kmax_ref[...], jnp.max(kn2.reshape(blk // sub, sub, LANES), axis=0))
            qa = (qall_ref[pl.ds(r0, blk), :].astype(F32) * q_scale).astype(BF16).astype(F32)
            for m in range(2):
                qt = jnp.where((lane < DK_A) if m == 0 else (lane >= DK_A), qa, 0.0).T
                qs_ref[:, pl.ds(c0 + m * blk, blk)] = qt.astype(BF16)
                qn2 = jnp.sum(jnp.sum((qt * qt).reshape(LANES // sub, sub, blk), axis=0), axis=0, keepdims=True)
                qn2 = jnp.broadcast_to(qn2, (sub, blk))
                r_ref[:, pl.ds(c0 + m * blk, blk)] = qn2
                qmax = jnp.maximum(qmax, qn2)
            return qmax
        nblocks = v_ref.shape[0] // blk
        qmax2 = jnp.max(lax.fori_loop(0, nblocks, tr, jnp.zeros((sub, blk), F32)))
        kmax2 = jnp.max(kmax_ref[...], axis=0, keepdims=True)
        worst = 2.0 * jnp.sqrt(qmax2 * kmax2) * 1.03 + bias_span
        flag_ref[0] = (jnp.max(worst) < SINGLE_PASS_LOG2_RANGE).astype(jnp.int32)

        kmax_nq = jnp.concatenate([jnp.broadcast_to(kmax2[:, m * DK_A:m * DK_A + 1], (sub, blk)) for m in range(2)],
                                  axis=1)

        def shifts(b, carry):
            c0 = pl.multiple_of(b * nq, nq)
            r_ref[:, pl.ds(c0, nq)] = jnp.sqrt(r_ref[:, pl.ds(c0, nq)] * kmax_nq) * 1.03 + bias_max
            return carry
        lax.fori_loop(0, nblocks, shifts, 0)

    acc_ref[...] = jnp.zeros(acc_ref.shape, F32)
    q0 = pl.multiple_of(i * nq, nq)
    single_pass = flag_ref[0] == 1

    @pl.when(single_pass)
    def _():
        _diffattn_fixed_shift(i, q0, k_ref, bias_ref, qs_ref, vt_ref, r_ref, m_ref, acc_ref, pa_ref, pb_ref)

    @pl.when(jnp.logical_not(single_pass))
    def _():
        _diffattn_online(i, q0, k_ref, bias_ref, qs_ref, vt_ref, m_ref, acc_ref, sa_ref, sb_ref)

    ot = acc_ref[0:dv, 0:nq] / acc_ref[dv:dv + 1, 0:nq]
    o = ot[:, 0:blk].T - lam_ref[...] * ot[:, blk:nq].T
    o = o * lax.rsqrt(jnp.mean(o * o, axis=-1, keepdims=True) + EPS) * g_ref[...]
    o_ref[...] = (o * out_scale).astype(o_ref.dtype)


def _diffattn_fixed_shift(i, q0, k_ref, bias_ref, qs_ref, vt_ref, shift_ref, l_ref, acc_ref, pa_ref, pb_ref):
    blk = BLK_A
    nq = 2 * blk
    sub = SUBLANES
    npart = len(pa_ref)
    wq = nq // npart
    l_ref[...] = jnp.zeros(l_ref.shape, F32)

    def probs(b, p_ref, bias):
        k = k_ref[pl.ds(pl.multiple_of(b * blk, blk), blk), :]
        for part in range(npart):
            cols = slice(part * wq, (part + 1) * wq)
            qcols = pl.ds(pl.multiple_of(q0 + part * wq, wq), wq)
            s = _dot(k, qs_ref[:, qcols])
            if bias is not None:
                b0 = (part * wq) % blk
                s = s + bias[:, b0:b0 + wq]
            p = jnp.exp2(s.reshape(blk // sub, sub, wq) - shift_ref[:, qcols][None])
            l_ref[:, cols] += jnp.sum(p, axis=0)
            p_ref[part][:, 0:wq] = p.reshape(blk, wq).astype(BF16)

    def accumulate(b, p_ref):
        vt = vt_ref[0:DV_A, pl.ds(pl.multiple_of(b * blk, blk), blk)]
        for part in range(npart):
            cols = slice(part * wq, (part + 1) * wq)
            acc_ref[0:DV_A, cols] += _dot(vt, p_ref[part][:, 0:wq])

    @pl.when(i == 0)
    def _():
        probs(0, pa_ref, bias_ref[0, 1])
        accumulate(0, pa_ref)

    @pl.when(i > 0)
    def _():
        nfar = i - 1
        probs(i, pa_ref, bias_ref[0, 1])
        probs(i - 1, pb_ref, bias_ref[0, 0])
        accumulate(i, pa_ref)

        def pair(t):
            probs(2 * t, pa_ref, None)
            accumulate(jnp.where(t == 0, i - 1, 2 * t - 1), pb_ref)
            probs(2 * t + 1, pb_ref, None)
            accumulate(2 * t, pa_ref)

        def four_pairs(u, carry):
            for v in range(4):
                pair(4 * u + v)
            return carry

        npairs = nfar // 2
        lax.fori_loop(0, npairs // 4, four_pairs, 0)

        def one_pair(t, carry):
            pair(t)
            return carry
        lax.fori_loop(4 * (npairs // 4), npairs, one_pair, 0)
        in_pb = jnp.where(npairs == 0, i - 1, 2 * npairs - 1)

        @pl.when(lax.rem(nfar, 2) == 1)
        def _():
            probs(nfar - 1, pa_ref, None)
            accumulate(in_pb, pb_ref)
            accumulate(nfar - 1, pa_ref)

        @pl.when(lax.rem(nfar, 2) == 0)
        def _():
            accumulate(in_pb, pb_ref)

    acc_ref[DV_A:DV_A + sub, 0:nq] = jnp.broadcast_to(jnp.sum(l_ref[...], axis=0, keepdims=True), (sub, nq))


def _diffattn_online(i, q0, k_ref, bias_ref, qs_ref, vt_ref, m_ref, acc_ref, sa_ref, sb_ref):
    blk = BLK_A
    nq = 2 * blk
    sub = SUBLANES
    npart = len(sa_ref) // 2
    wq = nq // npart
    m_ref[...] = jnp.full(m_ref.shape, NEG_INF, F32)

    def scores(b, s_ref):
        k = k_ref[pl.ds(pl.multiple_of(b * blk, blk), blk), :]
        for part in range(npart):
            s = _dot(k, qs_ref[:, pl.ds(pl.multiple_of(q0 + part * wq, wq), wq)])
            s_ref[part][:, 0:wq] = s
            s_ref[npart + part][...] = jnp.max(s.reshape(blk // sub, sub, wq), axis=0)

    def softmax_pv(b, s_ref, bias):
        vt = vt_ref[:, pl.ds(pl.multiple_of(b * blk, blk), blk)]
        for part in range(npart):
            cols = slice(part * wq, (part + 1) * wq)
            s = s_ref[part][:, 0:wq]
            if bias is not None:
                b0 = (part * wq) % blk
                s = s + bias[:, b0:b0 + wq]
            s = s.reshape(blk // sub, sub, wq)
            m_prev = m_ref[:, cols]
            smax = jnp.max(s, axis=0) if bias is not None else s_ref[npart + part][...]
            m_cur = jnp.max(smax, axis=0, keepdims=True)
            m_new = jnp.maximum(m_prev, m_cur)
            alpha = jnp.exp2(m_prev - m_new)
            p = jnp.exp2(s - m_new[None])
            pv = _dot(vt, p.reshape(blk, wq).astype(BF16))
            acc_ref[:, cols] = acc_ref[:, cols] * alpha[0:1] + pv
            m_ref[:, cols] = m_new

    nfar = jnp.maximum(i - 1, 0)
    odd = lax.rem(nfar, 2)

    @pl.when(i == 0)
    def _():
        scores(0, sb_ref)

    @pl.when(i > 0)
    def _():
        @pl.when(odd == 1)
        def _():
            scores(0, sb_ref)
            scores(1, sa_ref)
            softmax_pv(0, sb_ref, None)

        @pl.when(odd == 0)
        def _():
            scores(0, sa_ref)

        def pair(b):
            scores(b + 1, sb_ref)
            softmax_pv(b, sa_ref, None)
            scores(b + 2, sa_ref)
            softmax_pv(b + 1, sb_ref, None)

        def quad_body(t, carry):
            pair(odd + 4 * t)
            pair(odd + 4 * t + 2)
            return carry

        npairs = nfar // 2
        lax.fori_loop(0, npairs // 2, quad_body, 0)

        @pl.when(lax.rem(npairs, 2) == 1)
        def _():
            pair(odd + 2 * (npairs - 1))
        scores(i, sb_ref)
        softmax_pv(i - 1, sa_ref, bias_ref[0, 0])

    softmax_pv(i, sb_ref, bias_ref[0, 1])


_TOEPLITZ_ROWS = 512


def _toeplitz_kernel(v_ref, o_ref, *, keep):
    rows, cols = o_ref.shape[2:]
    x = jnp.broadcast_to(v_ref[0, 0], (rows, v_ref.shape[-1]))
    tile = pltpu.roll(x, 0, 1, stride=1, stride_axis=0)[:, :cols]
    r = lax.broadcasted_iota(jnp.int32, (rows, cols), 0) + pl.program_id(1) * rows
    c = lax.broadcasted_iota(jnp.int32, (rows, cols), 1)
    for variant in range(o_ref.shape[0]):
        o_ref[variant, 0] = jnp.where(keep(r, c, variant), tile, NEG_INF)


def _toeplitz_tiles(fn, keep, heads, rows, cols, variants=1):
    n = rows + cols
    rb = rows if rows % _TOEPLITZ_ROWS else _TOEPLITZ_ROWS
    assert rows % rb == 0 and n % LANES == 0
    idx = jnp.arange(n, dtype=jnp.int32)
    vec = fn(jnp.where(idx < cols, idx, idx - n)).astype(F32)
    vecs = jnp.stack([jnp.roll(vec, k * rb, axis=1) for k in range(rows // rb)], axis=1)
    return pl.pallas_call(
        functools.partial(_toeplitz_kernel, keep=keep),
        grid=(heads, rows // rb),
        in_specs=[pl.BlockSpec((1, 1, 1, n), lambda h, k: (h, k, 0, 0))],
        out_specs=pl.BlockSpec((variants, 1, rb, cols), lambda h, k: (0, h, k, 0)),
        out_shape=jax.ShapeDtypeStruct((variants, heads, rows, cols), F32),
        name="toeplitz_tiles",
    )(vecs.reshape(heads, rows // rb, 1, n))


def _bias_stats(tiles):
    finite = tiles > 0.5 * NEG_INF
    bias_max = jnp.maximum(jnp.max(jnp.where(finite, tiles, NEG_INF), axis=(1, 2)), 0.0)
    bias_min = jnp.minimum(jnp.min(jnp.where(finite, tiles, -NEG_INF), axis=(1, 2)), 0.0)
    return jnp.broadcast_to(jnp.stack([bias_max, bias_max - bias_min], axis=1)[:, :, None],
                            (tiles.shape[0], 2, LANES))


def _t5_bucket(rel):
    nb = NUM_BUCKETS // 2
    max_exact = nb // 2
    bucket = jnp.where(rel > 0, nb, 0)
    n = jnp.abs(rel)
    nf = jnp.maximum(n, 1).astype(F32)
    large = max_exact + (jnp.log(nf / max_exact) / math.log(MAX_DISTANCE / max_exact)
                         * (nb - max_exact)).astype(jnp.int32)
    large = jnp.minimum(large, nb - 1)
    return bucket + jnp.where(n < max_exact, n, large)


def _diff_bias_tiles(t5_table):
    blk = BLK_A
    table = t5_table.astype(F32)
    far = table[_t5_bucket(jnp.full((), -(blk + 1), jnp.int32))]
    def visible(r, c, variant):
        return jnp.floor_divide(r - blk, CHUNK) <= jnp.floor_divide(c, CHUNK)

    tiles = _toeplitz_tiles(lambda x: ((table[_t5_bucket(-x - blk)] - far) * LOG2E).T, visible,
                            N_HEADS_A, 2 * blk, blk)
    return tiles.reshape(N_HEADS_A, 2, blk, blk)


def _diff_attention(proj, t5_table, lam, subln_g, lam_init):
    seq = proj.shape[0]
    blk = BLK_A
    bias = _diff_bias_tiles(t5_table)
    ha = N_HEADS_A
    bstat = _bias_stats(bias.reshape(ha, 2 * blk, blk))
    kern = functools.partial(_diffattn_kernel, out_scale=1.0 - lam_init)
    return pl.pallas_call(
        kern,
        grid=(ha, seq // blk),
        in_specs=[
            pl.BlockSpec((seq, DV_A), lambda h, i: (0, h)),
            pl.BlockSpec((seq, DV_A), lambda h, i: (0, ha + h)),
            pl.BlockSpec((seq, DV_A), lambda h, i: (0, 2 * ha + h)),
            pl.BlockSpec((1, 2, blk, blk), lambda h, i: (h, 0, 0, 0)),
            pl.BlockSpec((1, 2, LANES), lambda h, i: (h, 0, 0)),
            pl.BlockSpec((1, DV_A), lambda h, i: (0, 0)),
            pl.BlockSpec((1, DV_A), lambda h, i: (0, 0)),
        ],
        out_specs=pl.BlockSpec((blk, DV_A), lambda h, i: (i, h)),
        out_shape=jax.ShapeDtypeStruct((seq, ha * DV_A), BF16),
        scratch_shapes=[
            pltpu.SMEM((1,), jnp.int32),
            pltpu.VMEM((DV_A, 2 * seq), BF16),
            pltpu.VMEM((DV_A + ONES_A, seq), BF16),
            pltpu.VMEM((SUBLANES, LANES), F32),
            pltpu.VMEM((SUBLANES, 2 * seq), F32),
            pltpu.VMEM((SUBLANES, 2 * blk), F32),
            pltpu.VMEM((DV_A + ONES_A, 2 * blk), F32),
        ] + 2 * ([pltpu.VMEM((blk, 2 * blk // NPART_A + SCORE_PAD), F32)] * NPART_A
                 + [pltpu.VMEM((SUBLANES, 2 * blk // NPART_A), F32)] * NPART_A)
        + 2 * [pltpu.VMEM((blk, 2 * blk // NPART_A + SCORE_PAD), BF16)] * NPART_A,
        compiler_params=pltpu.CompilerParams(dimension_semantics=("parallel", "arbitrary")),
        name="diff_attention",
    )(proj, proj, proj, bias, bstat, jnp.full((1, DV_A), lam, F32), subln_g.reshape(1, DV_A).astype(F32))


def _band_kernel(q_ref, kp_ref, kc_ref, vp_ref, vc_ref, *refs):
    qw, band = QW_B, BAND_B
    nbias = band // qw + 1
    bias_refs, bstat_ref, o_ref = refs[:nbias], refs[nbias], refs[nbias + 1]
    p_refs = refs[nbias + 2:]
    ngroups = len(p_refs)
    nk = band + qw
    sub = SUBLANES
    q = (q_ref[...].astype(F32) * (DH_B ** -0.5 * LOG2E)).astype(BF16).astype(F32)
    lane = lax.broadcasted_iota(jnp.int32, q.shape, 1)
    qt = (jnp.where(lane < DH_B, q, 0.0).T, jnp.where(lane >= DH_B, q, 0.0).T)
    qh = (qt[0].astype(BF16), qt[1].astype(BF16))
    k_all = jnp.concatenate([kp_ref[...], kc_ref[...]], axis=0)
    vt_all = jnp.concatenate([vp_ref[...], vc_ref[...]], axis=0).astype(F32).T.astype(BF16)

    def group_operands(g):
        k0 = g * qw
        qs = jnp.concatenate([qh[0][:, k0:k0 + qw], qh[1][:, k0:k0 + qw]], axis=1)
        bias_ref = bias_refs[min(g, nbias - 1)]
        bias = jnp.concatenate([bias_ref[0, 0], bias_ref[0, 1]], axis=1)
        return k0, qs, bias

    def store_group(g, ot):
        o = jnp.concatenate([ot[0:DH_B, 0:qw], ot[DH_B:2 * DH_B, qw:2 * qw]], axis=0)
        o_ref[g * qw:(g + 1) * qw, :] = o.T.astype(o_ref.dtype)

    same_head = (lax.broadcasted_iota(jnp.int32, (LANES, LANES), 0) // DH_B
                 == lax.broadcasted_iota(jnp.int32, (LANES, LANES), 1) // DH_B).astype(BF16)
    kf = k_all.astype(F32)
    kn2 = _dot((kf * kf).astype(BF16), same_head)
    kmax2 = jnp.max(jnp.max(kn2.reshape(kn2.shape[0] // sub, sub, LANES), axis=0), axis=0, keepdims=True)
    shifts, worst = [], None
    for m in range(2):
        qn2 = jnp.sum(qt[m] * qt[m], axis=0, keepdims=True)
        bound = jnp.sqrt(qn2 * kmax2[:, m * DH_B:m * DH_B + 1]) * 1.03
        shifts.append(bound + bstat_ref[m, 0:1, 0:1])
        spread = jnp.max(2.0 * bound + bstat_ref[m, 1:2, 0:1])
        worst = spread if worst is None else jnp.maximum(worst, spread)
    fixed_shift = worst < SINGLE_PASS_LOG2_RANGE

    @pl.when(fixed_shift)
    def _():
        sums = []
        for g in range(ngroups):
            k0, qs, bias = group_operands(g)
            r = jnp.concatenate([shifts[0][:, k0:k0 + qw], shifts[1][:, k0:k0 + qw]], axis=1)
            s = _dot(k_all[k0:k0 + nk], qs) + (bias - r)
            p = jnp.exp2(s).reshape(nk // sub, sub, 2 * qw)
            sums.append(jnp.sum(jnp.sum(p, axis=0), axis=0, keepdims=True))
            p_refs[g][:, 0:2 * qw] = p.reshape(nk, 2 * qw).astype(BF16)
        for g in range(ngroups):
            k0 = g * qw
            store_group(g, _dot(vt_all[:, k0:k0 + nk], p_refs[g][:, 0:2 * qw]) / sums[g])

    @pl.when(jnp.logical_not(fixed_shift))
    def _():
        vt_ones = jnp.concatenate([vt_all, jnp.ones((ONES_A, vt_all.shape[1]), BF16)], axis=0)
        for g in range(ngroups):
            k0, qs, bias = group_operands(g)
            s = (_dot(k_all[k0:k0 + nk], qs) + bias).reshape(nk // sub, sub, 2 * qw)
            m = jnp.max(jnp.max(s, axis=0), axis=0, keepdims=True)
            p = jnp.exp2(s - m[None])
            pv = _dot(vt_ones[:, k0:k0 + nk], p.reshape(nk, 2 * qw).astype(BF16))
            store_group(g, pv[0:2 * DH_B] / pv[2 * DH_B:2 * DH_B + 1])


def _band_bias_tiles(rel_bias):
    band = BAND_B

    def valid(r, c, variant):
        qchunk = jnp.floor_divide(c, CHUNK)
        kchunk = jnp.floor_divide(r - band, CHUNK)
        missing = jnp.where(variant == 0, 0, band - (variant - 1) * QW_B)
        return (kchunk <= qchunk) & (kchunk >= qchunk - LEFT_CHUNKS) & (r >= missing)

    return _toeplitz_tiles(
        lambda x: rel_bias.astype(F32)[:, jnp.clip(-x - band, -REL_CLIP, REL_CLIP) + REL_CLIP] * LOG2E, valid,
        N_HEADS_B, band + QW_B, QW_B, variants=1 + band // QW_B)


def _band_attention(proj, rel_bias):
    seq = proj.shape[0]
    blk, band, qw = BLK_B, BAND_B, QW_B
    bias = _band_bias_tiles(rel_bias)
    npair = N_HEADS_B // 2
    qc0 = 3 * N_HEADS_A
    per = blk // band
    prev = lambda c0: (lambda hp, i: (jnp.maximum(i * per - 1, 0), c0 + hp))
    cur = lambda c0: (lambda hp, i: (i, c0 + hp))
    return pl.pallas_call(
        _band_kernel,
        grid=(npair, seq // blk),
        in_specs=[
            pl.BlockSpec((blk, LANES), cur(qc0)),
            pl.BlockSpec((band, LANES), prev(qc0 + npair)),
            pl.BlockSpec((blk, LANES), cur(qc0 + npair)),
            pl.BlockSpec((band, LANES), prev(qc0 + 2 * npair)),
            pl.BlockSpec((blk, LANES), cur(qc0 + 2 * npair)),
        ] + [
            pl.BlockSpec((1, 2, band + qw, qw), (lambda hp, i, t=t: (jnp.where(i == 0, 1 + t, 0), hp, 0, 0)))
            for t in range(band // qw)
        ] + [
            pl.BlockSpec((1, 2, band + qw, qw), lambda hp, i: (0, hp, 0, 0)),
            pl.BlockSpec((2, 2, LANES), lambda hp, i: (hp, 0, 0)),
        ],
        out_specs=pl.BlockSpec((blk, LANES), lambda hp, i: (i, hp)),
        out_shape=jax.ShapeDtypeStruct((seq, N_HEADS_B * DH_B), BF16),
        scratch_shapes=[pltpu.VMEM((band + qw, 2 * qw + SCORE_PAD), BF16)] * (blk // qw),
        compiler_params=pltpu.CompilerParams(dimension_semantics=("parallel", "arbitrary")),
        name="band_attention",
    )(proj, proj, proj, proj, proj, *([bias] * bias.shape[0]), _bias_stats(bias[0]))


def _retention_kernel(qk_ref, v_ref, gate_ref, cos_ref, sin_ref, qdec_ref, kdec_ref, dmat_ref,
                      sdec_ref, o_ref, state_ref):
    @pl.when(pl.program_id(0) == 0)
    def _():
        state_ref[...] = jnp.zeros(state_ref.shape, F32)

    cos = cos_ref[...]
    sin = sin_ref[...]
    lane = lax.broadcasted_iota(jnp.int32, cos.shape, 1)
    first_half = (lane % DQK_C) < (DQK_C // 2)
    qk = qk_ref[...]
    parts = []
    for j in range(qk.shape[1] // LANES):
        t = qk[:, j * LANES:(j + 1) * LANES]
        partner = jnp.where(first_half, pltpu.roll(t, LANES - DQK_C // 2, 1), pltpu.roll(t, DQK_C // 2, 1))
        parts.append(t * cos + partner * sin)
    wq = N_HEADS_C * DQK_C
    q = jnp.concatenate(parts[:wq // LANES], axis=1)
    k = jnp.concatenate(parts[wq // LANES:], axis=1) * (DQK_C ** -0.5)
    qd = (q * qdec_ref[...]).astype(BF16)
    kd = (k * kdec_ref[...]).astype(BF16)
    qb = q.astype(BF16)
    kb = k.astype(BF16)
    vb = v_ref[...].astype(BF16)
    gate = gate_ref[...]
    outs = []
    for h in range(N_HEADS_C):
        qs = slice(h * DQK_C, (h + 1) * DQK_C)
        vs = slice(h * DV_C, (h + 1) * DV_C)
        scores = _dot_nt(qb[:, qs], kb[:, qs]) * dmat_ref[h]
        state = state_ref[h]
        r = _dot(scores.astype(BF16), vb[:, vs]) + _dot(qd[:, qs], state.astype(BF16))
        state_ref[h] = state * sdec_ref[h] + _dot_tn(kd[:, qs], vb[:, vs])
        r = r * lax.rsqrt(jnp.mean(r * r, axis=-1, keepdims=True) + EPS)
        g = gate[:, vs]
        outs.append(r * (g * jax.nn.sigmoid(g)))
    o_ref[...] = jnp.concatenate(outs, axis=1).astype(o_ref.dtype)


def _retention_tables(seq):
    t = BLK_C
    half = DQK_C // 2
    inv_freq = 1.0 / np.power(ROPE_BASE, np.arange(0, DQK_C, 2, dtype=np.float64) / DQK_C)
    ang = np.arange(seq, dtype=np.float64)[:, None] * inv_freq[None, :]
    reps = LANES // half
    cos = np.tile(np.cos(ang), (1, reps))
    sign = np.where((np.arange(LANES) % DQK_C) < half, -1.0, 1.0)
    sin = np.tile(np.sin(ang), (1, reps)) * sign[None, :]
    log_g = np.log(1.0 - np.power(2.0, -5.0 - np.arange(N_HEADS_C, dtype=np.float64)))
    pos = np.arange(t, dtype=np.float64)
    diff = pos[:, None] - pos[None, :]
    same_or_past = (np.arange(t)[None, :] // CHUNK) <= (np.arange(t)[:, None] // CHUNK)
    dmat = np.where(same_or_past[None], np.exp(log_g[:, None, None] * np.abs(diff)[None]), 0.0)
    qdec = np.repeat(np.exp(log_g[None, :] * (pos[:, None] + 1.0)), DQK_C, axis=1)
    kdec = np.repeat(np.exp(log_g[None, :] * (t - 1.0 - pos[:, None])), DQK_C, axis=1)
    sdec = np.broadcast_to(np.exp(log_g * t)[:, None, None], (N_HEADS_C, 1, DV_C))
    return tuple(jnp.asarray(a.astype(np.float32)) for a in (cos, sin, qdec, kdec, dmat, sdec))


def _retention(proj):
    seq = proj.shape[0]
    t = BLK_C
    cos, sin, qdec, kdec, dmat, sdec = _retention_tables(seq)
    wv = N_HEADS_C * DV_C
    return pl.pallas_call(
        _retention_kernel,
        grid=(seq // t,),
        in_specs=[
            pl.BlockSpec((t, wv), lambda i: (i, 0)),
            pl.BlockSpec((t, wv), lambda i: (i, 1)),
            pl.BlockSpec((t, wv), lambda i: (i, 2)),
            pl.BlockSpec((t, LANES), lambda i: (i, 0)),
            pl.BlockSpec((t, LANES), lambda i: (i, 0)),
            pl.BlockSpec((t, N_HEADS_C * DQK_C), lambda i: (0, 0)),
            pl.BlockSpec((t, N_HEADS_C * DQK_C), lambda i: (0, 0)),
            pl.BlockSpec((N_HEADS_C, t, t), lambda i: (0, 0, 0)),
            pl.BlockSpec((N_HEADS_C, 1, DV_C), lambda i: (0, 0, 0)),
        ],
        out_specs=pl.BlockSpec((t, wv), lambda i: (i, 0)),
        out_shape=jax.ShapeDtypeStruct((seq, wv), BF16),
        scratch_shapes=[pltpu.VMEM((N_HEADS_C, DQK_C, DV_C), F32)],
        compiler_params=pltpu.CompilerParams(dimension_semantics=("arbitrary",)),
        name="retention",
    )(proj, proj, proj, cos, sin, qdec, kdec, dmat, sdec)


def _s5_kernel(*refs):
    ncb = S5_CH // LANES
    u_refs = refs[:ncb]
    (mt_ref, bt_ref, ctr_ref, cti_ref, are_ref, aim_ref, y_ref,
     ut_ref, yt_ref, ys_ref, vr_ref, vi_ref, spr_ref, spi_ref, carry_ref) = refs[ncb:]
    tc = S5_TC
    gp = S5_GROUP
    n = S5_STATE
    ng = S5_GROUPS

    @pl.when(pl.program_id(0) == 0)
    def _():
        carry_ref[...] = jnp.zeros(carry_ref.shape, F32)

    for s in range(S5_T):
        for k in range(ncb):
            ut_ref[s, k * LANES:(k + 1) * LANES, :] = u_refs[k][pl.ds(s, tc, stride=S5_T), :].T

    unroll = 8

    def intra(it, carry):
        for k in range(unroll):
            g = it * unroll + k
            r0 = pl.multiple_of(g * gp, gp)
            ug = ut_ref[:, pl.ds(r0, gp), :].reshape(S5_T * gp, tc).astype(BF16)
            yt_ref[:, pl.ds(r0, gp), :] = _dot(mt_ref[g], ug).reshape(S5_T, gp, tc)
            vt = _dot(bt_ref[g], ug)
            n0 = pl.multiple_of(g * n, n)
            vr_ref[pl.ds(n0, n), :] = vt[0:n]
            vi_ref[pl.ds(n0, n), :] = vt[n:2 * n]
        return carry

    lax.fori_loop(0, ng // unroll, intra, 0)

    sub = SUBLANES
    nv = tc // sub
    row = lax.broadcasted_iota(jnp.int32, (tc, LANES), 0)
    in_vreg = lax.rem(row, sub)

    def rows_of(v, r):
        return jnp.broadcast_to(v[r:r + 1], (tc, LANES))

    for j in range(ng * n // LANES):
        cols = slice(j * LANES, (j + 1) * LANES)
        pwr, pwi = are_ref[:, cols], aim_ref[:, cols]
        xr = vr_ref[cols, :].T
        xi = vi_ref[cols, :].T
        for d in (1, 2, 4):
            keep = in_vreg >= d
            sr = jnp.where(keep, pltpu.roll(xr, d, 0), 0.0)
            si = jnp.where(keep, pltpu.roll(xi, d, 0), 0.0)
            fr, fi = rows_of(pwr, d - 1), rows_of(pwi, d - 1)
            xr, xi = xr + (fr * sr - fi * si), xi + (fr * si + fi * sr)
        cr, ci = carry_ref[0, :, cols], carry_ref[1, :, cols]
        cr0, ci0 = cr, ci
        outr, outi = [], []
        for v in range(nv):
            yr = xr[v * sub:(v + 1) * sub] + (pwr * cr - pwi * ci)
            yi = xi[v * sub:(v + 1) * sub] + (pwr * ci + pwi * cr)
            outr.append(yr)
            outi.append(yi)
            cr = jnp.broadcast_to(yr[sub - 1:sub], (sub, LANES))
            ci = jnp.broadcast_to(yi[sub - 1:sub], (sub, LANES))
        carry_ref[0, :, cols] = cr
        carry_ref[1, :, cols] = ci
        sr = jnp.concatenate(outr, axis=0)
        si = jnp.concatenate(outi, axis=0)
        first = row == 0
        spr_ref[j] = jnp.where(first, rows_of(cr0, 0), pltpu.roll(sr, 1, 0))
        spi_ref[j] = jnp.where(first, rows_of(ci0, 0), pltpu.roll(si, 1, 0))

    def cross(it, carry):
        for k in range(unroll):
            jp = it * unroll + k
            r0 = pl.multiple_of(jp * 2 * gp, 2 * gp)
            yc = (_dot_nt(ctr_ref[jp], spr_ref[jp].astype(BF16))
                  + _dot_nt(cti_ref[jp], spi_ref[jp].astype(BF16)))
            yt_ref[:, pl.ds(r0, 2 * gp), :] += yc.reshape(S5_T, 2 * gp, tc)
        return carry

    lax.fori_loop(0, ng // 2 // unroll, cross, 0)

    for s in range(S5_T):
        for k in range(ncb):
            ys_ref[k, pl.ds(s, tc, stride=S5_T), :] = yt_ref[s, k * LANES:(k + 1) * LANES, :].T
    for k in range(ncb):
        y_ref[:, k * LANES:(k + 1) * LANES] = ys_ref[k]


def _s5_matrices(lam_re, lam_im, log_step, b_re, b_im, c_re, c_im, d_skip):
    hi = lax.Precision.HIGHEST
    t, gp, n, ng = S5_T, S5_GROUP, S5_STATE, S5_GROUPS
    lam = lax.complex(lam_re.astype(F32), lam_im.astype(F32))
    step = jnp.exp(log_step.astype(F32))[:, None]
    ls = lam * step
    a_bar = jnp.exp(ls)
    b_bar = ((a_bar - 1.0) / lam)[..., None] * lax.complex(b_re.astype(F32), b_im.astype(F32))
    cm = lax.complex(c_re.astype(F32), c_im.astype(F32))

    def apow(k):
        kk = k.astype(F32).astype(jnp.complex64)
        return jnp.exp(ls.reshape((ng,) + (1,) * k.ndim + (n,)) * kk[None, ..., None])

    tt = jnp.arange(t)
    kmat = jnp.einsum('gpn,gln,gnq->glpq', cm, apow(tt), b_bar, precision=hi).real
    krev = jnp.transpose(kmat[:, ::-1], (0, 2, 1, 3)).reshape(ng, gp, t * gp)
    kpad = jnp.pad(krev, ((0, 0), (0, 0), (0, t * gp)))
    mt = jnp.concatenate([kpad[:, :, (t - 1 - to) * gp:(2 * t - 1 - to) * gp] for to in range(t)], axis=1)
    dvec = jnp.tile(d_skip.astype(F32).reshape(ng, 1, gp), (1, t, 1)).reshape(ng, t * gp)
    mt = mt + jnp.eye(t * gp, dtype=F32)[None] * dvec[:, :, None]
    z = jnp.swapaxes(apow(t - 1 - tt), 1, 2)[:, :, :, None] * b_bar[:, :, None, :]
    z = z.reshape(ng, n, t * gp)
    bt = jnp.concatenate([z.real, z.imag], axis=1)
    w = cm[:, None, :, :] * apow(tt + 1)[:, :, None, :]

    def pair_readout(x):
        x = x.reshape(ng // 2, 2, t, gp, n)
        first = jnp.pad(x[:, 0], ((0, 0), (0, 0), (0, 0), (0, n)))
        second = jnp.pad(x[:, 1], ((0, 0), (0, 0), (0, 0), (n, 0)))
        return jnp.stack([first, second], axis=2).reshape(ng // 2, t * 2 * gp, 2 * n).astype(BF16)

    ctr, cti = pair_readout(w.real), pair_readout(-w.imag)
    a_chunk = jnp.transpose(apow(t * (jnp.arange(SUBLANES) + 1)), (1, 0, 2)).reshape(SUBLANES, ng * n)
    return mt.astype(BF16), bt.astype(BF16), ctr, cti, a_chunk.real, a_chunk.imag


def _s5(proj, mats):
    seq, width = proj.shape
    t, tc, gp, n, ng = S5_T, S5_TC, S5_GROUP, S5_STATE, S5_GROUPS
    rows = t * tc
    ncb = S5_CH // LANES
    cb0 = (width - S5_CH) // LANES
    u_specs = [pl.BlockSpec((rows, LANES), (lambda i, k=k: (i, cb0 + k))) for k in range(ncb)]
    nsb = ng * n // LANES
    return pl.pallas_call(
        _s5_kernel,
        grid=(seq // rows,),
        in_specs=u_specs + [_const_spec(m.shape) for m in mats],
        out_specs=pl.BlockSpec((rows, S5_CH), lambda i: (i, 0)),
        out_shape=jax.ShapeDtypeStruct((seq, S5_CH), F32),
        scratch_shapes=[
            pltpu.VMEM((t, S5_CH, tc), F32),
            pltpu.VMEM((t, S5_CH, tc), F32),
            pltpu.VMEM((ncb, rows, LANES), F32),
            pltpu.VMEM((ng * n, tc), F32),
            pltpu.VMEM((ng * n, tc), F32),
            pltpu.VMEM((nsb, tc, LANES), F32),
            pltpu.VMEM((nsb, tc, LANES), F32),
            pltpu.VMEM((2, SUBLANES, ng * n), F32),
        ],
        compiler_params=pltpu.CompilerParams(dimension_semantics=("arbitrary",)),
        name="s5_scan",
    )(*([proj] * ncb), *mats)


def _mix_ffn_kernel(*refs, glu, final):
    (x_ref, a_ref, b_ref, wo_ref, g1_ref), refs = refs[:5], refs[5:]
    if glu:
        gw_ref, refs = refs[0], refs[1:]
    (g_ref, sc_ref, sh_ref, gate_ref, win_ref, cw_ref, cb_ref, wout_ref), refs = refs[:8], refs[8:]
    if final:
        fg_ref, o_ref, h_ref, act_ref, gbuf_ref, carry_ref = refs
    else:
        ng_ref, nsc_ref, nsh_ref, o_ref, hn_ref, h_ref, act_ref, gbuf_ref, carry_ref = refs
    tm = x_ref.shape[0]
    halo = gbuf_ref.shape[0] - tm

    @pl.when(pl.program_id(0) == 0)
    def _():
        carry_ref[...] = jnp.zeros(carry_ref.shape, F32)

    if glu:
        y = jax.nn.gelu(b_ref[...]).astype(BF16)
        gg = _dot(y, gw_ref[...])
        half = gg.shape[1] // 2
        b = (gg[:, :half] * jax.nn.sigmoid(gg[:, half:])).astype(BF16)
    else:
        b = b_ref[...]
    cat = jnp.concatenate([a_ref[...], b], axis=1)
    x = x_ref[...] + g1_ref[...] * _dot(cat, wo_ref[...])
    h_ref[...] = _mod_rmsnorm(x, g_ref[...], sc_ref[...], sh_ref[...]).astype(BF16)
    for f in range(D_FF // TF_FFN):
        cs = slice(f * TF_FFN, (f + 1) * TF_FFN)
        gs = slice(D_FF + f * TF_FFN, D_FF + (f + 1) * TF_FFN)
        h = h_ref[...]
        val = _dot(h, win_ref[:, cs])
        gate = _dot(h, win_ref[:, gs])
        gbuf_ref[0:halo, :] = carry_ref[:, cs]
        gbuf_ref[halo:halo + tm, :] = gate
        carry_ref[:, cs] = gate[tm - halo:tm, :]
        conv = (gate * cw_ref[2:3, cs] + gbuf_ref[halo - 1:halo - 1 + tm, :] * cw_ref[1:2, cs]
                + gbuf_ref[halo - 2:halo - 2 + tm, :] * cw_ref[0:1, cs] + cb_ref[:, cs])
        act_ref[:, cs] = (jax.nn.gelu(conv) * val).astype(BF16)
    xn = x + gate_ref[...] * _dot(act_ref[...], wout_ref[...])
    if final:
        xn = xn * lax.rsqrt(jnp.mean(xn * xn, axis=-1, keepdims=True) + EPS) * fg_ref[...]
    else:
        hn_ref[...] = _mod_rmsnorm(xn, ng_ref[...], nsc_ref[...], nsh_ref[...]).astype(BF16)
    o_ref[...] = xn


def _layer_spec(shape, layer):
    idx = (layer,) + (0,) * (len(shape) - 1)
    return pl.BlockSpec((None,) + tuple(shape[1:]), lambda *_: idx, pipeline_mode=pl.Buffered(1))


def _mix_ffn(x, a, b, wo, gate1, glu_w, g, scale, shift, gate2, w_in, conv_w, conv_b, w_out, tail, layer):
    seq, d = x.shape
    final = len(tail) == 1
    tm = TM_FFN
    halo = SUBLANES
    row = pl.BlockSpec((1, d), lambda i: (0, 0))
    rows = lambda w: pl.BlockSpec((tm, w), lambda i: (i, 0))
    conv_b = conv_b.reshape(conv_b.shape[0], 1, D_FF)
    in_specs = [rows(d), rows(a.shape[1]), rows(b.shape[1]), _const_spec(wo.shape), row]
    args = [x, a, b, wo, gate1]
    if glu_w is not None:
        in_specs.append(_const_spec(glu_w.shape))
        args.append(glu_w)
    in_specs += [
        row, row, row, row,
        _layer_spec(w_in.shape, layer),
        _layer_spec(conv_w.shape, layer),
        _layer_spec(conv_b.shape, layer),
        _layer_spec(w_out.shape, layer),
    ] + [row] * len(tail)
    args += [g.reshape(1, d), scale, shift, gate2, w_in, conv_w, conv_b, w_out]
    args += [t.reshape(1, d) for t in tail]
    out_specs = [rows(d)] if final else [rows(d), rows(d)]
    out_shape = [jax.ShapeDtypeStruct((seq, d), F32)] + ([] if final else [jax.ShapeDtypeStruct((seq, d), BF16)])
    return pl.pallas_call(
        functools.partial(_mix_ffn_kernel, glu=glu_w is not None, final=final),
        grid=(seq // tm,),
        in_specs=in_specs,
        out_specs=out_specs,
        out_shape=out_shape,
        scratch_shapes=[
            pltpu.VMEM((tm, d), BF16),
            pltpu.VMEM((tm, D_FF), BF16),
            pltpu.VMEM((tm + halo, TF_FFN), F32),
            pltpu.VMEM((halo, D_FF), F32),
        ],
        compiler_params=pltpu.CompilerParams(dimension_semantics=("arbitrary",)),
        name="mix_ffn",
    )(*args)


def kernel(x, c, t5_table, mod_w, mod_b, norm1_g, norm2_g, ffn_w_in, ffn_conv_w, ffn_conv_b, ffn_w_out,
           ev_w_in, ev_w_out, diff_lambda, diff_subln_g, band_rel_bias,
           od_w_in, od_w_out, s5_lam_re, s5_lam_im, s5_log_step, s5_b_re, s5_b_im, s5_c_re, s5_c_im,
           s5_d, s5_glu_w, final_g):
    assert x.shape[0] == 1 and x.shape[2] == D_MODEL
    seq = x.shape[1]
    assert seq % TM_PROJ == 0 and seq % (S5_T * S5_TC) == 0
    d = D_MODEL
    xs = x[0]
    mod = _modulation(c, mod_w, mod_b)
    ffn_w_in_b = ffn_w_in.astype(BF16)
    ffn_w_out_b = ffn_w_out.astype(BF16)
    mods = [[mod[i, :, k * d:(k + 1) * d] for k in range(6)] for i in range(DEPTH)]
    h = None
    for i in range(DEPTH):
        sh1, sc1, g1, sh2, sc2, g2 = mods[i]
        w_in = (ev_w_in if i % 2 == 0 else od_w_in)[i // 2].astype(BF16)
        proj_dtype = BF16 if i % 2 == 0 else F32
        if h is None:
            proj = _normproj(xs, norm1_g[i], sc1, sh1, w_in, proj_dtype)
        else:
            proj = _proj(h, w_in, proj_dtype)
        if i % 2 == 0:
            e = i // 2
            lam_init = 0.8 - 0.6 * math.exp(-0.3 * i)
            lp = diff_lambda[e].astype(F32)
            lam = jnp.exp(jnp.sum(lp[0] * lp[1])) - jnp.exp(jnp.sum(lp[2] * lp[3])) + lam_init
            mix_a = _diff_attention(proj, t5_table, lam, diff_subln_g[e], lam_init)
            mix_b = _band_attention(proj, band_rel_bias[e])
            wo, glu_w = ev_w_out[e].astype(BF16), None
        else:
            o = i // 2
            mix_a = _retention(proj)
            mats = _s5_matrices(s5_lam_re[o], s5_lam_im[o], s5_log_step[o], s5_b_re[o], s5_b_im[o],
                                s5_c_re[o], s5_c_im[o], s5_d[o])
            mix_b = _s5(proj, mats)
            wo, glu_w = od_w_out[o].astype(BF16), s5_glu_w[o].astype(BF16)
        if i == DEPTH - 1:
            tail = (final_g,)
        else:
            nsh1, nsc1 = mods[i + 1][0], mods[i + 1][1]
            tail = (norm1_g[i + 1], nsc1, nsh1)
        out = _mix_ffn(xs, mix_a, mix_b, wo, g1, glu_w, norm2_g[i], sc2, sh2, g2,
                       ffn_w_in_b, ffn_conv_w, ffn_conv_b, ffn_w_out_b, tail, layer=i)
        if i == DEPTH - 1:
            xs = out[0]
        else:
            xs, h = out
    return xs[None]
```

```python
import functools
import math

import jax
import jax.numpy as jnp
import numpy as np
from jax import lax
from jax.experimental import pallas as pl
from jax.experimental.pallas import tpu as pltpu

F32 = jnp.float32
BF16 = jnp.bfloat16

D_MODEL = 1024
DEPTH = 2
CHUNK = 64
GROUP_WIDTH = D_MODEL // 2
DK_A = 64
DV_A = 2 * DK_A
N_HEADS_A = GROUP_WIDTH // DV_A
DH_B = 64
N_HEADS_B = GROUP_WIDTH // DH_B
LEFT_CHUNKS = 8
REL_CLIP = 2 * CHUNK
NUM_BUCKETS = 32
MAX_DISTANCE = 128
DV_C = 128
DQK_C = DV_C // 2
N_HEADS_C = GROUP_WIDTH // DV_C
ROPE_BASE = 10000.0
S5_CH = GROUP_WIDTH
S5_GROUP = 16
S5_GROUPS = S5_CH // S5_GROUP
S5_STATE = 64
D_FF = ((8 * D_MODEL // 3 + 255) // 256) * 256
CONV_W = 3
EVEN_IN = 3 * N_HEADS_A * DV_A + 3 * N_HEADS_B * DH_B
ODD_IN = 2 * N_HEADS_C * DQK_C + 2 * N_HEADS_C * DV_C + S5_CH
EPS = 1e-6
NEG_INF = -1e30
LOG2E = math.log2(math.e)

LANES = 128
SUBLANES = 8
MXU_DIM = 256

TM_PROJ = 1024
TN_PROJ = 1024
TN_MOD = 1536
TM_FFN = 512
TF_FFN = MXU_DIM
BLK_A = 512
NPART_A = 2
ONES_A = 16
SINGLE_PASS_LOG2_RANGE = 96.0
SCORE_PAD = LANES
BLK_B = 4096
BAND_B = LEFT_CHUNKS * CHUNK
QW_B = 4 * CHUNK
BLK_C = 512
S5_T = 16
S5_TC = LANES

assert BLK_B % BAND_B == 0 and BLK_B % QW_B == 0 and BAND_B % QW_B == 0
assert BLK_A >= MAX_DISTANCE, "far key blocks must sit in the saturated T5 bucket"
assert DV_A == LANES and 2 * DK_A == LANES and 2 * DH_B == LANES, "attention heads are read as 128-lane column blocks"


def _dot(a, b):
    return jnp.dot(a, b, preferred_element_type=F32)


def _dot_nt(a, b):
    return lax.dot_general(a, b, (((1,), (1,)), ((), ())), preferred_element_type=F32)


def _dot_tn(a, b):
    return lax.dot_general(a, b, (((0,), (0,)), ((), ())), preferred_element_type=F32)


def _const_spec(shape):
    zeros = (0,) * len(shape)
    return pl.BlockSpec(shape, lambda *_: zeros, pipeline_mode=pl.Buffered(1))


def _mod_rmsnorm(x, g, scale, shift):
    y = x * lax.rsqrt(jnp.mean(x * x, axis=-1, keepdims=True) + EPS)
    y = y * g
    return y * (1.0 + scale) + shift


def _mod_kernel(c_ref, w_ref, b_ref, o_ref):
    c = c_ref[...]
    cond = c * jax.nn.sigmoid(c)
    o_ref[0] = jnp.sum(cond * w_ref[0], axis=0, keepdims=True) + b_ref[0]


def _modulation(c, mod_w, mod_b):
    depth, d, n = mod_w.shape
    tn = TN_MOD
    return pl.pallas_call(
        _mod_kernel,
        grid=(depth, n // tn),
        in_specs=[
            pl.BlockSpec((d, 1), lambda i, j: (0, 0)),
            pl.BlockSpec((1, d, tn), lambda i, j: (i, 0, j)),
            pl.BlockSpec((1, 1, tn), lambda i, j: (i, 0, j)),
        ],
        out_specs=pl.BlockSpec((1, 1, tn), lambda i, j: (i, 0, j)),
        out_shape=jax.ShapeDtypeStruct((depth, 1, n), F32),
        name="modulation",
    )(c.reshape(d, 1), mod_w, mod_b.reshape(depth, 1, n))


def _normproj_kernel(x_ref, g_ref, sc_ref, sh_ref, w_ref, o_ref):
    tm, n = o_ref.shape
    half = tm // 2
    for r in range(2):
        rows = slice(r * half, (r + 1) * half)
        h = _mod_rmsnorm(x_ref[rows, :], g_ref[...], sc_ref[...], sh_ref[...]).astype(BF16)
        for j in range(n // TN_PROJ):
            cols = slice(j * TN_PROJ, (j + 1) * TN_PROJ)
            o_ref[rows, cols] = _dot(h, w_ref[:, cols]).astype(o_ref.dtype)


def _normproj(x, g, scale, shift, w, out_dtype):
    seq, d = x.shape
    n = w.shape[1]
    tm = TM_PROJ
    row = pl.BlockSpec((1, d), lambda i: (0, 0))
    return pl.pallas_call(
        _normproj_kernel,
        grid=(seq // tm,),
        in_specs=[pl.BlockSpec((tm, d), lambda i: (i, 0)), row, row, row, _const_spec(w.shape)],
        out_specs=pl.BlockSpec((tm, n), lambda i: (i, 0)),
        out_shape=jax.ShapeDtypeStruct((seq, n), out_dtype),
        compiler_params=pltpu.CompilerParams(dimension_semantics=("parallel",)),
        name="normproj",
    )(x, g.reshape(1, d), scale, shift, w)


def _proj_kernel(h_ref, w_ref, o_ref):
    for j in range(o_ref.shape[1] // TN_PROJ):
        cols = slice(j * TN_PROJ, (j + 1) * TN_PROJ)
        o_ref[:, cols] = _dot(h_ref[...], w_ref[:, cols]).astype(o_ref.dtype)


def _proj(h, w, out_dtype):
    seq, d = h.shape
    n = w.shape[1]
    tm = TM_PROJ
    return pl.pallas_call(
        _proj_kernel,
        grid=(seq // tm,),
        in_specs=[pl.BlockSpec((tm, d), lambda i: (i, 0)), _const_spec(w.shape)],
        out_specs=pl.BlockSpec((tm, n), lambda i: (i, 0)),
        out_shape=jax.ShapeDtypeStruct((seq, n), out_dtype),
        compiler_params=pltpu.CompilerParams(dimension_semantics=("parallel",)),
        name="proj",
    )(h, w)


def _diffattn_kernel(qall_ref, k_ref, v_ref, bias_ref, bstat_ref, lam_ref, g_ref, o_ref,
                     flag_ref, qs_ref, vt_ref, kmax_ref, r_ref, m_ref, acc_ref, *s_refs, out_scale):
    blk = BLK_A
    nq = 2 * blk
    sub = SUBLANES
    dv = DV_A
    npart = NPART_A
    sa_ref, sb_ref = s_refs[:2 * npart], s_refs[2 * npart:4 * npart]
    pa_ref, pb_ref = s_refs[4 * npart:5 * npart], s_refs[5 * npart:6 * npart]
    i = pl.program_id(1)
    lane = lax.broadcasted_iota(jnp.int32, (blk, LANES), 1)
    same_subhead = (lax.broadcasted_iota(jnp.int32, (LANES, LANES), 0) // DK_A
                    == lax.broadcasted_iota(jnp.int32, (LANES, LANES), 1) // DK_A).astype(BF16)

    bias_max, bias_span = bstat_ref[0, 0:1, 0:1], bstat_ref[0, 1:2, 0:1]
    q_scale = DK_A ** -0.5 * LOG2E

    @pl.when(i == 0)
    def _():
        kmax_ref[...] = jnp.zeros(kmax_ref.shape, F32)

        def tr(b, qmax):
            r0 = pl.multiple_of(b * blk, blk)
            c0 = pl.multiple_of(b * nq, nq)
            vt_ref[0:dv, pl.ds(r0, blk)] = v_ref[pl.ds(r0, blk), :].astype(F32).T.astype(BF16)
            vt_ref[dv:dv + ONES_A, pl.ds(r0, blk)] = jnp.ones((ONES_A, blk), BF16)
            kf = k_ref[pl.ds(r0, blk), :].astype(F32)
            kn2 = _dot((kf * kf).astype(BF16), same_subhead)
            kmax_ref[...] = jnp.maximum(kmax_ref[...], jnp.max(kn2.reshape(blk // sub, sub, LANES), axis=0))
            qa = (qall_ref[pl.ds(r0, blk), :].astype(F32) * q_scale).astype(BF16).astype(F32)
            for m in range(2):
                qt = jnp.where((lane < DK_A) if m == 0 else (lane >= DK_A), qa, 0.0).T
                qs_ref[:, pl.ds(c0 + m * blk, blk)] = qt.astype(BF16)
                qn2 = jnp.sum(jnp.sum((qt * qt).reshape(LANES // sub, sub, blk), axis=0), axis=0, keepdims=True)
                qn2 = jnp.broadcast_to(qn2, (sub, blk))
                r_ref[:, pl.ds(c0 + m * blk, blk)] = qn2
                qmax = jnp.maximum(qmax, qn2)
            return qmax
        nblocks = v_ref.shape[0] // blk
        qmax2 = jnp.max(lax.fori_loop(0, nblocks, tr, jnp.zeros((sub, blk), F32)))
        kmax2 = jnp.max(kmax_ref[...], axis=0, keepdims=True)
        worst = 2.0 * jnp.sqrt(qmax2 * kmax2) * 1.03 + bias_span
        flag_ref[0] = (jnp.max(worst) < SINGLE_PASS_LOG2_RANGE).astype(jnp.int32)

        kmax_nq = jnp.concatenate([jnp.broadcast_to(kmax2[:, m * DK_A:m * DK_A + 1], (sub, blk)) for m in range(2)],
                                  axis=1)

        def shifts(b, carry):
            c0 = pl.multiple_of(b * nq, nq)
            r_ref[:, pl.ds(c0, nq)] = jnp.sqrt(r_ref[:, pl.ds(c0, nq)] * kmax_nq) * 1.03 + bias_max
            return carry
        lax.fori_loop(0, nblocks, shifts, 0)

    acc_ref[...] = jnp.zeros(acc_ref.shape, F32)
    q0 = pl.multiple_of(i * nq, nq)
    single_pass = flag_ref[0] == 1

    @pl.when(single_pass)
    def _():
        _diffattn_fixed_shift(i, q0, k_ref, bias_ref, qs_ref, vt_ref, r_ref, m_ref, acc_ref, pa_ref, pb_ref)

    @pl.when(jnp.logical_not(single_pass))
    def _():
        _diffattn_online(i, q0, k_ref, bias_ref, qs_ref, vt_ref, m_ref, acc_ref, sa_ref, sb_ref)

    ot = acc_ref[0:dv, 0:nq] / acc_ref[dv:dv + 1, 0:nq]
    o = ot[:, 0:blk].T - lam_ref[...] * ot[:, blk:nq].T
    o = o * lax.rsqrt(jnp.mean(o * o, axis=-1, keepdims=True) + EPS) * g_ref[...]
    o_ref[...] = (o * out_scale).astype(o_ref.dtype)


def _diffattn_fixed_shift(i, q0, k_ref, bias_ref, qs_ref, vt_ref, shift_ref, l_ref, acc_ref, pa_ref, pb_ref):
    blk = BLK_A
    nq = 2 * blk
    sub = SUBLANES
    npart = len(pa_ref)
    wq = nq // npart
    l_ref[...] = jnp.zeros(l_ref.shape, F32)

    def probs(b, p_ref, bias):
        k = k_ref[pl.ds(pl.multiple_of(b * blk, blk), blk), :]
        for part in range(npart):
            cols = slice(part * wq, (part + 1) * wq)
            qcols = pl.ds(pl.multiple_of(q0 + part * wq, wq), wq)
            s = _dot(k, qs_ref[:, qcols])
            if bias is not None:
                b0 = (part * wq) % blk
                s = s + bias[:, b0:b0 + wq]
            p = jnp.exp2(s.reshape(blk // sub, sub, wq) - shift_ref[:, qcols][None])
            l_ref[:, cols] += jnp.sum(p, axis=0)
            p_ref[part][:, 0:wq] = p.reshape(blk, wq).astype(BF16)

    def accumulate(b, p_ref):
        vt = vt_ref[0:DV_A, pl.ds(pl.multiple_of(b * blk, blk), blk)]
        for part in range(npart):
            cols = slice(part * wq, (part + 1) * wq)
            acc_ref[0:DV_A, cols] += _dot(vt, p_ref[part][:, 0:wq])

    @pl.when(i == 0)
    def _():
        probs(0, pa_ref, bias_ref[0, 1])
        accumulate(0, pa_ref)

    @pl.when(i > 0)
    def _():
        nfar = i - 1
        probs(i, pa_ref, bias_ref[0, 1])
        probs(i - 1, pb_ref, bias_ref[0, 0])
        accumulate(i, pa_ref)

        def pair(t):
            probs(2 * t, pa_ref, None)
            accumulate(jnp.where(t == 0, i - 1, 2 * t - 1), pb_ref)
            probs(2 * t + 1, pb_ref, None)
            accumulate(2 * t, pa_ref)

        def four_pairs(u, carry):
            for v in range(4):
                pair(4 * u + v)
            return carry

        npairs = nfar // 2
        lax.fori_loop(0, npairs // 4, four_pairs, 0)

        def one_pair(t, carry):
            pair(t)
            return carry
        lax.fori_loop(4 * (npairs // 4), npairs, one_pair, 0)
        in_pb = jnp.where(npairs == 0, i - 1, 2 * npairs - 1)

        @pl.when(lax.rem(nfar, 2) == 1)
        def _():
            probs(nfar - 1, pa_ref, None)
            accumulate(in_pb, pb_ref)
            accumulate(nfar - 1, pa_ref)

        @pl.when(lax.rem(nfar, 2) == 0)
        def _():
            accumulate(in_pb, pb_ref)

    acc_ref[DV_A:DV_A + sub, 0:nq] = jnp.broadcast_to(jnp.sum(l_ref[...], axis=0, keepdims=True), (sub, nq))


def _diffattn_online(i, q0, k_ref, bias_ref, qs_ref, vt_ref, m_ref, acc_ref, sa_ref, sb_ref):
    blk = BLK_A
    nq = 2 * blk
    sub = SUBLANES
    npart = len(sa_ref) // 2
    wq = nq // npart
    m_ref[...] = jnp.full(m_ref.shape, NEG_INF, F32)

    def scores(b, s_ref):
        k = k_ref[pl.ds(pl.multiple_of(b * blk, blk), blk), :]
        for part in range(npart):
            s = _dot(k, qs_ref[:, pl.ds(pl.multiple_of(q0 + part * wq, wq), wq)])
            s_ref[part][:, 0:wq] = s
            s_ref[npart + part][...] = jnp.max(s.reshape(blk // sub, sub, wq), axis=0)

    def softmax_pv(b, s_ref, bias):
        vt = vt_ref[:, pl.ds(pl.multiple_of(b * blk, blk), blk)]
        for part in range(npart):
            cols = slice(part * wq, (part + 1) * wq)
            s = s_ref[part][:, 0:wq]
            if bias is not None:
                b0 = (part * wq) % blk
                s = s + bias[:, b0:b0 + wq]
            s = s.reshape(blk // sub, sub, wq)
            m_prev = m_ref[:, cols]
            smax = jnp.max(s, axis=0) if bias is not None else s_ref[npart + part][...]
            m_cur = jnp.max(smax, axis=0, keepdims=True)
            m_new = jnp.maximum(m_prev, m_cur)
            alpha = jnp.exp2(m_prev - m_new)
            p = jnp.exp2(s - m_new[None])
            pv = _dot(vt, p.reshape(blk, wq).astype(BF16))
            acc_ref[:, cols] = acc_ref[:, cols] * alpha[0:1] + pv
            m_ref[:, cols] = m_new

    nfar = jnp.maximum(i - 1, 0)
    odd = lax.rem(nfar, 2)

    @pl.when(i == 0)
    def _():
        scores(0, sb_ref)

    @pl.when(i > 0)
    def _():
        @pl.when(odd == 1)
        def _():
            scores(0, sb_ref)
            scores(1, sa_ref)
            softmax_pv(0, sb_ref, None)

        @pl.when(odd == 0)
        def _():
            scores(0, sa_ref)

        def pair(b):
            scores(b + 1, sb_ref)
            softmax_pv(b, sa_ref, None)
            scores(b + 2, sa_ref)
            softmax_pv(b + 1, sb_ref, None)

        def quad_body(t, carry):
            pair(odd + 4 * t)
            pair(odd + 4 * t + 2)
            return carry

        npairs = nfar // 2
        lax.fori_loop(0, npairs // 2, quad_body, 0)

        @pl.when(lax.rem(npairs, 2) == 1)
        def _():
            pair(odd + 2 * (npairs - 1))
        scores(i, sb_ref)
        softmax_pv(i - 1, sa_ref, bias_ref[0, 0])

    softmax_pv(i, sb_ref, bias_ref[0, 1])


_TOEPLITZ_ROWS = 512


def _toeplitz_kernel(v_ref, o_ref, *, keep):
    rows, cols = o_ref.shape[2:]
    x = jnp.broadcast_to(v_ref[0, 0], (rows, v_ref.shape[-1]))
    tile = pltpu.roll(x, 0, 1, stride=1, stride_axis=0)[:, :cols]
    r = lax.broadcasted_iota(jnp.int32, (rows, cols), 0) + pl.program_id(1) * rows
    c = lax.broadcasted_iota(jnp.int32, (rows, cols), 1)
    for variant in range(o_ref.shape[0]):
        o_ref[variant, 0] = jnp.where(keep(r, c, variant), tile, NEG_INF)


def _toeplitz_tiles(fn, keep, heads, rows, cols, variants=1):
    n = rows + cols
    rb = rows if rows % _TOEPLITZ_ROWS else _TOEPLITZ_ROWS
    assert rows % rb == 0 and n % LANES == 0
    idx = jnp.arange(n, dtype=jnp.int32)
    vec = fn(jnp.where(idx < cols, idx, idx - n)).astype(F32)
    vecs = jnp.stack([jnp.roll(vec, k * rb, axis=1) for k in range(rows // rb)], axis=1)
    return pl.pallas_call(
        functools.partial(_toeplitz_kernel, keep=keep),
        grid=(heads, rows // rb),
        in_specs=[pl.BlockSpec((1, 1, 1, n), lambda h, k: (h, k, 0, 0))],
        out_specs=pl.BlockSpec((variants, 1, rb, cols), lambda h, k: (0, h, k, 0)),
        out_shape=jax.ShapeDtypeStruct((variants, heads, rows, cols), F32),
        name="toeplitz_tiles",
    )(vecs.reshape(heads, rows // rb, 1, n))


def _bias_stats(tiles):
    finite = tiles > 0.5 * NEG_INF
    bias_max = jnp.maximum(jnp.max(jnp.where(finite, tiles, NEG_INF), axis=(1, 2)), 0.0)
    bias_min = jnp.minimum(jnp.min(jnp.where(finite, tiles, -NEG_INF), axis=(1, 2)), 0.0)
    return jnp.broadcast_to(jnp.stack([bias_max, bias_max - bias_min], axis=1)[:, :, None],
                            (tiles.shape[0], 2, LANES))


def _t5_bucket(rel):
    nb = NUM_BUCKETS // 2
    max_exact = nb // 2
    bucket = jnp.where(rel > 0, nb, 0)
    n = jnp.abs(rel)
    nf = jnp.maximum(n, 1).astype(F32)
    large = max_exact + (jnp.log(nf / max_exact) / math.log(MAX_DISTANCE / max_exact)
                         * (nb - max_exact)).astype(jnp.int32)
    large = jnp.minimum(large, nb - 1)
    return bucket + jnp.where(n < max_exact, n, large)


def _diff_bias_tiles(t5_table):
    blk = BLK_A
    table = t5_table.astype(F32)
    far = table[_t5_bucket(jnp.full((), -(blk + 1), jnp.int32))]
    def visible(r, c, variant):
        return jnp.floor_divide(r - blk, CHUNK) <= jnp.floor_divide(c, CHUNK)

    tiles = _toeplitz_tiles(lambda x: ((table[_t5_bucket(-x - blk)] - far) * LOG2E).T, visible,
                            N_HEADS_A, 2 * blk, blk)
    return tiles.reshape(N_HEADS_A, 2, blk, blk)


def _diff_attention(proj, t5_table, lam, subln_g, lam_init):
    seq = proj.shape[0]
    blk = BLK_A
    bias = _diff_bias_tiles(t5_table)
    ha = N_HEADS_A
    bstat = _bias_stats(bias.reshape(ha, 2 * blk, blk))
    kern = functools.partial(_diffattn_kernel, out_scale=1.0 - lam_init)
    return pl.pallas_call(
        kern,
        grid=(ha, seq // blk),
        in_specs=[
            pl.BlockSpec((seq, DV_A), lambda h, i: (0, h)),
            pl.BlockSpec((seq, DV_A), lambda h, i: (0, ha + h)),
            pl.BlockSpec((seq, DV_A), lambda h, i: (0, 2 * ha + h)),
            pl.BlockSpec((1, 2, blk, blk), lambda h, i: (h, 0, 0, 0)),
            pl.BlockSpec((1, 2, LANES), lambda h, i: (h, 0, 0)),
            pl.BlockSpec((1, DV_A), lambda h, i: (0, 0)),
            pl.BlockSpec((1, DV_A), lambda h, i: (0, 0)),
        ],
        out_specs=pl.BlockSpec((blk, DV_A), lambda h, i: (i, h)),
        out_shape=jax.ShapeDtypeStruct((seq, ha * DV_A), BF16),
        scratch_shapes=[
            pltpu.SMEM((1,), jnp.int32),
            pltpu.VMEM((DV_A, 2 * seq), BF16),
            pltpu.VMEM((DV_A + ONES_A, seq), BF16),
            pltpu.VMEM((SUBLANES, LANES), F32),
            pltpu.VMEM((SUBLANES, 2 * seq), F32),
            pltpu.VMEM((SUBLANES, 2 * blk), F32),
            pltpu.VMEM((DV_A + ONES_A, 2 * blk), F32),
        ] + 2 * ([pltpu.VMEM((blk, 2 * blk // NPART_A + SCORE_PAD), F32)] * NPART_A
                 + [pltpu.VMEM((SUBLANES, 2 * blk // NPART_A), F32)] * NPART_A)
        + 2 * [pltpu.VMEM((blk, 2 * blk // NPART_A + SCORE_PAD), BF16)] * NPART_A,
        compiler_params=pltpu.CompilerParams(dimension_semantics=("parallel", "arbitrary")),
        name="diff_attention",
    )(proj, proj, proj, bias, bstat, jnp.full((1, DV_A), lam, F32), subln_g.reshape(1, DV_A).astype(F32))


def _band_kernel(q_ref, kp_ref, kc_ref, vp_ref, vc_ref, *refs):
    qw, band = QW_B, BAND_B
    nbias = band // qw + 1
    bias_refs, bstat_ref, o_ref = refs[:nbias], refs[nbias], refs[nbias + 1]
    p_refs = refs[nbias + 2:]
    ngroups = len(p_refs)
    nk = band + qw
    sub = SUBLANES
    q = (q_ref[...].astype(F32) * (DH_B ** -0.5 * LOG2E)).astype(BF16).astype(F32)
    lane = lax.broadcasted_iota(jnp.int32, q.shape, 1)
    qt = (jnp.where(lane < DH_B, q, 0.0).T, jnp.where(lane >= DH_B, q, 0.0).T)
    qh = (qt[0].astype(BF16), qt[1].astype(BF16))
    k_all = jnp.concatenate([kp_ref[...], kc_ref[...]], axis=0)
    vt_all = jnp.concatenate([vp_ref[...], vc_ref[...]], axis=0).astype(F32).T.astype(BF16)

    def group_operands(g):
        k0 = g * qw
        qs = jnp.concatenate([qh[0][:, k0:k0 + qw], qh[1][:, k0:k0 + qw]], axis=1)
        bias_ref = bias_refs[min(g, nbias - 1)]
        bias = jnp.concatenate([bias_ref[0, 0], bias_ref[0, 1]], axis=1)
        return k0, qs, bias

    def store_group(g, ot):
        o = jnp.concatenate([ot[0:DH_B, 0:qw], ot[DH_B:2 * DH_B, qw:2 * qw]], axis=0)
        o_ref[g * qw:(g + 1) * qw, :] = o.T.astype(o_ref.dtype)

    same_head = (lax.broadcasted_iota(jnp.int32, (LANES, LANES), 0) // DH_B
                 == lax.broadcasted_iota(jnp.int32, (LANES, LANES), 1) // DH_B).astype(BF16)
    kf = k_all.astype(F32)
    kn2 = _dot((kf * kf).astype(BF16), same_head)
    kmax2 = jnp.max(jnp.max(kn2.reshape(kn2.shape[0] // sub, sub, LANES), axis=0), axis=0, keepdims=True)
    shifts, worst = [], None
    for m in range(2):
        qn2 = jnp.sum(qt[m] * qt[m], axis=0, keepdims=True)
        bound = jnp.sqrt(qn2 * kmax2[:, m * DH_B:m * DH_B + 1]) * 1.03
        shifts.append(bound + bstat_ref[m, 0:1, 0:1])
        spread = jnp.max(2.0 * bound + bstat_ref[m, 1:2, 0:1])
        worst = spread if worst is None else jnp.maximum(worst, spread)
    fixed_shift = worst < SINGLE_PASS_LOG2_RANGE

    @pl.when(fixed_shift)
    def _():
        sums = []
        for g in range(ngroups):
            k0, qs, bias = group_operands(g)
            r = jnp.concatenate([shifts[0][:, k0:k0 + qw], shifts[1][:, k0:k0 + qw]], axis=1)
            s = _dot(k_all[k0:k0 + nk], qs) + (bias - r)
            p = jnp.exp2(s).reshape(nk // sub, sub, 2 * qw)
            sums.append(jnp.sum(jnp.sum(p, axis=0), axis=0, keepdims=True))
            p_refs[g][:, 0:2 * qw] = p.reshape(nk, 2 * qw).astype(BF16)
        for g in range(ngroups):
            k0 = g * qw
            store_group(g, _dot(vt_all[:, k0:k0 + nk], p_refs[g][:, 0:2 * qw]) / sums[g])

    @pl.when(jnp.logical_not(fixed_shift))
    def _():
        vt_ones = jnp.concatenate([vt_all, jnp.ones((ONES_A, vt_all.shape[1]), BF16)], axis=0)
        for g in range(ngroups):
            k0, qs, bias = group_operands(g)
            s = (_dot(k_all[k0:k0 + nk], qs) + bias).reshape(nk // sub, sub, 2 * qw)
            m = jnp.max(jnp.max(s, axis=0), axis=0, keepdims=True)
            p = jnp.exp2(s - m[None])
            pv = _dot(vt_ones[:, k0:k0 + nk], p.reshape(nk, 2 * qw).astype(BF16))
            store_group(g, pv[0:2 * DH_B] / pv[2 * DH_B:2 * DH_B + 1])


def _band_bias_tiles(rel_bias):
    band = BAND_B

    def valid(r, c, variant):
        qchunk = jnp.floor_divide(c, CHUNK)
        kchunk = jnp.floor_divide(r - band, CHUNK)
        missing = jnp.where(variant == 0, 0, band - (variant - 1) * QW_B)
        return (kchunk <= qchunk) & (kchunk >= qchunk - LEFT_CHUNKS) & (r >= missing)

    return _toeplitz_tiles(
        lambda x: rel_bias.astype(F32)[:, jnp.clip(-x - band, -REL_CLIP, REL_CLIP) + REL_CLIP] * LOG2E, valid,
        N_HEADS_B, band + QW_B, QW_B, variants=1 + band // QW_B)


def _band_attention(proj, rel_bias):
    seq = proj.shape[0]
    blk, band, qw = BLK_B, BAND_B, QW_B
    bias = _band_bias_tiles(rel_bias)
    npair = N_HEADS_B // 2
    qc0 = 3 * N_HEADS_A
    per = blk // band
    prev = lambda c0: (lambda hp, i: (jnp.maximum(i * per - 1, 0), c0 + hp))
    cur = lambda c0: (lambda hp, i: (i, c0 + hp))
    return pl.pallas_call(
        _band_kernel,
        grid=(npair, seq // blk),
        in_specs=[
            pl.BlockSpec((blk, LANES), cur(qc0)),
            pl.BlockSpec((band, LANES), prev(qc0 + npair)),
            pl.BlockSpec((blk, LANES), cur(qc0 + npair)),
            pl.BlockSpec((band, LANES), prev(qc0 + 2 * npair)),
            pl.BlockSpec((blk, LANES), cur(qc0 + 2 * npair)),
        ] + [
            pl.BlockSpec((1, 2, band + qw, qw), (lambda hp, i, t=t: (jnp.where(i == 0, 1 + t, 0), hp, 0, 0)))
            for t in range(band // qw)
        ] + [
            pl.BlockSpec((1, 2, band + qw, qw), lambda hp, i: (0, hp, 0, 0)),
            pl.BlockSpec((2, 2, LANES), lambda hp, i: (hp, 0, 0)),
        ],
        out_specs=pl.BlockSpec((blk, LANES), lambda hp, i: (i, hp)),
        out_shape=jax.ShapeDtypeStruct((seq, N_HEADS_B * DH_B), BF16),
        scratch_shapes=[pltpu.VMEM((band + qw, 2 * qw + SCORE_PAD), BF16)] * (blk // qw),
        compiler_params=pltpu.CompilerParams(dimension_semantics=("parallel", "arbitrary")),
        name="band_attention",
    )(proj, proj, proj, proj, proj, *([bias] * bias.shape[0]), _bias_stats(bias[0]))


def _retention_kernel(qk_ref, v_ref, gate_ref, cos_ref, sin_ref, qdec_ref, kdec_ref, dmat_ref,
                      sdec_ref, o_ref, state_ref):
    @pl.when(pl.program_id(0) == 0)
    def _():
        state_ref[...] = jnp.zeros(state_ref.shape, F32)

    cos = cos_ref[...]
    sin = sin_ref[...]
    lane = lax.broadcasted_iota(jnp.int32, cos.shape, 1)
    first_half = (lane % DQK_C) < (DQK_C // 2)
    qk = qk_ref[...]
    parts = []
    for j in range(qk.shape[1] // LANES):
        t = qk[:, j * LANES:(j + 1) * LANES]
        partner = jnp.where(first_half, pltpu.roll(t, LANES - DQK_C // 2, 1), pltpu.roll(t, DQK_C // 2, 1))
        parts.append(t * cos + partner * sin)
    wq = N_HEADS_C * DQK_C
    q = jnp.concatenate(parts[:wq // LANES], axis=1)
    k = jnp.concatenate(parts[wq // LANES:], axis=1) * (DQK_C ** -0.5)
    qd = (q * qdec_ref[...]).astype(BF16)
    kd = (k * kdec_ref[...]).astype(BF16)
    qb = q.astype(BF16)
    kb = k.astype(BF16)
    vb = v_ref[...].astype(BF16)
    gate = gate_ref[...]
    outs = []
    for h in range(N_HEADS_C):
        qs = slice(h * DQK_C, (h + 1) * DQK_C)
        vs = slice(h * DV_C, (h + 1) * DV_C)
        scores = _dot_nt(qb[:, qs], kb[:, qs]) * dmat_ref[h]
        state = state_ref[h]
        r = _dot(scores.astype(BF16), vb[:, vs]) + _dot(qd[:, qs], state.astype(BF16))
        state_ref[h] = state * sdec_ref[h] + _dot_tn(kd[:, qs], vb[:, vs])
        r = r * lax.rsqrt(jnp.mean(r * r, axis=-1, keepdims=True) + EPS)
        g = gate[:, vs]
        outs.append(r * (g * jax.nn.sigmoid(g)))
    o_ref[...] = jnp.concatenate(outs, axis=1).astype(o_ref.dtype)


def _retention_tables(seq):
    t = BLK_C
    half = DQK_C // 2
    inv_freq = 1.0 / np.power(ROPE_BASE, np.arange(0, DQK_C, 2, dtype=np.float64) / DQK_C)
    ang = np.arange(seq, dtype=np.float64)[:, None] * inv_freq[None, :]
    reps = LANES // half
    cos = np.tile(np.cos(ang), (1, reps))
    sign = np.where((np.arange(LANES) % DQK_C) < half, -1.0, 1.0)
    sin = np.tile(np.sin(ang), (1, reps)) * sign[None, :]
    log_g = np.log(1.0 - np.power(2.0, -5.0 - np.arange(N_HEADS_C, dtype=np.float64)))
    pos = np.arange(t, dtype=np.float64)
    diff = pos[:, None] - pos[None, :]
    same_or_past = (np.arange(t)[None, :] // CHUNK) <= (np.arange(t)[:, None] // CHUNK)
    dmat = np.where(same_or_past[None], np.exp(log_g[:, None, None] * np.abs(diff)[None]), 0.0)
    qdec = np.repeat(np.exp(log_g[None, :] * (pos[:, None] + 1.0)), DQK_C, axis=1)
    kdec = np.repeat(np.exp(log_g[None, :] * (t - 1.0 - pos[:, None])), DQK_C, axis=1)
    sdec = np.broadcast_to(np.exp(log_g * t)[:, None, None], (N_HEADS_C, 1, DV_C))
    return tuple(jnp.asarray(a.astype(np.float32)) for a in (cos, sin, qdec, kdec, dmat, sdec))


def _retention(proj):
    seq = proj.shape[0]
    t = BLK_C
    cos, sin, qdec, kdec, dmat, sdec = _retention_tables(seq)
    wv = N_HEADS_C * DV_C
    return pl.pallas_call(
        _retention_kernel,
        grid=(seq // t,),
        in_specs=[
            pl.BlockSpec((t, wv), lambda i: (i, 0)),
            pl.BlockSpec((t, wv), lambda i: (i, 1)),
            pl.BlockSpec((t, wv), lambda i: (i, 2)),
            pl.BlockSpec((t, LANES), lambda i: (i, 0)),
            pl.BlockSpec((t, LANES), lambda i: (i, 0)),
            pl.BlockSpec((t, N_HEADS_C * DQK_C), lambda i: (0, 0)),
            pl.BlockSpec((t, N_HEADS_C * DQK_C), lambda i: (0, 0)),
            pl.BlockSpec((N_HEADS_C, t, t), lambda i: (0, 0, 0)),
            pl.BlockSpec((N_HEADS_C, 1, DV_C), lambda i: (0, 0, 0)),
        ],
        out_specs=pl.BlockSpec((t, wv), lambda i: (i, 0)),
        out_shape=jax.ShapeDtypeStruct((seq, wv), BF16),
        scratch_shapes=[pltpu.VMEM((N_HEADS_C, DQK_C, DV_C), F32)],
        compiler_params=pltpu.CompilerParams(dimension_semantics=("arbitrary",)),
        name="retention",
    )(proj, proj, proj, cos, sin, qdec, kdec, dmat, sdec)


def _s5_kernel(*refs):
    ncb = S5_CH // LANES
    u_refs = refs[:ncb]
    (mt_ref, bt_ref, ctr_ref, cti_ref, are_ref, aim_ref, y_ref,
     ut_ref, yt_ref, ys_ref, vr_ref, vi_ref, spr_ref, spi_ref, carry_ref) = refs[ncb:]
    tc = S5_TC
    gp = S5_GROUP
    n = S5_STATE
    ng = S5_GROUPS

    @pl.when(pl.program_id(0) == 0)
    def _():
        carry_ref[...] = jnp.zeros(carry_ref.shape, F32)

    for s in range(S5_T):
        for k in range(ncb):
            ut_ref[s, k * LANES:(k + 1) * LANES, :] = u_refs[k][pl.ds(s, tc, stride=S5_T), :].T

    unroll = 16

    def intra(it, carry):
        for k in range(unroll):
            g = it * unroll + k
            r0 = pl.multiple_of(g * gp, gp)
            ug = ut_ref[:, pl.ds(r0, gp), :].reshape(S5_T * gp, tc).astype(BF16)
            yt_ref[:, pl.ds(r0, gp), :] = _dot(mt_ref[g], ug).reshape(S5_T, gp, tc)
            vt = _dot(bt_ref[g], ug)
            n0 = pl.multiple_of(g * n, n)
            vr_ref[pl.ds(n0, n), :] = vt[0:n]
            vi_ref[pl.ds(n0, n), :] = vt[n:2 * n]
        return carry

    lax.fori_loop(0, ng // unroll, intra, 0)

    sub = SUBLANES
    nv = tc // sub
    row = lax.broadcasted_iota(jnp.int32, (tc, LANES), 0)
    in_vreg = lax.rem(row, sub)

    def rows_of(v, r):
        return jnp.broadcast_to(v[r:r + 1], (tc, LANES))

    for j in range(ng * n // LANES):
        cols = slice(j * LANES, (j + 1) * LANES)
        pwr, pwi = are_ref[:, cols], aim_ref[:, cols]
        xr = vr_ref[cols, :].T
        xi = vi_ref[cols, :].T
        for d in (1, 2, 4):
            keep = in_vreg >= d
            sr = jnp.where(keep, pltpu.roll(xr, d, 0), 0.0)
            si = jnp.where(keep, pltpu.roll(xi, d, 0), 0.0)
            fr, fi = rows_of(pwr, d - 1), rows_of(pwi, d - 1)
            xr, xi = xr + (fr * sr - fi * si), xi + (fr * si + fi * sr)
        cr, ci = carry_ref[0, :, cols], carry_ref[1, :, cols]
        cr0, ci0 = cr, ci
        outr, outi = [], []
        for v in range(nv):
            yr = xr[v * sub:(v + 1) * sub] + (pwr * cr - pwi * ci)
            yi = xi[v * sub:(v + 1) * sub] + (pwr * ci + pwi * cr)
            outr.append(yr)
            outi.append(yi)
            cr = jnp.broadcast_to(yr[sub - 1:sub], (sub, LANES))
            ci = jnp.broadcast_to(yi[sub - 1:sub], (sub, LANES))
        carry_ref[0, :, cols] = cr
        carry_ref[1, :, cols] = ci
        sr = jnp.concatenate(outr, axis=0)
        si = jnp.concatenate(outi, axis=0)
        first = row == 0
        spr_ref[j] = jnp.where(first, rows_of(cr0, 0), pltpu.roll(sr, 1, 0))
        spi_ref[j] = jnp.where(first, rows_of(ci0, 0), pltpu.roll(si, 1, 0))

    def cross(it, carry):
        for k in range(unroll):
            jp = it * unroll + k
            r0 = pl.multiple_of(jp * 2 * gp, 2 * gp)
            yc = (_dot_nt(ctr_ref[jp], spr_ref[jp].astype(BF16))
                  + _dot_nt(cti_ref[jp], spi_ref[jp].astype(BF16)))
            yt_ref[:, pl.ds(r0, 2 * gp), :] += yc.reshape(S5_T, 2 * gp, tc)
        return carry

    lax.fori_loop(0, ng // 2 // unroll, cross, 0)

    for s in range(S5_T):
        for k in range(ncb):
            ys_ref[k, pl.ds(s, tc, stride=S5_T), :] = yt_ref[s, k * LANES:(k + 1) * LANES, :].T
    for k in range(ncb):
        y_ref[:, k * LANES:(k + 1) * LANES] = ys_ref[k]


def _s5_matrices(lam_re, lam_im, log_step, b_re, b_im, c_re, c_im, d_skip):
    hi = lax.Precision.HIGHEST
    t, gp, n, ng = S5_T, S5_GROUP, S5_STATE, S5_GROUPS
    lam = lax.complex(lam_re.astype(F32), lam_im.astype(F32))
    step = jnp.exp(log_step.astype(F32))[:, None]
    ls = lam * step
    a_bar = jnp.exp(ls)
    b_bar = ((a_bar - 1.0) / lam)[..., None] * lax.complex(b_re.astype(F32), b_im.astype(F32))
    cm = lax.complex(c_re.astype(F32), c_im.astype(F32))

    def apow(k):
        kk = k.astype(F32).astype(jnp.complex64)
        return jnp.exp(ls.reshape((ng,) + (1,) * k.ndim + (n,)) * kk[None, ..., None])

    tt = jnp.arange(t)
    kmat = jnp.einsum('gpn,gln,gnq->glpq', cm, apow(tt), b_bar, precision=hi).real
    krev = jnp.transpose(kmat[:, ::-1], (0, 2, 1, 3)).reshape(ng, gp, t * gp)
    kpad = jnp.pad(krev, ((0, 0), (0, 0), (0, t * gp)))
    mt = jnp.concatenate([kpad[:, :, (t - 1 - to) * gp:(2 * t - 1 - to) * gp] for to in range(t)], axis=1)
    dvec = jnp.tile(d_skip.astype(F32).reshape(ng, 1, gp), (1, t, 1)).reshape(ng, t * gp)
    mt = mt + jnp.eye(t * gp, dtype=F32)[None] * dvec[:, :, None]
    z = jnp.swapaxes(apow(t - 1 - tt), 1, 2)[:, :, :, None] * b_bar[:, :, None, :]
    z = z.reshape(ng, n, t * gp)
    bt = jnp.concatenate([z.real, z.imag], axis=1)
    w = cm[:, None, :, :] * apow(tt + 1)[:, :, None, :]

    def pair_readout(x):
        x = x.reshape(ng // 2, 2, t, gp, n)
        first = jnp.pad(x[:, 0], ((0, 0), (0, 0), (0, 0), (0, n)))
        second = jnp.pad(x[:, 1], ((0, 0), (0, 0), (0, 0), (n, 0)))
        return jnp.stack([first, second], axis=2).reshape(ng // 2, t * 2 * gp, 2 * n).astype(BF16)

    ctr, cti = pair_readout(w.real), pair_readout(-w.imag)
    a_chunk = jnp.transpose(apow(t * (jnp.arange(SUBLANES) + 1)), (1, 0, 2)).reshape(SUBLANES, ng * n)
    return mt.astype(BF16), bt.astype(BF16), ctr, cti, a_chunk.real, a_chunk.imag


def _s5(proj, mats):
    seq, width = proj.shape
    t, tc, gp, n, ng = S5_T, S5_TC, S5_GROUP, S5_STATE, S5_GROUPS
    rows = t * tc
    ncb = S5_CH // LANES
    cb0 = (width - S5_CH) // LANES
    u_specs = [pl.BlockSpec((rows, LANES), (lambda i, k=k: (i, cb0 + k))) for k in range(ncb)]
    nsb = ng * n // LANES
    return pl.pallas_call(
        _s5_kernel,
        grid=(seq // rows,),
        in_specs=u_specs + [_const_spec(m.shape) for m in mats],
        out_specs=pl.BlockSpec((rows, S5_CH), lambda i: (i, 0)),
        out_shape=jax.ShapeDtypeStruct((seq, S5_CH), F32),
        scratch_shapes=[
            pltpu.VMEM((t, S5_CH, tc), F32),
            pltpu.VMEM((t, S5_CH, tc), F32),
            pltpu.VMEM((ncb, rows, LANES), F32),
            pltpu.VMEM((ng * n, tc), F32),
            pltpu.VMEM((ng * n, tc), F32),
            pltpu.VMEM((nsb, tc, LANES), F32),
            pltpu.VMEM((nsb, tc, LANES), F32),
            pltpu.VMEM((2, SUBLANES, ng * n), F32),
        ],
        compiler_params=pltpu.CompilerParams(dimension_semantics=("arbitrary",)),
        name="s5_scan",
    )(*([proj] * ncb), *mats)


def _mix_ffn_kernel(*refs, glu, final):
    (x_ref, a_ref, b_ref, wo_ref, g1_ref), refs = refs[:5], refs[5:]
    if glu:
        gw_ref, refs = refs[0], refs[1:]
    (g_ref, sc_ref, sh_ref, gate_ref, win_ref, cw_ref, cb_ref, wout_ref), refs = refs[:8], refs[8:]
    if final:
        fg_ref, o_ref, h_ref, act_ref, gbuf_ref, carry_ref = refs
    else:
        ng_ref, nsc_ref, nsh_ref, o_ref, hn_ref, h_ref, act_ref, gbuf_ref, carry_ref = refs
    tm = x_ref.shape[0]
    halo = gbuf_ref.shape[0] - tm

    @pl.when(pl.program_id(0) == 0)
    def _():
        carry_ref[...] = jnp.zeros(carry_ref.shape, F32)

    if glu:
        y = jax.nn.gelu(b_ref[...]).astype(BF16)
        gg = _dot(y, gw_ref[...])
        half = gg.shape[1] // 2
        b = (gg[:, :half] * jax.nn.sigmoid(gg[:, half:])).astype(BF16)
    else:
        b = b_ref[...]
    cat = jnp.concatenate([a_ref[...], b], axis=1)
    x = x_ref[...] + g1_ref[...] * _dot(cat, wo_ref[...])
    h_ref[...] = _mod_rmsnorm(x, g_ref[...], sc_ref[...], sh_ref[...]).astype(BF16)
    for f in range(D_FF // TF_FFN):
        cs = slice(f * TF_FFN, (f + 1) * TF_FFN)
        gs = slice(D_FF + f * TF_FFN, D_FF + (f + 1) * TF_FFN)
        h = h_ref[...]
        val = _dot(h, win_ref[:, cs])
        gate = _dot(h, win_ref[:, gs])
        gbuf_ref[0:halo, :] = carry_ref[:, cs]
        gbuf_ref[halo:halo + tm, :] = gate
        carry_ref[:, cs] = gate[tm - halo:tm, :]
        conv = (gate * cw_ref[2:3, cs] + gbuf_ref[halo - 1:halo - 1 + tm, :] * cw_ref[1:2, cs]
                + gbuf_ref[halo - 2:halo - 2 + tm, :] * cw_ref[0:1, cs] + cb_ref[:, cs])
        act_ref[:, cs] = (jax.nn.gelu(conv) * val).astype(BF16)
    xn = x + gate_ref[...] * _dot(act_ref[...], wout_ref[...])
    if final:
        xn = xn * lax.rsqrt(jnp.mean(xn * xn, axis=-1, keepdims=True) + EPS) * fg_ref[...]
    else:
        hn_ref[...] = _mod_rmsnorm(xn, ng_ref[...], nsc_ref[...], nsh_ref[...]).astype(BF16)
    o_ref[...] = xn


def _layer_spec(shape, layer):
    idx = (layer,) + (0,) * (len(shape) - 1)
    return pl.BlockSpec((None,) + tuple(shape[1:]), lambda *_: idx, pipeline_mode=pl.Buffered(1))


def _mix_ffn(x, a, b, wo, gate1, glu_w, g, scale, shift, gate2, w_in, conv_w, conv_b, w_out, tail, layer):
    seq, d = x.shape
    final = len(tail) == 1
    tm = TM_FFN
    halo = SUBLANES
    row = pl.BlockSpec((1, d), lambda i: (0, 0))
    rows = lambda w: pl.BlockSpec((tm, w), lambda i: (i, 0))
    conv_b = conv_b.reshape(conv_b.shape[0], 1, D_FF)
    in_specs = [rows(d), rows(a.shape[1]), rows(b.shape[1]), _const_spec(wo.shape), row]
    args = [x, a, b, wo, gate1]
    if glu_w is not None:
        in_specs.append(_const_spec(glu_w.shape))
        args.append(glu_w)
    in_specs += [
        row, row, row, row,
        _layer_spec(w_in.shape, layer),
        _layer_spec(conv_w.shape, layer),
        _layer_spec(conv_b.shape, layer),
        _layer_spec(w_out.shape, layer),
    ] + [row] * len(tail)
    args += [g.reshape(1, d), scale, shift, gate2, w_in, conv_w, conv_b, w_out]
    args += [t.reshape(1, d) for t in tail]
    out_specs = [rows(d)] if final else [rows(d), rows(d)]
    out_shape = [jax.ShapeDtypeStruct((seq, d), F32)] + ([] if final else [jax.ShapeDtypeStruct((seq, d), BF16)])
    return pl.pallas_call(
        functools.partial(_mix_ffn_kernel, glu=glu_w is not None, final=final),
        grid=(seq // tm,),
        in_specs=in_specs,
        out_specs=out_specs,
        out_shape=out_shape,
        scratch_shapes=[
            pltpu.VMEM((tm, d), BF16),
            pltpu.VMEM((tm, D_FF), BF16),
            pltpu.VMEM((tm + halo, TF_FFN), F32),
            pltpu.VMEM((halo, D_FF), F32),
        ],
        compiler_params=pltpu.CompilerParams(dimension_semantics=("arbitrary",)),
        name="mix_ffn",
    )(*args)


def kernel(x, c, t5_table, mod_w, mod_b, norm1_g, norm2_g, ffn_w_in, ffn_conv_w, ffn_conv_b, ffn_w_out,
           ev_w_in, ev_w_out, diff_lambda, diff_subln_g, band_rel_bias,
           od_w_in, od_w_out, s5_lam_re, s5_lam_im, s5_log_step, s5_b_re, s5_b_im, s5_c_re, s5_c_im,
           s5_d, s5_glu_w, final_g):
    assert x.shape[0] == 1 and x.shape[2] == D_MODEL
    seq = x.shape[1]
    assert seq % TM_PROJ == 0 and seq % (S5_T * S5_TC) == 0
    d = D_MODEL
    xs = x[0]
    mod = _modulation(c, mod_w, mod_b)
    ffn_w_in_b = ffn_w_in.astype(BF16)
    ffn_w_out_b = ffn_w_out.astype(BF16)
    mods = [[mod[i, :, k * d:(k + 1) * d] for k in range(6)] for i in range(DEPTH)]
    h = None
    for i in range(DEPTH):
        sh1, sc1, g1, sh2, sc2, g2 = mods[i]
        w_in = (ev_w_in if i % 2 == 0 else od_w_in)[i // 2].astype(BF16)
        proj_dtype = BF16 if i % 2 == 0 else F32
        if h is None:
            proj = _normproj(xs, norm1_g[i], sc1, sh1, w_in, proj_dtype)
        else:
            proj = _proj(h, w_in, proj_dtype)
        if i % 2 == 0:
            e = i // 2
            lam_init = 0.8 - 0.6 * math.exp(-0.3 * i)
            lp = diff_lambda[e].astype(F32)
            lam = jnp.exp(jnp.sum(lp[0] * lp[1])) - jnp.exp(jnp.sum(lp[2] * lp[3])) + lam_init
            mix_a = _diff_attention(proj, t5_table, lam, diff_subln_g[e], lam_init)
            mix_b = _band_attention(proj, band_rel_bias[e])
            wo, glu_w = ev_w_out[e].astype(BF16), None
        else:
            o = i // 2
            mix_a = _retention(proj)
            mats = _s5_matrices(s5_lam_re[o], s5_lam_im[o], s5_log_step[o], s5_b_re[o], s5_b_im[o],
                                s5_c_re[o], s5_c_im[o], s5_d[o])
            mix_b = _s5(proj, mats)
            wo, glu_w = od_w_out[o].astype(BF16), s5_glu_w[o].astype(BF16)
        if i == DEPTH - 1:
            tail = (final_g,)
        else:
            nsh1, nsc1 = mods[i + 1][0], mods[i + 1][1]
            tail = (norm1_g[i + 1], nsc1, nsh1)
        out = _mix_ffn(xs, mix_a, mix_b, wo, g1, glu_w, norm2_g[i], sc2, sh2, g2,
                       ffn_w_in_b, ffn_conv_w, ffn_conv_b, ffn_w_out_b, tail, layer=i)
        if i == DEPTH - 1:
            xs = out[0]
        else:
            xs, h = out
    return xs[None]
```

```python
import functools
import math

import jax
import jax.numpy as jnp
import numpy as np
from jax import lax
from jax.experimental import pallas as pl
from jax.experimental.pallas import tpu as pltpu

F32 = jnp.float32
BF16 = jnp.bfloat16

D_MODEL = 1024
DEPTH = 2
CHUNK = 64
GROUP_WIDTH = D_MODEL // 2
DK_A = 64
DV_A = 2 * DK_A
N_HEADS_A = GROUP_WIDTH // DV_A
DH_B = 64
N_HEADS_B = GROUP_WIDTH // DH_B
LEFT_CHUNKS = 8
REL_CLIP = 2 * CHUNK
NUM_BUCKETS = 32
MAX_DISTANCE = 128
DV_C = 128
DQK_C = DV_C // 2
N_HEADS_C = GROUP_WIDTH // DV_C
ROPE_BASE = 10000.0
S5_CH = GROUP_WIDTH
S5_GROUP = 16
S5_GROUPS = S5_CH // S5_GROUP
S5_STATE = 64
D_FF = ((8 * D_MODEL // 3 + 255) // 256) * 256
CONV_W = 3
EVEN_IN = 3 * N_HEADS_A * DV_A + 3 * N_HEADS_B * DH_B
ODD_IN = 2 * N_HEADS_C * DQK_C + 2 * N_HEADS_C * DV_C + S5_CH
EPS = 1e-6
NEG_INF = -1e30
LOG2E = math.log2(math.e)

LANES = 128
SUBLANES = 8
MXU_DIM = 256

TM_PROJ = 1024
TN_PROJ = 1024
TN_MOD = 1536
TM_FFN = 512
TF_FFN = MXU_DIM
BLK_A = 512
NPART_A = 2
ONES_A = 16
SINGLE_PASS_LOG2_RANGE = 96.0
SCORE_PAD = LANES
BLK_B = 4096
BAND_B = LEFT_CHUNKS * CHUNK
QW_B = 4 * CHUNK
BLK_C = 512
S5_T = 16
S5_TC = LANES

assert BLK_B % BAND_B == 0 and BLK_B % QW_B == 0 and BAND_B % QW_B == 0
assert BLK_A >= MAX_DISTANCE, "far key blocks must sit in the saturated T5 bucket"
assert DV_A == LANES and 2 * DK_A == LANES and 2 * DH_B == LANES, "attention heads are read as 128-lane column blocks"


def _dot(a, b):
    return jnp.dot(a, b, preferred_element_type=F32)


def _dot_nt(a, b):
    return lax.dot_general(a, b, (((1,), (1,)), ((), ())), preferred_element_type=F32)


def _dot_tn(a, b):
    return lax.dot_general(a, b, (((0,), (0,)), ((), ())), preferred_element_type=F32)


def _const_spec(shape):
    zeros = (0,) * len(shape)
    return pl.BlockSpec(shape, lambda *_: zeros, pipeline_mode=pl.Buffered(1))


def _mod_rmsnorm(x, g, scale, shift):
    y = x * lax.rsqrt(jnp.mean(x * x, axis=-1, keepdims=True) + EPS)
    y = y * g
    return y * (1.0 + scale) + shift


def _mod_kernel(c_ref, w_ref, b_ref, o_ref):
    c = c_ref[...]
    cond = c * jax.nn.sigmoid(c)
    o_ref[0] = jnp.sum(cond * w_ref[0], axis=0, keepdims=True) + b_ref[0]


def _modulation(c, mod_w, mod_b):
    depth, d, n = mod_w.shape
    tn = TN_MOD
    return pl.pallas_call(
        _mod_kernel,
        grid=(depth, n // tn),
        in_specs=[
            pl.BlockSpec((d, 1), lambda i, j: (0, 0)),
            pl.BlockSpec((1, d, tn), lambda i, j: (i, 0, j)),
            pl.BlockSpec((1, 1, tn), lambda i, j: (i, 0, j)),
        ],
        out_specs=pl.BlockSpec((1, 1, tn), lambda i, j: (i, 0, j)),
        out_shape=jax.ShapeDtypeStruct((depth, 1, n), F32),
        name="modulation",
    )(c.reshape(d, 1), mod_w, mod_b.reshape(depth, 1, n))


def _normproj_kernel(x_ref, g_ref, sc_ref, sh_ref, w_ref, o_ref):
    tm, n = o_ref.shape
    half = tm // 2
    for r in range(2):
        rows = slice(r * half, (r + 1) * half)
        h = _mod_rmsnorm(x_ref[rows, :], g_ref[...], sc_ref[...], sh_ref[...]).astype(BF16)
        for j in range(n // TN_PROJ):
            cols = slice(j * TN_PROJ, (j + 1) * TN_PROJ)
            o_ref[rows, cols] = _dot(h, w_ref[:, cols]).astype(o_ref.dtype)


def _normproj(x, g, scale, shift, w, out_dtype):
    seq, d = x.shape
    n = w.shape[1]
    tm = TM_PROJ
    row = pl.BlockSpec((1, d), lambda i: (0, 0))
    return pl.pallas_call(
        _normproj_kernel,
        grid=(seq // tm,),
        in_specs=[pl.BlockSpec((tm, d), lambda i: (i, 0)), row, row, row, _const_spec(w.shape)],
        out_specs=pl.BlockSpec((tm, n), lambda i: (i, 0)),
        out_shape=jax.ShapeDtypeStruct((seq, n), out_dtype),
        compiler_params=pltpu.CompilerParams(dimension_semantics=("parallel",)),
        name="normproj",
    )(x, g.reshape(1, d), scale, shift, w)


def _proj_kernel(h_ref, w_ref, o_ref):
    for j in range(o_ref.shape[1] // TN_PROJ):
        cols = slice(j * TN_PROJ, (j + 1) * TN_PROJ)
        o_ref[:, cols] = _dot(h_ref[...], w_ref[:, cols]).astype(o_ref.dtype)


def _proj(h, w, out_dtype):
    seq, d = h.shape
    n = w.shape[1]
    tm = TM_PROJ
    return pl.pallas_call(
        _proj_kernel,
        grid=(seq // tm,),
        in_specs=[pl.BlockSpec((tm, d), lambda i: (i, 0)), _const_spec(w.shape)],
        out_specs=pl.BlockSpec((tm, n), lambda i: (i, 0)),
        out_shape=jax.ShapeDtypeStruct((seq, n), out_dtype),
        compiler_params=pltpu.CompilerParams(dimension_semantics=("parallel",)),
        name="proj",
    )(h, w)


def _diffattn_kernel(qall_ref, k_ref, v_ref, bias_ref, bstat_ref, lam_ref, g_ref, o_ref,
                     flag_ref, qs_ref, vt_ref, kmax_ref, r_ref, m_ref, acc_ref, *s_refs, out_scale):
    blk = BLK_A
    nq = 2 * blk
    sub = SUBLANES
    dv = DV_A
    npart = NPART_A
    sa_ref, sb_ref = s_refs[:2 * npart], s_refs[2 * npart:4 * npart]
    pa_ref, pb_ref = s_refs[4 * npart:5 * npart], s_refs[5 * npart:6 * npart]
    i = pl.program_id(1)
    lane = lax.broadcasted_iota(jnp.int32, (blk, LANES), 1)
    same_subhead = (lax.broadcasted_iota(jnp.int32, (LANES, LANES), 0) // DK_A
                    == lax.broadcasted_iota(jnp.int32, (LANES, LANES), 1) // DK_A).astype(BF16)

    bias_max, bias_span = bstat_ref[0, 0:1, 0:1], bstat_ref[0, 1:2, 0:1]
    q_scale = DK_A ** -0.5 * LOG2E

    @pl.when(i == 0)
    def _():
        kmax_ref[...] = jnp.zeros(kmax_ref.shape, F32)

        def tr(b, qmax):
            r0 = pl.multiple_of(b * blk, blk)
            c0 = pl.multiple_of(b * nq, nq)
            vt_ref[0:dv, pl.ds(r0, blk)] = v_ref[pl.ds(r0, blk), :].astype(F32).T.astype(BF16)
            vt_ref[dv:dv + ONES_A, pl.ds(r0, blk)] = jnp.ones((ONES_A, blk), BF16)
            kf = k_ref[pl.ds(r0, blk), :].astype(F32)
            kn2 = _dot((kf * kf).astype(BF16), same_subhead)
            kmax_ref[...] = jnp.maximum(kmax_ref[...], jnp.max(kn2.reshape(blk // sub, sub, LANES), axis=0))
            qa = (qall_ref[pl.ds(r0, blk), :].astype(F32) * q_scale).astype(BF16).astype(F32)
            for m in range(2):
                qt = jnp.where((lane < DK_A) if m == 0 else (lane >= DK_A), qa, 0.0).T
                qs_ref[:, pl.ds(c0 + m * blk, blk)] = qt.astype(BF16)
                qn2 = jnp.sum(jnp.sum((qt * qt).reshape(LANES // sub, sub, blk), axis=0), axis=0, keepdims=True)
                qn2 = jnp.broadcast_to(qn2, (sub, blk))
                r_ref[:, pl.ds(c0 + m * blk, blk)] = qn2
                qmax = jnp.maximum(qmax, qn2)
            return qmax
        nblocks = v_ref.shape[0] // blk
        qmax2 = jnp.max(lax.fori_loop(0, nblocks, tr, jnp.zeros((sub, blk), F32)))
        kmax2 = jnp.max(kmax_ref[...], axis=0, keepdims=True)
        worst = 2.0 * jnp.sqrt(qmax2 * kmax2) * 1.03 + bias_span
        flag_ref[0] = (jnp.max(worst) < SINGLE_PASS_LOG2_RANGE).astype(jnp.int32)

        kmax_nq = jnp.concatenate([jnp.broadcast_to(kmax2[:, m * DK_A:m * DK_A + 1], (sub, blk)) for m in range(2)],
                                  axis=1)

        def shifts(b, carry):
            c0 = pl.multiple_of(b * nq, nq)
            r_ref[:, pl.ds(c0, nq)] = jnp.sqrt(r_ref[:, pl.ds(c0, nq)] * kmax_nq) * 1.03 + bias_max
            return carry
        lax.fori_loop(0, nblocks, shifts, 0)

    acc_ref[...] = jnp.zeros(acc_ref.shape, F32)
    q0 = pl.multiple_of(i * nq, nq)
    single_pass = flag_ref[0] == 1

    @pl.when(single_pass)
    def _():
        _diffattn_fixed_shift(i, q0, k_ref, bias_ref, qs_ref, vt_ref, r_ref, m_ref, acc_ref, pa_ref, pb_ref)

    @pl.when(jnp.logical_not(single_pass))
    def _():
        _diffattn_online(i, q0, k_ref, bias_ref, qs_ref, vt_ref, m_ref, acc_ref, sa_ref, sb_ref)

    ot = acc_ref[0:dv, 0:nq] / acc_ref[dv:dv + 1, 0:nq]
    o = ot[:, 0:blk].T - lam_ref[...] * ot[:, blk:nq].T
    o = o * lax.rsqrt(jnp.mean(o * o, axis=-1, keepdims=True) + EPS) * g_ref[...]
    o_ref[...] = (o * out_scale).astype(o_ref.dtype)


def _diffattn_fixed_shift(i, q0, k_ref, bias_ref, qs_ref, vt_ref, shift_ref, l_ref, acc_ref, pa_ref, pb_ref):
    blk = BLK_A
    nq = 2 * blk
    sub = SUBLANES
    npart = len(pa_ref)
    wq = nq // npart
    l_ref[...] = jnp.zeros(l_ref.shape, F32)

    def probs(b, p_ref, bias):
        k = k_ref[pl.ds(pl.multiple_of(b * blk, blk), blk), :]
        for part in range(npart):
            cols = slice(part * wq, (part + 1) * wq)
            qcols = pl.ds(pl.multiple_of(q0 + part * wq, wq), wq)
            s = _dot(k, qs_ref[:, qcols])
            if bias is not None:
                b0 = (part * wq) % blk
                s = s + bias[:, b0:b0 + wq]
            p = jnp.exp2(s.reshape(blk // sub, sub, wq) - shift_ref[:, qcols][None])
            l_ref[:, cols] += jnp.sum(p, axis=0)
            p_ref[part][:, 0:wq] = p.reshape(blk, wq).astype(BF16)

    def accumulate(b, p_ref):
        vt = vt_ref[0:DV_A, pl.ds(pl.multiple_of(b * blk, blk), blk)]
        for part in range(npart):
            cols = slice(part * wq, (part + 1) * wq)
            acc_ref[0:DV_A, cols] += _dot(vt, p_ref[part][:, 0:wq])

    @pl.when(i == 0)
    def _():
        probs(0, pa_ref, bias_ref[0, 1])
        accumulate(0, pa_ref)

    @pl.when(i > 0)
    def _():
        nfar = i - 1
        probs(i, pa_ref, bias_ref[0, 1])
        probs(i - 1, pb_ref, bias_ref[0, 0])
        accumulate(i, pa_ref)

        def pair(t):
            probs(2 * t, pa_ref, None)
            accumulate(jnp.where(t == 0, i - 1, 2 * t - 1), pb_ref)
            probs(2 * t + 1, pb_ref, None)
            accumulate(2 * t, pa_ref)

        def four_pairs(u, carry):
            for v in range(4):
                pair(4 * u + v)
            return carry

        npairs = nfar // 2
        lax.fori_loop(0, npairs // 4, four_pairs, 0)

        def one_pair(t, carry):
            pair(t)
            return carry
        lax.fori_loop(4 * (npairs // 4), npairs, one_pair, 0)
        in_pb = jnp.where(npairs == 0, i - 1, 2 * npairs - 1)

        @pl.when(lax.rem(nfar, 2) == 1)
        def _():
            probs(nfar - 1, pa_ref, None)
            accumulate(in_pb, pb_ref)
            accumulate(nfar - 1, pa_ref)

        @pl.when(lax.rem(nfar, 2) == 0)
        def _():
            accumulate(in_pb, pb_ref)

    acc_ref[DV_A:DV_A + sub, 0:nq] = jnp.broadcast_to(jnp.sum(l_ref[...], axis=0, keepdims=True), (sub, nq))


def _diffattn_online(i, q0, k_ref, bias_ref, qs_ref, vt_ref, m_ref, acc_ref, sa_ref, sb_ref):
    blk = BLK_A
    nq = 2 * blk
    sub = SUBLANES
    npart = len(sa_ref) // 2
    wq = nq // npart
    m_ref[...] = jnp.full(m_ref.shape, NEG_INF, F32)

    def scores(b, s_ref):
        k = k_ref[pl.ds(pl.multiple_of(b * blk, blk), blk), :]
        for part in range(npart):
            s = _dot(k, qs_ref[:, pl.ds(pl.multiple_of(q0 + part * wq, wq), wq)])
            s_ref[part][:, 0:wq] = s
            s_ref[npart + part][...] = jnp.max(s.reshape(blk // sub, sub, wq), axis=0)

    def softmax_pv(b, s_ref, bias):
        vt = vt_ref[:, pl.ds(pl.multiple_of(b * blk, blk), blk)]
        for part in range(npart):
            cols = slice(part * wq, (part + 1) * wq)
            s = s_ref[part][:, 0:wq]
            if bias is not None:
                b0 = (part * wq) % blk
                s = s + bias[:, b0:b0 + wq]
            s = s.reshape(blk // sub, sub, wq)
            m_prev = m_ref[:, cols]
            smax = jnp.max(s, axis=0) if bias is not None else s_ref[npart + part][...]
            m_cur = jnp.max(smax, axis=0, keepdims=True)
            m_new = jnp.maximum(m_prev, m_cur)
            alpha = jnp.exp2(m_prev - m_new)
            p = jnp.exp2(s - m_new[None])
            pv = _dot(vt, p.reshape(blk, wq).astype(BF16))
            acc_ref[:, cols] = acc_ref[:, cols] * alpha[0:1] + pv
            m_ref[:, cols] = m_new

    nfar = jnp.maximum(i - 1, 0)
    odd = lax.rem(nfar, 2)

    @pl.when(i == 0)
    def _():
        scores(0, sb_ref)

    @pl.when(i > 0)
    def _():
        @pl.when(odd == 1)
        def _():
            scores(0, sb_ref)
            scores(1, sa_ref)
            softmax_pv(0, sb_ref, None)

        @pl.when(odd == 0)
        def _():
            scores(0, sa_ref)

        def pair(b):
            scores(b + 1, sb_ref)
            softmax_pv(b, sa_ref, None)
            scores(b + 2, sa_ref)
            softmax_pv(b + 1, sb_ref, None)

        def quad_body(t, carry):
            pair(odd + 4 * t)
            pair(odd + 4 * t + 2)
            return carry

        npairs = nfar // 2
        lax.fori_loop(0, npairs // 2, quad_body, 0)

        @pl.when(lax.rem(npairs, 2) == 1)
        def _():
            pair(odd + 2 * (npairs - 1))
        scores(i, sb_ref)
        softmax_pv(i - 1, sa_ref, bias_ref[0, 0])

    softmax_pv(i, sb_ref, bias_ref[0, 1])


_TOEPLITZ_ROWS = 512


def _toeplitz_kernel(v_ref, o_ref, *, keep):
    rows, cols = o_ref.shape[2:]
    x = jnp.broadcast_to(v_ref[0, 0], (rows, v_ref.shape[-1]))
    tile = pltpu.roll(x, 0, 1, stride=1, stride_axis=0)[:, :cols]
    r = lax.broadcasted_iota(jnp.int32, (rows, cols), 0) + pl.program_id(1) * rows
    c = lax.broadcasted_iota(jnp.int32, (rows, cols), 1)
    for variant in range(o_ref.shape[0]):
        o_ref[variant, 0] = jnp.where(keep(r, c, variant), tile, NEG_INF)


def _toeplitz_tiles(fn, keep, heads, rows, cols, variants=1):
    n = rows + cols
    rb = rows if rows % _TOEPLITZ_ROWS else _TOEPLITZ_ROWS
    assert rows % rb == 0 and n % LANES == 0
    idx = jnp.arange(n, dtype=jnp.int32)
    vec = fn(jnp.where(idx < cols, idx, idx - n)).astype(F32)
    vecs = jnp.stack([jnp.roll(vec, k * rb, axis=1) for k in range(rows // rb)], axis=1)
    return pl.pallas_call(
        functools.partial(_toeplitz_kernel, keep=keep),
        grid=(heads, rows // rb),
        in_specs=[pl.BlockSpec((1, 1, 1, n), lambda h, k: (h, k, 0, 0))],
        out_specs=pl.BlockSpec((variants, 1, rb, cols), lambda h, k: (0, h, k, 0)),
        out_shape=jax.ShapeDtypeStruct((variants, heads, rows, cols), F32),
        name="toeplitz_tiles",
    )(vecs.reshape(heads, rows // rb, 1, n))


def _bias_stats(tiles):
    finite = tiles > 0.5 * NEG_INF
    bias_max = jnp.maximum(jnp.max(jnp.where(finite, tiles, NEG_INF), axis=(1, 2)), 0.0)
    bias_min = jnp.minimum(jnp.min(jnp.where(finite, tiles, -NEG_INF), axis=(1, 2)), 0.0)
    return jnp.broadcast_to(jnp.stack([bias_max, bias_max - bias_min], axis=1)[:, :, None],
                            (tiles.shape[0], 2, LANES))


def _t5_bucket(rel):
    nb = NUM_BUCKETS // 2
    max_exact = nb // 2
    bucket = jnp.where(rel > 0, nb, 0)
    n = jnp.abs(rel)
    nf = jnp.maximum(n, 1).astype(F32)
    large = max_exact + (jnp.log(nf / max_exact) / math.log(MAX_DISTANCE / max_exact)
                         * (nb - max_exact)).astype(jnp.int32)
    large = jnp.minimum(large, nb - 1)
    return bucket + jnp.where(n < max_exact, n, large)


def _diff_bias_tiles(t5_table):
    blk = BLK_A
    table = t5_table.astype(F32)
    far = table[_t5_bucket(jnp.full((), -(blk + 1), jnp.int32))]
    def visible(r, c, variant):
        return jnp.floor_divide(r - blk, CHUNK) <= jnp.floor_divide(c, CHUNK)

    tiles = _toeplitz_tiles(lambda x: ((table[_t5_bucket(-x - blk)] - far) * LOG2E).T, visible,
                            N_HEADS_A, 2 * blk, blk)
    return tiles.reshape(N_HEADS_A, 2, blk, blk)


def _diff_attention(proj, t5_table, lam, subln_g, lam_init):
    seq = proj.shape[0]
    blk = BLK_A
    bias = _diff_bias_tiles(t5_table)
    ha = N_HEADS_A
    bstat = _bias_stats(bias.reshape(ha, 2 * blk, blk))
    kern = functools.partial(_diffattn_kernel, out_scale=1.0 - lam_init)
    return pl.pallas_call(
        kern,
        grid=(ha, seq // blk),
        in_specs=[
            pl.BlockSpec((seq, DV_A), lambda h, i: (0, h)),
            pl.BlockSpec((seq, DV_A), lambda h, i: (0, ha + h)),
            pl.BlockSpec((seq, DV_A), lambda h, i: (0, 2 * ha + h)),
            pl.BlockSpec((1, 2, blk, blk), lambda h, i: (h, 0, 0, 0)),
            pl.BlockSpec((1, 2, LANES), lambda h, i: (h, 0, 0)),
            pl.BlockSpec((1, DV_A), lambda h, i: (0, 0)),
            pl.BlockSpec((1, DV_A), lambda h, i: (0, 0)),
        ],
        out_specs=pl.BlockSpec((blk, DV_A), lambda h, i: (i, h)),
        out_shape=jax.ShapeDtypeStruct((seq, ha * DV_A), BF16),
        scratch_shapes=[
            pltpu.SMEM((1,), jnp.int32),
            pltpu.VMEM((DV_A, 2 * seq), BF16),
            pltpu.VMEM((DV_A + ONES_A, seq), BF16),
            pltpu.VMEM((SUBLANES, LANES), F32),
            pltpu.VMEM((SUBLANES, 2 * seq), F32),
            pltpu.VMEM((SUBLANES, 2 * blk), F32),
            pltpu.VMEM((DV_A + ONES_A, 2 * blk), F32),
        ] + 2 * ([pltpu.VMEM((blk, 2 * blk // NPART_A + SCORE_PAD), F32)] * NPART_A
                 + [pltpu.VMEM((SUBLANES, 2 * blk // NPART_A), F32)] * NPART_A)
        + 2 * [pltpu.VMEM((blk, 2 * blk // NPART_A + SCORE_PAD), BF16)] * NPART_A,
        compiler_params=pltpu.CompilerParams(dimension_semantics=("parallel", "arbitrary")),
        name="diff_attention",
    )(proj, proj, proj, bias, bstat, jnp.full((1, DV_A), lam, F32), subln_g.reshape(1, DV_A).astype(F32))


def _band_kernel(q_ref, kp_ref, kc_ref, vp_ref, vc_ref, *refs):
    qw, band = QW_B, BAND_B
    nbias = band // qw + 1
    bias_refs, bstat_ref, o_ref = refs[:nbias], refs[nbias], refs[nbias + 1]
    p_refs = refs[nbias + 2:]
    ngroups = len(p_refs)
    nk = band + qw
    sub = SUBLANES
    q = (q_ref[...].astype(F32) * (DH_B ** -0.5 * LOG2E)).astype(BF16).astype(F32)
    lane = lax.broadcasted_iota(jnp.int32, q.shape, 1)
    qt = (jnp.where(lane < DH_B, q, 0.0).T, jnp.where(lane >= DH_B, q, 0.0).T)
    qh = (qt[0].astype(BF16), qt[1].astype(BF16))
    k_all = jnp.concatenate([kp_ref[...], kc_ref[...]], axis=0)
    vt_all = jnp.concatenate([vp_ref[...], vc_ref[...]], axis=0).astype(F32).T.astype(BF16)

    def group_operands(g):
        k0 = g * qw
        qs = jnp.concatenate([qh[0][:, k0:k0 + qw], qh[1][:, k0:k0 + qw]], axis=1)
        bias_ref = bias_refs[min(g, nbias - 1)]
        bias = jnp.concatenate([bias_ref[0, 0], bias_ref[0, 1]], axis=1)
        return k0, qs, bias

    def store_group(g, ot):
        o = jnp.concatenate([ot[0:DH_B, 0:qw], ot[DH_B:2 * DH_B, qw:2 * qw]], axis=0)
        o_ref[g * qw:(g + 1) * qw, :] = o.T.astype(o_ref.dtype)

    same_head = (lax.broadcasted_iota(jnp.int32, (LANES, LANES), 0) // DH_B
                 == lax.broadcasted_iota(jnp.int32, (LANES, LANES), 1) // DH_B).astype(BF16)
    kf = k_all.astype(F32)
    kn2 = _dot((kf * kf).astype(BF16), same_head)
    kmax2 = jnp.max(jnp.max(kn2.reshape(kn2.shape[0] // sub, sub, LANES), axis=0), axis=0, keepdims=True)
    shifts, worst = [], None
    for m in range(2):
        qn2 = jnp.sum(qt[m] * qt[m], axis=0, keepdims=True)
        bound = jnp.sqrt(qn2 * kmax2[:, m * DH_B:m * DH_B + 1]) * 1.03
        shifts.append(bound + bstat_ref[m, 0:1, 0:1])
        spread = jnp.max(2.0 * bound + bstat_ref[m, 1:2, 0:1])
        worst = spread if worst is None else jnp.maximum(worst, spread)
    fixed_shift = worst < SINGLE_PASS_LOG2_RANGE

    @pl.when(fixed_shift)
    def _():
        sums = []
        for g in range(ngroups):
            k0, qs, bias = group_operands(g)
            r = jnp.concatenate([shifts[0][:, k0:k0 + qw], shifts[1][:, k0:k0 + qw]], axis=1)
            s = _dot(k_all[k0:k0 + nk], qs) + (bias - r)
            p = jnp.exp2(s).reshape(nk // sub, sub, 2 * qw)
            sums.append(jnp.sum(jnp.sum(p, axis=0), axis=0, keepdims=True))
            p_refs[g][:, 0:2 * qw] = p.reshape(nk, 2 * qw).astype(BF16)
        for g in range(ngroups):
            k0 = g * qw
            store_group(g, _dot(vt_all[:, k0:k0 + nk], p_refs[g][:, 0:2 * qw]) / sums[g])

    @pl.when(jnp.logical_not(fixed_shift))
    def _():
        vt_ones = jnp.concatenate([vt_all, jnp.ones((ONES_A, vt_all.shape[1]), BF16)], axis=0)
        for g in range(ngroups):
            k0, qs, bias = group_operands(g)
            s = (_dot(k_all[k0:k0 + nk], qs) + bias).reshape(nk // sub, sub, 2 * qw)
            m = jnp.max(jnp.max(s, axis=0), axis=0, keepdims=True)
            p = jnp.exp2(s - m[None])
            pv = _dot(vt_ones[:, k0:k0 + nk], p.reshape(nk, 2 * qw).astype(BF16))
            store_group(g, pv[0:2 * DH_B] / pv[2 * DH_B:2 * DH_B + 1])


def _band_bias_tiles(rel_bias):
    band = BAND_B

    def valid(r, c, variant):
        qchunk = jnp.floor_divide(c, CHUNK)
        kchunk = jnp.floor_divide(r - band, CHUNK)
        missing = jnp.where(variant == 0, 0, band - (variant - 1) * QW_B)
        return (kchunk <= qchunk) & (kchunk >= qchunk - LEFT_CHUNKS) & (r >= missing)

    return _toeplitz_tiles(
        lambda x: rel_bias.astype(F32)[:, jnp.clip(-x - band, -REL_CLIP, REL_CLIP) + REL_CLIP] * LOG2E, valid,
        N_HEADS_B, band + QW_B, QW_B, variants=1 + band // QW_B)


def _band_attention(proj, rel_bias):
    seq = proj.shape[0]
    blk, band, qw = BLK_B, BAND_B, QW_B
    bias = _band_bias_tiles(rel_bias)
    npair = N_HEADS_B // 2
    qc0 = 3 * N_HEADS_A
    per = blk // band
    prev = lambda c0: (lambda hp, i: (jnp.maximum(i * per - 1, 0), c0 + hp))
    cur = lambda c0: (lambda hp, i: (i, c0 + hp))
    return pl.pallas_call(
        _band_kernel,
        grid=(npair, seq // blk),
        in_specs=[
            pl.BlockSpec((blk, LANES), cur(qc0)),
            pl.BlockSpec((band, LANES), prev(qc0 + npair)),
            pl.BlockSpec((blk, LANES), cur(qc0 + npair)),
            pl.BlockSpec((band, LANES), prev(qc0 + 2 * npair)),
            pl.BlockSpec((blk, LANES), cur(qc0 + 2 * npair)),
        ] + [
            pl.BlockSpec((1, 2, band + qw, qw), (lambda hp, i, t=t: (jnp.where(i == 0, 1 + t, 0), hp, 0, 0)))
            for t in range(band // qw)
        ] + [
            pl.BlockSpec((1, 2, band + qw, qw), lambda hp, i: (0, hp, 0, 0)),
            pl.BlockSpec((2, 2, LANES), lambda hp, i: (hp, 0, 0)),
        ],
        out_specs=pl.BlockSpec((blk, LANES), lambda hp, i: (i, hp)),
        out_shape=jax.ShapeDtypeStruct((seq, N_HEADS_B * DH_B), BF16),
        scratch_shapes=[pltpu.VMEM((band + qw, 2 * qw + SCORE_PAD), BF16)] * (blk // qw),
        compiler_params=pltpu.CompilerParams(dimension_semantics=("parallel", "arbitrary")),
        name="band_attention",
    )(proj, proj, proj, proj, proj, *([bias] * bias.shape[0]), _bias_stats(bias[0]))


def _retention_kernel(qk_ref, v_ref, gate_ref, cos_ref, sin_ref, qdec_ref, kdec_ref, dmat_ref,
                      sdec_ref, o_ref, state_ref):
    @pl.when(pl.program_id(0) == 0)
    def _():
        state_ref[...] = jnp.zeros(state_ref.shape, F32)

    cos = cos_ref[...]
    sin = sin_ref[...]
    lane = lax.broadcasted_iota(jnp.int32, cos.shape, 1)
    first_half = (lane % DQK_C) < (DQK_C // 2)
    qk = qk_ref[...]
    parts = []
    for j in range(qk.shape[1] // LANES):
        t = qk[:, j * LANES:(j + 1) * LANES]
        partner = jnp.where(first_half, pltpu.roll(t, LANES - DQK_C // 2, 1), pltpu.roll(t, DQK_C // 2, 1))
        parts.append(t * cos + partner * sin)
    wq = N_HEADS_C * DQK_C
    q = jnp.concatenate(parts[:wq // LANES], axis=1)
    k = jnp.concatenate(parts[wq // LANES:], axis=1) * (DQK_C ** -0.5)
    qd = (q * qdec_ref[...]).astype(BF16)
    kd = (k * kdec_ref[...]).astype(BF16)
    qb = q.astype(BF16)
    kb = k.astype(BF16)
    vb = v_ref[...].astype(BF16)
    gate = gate_ref[...]
    outs = []
    for h in range(N_HEADS_C):
        qs = slice(h * DQK_C, (h + 1) * DQK_C)
        vs = slice(h * DV_C, (h + 1) * DV_C)
        scores = _dot_nt(qb[:, qs], kb[:, qs]) * dmat_ref[h]
        state = state_ref[h]
        r = _dot(scores.astype(BF16), vb[:, vs]) + _dot(qd[:, qs], state.astype(BF16))
        state_ref[h] = state * sdec_ref[h] + _dot_tn(kd[:, qs], vb[:, vs])
        r = r * lax.rsqrt(jnp.mean(r * r, axis=-1, keepdims=True) + EPS)
        g = gate[:, vs]
        outs.append(r * (g * jax.nn.sigmoid(g)))
    o_ref[...] = jnp.concatenate(outs, axis=1).astype(o_ref.dtype)


def _retention_tables(seq):
    t = BLK_C
    half = DQK_C // 2
    inv_freq = 1.0 / np.power(ROPE_BASE, np.arange(0, DQK_C, 2, dtype=np.float64) / DQK_C)
    ang = np.arange(seq, dtype=np.float64)[:, None] * inv_freq[None, :]
    reps = LANES // half
    cos = np.tile(np.cos(ang), (1, reps))
    sign = np.where((np.arange(LANES) % DQK_C) < half, -1.0, 1.0)
    sin = np.tile(np.sin(ang), (1, reps)) * sign[None, :]
    log_g = np.log(1.0 - np.power(2.0, -5.0 - np.arange(N_HEADS_C, dtype=np.float64)))
    pos = np.arange(t, dtype=np.float64)
    diff = pos[:, None] - pos[None, :]
    same_or_past = (np.arange(t)[None, :] // CHUNK) <= (np.arange(t)[:, None] // CHUNK)
    dmat = np.where(same_or_past[None], np.exp(log_g[:, None, None] * np.abs(diff)[None]), 0.0)
    qdec = np.repeat(np.exp(log_g[None, :] * (pos[:, None] + 1.0)), DQK_C, axis=1)
    kdec = np.repeat(np.exp(log_g[None, :] * (t - 1.0 - pos[:, None])), DQK_C, axis=1)
    sdec = np.broadcast_to(np.exp(log_g * t)[:, None, None], (N_HEADS_C, 1, DV_C))
    return tuple(jnp.asarray(a.astype(np.float32)) for a in (cos, sin, qdec, kdec, dmat, sdec))


def _retention(proj):
    seq = proj.shape[0]
    t = BLK_C
    cos, sin, qdec, kdec, dmat, sdec = _retention_tables(seq)
    wv = N_HEADS_C * DV_C
    return pl.pallas_call(
        _retention_kernel,
        grid=(seq // t,),
        in_specs=[
            pl.BlockSpec((t, wv), lambda i: (i, 0)),
            pl.BlockSpec((t, wv), lambda i: (i, 1)),
            pl.BlockSpec((t, wv), lambda i: (i, 2)),
            pl.BlockSpec((t, LANES), lambda i: (i, 0)),
            pl.BlockSpec((t, LANES), lambda i: (i, 0)),
            pl.BlockSpec((t, N_HEADS_C * DQK_C), lambda i: (0, 0)),
            pl.BlockSpec((t, N_HEADS_C * DQK_C), lambda i: (0, 0)),
            pl.BlockSpec((N_HEADS_C, t, t), lambda i: (0, 0, 0)),
            pl.BlockSpec((N_HEADS_C, 1, DV_C), lambda i: (0, 0, 0)),
        ],
        out_specs=pl.BlockSpec((t, wv), lambda i: (i, 0)),
        out_shape=jax.ShapeDtypeStruct((seq, wv), BF16),
        scratch_shapes=[pltpu.VMEM((N_HEADS_C, DQK_C, DV_C), F32)],
        compiler_params=pltpu.CompilerParams(dimension_semantics=("arbitrary",)),
        name="retention",
    )(proj, proj, proj, cos, sin, qdec, kdec, dmat, sdec)


def _s5_kernel(*refs):
    ncb = S5_CH // LANES
    u_refs = refs[:ncb]
    (mt_ref, bt_ref, ctr_ref, cti_ref, are_ref, aim_ref, y_ref,
     ut_ref, yt_ref, ys_ref, vr_ref, vi_ref, spr_ref, spi_ref, carry_ref) = refs[ncb:]
    tc = S5_TC
    gp = S5_GROUP
    n = S5_STATE
    ng = S5_GROUPS

    @pl.when(pl.program_id(0) == 0)
    def _():
        carry_ref[...] = jnp.zeros(carry_ref.shape, F32)

    for s in range(S5_T):
        for k in range(ncb):
            ut_ref[s, k * LANES:(k + 1) * LANES, :] = u_refs[k][pl.ds(s, tc, stride=S5_T), :].T

    for g in range(ng):
        ug = ut_ref[:, g * gp:(g + 1) * gp, :].reshape(S5_T * gp, tc).astype(BF16)
        yt_ref[:, g * gp:(g + 1) * gp, :] = _dot(mt_ref[g], ug).reshape(S5_T, gp, tc)
        vt = _dot(bt_ref[g], ug)
        vr_ref[g * n:(g + 1) * n, :] = vt[0:n]
        vi_ref[g * n:(g + 1) * n, :] = vt[n:2 * n]

    sub = SUBLANES
    nv = tc // sub
    row = lax.broadcasted_iota(jnp.int32, (tc, LANES), 0)
    in_vreg = lax.rem(row, sub)

    def rows_of(v, r):
        return jnp.broadcast_to(v[r:r + 1], (tc, LANES))

    for j in range(ng * n // LANES):
        cols = slice(j * LANES, (j + 1) * LANES)
        pwr, pwi = are_ref[:, cols], aim_ref[:, cols]
        xr = vr_ref[cols, :].T
        xi = vi_ref[cols, :].T
        for d in (1, 2, 4):
            keep = in_vreg >= d
            sr = jnp.where(keep, pltpu.roll(xr, d, 0), 0.0)
            si = jnp.where(keep, pltpu.roll(xi, d, 0), 0.0)
            fr, fi = rows_of(pwr, d - 1), rows_of(pwi, d - 1)
            xr, xi = xr + (fr * sr - fi * si), xi + (fr * si + fi * sr)
        cr, ci = carry_ref[0, :, cols], carry_ref[1, :, cols]
        cr0, ci0 = cr, ci
        outr, outi = [], []
        for v in range(nv):
            yr = xr[v * sub:(v + 1) * sub] + (pwr * cr - pwi * ci)
            yi = xi[v * sub:(v + 1) * sub] + (pwr * ci + pwi * cr)
            outr.append(yr)
            outi.append(yi)
            cr = jnp.broadcast_to(yr[sub - 1:sub], (sub, LANES))
            ci = jnp.broadcast_to(yi[sub - 1:sub], (sub, LANES))
        carry_ref[0, :, cols] = cr
        carry_ref[1, :, cols] = ci
        sr = jnp.concatenate(outr, axis=0)
        si = jnp.concatenate(outi, axis=0)
        first = row == 0
        spr_ref[j] = jnp.where(first, rows_of(cr0, 0), pltpu.roll(sr, 1, 0))
        spi_ref[j] = jnp.where(first, rows_of(ci0, 0), pltpu.roll(si, 1, 0))

    for jp in range(ng // 2):
        yc = (_dot_nt(ctr_ref[jp], spr_ref[jp].astype(BF16))
              + _dot_nt(cti_ref[jp], spi_ref[jp].astype(BF16)))
        yt_ref[:, jp * 2 * gp:(jp + 1) * 2 * gp, :] += yc.reshape(S5_T, 2 * gp, tc)

    for s in range(S5_T):
        for k in range(ncb):
            ys_ref[k, pl.ds(s, tc, stride=S5_T), :] = yt_ref[s, k * LANES:(k + 1) * LANES, :].T
    for k in range(ncb):
        y_ref[:, k * LANES:(k + 1) * LANES] = ys_ref[k]


def _s5_matrices(lam_re, lam_im, log_step, b_re, b_im, c_re, c_im, d_skip):
    hi = lax.Precision.HIGHEST
    t, gp, n, ng = S5_T, S5_GROUP, S5_STATE, S5_GROUPS
    lam = lax.complex(lam_re.astype(F32), lam_im.astype(F32))
    step = jnp.exp(log_step.astype(F32))[:, None]
    ls = lam * step
    a_bar = jnp.exp(ls)
    b_bar = ((a_bar - 1.0) / lam)[..., None] * lax.complex(b_re.astype(F32), b_im.astype(F32))
    cm = lax.complex(c_re.astype(F32), c_im.astype(F32))

    def apow(k):
        kk = k.astype(F32).astype(jnp.complex64)
        return jnp.exp(ls.reshape((ng,) + (1,) * k.ndim + (n,)) * kk[None, ..., None])

    tt = jnp.arange(t)
    kmat = jnp.einsum('gpn,gln,gnq->glpq', cm, apow(tt), b_bar, precision=hi).real
    krev = jnp.transpose(kmat[:, ::-1], (0, 2, 1, 3)).reshape(ng, gp, t * gp)
    kpad = jnp.pad(krev, ((0, 0), (0, 0), (0, t * gp)))
    mt = jnp.concatenate([kpad[:, :, (t - 1 - to) * gp:(2 * t - 1 - to) * gp] for to in range(t)], axis=1)
    dvec = jnp.tile(d_skip.astype(F32).reshape(ng, 1, gp), (1, t, 1)).reshape(ng, t * gp)
    mt = mt + jnp.eye(t * gp, dtype=F32)[None] * dvec[:, :, None]
    z = jnp.swapaxes(apow(t - 1 - tt), 1, 2)[:, :, :, None] * b_bar[:, :, None, :]
    z = z.reshape(ng, n, t * gp)
    bt = jnp.concatenate([z.real, z.imag], axis=1)
    w = cm[:, None, :, :] * apow(tt + 1)[:, :, None, :]

    def pair_readout(x):
        x = x.reshape(ng // 2, 2, t, gp, n)
        first = jnp.pad(x[:, 0], ((0, 0), (0, 0), (0, 0), (0, n)))
        second = jnp.pad(x[:, 1], ((0, 0), (0, 0), (0, 0), (n, 0)))
        return jnp.stack([first, second], axis=2).reshape(ng // 2, t * 2 * gp, 2 * n).astype(BF16)

    ctr, cti = pair_readout(w.real), pair_readout(-w.imag)
    a_chunk = jnp.transpose(apow(t * (jnp.arange(SUBLANES) + 1)), (1, 0, 2)).reshape(SUBLANES, ng * n)
    return mt.astype(BF16), bt.astype(BF16), ctr, cti, a_chunk.real, a_chunk.imag


def _s5(proj, mats):
    seq, width = proj.shape
    t, tc, gp, n, ng = S5_T, S5_TC, S5_GROUP, S5_STATE, S5_GROUPS
    rows = t * tc
    ncb = S5_CH // LANES
    cb0 = (width - S5_CH) // LANES
    u_specs = [pl.BlockSpec((rows, LANES), (lambda i, k=k: (i, cb0 + k))) for k in range(ncb)]
    nsb = ng * n // LANES
    return pl.pallas_call(
        _s5_kernel,
        grid=(seq // rows,),
        in_specs=u_specs + [_const_spec(m.shape) for m in mats],
        out_specs=pl.BlockSpec((rows, S5_CH), lambda i: (i, 0)),
        out_shape=jax.ShapeDtypeStruct((seq, S5_CH), F32),
        scratch_shapes=[
            pltpu.VMEM((t, S5_CH, tc), F32),
            pltpu.VMEM((t, S5_CH, tc), F32),
            pltpu.VMEM((ncb, rows, LANES), F32),
            pltpu.VMEM((ng * n, tc), F32),
            pltpu.VMEM((ng * n, tc), F32),
            pltpu.VMEM((nsb, tc, LANES), F32),
            pltpu.VMEM((nsb, tc, LANES), F32),
            pltpu.VMEM((2, SUBLANES, ng * n), F32),
        ],
        compiler_params=pltpu.CompilerParams(dimension_semantics=("arbitrary",)),
        name="s5_scan",
    )(*([proj] * ncb), *mats)


def _mix_ffn_kernel(*refs, glu, final):
    (x_ref, a_ref, b_ref, wo_ref, g1_ref), refs = refs[:5], refs[5:]
    if glu:
        gw_ref, refs = refs[0], refs[1:]
    (g_ref, sc_ref, sh_ref, gate_ref, win_ref, cw_ref, cb_ref, wout_ref), refs = refs[:8], refs[8:]
    if final:
        fg_ref, o_ref, h_ref, act_ref, gbuf_ref, carry_ref = refs
    else:
        ng_ref, nsc_ref, nsh_ref, o_ref, hn_ref, h_ref, act_ref, gbuf_ref, carry_ref = refs
    tm = x_ref.shape[0]
    halo = gbuf_ref.shape[0] - tm

    @pl.when(pl.program_id(0) == 0)
    def _():
        carry_ref[...] = jnp.zeros(carry_ref.shape, F32)

    if glu:
        y = jax.nn.gelu(b_ref[...]).astype(BF16)
        gg = _dot(y, gw_ref[...])
        half = gg.shape[1] // 2
        b = (gg[:, :half] * jax.nn.sigmoid(gg[:, half:])).astype(BF16)
    else:
        b = b_ref[...]
    cat = jnp.concatenate([a_ref[...], b], axis=1)
    x = x_ref[...] + g1_ref[...] * _dot(cat, wo_ref[...])
    h_ref[...] = _mod_rmsnorm(x, g_ref[...], sc_ref[...], sh_ref[...]).astype(BF16)
    for f in range(D_FF // TF_FFN):
        cs = slice(f * TF_FFN, (f + 1) * TF_FFN)
        gs = slice(D_FF + f * TF_FFN, D_FF + (f + 1) * TF_FFN)
        h = h_ref[...]
        val = _dot(h, win_ref[:, cs])
        gate = _dot(h, win_ref[:, gs])
        gbuf_ref[0:halo, :] = carry_ref[:, cs]
        gbuf_ref[halo:halo + tm, :] = gate
        carry_ref[:, cs] = gate[tm - halo:tm, :]
        conv = (gate * cw_ref[2:3, cs] + gbuf_ref[halo - 1:halo - 1 + tm, :] * cw_ref[1:2, cs]
                + gbuf_ref[halo - 2:halo - 2 + tm, :] * cw_ref[0:1, cs] + cb_ref[:, cs])
        act_ref[:, cs] = (jax.nn.gelu(conv) * val).astype(BF16)
    xn = x + gate_ref[...] * _dot(act_ref[...], wout_ref[...])
    if final:
        xn = xn * lax.rsqrt(jnp.mean(xn * xn, axis=-1, keepdims=True) + EPS) * fg_ref[...]
    else:
        hn_ref[...] = _mod_rmsnorm(xn, ng_ref[...], nsc_ref[...], nsh_ref[...]).astype(BF16)
    o_ref[...] = xn


def _layer_spec(shape, layer):
    idx = (layer,) + (0,) * (len(shape) - 1)
    return pl.BlockSpec((None,) + tuple(shape[1:]), lambda *_: idx, pipeline_mode=pl.Buffered(1))


def _mix_ffn(x, a, b, wo, gate1, glu_w, g, scale, shift, gate2, w_in, conv_w, conv_b, w_out, tail, layer):
    seq, d = x.shape
    final = len(tail) == 1
    tm = TM_FFN
    halo = SUBLANES
    row = pl.BlockSpec((1, d), lambda i: (0, 0))
    rows = lambda w: pl.BlockSpec((tm, w), lambda i: (i, 0))
    conv_b = conv_b.reshape(conv_b.shape[0], 1, D_FF)
    in_specs = [rows(d), rows(a.shape[1]), rows(b.shape[1]), _const_spec(wo.shape), row]
    args = [x, a, b, wo, gate1]
    if glu_w is not None:
        in_specs.append(_const_spec(glu_w.shape))
        args.append(glu_w)
    in_specs += [
        row, row, row, row,
        _layer_spec(w_in.shape, layer),
        _layer_spec(conv_w.shape, layer),
        _layer_spec(conv_b.shape, layer),
        _layer_spec(w_out.shape, layer),
    ] + [row] * len(tail)
    args += [g.reshape(1, d), scale, shift, gate2, w_in, conv_w, conv_b, w_out]
    args += [t.reshape(1, d) for t in tail]
    out_specs = [rows(d)] if final else [rows(d), rows(d)]
    out_shape = [jax.ShapeDtypeStruct((seq, d), F32)] + ([] if final else [jax.ShapeDtypeStruct((seq, d), BF16)])
    return pl.pallas_call(
        functools.partial(_mix_ffn_kernel, glu=glu_w is not None, final=final),
        grid=(seq // tm,),
        in_specs=in_specs,
        out_specs=out_specs,
        out_shape=out_shape,
        scratch_shapes=[
            pltpu.VMEM((tm, d), BF16),
            pltpu.VMEM((tm, D_FF), BF16),
            pltpu.VMEM((tm + halo, TF_FFN), F32),
            pltpu.VMEM((halo, D_FF), F32),
        ],
        compiler_params=pltpu.CompilerParams(dimension_semantics=("arbitrary",)),
        name="mix_ffn",
    )(*args)


def kernel(x, c, t5_table, mod_w, mod_b, norm1_g, norm2_g, ffn_w_in, ffn_conv_w, ffn_conv_b, ffn_w_out,
           ev_w_in, ev_w_out, diff_lambda, diff_subln_g, band_rel_bias,
           od_w_in, od_w_out, s5_lam_re, s5_lam_im, s5_log_step, s5_b_re, s5_b_im, s5_c_re, s5_c_im,
           s5_d, s5_glu_w, final_g):
    assert x.shape[0] == 1 and x.shape[2] == D_MODEL
    seq = x.shape[1]
    assert seq % TM_PROJ == 0 and seq % (S5_T * S5_TC) == 0
    d = D_MODEL
    xs = x[0]
    mod = _modulation(c, mod_w, mod_b)
    ffn_w_in_b = ffn_w_in.astype(BF16)
    ffn_w_out_b = ffn_w_out.astype(BF16)
    mods = [[mod[i, :, k * d:(k + 1) * d] for k in range(6)] for i in range(DEPTH)]
    h = None
    for i in range(DEPTH):
        sh1, sc1, g1, sh2, sc2, g2 = mods[i]
        w_in = (ev_w_in if i % 2 == 0 else od_w_in)[i // 2].astype(BF16)
        proj_dtype = BF16 if i % 2 == 0 else F32
        if h is None:
            proj = _normproj(xs, norm1_g[i], sc1, sh1, w_in, proj_dtype)
        else:
            proj = _proj(h, w_in, proj_dtype)
        if i % 2 == 0:
            e = i // 2
            lam_init = 0.8 - 0.6 * math.exp(-0.3 * i)
            lp = diff_lambda[e].astype(F32)
            lam = jnp.exp(jnp.sum(lp[0] * lp[1])) - jnp.exp(jnp.sum(lp[2] * lp[3])) + lam_init
            mix_a = _diff_attention(proj, t5_table, lam, diff_subln_g[e], lam_init)
            mix_b = _band_attention(proj, band_rel_bias[e])
            wo, glu_w = ev_w_out[e].astype(BF16), None
        else:
            o = i // 2
            mix_a = _retention(proj)
            mats = _s5_matrices(s5_lam_re[o], s5_lam_im[o], s5_log_step[o], s5_b_re[o], s5_b_im[o],
                                s5_c_re[o], s5_c_im[o], s5_d[o])
            mix_b = _s5(proj, mats)
            wo, glu_w = od_w_out[o].astype(BF16), s5_glu_w[o].astype(BF16)
        if i == DEPTH - 1:
            tail = (final_g,)
        else:
            nsh1, nsc1 = mods[i + 1][0], mods[i + 1][1]
            tail = (norm1_g[i + 1], nsc1, nsh1)
        out = _mix_ffn(xs, mix_a, mix_b, wo, g1, glu_w, norm2_g[i], sc2, sh2, g2,
                       ffn_w_in_b, ffn_conv_w, ffn_conv_b, ffn_w_out_b, tail, layer=i)
        if i == DEPTH - 1:
            xs = out[0]
        else:
            xs, h = out
    return xs[None]
```

```python
import functools
import math

import jax
import jax.numpy as jnp
import numpy as np
from jax import lax
from jax.experimental import pallas as pl
from jax.experimental.pallas import tpu as pltpu

F32 = jnp.float32
BF16 = jnp.bfloat16

D_MODEL = 1024
DEPTH = 2
CHUNK = 64
GROUP_WIDTH = D_MODEL // 2
DK_A = 64
DV_A = 2 * DK_A
N_HEADS_A = GROUP_WIDTH // DV_A
DH_B = 64
N_HEADS_B = GROUP_WIDTH // DH_B
LEFT_CHUNKS = 8
REL_CLIP = 2 * CHUNK
NUM_BUCKETS = 32
MAX_DISTANCE = 128
DV_C = 128
DQK_C = DV_C // 2
N_HEADS_C = GROUP_WIDTH // DV_C
ROPE_BASE = 10000.0
S5_CH = GROUP_WIDTH
S5_GROUP = 16
S5_GROUPS = S5_CH // S5_GROUP
S5_STATE = 64
D_FF = ((8 * D_MODEL // 3 + 255) // 256) * 256
CONV_W = 3
EVEN_IN = 3 * N_HEADS_A * DV_A + 3 * N_HEADS_B * DH_B
ODD_IN = 2 * N_HEADS_C * DQK_C + 2 * N_HEADS_C * DV_C + S5_CH
EPS = 1e-6
NEG_INF = -1e30
LOG2E = math.log2(math.e)

LANES = 128
SUBLANES = 8
MXU_DIM = 256

TM_PROJ = 1024
TN_PROJ = 1024
TN_MOD = 1536
TM_FFN = 512
TF_FFN = MXU_DIM
BLK_A = 512
NPART_A = 2
ONES_A = 16
SINGLE_PASS_LOG2_RANGE = 96.0
SCORE_PAD = LANES
BLK_B = 4096
BAND_B = LEFT_CHUNKS * CHUNK
QW_B = 4 * CHUNK
BLK_C = 512
S5_T = 16
S5_TC = LANES

assert BLK_B % BAND_B == 0 and BLK_B % QW_B == 0 and BAND_B % QW_B == 0
assert BLK_A >= MAX_DISTANCE, "far key blocks must sit in the saturated T5 bucket"
assert DV_A == LANES and 2 * DK_A == LANES and 2 * DH_B == LANES, "attention heads are read as 128-lane column blocks"


def _dot(a, b):
    return jnp.dot(a, b, preferred_element_type=F32)


def _dot_nt(a, b):
    return lax.dot_general(a, b, (((1,), (1,)), ((), ())), preferred_element_type=F32)


def _dot_tn(a, b):
    return lax.dot_general(a, b, (((0,), (0,)), ((), ())), preferred_element_type=F32)


def _const_spec(shape):
    zeros = (0,) * len(shape)
    return pl.BlockSpec(shape, lambda *_: zeros, pipeline_mode=pl.Buffered(1))


def _mod_rmsnorm(x, g, scale, shift):
    y = x * lax.rsqrt(jnp.mean(x * x, axis=-1, keepdims=True) + EPS)
    y = y * g
    return y * (1.0 + scale) + shift


def _mod_kernel(c_ref, w_ref, b_ref, o_ref):
    c = c_ref[...]
    cond = c * jax.nn.sigmoid(c)
    o_ref[0] = jnp.sum(cond * w_ref[0], axis=0, keepdims=True) + b_ref[0]


def _modulation(c, mod_w, mod_b):
    depth, d, n = mod_w.shape
    tn = TN_MOD
    return pl.pallas_call(
        _mod_kernel,
        grid=(depth, n // tn),
        in_specs=[
            pl.BlockSpec((d, 1), lambda i, j: (0, 0)),
            pl.BlockSpec((1, d, tn), lambda i, j: (i, 0, j)),
            pl.BlockSpec((1, 1, tn), lambda i, j: (i, 0, j)),
        ],
        out_specs=pl.BlockSpec((1, 1, tn), lambda i, j: (i, 0, j)),
        out_shape=jax.ShapeDtypeStruct((depth, 1, n), F32),
        name="modulation",
    )(c.reshape(d, 1), mod_w, mod_b.reshape(depth, 1, n))


def _normproj_kernel(x_ref, g_ref, sc_ref, sh_ref, w_ref, o_ref):
    tm, n = o_ref.shape
    half = tm // 2
    for r in range(2):
        rows = slice(r * half, (r + 1) * half)
        h = _mod_rmsnorm(x_ref[rows, :], g_ref[...], sc_ref[...], sh_ref[...]).astype(BF16)
        for j in range(n // TN_PROJ):
            cols = slice(j * TN_PROJ, (j + 1) * TN_PROJ)
            o_ref[rows, cols] = _dot(h, w_ref[:, cols]).astype(o_ref.dtype)


def _normproj(x, g, scale, shift, w, out_dtype):
    seq, d = x.shape
    n = w.shape[1]
    tm = TM_PROJ
    row = pl.BlockSpec((1, d), lambda i: (0, 0))
    return pl.pallas_call(
        _normproj_kernel,
        grid=(seq // tm,),
        in_specs=[pl.BlockSpec((tm, d), lambda i: (i, 0)), row, row, row, _const_spec(w.shape)],
        out_specs=pl.BlockSpec((tm, n), lambda i: (i, 0)),
        out_shape=jax.ShapeDtypeStruct((seq, n), out_dtype),
        compiler_params=pltpu.CompilerParams(dimension_semantics=("parallel",)),
        name="normproj",
    )(x, g.reshape(1, d), scale, shift, w)


def _proj_kernel(h_ref, w_ref, o_ref):
    for j in range(o_ref.shape[1] // TN_PROJ):
        cols = slice(j * TN_PROJ, (j + 1) * TN_PROJ)
        o_ref[:, cols] = _dot(h_ref[...], w_ref[:, cols]).astype(o_ref.dtype)


def _proj(h, w, out_dtype):
    seq, d = h.shape
    n = w.shape[1]
    tm = TM_PROJ
    return pl.pallas_call(
        _proj_kernel,
        grid=(seq // tm,),
        in_specs=[pl.BlockSpec((tm, d), lambda i: (i, 0)), _const_spec(w.shape)],
        out_specs=pl.BlockSpec((tm, n), lambda i: (i, 0)),
        out_shape=jax.ShapeDtypeStruct((seq, n), out_dtype),
        compiler_params=pltpu.CompilerParams(dimension_semantics=("parallel",)),
        name="proj",
    )(h, w)


def _diffattn_kernel(qall_ref, k_ref, v_ref, bias_ref, bstat_ref, lam_ref, g_ref, o_ref,
                     flag_ref, qs_ref, vt_ref, kmax_ref, r_ref, m_ref, acc_ref, *s_refs, out_scale):
    blk = BLK_A
    nq = 2 * blk
    sub = SUBLANES
    dv = DV_A
    npart = NPART_A
    sa_ref, sb_ref = s_refs[:2 * npart], s_refs[2 * npart:4 * npart]
    pa_ref, pb_ref = s_refs[4 * npart:5 * npart], s_refs[5 * npart:6 * npart]
    i = pl.program_id(1)
    lane = lax.broadcasted_iota(jnp.int32, (blk, LANES), 1)
    same_subhead = (lax.broadcasted_iota(jnp.int32, (LANES, LANES), 0) // DK_A
                    == lax.broadcasted_iota(jnp.int32, (LANES, LANES), 1) // DK_A).astype(BF16)

    bias_max, bias_span = bstat_ref[0, 0:1, 0:1], bstat_ref[0, 1:2, 0:1]
    q_scale = DK_A ** -0.5 * LOG2E

    @pl.when(i == 0)
    def _():
        kmax_ref[...] = jnp.zeros(kmax_ref.shape, F32)

        def tr(b, qmax):
            r0 = pl.multiple_of(b * blk, blk)
            c0 = pl.multiple_of(b * nq, nq)
            vt_ref[0:dv, pl.ds(r0, blk)] = v_ref[pl.ds(r0, blk), :].astype(F32).T.astype(BF16)
            vt_ref[dv:dv + ONES_A, pl.ds(r0, blk)] = jnp.ones((ONES_A, blk), BF16)
            kf = k_ref[pl.ds(r0, blk), :].astype(F32)
            kn2 = _dot((kf * kf).astype(BF16), same_subhead)
            kmax_ref[...] = jnp.maximum(kmax_ref[...], jnp.max(kn2.reshape(blk // sub, sub, LANES), axis=0))
            qa = (qall_ref[pl.ds(r0, blk), :].astype(F32) * q_scale).astype(BF16).astype(F32)
            for m in range(2):
                qt = jnp.where((lane < DK_A) if m == 0 else (lane >= DK_A), qa, 0.0).T
                qs_ref[:, pl.ds(c0 + m * blk, blk)] = qt.astype(BF16)
                qn2 = jnp.sum(jnp.sum((qt * qt).reshape(LANES // sub, sub, blk), axis=0), axis=0, keepdims=True)
                qn2 = jnp.broadcast_to(qn2, (sub, blk))
                r_ref[:, pl.ds(c0 + m * blk, blk)] = qn2
                qmax = jnp.maximum(qmax, qn2)
            return qmax
        nblocks = v_ref.shape[0] // blk
        qmax2 = jnp.max(lax.fori_loop(0, nblocks, tr, jnp.zeros((sub, blk), F32)))
        kmax2 = jnp.max(kmax_ref[...], axis=0, keepdims=True)
        worst = 2.0 * jnp.sqrt(qmax2 * kmax2) * 1.03 + bias_span
        flag_ref[0] = (jnp.max(worst) < SINGLE_PASS_LOG2_RANGE).astype(jnp.int32)

        kmax_nq = jnp.concatenate([jnp.broadcast_to(kmax2[:, m * DK_A:m * DK_A + 1], (sub, blk)) for m in range(2)],
                                  axis=1)

        def shifts(b, carry):
            c0 = pl.multiple_of(b * nq, nq)
            r_ref[:, pl.ds(c0, nq)] = jnp.sqrt(r_ref[:, pl.ds(c0, nq)] * kmax_nq) * 1.03 + bias_max
            return carry
        lax.fori_loop(0, nblocks, shifts, 0)

    acc_ref[...] = jnp.zeros(acc_ref.shape, F32)
    q0 = pl.multiple_of(i * nq, nq)
    single_pass = flag_ref[0] == 1

    @pl.when(single_pass)
    def _():
        _diffattn_fixed_shift(i, q0, k_ref, bias_ref, qs_ref, vt_ref, r_ref, m_ref, acc_ref, pa_ref, pb_ref)

    @pl.when(jnp.logical_not(single_pass))
    def _():
        _diffattn_online(i, q0, k_ref, bias_ref, qs_ref, vt_ref, m_ref, acc_ref, sa_ref, sb_ref)

    ot = acc_ref[0:dv, 0:nq] / acc_ref[dv:dv + 1, 0:nq]
    o = ot[:, 0:blk].T - lam_ref[...] * ot[:, blk:nq].T
    o = o * lax.rsqrt(jnp.mean(o * o, axis=-1, keepdims=True) + EPS) * g_ref[...]
    o_ref[...] = (o * out_scale).astype(o_ref.dtype)


def _diffattn_fixed_shift(i, q0, k_ref, bias_ref, qs_ref, vt_ref, shift_ref, l_ref, acc_ref, pa_ref, pb_ref):
    blk = BLK_A
    nq = 2 * blk
    sub = SUBLANES
    npart = len(pa_ref)
    wq = nq // npart
    l_ref[...] = jnp.zeros(l_ref.shape, F32)

    def probs(b, p_ref, bias):
        k = k_ref[pl.ds(pl.multiple_of(b * blk, blk), blk), :]
        for part in range(npart):
            cols = slice(part * wq, (part + 1) * wq)
            qcols = pl.ds(pl.multiple_of(q0 + part * wq, wq), wq)
            s = _dot(k, qs_ref[:, qcols])
            if bias is not None:
                b0 = (part * wq) % blk
                s = s + bias[:, b0:b0 + wq]
            p = jnp.exp2(s.reshape(blk // sub, sub, wq) - shift_ref[:, qcols][None])
            l_ref[:, cols] += jnp.sum(p, axis=0)
            p_ref[part][:, 0:wq] = p.reshape(blk, wq).astype(BF16)

    def accumulate(b, p_ref):
        vt = vt_ref[0:DV_A, pl.ds(pl.multiple_of(b * blk, blk), blk)]
        for part in range(npart):
            cols = slice(part * wq, (part + 1) * wq)
            acc_ref[0:DV_A, cols] += _dot(vt, p_ref[part][:, 0:wq])

    @pl.when(i == 0)
    def _():
        probs(0, pa_ref, bias_ref[0, 1])
        accumulate(0, pa_ref)

    @pl.when(i > 0)
    def _():
        nfar = i - 1
        probs(i, pa_ref, bias_ref[0, 1])
        probs(i - 1, pb_ref, bias_ref[0, 0])
        accumulate(i, pa_ref)

        def pair(t):
            probs(2 * t, pa_ref, None)
            accumulate(jnp.where(t == 0, i - 1, 2 * t - 1), pb_ref)
            probs(2 * t + 1, pb_ref, None)
            accumulate(2 * t, pa_ref)

        def four_pairs(u, carry):
            for v in range(4):
                pair(4 * u + v)
            return carry

        npairs = nfar // 2
        lax.fori_loop(0, npairs // 4, four_pairs, 0)

        def one_pair(t, carry):
            pair(t)
            return carry
        lax.fori_loop(4 * (npairs // 4), npairs, one_pair, 0)
        in_pb = jnp.where(npairs == 0, i - 1, 2 * npairs - 1)

        @pl.when(lax.rem(nfar, 2) == 1)
        def _():
            probs(nfar - 1, pa_ref, None)
            accumulate(in_pb, pb_ref)
            accumulate(nfar - 1, pa_ref)

        @pl.when(lax.rem(nfar, 2) == 0)
        def _():
            accumulate(in_pb, pb_ref)

    acc_ref[DV_A:DV_A + sub, 0:nq] = jnp.broadcast_to(jnp.sum(l_ref[...], axis=0, keepdims=True), (sub, nq))


def _diffattn_online(i, q0, k_ref, bias_ref, qs_ref, vt_ref, m_ref, acc_ref, sa_ref, sb_ref):
    blk = BLK_A
    nq = 2 * blk
    sub = SUBLANES
    npart = len(sa_ref) // 2
    wq = nq // npart
    m_ref[...] = jnp.full(m_ref.shape, NEG_INF, F32)

    def scores(b, s_ref):
        k = k_ref[pl.ds(pl.multiple_of(b * blk, blk), blk), :]
        for part in range(npart):
            s = _dot(k, qs_ref[:, pl.ds(pl.multiple_of(q0 + part * wq, wq), wq)])
            s_ref[part][:, 0:wq] = s
            s_ref[npart + part][...] = jnp.max(s.reshape(blk // sub, sub, wq), axis=0)

    def softmax_pv(b, s_ref, bias):
        vt = vt_ref[:, pl.ds(pl.multiple_of(b * blk, blk), blk)]
        for part in range(npart):
            cols = slice(part * wq, (part + 1) * wq)
            s = s_ref[part][:, 0:wq]
            if bias is not None:
                b0 = (part * wq) % blk
                s = s + bias[:, b0:b0 + wq]
            s = s.reshape(blk // sub, sub, wq)
            m_prev = m_ref[:, cols]
            smax = jnp.max(s, axis=0) if bias is not None else s_ref[npart + part][...]
            m_cur = jnp.max(smax, axis=0, keepdims=True)
            m_new = jnp.maximum(m_prev, m_cur)
            alpha = jnp.exp2(m_prev - m_new)
            p = jnp.exp2(s - m_new[None])
            pv = _dot(vt, p.reshape(blk, wq).astype(BF16))
            acc_ref[:, cols] = acc_ref[:, cols] * alpha[0:1] + pv
            m_ref[:, cols] = m_new

    nfar = jnp.maximum(i - 1, 0)
    odd = lax.rem(nfar, 2)

    @pl.when(i == 0)
    def _():
        scores(0, sb_ref)

    @pl.when(i > 0)
    def _():
        @pl.when(odd == 1)
        def _():
            scores(0, sb_ref)
            scores(1, sa_ref)
            softmax_pv(0, sb_ref, None)

        @pl.when(odd == 0)
        def _():
            scores(0, sa_ref)

        def pair(b):
            scores(b + 1, sb_ref)
            softmax_pv(b, sa_ref, None)
            scores(b + 2, sa_ref)
            softmax_pv(b + 1, sb_ref, None)

        def quad_body(t, carry):
            pair(odd + 4 * t)
            pair(odd + 4 * t + 2)
            return carry

        npairs = nfar // 2
        lax.fori_loop(0, npairs // 2, quad_body, 0)

        @pl.when(lax.rem(npairs, 2) == 1)
        def _():
            pair(odd + 2 * (npairs - 1))
        scores(i, sb_ref)
        softmax_pv(i - 1, sa_ref, bias_ref[0, 0])

    softmax_pv(i, sb_ref, bias_ref[0, 1])


_TOEPLITZ_ROWS = 512


def _toeplitz_kernel(v_ref, o_ref, *, keep):
    rows, cols = o_ref.shape[2:]
    x = jnp.broadcast_to(v_ref[0, 0], (rows, v_ref.shape[-1]))
    tile = pltpu.roll(x, 0, 1, stride=1, stride_axis=0)[:, :cols]
    r = lax.broadcasted_iota(jnp.int32, (rows, cols), 0) + pl.program_id(1) * rows
    c = lax.broadcasted_iota(jnp.int32, (rows, cols), 1)
    for variant in range(o_ref.shape[0]):
        o_ref[variant, 0] = jnp.where(keep(r, c, variant), tile, NEG_INF)


def _toeplitz_tiles(fn, keep, heads, rows, cols, variants=1):
    n = rows + cols
    rb = rows if rows % _TOEPLITZ_ROWS else _TOEPLITZ_ROWS
    assert rows % rb == 0 and n % LANES == 0
    idx = jnp.arange(n, dtype=jnp.int32)
    vec = fn(jnp.where(idx < cols, idx, idx - n)).astype(F32)
    vecs = jnp.stack([jnp.roll(vec, k * rb, axis=1) for k in range(rows // rb)], axis=1)
    return pl.pallas_call(
        functools.partial(_toeplitz_kernel, keep=keep),
        grid=(heads, rows // rb),
        in_specs=[pl.BlockSpec((1, 1, 1, n), lambda h, k: (h, k, 0, 0))],
        out_specs=pl.BlockSpec((variants, 1, rb, cols), lambda h, k: (0, h, k, 0)),
        out_shape=jax.ShapeDtypeStruct((variants, heads, rows, cols), F32),
        name="toeplitz_tiles",
    )(vecs.reshape(heads, rows // rb, 1, n))


def _bias_stats(tiles):
    finite = tiles > 0.5 * NEG_INF
    bias_max = jnp.maximum(jnp.max(jnp.where(finite, tiles, NEG_INF), axis=(1, 2)), 0.0)
    bias_min = jnp.minimum(jnp.min(jnp.where(finite, tiles, -NEG_INF), axis=(1, 2)), 0.0)
    return jnp.broadcast_to(jnp.stack([bias_max, bias_max - bias_min], axis=1)[:, :, None],
                            (tiles.shape[0], 2, LANES))


def _t5_bucket(rel):
    nb = NUM_BUCKETS // 2
    max_exact = nb // 2
    bucket = jnp.where(rel > 0, nb, 0)
    n = jnp.abs(rel)
    nf = jnp.maximum(n, 1).astype(F32)
    large = max_exact + (jnp.log(nf / max_exact) / math.log(MAX_DISTANCE / max_exact)
                         * (nb - max_exact)).astype(jnp.int32)
    large = jnp.minimum(large, nb - 1)
    return bucket + jnp.where(n < max_exact, n, large)


def _diff_bias_tiles(t5_table):
    blk = BLK_A
    table = t5_table.astype(F32)
    far = table[_t5_bucket(jnp.full((), -(blk + 1), jnp.int32))]
    def visible(r, c, variant):
        return jnp.floor_divide(r - blk, CHUNK) <= jnp.floor_divide(c, CHUNK)

    tiles = _toeplitz_tiles(lambda x: ((table[_t5_bucket(-x - blk)] - far) * LOG2E).T, visible,
                            N_HEADS_A, 2 * blk, blk)
    return tiles.reshape(N_HEADS_A, 2, blk, blk)


def _diff_attention(proj, t5_table, lam, subln_g, lam_init):
    seq = proj.shape[0]
    blk = BLK_A
    bias = _diff_bias_tiles(t5_table)
    ha = N_HEADS_A
    bstat = _bias_stats(bias.reshape(ha, 2 * blk, blk))
    kern = functools.partial(_diffattn_kernel, out_scale=1.0 - lam_init)
    return pl.pallas_call(
        kern,
        grid=(ha, seq // blk),
        in_specs=[
            pl.BlockSpec((seq, DV_A), lambda h, i: (0, h)),
            pl.BlockSpec((seq, DV_A), lambda h, i: (0, ha + h)),
            pl.BlockSpec((seq, DV_A), lambda h, i: (0, 2 * ha + h)),
            pl.BlockSpec((1, 2, blk, blk), lambda h, i: (h, 0, 0, 0)),
            pl.BlockSpec((1, 2, LANES), lambda h, i: (h, 0, 0)),
            pl.BlockSpec((1, DV_A), lambda h, i: (0, 0)),
            pl.BlockSpec((1, DV_A), lambda h, i: (0, 0)),
        ],
        out_specs=pl.BlockSpec((blk, DV_A), lambda h, i: (i, h)),
        out_shape=jax.ShapeDtypeStruct((seq, ha * DV_A), BF16),
        scratch_shapes=[
            pltpu.SMEM((1,), jnp.int32),
            pltpu.VMEM((DV_A, 2 * seq), BF16),
            pltpu.VMEM((DV_A + ONES_A, seq), BF16),
            pltpu.VMEM((SUBLANES, LANES), F32),
            pltpu.VMEM((SUBLANES, 2 * seq), F32),
            pltpu.VMEM((SUBLANES, 2 * blk), F32),
            pltpu.VMEM((DV_A + ONES_A, 2 * blk), F32),
        ] + 2 * ([pltpu.VMEM((blk, 2 * blk // NPART_A + SCORE_PAD), F32)] * NPART_A
                 + [pltpu.VMEM((SUBLANES, 2 * blk // NPART_A), F32)] * NPART_A)
        + 2 * [pltpu.VMEM((blk, 2 * blk // NPART_A + SCORE_PAD), BF16)] * NPART_A,
        compiler_params=pltpu.CompilerParams(dimension_semantics=("parallel", "arbitrary")),
        name="diff_attention",
    )(proj, proj, proj, bias, bstat, jnp.full((1, DV_A), lam, F32), subln_g.reshape(1, DV_A).astype(F32))


def _band_kernel(q_ref, kp_ref, kc_ref, vp_ref, vc_ref, *refs):
    qw, band = QW_B, BAND_B
    nbias = band // qw + 1
    bias_refs, bstat_ref, o_ref = refs[:nbias], refs[nbias], refs[nbias + 1]
    p_refs = refs[nbias + 2:]
    ngroups = len(p_refs)
    nk = band + qw
    sub = SUBLANES
    q = (q_ref[...].astype(F32) * (DH_B ** -0.5 * LOG2E)).astype(BF16).astype(F32)
    lane = lax.broadcasted_iota(jnp.int32, q.shape, 1)
    qt = (jnp.where(lane < DH_B, q, 0.0).T, jnp.where(lane >= DH_B, q, 0.0).T)
    qh = (qt[0].astype(BF16), qt[1].astype(BF16))
    k_all = jnp.concatenate([kp_ref[...], kc_ref[...]], axis=0)
    vt_all = jnp.concatenate([vp_ref[...], vc_ref[...]], axis=0).astype(F32).T.astype(BF16)

    def group_operands(g):
        k0 = g * qw
        qs = jnp.concatenate([qh[0][:, k0:k0 + qw], qh[1][:, k0:k0 + qw]], axis=1)
        bias_ref = bias_refs[min(g, nbias - 1)]
        bias = jnp.concatenate([bias_ref[0, 0], bias_ref[0, 1]], axis=1)
        return k0, qs, bias

    def store_group(g, ot):
        o = jnp.concatenate([ot[0:DH_B, 0:qw], ot[DH_B:2 * DH_B, qw:2 * qw]], axis=0)
        o_ref[g * qw:(g + 1) * qw, :] = o.T.astype(o_ref.dtype)

    same_head = (lax.broadcasted_iota(jnp.int32, (LANES, LANES), 0) // DH_B
                 == lax.broadcasted_iota(jnp.int32, (LANES, LANES), 1) // DH_B).astype(BF16)
    kf = k_all.astype(F32)
    kn2 = _dot((kf * kf).astype(BF16), same_head)
    kmax2 = jnp.max(jnp.max(kn2.reshape(kn2.shape[0] // sub, sub, LANES), axis=0), axis=0, keepdims=True)
    shifts, worst = [], None
    for m in range(2):
        qn2 = jnp.sum(qt[m] * qt[m], axis=0, keepdims=True)
        bound = jnp.sqrt(qn2 * kmax2[:, m * DH_B:m * DH_B + 1]) * 1.03
        shifts.append(bound + bstat_ref[m, 0:1, 0:1])
        spread = jnp.max(2.0 * bound + bstat_ref[m, 1:2, 0:1])
        worst = spread if worst is None else jnp.maximum(worst, spread)
    fixed_shift = worst < SINGLE_PASS_LOG2_RANGE

    @pl.when(fixed_shift)
    def _():
        sums = []
        for g in range(ngroups):
            k0, qs, bias = group_operands(g)
            r = jnp.concatenate([shifts[0][:, k0:k0 + qw], shifts[1][:, k0:k0 + qw]], axis=1)
            s = _dot(k_all[k0:k0 + nk], qs) + (bias - r)
            p = jnp.exp2(s).reshape(nk // sub, sub, 2 * qw)
            sums.append(jnp.sum(jnp.sum(p, axis=0), axis=0, keepdims=True))
            p_refs[g][:, 0:2 * qw] = p.reshape(nk, 2 * qw).astype(BF16)
        for g in range(ngroups):
            k0 = g * qw
            store_group(g, _dot(vt_all[:, k0:k0 + nk], p_refs[g][:, 0:2 * qw]) / sums[g])

    @pl.when(jnp.logical_not(fixed_shift))
    def _():
        vt_ones = jnp.concatenate([vt_all, jnp.ones((ONES_A, vt_all.shape[1]), BF16)], axis=0)
        for g in range(ngroups):
            k0, qs, bias = group_operands(g)
            s = (_dot(k_all[k0:k0 + nk], qs) + bias).reshape(nk // sub, sub, 2 * qw)
            m = jnp.max(jnp.max(s, axis=0), axis=0, keepdims=True)
            p = jnp.exp2(s - m[None])
            pv = _dot(vt_ones[:, k0:k0 + nk], p.reshape(nk, 2 * qw).astype(BF16))
            store_group(g, pv[0:2 * DH_B] / pv[2 * DH_B:2 * DH_B + 1])


def _band_bias_tiles(rel_bias):
    band = BAND_B

    def valid(r, c, variant):
        qchunk = jnp.floor_divide(c, CHUNK)
        kchunk = jnp.floor_divide(r - band, CHUNK)
        missing = jnp.where(variant == 0, 0, band - (variant - 1) * QW_B)
        return (kchunk <= qchunk) & (kchunk >= qchunk - LEFT_CHUNKS) & (r >= missing)

    return _toeplitz_tiles(
        lambda x: rel_bias.astype(F32)[:, jnp.clip(-x - band, -REL_CLIP, REL_CLIP) + REL_CLIP] * LOG2E, valid,
        N_HEADS_B, band + QW_B, QW_B, variants=1 + band // QW_B)


def _band_attention(proj, rel_bias):
    seq = proj.shape[0]
    blk, band, qw = BLK_B, BAND_B, QW_B
    bias = _band_bias_tiles(rel_bias)
    npair = N_HEADS_B // 2
    qc0 = 3 * N_HEADS_A
    per = blk // band
    prev = lambda c0: (lambda hp, i: (jnp.maximum(i * per - 1, 0), c0 + hp))
    cur = lambda c0: (lambda hp, i: (i, c0 + hp))
    return pl.pallas_call(
        _band_kernel,
        grid=(npair, seq // blk),
        in_specs=[
            pl.BlockSpec((blk, LANES), cur(qc0)),
            pl.BlockSpec((band, LANES), prev(qc0 + npair)),
            pl.BlockSpec((blk, LANES), cur(qc0 + npair)),
            pl.BlockSpec((band, LANES), prev(qc0 + 2 * npair)),
            pl.BlockSpec((blk, LANES), cur(qc0 + 2 * npair)),
        ] + [
            pl.BlockSpec((1, 2, band + qw, qw), (lambda hp, i, t=t: (jnp.where(i == 0, 1 + t, 0), hp, 0, 0)))
            for t in range(band // qw)
        ] + [
            pl.BlockSpec((1, 2, band + qw, qw), lambda hp, i: (0, hp, 0, 0)),
            pl.BlockSpec((2, 2, LANES), lambda hp, i: (hp, 0, 0)),
        ],
        out_specs=pl.BlockSpec((blk, LANES), lambda hp, i: (i, hp)),
        out_shape=jax.ShapeDtypeStruct((seq, N_HEADS_B * DH_B), BF16),
        scratch_shapes=[pltpu.VMEM((band + qw, 2 * qw + SCORE_PAD), BF16)] * (blk // qw),
        compiler_params=pltpu.CompilerParams(dimension_semantics=("parallel", "arbitrary")),
        name="band_attention",
    )(proj, proj, proj, proj, proj, *([bias] * bias.shape[0]), _bias_stats(bias[0]))


def _retention_kernel(qk_ref, v_ref, gate_ref, cos_ref, sin_ref, qdec_ref, kdec_ref, dmat_ref,
                      sdec_ref, o_ref, state_ref):
    @pl.when(pl.program_id(0) == 0)
    def _():
        state_ref[...] = jnp.zeros(state_ref.shape, F32)

    cos = cos_ref[...]
    sin = sin_ref[...]
    lane = lax.broadcasted_iota(jnp.int32, cos.shape, 1)
    first_half = (lane % DQK_C) < (DQK_C // 2)
    qk = qk_ref[...]
    parts = []
    for j in range(qk.shape[1] // LANES):
        t = qk[:, j * LANES:(j + 1) * LANES]
        partner = jnp.where(first_half, pltpu.roll(t, LANES - DQK_C // 2, 1), pltpu.roll(t, DQK_C // 2, 1))
        parts.append(t * cos + partner * sin)
    wq = N_HEADS_C * DQK_C
    q = jnp.concatenate(parts[:wq // LANES], axis=1)
    k = jnp.concatenate(parts[wq // LANES:], axis=1) * (DQK_C ** -0.5)
    qd = (q * qdec_ref[...]).astype(BF16)
    kd = (k * kdec_ref[...]).astype(BF16)
    qb = q.astype(BF16)
    kb = k.astype(BF16)
    vb = v_ref[...].astype(BF16)
    half = qk.shape[0] // 2
    for h in range(N_HEADS_C):
        qs = slice(h * DQK_C, (h + 1) * DQK_C)
        vs = slice(h * DV_C, (h + 1) * DV_C)
        state = state_ref[h]
        sb = state.astype(BF16)
        for rows, nkeys in ((slice(0, half), half), (slice(half, 2 * half), 2 * half)):
            scores = _dot_nt(qb[rows, qs], kb[0:nkeys, qs]) * dmat_ref[h, rows, 0:nkeys]
            r = _dot(scores.astype(BF16), vb[0:nkeys, vs]) + _dot(qd[rows, qs], sb)
            r = r * lax.rsqrt(jnp.mean(r * r, axis=-1, keepdims=True) + EPS)
            g = gate_ref[rows, vs]
            o_ref[rows, vs] = (r * (g * jax.nn.sigmoid(g))).astype(o_ref.dtype)
        state_ref[h] = state * sdec_ref[h] + _dot_tn(kd[:, qs], vb[:, vs])


def _retention_tables(seq):
    t = BLK_C
    half = DQK_C // 2
    inv_freq = 1.0 / np.power(ROPE_BASE, np.arange(0, DQK_C, 2, dtype=np.float64) / DQK_C)
    ang = np.arange(seq, dtype=np.float64)[:, None] * inv_freq[None, :]
    reps = LANES // half
    cos = np.tile(np.cos(ang), (1, reps))
    sign = np.where((np.arange(LANES) % DQK_C) < half, -1.0, 1.0)
    sin = np.tile(np.sin(ang), (1, reps)) * sign[None, :]
    log_g = np.log(1.0 - np.power(2.0, -5.0 - np.arange(N_HEADS_C, dtype=np.float64)))
    pos = np.arange(t, dtype=np.float64)
    diff = pos[:, None] - pos[None, :]
    same_or_past = (np.arange(t)[None, :] // CHUNK) <= (np.arange(t)[:, None] // CHUNK)
    dmat = np.where(same_or_past[None], np.exp(log_g[:, None, None] * np.abs(diff)[None]), 0.0)
    qdec = np.repeat(np.exp(log_g[None, :] * (pos[:, None] + 1.0)), DQK_C, axis=1)
    kdec = np.repeat(np.exp(log_g[None, :] * (t - 1.0 - pos[:, None])), DQK_C, axis=1)
    sdec = np.broadcast_to(np.exp(log_g * t)[:, None, None], (N_HEADS_C, 1, DV_C))
    return tuple(jnp.asarray(a.astype(np.float32)) for a in (cos, sin, qdec, kdec, dmat, sdec))


def _retention(proj):
    seq = proj.shape[0]
    t = BLK_C
    assert (t // 2) % CHUNK == 0
    cos, sin, qdec, kdec, dmat, sdec = _retention_tables(seq)
    wv = N_HEADS_C * DV_C
    return pl.pallas_call(
        _retention_kernel,
        grid=(seq // t,),
        in_specs=[
            pl.BlockSpec((t, wv), lambda i: (i, 0)),
            pl.BlockSpec((t, wv), lambda i: (i, 1)),
            pl.BlockSpec((t, wv), lambda i: (i, 2)),
            pl.BlockSpec((t, LANES), lambda i: (i, 0)),
            pl.BlockSpec((t, LANES), lambda i: (i, 0)),
            pl.BlockSpec((t, N_HEADS_C * DQK_C), lambda i: (0, 0)),
            pl.BlockSpec((t, N_HEADS_C * DQK_C), lambda i: (0, 0)),
            pl.BlockSpec((N_HEADS_C, t, t), lambda i: (0, 0, 0)),
            pl.BlockSpec((N_HEADS_C, 1, DV_C), lambda i: (0, 0, 0)),
        ],
        out_specs=pl.BlockSpec((t, wv), lambda i: (i, 0)),
        out_shape=jax.ShapeDtypeStruct((seq, wv), BF16),
        scratch_shapes=[pltpu.VMEM((N_HEADS_C, DQK_C, DV_C), F32)],
        compiler_params=pltpu.CompilerParams(dimension_semantics=("arbitrary",)),
        name="retention",
    )(proj, proj, proj, cos, sin, qdec, kdec, dmat, sdec)


def _s5_kernel(*refs):
    ncb = S5_CH // LANES
    u_refs = refs[:ncb]
    (mt_ref, bt_ref, ctr_ref, cti_ref, are_ref, aim_ref, y_ref,
     ut_ref, yt_ref, ys_ref, vr_ref, vi_ref, spr_ref, spi_ref, carry_ref) = refs[ncb:]
    tc = S5_TC
    gp = S5_GROUP
    n = S5_STATE
    ng = S5_GROUPS

    @pl.when(pl.program_id(0) == 0)
    def _():
        carry_ref[...] = jnp.zeros(carry_ref.shape, F32)

    for s in range(S5_T):
        for k in range(ncb):
            ut_ref[s, k * LANES:(k + 1) * LANES, :] = u_refs[k][pl.ds(s, tc, stride=S5_T), :].T

    for g in range(ng):
        ug = ut_ref[:, g * gp:(g + 1) * gp, :].reshape(S5_T * gp, tc).astype(BF16)
        yt_ref[:, g * gp:(g + 1) * gp, :] = _dot(mt_ref[g], ug).reshape(S5_T, gp, tc)
        vt = _dot(bt_ref[g], ug)
        vr_ref[g * n:(g + 1) * n, :] = vt[0:n]
        vi_ref[g * n:(g + 1) * n, :] = vt[n:2 * n]

    sub = SUBLANES
    nv = tc // sub
    row = lax.broadcasted_iota(jnp.int32, (tc, LANES), 0)
    in_vreg = lax.rem(row, sub)

    def rows_of(v, r):
        return jnp.broadcast_to(v[r:r + 1], (tc, LANES))

    for j in range(ng * n // LANES):
        cols = slice(j * LANES, (j + 1) * LANES)
        pwr, pwi = are_ref[:, cols], aim_ref[:, cols]
        xr = vr_ref[cols, :].T
        xi = vi_ref[cols, :].T
        for d in (1, 2, 4):
            keep = in_vreg >= d
            sr = jnp.where(keep, pltpu.roll(xr, d, 0), 0.0)
            si = jnp.where(keep, pltpu.roll(xi, d, 0), 0.0)
            fr, fi = rows_of(pwr, d - 1), rows_of(pwi, d - 1)
            xr, xi = xr + (fr * sr - fi * si), xi + (fr * si + fi * sr)
        cr, ci = carry_ref[0, :, cols], carry_ref[1, :, cols]
        cr0, ci0 = cr, ci
        outr, outi = [], []
        for v in range(nv):
            yr = xr[v * sub:(v + 1) * sub] + (pwr * cr - pwi * ci)
            yi = xi[v * sub:(v + 1) * sub] + (pwr * ci + pwi * cr)
            outr.append(yr)
            outi.append(yi)
            cr = jnp.broadcast_to(yr[sub - 1:sub], (sub, LANES))
            ci = jnp.broadcast_to(yi[sub - 1:sub], (sub, LANES))
        carry_ref[0, :, cols] = cr
        carry_ref[1, :, cols] = ci
        sr = jnp.concatenate(outr, axis=0)
        si = jnp.concatenate(outi, axis=0)
        first = row == 0
        spr_ref[j] = jnp.where(first, rows_of(cr0, 0), pltpu.roll(sr, 1, 0))
        spi_ref[j] = jnp.where(first, rows_of(ci0, 0), pltpu.roll(si, 1, 0))

    for jp in range(ng // 2):
        yc = (_dot_nt(ctr_ref[jp], spr_ref[jp].astype(BF16))
              + _dot_nt(cti_ref[jp], spi_ref[jp].astype(BF16)))
        yt_ref[:, jp * 2 * gp:(jp + 1) * 2 * gp, :] += yc.reshape(S5_T, 2 * gp, tc)

    for s in range(S5_T):
        for k in range(ncb):
            ys_ref[k, pl.ds(s, tc, stride=S5_T), :] = yt_ref[s, k * LANES:(k + 1) * LANES, :].T
    for k in range(ncb):
        y_ref[:, k * LANES:(k + 1) * LANES] = ys_ref[k]


def _s5_matrices(lam_re, lam_im, log_step, b_re, b_im, c_re, c_im, d_skip):
    hi = lax.Precision.HIGHEST
    t, gp, n, ng = S5_T, S5_GROUP, S5_STATE, S5_GROUPS
    lam = lax.complex(lam_re.astype(F32), lam_im.astype(F32))
    step = jnp.exp(log_step.astype(F32))[:, None]
    ls = lam * step
    a_bar = jnp.exp(ls)
    b_bar = ((a_bar - 1.0) / lam)[..., None] * lax.complex(b_re.astype(F32), b_im.astype(F32))
    cm = lax.complex(c_re.astype(F32), c_im.astype(F32))

    def apow(k):
        kk = k.astype(F32).astype(jnp.complex64)
        return jnp.exp(ls.reshape((ng,) + (1,) * k.ndim + (n,)) * kk[None, ..., None])

    tt = jnp.arange(t)
    kmat = jnp.einsum('gpn,gln,gnq->glpq', cm, apow(tt), b_bar, precision=hi).real
    krev = jnp.transpose(kmat[:, ::-1], (0, 2, 1, 3)).reshape(ng, gp, t * gp)
    kpad = jnp.pad(krev, ((0, 0), (0, 0), (0, t * gp)))
    mt = jnp.concatenate([kpad[:, :, (t - 1 - to) * gp:(2 * t - 1 - to) * gp] for to in range(t)], axis=1)
    dvec = jnp.tile(d_skip.astype(F32).reshape(ng, 1, gp), (1, t, 1)).reshape(ng, t * gp)
    mt = mt + jnp.eye(t * gp, dtype=F32)[None] * dvec[:, :, None]
    z = jnp.swapaxes(apow(t - 1 - tt), 1, 2)[:, :, :, None] * b_bar[:, :, None, :]
    z = z.reshape(ng, n, t * gp)
    bt = jnp.concatenate([z.real, z.imag], axis=1)
    w = cm[:, None, :, :] * apow(tt + 1)[:, :, None, :]

    def pair_readout(x):
        x = x.reshape(ng // 2, 2, t, gp, n)
        first = jnp.pad(x[:, 0], ((0, 0), (0, 0), (0, 0), (0, n)))
        second = jnp.pad(x[:, 1], ((0, 0), (0, 0), (0, 0), (n, 0)))
        return jnp.stack([first, second], axis=2).reshape(ng // 2, t * 2 * gp, 2 * n).astype(BF16)

    ctr, cti = pair_readout(w.real), pair_readout(-w.imag)
    a_chunk = jnp.transpose(apow(t * (jnp.arange(SUBLANES) + 1)), (1, 0, 2)).reshape(SUBLANES, ng * n)
    return mt.astype(BF16), bt.astype(BF16), ctr, cti, a_chunk.real, a_chunk.imag


def _s5(proj, mats):
    seq, width = proj.shape
    t, tc, gp, n, ng = S5_T, S5_TC, S5_GROUP, S5_STATE, S5_GROUPS
    rows = t * tc
    ncb = S5_CH // LANES
    cb0 = (width - S5_CH) // LANES
    u_specs = [pl.BlockSpec((rows, LANES), (lambda i, k=k: (i, cb0 + k))) for k in range(ncb)]
    nsb = ng * n // LANES
    return pl.pallas_call(
        _s5_kernel,
        grid=(seq // rows,),
        in_specs=u_specs + [_const_spec(m.shape) for m in mats],
        out_specs=pl.BlockSpec((rows, S5_CH), lambda i: (i, 0)),
        out_shape=jax.ShapeDtypeStruct((seq, S5_CH), F32),
        scratch_shapes=[
            pltpu.VMEM((t, S5_CH, tc), F32),
            pltpu.VMEM((t, S5_CH, tc), F32),
            pltpu.VMEM((ncb, rows, LANES), F32),
            pltpu.VMEM((ng * n, tc), F32),
            pltpu.VMEM((ng * n, tc), F32),
            pltpu.VMEM((nsb, tc, LANES), F32),
            pltpu.VMEM((nsb, tc, LANES), F32),
            pltpu.VMEM((2, SUBLANES, ng * n), F32),
        ],
        compiler_params=pltpu.CompilerParams(dimension_semantics=("arbitrary",)),
        name="s5_scan",
    )(*([proj] * ncb), *mats)


def _mix_ffn_kernel(*refs, glu, final):
    (x_ref, a_ref, b_ref, wo_ref, g1_ref), refs = refs[:5], refs[5:]
    if glu:
        gw_ref, refs = refs[0], refs[1:]
    (g_ref, sc_ref, sh_ref, gate_ref, win_ref, cw_ref, cb_ref, wout_ref), refs = refs[:8], refs[8:]
    if final:
        fg_ref, o_ref, h_ref, act_ref, gbuf_ref, carry_ref = refs
    else:
        ng_ref, nsc_ref, nsh_ref, o_ref, hn_ref, h_ref, act_ref, gbuf_ref, carry_ref = refs
    tm = x_ref.shape[0]
    halo = gbuf_ref.shape[0] - tm

    @pl.when(pl.program_id(0) == 0)
    def _():
        carry_ref[...] = jnp.zeros(carry_ref.shape, F32)

    if glu:
        y = jax.nn.gelu(b_ref[...]).astype(BF16)
        gg = _dot(y, gw_ref[...])
        half = gg.shape[1] // 2
        b = (gg[:, :half] * jax.nn.sigmoid(gg[:, half:])).astype(BF16)
    else:
        b = b_ref[...]
    cat = jnp.concatenate([a_ref[...], b], axis=1)
    x = x_ref[...] + g1_ref[...] * _dot(cat, wo_ref[...])
    h_ref[...] = _mod_rmsnorm(x, g_ref[...], sc_ref[...], sh_ref[...]).astype(BF16)
    for f in range(D_FF // TF_FFN):
        cs = slice(f * TF_FFN, (f + 1) * TF_FFN)
        gs = slice(D_FF + f * TF_FFN, D_FF + (f + 1) * TF_FFN)
        h = h_ref[...]
        val = _dot(h, win_ref[:, cs])
        gate = _dot(h, win_ref[:, gs])
        gbuf_ref[0:halo, :] = carry_ref[:, cs]
        gbuf_ref[halo:halo + tm, :] = gate
        carry_ref[:, cs] = gate[tm - halo:tm, :]
        conv = (gate * cw_ref[2:3, cs] + gbuf_ref[halo - 1:halo - 1 + tm, :] * cw_ref[1:2, cs]
                + gbuf_ref[halo - 2:halo - 2 + tm, :] * cw_ref[0:1, cs] + cb_ref[:, cs])
        act_ref[:, cs] = (jax.nn.gelu(conv) * val).astype(BF16)
    xn = x + gate_ref[...] * _dot(act_ref[...], wout_ref[...])
    if final:
        xn = xn * lax.rsqrt(jnp.mean(xn * xn, axis=-1, keepdims=True) + EPS) * fg_ref[...]
    else:
        hn_ref[...] = _mod_rmsnorm(xn, ng_ref[...], nsc_ref[...], nsh_ref[...]).astype(BF16)
    o_ref[...] = xn


def _layer_spec(shape, layer):
    idx = (layer,) + (0,) * (len(shape) - 1)
    return pl.BlockSpec((None,) + tuple(shape[1:]), lambda *_: idx, pipeline_mode=pl.Buffered(1))


def _mix_ffn(x, a, b, wo, gate1, glu_w, g, scale, shift, gate2, w_in, conv_w, conv_b, w_out, tail, layer):
    seq, d = x.shape
    final = len(tail) == 1
    tm = TM_FFN
    halo = SUBLANES
    row = pl.BlockSpec((1, d), lambda i: (0, 0))
    rows = lambda w: pl.BlockSpec((tm, w), lambda i: (i, 0))
    conv_b = conv_b.reshape(conv_b.shape[0], 1, D_FF)
    in_specs = [rows(d), rows(a.shape[1]), rows(b.shape[1]), _const_spec(wo.shape), row]
    args = [x, a, b, wo, gate1]
    if glu_w is not None:
        in_specs.append(_const_spec(glu_w.shape))
        args.append(glu_w)
    in_specs += [
        row, row, row, row,
        _layer_spec(w_in.shape, layer),
        _layer_spec(conv_w.shape, layer),
        _layer_spec(conv_b.shape, layer),
        _layer_spec(w_out.shape, layer),
    ] + [row] * len(tail)
    args += [g.reshape(1, d), scale, shift, gate2, w_in, conv_w, conv_b, w_out]
    args += [t.reshape(1, d) for t in tail]
    out_specs = [rows(d)] if final else [rows(d), rows(d)]
    out_shape = [jax.ShapeDtypeStruct((seq, d), F32)] + ([] if final else [jax.ShapeDtypeStruct((seq, d), BF16)])
    return pl.pallas_call(
        functools.partial(_mix_ffn_kernel, glu=glu_w is not None, final=final),
        grid=(seq // tm,),
        in_specs=in_specs,
        out_specs=out_specs,
        out_shape=out_shape,
        scratch_shapes=[
            pltpu.VMEM((tm, d), BF16),
            pltpu.VMEM((tm, D_FF), BF16),
            pltpu.VMEM((tm + halo, TF_FFN), F32),
            pltpu.VMEM((halo, D_FF), F32),
        ],
        compiler_params=pltpu.CompilerParams(dimension_semantics=("arbitrary",)),
        name="mix_ffn",
    )(*args)


def kernel(x, c, t5_table, mod_w, mod_b, norm1_g, norm2_g, ffn_w_in, ffn_conv_w, ffn_conv_b, ffn_w_out,
           ev_w_in, ev_w_out, diff_lambda, diff_subln_g, band_rel_bias,
           od_w_in, od_w_out, s5_lam_re, s5_lam_im, s5_log_step, s5_b_re, s5_b_im, s5_c_re, s5_c_im,
           s5_d, s5_glu_w, final_g):
    assert x.shape[0] == 1 and x.shape[2] == D_MODEL
    seq = x.shape[1]
    assert seq % TM_PROJ == 0 and seq % (S5_T * S5_TC) == 0
    d = D_MODEL
    xs = x[0]
    mod = _modulation(c, mod_w, mod_b)
    ffn_w_in_b = ffn_w_in.astype(BF16)
    ffn_w_out_b = ffn_w_out.astype(BF16)
    mods = [[mod[i, :, k * d:(k + 1) * d] for k in range(6)] for i in range(DEPTH)]
    h = None
    for i in range(DEPTH):
        sh1, sc1, g1, sh2, sc2, g2 = mods[i]
        w_in = (ev_w_in if i % 2 == 0 else od_w_in)[i // 2].astype(BF16)
        proj_dtype = BF16 if i % 2 == 0 else F32
        if h is None:
            proj = _normproj(xs, norm1_g[i], sc1, sh1, w_in, proj_dtype)
        else:
            proj = _proj(h, w_in, proj_dtype)
        if i % 2 == 0:
            e = i // 2
            lam_init = 0.8 - 0.6 * math.exp(-0.3 * i)
            lp = diff_lambda[e].astype(F32)
            lam = jnp.exp(jnp.sum(lp[0] * lp[1])) - jnp.exp(jnp.sum(lp[2] * lp[3])) + lam_init
            mix_a = _diff_attention(proj, t5_table, lam, diff_subln_g[e], lam_init)
            mix_b = _band_attention(proj, band_rel_bias[e])
            wo, glu_w = ev_w_out[e].astype(BF16), None
        else:
            o = i // 2
            mix_a = _retention(proj)
            mats = _s5_matrices(s5_lam_re[o], s5_lam_im[o], s5_log_step[o], s5_b_re[o], s5_b_im[o],
                                s5_c_re[o], s5_c_im[o], s5_d[o])
            mix_b = _s5(proj, mats)
            wo, glu_w = od_w_out[o].astype(BF16), s5_glu_w[o].astype(BF16)
        if i == DEPTH - 1:
            tail = (final_g,)
        else:
            nsh1, nsc1 = mods[i + 1][0], mods[i + 1][1]
            tail = (norm1_g[i + 1], nsc1, nsh1)
        out = _mix_ffn(xs, mix_a, mix_b, wo, g1, glu_w, norm2_g[i], sc2, sh2, g2,
                       ffn_w_in_b, ffn_conv_w, ffn_conv_b, ffn_w_out_b, tail, layer=i)
        if i == DEPTH - 1:
            xs = out[0]
        else:
            xs, h = out
    return xs[None]
```
